```python
import math
import jax, jax.numpy as jnp
from jax import lax
import numpy as np

D_MODEL = 1024
BATCH = 8
SEQ = 2048
DEPTH = 1

CHUNK = 64
N_META = 16
Q_BLOCK = 128

D_MIX = D_MODEL
D_POOL = D_MIX // 2
POOL_WINDOWS = (2, 4, 8, 16)
N_POOL_GROUPS = len(POOL_WINDOWS)
POOL_GROUP = D_POOL // N_POOL_GROUPS

N_HEADS = 4
QK_NOPE = 128
QK_ROPE = 64
V_HEAD = 128
D_ATTN = N_HEADS * V_HEAD
Q_LORA = 256
KV_LORA = 128
ROPE_THETA = 10000.0
EPS = 1e-6

SPLIT_POINTS = (D_POOL, 2 * D_POOL, 2 * D_POOL + Q_LORA,
                2 * D_POOL + Q_LORA + KV_LORA,
                2 * D_POOL + Q_LORA + KV_LORA + QK_ROPE)
D_IN = 2 * D_POOL + Q_LORA + KV_LORA + QK_ROPE + D_ATTN

kernel_name = "hymba_pool_mla_hybrid"


def rmsnorm(x, g):
    xf = x.astype(jnp.float32)
    y = xf * lax.rsqrt(jnp.mean(xf * xf, axis=-1, keepdims=True) + EPS)
    return y.astype(x.dtype) * g


def rope_tables(length):
    half = QK_ROPE // 2
    inv_freq = 1.0 / (ROPE_THETA ** (jnp.arange(half, dtype=jnp.float32) / half))
    ang = jnp.arange(length, dtype=jnp.float32)[:, None] * inv_freq[None, :]
    return jnp.cos(ang), jnp.sin(ang)


def apply_rope(x, cos, sin):
    cos = cos.astype(x.dtype)
    sin = sin.astype(x.dtype)
    x1, x2 = jnp.split(x, 2, axis=-1)
    return jnp.concatenate([x1 * cos - x2 * sin, x1 * sin + x2 * cos], axis=-1)


def pool_mixer(u, pool_w, pool_scale):
    B, L, _ = u.shape
    uf = u.astype(jnp.float32)
    groups = jnp.split(uf, N_POOL_GROUPS, axis=-1)
    count_max = jnp.arange(1, L + 1, dtype=jnp.float32)[None, :, None]
    outs = []
    for g, w in zip(groups, POOL_WINDOWS):
        c = jnp.cumsum(g, axis=1)
        c_prev = jnp.pad(c, ((0, 0), (w, 0), (0, 0)))[:, :L]
        mean = (c - c_prev) / jnp.minimum(count_max, float(w))
        outs.append(mean - g)
    pooled = jnp.stack(outs, axis=2).astype(u.dtype)
    mixed = jnp.einsum("blgc,gcd->blgd", pooled, pool_w)
    return mixed.reshape(B, L, D_POOL) * pool_scale


def _attend(qn, qr, q_ids, kn, kr, vv, k_ids):
    scale = (QK_NOPE + QK_ROPE) ** -0.5
    s = (jnp.einsum("bqhd,bkhd->bhqk", qn, kn, preferred_element_type=jnp.float32)
         + jnp.einsum("bqhd,bkd->bhqk", qr, kr, preferred_element_type=jnp.float32)) * scale
    mask = k_ids[None, :] <= q_ids[:, None]
    s = jnp.where(mask[None, None], s, jnp.finfo(jnp.float32).min)
    p = jax.nn.softmax(s, axis=-1).astype(vv.dtype)
    return jnp.einsum("bhqk,bkhd->bqhd", p, vv)


def mla_attention(q_nope, q_rope, k_nope, k_rope, v, chunk_id):
    B, L = q_nope.shape[0], q_nope.shape[1]
    n_blk = (L - N_META) // Q_BLOCK

    def blockify(t):
        t = t[:, N_META:]
        t = t.reshape((B, n_blk, Q_BLOCK) + t.shape[2:])
        return jnp.moveaxis(t, 1, 0)

    ids_b = chunk_id[N_META:].reshape(n_blk, Q_BLOCK)
    out_real = lax.map(
        lambda a: _attend(a[0], a[1], a[2], k_nope, k_rope, v, chunk_id),
        (blockify(q_nope), blockify(q_rope), ids_b))
    out_real = jnp.moveaxis(out_real, 0, 1).reshape(B, L - N_META, N_HEADS, V_HEAD)
    m_ids = chunk_id[:N_META]
    out_meta = _attend(q_nope[:, :N_META], q_rope[:, :N_META], m_ids,
                       k_nope[:, :N_META], k_rope[:, :N_META], v[:, :N_META], m_ids)
    return jnp.concatenate([out_meta, out_real], axis=1)


def hybrid_layer(h, cos, sin, chunk_id, norm_g, w_in, q_norm_g, w_q_b,
                 kv_norm_g, w_kv_b, pool_w, pool_scale, w_out):
    B, L, _ = h.shape
    u = rmsnorm(h, norm_g) @ w_in
    pool_in, pool_gate, c_q, c_kv, k_r, attn_gate = jnp.split(u, SPLIT_POINTS, axis=-1)

    pool_out = jax.nn.silu(pool_gate) * pool_mixer(pool_in, pool_w, pool_scale)

    q = (rmsnorm(c_q, q_norm_g) @ w_q_b).reshape(B, L, N_HEADS, QK_NOPE + QK_ROPE)
    q_nope = q[..., :QK_NOPE]
    q_rope = apply_rope(q[..., QK_NOPE:], cos[:, None, :], sin[:, None, :])
    kv = (rmsnorm(c_kv, kv_norm_g) @ w_kv_b).reshape(B, L, N_HEADS, QK_NOPE + V_HEAD)
    k_nope = kv[..., :QK_NOPE]
    v = kv[..., QK_NOPE:]
    k_rope = apply_rope(k_r, cos, sin)
    attn = mla_attention(q_nope, q_rope, k_nope, k_rope, v, chunk_id).reshape(B, L, D_ATTN)
    attn_out = jax.nn.silu(attn_gate) * attn

    mix = jnp.concatenate([pool_out, attn_out], axis=-1) @ w_out
    return h + mix


def _fwd_setup_inputs(seed: int = 0) -> dict:
    key = jax.random.key(seed)
    ks = jax.random.split(key, 16)
    f32 = jnp.float32

    def nrm(k, shape, scale):
        return jax.random.normal(k, shape, f32) * scale

    return {
        "x": nrm(ks[0], (BATCH, SEQ, D_MODEL), 1.0),
        "meta_tokens": nrm(ks[1], (N_META, D_MODEL), 1.0),
        "norm_g": 1.0 + nrm(ks[2], (DEPTH, D_MODEL), 0.02),
        "w_in": nrm(ks[3], (DEPTH, D_MODEL, D_IN), D_MODEL ** -0.5),
        "q_norm_g": 1.0 + nrm(ks[4], (DEPTH, Q_LORA), 0.02),
        "w_q_b": nrm(ks[5], (DEPTH, Q_LORA, N_HEADS * (QK_NOPE + QK_ROPE)), Q_LORA ** -0.5),
        "kv_norm_g": 1.0 + nrm(ks[6], (DEPTH, KV_LORA), 0.02),
        "w_kv_b": nrm(ks[7], (DEPTH, KV_LORA, N_HEADS * (QK_NOPE + V_HEAD)), KV_LORA ** -0.5),
        "pool_w": nrm(ks[8], (DEPTH, N_POOL_GROUPS, POOL_GROUP, POOL_GROUP), POOL_GROUP ** -0.5),
        "pool_scale": 1.0 + nrm(ks[9], (DEPTH, D_POOL), 0.02),
        "w_out": nrm(ks[10], (DEPTH, D_MIX, D_MODEL), D_MIX ** -0.5),
        "final_norm_g": 1.0 + nrm(ks[11], (D_MODEL,), 0.02),
    }


def _fwd_reference(x, meta_tokens, norm_g, w_in, q_norm_g, w_q_b, kv_norm_g, w_kv_b,
              pool_w, pool_scale, w_out, final_norm_g):
    B, S, D = x.shape
    L = S + N_META
    meta = jnp.broadcast_to(meta_tokens.astype(x.dtype)[None], (B, N_META, D))
    h = jnp.concatenate([meta, x], axis=1)
    chunk_id = jnp.concatenate([jnp.zeros((N_META,), jnp.int32),
                                1 + jnp.arange(S, dtype=jnp.int32) // CHUNK])
    cos, sin = rope_tables(L)
    for i in range(DEPTH):
        h = hybrid_layer(h, cos, sin, chunk_id, norm_g[i], w_in[i], q_norm_g[i], w_q_b[i],
                         kv_norm_g[i], w_kv_b[i], pool_w[i], pool_scale[i], w_out[i])
    return rmsnorm(h, final_norm_g)[:, N_META:]


import jax as _jax
import jax.numpy as _jnp

TWIN_FORMAT = 'train_step'
FWD_PARAMS = ['x', 'meta_tokens', 'norm_g', 'w_in', 'q_norm_g', 'w_q_b', 'kv_norm_g', 'w_kv_b', 'pool_w', 'pool_scale', 'w_out', 'final_norm_g']
TWIN_WEIGHTS = ['meta_tokens', 'norm_g', 'w_in', 'q_norm_g', 'w_q_b', 'kv_norm_g', 'w_kv_b', 'pool_w', 'pool_scale', 'w_out', 'final_norm_g']
TWIN_DIFF_INPUT = 'x'
TWIN_INPUTS = ['x', 'meta_tokens', 'norm_g', 'w_in', 'q_norm_g', 'w_q_b', 'kv_norm_g', 'w_kv_b', 'pool_w', 'pool_scale', 'w_out', 'final_norm_g', 'loss_target', 'm_meta_tokens', 'm_norm_g', 'm_w_in', 'm_q_norm_g', 'm_w_q_b', 'm_kv_norm_g', 'm_w_kv_b', 'm_pool_w', 'm_pool_scale', 'm_w_out', 'm_final_norm_g', 'v_meta_tokens', 'v_norm_g', 'v_w_in', 'v_q_norm_g', 'v_w_q_b', 'v_kv_norm_g', 'v_w_kv_b', 'v_pool_w', 'v_pool_scale', 'v_w_out', 'v_final_norm_g']
TWIN_OUTPUTS = ['loss', 'grad_x', 'grad_meta_tokens', 'grad_norm_g', 'grad_w_in', 'grad_q_norm_g', 'grad_w_q_b', 'grad_kv_norm_g', 'grad_w_kv_b', 'grad_pool_w', 'grad_pool_scale', 'grad_w_out', 'grad_final_norm_g', 'delta_meta_tokens', 'delta_norm_g', 'delta_w_in', 'delta_q_norm_g', 'delta_w_q_b', 'delta_kv_norm_g', 'delta_w_kv_b', 'delta_pool_w', 'delta_pool_scale', 'delta_w_out', 'delta_final_norm_g', 'new_m_meta_tokens', 'new_m_norm_g', 'new_m_w_in', 'new_m_q_norm_g', 'new_m_w_q_b', 'new_m_kv_norm_g', 'new_m_w_kv_b', 'new_m_pool_w', 'new_m_pool_scale', 'new_m_w_out', 'new_m_final_norm_g', 'new_v_meta_tokens', 'new_v_norm_g', 'new_v_w_in', 'new_v_q_norm_g', 'new_v_w_q_b', 'new_v_kv_norm_g', 'new_v_w_kv_b', 'new_v_pool_w', 'new_v_pool_scale', 'new_v_w_out', 'new_v_final_norm_g']
TWIN_LEAF_KINDS = {'loss': 'loss', 'grad_x': 'grad_x', 'grad_meta_tokens': 'grad_w', 'grad_norm_g': 'grad_w', 'grad_w_in': 'grad_w', 'grad_q_norm_g': 'grad_w', 'grad_w_q_b': 'grad_w', 'grad_kv_norm_g': 'grad_w', 'grad_w_kv_b': 'grad_w', 'grad_pool_w': 'grad_w', 'grad_pool_scale': 'grad_w', 'grad_w_out': 'grad_w', 'grad_final_norm_g': 'grad_w', 'delta_meta_tokens': 'delta_w', 'delta_norm_g': 'delta_w', 'delta_w_in': 'delta_w', 'delta_q_norm_g': 'delta_w', 'delta_w_q_b': 'delta_w', 'delta_kv_norm_g': 'delta_w', 'delta_w_kv_b': 'delta_w', 'delta_pool_w': 'delta_w', 'delta_pool_scale': 'delta_w', 'delta_w_out': 'delta_w', 'delta_final_norm_g': 'delta_w', 'new_m_meta_tokens': 'new_m', 'new_m_norm_g': 'new_m', 'new_m_w_in': 'new_m', 'new_m_q_norm_g': 'new_m', 'new_m_w_q_b': 'new_m', 'new_m_kv_norm_g': 'new_m', 'new_m_w_kv_b': 'new_m', 'new_m_pool_w': 'new_m', 'new_m_pool_scale': 'new_m', 'new_m_w_out': 'new_m', 'new_m_final_norm_g': 'new_m', 'new_v_meta_tokens': 'new_v', 'new_v_norm_g': 'new_v', 'new_v_w_in': 'new_v', 'new_v_q_norm_g': 'new_v', 'new_v_w_q_b': 'new_v', 'new_v_kv_norm_g': 'new_v', 'new_v_w_kv_b': 'new_v', 'new_v_pool_w': 'new_v', 'new_v_pool_scale': 'new_v', 'new_v_w_out': 'new_v', 'new_v_final_norm_g': 'new_v'}


def _forward(args):
    return _fwd_reference(*[args[k] for k in FWD_PARAMS])


def _output_shape():
    out = _jax.eval_shape(lambda: _forward(_fwd_setup_inputs(0)))
    return out.shape, out.dtype

N_MICROBATCH = 1
ADAM_LR = 0.001
ADAM_B1 = 0.9
ADAM_B2 = 0.999
ADAM_EPS = 1e-08
ADAM_WD = 0.01
ADAM_STEP = 10
PER_EXAMPLE_BATCH_AXIS = {'x': 0, 'loss_target': 0}
SHARED_INPUTS = []
_WEIGHT_DTYPES = {'meta_tokens': _jnp.float32, 'norm_g': _jnp.float32, 'w_in': _jnp.float32, 'q_norm_g': _jnp.float32, 'w_q_b': _jnp.float32, 'kv_norm_g': _jnp.float32, 'w_kv_b': _jnp.float32, 'pool_w': _jnp.float32, 'pool_scale': _jnp.float32, 'w_out': _jnp.float32, 'final_norm_g': _jnp.float32}
MOMENT_SCALE = {'meta_tokens': 9.305196e-04, 'norm_g': 6.634354e-02, 'w_in': 4.555850e-02, 'q_norm_g': 1.421883e-02, 'w_q_b': 8.473079e-03, 'kv_norm_g': 3.341981e-02, 'w_kv_b': 9.885670e-03, 'pool_w': 6.166660e-02, 'pool_scale': 6.376430e-02, 'w_out': 4.367892e-02, 'final_norm_g': 1.599550e+01}


def _to_microbatches(a, axis):
    t = _jnp.moveaxis(a, axis, 0)
    t = t.reshape((N_MICROBATCH, t.shape[0] // N_MICROBATCH) + t.shape[1:])
    return _jnp.moveaxis(t, 1, axis + 1)


def setup_inputs(seed: int = 0) -> dict:
    inp = _fwd_setup_inputs(seed)
    key = _jax.random.fold_in(_jax.random.key(seed), 7919)
    shape, _ = _output_shape()
    out = dict(inp)
    out["loss_target"] = _jax.random.normal(_jax.random.fold_in(key, 0), shape, _jnp.float32)
    for i, name in enumerate(TWIN_WEIGHTS):
        w = inp[name].astype(_jnp.float32)
        if MOMENT_SCALE is None:
            s = _jnp.sqrt(_jnp.mean(_jnp.square(w)) + 1e-30)
        else:
            s = MOMENT_SCALE[name]
        km, kv = _jax.random.split(_jax.random.fold_in(key, i + 1))
        out[name] = w
        out["m_" + name] = s * _jax.random.normal(km, w.shape, _jnp.float32)
        out["v_" + name] = (s * s) * _jax.random.uniform(kv, w.shape, _jnp.float32, 0.5, 1.5)
    if N_MICROBATCH > 1:
        for name, axis in PER_EXAMPLE_BATCH_AXIS.items():
            out[name] = _to_microbatches(out[name], axis)
    return {'x': out['x'], 'meta_tokens': out['meta_tokens'], 'norm_g': out['norm_g'], 'w_in': out['w_in'], 'q_norm_g': out['q_norm_g'], 'w_q_b': out['w_q_b'], 'kv_norm_g': out['kv_norm_g'], 'w_kv_b': out['w_kv_b'], 'pool_w': out['pool_w'], 'pool_scale': out['pool_scale'], 'w_out': out['w_out'], 'final_norm_g': out['final_norm_g'], 'loss_target': out['loss_target'], 'm_meta_tokens': out['m_meta_tokens'], 'm_norm_g': out['m_norm_g'], 'm_w_in': out['m_w_in'], 'm_q_norm_g': out['m_q_norm_g'], 'm_w_q_b': out['m_w_q_b'], 'm_kv_norm_g': out['m_kv_norm_g'], 'm_w_kv_b': out['m_w_kv_b'], 'm_pool_w': out['m_pool_w'], 'm_pool_scale': out['m_pool_scale'], 'm_w_out': out['m_w_out'], 'm_final_norm_g': out['m_final_norm_g'], 'v_meta_tokens': out['v_meta_tokens'], 'v_norm_g': out['v_norm_g'], 'v_w_in': out['v_w_in'], 'v_q_norm_g': out['v_q_norm_g'], 'v_w_q_b': out['v_w_q_b'], 'v_kv_norm_g': out['v_kv_norm_g'], 'v_w_kv_b': out['v_w_kv_b'], 'v_pool_w': out['v_pool_w'], 'v_pool_scale': out['v_pool_scale'], 'v_w_out': out['v_w_out'], 'v_final_norm_g': out['v_final_norm_g']}


def _loss(weights, diff, rest, loss_target):
    with _jax.named_scope("forward"):
        args = {**rest, TWIN_DIFF_INPUT: diff, **{k: w.astype(_WEIGHT_DTYPES[k]) for k, w in weights.items()}}
        y = _forward(args)
    with _jax.named_scope("loss_head"):
        err = _jnp.square(y.astype(_jnp.float32) - loss_target)
        return 0.5 * _jnp.sum(_jnp.mean(err, axis=-1)) if err.ndim else 0.5 * err


def _adamw(w, g, m, v):
    m = ADAM_B1 * m + (1.0 - ADAM_B1) * g
    v = ADAM_B2 * v + (1.0 - ADAM_B2) * _jnp.square(g)
    m_hat = m / (1.0 - ADAM_B1 ** ADAM_STEP)
    v_hat = v / (1.0 - ADAM_B2 ** ADAM_STEP)
    delta = -ADAM_LR * (m_hat / (_jnp.sqrt(v_hat) + ADAM_EPS) + ADAM_WD * w)
    return delta, m, v


def reference(x, meta_tokens, norm_g, w_in, q_norm_g, w_q_b, kv_norm_g, w_kv_b, pool_w, pool_scale, w_out, final_norm_g, loss_target, m_meta_tokens, m_norm_g, m_w_in, m_q_norm_g, m_w_q_b, m_kv_norm_g, m_w_kv_b, m_pool_w, m_pool_scale, m_w_out, m_final_norm_g, v_meta_tokens, v_norm_g, v_w_in, v_q_norm_g, v_w_q_b, v_kv_norm_g, v_w_kv_b, v_pool_w, v_pool_scale, v_w_out, v_final_norm_g):
    given = dict(x=x, meta_tokens=meta_tokens, norm_g=norm_g, w_in=w_in, q_norm_g=q_norm_g, w_q_b=w_q_b, kv_norm_g=kv_norm_g, w_kv_b=w_kv_b, pool_w=pool_w, pool_scale=pool_scale, w_out=w_out, final_norm_g=final_norm_g, loss_target=loss_target, m_meta_tokens=m_meta_tokens, m_norm_g=m_norm_g, m_w_in=m_w_in, m_q_norm_g=m_q_norm_g, m_w_q_b=m_w_q_b, m_kv_norm_g=m_kv_norm_g, m_w_kv_b=m_w_kv_b, m_pool_w=m_pool_w, m_pool_scale=m_pool_scale, m_w_out=m_w_out, m_final_norm_g=m_final_norm_g, v_meta_tokens=v_meta_tokens, v_norm_g=v_norm_g, v_w_in=v_w_in, v_q_norm_g=v_q_norm_g, v_w_q_b=v_w_q_b, v_kv_norm_g=v_kv_norm_g, v_w_kv_b=v_w_kv_b, v_pool_w=v_pool_w, v_pool_scale=v_pool_scale, v_w_out=v_w_out, v_final_norm_g=v_final_norm_g)
    weights = {n: given[n] for n in TWIN_WEIGHTS}
    shared = {n: given[n] for n in SHARED_INPUTS}
    per_example = {n: given[n] for n in ['x']}
    grad_fn = _jax.value_and_grad(_loss, argnums=(0, 1))

    def one_microbatch(ex, loss_target):
        ex = dict(ex)
        diff = ex.pop(TWIN_DIFF_INPUT)
        return grad_fn(weights, diff, {**shared, **ex}, loss_target)

    if N_MICROBATCH == 1:
        loss, (grad_w, grad_x) = one_microbatch(per_example, given["loss_target"])
    else:
        def body(carry, xs):
            loss_sum, grad_sum = carry
            l_k, (gw_k, gx_k) = one_microbatch(xs[0], xs[1])
            with _jax.named_scope("update"):
                return (loss_sum + l_k, _jax.tree.map(_jnp.add, grad_sum, gw_k)), gx_k

        init = (_jnp.zeros((), _jnp.float32), _jax.tree.map(_jnp.zeros_like, weights))
        (loss, grad_w), grad_x = _jax.lax.scan(body, init, (per_example, given["loss_target"]))
    with _jax.named_scope("update"):
        delta_w, new_m, new_v = {}, {}, {}
        for n in TWIN_WEIGHTS:
            delta_w[n], new_m[n], new_v[n] = _adamw(weights[n], grad_w[n], given["m_" + n], given["v_" + n])
    return (loss, grad_x, *[grad_w[n] for n in TWIN_WEIGHTS], *[delta_w[n] for n in TWIN_WEIGHTS],
            *[new_m[n] for n in TWIN_WEIGHTS], *[new_v[n] for n in TWIN_WEIGHTS])
```

```python
import functools

import jax
import jax.numpy as jnp
from jax import lax
from jax.experimental import pallas as pl
from jax.experimental.pallas import tpu as pltpu

F32 = jnp.float32
BF16 = jnp.bfloat16

D = 1024
S = 2048
N_META = 16
PAD = 112
N = PAD + N_META + S
TR = 128
NT = N // TR
D_POOL = 512
POOL_WINDOWS = (2, 4, 8, 16)
POOL_GROUP = 128
HALO = 16
HEADS = 4
QK_NOPE = 128
QK_ROPE = 64
V_HEAD = 128
Q_LORA = 256
KV_LORA = 128
D_IN = 1984
EPS = 1e-6
ROPE_THETA = 10000.0
SCALE = (QK_NOPE + QK_ROPE) ** -0.5
CHIPS = 4

O_PI, O_PG, O_CQ, O_CKV, O_KR, O_AG = 0, 512, 1024, 1280, 1408, 1536
W_ROWS = 2048
SHARD_IN = D_IN // CHIPS
PIECE_A = 480
PIECE_B = SHARD_IN - PIECE_A
SHARD_OUT = D // CHIPS

LR, B1, B2, ADAM_EPS, WD, STEP = 0.001, 0.9, 0.999, 1e-08, 0.01, 10
C1 = 1.0 - B1**STEP
C2 = 1.0 - B2**STEP

VMEM_LIMIT = 56 * 1024 * 1024
MESH = pl.DeviceIdType.MESH
NEG = -1e30

SM_POOLW, SM_NORM, SM_FINAL, SM_MISC, SM_ROWS = 0, 64, 65, 66, 72
MISC_GQ, MISC_GKV, MISC_PS, MISC_LOSS = 0, 256, 384, 896


def _cparams(**kw):
    return pltpu.CompilerParams(vmem_limit_bytes=VMEM_LIMIT, **kw)


def _nt(a, b):
    return lax.dot_general(a, b, (((1,), (1,)), ((), ())), preferred_element_type=F32)


def _tn(a, b):
    return lax.dot_general(a, b, (((0,), (0,)), ((), ())), preferred_element_type=F32)


def _nn(a, b):
    return jnp.dot(a, b, preferred_element_type=F32)


def _swap64(t):
    return pltpu.roll(t, 32, 1) + pltpu.roll(t, 96, 1)


def _sigmoid(x):
    return 1.0 / (1.0 + jnp.exp(-x))


def _row_block(i):
    return jnp.maximum(i - 1, 0)


def _chunk_mask(j):
    col = lax.broadcasted_iota(jnp.int32, (1, N), 1)
    kchunk = jnp.where(col < PAD, 1 << 20, jnp.where(col < TR, 0, 1 + ((col - TR) >> 6)))
    row = lax.broadcasted_iota(jnp.int32, (TR, 1), 0)
    qchunk = jnp.where(j == 0, 0, 2 * j - 1 + (row >= 64).astype(jnp.int32))
    return kchunk <= qchunk


def _fwd_in(x2, h0, norm_g, win, gq, wq, gkv, wkv, cosf, sinf):
    def body(x_ref, h0_ref, g_ref, win_ref, gq_ref, wq_ref, gkv_ref, wkv_ref, cos_ref, sin_ref,
             pi_ref, pg_ref, cq_ref, ckv_ref, ag_ref, q_ref, k_ref, v_ref):
        i = pl.program_id(0)
        h = jnp.where(i == 0, h0_ref[...], x_ref[...])
        r = lax.rsqrt(jnp.mean(h * h, axis=-1, keepdims=True) + EPS)
        hn = ((h * r) * g_ref[...]).astype(BF16)
        u = _nt(hn, win_ref[...])
        pi_ref[...] = u[:, O_PI:O_PG]
        pg_ref[...] = u[:, O_PG:O_CQ]
        cq = u[:, O_CQ:O_CKV]
        ckv = u[:, O_CKV:O_KR]
        cq_ref[...] = cq
        ckv_ref[...] = ckv
        ag_ref[...] = u[:, O_AG:W_ROWS]
        cosv = cos_ref[...]
        sinv = sin_ref[...]
        kr = u[:, O_KR:O_AG]
        kr = (kr * cosv + _swap64(kr) * sinv).astype(BF16)
        rq = lax.rsqrt(jnp.mean(cq * cq, axis=-1, keepdims=True) + EPS)
        cqn = ((cq * rq) * gq_ref[...]).astype(BF16)
        rkv = lax.rsqrt(jnp.mean(ckv * ckv, axis=-1, keepdims=True) + EPS)
        ckvn = ((ckv * rkv) * gkv_ref[...]).astype(BF16)
        for hd in range(HEADS):
            qh = _nn(cqn, wq_ref[hd])
            z = qh[:, QK_NOPE:]
            q_ref[hd, :, 0:QK_NOPE] = qh[:, 0:QK_NOPE].astype(BF16)
            q_ref[hd, :, QK_NOPE:] = (z * cosv + _swap64(z) * sinv).astype(BF16)
            kvh = _nn(ckvn, wkv_ref[hd])
            k_ref[hd, :, 0:QK_NOPE] = kvh[:, 0:QK_NOPE].astype(BF16)
            k_ref[hd, :, QK_NOPE:] = kr
            v_ref[hd] = kvh[:, QK_NOPE:].astype(BF16)

    row = lambda w: pl.BlockSpec((TR, w), lambda i: (i, 0))
    full = lambda *s: pl.BlockSpec(s, lambda i: (0,) * len(s))
    head = lambda w: pl.BlockSpec((HEADS, TR, w), lambda i: (0, i, 0))
    return pl.pallas_call(
        body,
        name="fwd_in",
        grid=(NT,),
        in_specs=[
            pl.BlockSpec((TR, D), lambda i: (_row_block(i), 0)),
            full(TR, D), full(1, D), full(W_ROWS, D), full(1, Q_LORA), full(HEADS, Q_LORA, 256),
            full(1, KV_LORA), full(HEADS, KV_LORA, 256), row(128), row(128),
        ],
        out_specs=[row(D_POOL), row(D_POOL), row(Q_LORA), row(KV_LORA), row(D_POOL), head(256), head(256), head(V_HEAD)],
        out_shape=[
            jax.ShapeDtypeStruct((N, D_POOL), F32), jax.ShapeDtypeStruct((N, D_POOL), F32),
            jax.ShapeDtypeStruct((N, Q_LORA), F32), jax.ShapeDtypeStruct((N, KV_LORA), F32),
            jax.ShapeDtypeStruct((N, D_POOL), F32),
            jax.ShapeDtypeStruct((HEADS, N, 256), BF16), jax.ShapeDtypeStruct((HEADS, N, 256), BF16),
            jax.ShapeDtypeStruct((HEADS, N, V_HEAD), BF16),
        ],
        compiler_params=_cparams(dimension_semantics=("arbitrary",)),
    )(x2, h0, norm_g, win, gq, wq, gkv, wkv, cosf, sinf)


def _attn_fwd(q, k, v):
    def body(q_ref, k_ref, v_ref, o_ref, lse_ref):
        j = pl.program_id(1)
        s = _nt(q_ref[0], k_ref[0]) * SCALE
        s = jnp.where(_chunk_mask(j), s, NEG)
        m = jnp.max(s, axis=-1, keepdims=True)
        p = jnp.exp(s - m)
        l = jnp.sum(p, axis=-1, keepdims=True)
        o_ref[...] = _nn(p.astype(BF16), v_ref[0]) / l
        lse_ref[0] = m + jnp.log(l)

    return pl.pallas_call(
        body,
        name="attn_fwd",
        grid=(HEADS, NT),
        in_specs=[
            pl.BlockSpec((1, TR, 256), lambda h, j: (h, j, 0)),
            pl.BlockSpec((1, N, 256), lambda h, j: (h, 0, 0)),
            pl.BlockSpec((1, N, V_HEAD), lambda h, j: (h, 0, 0)),
        ],
        out_specs=[
            pl.BlockSpec((TR, V_HEAD), lambda h, j: (j, h)),
            pl.BlockSpec((1, TR, 1), lambda h, j: (h, j, 0)),
        ],
        out_shape=[jax.ShapeDtypeStruct((N, HEADS * V_HEAD), F32), jax.ShapeDtypeStruct((HEADS, N, 1), F32)],
        compiler_params=_cparams(dimension_semantics=("arbitrary", "arbitrary")),
    )(q, k, v)


def _inv_count(i, w):
    row = i * TR + lax.broadcasted_iota(jnp.int32, (TR, 1), 0)
    return 1.0 / jnp.clip(row - (PAD - 1), 1, w).astype(F32)


def _mid(x2, h0, tgt2, pool_in, pool_gate, attn_gate, attn, pool_w, pool_scale, wout, gf):
    def body(x_ref, h0_ref, t_ref, pin_ref, pg_ref, ag_ref, at_ref, pw_ref, ps_ref, wout_ref, gf_ref,
             dh2_ref, do_ref, dag_ref, dpg_ref, dpl_ref, dwout_ref, dpw_ref, dps_ref, dgf_ref, loss_ref):
        i = pl.program_id(0)

        @pl.when(i == 0)
        def _():
            dwout_ref[...] = jnp.zeros_like(dwout_ref)
            dpw_ref[...] = jnp.zeros_like(dpw_ref)
            dps_ref[...] = jnp.zeros_like(dps_ref)
            dgf_ref[...] = jnp.zeros_like(dgf_ref)
            loss_ref[...] = jnp.zeros_like(loss_ref)

        real = (i > 0).astype(F32)
        h = jnp.where(i == 0, h0_ref[...], x_ref[...])

        start = pl.multiple_of(i * TR, TR)
        halo = pin_ref[pl.ds(pl.multiple_of(jnp.maximum(i * TR - HALO, 0), HALO), HALO), :] * real
        ext = jnp.concatenate([halo, pin_ref[pl.ds(start, TR), :]], axis=0)
        pooled = []
        for g, w in enumerate(POOL_WINDOWS):
            e = ext[:, g * POOL_GROUP:(g + 1) * POOL_GROUP]
            acc = e
            shift = 1
            while shift < w:
                acc = acc + pltpu.roll(acc, shift, 0)
                shift *= 2
            pooled.append((acc[HALO:] * _inv_count(i, w) - e[HALO:]).astype(BF16))
        pw = [pw_ref[g].astype(BF16) for g in range(len(POOL_WINDOWS))]
        mixed = jnp.concatenate([_nn(pooled[g], pw[g]) for g in range(len(POOL_WINDOWS))], axis=1)
        ps = ps_ref[...]
        mixed_s = mixed * ps
        pg = pg_ref[...]
        sig_p = _sigmoid(pg)
        silu_p = pg * sig_p
        pool_out = (silu_p * mixed_s).astype(BF16)
        ag = ag_ref[...]
        sig_a = _sigmoid(ag)
        silu_a = ag * sig_a
        at = at_ref[...]
        attn_out = (silu_a * at).astype(BF16)
        mix = _nn(pool_out, wout_ref[0:D_POOL, :]) + _nn(attn_out, wout_ref[D_POOL:D, :])
        h2 = h + mix

        r2 = lax.rsqrt(jnp.mean(h2 * h2, axis=-1, keepdims=True) + EPS)
        n2 = h2 * r2
        gfv = gf_ref[...]
        err = (n2 * gfv - t_ref[...]) * real
        loss_ref[...] += jnp.sum(jnp.sum(err * err, axis=-1, keepdims=True), axis=0, keepdims=True) * (0.5 / D)
        dy = err * (1.0 / D)
        dgf_ref[...] += jnp.sum(dy * n2, axis=0, keepdims=True)
        dn = dy * gfv
        dh2 = r2 * (dn - n2 * jnp.mean(dn * n2, axis=-1, keepdims=True))
        dh2_ref[...] = dh2
        dh2b = dh2.astype(BF16)

        dwout_ref[0:D_POOL, :] += _tn(pool_out, dh2b)
        dwout_ref[D_POOL:D, :] += _tn(attn_out, dh2b)
        dcat = _nt(dh2b, wout_ref[...])
        dpo = dcat[:, 0:D_POOL]
        dao = dcat[:, D_POOL:D]
        do_ref[...] = dao * silu_a
        dag_ref[...] = dao * at * (sig_a * (1.0 + ag * (1.0 - sig_a)))
        dmixed_s = dpo * silu_p
        dpg_ref[...] = dpo * mixed_s * (sig_p * (1.0 + pg * (1.0 - sig_p)))
        dps_ref[...] += jnp.sum(dmixed_s * mixed, axis=0, keepdims=True)
        dmixed = (dmixed_s * ps).astype(BF16)
        dpl = []
        for g in range(len(POOL_WINDOWS)):
            dm = dmixed[:, g * POOL_GROUP:(g + 1) * POOL_GROUP]
            dpl.append(_nt(dm, pw[g]))
            dpw_ref[g] += _tn(pooled[g], dm)
        dpl_ref[...] = jnp.concatenate(dpl, axis=1)

    row = lambda w: pl.BlockSpec((TR, w), lambda i: (i, 0))
    frame = pl.BlockSpec((TR, D), lambda i: (_row_block(i), 0))
    full = lambda *s: pl.BlockSpec(s, lambda i: (0,) * len(s))
    return pl.pallas_call(
        body,
        name="mid",
        grid=(NT,),
        in_specs=[
            frame, full(TR, D), frame, full(N, D_POOL), row(D_POOL), row(D_POOL), row(D_POOL),
            full(len(POOL_WINDOWS), POOL_GROUP, POOL_GROUP), full(1, D_POOL), full(D, D), full(1, D),
        ],
        out_specs=[
            row(D), row(D_POOL), row(D_POOL), row(D_POOL), row(D_POOL),
            full(D, D), full(len(POOL_WINDOWS), POOL_GROUP, POOL_GROUP), full(1, D_POOL), full(1, D), full(1, 1),
        ],
        out_shape=[
            jax.ShapeDtypeStruct((N, D), F32), jax.ShapeDtypeStruct((N, D_POOL), F32),
            jax.ShapeDtypeStruct((N, D_POOL), F32), jax.ShapeDtypeStruct((N, D_POOL), F32),
            jax.ShapeDtypeStruct((N, D_POOL), F32), jax.ShapeDtypeStruct((D, D), F32),
            jax.ShapeDtypeStruct((len(POOL_WINDOWS), POOL_GROUP, POOL_GROUP), F32),
            jax.ShapeDtypeStruct((1, D_POOL), F32), jax.ShapeDtypeStruct((1, D), F32), jax.ShapeDtypeStruct((1, 1), F32),
        ],
        compiler_params=_cparams(dimension_semantics=("arbitrary",)),
    )(x2, h0, tgt2, pool_in, pool_gate, attn_gate, attn, pool_w, pool_scale, wout, gf)


def _attn_bwd(q, k, v, do, o, lse):
    def body(q_ref, k_ref, v_ref, do_ref, o_ref, lse_ref, dq_ref, dk_ref, dv_ref):
        j = pl.program_id(1)

        @pl.when(j == 0)
        def _():
            dk_ref[...] = jnp.zeros_like(dk_ref)
            dv_ref[...] = jnp.zeros_like(dv_ref)

        qv = q_ref[0]
        s = _nt(qv, k_ref[0]) * SCALE
        s = jnp.where(_chunk_mask(j), s, NEG)
        p = jnp.exp(s - lse_ref[0])
        dov = do_ref[...]
        dob = dov.astype(BF16)
        delta = jnp.sum(dov * o_ref[...], axis=-1, keepdims=True)
        dp = _nt(dob, v_ref[0])
        ds = (p * (dp - delta) * SCALE).astype(BF16)
        dq_ref[0] = _nn(ds, k_ref[0])
        dk_ref[0] += _tn(ds, qv)
        dv_ref[0] += _tn(p.astype(BF16), dob)

    return pl.pallas_call(
        body,
        name="attn_bwd",
        grid=(HEADS, NT),
        in_specs=[
            pl.BlockSpec((1, TR, 256), lambda h, j: (h, j, 0)),
            pl.BlockSpec((1, N, 256), lambda h, j: (h, 0, 0)),
            pl.BlockSpec((1, N, V_HEAD), lambda h, j: (h, 0, 0)),
            pl.BlockSpec((TR, V_HEAD), lambda h, j: (j, h)),
            pl.BlockSpec((TR, V_HEAD), lambda h, j: (j, h)),
            pl.BlockSpec((1, TR, 1), lambda h, j: (h, j, 0)),
        ],
        out_specs=[
            pl.BlockSpec((1, TR, 256), lambda h, j: (h, j, 0)),
            pl.BlockSpec((1, N, 256), lambda h, j: (h, 0, 0)),
            pl.BlockSpec((1, N, V_HEAD), lambda h, j: (h, 0, 0)),
        ],
        out_shape=[
            jax.ShapeDtypeStruct((HEADS, N, 256), F32), jax.ShapeDtypeStruct((HEADS, N, 256), F32),
            jax.ShapeDtypeStruct((HEADS, N, V_HEAD), F32),
        ],
        compiler_params=_cparams(dimension_semantics=("arbitrary", "arbitrary")),
    )(q, k, v, do, o, lse)


def _bwd_in(x2, h0, dh2, dq, dk, dv, cq, ckv, dpl, dpg, dag, norm_g, win, gq, wq, gkv, wkv, cosf, sinf):
    def body(x_ref, h0_ref, dh2_ref, dq_ref, dk_ref, dv_ref, cq_ref, ckv_ref, dpl_ref, dpg_ref, dag_ref,
             g_ref, win_ref, gq_ref, wq_ref, gkv_ref, wkv_ref, cos_ref, sin_ref,
             gx_ref, dmeta_ref, dwin_ref, dwq_ref, dwkv_ref, dg_ref, dgq_ref, dgkv_ref):
        i = pl.program_id(0)

        @pl.when(i == 0)
        def _():
            dwin_ref[...] = jnp.zeros_like(dwin_ref)
            dwq_ref[...] = jnp.zeros_like(dwq_ref)
            dwkv_ref[...] = jnp.zeros_like(dwkv_ref)
            dg_ref[...] = jnp.zeros_like(dg_ref)
            dgq_ref[...] = jnp.zeros_like(dgq_ref)
            dgkv_ref[...] = jnp.zeros_like(dgkv_ref)

        h = jnp.where(i == 0, h0_ref[...], x_ref[...])
        r = lax.rsqrt(jnp.mean(h * h, axis=-1, keepdims=True) + EPS)
        n = h * r
        gv = g_ref[...]
        hn = (n * gv).astype(BF16)
        cosv = cos_ref[...]
        sinv = sin_ref[...]
        low = (lax.broadcasted_iota(jnp.int32, (1, 128), 1) < QK_ROPE).astype(F32)

        def unrope(dy):
            return dy * cosv + _swap64(dy * sinv) * low

        cq = cq_ref[...]
        rq = lax.rsqrt(jnp.mean(cq * cq, axis=-1, keepdims=True) + EPS)
        nq = cq * rq
        gqv = gq_ref[...]
        cqn = (nq * gqv).astype(BF16)
        dcqn = jnp.zeros((TR, Q_LORA), F32)
        for hd in range(HEADS):
            dqh = dq_ref[hd]
            dqf = jnp.concatenate([dqh[:, 0:QK_NOPE], unrope(dqh[:, QK_NOPE:])], axis=1).astype(BF16)
            dcqn = dcqn + _nt(dqf, wq_ref[hd])
            dwq_ref[hd] += _tn(cqn, dqf)
        dgq_ref[...] += jnp.sum(dcqn * nq, axis=0, keepdims=True)
        dnq = dcqn * gqv
        dcq = rq * (dnq - nq * jnp.mean(dnq * nq, axis=-1, keepdims=True))

        ckv = ckv_ref[...]
        rkv = lax.rsqrt(jnp.mean(ckv * ckv, axis=-1, keepdims=True) + EPS)
        nkv = ckv * rkv
        gkvv = gkv_ref[...]
        ckvn = (nkv * gkvv).astype(BF16)
        dckvn = jnp.zeros((TR, KV_LORA), F32)
        dkr = jnp.zeros((TR, 128), F32)
        for hd in range(HEADS):
            dkh = dk_ref[hd]
            dkr = dkr + dkh[:, QK_NOPE:]
            dkv = jnp.concatenate([dkh[:, 0:QK_NOPE], dv_ref[hd]], axis=1).astype(BF16)
            dckvn = dckvn + _nt(dkv, wkv_ref[hd])
            dwkv_ref[hd] += _tn(ckvn, dkv)
        dgkv_ref[...] += jnp.sum(dckvn * nkv, axis=0, keepdims=True)
        dnkv = dckvn * gkvv
        dckv = rkv * (dnkv - nkv * jnp.mean(dnkv * nkv, axis=-1, keepdims=True))
        dkr = unrope(dkr)

        start = pl.multiple_of(i * TR, TR)
        nxt = pl.multiple_of(jnp.minimum(i * TR + TR, N - HALO), HALO)
        last = (i < NT - 1).astype(F32)
        cur = dpl_ref[pl.ds(start, TR), :]
        halo = dpl_ref[pl.ds(nxt, HALO), :] * last
        dpi = []
        for g, w in enumerate(POOL_WINDOWS):
            sl = slice(g * POOL_GROUP, (g + 1) * POOL_GROUP)
            a = jnp.concatenate([cur[:, sl] * _inv_count(i, w), halo[:, sl] * _inv_count(i + 1, w)[0:HALO]], axis=0)
            acc = a
            shift = 1
            while shift < w:
                acc = acc + pltpu.roll(acc, TR + HALO - shift, 0)
                shift *= 2
            dpi.append(acc[0:TR] - cur[:, sl])

        du = jnp.concatenate(dpi + [dpg_ref[...], dcq, dckv, dkr, dag_ref[...]], axis=1).astype(BF16)
        dwin_ref[...] += _tn(du, hn)
        dhn = _nn(du, win_ref[...])
        dg_ref[...] += jnp.sum(dhn * n, axis=0, keepdims=True)
        dn = dhn * gv
        dh = dh2_ref[...] + r * (dn - n * jnp.mean(dn * n, axis=-1, keepdims=True))
        gx_ref[...] = dh

        @pl.when(i == 0)
        def _():
            dmeta_ref[...] = dh

    row = lambda w: pl.BlockSpec((TR, w), lambda i: (i, 0))
    frame = pl.BlockSpec((TR, D), lambda i: (_row_block(i), 0))
    full = lambda *s: pl.BlockSpec(s, lambda i: (0,) * len(s))
    head = lambda w: pl.BlockSpec((HEADS, TR, w), lambda i: (0, i, 0))
    return pl.pallas_call(
        body,
        name="bwd_in",
        grid=(NT,),
        in_specs=[
            frame, full(TR, D), row(D), head(256), head(256), head(V_HEAD), row(Q_LORA), row(KV_LORA),
            full(N, D_POOL), row(D_POOL), row(D_POOL),
            full(1, D), full(W_ROWS, D), full(1, Q_LORA), full(HEADS, Q_LORA, 256),
            full(1, KV_LORA), full(HEADS, KV_LORA, 256), row(128), row(128),
        ],
        out_specs=[
            frame, full(TR, D), full(W_ROWS, D), full(HEADS, Q_LORA, 256), full(HEADS, KV_LORA, 256),
            full(1, D), full(1, Q_LORA), full(1, KV_LORA),
        ],
        out_shape=[
            jax.ShapeDtypeStruct((S, D), F32), jax.ShapeDtypeStruct((TR, D), F32),
            jax.ShapeDtypeStruct((W_ROWS, D), F32), jax.ShapeDtypeStruct((HEADS, Q_LORA, 256), F32),
            jax.ShapeDtypeStruct((HEADS, KV_LORA, 256), F32),
            jax.ShapeDtypeStruct((1, D), F32), jax.ShapeDtypeStruct((1, Q_LORA), F32), jax.ShapeDtypeStruct((1, KV_LORA), F32),
        ],
        compiler_params=_cparams(dimension_semantics=("arbitrary",)),
    )(x2, h0, dh2, dq, dk, dv, cq, ckv, dpl, dpg, dag, norm_g, win, gq, wq, gkv, wkv, cosf, sinf)


def _local_step(x2, tgt2, h0, norm_g, win, gq, wq, gkv, wkv, pool_w, pool_scale, wout, gf, cosf, sinf):
    pool_in, pool_gate, cq, ckv, attn_gate, q, k, v = _fwd_in(x2, h0, norm_g, win, gq, wq, gkv, wkv, cosf, sinf)
    attn, lse = _attn_fwd(q, k, v)
    dh2, do, dag, dpg, dpl, dwout, dpw, dps, dgf, loss = _mid(
        x2, h0, tgt2, pool_in, pool_gate, attn_gate, attn, pool_w, pool_scale, wout, gf)
    dq, dk, dv = _attn_bwd(q, k, v, do, attn, lse)
    gx, dmeta, dwin, dwq, dwkv, dg, dgq, dgkv = _bwd_in(
        x2, h0, dh2, dq, dk, dv, cq, ckv, dpl, dpg, dag, norm_g, win, gq, wq, gkv, wkv, cosf, sinf)
    return dict(gx=gx, dmeta=dmeta, dwin=dwin, dwq=dwq, dwkv=dwkv, dwout=dwout, dg=dg, dgq=dgq, dgkv=dgkv,
                dpw=dpw, dps=dps, dgf=dgf, loss=loss)


_CHIP_FLIPS = ((1, 0), (0, 1), (1, 1))


def _in_offsets(chip):
    a = SHARD_IN * chip + 64 * (chip == 3).astype(jnp.int32)
    b = SHARD_IN * chip + PIECE_A + 64 * (chip >= 2).astype(jnp.int32)
    return pl.multiple_of(a, 16), pl.multiple_of(b, 16)


def _gather_weights(winT_s, wq_s, wkv_s, wout_s, meta_s):
    n_arr = 6

    def body(win_ref, wq_ref, wkv_ref, wout_ref, meta_ref,
             win_o, wq_o, wkv_o, wout_o, meta_o, s_win, s_wq, s_wkv, s_wout, send_sems, recv_sems):
        x, y, c = lax.axis_index("x"), lax.axis_index("y"), lax.axis_index("c")
        me = 2 * x + y
        s_win[...] = win_ref[...].astype(BF16)
        s_wq[...] = wq_ref[...].astype(BF16)
        s_wkv[...] = wkv_ref[...].astype(BF16)
        s_wout[...] = wout_ref[...].astype(BF16)

        def copies(src_chip, to, p):
            a, b = _in_offsets(src_chip)
            pairs = [
                (s_win.at[pl.ds(0, PIECE_A), :], win_o.at[pl.ds(a, PIECE_A), :]),
                (s_win.at[pl.ds(PIECE_A, PIECE_B), :], win_o.at[pl.ds(b, PIECE_B), :]),
                (s_wq, wq_o.at[src_chip]),
                (s_wkv, wkv_o.at[src_chip]),
                (s_wout, wout_o.at[pl.ds(pl.multiple_of(SHARD_OUT * src_chip, SHARD_OUT), SHARD_OUT), :]),
                (meta_ref, meta_o.at[src_chip]),
            ]
            return [
                pltpu.make_async_remote_copy(src_ref=s, dst_ref=d, send_sem=send_sems.at[p * n_arr + t],
                                             recv_sem=recv_sems.at[p * n_arr + t], device_id=to, device_id_type=MESH)
                for t, (s, d) in enumerate(pairs)
            ]

        sends = []
        for p, (fx, fy) in enumerate(_CHIP_FLIPS):
            sends += copies(me, (x ^ fx, y ^ fy, c), p)
        for cp in sends:
            cp.start()

        a, b = _in_offsets(me)
        win_o[pl.ds(a, PIECE_A), :] = s_win[0:PIECE_A, :]
        win_o[pl.ds(b, PIECE_B), :] = s_win[PIECE_A:SHARD_IN, :]
        win_o[O_KR + QK_ROPE:O_AG, :] = jnp.zeros((O_AG - O_KR - QK_ROPE, D), BF16)
        wq_o[me] = s_wq[...]
        wkv_o[me] = s_wkv[...]
        wout_o[pl.ds(pl.multiple_of(SHARD_OUT * me, SHARD_OUT), SHARD_OUT), :] = s_wout[...]
        meta_o[me] = meta_ref[...]

        for p, (fx, fy) in enumerate(_CHIP_FLIPS):
            peer = 2 * (x ^ fx) + (y ^ fy)
            for cp in copies(peer, (x, y, c), p):
                cp.wait_recv()
        for cp in sends:
            cp.wait_send()

    vm = pl.BlockSpec(memory_space=pltpu.VMEM)
    return pl.pallas_call(
        body,
        name="gather_weights",
        in_specs=[vm] * 5,
        out_specs=[vm] * 5,
        out_shape=[
            jax.ShapeDtypeStruct((W_ROWS, D), BF16), jax.ShapeDtypeStruct((CHIPS, Q_LORA, 256), BF16),
            jax.ShapeDtypeStruct((CHIPS, KV_LORA, 256), BF16), jax.ShapeDtypeStruct((D, D), BF16),
            jax.ShapeDtypeStruct((CHIPS, N_META, 256), F32),
        ],
        scratch_shapes=[
            pltpu.VMEM((SHARD_IN, D), BF16), pltpu.VMEM((Q_LORA, 256), BF16), pltpu.VMEM((KV_LORA, 256), BF16),
            pltpu.VMEM((SHARD_OUT, D), BF16),
            pltpu.SemaphoreType.DMA((3 * n_arr,)), pltpu.SemaphoreType.DMA((3 * n_arr,)),
        ],
        compiler_params=_cparams(),
    )(winT_s, wq_s, wkv_s, wout_s, meta_s)


def _reduce_grads(dwin, dwq, dwkv, dwout, dmeta4, small):
    n_arr = 5

    def body(dwin_ref, dwq_ref, dwkv_ref, dwout_ref, dmeta_ref, sm_ref,
             gwin_o, gwq_o, gwkv_o, gwout_o, gmeta_o, gsm_o,
             st_win, st_wq, st_wkv, st_wout, rc_win, rc_wq, rc_wkv, rc_wout, rc_meta,
             pc_win, pc_wq, pc_wkv, pc_wout, pc_meta, sb_win, sb_wq, sb_wkv, sb_wout, sb_meta, rc_sm,
             send_sems, recv_sems, sib_send, sib_recv, sm_send, sm_recv):
        x, y, c = lax.axis_index("x"), lax.axis_index("y"), lax.axis_index("c")
        me = 2 * x + y

        def shard_rows(chip):
            a, b = _in_offsets(chip)
            return pl.ds(pl.multiple_of(a, 16), PIECE_A), pl.ds(pl.multiple_of(b, 16), PIECE_B)

        def out_rows(chip):
            return pl.ds(pl.multiple_of(SHARD_OUT * chip, SHARD_OUT), SHARD_OUT)

        sends = []
        for p, (fx, fy) in enumerate(_CHIP_FLIPS):
            peer = 2 * (x ^ fx) + (y ^ fy)
            ra, rb = shard_rows(peer)
            st_win[p, 0:PIECE_A, :] = dwin_ref[ra, :].astype(BF16)
            st_win[p, PIECE_A:SHARD_IN, :] = dwin_ref[rb, :].astype(BF16)
            st_wq[p] = dwq_ref[peer].astype(BF16)
            st_wkv[p] = dwkv_ref[peer].astype(BF16)
            st_wout[p] = dwout_ref[out_rows(peer), :].astype(BF16)
            to = (x ^ fx, y ^ fy, c)
            pairs = [(st_win.at[p], rc_win.at[p]), (st_wq.at[p], rc_wq.at[p]), (st_wkv.at[p], rc_wkv.at[p]),
                     (st_wout.at[p], rc_wout.at[p]), (dmeta_ref.at[peer], rc_meta.at[p])]
            for t, (s, d) in enumerate(pairs):
                cp = pltpu.make_async_remote_copy(src_ref=s, dst_ref=d, send_sem=send_sems.at[p * n_arr + t],
                                                  recv_sem=recv_sems.at[p * n_arr + t], device_id=to, device_id_type=MESH)
                cp.start()
                sends.append(cp)

        sm_sends = []
        for rel in range(1, 8):
            fx, fy, fc = (rel >> 2) & 1, (rel >> 1) & 1, rel & 1
            cp = pltpu.make_async_remote_copy(src_ref=sm_ref, dst_ref=rc_sm.at[rel - 1], send_sem=sm_send.at[rel - 1],
                                              recv_sem=sm_recv.at[rel - 1], device_id=(x ^ fx, y ^ fy, c ^ fc),
                                              device_id_type=MESH)
            cp.start()
            sm_sends.append(cp)

        for cp in sends:
            cp.wait_recv()
        ra, rb = shard_rows(me)
        acc_a = dwin_ref[ra, :]
        acc_b = dwin_ref[rb, :]
        acc_q = dwq_ref[me]
        acc_kv = dwkv_ref[me]
        acc_out = dwout_ref[out_rows(me), :]
        acc_meta = dmeta_ref[me]
        for p in range(3):
            acc_a = acc_a + rc_win[p, 0:PIECE_A, :].astype(F32)
            acc_b = acc_b + rc_win[p, PIECE_A:SHARD_IN, :].astype(F32)
            acc_q = acc_q + rc_wq[p].astype(F32)
            acc_kv = acc_kv + rc_wkv[p].astype(F32)
            acc_out = acc_out + rc_wout[p].astype(F32)
            acc_meta = acc_meta + rc_meta[p]
        pc_win[0:PIECE_A, :] = acc_a
        pc_win[PIECE_A:SHARD_IN, :] = acc_b
        pc_wq[...] = acc_q
        pc_wkv[...] = acc_kv
        pc_wout[...] = acc_out
        pc_meta[...] = acc_meta

        sib = []
        for t, (s, d) in enumerate([(pc_win, sb_win), (pc_wq, sb_wq), (pc_wkv, sb_wkv), (pc_wout, sb_wout), (pc_meta, sb_meta)]):
            cp = pltpu.make_async_remote_copy(src_ref=s, dst_ref=d, send_sem=sib_send.at[t], recv_sem=sib_recv.at[t],
                                              device_id=(x, y, 1 - c), device_id_type=MESH)
            cp.start()
            sib.append(cp)
        for cp in sib:
            cp.wait_recv()
        gwin_o[...] = pc_win[...] + sb_win[...]
        gwq_o[...] = pc_wq[...] + sb_wq[...]
        gwkv_o[...] = pc_wkv[...] + sb_wkv[...]
        gwout_o[...] = pc_wout[...] + sb_wout[...]
        gmeta_o[...] = pc_meta[...] + sb_meta[...]

        for cp in sm_sends:
            cp.wait_recv()
        my_id = 4 * x + 2 * y + c
        total = jnp.zeros((SM_ROWS, D), F32)
        for dev in range(8):
            rel = dev ^ my_id
            theirs = rc_sm[jnp.maximum(rel - 1, 0)]
            total = total + jnp.where(rel == 0, sm_ref[...], theirs)
        gsm_o[...] = total

        for cp in sends + sm_sends + sib:
            cp.wait_send()

    vm = pl.BlockSpec(memory_space=pltpu.VMEM)
    shard_shapes = [((SHARD_IN, D)), ((Q_LORA, 256)), ((KV_LORA, 256)), ((SHARD_OUT, D))]
    return pl.pallas_call(
        body,
        name="reduce_grads",
        in_specs=[vm] * 6,
        out_specs=[vm] * 6,
        out_shape=[jax.ShapeDtypeStruct(s, F32) for s in shard_shapes]
        + [jax.ShapeDtypeStruct((N_META, 256), F32), jax.ShapeDtypeStruct((SM_ROWS, D), F32)],
        scratch_shapes=[pltpu.VMEM((3,) + s, BF16) for s in shard_shapes]
        + [pltpu.VMEM((3,) + s, BF16) for s in shard_shapes] + [pltpu.VMEM((3, N_META, 256), F32)]
        + [pltpu.VMEM(s, F32) for s in shard_shapes] + [pltpu.VMEM((N_META, 256), F32)]
        + [pltpu.VMEM(s, F32) for s in shard_shapes] + [pltpu.VMEM((N_META, 256), F32)]
        + [pltpu.VMEM((7, SM_ROWS, D), F32)]
        + [pltpu.SemaphoreType.DMA((3 * n_arr,)), pltpu.SemaphoreType.DMA((3 * n_arr,)),
           pltpu.SemaphoreType.DMA((n_arr,)), pltpu.SemaphoreType.DMA((n_arr,)),
           pltpu.SemaphoreType.DMA((7,)), pltpu.SemaphoreType.DMA((7,))],
        compiler_params=_cparams(),
    )(dwin, dwq, dwkv, dwout, dmeta4, small)


def _adamw_math(w, g, m, v):
    m = B1 * m + (1.0 - B1) * g
    v = B2 * v + (1.0 - B2) * (g * g)
    m_hat = m / C1
    v_hat = v / C2
    delta = -LR * (m_hat / (jnp.sqrt(v_hat) + ADAM_EPS) + WD * w)
    return delta, m, v


def _adamw_rows(name, w, g, m, v, block_rows):
    rows, cols = w.shape

    def body(w_ref, g_ref, m_ref, v_ref, d_ref, nm_ref, nv_ref):
        d_ref[...], nm_ref[...], nv_ref[...] = _adamw_math(w_ref[...], g_ref[...], m_ref[...], v_ref[...])

    spec = pl.BlockSpec((block_rows, cols), lambda i: (i, 0))
    return pl.pallas_call(
        body,
        name=name,
        grid=(rows // block_rows,),
        in_specs=[spec] * 4,
        out_specs=[spec] * 3,
        out_shape=[jax.ShapeDtypeStruct(w.shape, F32)] * 3,
        compiler_params=_cparams(dimension_semantics=("arbitrary",)),
    )(w, g, m, v)


def _adamw_small(groups):
    n = len(groups)

    def body(*refs):
        ins, outs = refs[:4 * n], refs[4 * n:]
        for t in range(n):
            w_ref, g_ref, m_ref, v_ref = ins[4 * t:4 * t + 4]
            outs[3 * t][...], outs[3 * t + 1][...], outs[3 * t + 2][...] = _adamw_math(
                w_ref[...], g_ref[...], m_ref[...], v_ref[...])

    vm = pl.BlockSpec(memory_space=pltpu.VMEM)
    flat = [a for grp in groups for a in grp]
    outs = pl.pallas_call(
        body,
        name="adamw_small",
        in_specs=[vm] * (4 * n),
        out_specs=[vm] * (3 * n),
        out_shape=[jax.ShapeDtypeStruct(grp[0].shape, F32) for grp in groups for _ in range(3)],
        compiler_params=_cparams(),
    )(*flat)
    return [tuple(outs[3 * t:3 * t + 3]) for t in range(n)]


def _rope_tables():
    half = QK_ROPE // 2
    inv_freq = 1.0 / (ROPE_THETA ** (jnp.arange(half, dtype=F32) / half))
    pos = jnp.arange(N, dtype=F32) - PAD
    ang = pos[:, None] * inv_freq[None, :]
    cos, sin = jnp.cos(ang), jnp.sin(ang)
    zero = jnp.zeros((N, 128 - QK_ROPE), F32)
    return jnp.concatenate([cos, cos, zero], axis=1), jnp.concatenate([-sin, sin, zero], axis=1)


def kernel(x, meta_tokens, norm_g, w_in, q_norm_g, w_q_b, kv_norm_g, w_kv_b, pool_w, pool_scale, w_out, final_norm_g, loss_target, m_meta_tokens, m_norm_g, m_w_in, m_q_norm_g, m_w_q_b, m_kv_norm_g, m_w_kv_b, m_pool_w, m_pool_scale, m_w_out, m_final_norm_g, v_meta_tokens, v_norm_g, v_w_in, v_q_norm_g, v_w_q_b, v_kv_norm_g, v_w_kv_b, v_pool_w, v_pool_scale, v_w_out, v_final_norm_g):
    winT_s = w_in[0].T
    wq_s = jnp.pad(w_q_b[0], ((0, 0), (0, 256 - QK_NOPE - QK_ROPE)))
    win, wq, wkv, wout, meta4 = _gather_weights(winT_s, wq_s, w_kv_b[0], w_out[0], meta_tokens)
    meta_full = jnp.transpose(meta4, (1, 0, 2)).reshape(N_META, D)
    h0 = jnp.concatenate([jnp.zeros((PAD, D), F32), meta_full], axis=0)
    cosf, sinf = _rope_tables()
    gf = final_norm_g.reshape(1, D)

    part = _local_step(x[0], loss_target[0], h0, norm_g, win, q_norm_g, wq, kv_norm_g, wkv, pool_w[0], pool_scale,
                       wout, gf, cosf, sinf)

    misc = jnp.concatenate([part["dgq"], part["dgkv"], part["dps"], part["loss"],
                            jnp.zeros((1, D - MISC_LOSS - 1), F32)], axis=1)
    small = jnp.concatenate([part["dpw"].reshape(64, D), part["dg"], part["dgf"], misc,
                             jnp.zeros((SM_ROWS - SM_MISC - 1, D), F32)], axis=0)
    dmeta4 = jnp.transpose(part["dmeta"][PAD:TR].reshape(N_META, CHIPS, 256), (1, 0, 2))
    gwinT, gwq, gwkv, gwout, gmeta, gsm = _reduce_grads(part["dwin"], part["dwq"], part["dwkv"], part["dwout"], dmeta4, small)

    loss = gsm[SM_MISC, MISC_LOSS]
    g_w_in = gwinT.T
    g_w_q_b = gwq[:, 0:QK_NOPE + QK_ROPE]
    g_pool_w = gsm[SM_POOLW:SM_POOLW + 64].reshape(4 * POOL_GROUP, POOL_GROUP)
    g_norm = gsm[SM_NORM:SM_NORM + 1]
    g_final = gsm[SM_FINAL:SM_FINAL + 1]
    g_gq = gsm[SM_MISC:SM_MISC + 1, MISC_GQ:MISC_GQ + Q_LORA]
    g_gkv = gsm[SM_MISC:SM_MISC + 1, MISC_GKV:MISC_GKV + KV_LORA]
    g_ps = gsm[SM_MISC:SM_MISC + 1, MISC_PS:MISC_PS + D_POOL]

    d_in, nm_in, nv_in = _adamw_rows("adamw_w_in", w_in[0], g_w_in, m_w_in[0], v_w_in[0], 256)
    d_out, nm_out, nv_out = _adamw_rows("adamw_w_out", w_out[0], gwout, m_w_out[0], v_w_out[0], 128)
    pw2 = lambda a: a.reshape(4 * POOL_GROUP, POOL_GROUP)
    fn2 = lambda a: a.reshape(1, D)
    res = _adamw_small([
        (meta_tokens, gmeta, m_meta_tokens, v_meta_tokens),
        (norm_g, g_norm, m_norm_g, v_norm_g),
        (q_norm_g, g_gq, m_q_norm_g, v_q_norm_g),
        (w_q_b[0], g_w_q_b, m_w_q_b[0], v_w_q_b[0]),
        (kv_norm_g, g_gkv, m_kv_norm_g, v_kv_norm_g),
        (w_kv_b[0], gwkv, m_w_kv_b[0], v_w_kv_b[0]),
        (pw2(pool_w), g_pool_w, pw2(m_pool_w), pw2(v_pool_w)),
        (pool_scale, g_ps, m_pool_scale, v_pool_scale),
        (fn2(final_norm_g), g_final, fn2(m_final_norm_g), fn2(v_final_norm_g)),
    ])
    (r_meta, r_norm, r_gq, r_wq, r_gkv, r_wkv, r_pw, r_ps, r_fn) = res
    pw4 = lambda a: a.reshape(1, 4, POOL_GROUP, POOL_GROUP)

    grads = [gmeta, g_norm, g_w_in[None], g_gq, g_w_q_b[None], g_gkv, gwkv[None], pw4(g_pool_w), g_ps, gwout[None],
             g_final.reshape(D)]
    per_kind = []
    for kind in range(3):
        per_kind.append([
            r_meta[kind], r_norm[kind], (d_in, nm_in, nv_in)[kind][None], r_gq[kind], r_wq[kind][None], r_gkv[kind],
            r_wkv[kind][None], pw4(r_pw[kind]), r_ps[kind], (d_out, nm_out, nv_out)[kind][None], r_fn[kind].reshape(D),
        ])
    return (loss, part["gx"][None], *grads, *per_kind[0], *per_kind[1], *per_kind[2])
```

```python
import functools

import jax
import jax.numpy as jnp
from jax import lax
from jax.experimental import pallas as pl
from jax.experimental.pallas import tpu as pltpu

F32 = jnp.float32
BF16 = jnp.bfloat16

D = 1024
S = 2048
N_META = 16
PAD = 112
N = PAD + N_META + S
TR = 128
NT = N // TR
D_POOL = 512
POOL_WINDOWS = (2, 4, 8, 16)
POOL_GROUP = 128
HALO = 16
HEADS = 4
QK_NOPE = 128
QK_ROPE = 64
V_HEAD = 128
Q_LORA = 256
KV_LORA = 128
D_IN = 1984
EPS = 1e-6
ROPE_THETA = 10000.0
SCALE = (QK_NOPE + QK_ROPE) ** -0.5
CHIPS = 4

O_PI, O_PG, O_CQ, O_CKV, O_KR, O_AG = 0, 512, 1024, 1280, 1408, 1472
O_KR_END = O_KR + 128
SHARD_IN = D_IN // CHIPS
SHARD_OUT = D // CHIPS
TQ = 256
NQ = S // TQ

LR, B1, B2, ADAM_EPS, WD, STEP = 0.001, 0.9, 0.999, 1e-08, 0.01, 10
C1 = 1.0 - B1**STEP
C2 = 1.0 - B2**STEP

VMEM_LIMIT = 56 * 1024 * 1024
MESH = pl.DeviceIdType.MESH
NEG = -1e30

SM_POOLW, SM_NORM, SM_FINAL, SM_MISC, SM_ROWS = 0, 64, 65, 66, 72
SM_ROWS_CORE0 = 40
MISC_GQ, MISC_GKV, MISC_PS, MISC_LOSS = 0, 256, 384, 896


def _cparams(**kw):
    return pltpu.CompilerParams(vmem_limit_bytes=VMEM_LIMIT, **kw)


def _nt(a, b):
    return lax.dot_general(a, b, (((1,), (1,)), ((), ())), preferred_element_type=F32)


def _tn(a, b):
    return lax.dot_general(a, b, (((0,), (0,)), ((), ())), preferred_element_type=F32)


def _nn(a, b):
    return jnp.dot(a, b, preferred_element_type=F32)


def _swap64(t):
    return pltpu.roll(t, 32, 1) + pltpu.roll(t, 96, 1)


def _sigmoid(x):
    return 1.0 / (1.0 + jnp.exp(-x))


def _row_block(i):
    return jnp.maximum(i - 1, 0)


def _low_lanes():
    return (lax.broadcasted_iota(jnp.int32, (1, 128), 1) < QK_ROPE).astype(F32)


def _attn_tiles():
    return [(0, TR, TR)] + [(TR + TQ * t, TQ, TR + TQ * (t + 1)) for t in range(NQ)]


def _masked_scores(q, k, rows, klen):
    s = _nt(q, k)
    col = lax.broadcasted_iota(jnp.int32, (1, TR), 1)
    head_bias = jnp.where(col >= PAD, 0.0, NEG)
    if klen == TR:
        return s + head_bias
    r = lax.broadcasted_iota(jnp.int32, (rows, 1), 0) >> 6
    c = lax.broadcasted_iota(jnp.int32, (1, rows), 1) >> 6
    diag_bias = jnp.where(c <= r, 0.0, NEG)
    parts = [s[:, 0:TR] + head_bias]
    if klen - rows > TR:
        parts.append(s[:, TR:klen - rows])
    parts.append(s[:, klen - rows:klen] + diag_bias)
    return jnp.concatenate(parts, axis=1)


def _fwd_in(x2, h0, norm_g, win, gq, wq, gkv, wkv, cosf, sinf):
    def body(x_ref, h0_ref, g_ref, win_ref, gq_ref, wq_ref, gkv_ref, wkv_ref, cos_ref, sin_ref,
             pi_ref, pg_ref, cq_ref, ckv_ref, ag_ref, q_ref, k_ref, v_ref):
        i = pl.program_id(0)
        h = jnp.where(i == 0, h0_ref[...], x_ref[...])
        r = lax.rsqrt(jnp.mean(h * h, axis=-1, keepdims=True) + EPS)
        hn = ((h * r) * g_ref[...]).astype(BF16)
        u = _nt(hn, win_ref[0:O_KR_END, :])
        pi_ref[...] = u[:, O_PI:O_PG]
        pg_ref[...] = u[:, O_PG:O_CQ]
        cq = u[:, O_CQ:O_CKV]
        ckv = u[:, O_CKV:O_KR]
        cq_ref[...] = cq
        ckv_ref[...] = ckv
        ag_ref[...] = _nt(hn, win_ref[O_AG:D_IN, :])
        cosv = cos_ref[...]
        sinv = sin_ref[...]
        kr = u[:, O_KR:O_KR_END] * _low_lanes()
        kr = (kr * cosv + _swap64(kr) * sinv).astype(BF16)
        rq = lax.rsqrt(jnp.mean(cq * cq, axis=-1, keepdims=True) + EPS)
        cqn = ((cq * rq) * gq_ref[...]).astype(BF16)
        rkv = lax.rsqrt(jnp.mean(ckv * ckv, axis=-1, keepdims=True) + EPS)
        ckvn = ((ckv * rkv) * gkv_ref[...]).astype(BF16)
        for hd in range(HEADS):
            qh = _nn(cqn, wq_ref[hd]) * SCALE
            z = qh[:, QK_NOPE:]
            q_ref[hd, :, 0:QK_NOPE] = qh[:, 0:QK_NOPE].astype(BF16)
            q_ref[hd, :, QK_NOPE:] = (z * cosv + _swap64(z) * sinv).astype(BF16)
            kvh = _nn(ckvn, wkv_ref[hd])
            k_ref[hd, :, 0:QK_NOPE] = kvh[:, 0:QK_NOPE].astype(BF16)
            k_ref[hd, :, QK_NOPE:] = kr
            v_ref[hd] = kvh[:, QK_NOPE:].astype(BF16)

    row = lambda w: pl.BlockSpec((TR, w), lambda i: (i, 0))
    full = lambda *s: pl.BlockSpec(s, lambda i: (0,) * len(s))
    head = lambda w: pl.BlockSpec((HEADS, TR, w), lambda i: (0, i, 0))
    return pl.pallas_call(
        body,
        name="fwd_in",
        grid=(NT,),
        in_specs=[
            pl.BlockSpec((TR, D), lambda i: (_row_block(i), 0)),
            full(TR, D), full(1, D), full(D_IN, D), full(1, Q_LORA), full(HEADS, Q_LORA, 256),
            full(1, KV_LORA), full(HEADS, KV_LORA, 256), row(128), row(128),
        ],
        out_specs=[row(D_POOL), row(D_POOL), row(Q_LORA), row(KV_LORA), row(D_POOL), head(256), head(256), head(V_HEAD)],
        out_shape=[
            jax.ShapeDtypeStruct((N, D_POOL), F32), jax.ShapeDtypeStruct((N, D_POOL), F32),
            jax.ShapeDtypeStruct((N, Q_LORA), F32), jax.ShapeDtypeStruct((N, KV_LORA), F32),
            jax.ShapeDtypeStruct((N, D_POOL), F32),
            jax.ShapeDtypeStruct((HEADS, N, 256), BF16), jax.ShapeDtypeStruct((HEADS, N, 256), BF16),
            jax.ShapeDtypeStruct((HEADS, N, V_HEAD), BF16),
        ],
        compiler_params=_cparams(dimension_semantics=("arbitrary",)),
    )(x2, h0, norm_g, win, gq, wq, gkv, wkv, cosf, sinf)


def _attn_fwd(q, k, v):
    tiles = _attn_tiles()

    def body(q_ref, k_ref, v_ref, o_ref, lse_ref):
        step = pl.program_id(1)
        for idx, (q0, rows, klen) in enumerate(tiles):
            @pl.when(step == idx)
            def _(q0=q0, rows=rows, klen=klen):
                s = _masked_scores(q_ref[0, q0:q0 + rows, :], k_ref[0, 0:klen, :], rows, klen)
                m = jnp.max(s, axis=-1, keepdims=True)
                p = jnp.exp(s - m)
                l = jnp.sum(p, axis=-1, keepdims=True)
                o_ref[q0:q0 + rows, :] = _nn(p.astype(BF16), v_ref[0, 0:klen, :]) / l
                lse_ref[0, q0:q0 + rows, :] = m + jnp.log(l)

    per_head = lambda w: pl.BlockSpec((1, N, w), lambda h, t: (h, 0, 0))
    return pl.pallas_call(
        body,
        name="attn_fwd",
        grid=(HEADS, len(tiles)),
        in_specs=[per_head(256), per_head(256), per_head(V_HEAD)],
        out_specs=[pl.BlockSpec((N, V_HEAD), lambda h, t: (0, h)), per_head(1)],
        out_shape=[jax.ShapeDtypeStruct((N, HEADS * V_HEAD), F32), jax.ShapeDtypeStruct((HEADS, N, 1), F32)],
        compiler_params=_cparams(dimension_semantics=("arbitrary", "arbitrary")),
    )(q, k, v)


def _inv_count(i, w):
    row = i * TR + lax.broadcasted_iota(jnp.int32, (TR, 1), 0)
    return 1.0 / jnp.clip(row - (PAD - 1), 1, w).astype(F32)


def _mid(x2, h0, tgt2, pool_in, pool_gate, attn_gate, attn, pool_w, pool_scale, wout, gf):
    def body(x_ref, h0_ref, t_ref, pin_ref, pg_ref, ag_ref, at_ref, pw_ref, ps_ref, wout_ref, gf_ref,
             dh2_ref, do_ref, dag_ref, dpg_ref, dpl_ref, dwout_ref, dpw_ref, dps_ref, dgf_ref, loss_ref):
        i = pl.program_id(0)

        @pl.when(i == 0)
        def _():
            dwout_ref[...] = jnp.zeros_like(dwout_ref)
            dpw_ref[...] = jnp.zeros_like(dpw_ref)
            dps_ref[...] = jnp.zeros_like(dps_ref)
            dgf_ref[...] = jnp.zeros_like(dgf_ref)
            loss_ref[...] = jnp.zeros_like(loss_ref)

        real = (i > 0).astype(F32)
        h = jnp.where(i == 0, h0_ref[...], x_ref[...])

        start = pl.multiple_of(i * TR, TR)
        halo = pin_ref[pl.ds(pl.multiple_of(jnp.maximum(i * TR - HALO, 0), HALO), HALO), :] * real
        ext = jnp.concatenate([halo, pin_ref[pl.ds(start, TR), :]], axis=0)
        pooled = []
        for g, w in enumerate(POOL_WINDOWS):
            e = ext[:, g * POOL_GROUP:(g + 1) * POOL_GROUP]
            acc = e
            shift = 1
            while shift < w:
                acc = acc + pltpu.roll(acc, shift, 0)
                shift *= 2
            pooled.append((acc[HALO:] * _inv_count(i, w) - e[HALO:]).astype(BF16))
        pw = [pw_ref[g].astype(BF16) for g in range(len(POOL_WINDOWS))]
        mixed = jnp.concatenate([_nn(pooled[g], pw[g]) for g in range(len(POOL_WINDOWS))], axis=1)
        ps = ps_ref[...]
        mixed_s = mixed * ps
        pg = pg_ref[...]
        sig_p = _sigmoid(pg)
        silu_p = pg * sig_p
        pool_out = (silu_p * mixed_s).astype(BF16)
        ag = ag_ref[...]
        sig_a = _sigmoid(ag)
        silu_a = ag * sig_a
        at = at_ref[...]
        attn_out = (silu_a * at).astype(BF16)
        mix = _nn(pool_out, wout_ref[0:D_POOL, :]) + _nn(attn_out, wout_ref[D_POOL:D, :])
        h2 = h + mix

        r2 = lax.rsqrt(jnp.mean(h2 * h2, axis=-1, keepdims=True) + EPS)
        n2 = h2 * r2
        gfv = gf_ref[...]
        err = (n2 * gfv - t_ref[...]) * real
        loss_ref[...] += jnp.sum(jnp.sum(err * err, axis=-1, keepdims=True), axis=0, keepdims=True) * (0.5 / D)
        dy = err * (1.0 / D)
        dgf_ref[...] += jnp.sum(dy * n2, axis=0, keepdims=True)
        dn = dy * gfv
        dh2 = r2 * (dn - n2 * jnp.mean(dn * n2, axis=-1, keepdims=True))
        dh2_ref[...] = dh2
        dh2b = dh2.astype(BF16)

        dwout_ref[0:D_POOL, :] += _tn(pool_out, dh2b)
        dwout_ref[D_POOL:D, :] += _tn(attn_out, dh2b)
        dcat = _nt(dh2b, wout_ref[...])
        dpo = dcat[:, 0:D_POOL]
        dao = dcat[:, D_POOL:D]
        do_ref[...] = dao * silu_a
        dag_ref[...] = dao * at * (sig_a * (1.0 + ag * (1.0 - sig_a)))
        dmixed_s = dpo * silu_p
        dpg_ref[...] = dpo * mixed_s * (sig_p * (1.0 + pg * (1.0 - sig_p)))
        dps_ref[...] += jnp.sum(dmixed_s * mixed, axis=0, keepdims=True)
        dmixed = (dmixed_s * ps).astype(BF16)
        dpl = []
        for g in range(len(POOL_WINDOWS)):
            dm = dmixed[:, g * POOL_GROUP:(g + 1) * POOL_GROUP]
            dpl.append(_nt(dm, pw[g]))
            dpw_ref[g] += _tn(pooled[g], dm)
        dpl_ref[...] = jnp.concatenate(dpl, axis=1)

    row = lambda w: pl.BlockSpec((TR, w), lambda i: (i, 0))
    frame = pl.BlockSpec((TR, D), lambda i: (_row_block(i), 0))
    full = lambda *s: pl.BlockSpec(s, lambda i: (0,) * len(s))
    return pl.pallas_call(
        body,
        name="mid",
        grid=(NT,),
        in_specs=[
            frame, full(TR, D), frame, full(N, D_POOL), row(D_POOL), row(D_POOL), row(D_POOL),
            full(len(POOL_WINDOWS), POOL_GROUP, POOL_GROUP), full(1, D_POOL), full(D, D), full(1, D),
        ],
        out_specs=[
            row(D), row(D_POOL), row(D_POOL), row(D_POOL), row(D_POOL),
            full(D, D), full(len(POOL_WINDOWS), POOL_GROUP, POOL_GROUP), full(1, D_POOL), full(1, D), full(1, 1),
        ],
        out_shape=[
            jax.ShapeDtypeStruct((N, D), F32), jax.ShapeDtypeStruct((N, D_POOL), F32),
            jax.ShapeDtypeStruct((N, D_POOL), F32), jax.ShapeDtypeStruct((N, D_POOL), F32),
            jax.ShapeDtypeStruct((N, D_POOL), F32), jax.ShapeDtypeStruct((D, D), F32),
            jax.ShapeDtypeStruct((len(POOL_WINDOWS), POOL_GROUP, POOL_GROUP), F32),
            jax.ShapeDtypeStruct((1, D_POOL), F32), jax.ShapeDtypeStruct((1, D), F32), jax.ShapeDtypeStruct((1, 1), F32),
        ],
        compiler_params=_cparams(dimension_semantics=("arbitrary",)),
    )(x2, h0, tgt2, pool_in, pool_gate, attn_gate, attn, pool_w, pool_scale, wout, gf)


def _attn_bwd(q, k, v, do, o, lse):
    tiles = _attn_tiles()

    def body(q_ref, k_ref, v_ref, do_ref, o_ref, lse_ref, dq_ref, dk_ref, dv_ref):
        step = pl.program_id(1)

        @pl.when(step == 0)
        def _():
            dk_ref[...] = jnp.zeros_like(dk_ref)
            dv_ref[...] = jnp.zeros_like(dv_ref)

        for idx, (q0, rows, klen) in enumerate(tiles):
            @pl.when(step == idx)
            def _(q0=q0, rows=rows, klen=klen):
                qs = pl.ds(q0, rows)
                qv = q_ref[0, qs, :]
                kv = k_ref[0, 0:klen, :]
                p = jnp.exp(_masked_scores(qv, kv, rows, klen) - lse_ref[0, qs, :])
                dov = do_ref[qs, :]
                dob = dov.astype(BF16)
                delta = jnp.sum(dov * o_ref[qs, :], axis=-1, keepdims=True)
                ds = (p * (_nt(dob, v_ref[0, 0:klen, :]) - delta)).astype(BF16)
                dq_ref[0, qs, :] = _nn(ds, kv) * SCALE
                dk_ref[0, 0:klen, :] += _tn(ds, qv)
                dv_ref[0, 0:klen, :] += _tn(p.astype(BF16), dob)

    per_head = lambda w: pl.BlockSpec((1, N, w), lambda h, t: (h, 0, 0))
    cols = pl.BlockSpec((N, V_HEAD), lambda h, t: (0, h))
    return pl.pallas_call(
        body,
        name="attn_bwd",
        grid=(HEADS, len(tiles)),
        in_specs=[per_head(256), per_head(256), per_head(V_HEAD), cols, cols, per_head(1)],
        out_specs=[per_head(256), per_head(256), per_head(V_HEAD)],
        out_shape=[
            jax.ShapeDtypeStruct((HEADS, N, 256), F32), jax.ShapeDtypeStruct((HEADS, N, 256), F32),
            jax.ShapeDtypeStruct((HEADS, N, V_HEAD), F32),
        ],
        compiler_params=_cparams(dimension_semantics=("arbitrary", "arbitrary")),
    )(q, k, v, do, o, lse)


def _bwd_in(x2, h0, dh2, dq, dk, dv, cq, ckv, dpl, dpg, dag, norm_g, win, gq, wq, gkv, wkv, cosf, sinf):
    def body(x_ref, h0_ref, dh2_ref, dq_ref, dk_ref, dv_ref, cq_ref, ckv_ref, dpl_ref, dpg_ref, dag_ref,
             g_ref, win_ref, gq_ref, wq_ref, gkv_ref, wkv_ref, cos_ref, sin_ref,
             gx_ref, dmeta_ref, dwin_ref, dwq_ref, dwkv_ref, dg_ref, dgq_ref, dgkv_ref):
        i = pl.program_id(0)

        @pl.when(i == 0)
        def _():
            dwin_ref[...] = jnp.zeros_like(dwin_ref)
            dwq_ref[...] = jnp.zeros_like(dwq_ref)
            dwkv_ref[...] = jnp.zeros_like(dwkv_ref)
            dg_ref[...] = jnp.zeros_like(dg_ref)
            dgq_ref[...] = jnp.zeros_like(dgq_ref)
            dgkv_ref[...] = jnp.zeros_like(dgkv_ref)

        h = jnp.where(i == 0, h0_ref[...], x_ref[...])
        r = lax.rsqrt(jnp.mean(h * h, axis=-1, keepdims=True) + EPS)
        n = h * r
        gv = g_ref[...]
        hn = (n * gv).astype(BF16)
        cosv = cos_ref[...]
        sinv = sin_ref[...]
        low = _low_lanes()

        def unrope(dy):
            return dy * cosv + _swap64(dy * sinv) * low

        cq = cq_ref[...]
        rq = lax.rsqrt(jnp.mean(cq * cq, axis=-1, keepdims=True) + EPS)
        nq = cq * rq
        gqv = gq_ref[...]
        cqn = (nq * gqv).astype(BF16)
        dcqn = jnp.zeros((TR, Q_LORA), F32)
        for hd in range(HEADS):
            dqh = dq_ref[hd]
            dqf = jnp.concatenate([dqh[:, 0:QK_NOPE], unrope(dqh[:, QK_NOPE:])], axis=1).astype(BF16)
            dcqn = dcqn + _nt(dqf, wq_ref[hd])
            dwq_ref[hd] += _tn(cqn, dqf)
        dgq_ref[...] += jnp.sum(dcqn * nq, axis=0, keepdims=True)
        dnq = dcqn * gqv
        dcq = rq * (dnq - nq * jnp.mean(dnq * nq, axis=-1, keepdims=True))

        ckv = ckv_ref[...]
        rkv = lax.rsqrt(jnp.mean(ckv * ckv, axis=-1, keepdims=True) + EPS)
        nkv = ckv * rkv
        gkvv = gkv_ref[...]
        ckvn = (nkv * gkvv).astype(BF16)
        dckvn = jnp.zeros((TR, KV_LORA), F32)
        dkr = jnp.zeros((TR, 128), F32)
        for hd in range(HEADS):
            dkh = dk_ref[hd]
            dkr = dkr + dkh[:, QK_NOPE:]
            dkv = jnp.concatenate([dkh[:, 0:QK_NOPE], dv_ref[hd]], axis=1).astype(BF16)
            dckvn = dckvn + _nt(dkv, wkv_ref[hd])
            dwkv_ref[hd] += _tn(ckvn, dkv)
        dgkv_ref[...] += jnp.sum(dckvn * nkv, axis=0, keepdims=True)
        dnkv = dckvn * gkvv
        dckv = rkv * (dnkv - nkv * jnp.mean(dnkv * nkv, axis=-1, keepdims=True))
        dkr = unrope(dkr)

        start = pl.multiple_of(i * TR, TR)
        nxt = pl.multiple_of(jnp.minimum(i * TR + TR, N - HALO), HALO)
        last = (i < NT - 1).astype(F32)
        cur = dpl_ref[pl.ds(start, TR), :]
        halo = dpl_ref[pl.ds(nxt, HALO), :] * last
        dpi = []
        for g, w in enumerate(POOL_WINDOWS):
            sl = slice(g * POOL_GROUP, (g + 1) * POOL_GROUP)
            a = jnp.concatenate([cur[:, sl] * _inv_count(i, w), halo[:, sl] * _inv_count(i + 1, w)[0:HALO]], axis=0)
            acc = a
            shift = 1
            while shift < w:
                acc = acc + pltpu.roll(acc, TR + HALO - shift, 0)
                shift *= 2
            dpi.append(acc[0:TR] - cur[:, sl])

        du = jnp.concatenate(dpi + [dpg_ref[...], dcq, dckv, dkr], axis=1).astype(BF16)
        dagb = dag_ref[...].astype(BF16)
        dwin_ref[0:O_KR_END, :] += _tn(du, hn)
        dwin_ref[O_AG:D_IN, :] += _tn(dagb, hn)
        dhn = _nn(du, win_ref[0:O_KR_END, :]) + _nn(dagb, win_ref[O_AG:D_IN, :])
        dg_ref[...] += jnp.sum(dhn * n, axis=0, keepdims=True)
        dn = dhn * gv
        dh = dh2_ref[...] + r * (dn - n * jnp.mean(dn * n, axis=-1, keepdims=True))
        gx_ref[...] = dh

        @pl.when(i == 0)
        def _():
            dmeta_ref[...] = dh

    row = lambda w: pl.BlockSpec((TR, w), lambda i: (i, 0))
    frame = pl.BlockSpec((TR, D), lambda i: (_row_block(i), 0))
    full = lambda *s: pl.BlockSpec(s, lambda i: (0,) * len(s))
    head = lambda w: pl.BlockSpec((HEADS, TR, w), lambda i: (0, i, 0))
    return pl.pallas_call(
        body,
        name="bwd_in",
        grid=(NT,),
        in_specs=[
            frame, full(TR, D), row(D), head(256), head(256), head(V_HEAD), row(Q_LORA), row(KV_LORA),
            full(N, D_POOL), row(D_POOL), row(D_POOL),
            full(1, D), full(D_IN, D), full(1, Q_LORA), full(HEADS, Q_LORA, 256),
            full(1, KV_LORA), full(HEADS, KV_LORA, 256), row(128), row(128),
        ],
        out_specs=[
            frame, full(TR, D), full(D_IN, D), full(HEADS, Q_LORA, 256), full(HEADS, KV_LORA, 256),
            full(1, D), full(1, Q_LORA), full(1, KV_LORA),
        ],
        out_shape=[
            jax.ShapeDtypeStruct((S, D), F32), jax.ShapeDtypeStruct((TR, D), F32),
            jax.ShapeDtypeStruct((D_IN, D), F32), jax.ShapeDtypeStruct((HEADS, Q_LORA, 256), F32),
            jax.ShapeDtypeStruct((HEADS, KV_LORA, 256), F32),
            jax.ShapeDtypeStruct((1, D), F32), jax.ShapeDtypeStruct((1, Q_LORA), F32), jax.ShapeDtypeStruct((1, KV_LORA), F32),
        ],
        compiler_params=_cparams(dimension_semantics=("arbitrary",)),
    )(x2, h0, dh2, dq, dk, dv, cq, ckv, dpl, dpg, dag, norm_g, win, gq, wq, gkv, wkv, cosf, sinf)


def _local_step(x2, tgt2, h0, norm_g, win, gq, wq, gkv, wkv, pool_w, pool_scale, wout, gf, cosf, sinf):
    pool_in, pool_gate, cq, ckv, attn_gate, q, k, v = _fwd_in(x2, h0, norm_g, win, gq, wq, gkv, wkv, cosf, sinf)
    attn, lse = _attn_fwd(q, k, v)
    dh2, do, dag, dpg, dpl, dwout, dpw, dps, dgf, loss = _mid(
        x2, h0, tgt2, pool_in, pool_gate, attn_gate, attn, pool_w, pool_scale, wout, gf)
    dq, dk, dv = _attn_bwd(q, k, v, do, attn, lse)
    gx, dmeta, dwin, dwq, dwkv, dg, dgq, dgkv = _bwd_in(
        x2, h0, dh2, dq, dk, dv, cq, ckv, dpl, dpg, dag, norm_g, win, gq, wq, gkv, wkv, cosf, sinf)
    return dict(gx=gx, dmeta=dmeta, dwin=dwin, dwq=dwq, dwkv=dwkv, dwout=dwout, dg=dg, dgq=dgq, dgkv=dgkv,
                dpw=dpw, dps=dps, dgf=dgf, loss=loss)


_CHIP_RELS = ((0, 0), (1, 0), (0, 1), (1, 1))

_ARR_ROWS = (SHARD_IN, SHARD_OUT, Q_LORA, KV_LORA, N_META)
_ARR_COLS = (D, D, 256, 256, 256)
_PIECES = (
    (0, 0, 256, 0), (0, 256, SHARD_IN - 256, 1),
    (1, 0, 128, 0), (1, 128, 128, 1),
    (2, 0, 128, 0), (2, 128, 128, 1),
    (3, 0, 64, 0), (3, 64, 64, 1),
    (4, 0, N_META, 0),
)
_NP = len(_PIECES)
_PIECE_MAX = (256, 128, 128, 64, N_META)


def _shard_rows(arr, chip, r0, n):
    return pl.ds(pl.multiple_of(_ARR_ROWS[arr] * chip + r0, 16), n)


def _gathered_at(refs, arr, chip, r0, n):
    if arr in (0, 1):
        return refs[arr].at[_shard_rows(arr, chip, r0, n), :]
    return refs[arr].at[chip, pl.ds(r0, n), :]


def _remote(src, dst, send_sem, recv_sem, to):
    return pltpu.make_async_remote_copy(src_ref=src, dst_ref=dst, send_sem=send_sem, recv_sem=recv_sem,
                                        device_id=to, device_id_type=MESH)


def _gather_weights(winT_s, wout_s, wq_s, wkv_s, meta_s):
    def body(win_ref, wout_ref, wq_ref, wkv_ref, meta_ref, win_o, wout_o, wq_o, wkv_o, meta_o,
             s_win, s_wout, s_wq, s_wkv, ici_send, ici_recv, fwd_send, fwd_recv):
        x, y, c = lax.axis_index("x"), lax.axis_index("y"), lax.axis_index("c")
        me = 2 * x + y
        ins = (win_ref, wout_ref, wq_ref, wkv_ref)
        stage = (s_win, s_wout, s_wq, s_wkv, meta_ref)
        outs = (win_o, wout_o, wq_o, wkv_o, meta_o)
        for arr in range(4):
            stage[arr][...] = ins[arr][...].astype(BF16)

        def chip_of(rel):
            fx, fy = _CHIP_RELS[rel]
            return 2 * (x ^ fx) + (y ^ fy)

        def same_core_of(rel):
            fx, fy = _CHIP_RELS[rel]
            return (x ^ fx, y ^ fy, c)

        def ici_copy(rel, i, src_chip, to):
            arr, r0, n, _ = _PIECES[i]
            k = (rel - 1) * _NP + i
            return _remote(stage[arr].at[pl.ds(r0, n), :], _gathered_at(outs, arr, src_chip, r0, n),
                           ici_send.at[k], ici_recv.at[k], to)

        def fwd_copy(rel, i, to):
            arr, r0, n, _ = _PIECES[i]
            k = (rel - 1) * _NP + i
            place = _gathered_at(outs, arr, chip_of(rel), r0, n)
            return _remote(place, place, fwd_send.at[k], fwd_recv.at[k], to)

        for core in (0, 1):
            @pl.when(c == core)
            def _(core=core):
                mine = [i for i in range(_NP) if _PIECES[i][3] == core]
                theirs = [i for i in range(_NP) if _PIECES[i][3] != core]
                sends = [ici_copy(rel, i, me, same_core_of(rel)) for rel in (1, 2, 3) for i in mine]
                for cp in sends:
                    cp.start()
                for arr in range(5):
                    _gathered_at(outs, arr, me, 0, _ARR_ROWS[arr])[...] = stage[arr][...]
                for rel in (1, 2, 3):
                    for i in mine:
                        ici_copy(rel, i, chip_of(rel), (x, y, c)).wait_recv()
                        fwd = fwd_copy(rel, i, (x, y, 1 - c))
                        fwd.start()
                        sends.append(fwd)
                for rel in (1, 2, 3):
                    for i in theirs:
                        fwd_copy(rel, i, (x, y, c)).wait_recv()
                for cp in sends:
                    cp.wait_send()

    vm = pl.BlockSpec(memory_space=pltpu.VMEM)
    return pl.pallas_call(
        body,
        name="gather_weights",
        in_specs=[vm] * 5,
        out_specs=[vm] * 5,
        out_shape=[
            jax.ShapeDtypeStruct((D_IN, D), BF16), jax.ShapeDtypeStruct((D, D), BF16),
            jax.ShapeDtypeStruct((CHIPS, Q_LORA, 256), BF16), jax.ShapeDtypeStruct((CHIPS, KV_LORA, 256), BF16),
            jax.ShapeDtypeStruct((CHIPS, N_META, 256), F32),
        ],
        scratch_shapes=[pltpu.VMEM((_ARR_ROWS[a], _ARR_COLS[a]), BF16) for a in range(4)]
        + [pltpu.SemaphoreType.DMA((3 * _NP,))] * 4,
        compiler_params=_cparams(),
    )(winT_s, wout_s, wq_s, wkv_s, meta_s)


def _reduce_grads(dwin, dwout, dwq, dwkv, dmeta4, small):
    half = SM_ROWS_CORE0

    def body(dwin_ref, dwout_ref, dwq_ref, dwkv_ref, dmeta_ref, sm_ref,
             gwin_o, gwout_o, gwq_o, gwkv_o, gmeta_o, gsm_o,
             sb0, sb1, sb2, sb3, sb4, st0, st1, st2, st3, st4, rc0, rc1, rc2, rc3, rc4, sb_sm, cs_sm, rc_sm,
             d2d_send, d2d_recv, ici_send, ici_recv, fin_send, fin_recv, sm_send, sm_recv):
        x, y, c = lax.axis_index("x"), lax.axis_index("y"), lax.axis_index("c")
        me = 2 * x + y
        grads = (dwin_ref, dwout_ref, dwq_ref, dwkv_ref, dmeta_ref)
        outs = (gwin_o, gwout_o, gwq_o, gwkv_o, gmeta_o)
        sib_buf = (sb0, sb1, sb2, sb3, sb4)
        stage = (st0, st1, st2, st3, st4)
        recv = (rc0, rc1, rc2, rc3, rc4)
        sibling = (x, y, 1 - c)

        def chip_of(rel):
            fx, fy = _CHIP_RELS[rel]
            return 2 * (x ^ fx) + (y ^ fy)

        def same_core_of(rel):
            fx, fy = _CHIP_RELS[rel]
            return (x ^ fx, y ^ fy, c)

        def slot(bufs, i, idx):
            arr, _, n, _ = _PIECES[i]
            return bufs[arr].at[idx, pl.ds(0, n), :]

        def d2d_copy(rel, i):
            arr, r0, n, _ = _PIECES[i]
            k = rel * _NP + i
            return _remote(_gathered_at(grads, arr, chip_of(rel), r0, n), slot(sib_buf, i, rel),
                           d2d_send.at[k], d2d_recv.at[k], sibling)

        def ici_copy(rel, i):
            k = (rel - 1) * _NP + i
            return _remote(slot(stage, i, rel - 1), slot(recv, i, rel - 1), ici_send.at[k], ici_recv.at[k],
                           same_core_of(rel))

        def fin_copy(i):
            arr, r0, n, _ = _PIECES[i]
            place = outs[arr].at[pl.ds(r0, n), :]
            return _remote(place, place, fin_send.at[i], fin_recv.at[i], sibling)

        sm_swap = _remote(sm_ref, sb_sm, sm_send.at[0], sm_recv.at[0], sibling)
        sm_swap.start()

        for core in (0, 1):
            @pl.when(c == core)
            def _(core=core):
                mine = [i for i in range(_NP) if _PIECES[i][3] == core]
                theirs = [i for i in range(_NP) if _PIECES[i][3] != core]
                sm_rows = pl.ds(0, half) if core == 0 else pl.ds(half, SM_ROWS - half)
                n_sm = half if core == 0 else SM_ROWS - half
                sends = [sm_swap]

                for rel in (1, 2, 3, 0):
                    for i in theirs:
                        cp = d2d_copy(rel, i)
                        cp.start()
                        sends.append(cp)

                sm_swap.wait_recv()
                cs_sm[...] = sm_ref[...] + sb_sm[...]
                for rel in (1, 2, 3):
                    cp = _remote(cs_sm.at[sm_rows, :], rc_sm.at[rel - 1, sm_rows, :], sm_send.at[rel], sm_recv.at[rel],
                                 same_core_of(rel))
                    cp.start()
                    sends.append(cp)

                for rel in (1, 2, 3):
                    for i in mine:
                        arr, r0, n, _ = _PIECES[i]
                        d2d_copy(rel, i).wait_recv()
                        total = _gathered_at(grads, arr, chip_of(rel), r0, n)[...] + slot(sib_buf, i, rel)[...]
                        slot(stage, i, rel - 1)[...] = total.astype(stage[arr].dtype)
                        cp = ici_copy(rel, i)
                        cp.start()
                        sends.append(cp)

                for i in mine:
                    arr, r0, n, _ = _PIECES[i]
                    d2d_copy(0, i).wait_recv()
                    total = _gathered_at(grads, arr, me, r0, n)[...] + slot(sib_buf, i, 0)[...]
                    for rel in (1, 2, 3):
                        ici_copy(rel, i).wait_recv()
                        total = total + slot(recv, i, rel - 1)[...].astype(F32)
                    outs[arr][pl.ds(r0, n), :] = total
                    cp = fin_copy(i)
                    cp.start()
                    sends.append(cp)

                for rel in (1, 2, 3):
                    _remote(cs_sm.at[sm_rows, :], rc_sm.at[rel - 1, sm_rows, :], sm_send.at[rel], sm_recv.at[rel],
                            same_core_of(rel)).wait_recv()
                total = jnp.zeros((n_sm, D), F32)
                for chip in range(CHIPS):
                    flips = chip ^ me
                    rel = jnp.where(flips == 2, 1, jnp.where(flips == 1, 2, flips))
                    theirs_rows = rc_sm[jnp.maximum(rel - 1, 0), sm_rows, :]
                    total = total + jnp.where(rel == 0, cs_sm[sm_rows, :], theirs_rows)
                gsm_o[sm_rows, :] = total
                other_rows = pl.ds(half, SM_ROWS - half) if core == 0 else pl.ds(0, half)
                cp = _remote(gsm_o.at[sm_rows, :], gsm_o.at[sm_rows, :], sm_send.at[4], sm_recv.at[4], sibling)
                cp.start()
                sends.append(cp)

                for i in theirs:
                    fin_copy(i).wait_recv()
                _remote(gsm_o.at[other_rows, :], gsm_o.at[other_rows, :], sm_send.at[4], sm_recv.at[4], sibling).wait_recv()
                for cp in sends:
                    cp.wait_send()

    vm = pl.BlockSpec(memory_space=pltpu.VMEM)
    piece_buf = lambda lead, dtype: [pltpu.VMEM((lead, _PIECE_MAX[a], _ARR_COLS[a]), F32 if a == 4 else dtype)
                                     for a in range(5)]
    return pl.pallas_call(
        body,
        name="reduce_grads",
        in_specs=[vm] * 6,
        out_specs=[vm] * 6,
        out_shape=[jax.ShapeDtypeStruct((_ARR_ROWS[a], _ARR_COLS[a]), F32) for a in range(5)]
        + [jax.ShapeDtypeStruct((SM_ROWS, D), F32)],
        scratch_shapes=piece_buf(CHIPS, F32) + piece_buf(3, BF16) + piece_buf(3, BF16)
        + [pltpu.VMEM((SM_ROWS, D), F32), pltpu.VMEM((SM_ROWS, D), F32), pltpu.VMEM((3, SM_ROWS, D), F32)]
        + [pltpu.SemaphoreType.DMA((CHIPS * _NP,))] * 2 + [pltpu.SemaphoreType.DMA((3 * _NP,))] * 2
        + [pltpu.SemaphoreType.DMA((_NP,))] * 2 + [pltpu.SemaphoreType.DMA((5,))] * 2,
        compiler_params=_cparams(),
    )(dwin, dwout, dwq, dwkv, dmeta4, small)


def _adamw_math(w, g, m, v):
    m = B1 * m + (1.0 - B1) * g
    v = B2 * v + (1.0 - B2) * (g * g)
    m_hat = m / C1
    v_hat = v / C2
    delta = -LR * (m_hat / (jnp.sqrt(v_hat) + ADAM_EPS) + WD * w)
    return delta, m, v


def _adamw_rows(name, w, g, m, v, block_rows):
    rows, cols = w.shape

    def body(w_ref, g_ref, m_ref, v_ref, d_ref, nm_ref, nv_ref):
        d_ref[...], nm_ref[...], nv_ref[...] = _adamw_math(w_ref[...], g_ref[...], m_ref[...], v_ref[...])

    spec = pl.BlockSpec((block_rows, cols), lambda i: (i, 0))
    return pl.pallas_call(
        body,
        name=name,
        grid=(rows // block_rows,),
        in_specs=[spec] * 4,
        out_specs=[spec] * 3,
        out_shape=[jax.ShapeDtypeStruct(w.shape, F32)] * 3,
        compiler_params=_cparams(dimension_semantics=("arbitrary",)),
    )(w, g, m, v)


def _adamw_small(groups):
    n = len(groups)

    def body(*refs):
        ins, outs = refs[:4 * n], refs[4 * n:]
        for t in range(n):
            w_ref, g_ref, m_ref, v_ref = ins[4 * t:4 * t + 4]
            outs[3 * t][...], outs[3 * t + 1][...], outs[3 * t + 2][...] = _adamw_math(
                w_ref[...], g_ref[...], m_ref[...], v_ref[...])

    vm = pl.BlockSpec(memory_space=pltpu.VMEM)
    flat = [a for grp in groups for a in grp]
    outs = pl.pallas_call(
        body,
        name="adamw_small",
        in_specs=[vm] * (4 * n),
        out_specs=[vm] * (3 * n),
        out_shape=[jax.ShapeDtypeStruct(grp[0].shape, F32) for grp in groups for _ in range(3)],
        compiler_params=_cparams(),
    )(*flat)
    return [tuple(outs[3 * t:3 * t + 3]) for t in range(n)]


def _rope_tables():
    half = QK_ROPE // 2
    inv_freq = 1.0 / (ROPE_THETA ** (jnp.arange(half, dtype=F32) / half))
    pos = jnp.arange(N, dtype=F32) - PAD
    ang = pos[:, None] * inv_freq[None, :]
    cos, sin = jnp.cos(ang), jnp.sin(ang)
    zero = jnp.zeros((N, 128 - QK_ROPE), F32)
    return jnp.concatenate([cos, cos, zero], axis=1), jnp.concatenate([-sin, sin, zero], axis=1)


def kernel(x, meta_tokens, norm_g, w_in, q_norm_g, w_q_b, kv_norm_g, w_kv_b, pool_w, pool_scale, w_out, final_norm_g, loss_target, m_meta_tokens, m_norm_g, m_w_in, m_q_norm_g, m_w_q_b, m_kv_norm_g, m_w_kv_b, m_pool_w, m_pool_scale, m_w_out, m_final_norm_g, v_meta_tokens, v_norm_g, v_w_in, v_q_norm_g, v_w_q_b, v_kv_norm_g, v_w_kv_b, v_pool_w, v_pool_scale, v_w_out, v_final_norm_g):
    winT_s = w_in[0].T
    wq_s = jnp.pad(w_q_b[0], ((0, 0), (0, 256 - QK_NOPE - QK_ROPE)))
    win, wout, wq, wkv, meta4 = _gather_weights(winT_s, w_out[0], wq_s, w_kv_b[0], meta_tokens)
    meta_full = jnp.transpose(meta4, (1, 0, 2)).reshape(N_META, D)
    h0 = jnp.concatenate([jnp.zeros((PAD, D), F32), meta_full], axis=0)
    cosf, sinf = _rope_tables()
    gf = final_norm_g.reshape(1, D)

    part = _local_step(x[0], loss_target[0], h0, norm_g, win, q_norm_g, wq, kv_norm_g, wkv, pool_w[0], pool_scale,
                       wout, gf, cosf, sinf)

    misc = jnp.concatenate([part["dgq"], part["dgkv"], part["dps"], part["loss"],
                            jnp.zeros((1, D - MISC_LOSS - 1), F32)], axis=1)
    small = jnp.concatenate([part["dpw"].reshape(64, D), part["dg"], part["dgf"], misc,
                             jnp.zeros((SM_ROWS - SM_MISC - 1, D), F32)], axis=0)
    dmeta4 = jnp.transpose(part["dmeta"][PAD:TR].reshape(N_META, CHIPS, 256), (1, 0, 2))
    gwinT, gwout, gwq, gwkv, gmeta, gsm = _reduce_grads(part["dwin"], part["dwout"], part["dwq"], part["dwkv"], dmeta4, small)

    loss = gsm[SM_MISC, MISC_LOSS]
    g_w_in = gwinT.T
    g_w_q_b = gwq[:, 0:QK_NOPE + QK_ROPE]
    g_pool_w = gsm[SM_POOLW:SM_POOLW + 64].reshape(4 * POOL_GROUP, POOL_GROUP)
    g_norm = gsm[SM_NORM:SM_NORM + 1]
    g_final = gsm[SM_FINAL:SM_FINAL + 1]
    g_gq = gsm[SM_MISC:SM_MISC + 1, MISC_GQ:MISC_GQ + Q_LORA]
    g_gkv = gsm[SM_MISC:SM_MISC + 1, MISC_GKV:MISC_GKV + KV_LORA]
    g_ps = gsm[SM_MISC:SM_MISC + 1, MISC_PS:MISC_PS + D_POOL]

    d_in, nm_in, nv_in = _adamw_rows("adamw_w_in", w_in[0], g_w_in, m_w_in[0], v_w_in[0], 256)
    d_out, nm_out, nv_out = _adamw_rows("adamw_w_out", w_out[0], gwout, m_w_out[0], v_w_out[0], 128)
    pw2 = lambda a: a.reshape(4 * POOL_GROUP, POOL_GROUP)
    fn2 = lambda a: a.reshape(1, D)
    res = _adamw_small([
        (meta_tokens, gmeta, m_meta_tokens, v_meta_tokens),
        (norm_g, g_norm, m_norm_g, v_norm_g),
        (q_norm_g, g_gq, m_q_norm_g, v_q_norm_g),
        (w_q_b[0], g_w_q_b, m_w_q_b[0], v_w_q_b[0]),
        (kv_norm_g, g_gkv, m_kv_norm_g, v_kv_norm_g),
        (w_kv_b[0], gwkv, m_w_kv_b[0], v_w_kv_b[0]),
        (pw2(pool_w), g_pool_w, pw2(m_pool_w), pw2(v_pool_w)),
        (pool_scale, g_ps, m_pool_scale, v_pool_scale),
        (fn2(final_norm_g), g_final, fn2(m_final_norm_g), fn2(v_final_norm_g)),
    ])
    (r_meta, r_norm, r_gq, r_wq, r_gkv, r_wkv, r_pw, r_ps, r_fn) = res
    pw4 = lambda a: a.reshape(1, 4, POOL_GROUP, POOL_GROUP)

    grads = [gmeta, g_norm, g_w_in[None], g_gq, g_w_q_b[None], g_gkv, gwkv[None], pw4(g_pool_w), g_ps, gwout[None],
             g_final.reshape(D)]
    per_kind = []
    for kind in range(3):
        per_kind.append([
            r_meta[kind], r_norm[kind], (d_in, nm_in, nv_in)[kind][None], r_gq[kind], r_wq[kind][None], r_gkv[kind],
            r_wkv[kind][None], pw4(r_pw[kind]), r_ps[kind], (d_out, nm_out, nv_out)[kind][None], r_fn[kind].reshape(D),
        ])
    return (loss, part["gx"][None], *grads, *per_kind[0], *per_kind[1], *per_kind[2])
```

```python
import jax
import jax.numpy as jnp
from jax import lax
from jax.experimental import pallas as pl
from jax.experimental.pallas import tpu as pltpu

F32 = jnp.float32
BF16 = jnp.bfloat16

D = 1024
S = 2048
N_META = 16
PAD = 112
HEAD_ROWS = PAD + N_META
N = HEAD_ROWS + S
D_POOL = 512
POOL_WINDOWS = (2, 4, 8, 16)
POOL_GROUP = 128
HALO = 16
HEADS = 4
QK_NOPE = 128
QK_ROPE = 64
QK = QK_NOPE + QK_ROPE
V_HEAD = 128
Q_LORA = 256
KV_LORA = 128
D_IN = 1984
EPS = 1e-6
ROPE_THETA = 10000.0
SCALE = QK ** -0.5
CHIPS = 4

ROWS_FWD = 544
ROWS_MID = 544
ROWS_BWD = 272
TK = 128
TQ = 256
NQ = S // TQ

O_PI, O_PG, O_CQ, O_CKV, O_KR, O_AG = 0, 512, 1024, 1280, 1408, 1472
O_KR_END = O_KR + 128
SHARD_IN = D_IN // CHIPS
SHARD_OUT = D // CHIPS

LR, B1, B2, ADAM_EPS, WD, STEP = 0.001, 0.9, 0.999, 1e-08, 0.01, 10
C1 = 1.0 - B1**STEP
C2 = 1.0 - B2**STEP

VMEM_LIMIT = 60 * 1024 * 1024
MESH = pl.DeviceIdType.MESH
NEG = -1e30

VEC_ROWS = 8
V_GQ, V_GKV, V_PS, V_LOSS = 0, 256, 384, 896


def _cparams(**kw):
    return pltpu.CompilerParams(vmem_limit_bytes=VMEM_LIMIT, **kw)


def _nt(a, b):
    return lax.dot_general(a, b, (((1,), (1,)), ((), ())), preferred_element_type=F32)


def _tn(a, b):
    return lax.dot_general(a, b, (((0,), (0,)), ((), ())), preferred_element_type=F32)


def _nn(a, b):
    return jnp.dot(a, b, preferred_element_type=F32)


def _swap64(t):
    return pltpu.roll(t, 32, 1) + pltpu.roll(t, 96, 1)


def _sigmoid(x):
    return 1.0 / (1.0 + jnp.exp(-x))


def _low_lanes():
    return (lax.broadcasted_iota(jnp.int32, (1, 128), 1) < QK_ROPE).astype(F32)


def _rows(w, rows):
    return pl.BlockSpec((rows, w), lambda i: (i, 0))


def _const(*shape):
    return pl.BlockSpec(shape, lambda *_: (0,) * len(shape), pipeline_mode=pl.Buffered(1))


def _attn_tiles():
    return [(0, TK, TK)] + [(TK + TQ * t, TQ, TK + TQ * (t + 1)) for t in range(NQ)]


def _masked_scores(q, k, rows, klen):
    s = _nt(q, k)
    col = lax.broadcasted_iota(jnp.int32, (1, TK), 1)
    head_bias = jnp.where(col >= PAD, 0.0, NEG)
    if klen == TK:
        return s + head_bias
    r = lax.broadcasted_iota(jnp.int32, (rows, 1), 0) >> 6
    c = lax.broadcasted_iota(jnp.int32, (1, rows), 1) >> 6
    diag_bias = jnp.where(c <= r, 0.0, NEG)
    parts = [s[:, 0:TK] + head_bias]
    if klen - rows > TK:
        parts.append(s[:, TK:klen - rows])
    parts.append(s[:, klen - rows:klen] + diag_bias)
    return jnp.concatenate(parts, axis=1)


def _fwd_in(h, norm_g, win, gq, wq, gkv, wkv, cosf, sinf):
    tr = ROWS_FWD

    def body(h_ref, g_ref, win_ref, gq_ref, wq_ref, gkv_ref, wkv_ref, cos_ref, sin_ref,
             pi_ref, pg_ref, cq_ref, ckv_ref, ag_ref, q_ref, k_ref, v_ref):
        h = h_ref[...]
        r = lax.rsqrt(jnp.mean(h * h, axis=-1, keepdims=True) + EPS)
        hn = ((h * r) * g_ref[...]).astype(BF16)
        u = _nt(hn, win_ref[0:O_KR_END, :])
        pi_ref[...] = u[:, O_PI:O_PG]
        pg_ref[...] = u[:, O_PG:O_CQ]
        cq = u[:, O_CQ:O_CKV]
        ckv = u[:, O_CKV:O_KR]
        cq_ref[...] = cq
        ckv_ref[...] = ckv
        ag_ref[...] = _nt(hn, win_ref[O_AG:D_IN, :])
        cosv = cos_ref[...]
        sinv = sin_ref[...]
        kr = u[:, O_KR:O_KR_END] * _low_lanes()
        kr = (kr * cosv + _swap64(kr) * sinv).astype(BF16)
        rq = lax.rsqrt(jnp.mean(cq * cq, axis=-1, keepdims=True) + EPS)
        cqn = ((cq * rq) * gq_ref[...]).astype(BF16)
        rkv = lax.rsqrt(jnp.mean(ckv * ckv, axis=-1, keepdims=True) + EPS)
        ckvn = ((ckv * rkv) * gkv_ref[...]).astype(BF16)
        for hd in range(HEADS):
            qh = _nt(cqn, wq_ref[hd]) * SCALE
            z = qh[:, QK_NOPE:]
            q_ref[hd, :, 0:QK_NOPE] = qh[:, 0:QK_NOPE].astype(BF16)
            q_ref[hd, :, QK_NOPE:] = (z * cosv + _swap64(z) * sinv).astype(BF16)
            kvh = _nn(ckvn, wkv_ref[hd])
            k_ref[hd, :, 0:QK_NOPE] = kvh[:, 0:QK_NOPE].astype(BF16)
            k_ref[hd, :, QK_NOPE:] = kr
            v_ref[hd] = kvh[:, QK_NOPE:].astype(BF16)

    head = lambda w: pl.BlockSpec((HEADS, tr, w), lambda i: (0, i, 0))
    return pl.pallas_call(
        body,
        name="fwd_in",
        grid=(N // tr,),
        in_specs=[
            _rows(D, tr), _const(1, D), _const(D_IN, D), _const(1, Q_LORA), _const(HEADS, 256, Q_LORA),
            _const(1, KV_LORA), _const(HEADS, KV_LORA, 256), _rows(128, tr), _rows(128, tr),
        ],
        out_specs=[_rows(D_POOL, tr), _rows(D_POOL, tr), _rows(Q_LORA, tr), _rows(KV_LORA, tr), _rows(D_POOL, tr),
                   head(256), head(256), head(V_HEAD)],
        out_shape=[
            jax.ShapeDtypeStruct((N, D_POOL), F32), jax.ShapeDtypeStruct((N, D_POOL), F32),
            jax.ShapeDtypeStruct((N, Q_LORA), F32), jax.ShapeDtypeStruct((N, KV_LORA), F32),
            jax.ShapeDtypeStruct((N, D_POOL), F32),
            jax.ShapeDtypeStruct((HEADS, N, 256), BF16), jax.ShapeDtypeStruct((HEADS, N, 256), BF16),
            jax.ShapeDtypeStruct((HEADS, N, V_HEAD), BF16),
        ],
        compiler_params=_cparams(dimension_semantics=("arbitrary",)),
    )(h, norm_g, win, gq, wq, gkv, wkv, cosf, sinf)


def _attn_fwd(q, k, v):
    tiles = _attn_tiles()

    def body(q_ref, k_ref, v_ref, o_ref, lse_ref):
        step = pl.program_id(1)
        for idx, (q0, rows, klen) in enumerate(tiles):
            @pl.when(step == idx)
            def _(q0=q0, rows=rows, klen=klen):
                s = _masked_scores(q_ref[0, q0:q0 + rows, :], k_ref[0, 0:klen, :], rows, klen)
                m = jnp.max(s, axis=-1, keepdims=True)
                p = jnp.exp(s - m)
                l = jnp.sum(p, axis=-1, keepdims=True)
                o_ref[q0:q0 + rows, :] = _nn(p.astype(BF16), v_ref[0, 0:klen, :]) / l
                lse_ref[0, q0:q0 + rows, :] = m + jnp.log(l)

    per_head = lambda w: pl.BlockSpec((1, N, w), lambda h, t: (h, 0, 0))
    return pl.pallas_call(
        body,
        name="attn_fwd",
        grid=(HEADS, len(tiles)),
        in_specs=[per_head(256), per_head(256), per_head(V_HEAD)],
        out_specs=[pl.BlockSpec((N, V_HEAD), lambda h, t: (0, h)), per_head(1)],
        out_shape=[jax.ShapeDtypeStruct((N, HEADS * V_HEAD), F32), jax.ShapeDtypeStruct((HEADS, N, 1), F32)],
        compiler_params=_cparams(dimension_semantics=("arbitrary", "arbitrary")),
    )(q, k, v)


def _inv_count(row0, rows, w):
    row = row0 + lax.broadcasted_iota(jnp.int32, (rows, 1), 0)
    return 1.0 / jnp.clip(row - (PAD - 1), 1, w).astype(F32)


def _mid(h, tgt, pool_in, pool_gate, attn_gate, attn, pool_w, pool_scale, wout, gf):
    tr = ROWS_MID
    per = tr // HALO
    ng = len(POOL_WINDOWS)

    def body(h_ref, t_ref, pin_ref, halo_ref, pg_ref, ag_ref, at_ref, pw_ref, ps_ref, wout_ref, gf_ref,
             dh2_ref, do_ref, dag_ref, dpg_ref, dpl_ref, dwout_ref, dpw_ref, dps_ref, dgf_ref, loss_ref):
        i = pl.program_id(0)

        @pl.when(i == 0)
        def _():
            dwout_ref[...] = jnp.zeros_like(dwout_ref)
            dpw_ref[...] = jnp.zeros_like(dpw_ref)
            dps_ref[...] = jnp.zeros_like(dps_ref)
            dgf_ref[...] = jnp.zeros_like(dgf_ref)
            loss_ref[...] = jnp.zeros_like(loss_ref)

        row0 = i * tr
        real = (row0 + lax.broadcasted_iota(jnp.int32, (tr, 1), 0)) >= HEAD_ROWS
        h = h_ref[...]

        halo = jnp.where(i > 0, halo_ref[...], 0.0)
        ext = jnp.concatenate([halo, pin_ref[...]], axis=0)
        pooled = []
        for g, w in enumerate(POOL_WINDOWS):
            e = ext[:, g * POOL_GROUP:(g + 1) * POOL_GROUP]
            acc = e
            shift = 1
            while shift < w:
                acc = acc + pltpu.roll(acc, shift, 0)
                shift *= 2
            pooled.append((acc[HALO:] * _inv_count(row0, tr, w) - e[HALO:]).astype(BF16))
        pw = [pw_ref[g].astype(BF16) for g in range(ng)]
        mixed = jnp.concatenate([_nn(pooled[g], pw[g]) for g in range(ng)], axis=1)
        ps = ps_ref[...]
        mixed_s = mixed * ps
        pg = pg_ref[...]
        sig_p = _sigmoid(pg)
        silu_p = pg * sig_p
        pool_out = (silu_p * mixed_s).astype(BF16)
        ag = ag_ref[...]
        sig_a = _sigmoid(ag)
        silu_a = ag * sig_a
        at = at_ref[...]
        attn_out = (silu_a * at).astype(BF16)
        mix = _nn(pool_out, wout_ref[0:D_POOL, :]) + _nn(attn_out, wout_ref[D_POOL:D, :])
        h2 = h + mix

        r2 = lax.rsqrt(jnp.mean(h2 * h2, axis=-1, keepdims=True) + EPS)
        n2 = h2 * r2
        gfv = gf_ref[...]
        err = jnp.where(real, n2 * gfv - t_ref[...], 0.0)
        loss_ref[...] += jnp.sum(jnp.sum(err * err, axis=-1, keepdims=True), axis=0, keepdims=True) * (0.5 / D)
        dy = err * (1.0 / D)
        dgf_ref[...] += jnp.sum(dy * n2, axis=0, keepdims=True)
        dn = dy * gfv
        dh2 = r2 * (dn - n2 * jnp.mean(dn * n2, axis=-1, keepdims=True))
        dh2_ref[...] = dh2
        dh2b = dh2.astype(BF16)

        dwout_ref[0:D_POOL, :] += _tn(pool_out, dh2b)
        dwout_ref[D_POOL:D, :] += _tn(attn_out, dh2b)
        dcat = _nt(dh2b, wout_ref[...])
        dpo = dcat[:, 0:D_POOL]
        dao = dcat[:, D_POOL:D]
        do_ref[...] = dao * silu_a
        dag_ref[...] = dao * at * (sig_a * (1.0 + ag * (1.0 - sig_a)))
        dmixed_s = dpo * silu_p
        dpg_ref[...] = dpo * mixed_s * (sig_p * (1.0 + pg * (1.0 - sig_p)))
        dps_ref[...] += jnp.sum(dmixed_s * mixed, axis=0, keepdims=True)
        dmixed = (dmixed_s * ps).astype(BF16)
        dpl = []
        for g in range(ng):
            dm = dmixed[:, g * POOL_GROUP:(g + 1) * POOL_GROUP]
            dpl.append(_nt(dm, pw[g]))
            dpw_ref[g] += _tn(pooled[g], dm)
        dpl_ref[...] = jnp.concatenate(dpl, axis=1)

    halo_spec = pl.BlockSpec((HALO, D_POOL), lambda i: (jnp.maximum(i * per - 1, 0), 0))
    return pl.pallas_call(
        body,
        name="mid",
        grid=(N // tr,),
        in_specs=[
            _rows(D, tr), _rows(D, tr), _rows(D_POOL, tr), halo_spec, _rows(D_POOL, tr), _rows(D_POOL, tr),
            _rows(D_POOL, tr), _const(ng, POOL_GROUP, POOL_GROUP), _const(1, D_POOL), _const(D, D), _const(1, D),
        ],
        out_specs=[
            _rows(D, tr), _rows(D_POOL, tr), _rows(D_POOL, tr), _rows(D_POOL, tr), _rows(D_POOL, tr),
            _const(D, D), _const(ng, POOL_GROUP, POOL_GROUP), _const(1, D_POOL), _const(1, D), _const(1, 128),
        ],
        out_shape=[
            jax.ShapeDtypeStruct((N, D), F32), jax.ShapeDtypeStruct((N, D_POOL), F32),
            jax.ShapeDtypeStruct((N, D_POOL), F32), jax.ShapeDtypeStruct((N, D_POOL), F32),
            jax.ShapeDtypeStruct((N, D_POOL), F32), jax.ShapeDtypeStruct((D, D), F32),
            jax.ShapeDtypeStruct((ng, POOL_GROUP, POOL_GROUP), F32),
            jax.ShapeDtypeStruct((1, D_POOL), F32), jax.ShapeDtypeStruct((1, D), F32), jax.ShapeDtypeStruct((1, 128), F32),
        ],
        compiler_params=_cparams(dimension_semantics=("arbitrary",)),
    )(h, tgt, pool_in, pool_in, pool_gate, attn_gate, attn, pool_w, pool_scale, wout, gf)


def _attn_bwd(q, k, v, do, o, lse):
    tiles = _attn_tiles()

    def body(q_ref, k_ref, v_ref, do_ref, o_ref, lse_ref, dq_ref, dk_ref, dv_ref):
        step = pl.program_id(1)

        @pl.when(step == 0)
        def _():
            dk_ref[...] = jnp.zeros_like(dk_ref)
            dv_ref[...] = jnp.zeros_like(dv_ref)

        for idx, (q0, rows, klen) in enumerate(tiles):
            @pl.when(step == idx)
            def _(q0=q0, rows=rows, klen=klen):
                qs = pl.ds(q0, rows)
                qv = q_ref[0, qs, :]
                kv = k_ref[0, 0:klen, :]
                p = jnp.exp(_masked_scores(qv, kv, rows, klen) - lse_ref[0, qs, :])
                dov = do_ref[qs, :]
                dob = dov.astype(BF16)
                delta = jnp.sum(dov * o_ref[qs, :], axis=-1, keepdims=True)
                ds = (p * (_nt(dob, v_ref[0, 0:klen, :]) - delta)).astype(BF16)
                dq_ref[0, qs, :] = _nn(ds, kv) * SCALE
                dk_ref[0, 0:klen, :] += _tn(ds, qv)
                dv_ref[0, 0:klen, :] += _tn(p.astype(BF16), dob)

    per_head = lambda w: pl.BlockSpec((1, N, w), lambda h, t: (h, 0, 0))
    cols = pl.BlockSpec((N, V_HEAD), lambda h, t: (0, h))
    return pl.pallas_call(
        body,
        name="attn_bwd",
        grid=(HEADS, len(tiles)),
        in_specs=[per_head(256), per_head(256), per_head(V_HEAD), cols, cols, per_head(1)],
        out_specs=[per_head(256), per_head(256), per_head(V_HEAD)],
        out_shape=[
            jax.ShapeDtypeStruct((HEADS, N, 256), F32), jax.ShapeDtypeStruct((HEADS, N, 256), F32),
            jax.ShapeDtypeStruct((HEADS, N, V_HEAD), F32),
        ],
        compiler_params=_cparams(dimension_semantics=("arbitrary", "arbitrary")),
    )(q, k, v, do, o, lse)


def _bwd_in(h, dh2, dq, dk, dv, cq, ckv, dpl, dpg, dag, norm_g, win, gq, wq, gkv, wkv, cosf, sinf):
    tr = ROWS_BWD
    nb = N // tr
    per = tr // HALO
    lead = HEAD_ROWS

    def body(h_ref, dh2_ref, dq_ref, dk_ref, dv_ref, cq_ref, ckv_ref, dpl_ref, halo_ref, dpg_ref, dag_ref,
             g_ref, win_ref, gq_ref, wq_ref, gkv_ref, wkv_ref, cos_ref, sin_ref,
             gx_ref, dmeta_ref, dwin_ref, dwq_ref, dwkv_ref, dg_ref, dgq_ref, dgkv_ref, dh_buf, gx_sem):
        i = pl.program_id(0)

        @pl.when(i == 0)
        def _():
            dwin_ref[...] = jnp.zeros_like(dwin_ref)
            dwq_ref[...] = jnp.zeros_like(dwq_ref)
            dwkv_ref[...] = jnp.zeros_like(dwkv_ref)
            dg_ref[...] = jnp.zeros_like(dg_ref)
            dgq_ref[...] = jnp.zeros_like(dgq_ref)
            dgkv_ref[...] = jnp.zeros_like(dgkv_ref)

        row0 = i * tr
        h = h_ref[...]
        r = lax.rsqrt(jnp.mean(h * h, axis=-1, keepdims=True) + EPS)
        n = h * r
        gv = g_ref[...]
        hn = (n * gv).astype(BF16)
        cosv = cos_ref[...]
        sinv = sin_ref[...]
        low = _low_lanes()

        def unrope(dy):
            return dy * cosv + _swap64(dy * sinv) * low

        cq = cq_ref[...]
        rq = lax.rsqrt(jnp.mean(cq * cq, axis=-1, keepdims=True) + EPS)
        nq = cq * rq
        gqv = gq_ref[...]
        cqn = (nq * gqv).astype(BF16)
        dcqn = jnp.zeros((tr, Q_LORA), F32)
        for hd in range(HEADS):
            dqh = dq_ref[hd]
            dqf = jnp.concatenate([dqh[:, 0:QK_NOPE], unrope(dqh[:, QK_NOPE:])], axis=1).astype(BF16)
            dcqn = dcqn + _nn(dqf, wq_ref[hd])
            dwq_ref[hd] += _tn(dqf, cqn)
        dgq_ref[...] += jnp.sum(dcqn * nq, axis=0, keepdims=True)
        dnq = dcqn * gqv
        dcq = rq * (dnq - nq * jnp.mean(dnq * nq, axis=-1, keepdims=True))

        ckv = ckv_ref[...]
        rkv = lax.rsqrt(jnp.mean(ckv * ckv, axis=-1, keepdims=True) + EPS)
        nkv = ckv * rkv
        gkvv = gkv_ref[...]
        ckvn = (nkv * gkvv).astype(BF16)
        dckvn = jnp.zeros((tr, KV_LORA), F32)
        dkr = jnp.zeros((tr, 128), F32)
        for hd in range(HEADS):
            dkh = dk_ref[hd]
            dkr = dkr + dkh[:, QK_NOPE:]
            dkv = jnp.concatenate([dkh[:, 0:QK_NOPE], dv_ref[hd]], axis=1).astype(BF16)
            dckvn = dckvn + _nt(dkv, wkv_ref[hd])
            dwkv_ref[hd] += _tn(ckvn, dkv)
        dgkv_ref[...] += jnp.sum(dckvn * nkv, axis=0, keepdims=True)
        dnkv = dckvn * gkvv
        dckv = rkv * (dnkv - nkv * jnp.mean(dnkv * nkv, axis=-1, keepdims=True))
        dkr = unrope(dkr)

        cur = dpl_ref[...]
        halo = jnp.where(i < nb - 1, halo_ref[...], 0.0)
        dpi = []
        for g, w in enumerate(POOL_WINDOWS):
            sl = slice(g * POOL_GROUP, (g + 1) * POOL_GROUP)
            a = jnp.concatenate([cur[:, sl] * _inv_count(row0, tr, w), halo[:, sl] * _inv_count(row0 + tr, HALO, w)], axis=0)
            acc = a
            shift = 1
            while shift < w:
                acc = acc + pltpu.roll(acc, tr + HALO - shift, 0)
                shift *= 2
            dpi.append(acc[0:tr] - cur[:, sl])

        du = jnp.concatenate(dpi + [dpg_ref[...], dcq, dckv, dkr], axis=1).astype(BF16)
        dagb = dag_ref[...].astype(BF16)
        dwin_ref[0:O_KR_END, :] += _tn(du, hn)
        dwin_ref[O_AG:D_IN, :] += _tn(dagb, hn)
        dhn = _nn(du, win_ref[0:O_KR_END, :]) + _nn(dagb, win_ref[O_AG:D_IN, :])
        dg_ref[...] += jnp.sum(dhn * n, axis=0, keepdims=True)
        dn = dhn * gv
        dh = dh2_ref[...] + r * (dn - n * jnp.mean(dn * n, axis=-1, keepdims=True))

        first = pltpu.make_async_copy(dh_buf.at[pl.ds(lead, tr - lead), :], gx_ref.at[pl.ds(0, tr - lead), :], gx_sem)
        later = lambda step: pltpu.make_async_copy(
            dh_buf, gx_ref.at[pl.ds(pl.multiple_of(step * tr - lead, 16), tr), :], gx_sem)

        @pl.when(i == 1)
        def _():
            first.wait()

        @pl.when(i > 1)
        def _():
            later(i - 1).wait()

        dh_buf[...] = dh

        @pl.when(i == 0)
        def _():
            first.start()
            for chip in range(CHIPS):
                dmeta_ref[chip] = dh[PAD:HEAD_ROWS, chip * 256:(chip + 1) * 256]

        @pl.when(i > 0)
        def _():
            later(i).start()

        @pl.when(i == nb - 1)
        def _():
            later(i).wait()

    head = lambda w: pl.BlockSpec((HEADS, tr, w), lambda i: (0, i, 0))
    halo_spec = pl.BlockSpec((HALO, D_POOL), lambda i: (jnp.minimum((i + 1) * per, N // HALO - 1), 0))
    return pl.pallas_call(
        body,
        name="bwd_in",
        grid=(nb,),
        in_specs=[
            _rows(D, tr), _rows(D, tr), head(256), head(256), head(V_HEAD), _rows(Q_LORA, tr), _rows(KV_LORA, tr),
            _rows(D_POOL, tr), halo_spec, _rows(D_POOL, tr), _rows(D_POOL, tr),
            _const(1, D), _const(D_IN, D), _const(1, Q_LORA), _const(HEADS, 256, Q_LORA),
            _const(1, KV_LORA), _const(HEADS, KV_LORA, 256), _rows(128, tr), _rows(128, tr),
        ],
        out_specs=[
            pl.BlockSpec(memory_space=pl.ANY), _const(CHIPS, N_META, 256), _const(D_IN, D), _const(HEADS, 256, Q_LORA),
            _const(HEADS, KV_LORA, 256), _const(1, D), _const(1, Q_LORA), _const(1, KV_LORA),
        ],
        out_shape=[
            jax.ShapeDtypeStruct((S, D), F32), jax.ShapeDtypeStruct((CHIPS, N_META, 256), F32),
            jax.ShapeDtypeStruct((D_IN, D), F32), jax.ShapeDtypeStruct((HEADS, 256, Q_LORA), F32),
            jax.ShapeDtypeStruct((HEADS, KV_LORA, 256), F32),
            jax.ShapeDtypeStruct((1, D), F32), jax.ShapeDtypeStruct((1, Q_LORA), F32), jax.ShapeDtypeStruct((1, KV_LORA), F32),
        ],
        scratch_shapes=[pltpu.VMEM((tr, D), F32), pltpu.SemaphoreType.DMA],
        compiler_params=_cparams(dimension_semantics=("arbitrary",)),
    )(h, dh2, dq, dk, dv, cq, ckv, dpl, dpl, dpg, dag, norm_g, win, gq, wq, gkv, wkv, cosf, sinf)


def _local_step(h, tgt, norm_g, win, gq, wq, gkv, wkv, pool_w, pool_scale, wout, gf, cosf, sinf):
    pool_in, pool_gate, cq, ckv, attn_gate, q, k, v = _fwd_in(h, norm_g, win, gq, wq, gkv, wkv, cosf, sinf)
    attn, lse = _attn_fwd(q, k, v)
    dh2, do, dag, dpg, dpl, dwout, dpw, dps, dgf, loss = _mid(
        h, tgt, pool_in, pool_gate, attn_gate, attn, pool_w, pool_scale, wout, gf)
    dq, dk, dv = _attn_bwd(q, k, v, do, attn, lse)
    gx, dmeta, dwin, dwq, dwkv, dg, dgq, dgkv = _bwd_in(
        h, dh2, dq, dk, dv, cq, ckv, dpl, dpg, dag, norm_g, win, gq, wq, gkv, wkv, cosf, sinf)
    return dict(gx=gx, dmeta=dmeta, dwin=dwin, dwq=dwq, dwkv=dwkv, dwout=dwout, dg=dg, dgq=dgq, dgkv=dgkv,
                dpw=dpw, dps=dps, dgf=dgf, loss=loss)


_CHIP_RELS = ((0, 0), (1, 0), (0, 1), (1, 1))

_ARR_ROWS = (SHARD_IN, SHARD_OUT, 256, KV_LORA, N_META)
_ARR_COLS = (D, D, Q_LORA, 256, 256)
_PIECES = (
    (0, 0, 256, 0), (0, 256, SHARD_IN - 256, 1),
    (1, 0, 128, 0), (1, 128, 128, 1),
    (2, 0, 128, 0), (2, 128, 128, 1),
    (3, 0, 64, 0), (3, 64, 64, 1),
    (4, 0, N_META, 0),
)
_NP = len(_PIECES)
_PIECE_MAX = (256, 128, 128, 64, N_META)


def _gathered_at(refs, arr, chip, r0, n):
    if arr in (0, 1):
        return refs[arr].at[pl.ds(pl.multiple_of(_ARR_ROWS[arr] * chip + r0, 16), n), :]
    return refs[arr].at[chip, pl.ds(r0, n), :]


def _remote(src, dst, send_sem, recv_sem, to):
    return pltpu.make_async_remote_copy(src_ref=src, dst_ref=dst, send_sem=send_sem, recv_sem=recv_sem,
                                        device_id=to, device_id_type=MESH)


def _gather_weights(winT_s, wout_s, wqT_s, wkv_s, meta_s, x2, tgt2):
    def body(win_ref, wout_ref, wq_ref, wkv_ref, meta_ref, x_ref, t_ref, win_o, wout_o, wq_o, wkv_o, h_o, tp_o,
             s_win, s_wout, s_wq, s_wkv, meta_all, head_buf, ici_send, ici_recv, fwd_send, fwd_recv, loc_sems):
        x, y, c = lax.axis_index("x"), lax.axis_index("y"), lax.axis_index("c")
        me = 2 * x + y
        stage = (s_win, s_wout, s_wq, s_wkv, meta_ref)
        outs = (win_o, wout_o, wq_o, wkv_o, meta_all)

        frames = pl.ds(HEAD_ROWS, S)
        local = [pltpu.make_async_copy(x_ref, h_o.at[frames, :], loc_sems.at[0]),
                 pltpu.make_async_copy(t_ref, tp_o.at[frames, :], loc_sems.at[1])]
        for cp in local:
            cp.start()

        s_win[...] = win_ref[...].astype(BF16)
        s_wout[...] = wout_ref[...].astype(BF16)
        s_wq[0:QK, :] = wq_ref[...].astype(BF16)
        s_wq[QK:256, :] = jnp.zeros((256 - QK, Q_LORA), BF16)
        s_wkv[...] = wkv_ref[...].astype(BF16)

        def chip_of(rel):
            fx, fy = _CHIP_RELS[rel]
            return 2 * (x ^ fx) + (y ^ fy)

        def same_core_of(rel):
            fx, fy = _CHIP_RELS[rel]
            return (x ^ fx, y ^ fy, c)

        def ici_copy(rel, i, src_chip, to):
            arr, r0, n, _ = _PIECES[i]
            k = (rel - 1) * _NP + i
            return _remote(stage[arr].at[pl.ds(r0, n), :], _gathered_at(outs, arr, src_chip, r0, n),
                           ici_send.at[k], ici_recv.at[k], to)

        def fwd_copy(rel, i, to):
            arr, r0, n, _ = _PIECES[i]
            k = (rel - 1) * _NP + i
            place = _gathered_at(outs, arr, chip_of(rel), r0, n)
            return _remote(place, place, fwd_send.at[k], fwd_recv.at[k], to)

        for core in (0, 1):
            @pl.when(c == core)
            def _(core=core):
                mine = [i for i in range(_NP) if _PIECES[i][3] == core]
                theirs = [i for i in range(_NP) if _PIECES[i][3] != core]
                sends = [ici_copy(rel, i, me, same_core_of(rel)) for rel in (1, 2, 3) for i in mine]
                for cp in sends:
                    cp.start()
                for arr in range(5):
                    _gathered_at(outs, arr, me, 0, _ARR_ROWS[arr])[...] = stage[arr][...]
                for rel in (1, 2, 3):
                    for i in mine:
                        ici_copy(rel, i, chip_of(rel), (x, y, c)).wait_recv()
                        fwd = fwd_copy(rel, i, (x, y, 1 - c))
                        fwd.start()
                        sends.append(fwd)
                for rel in (1, 2, 3):
                    for i in theirs:
                        fwd_copy(rel, i, (x, y, c)).wait_recv()
                for cp in sends:
                    cp.wait_send()

        head_buf[...] = jnp.zeros_like(head_buf)
        zeros = pltpu.make_async_copy(head_buf, tp_o.at[pl.ds(0, HEAD_ROWS), :], loc_sems.at[2])
        zeros.start()
        zeros.wait()
        for chip in range(CHIPS):
            head_buf[PAD:HEAD_ROWS, chip * 256:(chip + 1) * 256] = meta_all[chip]
        head = pltpu.make_async_copy(head_buf, h_o.at[pl.ds(0, HEAD_ROWS), :], loc_sems.at[2])
        head.start()
        head.wait()
        for cp in local:
            cp.wait()

    vm = pl.BlockSpec(memory_space=pltpu.VMEM)
    hbm = pl.BlockSpec(memory_space=pl.ANY)
    return pl.pallas_call(
        body,
        name="gather_weights",
        in_specs=[vm] * 5 + [hbm] * 2,
        out_specs=[vm] * 4 + [hbm] * 2,
        out_shape=[
            jax.ShapeDtypeStruct((D_IN, D), BF16), jax.ShapeDtypeStruct((D, D), BF16),
            jax.ShapeDtypeStruct((CHIPS, 256, Q_LORA), BF16), jax.ShapeDtypeStruct((CHIPS, KV_LORA, 256), BF16),
            jax.ShapeDtypeStruct((N, D), F32), jax.ShapeDtypeStruct((N, D), F32),
        ],
        scratch_shapes=[pltpu.VMEM((_ARR_ROWS[a], _ARR_COLS[a]), BF16) for a in range(4)]
        + [pltpu.VMEM((CHIPS, N_META, 256), F32), pltpu.VMEM((HEAD_ROWS, D), F32)]
        + [pltpu.SemaphoreType.DMA((3 * _NP,))] * 4 + [pltpu.SemaphoreType.DMA((3,))],
        compiler_params=_cparams(),
    )(winT_s, wout_s, wqT_s, wkv_s, meta_s, x2, tgt2)


_SM_ROWS = (len(POOL_WINDOWS) * POOL_GROUP, VEC_ROWS)
_SM_COLS = (POOL_GROUP, D)
_SM_PIECES = ((0, 0, 256, 0), (0, 256, 256, 1), (1, 0, VEC_ROWS, 0))
_NSP = len(_SM_PIECES)


def _reduce_grads(dwin, dwout, dwq, dwkv, dmeta4, dpw, dg, dgf, dgq, dgkv, dps, loss):
    def body(dwin_ref, dwout_ref, dwq_ref, dwkv_ref, dmeta_ref, dpw_ref, dg_ref, dgf_ref, dgq_ref, dgkv_ref, dps_ref,
             loss_ref, gwin_o, gwout_o, gwq_o, gwkv_o, gmeta_o, gpw_o, gg_o, ggf_o, ggq_o, ggkv_o, gps_o, gloss_o,
             sb0, sb1, sb2, sb3, sb4, st0, st1, st2, st3, st4, rc0, rc1, rc2, rc3, rc4,
             vec, sm_sb0, sm_sb1, sm_cs0, sm_cs1, sm_rc0, sm_rc1, vec_fin,
             d2d_send, d2d_recv, ici_send, ici_recv, fin_send, fin_recv,
             swap_send, swap_recv, smi_send, smi_recv, smf_send, smf_recv):
        x, y, c = lax.axis_index("x"), lax.axis_index("y"), lax.axis_index("c")
        me = 2 * x + y
        grads = (dwin_ref, dwout_ref, dwq_ref, dwkv_ref, dmeta_ref)
        outs = (gwin_o, gwout_o, gwq_o, gwkv_o, gmeta_o)
        sib_buf = (sb0, sb1, sb2, sb3, sb4)
        stage = (st0, st1, st2, st3, st4)
        recv = (rc0, rc1, rc2, rc3, rc4)
        sm_mine = (dpw_ref, vec)
        sm_sib = (sm_sb0, sm_sb1)
        sm_chip = (sm_cs0, sm_cs1)
        sm_recv = (sm_rc0, sm_rc1)
        sm_out = (gpw_o, vec_fin)
        sibling = (x, y, 1 - c)

        def chip_of(rel):
            fx, fy = _CHIP_RELS[rel]
            return 2 * (x ^ fx) + (y ^ fy)

        def same_core_of(rel):
            fx, fy = _CHIP_RELS[rel]
            return (x ^ fx, y ^ fy, c)

        def slot(bufs, i, idx):
            arr, _, n, _ = _PIECES[i]
            return bufs[arr].at[idx, pl.ds(0, n), :]

        def d2d_copy(rel, i):
            arr, r0, n, _ = _PIECES[i]
            k = rel * _NP + i
            return _remote(_gathered_at(grads, arr, chip_of(rel), r0, n), slot(sib_buf, i, rel),
                           d2d_send.at[k], d2d_recv.at[k], sibling)

        def ici_copy(rel, i):
            k = (rel - 1) * _NP + i
            return _remote(slot(stage, i, rel - 1), slot(recv, i, rel - 1), ici_send.at[k], ici_recv.at[k],
                           same_core_of(rel))

        def fin_copy(i):
            arr, r0, n, _ = _PIECES[i]
            place = outs[arr].at[pl.ds(r0, n), :]
            return _remote(place, place, fin_send.at[i], fin_recv.at[i], sibling)

        def sm_ici_copy(rel, j):
            blk, r0, n, _ = _SM_PIECES[j]
            k = (rel - 1) * _NSP + j
            return _remote(sm_chip[blk].at[pl.ds(r0, n), :], sm_recv[blk].at[rel - 1, pl.ds(r0, n), :],
                           smi_send.at[k], smi_recv.at[k], same_core_of(rel))

        def sm_fin_copy(j):
            blk, r0, n, _ = _SM_PIECES[j]
            place = sm_out[blk].at[pl.ds(r0, n), :]
            return _remote(place, place, smf_send.at[j], smf_recv.at[j], sibling)

        vec[...] = jnp.zeros_like(vec)
        vec[0:1, :] = dg_ref[...]
        vec[1:2, :] = dgf_ref[...]
        vec[2:3, V_GQ:V_GQ + Q_LORA] = dgq_ref[...]
        vec[2:3, V_GKV:V_GKV + KV_LORA] = dgkv_ref[...]
        vec[2:3, V_PS:V_PS + D_POOL] = dps_ref[...]
        vec[2:3, V_LOSS:D] = loss_ref[...]
        swaps = [_remote(sm_mine[b], sm_sib[b], swap_send.at[b], swap_recv.at[b], sibling) for b in (0, 1)]
        for cp in swaps:
            cp.start()

        for core in (0, 1):
            @pl.when(c == core)
            def _(core=core):
                mine = [i for i in range(_NP) if _PIECES[i][3] == core]
                theirs = [i for i in range(_NP) if _PIECES[i][3] != core]
                sm_mine_p = [j for j in range(_NSP) if _SM_PIECES[j][3] == core]
                sm_theirs_p = [j for j in range(_NSP) if _SM_PIECES[j][3] != core]
                sends = list(swaps)

                for rel in (1, 2, 3, 0):
                    for i in theirs:
                        cp = d2d_copy(rel, i)
                        cp.start()
                        sends.append(cp)

                for b in (0, 1):
                    swaps[b].wait_recv()
                    sm_chip[b][...] = sm_mine[b][...] + sm_sib[b][...]
                for rel in (1, 2, 3):
                    for j in sm_mine_p:
                        cp = sm_ici_copy(rel, j)
                        cp.start()
                        sends.append(cp)

                for rel in (1, 2, 3):
                    for i in mine:
                        arr, r0, n, _ = _PIECES[i]
                        d2d_copy(rel, i).wait_recv()
                        total = _gathered_at(grads, arr, chip_of(rel), r0, n)[...] + slot(sib_buf, i, rel)[...]
                        slot(stage, i, rel - 1)[...] = total.astype(stage[arr].dtype)
                        cp = ici_copy(rel, i)
                        cp.start()
                        sends.append(cp)

                for i in mine:
                    arr, r0, n, _ = _PIECES[i]
                    d2d_copy(0, i).wait_recv()
                    total = _gathered_at(grads, arr, me, r0, n)[...] + slot(sib_buf, i, 0)[...]
                    for rel in (1, 2, 3):
                        ici_copy(rel, i).wait_recv()
                        total = total + slot(recv, i, rel - 1)[...].astype(F32)
                    outs[arr][pl.ds(r0, n), :] = total
                    cp = fin_copy(i)
                    cp.start()
                    sends.append(cp)

                for j in sm_mine_p:
                    blk, r0, n, _ = _SM_PIECES[j]
                    for rel in (1, 2, 3):
                        sm_ici_copy(rel, j).wait_recv()
                    total = jnp.zeros((n, _SM_COLS[blk]), F32)
                    for chip in range(CHIPS):
                        flips = chip ^ me
                        rel = jnp.where(flips == 2, 1, jnp.where(flips == 1, 2, flips))
                        theirs_rows = sm_recv[blk][jnp.maximum(rel - 1, 0), pl.ds(r0, n), :]
                        total = total + jnp.where(rel == 0, sm_chip[blk][pl.ds(r0, n), :], theirs_rows)
                    sm_out[blk][pl.ds(r0, n), :] = total
                    cp = sm_fin_copy(j)
                    cp.start()
                    sends.append(cp)

                for i in theirs:
                    fin_copy(i).wait_recv()
                for j in sm_theirs_p:
                    sm_fin_copy(j).wait_recv()
                for cp in sends:
                    cp.wait_send()

        gg_o[...] = vec_fin[0:1, :]
        ggf_o[...] = vec_fin[1:2, :]
        ggq_o[...] = vec_fin[2:3, V_GQ:V_GQ + Q_LORA]
        ggkv_o[...] = vec_fin[2:3, V_GKV:V_GKV + KV_LORA]
        gps_o[...] = vec_fin[2:3, V_PS:V_PS + D_POOL]
        gloss_o[...] = vec_fin[2:3, V_LOSS:D]

    vm = pl.BlockSpec(memory_space=pltpu.VMEM)
    piece_buf = lambda lead, dtype: [pltpu.VMEM((lead, _PIECE_MAX[a], _ARR_COLS[a]), F32 if a == 4 else dtype)
                                     for a in range(5)]
    sm_buf = lambda *lead: [pltpu.VMEM(lead + (_SM_ROWS[b], _SM_COLS[b]), F32) for b in (0, 1)]
    dma = lambda n: [pltpu.SemaphoreType.DMA((n,))] * 2
    return pl.pallas_call(
        body,
        name="reduce_grads",
        in_specs=[vm] * 12,
        out_specs=[vm] * 12,
        out_shape=[jax.ShapeDtypeStruct((_ARR_ROWS[a], _ARR_COLS[a]), F32) for a in range(5)]
        + [jax.ShapeDtypeStruct((_SM_ROWS[0], _SM_COLS[0]), F32), jax.ShapeDtypeStruct((1, D), F32),
           jax.ShapeDtypeStruct((1, D), F32), jax.ShapeDtypeStruct((1, Q_LORA), F32),
           jax.ShapeDtypeStruct((1, KV_LORA), F32), jax.ShapeDtypeStruct((1, D_POOL), F32),
           jax.ShapeDtypeStruct((1, 128), F32)],
        scratch_shapes=piece_buf(CHIPS, F32) + piece_buf(3, BF16) + piece_buf(3, BF16)
        + [pltpu.VMEM((VEC_ROWS, D), F32)] + sm_buf() + sm_buf() + sm_buf(3) + [pltpu.VMEM((VEC_ROWS, D), F32)]
        + dma(CHIPS * _NP) + dma(3 * _NP) + dma(_NP) + dma(2) + dma(3 * _NSP) + dma(_NSP),
        compiler_params=_cparams(),
    )(dwin, dwout, dwq, dwkv, dmeta4, dpw, dg, dgf, dgq, dgkv, dps, loss)


def _adamw_math(w, g, m, v):
    m = B1 * m + (1.0 - B1) * g
    v = B2 * v + (1.0 - B2) * (g * g)
    m_hat = m / C1
    v_hat = v / C2
    delta = -LR * (m_hat / (jnp.sqrt(v_hat) + ADAM_EPS) + WD * w)
    return delta, m, v


def _adamw_rows(name, w, g, m, v, block_rows):
    rows, cols = w.shape

    def body(w_ref, g_ref, m_ref, v_ref, go_ref, d_ref, nm_ref, nv_ref):
        g = g_ref[...]
        go_ref[...] = g
        d_ref[...], nm_ref[...], nv_ref[...] = _adamw_math(w_ref[...], g, m_ref[...], v_ref[...])

    spec = pl.BlockSpec((block_rows, cols), lambda i: (i, 0))
    return pl.pallas_call(
        body,
        name=name,
        grid=(rows // block_rows,),
        in_specs=[spec] * 4,
        out_specs=[spec] * 4,
        out_shape=[jax.ShapeDtypeStruct(w.shape, F32)] * 4,
        compiler_params=_cparams(dimension_semantics=("arbitrary",)),
    )(w, g, m, v)


def _adamw_small(groups):
    n = len(groups)

    def body(*refs):
        ins, outs = refs[:4 * n], refs[4 * n:]
        for t in range(n):
            w_ref, g_ref, m_ref, v_ref = ins[4 * t:4 * t + 4]
            g = g_ref[0:w_ref.shape[0], :]
            outs[4 * t][...] = g
            outs[4 * t + 1][...], outs[4 * t + 2][...], outs[4 * t + 3][...] = _adamw_math(
                w_ref[...], g, m_ref[...], v_ref[...])

    vm = pl.BlockSpec(memory_space=pltpu.VMEM)
    flat = [a for grp in groups for a in grp]
    outs = pl.pallas_call(
        body,
        name="adamw_small",
        in_specs=[vm] * (4 * n),
        out_specs=[vm] * (4 * n),
        out_shape=[jax.ShapeDtypeStruct(grp[0].shape, F32) for grp in groups for _ in range(4)],
        compiler_params=_cparams(),
    )(*flat)
    return [tuple(outs[4 * t:4 * t + 4]) for t in range(n)]


def _rope_tables():
    half = QK_ROPE // 2
    inv_freq = 1.0 / (ROPE_THETA ** (jnp.arange(half, dtype=F32) / half))
    pos = jnp.arange(N, dtype=F32) - PAD
    ang = pos[:, None] * inv_freq[None, :]
    cos, sin = jnp.cos(ang), jnp.sin(ang)
    zero = jnp.zeros((N, 128 - QK_ROPE), F32)
    return jnp.concatenate([cos, cos, zero], axis=1), jnp.concatenate([-sin, sin, zero], axis=1)


def kernel(x, meta_tokens, norm_g, w_in, q_norm_g, w_q_b, kv_norm_g, w_kv_b, pool_w, pool_scale, w_out, final_norm_g, loss_target, m_meta_tokens, m_norm_g, m_w_in, m_q_norm_g, m_w_q_b, m_kv_norm_g, m_w_kv_b, m_pool_w, m_pool_scale, m_w_out, m_final_norm_g, v_meta_tokens, v_norm_g, v_w_in, v_q_norm_g, v_w_q_b, v_kv_norm_g, v_w_kv_b, v_pool_w, v_pool_scale, v_w_out, v_final_norm_g):
    tr = lambda a: a[0].T
    win, wout, wq, wkv, h, tgt = _gather_weights(tr(w_in), w_out[0], tr(w_q_b), w_kv_b[0], meta_tokens, x[0], loss_target[0])
    cosf, sinf = _rope_tables()
    gf = final_norm_g.reshape(1, D)

    part = _local_step(h, tgt, norm_g, win, q_norm_g, wq, kv_norm_g, wkv, pool_w[0], pool_scale, wout, gf, cosf, sinf)

    pw2 = lambda a: a.reshape(len(POOL_WINDOWS) * POOL_GROUP, POOL_GROUP)
    gwinT, gwout, gwqT, gwkv, gmeta, gpw, gg, ggf, ggq, ggkv, gps, gloss = _reduce_grads(
        part["dwin"], part["dwout"], part["dwq"], part["dwkv"], part["dmeta"], pw2(part["dpw"]), part["dg"],
        part["dgf"], part["dgq"], part["dgkv"], part["dps"], part["loss"])

    r_in = _adamw_rows("adamw_w_in", tr(w_in), gwinT, tr(m_w_in), tr(v_w_in), 248)
    r_out = _adamw_rows("adamw_w_out", w_out[0], gwout, m_w_out[0], v_w_out[0], 128)
    fn2 = lambda a: a.reshape(1, D)
    r_meta, r_norm, r_gq, r_wq, r_gkv, r_wkv, r_pw, r_ps, r_fn = _adamw_small([
        (meta_tokens, gmeta, m_meta_tokens, v_meta_tokens),
        (norm_g, gg, m_norm_g, v_norm_g),
        (q_norm_g, ggq, m_q_norm_g, v_q_norm_g),
        (tr(w_q_b), gwqT, tr(m_w_q_b), tr(v_w_q_b)),
        (kv_norm_g, ggkv, m_kv_norm_g, v_kv_norm_g),
        (w_kv_b[0], gwkv, m_w_kv_b[0], v_w_kv_b[0]),
        (pw2(pool_w), gpw, pw2(m_pool_w), pw2(v_pool_w)),
        (pool_scale, gps, m_pool_scale, v_pool_scale),
        (fn2(final_norm_g), ggf, fn2(m_final_norm_g), fn2(v_final_norm_g)),
    ])
    untr = lambda a: a.T[None]
    pw4 = lambda a: a.reshape(1, len(POOL_WINDOWS), POOL_GROUP, POOL_GROUP)
    per_kind = [[
        r_meta[kind], r_norm[kind], untr(r_in[kind]), r_gq[kind], untr(r_wq[kind]), r_gkv[kind], r_wkv[kind][None],
        pw4(r_pw[kind]), r_ps[kind], r_out[kind][None], r_fn[kind].reshape(D),
    ] for kind in range(4)]
    return (gloss[0, 0], part["gx"][None], *per_kind[0], *per_kind[1], *per_kind[2], *per_kind[3])
```

```python
import jax
import jax.numpy as jnp
import numpy as np
from jax import lax
from jax.experimental import pallas as pl
from jax.experimental.pallas import tpu as pltpu

F32 = jnp.float32
BF16 = jnp.bfloat16

D = 1024
S = 2048
N_META = 16
PAD = 112
HEAD_ROWS = PAD + N_META
N = HEAD_ROWS + S
D_POOL = 512
POOL_WINDOWS = (2, 4, 8, 16)
POOL_GROUP = 128
HALO = 16
HEADS = 4
QK_NOPE = 128
QK_ROPE = 64
QK = QK_NOPE + QK_ROPE
V_HEAD = 128
Q_LORA = 256
KV_LORA = 128
D_IN = 1984
EPS = 1e-6
ROPE_THETA = 10000.0
SCALE = QK ** -0.5
CHIPS = 4

ROWS_FWD = 544
ROWS_MID = 544
ROWS_BWD = 544
TK = 128
TQ = 256
NQ = S // TQ

O_PI, O_PG, O_CQ, O_CKV, O_KR, O_AG = 0, 512, 1024, 1280, 1408, 1472
O_KR_END = O_KR + 128
SHARD_IN = D_IN // CHIPS
SHARD_OUT = D // CHIPS

LR, B1, B2, ADAM_EPS, WD, STEP = 0.001, 0.9, 0.999, 1e-08, 0.01, 10
C1 = 1.0 - B1**STEP
C2 = 1.0 - B2**STEP

VMEM_LIMIT = 60 * 1024 * 1024
MESH = pl.DeviceIdType.MESH
NEG = -1e30

VEC_ROWS = 8
V_GQ, V_GKV, V_PS, V_LOSS = 0, 256, 384, 896


def _cparams(**kw):
    return pltpu.CompilerParams(vmem_limit_bytes=VMEM_LIMIT, **kw)


def _nt(a, b):
    return lax.dot_general(a, b, (((1,), (1,)), ((), ())), preferred_element_type=F32)


def _tn(a, b):
    return lax.dot_general(a, b, (((0,), (0,)), ((), ())), preferred_element_type=F32)


def _nn(a, b):
    return jnp.dot(a, b, preferred_element_type=F32)


def _swap64(t):
    return pltpu.roll(t, 32, 1) + pltpu.roll(t, 96, 1)


def _sigmoid(x):
    return 1.0 / (1.0 + jnp.exp(-x))


def _low_lanes():
    return (lax.broadcasted_iota(jnp.int32, (1, 128), 1) < QK_ROPE).astype(F32)


def _rows(w, rows):
    return pl.BlockSpec((rows, w), lambda i: (i, 0))


def _const(*shape):
    return pl.BlockSpec(shape, lambda *_: (0,) * len(shape), pipeline_mode=pl.Buffered(1))


def _attn_tiles():
    return [(0, TK, TK)] + [(TK + TQ * t, TQ, TK + TQ * (t + 1)) for t in range(NQ)]


def _masked_scores(q, k, rows, klen):
    s = _nt(q, k)
    col = lax.broadcasted_iota(jnp.int32, (1, TK), 1)
    head_bias = jnp.where(col >= PAD, 0.0, NEG)
    if klen == TK:
        return s + head_bias
    r = lax.broadcasted_iota(jnp.int32, (rows, 1), 0) >> 6
    c = lax.broadcasted_iota(jnp.int32, (1, rows), 1) >> 6
    diag_bias = jnp.where(c <= r, 0.0, NEG)
    parts = [s[:, 0:TK] + head_bias]
    if klen - rows > TK:
        parts.append(s[:, TK:klen - rows])
    parts.append(s[:, klen - rows:klen] + diag_bias)
    return jnp.concatenate(parts, axis=1)


def _fwd_in(h, norm_g, win, gq, wq, gkv, wkv, cosf, sinf):
    tr = ROWS_FWD

    def body(h_ref, g_ref, win_ref, gq_ref, wq_ref, gkv_ref, wkv_ref, cos_ref, sin_ref,
             pi_ref, pg_ref, cq_ref, ckv_ref, ag_ref, q_ref, k_ref, v_ref):
        h = h_ref[...]
        r = lax.rsqrt(jnp.mean(h * h, axis=-1, keepdims=True) + EPS)
        hn = ((h * r) * g_ref[...]).astype(BF16)
        u = _nt(hn, win_ref[0:O_KR_END, :])
        pi_ref[...] = u[:, O_PI:O_PG]
        pg_ref[...] = u[:, O_PG:O_CQ]
        cq = u[:, O_CQ:O_CKV]
        ckv = u[:, O_CKV:O_KR]
        cq_ref[...] = cq
        ckv_ref[...] = ckv
        ag_ref[...] = _nt(hn, win_ref[O_AG:D_IN, :])
        cosv = cos_ref[...]
        sinv = sin_ref[...]
        kr = u[:, O_KR:O_KR_END] * _low_lanes()
        kr = (kr * cosv + _swap64(kr) * sinv).astype(BF16)
        rq = lax.rsqrt(jnp.mean(cq * cq, axis=-1, keepdims=True) + EPS)
        cqn = ((cq * rq) * gq_ref[...]).astype(BF16)
        rkv = lax.rsqrt(jnp.mean(ckv * ckv, axis=-1, keepdims=True) + EPS)
        ckvn = ((ckv * rkv) * gkv_ref[...]).astype(BF16)
        for hd in range(HEADS):
            qh = _nt(cqn, wq_ref[hd]) * SCALE
            z = qh[:, QK_NOPE:]
            q_ref[hd, :, 0:QK_NOPE] = qh[:, 0:QK_NOPE].astype(BF16)
            q_ref[hd, :, QK_NOPE:] = (z * cosv + _swap64(z) * sinv).astype(BF16)
            kvh = _nn(ckvn, wkv_ref[hd])
            k_ref[hd, :, 0:QK_NOPE] = kvh[:, 0:QK_NOPE].astype(BF16)
            k_ref[hd, :, QK_NOPE:] = kr
            v_ref[hd] = kvh[:, QK_NOPE:].astype(BF16)

    head = lambda w: pl.BlockSpec((HEADS, tr, w), lambda i: (0, i, 0))
    return pl.pallas_call(
        body,
        name="fwd_in",
        grid=(N // tr,),
        in_specs=[
            _rows(D, tr), _const(1, D), _const(D_IN, D), _const(1, Q_LORA), _const(HEADS, 256, Q_LORA),
            _const(1, KV_LORA), _const(HEADS, KV_LORA, 256), _rows(128, tr), _rows(128, tr),
        ],
        out_specs=[_rows(D_POOL, tr), _rows(D_POOL, tr), _rows(Q_LORA, tr), _rows(KV_LORA, tr), _rows(D_POOL, tr),
                   head(256), head(256), head(V_HEAD)],
        out_shape=[
            jax.ShapeDtypeStruct((N, D_POOL), F32), jax.ShapeDtypeStruct((N, D_POOL), F32),
            jax.ShapeDtypeStruct((N, Q_LORA), F32), jax.ShapeDtypeStruct((N, KV_LORA), F32),
            jax.ShapeDtypeStruct((N, D_POOL), F32),
            jax.ShapeDtypeStruct((HEADS, N, 256), BF16), jax.ShapeDtypeStruct((HEADS, N, 256), BF16),
            jax.ShapeDtypeStruct((HEADS, N, V_HEAD), BF16),
        ],
        compiler_params=_cparams(dimension_semantics=("arbitrary",)),
    )(h, norm_g, win, gq, wq, gkv, wkv, cosf, sinf)


def _attn_fwd(q, k, v):
    tiles = _attn_tiles()

    def body(q_ref, k_ref, v_ref, o_ref, lse_ref):
        step = pl.program_id(1)
        for idx, (q0, rows, klen) in enumerate(tiles):
            @pl.when(step == idx)
            def _(q0=q0, rows=rows, klen=klen):
                s = _masked_scores(q_ref[0, q0:q0 + rows, :], k_ref[0, 0:klen, :], rows, klen)
                m = jnp.max(s, axis=-1, keepdims=True)
                p = jnp.exp(s - m)
                l = jnp.sum(p, axis=-1, keepdims=True)
                o_ref[q0:q0 + rows, :] = _nn(p.astype(BF16), v_ref[0, 0:klen, :]) / l
                lse_ref[0, q0:q0 + rows, :] = m + jnp.log(l)

    per_head = lambda w: pl.BlockSpec((1, N, w), lambda h, t: (h, 0, 0))
    return pl.pallas_call(
        body,
        name="attn_fwd",
        grid=(HEADS, len(tiles)),
        in_specs=[per_head(256), per_head(256), per_head(V_HEAD)],
        out_specs=[pl.BlockSpec((N, V_HEAD), lambda h, t: (0, h)), per_head(1)],
        out_shape=[jax.ShapeDtypeStruct((N, HEADS * V_HEAD), F32), jax.ShapeDtypeStruct((HEADS, N, 1), F32)],
        compiler_params=_cparams(dimension_semantics=("arbitrary", "arbitrary")),
    )(q, k, v)


def _inv_count(row0, rows, w):
    row = row0 + lax.broadcasted_iota(jnp.int32, (rows, 1), 0)
    return 1.0 / jnp.clip(row - (PAD - 1), 1, w).astype(F32)


def _mid(h, tgt, pool_in, pool_gate, attn_gate, attn, pool_w, pool_scale, wout, gf):
    tr = ROWS_MID
    per = tr // HALO
    ng = len(POOL_WINDOWS)

    def body(h_ref, t_ref, pin_ref, halo_ref, pg_ref, ag_ref, at_ref, pw_ref, ps_ref, wout_ref, gf_ref,
             dh2_ref, do_ref, dag_ref, dpg_ref, dpl_ref, dwout_ref, dpw_ref, dps_ref, dgf_ref, loss_ref):
        i = pl.program_id(0)

        @pl.when(i == 0)
        def _():
            dwout_ref[...] = jnp.zeros_like(dwout_ref)
            dpw_ref[...] = jnp.zeros_like(dpw_ref)
            dps_ref[...] = jnp.zeros_like(dps_ref)
            dgf_ref[...] = jnp.zeros_like(dgf_ref)
            loss_ref[...] = jnp.zeros_like(loss_ref)

        row0 = i * tr
        real = (row0 + lax.broadcasted_iota(jnp.int32, (tr, 1), 0)) >= HEAD_ROWS
        h = h_ref[...]

        halo = jnp.where(i > 0, halo_ref[...], 0.0)
        ext = jnp.concatenate([halo, pin_ref[...]], axis=0)
        pooled = []
        for g, w in enumerate(POOL_WINDOWS):
            e = ext[:, g * POOL_GROUP:(g + 1) * POOL_GROUP]
            acc = e
            shift = 1
            while shift < w:
                acc = acc + pltpu.roll(acc, shift, 0)
                shift *= 2
            pooled.append((acc[HALO:] * _inv_count(row0, tr, w) - e[HALO:]).astype(BF16))
        pw = [pw_ref[g].astype(BF16) for g in range(ng)]
        mixed = jnp.concatenate([_nn(pooled[g], pw[g]) for g in range(ng)], axis=1)
        ps = ps_ref[...]
        mixed_s = mixed * ps
        pg = pg_ref[...]
        sig_p = _sigmoid(pg)
        silu_p = pg * sig_p
        pool_out = (silu_p * mixed_s).astype(BF16)
        ag = ag_ref[...]
        sig_a = _sigmoid(ag)
        silu_a = ag * sig_a
        at = at_ref[...]
        attn_out = (silu_a * at).astype(BF16)
        mix = _nn(pool_out, wout_ref[0:D_POOL, :]) + _nn(attn_out, wout_ref[D_POOL:D, :])
        h2 = h + mix

        r2 = lax.rsqrt(jnp.mean(h2 * h2, axis=-1, keepdims=True) + EPS)
        n2 = h2 * r2
        gfv = gf_ref[...]
        err = jnp.where(real, n2 * gfv - t_ref[...], 0.0)
        loss_ref[...] += jnp.sum(jnp.sum(err * err, axis=-1, keepdims=True), axis=0, keepdims=True) * (0.5 / D)
        dy = err * (1.0 / D)
        dgf_ref[...] += jnp.sum(dy * n2, axis=0, keepdims=True)
        dn = dy * gfv
        dh2 = r2 * (dn - n2 * jnp.mean(dn * n2, axis=-1, keepdims=True))
        dh2_ref[...] = dh2
        dh2b = dh2.astype(BF16)

        dwout_ref[0:D_POOL, :] += _tn(pool_out, dh2b)
        dwout_ref[D_POOL:D, :] += _tn(attn_out, dh2b)
        dcat = _nt(dh2b, wout_ref[...])
        dpo = dcat[:, 0:D_POOL]
        dao = dcat[:, D_POOL:D]
        do_ref[...] = dao * silu_a
        dag_ref[...] = dao * at * (sig_a * (1.0 + ag * (1.0 - sig_a)))
        dmixed_s = dpo * silu_p
        dpg_ref[...] = dpo * mixed_s * (sig_p * (1.0 + pg * (1.0 - sig_p)))
        dps_ref[...] += jnp.sum(dmixed_s * mixed, axis=0, keepdims=True)
        dmixed = (dmixed_s * ps).astype(BF16)
        dpl = []
        for g in range(ng):
            dm = dmixed[:, g * POOL_GROUP:(g + 1) * POOL_GROUP]
            dpl.append(_nt(dm, pw[g]))
            dpw_ref[g] += _tn(pooled[g], dm)
        dpl_ref[...] = jnp.concatenate(dpl, axis=1)

    halo_spec = pl.BlockSpec((HALO, D_POOL), lambda i: (jnp.maximum(i * per - 1, 0), 0))
    return pl.pallas_call(
        body,
        name="mid",
        grid=(N // tr,),
        in_specs=[
            _rows(D, tr), _rows(D, tr), _rows(D_POOL, tr), halo_spec, _rows(D_POOL, tr), _rows(D_POOL, tr),
            _rows(D_POOL, tr), _const(ng, POOL_GROUP, POOL_GROUP), _const(1, D_POOL), _const(D, D), _const(1, D),
        ],
        out_specs=[
            _rows(D, tr), _rows(D_POOL, tr), _rows(D_POOL, tr), _rows(D_POOL, tr), _rows(D_POOL, tr),
            _const(D, D), _const(ng, POOL_GROUP, POOL_GROUP), _const(1, D_POOL), _const(1, D), _const(1, 128),
        ],
        out_shape=[
            jax.ShapeDtypeStruct((N, D), F32), jax.ShapeDtypeStruct((N, D_POOL), F32),
            jax.ShapeDtypeStruct((N, D_POOL), F32), jax.ShapeDtypeStruct((N, D_POOL), F32),
            jax.ShapeDtypeStruct((N, D_POOL), F32), jax.ShapeDtypeStruct((D, D), F32),
            jax.ShapeDtypeStruct((ng, POOL_GROUP, POOL_GROUP), F32),
            jax.ShapeDtypeStruct((1, D_POOL), F32), jax.ShapeDtypeStruct((1, D), F32), jax.ShapeDtypeStruct((1, 128), F32),
        ],
        compiler_params=_cparams(dimension_semantics=("arbitrary",)),
    )(h, tgt, pool_in, pool_in, pool_gate, attn_gate, attn, pool_w, pool_scale, wout, gf)


def _attn_bwd(q, k, v, do, o, lse):
    tiles = _attn_tiles()

    def body(q_ref, k_ref, v_ref, do_ref, o_ref, lse_ref, dq_ref, dk_ref, dv_ref):
        step = pl.program_id(1)

        @pl.when(step == 0)
        def _():
            dk_ref[...] = jnp.zeros_like(dk_ref)
            dv_ref[...] = jnp.zeros_like(dv_ref)

        for idx, (q0, rows, klen) in enumerate(tiles):
            @pl.when(step == idx)
            def _(q0=q0, rows=rows, klen=klen):
                qs = pl.ds(q0, rows)
                qv = q_ref[0, qs, :]
                kv = k_ref[0, 0:klen, :]
                p = jnp.exp(_masked_scores(qv, kv, rows, klen) - lse_ref[0, qs, :])
                dov = do_ref[qs, :]
                dob = dov.astype(BF16)
                delta = jnp.sum(dov * o_ref[qs, :], axis=-1, keepdims=True)
                ds = (p * (_nt(dob, v_ref[0, 0:klen, :]) - delta)).astype(BF16)
                dq_ref[0, qs, :] = _nn(ds, kv) * SCALE
                dk_ref[0, 0:klen, :] += _tn(ds, qv)
                dv_ref[0, 0:klen, :] += _tn(p.astype(BF16), dob)

    per_head = lambda w: pl.BlockSpec((1, N, w), lambda h, t: (h, 0, 0))
    cols = pl.BlockSpec((N, V_HEAD), lambda h, t: (0, h))
    return pl.pallas_call(
        body,
        name="attn_bwd",
        grid=(HEADS, len(tiles)),
        in_specs=[per_head(256), per_head(256), per_head(V_HEAD), cols, cols, per_head(1)],
        out_specs=[per_head(256), per_head(256), per_head(V_HEAD)],
        out_shape=[
            jax.ShapeDtypeStruct((HEADS, N, 256), F32), jax.ShapeDtypeStruct((HEADS, N, 256), F32),
            jax.ShapeDtypeStruct((HEADS, N, V_HEAD), F32),
        ],
        compiler_params=_cparams(dimension_semantics=("arbitrary", "arbitrary")),
    )(q, k, v, do, o, lse)


def _bwd_in(h, dh2, dq, dk, dv, cq, ckv, dpl, dpg, dag, norm_g, win, gq, wq, gkv, wkv, cosf, sinf):
    tr = ROWS_BWD
    nb = N // tr
    per = tr // HALO
    lead = HEAD_ROWS

    def body(h_ref, dh2_ref, dq_ref, dk_ref, dv_ref, cq_ref, ckv_ref, dpl_ref, halo_ref, dpg_ref, dag_ref,
             g_ref, win_ref, gq_ref, wq_ref, gkv_ref, wkv_ref, cos_ref, sin_ref,
             gx_ref, dmeta_ref, dwin_ref, dwq_ref, dwkv_ref, dg_ref, dgq_ref, dgkv_ref, dh_buf, gx_sem):
        i = pl.program_id(0)

        @pl.when(i == 0)
        def _():
            dwin_ref[...] = jnp.zeros_like(dwin_ref)
            dwq_ref[...] = jnp.zeros_like(dwq_ref)
            dwkv_ref[...] = jnp.zeros_like(dwkv_ref)
            dg_ref[...] = jnp.zeros_like(dg_ref)
            dgq_ref[...] = jnp.zeros_like(dgq_ref)
            dgkv_ref[...] = jnp.zeros_like(dgkv_ref)

        row0 = i * tr
        h = h_ref[...]
        r = lax.rsqrt(jnp.mean(h * h, axis=-1, keepdims=True) + EPS)
        n = h * r
        gv = g_ref[...]
        hn = (n * gv).astype(BF16)
        cosv = cos_ref[...]
        sinv = sin_ref[...]
        low = _low_lanes()

        def unrope(dy):
            return dy * cosv + _swap64(dy * sinv) * low

        cq = cq_ref[...]
        rq = lax.rsqrt(jnp.mean(cq * cq, axis=-1, keepdims=True) + EPS)
        nq = cq * rq
        gqv = gq_ref[...]
        cqn = (nq * gqv).astype(BF16)
        dcqn = jnp.zeros((tr, Q_LORA), F32)
        for hd in range(HEADS):
            dqh = dq_ref[hd]
            dqf = jnp.concatenate([dqh[:, 0:QK_NOPE], unrope(dqh[:, QK_NOPE:])], axis=1).astype(BF16)
            dcqn = dcqn + _nn(dqf, wq_ref[hd])
            dwq_ref[hd] += _tn(dqf, cqn)
        dgq_ref[...] += jnp.sum(dcqn * nq, axis=0, keepdims=True)
        dnq = dcqn * gqv
        dcq = rq * (dnq - nq * jnp.mean(dnq * nq, axis=-1, keepdims=True))

        ckv = ckv_ref[...]
        rkv = lax.rsqrt(jnp.mean(ckv * ckv, axis=-1, keepdims=True) + EPS)
        nkv = ckv * rkv
        gkvv = gkv_ref[...]
        ckvn = (nkv * gkvv).astype(BF16)
        dckvn = jnp.zeros((tr, KV_LORA), F32)
        dkr = jnp.zeros((tr, 128), F32)
        for hd in range(HEADS):
            dkh = dk_ref[hd]
            dkr = dkr + dkh[:, QK_NOPE:]
            dkv = jnp.concatenate([dkh[:, 0:QK_NOPE], dv_ref[hd]], axis=1).astype(BF16)
            dckvn = dckvn + _nt(dkv, wkv_ref[hd])
            dwkv_ref[hd] += _tn(ckvn, dkv)
        dgkv_ref[...] += jnp.sum(dckvn * nkv, axis=0, keepdims=True)
        dnkv = dckvn * gkvv
        dckv = rkv * (dnkv - nkv * jnp.mean(dnkv * nkv, axis=-1, keepdims=True))
        dkr = unrope(dkr)

        cur = dpl_ref[...]
        halo = jnp.where(i < nb - 1, halo_ref[...], 0.0)
        dpi = []
        for g, w in enumerate(POOL_WINDOWS):
            sl = slice(g * POOL_GROUP, (g + 1) * POOL_GROUP)
            a = jnp.concatenate([cur[:, sl] * _inv_count(row0, tr, w), halo[:, sl] * _inv_count(row0 + tr, HALO, w)], axis=0)
            acc = a
            shift = 1
            while shift < w:
                acc = acc + pltpu.roll(acc, tr + HALO - shift, 0)
                shift *= 2
            dpi.append(acc[0:tr] - cur[:, sl])

        du = jnp.concatenate(dpi + [dpg_ref[...], dcq, dckv, dkr], axis=1).astype(BF16)
        dagb = dag_ref[...].astype(BF16)
        dwin_ref[0:O_KR_END, :] += _tn(du, hn)
        dwin_ref[O_AG:D_IN, :] += _tn(dagb, hn)
        dhn = _nn(du, win_ref[0:O_KR_END, :]) + _nn(dagb, win_ref[O_AG:D_IN, :])
        dg_ref[...] += jnp.sum(dhn * n, axis=0, keepdims=True)
        dn = dhn * gv
        dh = dh2_ref[...] + r * (dn - n * jnp.mean(dn * n, axis=-1, keepdims=True))

        first = pltpu.make_async_copy(dh_buf.at[pl.ds(lead, tr - lead), :], gx_ref.at[pl.ds(0, tr - lead), :], gx_sem)
        later = lambda step: pltpu.make_async_copy(
            dh_buf, gx_ref.at[pl.ds(pl.multiple_of(step * tr - lead, 16), tr), :], gx_sem)

        @pl.when(i == 1)
        def _():
            first.wait()

        @pl.when(i > 1)
        def _():
            later(i - 1).wait()

        dh_buf[...] = dh

        @pl.when(i == 0)
        def _():
            first.start()
            for chip in range(CHIPS):
                dmeta_ref[chip] = dh[PAD:HEAD_ROWS, chip * 256:(chip + 1) * 256]

        @pl.when(i > 0)
        def _():
            later(i).start()

        @pl.when(i == nb - 1)
        def _():
            later(i).wait()

    head = lambda w: pl.BlockSpec((HEADS, tr, w), lambda i: (0, i, 0))
    halo_spec = pl.BlockSpec((HALO, D_POOL), lambda i: (jnp.minimum((i + 1) * per, N // HALO - 1), 0))
    return pl.pallas_call(
        body,
        name="bwd_in",
        grid=(nb,),
        in_specs=[
            _rows(D, tr), _rows(D, tr), head(256), head(256), head(V_HEAD), _rows(Q_LORA, tr), _rows(KV_LORA, tr),
            _rows(D_POOL, tr), halo_spec, _rows(D_POOL, tr), _rows(D_POOL, tr),
            _const(1, D), _const(D_IN, D), _const(1, Q_LORA), _const(HEADS, 256, Q_LORA),
            _const(1, KV_LORA), _const(HEADS, KV_LORA, 256), _rows(128, tr), _rows(128, tr),
        ],
        out_specs=[
            pl.BlockSpec(memory_space=pl.ANY), _const(CHIPS, N_META, 256), _const(D_IN, D), _const(HEADS, 256, Q_LORA),
            _const(HEADS, KV_LORA, 256), _const(1, D), _const(1, Q_LORA), _const(1, KV_LORA),
        ],
        out_shape=[
            jax.ShapeDtypeStruct((S, D), F32), jax.ShapeDtypeStruct((CHIPS, N_META, 256), F32),
            jax.ShapeDtypeStruct((D_IN, D), F32), jax.ShapeDtypeStruct((HEADS, 256, Q_LORA), F32),
            jax.ShapeDtypeStruct((HEADS, KV_LORA, 256), F32),
            jax.ShapeDtypeStruct((1, D), F32), jax.ShapeDtypeStruct((1, Q_LORA), F32), jax.ShapeDtypeStruct((1, KV_LORA), F32),
        ],
        scratch_shapes=[pltpu.VMEM((tr, D), F32), pltpu.SemaphoreType.DMA],
        compiler_params=_cparams(dimension_semantics=("arbitrary",)),
    )(h, dh2, dq, dk, dv, cq, ckv, dpl, dpl, dpg, dag, norm_g, win, gq, wq, gkv, wkv, cosf, sinf)


def _local_step(h, tgt, norm_g, win, gq, wq, gkv, wkv, pool_w, pool_scale, wout, gf, cosf, sinf):
    pool_in, pool_gate, cq, ckv, attn_gate, q, k, v = _fwd_in(h, norm_g, win, gq, wq, gkv, wkv, cosf, sinf)
    attn, lse = _attn_fwd(q, k, v)
    dh2, do, dag, dpg, dpl, dwout, dpw, dps, dgf, loss = _mid(
        h, tgt, pool_in, pool_gate, attn_gate, attn, pool_w, pool_scale, wout, gf)
    dq, dk, dv = _attn_bwd(q, k, v, do, attn, lse)
    gx, dmeta, dwin, dwq, dwkv, dg, dgq, dgkv = _bwd_in(
        h, dh2, dq, dk, dv, cq, ckv, dpl, dpg, dag, norm_g, win, gq, wq, gkv, wkv, cosf, sinf)
    return dict(gx=gx, dmeta=dmeta, dwin=dwin, dwq=dwq, dwkv=dwkv, dwout=dwout, dg=dg, dgq=dgq, dgkv=dgkv,
                dpw=dpw, dps=dps, dgf=dgf, loss=loss)


_CHIP_RELS = ((0, 0), (1, 0), (0, 1), (1, 1))

_ARR_ROWS = (SHARD_IN, SHARD_OUT, 256, KV_LORA, N_META)
_ARR_COLS = (D, D, Q_LORA, 256, 256)
_PIECES = (
    (0, 0, 256, 0), (0, 256, SHARD_IN - 256, 1),
    (1, 0, 128, 0), (1, 128, 128, 1),
    (2, 0, 128, 0), (2, 128, 128, 1),
    (3, 0, 64, 0), (3, 64, 64, 1),
    (4, 0, N_META, 0),
)
_NP = len(_PIECES)
_PIECE_MAX = (256, 128, 128, 64, N_META)


def _gathered_at(refs, arr, chip, r0, n):
    if arr in (0, 1):
        return refs[arr].at[pl.ds(pl.multiple_of(_ARR_ROWS[arr] * chip + r0, 16), n), :]
    return refs[arr].at[chip, pl.ds(r0, n), :]


def _remote(src, dst, send_sem, recv_sem, to):
    return pltpu.make_async_remote_copy(src_ref=src, dst_ref=dst, send_sem=send_sem, recv_sem=recv_sem,
                                        device_id=to, device_id_type=MESH)


def _gather_weights(winT_s, wout_s, wqT_s, wkv_s, meta_s, x2, tgt2):
    def body(win_ref, wout_ref, wq_ref, wkv_ref, meta_ref, x_ref, t_ref, win_o, wout_o, wq_o, wkv_o, h_o, tp_o,
             s_win, s_wout, s_wq, s_wkv, meta_all, head_buf, x_buf, t_buf, ici_send, ici_recv, fwd_send, fwd_recv,
             loc_sems):
        x, y, c = lax.axis_index("x"), lax.axis_index("y"), lax.axis_index("c")
        me = 2 * x + y
        stage = (s_win, s_wout, s_wq, s_wkv, meta_ref)
        outs = (win_o, wout_o, wq_o, wkv_o, meta_all)

        frames = pl.ds(HEAD_ROWS, S)
        loads = [pltpu.make_async_copy(x_ref, x_buf, loc_sems.at[0]), pltpu.make_async_copy(t_ref, t_buf, loc_sems.at[1])]
        local = [pltpu.make_async_copy(x_buf, h_o.at[frames, :], loc_sems.at[0]),
                 pltpu.make_async_copy(t_buf, tp_o.at[frames, :], loc_sems.at[1])]
        for cp in loads:
            cp.start()

        s_win[...] = win_ref[...].astype(BF16)
        s_wout[...] = wout_ref[...].astype(BF16)
        s_wq[0:QK, :] = wq_ref[...].astype(BF16)
        s_wq[QK:256, :] = jnp.zeros((256 - QK, Q_LORA), BF16)
        s_wkv[...] = wkv_ref[...].astype(BF16)

        def chip_of(rel):
            fx, fy = _CHIP_RELS[rel]
            return 2 * (x ^ fx) + (y ^ fy)

        def same_core_of(rel):
            fx, fy = _CHIP_RELS[rel]
            return (x ^ fx, y ^ fy, c)

        def ici_copy(rel, i, src_chip, to):
            arr, r0, n, _ = _PIECES[i]
            k = (rel - 1) * _NP + i
            return _remote(stage[arr].at[pl.ds(r0, n), :], _gathered_at(outs, arr, src_chip, r0, n),
                           ici_send.at[k], ici_recv.at[k], to)

        def fwd_copy(rel, i, to):
            arr, r0, n, _ = _PIECES[i]
            k = (rel - 1) * _NP + i
            place = _gathered_at(outs, arr, chip_of(rel), r0, n)
            return _remote(place, place, fwd_send.at[k], fwd_recv.at[k], to)

        for core in (0, 1):
            @pl.when(c == core)
            def _(core=core):
                mine = [i for i in range(_NP) if _PIECES[i][3] == core]
                theirs = [i for i in range(_NP) if _PIECES[i][3] != core]
                sends = [ici_copy(rel, i, me, same_core_of(rel)) for rel in (1, 2, 3) for i in mine]
                for cp in sends:
                    cp.start()
                for ld, st in zip(loads, local):
                    ld.wait()
                    st.start()
                for arr in range(5):
                    _gathered_at(outs, arr, me, 0, _ARR_ROWS[arr])[...] = stage[arr][...]
                for rel in (1, 2, 3):
                    for i in mine:
                        ici_copy(rel, i, chip_of(rel), (x, y, c)).wait_recv()
                        fwd = fwd_copy(rel, i, (x, y, 1 - c))
                        fwd.start()
                        sends.append(fwd)
                for rel in (1, 2, 3):
                    for i in theirs:
                        fwd_copy(rel, i, (x, y, c)).wait_recv()
                for cp in sends:
                    cp.wait_send()

        head_buf[...] = jnp.zeros_like(head_buf)
        zeros = pltpu.make_async_copy(head_buf, tp_o.at[pl.ds(0, HEAD_ROWS), :], loc_sems.at[2])
        zeros.start()
        zeros.wait()
        for chip in range(CHIPS):
            head_buf[PAD:HEAD_ROWS, chip * 256:(chip + 1) * 256] = meta_all[chip]
        head = pltpu.make_async_copy(head_buf, h_o.at[pl.ds(0, HEAD_ROWS), :], loc_sems.at[2])
        head.start()
        head.wait()
        for cp in local:
            cp.wait()

    vm = pl.BlockSpec(memory_space=pltpu.VMEM)
    hbm = pl.BlockSpec(memory_space=pl.ANY)
    return pl.pallas_call(
        body,
        name="gather_weights",
        in_specs=[vm] * 5 + [hbm] * 2,
        out_specs=[vm] * 4 + [hbm] * 2,
        out_shape=[
            jax.ShapeDtypeStruct((D_IN, D), BF16), jax.ShapeDtypeStruct((D, D), BF16),
            jax.ShapeDtypeStruct((CHIPS, 256, Q_LORA), BF16), jax.ShapeDtypeStruct((CHIPS, KV_LORA, 256), BF16),
            jax.ShapeDtypeStruct((N, D), F32), jax.ShapeDtypeStruct((N, D), F32),
        ],
        scratch_shapes=[pltpu.VMEM((_ARR_ROWS[a], _ARR_COLS[a]), BF16) for a in range(4)]
        + [pltpu.VMEM((CHIPS, N_META, 256), F32), pltpu.VMEM((HEAD_ROWS, D), F32), pltpu.VMEM((S, D), F32),
           pltpu.VMEM((S, D), F32)]
        + [pltpu.SemaphoreType.DMA((3 * _NP,))] * 4 + [pltpu.SemaphoreType.DMA((3,))],
        compiler_params=_cparams(),
    )(winT_s, wout_s, wqT_s, wkv_s, meta_s, x2, tgt2)


_SM_ROWS = (len(POOL_WINDOWS) * POOL_GROUP, VEC_ROWS)
_SM_COLS = (POOL_GROUP, D)
_SM_PIECES = ((0, 0, 256, 0), (0, 256, 256, 1), (1, 0, VEC_ROWS, 0))
_NSP = len(_SM_PIECES)


def _reduce_grads(dwin, dwout, dwq, dwkv, dmeta4, dpw, dg, dgf, dgq, dgkv, dps, loss):
    def body(dwin_ref, dwout_ref, dwq_ref, dwkv_ref, dmeta_ref, dpw_ref, dg_ref, dgf_ref, dgq_ref, dgkv_ref, dps_ref,
             loss_ref, gwin_o, gwout_o, gwq_o, gwkv_o, gmeta_o, gpw_o, gg_o, ggf_o, ggq_o, ggkv_o, gps_o, gloss_o,
             sb0, sb1, sb2, sb3, sb4, st0, st1, st2, st3, st4, rc0, rc1, rc2, rc3, rc4,
             vec, sm_sb0, sm_sb1, sm_cs0, sm_cs1, sm_rc0, sm_rc1, vec_fin,
             d2d_send, d2d_recv, ici_send, ici_recv, fin_send, fin_recv,
             swap_send, swap_recv, smi_send, smi_recv, smf_send, smf_recv):
        x, y, c = lax.axis_index("x"), lax.axis_index("y"), lax.axis_index("c")
        me = 2 * x + y
        grads = (dwin_ref, dwout_ref, dwq_ref, dwkv_ref, dmeta_ref)
        outs = (gwin_o, gwout_o, gwq_o, gwkv_o, gmeta_o)
        sib_buf = (sb0, sb1, sb2, sb3, sb4)
        stage = (st0, st1, st2, st3, st4)
        recv = (rc0, rc1, rc2, rc3, rc4)
        sm_mine = (dpw_ref, vec)
        sm_sib = (sm_sb0, sm_sb1)
        sm_chip = (sm_cs0, sm_cs1)
        sm_recv = (sm_rc0, sm_rc1)
        sm_out = (gpw_o, vec_fin)
        sibling = (x, y, 1 - c)

        def chip_of(rel):
            fx, fy = _CHIP_RELS[rel]
            return 2 * (x ^ fx) + (y ^ fy)

        def same_core_of(rel):
            fx, fy = _CHIP_RELS[rel]
            return (x ^ fx, y ^ fy, c)

        def slot(bufs, i, idx):
            arr, _, n, _ = _PIECES[i]
            return bufs[arr].at[idx, pl.ds(0, n), :]

        def d2d_copy(rel, i):
            arr, r0, n, _ = _PIECES[i]
            k = rel * _NP + i
            return _remote(_gathered_at(grads, arr, chip_of(rel), r0, n), slot(sib_buf, i, rel),
                           d2d_send.at[k], d2d_recv.at[k], sibling)

        def ici_copy(rel, i):
            k = (rel - 1) * _NP + i
            return _remote(slot(stage, i, rel - 1), slot(recv, i, rel - 1), ici_send.at[k], ici_recv.at[k],
                           same_core_of(rel))

        def fin_copy(i):
            arr, r0, n, _ = _PIECES[i]
            place = outs[arr].at[pl.ds(r0, n), :]
            return _remote(place, place, fin_send.at[i], fin_recv.at[i], sibling)

        def sm_ici_copy(rel, j):
            blk, r0, n, _ = _SM_PIECES[j]
            k = (rel - 1) * _NSP + j
            return _remote(sm_chip[blk].at[pl.ds(r0, n), :], sm_recv[blk].at[rel - 1, pl.ds(r0, n), :],
                           smi_send.at[k], smi_recv.at[k], same_core_of(rel))

        def sm_fin_copy(j):
            blk, r0, n, _ = _SM_PIECES[j]
            place = sm_out[blk].at[pl.ds(r0, n), :]
            return _remote(place, place, smf_send.at[j], smf_recv.at[j], sibling)

        vec[...] = jnp.zeros_like(vec)
        vec[0:1, :] = dg_ref[...]
        vec[1:2, :] = dgf_ref[...]
        vec[2:3, V_GQ:V_GQ + Q_LORA] = dgq_ref[...]
        vec[2:3, V_GKV:V_GKV + KV_LORA] = dgkv_ref[...]
        vec[2:3, V_PS:V_PS + D_POOL] = dps_ref[...]
        vec[2:3, V_LOSS:D] = loss_ref[...]
        swaps = [_remote(sm_mine[b], sm_sib[b], swap_send.at[b], swap_recv.at[b], sibling) for b in (0, 1)]
        for cp in swaps:
            cp.start()

        for core in (0, 1):
            @pl.when(c == core)
            def _(core=core):
                mine = [i for i in range(_NP) if _PIECES[i][3] == core]
                theirs = [i for i in range(_NP) if _PIECES[i][3] != core]
                sm_mine_p = [j for j in range(_NSP) if _SM_PIECES[j][3] == core]
                sm_theirs_p = [j for j in range(_NSP) if _SM_PIECES[j][3] != core]
                sends = list(swaps)

                for rel in (1, 2, 3, 0):
                    for i in theirs:
                        cp = d2d_copy(rel, i)
                        cp.start()
                        sends.append(cp)

                for b in (0, 1):
                    swaps[b].wait_recv()
                    sm_chip[b][...] = sm_mine[b][...] + sm_sib[b][...]
                for rel in (1, 2, 3):
                    for j in sm_mine_p:
                        cp = sm_ici_copy(rel, j)
                        cp.start()
                        sends.append(cp)

                for rel in (1, 2, 3):
                    for i in mine:
                        arr, r0, n, _ = _PIECES[i]
                        d2d_copy(rel, i).wait_recv()
                        total = _gathered_at(grads, arr, chip_of(rel), r0, n)[...] + slot(sib_buf, i, rel)[...]
                        slot(stage, i, rel - 1)[...] = total.astype(stage[arr].dtype)
                        cp = ici_copy(rel, i)
                        cp.start()
                        sends.append(cp)

                for i in mine:
                    arr, r0, n, _ = _PIECES[i]
                    d2d_copy(0, i).wait_recv()
                    total = _gathered_at(grads, arr, me, r0, n)[...] + slot(sib_buf, i, 0)[...]
                    for rel in (1, 2, 3):
                        ici_copy(rel, i).wait_recv()
                        total = total + slot(recv, i, rel - 1)[...].astype(F32)
                    outs[arr][pl.ds(r0, n), :] = total
                    cp = fin_copy(i)
                    cp.start()
                    sends.append(cp)

                for j in sm_mine_p:
                    blk, r0, n, _ = _SM_PIECES[j]
                    for rel in (1, 2, 3):
                        sm_ici_copy(rel, j).wait_recv()
                    total = jnp.zeros((n, _SM_COLS[blk]), F32)
                    for chip in range(CHIPS):
                        flips = chip ^ me
                        rel = jnp.where(flips == 2, 1, jnp.where(flips == 1, 2, flips))
                        theirs_rows = sm_recv[blk][jnp.maximum(rel - 1, 0), pl.ds(r0, n), :]
                        total = total + jnp.where(rel == 0, sm_chip[blk][pl.ds(r0, n), :], theirs_rows)
                    sm_out[blk][pl.ds(r0, n), :] = total
                    cp = sm_fin_copy(j)
                    cp.start()
                    sends.append(cp)

                for i in theirs:
                    fin_copy(i).wait_recv()
                for j in sm_theirs_p:
                    sm_fin_copy(j).wait_recv()
                for cp in sends:
                    cp.wait_send()

        gg_o[...] = vec_fin[0:1, :]
        ggf_o[...] = vec_fin[1:2, :]
        ggq_o[...] = vec_fin[2:3, V_GQ:V_GQ + Q_LORA]
        ggkv_o[...] = vec_fin[2:3, V_GKV:V_GKV + KV_LORA]
        gps_o[...] = vec_fin[2:3, V_PS:V_PS + D_POOL]
        gloss_o[...] = vec_fin[2:3, V_LOSS:D]

    vm = pl.BlockSpec(memory_space=pltpu.VMEM)
    piece_buf = lambda lead, dtype: [pltpu.VMEM((lead, _PIECE_MAX[a], _ARR_COLS[a]), F32 if a == 4 else dtype)
                                     for a in range(5)]
    sm_buf = lambda *lead: [pltpu.VMEM(lead + (_SM_ROWS[b], _SM_COLS[b]), F32) for b in (0, 1)]
    dma = lambda n: [pltpu.SemaphoreType.DMA((n,))] * 2
    return pl.pallas_call(
        body,
        name="reduce_grads",
        in_specs=[vm] * 12,
        out_specs=[vm] * 12,
        out_shape=[jax.ShapeDtypeStruct((_ARR_ROWS[a], _ARR_COLS[a]), F32) for a in range(5)]
        + [jax.ShapeDtypeStruct((_SM_ROWS[0], _SM_COLS[0]), F32), jax.ShapeDtypeStruct((1, D), F32),
           jax.ShapeDtypeStruct((1, D), F32), jax.ShapeDtypeStruct((1, Q_LORA), F32),
           jax.ShapeDtypeStruct((1, KV_LORA), F32), jax.ShapeDtypeStruct((1, D_POOL), F32),
           jax.ShapeDtypeStruct((1, 128), F32)],
        scratch_shapes=piece_buf(CHIPS, F32) + piece_buf(3, BF16) + piece_buf(3, BF16)
        + [pltpu.VMEM((VEC_ROWS, D), F32)] + sm_buf() + sm_buf() + sm_buf(3) + [pltpu.VMEM((VEC_ROWS, D), F32)]
        + dma(CHIPS * _NP) + dma(3 * _NP) + dma(_NP) + dma(2) + dma(3 * _NSP) + dma(_NSP),
        compiler_params=_cparams(),
    )(dwin, dwout, dwq, dwkv, dmeta4, dpw, dg, dgf, dgq, dgkv, dps, loss)


def _adamw_math(w, g, m, v):
    m = B1 * m + (1.0 - B1) * g
    v = B2 * v + (1.0 - B2) * (g * g)
    m_hat = m / C1
    v_hat = v / C2
    delta = -LR * (m_hat / (jnp.sqrt(v_hat) + ADAM_EPS) + WD * w)
    return delta, m, v


def _adamw_rows(name, w, g, m, v, block_rows):
    rows, cols = w.shape

    def body(w_ref, g_ref, m_ref, v_ref, go_ref, d_ref, nm_ref, nv_ref):
        g = g_ref[...]
        go_ref[...] = g
        d_ref[...], nm_ref[...], nv_ref[...] = _adamw_math(w_ref[...], g, m_ref[...], v_ref[...])

    spec = pl.BlockSpec((block_rows, cols), lambda i: (i, 0))
    return pl.pallas_call(
        body,
        name=name,
        grid=(rows // block_rows,),
        in_specs=[spec] * 4,
        out_specs=[spec] * 4,
        out_shape=[jax.ShapeDtypeStruct(w.shape, F32)] * 4,
        compiler_params=_cparams(dimension_semantics=("arbitrary",)),
    )(w, g, m, v)


def _adamw_small(groups):
    n = len(groups)

    def body(*refs):
        ins, outs = refs[:4 * n], refs[4 * n:]
        for t in range(n):
            w_ref, g_ref, m_ref, v_ref = ins[4 * t:4 * t + 4]
            g = g_ref[0:w_ref.shape[0], :]
            outs[4 * t][...] = g
            outs[4 * t + 1][...], outs[4 * t + 2][...], outs[4 * t + 3][...] = _adamw_math(
                w_ref[...], g, m_ref[...], v_ref[...])

    vm = pl.BlockSpec(memory_space=pltpu.VMEM)
    flat = [a for grp in groups for a in grp]
    outs = pl.pallas_call(
        body,
        name="adamw_small",
        in_specs=[vm] * (4 * n),
        out_specs=[vm] * (4 * n),
        out_shape=[jax.ShapeDtypeStruct(grp[0].shape, F32) for grp in groups for _ in range(4)],
        compiler_params=_cparams(),
    )(*flat)
    return [tuple(outs[4 * t:4 * t + 4]) for t in range(n)]


def _rope_tables():
    half = QK_ROPE // 2
    f32 = np.float32
    inv_freq = (f32(1.0) / (f32(ROPE_THETA) ** (np.arange(half, dtype=f32) / f32(half)))).astype(f32)
    pos = np.arange(N, dtype=f32) - f32(PAD)
    ang = (pos[:, None] * inv_freq[None, :]).astype(f32)
    cos, sin = np.cos(ang).astype(f32), np.sin(ang).astype(f32)
    zero = np.zeros((N, 128 - QK_ROPE), f32)
    return jnp.asarray(np.concatenate([cos, cos, zero], axis=1)), jnp.asarray(np.concatenate([-sin, sin, zero], axis=1))


def kernel(x, meta_tokens, norm_g, w_in, q_norm_g, w_q_b, kv_norm_g, w_kv_b, pool_w, pool_scale, w_out, final_norm_g, loss_target, m_meta_tokens, m_norm_g, m_w_in, m_q_norm_g, m_w_q_b, m_kv_norm_g, m_w_kv_b, m_pool_w, m_pool_scale, m_w_out, m_final_norm_g, v_meta_tokens, v_norm_g, v_w_in, v_q_norm_g, v_w_q_b, v_kv_norm_g, v_w_kv_b, v_pool_w, v_pool_scale, v_w_out, v_final_norm_g):
    tr = lambda a: a[0].T
    win, wout, wq, wkv, h, tgt = _gather_weights(tr(w_in), w_out[0], tr(w_q_b), w_kv_b[0], meta_tokens, x[0], loss_target[0])
    cosf, sinf = _rope_tables()
    gf = final_norm_g.reshape(1, D)

    part = _local_step(h, tgt, norm_g, win, q_norm_g, wq, kv_norm_g, wkv, pool_w[0], pool_scale, wout, gf, cosf, sinf)

    pw2 = lambda a: a.reshape(len(POOL_WINDOWS) * POOL_GROUP, POOL_GROUP)
    gwinT, gwout, gwqT, gwkv, gmeta, gpw, gg, ggf, ggq, ggkv, gps, gloss = _reduce_grads(
        part["dwin"], part["dwout"], part["dwq"], part["dwkv"], part["dmeta"], pw2(part["dpw"]), part["dg"],
        part["dgf"], part["dgq"], part["dgkv"], part["dps"], part["loss"])

    r_in = _adamw_rows("adamw_w_in", tr(w_in), gwinT, tr(m_w_in), tr(v_w_in), 248)
    r_out = _adamw_rows("adamw_w_out", w_out[0], gwout, m_w_out[0], v_w_out[0], 128)
    fn2 = lambda a: a.reshape(1, D)
    r_meta, r_norm, r_gq, r_wq, r_gkv, r_wkv, r_pw, r_ps, r_fn = _adamw_small([
        (meta_tokens, gmeta, m_meta_tokens, v_meta_tokens),
        (norm_g, gg, m_norm_g, v_norm_g),
        (q_norm_g, ggq, m_q_norm_g, v_q_norm_g),
        (tr(w_q_b), gwqT, tr(m_w_q_b), tr(v_w_q_b)),
        (kv_norm_g, ggkv, m_kv_norm_g, v_kv_norm_g),
        (w_kv_b[0], gwkv, m_w_kv_b[0], v_w_kv_b[0]),
        (pw2(pool_w), gpw, pw2(m_pool_w), pw2(v_pool_w)),
        (pool_scale, gps, m_pool_scale, v_pool_scale),
        (fn2(final_norm_g), ggf, fn2(m_final_norm_g), fn2(v_final_norm_g)),
    ])
    untr = lambda a: a.T[None]
    pw4 = lambda a: a.reshape(1, len(POOL_WINDOWS), POOL_GROUP, POOL_GROUP)
    per_kind = [[
        r_meta[kind], r_norm[kind], untr(r_in[kind]), r_gq[kind], untr(r_wq[kind]), r_gkv[kind], r_wkv[kind][None],
        pw4(r_pw[kind]), r_ps[kind], r_out[kind][None], r_fn[kind].reshape(D),
    ] for kind in range(4)]
    return (gloss[0, 0], part["gx"][None], *per_kind[0], *per_kind[1], *per_kind[2], *per_kind[3])
```

```python
import jax
import jax.numpy as jnp
import numpy as np
from jax import lax
from jax.experimental import pallas as pl
from jax.experimental.pallas import tpu as pltpu

F32 = jnp.float32
BF16 = jnp.bfloat16

D = 1024
S = 2048
N_META = 16
PAD = 112
HEAD_ROWS = PAD + N_META
N = HEAD_ROWS + S
D_POOL = 512
POOL_WINDOWS = (2, 4, 8, 16)
POOL_GROUP = 128
HALO = 16
HEADS = 4
QK_NOPE = 128
QK_ROPE = 64
QK = QK_NOPE + QK_ROPE
V_HEAD = 128
Q_LORA = 256
KV_LORA = 128
D_IN = 1984
EPS = 1e-6
ROPE_THETA = 10000.0
SCALE = QK ** -0.5
CHIPS = 4

ROWS_FWD = 544
ROWS_MID = 544
ROWS_BWD = 544
TK = 128
TQ = 256
NQ = S // TQ
HEADS_PER_STEP = 4
HEADS_PER_STEP_BWD = 2

O_PI, O_PG, O_CQ, O_CKV, O_KR, O_AG = 0, 512, 1024, 1280, 1408, 1472
O_KR_END = O_KR + 128
SHARD_IN = D_IN // CHIPS
SHARD_OUT = D // CHIPS

LR, B1, B2, ADAM_EPS, WD, STEP = 0.001, 0.9, 0.999, 1e-08, 0.01, 10
C1 = 1.0 - B1**STEP
C2 = 1.0 - B2**STEP

VMEM_LIMIT = 60 * 1024 * 1024
MESH = pl.DeviceIdType.MESH
NEG = -1e30

VEC_ROWS = 8
V_GQ, V_GKV, V_PS, V_LOSS = 0, 256, 384, 896


def _cparams(**kw):
    return pltpu.CompilerParams(vmem_limit_bytes=VMEM_LIMIT, **kw)


def _nt(a, b):
    return lax.dot_general(a, b, (((1,), (1,)), ((), ())), preferred_element_type=F32)


def _tn(a, b):
    return lax.dot_general(a, b, (((0,), (0,)), ((), ())), preferred_element_type=F32)


def _nn(a, b):
    return jnp.dot(a, b, preferred_element_type=F32)


def _swap64(t):
    return pltpu.roll(t, 32, 1) + pltpu.roll(t, 96, 1)


def _sigmoid(x):
    return 1.0 / (1.0 + jnp.exp(-x))


def _low_lanes():
    return (lax.broadcasted_iota(jnp.int32, (1, 128), 1) < QK_ROPE).astype(F32)


def _rows(w, rows):
    return pl.BlockSpec((rows, w), lambda i: (i, 0))


def _const(*shape):
    return pl.BlockSpec(shape, lambda *_: (0,) * len(shape), pipeline_mode=pl.Buffered(1))


def _attn_tiles():
    return [(0, TK, TK)] + [(TK + TQ * t, TQ, TK + TQ * (t + 1)) for t in range(NQ)]


def _masked_scores(q, k, rows, klen):
    s = _nt(q, k)
    col = lax.broadcasted_iota(jnp.int32, (1, TK), 1)
    head_bias = jnp.where(col >= PAD, 0.0, NEG)
    if klen == TK:
        return s + head_bias
    r = lax.broadcasted_iota(jnp.int32, (rows, 1), 0) >> 6
    c = lax.broadcasted_iota(jnp.int32, (1, rows), 1) >> 6
    diag_bias = jnp.where(c <= r, 0.0, NEG)
    parts = [s[:, 0:TK] + head_bias]
    if klen - rows > TK:
        parts.append(s[:, TK:klen - rows])
    parts.append(s[:, klen - rows:klen] + diag_bias)
    return jnp.concatenate(parts, axis=1)


def _fwd_in(h, norm_g, win, gq, wq, gkv, wkv, cosf, sinf):
    tr = ROWS_FWD

    def body(h_ref, g_ref, win_ref, gq_ref, wq_ref, gkv_ref, wkv_ref, cos_ref, sin_ref,
             pi_ref, pg_ref, cq_ref, ckv_ref, ag_ref, q_ref, k_ref, v_ref):
        h = h_ref[...]
        r = lax.rsqrt(jnp.mean(h * h, axis=-1, keepdims=True) + EPS)
        hn = ((h * r) * g_ref[...]).astype(BF16)
        u = _nt(hn, win_ref[0:O_KR_END, :])
        pi_ref[...] = u[:, O_PI:O_PG]
        pg_ref[...] = u[:, O_PG:O_CQ]
        cq = u[:, O_CQ:O_CKV]
        ckv = u[:, O_CKV:O_KR]
        cq_ref[...] = cq
        ckv_ref[...] = ckv
        ag_ref[...] = _nt(hn, win_ref[O_AG:D_IN, :])
        cosv = cos_ref[...]
        sinv = sin_ref[...]
        kr = u[:, O_KR:O_KR_END] * _low_lanes()
        kr = (kr * cosv + _swap64(kr) * sinv).astype(BF16)
        rq = lax.rsqrt(jnp.mean(cq * cq, axis=-1, keepdims=True) + EPS)
        cqn = ((cq * rq) * gq_ref[...]).astype(BF16)
        rkv = lax.rsqrt(jnp.mean(ckv * ckv, axis=-1, keepdims=True) + EPS)
        ckvn = ((ckv * rkv) * gkv_ref[...]).astype(BF16)
        for hd in range(HEADS):
            qh = _nt(cqn, wq_ref[hd]) * SCALE
            z = qh[:, QK_NOPE:]
            q_ref[hd, :, 0:QK_NOPE] = qh[:, 0:QK_NOPE].astype(BF16)
            q_ref[hd, :, QK_NOPE:] = (z * cosv + _swap64(z) * sinv).astype(BF16)
            kvh = _nn(ckvn, wkv_ref[hd])
            k_ref[hd, :, 0:QK_NOPE] = kvh[:, 0:QK_NOPE].astype(BF16)
            k_ref[hd, :, QK_NOPE:] = kr
            v_ref[hd] = kvh[:, QK_NOPE:].astype(BF16)

    head = lambda w: pl.BlockSpec((HEADS, tr, w), lambda i: (0, i, 0))
    return pl.pallas_call(
        body,
        name="fwd_in",
        grid=(N // tr,),
        in_specs=[
            _rows(D, tr), _const(1, D), _const(D_IN, D), _const(1, Q_LORA), _const(HEADS, 256, Q_LORA),
            _const(1, KV_LORA), _const(HEADS, KV_LORA, 256), _rows(128, tr), _rows(128, tr),
        ],
        out_specs=[_rows(D_POOL, tr), _rows(D_POOL, tr), _rows(Q_LORA, tr), _rows(KV_LORA, tr), _rows(D_POOL, tr),
                   head(256), head(256), head(V_HEAD)],
        out_shape=[
            jax.ShapeDtypeStruct((N, D_POOL), F32), jax.ShapeDtypeStruct((N, D_POOL), F32),
            jax.ShapeDtypeStruct((N, Q_LORA), F32), jax.ShapeDtypeStruct((N, KV_LORA), F32),
            jax.ShapeDtypeStruct((N, D_POOL), F32),
            jax.ShapeDtypeStruct((HEADS, N, 256), BF16), jax.ShapeDtypeStruct((HEADS, N, 256), BF16),
            jax.ShapeDtypeStruct((HEADS, N, V_HEAD), BF16),
        ],
        compiler_params=_cparams(dimension_semantics=("arbitrary",)),
    )(h, norm_g, win, gq, wq, gkv, wkv, cosf, sinf)


def _attn_fwd(q, k, v):
    tiles = _attn_tiles()

    hp = HEADS_PER_STEP

    def body(q_ref, k_ref, v_ref, o_ref, lse_ref):
        step = pl.program_id(1)
        for idx, (q0, rows, klen) in enumerate(tiles):
            @pl.when(step == idx)
            def _(q0=q0, rows=rows, klen=klen):
                for hd in range(hp):
                    s = _masked_scores(q_ref[hd, q0:q0 + rows, :], k_ref[hd, 0:klen, :], rows, klen)
                    m = jnp.max(s, axis=-1, keepdims=True)
                    p = jnp.exp(s - m)
                    l = jnp.sum(p, axis=-1, keepdims=True)
                    o_ref[q0:q0 + rows, hd * V_HEAD:(hd + 1) * V_HEAD] = _nn(p.astype(BF16), v_ref[hd, 0:klen, :]) / l
                    lse_ref[hd, q0:q0 + rows, :] = m + jnp.log(l)

    per_head = lambda w: pl.BlockSpec((hp, N, w), lambda g, t: (g, 0, 0), pipeline_mode=pl.Buffered(1))
    return pl.pallas_call(
        body,
        name="attn_fwd",
        grid=(HEADS // hp, len(tiles)),
        in_specs=[per_head(256), per_head(256), per_head(V_HEAD)],
        out_specs=[pl.BlockSpec((N, hp * V_HEAD), lambda g, t: (0, g), pipeline_mode=pl.Buffered(1)), per_head(1)],
        out_shape=[jax.ShapeDtypeStruct((N, HEADS * V_HEAD), F32), jax.ShapeDtypeStruct((HEADS, N, 1), F32)],
        compiler_params=_cparams(dimension_semantics=("arbitrary", "arbitrary")),
    )(q, k, v)


def _inv_count(row0, rows, w):
    row = row0 + lax.broadcasted_iota(jnp.int32, (rows, 1), 0)
    return 1.0 / jnp.clip(row - (PAD - 1), 1, w).astype(F32)


def _mid(h, tgt, pool_in, pool_gate, attn_gate, attn, pool_w, pool_scale, wout, gf):
    tr = ROWS_MID
    per = tr // HALO
    ng = len(POOL_WINDOWS)

    def body(h_ref, t_ref, pin_ref, halo_ref, pg_ref, ag_ref, at_ref, pw_ref, ps_ref, wout_ref, gf_ref,
             dh2_ref, do_ref, dag_ref, dpg_ref, dpl_ref, dwout_ref, dpw_ref, dps_ref, dgf_ref, loss_ref):
        i = pl.program_id(0)

        @pl.when(i == 0)
        def _():
            dwout_ref[...] = jnp.zeros_like(dwout_ref)
            dpw_ref[...] = jnp.zeros_like(dpw_ref)
            dps_ref[...] = jnp.zeros_like(dps_ref)
            dgf_ref[...] = jnp.zeros_like(dgf_ref)
            loss_ref[...] = jnp.zeros_like(loss_ref)

        row0 = i * tr
        real = (row0 + lax.broadcasted_iota(jnp.int32, (tr, 1), 0)) >= HEAD_ROWS
        h = h_ref[...]

        halo = jnp.where(i > 0, halo_ref[...], 0.0)
        ext = jnp.concatenate([halo, pin_ref[...]], axis=0)
        pooled = []
        for g, w in enumerate(POOL_WINDOWS):
            e = ext[:, g * POOL_GROUP:(g + 1) * POOL_GROUP]
            acc = e
            shift = 1
            while shift < w:
                acc = acc + pltpu.roll(acc, shift, 0)
                shift *= 2
            pooled.append((acc[HALO:] * _inv_count(row0, tr, w) - e[HALO:]).astype(BF16))
        pw = [pw_ref[g].astype(BF16) for g in range(ng)]
        mixed = jnp.concatenate([_nn(pooled[g], pw[g]) for g in range(ng)], axis=1)
        ps = ps_ref[...]
        mixed_s = mixed * ps
        pg = pg_ref[...]
        sig_p = _sigmoid(pg)
        silu_p = pg * sig_p
        pool_out = (silu_p * mixed_s).astype(BF16)
        ag = ag_ref[...]
        sig_a = _sigmoid(ag)
        silu_a = ag * sig_a
        at = at_ref[...]
        attn_out = (silu_a * at).astype(BF16)
        mix = _nn(pool_out, wout_ref[0:D_POOL, :]) + _nn(attn_out, wout_ref[D_POOL:D, :])
        h2 = h + mix

        r2 = lax.rsqrt(jnp.mean(h2 * h2, axis=-1, keepdims=True) + EPS)
        n2 = h2 * r2
        gfv = gf_ref[...]
        err = jnp.where(real, n2 * gfv - t_ref[...], 0.0)
        loss_ref[...] += jnp.sum(jnp.sum(err * err, axis=-1, keepdims=True), axis=0, keepdims=True) * (0.5 / D)
        dy = err * (1.0 / D)
        dgf_ref[...] += jnp.sum(dy * n2, axis=0, keepdims=True)
        dn = dy * gfv
        dh2 = r2 * (dn - n2 * jnp.mean(dn * n2, axis=-1, keepdims=True))
        dh2_ref[...] = dh2
        dh2b = dh2.astype(BF16)

        dwout_ref[0:D_POOL, :] += _tn(pool_out, dh2b)
        dwout_ref[D_POOL:D, :] += _tn(attn_out, dh2b)
        dcat = _nt(dh2b, wout_ref[...])
        dpo = dcat[:, 0:D_POOL]
        dao = dcat[:, D_POOL:D]
        do_ref[...] = dao * silu_a
        dag_ref[...] = dao * at * (sig_a * (1.0 + ag * (1.0 - sig_a)))
        dmixed_s = dpo * silu_p
        dpg_ref[...] = dpo * mixed_s * (sig_p * (1.0 + pg * (1.0 - sig_p)))
        dps_ref[...] += jnp.sum(dmixed_s * mixed, axis=0, keepdims=True)
        dmixed = (dmixed_s * ps).astype(BF16)
        dpl = []
        for g in range(ng):
            dm = dmixed[:, g * POOL_GROUP:(g + 1) * POOL_GROUP]
            dpl.append(_nt(dm, pw[g]))
            dpw_ref[g] += _tn(pooled[g], dm)
        dpl_ref[...] = jnp.concatenate(dpl, axis=1)

    halo_spec = pl.BlockSpec((HALO, D_POOL), lambda i: (jnp.maximum(i * per - 1, 0), 0))
    return pl.pallas_call(
        body,
        name="mid",
        grid=(N // tr,),
        in_specs=[
            _rows(D, tr), _rows(D, tr), _rows(D_POOL, tr), halo_spec, _rows(D_POOL, tr), _rows(D_POOL, tr),
            _rows(D_POOL, tr), _const(ng, POOL_GROUP, POOL_GROUP), _const(1, D_POOL), _const(D, D), _const(1, D),
        ],
        out_specs=[
            _rows(D, tr), _rows(D_POOL, tr), _rows(D_POOL, tr), _rows(D_POOL, tr), _rows(D_POOL, tr),
            _const(D, D), _const(ng, POOL_GROUP, POOL_GROUP), _const(1, D_POOL), _const(1, D), _const(1, 128),
        ],
        out_shape=[
            jax.ShapeDtypeStruct((N, D), F32), jax.ShapeDtypeStruct((N, D_POOL), F32),
            jax.ShapeDtypeStruct((N, D_POOL), F32), jax.ShapeDtypeStruct((N, D_POOL), F32),
            jax.ShapeDtypeStruct((N, D_POOL), F32), jax.ShapeDtypeStruct((D, D), F32),
            jax.ShapeDtypeStruct((ng, POOL_GROUP, POOL_GROUP), F32),
            jax.ShapeDtypeStruct((1, D_POOL), F32), jax.ShapeDtypeStruct((1, D), F32), jax.ShapeDtypeStruct((1, 128), F32),
        ],
        compiler_params=_cparams(dimension_semantics=("arbitrary",)),
    )(h, tgt, pool_in, pool_in, pool_gate, attn_gate, attn, pool_w, pool_scale, wout, gf)


def _attn_bwd(q, k, v, do, o, lse):
    tiles = _attn_tiles()
    hp = HEADS_PER_STEP_BWD

    def body(q_ref, k_ref, v_ref, do_ref, o_ref, lse_ref, dq_ref, dk_ref, dv_ref):
        step = pl.program_id(1)

        @pl.when(step == 0)
        def _():
            dk_ref[...] = jnp.zeros_like(dk_ref)
            dv_ref[...] = jnp.zeros_like(dv_ref)

        for idx, (q0, rows, klen) in enumerate(tiles):
            @pl.when(step == idx)
            def _(q0=q0, rows=rows, klen=klen):
                qs = pl.ds(q0, rows)
                for hd in range(hp):
                    vs = pl.ds(hd * V_HEAD, V_HEAD)
                    qv = q_ref[hd, qs, :]
                    kv = k_ref[hd, 0:klen, :]
                    p = jnp.exp(_masked_scores(qv, kv, rows, klen) - lse_ref[hd, qs, :])
                    dov = do_ref[qs, vs]
                    dob = dov.astype(BF16)
                    delta = jnp.sum(dov * o_ref[qs, vs], axis=-1, keepdims=True)
                    ds = (p * (_nt(dob, v_ref[hd, 0:klen, :]) - delta)).astype(BF16)
                    dq_ref[hd, qs, :] = _nn(ds, kv) * SCALE
                    dk_ref[hd, 0:klen, :] += _tn(ds, qv)
                    dv_ref[hd, 0:klen, :] += _tn(p.astype(BF16), dob)

    per_head = lambda w: pl.BlockSpec((hp, N, w), lambda g, t: (g, 0, 0), pipeline_mode=pl.Buffered(1))
    cols = pl.BlockSpec((N, hp * V_HEAD), lambda g, t: (0, g), pipeline_mode=pl.Buffered(1))
    return pl.pallas_call(
        body,
        name="attn_bwd",
        grid=(HEADS // hp, len(tiles)),
        in_specs=[per_head(256), per_head(256), per_head(V_HEAD), cols, cols, per_head(1)],
        out_specs=[per_head(256), per_head(256), per_head(V_HEAD)],
        out_shape=[
            jax.ShapeDtypeStruct((HEADS, N, 256), F32), jax.ShapeDtypeStruct((HEADS, N, 256), F32),
            jax.ShapeDtypeStruct((HEADS, N, V_HEAD), F32),
        ],
        compiler_params=_cparams(dimension_semantics=("arbitrary", "arbitrary")),
    )(q, k, v, do, o, lse)


def _bwd_in(h, dh2, dq, dk, dv, cq, ckv, dpl, dpg, dag, norm_g, win, gq, wq, gkv, wkv, cosf, sinf):
    tr = ROWS_BWD
    nb = N // tr
    per = tr // HALO
    lead = HEAD_ROWS

    def body(h_ref, dh2_ref, dq_ref, dk_ref, dv_ref, cq_ref, ckv_ref, dpl_ref, halo_ref, dpg_ref, dag_ref,
             g_ref, win_ref, gq_ref, wq_ref, gkv_ref, wkv_ref, cos_ref, sin_ref,
             gx_ref, dmeta_ref, dwin_ref, dwq_ref, dwkv_ref, dg_ref, dgq_ref, dgkv_ref, dh_buf, gx_sem):
        i = pl.program_id(0)

        @pl.when(i == 0)
        def _():
            dwin_ref[...] = jnp.zeros_like(dwin_ref)
            dwq_ref[...] = jnp.zeros_like(dwq_ref)
            dwkv_ref[...] = jnp.zeros_like(dwkv_ref)
            dg_ref[...] = jnp.zeros_like(dg_ref)
            dgq_ref[...] = jnp.zeros_like(dgq_ref)
            dgkv_ref[...] = jnp.zeros_like(dgkv_ref)

        row0 = i * tr
        h = h_ref[...]
        r = lax.rsqrt(jnp.mean(h * h, axis=-1, keepdims=True) + EPS)
        n = h * r
        gv = g_ref[...]
        hn = (n * gv).astype(BF16)
        cosv = cos_ref[...]
        sinv = sin_ref[...]
        low = _low_lanes()

        def unrope(dy):
            return dy * cosv + _swap64(dy * sinv) * low

        cq = cq_ref[...]
        rq = lax.rsqrt(jnp.mean(cq * cq, axis=-1, keepdims=True) + EPS)
        nq = cq * rq
        gqv = gq_ref[...]
        cqn = (nq * gqv).astype(BF16)
        dcqn = jnp.zeros((tr, Q_LORA), F32)
        for hd in range(HEADS):
            dqh = dq_ref[hd]
            dqf = jnp.concatenate([dqh[:, 0:QK_NOPE], unrope(dqh[:, QK_NOPE:])], axis=1).astype(BF16)
            dcqn = dcqn + _nn(dqf, wq_ref[hd])
            dwq_ref[hd] += _tn(dqf, cqn)
        dgq_ref[...] += jnp.sum(dcqn * nq, axis=0, keepdims=True)
        dnq = dcqn * gqv
        dcq = rq * (dnq - nq * jnp.mean(dnq * nq, axis=-1, keepdims=True))

        ckv = ckv_ref[...]
        rkv = lax.rsqrt(jnp.mean(ckv * ckv, axis=-1, keepdims=True) + EPS)
        nkv = ckv * rkv
        gkvv = gkv_ref[...]
        ckvn = (nkv * gkvv).astype(BF16)
        dckvn = jnp.zeros((tr, KV_LORA), F32)
        dkr = jnp.zeros((tr, 128), F32)
        for hd in range(HEADS):
            dkh = dk_ref[hd]
            dkr = dkr + dkh[:, QK_NOPE:]
            dkv = jnp.concatenate([dkh[:, 0:QK_NOPE], dv_ref[hd]], axis=1).astype(BF16)
            dckvn = dckvn + _nt(dkv, wkv_ref[hd])
            dwkv_ref[hd] += _tn(ckvn, dkv)
        dgkv_ref[...] += jnp.sum(dckvn * nkv, axis=0, keepdims=True)
        dnkv = dckvn * gkvv
        dckv = rkv * (dnkv - nkv * jnp.mean(dnkv * nkv, axis=-1, keepdims=True))
        dkr = unrope(dkr)

        cur = dpl_ref[...]
        halo = jnp.where(i < nb - 1, halo_ref[...], 0.0)
        dpi = []
        for g, w in enumerate(POOL_WINDOWS):
            sl = slice(g * POOL_GROUP, (g + 1) * POOL_GROUP)
            a = jnp.concatenate([cur[:, sl] * _inv_count(row0, tr, w), halo[:, sl] * _inv_count(row0 + tr, HALO, w)], axis=0)
            acc = a
            shift = 1
            while shift < w:
                acc = acc + pltpu.roll(acc, tr + HALO - shift, 0)
                shift *= 2
            dpi.append(acc[0:tr] - cur[:, sl])

        du = jnp.concatenate(dpi + [dpg_ref[...], dcq, dckv, dkr], axis=1).astype(BF16)
        dagb = dag_ref[...].astype(BF16)
        dwin_ref[0:O_KR_END, :] += _tn(du, hn)
        dwin_ref[O_AG:D_IN, :] += _tn(dagb, hn)
        dhn = _nn(du, win_ref[0:O_KR_END, :]) + _nn(dagb, win_ref[O_AG:D_IN, :])
        dg_ref[...] += jnp.sum(dhn * n, axis=0, keepdims=True)
        dn = dhn * gv
        dh = dh2_ref[...] + r * (dn - n * jnp.mean(dn * n, axis=-1, keepdims=True))

        first = pltpu.make_async_copy(dh_buf.at[pl.ds(lead, tr - lead), :], gx_ref.at[pl.ds(0, tr - lead), :], gx_sem)
        later = lambda step: pltpu.make_async_copy(
            dh_buf, gx_ref.at[pl.ds(pl.multiple_of(step * tr - lead, 16), tr), :], gx_sem)

        @pl.when(i == 1)
        def _():
            first.wait()

        @pl.when(i > 1)
        def _():
            later(i - 1).wait()

        dh_buf[...] = dh

        @pl.when(i == 0)
        def _():
            first.start()
            for chip in range(CHIPS):
                dmeta_ref[chip] = dh[PAD:HEAD_ROWS, chip * 256:(chip + 1) * 256]

        @pl.when(i > 0)
        def _():
            later(i).start()

        @pl.when(i == nb - 1)
        def _():
            later(i).wait()

    head = lambda w: pl.BlockSpec((HEADS, tr, w), lambda i: (0, i, 0))
    halo_spec = pl.BlockSpec((HALO, D_POOL), lambda i: (jnp.minimum((i + 1) * per, N // HALO - 1), 0))
    return pl.pallas_call(
        body,
        name="bwd_in",
        grid=(nb,),
        in_specs=[
            _rows(D, tr), _rows(D, tr), head(256), head(256), head(V_HEAD), _rows(Q_LORA, tr), _rows(KV_LORA, tr),
            _rows(D_POOL, tr), halo_spec, _rows(D_POOL, tr), _rows(D_POOL, tr),
            _const(1, D), _const(D_IN, D), _const(1, Q_LORA), _const(HEADS, 256, Q_LORA),
            _const(1, KV_LORA), _const(HEADS, KV_LORA, 256), _rows(128, tr), _rows(128, tr),
        ],
        out_specs=[
            pl.BlockSpec(memory_space=pl.ANY), _const(CHIPS, N_META, 256), _const(D_IN, D), _const(HEADS, 256, Q_LORA),
            _const(HEADS, KV_LORA, 256), _const(1, D), _const(1, Q_LORA), _const(1, KV_LORA),
        ],
        out_shape=[
            jax.ShapeDtypeStruct((S, D), F32), jax.ShapeDtypeStruct((CHIPS, N_META, 256), F32),
            jax.ShapeDtypeStruct((D_IN, D), F32), jax.ShapeDtypeStruct((HEADS, 256, Q_LORA), F32),
            jax.ShapeDtypeStruct((HEADS, KV_LORA, 256), F32),
            jax.ShapeDtypeStruct((1, D), F32), jax.ShapeDtypeStruct((1, Q_LORA), F32), jax.ShapeDtypeStruct((1, KV_LORA), F32),
        ],
        scratch_shapes=[pltpu.VMEM((tr, D), F32), pltpu.SemaphoreType.DMA],
        compiler_params=_cparams(dimension_semantics=("arbitrary",)),
    )(h, dh2, dq, dk, dv, cq, ckv, dpl, dpl, dpg, dag, norm_g, win, gq, wq, gkv, wkv, cosf, sinf)


def _local_step(h, tgt, norm_g, win, gq, wq, gkv, wkv, pool_w, pool_scale, wout, gf, cosf, sinf):
    pool_in, pool_gate, cq, ckv, attn_gate, q, k, v = _fwd_in(h, norm_g, win, gq, wq, gkv, wkv, cosf, sinf)
    attn, lse = _attn_fwd(q, k, v)
    dh2, do, dag, dpg, dpl, dwout, dpw, dps, dgf, loss = _mid(
        h, tgt, pool_in, pool_gate, attn_gate, attn, pool_w, pool_scale, wout, gf)
    dq, dk, dv = _attn_bwd(q, k, v, do, attn, lse)
    gx, dmeta, dwin, dwq, dwkv, dg, dgq, dgkv = _bwd_in(
        h, dh2, dq, dk, dv, cq, ckv, dpl, dpg, dag, norm_g, win, gq, wq, gkv, wkv, cosf, sinf)
    return dict(gx=gx, dmeta=dmeta, dwin=dwin, dwq=dwq, dwkv=dwkv, dwout=dwout, dg=dg, dgq=dgq, dgkv=dgkv,
                dpw=dpw, dps=dps, dgf=dgf, loss=loss)


_CHIP_RELS = ((0, 0), (1, 0), (0, 1), (1, 1))

_ARR_ROWS = (SHARD_IN, SHARD_OUT, 256, KV_LORA, N_META)
_ARR_COLS = (D, D, Q_LORA, 256, 256)
_PIECES = (
    (0, 0, 256, 0), (0, 256, SHARD_IN - 256, 1),
    (1, 0, 128, 0), (1, 128, 128, 1),
    (2, 0, 128, 0), (2, 128, 128, 1),
    (3, 0, 64, 0), (3, 64, 64, 1),
    (4, 0, N_META, 0),
)
_NP = len(_PIECES)
_PIECE_MAX = (256, 128, 128, 64, N_META)


def _gathered_at(refs, arr, chip, r0, n):
    if arr in (0, 1):
        return refs[arr].at[pl.ds(pl.multiple_of(_ARR_ROWS[arr] * chip + r0, 16), n), :]
    return refs[arr].at[chip, pl.ds(r0, n), :]


def _remote(src, dst, send_sem, recv_sem, to):
    return pltpu.make_async_remote_copy(src_ref=src, dst_ref=dst, send_sem=send_sem, recv_sem=recv_sem,
                                        device_id=to, device_id_type=MESH)


def _gather_weights(winT_s, wout_s, wqT_s, wkv_s, meta_s, x2, tgt2):
    def body(win_ref, wout_ref, wq_ref, wkv_ref, meta_ref, x_ref, t_ref, win_o, wout_o, wq_o, wkv_o, h_o, tp_o,
             s_win, s_wout, s_wq, s_wkv, meta_all, head_buf, x_buf, t_buf, ici_send, ici_recv, fwd_send, fwd_recv,
             loc_sems):
        x, y, c = lax.axis_index("x"), lax.axis_index("y"), lax.axis_index("c")
        me = 2 * x + y
        stage = (s_win, s_wout, s_wq, s_wkv, meta_ref)
        outs = (win_o, wout_o, wq_o, wkv_o, meta_all)

        frames = pl.ds(HEAD_ROWS, S)
        loads = [pltpu.make_async_copy(x_ref, x_buf, loc_sems.at[0]), pltpu.make_async_copy(t_ref, t_buf, loc_sems.at[1])]
        local = [pltpu.make_async_copy(x_buf, h_o.at[frames, :], loc_sems.at[0]),
                 pltpu.make_async_copy(t_buf, tp_o.at[frames, :], loc_sems.at[1])]
        for cp in loads:
            cp.start()

        s_win[...] = win_ref[...].astype(BF16)
        s_wout[...] = wout_ref[...].astype(BF16)
        s_wq[0:QK, :] = wq_ref[...].astype(BF16)
        s_wq[QK:256, :] = jnp.zeros((256 - QK, Q_LORA), BF16)
        s_wkv[...] = wkv_ref[...].astype(BF16)

        def chip_of(rel):
            fx, fy = _CHIP_RELS[rel]
            return 2 * (x ^ fx) + (y ^ fy)

        def same_core_of(rel):
            fx, fy = _CHIP_RELS[rel]
            return (x ^ fx, y ^ fy, c)

        def ici_copy(rel, i, src_chip, to):
            arr, r0, n, _ = _PIECES[i]
            k = (rel - 1) * _NP + i
            return _remote(stage[arr].at[pl.ds(r0, n), :], _gathered_at(outs, arr, src_chip, r0, n),
                           ici_send.at[k], ici_recv.at[k], to)

        def fwd_copy(rel, i, to):
            arr, r0, n, _ = _PIECES[i]
            k = (rel - 1) * _NP + i
            place = _gathered_at(outs, arr, chip_of(rel), r0, n)
            return _remote(place, place, fwd_send.at[k], fwd_recv.at[k], to)

        for core in (0, 1):
            @pl.when(c == core)
            def _(core=core):
                mine = [i for i in range(_NP) if _PIECES[i][3] == core]
                theirs = [i for i in range(_NP) if _PIECES[i][3] != core]
                sends = [ici_copy(rel, i, me, same_core_of(rel)) for rel in (1, 2, 3) for i in mine]
                for cp in sends:
                    cp.start()
                for ld, st in zip(loads, local):
                    ld.wait()
                    st.start()
                for arr in range(5):
                    _gathered_at(outs, arr, me, 0, _ARR_ROWS[arr])[...] = stage[arr][...]
                for rel in (1, 2, 3):
                    for i in mine:
                        ici_copy(rel, i, chip_of(rel), (x, y, c)).wait_recv()
                        fwd = fwd_copy(rel, i, (x, y, 1 - c))
                        fwd.start()
                        sends.append(fwd)
                for rel in (1, 2, 3):
                    for i in theirs:
                        fwd_copy(rel, i, (x, y, c)).wait_recv()
                for cp in sends:
                    cp.wait_send()

        head_buf[...] = jnp.zeros_like(head_buf)
        zeros = pltpu.make_async_copy(head_buf, tp_o.at[pl.ds(0, HEAD_ROWS), :], loc_sems.at[2])
        zeros.start()
        zeros.wait()
        for chip in range(CHIPS):
            head_buf[PAD:HEAD_ROWS, chip * 256:(chip + 1) * 256] = meta_all[chip]
        head = pltpu.make_async_copy(head_buf, h_o.at[pl.ds(0, HEAD_ROWS), :], loc_sems.at[2])
        head.start()
        head.wait()
        for cp in local:
            cp.wait()

    vm = pl.BlockSpec(memory_space=pltpu.VMEM)
    hbm = pl.BlockSpec(memory_space=pl.ANY)
    return pl.pallas_call(
        body,
        name="gather_weights",
        in_specs=[vm] * 5 + [hbm] * 2,
        out_specs=[vm] * 4 + [hbm] * 2,
        out_shape=[
            jax.ShapeDtypeStruct((D_IN, D), BF16), jax.ShapeDtypeStruct((D, D), BF16),
            jax.ShapeDtypeStruct((CHIPS, 256, Q_LORA), BF16), jax.ShapeDtypeStruct((CHIPS, KV_LORA, 256), BF16),
            jax.ShapeDtypeStruct((N, D), F32), jax.ShapeDtypeStruct((N, D), F32),
        ],
        scratch_shapes=[pltpu.VMEM((_ARR_ROWS[a], _ARR_COLS[a]), BF16) for a in range(4)]
        + [pltpu.VMEM((CHIPS, N_META, 256), F32), pltpu.VMEM((HEAD_ROWS, D), F32), pltpu.VMEM((S, D), F32),
           pltpu.VMEM((S, D), F32)]
        + [pltpu.SemaphoreType.DMA((3 * _NP,))] * 4 + [pltpu.SemaphoreType.DMA((3,))],
        compiler_params=_cparams(),
    )(winT_s, wout_s, wqT_s, wkv_s, meta_s, x2, tgt2)


_SM_ROWS = (len(POOL_WINDOWS) * POOL_GROUP, VEC_ROWS)
_SM_COLS = (POOL_GROUP, D)
_SM_PIECES = ((0, 0, 256, 0), (0, 256, 256, 1), (1, 0, VEC_ROWS, 0))
_NSP = len(_SM_PIECES)


def _reduce_grads(dwin, dwout, dwq, dwkv, dmeta4, dpw, dg, dgf, dgq, dgkv, dps, loss):
    def body(dwin_ref, dwout_ref, dwq_ref, dwkv_ref, dmeta_ref, dpw_ref, dg_ref, dgf_ref, dgq_ref, dgkv_ref, dps_ref,
             loss_ref, gwin_o, gwout_o, gwq_o, gwkv_o, gmeta_o, gpw_o, gg_o, ggf_o, ggq_o, ggkv_o, gps_o, gloss_o,
             sb0, sb1, sb2, sb3, sb4, st0, st1, st2, st3, st4, rc0, rc1, rc2, rc3, rc4,
             vec, sm_sb0, sm_sb1, sm_cs0, sm_cs1, sm_rc0, sm_rc1, vec_fin,
             d2d_send, d2d_recv, ici_send, ici_recv, fin_send, fin_recv,
             swap_send, swap_recv, smi_send, smi_recv, smf_send, smf_recv):
        x, y, c = lax.axis_index("x"), lax.axis_index("y"), lax.axis_index("c")
        me = 2 * x + y
        grads = (dwin_ref, dwout_ref, dwq_ref, dwkv_ref, dmeta_ref)
        outs = (gwin_o, gwout_o, gwq_o, gwkv_o, gmeta_o)
        sib_buf = (sb0, sb1, sb2, sb3, sb4)
        stage = (st0, st1, st2, st3, st4)
        recv = (rc0, rc1, rc2, rc3, rc4)
        sm_mine = (dpw_ref, vec)
        sm_sib = (sm_sb0, sm_sb1)
        sm_chip = (sm_cs0, sm_cs1)
        sm_recv = (sm_rc0, sm_rc1)
        sm_out = (gpw_o, vec_fin)
        sibling = (x, y, 1 - c)

        def chip_of(rel):
            fx, fy = _CHIP_RELS[rel]
            return 2 * (x ^ fx) + (y ^ fy)

        def same_core_of(rel):
            fx, fy = _CHIP_RELS[rel]
            return (x ^ fx, y ^ fy, c)

        def slot(bufs, i, idx):
            arr, _, n, _ = _PIECES[i]
            return bufs[arr].at[idx, pl.ds(0, n), :]

        def d2d_copy(rel, i):
            arr, r0, n, _ = _PIECES[i]
            k = rel * _NP + i
            return _remote(_gathered_at(grads, arr, chip_of(rel), r0, n), slot(sib_buf, i, rel),
                           d2d_send.at[k], d2d_recv.at[k], sibling)

        def ici_copy(rel, i):
            k = (rel - 1) * _NP + i
            return _remote(slot(stage, i, rel - 1), slot(recv, i, rel - 1), ici_send.at[k], ici_recv.at[k],
                           same_core_of(rel))

        def fin_copy(i):
            arr, r0, n, _ = _PIECES[i]
            place = outs[arr].at[pl.ds(r0, n), :]
            return _remote(place, place, fin_send.at[i], fin_recv.at[i], sibling)

        def sm_ici_copy(rel, j):
            blk, r0, n, _ = _SM_PIECES[j]
            k = (rel - 1) * _NSP + j
            return _remote(sm_chip[blk].at[pl.ds(r0, n), :], sm_recv[blk].at[rel - 1, pl.ds(r0, n), :],
                           smi_send.at[k], smi_recv.at[k], same_core_of(rel))

        def sm_fin_copy(j):
            blk, r0, n, _ = _SM_PIECES[j]
            place = sm_out[blk].at[pl.ds(r0, n), :]
            return _remote(place, place, smf_send.at[j], smf_recv.at[j], sibling)

        vec[...] = jnp.zeros_like(vec)
        vec[0:1, :] = dg_ref[...]
        vec[1:2, :] = dgf_ref[...]
        vec[2:3, V_GQ:V_GQ + Q_LORA] = dgq_ref[...]
        vec[2:3, V_GKV:V_GKV + KV_LORA] = dgkv_ref[...]
        vec[2:3, V_PS:V_PS + D_POOL] = dps_ref[...]
        vec[2:3, V_LOSS:D] = loss_ref[...]
        swaps = [_remote(sm_mine[b], sm_sib[b], swap_send.at[b], swap_recv.at[b], sibling) for b in (0, 1)]
        for cp in swaps:
            cp.start()

        for core in (0, 1):
            @pl.when(c == core)
            def _(core=core):
                mine = [i for i in range(_NP) if _PIECES[i][3] == core]
                theirs = [i for i in range(_NP) if _PIECES[i][3] != core]
                sm_mine_p = [j for j in range(_NSP) if _SM_PIECES[j][3] == core]
                sm_theirs_p = [j for j in range(_NSP) if _SM_PIECES[j][3] != core]
                sends = list(swaps)

                for rel in (1, 2, 3, 0):
                    for i in theirs:
                        cp = d2d_copy(rel, i)
                        cp.start()
                        sends.append(cp)

                for b in (0, 1):
                    swaps[b].wait_recv()
                    sm_chip[b][...] = sm_mine[b][...] + sm_sib[b][...]
                for rel in (1, 2, 3):
                    for j in sm_mine_p:
                        cp = sm_ici_copy(rel, j)
                        cp.start()
                        sends.append(cp)

                for rel in (1, 2, 3):
                    for i in mine:
                        arr, r0, n, _ = _PIECES[i]
                        d2d_copy(rel, i).wait_recv()
                        total = _gathered_at(grads, arr, chip_of(rel), r0, n)[...] + slot(sib_buf, i, rel)[...]
                        slot(stage, i, rel - 1)[...] = total.astype(stage[arr].dtype)
                        cp = ici_copy(rel, i)
                        cp.start()
                        sends.append(cp)

                for i in mine:
                    arr, r0, n, _ = _PIECES[i]
                    d2d_copy(0, i).wait_recv()
                    total = _gathered_at(grads, arr, me, r0, n)[...] + slot(sib_buf, i, 0)[...]
                    for rel in (1, 2, 3):
                        ici_copy(rel, i).wait_recv()
                        total = total + slot(recv, i, rel - 1)[...].astype(F32)
                    outs[arr][pl.ds(r0, n), :] = total
                    cp = fin_copy(i)
                    cp.start()
                    sends.append(cp)

                for j in sm_mine_p:
                    blk, r0, n, _ = _SM_PIECES[j]
                    for rel in (1, 2, 3):
                        sm_ici_copy(rel, j).wait_recv()
                    total = jnp.zeros((n, _SM_COLS[blk]), F32)
                    for chip in range(CHIPS):
                        flips = chip ^ me
                        rel = jnp.where(flips == 2, 1, jnp.where(flips == 1, 2, flips))
                        theirs_rows = sm_recv[blk][jnp.maximum(rel - 1, 0), pl.ds(r0, n), :]
                        total = total + jnp.where(rel == 0, sm_chip[blk][pl.ds(r0, n), :], theirs_rows)
                    sm_out[blk][pl.ds(r0, n), :] = total
                    cp = sm_fin_copy(j)
                    cp.start()
                    sends.append(cp)

                for i in theirs:
                    fin_copy(i).wait_recv()
                for j in sm_theirs_p:
                    sm_fin_copy(j).wait_recv()
                for cp in sends:
                    cp.wait_send()

        gg_o[...] = vec_fin[0:1, :]
        ggf_o[...] = vec_fin[1:2, :]
        ggq_o[...] = vec_fin[2:3, V_GQ:V_GQ + Q_LORA]
        ggkv_o[...] = vec_fin[2:3, V_GKV:V_GKV + KV_LORA]
        gps_o[...] = vec_fin[2:3, V_PS:V_PS + D_POOL]
        gloss_o[...] = vec_fin[2:3, V_LOSS:D]

    vm = pl.BlockSpec(memory_space=pltpu.VMEM)
    piece_buf = lambda lead, dtype: [pltpu.VMEM((lead, _PIECE_MAX[a], _ARR_COLS[a]), F32 if a == 4 else dtype)
                                     for a in range(5)]
    sm_buf = lambda *lead: [pltpu.VMEM(lead + (_SM_ROWS[b], _SM_COLS[b]), F32) for b in (0, 1)]
    dma = lambda n: [pltpu.SemaphoreType.DMA((n,))] * 2
    return pl.pallas_call(
        body,
        name="reduce_grads",
        in_specs=[vm] * 12,
        out_specs=[vm] * 12,
        out_shape=[jax.ShapeDtypeStruct((_ARR_ROWS[a], _ARR_COLS[a]), F32) for a in range(5)]
        + [jax.ShapeDtypeStruct((_SM_ROWS[0], _SM_COLS[0]), F32), jax.ShapeDtypeStruct((1, D), F32),
           jax.ShapeDtypeStruct((1, D), F32), jax.ShapeDtypeStruct((1, Q_LORA), F32),
           jax.ShapeDtypeStruct((1, KV_LORA), F32), jax.ShapeDtypeStruct((1, D_POOL), F32),
           jax.ShapeDtypeStruct((1, 128), F32)],
        scratch_shapes=piece_buf(CHIPS, F32) + piece_buf(3, BF16) + piece_buf(3, BF16)
        + [pltpu.VMEM((VEC_ROWS, D), F32)] + sm_buf() + sm_buf() + sm_buf(3) + [pltpu.VMEM((VEC_ROWS, D), F32)]
        + dma(CHIPS * _NP) + dma(3 * _NP) + dma(_NP) + dma(2) + dma(3 * _NSP) + dma(_NSP),
        compiler_params=_cparams(),
    )(dwin, dwout, dwq, dwkv, dmeta4, dpw, dg, dgf, dgq, dgkv, dps, loss)


def _adamw_math(w, g, m, v):
    m = B1 * m + (1.0 - B1) * g
    v = B2 * v + (1.0 - B2) * (g * g)
    m_hat = m / C1
    v_hat = v / C2
    delta = -LR * (m_hat / (jnp.sqrt(v_hat) + ADAM_EPS) + WD * w)
    return delta, m, v


def _adamw_rows(name, w, g, m, v, block_rows):
    rows, cols = w.shape

    def body(w_ref, g_ref, m_ref, v_ref, go_ref, d_ref, nm_ref, nv_ref):
        g = g_ref[...]
        go_ref[...] = g
        d_ref[...], nm_ref[...], nv_ref[...] = _adamw_math(w_ref[...], g, m_ref[...], v_ref[...])

    spec = pl.BlockSpec((block_rows, cols), lambda i: (i, 0))
    return pl.pallas_call(
        body,
        name=name,
        grid=(rows // block_rows,),
        in_specs=[spec] * 4,
        out_specs=[spec] * 4,
        out_shape=[jax.ShapeDtypeStruct(w.shape, F32)] * 4,
        compiler_params=_cparams(dimension_semantics=("arbitrary",)),
    )(w, g, m, v)


def _adamw_small(groups):
    n = len(groups)

    def body(*refs):
        ins, outs = refs[:4 * n], refs[4 * n:]
        for t in range(n):
            w_ref, g_ref, m_ref, v_ref = ins[4 * t:4 * t + 4]
            g = g_ref[0:w_ref.shape[0], :]
            outs[4 * t][...] = g
            outs[4 * t + 1][...], outs[4 * t + 2][...], outs[4 * t + 3][...] = _adamw_math(
                w_ref[...], g, m_ref[...], v_ref[...])

    vm = pl.BlockSpec(memory_space=pltpu.VMEM)
    flat = [a for grp in groups for a in grp]
    outs = pl.pallas_call(
        body,
        name="adamw_small",
        in_specs=[vm] * (4 * n),
        out_specs=[vm] * (4 * n),
        out_shape=[jax.ShapeDtypeStruct(grp[0].shape, F32) for grp in groups for _ in range(4)],
        compiler_params=_cparams(),
    )(*flat)
    return [tuple(outs[4 * t:4 * t + 4]) for t in range(n)]


def _rope_tables():
    half = QK_ROPE // 2
    f32 = np.float32
    inv_freq = (f32(1.0) / (f32(ROPE_THETA) ** (np.arange(half, dtype=f32) / f32(half)))).astype(f32)
    pos = np.arange(N, dtype=f32) - f32(PAD)
    ang = (pos[:, None] * inv_freq[None, :]).astype(f32)
    cos, sin = np.cos(ang).astype(f32), np.sin(ang).astype(f32)
    zero = np.zeros((N, 128 - QK_ROPE), f32)
    return jnp.asarray(np.concatenate([cos, cos, zero], axis=1)), jnp.asarray(np.concatenate([-sin, sin, zero], axis=1))


def kernel(x, meta_tokens, norm_g, w_in, q_norm_g, w_q_b, kv_norm_g, w_kv_b, pool_w, pool_scale, w_out, final_norm_g, loss_target, m_meta_tokens, m_norm_g, m_w_in, m_q_norm_g, m_w_q_b, m_kv_norm_g, m_w_kv_b, m_pool_w, m_pool_scale, m_w_out, m_final_norm_g, v_meta_tokens, v_norm_g, v_w_in, v_q_norm_g, v_w_q_b, v_kv_norm_g, v_w_kv_b, v_pool_w, v_pool_scale, v_w_out, v_final_norm_g):
    tr = lambda a: a[0].T
    win, wout, wq, wkv, h, tgt = _gather_weights(tr(w_in), w_out[0], tr(w_q_b), w_kv_b[0], meta_tokens, x[0], loss_target[0])
    cosf, sinf = _rope_tables()
    gf = final_norm_g.reshape(1, D)

    part = _local_step(h, tgt, norm_g, win, q_norm_g, wq, kv_norm_g, wkv, pool_w[0], pool_scale, wout, gf, cosf, sinf)

    pw2 = lambda a: a.reshape(len(POOL_WINDOWS) * POOL_GROUP, POOL_GROUP)
    gwinT, gwout, gwqT, gwkv, gmeta, gpw, gg, ggf, ggq, ggkv, gps, gloss = _reduce_grads(
        part["dwin"], part["dwout"], part["dwq"], part["dwkv"], part["dmeta"], pw2(part["dpw"]), part["dg"],
        part["dgf"], part["dgq"], part["dgkv"], part["dps"], part["loss"])

    r_in = _adamw_rows("adamw_w_in", tr(w_in), gwinT, tr(m_w_in), tr(v_w_in), 248)
    r_out = _adamw_rows("adamw_w_out", w_out[0], gwout, m_w_out[0], v_w_out[0], 128)
    fn2 = lambda a: a.reshape(1, D)
    r_meta, r_norm, r_gq, r_wq, r_gkv, r_wkv, r_pw, r_ps, r_fn = _adamw_small([
        (meta_tokens, gmeta, m_meta_tokens, v_meta_tokens),
        (norm_g, gg, m_norm_g, v_norm_g),
        (q_norm_g, ggq, m_q_norm_g, v_q_norm_g),
        (tr(w_q_b), gwqT, tr(m_w_q_b), tr(v_w_q_b)),
        (kv_norm_g, ggkv, m_kv_norm_g, v_kv_norm_g),
        (w_kv_b[0], gwkv, m_w_kv_b[0], v_w_kv_b[0]),
        (pw2(pool_w), gpw, pw2(m_pool_w), pw2(v_pool_w)),
        (pool_scale, gps, m_pool_scale, v_pool_scale),
        (fn2(final_norm_g), ggf, fn2(m_final_norm_g), fn2(v_final_norm_g)),
    ])
    untr = lambda a: a.T[None]
    pw4 = lambda a: a.reshape(1, len(POOL_WINDOWS), POOL_GROUP, POOL_GROUP)
    per_kind = [[
        r_meta[kind], r_norm[kind], untr(r_in[kind]), r_gq[kind], untr(r_wq[kind]), r_gkv[kind], r_wkv[kind][None],
        pw4(r_pw[kind]), r_ps[kind], r_out[kind][None], r_fn[kind].reshape(D),
    ] for kind in range(4)]
    return (gloss[0, 0], part["gx"][None], *per_kind[0], *per_kind[1], *per_kind[2], *per_kind[3])
```

```python
import jax
import jax.numpy as jnp
import numpy as np
from jax import lax
from jax.experimental import pallas as pl
from jax.experimental.pallas import tpu as pltpu

F32 = jnp.float32
BF16 = jnp.bfloat16

D = 1024
S = 2048
N_META = 16
PAD = 112
HEAD_ROWS = PAD + N_META
N = HEAD_ROWS + S
D_POOL = 512
POOL_WINDOWS = (2, 4, 8, 16)
POOL_GROUP = 128
HALO = 16
HEADS = 4
QK_NOPE = 128
QK_ROPE = 64
QK = QK_NOPE + QK_ROPE
V_HEAD = 128
Q_LORA = 256
KV_LORA = 128
D_IN = 1984
EPS = 1e-6
ROPE_THETA = 10000.0
SCALE = QK ** -0.5
CHIPS = 4

ROWS_FWD = 544
ROWS_MID = 544
ROWS_BWD = 544
TK = 128
TQ = 256
NQ = S // TQ
HEADS_PER_STEP = 4
HEADS_PER_STEP_BWD = 2

O_PI, O_PG, O_CQ, O_CKV, O_KR, O_AG = 0, 512, 1024, 1280, 1408, 1472
O_KR_END = O_KR + 128
SHARD_IN = D_IN // CHIPS
SHARD_OUT = D // CHIPS

LR, B1, B2, ADAM_EPS, WD, STEP = 0.001, 0.9, 0.999, 1e-08, 0.01, 10
C1 = 1.0 - B1**STEP
C2 = 1.0 - B2**STEP

VMEM_LIMIT = 60 * 1024 * 1024
MESH = pl.DeviceIdType.MESH
NEG = -1e30

VEC_ROWS = 8
V_GQ, V_GKV, V_PS, V_LOSS = 0, 256, 384, 896


def _cparams(**kw):
    return pltpu.CompilerParams(vmem_limit_bytes=VMEM_LIMIT, **kw)


def _nt(a, b):
    return lax.dot_general(a, b, (((1,), (1,)), ((), ())), preferred_element_type=F32)


def _tn(a, b):
    return lax.dot_general(a, b, (((0,), (0,)), ((), ())), preferred_element_type=F32)


def _nn(a, b):
    return jnp.dot(a, b, preferred_element_type=F32)


def _swap64(t):
    return pltpu.roll(t, 32, 1) + pltpu.roll(t, 96, 1)


def _sigmoid(x):
    return 1.0 / (1.0 + jnp.exp(-x))


def _low_lanes():
    return (lax.broadcasted_iota(jnp.int32, (1, 128), 1) < QK_ROPE).astype(F32)


def _rows(w, rows):
    return pl.BlockSpec((rows, w), lambda i: (i, 0))


def _const(*shape):
    return pl.BlockSpec(shape, lambda *_: (0,) * len(shape), pipeline_mode=pl.Buffered(1))


STAT_GROUPS = HEADS // HEADS_PER_STEP_BWD


def _stat_slot(head):
    return head // HEADS_PER_STEP_BWD, head % HEADS_PER_STEP_BWD


def _attn_tiles():
    return [(0, TK, TK)] + [(TK + TQ * t, TQ, TK + TQ * (t + 1)) for t in range(NQ)]


def _masked_scores(q, k, rows, klen):
    s = _nt(q, k)
    col = lax.broadcasted_iota(jnp.int32, (1, TK), 1)
    head_bias = jnp.where(col >= PAD, 0.0, NEG)
    if klen == TK:
        return s + head_bias
    r = lax.broadcasted_iota(jnp.int32, (rows, 1), 0) >> 6
    c = lax.broadcasted_iota(jnp.int32, (1, rows), 1) >> 6
    diag_bias = jnp.where(c <= r, 0.0, NEG)
    parts = [s[:, 0:TK] + head_bias]
    if klen - rows > TK:
        parts.append(s[:, TK:klen - rows])
    parts.append(s[:, klen - rows:klen] + diag_bias)
    return jnp.concatenate(parts, axis=1)


def _fwd_in(h, norm_g, win, gq, wq, gkv, wkv, cosf, sinf):
    tr = ROWS_FWD

    def body(h_ref, g_ref, win_ref, gq_ref, wq_ref, gkv_ref, wkv_ref, cos_ref, sin_ref,
             pi_ref, pg_ref, cq_ref, ckv_ref, ag_ref, q_ref, k_ref, v_ref):
        h = h_ref[...]
        r = lax.rsqrt(jnp.mean(h * h, axis=-1, keepdims=True) + EPS)
        hn = ((h * r) * g_ref[...]).astype(BF16)
        u = _nt(hn, win_ref[0:O_KR_END, :])
        pi_ref[...] = u[:, O_PI:O_PG]
        pg_ref[...] = u[:, O_PG:O_CQ]
        cq = u[:, O_CQ:O_CKV]
        ckv = u[:, O_CKV:O_KR]
        cq_ref[...] = cq
        ckv_ref[...] = ckv
        ag_ref[...] = _nt(hn, win_ref[O_AG:D_IN, :])
        cosv = cos_ref[...]
        sinv = sin_ref[...]
        kr = u[:, O_KR:O_KR_END] * _low_lanes()
        kr = (kr * cosv + _swap64(kr) * sinv).astype(BF16)
        rq = lax.rsqrt(jnp.mean(cq * cq, axis=-1, keepdims=True) + EPS)
        cqn = ((cq * rq) * gq_ref[...]).astype(BF16)
        rkv = lax.rsqrt(jnp.mean(ckv * ckv, axis=-1, keepdims=True) + EPS)
        ckvn = ((ckv * rkv) * gkv_ref[...]).astype(BF16)
        for hd in range(HEADS):
            qh = _nt(cqn, wq_ref[hd]) * SCALE
            z = qh[:, QK_NOPE:]
            q_ref[hd, :, 0:QK_NOPE] = qh[:, 0:QK_NOPE].astype(BF16)
            q_ref[hd, :, QK_NOPE:] = (z * cosv + _swap64(z) * sinv).astype(BF16)
            kvh = _nn(ckvn, wkv_ref[hd])
            k_ref[hd, :, 0:QK_NOPE] = kvh[:, 0:QK_NOPE].astype(BF16)
            k_ref[hd, :, QK_NOPE:] = kr
            v_ref[hd] = kvh[:, QK_NOPE:].astype(BF16)

    head = lambda w: pl.BlockSpec((HEADS, tr, w), lambda i: (0, i, 0))
    return pl.pallas_call(
        body,
        name="fwd_in",
        grid=(N // tr,),
        in_specs=[
            _rows(D, tr), _const(1, D), _const(D_IN, D), _const(1, Q_LORA), _const(HEADS, 256, Q_LORA),
            _const(1, KV_LORA), _const(HEADS, KV_LORA, 256), _rows(128, tr), _rows(128, tr),
        ],
        out_specs=[_rows(D_POOL, tr), _rows(D_POOL, tr), _rows(Q_LORA, tr), _rows(KV_LORA, tr), _rows(D_POOL, tr),
                   head(256), head(256), head(V_HEAD)],
        out_shape=[
            jax.ShapeDtypeStruct((N, D_POOL), F32), jax.ShapeDtypeStruct((N, D_POOL), F32),
            jax.ShapeDtypeStruct((N, Q_LORA), F32), jax.ShapeDtypeStruct((N, KV_LORA), F32),
            jax.ShapeDtypeStruct((N, D_POOL), F32),
            jax.ShapeDtypeStruct((HEADS, N, 256), BF16), jax.ShapeDtypeStruct((HEADS, N, 256), BF16),
            jax.ShapeDtypeStruct((HEADS, N, V_HEAD), BF16),
        ],
        compiler_params=_cparams(dimension_semantics=("arbitrary",)),
    )(h, norm_g, win, gq, wq, gkv, wkv, cosf, sinf)


def _attn_fwd(q, k, v):
    tiles = _attn_tiles()

    hp = HEADS_PER_STEP

    def body(q_ref, k_ref, v_ref, o_ref, lse_ref):
        step = pl.program_id(1)

        @pl.when(step == 0)
        def _():
            lse_ref[...] = jnp.zeros_like(lse_ref)

        for idx, (q0, rows, klen) in enumerate(tiles):
            @pl.when(step == idx)
            def _(q0=q0, rows=rows, klen=klen):
                for hd in range(hp):
                    s = _masked_scores(q_ref[hd, q0:q0 + rows, :], k_ref[hd, 0:klen, :], rows, klen)
                    m = jnp.max(s, axis=-1, keepdims=True)
                    p = jnp.exp(s - m)
                    l = jnp.sum(p, axis=-1, keepdims=True)
                    o_ref[q0:q0 + rows, hd * V_HEAD:(hd + 1) * V_HEAD] = _nn(p.astype(BF16), v_ref[hd, 0:klen, :]) / l
                    grp, lane = _stat_slot(hd)
                    lse_ref[grp, q0:q0 + rows, lane:lane + 1] = m + jnp.log(l)

    assert hp == HEADS
    per_head = lambda w: pl.BlockSpec((hp, N, w), lambda g, t: (g, 0, 0), pipeline_mode=pl.Buffered(1))
    return pl.pallas_call(
        body,
        name="attn_fwd",
        grid=(HEADS // hp, len(tiles)),
        in_specs=[per_head(256), per_head(256), per_head(V_HEAD)],
        out_specs=[pl.BlockSpec((N, hp * V_HEAD), lambda g, t: (0, g), pipeline_mode=pl.Buffered(1)),
                   _const(STAT_GROUPS, N, 128)],
        out_shape=[jax.ShapeDtypeStruct((N, HEADS * V_HEAD), F32), jax.ShapeDtypeStruct((STAT_GROUPS, N, 128), F32)],
        compiler_params=_cparams(dimension_semantics=("arbitrary", "arbitrary")),
    )(q, k, v)


def _inv_count(row0, rows, w):
    row = row0 + lax.broadcasted_iota(jnp.int32, (rows, 1), 0)
    return 1.0 / jnp.clip(row - (PAD - 1), 1, w).astype(F32)


def _mid(h, tgt, pool_in, pool_gate, attn_gate, attn, pool_w, pool_scale, wout, gf):
    tr = ROWS_MID
    per = tr // HALO
    ng = len(POOL_WINDOWS)

    def body(h_ref, t_ref, pin_ref, halo_ref, pg_ref, ag_ref, at_ref, pw_ref, ps_ref, wout_ref, gf_ref,
             dh2_ref, do_ref, delta_ref, dag_ref, dpg_ref, dpl_ref, dwout_ref, dpw_ref, dps_ref, dgf_ref, loss_ref):
        i = pl.program_id(0)

        @pl.when(i == 0)
        def _():
            dwout_ref[...] = jnp.zeros_like(dwout_ref)
            dpw_ref[...] = jnp.zeros_like(dpw_ref)
            dps_ref[...] = jnp.zeros_like(dps_ref)
            dgf_ref[...] = jnp.zeros_like(dgf_ref)
            loss_ref[...] = jnp.zeros_like(loss_ref)

        row0 = i * tr
        real = (row0 + lax.broadcasted_iota(jnp.int32, (tr, 1), 0)) >= HEAD_ROWS
        h = h_ref[...]

        halo = jnp.where(i > 0, halo_ref[...], 0.0)
        ext = jnp.concatenate([halo, pin_ref[...]], axis=0)
        pooled = []
        for g, w in enumerate(POOL_WINDOWS):
            e = ext[:, g * POOL_GROUP:(g + 1) * POOL_GROUP]
            acc = e
            shift = 1
            while shift < w:
                acc = acc + pltpu.roll(acc, shift, 0)
                shift *= 2
            pooled.append((acc[HALO:] * _inv_count(row0, tr, w) - e[HALO:]).astype(BF16))
        pw = [pw_ref[g].astype(BF16) for g in range(ng)]
        mixed = jnp.concatenate([_nn(pooled[g], pw[g]) for g in range(ng)], axis=1)
        ps = ps_ref[...]
        mixed_s = mixed * ps
        pg = pg_ref[...]
        sig_p = _sigmoid(pg)
        silu_p = pg * sig_p
        pool_out = (silu_p * mixed_s).astype(BF16)
        ag = ag_ref[...]
        sig_a = _sigmoid(ag)
        silu_a = ag * sig_a
        at = at_ref[...]
        attn_out = (silu_a * at).astype(BF16)
        mix = _nn(pool_out, wout_ref[0:D_POOL, :]) + _nn(attn_out, wout_ref[D_POOL:D, :])
        h2 = h + mix

        r2 = lax.rsqrt(jnp.mean(h2 * h2, axis=-1, keepdims=True) + EPS)
        n2 = h2 * r2
        gfv = gf_ref[...]
        err = jnp.where(real, n2 * gfv - t_ref[...], 0.0)
        loss_ref[...] += jnp.sum(jnp.sum(err * err, axis=-1, keepdims=True), axis=0, keepdims=True) * (0.5 / D)
        dy = err * (1.0 / D)
        dgf_ref[...] += jnp.sum(dy * n2, axis=0, keepdims=True)
        dn = dy * gfv
        dh2 = r2 * (dn - n2 * jnp.mean(dn * n2, axis=-1, keepdims=True))
        dh2_ref[...] = dh2
        dh2b = dh2.astype(BF16)

        dwout_ref[0:D_POOL, :] += _tn(pool_out, dh2b)
        dwout_ref[D_POOL:D, :] += _tn(attn_out, dh2b)
        dcat = _nt(dh2b, wout_ref[...])
        dpo = dcat[:, 0:D_POOL]
        dao = dcat[:, D_POOL:D]
        do = dao * silu_a
        do_ref[...] = do.astype(BF16)
        prod = do * at
        delta_ref[...] = jnp.zeros_like(delta_ref)
        for hd in range(HEADS):
            grp, lane = _stat_slot(hd)
            delta_ref[grp, :, lane:lane + 1] = jnp.sum(prod[:, hd * V_HEAD:(hd + 1) * V_HEAD], axis=-1, keepdims=True)
        dag_ref[...] = (dao * at * (sig_a * (1.0 + ag * (1.0 - sig_a)))).astype(BF16)
        dmixed_s = dpo * silu_p
        dpg_ref[...] = (dpo * mixed_s * (sig_p * (1.0 + pg * (1.0 - sig_p)))).astype(BF16)
        dps_ref[...] += jnp.sum(dmixed_s * mixed, axis=0, keepdims=True)
        dmixed = (dmixed_s * ps).astype(BF16)
        dpl = []
        for g in range(ng):
            dm = dmixed[:, g * POOL_GROUP:(g + 1) * POOL_GROUP]
            dpl.append(_nt(dm, pw[g]))
            dpw_ref[g] += _tn(pooled[g], dm)
        dpl_ref[...] = jnp.concatenate(dpl, axis=1)

    halo_spec = pl.BlockSpec((HALO, D_POOL), lambda i: (jnp.maximum(i * per - 1, 0), 0))
    return pl.pallas_call(
        body,
        name="mid",
        grid=(N // tr,),
        in_specs=[
            _rows(D, tr), _rows(D, tr), _rows(D_POOL, tr), halo_spec, _rows(D_POOL, tr), _rows(D_POOL, tr),
            _rows(D_POOL, tr), _const(ng, POOL_GROUP, POOL_GROUP), _const(1, D_POOL), _const(D, D), _const(1, D),
        ],
        out_specs=[
            _rows(D, tr), _rows(D_POOL, tr), pl.BlockSpec((STAT_GROUPS, tr, 128), lambda i: (0, i, 0)),
            _rows(D_POOL, tr), _rows(D_POOL, tr), _rows(D_POOL, tr),
            _const(D, D), _const(ng, POOL_GROUP, POOL_GROUP), _const(1, D_POOL), _const(1, D), _const(1, 128),
        ],
        out_shape=[
            jax.ShapeDtypeStruct((N, D), F32), jax.ShapeDtypeStruct((N, D_POOL), BF16),
            jax.ShapeDtypeStruct((STAT_GROUPS, N, 128), F32),
            jax.ShapeDtypeStruct((N, D_POOL), BF16), jax.ShapeDtypeStruct((N, D_POOL), BF16),
            jax.ShapeDtypeStruct((N, D_POOL), F32), jax.ShapeDtypeStruct((D, D), F32),
            jax.ShapeDtypeStruct((ng, POOL_GROUP, POOL_GROUP), F32),
            jax.ShapeDtypeStruct((1, D_POOL), F32), jax.ShapeDtypeStruct((1, D), F32), jax.ShapeDtypeStruct((1, 128), F32),
        ],
        compiler_params=_cparams(dimension_semantics=("arbitrary",)),
    )(h, tgt, pool_in, pool_in, pool_gate, attn_gate, attn, pool_w, pool_scale, wout, gf)


def _unrope(dy, cosv, sinv):
    return dy * cosv + _swap64(dy * sinv) * _low_lanes()


def _attn_bwd(q, k, v, do, lse, delta, cosf, sinf):
    tiles = _attn_tiles()
    hp = HEADS_PER_STEP_BWD
    last = len(tiles) - 1

    def body(q_ref, k_ref, v_ref, do_ref, lse_ref, delta_ref, cos_ref, sin_ref, dq_ref, dkv_ref, dkr_ref, dk_acc, dv_acc):
        grp = pl.program_id(0)
        step = pl.program_id(1)

        @pl.when(step == 0)
        def _():
            dk_acc[...] = jnp.zeros_like(dk_acc)
            dv_acc[...] = jnp.zeros_like(dv_acc)

        @pl.when((step == 0) & (grp == 0))
        def _():
            dkr_ref[...] = jnp.zeros_like(dkr_ref)

        for idx, (q0, rows, klen) in enumerate(tiles):
            @pl.when(step == idx)
            def _(q0=q0, rows=rows, klen=klen):
                qs = pl.ds(q0, rows)
                for hd in range(hp):
                    qv = q_ref[hd, qs, :]
                    kv = k_ref[hd, 0:klen, :]
                    p = jnp.exp(_masked_scores(qv, kv, rows, klen) - lse_ref[0, qs, hd:hd + 1])
                    dob = do_ref[qs, hd * V_HEAD:(hd + 1) * V_HEAD]
                    ds = (p * (_nt(dob, v_ref[hd, 0:klen, :]) - delta_ref[0, qs, hd:hd + 1])).astype(BF16)
                    dq = _nn(ds, kv) * SCALE
                    dq_ref[hd, qs, 0:QK_NOPE] = dq[:, 0:QK_NOPE].astype(BF16)
                    dq_ref[hd, qs, QK_NOPE:] = _unrope(dq[:, QK_NOPE:], cos_ref[qs, :], sin_ref[qs, :]).astype(BF16)
                    dk_acc[hd, 0:klen, :] += _tn(ds, qv)
                    dv_acc[hd, 0:klen, :] += _tn(p.astype(BF16), dob)

        @pl.when(step == last)
        def _():
            for hd in range(hp):
                dkv_ref[hd, :, 0:QK_NOPE] = dk_acc[hd, :, 0:QK_NOPE].astype(BF16)
                dkv_ref[hd, :, QK_NOPE:] = dv_acc[hd].astype(BF16)
                dkr_ref[...] += dk_acc[hd, :, QK_NOPE:]

    per_head = lambda w: pl.BlockSpec((hp, N, w), lambda g, t: (g, 0, 0), pipeline_mode=pl.Buffered(1))
    cols = pl.BlockSpec((N, hp * V_HEAD), lambda g, t: (0, g), pipeline_mode=pl.Buffered(1))
    stat = pl.BlockSpec((1, N, 128), lambda g, t: (g, 0, 0), pipeline_mode=pl.Buffered(1))
    return pl.pallas_call(
        body,
        name="attn_bwd",
        grid=(HEADS // hp, len(tiles)),
        in_specs=[per_head(256), per_head(256), per_head(V_HEAD), cols, stat, stat, _const(N, 128), _const(N, 128)],
        out_specs=[per_head(256), per_head(256), _const(N, 128)],
        out_shape=[
            jax.ShapeDtypeStruct((HEADS, N, 256), BF16), jax.ShapeDtypeStruct((HEADS, N, 256), BF16),
            jax.ShapeDtypeStruct((N, 128), F32),
        ],
        scratch_shapes=[pltpu.VMEM((hp, N, 256), F32), pltpu.VMEM((hp, N, V_HEAD), F32)],
        compiler_params=_cparams(dimension_semantics=("arbitrary", "arbitrary")),
    )(q, k, v, do, lse, delta, cosf, sinf)


def _bwd_in(h, dh2, dq, dkv, dkr, cq, ckv, dpl, dpg, dag, norm_g, win, gq, wq, gkv, wkv, cosf, sinf):
    tr = ROWS_BWD
    nb = N // tr
    per = tr // HALO
    lead = HEAD_ROWS

    def body(h_ref, dh2_ref, dq_ref, dkv_ref, dkr_ref, cq_ref, ckv_ref, dpl_ref, halo_ref, dpg_ref, dag_ref,
             g_ref, win_ref, gq_ref, wq_ref, gkv_ref, wkv_ref, cos_ref, sin_ref,
             gx_ref, dmeta_ref, dwin_ref, dwq_ref, dwkv_ref, dg_ref, dgq_ref, dgkv_ref, dh_buf, gx_sem):
        i = pl.program_id(0)

        @pl.when(i == 0)
        def _():
            dwin_ref[...] = jnp.zeros_like(dwin_ref)
            dwq_ref[...] = jnp.zeros_like(dwq_ref)
            dwkv_ref[...] = jnp.zeros_like(dwkv_ref)
            dg_ref[...] = jnp.zeros_like(dg_ref)
            dgq_ref[...] = jnp.zeros_like(dgq_ref)
            dgkv_ref[...] = jnp.zeros_like(dgkv_ref)

        row0 = i * tr
        h = h_ref[...]
        r = lax.rsqrt(jnp.mean(h * h, axis=-1, keepdims=True) + EPS)
        n = h * r
        gv = g_ref[...]
        hn = (n * gv).astype(BF16)
        cq = cq_ref[...]
        rq = lax.rsqrt(jnp.mean(cq * cq, axis=-1, keepdims=True) + EPS)
        nq = cq * rq
        gqv = gq_ref[...]
        cqn = (nq * gqv).astype(BF16)
        dcqn = jnp.zeros((tr, Q_LORA), F32)
        for hd in range(HEADS):
            dqf = dq_ref[hd]
            dcqn = dcqn + _nn(dqf, wq_ref[hd])
            dwq_ref[hd] += _tn(dqf, cqn)
        dgq_ref[...] += jnp.sum(dcqn * nq, axis=0, keepdims=True)
        dnq = dcqn * gqv
        dcq = rq * (dnq - nq * jnp.mean(dnq * nq, axis=-1, keepdims=True))

        ckv = ckv_ref[...]
        rkv = lax.rsqrt(jnp.mean(ckv * ckv, axis=-1, keepdims=True) + EPS)
        nkv = ckv * rkv
        gkvv = gkv_ref[...]
        ckvn = (nkv * gkvv).astype(BF16)
        dckvn = jnp.zeros((tr, KV_LORA), F32)
        for hd in range(HEADS):
            dkv = dkv_ref[hd]
            dckvn = dckvn + _nt(dkv, wkv_ref[hd])
            dwkv_ref[hd] += _tn(ckvn, dkv)
        dgkv_ref[...] += jnp.sum(dckvn * nkv, axis=0, keepdims=True)
        dnkv = dckvn * gkvv
        dckv = rkv * (dnkv - nkv * jnp.mean(dnkv * nkv, axis=-1, keepdims=True))
        dkr = _unrope(dkr_ref[...], cos_ref[...], sin_ref[...])

        cur = dpl_ref[...]
        halo = jnp.where(i < nb - 1, halo_ref[...], 0.0)
        dpi = []
        for g, w in enumerate(POOL_WINDOWS):
            sl = slice(g * POOL_GROUP, (g + 1) * POOL_GROUP)
            a = jnp.concatenate([cur[:, sl] * _inv_count(row0, tr, w), halo[:, sl] * _inv_count(row0 + tr, HALO, w)], axis=0)
            acc = a
            shift = 1
            while shift < w:
                acc = acc + pltpu.roll(acc, tr + HALO - shift, 0)
                shift *= 2
            dpi.append(acc[0:tr] - cur[:, sl])

        du = jnp.concatenate([t.astype(BF16) for t in dpi] + [dpg_ref[...]] + [t.astype(BF16) for t in (dcq, dckv, dkr)],
                             axis=1)
        dagb = dag_ref[...]
        dwin_ref[0:O_KR_END, :] += _tn(du, hn)
        dwin_ref[O_AG:D_IN, :] += _tn(dagb, hn)
        dhn = _nn(du, win_ref[0:O_KR_END, :]) + _nn(dagb, win_ref[O_AG:D_IN, :])
        dg_ref[...] += jnp.sum(dhn * n, axis=0, keepdims=True)
        dn = dhn * gv
        dh = dh2_ref[...] + r * (dn - n * jnp.mean(dn * n, axis=-1, keepdims=True))

        first = pltpu.make_async_copy(dh_buf.at[pl.ds(lead, tr - lead), :], gx_ref.at[pl.ds(0, tr - lead), :], gx_sem)
        later = lambda step: pltpu.make_async_copy(
            dh_buf, gx_ref.at[pl.ds(pl.multiple_of(step * tr - lead, 16), tr), :], gx_sem)

        @pl.when(i == 1)
        def _():
            first.wait()

        @pl.when(i > 1)
        def _():
            later(i - 1).wait()

        dh_buf[...] = dh

        @pl.when(i == 0)
        def _():
            first.start()
            for chip in range(CHIPS):
                dmeta_ref[chip] = dh[PAD:HEAD_ROWS, chip * 256:(chip + 1) * 256]

        @pl.when(i > 0)
        def _():
            later(i).start()

        @pl.when(i == nb - 1)
        def _():
            later(i).wait()

    head = lambda w: pl.BlockSpec((HEADS, tr, w), lambda i: (0, i, 0))
    halo_spec = pl.BlockSpec((HALO, D_POOL), lambda i: (jnp.minimum((i + 1) * per, N // HALO - 1), 0))
    return pl.pallas_call(
        body,
        name="bwd_in",
        grid=(nb,),
        in_specs=[
            _rows(D, tr), _rows(D, tr), head(256), head(256), _rows(128, tr), _rows(Q_LORA, tr), _rows(KV_LORA, tr),
            _rows(D_POOL, tr), halo_spec, _rows(D_POOL, tr), _rows(D_POOL, tr),
            _const(1, D), _const(D_IN, D), _const(1, Q_LORA), _const(HEADS, 256, Q_LORA),
            _const(1, KV_LORA), _const(HEADS, KV_LORA, 256), _rows(128, tr), _rows(128, tr),
        ],
        out_specs=[
            pl.BlockSpec(memory_space=pl.ANY), _const(CHIPS, N_META, 256), _const(D_IN, D), _const(HEADS, 256, Q_LORA),
            _const(HEADS, KV_LORA, 256), _const(1, D), _const(1, Q_LORA), _const(1, KV_LORA),
        ],
        out_shape=[
            jax.ShapeDtypeStruct((S, D), F32), jax.ShapeDtypeStruct((CHIPS, N_META, 256), F32),
            jax.ShapeDtypeStruct((D_IN, D), F32), jax.ShapeDtypeStruct((HEADS, 256, Q_LORA), F32),
            jax.ShapeDtypeStruct((HEADS, KV_LORA, 256), F32),
            jax.ShapeDtypeStruct((1, D), F32), jax.ShapeDtypeStruct((1, Q_LORA), F32), jax.ShapeDtypeStruct((1, KV_LORA), F32),
        ],
        scratch_shapes=[pltpu.VMEM((tr, D), F32), pltpu.SemaphoreType.DMA],
        compiler_params=_cparams(dimension_semantics=("arbitrary",)),
    )(h, dh2, dq, dkv, dkr, cq, ckv, dpl, dpl, dpg, dag, norm_g, win, gq, wq, gkv, wkv, cosf, sinf)


def _local_step(h, tgt, norm_g, win, gq, wq, gkv, wkv, pool_w, pool_scale, wout, gf, cosf, sinf):
    pool_in, pool_gate, cq, ckv, attn_gate, q, k, v = _fwd_in(h, norm_g, win, gq, wq, gkv, wkv, cosf, sinf)
    attn, lse = _attn_fwd(q, k, v)
    dh2, do, delta, dag, dpg, dpl, dwout, dpw, dps, dgf, loss = _mid(
        h, tgt, pool_in, pool_gate, attn_gate, attn, pool_w, pool_scale, wout, gf)
    dq, dkv, dkr = _attn_bwd(q, k, v, do, lse, delta, cosf, sinf)
    gx, dmeta, dwin, dwq, dwkv, dg, dgq, dgkv = _bwd_in(
        h, dh2, dq, dkv, dkr, cq, ckv, dpl, dpg, dag, norm_g, win, gq, wq, gkv, wkv, cosf, sinf)
    return dict(gx=gx, dmeta=dmeta, dwin=dwin, dwq=dwq, dwkv=dwkv, dwout=dwout, dg=dg, dgq=dgq, dgkv=dgkv,
                dpw=dpw, dps=dps, dgf=dgf, loss=loss)


_CHIP_RELS = ((0, 0), (1, 0), (0, 1), (1, 1))

_ARR_ROWS = (SHARD_IN, SHARD_OUT, 256, KV_LORA, N_META)
_ARR_COLS = (D, D, Q_LORA, 256, 256)
_PIECES = (
    (0, 0, 256, 0), (0, 256, SHARD_IN - 256, 1),
    (1, 0, 128, 0), (1, 128, 128, 1),
    (2, 0, 128, 0), (2, 128, 128, 1),
    (3, 0, 64, 0), (3, 64, 64, 1),
    (4, 0, N_META, 0),
)
_NP = len(_PIECES)
_PIECE_MAX = (256, 128, 128, 64, N_META)


def _gathered_at(refs, arr, chip, r0, n):
    if arr in (0, 1):
        return refs[arr].at[pl.ds(pl.multiple_of(_ARR_ROWS[arr] * chip + r0, 16), n), :]
    return refs[arr].at[chip, pl.ds(r0, n), :]


def _remote(src, dst, send_sem, recv_sem, to):
    return pltpu.make_async_remote_copy(src_ref=src, dst_ref=dst, send_sem=send_sem, recv_sem=recv_sem,
                                        device_id=to, device_id_type=MESH)


def _gather_weights(winT_s, wout_s, wqT_s, wkv_s, meta_s, x2, tgt2):
    def body(win_ref, wout_ref, wq_ref, wkv_ref, meta_ref, x_ref, t_ref, win_o, wout_o, wq_o, wkv_o, h_o, tp_o,
             s_win, s_wout, s_wq, s_wkv, meta_all, head_buf, x_buf, t_buf, ici_send, ici_recv, fwd_send, fwd_recv,
             loc_sems):
        x, y, c = lax.axis_index("x"), lax.axis_index("y"), lax.axis_index("c")
        me = 2 * x + y
        stage = (s_win, s_wout, s_wq, s_wkv, meta_ref)
        outs = (win_o, wout_o, wq_o, wkv_o, meta_all)

        frames = pl.ds(HEAD_ROWS, S)
        loads = [pltpu.make_async_copy(x_ref, x_buf, loc_sems.at[0]), pltpu.make_async_copy(t_ref, t_buf, loc_sems.at[1])]
        local = [pltpu.make_async_copy(x_buf, h_o.at[frames, :], loc_sems.at[0]),
                 pltpu.make_async_copy(t_buf, tp_o.at[frames, :], loc_sems.at[1])]
        for cp in loads:
            cp.start()

        s_win[...] = win_ref[...].astype(BF16)
        s_wout[...] = wout_ref[...].astype(BF16)
        s_wq[0:QK, :] = wq_ref[...].astype(BF16)
        s_wq[QK:256, :] = jnp.zeros((256 - QK, Q_LORA), BF16)
        s_wkv[...] = wkv_ref[...].astype(BF16)

        def chip_of(rel):
            fx, fy = _CHIP_RELS[rel]
            return 2 * (x ^ fx) + (y ^ fy)

        def same_core_of(rel):
            fx, fy = _CHIP_RELS[rel]
            return (x ^ fx, y ^ fy, c)

        def ici_copy(rel, i, src_chip, to):
            arr, r0, n, _ = _PIECES[i]
            k = (rel - 1) * _NP + i
            return _remote(stage[arr].at[pl.ds(r0, n), :], _gathered_at(outs, arr, src_chip, r0, n),
                           ici_send.at[k], ici_recv.at[k], to)

        def fwd_copy(rel, i, to):
            arr, r0, n, _ = _PIECES[i]
            k = (rel - 1) * _NP + i
            place = _gathered_at(outs, arr, chip_of(rel), r0, n)
            return _remote(place, place, fwd_send.at[k], fwd_recv.at[k], to)

        for core in (0, 1):
            @pl.when(c == core)
            def _(core=core):
                mine = [i for i in range(_NP) if _PIECES[i][3] == core]
                theirs = [i for i in range(_NP) if _PIECES[i][3] != core]
                sends = [ici_copy(rel, i, me, same_core_of(rel)) for rel in (1, 2, 3) for i in mine]
                for cp in sends:
                    cp.start()
                for ld, st in zip(loads, local):
                    ld.wait()
                    st.start()
                for arr in range(5):
                    _gathered_at(outs, arr, me, 0, _ARR_ROWS[arr])[...] = stage[arr][...]
                for rel in (1, 2, 3):
                    for i in mine:
                        ici_copy(rel, i, chip_of(rel), (x, y, c)).wait_recv()
                        fwd = fwd_copy(rel, i, (x, y, 1 - c))
                        fwd.start()
                        sends.append(fwd)
                for rel in (1, 2, 3):
                    for i in theirs:
                        fwd_copy(rel, i, (x, y, c)).wait_recv()
                for cp in sends:
                    cp.wait_send()

        head_buf[...] = jnp.zeros_like(head_buf)
        zeros = pltpu.make_async_copy(head_buf, tp_o.at[pl.ds(0, HEAD_ROWS), :], loc_sems.at[2])
        zeros.start()
        zeros.wait()
        for chip in range(CHIPS):
            head_buf[PAD:HEAD_ROWS, chip * 256:(chip + 1) * 256] = meta_all[chip]
        head = pltpu.make_async_copy(head_buf, h_o.at[pl.ds(0, HEAD_ROWS), :], loc_sems.at[2])
        head.start()
        head.wait()
        for cp in local:
            cp.wait()

    vm = pl.BlockSpec(memory_space=pltpu.VMEM)
    hbm = pl.BlockSpec(memory_space=pl.ANY)
    return pl.pallas_call(
        body,
        name="gather_weights",
        in_specs=[vm] * 5 + [hbm] * 2,
        out_specs=[vm] * 4 + [hbm] * 2,
        out_shape=[
            jax.ShapeDtypeStruct((D_IN, D), BF16), jax.ShapeDtypeStruct((D, D), BF16),
            jax.ShapeDtypeStruct((CHIPS, 256, Q_LORA), BF16), jax.ShapeDtypeStruct((CHIPS, KV_LORA, 256), BF16),
            jax.ShapeDtypeStruct((N, D), F32), jax.ShapeDtypeStruct((N, D), F32),
        ],
        scratch_shapes=[pltpu.VMEM((_ARR_ROWS[a], _ARR_COLS[a]), BF16) for a in range(4)]
        + [pltpu.VMEM((CHIPS, N_META, 256), F32), pltpu.VMEM((HEAD_ROWS, D), F32), pltpu.VMEM((S, D), F32),
           pltpu.VMEM((S, D), F32)]
        + [pltpu.SemaphoreType.DMA((3 * _NP,))] * 4 + [pltpu.SemaphoreType.DMA((3,))],
        compiler_params=_cparams(),
    )(winT_s, wout_s, wqT_s, wkv_s, meta_s, x2, tgt2)


_SM_ROWS = (len(POOL_WINDOWS) * POOL_GROUP, VEC_ROWS)
_SM_COLS = (POOL_GROUP, D)
_SM_PIECES = ((0, 0, 256, 0), (0, 256, 256, 1), (1, 0, VEC_ROWS, 0))
_NSP = len(_SM_PIECES)


def _reduce_grads(dwin, dwout, dwq, dwkv, dmeta4, dpw, dg, dgf, dgq, dgkv, dps, loss):
    def body(dwin_ref, dwout_ref, dwq_ref, dwkv_ref, dmeta_ref, dpw_ref, dg_ref, dgf_ref, dgq_ref, dgkv_ref, dps_ref,
             loss_ref, gwin_o, gwout_o, gwq_o, gwkv_o, gmeta_o, gpw_o, gg_o, ggf_o, ggq_o, ggkv_o, gps_o, gloss_o,
             sb0, sb1, sb2, sb3, sb4, st0, st1, st2, st3, st4, rc0, rc1, rc2, rc3, rc4,
             vec, sm_sb0, sm_sb1, sm_cs0, sm_cs1, sm_rc0, sm_rc1, vec_fin,
             d2d_send, d2d_recv, ici_send, ici_recv, fin_send, fin_recv,
             swap_send, swap_recv, smi_send, smi_recv, smf_send, smf_recv):
        x, y, c = lax.axis_index("x"), lax.axis_index("y"), lax.axis_index("c")
        me = 2 * x + y
        grads = (dwin_ref, dwout_ref, dwq_ref, dwkv_ref, dmeta_ref)
        outs = (gwin_o, gwout_o, gwq_o, gwkv_o, gmeta_o)
        sib_buf = (sb0, sb1, sb2, sb3, sb4)
        stage = (st0, st1, st2, st3, st4)
        recv = (rc0, rc1, rc2, rc3, rc4)
        sm_mine = (dpw_ref, vec)
        sm_sib = (sm_sb0, sm_sb1)
        sm_chip = (sm_cs0, sm_cs1)
        sm_recv = (sm_rc0, sm_rc1)
        sm_out = (gpw_o, vec_fin)
        sibling = (x, y, 1 - c)

        def chip_of(rel):
            fx, fy = _CHIP_RELS[rel]
            return 2 * (x ^ fx) + (y ^ fy)

        def same_core_of(rel):
            fx, fy = _CHIP_RELS[rel]
            return (x ^ fx, y ^ fy, c)

        def slot(bufs, i, idx):
            arr, _, n, _ = _PIECES[i]
            return bufs[arr].at[idx, pl.ds(0, n), :]

        def d2d_copy(rel, i):
            arr, r0, n, _ = _PIECES[i]
            k = rel * _NP + i
            return _remote(_gathered_at(grads, arr, chip_of(rel), r0, n), slot(sib_buf, i, rel),
                           d2d_send.at[k], d2d_recv.at[k], sibling)

        def ici_copy(rel, i):
            k = (rel - 1) * _NP + i
            return _remote(slot(stage, i, rel - 1), slot(recv, i, rel - 1), ici_send.at[k], ici_recv.at[k],
                           same_core_of(rel))

        def fin_copy(i):
            arr, r0, n, _ = _PIECES[i]
            place = outs[arr].at[pl.ds(r0, n), :]
            return _remote(place, place, fin_send.at[i], fin_recv.at[i], sibling)

        def sm_ici_copy(rel, j):
            blk, r0, n, _ = _SM_PIECES[j]
            k = (rel - 1) * _NSP + j
            return _remote(sm_chip[blk].at[pl.ds(r0, n), :], sm_recv[blk].at[rel - 1, pl.ds(r0, n), :],
                           smi_send.at[k], smi_recv.at[k], same_core_of(rel))

        def sm_fin_copy(j):
            blk, r0, n, _ = _SM_PIECES[j]
            place = sm_out[blk].at[pl.ds(r0, n), :]
            return _remote(place, place, smf_send.at[j], smf_recv.at[j], sibling)

        vec[...] = jnp.zeros_like(vec)
        vec[0:1, :] = dg_ref[...]
        vec[1:2, :] = dgf_ref[...]
        vec[2:3, V_GQ:V_GQ + Q_LORA] = dgq_ref[...]
        vec[2:3, V_GKV:V_GKV + KV_LORA] = dgkv_ref[...]
        vec[2:3, V_PS:V_PS + D_POOL] = dps_ref[...]
        vec[2:3, V_LOSS:D] = loss_ref[...]
        swaps = [_remote(sm_mine[b], sm_sib[b], swap_send.at[b], swap_recv.at[b], sibling) for b in (0, 1)]
        for cp in swaps:
            cp.start()

        for core in (0, 1):
            @pl.when(c == core)
            def _(core=core):
                mine = [i for i in range(_NP) if _PIECES[i][3] == core]
                theirs = [i for i in range(_NP) if _PIECES[i][3] != core]
                sm_mine_p = [j for j in range(_NSP) if _SM_PIECES[j][3] == core]
                sm_theirs_p = [j for j in range(_NSP) if _SM_PIECES[j][3] != core]
                sends = list(swaps)

                for rel in (1, 2, 3, 0):
                    for i in theirs:
                        cp = d2d_copy(rel, i)
                        cp.start()
                        sends.append(cp)

                for b in (0, 1):
                    swaps[b].wait_recv()
                    sm_chip[b][...] = sm_mine[b][...] + sm_sib[b][...]
                for rel in (1, 2, 3):
                    for j in sm_mine_p:
                        cp = sm_ici_copy(rel, j)
                        cp.start()
                        sends.append(cp)

                for rel in (1, 2, 3):
                    for i in mine:
                        arr, r0, n, _ = _PIECES[i]
                        d2d_copy(rel, i).wait_recv()
                        total = _gathered_at(grads, arr, chip_of(rel), r0, n)[...] + slot(sib_buf, i, rel)[...]
                        slot(stage, i, rel - 1)[...] = total.astype(stage[arr].dtype)
                        cp = ici_copy(rel, i)
                        cp.start()
                        sends.append(cp)

                for i in mine:
                    arr, r0, n, _ = _PIECES[i]
                    d2d_copy(0, i).wait_recv()
                    total = _gathered_at(grads, arr, me, r0, n)[...] + slot(sib_buf, i, 0)[...]
                    for rel in (1, 2, 3):
                        ici_copy(rel, i).wait_recv()
                        total = total + slot(recv, i, rel - 1)[...].astype(F32)
                    outs[arr][pl.ds(r0, n), :] = total
                    cp = fin_copy(i)
                    cp.start()
                    sends.append(cp)

                for j in sm_mine_p:
                    blk, r0, n, _ = _SM_PIECES[j]
                    for rel in (1, 2, 3):
                        sm_ici_copy(rel, j).wait_recv()
                    total = jnp.zeros((n, _SM_COLS[blk]), F32)
                    for chip in range(CHIPS):
                        flips = chip ^ me
                        rel = jnp.where(flips == 2, 1, jnp.where(flips == 1, 2, flips))
                        theirs_rows = sm_recv[blk][jnp.maximum(rel - 1, 0), pl.ds(r0, n), :]
                        total = total + jnp.where(rel == 0, sm_chip[blk][pl.ds(r0, n), :], theirs_rows)
                    sm_out[blk][pl.ds(r0, n), :] = total
                    cp = sm_fin_copy(j)
                    cp.start()
                    sends.append(cp)

                for i in theirs:
                    fin_copy(i).wait_recv()
                for j in sm_theirs_p:
                    sm_fin_copy(j).wait_recv()
                for cp in sends:
                    cp.wait_send()

        gg_o[...] = vec_fin[0:1, :]
        ggf_o[...] = vec_fin[1:2, :]
        ggq_o[...] = vec_fin[2:3, V_GQ:V_GQ + Q_LORA]
        ggkv_o[...] = vec_fin[2:3, V_GKV:V_GKV + KV_LORA]
        gps_o[...] = vec_fin[2:3, V_PS:V_PS + D_POOL]
        gloss_o[...] = vec_fin[2:3, V_LOSS:D]

    vm = pl.BlockSpec(memory_space=pltpu.VMEM)
    piece_buf = lambda lead, dtype: [pltpu.VMEM((lead, _PIECE_MAX[a], _ARR_COLS[a]), F32 if a == 4 else dtype)
                                     for a in range(5)]
    sm_buf = lambda *lead: [pltpu.VMEM(lead + (_SM_ROWS[b], _SM_COLS[b]), F32) for b in (0, 1)]
    dma = lambda n: [pltpu.SemaphoreType.DMA((n,))] * 2
    return pl.pallas_call(
        body,
        name="reduce_grads",
        in_specs=[vm] * 12,
        out_specs=[vm] * 12,
        out_shape=[jax.ShapeDtypeStruct((_ARR_ROWS[a], _ARR_COLS[a]), F32) for a in range(5)]
        + [jax.ShapeDtypeStruct((_SM_ROWS[0], _SM_COLS[0]), F32), jax.ShapeDtypeStruct((1, D), F32),
           jax.ShapeDtypeStruct((1, D), F32), jax.ShapeDtypeStruct((1, Q_LORA), F32),
           jax.ShapeDtypeStruct((1, KV_LORA), F32), jax.ShapeDtypeStruct((1, D_POOL), F32),
           jax.ShapeDtypeStruct((1, 128), F32)],
        scratch_shapes=piece_buf(CHIPS, F32) + piece_buf(3, BF16) + piece_buf(3, BF16)
        + [pltpu.VMEM((VEC_ROWS, D), F32)] + sm_buf() + sm_buf() + sm_buf(3) + [pltpu.VMEM((VEC_ROWS, D), F32)]
        + dma(CHIPS * _NP) + dma(3 * _NP) + dma(_NP) + dma(2) + dma(3 * _NSP) + dma(_NSP),
        compiler_params=_cparams(),
    )(dwin, dwout, dwq, dwkv, dmeta4, dpw, dg, dgf, dgq, dgkv, dps, loss)


def _adamw_math(w, g, m, v):
    m = B1 * m + (1.0 - B1) * g
    v = B2 * v + (1.0 - B2) * (g * g)
    m_hat = m / C1
    v_hat = v / C2
    delta = -LR * (m_hat / (jnp.sqrt(v_hat) + ADAM_EPS) + WD * w)
    return delta, m, v


def _adamw_rows(name, w, g, m, v, block_rows):
    rows, cols = w.shape

    def body(w_ref, g_ref, m_ref, v_ref, go_ref, d_ref, nm_ref, nv_ref):
        g = g_ref[...]
        go_ref[...] = g
        d_ref[...], nm_ref[...], nv_ref[...] = _adamw_math(w_ref[...], g, m_ref[...], v_ref[...])

    spec = pl.BlockSpec((block_rows, cols), lambda i: (i, 0))
    return pl.pallas_call(
        body,
        name=name,
        grid=(rows // block_rows,),
        in_specs=[spec] * 4,
        out_specs=[spec] * 4,
        out_shape=[jax.ShapeDtypeStruct(w.shape, F32)] * 4,
        compiler_params=_cparams(dimension_semantics=("arbitrary",)),
    )(w, g, m, v)


def _adamw_small(groups):
    n = len(groups)

    def body(*refs):
        ins, outs = refs[:4 * n], refs[4 * n:]
        for t in range(n):
            w_ref, g_ref, m_ref, v_ref = ins[4 * t:4 * t + 4]
            g = g_ref[0:w_ref.shape[0], :]
            outs[4 * t][...] = g
            outs[4 * t + 1][...], outs[4 * t + 2][...], outs[4 * t + 3][...] = _adamw_math(
                w_ref[...], g, m_ref[...], v_ref[...])

    vm = pl.BlockSpec(memory_space=pltpu.VMEM)
    flat = [a for grp in groups for a in grp]
    outs = pl.pallas_call(
        body,
        name="adamw_small",
        in_specs=[vm] * (4 * n),
        out_specs=[vm] * (4 * n),
        out_shape=[jax.ShapeDtypeStruct(grp[0].shape, F32) for grp in groups for _ in range(4)],
        compiler_params=_cparams(),
    )(*flat)
    return [tuple(outs[4 * t:4 * t + 4]) for t in range(n)]


def _rope_tables():
    half = QK_ROPE // 2
    f32 = np.float32
    inv_freq = (f32(1.0) / (f32(ROPE_THETA) ** (np.arange(half, dtype=f32) / f32(half)))).astype(f32)
    pos = np.arange(N, dtype=f32) - f32(PAD)
    ang = (pos[:, None] * inv_freq[None, :]).astype(f32)
    cos, sin = np.cos(ang).astype(f32), np.sin(ang).astype(f32)
    zero = np.zeros((N, 128 - QK_ROPE), f32)
    return jnp.asarray(np.concatenate([cos, cos, zero], axis=1)), jnp.asarray(np.concatenate([-sin, sin, zero], axis=1))


def kernel(x, meta_tokens, norm_g, w_in, q_norm_g, w_q_b, kv_norm_g, w_kv_b, pool_w, pool_scale, w_out, final_norm_g, loss_target, m_meta_tokens, m_norm_g, m_w_in, m_q_norm_g, m_w_q_b, m_kv_norm_g, m_w_kv_b, m_pool_w, m_pool_scale, m_w_out, m_final_norm_g, v_meta_tokens, v_norm_g, v_w_in, v_q_norm_g, v_w_q_b, v_kv_norm_g, v_w_kv_b, v_pool_w, v_pool_scale, v_w_out, v_final_norm_g):
    tr = lambda a: a[0].T
    win, wout, wq, wkv, h, tgt = _gather_weights(tr(w_in), w_out[0], tr(w_q_b), w_kv_b[0], meta_tokens, x[0], loss_target[0])
    cosf, sinf = _rope_tables()
    gf = final_norm_g.reshape(1, D)

    part = _local_step(h, tgt, norm_g, win, q_norm_g, wq, kv_norm_g, wkv, pool_w[0], pool_scale, wout, gf, cosf, sinf)

    pw2 = lambda a: a.reshape(len(POOL_WINDOWS) * POOL_GROUP, POOL_GROUP)
    gwinT, gwout, gwqT, gwkv, gmeta, gpw, gg, ggf, ggq, ggkv, gps, gloss = _reduce_grads(
        part["dwin"], part["dwout"], part["dwq"], part["dwkv"], part["dmeta"], pw2(part["dpw"]), part["dg"],
        part["dgf"], part["dgq"], part["dgkv"], part["dps"], part["loss"])

    r_in = _adamw_rows("adamw_w_in", tr(w_in), gwinT, tr(m_w_in), tr(v_w_in), 248)
    r_out = _adamw_rows("adamw_w_out", w_out[0], gwout, m_w_out[0], v_w_out[0], 128)
    fn2 = lambda a: a.reshape(1, D)
    r_meta, r_norm, r_gq, r_wq, r_gkv, r_wkv, r_pw, r_ps, r_fn = _adamw_small([
        (meta_tokens, gmeta, m_meta_tokens, v_meta_tokens),
        (norm_g, gg, m_norm_g, v_norm_g),
        (q_norm_g, ggq, m_q_norm_g, v_q_norm_g),
        (tr(w_q_b), gwqT, tr(m_w_q_b), tr(v_w_q_b)),
        (kv_norm_g, ggkv, m_kv_norm_g, v_kv_norm_g),
        (w_kv_b[0], gwkv, m_w_kv_b[0], v_w_kv_b[0]),
        (pw2(pool_w), gpw, pw2(m_pool_w), pw2(v_pool_w)),
        (pool_scale, gps, m_pool_scale, v_pool_scale),
        (fn2(final_norm_g), ggf, fn2(m_final_norm_g), fn2(v_final_norm_g)),
    ])
    untr = lambda a: a.T[None]
    pw4 = lambda a: a.reshape(1, len(POOL_WINDOWS), POOL_GROUP, POOL_GROUP)
    per_kind = [[
        r_meta[kind], r_norm[kind], untr(r_in[kind]), r_gq[kind], untr(r_wq[kind]), r_gkv[kind], r_wkv[kind][None],
        pw4(r_pw[kind]), r_ps[kind], r_out[kind][None], r_fn[kind].reshape(D),
    ] for kind in range(4)]
    return (gloss[0, 0], part["gx"][None], *per_kind[0], *per_kind[1], *per_kind[2], *per_kind[3])
```

```python
import jax
import jax.numpy as jnp
import numpy as np
from jax import lax
from jax.experimental import pallas as pl
from jax.experimental.pallas import tpu as pltpu

F32 = jnp.float32
BF16 = jnp.bfloat16

D = 1024
S = 2048
N_META = 16
PAD = 112
HEAD_ROWS = PAD + N_META
N = HEAD_ROWS + S
D_POOL = 512
POOL_WINDOWS = (2, 4, 8, 16)
POOL_GROUP = 128
HALO = 16
HEADS = 4
QK_NOPE = 128
QK_ROPE = 64
QK = QK_NOPE + QK_ROPE
V_HEAD = 128
Q_LORA = 256
KV_LORA = 128
D_IN = 1984
EPS = 1e-6
ROPE_THETA = 10000.0
SCALE = QK ** -0.5
CHIPS = 4

ROWS_FWD = 544
ROWS_MID = 544
ROWS_BWD = 544
TK = 128
TQ = 256
NQ = S // TQ
HEADS_PER_STEP_BWD = 2

O_PI, O_PG, O_CQ, O_CKV, O_KR, O_AG = 0, 512, 1024, 1280, 1408, 1472
O_KR_END = O_KR + 128
SHARD_IN = D_IN // CHIPS
SHARD_OUT = D // CHIPS

LR, B1, B2, ADAM_EPS, WD, STEP = 0.001, 0.9, 0.999, 1e-08, 0.01, 10
C1 = 1.0 - B1**STEP
C2 = 1.0 - B2**STEP

VMEM_LIMIT = 60 * 1024 * 1024
MESH = pl.DeviceIdType.MESH
NEG = -1e30

VEC_ROWS = 8
V_GQ, V_GKV, V_PS, V_LOSS = 0, 256, 384, 896


def _cparams(**kw):
    return pltpu.CompilerParams(vmem_limit_bytes=VMEM_LIMIT, **kw)


def _nt(a, b):
    return lax.dot_general(a, b, (((1,), (1,)), ((), ())), preferred_element_type=F32)


def _tn(a, b):
    return lax.dot_general(a, b, (((0,), (0,)), ((), ())), preferred_element_type=F32)


def _nn(a, b):
    return jnp.dot(a, b, preferred_element_type=F32)


def _swap64(t):
    return pltpu.roll(t, 32, 1) + pltpu.roll(t, 96, 1)


def _sigmoid(x):
    return 1.0 / (1.0 + jnp.exp(-x))


def _low_lanes():
    return (lax.broadcasted_iota(jnp.int32, (1, 128), 1) < QK_ROPE).astype(F32)


def _rows(w, rows):
    return pl.BlockSpec((rows, w), lambda i: (i, 0))


def _const(*shape):
    return pl.BlockSpec(shape, lambda *_: (0,) * len(shape), pipeline_mode=pl.Buffered(1))


STAT_GROUPS = HEADS // HEADS_PER_STEP_BWD


def _stat_slot(head):
    return head // HEADS_PER_STEP_BWD, head % HEADS_PER_STEP_BWD


def _attn_tiles():
    return [(0, TK, TK)] + [(TK + TQ * t, TQ, TK + TQ * (t + 1)) for t in range(NQ)]


def _masked_scores(q, k, rows, klen):
    s = _nt(q, k)
    col = lax.broadcasted_iota(jnp.int32, (1, TK), 1)
    head_bias = jnp.where(col >= PAD, 0.0, NEG)
    if klen == TK:
        return s + head_bias
    r = lax.broadcasted_iota(jnp.int32, (rows, 1), 0) >> 6
    c = lax.broadcasted_iota(jnp.int32, (1, rows), 1) >> 6
    diag_bias = jnp.where(c <= r, 0.0, NEG)
    parts = [s[:, 0:TK] + head_bias]
    if klen - rows > TK:
        parts.append(s[:, TK:klen - rows])
    parts.append(s[:, klen - rows:klen] + diag_bias)
    return jnp.concatenate(parts, axis=1)


def _fwd_in(h, norm_g, win, gq, wq, gkv, wkv, cosf, sinf):
    tr = ROWS_FWD

    def body(h_ref, g_ref, win_ref, gq_ref, wq_ref, gkv_ref, wkv_ref, cos_ref, sin_ref,
             pi_ref, pg_ref, cq_ref, ckv_ref, ag_ref, q_ref, k_ref, v_ref):
        h = h_ref[...]
        r = lax.rsqrt(jnp.mean(h * h, axis=-1, keepdims=True) + EPS)
        hn = ((h * r) * g_ref[...]).astype(BF16)
        u = _nt(hn, win_ref[0:O_KR_END, :])
        pi_ref[...] = u[:, O_PI:O_PG]
        pg_ref[...] = u[:, O_PG:O_CQ]
        cq = u[:, O_CQ:O_CKV]
        ckv = u[:, O_CKV:O_KR]
        cq_ref[...] = cq
        ckv_ref[...] = ckv
        ag_ref[...] = _nt(hn, win_ref[O_AG:D_IN, :])
        cosv = cos_ref[...]
        sinv = sin_ref[...]
        kr = u[:, O_KR:O_KR_END] * _low_lanes()
        kr = (kr * cosv + _swap64(kr) * sinv).astype(BF16)
        rq = lax.rsqrt(jnp.mean(cq * cq, axis=-1, keepdims=True) + EPS)
        cqn = ((cq * rq) * gq_ref[...]).astype(BF16)
        rkv = lax.rsqrt(jnp.mean(ckv * ckv, axis=-1, keepdims=True) + EPS)
        ckvn = ((ckv * rkv) * gkv_ref[...]).astype(BF16)
        for hd in range(HEADS):
            qh = _nt(cqn, wq_ref[hd]) * SCALE
            z = qh[:, QK_NOPE:]
            q_ref[hd, :, 0:QK_NOPE] = qh[:, 0:QK_NOPE].astype(BF16)
            q_ref[hd, :, QK_NOPE:] = (z * cosv + _swap64(z) * sinv).astype(BF16)
            kvh = _nn(ckvn, wkv_ref[hd])
            k_ref[hd, :, 0:QK_NOPE] = kvh[:, 0:QK_NOPE].astype(BF16)
            k_ref[hd, :, QK_NOPE:] = kr
            v_ref[hd] = kvh[:, QK_NOPE:].astype(BF16)

    head = lambda w: pl.BlockSpec((HEADS, tr, w), lambda i: (0, i, 0))
    return pl.pallas_call(
        body,
        name="fwd_in",
        grid=(N // tr,),
        in_specs=[
            _rows(D, tr), _const(1, D), _const(D_IN, D), _const(1, Q_LORA), _const(HEADS, 256, Q_LORA),
            _const(1, KV_LORA), _const(HEADS, KV_LORA, 256), _rows(128, tr), _rows(128, tr),
        ],
        out_specs=[_rows(D_POOL, tr), _rows(D_POOL, tr), _rows(Q_LORA, tr), _rows(KV_LORA, tr), _rows(D_POOL, tr),
                   head(256), head(256), head(V_HEAD)],
        out_shape=[
            jax.ShapeDtypeStruct((N, D_POOL), F32), jax.ShapeDtypeStruct((N, D_POOL), F32),
            jax.ShapeDtypeStruct((N, Q_LORA), F32), jax.ShapeDtypeStruct((N, KV_LORA), F32),
            jax.ShapeDtypeStruct((N, D_POOL), F32),
            jax.ShapeDtypeStruct((HEADS, N, 256), BF16), jax.ShapeDtypeStruct((HEADS, N, 256), BF16),
            jax.ShapeDtypeStruct((HEADS, N, V_HEAD), BF16),
        ],
        compiler_params=_cparams(dimension_semantics=("arbitrary",)),
    )(h, norm_g, win, gq, wq, gkv, wkv, cosf, sinf)


def _attn_fwd(q, k, v):
    tiles = _attn_tiles()
    n_t = len(tiles)

    def body(q_hbm, k_hbm, v_hbm, o_hbm, lse_ref, q_buf, k_buf, v_buf, o_buf, in_sems, out_sems):
        step = pl.program_id(0)

        def loads(idx):
            q0, rows, _ = tiles[idx]
            rs = pl.ds(q0, rows)
            return [pltpu.make_async_copy(src.at[:, rs, :], dst.at[:, rs, :], in_sems.at[a, idx % 2])
                    for a, (src, dst) in enumerate(((q_hbm, q_buf), (k_hbm, k_buf), (v_hbm, v_buf)))]

        def store(idx):
            q0, rows, _ = tiles[idx]
            return pltpu.make_async_copy(o_buf.at[idx % 2, pl.ds(0, rows), :], o_hbm.at[pl.ds(q0, rows), :],
                                         out_sems.at[idx % 2])

        @pl.when(step == 0)
        def _():
            lse_ref[...] = jnp.zeros_like(lse_ref)
            for cp in loads(0):
                cp.start()

        for idx, (q0, rows, klen) in enumerate(tiles):
            @pl.when(step == idx)
            def _(idx=idx, q0=q0, rows=rows, klen=klen):
                for cp in loads(idx):
                    cp.wait()
                if idx + 1 < n_t:
                    for cp in loads(idx + 1):
                        cp.start()
                if idx >= 2:
                    store(idx - 2).wait()
                for hd in range(HEADS):
                    s = _masked_scores(q_buf[hd, q0:q0 + rows, :], k_buf[hd, 0:klen, :], rows, klen)
                    m = jnp.max(s, axis=-1, keepdims=True)
                    p = jnp.exp(s - m)
                    l = jnp.sum(p, axis=-1, keepdims=True)
                    o_buf[idx % 2, 0:rows, hd * V_HEAD:(hd + 1) * V_HEAD] = _nn(p.astype(BF16), v_buf[hd, 0:klen, :]) / l
                    grp, lane = _stat_slot(hd)
                    lse_ref[grp, q0:q0 + rows, lane:lane + 1] = m + jnp.log(l)
                store(idx).start()
                if idx == n_t - 1:
                    store(idx - 1).wait()
                    store(idx).wait()

    hbm = pl.BlockSpec(memory_space=pl.ANY)
    return pl.pallas_call(
        body,
        name="attn_fwd",
        grid=(n_t,),
        in_specs=[hbm, hbm, hbm],
        out_specs=[hbm, _const(STAT_GROUPS, N, 128)],
        out_shape=[jax.ShapeDtypeStruct((N, HEADS * V_HEAD), F32), jax.ShapeDtypeStruct((STAT_GROUPS, N, 128), F32)],
        scratch_shapes=[pltpu.VMEM((HEADS, N, 256), BF16), pltpu.VMEM((HEADS, N, 256), BF16),
                        pltpu.VMEM((HEADS, N, V_HEAD), BF16), pltpu.VMEM((2, TQ, HEADS * V_HEAD), F32),
                        pltpu.SemaphoreType.DMA((3, 2)), pltpu.SemaphoreType.DMA((2,))],
        compiler_params=_cparams(dimension_semantics=("arbitrary",)),
    )(q, k, v)


def _inv_count(row0, rows, w):
    row = row0 + lax.broadcasted_iota(jnp.int32, (rows, 1), 0)
    return 1.0 / jnp.clip(row - (PAD - 1), 1, w).astype(F32)


def _mid(h, tgt, pool_in, pool_gate, attn_gate, attn, pool_w, pool_scale, wout, gf):
    tr = ROWS_MID
    per = tr // HALO
    ng = len(POOL_WINDOWS)

    def body(h_ref, t_ref, pin_ref, halo_ref, pg_ref, ag_ref, at_ref, pw_ref, ps_ref, wout_ref, gf_ref,
             dh2_ref, do_ref, delta_ref, dag_ref, dpg_ref, dpl_ref, dwout_ref, dpw_ref, dps_ref, dgf_ref, loss_ref):
        i = pl.program_id(0)

        @pl.when(i == 0)
        def _():
            dwout_ref[...] = jnp.zeros_like(dwout_ref)
            dpw_ref[...] = jnp.zeros_like(dpw_ref)
            dps_ref[...] = jnp.zeros_like(dps_ref)
            dgf_ref[...] = jnp.zeros_like(dgf_ref)
            loss_ref[...] = jnp.zeros_like(loss_ref)

        row0 = i * tr
        real = (row0 + lax.broadcasted_iota(jnp.int32, (tr, 1), 0)) >= HEAD_ROWS
        h = h_ref[...]

        halo = jnp.where(i > 0, halo_ref[...], 0.0)
        ext = jnp.concatenate([halo, pin_ref[...]], axis=0)
        pooled = []
        for g, w in enumerate(POOL_WINDOWS):
            e = ext[:, g * POOL_GROUP:(g + 1) * POOL_GROUP]
            acc = e
            shift = 1
            while shift < w:
                acc = acc + pltpu.roll(acc, shift, 0)
                shift *= 2
            pooled.append((acc[HALO:] * _inv_count(row0, tr, w) - e[HALO:]).astype(BF16))
        pw = [pw_ref[g].astype(BF16) for g in range(ng)]
        mixed = jnp.concatenate([_nn(pooled[g], pw[g]) for g in range(ng)], axis=1)
        ps = ps_ref[...]
        mixed_s = mixed * ps
        pg = pg_ref[...]
        sig_p = _sigmoid(pg)
        silu_p = pg * sig_p
        pool_out = (silu_p * mixed_s).astype(BF16)
        ag = ag_ref[...]
        sig_a = _sigmoid(ag)
        silu_a = ag * sig_a
        at = at_ref[...]
        attn_out = (silu_a * at).astype(BF16)
        mix = _nn(pool_out, wout_ref[0:D_POOL, :]) + _nn(attn_out, wout_ref[D_POOL:D, :])
        h2 = h + mix

        r2 = lax.rsqrt(jnp.mean(h2 * h2, axis=-1, keepdims=True) + EPS)
        n2 = h2 * r2
        gfv = gf_ref[...]
        err = jnp.where(real, n2 * gfv - t_ref[...], 0.0)
        loss_ref[...] += jnp.sum(jnp.sum(err * err, axis=-1, keepdims=True), axis=0, keepdims=True) * (0.5 / D)
        dy = err * (1.0 / D)
        dgf_ref[...] += jnp.sum(dy * n2, axis=0, keepdims=True)
        dn = dy * gfv
        dh2 = r2 * (dn - n2 * jnp.mean(dn * n2, axis=-1, keepdims=True))
        dh2_ref[...] = dh2
        dh2b = dh2.astype(BF16)

        dwout_ref[0:D_POOL, :] += _tn(pool_out, dh2b)
        dwout_ref[D_POOL:D, :] += _tn(attn_out, dh2b)
        dcat = _nt(dh2b, wout_ref[...])
        dpo = dcat[:, 0:D_POOL]
        dao = dcat[:, D_POOL:D]
        do = dao * silu_a
        prod = do * at
        delta_ref[...] = jnp.zeros_like(delta_ref)
        for hd in range(HEADS):
            grp, lane = _stat_slot(hd)
            cols = slice(hd * V_HEAD, (hd + 1) * V_HEAD)
            do_ref[grp, :, lane * V_HEAD:(lane + 1) * V_HEAD] = do[:, cols].astype(BF16)
            delta_ref[grp, :, lane:lane + 1] = jnp.sum(prod[:, cols], axis=-1, keepdims=True)
        dag_ref[...] = (dao * at * (sig_a * (1.0 + ag * (1.0 - sig_a)))).astype(BF16)
        dmixed_s = dpo * silu_p
        dpg_ref[...] = (dpo * mixed_s * (sig_p * (1.0 + pg * (1.0 - sig_p)))).astype(BF16)
        dps_ref[...] += jnp.sum(dmixed_s * mixed, axis=0, keepdims=True)
        dmixed = (dmixed_s * ps).astype(BF16)
        dpl = []
        for g in range(ng):
            dm = dmixed[:, g * POOL_GROUP:(g + 1) * POOL_GROUP]
            dpl.append(_nt(dm, pw[g]))
            dpw_ref[g] += _tn(pooled[g], dm)
        dpl_ref[...] = jnp.concatenate(dpl, axis=1)

    halo_spec = pl.BlockSpec((HALO, D_POOL), lambda i: (jnp.maximum(i * per - 1, 0), 0))
    return pl.pallas_call(
        body,
        name="mid",
        grid=(N // tr,),
        in_specs=[
            _rows(D, tr), _rows(D, tr), _rows(D_POOL, tr), halo_spec, _rows(D_POOL, tr), _rows(D_POOL, tr),
            _rows(D_POOL, tr), _const(ng, POOL_GROUP, POOL_GROUP), _const(1, D_POOL), _const(D, D), _const(1, D),
        ],
        out_specs=[
            _rows(D, tr), pl.BlockSpec((STAT_GROUPS, tr, HEADS_PER_STEP_BWD * V_HEAD), lambda i: (0, i, 0)),
            pl.BlockSpec((STAT_GROUPS, tr, 128), lambda i: (0, i, 0)),
            _rows(D_POOL, tr), _rows(D_POOL, tr), _rows(D_POOL, tr),
            _const(D, D), _const(ng, POOL_GROUP, POOL_GROUP), _const(1, D_POOL), _const(1, D), _const(1, 128),
        ],
        out_shape=[
            jax.ShapeDtypeStruct((N, D), F32), jax.ShapeDtypeStruct((STAT_GROUPS, N, HEADS_PER_STEP_BWD * V_HEAD), BF16),
            jax.ShapeDtypeStruct((STAT_GROUPS, N, 128), F32),
            jax.ShapeDtypeStruct((N, D_POOL), BF16), jax.ShapeDtypeStruct((N, D_POOL), BF16),
            jax.ShapeDtypeStruct((N, D_POOL), F32), jax.ShapeDtypeStruct((D, D), F32),
            jax.ShapeDtypeStruct((ng, POOL_GROUP, POOL_GROUP), F32),
            jax.ShapeDtypeStruct((1, D_POOL), F32), jax.ShapeDtypeStruct((1, D), F32), jax.ShapeDtypeStruct((1, 128), F32),
        ],
        compiler_params=_cparams(dimension_semantics=("arbitrary",)),
    )(h, tgt, pool_in, pool_in, pool_gate, attn_gate, attn, pool_w, pool_scale, wout, gf)


def _unrope(dy, cosv, sinv):
    return dy * cosv + _swap64(dy * sinv) * _low_lanes()


def _attn_bwd(q, k, v, do, lse, delta, cosf, sinf):
    tiles = _attn_tiles()
    hp = HEADS_PER_STEP_BWD
    n_g = HEADS // hp
    n_t = len(tiles)

    def body(q_hbm, k_hbm, v_hbm, do_hbm, lse_ref, delta_ref, cos_ref, sin_ref, dq_hbm, dkv_ref, dkr_ref,
             q_buf, k_buf, v_buf, do_buf, dq_buf, dk_acc, dv_acc, in_sems, out_sems):
        grp = pl.program_id(0)
        step = pl.program_id(1)
        heads = pl.ds(grp * hp, hp)

        def loads(g, idx):
            q0, rows, _ = tiles[idx]
            rs = pl.ds(q0, rows)
            par = (g * n_t + idx) % 2
            hs = pl.ds(g * hp, hp)
            pairs = ((q_hbm.at[hs, rs, :], q_buf.at[:, rs, :]), (k_hbm.at[hs, rs, :], k_buf.at[:, rs, :]),
                     (v_hbm.at[hs, rs, :], v_buf.at[:, rs, :]), (do_hbm.at[g, rs, :], do_buf.at[rs, :]))
            return [pltpu.make_async_copy(src, dst, in_sems.at[a, par]) for a, (src, dst) in enumerate(pairs)]

        def store(idx):
            q0, rows, _ = tiles[idx]
            return pltpu.make_async_copy(dq_buf.at[idx % 2, :, pl.ds(0, rows), :], dq_hbm.at[heads, pl.ds(q0, rows), :],
                                         out_sems.at[idx % 2])

        @pl.when(step == 0)
        def _():
            dk_acc[...] = jnp.zeros_like(dk_acc)
            dv_acc[...] = jnp.zeros_like(dv_acc)

        @pl.when((step == 0) & (grp == 0))
        def _():
            dkr_ref[...] = jnp.zeros_like(dkr_ref)
            for cp in loads(grp, 0):
                cp.start()

        for idx, (q0, rows, klen) in enumerate(tiles):
            @pl.when(step == idx)
            def _(idx=idx, q0=q0, rows=rows, klen=klen):
                for cp in loads(grp, idx):
                    cp.wait()
                if idx + 1 < n_t:
                    for cp in loads(grp, idx + 1):
                        cp.start()
                if idx >= 2:
                    store(idx - 2).wait()
                qs = pl.ds(q0, rows)
                for hd in range(hp):
                    qv = q_buf[hd, qs, :]
                    kv = k_buf[hd, 0:klen, :]
                    p = jnp.exp(_masked_scores(qv, kv, rows, klen) - lse_ref[0, qs, hd:hd + 1])
                    dob = do_buf[qs, hd * V_HEAD:(hd + 1) * V_HEAD]
                    ds = (p * (_nt(dob, v_buf[hd, 0:klen, :]) - delta_ref[0, qs, hd:hd + 1])).astype(BF16)
                    dq = _nn(ds, kv) * SCALE
                    dq_buf[idx % 2, hd, 0:rows, 0:QK_NOPE] = dq[:, 0:QK_NOPE].astype(BF16)
                    dq_buf[idx % 2, hd, 0:rows, QK_NOPE:] = _unrope(dq[:, QK_NOPE:], cos_ref[qs, :], sin_ref[qs, :]).astype(BF16)
                    dk_acc[hd, 0:klen, :] += _tn(ds, qv)
                    dv_acc[hd, 0:klen, :] += _tn(p.astype(BF16), dob)
                store(idx).start()

        @pl.when(step == n_t - 1)
        def _():
            @pl.when(grp + 1 < n_g)
            def _():
                for cp in loads(grp + 1, 0):
                    cp.start()

            for hd in range(hp):
                dkv_ref[hd, :, 0:QK_NOPE] = dk_acc[hd, :, 0:QK_NOPE].astype(BF16)
                dkv_ref[hd, :, QK_NOPE:] = dv_acc[hd].astype(BF16)
                dkr_ref[...] += dk_acc[hd, :, QK_NOPE:]
            store(n_t - 2).wait()
            store(n_t - 1).wait()

    hbm = pl.BlockSpec(memory_space=pl.ANY)
    stat = pl.BlockSpec((1, N, 128), lambda g, t: (g, 0, 0), pipeline_mode=pl.Buffered(1))
    return pl.pallas_call(
        body,
        name="attn_bwd",
        grid=(n_g, n_t),
        in_specs=[hbm, hbm, hbm, hbm, stat, stat, _const(N, 128), _const(N, 128)],
        out_specs=[hbm, pl.BlockSpec((hp, N, 256), lambda g, t: (g, 0, 0), pipeline_mode=pl.Buffered(1)), _const(N, 128)],
        out_shape=[
            jax.ShapeDtypeStruct((HEADS, N, 256), BF16), jax.ShapeDtypeStruct((HEADS, N, 256), BF16),
            jax.ShapeDtypeStruct((N, 128), F32),
        ],
        scratch_shapes=[pltpu.VMEM((hp, N, 256), BF16), pltpu.VMEM((hp, N, 256), BF16), pltpu.VMEM((hp, N, V_HEAD), BF16),
                        pltpu.VMEM((N, hp * V_HEAD), BF16), pltpu.VMEM((2, hp, TQ, 256), BF16),
                        pltpu.VMEM((hp, N, 256), F32), pltpu.VMEM((hp, N, V_HEAD), F32),
                        pltpu.SemaphoreType.DMA((4, 2)), pltpu.SemaphoreType.DMA((2,))],
        compiler_params=_cparams(dimension_semantics=("arbitrary", "arbitrary")),
    )(q, k, v, do, lse, delta, cosf, sinf)


def _bwd_in(h, dh2, dq, dkv, dkr, cq, ckv, dpl, dpg, dag, norm_g, win, gq, wq, gkv, wkv, cosf, sinf):
    tr = ROWS_BWD
    nb = N // tr
    per = tr // HALO
    lead = HEAD_ROWS

    def body(h_ref, dh2_ref, dq_ref, dkv_ref, dkr_ref, cq_ref, ckv_ref, dpl_ref, halo_ref, dpg_ref, dag_ref,
             g_ref, win_ref, gq_ref, wq_ref, gkv_ref, wkv_ref, cos_ref, sin_ref,
             gx_ref, dmeta_ref, dwin_ref, dwq_ref, dwkv_ref, dg_ref, dgq_ref, dgkv_ref, dh_buf, gx_sem):
        i = pl.program_id(0)

        @pl.when(i == 0)
        def _():
            dwin_ref[...] = jnp.zeros_like(dwin_ref)
            dwq_ref[...] = jnp.zeros_like(dwq_ref)
            dwkv_ref[...] = jnp.zeros_like(dwkv_ref)
            dg_ref[...] = jnp.zeros_like(dg_ref)
            dgq_ref[...] = jnp.zeros_like(dgq_ref)
            dgkv_ref[...] = jnp.zeros_like(dgkv_ref)

        row0 = i * tr
        h = h_ref[...]
        r = lax.rsqrt(jnp.mean(h * h, axis=-1, keepdims=True) + EPS)
        n = h * r
        gv = g_ref[...]
        hn = (n * gv).astype(BF16)
        cq = cq_ref[...]
        rq = lax.rsqrt(jnp.mean(cq * cq, axis=-1, keepdims=True) + EPS)
        nq = cq * rq
        gqv = gq_ref[...]
        cqn = (nq * gqv).astype(BF16)
        dcqn = jnp.zeros((tr, Q_LORA), F32)
        for hd in range(HEADS):
            dqf = dq_ref[hd]
            dcqn = dcqn + _nn(dqf, wq_ref[hd])
            dwq_ref[hd] += _tn(dqf, cqn)
        dgq_ref[...] += jnp.sum(dcqn * nq, axis=0, keepdims=True)
        dnq = dcqn * gqv
        dcq = rq * (dnq - nq * jnp.mean(dnq * nq, axis=-1, keepdims=True))

        ckv = ckv_ref[...]
        rkv = lax.rsqrt(jnp.mean(ckv * ckv, axis=-1, keepdims=True) + EPS)
        nkv = ckv * rkv
        gkvv = gkv_ref[...]
        ckvn = (nkv * gkvv).astype(BF16)
        dckvn = jnp.zeros((tr, KV_LORA), F32)
        for hd in range(HEADS):
            dkv = dkv_ref[hd]
            dckvn = dckvn + _nt(dkv, wkv_ref[hd])
            dwkv_ref[hd] += _tn(ckvn, dkv)
        dgkv_ref[...] += jnp.sum(dckvn * nkv, axis=0, keepdims=True)
        dnkv = dckvn * gkvv
        dckv = rkv * (dnkv - nkv * jnp.mean(dnkv * nkv, axis=-1, keepdims=True))
        dkr = _unrope(dkr_ref[...], cos_ref[...], sin_ref[...])

        cur = dpl_ref[...]
        halo = jnp.where(i < nb - 1, halo_ref[...], 0.0)
        dpi = []
        for g, w in enumerate(POOL_WINDOWS):
            sl = slice(g * POOL_GROUP, (g + 1) * POOL_GROUP)
            a = jnp.concatenate([cur[:, sl] * _inv_count(row0, tr, w), halo[:, sl] * _inv_count(row0 + tr, HALO, w)], axis=0)
            acc = a
            shift = 1
            while shift < w:
                acc = acc + pltpu.roll(acc, tr + HALO - shift, 0)
                shift *= 2
            dpi.append(acc[0:tr] - cur[:, sl])

        du = jnp.concatenate([t.astype(BF16) for t in dpi] + [dpg_ref[...]] + [t.astype(BF16) for t in (dcq, dckv, dkr)],
                             axis=1)
        dagb = dag_ref[...]
        dwin_ref[0:O_KR_END, :] += _tn(du, hn)
        dwin_ref[O_AG:D_IN, :] += _tn(dagb, hn)
        dhn = _nn(du, win_ref[0:O_KR_END, :]) + _nn(dagb, win_ref[O_AG:D_IN, :])
        dg_ref[...] += jnp.sum(dhn * n, axis=0, keepdims=True)
        dn = dhn * gv
        dh = dh2_ref[...] + r * (dn - n * jnp.mean(dn * n, axis=-1, keepdims=True))

        first = pltpu.make_async_copy(dh_buf.at[pl.ds(lead, tr - lead), :], gx_ref.at[pl.ds(0, tr - lead), :], gx_sem)
        later = lambda step: pltpu.make_async_copy(
            dh_buf, gx_ref.at[pl.ds(pl.multiple_of(step * tr - lead, 16), tr), :], gx_sem)

        @pl.when(i == 1)
        def _():
            first.wait()

        @pl.when(i > 1)
        def _():
            later(i - 1).wait()

        dh_buf[...] = dh

        @pl.when(i == 0)
        def _():
            first.start()
            for chip in range(CHIPS):
                dmeta_ref[chip] = dh[PAD:HEAD_ROWS, chip * 256:(chip + 1) * 256]

        @pl.when(i > 0)
        def _():
            later(i).start()

        @pl.when(i == nb - 1)
        def _():
            later(i).wait()

    head = lambda w: pl.BlockSpec((HEADS, tr, w), lambda i: (0, i, 0))
    halo_spec = pl.BlockSpec((HALO, D_POOL), lambda i: (jnp.minimum((i + 1) * per, N // HALO - 1), 0))
    return pl.pallas_call(
        body,
        name="bwd_in",
        grid=(nb,),
        in_specs=[
            _rows(D, tr), _rows(D, tr), head(256), head(256), _rows(128, tr), _rows(Q_LORA, tr), _rows(KV_LORA, tr),
            _rows(D_POOL, tr), halo_spec, _rows(D_POOL, tr), _rows(D_POOL, tr),
            _const(1, D), _const(D_IN, D), _const(1, Q_LORA), _const(HEADS, 256, Q_LORA),
            _const(1, KV_LORA), _const(HEADS, KV_LORA, 256), _rows(128, tr), _rows(128, tr),
        ],
        out_specs=[
            pl.BlockSpec(memory_space=pl.ANY), _const(CHIPS, N_META, 256), _const(D_IN, D), _const(HEADS, 256, Q_LORA),
            _const(HEADS, KV_LORA, 256), _const(1, D), _const(1, Q_LORA), _const(1, KV_LORA),
        ],
        out_shape=[
            jax.ShapeDtypeStruct((S, D), F32), jax.ShapeDtypeStruct((CHIPS, N_META, 256), F32),
            jax.ShapeDtypeStruct((D_IN, D), F32), jax.ShapeDtypeStruct((HEADS, 256, Q_LORA), F32),
            jax.ShapeDtypeStruct((HEADS, KV_LORA, 256), F32),
            jax.ShapeDtypeStruct((1, D), F32), jax.ShapeDtypeStruct((1, Q_LORA), F32), jax.ShapeDtypeStruct((1, KV_LORA), F32),
        ],
        scratch_shapes=[pltpu.VMEM((tr, D), F32), pltpu.SemaphoreType.DMA],
        compiler_params=_cparams(dimension_semantics=("arbitrary",)),
    )(h, dh2, dq, dkv, dkr, cq, ckv, dpl, dpl, dpg, dag, norm_g, win, gq, wq, gkv, wkv, cosf, sinf)


def _local_step(h, tgt, norm_g, win, gq, wq, gkv, wkv, pool_w, pool_scale, wout, gf, cosf, sinf):
    pool_in, pool_gate, cq, ckv, attn_gate, q, k, v = _fwd_in(h, norm_g, win, gq, wq, gkv, wkv, cosf, sinf)
    attn, lse = _attn_fwd(q, k, v)
    dh2, do, delta, dag, dpg, dpl, dwout, dpw, dps, dgf, loss = _mid(
        h, tgt, pool_in, pool_gate, attn_gate, attn, pool_w, pool_scale, wout, gf)
    dq, dkv, dkr = _attn_bwd(q, k, v, do, lse, delta, cosf, sinf)
    gx, dmeta, dwin, dwq, dwkv, dg, dgq, dgkv = _bwd_in(
        h, dh2, dq, dkv, dkr, cq, ckv, dpl, dpg, dag, norm_g, win, gq, wq, gkv, wkv, cosf, sinf)
    return dict(gx=gx, dmeta=dmeta, dwin=dwin, dwq=dwq, dwkv=dwkv, dwout=dwout, dg=dg, dgq=dgq, dgkv=dgkv,
                dpw=dpw, dps=dps, dgf=dgf, loss=loss)


_CHIP_RELS = ((0, 0), (1, 0), (0, 1), (1, 1))

_ARR_ROWS = (SHARD_IN, SHARD_OUT, 256, KV_LORA, N_META)
_ARR_COLS = (D, D, Q_LORA, 256, 256)
_PIECES = (
    (0, 0, 256, 0), (0, 256, SHARD_IN - 256, 1),
    (1, 0, 128, 0), (1, 128, 128, 1),
    (2, 0, 128, 0), (2, 128, 128, 1),
    (3, 0, 64, 0), (3, 64, 64, 1),
    (4, 0, N_META, 0),
)
_NP = len(_PIECES)
_PIECE_MAX = (256, 128, 128, 64, N_META)


def _gathered_at(refs, arr, chip, r0, n):
    if arr in (0, 1):
        return refs[arr].at[pl.ds(pl.multiple_of(_ARR_ROWS[arr] * chip + r0, 16), n), :]
    return refs[arr].at[chip, pl.ds(r0, n), :]


def _remote(src, dst, send_sem, recv_sem, to):
    return pltpu.make_async_remote_copy(src_ref=src, dst_ref=dst, send_sem=send_sem, recv_sem=recv_sem,
                                        device_id=to, device_id_type=MESH)


def _gather_weights(winT_s, wout_s, wqT_s, wkv_s, meta_s, x2, tgt2):
    def body(win_ref, wout_ref, wq_ref, wkv_ref, meta_ref, x_ref, t_ref, win_o, wout_o, wq_o, wkv_o, h_o, tp_o,
             s_win, s_wout, s_wq, s_wkv, meta_all, head_buf, x_buf, t_buf, ici_send, ici_recv, fwd_send, fwd_recv,
             loc_sems):
        x, y, c = lax.axis_index("x"), lax.axis_index("y"), lax.axis_index("c")
        me = 2 * x + y
        stage = (s_win, s_wout, s_wq, s_wkv, meta_ref)
        outs = (win_o, wout_o, wq_o, wkv_o, meta_all)

        frames = pl.ds(HEAD_ROWS, S)
        loads = [pltpu.make_async_copy(x_ref, x_buf, loc_sems.at[0]), pltpu.make_async_copy(t_ref, t_buf, loc_sems.at[1])]
        local = [pltpu.make_async_copy(x_buf, h_o.at[frames, :], loc_sems.at[0]),
                 pltpu.make_async_copy(t_buf, tp_o.at[frames, :], loc_sems.at[1])]
        for cp in loads:
            cp.start()

        s_win[...] = win_ref[...].astype(BF16)
        s_wout[...] = wout_ref[...].astype(BF16)
        s_wq[0:QK, :] = wq_ref[...].astype(BF16)
        s_wq[QK:256, :] = jnp.zeros((256 - QK, Q_LORA), BF16)
        s_wkv[...] = wkv_ref[...].astype(BF16)

        def chip_of(rel):
            fx, fy = _CHIP_RELS[rel]
            return 2 * (x ^ fx) + (y ^ fy)

        def same_core_of(rel):
            fx, fy = _CHIP_RELS[rel]
            return (x ^ fx, y ^ fy, c)

        def ici_copy(rel, i, src_chip, to):
            arr, r0, n, _ = _PIECES[i]
            k = (rel - 1) * _NP + i
            return _remote(stage[arr].at[pl.ds(r0, n), :], _gathered_at(outs, arr, src_chip, r0, n),
                           ici_send.at[k], ici_recv.at[k], to)

        def fwd_copy(rel, i, to):
            arr, r0, n, _ = _PIECES[i]
            k = (rel - 1) * _NP + i
            place = _gathered_at(outs, arr, chip_of(rel), r0, n)
            return _remote(place, place, fwd_send.at[k], fwd_recv.at[k], to)

        for core in (0, 1):
            @pl.when(c == core)
            def _(core=core):
                mine = [i for i in range(_NP) if _PIECES[i][3] == core]
                theirs = [i for i in range(_NP) if _PIECES[i][3] != core]
                sends = [ici_copy(rel, i, me, same_core_of(rel)) for rel in (1, 2, 3) for i in mine]
                for cp in sends:
                    cp.start()
                for ld, st in zip(loads, local):
                    ld.wait()
                    st.start()
                for arr in range(5):
                    _gathered_at(outs, arr, me, 0, _ARR_ROWS[arr])[...] = stage[arr][...]
                for rel in (1, 2, 3):
                    for i in mine:
                        ici_copy(rel, i, chip_of(rel), (x, y, c)).wait_recv()
                        fwd = fwd_copy(rel, i, (x, y, 1 - c))
                        fwd.start()
                        sends.append(fwd)
                for rel in (1, 2, 3):
                    for i in theirs:
                        fwd_copy(rel, i, (x, y, c)).wait_recv()
                for cp in sends:
                    cp.wait_send()

        head_buf[...] = jnp.zeros_like(head_buf)
        zeros = pltpu.make_async_copy(head_buf, tp_o.at[pl.ds(0, HEAD_ROWS), :], loc_sems.at[2])
        zeros.start()
        zeros.wait()
        for chip in range(CHIPS):
            head_buf[PAD:HEAD_ROWS, chip * 256:(chip + 1) * 256] = meta_all[chip]
        head = pltpu.make_async_copy(head_buf, h_o.at[pl.ds(0, HEAD_ROWS), :], loc_sems.at[2])
        head.start()
        head.wait()
        for cp in local:
            cp.wait()

    vm = pl.BlockSpec(memory_space=pltpu.VMEM)
    hbm = pl.BlockSpec(memory_space=pl.ANY)
    return pl.pallas_call(
        body,
        name="gather_weights",
        in_specs=[vm] * 5 + [hbm] * 2,
        out_specs=[vm] * 4 + [hbm] * 2,
        out_shape=[
            jax.ShapeDtypeStruct((D_IN, D), BF16), jax.ShapeDtypeStruct((D, D), BF16),
            jax.ShapeDtypeStruct((CHIPS, 256, Q_LORA), BF16), jax.ShapeDtypeStruct((CHIPS, KV_LORA, 256), BF16),
            jax.ShapeDtypeStruct((N, D), F32), jax.ShapeDtypeStruct((N, D), F32),
        ],
        scratch_shapes=[pltpu.VMEM((_ARR_ROWS[a], _ARR_COLS[a]), BF16) for a in range(4)]
        + [pltpu.VMEM((CHIPS, N_META, 256), F32), pltpu.VMEM((HEAD_ROWS, D), F32), pltpu.VMEM((S, D), F32),
           pltpu.VMEM((S, D), F32)]
        + [pltpu.SemaphoreType.DMA((3 * _NP,))] * 4 + [pltpu.SemaphoreType.DMA((3,))],
        compiler_params=_cparams(),
    )(winT_s, wout_s, wqT_s, wkv_s, meta_s, x2, tgt2)


_SM_ROWS = (len(POOL_WINDOWS) * POOL_GROUP, VEC_ROWS)
_SM_COLS = (POOL_GROUP, D)
_SM_PIECES = ((0, 0, 256, 0), (0, 256, 256, 1), (1, 0, VEC_ROWS, 0))
_NSP = len(_SM_PIECES)


def _reduce_grads(dwin, dwout, dwq, dwkv, dmeta4, dpw, dg, dgf, dgq, dgkv, dps, loss):
    def body(dwin_ref, dwout_ref, dwq_ref, dwkv_ref, dmeta_ref, dpw_ref, dg_ref, dgf_ref, dgq_ref, dgkv_ref, dps_ref,
             loss_ref, gwin_o, gwout_o, gwq_o, gwkv_o, gmeta_o, gpw_o, gg_o, ggf_o, ggq_o, ggkv_o, gps_o, gloss_o,
             sb0, sb1, sb2, sb3, sb4, st0, st1, st2, st3, st4, rc0, rc1, rc2, rc3, rc4,
             vec, sm_sb0, sm_sb1, sm_cs0, sm_cs1, sm_rc0, sm_rc1, vec_fin,
             d2d_send, d2d_recv, ici_send, ici_recv, fin_send, fin_recv,
             swap_send, swap_recv, smi_send, smi_recv, smf_send, smf_recv):
        x, y, c = lax.axis_index("x"), lax.axis_index("y"), lax.axis_index("c")
        me = 2 * x + y
        grads = (dwin_ref, dwout_ref, dwq_ref, dwkv_ref, dmeta_ref)
        outs = (gwin_o, gwout_o, gwq_o, gwkv_o, gmeta_o)
        sib_buf = (sb0, sb1, sb2, sb3, sb4)
        stage = (st0, st1, st2, st3, st4)
        recv = (rc0, rc1, rc2, rc3, rc4)
        sm_mine = (dpw_ref, vec)
        sm_sib = (sm_sb0, sm_sb1)
        sm_chip = (sm_cs0, sm_cs1)
        sm_recv = (sm_rc0, sm_rc1)
        sm_out = (gpw_o, vec_fin)
        sibling = (x, y, 1 - c)

        def chip_of(rel):
            fx, fy = _CHIP_RELS[rel]
            return 2 * (x ^ fx) + (y ^ fy)

        def same_core_of(rel):
            fx, fy = _CHIP_RELS[rel]
            return (x ^ fx, y ^ fy, c)

        def slot(bufs, i, idx):
            arr, _, n, _ = _PIECES[i]
            return bufs[arr].at[idx, pl.ds(0, n), :]

        def d2d_copy(rel, i):
            arr, r0, n, _ = _PIECES[i]
            k = rel * _NP + i
            return _remote(_gathered_at(grads, arr, chip_of(rel), r0, n), slot(sib_buf, i, rel),
                           d2d_send.at[k], d2d_recv.at[k], sibling)

        def ici_copy(rel, i):
            k = (rel - 1) * _NP + i
            return _remote(slot(stage, i, rel - 1), slot(recv, i, rel - 1), ici_send.at[k], ici_recv.at[k],
                           same_core_of(rel))

        def fin_copy(i):
            arr, r0, n, _ = _PIECES[i]
            place = outs[arr].at[pl.ds(r0, n), :]
            return _remote(place, place, fin_send.at[i], fin_recv.at[i], sibling)

        def sm_ici_copy(rel, j):
            blk, r0, n, _ = _SM_PIECES[j]
            k = (rel - 1) * _NSP + j
            return _remote(sm_chip[blk].at[pl.ds(r0, n), :], sm_recv[blk].at[rel - 1, pl.ds(r0, n), :],
                           smi_send.at[k], smi_recv.at[k], same_core_of(rel))

        def sm_fin_copy(j):
            blk, r0, n, _ = _SM_PIECES[j]
            place = sm_out[blk].at[pl.ds(r0, n), :]
            return _remote(place, place, smf_send.at[j], smf_recv.at[j], sibling)

        vec[...] = jnp.zeros_like(vec)
        vec[0:1, :] = dg_ref[...]
        vec[1:2, :] = dgf_ref[...]
        vec[2:3, V_GQ:V_GQ + Q_LORA] = dgq_ref[...]
        vec[2:3, V_GKV:V_GKV + KV_LORA] = dgkv_ref[...]
        vec[2:3, V_PS:V_PS + D_POOL] = dps_ref[...]
        vec[2:3, V_LOSS:D] = loss_ref[...]
        swaps = [_remote(sm_mine[b], sm_sib[b], swap_send.at[b], swap_recv.at[b], sibling) for b in (0, 1)]
        for cp in swaps:
            cp.start()

        for core in (0, 1):
            @pl.when(c == core)
            def _(core=core):
                mine = [i for i in range(_NP) if _PIECES[i][3] == core]
                theirs = [i for i in range(_NP) if _PIECES[i][3] != core]
                sm_mine_p = [j for j in range(_NSP) if _SM_PIECES[j][3] == core]
                sm_theirs_p = [j for j in range(_NSP) if _SM_PIECES[j][3] != core]
                sends = list(swaps)

                for rel in (1, 2, 3, 0):
                    for i in theirs:
                        cp = d2d_copy(rel, i)
                        cp.start()
                        sends.append(cp)

                for b in (0, 1):
                    swaps[b].wait_recv()
                    sm_chip[b][...] = sm_mine[b][...] + sm_sib[b][...]
                for rel in (1, 2, 3):
                    for j in sm_mine_p:
                        cp = sm_ici_copy(rel, j)
                        cp.start()
                        sends.append(cp)

                for rel in (1, 2, 3):
                    for i in mine:
                        arr, r0, n, _ = _PIECES[i]
                        d2d_copy(rel, i).wait_recv()
                        total = _gathered_at(grads, arr, chip_of(rel), r0, n)[...] + slot(sib_buf, i, rel)[...]
                        slot(stage, i, rel - 1)[...] = total.astype(stage[arr].dtype)
                        cp = ici_copy(rel, i)
                        cp.start()
                        sends.append(cp)

                for i in mine:
                    arr, r0, n, _ = _PIECES[i]
                    d2d_copy(0, i).wait_recv()
                    total = _gathered_at(grads, arr, me, r0, n)[...] + slot(sib_buf, i, 0)[...]
                    for rel in (1, 2, 3):
                        ici_copy(rel, i).wait_recv()
                        total = total + slot(recv, i, rel - 1)[...].astype(F32)
                    outs[arr][pl.ds(r0, n), :] = total
                    cp = fin_copy(i)
                    cp.start()
                    sends.append(cp)

                for j in sm_mine_p:
                    blk, r0, n, _ = _SM_PIECES[j]
                    for rel in (1, 2, 3):
                        sm_ici_copy(rel, j).wait_recv()
                    total = jnp.zeros((n, _SM_COLS[blk]), F32)
                    for chip in range(CHIPS):
                        flips = chip ^ me
                        rel = jnp.where(flips == 2, 1, jnp.where(flips == 1, 2, flips))
                        theirs_rows = sm_recv[blk][jnp.maximum(rel - 1, 0), pl.ds(r0, n), :]
                        total = total + jnp.where(rel == 0, sm_chip[blk][pl.ds(r0, n), :], theirs_rows)
                    sm_out[blk][pl.ds(r0, n), :] = total
                    cp = sm_fin_copy(j)
                    cp.start()
                    sends.append(cp)

                for i in theirs:
                    fin_copy(i).wait_recv()
                for j in sm_theirs_p:
                    sm_fin_copy(j).wait_recv()
                for cp in sends:
                    cp.wait_send()

        gg_o[...] = vec_fin[0:1, :]
        ggf_o[...] = vec_fin[1:2, :]
        ggq_o[...] = vec_fin[2:3, V_GQ:V_GQ + Q_LORA]
        ggkv_o[...] = vec_fin[2:3, V_GKV:V_GKV + KV_LORA]
        gps_o[...] = vec_fin[2:3, V_PS:V_PS + D_POOL]
        gloss_o[...] = vec_fin[2:3, V_LOSS:D]

    vm = pl.BlockSpec(memory_space=pltpu.VMEM)
    piece_buf = lambda lead, dtype: [pltpu.VMEM((lead, _PIECE_MAX[a], _ARR_COLS[a]), F32 if a == 4 else dtype)
                                     for a in range(5)]
    sm_buf = lambda *lead: [pltpu.VMEM(lead + (_SM_ROWS[b], _SM_COLS[b]), F32) for b in (0, 1)]
    dma = lambda n: [pltpu.SemaphoreType.DMA((n,))] * 2
    return pl.pallas_call(
        body,
        name="reduce_grads",
        in_specs=[vm] * 12,
        out_specs=[vm] * 12,
        out_shape=[jax.ShapeDtypeStruct((_ARR_ROWS[a], _ARR_COLS[a]), F32) for a in range(5)]
        + [jax.ShapeDtypeStruct((_SM_ROWS[0], _SM_COLS[0]), F32), jax.ShapeDtypeStruct((1, D), F32),
           jax.ShapeDtypeStruct((1, D), F32), jax.ShapeDtypeStruct((1, Q_LORA), F32),
           jax.ShapeDtypeStruct((1, KV_LORA), F32), jax.ShapeDtypeStruct((1, D_POOL), F32),
           jax.ShapeDtypeStruct((1, 128), F32)],
        scratch_shapes=piece_buf(CHIPS, F32) + piece_buf(3, BF16) + piece_buf(3, BF16)
        + [pltpu.VMEM((VEC_ROWS, D), F32)] + sm_buf() + sm_buf() + sm_buf(3) + [pltpu.VMEM((VEC_ROWS, D), F32)]
        + dma(CHIPS * _NP) + dma(3 * _NP) + dma(_NP) + dma(2) + dma(3 * _NSP) + dma(_NSP),
        compiler_params=_cparams(),
    )(dwin, dwout, dwq, dwkv, dmeta4, dpw, dg, dgf, dgq, dgkv, dps, loss)


def _adamw_math(w, g, m, v):
    m = B1 * m + (1.0 - B1) * g
    v = B2 * v + (1.0 - B2) * (g * g)
    m_hat = m / C1
    v_hat = v / C2
    delta = -LR * (m_hat / (jnp.sqrt(v_hat) + ADAM_EPS) + WD * w)
    return delta, m, v


def _adamw_rows(name, w, g, m, v, block_rows):
    rows, cols = w.shape

    def body(w_ref, g_ref, m_ref, v_ref, go_ref, d_ref, nm_ref, nv_ref):
        g = g_ref[...]
        go_ref[...] = g
        d_ref[...], nm_ref[...], nv_ref[...] = _adamw_math(w_ref[...], g, m_ref[...], v_ref[...])

    spec = pl.BlockSpec((block_rows, cols), lambda i: (i, 0))
    return pl.pallas_call(
        body,
        name=name,
        grid=(rows // block_rows,),
        in_specs=[spec] * 4,
        out_specs=[spec] * 4,
        out_shape=[jax.ShapeDtypeStruct(w.shape, F32)] * 4,
        compiler_params=_cparams(dimension_semantics=("arbitrary",)),
    )(w, g, m, v)


def _adamw_small(groups):
    n = len(groups)

    def body(*refs):
        ins, outs = refs[:4 * n], refs[4 * n:]
        for t in range(n):
            w_ref, g_ref, m_ref, v_ref = ins[4 * t:4 * t + 4]
            g = g_ref[0:w_ref.shape[0], :]
            outs[4 * t][...] = g
            outs[4 * t + 1][...], outs[4 * t + 2][...], outs[4 * t + 3][...] = _adamw_math(
                w_ref[...], g, m_ref[...], v_ref[...])

    vm = pl.BlockSpec(memory_space=pltpu.VMEM)
    flat = [a for grp in groups for a in grp]
    outs = pl.pallas_call(
        body,
        name="adamw_small",
        in_specs=[vm] * (4 * n),
        out_specs=[vm] * (4 * n),
        out_shape=[jax.ShapeDtypeStruct(grp[0].shape, F32) for grp in groups for _ in range(4)],
        compiler_params=_cparams(),
    )(*flat)
    return [tuple(outs[4 * t:4 * t + 4]) for t in range(n)]


def _rope_tables():
    half = QK_ROPE // 2
    f32 = np.float32
    inv_freq = (f32(1.0) / (f32(ROPE_THETA) ** (np.arange(half, dtype=f32) / f32(half)))).astype(f32)
    pos = np.arange(N, dtype=f32) - f32(PAD)
    ang = (pos[:, None] * inv_freq[None, :]).astype(f32)
    cos, sin = np.cos(ang).astype(f32), np.sin(ang).astype(f32)
    zero = np.zeros((N, 128 - QK_ROPE), f32)
    return jnp.asarray(np.concatenate([cos, cos, zero], axis=1)), jnp.asarray(np.concatenate([-sin, sin, zero], axis=1))


def kernel(x, meta_tokens, norm_g, w_in, q_norm_g, w_q_b, kv_norm_g, w_kv_b, pool_w, pool_scale, w_out, final_norm_g, loss_target, m_meta_tokens, m_norm_g, m_w_in, m_q_norm_g, m_w_q_b, m_kv_norm_g, m_w_kv_b, m_pool_w, m_pool_scale, m_w_out, m_final_norm_g, v_meta_tokens, v_norm_g, v_w_in, v_q_norm_g, v_w_q_b, v_kv_norm_g, v_w_kv_b, v_pool_w, v_pool_scale, v_w_out, v_final_norm_g):
    tr = lambda a: a[0].T
    win, wout, wq, wkv, h, tgt = _gather_weights(tr(w_in), w_out[0], tr(w_q_b), w_kv_b[0], meta_tokens, x[0], loss_target[0])
    cosf, sinf = _rope_tables()
    gf = final_norm_g.reshape(1, D)

    part = _local_step(h, tgt, norm_g, win, q_norm_g, wq, kv_norm_g, wkv, pool_w[0], pool_scale, wout, gf, cosf, sinf)

    pw2 = lambda a: a.reshape(len(POOL_WINDOWS) * POOL_GROUP, POOL_GROUP)
    gwinT, gwout, gwqT, gwkv, gmeta, gpw, gg, ggf, ggq, ggkv, gps, gloss = _reduce_grads(
        part["dwin"], part["dwout"], part["dwq"], part["dwkv"], part["dmeta"], pw2(part["dpw"]), part["dg"],
        part["dgf"], part["dgq"], part["dgkv"], part["dps"], part["loss"])

    r_in = _adamw_rows("adamw_w_in", tr(w_in), gwinT, tr(m_w_in), tr(v_w_in), 248)
    r_out = _adamw_rows("adamw_w_out", w_out[0], gwout, m_w_out[0], v_w_out[0], 128)
    fn2 = lambda a: a.reshape(1, D)
    r_meta, r_norm, r_gq, r_wq, r_gkv, r_wkv, r_pw, r_ps, r_fn = _adamw_small([
        (meta_tokens, gmeta, m_meta_tokens, v_meta_tokens),
        (norm_g, gg, m_norm_g, v_norm_g),
        (q_norm_g, ggq, m_q_norm_g, v_q_norm_g),
        (tr(w_q_b), gwqT, tr(m_w_q_b), tr(v_w_q_b)),
        (kv_norm_g, ggkv, m_kv_norm_g, v_kv_norm_g),
        (w_kv_b[0], gwkv, m_w_kv_b[0], v_w_kv_b[0]),
        (pw2(pool_w), gpw, pw2(m_pool_w), pw2(v_pool_w)),
        (pool_scale, gps, m_pool_scale, v_pool_scale),
        (fn2(final_norm_g), ggf, fn2(m_final_norm_g), fn2(v_final_norm_g)),
    ])
    untr = lambda a: a.T[None]
    pw4 = lambda a: a.reshape(1, len(POOL_WINDOWS), POOL_GROUP, POOL_GROUP)
    per_kind = [[
        r_meta[kind], r_norm[kind], untr(r_in[kind]), r_gq[kind], untr(r_wq[kind]), r_gkv[kind], r_wkv[kind][None],
        pw4(r_pw[kind]), r_ps[kind], r_out[kind][None], r_fn[kind].reshape(D),
    ] for kind in range(4)]
    return (gloss[0, 0], part["gx"][None], *per_kind[0], *per_kind[1], *per_kind[2], *per_kind[3])
```

```python
import jax
import jax.numpy as jnp
import numpy as np
from jax import lax
from jax.experimental import pallas as pl
from jax.experimental.pallas import tpu as pltpu

F32 = jnp.float32
BF16 = jnp.bfloat16

D = 1024
S = 2048
N_META = 16
PAD = 112
HEAD_ROWS = PAD + N_META
N = HEAD_ROWS + S
D_POOL = 512
POOL_WINDOWS = (2, 4, 8, 16)
POOL_GROUP = 128
HALO = 16
HEADS = 4
QK_NOPE = 128
QK_ROPE = 64
QK = QK_NOPE + QK_ROPE
V_HEAD = 128
Q_LORA = 256
KV_LORA = 128
D_IN = 1984
EPS = 1e-6
ROPE_THETA = 10000.0
SCALE = QK ** -0.5
CHIPS = 4

ROWS_FWD = 544
ROWS_MID = 544
ROWS_BWD = 544
TK = 128
TQ = 256
NQ = S // TQ
HEADS_PER_STEP_BWD = 2

O_PI, O_PG, O_CQ, O_CKV, O_KR, O_AG = 0, 512, 1024, 1280, 1408, 1472
O_KR_END = O_KR + 128
SHARD_IN = D_IN // CHIPS
SHARD_OUT = D // CHIPS

LR, B1, B2, ADAM_EPS, WD, STEP = 0.001, 0.9, 0.999, 1e-08, 0.01, 10
C1 = 1.0 - B1**STEP
C2 = 1.0 - B2**STEP

VMEM_LIMIT = 60 * 1024 * 1024
MESH = pl.DeviceIdType.MESH
NEG = -1e30

VEC_ROWS = 8
V_GQ, V_GKV, V_PS, V_LOSS = 0, 256, 384, 896


def _cparams(**kw):
    return pltpu.CompilerParams(vmem_limit_bytes=VMEM_LIMIT, **kw)


def _nt(a, b):
    return lax.dot_general(a, b, (((1,), (1,)), ((), ())), preferred_element_type=F32)


def _tn(a, b):
    return lax.dot_general(a, b, (((0,), (0,)), ((), ())), preferred_element_type=F32)


def _nn(a, b):
    return jnp.dot(a, b, preferred_element_type=F32)


def _swap64(t):
    return pltpu.roll(t, 32, 1) + pltpu.roll(t, 96, 1)


def _sigmoid(x):
    return 1.0 / (1.0 + jnp.exp(-x))


def _low_lanes():
    return (lax.broadcasted_iota(jnp.int32, (1, 128), 1) < QK_ROPE).astype(F32)


def _rows(w, rows):
    return pl.BlockSpec((rows, w), lambda i: (i, 0))


def _const(*shape):
    return pl.BlockSpec(shape, lambda *_: (0,) * len(shape), pipeline_mode=pl.Buffered(1))


STAT_GROUPS = HEADS // HEADS_PER_STEP_BWD


def _stat_slot(head):
    return head // HEADS_PER_STEP_BWD, head % HEADS_PER_STEP_BWD


def _attn_tiles():
    return [(0, TK, TK)] + [(TK + TQ * t, TQ, TK + TQ * (t + 1)) for t in range(NQ)]


def _masked_scores(q, k, rows, klen):
    s = _nt(q, k)
    col = lax.broadcasted_iota(jnp.int32, (1, TK), 1)
    head_bias = jnp.where(col >= PAD, 0.0, NEG)
    if klen == TK:
        return s + head_bias
    r = lax.broadcasted_iota(jnp.int32, (rows, 1), 0) >> 6
    c = lax.broadcasted_iota(jnp.int32, (1, rows), 1) >> 6
    diag_bias = jnp.where(c <= r, 0.0, NEG)
    parts = [s[:, 0:TK] + head_bias]
    if klen - rows > TK:
        parts.append(s[:, TK:klen - rows])
    parts.append(s[:, klen - rows:klen] + diag_bias)
    return jnp.concatenate(parts, axis=1)


def _fwd_in(h, norm_g, win, gq, wq, gkv, wkv, cosf, sinf):
    tr = ROWS_FWD

    def body(h_ref, g_ref, win_ref, gq_ref, wq_ref, gkv_ref, wkv_ref, cos_ref, sin_ref,
             pi_ref, pg_ref, cq_ref, ckv_ref, ag_ref, q_ref, k_ref, v_ref):
        h = h_ref[...]
        r = lax.rsqrt(jnp.mean(h * h, axis=-1, keepdims=True) + EPS)
        hn = ((h * r) * g_ref[...]).astype(BF16)
        u = _nt(hn, win_ref[0:O_KR_END, :])
        pi_ref[...] = u[:, O_PI:O_PG]
        pg_ref[...] = u[:, O_PG:O_CQ]
        cq = u[:, O_CQ:O_CKV]
        ckv = u[:, O_CKV:O_KR]
        cq_ref[...] = cq
        ckv_ref[...] = ckv
        ag_ref[...] = _nt(hn, win_ref[O_AG:D_IN, :])
        cosv = cos_ref[...]
        sinv = sin_ref[...]
        kr = u[:, O_KR:O_KR_END] * _low_lanes()
        kr = (kr * cosv + _swap64(kr) * sinv).astype(BF16)
        rq = lax.rsqrt(jnp.mean(cq * cq, axis=-1, keepdims=True) + EPS)
        cqn = ((cq * rq) * gq_ref[...]).astype(BF16)
        rkv = lax.rsqrt(jnp.mean(ckv * ckv, axis=-1, keepdims=True) + EPS)
        ckvn = ((ckv * rkv) * gkv_ref[...]).astype(BF16)
        for hd in range(HEADS):
            qh = _nt(cqn, wq_ref[hd]) * SCALE
            z = qh[:, QK_NOPE:]
            q_ref[hd, :, 0:QK_NOPE] = qh[:, 0:QK_NOPE].astype(BF16)
            q_ref[hd, :, QK_NOPE:] = (z * cosv + _swap64(z) * sinv).astype(BF16)
            kvh = _nn(ckvn, wkv_ref[hd])
            k_ref[hd, :, 0:QK_NOPE] = kvh[:, 0:QK_NOPE].astype(BF16)
            k_ref[hd, :, QK_NOPE:] = kr
            v_ref[hd] = kvh[:, QK_NOPE:].astype(BF16)

    head = lambda w: pl.BlockSpec((HEADS, tr, w), lambda i: (0, i, 0))
    return pl.pallas_call(
        body,
        name="fwd_in",
        grid=(N // tr,),
        in_specs=[
            _rows(D, tr), _const(1, D), _const(D_IN, D), _const(1, Q_LORA), _const(HEADS, 256, Q_LORA),
            _const(1, KV_LORA), _const(HEADS, KV_LORA, 256), _rows(128, tr), _rows(128, tr),
        ],
        out_specs=[_rows(D_POOL, tr), _rows(D_POOL, tr), _rows(Q_LORA, tr), _rows(KV_LORA, tr), _rows(D_POOL, tr),
                   head(256), head(256), head(V_HEAD)],
        out_shape=[
            jax.ShapeDtypeStruct((N, D_POOL), F32), jax.ShapeDtypeStruct((N, D_POOL), F32),
            jax.ShapeDtypeStruct((N, Q_LORA), F32), jax.ShapeDtypeStruct((N, KV_LORA), F32),
            jax.ShapeDtypeStruct((N, D_POOL), F32),
            jax.ShapeDtypeStruct((HEADS, N, 256), BF16), jax.ShapeDtypeStruct((HEADS, N, 256), BF16),
            jax.ShapeDtypeStruct((HEADS, N, V_HEAD), BF16),
        ],
        compiler_params=_cparams(dimension_semantics=("arbitrary",)),
    )(h, norm_g, win, gq, wq, gkv, wkv, cosf, sinf)


def _attn_fwd(q, k, v):
    tiles = _attn_tiles()
    n_t = len(tiles)

    def body(q_hbm, k_hbm, v_hbm, o_hbm, lse_ref, q_buf, k_buf, v_buf, o_buf, in_sems, out_sems):
        step = pl.program_id(0)

        def loads(idx):
            q0, rows, _ = tiles[idx]
            rs = pl.ds(q0, rows)
            return [pltpu.make_async_copy(src.at[:, rs, :], dst.at[:, rs, :], in_sems.at[a, idx % 2])
                    for a, (src, dst) in enumerate(((q_hbm, q_buf), (k_hbm, k_buf), (v_hbm, v_buf)))]

        def store(idx):
            q0, rows, _ = tiles[idx]
            return pltpu.make_async_copy(o_buf.at[idx % 2, pl.ds(0, rows), :], o_hbm.at[pl.ds(q0, rows), :],
                                         out_sems.at[idx % 2])

        @pl.when(step == 0)
        def _():
            lse_ref[...] = jnp.zeros_like(lse_ref)
            for cp in loads(0):
                cp.start()

        for idx, (q0, rows, klen) in enumerate(tiles):
            @pl.when(step == idx)
            def _(idx=idx, q0=q0, rows=rows, klen=klen):
                for cp in loads(idx):
                    cp.wait()
                if idx + 1 < n_t:
                    for cp in loads(idx + 1):
                        cp.start()
                if idx >= 2:
                    store(idx - 2).wait()
                for hd in range(HEADS):
                    s = _masked_scores(q_buf[hd, q0:q0 + rows, :], k_buf[hd, 0:klen, :], rows, klen)
                    m = jnp.max(s, axis=-1, keepdims=True)
                    p = jnp.exp(s - m)
                    l = jnp.sum(p, axis=-1, keepdims=True)
                    o_buf[idx % 2, 0:rows, hd * V_HEAD:(hd + 1) * V_HEAD] = _nn(p.astype(BF16), v_buf[hd, 0:klen, :]) / l
                    grp, lane = _stat_slot(hd)
                    lse_ref[grp, q0:q0 + rows, lane:lane + 1] = m + jnp.log(l)
                store(idx).start()
                if idx == n_t - 1:
                    store(idx - 1).wait()
                    store(idx).wait()

    hbm = pl.BlockSpec(memory_space=pl.ANY)
    return pl.pallas_call(
        body,
        name="attn_fwd",
        grid=(n_t,),
        in_specs=[hbm, hbm, hbm],
        out_specs=[hbm, _const(STAT_GROUPS, N, 128)],
        out_shape=[jax.ShapeDtypeStruct((N, HEADS * V_HEAD), F32), jax.ShapeDtypeStruct((STAT_GROUPS, N, 128), F32)],
        scratch_shapes=[pltpu.VMEM((HEADS, N, 256), BF16), pltpu.VMEM((HEADS, N, 256), BF16),
                        pltpu.VMEM((HEADS, N, V_HEAD), BF16), pltpu.VMEM((2, TQ, HEADS * V_HEAD), F32),
                        pltpu.SemaphoreType.DMA((3, 2)), pltpu.SemaphoreType.DMA((2,))],
        compiler_params=_cparams(dimension_semantics=("arbitrary",)),
    )(q, k, v)


def _inv_count(row0, rows, w):
    row = row0 + lax.broadcasted_iota(jnp.int32, (rows, 1), 0)
    return 1.0 / jnp.clip(row - (PAD - 1), 1, w).astype(F32)


def _mid(h, tgt, pool_in, pool_gate, attn_gate, attn, pool_w, pool_scale, wout, gf):
    tr = ROWS_MID
    per = tr // HALO
    ng = len(POOL_WINDOWS)

    def body(h_ref, t_ref, pin_ref, halo_ref, pg_ref, ag_ref, at_ref, pw_ref, ps_ref, wout_ref, gf_ref,
             dh2_ref, do_ref, delta_ref, dag_ref, dpg_ref, dpl_ref, dwout_ref, dpw_ref, dps_ref, dgf_ref, loss_ref):
        i = pl.program_id(0)

        @pl.when(i == 0)
        def _():
            dwout_ref[...] = jnp.zeros_like(dwout_ref)
            dpw_ref[...] = jnp.zeros_like(dpw_ref)
            dps_ref[...] = jnp.zeros_like(dps_ref)
            dgf_ref[...] = jnp.zeros_like(dgf_ref)
            loss_ref[...] = jnp.zeros_like(loss_ref)

        row0 = i * tr
        real = (row0 + lax.broadcasted_iota(jnp.int32, (tr, 1), 0)) >= HEAD_ROWS
        h = h_ref[...]

        halo = jnp.where(i > 0, halo_ref[...], 0.0)
        ext = jnp.concatenate([halo, pin_ref[...]], axis=0)
        pooled = []
        for g, w in enumerate(POOL_WINDOWS):
            e = ext[:, g * POOL_GROUP:(g + 1) * POOL_GROUP]
            acc = e
            shift = 1
            while shift < w:
                acc = acc + pltpu.roll(acc, shift, 0)
                shift *= 2
            pooled.append((acc[HALO:] * _inv_count(row0, tr, w) - e[HALO:]).astype(BF16))
        pw = [pw_ref[g].astype(BF16) for g in range(ng)]
        mixed = jnp.concatenate([_nn(pooled[g], pw[g]) for g in range(ng)], axis=1)
        ps = ps_ref[...]
        mixed_s = mixed * ps
        pg = pg_ref[...]
        sig_p = _sigmoid(pg)
        silu_p = pg * sig_p
        pool_out = (silu_p * mixed_s).astype(BF16)
        ag = ag_ref[...]
        sig_a = _sigmoid(ag)
        silu_a = ag * sig_a
        at = at_ref[...]
        attn_out = (silu_a * at).astype(BF16)
        mix = _nn(pool_out, wout_ref[0:D_POOL, :]) + _nn(attn_out, wout_ref[D_POOL:D, :])
        h2 = h + mix

        r2 = lax.rsqrt(jnp.mean(h2 * h2, axis=-1, keepdims=True) + EPS)
        n2 = h2 * r2
        gfv = gf_ref[...]
        err = jnp.where(real, n2 * gfv - t_ref[...], 0.0)
        loss_ref[...] += jnp.sum(jnp.sum(err * err, axis=-1, keepdims=True), axis=0, keepdims=True) * (0.5 / D)
        dy = err * (1.0 / D)
        dgf_ref[...] += jnp.sum(dy * n2, axis=0, keepdims=True)
        dn = dy * gfv
        dh2 = r2 * (dn - n2 * jnp.mean(dn * n2, axis=-1, keepdims=True))
        dh2_ref[...] = dh2
        dh2b = dh2.astype(BF16)

        dwout_ref[0:D_POOL, :] += _tn(pool_out, dh2b)
        dwout_ref[D_POOL:D, :] += _tn(attn_out, dh2b)
        dcat = _nt(dh2b, wout_ref[...])
        dpo = dcat[:, 0:D_POOL]
        dao = dcat[:, D_POOL:D]
        do = dao * silu_a
        prod = do * at
        delta_ref[...] = jnp.zeros_like(delta_ref)
        for hd in range(HEADS):
            grp, lane = _stat_slot(hd)
            cols = slice(hd * V_HEAD, (hd + 1) * V_HEAD)
            do_ref[grp, :, lane * V_HEAD:(lane + 1) * V_HEAD] = do[:, cols].astype(BF16)
            delta_ref[grp, :, lane:lane + 1] = jnp.sum(prod[:, cols], axis=-1, keepdims=True)
        dag_ref[...] = (dao * at * (sig_a * (1.0 + ag * (1.0 - sig_a)))).astype(BF16)
        dmixed_s = dpo * silu_p
        dpg_ref[...] = (dpo * mixed_s * (sig_p * (1.0 + pg * (1.0 - sig_p)))).astype(BF16)
        dps_ref[...] += jnp.sum(dmixed_s * mixed, axis=0, keepdims=True)
        dmixed = (dmixed_s * ps).astype(BF16)
        dpl = []
        for g in range(ng):
            dm = dmixed[:, g * POOL_GROUP:(g + 1) * POOL_GROUP]
            dpl.append(_nt(dm, pw[g]))
            dpw_ref[g] += _tn(pooled[g], dm)
        dpl_ref[...] = jnp.concatenate(dpl, axis=1)

    halo_spec = pl.BlockSpec((HALO, D_POOL), lambda i: (jnp.maximum(i * per - 1, 0), 0))
    return pl.pallas_call(
        body,
        name="mid",
        grid=(N // tr,),
        in_specs=[
            _rows(D, tr), _rows(D, tr), _rows(D_POOL, tr), halo_spec, _rows(D_POOL, tr), _rows(D_POOL, tr),
            _rows(D_POOL, tr), _const(ng, POOL_GROUP, POOL_GROUP), _const(1, D_POOL), _const(D, D), _const(1, D),
        ],
        out_specs=[
            _rows(D, tr), pl.BlockSpec((STAT_GROUPS, tr, HEADS_PER_STEP_BWD * V_HEAD), lambda i: (0, i, 0)),
            pl.BlockSpec((STAT_GROUPS, tr, 128), lambda i: (0, i, 0)),
            _rows(D_POOL, tr), _rows(D_POOL, tr), _rows(D_POOL, tr),
            _const(D, D), _const(ng, POOL_GROUP, POOL_GROUP), _const(1, D_POOL), _const(1, D), _const(1, 128),
        ],
        out_shape=[
            jax.ShapeDtypeStruct((N, D), F32), jax.ShapeDtypeStruct((STAT_GROUPS, N, HEADS_PER_STEP_BWD * V_HEAD), BF16),
            jax.ShapeDtypeStruct((STAT_GROUPS, N, 128), F32),
            jax.ShapeDtypeStruct((N, D_POOL), BF16), jax.ShapeDtypeStruct((N, D_POOL), BF16),
            jax.ShapeDtypeStruct((N, D_POOL), F32), jax.ShapeDtypeStruct((D, D), F32),
            jax.ShapeDtypeStruct((ng, POOL_GROUP, POOL_GROUP), F32),
            jax.ShapeDtypeStruct((1, D_POOL), F32), jax.ShapeDtypeStruct((1, D), F32), jax.ShapeDtypeStruct((1, 128), F32),
        ],
        compiler_params=_cparams(dimension_semantics=("arbitrary",)),
    )(h, tgt, pool_in, pool_in, pool_gate, attn_gate, attn, pool_w, pool_scale, wout, gf)


def _unrope(dy, cosv, sinv):
    return dy * cosv + _swap64(dy * sinv) * _low_lanes()


def _attn_bwd(q, k, v, do, lse, delta, cosf, sinf):
    tiles = _attn_tiles()
    hp = HEADS_PER_STEP_BWD
    n_g = HEADS // hp
    n_t = len(tiles)

    def body(q_hbm, k_hbm, v_hbm, do_hbm, lse_ref, delta_ref, cos_ref, sin_ref, dq_hbm, dkv_ref, dkr_ref,
             q_buf, k_buf, v_buf, do_buf, dq_buf, dk_acc, dv_acc, in_sems, out_sems):
        grp = pl.program_id(0)
        step = pl.program_id(1)
        heads = pl.ds(grp * hp, hp)

        def loads(g, idx):
            q0, rows, _ = tiles[idx]
            rs = pl.ds(q0, rows)
            par = (g * n_t + idx) % 2
            hs = pl.ds(g * hp, hp)
            pairs = ((q_hbm.at[hs, rs, :], q_buf.at[:, rs, :]), (k_hbm.at[hs, rs, :], k_buf.at[:, rs, :]),
                     (v_hbm.at[hs, rs, :], v_buf.at[:, rs, :]), (do_hbm.at[g, rs, :], do_buf.at[rs, :]))
            return [pltpu.make_async_copy(src, dst, in_sems.at[a, par]) for a, (src, dst) in enumerate(pairs)]

        def store(idx):
            q0, rows, _ = tiles[idx]
            return pltpu.make_async_copy(dq_buf.at[idx % 2, :, pl.ds(0, rows), :], dq_hbm.at[heads, pl.ds(q0, rows), :],
                                         out_sems.at[idx % 2])

        @pl.when(step == 0)
        def _():
            dk_acc[...] = jnp.zeros_like(dk_acc)
            dv_acc[...] = jnp.zeros_like(dv_acc)

        @pl.when((step == 0) & (grp == 0))
        def _():
            dkr_ref[...] = jnp.zeros_like(dkr_ref)
            for cp in loads(grp, 0):
                cp.start()

        for idx, (q0, rows, klen) in enumerate(tiles):
            @pl.when(step == idx)
            def _(idx=idx, q0=q0, rows=rows, klen=klen):
                for cp in loads(grp, idx):
                    cp.wait()
                if idx + 1 < n_t:
                    for cp in loads(grp, idx + 1):
                        cp.start()
                if idx >= 2:
                    store(idx - 2).wait()
                qs = pl.ds(q0, rows)
                for hd in range(hp):
                    qv = q_buf[hd, qs, :]
                    kv = k_buf[hd, 0:klen, :]
                    p = jnp.exp(_masked_scores(qv, kv, rows, klen) - lse_ref[0, qs, hd:hd + 1])
                    dob = do_buf[qs, hd * V_HEAD:(hd + 1) * V_HEAD]
                    ds = (p * (_nt(dob, v_buf[hd, 0:klen, :]) - delta_ref[0, qs, hd:hd + 1])).astype(BF16)
                    dq = _nn(ds, kv) * SCALE
                    dq_buf[idx % 2, hd, 0:rows, 0:QK_NOPE] = dq[:, 0:QK_NOPE].astype(BF16)
                    dq_buf[idx % 2, hd, 0:rows, QK_NOPE:] = _unrope(dq[:, QK_NOPE:], cos_ref[qs, :], sin_ref[qs, :]).astype(BF16)
                    dk_acc[hd, 0:klen, :] += _tn(ds, qv)
                    dv_acc[hd, 0:klen, :] += _tn(p.astype(BF16), dob)
                store(idx).start()

        @pl.when(step == n_t - 1)
        def _():
            @pl.when(grp + 1 < n_g)
            def _():
                for cp in loads(grp + 1, 0):
                    cp.start()

            for hd in range(hp):
                dkv_ref[hd, :, 0:QK_NOPE] = dk_acc[hd, :, 0:QK_NOPE].astype(BF16)
                dkv_ref[hd, :, QK_NOPE:] = dv_acc[hd].astype(BF16)
                dkr_ref[...] += dk_acc[hd, :, QK_NOPE:]
            store(n_t - 2).wait()
            store(n_t - 1).wait()

    hbm = pl.BlockSpec(memory_space=pl.ANY)
    stat = pl.BlockSpec((1, N, 128), lambda g, t: (g, 0, 0), pipeline_mode=pl.Buffered(1))
    return pl.pallas_call(
        body,
        name="attn_bwd",
        grid=(n_g, n_t),
        in_specs=[hbm, hbm, hbm, hbm, stat, stat, _const(N, 128), _const(N, 128)],
        out_specs=[hbm, pl.BlockSpec((hp, N, 256), lambda g, t: (g, 0, 0), pipeline_mode=pl.Buffered(1)), _const(N, 128)],
        out_shape=[
            jax.ShapeDtypeStruct((HEADS, N, 256), BF16), jax.ShapeDtypeStruct((HEADS, N, 256), BF16),
            jax.ShapeDtypeStruct((N, 128), F32),
        ],
        scratch_shapes=[pltpu.VMEM((hp, N, 256), BF16), pltpu.VMEM((hp, N, 256), BF16), pltpu.VMEM((hp, N, V_HEAD), BF16),
                        pltpu.VMEM((N, hp * V_HEAD), BF16), pltpu.VMEM((2, hp, TQ, 256), BF16),
                        pltpu.VMEM((hp, N, 256), F32), pltpu.VMEM((hp, N, V_HEAD), F32),
                        pltpu.SemaphoreType.DMA((4, 2)), pltpu.SemaphoreType.DMA((2,))],
        compiler_params=_cparams(dimension_semantics=("arbitrary", "arbitrary")),
    )(q, k, v, do, lse, delta, cosf, sinf)


def _bwd_in(h, dh2, dq, dkv, dkr, cq, ckv, dpl, dpg, dag, norm_g, win, gq, wq, gkv, wkv, cosf, sinf):
    tr = ROWS_BWD
    nb = N // tr
    per = tr // HALO
    lead = HEAD_ROWS

    def body(h_ref, dh2_ref, dq_ref, dkv_ref, dkr_ref, cq_ref, ckv_ref, dpl_ref, halo_ref, dpg_ref, dag_ref,
             g_ref, win_ref, gq_ref, wq_ref, gkv_ref, wkv_ref, cos_ref, sin_ref,
             gx_ref, dmeta_ref, dwin_ref, dwq_ref, dwkv_ref, dg_ref, dgq_ref, dgkv_ref, dh_buf, gx_sem):
        i = pl.program_id(0)

        @pl.when(i == 0)
        def _():
            dwin_ref[...] = jnp.zeros_like(dwin_ref)
            dwq_ref[...] = jnp.zeros_like(dwq_ref)
            dwkv_ref[...] = jnp.zeros_like(dwkv_ref)
            dg_ref[...] = jnp.zeros_like(dg_ref)
            dgq_ref[...] = jnp.zeros_like(dgq_ref)
            dgkv_ref[...] = jnp.zeros_like(dgkv_ref)

        row0 = i * tr
        h = h_ref[...]
        r = lax.rsqrt(jnp.mean(h * h, axis=-1, keepdims=True) + EPS)
        n = h * r
        gv = g_ref[...]
        hn = (n * gv).astype(BF16)
        cq = cq_ref[...]
        rq = lax.rsqrt(jnp.mean(cq * cq, axis=-1, keepdims=True) + EPS)
        nq = cq * rq
        gqv = gq_ref[...]
        cqn = (nq * gqv).astype(BF16)
        dcqn = jnp.zeros((tr, Q_LORA), F32)
        for hd in range(HEADS):
            dqf = dq_ref[hd]
            dcqn = dcqn + _nn(dqf, wq_ref[hd])
            dwq_ref[hd] += _tn(dqf, cqn)
        dgq_ref[...] += jnp.sum(dcqn * nq, axis=0, keepdims=True)
        dnq = dcqn * gqv
        dcq = rq * (dnq - nq * jnp.mean(dnq * nq, axis=-1, keepdims=True))

        ckv = ckv_ref[...]
        rkv = lax.rsqrt(jnp.mean(ckv * ckv, axis=-1, keepdims=True) + EPS)
        nkv = ckv * rkv
        gkvv = gkv_ref[...]
        ckvn = (nkv * gkvv).astype(BF16)
        dckvn = jnp.zeros((tr, KV_LORA), F32)
        for hd in range(HEADS):
            dkv = dkv_ref[hd]
            dckvn = dckvn + _nt(dkv, wkv_ref[hd])
            dwkv_ref[hd] += _tn(ckvn, dkv)
        dgkv_ref[...] += jnp.sum(dckvn * nkv, axis=0, keepdims=True)
        dnkv = dckvn * gkvv
        dckv = rkv * (dnkv - nkv * jnp.mean(dnkv * nkv, axis=-1, keepdims=True))
        dkr = _unrope(dkr_ref[...], cos_ref[...], sin_ref[...])

        cur = dpl_ref[...]
        halo = jnp.where(i < nb - 1, halo_ref[...], 0.0)
        dpi = []
        for g, w in enumerate(POOL_WINDOWS):
            sl = slice(g * POOL_GROUP, (g + 1) * POOL_GROUP)
            a = jnp.concatenate([cur[:, sl] * _inv_count(row0, tr, w), halo[:, sl] * _inv_count(row0 + tr, HALO, w)], axis=0)
            acc = a
            shift = 1
            while shift < w:
                acc = acc + pltpu.roll(acc, tr + HALO - shift, 0)
                shift *= 2
            dpi.append(acc[0:tr] - cur[:, sl])

        du = jnp.concatenate([t.astype(BF16) for t in dpi] + [dpg_ref[...]] + [t.astype(BF16) for t in (dcq, dckv, dkr)],
                             axis=1)
        dagb = dag_ref[...]
        dwin_ref[0:O_KR_END, :] += _tn(du, hn)
        dwin_ref[O_AG:D_IN, :] += _tn(dagb, hn)
        dhn = _nn(du, win_ref[0:O_KR_END, :]) + _nn(dagb, win_ref[O_AG:D_IN, :])
        dg_ref[...] += jnp.sum(dhn * n, axis=0, keepdims=True)
        dn = dhn * gv
        dh = dh2_ref[...] + r * (dn - n * jnp.mean(dn * n, axis=-1, keepdims=True))

        first = pltpu.make_async_copy(dh_buf.at[pl.ds(lead, tr - lead), :], gx_ref.at[pl.ds(0, tr - lead), :], gx_sem)
        later = lambda step: pltpu.make_async_copy(
            dh_buf, gx_ref.at[pl.ds(pl.multiple_of(step * tr - lead, 16), tr), :], gx_sem)

        @pl.when(i == 1)
        def _():
            first.wait()

        @pl.when(i > 1)
        def _():
            later(i - 1).wait()

        dh_buf[...] = dh

        @pl.when(i == 0)
        def _():
            first.start()
            for chip in range(CHIPS):
                dmeta_ref[chip] = dh[PAD:HEAD_ROWS, chip * 256:(chip + 1) * 256]

        @pl.when(i > 0)
        def _():
            later(i).start()

        @pl.when(i == nb - 1)
        def _():
            later(i).wait()

    head = lambda w: pl.BlockSpec((HEADS, tr, w), lambda i: (0, i, 0))
    halo_spec = pl.BlockSpec((HALO, D_POOL), lambda i: (jnp.minimum((i + 1) * per, N // HALO - 1), 0))
    return pl.pallas_call(
        body,
        name="bwd_in",
        grid=(nb,),
        in_specs=[
            _rows(D, tr), _rows(D, tr), head(256), head(256), _rows(128, tr), _rows(Q_LORA, tr), _rows(KV_LORA, tr),
            _rows(D_POOL, tr), halo_spec, _rows(D_POOL, tr), _rows(D_POOL, tr),
            _const(1, D), _const(D_IN, D), _const(1, Q_LORA), _const(HEADS, 256, Q_LORA),
            _const(1, KV_LORA), _const(HEADS, KV_LORA, 256), _rows(128, tr), _rows(128, tr),
        ],
        out_specs=[
            pl.BlockSpec(memory_space=pl.ANY), _const(CHIPS, N_META, 256), _const(D_IN, D), _const(HEADS, 256, Q_LORA),
            _const(HEADS, KV_LORA, 256), _const(1, D), _const(1, Q_LORA), _const(1, KV_LORA),
        ],
        out_shape=[
            jax.ShapeDtypeStruct((S, D), F32), jax.ShapeDtypeStruct((CHIPS, N_META, 256), F32),
            jax.ShapeDtypeStruct((D_IN, D), F32), jax.ShapeDtypeStruct((HEADS, 256, Q_LORA), F32),
            jax.ShapeDtypeStruct((HEADS, KV_LORA, 256), F32),
            jax.ShapeDtypeStruct((1, D), F32), jax.ShapeDtypeStruct((1, Q_LORA), F32), jax.ShapeDtypeStruct((1, KV_LORA), F32),
        ],
        scratch_shapes=[pltpu.VMEM((tr, D), F32), pltpu.SemaphoreType.DMA],
        compiler_params=_cparams(dimension_semantics=("arbitrary",)),
    )(h, dh2, dq, dkv, dkr, cq, ckv, dpl, dpl, dpg, dag, norm_g, win, gq, wq, gkv, wkv, cosf, sinf)


def _local_step(h, tgt, norm_g, win, gq, wq, gkv, wkv, pool_w, pool_scale, wout, gf, cosf, sinf):
    pool_in, pool_gate, cq, ckv, attn_gate, q, k, v = _fwd_in(h, norm_g, win, gq, wq, gkv, wkv, cosf, sinf)
    attn, lse = _attn_fwd(q, k, v)
    dh2, do, delta, dag, dpg, dpl, dwout, dpw, dps, dgf, loss = _mid(
        h, tgt, pool_in, pool_gate, attn_gate, attn, pool_w, pool_scale, wout, gf)
    dq, dkv, dkr = _attn_bwd(q, k, v, do, lse, delta, cosf, sinf)
    gx, dmeta, dwin, dwq, dwkv, dg, dgq, dgkv = _bwd_in(
        h, dh2, dq, dkv, dkr, cq, ckv, dpl, dpg, dag, norm_g, win, gq, wq, gkv, wkv, cosf, sinf)
    return dict(gx=gx, dmeta=dmeta, dwin=dwin, dwq=dwq, dwkv=dwkv, dwout=dwout, dg=dg, dgq=dgq, dgkv=dgkv,
                dpw=dpw, dps=dps, dgf=dgf, loss=loss)


_CHIP_RELS = ((0, 0), (1, 0), (0, 1), (1, 1))

_ARR_ROWS = (SHARD_IN, SHARD_OUT, 256, KV_LORA, N_META)
_ARR_COLS = (D, D, Q_LORA, 256, 256)
_PIECES = (
    (0, 0, 256, 0), (0, 256, SHARD_IN - 256, 1),
    (1, 0, 128, 0), (1, 128, 128, 1),
    (2, 0, 128, 0), (2, 128, 128, 1),
    (3, 0, 64, 0), (3, 64, 64, 1),
    (4, 0, N_META, 0),
)
_NP = len(_PIECES)
_PIECE_MAX = (256, 128, 128, 64, N_META)


def _gathered_at(refs, arr, chip, r0, n):
    if arr in (0, 1):
        return refs[arr].at[pl.ds(pl.multiple_of(_ARR_ROWS[arr] * chip + r0, 16), n), :]
    return refs[arr].at[chip, pl.ds(r0, n), :]


def _remote(src, dst, send_sem, recv_sem, to):
    return pltpu.make_async_remote_copy(src_ref=src, dst_ref=dst, send_sem=send_sem, recv_sem=recv_sem,
                                        device_id=to, device_id_type=MESH)


def _gather_weights(winT_s, wout_s, wqT_s, wkv_s, meta_s, x2, tgt2):
    def body(win_ref, wout_ref, wq_ref, wkv_ref, meta_ref, x_ref, t_ref, win_o, wout_o, wq_o, wkv_o, h_o, tp_o,
             s_win, s_wout, s_wq, s_wkv, meta_all, head_buf, x_buf, t_buf, ici_send, ici_recv, fwd_send, fwd_recv,
             loc_sems, own_sems):
        x, y, c = lax.axis_index("x"), lax.axis_index("y"), lax.axis_index("c")
        me = 2 * x + y
        stage = (s_win, s_wout, s_wq, s_wkv, meta_ref)
        outs = (win_o, wout_o, wq_o, wkv_o, meta_all)

        frames = pl.ds(HEAD_ROWS, S)
        loads = [pltpu.make_async_copy(x_ref, x_buf, loc_sems.at[0]), pltpu.make_async_copy(t_ref, t_buf, loc_sems.at[1])]
        local = [pltpu.make_async_copy(x_buf, h_o.at[frames, :], loc_sems.at[0]),
                 pltpu.make_async_copy(t_buf, tp_o.at[frames, :], loc_sems.at[1])]
        for cp in loads:
            cp.start()

        s_win[...] = win_ref[...].astype(BF16)
        s_wout[...] = wout_ref[...].astype(BF16)
        s_wq[0:QK, :] = wq_ref[...].astype(BF16)
        s_wq[QK:256, :] = jnp.zeros((256 - QK, Q_LORA), BF16)
        s_wkv[...] = wkv_ref[...].astype(BF16)

        def chip_of(rel):
            fx, fy = _CHIP_RELS[rel]
            return 2 * (x ^ fx) + (y ^ fy)

        def same_core_of(rel):
            fx, fy = _CHIP_RELS[rel]
            return (x ^ fx, y ^ fy, c)

        def ici_copy(rel, i, src_chip, to):
            arr, r0, n, _ = _PIECES[i]
            k = (rel - 1) * _NP + i
            return _remote(stage[arr].at[pl.ds(r0, n), :], _gathered_at(outs, arr, src_chip, r0, n),
                           ici_send.at[k], ici_recv.at[k], to)

        def fwd_copy(rel, i, to):
            arr, r0, n, _ = _PIECES[i]
            k = (rel - 1) * _NP + i
            place = _gathered_at(outs, arr, chip_of(rel), r0, n)
            return _remote(place, place, fwd_send.at[k], fwd_recv.at[k], to)

        for core in (0, 1):
            @pl.when(c == core)
            def _(core=core):
                mine = [i for i in range(_NP) if _PIECES[i][3] == core]
                theirs = [i for i in range(_NP) if _PIECES[i][3] != core]
                sends = [ici_copy(rel, i, me, same_core_of(rel)) for rel in (1, 2, 3) for i in mine]
                for cp in sends:
                    cp.start()
                for ld, st in zip(loads, local):
                    ld.wait()
                    st.start()
                own = [pltpu.make_async_copy(stage[arr], _gathered_at(outs, arr, me, 0, _ARR_ROWS[arr]), own_sems.at[arr])
                       for arr in range(4)]
                for cp in own:
                    cp.start()
                meta_all[me] = meta_ref[...]
                for rel in (1, 2, 3):
                    for i in mine:
                        ici_copy(rel, i, chip_of(rel), (x, y, c)).wait_recv()
                        fwd = fwd_copy(rel, i, (x, y, 1 - c))
                        fwd.start()
                        sends.append(fwd)
                for rel in (1, 2, 3):
                    for i in theirs:
                        fwd_copy(rel, i, (x, y, c)).wait_recv()
                for cp in sends:
                    cp.wait_send()
                for cp in own:
                    cp.wait()

        head_buf[...] = jnp.zeros_like(head_buf)
        zeros = pltpu.make_async_copy(head_buf, tp_o.at[pl.ds(0, HEAD_ROWS), :], loc_sems.at[2])
        zeros.start()
        zeros.wait()
        for chip in range(CHIPS):
            head_buf[PAD:HEAD_ROWS, chip * 256:(chip + 1) * 256] = meta_all[chip]
        head = pltpu.make_async_copy(head_buf, h_o.at[pl.ds(0, HEAD_ROWS), :], loc_sems.at[2])
        head.start()
        head.wait()
        for cp in local:
            cp.wait()

    vm = pl.BlockSpec(memory_space=pltpu.VMEM)
    hbm = pl.BlockSpec(memory_space=pl.ANY)
    return pl.pallas_call(
        body,
        name="gather_weights",
        in_specs=[vm] * 5 + [hbm] * 2,
        out_specs=[hbm] * 6,
        out_shape=[
            jax.ShapeDtypeStruct((D_IN, D), BF16), jax.ShapeDtypeStruct((D, D), BF16),
            jax.ShapeDtypeStruct((CHIPS, 256, Q_LORA), BF16), jax.ShapeDtypeStruct((CHIPS, KV_LORA, 256), BF16),
            jax.ShapeDtypeStruct((N, D), F32), jax.ShapeDtypeStruct((N, D), F32),
        ],
        scratch_shapes=[pltpu.VMEM((_ARR_ROWS[a], _ARR_COLS[a]), BF16) for a in range(4)]
        + [pltpu.VMEM((CHIPS, N_META, 256), F32), pltpu.VMEM((HEAD_ROWS, D), F32), pltpu.VMEM((S, D), F32),
           pltpu.VMEM((S, D), F32)]
        + [pltpu.SemaphoreType.DMA((3 * _NP,))] * 4 + [pltpu.SemaphoreType.DMA((3,)), pltpu.SemaphoreType.DMA((4,))],
        compiler_params=_cparams(),
    )(winT_s, wout_s, wqT_s, wkv_s, meta_s, x2, tgt2)


_SM_ROWS = (len(POOL_WINDOWS) * POOL_GROUP, VEC_ROWS)
_SM_COLS = (POOL_GROUP, D)
_SM_PIECES = ((0, 0, 256, 0), (0, 256, 256, 1), (1, 0, VEC_ROWS, 0))
_NSP = len(_SM_PIECES)


def _reduce_grads(dwin, dwout, dwq, dwkv, dmeta4, dpw, dg, dgf, dgq, dgkv, dps, loss):
    def body(dwin_ref, dwout_ref, dwq_ref, dwkv_ref, dmeta_ref, dpw_ref, dg_ref, dgf_ref, dgq_ref, dgkv_ref, dps_ref,
             loss_ref, gwin_o, gwout_o, gwq_o, gwkv_o, gmeta_o, gpw_o, gg_o, ggf_o, ggq_o, ggkv_o, gps_o, gloss_o,
             ow0, ow1, ow2, ow3, ow4, sb0, sb1, sb2, sb3, sb4, st0, st1, st2, st3, st4, rc0, rc1, rc2, rc3, rc4,
             vec, sm_sb0, sm_sb1, sm_cs0, sm_cs1, sm_rc0, sm_rc1, vec_fin,
             own_sems, d2d_send, d2d_recv, ici_send, ici_recv, fin_send, fin_recv,
             swap_send, swap_recv, smi_send, smi_recv, smf_send, smf_recv):
        x, y, c = lax.axis_index("x"), lax.axis_index("y"), lax.axis_index("c")
        me = 2 * x + y
        grads = (dwin_ref, dwout_ref, dwq_ref, dwkv_ref, dmeta_ref)
        outs = (gwin_o, gwout_o, gwq_o, gwkv_o, gmeta_o)
        own_buf = (ow0, ow1, ow2, ow3, ow4)
        sib_buf = (sb0, sb1, sb2, sb3, sb4)
        stage = (st0, st1, st2, st3, st4)
        recv = (rc0, rc1, rc2, rc3, rc4)
        sm_mine = (dpw_ref, vec)
        sm_sib = (sm_sb0, sm_sb1)
        sm_chip = (sm_cs0, sm_cs1)
        sm_recv = (sm_rc0, sm_rc1)
        sm_out = (gpw_o, vec_fin)
        sibling = (x, y, 1 - c)

        def chip_of(rel):
            fx, fy = _CHIP_RELS[rel]
            return 2 * (x ^ fx) + (y ^ fy)

        def same_core_of(rel):
            fx, fy = _CHIP_RELS[rel]
            return (x ^ fx, y ^ fy, c)

        def slot(bufs, i, idx):
            arr, _, n, _ = _PIECES[i]
            return bufs[arr].at[idx, pl.ds(0, n), :]

        def own_load(rel, i):
            arr, r0, n, _ = _PIECES[i]
            return pltpu.make_async_copy(_gathered_at(grads, arr, chip_of(rel), r0, n), slot(own_buf, i, rel),
                                         own_sems.at[rel * _NP + i])

        def d2d_copy(rel, i):
            arr, r0, n, _ = _PIECES[i]
            k = rel * _NP + i
            return _remote(_gathered_at(grads, arr, chip_of(rel), r0, n), slot(sib_buf, i, rel),
                           d2d_send.at[k], d2d_recv.at[k], sibling)

        def ici_copy(rel, i):
            k = (rel - 1) * _NP + i
            return _remote(slot(stage, i, rel - 1), slot(recv, i, rel - 1), ici_send.at[k], ici_recv.at[k],
                           same_core_of(rel))

        def fin_copy(i):
            arr, r0, n, _ = _PIECES[i]
            place = outs[arr].at[pl.ds(r0, n), :]
            return _remote(place, place, fin_send.at[i], fin_recv.at[i], sibling)

        def sm_ici_copy(rel, j):
            blk, r0, n, _ = _SM_PIECES[j]
            k = (rel - 1) * _NSP + j
            return _remote(sm_chip[blk].at[pl.ds(r0, n), :], sm_recv[blk].at[rel - 1, pl.ds(r0, n), :],
                           smi_send.at[k], smi_recv.at[k], same_core_of(rel))

        def sm_fin_copy(j):
            blk, r0, n, _ = _SM_PIECES[j]
            place = sm_out[blk].at[pl.ds(r0, n), :]
            return _remote(place, place, smf_send.at[j], smf_recv.at[j], sibling)

        vec[...] = jnp.zeros_like(vec)
        vec[0:1, :] = dg_ref[...]
        vec[1:2, :] = dgf_ref[...]
        vec[2:3, V_GQ:V_GQ + Q_LORA] = dgq_ref[...]
        vec[2:3, V_GKV:V_GKV + KV_LORA] = dgkv_ref[...]
        vec[2:3, V_PS:V_PS + D_POOL] = dps_ref[...]
        vec[2:3, V_LOSS:D] = loss_ref[...]
        swaps = [_remote(sm_mine[b], sm_sib[b], swap_send.at[b], swap_recv.at[b], sibling) for b in (0, 1)]
        for cp in swaps:
            cp.start()

        for core in (0, 1):
            @pl.when(c == core)
            def _(core=core):
                mine = [i for i in range(_NP) if _PIECES[i][3] == core]
                theirs = [i for i in range(_NP) if _PIECES[i][3] != core]
                sm_mine_p = [j for j in range(_NSP) if _SM_PIECES[j][3] == core]
                sm_theirs_p = [j for j in range(_NSP) if _SM_PIECES[j][3] != core]
                sends = list(swaps)

                for rel in (1, 2, 3, 0):
                    for i in theirs:
                        cp = d2d_copy(rel, i)
                        cp.start()
                        sends.append(cp)
                    for i in mine:
                        own_load(rel, i).start()

                for b in (0, 1):
                    swaps[b].wait_recv()
                    sm_chip[b][...] = sm_mine[b][...] + sm_sib[b][...]
                for rel in (1, 2, 3):
                    for j in sm_mine_p:
                        cp = sm_ici_copy(rel, j)
                        cp.start()
                        sends.append(cp)

                for rel in (1, 2, 3):
                    for i in mine:
                        arr, r0, n, _ = _PIECES[i]
                        own_load(rel, i).wait()
                        d2d_copy(rel, i).wait_recv()
                        total = slot(own_buf, i, rel)[...] + slot(sib_buf, i, rel)[...]
                        slot(stage, i, rel - 1)[...] = total.astype(stage[arr].dtype)
                        cp = ici_copy(rel, i)
                        cp.start()
                        sends.append(cp)

                for i in mine:
                    arr, r0, n, _ = _PIECES[i]
                    own_load(0, i).wait()
                    d2d_copy(0, i).wait_recv()
                    total = slot(own_buf, i, 0)[...] + slot(sib_buf, i, 0)[...]
                    for rel in (1, 2, 3):
                        ici_copy(rel, i).wait_recv()
                        total = total + slot(recv, i, rel - 1)[...].astype(F32)
                    outs[arr][pl.ds(r0, n), :] = total
                    cp = fin_copy(i)
                    cp.start()
                    sends.append(cp)

                for j in sm_mine_p:
                    blk, r0, n, _ = _SM_PIECES[j]
                    for rel in (1, 2, 3):
                        sm_ici_copy(rel, j).wait_recv()
                    total = jnp.zeros((n, _SM_COLS[blk]), F32)
                    for chip in range(CHIPS):
                        flips = chip ^ me
                        rel = jnp.where(flips == 2, 1, jnp.where(flips == 1, 2, flips))
                        theirs_rows = sm_recv[blk][jnp.maximum(rel - 1, 0), pl.ds(r0, n), :]
                        total = total + jnp.where(rel == 0, sm_chip[blk][pl.ds(r0, n), :], theirs_rows)
                    sm_out[blk][pl.ds(r0, n), :] = total
                    cp = sm_fin_copy(j)
                    cp.start()
                    sends.append(cp)

                for i in theirs:
                    fin_copy(i).wait_recv()
                for j in sm_theirs_p:
                    sm_fin_copy(j).wait_recv()
                for cp in sends:
                    cp.wait_send()

        gg_o[...] = vec_fin[0:1, :]
        ggf_o[...] = vec_fin[1:2, :]
        ggq_o[...] = vec_fin[2:3, V_GQ:V_GQ + Q_LORA]
        ggkv_o[...] = vec_fin[2:3, V_GKV:V_GKV + KV_LORA]
        gps_o[...] = vec_fin[2:3, V_PS:V_PS + D_POOL]
        gloss_o[...] = vec_fin[2:3, V_LOSS:D]

    vm = pl.BlockSpec(memory_space=pltpu.VMEM)
    piece_buf = lambda lead, dtype: [pltpu.VMEM((lead, _PIECE_MAX[a], _ARR_COLS[a]), F32 if a == 4 else dtype)
                                     for a in range(5)]
    sm_buf = lambda *lead: [pltpu.VMEM(lead + (_SM_ROWS[b], _SM_COLS[b]), F32) for b in (0, 1)]
    dma = lambda n: [pltpu.SemaphoreType.DMA((n,))] * 2
    return pl.pallas_call(
        body,
        name="reduce_grads",
        in_specs=[pl.BlockSpec(memory_space=pl.ANY)] * 4 + [vm] * 8,
        out_specs=[vm] * 12,
        out_shape=[jax.ShapeDtypeStruct((_ARR_ROWS[a], _ARR_COLS[a]), F32) for a in range(5)]
        + [jax.ShapeDtypeStruct((_SM_ROWS[0], _SM_COLS[0]), F32), jax.ShapeDtypeStruct((1, D), F32),
           jax.ShapeDtypeStruct((1, D), F32), jax.ShapeDtypeStruct((1, Q_LORA), F32),
           jax.ShapeDtypeStruct((1, KV_LORA), F32), jax.ShapeDtypeStruct((1, D_POOL), F32),
           jax.ShapeDtypeStruct((1, 128), F32)],
        scratch_shapes=piece_buf(CHIPS, F32) + piece_buf(CHIPS, F32) + piece_buf(3, BF16) + piece_buf(3, BF16)
        + [pltpu.VMEM((VEC_ROWS, D), F32)] + sm_buf() + sm_buf() + sm_buf(3) + [pltpu.VMEM((VEC_ROWS, D), F32)]
        + [pltpu.SemaphoreType.DMA((CHIPS * _NP,))]
        + dma(CHIPS * _NP) + dma(3 * _NP) + dma(_NP) + dma(2) + dma(3 * _NSP) + dma(_NSP),
        compiler_params=_cparams(),
    )(dwin, dwout, dwq, dwkv, dmeta4, dpw, dg, dgf, dgq, dgkv, dps, loss)


def _adamw_math(w, g, m, v):
    m = B1 * m + (1.0 - B1) * g
    v = B2 * v + (1.0 - B2) * (g * g)
    m_hat = m / C1
    v_hat = v / C2
    delta = -LR * (m_hat / (jnp.sqrt(v_hat) + ADAM_EPS) + WD * w)
    return delta, m, v


def _adamw_rows(name, w, g, m, v, block_rows):
    rows, cols = w.shape

    def body(w_ref, g_ref, m_ref, v_ref, go_ref, d_ref, nm_ref, nv_ref):
        g = g_ref[...]
        go_ref[...] = g
        d_ref[...], nm_ref[...], nv_ref[...] = _adamw_math(w_ref[...], g, m_ref[...], v_ref[...])

    spec = pl.BlockSpec((block_rows, cols), lambda i: (i, 0))
    return pl.pallas_call(
        body,
        name=name,
        grid=(rows // block_rows,),
        in_specs=[spec] * 4,
        out_specs=[spec] * 4,
        out_shape=[jax.ShapeDtypeStruct(w.shape, F32)] * 4,
        compiler_params=_cparams(dimension_semantics=("arbitrary",)),
    )(w, g, m, v)


def _adamw_small(groups):
    n = len(groups)

    def body(*refs):
        ins, outs = refs[:4 * n], refs[4 * n:]
        for t in range(n):
            w_ref, g_ref, m_ref, v_ref = ins[4 * t:4 * t + 4]
            g = g_ref[0:w_ref.shape[0], :]
            outs[4 * t][...] = g
            outs[4 * t + 1][...], outs[4 * t + 2][...], outs[4 * t + 3][...] = _adamw_math(
                w_ref[...], g, m_ref[...], v_ref[...])

    vm = pl.BlockSpec(memory_space=pltpu.VMEM)
    flat = [a for grp in groups for a in grp]
    outs = pl.pallas_call(
        body,
        name="adamw_small",
        in_specs=[vm] * (4 * n),
        out_specs=[vm] * (4 * n),
        out_shape=[jax.ShapeDtypeStruct(grp[0].shape, F32) for grp in groups for _ in range(4)],
        compiler_params=_cparams(),
    )(*flat)
    return [tuple(outs[4 * t:4 * t + 4]) for t in range(n)]


def _rope_tables():
    half = QK_ROPE // 2
    f32 = np.float32
    inv_freq = (f32(1.0) / (f32(ROPE_THETA) ** (np.arange(half, dtype=f32) / f32(half)))).astype(f32)
    pos = np.arange(N, dtype=f32) - f32(PAD)
    ang = (pos[:, None] * inv_freq[None, :]).astype(f32)
    cos, sin = np.cos(ang).astype(f32), np.sin(ang).astype(f32)
    zero = np.zeros((N, 128 - QK_ROPE), f32)
    return jnp.asarray(np.concatenate([cos, cos, zero], axis=1)), jnp.asarray(np.concatenate([-sin, sin, zero], axis=1))


def kernel(x, meta_tokens, norm_g, w_in, q_norm_g, w_q_b, kv_norm_g, w_kv_b, pool_w, pool_scale, w_out, final_norm_g, loss_target, m_meta_tokens, m_norm_g, m_w_in, m_q_norm_g, m_w_q_b, m_kv_norm_g, m_w_kv_b, m_pool_w, m_pool_scale, m_w_out, m_final_norm_g, v_meta_tokens, v_norm_g, v_w_in, v_q_norm_g, v_w_q_b, v_kv_norm_g, v_w_kv_b, v_pool_w, v_pool_scale, v_w_out, v_final_norm_g):
    tr = lambda a: a[0].T
    win, wout, wq, wkv, h, tgt = _gather_weights(tr(w_in), w_out[0], tr(w_q_b), w_kv_b[0], meta_tokens, x[0], loss_target[0])
    cosf, sinf = _rope_tables()
    gf = final_norm_g.reshape(1, D)

    part = _local_step(h, tgt, norm_g, win, q_norm_g, wq, kv_norm_g, wkv, pool_w[0], pool_scale, wout, gf, cosf, sinf)

    pw2 = lambda a: a.reshape(len(POOL_WINDOWS) * POOL_GROUP, POOL_GROUP)
    gwinT, gwout, gwqT, gwkv, gmeta, gpw, gg, ggf, ggq, ggkv, gps, gloss = _reduce_grads(
        part["dwin"], part["dwout"], part["dwq"], part["dwkv"], part["dmeta"], pw2(part["dpw"]), part["dg"],
        part["dgf"], part["dgq"], part["dgkv"], part["dps"], part["loss"])

    r_in = _adamw_rows("adamw_w_in", tr(w_in), gwinT, tr(m_w_in), tr(v_w_in), 248)
    r_out = _adamw_rows("adamw_w_out", w_out[0], gwout, m_w_out[0], v_w_out[0], 128)
    fn2 = lambda a: a.reshape(1, D)
    r_meta, r_norm, r_gq, r_wq, r_gkv, r_wkv, r_pw, r_ps, r_fn = _adamw_small([
        (meta_tokens, gmeta, m_meta_tokens, v_meta_tokens),
        (norm_g, gg, m_norm_g, v_norm_g),
        (q_norm_g, ggq, m_q_norm_g, v_q_norm_g),
        (tr(w_q_b), gwqT, tr(m_w_q_b), tr(v_w_q_b)),
        (kv_norm_g, ggkv, m_kv_norm_g, v_kv_norm_g),
        (w_kv_b[0], gwkv, m_w_kv_b[0], v_w_kv_b[0]),
        (pw2(pool_w), gpw, pw2(m_pool_w), pw2(v_pool_w)),
        (pool_scale, gps, m_pool_scale, v_pool_scale),
        (fn2(final_norm_g), ggf, fn2(m_final_norm_g), fn2(v_final_norm_g)),
    ])
    untr = lambda a: a.T[None]
    pw4 = lambda a: a.reshape(1, len(POOL_WINDOWS), POOL_GROUP, POOL_GROUP)
    per_kind = [[
        r_meta[kind], r_norm[kind], untr(r_in[kind]), r_gq[kind], untr(r_wq[kind]), r_gkv[kind], r_wkv[kind][None],
        pw4(r_pw[kind]), r_ps[kind], r_out[kind][None], r_fn[kind].reshape(D),
    ] for kind in range(4)]
    return (gloss[0, 0], part["gx"][None], *per_kind[0], *per_kind[1], *per_kind[2], *per_kind[3])
```

```python
import jax
import jax.numpy as jnp
import numpy as np
from jax import lax
from jax.experimental import pallas as pl
from jax.experimental.pallas import tpu as pltpu

F32 = jnp.float32
BF16 = jnp.bfloat16

D = 1024
S = 2048
N_META = 16
PAD = 112
HEAD_ROWS = PAD + N_META
N = HEAD_ROWS + S
D_POOL = 512
POOL_WINDOWS = (2, 4, 8, 16)
POOL_GROUP = 128
HALO = 16
HEADS = 4
QK_NOPE = 128
QK_ROPE = 64
QK = QK_NOPE + QK_ROPE
V_HEAD = 128
Q_LORA = 256
KV_LORA = 128
D_IN = 1984
EPS = 1e-6
ROPE_THETA = 10000.0
SCALE = QK ** -0.5
CHIPS = 4

ROWS_FWD = 544
ROWS_MID = 544
ROWS_BWD = 544
TK = 128
TQ = 256
NQ = S // TQ
HEADS_PER_STEP_BWD = 2

O_PI, O_PG, O_CQ, O_CKV, O_KR, O_AG = 0, 512, 1024, 1280, 1408, 1472
O_KR_END = O_KR + 128
SHARD_IN = D_IN // CHIPS
SHARD_OUT = D // CHIPS

LR, B1, B2, ADAM_EPS, WD, STEP = 0.001, 0.9, 0.999, 1e-08, 0.01, 10
C1 = 1.0 - B1**STEP
C2 = 1.0 - B2**STEP

VMEM_LIMIT = 60 * 1024 * 1024
MESH = pl.DeviceIdType.MESH
NEG = -1e30

VEC_ROWS = 8
V_GQ, V_GKV, V_PS, V_LOSS = 0, 256, 384, 896


def _cparams(**kw):
    return pltpu.CompilerParams(vmem_limit_bytes=VMEM_LIMIT, **kw)


def _nt(a, b):
    return lax.dot_general(a, b, (((1,), (1,)), ((), ())), preferred_element_type=F32)


def _tn(a, b):
    return lax.dot_general(a, b, (((0,), (0,)), ((), ())), preferred_element_type=F32)


def _nn(a, b):
    return jnp.dot(a, b, preferred_element_type=F32)


def _swap64(t):
    return pltpu.roll(t, 32, 1) + pltpu.roll(t, 96, 1)


def _sigmoid(x):
    return 1.0 / (1.0 + jnp.exp(-x))


def _low_lanes():
    return (lax.broadcasted_iota(jnp.int32, (1, 128), 1) < QK_ROPE).astype(F32)


def _rows(w, rows):
    return pl.BlockSpec((rows, w), lambda i: (i, 0))


def _const(*shape):
    return pl.BlockSpec(shape, lambda *_: (0,) * len(shape), pipeline_mode=pl.Buffered(1))


STAT_GROUPS = HEADS // HEADS_PER_STEP_BWD


def _stat_slot(head):
    return head // HEADS_PER_STEP_BWD, head % HEADS_PER_STEP_BWD


def _attn_tiles():
    return [(0, TK, TK)] + [(TK + TQ * t, TQ, TK + TQ * (t + 1)) for t in range(NQ)]


def _masked_scores(q, k, rows, klen):
    s = _nt(q, k)
    col = lax.broadcasted_iota(jnp.int32, (1, TK), 1)
    head_bias = jnp.where(col >= PAD, 0.0, NEG)
    if klen == TK:
        return s + head_bias
    r = lax.broadcasted_iota(jnp.int32, (rows, 1), 0) >> 6
    c = lax.broadcasted_iota(jnp.int32, (1, rows), 1) >> 6
    diag_bias = jnp.where(c <= r, 0.0, NEG)
    parts = [s[:, 0:TK] + head_bias]
    if klen - rows > TK:
        parts.append(s[:, TK:klen - rows])
    parts.append(s[:, klen - rows:klen] + diag_bias)
    return jnp.concatenate(parts, axis=1)


def _fwd_in(h, norm_g, win, gq, wq, gkv, wkv, cosf, sinf):
    tr = ROWS_FWD

    def body(h_ref, g_ref, win_ref, gq_ref, wq_ref, gkv_ref, wkv_ref, cos_ref, sin_ref,
             pi_ref, pg_ref, cq_ref, ckv_ref, ag_ref, q_ref, k_ref, v_ref):
        h = h_ref[...]
        r = lax.rsqrt(jnp.mean(h * h, axis=-1, keepdims=True) + EPS)
        hn = ((h * r) * g_ref[...]).astype(BF16)
        u = _nt(hn, win_ref[0:O_KR_END, :])
        pi_ref[...] = u[:, O_PI:O_PG]
        pg_ref[...] = u[:, O_PG:O_CQ]
        cq = u[:, O_CQ:O_CKV]
        ckv = u[:, O_CKV:O_KR]
        cq_ref[...] = cq
        ckv_ref[...] = ckv
        ag_ref[...] = _nt(hn, win_ref[O_AG:D_IN, :])
        cosv = cos_ref[...]
        sinv = sin_ref[...]
        kr = u[:, O_KR:O_KR_END] * _low_lanes()
        kr = (kr * cosv + _swap64(kr) * sinv).astype(BF16)
        rq = lax.rsqrt(jnp.mean(cq * cq, axis=-1, keepdims=True) + EPS)
        cqn = ((cq * rq) * gq_ref[...]).astype(BF16)
        rkv = lax.rsqrt(jnp.mean(ckv * ckv, axis=-1, keepdims=True) + EPS)
        ckvn = ((ckv * rkv) * gkv_ref[...]).astype(BF16)
        for hd in range(HEADS):
            qh = _nt(cqn, wq_ref[hd]) * SCALE
            z = qh[:, QK_NOPE:]
            q_ref[hd, :, 0:QK_NOPE] = qh[:, 0:QK_NOPE].astype(BF16)
            q_ref[hd, :, QK_NOPE:] = (z * cosv + _swap64(z) * sinv).astype(BF16)
            kvh = _nn(ckvn, wkv_ref[hd])
            k_ref[hd, :, 0:QK_NOPE] = kvh[:, 0:QK_NOPE].astype(BF16)
            k_ref[hd, :, QK_NOPE:] = kr
            v_ref[hd] = kvh[:, QK_NOPE:].astype(BF16)

    head = lambda w: pl.BlockSpec((HEADS, tr, w), lambda i: (0, i, 0))
    return pl.pallas_call(
        body,
        name="fwd_in",
        grid=(N // tr,),
        in_specs=[
            _rows(D, tr), _const(1, D), _const(D_IN, D), _const(1, Q_LORA), _const(HEADS, 256, Q_LORA),
            _const(1, KV_LORA), _const(HEADS, KV_LORA, 256), _rows(128, tr), _rows(128, tr),
        ],
        out_specs=[_rows(D_POOL, tr), _rows(D_POOL, tr), _rows(Q_LORA, tr), _rows(KV_LORA, tr), _rows(D_POOL, tr),
                   head(256), head(256), head(V_HEAD)],
        out_shape=[
            jax.ShapeDtypeStruct((N, D_POOL), F32), jax.ShapeDtypeStruct((N, D_POOL), F32),
            jax.ShapeDtypeStruct((N, Q_LORA), F32), jax.ShapeDtypeStruct((N, KV_LORA), F32),
            jax.ShapeDtypeStruct((N, D_POOL), F32),
            jax.ShapeDtypeStruct((HEADS, N, 256), BF16), jax.ShapeDtypeStruct((HEADS, N, 256), BF16),
            jax.ShapeDtypeStruct((HEADS, N, V_HEAD), BF16),
        ],
        compiler_params=_cparams(dimension_semantics=("arbitrary",)),
    )(h, norm_g, win, gq, wq, gkv, wkv, cosf, sinf)


def _attn_fwd(q, k, v, wout_s):
    tiles = _attn_tiles()
    n_t = len(tiles)
    half = SHARD_OUT // 2
    fwd_step = n_t // 2

    def body(q_hbm, k_hbm, v_hbm, wout_ref, o_hbm, lse_ref, wout_o, q_buf, k_buf, v_buf, o_buf, s_wout, in_sems, out_sems,
             ici_send, ici_recv, fwd_send, fwd_recv, own_sem):
        step = pl.program_id(0)
        x, y, c = lax.axis_index("x"), lax.axis_index("y"), lax.axis_index("c")
        me = 2 * x + y

        def chip_of(rel):
            fx, fy = _CHIP_RELS[rel]
            return 2 * (x ^ fx) + (y ^ fy)

        def place(chip, core):
            return wout_o.at[pl.ds(pl.multiple_of(SHARD_OUT * chip + half * core, half), half), :]

        def ici_copy(rel, src_chip, to):
            return _remote(s_wout.at[pl.ds(pl.multiple_of(half * c, half), half), :], place(src_chip, c),
                           ici_send.at[rel - 1], ici_recv.at[rel - 1], to)

        def fwd_copy(rel, core, to):
            spot = place(chip_of(rel), core)
            return _remote(spot, spot, fwd_send.at[rel - 1], fwd_recv.at[rel - 1], to)

        own = pltpu.make_async_copy(s_wout, wout_o.at[pl.ds(pl.multiple_of(SHARD_OUT * me, SHARD_OUT), SHARD_OUT), :], own_sem)

        @pl.when(step == 0)
        def _():
            s_wout[...] = wout_ref[...].astype(BF16)
            own.start()
            for rel in (1, 2, 3):
                fx, fy = _CHIP_RELS[rel]
                ici_copy(rel, me, (x ^ fx, y ^ fy, c)).start()

        @pl.when(step == fwd_step)
        def _():
            for rel in (1, 2, 3):
                ici_copy(rel, chip_of(rel), (x, y, c)).wait_recv()
                fwd_copy(rel, c, (x, y, 1 - c)).start()

        def finish_wout():
            for rel in (1, 2, 3):
                fwd_copy(rel, 1 - c, (x, y, c)).wait_recv()
            for rel in (1, 2, 3):
                ici_copy(rel, me, (x, y, c)).wait_send()
                fwd_copy(rel, c, (x, y, c)).wait_send()
            own.wait()

        def loads(idx):
            q0, rows, _ = tiles[idx]
            rs = pl.ds(q0, rows)
            return [pltpu.make_async_copy(src.at[:, rs, :], dst.at[:, rs, :], in_sems.at[a, idx % 2])
                    for a, (src, dst) in enumerate(((q_hbm, q_buf), (k_hbm, k_buf), (v_hbm, v_buf)))]

        def store(idx):
            q0, rows, _ = tiles[idx]
            return pltpu.make_async_copy(o_buf.at[idx % 2, pl.ds(0, rows), :], o_hbm.at[pl.ds(q0, rows), :],
                                         out_sems.at[idx % 2])

        @pl.when(step == 0)
        def _():
            lse_ref[...] = jnp.zeros_like(lse_ref)
            for cp in loads(0):
                cp.start()

        for idx, (q0, rows, klen) in enumerate(tiles):
            @pl.when(step == idx)
            def _(idx=idx, q0=q0, rows=rows, klen=klen):
                for cp in loads(idx):
                    cp.wait()
                if idx + 1 < n_t:
                    for cp in loads(idx + 1):
                        cp.start()
                if idx >= 2:
                    store(idx - 2).wait()
                for hd in range(HEADS):
                    s = _masked_scores(q_buf[hd, q0:q0 + rows, :], k_buf[hd, 0:klen, :], rows, klen)
                    m = jnp.max(s, axis=-1, keepdims=True)
                    p = jnp.exp(s - m)
                    l = jnp.sum(p, axis=-1, keepdims=True)
                    o_buf[idx % 2, 0:rows, hd * V_HEAD:(hd + 1) * V_HEAD] = _nn(p.astype(BF16), v_buf[hd, 0:klen, :]) / l
                    grp, lane = _stat_slot(hd)
                    lse_ref[grp, q0:q0 + rows, lane:lane + 1] = m + jnp.log(l)
                store(idx).start()
                if idx == n_t - 1:
                    store(idx - 1).wait()
                    store(idx).wait()
                    finish_wout()

    hbm = pl.BlockSpec(memory_space=pl.ANY)
    return pl.pallas_call(
        body,
        name="attn_fwd",
        grid=(n_t,),
        in_specs=[hbm, hbm, hbm, _const(SHARD_OUT, D)],
        out_specs=[hbm, _const(STAT_GROUPS, N, 128), hbm],
        out_shape=[jax.ShapeDtypeStruct((N, HEADS * V_HEAD), F32), jax.ShapeDtypeStruct((STAT_GROUPS, N, 128), F32),
                   jax.ShapeDtypeStruct((D, D), BF16)],
        scratch_shapes=[pltpu.VMEM((HEADS, N, 256), BF16), pltpu.VMEM((HEADS, N, 256), BF16),
                        pltpu.VMEM((HEADS, N, V_HEAD), BF16), pltpu.VMEM((2, TQ, HEADS * V_HEAD), F32),
                        pltpu.VMEM((SHARD_OUT, D), BF16),
                        pltpu.SemaphoreType.DMA((3, 2)), pltpu.SemaphoreType.DMA((2,))]
        + [pltpu.SemaphoreType.DMA((3,))] * 4 + [pltpu.SemaphoreType.DMA],
        compiler_params=_cparams(dimension_semantics=("arbitrary",)),
    )(q, k, v, wout_s)


def _inv_count(row0, rows, w):
    row = row0 + lax.broadcasted_iota(jnp.int32, (rows, 1), 0)
    return 1.0 / jnp.clip(row - (PAD - 1), 1, w).astype(F32)


def _mid(h, tgt, pool_in, pool_gate, attn_gate, attn, pool_w, pool_scale, wout, gf):
    tr = ROWS_MID
    per = tr // HALO
    ng = len(POOL_WINDOWS)

    def body(h_ref, t_ref, pin_ref, halo_ref, pg_ref, ag_ref, at_ref, pw_ref, ps_ref, wout_ref, gf_ref,
             dh2_ref, do_ref, delta_ref, dag_ref, dpg_ref, dpl_ref, dwout_ref, dpw_ref, dps_ref, dgf_ref, loss_ref):
        i = pl.program_id(0)

        @pl.when(i == 0)
        def _():
            dwout_ref[...] = jnp.zeros_like(dwout_ref)
            dpw_ref[...] = jnp.zeros_like(dpw_ref)
            dps_ref[...] = jnp.zeros_like(dps_ref)
            dgf_ref[...] = jnp.zeros_like(dgf_ref)
            loss_ref[...] = jnp.zeros_like(loss_ref)

        row0 = i * tr
        real = (row0 + lax.broadcasted_iota(jnp.int32, (tr, 1), 0)) >= HEAD_ROWS
        h = h_ref[...]

        halo = jnp.where(i > 0, halo_ref[...], 0.0)
        ext = jnp.concatenate([halo, pin_ref[...]], axis=0)
        pooled = []
        for g, w in enumerate(POOL_WINDOWS):
            e = ext[:, g * POOL_GROUP:(g + 1) * POOL_GROUP]
            acc = e
            shift = 1
            while shift < w:
                acc = acc + pltpu.roll(acc, shift, 0)
                shift *= 2
            pooled.append((acc[HALO:] * _inv_count(row0, tr, w) - e[HALO:]).astype(BF16))
        pw = [pw_ref[g].astype(BF16) for g in range(ng)]
        mixed = jnp.concatenate([_nn(pooled[g], pw[g]) for g in range(ng)], axis=1)
        ps = ps_ref[...]
        mixed_s = mixed * ps
        pg = pg_ref[...]
        sig_p = _sigmoid(pg)
        silu_p = pg * sig_p
        pool_out = (silu_p * mixed_s).astype(BF16)
        ag = ag_ref[...]
        sig_a = _sigmoid(ag)
        silu_a = ag * sig_a
        at = at_ref[...]
        attn_out = (silu_a * at).astype(BF16)
        mix = _nn(pool_out, wout_ref[0:D_POOL, :]) + _nn(attn_out, wout_ref[D_POOL:D, :])
        h2 = h + mix

        r2 = lax.rsqrt(jnp.mean(h2 * h2, axis=-1, keepdims=True) + EPS)
        n2 = h2 * r2
        gfv = gf_ref[...]
        err = jnp.where(real, n2 * gfv - t_ref[...], 0.0)
        loss_ref[...] += jnp.sum(jnp.sum(err * err, axis=-1, keepdims=True), axis=0, keepdims=True) * (0.5 / D)
        dy = err * (1.0 / D)
        dgf_ref[...] += jnp.sum(dy * n2, axis=0, keepdims=True)
        dn = dy * gfv
        dh2 = r2 * (dn - n2 * jnp.mean(dn * n2, axis=-1, keepdims=True))
        dh2_ref[...] = dh2
        dh2b = dh2.astype(BF16)

        dwout_ref[0:D_POOL, :] += _tn(pool_out, dh2b)
        dwout_ref[D_POOL:D, :] += _tn(attn_out, dh2b)
        dcat = _nt(dh2b, wout_ref[...])
        dpo = dcat[:, 0:D_POOL]
        dao = dcat[:, D_POOL:D]
        do = dao * silu_a
        prod = do * at
        delta_ref[...] = jnp.zeros_like(delta_ref)
        for hd in range(HEADS):
            grp, lane = _stat_slot(hd)
            cols = slice(hd * V_HEAD, (hd + 1) * V_HEAD)
            do_ref[grp, :, lane * V_HEAD:(lane + 1) * V_HEAD] = do[:, cols].astype(BF16)
            delta_ref[grp, :, lane:lane + 1] = jnp.sum(prod[:, cols], axis=-1, keepdims=True)
        dag_ref[...] = (dao * at * (sig_a * (1.0 + ag * (1.0 - sig_a)))).astype(BF16)
        dmixed_s = dpo * silu_p
        dpg_ref[...] = (dpo * mixed_s * (sig_p * (1.0 + pg * (1.0 - sig_p)))).astype(BF16)
        dps_ref[...] += jnp.sum(dmixed_s * mixed, axis=0, keepdims=True)
        dmixed = (dmixed_s * ps).astype(BF16)
        dpl = []
        for g in range(ng):
            dm = dmixed[:, g * POOL_GROUP:(g + 1) * POOL_GROUP]
            dpl.append(_nt(dm, pw[g]))
            dpw_ref[g] += _tn(pooled[g], dm)
        dpl_ref[...] = jnp.concatenate(dpl, axis=1)

    halo_spec = pl.BlockSpec((HALO, D_POOL), lambda i: (jnp.maximum(i * per - 1, 0), 0))
    return pl.pallas_call(
        body,
        name="mid",
        grid=(N // tr,),
        in_specs=[
            _rows(D, tr), _rows(D, tr), _rows(D_POOL, tr), halo_spec, _rows(D_POOL, tr), _rows(D_POOL, tr),
            _rows(D_POOL, tr), _const(ng, POOL_GROUP, POOL_GROUP), _const(1, D_POOL), _const(D, D), _const(1, D),
        ],
        out_specs=[
            _rows(D, tr), pl.BlockSpec((STAT_GROUPS, tr, HEADS_PER_STEP_BWD * V_HEAD), lambda i: (0, i, 0)),
            pl.BlockSpec((STAT_GROUPS, tr, 128), lambda i: (0, i, 0)),
            _rows(D_POOL, tr), _rows(D_POOL, tr), _rows(D_POOL, tr),
            _const(D, D), _const(ng, POOL_GROUP, POOL_GROUP), _const(1, D_POOL), _const(1, D), _const(1, 128),
        ],
        out_shape=[
            jax.ShapeDtypeStruct((N, D), F32), jax.ShapeDtypeStruct((STAT_GROUPS, N, HEADS_PER_STEP_BWD * V_HEAD), BF16),
            jax.ShapeDtypeStruct((STAT_GROUPS, N, 128), F32),
            jax.ShapeDtypeStruct((N, D_POOL), BF16), jax.ShapeDtypeStruct((N, D_POOL), BF16),
            jax.ShapeDtypeStruct((N, D_POOL), F32), jax.ShapeDtypeStruct((D, D), F32),
            jax.ShapeDtypeStruct((ng, POOL_GROUP, POOL_GROUP), F32),
            jax.ShapeDtypeStruct((1, D_POOL), F32), jax.ShapeDtypeStruct((1, D), F32), jax.ShapeDtypeStruct((1, 128), F32),
        ],
        compiler_params=_cparams(dimension_semantics=("arbitrary",)),
    )(h, tgt, pool_in, pool_in, pool_gate, attn_gate, attn, pool_w, pool_scale, wout, gf)


def _unrope(dy, cosv, sinv):
    return dy * cosv + _swap64(dy * sinv) * _low_lanes()


def _attn_bwd(q, k, v, do, lse, delta, cosf, sinf):
    tiles = _attn_tiles()
    hp = HEADS_PER_STEP_BWD
    n_g = HEADS // hp
    n_t = len(tiles)

    def body(q_hbm, k_hbm, v_hbm, do_hbm, lse_ref, delta_ref, cos_ref, sin_ref, dq_hbm, dkv_ref, dkr_ref,
             q_buf, k_buf, v_buf, do_buf, dq_buf, dk_acc, dv_acc, in_sems, out_sems):
        grp = pl.program_id(0)
        step = pl.program_id(1)
        heads = pl.ds(grp * hp, hp)

        def loads(g, idx):
            q0, rows, _ = tiles[idx]
            rs = pl.ds(q0, rows)
            par = (g * n_t + idx) % 2
            hs = pl.ds(g * hp, hp)
            pairs = ((q_hbm.at[hs, rs, :], q_buf.at[:, rs, :]), (k_hbm.at[hs, rs, :], k_buf.at[:, rs, :]),
                     (v_hbm.at[hs, rs, :], v_buf.at[:, rs, :]), (do_hbm.at[g, rs, :], do_buf.at[rs, :]))
            return [pltpu.make_async_copy(src, dst, in_sems.at[a, par]) for a, (src, dst) in enumerate(pairs)]

        def store(idx):
            q0, rows, _ = tiles[idx]
            return pltpu.make_async_copy(dq_buf.at[idx % 2, :, pl.ds(0, rows), :], dq_hbm.at[heads, pl.ds(q0, rows), :],
                                         out_sems.at[idx % 2])

        @pl.when(step == 0)
        def _():
            dk_acc[...] = jnp.zeros_like(dk_acc)
            dv_acc[...] = jnp.zeros_like(dv_acc)

        @pl.when((step == 0) & (grp == 0))
        def _():
            dkr_ref[...] = jnp.zeros_like(dkr_ref)
            for cp in loads(grp, 0):
                cp.start()

        for idx, (q0, rows, klen) in enumerate(tiles):
            @pl.when(step == idx)
            def _(idx=idx, q0=q0, rows=rows, klen=klen):
                for cp in loads(grp, idx):
                    cp.wait()
                if idx + 1 < n_t:
                    for cp in loads(grp, idx + 1):
                        cp.start()
                if idx >= 2:
                    store(idx - 2).wait()
                qs = pl.ds(q0, rows)
                for hd in range(hp):
                    qv = q_buf[hd, qs, :]
                    kv = k_buf[hd, 0:klen, :]
                    p = jnp.exp(_masked_scores(qv, kv, rows, klen) - lse_ref[0, qs, hd:hd + 1])
                    dob = do_buf[qs, hd * V_HEAD:(hd + 1) * V_HEAD]
                    ds = (p * (_nt(dob, v_buf[hd, 0:klen, :]) - delta_ref[0, qs, hd:hd + 1])).astype(BF16)
                    dq = _nn(ds, kv) * SCALE
                    dq_buf[idx % 2, hd, 0:rows, 0:QK_NOPE] = dq[:, 0:QK_NOPE].astype(BF16)
                    dq_buf[idx % 2, hd, 0:rows, QK_NOPE:] = _unrope(dq[:, QK_NOPE:], cos_ref[qs, :], sin_ref[qs, :]).astype(BF16)
                    dk_acc[hd, 0:klen, :] += _tn(ds, qv)
                    dv_acc[hd, 0:klen, :] += _tn(p.astype(BF16), dob)
                store(idx).start()

        @pl.when(step == n_t - 1)
        def _():
            @pl.when(grp + 1 < n_g)
            def _():
                for cp in loads(grp + 1, 0):
                    cp.start()

            for hd in range(hp):
                dkv_ref[hd, :, 0:QK_NOPE] = dk_acc[hd, :, 0:QK_NOPE].astype(BF16)
                dkv_ref[hd, :, QK_NOPE:] = dv_acc[hd].astype(BF16)
                dkr_ref[...] += dk_acc[hd, :, QK_NOPE:]
            store(n_t - 2).wait()
            store(n_t - 1).wait()

    hbm = pl.BlockSpec(memory_space=pl.ANY)
    stat = pl.BlockSpec((1, N, 128), lambda g, t: (g, 0, 0), pipeline_mode=pl.Buffered(1))
    return pl.pallas_call(
        body,
        name="attn_bwd",
        grid=(n_g, n_t),
        in_specs=[hbm, hbm, hbm, hbm, stat, stat, _const(N, 128), _const(N, 128)],
        out_specs=[hbm, pl.BlockSpec((hp, N, 256), lambda g, t: (g, 0, 0), pipeline_mode=pl.Buffered(1)), _const(N, 128)],
        out_shape=[
            jax.ShapeDtypeStruct((HEADS, N, 256), BF16), jax.ShapeDtypeStruct((HEADS, N, 256), BF16),
            jax.ShapeDtypeStruct((N, 128), F32),
        ],
        scratch_shapes=[pltpu.VMEM((hp, N, 256), BF16), pltpu.VMEM((hp, N, 256), BF16), pltpu.VMEM((hp, N, V_HEAD), BF16),
                        pltpu.VMEM((N, hp * V_HEAD), BF16), pltpu.VMEM((2, hp, TQ, 256), BF16),
                        pltpu.VMEM((hp, N, 256), F32), pltpu.VMEM((hp, N, V_HEAD), F32),
                        pltpu.SemaphoreType.DMA((4, 2)), pltpu.SemaphoreType.DMA((2,))],
        compiler_params=_cparams(dimension_semantics=("arbitrary", "arbitrary")),
    )(q, k, v, do, lse, delta, cosf, sinf)


def _bwd_in(h, dh2, dq, dkv, dkr, cq, ckv, dpl, dpg, dag, norm_g, win, gq, wq, gkv, wkv, cosf, sinf):
    tr = ROWS_BWD
    nb = N // tr
    per = tr // HALO
    lead = HEAD_ROWS

    def body(h_ref, dh2_ref, dq_ref, dkv_ref, dkr_ref, cq_ref, ckv_ref, dpl_ref, halo_ref, dpg_ref, dag_ref,
             g_ref, win_ref, gq_ref, wq_ref, gkv_ref, wkv_ref, cos_ref, sin_ref,
             gx_ref, dmeta_ref, dwin_ref, dwq_ref, dwkv_ref, dg_ref, dgq_ref, dgkv_ref, dh_buf, gx_sem):
        i = pl.program_id(0)

        @pl.when(i == 0)
        def _():
            dwin_ref[...] = jnp.zeros_like(dwin_ref)
            dwq_ref[...] = jnp.zeros_like(dwq_ref)
            dwkv_ref[...] = jnp.zeros_like(dwkv_ref)
            dg_ref[...] = jnp.zeros_like(dg_ref)
            dgq_ref[...] = jnp.zeros_like(dgq_ref)
            dgkv_ref[...] = jnp.zeros_like(dgkv_ref)

        row0 = i * tr
        h = h_ref[...]
        r = lax.rsqrt(jnp.mean(h * h, axis=-1, keepdims=True) + EPS)
        n = h * r
        gv = g_ref[...]
        hn = (n * gv).astype(BF16)
        cq = cq_ref[...]
        rq = lax.rsqrt(jnp.mean(cq * cq, axis=-1, keepdims=True) + EPS)
        nq = cq * rq
        gqv = gq_ref[...]
        cqn = (nq * gqv).astype(BF16)
        dcqn = jnp.zeros((tr, Q_LORA), F32)
        for hd in range(HEADS):
            dqf = dq_ref[hd]
            dcqn = dcqn + _nn(dqf, wq_ref[hd])
            dwq_ref[hd] += _tn(dqf, cqn)
        dgq_ref[...] += jnp.sum(dcqn * nq, axis=0, keepdims=True)
        dnq = dcqn * gqv
        dcq = rq * (dnq - nq * jnp.mean(dnq * nq, axis=-1, keepdims=True))

        ckv = ckv_ref[...]
        rkv = lax.rsqrt(jnp.mean(ckv * ckv, axis=-1, keepdims=True) + EPS)
        nkv = ckv * rkv
        gkvv = gkv_ref[...]
        ckvn = (nkv * gkvv).astype(BF16)
        dckvn = jnp.zeros((tr, KV_LORA), F32)
        for hd in range(HEADS):
            dkv = dkv_ref[hd]
            dckvn = dckvn + _nt(dkv, wkv_ref[hd])
            dwkv_ref[hd] += _tn(ckvn, dkv)
        dgkv_ref[...] += jnp.sum(dckvn * nkv, axis=0, keepdims=True)
        dnkv = dckvn * gkvv
        dckv = rkv * (dnkv - nkv * jnp.mean(dnkv * nkv, axis=-1, keepdims=True))
        dkr = _unrope(dkr_ref[...], cos_ref[...], sin_ref[...])

        cur = dpl_ref[...]
        halo = jnp.where(i < nb - 1, halo_ref[...], 0.0)
        dpi = []
        for g, w in enumerate(POOL_WINDOWS):
            sl = slice(g * POOL_GROUP, (g + 1) * POOL_GROUP)
            a = jnp.concatenate([cur[:, sl] * _inv_count(row0, tr, w), halo[:, sl] * _inv_count(row0 + tr, HALO, w)], axis=0)
            acc = a
            shift = 1
            while shift < w:
                acc = acc + pltpu.roll(acc, tr + HALO - shift, 0)
                shift *= 2
            dpi.append(acc[0:tr] - cur[:, sl])

        du = jnp.concatenate([t.astype(BF16) for t in dpi] + [dpg_ref[...]] + [t.astype(BF16) for t in (dcq, dckv, dkr)],
                             axis=1)
        dagb = dag_ref[...]
        dwin_ref[0:O_KR_END, :] += _tn(du, hn)
        dwin_ref[O_AG:D_IN, :] += _tn(dagb, hn)
        dhn = _nn(du, win_ref[0:O_KR_END, :]) + _nn(dagb, win_ref[O_AG:D_IN, :])
        dg_ref[...] += jnp.sum(dhn * n, axis=0, keepdims=True)
        dn = dhn * gv
        dh = dh2_ref[...] + r * (dn - n * jnp.mean(dn * n, axis=-1, keepdims=True))

        first = pltpu.make_async_copy(dh_buf.at[pl.ds(lead, tr - lead), :], gx_ref.at[pl.ds(0, tr - lead), :], gx_sem)
        later = lambda step: pltpu.make_async_copy(
            dh_buf, gx_ref.at[pl.ds(pl.multiple_of(step * tr - lead, 16), tr), :], gx_sem)

        @pl.when(i == 1)
        def _():
            first.wait()

        @pl.when(i > 1)
        def _():
            later(i - 1).wait()

        dh_buf[...] = dh

        @pl.when(i == 0)
        def _():
            first.start()
            for chip in range(CHIPS):
                dmeta_ref[chip] = dh[PAD:HEAD_ROWS, chip * 256:(chip + 1) * 256]

        @pl.when(i > 0)
        def _():
            later(i).start()

        @pl.when(i == nb - 1)
        def _():
            later(i).wait()

    head = lambda w: pl.BlockSpec((HEADS, tr, w), lambda i: (0, i, 0))
    halo_spec = pl.BlockSpec((HALO, D_POOL), lambda i: (jnp.minimum((i + 1) * per, N // HALO - 1), 0))
    return pl.pallas_call(
        body,
        name="bwd_in",
        grid=(nb,),
        in_specs=[
            _rows(D, tr), _rows(D, tr), head(256), head(256), _rows(128, tr), _rows(Q_LORA, tr), _rows(KV_LORA, tr),
            _rows(D_POOL, tr), halo_spec, _rows(D_POOL, tr), _rows(D_POOL, tr),
            _const(1, D), _const(D_IN, D), _const(1, Q_LORA), _const(HEADS, 256, Q_LORA),
            _const(1, KV_LORA), _const(HEADS, KV_LORA, 256), _rows(128, tr), _rows(128, tr),
        ],
        out_specs=[
            pl.BlockSpec(memory_space=pl.ANY), _const(CHIPS, N_META, 256), _const(D_IN, D), _const(HEADS, 256, Q_LORA),
            _const(HEADS, KV_LORA, 256), _const(1, D), _const(1, Q_LORA), _const(1, KV_LORA),
        ],
        out_shape=[
            jax.ShapeDtypeStruct((S, D), F32), jax.ShapeDtypeStruct((CHIPS, N_META, 256), F32),
            jax.ShapeDtypeStruct((D_IN, D), F32), jax.ShapeDtypeStruct((HEADS, 256, Q_LORA), F32),
            jax.ShapeDtypeStruct((HEADS, KV_LORA, 256), F32),
            jax.ShapeDtypeStruct((1, D), F32), jax.ShapeDtypeStruct((1, Q_LORA), F32), jax.ShapeDtypeStruct((1, KV_LORA), F32),
        ],
        scratch_shapes=[pltpu.VMEM((tr, D), F32), pltpu.SemaphoreType.DMA],
        compiler_params=_cparams(dimension_semantics=("arbitrary",)),
    )(h, dh2, dq, dkv, dkr, cq, ckv, dpl, dpl, dpg, dag, norm_g, win, gq, wq, gkv, wkv, cosf, sinf)


def _local_step(h, tgt, norm_g, win, gq, wq, gkv, wkv, pool_w, pool_scale, wout_s, gf, cosf, sinf):
    pool_in, pool_gate, cq, ckv, attn_gate, q, k, v = _fwd_in(h, norm_g, win, gq, wq, gkv, wkv, cosf, sinf)
    attn, lse, wout = _attn_fwd(q, k, v, wout_s)
    dh2, do, delta, dag, dpg, dpl, dwout, dpw, dps, dgf, loss = _mid(
        h, tgt, pool_in, pool_gate, attn_gate, attn, pool_w, pool_scale, wout, gf)
    dq, dkv, dkr = _attn_bwd(q, k, v, do, lse, delta, cosf, sinf)
    gx, dmeta, dwin, dwq, dwkv, dg, dgq, dgkv = _bwd_in(
        h, dh2, dq, dkv, dkr, cq, ckv, dpl, dpg, dag, norm_g, win, gq, wq, gkv, wkv, cosf, sinf)
    return dict(gx=gx, dmeta=dmeta, dwin=dwin, dwq=dwq, dwkv=dwkv, dwout=dwout, dg=dg, dgq=dgq, dgkv=dgkv,
                dpw=dpw, dps=dps, dgf=dgf, loss=loss)


_CHIP_RELS = ((0, 0), (1, 0), (0, 1), (1, 1))

_ARR_ROWS = (SHARD_IN, SHARD_OUT, 256, KV_LORA, N_META)
_ARR_COLS = (D, D, Q_LORA, 256, 256)
_PIECES = (
    (0, 0, 256, 0), (0, 256, SHARD_IN - 256, 1),
    (1, 0, 128, 0), (1, 128, 128, 1),
    (2, 0, 128, 0), (2, 128, 128, 1),
    (3, 0, 64, 0), (3, 64, 64, 1),
    (4, 0, N_META, 0),
)
_NP = len(_PIECES)
_PIECE_MAX = (256, 128, 128, 64, N_META)


def _gathered_at(refs, arr, chip, r0, n):
    if arr in (0, 1):
        return refs[arr].at[pl.ds(pl.multiple_of(_ARR_ROWS[arr] * chip + r0, 16), n), :]
    return refs[arr].at[chip, pl.ds(r0, n), :]


def _remote(src, dst, send_sem, recv_sem, to):
    return pltpu.make_async_remote_copy(src_ref=src, dst_ref=dst, send_sem=send_sem, recv_sem=recv_sem,
                                        device_id=to, device_id_type=MESH)


def _gather_weights(winT_s, wqT_s, wkv_s, meta_s, x2, tgt2):
    arrays = (0, 2, 3, 4)

    def body(win_ref, wq_ref, wkv_ref, meta_ref, x_ref, t_ref, win_o, wq_o, wkv_o, h_o, tp_o,
             s_win, s_wq, s_wkv, meta_all, head_buf, x_buf, t_buf, ici_send, ici_recv, fwd_send, fwd_recv,
             loc_sems, own_sems):
        x, y, c = lax.axis_index("x"), lax.axis_index("y"), lax.axis_index("c")
        me = 2 * x + y
        stage = (s_win, None, s_wq, s_wkv, meta_ref)
        outs = (win_o, None, wq_o, wkv_o, meta_all)

        frames = pl.ds(HEAD_ROWS, S)
        loads = [pltpu.make_async_copy(x_ref, x_buf, loc_sems.at[0]), pltpu.make_async_copy(t_ref, t_buf, loc_sems.at[1])]
        local = [pltpu.make_async_copy(x_buf, h_o.at[frames, :], loc_sems.at[0]),
                 pltpu.make_async_copy(t_buf, tp_o.at[frames, :], loc_sems.at[1])]
        for cp in loads:
            cp.start()

        s_win[...] = win_ref[...].astype(BF16)
        s_wq[0:QK, :] = wq_ref[...].astype(BF16)
        s_wq[QK:256, :] = jnp.zeros((256 - QK, Q_LORA), BF16)
        s_wkv[...] = wkv_ref[...].astype(BF16)

        def chip_of(rel):
            fx, fy = _CHIP_RELS[rel]
            return 2 * (x ^ fx) + (y ^ fy)

        def same_core_of(rel):
            fx, fy = _CHIP_RELS[rel]
            return (x ^ fx, y ^ fy, c)

        def ici_copy(rel, i, src_chip, to):
            arr, r0, n, _ = _PIECES[i]
            k = (rel - 1) * _NP + i
            return _remote(stage[arr].at[pl.ds(r0, n), :], _gathered_at(outs, arr, src_chip, r0, n),
                           ici_send.at[k], ici_recv.at[k], to)

        def fwd_copy(rel, i, to):
            arr, r0, n, _ = _PIECES[i]
            k = (rel - 1) * _NP + i
            place = _gathered_at(outs, arr, chip_of(rel), r0, n)
            return _remote(place, place, fwd_send.at[k], fwd_recv.at[k], to)

        for core in (0, 1):
            @pl.when(c == core)
            def _(core=core):
                mine = [i for i in range(_NP) if _PIECES[i][3] == core and _PIECES[i][0] in arrays]
                theirs = [i for i in range(_NP) if _PIECES[i][3] != core and _PIECES[i][0] in arrays]
                sends = [ici_copy(rel, i, me, same_core_of(rel)) for rel in (1, 2, 3) for i in mine]
                for cp in sends:
                    cp.start()
                for ld, st in zip(loads, local):
                    ld.wait()
                    st.start()
                own = [pltpu.make_async_copy(stage[arr], _gathered_at(outs, arr, me, 0, _ARR_ROWS[arr]), own_sems.at[arr])
                       for arr in arrays if arr != 4]
                for cp in own:
                    cp.start()
                meta_all[me] = meta_ref[...]
                for rel in (1, 2, 3):
                    for i in mine:
                        ici_copy(rel, i, chip_of(rel), (x, y, c)).wait_recv()
                        fwd = fwd_copy(rel, i, (x, y, 1 - c))
                        fwd.start()
                        sends.append(fwd)
                for rel in (1, 2, 3):
                    for i in theirs:
                        fwd_copy(rel, i, (x, y, c)).wait_recv()
                for cp in sends:
                    cp.wait_send()
                for cp in own:
                    cp.wait()

        head_buf[...] = jnp.zeros_like(head_buf)
        zeros = pltpu.make_async_copy(head_buf, tp_o.at[pl.ds(0, HEAD_ROWS), :], loc_sems.at[2])
        zeros.start()
        zeros.wait()
        for chip in range(CHIPS):
            head_buf[PAD:HEAD_ROWS, chip * 256:(chip + 1) * 256] = meta_all[chip]
        head = pltpu.make_async_copy(head_buf, h_o.at[pl.ds(0, HEAD_ROWS), :], loc_sems.at[2])
        head.start()
        head.wait()
        for cp in local:
            cp.wait()

    vm = pl.BlockSpec(memory_space=pltpu.VMEM)
    hbm = pl.BlockSpec(memory_space=pl.ANY)
    return pl.pallas_call(
        body,
        name="gather_weights",
        in_specs=[vm] * 4 + [hbm] * 2,
        out_specs=[hbm] * 5,
        out_shape=[
            jax.ShapeDtypeStruct((D_IN, D), BF16),
            jax.ShapeDtypeStruct((CHIPS, 256, Q_LORA), BF16), jax.ShapeDtypeStruct((CHIPS, KV_LORA, 256), BF16),
            jax.ShapeDtypeStruct((N, D), F32), jax.ShapeDtypeStruct((N, D), F32),
        ],
        scratch_shapes=[pltpu.VMEM((_ARR_ROWS[a], _ARR_COLS[a]), BF16) for a in (0, 2, 3)]
        + [pltpu.VMEM((CHIPS, N_META, 256), F32), pltpu.VMEM((HEAD_ROWS, D), F32), pltpu.VMEM((S, D), F32),
           pltpu.VMEM((S, D), F32)]
        + [pltpu.SemaphoreType.DMA((3 * _NP,))] * 4 + [pltpu.SemaphoreType.DMA((3,)), pltpu.SemaphoreType.DMA((4,))],
        compiler_params=_cparams(),
    )(winT_s, wqT_s, wkv_s, meta_s, x2, tgt2)


_SM_ROWS = (len(POOL_WINDOWS) * POOL_GROUP, VEC_ROWS)
_SM_COLS = (POOL_GROUP, D)
_SM_PIECES = ((0, 0, 256, 0), (0, 256, 256, 1), (1, 0, VEC_ROWS, 0))
_NSP = len(_SM_PIECES)


def _reduce_grads(dwin, dwout, dwq, dwkv, dmeta4, dpw, dg, dgf, dgq, dgkv, dps, loss):
    def body(dwin_ref, dwout_ref, dwq_ref, dwkv_ref, dmeta_ref, dpw_ref, dg_ref, dgf_ref, dgq_ref, dgkv_ref, dps_ref,
             loss_ref, gwin_o, gwout_o, gwq_o, gwkv_o, gmeta_o, gpw_o, gg_o, ggf_o, ggq_o, ggkv_o, gps_o, gloss_o,
             ow0, ow1, ow2, ow3, ow4, sb0, sb1, sb2, sb3, sb4, st0, st1, st2, st3, st4, rc0, rc1, rc2, rc3, rc4,
             vec, sm_sb0, sm_sb1, sm_cs0, sm_cs1, sm_rc0, sm_rc1, vec_fin,
             own_sems, d2d_send, d2d_recv, ici_send, ici_recv, fin_send, fin_recv,
             swap_send, swap_recv, smi_send, smi_recv, smf_send, smf_recv):
        x, y, c = lax.axis_index("x"), lax.axis_index("y"), lax.axis_index("c")
        me = 2 * x + y
        grads = (dwin_ref, dwout_ref, dwq_ref, dwkv_ref, dmeta_ref)
        outs = (gwin_o, gwout_o, gwq_o, gwkv_o, gmeta_o)
        own_buf = (ow0, ow1, ow2, ow3, ow4)
        sib_buf = (sb0, sb1, sb2, sb3, sb4)
        stage = (st0, st1, st2, st3, st4)
        recv = (rc0, rc1, rc2, rc3, rc4)
        sm_mine = (dpw_ref, vec)
        sm_sib = (sm_sb0, sm_sb1)
        sm_chip = (sm_cs0, sm_cs1)
        sm_recv = (sm_rc0, sm_rc1)
        sm_out = (gpw_o, vec_fin)
        sibling = (x, y, 1 - c)

        def chip_of(rel):
            fx, fy = _CHIP_RELS[rel]
            return 2 * (x ^ fx) + (y ^ fy)

        def same_core_of(rel):
            fx, fy = _CHIP_RELS[rel]
            return (x ^ fx, y ^ fy, c)

        def slot(bufs, i, idx):
            arr, _, n, _ = _PIECES[i]
            return bufs[arr].at[idx, pl.ds(0, n), :]

        def own_load(rel, i):
            arr, r0, n, _ = _PIECES[i]
            return pltpu.make_async_copy(_gathered_at(grads, arr, chip_of(rel), r0, n), slot(own_buf, i, rel),
                                         own_sems.at[rel * _NP + i])

        def d2d_copy(rel, i):
            arr, r0, n, _ = _PIECES[i]
            k = rel * _NP + i
            return _remote(_gathered_at(grads, arr, chip_of(rel), r0, n), slot(sib_buf, i, rel),
                           d2d_send.at[k], d2d_recv.at[k], sibling)

        def ici_copy(rel, i):
            k = (rel - 1) * _NP + i
            return _remote(slot(stage, i, rel - 1), slot(recv, i, rel - 1), ici_send.at[k], ici_recv.at[k],
                           same_core_of(rel))

        def fin_copy(i):
            arr, r0, n, _ = _PIECES[i]
            place = outs[arr].at[pl.ds(r0, n), :]
            return _remote(place, place, fin_send.at[i], fin_recv.at[i], sibling)

        def sm_ici_copy(rel, j):
            blk, r0, n, _ = _SM_PIECES[j]
            k = (rel - 1) * _NSP + j
            return _remote(sm_chip[blk].at[pl.ds(r0, n), :], sm_recv[blk].at[rel - 1, pl.ds(r0, n), :],
                           smi_send.at[k], smi_recv.at[k], same_core_of(rel))

        def sm_fin_copy(j):
            blk, r0, n, _ = _SM_PIECES[j]
            place = sm_out[blk].at[pl.ds(r0, n), :]
            return _remote(place, place, smf_send.at[j], smf_recv.at[j], sibling)

        vec[...] = jnp.zeros_like(vec)
        vec[0:1, :] = dg_ref[...]
        vec[1:2, :] = dgf_ref[...]
        vec[2:3, V_GQ:V_GQ + Q_LORA] = dgq_ref[...]
        vec[2:3, V_GKV:V_GKV + KV_LORA] = dgkv_ref[...]
        vec[2:3, V_PS:V_PS + D_POOL] = dps_ref[...]
        vec[2:3, V_LOSS:D] = loss_ref[...]
        swaps = [_remote(sm_mine[b], sm_sib[b], swap_send.at[b], swap_recv.at[b], sibling) for b in (0, 1)]
        for cp in swaps:
            cp.start()

        for core in (0, 1):
            @pl.when(c == core)
            def _(core=core):
                mine = [i for i in range(_NP) if _PIECES[i][3] == core]
                theirs = [i for i in range(_NP) if _PIECES[i][3] != core]
                sm_mine_p = [j for j in range(_NSP) if _SM_PIECES[j][3] == core]
                sm_theirs_p = [j for j in range(_NSP) if _SM_PIECES[j][3] != core]
                sends = list(swaps)

                for rel in (1, 2, 3, 0):
                    for i in theirs:
                        cp = d2d_copy(rel, i)
                        cp.start()
                        sends.append(cp)
                    for i in mine:
                        own_load(rel, i).start()

                for b in (0, 1):
                    swaps[b].wait_recv()
                    sm_chip[b][...] = sm_mine[b][...] + sm_sib[b][...]
                for rel in (1, 2, 3):
                    for j in sm_mine_p:
                        cp = sm_ici_copy(rel, j)
                        cp.start()
                        sends.append(cp)

                for rel in (1, 2, 3):
                    for i in mine:
                        arr, r0, n, _ = _PIECES[i]
                        own_load(rel, i).wait()
                        d2d_copy(rel, i).wait_recv()
                        total = slot(own_buf, i, rel)[...] + slot(sib_buf, i, rel)[...]
                        slot(stage, i, rel - 1)[...] = total.astype(stage[arr].dtype)
                        cp = ici_copy(rel, i)
                        cp.start()
                        sends.append(cp)

                for i in mine:
                    arr, r0, n, _ = _PIECES[i]
                    own_load(0, i).wait()
                    d2d_copy(0, i).wait_recv()
                    total = slot(own_buf, i, 0)[...] + slot(sib_buf, i, 0)[...]
                    for rel in (1, 2, 3):
                        ici_copy(rel, i).wait_recv()
                        total = total + slot(recv, i, rel - 1)[...].astype(F32)
                    outs[arr][pl.ds(r0, n), :] = total
                    cp = fin_copy(i)
                    cp.start()
                    sends.append(cp)

                for j in sm_mine_p:
                    blk, r0, n, _ = _SM_PIECES[j]
                    for rel in (1, 2, 3):
                        sm_ici_copy(rel, j).wait_recv()
                    total = jnp.zeros((n, _SM_COLS[blk]), F32)
                    for chip in range(CHIPS):
                        flips = chip ^ me
                        rel = jnp.where(flips == 2, 1, jnp.where(flips == 1, 2, flips))
                        theirs_rows = sm_recv[blk][jnp.maximum(rel - 1, 0), pl.ds(r0, n), :]
                        total = total + jnp.where(rel == 0, sm_chip[blk][pl.ds(r0, n), :], theirs_rows)
                    sm_out[blk][pl.ds(r0, n), :] = total
                    cp = sm_fin_copy(j)
                    cp.start()
                    sends.append(cp)

                for i in theirs:
                    fin_copy(i).wait_recv()
                for j in sm_theirs_p:
                    sm_fin_copy(j).wait_recv()
                for cp in sends:
                    cp.wait_send()

        gg_o[...] = vec_fin[0:1, :]
        ggf_o[...] = vec_fin[1:2, :]
        ggq_o[...] = vec_fin[2:3, V_GQ:V_GQ + Q_LORA]
        ggkv_o[...] = vec_fin[2:3, V_GKV:V_GKV + KV_LORA]
        gps_o[...] = vec_fin[2:3, V_PS:V_PS + D_POOL]
        gloss_o[...] = vec_fin[2:3, V_LOSS:D]

    vm = pl.BlockSpec(memory_space=pltpu.VMEM)
    piece_buf = lambda lead, dtype: [pltpu.VMEM((lead, _PIECE_MAX[a], _ARR_COLS[a]), F32 if a == 4 else dtype)
                                     for a in range(5)]
    sm_buf = lambda *lead: [pltpu.VMEM(lead + (_SM_ROWS[b], _SM_COLS[b]), F32) for b in (0, 1)]
    dma = lambda n: [pltpu.SemaphoreType.DMA((n,))] * 2
    return pl.pallas_call(
        body,
        name="reduce_grads",
        in_specs=[pl.BlockSpec(memory_space=pl.ANY)] * 4 + [vm] * 8,
        out_specs=[vm] * 12,
        out_shape=[jax.ShapeDtypeStruct((_ARR_ROWS[a], _ARR_COLS[a]), F32) for a in range(5)]
        + [jax.ShapeDtypeStruct((_SM_ROWS[0], _SM_COLS[0]), F32), jax.ShapeDtypeStruct((1, D), F32),
           jax.ShapeDtypeStruct((1, D), F32), jax.ShapeDtypeStruct((1, Q_LORA), F32),
           jax.ShapeDtypeStruct((1, KV_LORA), F32), jax.ShapeDtypeStruct((1, D_POOL), F32),
           jax.ShapeDtypeStruct((1, 128), F32)],
        scratch_shapes=piece_buf(CHIPS, F32) + piece_buf(CHIPS, F32) + piece_buf(3, BF16) + piece_buf(3, BF16)
        + [pltpu.VMEM((VEC_ROWS, D), F32)] + sm_buf() + sm_buf() + sm_buf(3) + [pltpu.VMEM((VEC_ROWS, D), F32)]
        + [pltpu.SemaphoreType.DMA((CHIPS * _NP,))]
        + dma(CHIPS * _NP) + dma(3 * _NP) + dma(_NP) + dma(2) + dma(3 * _NSP) + dma(_NSP),
        compiler_params=_cparams(),
    )(dwin, dwout, dwq, dwkv, dmeta4, dpw, dg, dgf, dgq, dgkv, dps, loss)


def _adamw_math(w, g, m, v):
    m = B1 * m + (1.0 - B1) * g
    v = B2 * v + (1.0 - B2) * (g * g)
    m_hat = m / C1
    v_hat = v / C2
    delta = -LR * (m_hat / (jnp.sqrt(v_hat) + ADAM_EPS) + WD * w)
    return delta, m, v


def _adamw_rows(name, w, g, m, v, block_rows):
    rows, cols = w.shape

    def body(w_ref, g_ref, m_ref, v_ref, go_ref, d_ref, nm_ref, nv_ref):
        g = g_ref[...]
        go_ref[...] = g
        d_ref[...], nm_ref[...], nv_ref[...] = _adamw_math(w_ref[...], g, m_ref[...], v_ref[...])

    spec = pl.BlockSpec((block_rows, cols), lambda i: (i, 0))
    return pl.pallas_call(
        body,
        name=name,
        grid=(rows // block_rows,),
        in_specs=[spec] * 4,
        out_specs=[spec] * 4,
        out_shape=[jax.ShapeDtypeStruct(w.shape, F32)] * 4,
        compiler_params=_cparams(dimension_semantics=("arbitrary",)),
    )(w, g, m, v)


def _adamw_small(groups):
    n = len(groups)

    def body(*refs):
        ins, outs = refs[:4 * n], refs[4 * n:]
        for t in range(n):
            w_ref, g_ref, m_ref, v_ref = ins[4 * t:4 * t + 4]
            g = g_ref[0:w_ref.shape[0], :]
            outs[4 * t][...] = g
            outs[4 * t + 1][...], outs[4 * t + 2][...], outs[4 * t + 3][...] = _adamw_math(
                w_ref[...], g, m_ref[...], v_ref[...])

    vm = pl.BlockSpec(memory_space=pltpu.VMEM)
    flat = [a for grp in groups for a in grp]
    outs = pl.pallas_call(
        body,
        name="adamw_small",
        in_specs=[vm] * (4 * n),
        out_specs=[vm] * (4 * n),
        out_shape=[jax.ShapeDtypeStruct(grp[0].shape, F32) for grp in groups for _ in range(4)],
        compiler_params=_cparams(),
    )(*flat)
    return [tuple(outs[4 * t:4 * t + 4]) for t in range(n)]


def _rope_tables():
    half = QK_ROPE // 2
    f32 = np.float32
    inv_freq = (f32(1.0) / (f32(ROPE_THETA) ** (np.arange(half, dtype=f32) / f32(half)))).astype(f32)
    pos = np.arange(N, dtype=f32) - f32(PAD)
    ang = (pos[:, None] * inv_freq[None, :]).astype(f32)
    cos, sin = np.cos(ang).astype(f32), np.sin(ang).astype(f32)
    zero = np.zeros((N, 128 - QK_ROPE), f32)
    return jnp.asarray(np.concatenate([cos, cos, zero], axis=1)), jnp.asarray(np.concatenate([-sin, sin, zero], axis=1))


def kernel(x, meta_tokens, norm_g, w_in, q_norm_g, w_q_b, kv_norm_g, w_kv_b, pool_w, pool_scale, w_out, final_norm_g, loss_target, m_meta_tokens, m_norm_g, m_w_in, m_q_norm_g, m_w_q_b, m_kv_norm_g, m_w_kv_b, m_pool_w, m_pool_scale, m_w_out, m_final_norm_g, v_meta_tokens, v_norm_g, v_w_in, v_q_norm_g, v_w_q_b, v_kv_norm_g, v_w_kv_b, v_pool_w, v_pool_scale, v_w_out, v_final_norm_g):
    tr = lambda a: a[0].T
    win, wq, wkv, h, tgt = _gather_weights(tr(w_in), tr(w_q_b), w_kv_b[0], meta_tokens, x[0], loss_target[0])
    cosf, sinf = _rope_tables()
    gf = final_norm_g.reshape(1, D)

    part = _local_step(h, tgt, norm_g, win, q_norm_g, wq, kv_norm_g, wkv, pool_w[0], pool_scale, w_out[0], gf, cosf, sinf)

    pw2 = lambda a: a.reshape(len(POOL_WINDOWS) * POOL_GROUP, POOL_GROUP)
    gwinT, gwout, gwqT, gwkv, gmeta, gpw, gg, ggf, ggq, ggkv, gps, gloss = _reduce_grads(
        part["dwin"], part["dwout"], part["dwq"], part["dwkv"], part["dmeta"], pw2(part["dpw"]), part["dg"],
        part["dgf"], part["dgq"], part["dgkv"], part["dps"], part["loss"])

    r_in = _adamw_rows("adamw_w_in", tr(w_in), gwinT, tr(m_w_in), tr(v_w_in), 248)
    r_out = _adamw_rows("adamw_w_out", w_out[0], gwout, m_w_out[0], v_w_out[0], 128)
    fn2 = lambda a: a.reshape(1, D)
    r_meta, r_norm, r_gq, r_wq, r_gkv, r_wkv, r_pw, r_ps, r_fn = _adamw_small([
        (meta_tokens, gmeta, m_meta_tokens, v_meta_tokens),
        (norm_g, gg, m_norm_g, v_norm_g),
        (q_norm_g, ggq, m_q_norm_g, v_q_norm_g),
        (tr(w_q_b), gwqT, tr(m_w_q_b), tr(v_w_q_b)),
        (kv_norm_g, ggkv, m_kv_norm_g, v_kv_norm_g),
        (w_kv_b[0], gwkv, m_w_kv_b[0], v_w_kv_b[0]),
        (pw2(pool_w), gpw, pw2(m_pool_w), pw2(v_pool_w)),
        (pool_scale, gps, m_pool_scale, v_pool_scale),
        (fn2(final_norm_g), ggf, fn2(m_final_norm_g), fn2(v_final_norm_g)),
    ])
    untr = lambda a: a.T[None]
    pw4 = lambda a: a.reshape(1, len(POOL_WINDOWS), POOL_GROUP, POOL_GROUP)
    per_kind = [[
        r_meta[kind], r_norm[kind], untr(r_in[kind]), r_gq[kind], untr(r_wq[kind]), r_gkv[kind], r_wkv[kind][None],
        pw4(r_pw[kind]), r_ps[kind], r_out[kind][None], r_fn[kind].reshape(D),
    ] for kind in range(4)]
    return (gloss[0, 0], part["gx"][None], *per_kind[0], *per_kind[1], *per_kind[2], *per_kind[3])
```

```python
import jax
import jax.numpy as jnp
import numpy as np
from jax import lax
from jax.experimental import pallas as pl
from jax.experimental.pallas import tpu as pltpu

F32 = jnp.float32
BF16 = jnp.bfloat16

D = 1024
S = 2048
N_META = 16
PAD = 112
HEAD_ROWS = PAD + N_META
N = HEAD_ROWS + S
D_POOL = 512
POOL_WINDOWS = (2, 4, 8, 16)
POOL_GROUP = 128
HALO = 16
HEADS = 4
QK_NOPE = 128
QK_ROPE = 64
QK = QK_NOPE + QK_ROPE
V_HEAD = 128
Q_LORA = 256
KV_LORA = 128
D_IN = 1984
EPS = 1e-6
ROPE_THETA = 10000.0
SCALE = QK ** -0.5
CHIPS = 4

ROWS_FWD = 544
ROWS_MID = 544
ROWS_BWD = 544
TK = 128
TQ = 256
NQ = S // TQ
HEADS_PER_STEP_BWD = 2

O_PI, O_PG, O_CQ, O_CKV, O_KR, O_AG = 0, 512, 1024, 1280, 1408, 1472
O_KR_END = O_KR + 128
SHARD_IN = D_IN // CHIPS
SHARD_OUT = D // CHIPS

LR, B1, B2, ADAM_EPS, WD, STEP = 0.001, 0.9, 0.999, 1e-08, 0.01, 10
C1 = 1.0 - B1**STEP
C2 = 1.0 - B2**STEP

VMEM_LIMIT = 60 * 1024 * 1024
MESH = pl.DeviceIdType.MESH
NEG = -1e30

VEC_ROWS = 8
V_GQ, V_GKV, V_PS, V_LOSS = 0, 256, 384, 896


def _cparams(**kw):
    return pltpu.CompilerParams(vmem_limit_bytes=VMEM_LIMIT, **kw)


def _nt(a, b):
    return lax.dot_general(a, b, (((1,), (1,)), ((), ())), preferred_element_type=F32)


def _tn(a, b):
    return lax.dot_general(a, b, (((0,), (0,)), ((), ())), preferred_element_type=F32)


def _nn(a, b):
    return jnp.dot(a, b, preferred_element_type=F32)


def _swap64(t):
    return pltpu.roll(t, 32, 1) + pltpu.roll(t, 96, 1)


def _sigmoid(x):
    return 1.0 / (1.0 + jnp.exp(-x))


def _low_lanes():
    return (lax.broadcasted_iota(jnp.int32, (1, 128), 1) < QK_ROPE).astype(F32)


def _rows(w, rows):
    return pl.BlockSpec((rows, w), lambda i: (i, 0))


def _const(*shape):
    return pl.BlockSpec(shape, lambda *_: (0,) * len(shape), pipeline_mode=pl.Buffered(1))


STAT_GROUPS = HEADS // HEADS_PER_STEP_BWD


def _stat_slot(head):
    return head // HEADS_PER_STEP_BWD, head % HEADS_PER_STEP_BWD


def _attn_tiles():
    return [(0, TK, TK)] + [(TK + TQ * t, TQ, TK + TQ * (t + 1)) for t in range(NQ)]


def _masked_scores(q, k, rows, klen):
    s = _nt(q, k)
    col = lax.broadcasted_iota(jnp.int32, (1, TK), 1)
    head_bias = jnp.where(col >= PAD, 0.0, NEG)
    if klen == TK:
        return s + head_bias
    r = lax.broadcasted_iota(jnp.int32, (rows, 1), 0) >> 6
    c = lax.broadcasted_iota(jnp.int32, (1, rows), 1) >> 6
    diag_bias = jnp.where(c <= r, 0.0, NEG)
    parts = [s[:, 0:TK] + head_bias]
    if klen - rows > TK:
        parts.append(s[:, TK:klen - rows])
    parts.append(s[:, klen - rows:klen] + diag_bias)
    return jnp.concatenate(parts, axis=1)


def _fwd_in(h, norm_g, win, gq, wq, gkv, wkv, cosf, sinf):
    tr = ROWS_FWD

    def body(h_ref, g_ref, win_ref, gq_ref, wq_ref, gkv_ref, wkv_ref, cos_ref, sin_ref,
             pi_ref, pg_ref, cq_ref, ckv_ref, ag_ref, q_ref, k_ref, v_ref):
        h = h_ref[...]
        r = lax.rsqrt(jnp.mean(h * h, axis=-1, keepdims=True) + EPS)
        hn = ((h * r) * g_ref[...]).astype(BF16)
        u = _nt(hn, win_ref[0:O_KR_END, :])
        pi_ref[...] = u[:, O_PI:O_PG]
        pg_ref[...] = u[:, O_PG:O_CQ]
        cq = u[:, O_CQ:O_CKV]
        ckv = u[:, O_CKV:O_KR]
        cq_ref[...] = cq
        ckv_ref[...] = ckv
        ag_ref[...] = _nt(hn, win_ref[O_AG:D_IN, :])
        cosv = cos_ref[...]
        sinv = sin_ref[...]
        kr = u[:, O_KR:O_KR_END] * _low_lanes()
        kr = (kr * cosv + _swap64(kr) * sinv).astype(BF16)
        rq = lax.rsqrt(jnp.mean(cq * cq, axis=-1, keepdims=True) + EPS)
        cqn = ((cq * rq) * gq_ref[...]).astype(BF16)
        rkv = lax.rsqrt(jnp.mean(ckv * ckv, axis=-1, keepdims=True) + EPS)
        ckvn = ((ckv * rkv) * gkv_ref[...]).astype(BF16)
        for hd in range(HEADS):
            qh = _nt(cqn, wq_ref[hd]) * SCALE
            z = qh[:, QK_NOPE:]
            q_ref[hd, :, 0:QK_NOPE] = qh[:, 0:QK_NOPE].astype(BF16)
            q_ref[hd, :, QK_NOPE:] = (z * cosv + _swap64(z) * sinv).astype(BF16)
            kvh = _nn(ckvn, wkv_ref[hd])
            k_ref[hd, :, 0:QK_NOPE] = kvh[:, 0:QK_NOPE].astype(BF16)
            k_ref[hd, :, QK_NOPE:] = kr
            v_ref[hd] = kvh[:, QK_NOPE:].astype(BF16)

    head = lambda w: pl.BlockSpec((HEADS, tr, w), lambda i: (0, i, 0))
    return pl.pallas_call(
        body,
        name="fwd_in",
        grid=(N // tr,),
        in_specs=[
            _rows(D, tr), _const(1, D), _const(D_IN, D), _const(1, Q_LORA), _const(HEADS, 256, Q_LORA),
            _const(1, KV_LORA), _const(HEADS, KV_LORA, 256), _rows(128, tr), _rows(128, tr),
        ],
        out_specs=[_rows(D_POOL, tr), _rows(D_POOL, tr), _rows(Q_LORA, tr), _rows(KV_LORA, tr), _rows(D_POOL, tr),
                   head(256), head(256), head(V_HEAD)],
        out_shape=[
            jax.ShapeDtypeStruct((N, D_POOL), F32), jax.ShapeDtypeStruct((N, D_POOL), F32),
            jax.ShapeDtypeStruct((N, Q_LORA), F32), jax.ShapeDtypeStruct((N, KV_LORA), F32),
            jax.ShapeDtypeStruct((N, D_POOL), F32),
            jax.ShapeDtypeStruct((HEADS, N, 256), BF16), jax.ShapeDtypeStruct((HEADS, N, 256), BF16),
            jax.ShapeDtypeStruct((HEADS, N, V_HEAD), BF16),
        ],
        compiler_params=_cparams(dimension_semantics=("arbitrary",)),
    )(h, norm_g, win, gq, wq, gkv, wkv, cosf, sinf)


def _attn_fwd(q, k, v, wout_s):
    tiles = _attn_tiles()
    n_t = len(tiles)
    half = SHARD_OUT // 2
    fwd_step = n_t - 2

    def body(q_hbm, k_hbm, v_hbm, wout_ref, o_hbm, lse_ref, wout_o, q_buf, k_buf, v_buf, o_buf, s_wout, in_sems, out_sems,
             ici_send, ici_recv, fwd_send, fwd_recv, own_sem):
        step = pl.program_id(0)
        x, y, c = lax.axis_index("x"), lax.axis_index("y"), lax.axis_index("c")
        me = 2 * x + y

        def chip_of(rel):
            fx, fy = _CHIP_RELS[rel]
            return 2 * (x ^ fx) + (y ^ fy)

        def place(chip, core):
            return wout_o.at[pl.ds(pl.multiple_of(SHARD_OUT * chip + half * core, half), half), :]

        def ici_copy(rel, src_chip, to):
            return _remote(s_wout.at[pl.ds(pl.multiple_of(half * c, half), half), :], place(src_chip, c),
                           ici_send.at[rel - 1], ici_recv.at[rel - 1], to)

        def fwd_copy(rel, core, to):
            spot = place(chip_of(rel), core)
            return _remote(spot, spot, fwd_send.at[rel - 1], fwd_recv.at[rel - 1], to)

        own = pltpu.make_async_copy(s_wout, wout_o.at[pl.ds(pl.multiple_of(SHARD_OUT * me, SHARD_OUT), SHARD_OUT), :], own_sem)

        @pl.when(step == 0)
        def _():
            s_wout[...] = wout_ref[...].astype(BF16)
            own.start()
            for rel in (1, 2, 3):
                fx, fy = _CHIP_RELS[rel]
                ici_copy(rel, me, (x ^ fx, y ^ fy, c)).start()

        @pl.when(step == fwd_step)
        def _():
            for rel in (1, 2, 3):
                ici_copy(rel, chip_of(rel), (x, y, c)).wait_recv()
                fwd_copy(rel, c, (x, y, 1 - c)).start()

        def finish_wout():
            for rel in (1, 2, 3):
                fwd_copy(rel, 1 - c, (x, y, c)).wait_recv()
            for rel in (1, 2, 3):
                ici_copy(rel, me, (x, y, c)).wait_send()
                fwd_copy(rel, c, (x, y, c)).wait_send()
            own.wait()

        def loads(idx):
            q0, rows, _ = tiles[idx]
            rs = pl.ds(q0, rows)
            return [pltpu.make_async_copy(src.at[:, rs, :], dst.at[:, rs, :], in_sems.at[a, idx % 2])
                    for a, (src, dst) in enumerate(((q_hbm, q_buf), (k_hbm, k_buf), (v_hbm, v_buf)))]

        def store(idx):
            q0, rows, _ = tiles[idx]
            return pltpu.make_async_copy(o_buf.at[idx % 2, pl.ds(0, rows), :], o_hbm.at[pl.ds(q0, rows), :],
                                         out_sems.at[idx % 2])

        @pl.when(step == 0)
        def _():
            lse_ref[...] = jnp.zeros_like(lse_ref)
            for cp in loads(0):
                cp.start()

        for idx, (q0, rows, klen) in enumerate(tiles):
            @pl.when(step == idx)
            def _(idx=idx, q0=q0, rows=rows, klen=klen):
                for cp in loads(idx):
                    cp.wait()
                if idx + 1 < n_t:
                    for cp in loads(idx + 1):
                        cp.start()
                if idx >= 2:
                    store(idx - 2).wait()
                for hd in range(HEADS):
                    s = _masked_scores(q_buf[hd, q0:q0 + rows, :], k_buf[hd, 0:klen, :], rows, klen)
                    m = jnp.max(s, axis=-1, keepdims=True)
                    p = jnp.exp(s - m)
                    l = jnp.sum(p, axis=-1, keepdims=True)
                    o_buf[idx % 2, 0:rows, hd * V_HEAD:(hd + 1) * V_HEAD] = _nn(p.astype(BF16), v_buf[hd, 0:klen, :]) / l
                    grp, lane = _stat_slot(hd)
                    lse_ref[grp, q0:q0 + rows, lane:lane + 1] = m + jnp.log(l)
                store(idx).start()
                if idx == n_t - 1:
                    store(idx - 1).wait()
                    store(idx).wait()
                    finish_wout()

    hbm = pl.BlockSpec(memory_space=pl.ANY)
    return pl.pallas_call(
        body,
        name="attn_fwd",
        grid=(n_t,),
        in_specs=[hbm, hbm, hbm, _const(SHARD_OUT, D)],
        out_specs=[hbm, _const(STAT_GROUPS, N, 128), hbm],
        out_shape=[jax.ShapeDtypeStruct((N, HEADS * V_HEAD), F32), jax.ShapeDtypeStruct((STAT_GROUPS, N, 128), F32),
                   jax.ShapeDtypeStruct((D, D), BF16)],
        scratch_shapes=[pltpu.VMEM((HEADS, N, 256), BF16), pltpu.VMEM((HEADS, N, 256), BF16),
                        pltpu.VMEM((HEADS, N, V_HEAD), BF16), pltpu.VMEM((2, TQ, HEADS * V_HEAD), F32),
                        pltpu.VMEM((SHARD_OUT, D), BF16),
                        pltpu.SemaphoreType.DMA((3, 2)), pltpu.SemaphoreType.DMA((2,))]
        + [pltpu.SemaphoreType.DMA((3,))] * 4 + [pltpu.SemaphoreType.DMA],
        compiler_params=_cparams(dimension_semantics=("arbitrary",)),
    )(q, k, v, wout_s)


def _inv_count(row0, rows, w):
    row = row0 + lax.broadcasted_iota(jnp.int32, (rows, 1), 0)
    return 1.0 / jnp.clip(row - (PAD - 1), 1, w).astype(F32)


def _mid(h, tgt, pool_in, pool_gate, attn_gate, attn, pool_w, pool_scale, wout, gf):
    tr = ROWS_MID
    per = tr // HALO
    ng = len(POOL_WINDOWS)

    def body(h_ref, t_ref, pin_ref, halo_ref, pg_ref, ag_ref, at_ref, pw_ref, ps_ref, wout_ref, gf_ref,
             dh2_ref, do_ref, delta_ref, dag_ref, dpg_ref, dpl_ref, dwout_ref, dpw_ref, dps_ref, dgf_ref, loss_ref):
        i = pl.program_id(0)

        @pl.when(i == 0)
        def _():
            dwout_ref[...] = jnp.zeros_like(dwout_ref)
            dpw_ref[...] = jnp.zeros_like(dpw_ref)
            dps_ref[...] = jnp.zeros_like(dps_ref)
            dgf_ref[...] = jnp.zeros_like(dgf_ref)
            loss_ref[...] = jnp.zeros_like(loss_ref)

        row0 = i * tr
        real = (row0 + lax.broadcasted_iota(jnp.int32, (tr, 1), 0)) >= HEAD_ROWS
        h = h_ref[...]

        halo = jnp.where(i > 0, halo_ref[...], 0.0)
        ext = jnp.concatenate([halo, pin_ref[...]], axis=0)
        pooled = []
        for g, w in enumerate(POOL_WINDOWS):
            e = ext[:, g * POOL_GROUP:(g + 1) * POOL_GROUP]
            acc = e
            shift = 1
            while shift < w:
                acc = acc + pltpu.roll(acc, shift, 0)
                shift *= 2
            pooled.append((acc[HALO:] * _inv_count(row0, tr, w) - e[HALO:]).astype(BF16))
        pw = [pw_ref[g].astype(BF16) for g in range(ng)]
        mixed = jnp.concatenate([_nn(pooled[g], pw[g]) for g in range(ng)], axis=1)
        ps = ps_ref[...]
        mixed_s = mixed * ps
        pg = pg_ref[...]
        sig_p = _sigmoid(pg)
        silu_p = pg * sig_p
        pool_out = (silu_p * mixed_s).astype(BF16)
        ag = ag_ref[...]
        sig_a = _sigmoid(ag)
        silu_a = ag * sig_a
        at = at_ref[...]
        attn_out = (silu_a * at).astype(BF16)
        mix = _nn(pool_out, wout_ref[0:D_POOL, :]) + _nn(attn_out, wout_ref[D_POOL:D, :])
        h2 = h + mix

        r2 = lax.rsqrt(jnp.mean(h2 * h2, axis=-1, keepdims=True) + EPS)
        n2 = h2 * r2
        gfv = gf_ref[...]
        err = jnp.where(real, n2 * gfv - t_ref[...], 0.0)
        loss_ref[...] += jnp.sum(jnp.sum(err * err, axis=-1, keepdims=True), axis=0, keepdims=True) * (0.5 / D)
        dy = err * (1.0 / D)
        dgf_ref[...] += jnp.sum(dy * n2, axis=0, keepdims=True)
        dn = dy * gfv
        dh2 = r2 * (dn - n2 * jnp.mean(dn * n2, axis=-1, keepdims=True))
        dh2_ref[...] = dh2
        dh2b = dh2.astype(BF16)

        dwout_ref[0:D_POOL, :] += _tn(pool_out, dh2b)
        dwout_ref[D_POOL:D, :] += _tn(attn_out, dh2b)
        dcat = _nt(dh2b, wout_ref[...])
        dpo = dcat[:, 0:D_POOL]
        dao = dcat[:, D_POOL:D]
        do = dao * silu_a
        prod = do * at
        delta_ref[...] = jnp.zeros_like(delta_ref)
        for hd in range(HEADS):
            grp, lane = _stat_slot(hd)
            cols = slice(hd * V_HEAD, (hd + 1) * V_HEAD)
            do_ref[grp, :, lane * V_HEAD:(lane + 1) * V_HEAD] = do[:, cols].astype(BF16)
            delta_ref[grp, :, lane:lane + 1] = jnp.sum(prod[:, cols], axis=-1, keepdims=True)
        dag_ref[...] = (dao * at * (sig_a * (1.0 + ag * (1.0 - sig_a)))).astype(BF16)
        dmixed_s = dpo * silu_p
        dpg_ref[...] = (dpo * mixed_s * (sig_p * (1.0 + pg * (1.0 - sig_p)))).astype(BF16)
        dps_ref[...] += jnp.sum(dmixed_s * mixed, axis=0, keepdims=True)
        dmixed = (dmixed_s * ps).astype(BF16)
        dpl = []
        for g in range(ng):
            dm = dmixed[:, g * POOL_GROUP:(g + 1) * POOL_GROUP]
            dpl.append(_nt(dm, pw[g]))
            dpw_ref[g] += _tn(pooled[g], dm)
        dpl_ref[...] = jnp.concatenate(dpl, axis=1)

    halo_spec = pl.BlockSpec((HALO, D_POOL), lambda i: (jnp.maximum(i * per - 1, 0), 0))
    return pl.pallas_call(
        body,
        name="mid",
        grid=(N // tr,),
        in_specs=[
            _rows(D, tr), _rows(D, tr), _rows(D_POOL, tr), halo_spec, _rows(D_POOL, tr), _rows(D_POOL, tr),
            _rows(D_POOL, tr), _const(ng, POOL_GROUP, POOL_GROUP), _const(1, D_POOL), _const(D, D), _const(1, D),
        ],
        out_specs=[
            _rows(D, tr), pl.BlockSpec((STAT_GROUPS, tr, HEADS_PER_STEP_BWD * V_HEAD), lambda i: (0, i, 0)),
            pl.BlockSpec((STAT_GROUPS, tr, 128), lambda i: (0, i, 0)),
            _rows(D_POOL, tr), _rows(D_POOL, tr), _rows(D_POOL, tr),
            _const(D, D), _const(ng, POOL_GROUP, POOL_GROUP), _const(1, D_POOL), _const(1, D), _const(1, 128),
        ],
        out_shape=[
            jax.ShapeDtypeStruct((N, D), F32), jax.ShapeDtypeStruct((STAT_GROUPS, N, HEADS_PER_STEP_BWD * V_HEAD), BF16),
            jax.ShapeDtypeStruct((STAT_GROUPS, N, 128), F32),
            jax.ShapeDtypeStruct((N, D_POOL), BF16), jax.ShapeDtypeStruct((N, D_POOL), BF16),
            jax.ShapeDtypeStruct((N, D_POOL), F32), jax.ShapeDtypeStruct((D, D), F32),
            jax.ShapeDtypeStruct((ng, POOL_GROUP, POOL_GROUP), F32),
            jax.ShapeDtypeStruct((1, D_POOL), F32), jax.ShapeDtypeStruct((1, D), F32), jax.ShapeDtypeStruct((1, 128), F32),
        ],
        compiler_params=_cparams(dimension_semantics=("arbitrary",)),
    )(h, tgt, pool_in, pool_in, pool_gate, attn_gate, attn, pool_w, pool_scale, wout, gf)


def _unrope(dy, cosv, sinv):
    return dy * cosv + _swap64(dy * sinv) * _low_lanes()


def _attn_bwd(q, k, v, do, lse, delta, cosf, sinf):
    tiles = _attn_tiles()
    hp = HEADS_PER_STEP_BWD
    n_g = HEADS // hp
    n_t = len(tiles)

    def body(q_hbm, k_hbm, v_hbm, do_hbm, lse_ref, delta_ref, cos_ref, sin_ref, dq_hbm, dkv_ref, dkr_ref,
             q_buf, k_buf, v_buf, do_buf, dq_buf, dk_acc, dv_acc, in_sems, out_sems):
        grp = pl.program_id(0)
        step = pl.program_id(1)
        heads = pl.ds(grp * hp, hp)

        def loads(g, idx):
            q0, rows, _ = tiles[idx]
            rs = pl.ds(q0, rows)
            par = (g * n_t + idx) % 2
            hs = pl.ds(g * hp, hp)
            pairs = ((q_hbm.at[hs, rs, :], q_buf.at[:, rs, :]), (k_hbm.at[hs, rs, :], k_buf.at[:, rs, :]),
                     (v_hbm.at[hs, rs, :], v_buf.at[:, rs, :]), (do_hbm.at[g, rs, :], do_buf.at[rs, :]))
            return [pltpu.make_async_copy(src, dst, in_sems.at[a, par]) for a, (src, dst) in enumerate(pairs)]

        def store(idx):
            q0, rows, _ = tiles[idx]
            return pltpu.make_async_copy(dq_buf.at[idx % 2, :, pl.ds(0, rows), :], dq_hbm.at[heads, pl.ds(q0, rows), :],
                                         out_sems.at[idx % 2])

        @pl.when(step == 0)
        def _():
            dk_acc[...] = jnp.zeros_like(dk_acc)
            dv_acc[...] = jnp.zeros_like(dv_acc)

        @pl.when((step == 0) & (grp == 0))
        def _():
            dkr_ref[...] = jnp.zeros_like(dkr_ref)
            for cp in loads(grp, 0):
                cp.start()

        for idx, (q0, rows, klen) in enumerate(tiles):
            @pl.when(step == idx)
            def _(idx=idx, q0=q0, rows=rows, klen=klen):
                for cp in loads(grp, idx):
                    cp.wait()
                if idx + 1 < n_t:
                    for cp in loads(grp, idx + 1):
                        cp.start()
                if idx >= 2:
                    store(idx - 2).wait()
                qs = pl.ds(q0, rows)
                for hd in range(hp):
                    qv = q_buf[hd, qs, :]
                    kv = k_buf[hd, 0:klen, :]
                    p = jnp.exp(_masked_scores(qv, kv, rows, klen) - lse_ref[0, qs, hd:hd + 1])
                    dob = do_buf[qs, hd * V_HEAD:(hd + 1) * V_HEAD]
                    ds = (p * (_nt(dob, v_buf[hd, 0:klen, :]) - delta_ref[0, qs, hd:hd + 1])).astype(BF16)
                    dq = _nn(ds, kv) * SCALE
                    dq_buf[idx % 2, hd, 0:rows, 0:QK_NOPE] = dq[:, 0:QK_NOPE].astype(BF16)
                    dq_buf[idx % 2, hd, 0:rows, QK_NOPE:] = _unrope(dq[:, QK_NOPE:], cos_ref[qs, :], sin_ref[qs, :]).astype(BF16)
                    dk_acc[hd, 0:klen, :] += _tn(ds, qv)
                    dv_acc[hd, 0:klen, :] += _tn(p.astype(BF16), dob)
                store(idx).start()

        @pl.when(step == n_t - 1)
        def _():
            @pl.when(grp + 1 < n_g)
            def _():
                for cp in loads(grp + 1, 0):
                    cp.start()

            for hd in range(hp):
                dkv_ref[hd, :, 0:QK_NOPE] = dk_acc[hd, :, 0:QK_NOPE].astype(BF16)
                dkv_ref[hd, :, QK_NOPE:] = dv_acc[hd].astype(BF16)
                dkr_ref[...] += dk_acc[hd, :, QK_NOPE:]
            store(n_t - 2).wait()
            store(n_t - 1).wait()

    hbm = pl.BlockSpec(memory_space=pl.ANY)
    stat = pl.BlockSpec((1, N, 128), lambda g, t: (g, 0, 0), pipeline_mode=pl.Buffered(1))
    return pl.pallas_call(
        body,
        name="attn_bwd",
        grid=(n_g, n_t),
        in_specs=[hbm, hbm, hbm, hbm, stat, stat, _const(N, 128), _const(N, 128)],
        out_specs=[hbm, pl.BlockSpec((hp, N, 256), lambda g, t: (g, 0, 0), pipeline_mode=pl.Buffered(1)), _const(N, 128)],
        out_shape=[
            jax.ShapeDtypeStruct((HEADS, N, 256), BF16), jax.ShapeDtypeStruct((HEADS, N, 256), BF16),
            jax.ShapeDtypeStruct((N, 128), F32),
        ],
        scratch_shapes=[pltpu.VMEM((hp, N, 256), BF16), pltpu.VMEM((hp, N, 256), BF16), pltpu.VMEM((hp, N, V_HEAD), BF16),
                        pltpu.VMEM((N, hp * V_HEAD), BF16), pltpu.VMEM((2, hp, TQ, 256), BF16),
                        pltpu.VMEM((hp, N, 256), F32), pltpu.VMEM((hp, N, V_HEAD), F32),
                        pltpu.SemaphoreType.DMA((4, 2)), pltpu.SemaphoreType.DMA((2,))],
        compiler_params=_cparams(dimension_semantics=("arbitrary", "arbitrary")),
    )(q, k, v, do, lse, delta, cosf, sinf)


def _bwd_in(h, dh2, dq, dkv, dkr, cq, ckv, dpl, dpg, dag, norm_g, win, gq, wq, gkv, wkv, cosf, sinf):
    tr = ROWS_BWD
    nb = N // tr
    per = tr // HALO
    lead = HEAD_ROWS

    def body(h_ref, dh2_ref, dq_ref, dkv_ref, dkr_ref, cq_ref, ckv_ref, dpl_ref, halo_ref, dpg_ref, dag_ref,
             g_ref, win_ref, gq_ref, wq_ref, gkv_ref, wkv_ref, cos_ref, sin_ref,
             gx_ref, dmeta_ref, dwin_ref, dwq_ref, dwkv_ref, dg_ref, dgq_ref, dgkv_ref, dh_buf, gx_sem):
        i = pl.program_id(0)

        @pl.when(i == 0)
        def _():
            dwin_ref[...] = jnp.zeros_like(dwin_ref)
            dwq_ref[...] = jnp.zeros_like(dwq_ref)
            dwkv_ref[...] = jnp.zeros_like(dwkv_ref)
            dg_ref[...] = jnp.zeros_like(dg_ref)
            dgq_ref[...] = jnp.zeros_like(dgq_ref)
            dgkv_ref[...] = jnp.zeros_like(dgkv_ref)

        row0 = i * tr
        h = h_ref[...]
        r = lax.rsqrt(jnp.mean(h * h, axis=-1, keepdims=True) + EPS)
        n = h * r
        gv = g_ref[...]
        hn = (n * gv).astype(BF16)
        cq = cq_ref[...]
        rq = lax.rsqrt(jnp.mean(cq * cq, axis=-1, keepdims=True) + EPS)
        nq = cq * rq
        gqv = gq_ref[...]
        cqn = (nq * gqv).astype(BF16)
        dcqn = jnp.zeros((tr, Q_LORA), F32)
        for hd in range(HEADS):
            dqf = dq_ref[hd]
            dcqn = dcqn + _nn(dqf, wq_ref[hd])
            dwq_ref[hd] += _tn(dqf, cqn)
        dgq_ref[...] += jnp.sum(dcqn * nq, axis=0, keepdims=True)
        dnq = dcqn * gqv
        dcq = rq * (dnq - nq * jnp.mean(dnq * nq, axis=-1, keepdims=True))

        ckv = ckv_ref[...]
        rkv = lax.rsqrt(jnp.mean(ckv * ckv, axis=-1, keepdims=True) + EPS)
        nkv = ckv * rkv
        gkvv = gkv_ref[...]
        ckvn = (nkv * gkvv).astype(BF16)
        dckvn = jnp.zeros((tr, KV_LORA), F32)
        for hd in range(HEADS):
            dkv = dkv_ref[hd]
            dckvn = dckvn + _nt(dkv, wkv_ref[hd])
            dwkv_ref[hd] += _tn(ckvn, dkv)
        dgkv_ref[...] += jnp.sum(dckvn * nkv, axis=0, keepdims=True)
        dnkv = dckvn * gkvv
        dckv = rkv * (dnkv - nkv * jnp.mean(dnkv * nkv, axis=-1, keepdims=True))
        dkr = _unrope(dkr_ref[...], cos_ref[...], sin_ref[...])

        cur = dpl_ref[...]
        halo = jnp.where(i < nb - 1, halo_ref[...], 0.0)
        dpi = []
        for g, w in enumerate(POOL_WINDOWS):
            sl = slice(g * POOL_GROUP, (g + 1) * POOL_GROUP)
            a = jnp.concatenate([cur[:, sl] * _inv_count(row0, tr, w), halo[:, sl] * _inv_count(row0 + tr, HALO, w)], axis=0)
            acc = a
            shift = 1
            while shift < w:
                acc = acc + pltpu.roll(acc, tr + HALO - shift, 0)
                shift *= 2
            dpi.append(acc[0:tr] - cur[:, sl])

        du = jnp.concatenate([t.astype(BF16) for t in dpi] + [dpg_ref[...]] + [t.astype(BF16) for t in (dcq, dckv, dkr)],
                             axis=1)
        dagb = dag_ref[...]
        dwin_ref[0:O_KR_END, :] += _tn(du, hn)
        dwin_ref[O_AG:D_IN, :] += _tn(dagb, hn)
        dhn = _nn(du, win_ref[0:O_KR_END, :]) + _nn(dagb, win_ref[O_AG:D_IN, :])
        dg_ref[...] += jnp.sum(dhn * n, axis=0, keepdims=True)
        dn = dhn * gv
        dh = dh2_ref[...] + r * (dn - n * jnp.mean(dn * n, axis=-1, keepdims=True))

        first = pltpu.make_async_copy(dh_buf.at[pl.ds(lead, tr - lead), :], gx_ref.at[pl.ds(0, tr - lead), :], gx_sem)
        later = lambda step: pltpu.make_async_copy(
            dh_buf, gx_ref.at[pl.ds(pl.multiple_of(step * tr - lead, 16), tr), :], gx_sem)

        @pl.when(i == 1)
        def _():
            first.wait()

        @pl.when(i > 1)
        def _():
            later(i - 1).wait()

        dh_buf[...] = dh

        @pl.when(i == 0)
        def _():
            first.start()
            for chip in range(CHIPS):
                dmeta_ref[chip] = dh[PAD:HEAD_ROWS, chip * 256:(chip + 1) * 256]

        @pl.when(i > 0)
        def _():
            later(i).start()

        @pl.when(i == nb - 1)
        def _():
            later(i).wait()

    head = lambda w: pl.BlockSpec((HEADS, tr, w), lambda i: (0, i, 0))
    halo_spec = pl.BlockSpec((HALO, D_POOL), lambda i: (jnp.minimum((i + 1) * per, N // HALO - 1), 0))
    return pl.pallas_call(
        body,
        name="bwd_in",
        grid=(nb,),
        in_specs=[
            _rows(D, tr), _rows(D, tr), head(256), head(256), _rows(128, tr), _rows(Q_LORA, tr), _rows(KV_LORA, tr),
            _rows(D_POOL, tr), halo_spec, _rows(D_POOL, tr), _rows(D_POOL, tr),
            _const(1, D), _const(D_IN, D), _const(1, Q_LORA), _const(HEADS, 256, Q_LORA),
            _const(1, KV_LORA), _const(HEADS, KV_LORA, 256), _rows(128, tr), _rows(128, tr),
        ],
        out_specs=[
            pl.BlockSpec(memory_space=pl.ANY), _const(CHIPS, N_META, 256), _const(D_IN, D), _const(HEADS, 256, Q_LORA),
            _const(HEADS, KV_LORA, 256), _const(1, D), _const(1, Q_LORA), _const(1, KV_LORA),
        ],
        out_shape=[
            jax.ShapeDtypeStruct((S, D), F32), jax.ShapeDtypeStruct((CHIPS, N_META, 256), F32),
            jax.ShapeDtypeStruct((D_IN, D), F32), jax.ShapeDtypeStruct((HEADS, 256, Q_LORA), F32),
            jax.ShapeDtypeStruct((HEADS, KV_LORA, 256), F32),
            jax.ShapeDtypeStruct((1, D), F32), jax.ShapeDtypeStruct((1, Q_LORA), F32), jax.ShapeDtypeStruct((1, KV_LORA), F32),
        ],
        scratch_shapes=[pltpu.VMEM((tr, D), F32), pltpu.SemaphoreType.DMA],
        compiler_params=_cparams(dimension_semantics=("arbitrary",)),
    )(h, dh2, dq, dkv, dkr, cq, ckv, dpl, dpl, dpg, dag, norm_g, win, gq, wq, gkv, wkv, cosf, sinf)


def _local_step(h, tgt, norm_g, win, gq, wq, gkv, wkv, pool_w, pool_scale, wout_s, gf, cosf, sinf):
    pool_in, pool_gate, cq, ckv, attn_gate, q, k, v = _fwd_in(h, norm_g, win, gq, wq, gkv, wkv, cosf, sinf)
    attn, lse, wout = _attn_fwd(q, k, v, wout_s)
    dh2, do, delta, dag, dpg, dpl, dwout, dpw, dps, dgf, loss = _mid(
        h, tgt, pool_in, pool_gate, attn_gate, attn, pool_w, pool_scale, wout, gf)
    dq, dkv, dkr = _attn_bwd(q, k, v, do, lse, delta, cosf, sinf)
    gx, dmeta, dwin, dwq, dwkv, dg, dgq, dgkv = _bwd_in(
        h, dh2, dq, dkv, dkr, cq, ckv, dpl, dpg, dag, norm_g, win, gq, wq, gkv, wkv, cosf, sinf)
    return dict(gx=gx, dmeta=dmeta, dwin=dwin, dwq=dwq, dwkv=dwkv, dwout=dwout, dg=dg, dgq=dgq, dgkv=dgkv,
                dpw=dpw, dps=dps, dgf=dgf, loss=loss)


_CHIP_RELS = ((0, 0), (1, 0), (0, 1), (1, 1))

_ARR_ROWS = (SHARD_IN, SHARD_OUT, 256, KV_LORA, N_META)
_ARR_COLS = (D, D, Q_LORA, 256, 256)
_PIECES = (
    (0, 0, 256, 0), (0, 256, SHARD_IN - 256, 1),
    (1, 0, 128, 0), (1, 128, 128, 1),
    (2, 0, 128, 0), (2, 128, 128, 1),
    (3, 0, 64, 0), (3, 64, 64, 1),
    (4, 0, N_META, 0),
)
_NP = len(_PIECES)
_PIECE_MAX = (256, 128, 128, 64, N_META)


def _gathered_at(refs, arr, chip, r0, n):
    if arr in (0, 1):
        return refs[arr].at[pl.ds(pl.multiple_of(_ARR_ROWS[arr] * chip + r0, 16), n), :]
    return refs[arr].at[chip, pl.ds(r0, n), :]


def _remote(src, dst, send_sem, recv_sem, to):
    return pltpu.make_async_remote_copy(src_ref=src, dst_ref=dst, send_sem=send_sem, recv_sem=recv_sem,
                                        device_id=to, device_id_type=MESH)


def _gather_weights(winT_s, wqT_s, wkv_s, meta_s, x2, tgt2):
    arrays = (0, 2, 3, 4)

    def body(win_ref, wq_ref, wkv_ref, meta_ref, x_ref, t_ref, win_o, wq_o, wkv_o, h_o, tp_o,
             s_win, s_wq, s_wkv, meta_all, head_buf, x_buf, t_buf, ici_send, ici_recv, fwd_send, fwd_recv,
             loc_sems, own_sems):
        x, y, c = lax.axis_index("x"), lax.axis_index("y"), lax.axis_index("c")
        me = 2 * x + y
        stage = (s_win, None, s_wq, s_wkv, meta_ref)
        outs = (win_o, None, wq_o, wkv_o, meta_all)

        frames = pl.ds(HEAD_ROWS, S)
        loads = [pltpu.make_async_copy(x_ref, x_buf, loc_sems.at[0]), pltpu.make_async_copy(t_ref, t_buf, loc_sems.at[1])]
        local = [pltpu.make_async_copy(x_buf, h_o.at[frames, :], loc_sems.at[0]),
                 pltpu.make_async_copy(t_buf, tp_o.at[frames, :], loc_sems.at[1])]
        for cp in loads:
            cp.start()

        s_win[...] = win_ref[...].astype(BF16)
        s_wq[0:QK, :] = wq_ref[...].astype(BF16)
        s_wq[QK:256, :] = jnp.zeros((256 - QK, Q_LORA), BF16)
        s_wkv[...] = wkv_ref[...].astype(BF16)

        def chip_of(rel):
            fx, fy = _CHIP_RELS[rel]
            return 2 * (x ^ fx) + (y ^ fy)

        def same_core_of(rel):
            fx, fy = _CHIP_RELS[rel]
            return (x ^ fx, y ^ fy, c)

        def ici_copy(rel, i, src_chip, to):
            arr, r0, n, _ = _PIECES[i]
            k = (rel - 1) * _NP + i
            return _remote(stage[arr].at[pl.ds(r0, n), :], _gathered_at(outs, arr, src_chip, r0, n),
                           ici_send.at[k], ici_recv.at[k], to)

        def fwd_copy(rel, i, to):
            arr, r0, n, _ = _PIECES[i]
            k = (rel - 1) * _NP + i
            place = _gathered_at(outs, arr, chip_of(rel), r0, n)
            return _remote(place, place, fwd_send.at[k], fwd_recv.at[k], to)

        for core in (0, 1):
            @pl.when(c == core)
            def _(core=core):
                mine = [i for i in range(_NP) if _PIECES[i][3] == core and _PIECES[i][0] in arrays]
                theirs = [i for i in range(_NP) if _PIECES[i][3] != core and _PIECES[i][0] in arrays]
                sends = [ici_copy(rel, i, me, same_core_of(rel)) for rel in (1, 2, 3) for i in mine]
                for cp in sends:
                    cp.start()
                for ld, st in zip(loads, local):
                    ld.wait()
                    st.start()
                own = [pltpu.make_async_copy(stage[arr], _gathered_at(outs, arr, me, 0, _ARR_ROWS[arr]), own_sems.at[arr])
                       for arr in arrays if arr != 4]
                for cp in own:
                    cp.start()
                meta_all[me] = meta_ref[...]
                for rel in (1, 2, 3):
                    for i in mine:
                        ici_copy(rel, i, chip_of(rel), (x, y, c)).wait_recv()
                        fwd = fwd_copy(rel, i, (x, y, 1 - c))
                        fwd.start()
                        sends.append(fwd)
                for rel in (1, 2, 3):
                    for i in theirs:
                        fwd_copy(rel, i, (x, y, c)).wait_recv()
                for cp in sends:
                    cp.wait_send()
                for cp in own:
                    cp.wait()

        head_buf[...] = jnp.zeros_like(head_buf)
        zeros = pltpu.make_async_copy(head_buf, tp_o.at[pl.ds(0, HEAD_ROWS), :], loc_sems.at[2])
        zeros.start()
        zeros.wait()
        for chip in range(CHIPS):
            head_buf[PAD:HEAD_ROWS, chip * 256:(chip + 1) * 256] = meta_all[chip]
        head = pltpu.make_async_copy(head_buf, h_o.at[pl.ds(0, HEAD_ROWS), :], loc_sems.at[2])
        head.start()
        head.wait()
        for cp in local:
            cp.wait()

    vm = pl.BlockSpec(memory_space=pltpu.VMEM)
    hbm = pl.BlockSpec(memory_space=pl.ANY)
    return pl.pallas_call(
        body,
        name="gather_weights",
        in_specs=[vm] * 4 + [hbm] * 2,
        out_specs=[hbm] * 5,
        out_shape=[
            jax.ShapeDtypeStruct((D_IN, D), BF16),
            jax.ShapeDtypeStruct((CHIPS, 256, Q_LORA), BF16), jax.ShapeDtypeStruct((CHIPS, KV_LORA, 256), BF16),
            jax.ShapeDtypeStruct((N, D), F32), jax.ShapeDtypeStruct((N, D), F32),
        ],
        scratch_shapes=[pltpu.VMEM((_ARR_ROWS[a], _ARR_COLS[a]), BF16) for a in (0, 2, 3)]
        + [pltpu.VMEM((CHIPS, N_META, 256), F32), pltpu.VMEM((HEAD_ROWS, D), F32), pltpu.VMEM((S, D), F32),
           pltpu.VMEM((S, D), F32)]
        + [pltpu.SemaphoreType.DMA((3 * _NP,))] * 4 + [pltpu.SemaphoreType.DMA((3,)), pltpu.SemaphoreType.DMA((4,))],
        compiler_params=_cparams(),
    )(winT_s, wqT_s, wkv_s, meta_s, x2, tgt2)


_SM_ROWS = (len(POOL_WINDOWS) * POOL_GROUP, VEC_ROWS)
_SM_COLS = (POOL_GROUP, D)
_SM_PIECES = ((0, 0, 256, 0), (0, 256, 256, 1), (1, 0, VEC_ROWS, 0))
_NSP = len(_SM_PIECES)


def _reduce_grads(dwin, dwout, dwq, dwkv, dmeta4, dpw, dg, dgf, dgq, dgkv, dps, loss):
    def body(dwin_ref, dwout_ref, dwq_ref, dwkv_ref, dmeta_ref, dpw_ref, dg_ref, dgf_ref, dgq_ref, dgkv_ref, dps_ref,
             loss_ref, gwin_o, gwout_o, gwq_o, gwkv_o, gmeta_o, gpw_o, gg_o, ggf_o, ggq_o, ggkv_o, gps_o, gloss_o,
             ow0, ow1, ow2, ow3, ow4, sb0, sb1, sb2, sb3, sb4, st0, st1, st2, st3, st4, rc0, rc1, rc2, rc3, rc4,
             vec, sm_sb0, sm_sb1, sm_cs0, sm_cs1, sm_rc0, sm_rc1, vec_fin,
             own_sems, d2d_send, d2d_recv, ici_send, ici_recv, fin_send, fin_recv,
             swap_send, swap_recv, smi_send, smi_recv, smf_send, smf_recv):
        x, y, c = lax.axis_index("x"), lax.axis_index("y"), lax.axis_index("c")
        me = 2 * x + y
        grads = (dwin_ref, dwout_ref, dwq_ref, dwkv_ref, dmeta_ref)
        outs = (gwin_o, gwout_o, gwq_o, gwkv_o, gmeta_o)
        own_buf = (ow0, ow1, ow2, ow3, ow4)
        sib_buf = (sb0, sb1, sb2, sb3, sb4)
        stage = (st0, st1, st2, st3, st4)
        recv = (rc0, rc1, rc2, rc3, rc4)
        sm_mine = (dpw_ref, vec)
        sm_sib = (sm_sb0, sm_sb1)
        sm_chip = (sm_cs0, sm_cs1)
        sm_recv = (sm_rc0, sm_rc1)
        sm_out = (gpw_o, vec_fin)
        sibling = (x, y, 1 - c)

        def chip_of(rel):
            fx, fy = _CHIP_RELS[rel]
            return 2 * (x ^ fx) + (y ^ fy)

        def same_core_of(rel):
            fx, fy = _CHIP_RELS[rel]
            return (x ^ fx, y ^ fy, c)

        def slot(bufs, i, idx):
            arr, _, n, _ = _PIECES[i]
            return bufs[arr].at[idx, pl.ds(0, n), :]

        def own_load(rel, i):
            arr, r0, n, _ = _PIECES[i]
            return pltpu.make_async_copy(_gathered_at(grads, arr, chip_of(rel), r0, n), slot(own_buf, i, rel),
                                         own_sems.at[rel * _NP + i])

        def d2d_copy(rel, i):
            arr, r0, n, _ = _PIECES[i]
            k = rel * _NP + i
            return _remote(_gathered_at(grads, arr, chip_of(rel), r0, n), slot(sib_buf, i, rel),
                           d2d_send.at[k], d2d_recv.at[k], sibling)

        def ici_copy(rel, i):
            k = (rel - 1) * _NP + i
            return _remote(slot(stage, i, rel - 1), slot(recv, i, rel - 1), ici_send.at[k], ici_recv.at[k],
                           same_core_of(rel))

        def fin_copy(i):
            arr, r0, n, _ = _PIECES[i]
            place = outs[arr].at[pl.ds(r0, n), :]
            return _remote(place, place, fin_send.at[i], fin_recv.at[i], sibling)

        def sm_ici_copy(rel, j):
            blk, r0, n, _ = _SM_PIECES[j]
            k = (rel - 1) * _NSP + j
            return _remote(sm_chip[blk].at[pl.ds(r0, n), :], sm_recv[blk].at[rel - 1, pl.ds(r0, n), :],
                           smi_send.at[k], smi_recv.at[k], same_core_of(rel))

        def sm_fin_copy(j):
            blk, r0, n, _ = _SM_PIECES[j]
            place = sm_out[blk].at[pl.ds(r0, n), :]
            return _remote(place, place, smf_send.at[j], smf_recv.at[j], sibling)

        vec[...] = jnp.zeros_like(vec)
        vec[0:1, :] = dg_ref[...]
        vec[1:2, :] = dgf_ref[...]
        vec[2:3, V_GQ:V_GQ + Q_LORA] = dgq_ref[...]
        vec[2:3, V_GKV:V_GKV + KV_LORA] = dgkv_ref[...]
        vec[2:3, V_PS:V_PS + D_POOL] = dps_ref[...]
        vec[2:3, V_LOSS:D] = loss_ref[...]
        swaps = [_remote(sm_mine[b], sm_sib[b], swap_send.at[b], swap_recv.at[b], sibling) for b in (0, 1)]
        for cp in swaps:
            cp.start()

        for core in (0, 1):
            @pl.when(c == core)
            def _(core=core):
                mine = [i for i in range(_NP) if _PIECES[i][3] == core]
                theirs = [i for i in range(_NP) if _PIECES[i][3] != core]
                sm_mine_p = [j for j in range(_NSP) if _SM_PIECES[j][3] == core]
                sm_theirs_p = [j for j in range(_NSP) if _SM_PIECES[j][3] != core]
                sends = list(swaps)

                for rel in (1, 2, 3, 0):
                    for i in theirs:
                        cp = d2d_copy(rel, i)
                        cp.start()
                        sends.append(cp)
                    for i in mine:
                        own_load(rel, i).start()

                for b in (0, 1):
                    swaps[b].wait_recv()
                    sm_chip[b][...] = sm_mine[b][...] + sm_sib[b][...]
                for rel in (1, 2, 3):
                    for j in sm_mine_p:
                        cp = sm_ici_copy(rel, j)
                        cp.start()
                        sends.append(cp)

                for rel in (1, 2, 3):
                    for i in mine:
                        arr, r0, n, _ = _PIECES[i]
                        own_load(rel, i).wait()
                        d2d_copy(rel, i).wait_recv()
                        total = slot(own_buf, i, rel)[...] + slot(sib_buf, i, rel)[...]
                        slot(stage, i, rel - 1)[...] = total.astype(stage[arr].dtype)
                        cp = ici_copy(rel, i)
                        cp.start()
                        sends.append(cp)

                for i in mine:
                    arr, r0, n, _ = _PIECES[i]
                    own_load(0, i).wait()
                    d2d_copy(0, i).wait_recv()
                    total = slot(own_buf, i, 0)[...] + slot(sib_buf, i, 0)[...]
                    for rel in (1, 2, 3):
                        ici_copy(rel, i).wait_recv()
                        total = total + slot(recv, i, rel - 1)[...].astype(F32)
                    outs[arr][pl.ds(r0, n), :] = total
                    cp = fin_copy(i)
                    cp.start()
                    sends.append(cp)

                for j in sm_mine_p:
                    blk, r0, n, _ = _SM_PIECES[j]
                    for rel in (1, 2, 3):
                        sm_ici_copy(rel, j).wait_recv()
                    total = jnp.zeros((n, _SM_COLS[blk]), F32)
                    for chip in range(CHIPS):
                        flips = chip ^ me
                        rel = jnp.where(flips == 2, 1, jnp.where(flips == 1, 2, flips))
                        theirs_rows = sm_recv[blk][jnp.maximum(rel - 1, 0), pl.ds(r0, n), :]
                        total = total + jnp.where(rel == 0, sm_chip[blk][pl.ds(r0, n), :], theirs_rows)
                    sm_out[blk][pl.ds(r0, n), :] = total
                    cp = sm_fin_copy(j)
                    cp.start()
                    sends.append(cp)

                for i in theirs:
                    fin_copy(i).wait_recv()
                for j in sm_theirs_p:
                    sm_fin_copy(j).wait_recv()
                for cp in sends:
                    cp.wait_send()

        gg_o[...] = vec_fin[0:1, :]
        ggf_o[...] = vec_fin[1:2, :]
        ggq_o[...] = vec_fin[2:3, V_GQ:V_GQ + Q_LORA]
        ggkv_o[...] = vec_fin[2:3, V_GKV:V_GKV + KV_LORA]
        gps_o[...] = vec_fin[2:3, V_PS:V_PS + D_POOL]
        gloss_o[...] = vec_fin[2:3, V_LOSS:D]

    vm = pl.BlockSpec(memory_space=pltpu.VMEM)
    piece_buf = lambda lead, dtype: [pltpu.VMEM((lead, _PIECE_MAX[a], _ARR_COLS[a]), F32 if a == 4 else dtype)
                                     for a in range(5)]
    sm_buf = lambda *lead: [pltpu.VMEM(lead + (_SM_ROWS[b], _SM_COLS[b]), F32) for b in (0, 1)]
    dma = lambda n: [pltpu.SemaphoreType.DMA((n,))] * 2
    return pl.pallas_call(
        body,
        name="reduce_grads",
        in_specs=[pl.BlockSpec(memory_space=pl.ANY)] * 4 + [vm] * 8,
        out_specs=[vm] * 12,
        out_shape=[jax.ShapeDtypeStruct((_ARR_ROWS[a], _ARR_COLS[a]), F32) for a in range(5)]
        + [jax.ShapeDtypeStruct((_SM_ROWS[0], _SM_COLS[0]), F32), jax.ShapeDtypeStruct((1, D), F32),
           jax.ShapeDtypeStruct((1, D), F32), jax.ShapeDtypeStruct((1, Q_LORA), F32),
           jax.ShapeDtypeStruct((1, KV_LORA), F32), jax.ShapeDtypeStruct((1, D_POOL), F32),
           jax.ShapeDtypeStruct((1, 128), F32)],
        scratch_shapes=piece_buf(CHIPS, F32) + piece_buf(CHIPS, F32) + piece_buf(3, BF16) + piece_buf(3, BF16)
        + [pltpu.VMEM((VEC_ROWS, D), F32)] + sm_buf() + sm_buf() + sm_buf(3) + [pltpu.VMEM((VEC_ROWS, D), F32)]
        + [pltpu.SemaphoreType.DMA((CHIPS * _NP,))]
        + dma(CHIPS * _NP) + dma(3 * _NP) + dma(_NP) + dma(2) + dma(3 * _NSP) + dma(_NSP),
        compiler_params=_cparams(),
    )(dwin, dwout, dwq, dwkv, dmeta4, dpw, dg, dgf, dgq, dgkv, dps, loss)


def _adamw_math(w, g, m, v):
    m = B1 * m + (1.0 - B1) * g
    v = B2 * v + (1.0 - B2) * (g * g)
    m_hat = m / C1
    v_hat = v / C2
    delta = -LR * (m_hat / (jnp.sqrt(v_hat) + ADAM_EPS) + WD * w)
    return delta, m, v


def _adamw_rows(name, w, g, m, v, block_rows):
    rows, cols = w.shape

    def body(w_ref, g_ref, m_ref, v_ref, go_ref, d_ref, nm_ref, nv_ref):
        g = g_ref[...]
        go_ref[...] = g
        d_ref[...], nm_ref[...], nv_ref[...] = _adamw_math(w_ref[...], g, m_ref[...], v_ref[...])

    spec = pl.BlockSpec((block_rows, cols), lambda i: (i, 0))
    return pl.pallas_call(
        body,
        name=name,
        grid=(rows // block_rows,),
        in_specs=[spec] * 4,
        out_specs=[spec] * 4,
        out_shape=[jax.ShapeDtypeStruct(w.shape, F32)] * 4,
        compiler_params=_cparams(dimension_semantics=("arbitrary",)),
    )(w, g, m, v)


def _adamw_small(groups):
    n = len(groups)

    def body(*refs):
        ins, outs = refs[:4 * n], refs[4 * n:]
        for t in range(n):
            w_ref, g_ref, m_ref, v_ref = ins[4 * t:4 * t + 4]
            g = g_ref[0:w_ref.shape[0], :]
            outs[4 * t][...] = g
            outs[4 * t + 1][...], outs[4 * t + 2][...], outs[4 * t + 3][...] = _adamw_math(
                w_ref[...], g, m_ref[...], v_ref[...])

    vm = pl.BlockSpec(memory_space=pltpu.VMEM)
    flat = [a for grp in groups for a in grp]
    outs = pl.pallas_call(
        body,
        name="adamw_small",
        in_specs=[vm] * (4 * n),
        out_specs=[vm] * (4 * n),
        out_shape=[jax.ShapeDtypeStruct(grp[0].shape, F32) for grp in groups for _ in range(4)],
        compiler_params=_cparams(),
    )(*flat)
    return [tuple(outs[4 * t:4 * t + 4]) for t in range(n)]


def _rope_tables():
    half = QK_ROPE // 2
    f32 = np.float32
    inv_freq = (f32(1.0) / (f32(ROPE_THETA) ** (np.arange(half, dtype=f32) / f32(half)))).astype(f32)
    pos = np.arange(N, dtype=f32) - f32(PAD)
    ang = (pos[:, None] * inv_freq[None, :]).astype(f32)
    cos, sin = np.cos(ang).astype(f32), np.sin(ang).astype(f32)
    zero = np.zeros((N, 128 - QK_ROPE), f32)
    return jnp.asarray(np.concatenate([cos, cos, zero], axis=1)), jnp.asarray(np.concatenate([-sin, sin, zero], axis=1))


def kernel(x, meta_tokens, norm_g, w_in, q_norm_g, w_q_b, kv_norm_g, w_kv_b, pool_w, pool_scale, w_out, final_norm_g, loss_target, m_meta_tokens, m_norm_g, m_w_in, m_q_norm_g, m_w_q_b, m_kv_norm_g, m_w_kv_b, m_pool_w, m_pool_scale, m_w_out, m_final_norm_g, v_meta_tokens, v_norm_g, v_w_in, v_q_norm_g, v_w_q_b, v_kv_norm_g, v_w_kv_b, v_pool_w, v_pool_scale, v_w_out, v_final_norm_g):
    tr = lambda a: a[0].T
    win, wq, wkv, h, tgt = _gather_weights(tr(w_in), tr(w_q_b), w_kv_b[0], meta_tokens, x[0], loss_target[0])
    cosf, sinf = _rope_tables()
    gf = final_norm_g.reshape(1, D)

    part = _local_step(h, tgt, norm_g, win, q_norm_g, wq, kv_norm_g, wkv, pool_w[0], pool_scale, w_out[0], gf, cosf, sinf)

    pw2 = lambda a: a.reshape(len(POOL_WINDOWS) * POOL_GROUP, POOL_GROUP)
    gwinT, gwout, gwqT, gwkv, gmeta, gpw, gg, ggf, ggq, ggkv, gps, gloss = _reduce_grads(
        part["dwin"], part["dwout"], part["dwq"], part["dwkv"], part["dmeta"], pw2(part["dpw"]), part["dg"],
        part["dgf"], part["dgq"], part["dgkv"], part["dps"], part["loss"])

    r_in = _adamw_rows("adamw_w_in", tr(w_in), gwinT, tr(m_w_in), tr(v_w_in), 248)
    r_out = _adamw_rows("adamw_w_out", w_out[0], gwout, m_w_out[0], v_w_out[0], 128)
    fn2 = lambda a: a.reshape(1, D)
    r_meta, r_norm, r_gq, r_wq, r_gkv, r_wkv, r_pw, r_ps, r_fn = _adamw_small([
        (meta_tokens, gmeta, m_meta_tokens, v_meta_tokens),
        (norm_g, gg, m_norm_g, v_norm_g),
        (q_norm_g, ggq, m_q_norm_g, v_q_norm_g),
        (tr(w_q_b), gwqT, tr(m_w_q_b), tr(v_w_q_b)),
        (kv_norm_g, ggkv, m_kv_norm_g, v_kv_norm_g),
        (w_kv_b[0], gwkv, m_w_kv_b[0], v_w_kv_b[0]),
        (pw2(pool_w), gpw, pw2(m_pool_w), pw2(v_pool_w)),
        (pool_scale, gps, m_pool_scale, v_pool_scale),
        (fn2(final_norm_g), ggf, fn2(m_final_norm_g), fn2(v_final_norm_g)),
    ])
    untr = lambda a: a.T[None]
    pw4 = lambda a: a.reshape(1, len(POOL_WINDOWS), POOL_GROUP, POOL_GROUP)
    per_kind = [[
        r_meta[kind], r_norm[kind], untr(r_in[kind]), r_gq[kind], untr(r_wq[kind]), r_gkv[kind], r_wkv[kind][None],
        pw4(r_pw[kind]), r_ps[kind], r_out[kind][None], r_fn[kind].reshape(D),
    ] for kind in range(4)]
    return (gloss[0, 0], part["gx"][None], *per_kind[0], *per_kind[1], *per_kind[2], *per_kind[3])
```

```python
import jax
import jax.numpy as jnp
import numpy as np
from jax import lax
from jax.experimental import pallas as pl
from jax.experimental.pallas import tpu as pltpu

F32 = jnp.float32
BF16 = jnp.bfloat16

D = 1024
S = 2048
N_META = 16
PAD = 112
HEAD_ROWS = PAD + N_META
N = HEAD_ROWS + S
D_POOL = 512
POOL_WINDOWS = (2, 4, 8, 16)
POOL_GROUP = 128
HALO = 16
HEADS = 4
QK_NOPE = 128
QK_ROPE = 64
QK = QK_NOPE + QK_ROPE
V_HEAD = 128
Q_LORA = 256
KV_LORA = 128
D_IN = 1984
EPS = 1e-6
ROPE_THETA = 10000.0
SCALE = QK ** -0.5
CHIPS = 4

ROWS_FWD = 544
ROWS_MID = 544
ROWS_BWD = 544
TK = 128
TQ = 256
NQ = S // TQ
HEADS_PER_STEP_BWD = 2

O_PI, O_PG, O_CQ, O_CKV, O_KR, O_AG = 0, 512, 1024, 1280, 1408, 1472
O_KR_END = O_KR + 128
SHARD_IN = D_IN // CHIPS
SHARD_OUT = D // CHIPS

LR, B1, B2, ADAM_EPS, WD, STEP = 0.001, 0.9, 0.999, 1e-08, 0.01, 10
C1 = 1.0 - B1**STEP
C2 = 1.0 - B2**STEP

VMEM_LIMIT = 60 * 1024 * 1024
MESH = pl.DeviceIdType.MESH
NEG = -1e30

VEC_ROWS = 8
V_GQ, V_GKV, V_PS, V_LOSS = 0, 256, 384, 896


def _cparams(**kw):
    return pltpu.CompilerParams(vmem_limit_bytes=VMEM_LIMIT, **kw)


def _nt(a, b):
    return lax.dot_general(a, b, (((1,), (1,)), ((), ())), preferred_element_type=F32)


def _tn(a, b):
    return lax.dot_general(a, b, (((0,), (0,)), ((), ())), preferred_element_type=F32)


def _nn(a, b):
    return jnp.dot(a, b, preferred_element_type=F32)


def _swap64(t):
    return pltpu.roll(t, 32, 1) + pltpu.roll(t, 96, 1)


def _sigmoid(x):
    return 1.0 / (1.0 + jnp.exp(-x))


def _low_lanes():
    return (lax.broadcasted_iota(jnp.int32, (1, 128), 1) < QK_ROPE).astype(F32)


def _rows(w, rows):
    return pl.BlockSpec((rows, w), lambda i: (i, 0))


def _const(*shape):
    return pl.BlockSpec(shape, lambda *_: (0,) * len(shape), pipeline_mode=pl.Buffered(1))


STAT_GROUPS = HEADS // HEADS_PER_STEP_BWD


def _stat_slot(head):
    return head // HEADS_PER_STEP_BWD, head % HEADS_PER_STEP_BWD


def _attn_tiles():
    return [(0, TK, TK)] + [(TK + TQ * t, TQ, TK + TQ * (t + 1)) for t in range(NQ)]


def _masked_scores(q, k, rows, klen):
    s = _nt(q, k)
    col = lax.broadcasted_iota(jnp.int32, (1, TK), 1)
    head_bias = jnp.where(col >= PAD, 0.0, NEG)
    if klen == TK:
        return s + head_bias
    r = lax.broadcasted_iota(jnp.int32, (rows, 1), 0) >> 6
    c = lax.broadcasted_iota(jnp.int32, (1, rows), 1) >> 6
    diag_bias = jnp.where(c <= r, 0.0, NEG)
    parts = [s[:, 0:TK] + head_bias]
    if klen - rows > TK:
        parts.append(s[:, TK:klen - rows])
    parts.append(s[:, klen - rows:klen] + diag_bias)
    return jnp.concatenate(parts, axis=1)


def _fwd_in(h, norm_g, win, gq, wq, gkv, wkv, cosf, sinf):
    tr = ROWS_FWD

    def body(h_ref, g_ref, win_ref, gq_ref, wq_ref, gkv_ref, wkv_ref, cos_ref, sin_ref,
             pi_ref, pg_ref, cq_ref, ckv_ref, ag_ref, q_ref, k_ref, v_ref):
        h = h_ref[...]
        r = lax.rsqrt(jnp.mean(h * h, axis=-1, keepdims=True) + EPS)
        hn = ((h * r) * g_ref[...]).astype(BF16)
        u = _nt(hn, win_ref[0:O_KR_END, :])
        pi_ref[...] = u[:, O_PI:O_PG]
        pg_ref[...] = u[:, O_PG:O_CQ]
        cq = u[:, O_CQ:O_CKV]
        ckv = u[:, O_CKV:O_KR]
        cq_ref[...] = cq
        ckv_ref[...] = ckv
        ag_ref[...] = _nt(hn, win_ref[O_AG:D_IN, :])
        cosv = cos_ref[...]
        sinv = sin_ref[...]
        kr = u[:, O_KR:O_KR_END] * _low_lanes()
        kr = (kr * cosv + _swap64(kr) * sinv).astype(BF16)
        rq = lax.rsqrt(jnp.mean(cq * cq, axis=-1, keepdims=True) + EPS)
        cqn = ((cq * rq) * gq_ref[...]).astype(BF16)
        rkv = lax.rsqrt(jnp.mean(ckv * ckv, axis=-1, keepdims=True) + EPS)
        ckvn = ((ckv * rkv) * gkv_ref[...]).astype(BF16)
        for hd in range(HEADS):
            qh = _nt(cqn, wq_ref[hd]) * SCALE
            z = qh[:, QK_NOPE:]
            q_ref[hd, :, 0:QK_NOPE] = qh[:, 0:QK_NOPE].astype(BF16)
            q_ref[hd, :, QK_NOPE:] = (z * cosv + _swap64(z) * sinv).astype(BF16)
            kvh = _nn(ckvn, wkv_ref[hd])
            k_ref[hd, :, 0:QK_NOPE] = kvh[:, 0:QK_NOPE].astype(BF16)
            k_ref[hd, :, QK_NOPE:] = kr
            v_ref[hd] = kvh[:, QK_NOPE:].astype(BF16)

    head = lambda w: pl.BlockSpec((HEADS, tr, w), lambda i: (0, i, 0))
    return pl.pallas_call(
        body,
        name="fwd_in",
        grid=(N // tr,),
        in_specs=[
            _rows(D, tr), _const(1, D), _const(D_IN, D), _const(1, Q_LORA), _const(HEADS, 256, Q_LORA),
            _const(1, KV_LORA), _const(HEADS, KV_LORA, 256), _rows(128, tr), _rows(128, tr),
        ],
        out_specs=[_rows(D_POOL, tr), _rows(D_POOL, tr), _rows(Q_LORA, tr), _rows(KV_LORA, tr), _rows(D_POOL, tr),
                   head(256), head(256), head(V_HEAD)],
        out_shape=[
            jax.ShapeDtypeStruct((N, D_POOL), F32), jax.ShapeDtypeStruct((N, D_POOL), F32),
            jax.ShapeDtypeStruct((N, Q_LORA), F32), jax.ShapeDtypeStruct((N, KV_LORA), F32),
            jax.ShapeDtypeStruct((N, D_POOL), F32),
            jax.ShapeDtypeStruct((HEADS, N, 256), BF16), jax.ShapeDtypeStruct((HEADS, N, 256), BF16),
            jax.ShapeDtypeStruct((HEADS, N, V_HEAD), BF16),
        ],
        compiler_params=_cparams(dimension_semantics=("arbitrary",)),
    )(h, norm_g, win, gq, wq, gkv, wkv, cosf, sinf)


def _attn_fwd(q, k, v, wout_s):
    tiles = _attn_tiles()
    n_t = len(tiles)
    half = SHARD_OUT // 2
    fwd_step = n_t - 2

    def body(q_hbm, k_hbm, v_hbm, wout_ref, o_hbm, lse_ref, wout_o, q_buf, k_buf, v_buf, o_buf, s_wout, in_sems, out_sems,
             ici_send, ici_recv, fwd_send, fwd_recv, own_sem):
        step = pl.program_id(0)
        x, y, c = lax.axis_index("x"), lax.axis_index("y"), lax.axis_index("c")
        me = 2 * x + y

        def chip_of(rel):
            fx, fy = _CHIP_RELS[rel]
            return 2 * (x ^ fx) + (y ^ fy)

        def place(chip, core):
            return wout_o.at[pl.ds(pl.multiple_of(SHARD_OUT * chip + half * core, half), half), :]

        def ici_copy(rel, src_chip, to):
            return _remote(s_wout.at[pl.ds(pl.multiple_of(half * c, half), half), :], place(src_chip, c),
                           ici_send.at[rel - 1], ici_recv.at[rel - 1], to)

        def fwd_copy(rel, core, to):
            spot = place(chip_of(rel), core)
            return _remote(spot, spot, fwd_send.at[rel - 1], fwd_recv.at[rel - 1], to)

        own = pltpu.make_async_copy(s_wout, wout_o.at[pl.ds(pl.multiple_of(SHARD_OUT * me, SHARD_OUT), SHARD_OUT), :], own_sem)

        @pl.when(step == 0)
        def _():
            s_wout[...] = wout_ref[...].astype(BF16)
            own.start()
            for rel in (1, 2, 3):
                fx, fy = _CHIP_RELS[rel]
                ici_copy(rel, me, (x ^ fx, y ^ fy, c)).start()

        @pl.when(step == fwd_step)
        def _():
            for rel in (1, 2, 3):
                ici_copy(rel, chip_of(rel), (x, y, c)).wait_recv()
                fwd_copy(rel, c, (x, y, 1 - c)).start()

        def finish_wout():
            for rel in (1, 2, 3):
                fwd_copy(rel, 1 - c, (x, y, c)).wait_recv()
            for rel in (1, 2, 3):
                ici_copy(rel, me, (x, y, c)).wait_send()
                fwd_copy(rel, c, (x, y, c)).wait_send()
            own.wait()

        def loads(idx):
            q0, rows, _ = tiles[idx]
            rs = pl.ds(q0, rows)
            return [pltpu.make_async_copy(src.at[:, rs, :], dst.at[:, rs, :], in_sems.at[a, idx % 2])
                    for a, (src, dst) in enumerate(((q_hbm, q_buf), (k_hbm, k_buf), (v_hbm, v_buf)))]

        def store(idx):
            q0, rows, _ = tiles[idx]
            return pltpu.make_async_copy(o_buf.at[idx % 2, pl.ds(0, rows), :], o_hbm.at[pl.ds(q0, rows), :],
                                         out_sems.at[idx % 2])

        @pl.when(step == 0)
        def _():
            lse_ref[...] = jnp.zeros_like(lse_ref)
            for cp in loads(0):
                cp.start()

        for idx, (q0, rows, klen) in enumerate(tiles):
            @pl.when(step == idx)
            def _(idx=idx, q0=q0, rows=rows, klen=klen):
                for cp in loads(idx):
                    cp.wait()
                if idx + 1 < n_t:
                    for cp in loads(idx + 1):
                        cp.start()
                if idx >= 2:
                    store(idx - 2).wait()
                for hd in range(HEADS):
                    s = _masked_scores(q_buf[hd, q0:q0 + rows, :], k_buf[hd, 0:klen, :], rows, klen)
                    m = jnp.max(s, axis=-1, keepdims=True)
                    p = jnp.exp(s - m)
                    l = jnp.sum(p, axis=-1, keepdims=True)
                    o_buf[idx % 2, 0:rows, hd * V_HEAD:(hd + 1) * V_HEAD] = _nn(p.astype(BF16), v_buf[hd, 0:klen, :]) / l
                    grp, lane = _stat_slot(hd)
                    lse_ref[grp, q0:q0 + rows, lane:lane + 1] = m + jnp.log(l)
                store(idx).start()
                if idx == n_t - 1:
                    store(idx - 1).wait()
                    store(idx).wait()
                    finish_wout()

    hbm = pl.BlockSpec(memory_space=pl.ANY)
    return pl.pallas_call(
        body,
        name="attn_fwd",
        grid=(n_t,),
        in_specs=[hbm, hbm, hbm, _const(SHARD_OUT, D)],
        out_specs=[hbm, _const(STAT_GROUPS, N, 128), hbm],
        out_shape=[jax.ShapeDtypeStruct((N, HEADS * V_HEAD), F32), jax.ShapeDtypeStruct((STAT_GROUPS, N, 128), F32),
                   jax.ShapeDtypeStruct((D, D), BF16)],
        scratch_shapes=[pltpu.VMEM((HEADS, N, 256), BF16), pltpu.VMEM((HEADS, N, 256), BF16),
                        pltpu.VMEM((HEADS, N, V_HEAD), BF16), pltpu.VMEM((2, TQ, HEADS * V_HEAD), F32),
                        pltpu.VMEM((SHARD_OUT, D), BF16),
                        pltpu.SemaphoreType.DMA((3, 2)), pltpu.SemaphoreType.DMA((2,))]
        + [pltpu.SemaphoreType.DMA((3,))] * 4 + [pltpu.SemaphoreType.DMA],
        compiler_params=_cparams(dimension_semantics=("arbitrary",)),
    )(q, k, v, wout_s)


def _inv_count(row0, rows, w):
    row = row0 + lax.broadcasted_iota(jnp.int32, (rows, 1), 0)
    return 1.0 / jnp.clip(row - (PAD - 1), 1, w).astype(F32)


def _mid(h, tgt, pool_in, pool_gate, attn_gate, attn, pool_w, pool_scale, wout, gf):
    tr = ROWS_MID
    per = tr // HALO
    ng = len(POOL_WINDOWS)

    def body(h_ref, t_ref, pin_ref, halo_ref, pg_ref, ag_ref, at_ref, pw_ref, ps_ref, wout_ref, gf_ref,
             dh2_ref, do_ref, delta_ref, dag_ref, dpg_ref, dpl_ref, dwout_ref, dpw_ref, dps_ref, dgf_ref, loss_ref):
        i = pl.program_id(0)

        @pl.when(i == 0)
        def _():
            dwout_ref[...] = jnp.zeros_like(dwout_ref)
            dpw_ref[...] = jnp.zeros_like(dpw_ref)
            dps_ref[...] = jnp.zeros_like(dps_ref)
            dgf_ref[...] = jnp.zeros_like(dgf_ref)
            loss_ref[...] = jnp.zeros_like(loss_ref)

        row0 = i * tr
        real = (row0 + lax.broadcasted_iota(jnp.int32, (tr, 1), 0)) >= HEAD_ROWS
        h = h_ref[...]

        halo = jnp.where(i > 0, halo_ref[...], 0.0)
        ext = jnp.concatenate([halo, pin_ref[...]], axis=0)
        pooled = []
        for g, w in enumerate(POOL_WINDOWS):
            e = ext[:, g * POOL_GROUP:(g + 1) * POOL_GROUP]
            acc = e
            shift = 1
            while shift < w:
                acc = acc + pltpu.roll(acc, shift, 0)
                shift *= 2
            pooled.append((acc[HALO:] * _inv_count(row0, tr, w) - e[HALO:]).astype(BF16))
        pw = [pw_ref[g].astype(BF16) for g in range(ng)]
        mixed = jnp.concatenate([_nn(pooled[g], pw[g]) for g in range(ng)], axis=1)
        ps = ps_ref[...]
        mixed_s = mixed * ps
        pg = pg_ref[...]
        sig_p = _sigmoid(pg)
        silu_p = pg * sig_p
        pool_out = (silu_p * mixed_s).astype(BF16)
        ag = ag_ref[...]
        sig_a = _sigmoid(ag)
        silu_a = ag * sig_a
        at = at_ref[...]
        attn_out = (silu_a * at).astype(BF16)
        mix = _nn(pool_out, wout_ref[0:D_POOL, :]) + _nn(attn_out, wout_ref[D_POOL:D, :])
        h2 = h + mix

        r2 = lax.rsqrt(jnp.mean(h2 * h2, axis=-1, keepdims=True) + EPS)
        n2 = h2 * r2
        gfv = gf_ref[...]
        err = jnp.where(real, n2 * gfv - t_ref[...], 0.0)
        loss_ref[...] += jnp.sum(jnp.sum(err * err, axis=-1, keepdims=True), axis=0, keepdims=True) * (0.5 / D)
        dy = err * (1.0 / D)
        dgf_ref[...] += jnp.sum(dy * n2, axis=0, keepdims=True)
        dn = dy * gfv
        dh2 = r2 * (dn - n2 * jnp.mean(dn * n2, axis=-1, keepdims=True))
        dh2_ref[...] = dh2
        dh2b = dh2.astype(BF16)

        dwout_ref[0:D_POOL, :] += _tn(pool_out, dh2b)
        dwout_ref[D_POOL:D, :] += _tn(attn_out, dh2b)
        dcat = _nt(dh2b, wout_ref[...])
        dpo = dcat[:, 0:D_POOL]
        dao = dcat[:, D_POOL:D]
        do = dao * silu_a
        prod = do * at
        delta_ref[...] = jnp.zeros_like(delta_ref)
        for hd in range(HEADS):
            grp, lane = _stat_slot(hd)
            cols = slice(hd * V_HEAD, (hd + 1) * V_HEAD)
            do_ref[grp, :, lane * V_HEAD:(lane + 1) * V_HEAD] = do[:, cols].astype(BF16)
            delta_ref[grp, :, lane:lane + 1] = jnp.sum(prod[:, cols], axis=-1, keepdims=True)
        dag_ref[...] = (dao * at * (sig_a * (1.0 + ag * (1.0 - sig_a)))).astype(BF16)
        dmixed_s = dpo * silu_p
        dpg_ref[...] = (dpo * mixed_s * (sig_p * (1.0 + pg * (1.0 - sig_p)))).astype(BF16)
        dps_ref[...] += jnp.sum(dmixed_s * mixed, axis=0, keepdims=True)
        dmixed = (dmixed_s * ps).astype(BF16)
        dpl = []
        for g in range(ng):
            dm = dmixed[:, g * POOL_GROUP:(g + 1) * POOL_GROUP]
            dpl.append(_nt(dm, pw[g]))
            dpw_ref[g] += _tn(pooled[g], dm)
        dpl_ref[...] = jnp.concatenate(dpl, axis=1)

    halo_spec = pl.BlockSpec((HALO, D_POOL), lambda i: (jnp.maximum(i * per - 1, 0), 0))
    return pl.pallas_call(
        body,
        name="mid",
        grid=(N // tr,),
        in_specs=[
            _rows(D, tr), _rows(D, tr), _rows(D_POOL, tr), halo_spec, _rows(D_POOL, tr), _rows(D_POOL, tr),
            _rows(D_POOL, tr), _const(ng, POOL_GROUP, POOL_GROUP), _const(1, D_POOL), _const(D, D), _const(1, D),
        ],
        out_specs=[
            _rows(D, tr), pl.BlockSpec((STAT_GROUPS, tr, HEADS_PER_STEP_BWD * V_HEAD), lambda i: (0, i, 0)),
            pl.BlockSpec((STAT_GROUPS, tr, 128), lambda i: (0, i, 0)),
            _rows(D_POOL, tr), _rows(D_POOL, tr), _rows(D_POOL, tr),
            _const(D, D), _const(ng, POOL_GROUP, POOL_GROUP), _const(1, D_POOL), _const(1, D), _const(1, 128),
        ],
        out_shape=[
            jax.ShapeDtypeStruct((N, D), F32), jax.ShapeDtypeStruct((STAT_GROUPS, N, HEADS_PER_STEP_BWD * V_HEAD), BF16),
            jax.ShapeDtypeStruct((STAT_GROUPS, N, 128), F32),
            jax.ShapeDtypeStruct((N, D_POOL), BF16), jax.ShapeDtypeStruct((N, D_POOL), BF16),
            jax.ShapeDtypeStruct((N, D_POOL), F32), jax.ShapeDtypeStruct((D, D), F32),
            jax.ShapeDtypeStruct((ng, POOL_GROUP, POOL_GROUP), F32),
            jax.ShapeDtypeStruct((1, D_POOL), F32), jax.ShapeDtypeStruct((1, D), F32), jax.ShapeDtypeStruct((1, 128), F32),
        ],
        compiler_params=_cparams(dimension_semantics=("arbitrary",)),
    )(h, tgt, pool_in, pool_in, pool_gate, attn_gate, attn, pool_w, pool_scale, wout, gf)


def _unrope(dy, cosv, sinv):
    return dy * cosv + _swap64(dy * sinv) * _low_lanes()


def _attn_bwd(q, k, v, do, lse, delta, cosf, sinf, dwout):
    tiles = _attn_tiles()
    hp = HEADS_PER_STEP_BWD
    n_g = HEADS // hp
    n_t = len(tiles)
    half = SHARD_OUT // 2
    send_at, sum_at = (0, 2), (n_g - 1, n_t // 2)

    def body(q_hbm, k_hbm, v_hbm, do_hbm, lse_ref, delta_ref, cos_ref, sin_ref, dwout_hbm, dq_hbm, dkv_ref, dkr_ref,
             gwout_ref, q_buf, k_buf, v_buf, do_buf, dq_buf, dk_acc, dv_acc, own_w, sib_w, stage_w, recv_w, gw_buf,
             in_sems, out_sems, ow_sems, d2d_send, d2d_recv, ici_send, ici_recv, fin_send, fin_recv):
        grp = pl.program_id(0)
        step = pl.program_id(1)
        heads = pl.ds(grp * hp, hp)
        x, y, c = lax.axis_index("x"), lax.axis_index("y"), lax.axis_index("c")
        sibling = (x, y, 1 - c)

        def chip_of(rel):
            fx, fy = _CHIP_RELS[rel]
            return 2 * (x ^ fx) + (y ^ fy)

        def piece(chip, core):
            return dwout_hbm.at[pl.ds(pl.multiple_of(SHARD_OUT * chip + half * core, half), half), :]

        def own_load(rel):
            return pltpu.make_async_copy(piece(chip_of(rel), c), own_w.at[rel], ow_sems.at[rel])

        def d2d_copy(rel):
            return _remote(piece(chip_of(rel), 1 - c), sib_w.at[rel], d2d_send.at[rel], d2d_recv.at[rel], sibling)

        def ici_copy(rel):
            fx, fy = _CHIP_RELS[rel]
            return _remote(stage_w.at[rel - 1], recv_w.at[rel - 1], ici_send.at[rel - 1], ici_recv.at[rel - 1],
                           (x ^ fx, y ^ fy, c))

        def fin_copy(core):
            spot = gw_buf.at[pl.ds(pl.multiple_of(half * core, half), half), :]
            return _remote(spot, spot, fin_send.at[0], fin_recv.at[0], sibling)

        @pl.when((grp == 0) & (step == 0))
        def _():
            for rel in (1, 2, 3, 0):
                d2d_copy(rel).start()
                own_load(rel).start()

        @pl.when((grp == send_at[0]) & (step == send_at[1]))
        def _():
            for rel in (1, 2, 3):
                own_load(rel).wait()
                d2d_copy(rel).wait_recv()
                stage_w[rel - 1] = (own_w[rel] + sib_w[rel]).astype(BF16)
                ici_copy(rel).start()

        @pl.when((grp == sum_at[0]) & (step == sum_at[1]))
        def _():
            own_load(0).wait()
            d2d_copy(0).wait_recv()
            total = own_w[0] + sib_w[0]
            for rel in (1, 2, 3):
                ici_copy(rel).wait_recv()
                total = total + recv_w[rel - 1].astype(F32)
            gw_buf[pl.ds(pl.multiple_of(half * c, half), half), :] = total
            fin_copy(c).start()

        def finish_dwout():
            fin_copy(1 - c).wait_recv()
            for rel in (0, 1, 2, 3):
                d2d_copy(rel).wait_send()
            for rel in (1, 2, 3):
                ici_copy(rel).wait_send()
            fin_copy(c).wait_send()
            gwout_ref[...] = gw_buf[...]

        def loads(g, idx):
            q0, rows, _ = tiles[idx]
            rs = pl.ds(q0, rows)
            par = (g * n_t + idx) % 2
            hs = pl.ds(g * hp, hp)
            pairs = ((q_hbm.at[hs, rs, :], q_buf.at[:, rs, :]), (k_hbm.at[hs, rs, :], k_buf.at[:, rs, :]),
                     (v_hbm.at[hs, rs, :], v_buf.at[:, rs, :]), (do_hbm.at[g, rs, :], do_buf.at[rs, :]))
            return [pltpu.make_async_copy(src, dst, in_sems.at[a, par]) for a, (src, dst) in enumerate(pairs)]

        def store(idx):
            q0, rows, _ = tiles[idx]
            return pltpu.make_async_copy(dq_buf.at[idx % 2, :, pl.ds(0, rows), :], dq_hbm.at[heads, pl.ds(q0, rows), :],
                                         out_sems.at[idx % 2])

        @pl.when(step == 0)
        def _():
            dk_acc[...] = jnp.zeros_like(dk_acc)
            dv_acc[...] = jnp.zeros_like(dv_acc)

        @pl.when((step == 0) & (grp == 0))
        def _():
            dkr_ref[...] = jnp.zeros_like(dkr_ref)
            for cp in loads(grp, 0):
                cp.start()

        for idx, (q0, rows, klen) in enumerate(tiles):
            @pl.when(step == idx)
            def _(idx=idx, q0=q0, rows=rows, klen=klen):
                for cp in loads(grp, idx):
                    cp.wait()
                if idx + 1 < n_t:
                    for cp in loads(grp, idx + 1):
                        cp.start()
                if idx >= 2:
                    store(idx - 2).wait()
                qs = pl.ds(q0, rows)
                for hd in range(hp):
                    qv = q_buf[hd, qs, :]
                    kv = k_buf[hd, 0:klen, :]
                    p = jnp.exp(_masked_scores(qv, kv, rows, klen) - lse_ref[0, qs, hd:hd + 1])
                    dob = do_buf[qs, hd * V_HEAD:(hd + 1) * V_HEAD]
                    ds = (p * (_nt(dob, v_buf[hd, 0:klen, :]) - delta_ref[0, qs, hd:hd + 1])).astype(BF16)
                    dq = _nn(ds, kv) * SCALE
                    dq_buf[idx % 2, hd, 0:rows, 0:QK_NOPE] = dq[:, 0:QK_NOPE].astype(BF16)
                    dq_buf[idx % 2, hd, 0:rows, QK_NOPE:] = _unrope(dq[:, QK_NOPE:], cos_ref[qs, :], sin_ref[qs, :]).astype(BF16)
                    dk_acc[hd, 0:klen, :] += _tn(ds, qv)
                    dv_acc[hd, 0:klen, :] += _tn(p.astype(BF16), dob)
                store(idx).start()

        @pl.when(step == n_t - 1)
        def _():
            @pl.when(grp + 1 < n_g)
            def _():
                for cp in loads(grp + 1, 0):
                    cp.start()

            for hd in range(hp):
                dkv_ref[hd, :, 0:QK_NOPE] = dk_acc[hd, :, 0:QK_NOPE].astype(BF16)
                dkv_ref[hd, :, QK_NOPE:] = dv_acc[hd].astype(BF16)
                dkr_ref[...] += dk_acc[hd, :, QK_NOPE:]
            store(n_t - 2).wait()
            store(n_t - 1).wait()

            @pl.when(grp == n_g - 1)
            def _():
                finish_dwout()

    hbm = pl.BlockSpec(memory_space=pl.ANY)
    stat = pl.BlockSpec((1, N, 128), lambda g, t: (g, 0, 0), pipeline_mode=pl.Buffered(1))
    piece_f32 = lambda lead: pltpu.VMEM((lead, half, D), F32)
    piece_bf16 = lambda lead: pltpu.VMEM((lead, half, D), BF16)
    return pl.pallas_call(
        body,
        name="attn_bwd",
        grid=(n_g, n_t),
        in_specs=[hbm, hbm, hbm, hbm, stat, stat, _const(N, 128), _const(N, 128), hbm],
        out_specs=[hbm, pl.BlockSpec((hp, N, 256), lambda g, t: (g, 0, 0), pipeline_mode=pl.Buffered(1)), _const(N, 128),
                   _const(SHARD_OUT, D)],
        out_shape=[
            jax.ShapeDtypeStruct((HEADS, N, 256), BF16), jax.ShapeDtypeStruct((HEADS, N, 256), BF16),
            jax.ShapeDtypeStruct((N, 128), F32), jax.ShapeDtypeStruct((SHARD_OUT, D), F32),
        ],
        scratch_shapes=[pltpu.VMEM((hp, N, 256), BF16), pltpu.VMEM((hp, N, 256), BF16), pltpu.VMEM((hp, N, V_HEAD), BF16),
                        pltpu.VMEM((N, hp * V_HEAD), BF16), pltpu.VMEM((2, hp, TQ, 256), BF16),
                        pltpu.VMEM((hp, N, 256), F32), pltpu.VMEM((hp, N, V_HEAD), F32),
                        piece_f32(CHIPS), piece_f32(CHIPS), piece_bf16(3), piece_bf16(3), pltpu.VMEM((SHARD_OUT, D), F32),
                        pltpu.SemaphoreType.DMA((4, 2)), pltpu.SemaphoreType.DMA((2,)), pltpu.SemaphoreType.DMA((CHIPS,)),
                        pltpu.SemaphoreType.DMA((CHIPS,)), pltpu.SemaphoreType.DMA((CHIPS,)),
                        pltpu.SemaphoreType.DMA((3,)), pltpu.SemaphoreType.DMA((3,)),
                        pltpu.SemaphoreType.DMA((1,)), pltpu.SemaphoreType.DMA((1,))],
        compiler_params=_cparams(dimension_semantics=("arbitrary", "arbitrary")),
    )(q, k, v, do, lse, delta, cosf, sinf, dwout)


def _bwd_in(h, dh2, dq, dkv, dkr, cq, ckv, dpl, dpg, dag, norm_g, win, gq, wq, gkv, wkv, cosf, sinf):
    tr = ROWS_BWD
    nb = N // tr
    per = tr // HALO
    lead = HEAD_ROWS

    def body(h_ref, dh2_ref, dq_ref, dkv_ref, dkr_ref, cq_ref, ckv_ref, dpl_ref, halo_ref, dpg_ref, dag_ref,
             g_ref, win_ref, gq_ref, wq_ref, gkv_ref, wkv_ref, cos_ref, sin_ref,
             gx_ref, dmeta_ref, dwin_ref, dwq_ref, dwkv_ref, dg_ref, dgq_ref, dgkv_ref, dh_buf, gx_sem):
        i = pl.program_id(0)

        @pl.when(i == 0)
        def _():
            dwin_ref[...] = jnp.zeros_like(dwin_ref)
            dwq_ref[...] = jnp.zeros_like(dwq_ref)
            dwkv_ref[...] = jnp.zeros_like(dwkv_ref)
            dg_ref[...] = jnp.zeros_like(dg_ref)
            dgq_ref[...] = jnp.zeros_like(dgq_ref)
            dgkv_ref[...] = jnp.zeros_like(dgkv_ref)

        row0 = i * tr
        h = h_ref[...]
        r = lax.rsqrt(jnp.mean(h * h, axis=-1, keepdims=True) + EPS)
        n = h * r
        gv = g_ref[...]
        hn = (n * gv).astype(BF16)
        cq = cq_ref[...]
        rq = lax.rsqrt(jnp.mean(cq * cq, axis=-1, keepdims=True) + EPS)
        nq = cq * rq
        gqv = gq_ref[...]
        cqn = (nq * gqv).astype(BF16)
        dcqn = jnp.zeros((tr, Q_LORA), F32)
        for hd in range(HEADS):
            dqf = dq_ref[hd]
            dcqn = dcqn + _nn(dqf, wq_ref[hd])
            dwq_ref[hd] += _tn(dqf, cqn)
        dgq_ref[...] += jnp.sum(dcqn * nq, axis=0, keepdims=True)
        dnq = dcqn * gqv
        dcq = rq * (dnq - nq * jnp.mean(dnq * nq, axis=-1, keepdims=True))

        ckv = ckv_ref[...]
        rkv = lax.rsqrt(jnp.mean(ckv * ckv, axis=-1, keepdims=True) + EPS)
        nkv = ckv * rkv
        gkvv = gkv_ref[...]
        ckvn = (nkv * gkvv).astype(BF16)
        dckvn = jnp.zeros((tr, KV_LORA), F32)
        for hd in range(HEADS):
            dkv = dkv_ref[hd]
            dckvn = dckvn + _nt(dkv, wkv_ref[hd])
            dwkv_ref[hd] += _tn(ckvn, dkv)
        dgkv_ref[...] += jnp.sum(dckvn * nkv, axis=0, keepdims=True)
        dnkv = dckvn * gkvv
        dckv = rkv * (dnkv - nkv * jnp.mean(dnkv * nkv, axis=-1, keepdims=True))
        dkr = _unrope(dkr_ref[...], cos_ref[...], sin_ref[...])

        cur = dpl_ref[...]
        halo = jnp.where(i < nb - 1, halo_ref[...], 0.0)
        dpi = []
        for g, w in enumerate(POOL_WINDOWS):
            sl = slice(g * POOL_GROUP, (g + 1) * POOL_GROUP)
            a = jnp.concatenate([cur[:, sl] * _inv_count(row0, tr, w), halo[:, sl] * _inv_count(row0 + tr, HALO, w)], axis=0)
            acc = a
            shift = 1
            while shift < w:
                acc = acc + pltpu.roll(acc, tr + HALO - shift, 0)
                shift *= 2
            dpi.append(acc[0:tr] - cur[:, sl])

        du = jnp.concatenate([t.astype(BF16) for t in dpi] + [dpg_ref[...]] + [t.astype(BF16) for t in (dcq, dckv, dkr)],
                             axis=1)
        dagb = dag_ref[...]
        dwin_ref[0:O_KR_END, :] += _tn(du, hn)
        dwin_ref[O_AG:D_IN, :] += _tn(dagb, hn)
        dhn = _nn(du, win_ref[0:O_KR_END, :]) + _nn(dagb, win_ref[O_AG:D_IN, :])
        dg_ref[...] += jnp.sum(dhn * n, axis=0, keepdims=True)
        dn = dhn * gv
        dh = dh2_ref[...] + r * (dn - n * jnp.mean(dn * n, axis=-1, keepdims=True))

        first = pltpu.make_async_copy(dh_buf.at[pl.ds(lead, tr - lead), :], gx_ref.at[pl.ds(0, tr - lead), :], gx_sem)
        later = lambda step: pltpu.make_async_copy(
            dh_buf, gx_ref.at[pl.ds(pl.multiple_of(step * tr - lead, 16), tr), :], gx_sem)

        @pl.when(i == 1)
        def _():
            first.wait()

        @pl.when(i > 1)
        def _():
            later(i - 1).wait()

        dh_buf[...] = dh

        @pl.when(i == 0)
        def _():
            first.start()
            for chip in range(CHIPS):
                dmeta_ref[chip] = dh[PAD:HEAD_ROWS, chip * 256:(chip + 1) * 256]

        @pl.when(i > 0)
        def _():
            later(i).start()

        @pl.when(i == nb - 1)
        def _():
            later(i).wait()

    head = lambda w: pl.BlockSpec((HEADS, tr, w), lambda i: (0, i, 0))
    halo_spec = pl.BlockSpec((HALO, D_POOL), lambda i: (jnp.minimum((i + 1) * per, N // HALO - 1), 0))
    return pl.pallas_call(
        body,
        name="bwd_in",
        grid=(nb,),
        in_specs=[
            _rows(D, tr), _rows(D, tr), head(256), head(256), _rows(128, tr), _rows(Q_LORA, tr), _rows(KV_LORA, tr),
            _rows(D_POOL, tr), halo_spec, _rows(D_POOL, tr), _rows(D_POOL, tr),
            _const(1, D), _const(D_IN, D), _const(1, Q_LORA), _const(HEADS, 256, Q_LORA),
            _const(1, KV_LORA), _const(HEADS, KV_LORA, 256), _rows(128, tr), _rows(128, tr),
        ],
        out_specs=[
            pl.BlockSpec(memory_space=pl.ANY), _const(CHIPS, N_META, 256), _const(D_IN, D), _const(HEADS, 256, Q_LORA),
            _const(HEADS, KV_LORA, 256), _const(1, D), _const(1, Q_LORA), _const(1, KV_LORA),
        ],
        out_shape=[
            jax.ShapeDtypeStruct((S, D), F32), jax.ShapeDtypeStruct((CHIPS, N_META, 256), F32),
            jax.ShapeDtypeStruct((D_IN, D), F32), jax.ShapeDtypeStruct((HEADS, 256, Q_LORA), F32),
            jax.ShapeDtypeStruct((HEADS, KV_LORA, 256), F32),
            jax.ShapeDtypeStruct((1, D), F32), jax.ShapeDtypeStruct((1, Q_LORA), F32), jax.ShapeDtypeStruct((1, KV_LORA), F32),
        ],
        scratch_shapes=[pltpu.VMEM((tr, D), F32), pltpu.SemaphoreType.DMA],
        compiler_params=_cparams(dimension_semantics=("arbitrary",)),
    )(h, dh2, dq, dkv, dkr, cq, ckv, dpl, dpl, dpg, dag, norm_g, win, gq, wq, gkv, wkv, cosf, sinf)


def _local_step(h, tgt, norm_g, win, gq, wq, gkv, wkv, pool_w, pool_scale, wout_s, gf, cosf, sinf):
    pool_in, pool_gate, cq, ckv, attn_gate, q, k, v = _fwd_in(h, norm_g, win, gq, wq, gkv, wkv, cosf, sinf)
    attn, lse, wout = _attn_fwd(q, k, v, wout_s)
    dh2, do, delta, dag, dpg, dpl, dwout, dpw, dps, dgf, loss = _mid(
        h, tgt, pool_in, pool_gate, attn_gate, attn, pool_w, pool_scale, wout, gf)
    dq, dkv, dkr, gwout = _attn_bwd(q, k, v, do, lse, delta, cosf, sinf, dwout)
    gx, dmeta, dwin, dwq, dwkv, dg, dgq, dgkv = _bwd_in(
        h, dh2, dq, dkv, dkr, cq, ckv, dpl, dpg, dag, norm_g, win, gq, wq, gkv, wkv, cosf, sinf)
    return dict(gx=gx, dmeta=dmeta, dwin=dwin, dwq=dwq, dwkv=dwkv, gwout=gwout, dg=dg, dgq=dgq, dgkv=dgkv,
                dpw=dpw, dps=dps, dgf=dgf, loss=loss)


_CHIP_RELS = ((0, 0), (1, 0), (0, 1), (1, 1))

_ARR_ROWS = (SHARD_IN, SHARD_OUT, 256, KV_LORA, N_META)
_ARR_COLS = (D, D, Q_LORA, 256, 256)
_PIECES = (
    (0, 0, 256, 0), (0, 256, SHARD_IN - 256, 1),
    (1, 0, 128, 0), (1, 128, 128, 1),
    (2, 0, 128, 0), (2, 128, 128, 1),
    (3, 0, 64, 0), (3, 64, 64, 1),
    (4, 0, N_META, 0),
)
_NP = len(_PIECES)
_PIECE_MAX = (256, 128, 128, 64, N_META)


def _gathered_at(refs, arr, chip, r0, n):
    if arr in (0, 1):
        return refs[arr].at[pl.ds(pl.multiple_of(_ARR_ROWS[arr] * chip + r0, 16), n), :]
    return refs[arr].at[chip, pl.ds(r0, n), :]


def _remote(src, dst, send_sem, recv_sem, to):
    return pltpu.make_async_remote_copy(src_ref=src, dst_ref=dst, send_sem=send_sem, recv_sem=recv_sem,
                                        device_id=to, device_id_type=MESH)


def _gather_weights(winT_s, wqT_s, wkv_s, meta_s, x2, tgt2):
    arrays = (0, 2, 3, 4)

    def body(win_ref, wq_ref, wkv_ref, meta_ref, x_ref, t_ref, win_o, wq_o, wkv_o, h_o, tp_o,
             s_win, s_wq, s_wkv, meta_all, head_buf, x_buf, t_buf, ici_send, ici_recv, fwd_send, fwd_recv,
             loc_sems, own_sems):
        x, y, c = lax.axis_index("x"), lax.axis_index("y"), lax.axis_index("c")
        me = 2 * x + y
        stage = (s_win, None, s_wq, s_wkv, meta_ref)
        outs = (win_o, None, wq_o, wkv_o, meta_all)

        frames = pl.ds(HEAD_ROWS, S)
        loads = [pltpu.make_async_copy(x_ref, x_buf, loc_sems.at[0]), pltpu.make_async_copy(t_ref, t_buf, loc_sems.at[1])]
        local = [pltpu.make_async_copy(x_buf, h_o.at[frames, :], loc_sems.at[0]),
                 pltpu.make_async_copy(t_buf, tp_o.at[frames, :], loc_sems.at[1])]
        for cp in loads:
            cp.start()

        s_win[...] = win_ref[...].astype(BF16)
        s_wq[0:QK, :] = wq_ref[...].astype(BF16)
        s_wq[QK:256, :] = jnp.zeros((256 - QK, Q_LORA), BF16)
        s_wkv[...] = wkv_ref[...].astype(BF16)

        def chip_of(rel):
            fx, fy = _CHIP_RELS[rel]
            return 2 * (x ^ fx) + (y ^ fy)

        def same_core_of(rel):
            fx, fy = _CHIP_RELS[rel]
            return (x ^ fx, y ^ fy, c)

        def ici_copy(rel, i, src_chip, to):
            arr, r0, n, _ = _PIECES[i]
            k = (rel - 1) * _NP + i
            return _remote(stage[arr].at[pl.ds(r0, n), :], _gathered_at(outs, arr, src_chip, r0, n),
                           ici_send.at[k], ici_recv.at[k], to)

        def fwd_copy(rel, i, to):
            arr, r0, n, _ = _PIECES[i]
            k = (rel - 1) * _NP + i
            place = _gathered_at(outs, arr, chip_of(rel), r0, n)
            return _remote(place, place, fwd_send.at[k], fwd_recv.at[k], to)

        for core in (0, 1):
            @pl.when(c == core)
            def _(core=core):
                mine = [i for i in range(_NP) if _PIECES[i][3] == core and _PIECES[i][0] in arrays]
                theirs = [i for i in range(_NP) if _PIECES[i][3] != core and _PIECES[i][0] in arrays]
                sends = [ici_copy(rel, i, me, same_core_of(rel)) for rel in (1, 2, 3) for i in mine]
                for cp in sends:
                    cp.start()
                for ld, st in zip(loads, local):
                    ld.wait()
                    st.start()
                own = [pltpu.make_async_copy(stage[arr], _gathered_at(outs, arr, me, 0, _ARR_ROWS[arr]), own_sems.at[arr])
                       for arr in arrays if arr != 4]
                for cp in own:
                    cp.start()
                meta_all[me] = meta_ref[...]
                for rel in (1, 2, 3):
                    for i in mine:
                        ici_copy(rel, i, chip_of(rel), (x, y, c)).wait_recv()
                        fwd = fwd_copy(rel, i, (x, y, 1 - c))
                        fwd.start()
                        sends.append(fwd)
                for rel in (1, 2, 3):
                    for i in theirs:
                        fwd_copy(rel, i, (x, y, c)).wait_recv()
                for cp in sends:
                    cp.wait_send()
                for cp in own:
                    cp.wait()

        head_buf[...] = jnp.zeros_like(head_buf)
        zeros = pltpu.make_async_copy(head_buf, tp_o.at[pl.ds(0, HEAD_ROWS), :], loc_sems.at[2])
        zeros.start()
        zeros.wait()
        for chip in range(CHIPS):
            head_buf[PAD:HEAD_ROWS, chip * 256:(chip + 1) * 256] = meta_all[chip]
        head = pltpu.make_async_copy(head_buf, h_o.at[pl.ds(0, HEAD_ROWS), :], loc_sems.at[2])
        head.start()
        head.wait()
        for cp in local:
            cp.wait()

    vm = pl.BlockSpec(memory_space=pltpu.VMEM)
    hbm = pl.BlockSpec(memory_space=pl.ANY)
    return pl.pallas_call(
        body,
        name="gather_weights",
        in_specs=[vm] * 4 + [hbm] * 2,
        out_specs=[hbm] * 5,
        out_shape=[
            jax.ShapeDtypeStruct((D_IN, D), BF16),
            jax.ShapeDtypeStruct((CHIPS, 256, Q_LORA), BF16), jax.ShapeDtypeStruct((CHIPS, KV_LORA, 256), BF16),
            jax.ShapeDtypeStruct((N, D), F32), jax.ShapeDtypeStruct((N, D), F32),
        ],
        scratch_shapes=[pltpu.VMEM((_ARR_ROWS[a], _ARR_COLS[a]), BF16) for a in (0, 2, 3)]
        + [pltpu.VMEM((CHIPS, N_META, 256), F32), pltpu.VMEM((HEAD_ROWS, D), F32), pltpu.VMEM((S, D), F32),
           pltpu.VMEM((S, D), F32)]
        + [pltpu.SemaphoreType.DMA((3 * _NP,))] * 4 + [pltpu.SemaphoreType.DMA((3,)), pltpu.SemaphoreType.DMA((4,))],
        compiler_params=_cparams(),
    )(winT_s, wqT_s, wkv_s, meta_s, x2, tgt2)


_SM_ROWS = (len(POOL_WINDOWS) * POOL_GROUP, VEC_ROWS)
_SM_COLS = (POOL_GROUP, D)
_SM_PIECES = ((0, 0, 256, 0), (0, 256, 256, 1), (1, 0, VEC_ROWS, 0))
_NSP = len(_SM_PIECES)


def _reduce_grads(dwin, dwq, dwkv, dmeta4, dpw, dg, dgf, dgq, dgkv, dps, loss):
    arrays = (0, 2, 3, 4)

    def body(dwin_ref, dwq_ref, dwkv_ref, dmeta_ref, dpw_ref, dg_ref, dgf_ref, dgq_ref, dgkv_ref, dps_ref,
             loss_ref, gwin_o, gwq_o, gwkv_o, gmeta_o, gpw_o, gg_o, ggf_o, ggq_o, ggkv_o, gps_o, gloss_o,
             ow0, ow2, ow3, ow4, sb0, sb2, sb3, sb4, st0, st2, st3, st4, rc0, rc2, rc3, rc4,
             vec, sm_sb0, sm_sb1, sm_cs0, sm_cs1, sm_rc0, sm_rc1, vec_fin,
             own_sems, d2d_send, d2d_recv, ici_send, ici_recv, fin_send, fin_recv,
             swap_send, swap_recv, smi_send, smi_recv, smf_send, smf_recv):
        x, y, c = lax.axis_index("x"), lax.axis_index("y"), lax.axis_index("c")
        me = 2 * x + y
        grads = (dwin_ref, None, dwq_ref, dwkv_ref, dmeta_ref)
        outs = (gwin_o, None, gwq_o, gwkv_o, gmeta_o)
        own_buf = (ow0, None, ow2, ow3, ow4)
        sib_buf = (sb0, None, sb2, sb3, sb4)
        stage = (st0, None, st2, st3, st4)
        recv = (rc0, None, rc2, rc3, rc4)
        sm_mine = (dpw_ref, vec)
        sm_sib = (sm_sb0, sm_sb1)
        sm_chip = (sm_cs0, sm_cs1)
        sm_recv = (sm_rc0, sm_rc1)
        sm_out = (gpw_o, vec_fin)
        sibling = (x, y, 1 - c)

        def chip_of(rel):
            fx, fy = _CHIP_RELS[rel]
            return 2 * (x ^ fx) + (y ^ fy)

        def same_core_of(rel):
            fx, fy = _CHIP_RELS[rel]
            return (x ^ fx, y ^ fy, c)

        def slot(bufs, i, idx):
            arr, _, n, _ = _PIECES[i]
            return bufs[arr].at[idx, pl.ds(0, n), :]

        def own_load(rel, i):
            arr, r0, n, _ = _PIECES[i]
            return pltpu.make_async_copy(_gathered_at(grads, arr, chip_of(rel), r0, n), slot(own_buf, i, rel),
                                         own_sems.at[rel * _NP + i])

        def d2d_copy(rel, i):
            arr, r0, n, _ = _PIECES[i]
            k = rel * _NP + i
            return _remote(_gathered_at(grads, arr, chip_of(rel), r0, n), slot(sib_buf, i, rel),
                           d2d_send.at[k], d2d_recv.at[k], sibling)

        def ici_copy(rel, i):
            k = (rel - 1) * _NP + i
            return _remote(slot(stage, i, rel - 1), slot(recv, i, rel - 1), ici_send.at[k], ici_recv.at[k],
                           same_core_of(rel))

        def fin_copy(i):
            arr, r0, n, _ = _PIECES[i]
            place = outs[arr].at[pl.ds(r0, n), :]
            return _remote(place, place, fin_send.at[i], fin_recv.at[i], sibling)

        def sm_ici_copy(rel, j):
            blk, r0, n, _ = _SM_PIECES[j]
            k = (rel - 1) * _NSP + j
            return _remote(sm_chip[blk].at[pl.ds(r0, n), :], sm_recv[blk].at[rel - 1, pl.ds(r0, n), :],
                           smi_send.at[k], smi_recv.at[k], same_core_of(rel))

        def sm_fin_copy(j):
            blk, r0, n, _ = _SM_PIECES[j]
            place = sm_out[blk].at[pl.ds(r0, n), :]
            return _remote(place, place, smf_send.at[j], smf_recv.at[j], sibling)

        vec[...] = jnp.zeros_like(vec)
        vec[0:1, :] = dg_ref[...]
        vec[1:2, :] = dgf_ref[...]
        vec[2:3, V_GQ:V_GQ + Q_LORA] = dgq_ref[...]
        vec[2:3, V_GKV:V_GKV + KV_LORA] = dgkv_ref[...]
        vec[2:3, V_PS:V_PS + D_POOL] = dps_ref[...]
        vec[2:3, V_LOSS:D] = loss_ref[...]
        swaps = [_remote(sm_mine[b], sm_sib[b], swap_send.at[b], swap_recv.at[b], sibling) for b in (0, 1)]
        for cp in swaps:
            cp.start()

        for core in (0, 1):
            @pl.when(c == core)
            def _(core=core):
                mine = [i for i in range(_NP) if _PIECES[i][3] == core and _PIECES[i][0] in arrays]
                theirs = [i for i in range(_NP) if _PIECES[i][3] != core and _PIECES[i][0] in arrays]
                sm_mine_p = [j for j in range(_NSP) if _SM_PIECES[j][3] == core]
                sm_theirs_p = [j for j in range(_NSP) if _SM_PIECES[j][3] != core]
                sends = list(swaps)

                for rel in (1, 2, 3, 0):
                    for i in theirs:
                        cp = d2d_copy(rel, i)
                        cp.start()
                        sends.append(cp)
                    for i in mine:
                        own_load(rel, i).start()

                for b in (0, 1):
                    swaps[b].wait_recv()
                    sm_chip[b][...] = sm_mine[b][...] + sm_sib[b][...]
                for rel in (1, 2, 3):
                    for j in sm_mine_p:
                        cp = sm_ici_copy(rel, j)
                        cp.start()
                        sends.append(cp)

                for rel in (1, 2, 3):
                    for i in mine:
                        arr, r0, n, _ = _PIECES[i]
                        own_load(rel, i).wait()
                        d2d_copy(rel, i).wait_recv()
                        total = slot(own_buf, i, rel)[...] + slot(sib_buf, i, rel)[...]
                        slot(stage, i, rel - 1)[...] = total.astype(stage[arr].dtype)
                        cp = ici_copy(rel, i)
                        cp.start()
                        sends.append(cp)

                for i in mine:
                    arr, r0, n, _ = _PIECES[i]
                    own_load(0, i).wait()
                    d2d_copy(0, i).wait_recv()
                    total = slot(own_buf, i, 0)[...] + slot(sib_buf, i, 0)[...]
                    for rel in (1, 2, 3):
                        ici_copy(rel, i).wait_recv()
                        total = total + slot(recv, i, rel - 1)[...].astype(F32)
                    outs[arr][pl.ds(r0, n), :] = total
                    cp = fin_copy(i)
                    cp.start()
                    sends.append(cp)

                for j in sm_mine_p:
                    blk, r0, n, _ = _SM_PIECES[j]
                    for rel in (1, 2, 3):
                        sm_ici_copy(rel, j).wait_recv()
                    total = jnp.zeros((n, _SM_COLS[blk]), F32)
                    for chip in range(CHIPS):
                        flips = chip ^ me
                        rel = jnp.where(flips == 2, 1, jnp.where(flips == 1, 2, flips))
                        theirs_rows = sm_recv[blk][jnp.maximum(rel - 1, 0), pl.ds(r0, n), :]
                        total = total + jnp.where(rel == 0, sm_chip[blk][pl.ds(r0, n), :], theirs_rows)
                    sm_out[blk][pl.ds(r0, n), :] = total
                    cp = sm_fin_copy(j)
                    cp.start()
                    sends.append(cp)

                for i in theirs:
                    fin_copy(i).wait_recv()
                for j in sm_theirs_p:
                    sm_fin_copy(j).wait_recv()
                for cp in sends:
                    cp.wait_send()

        gg_o[...] = vec_fin[0:1, :]
        ggf_o[...] = vec_fin[1:2, :]
        ggq_o[...] = vec_fin[2:3, V_GQ:V_GQ + Q_LORA]
        ggkv_o[...] = vec_fin[2:3, V_GKV:V_GKV + KV_LORA]
        gps_o[...] = vec_fin[2:3, V_PS:V_PS + D_POOL]
        gloss_o[...] = vec_fin[2:3, V_LOSS:D]

    vm = pl.BlockSpec(memory_space=pltpu.VMEM)
    piece_buf = lambda lead, dtype: [pltpu.VMEM((lead, _PIECE_MAX[a], _ARR_COLS[a]), F32 if a == 4 else dtype)
                                     for a in arrays]
    sm_buf = lambda *lead: [pltpu.VMEM(lead + (_SM_ROWS[b], _SM_COLS[b]), F32) for b in (0, 1)]
    dma = lambda n: [pltpu.SemaphoreType.DMA((n,))] * 2
    return pl.pallas_call(
        body,
        name="reduce_grads",
        in_specs=[pl.BlockSpec(memory_space=pl.ANY)] * 3 + [vm] * 8,
        out_specs=[vm] * 11,
        out_shape=[jax.ShapeDtypeStruct((_ARR_ROWS[a], _ARR_COLS[a]), F32) for a in arrays]
        + [jax.ShapeDtypeStruct((_SM_ROWS[0], _SM_COLS[0]), F32), jax.ShapeDtypeStruct((1, D), F32),
           jax.ShapeDtypeStruct((1, D), F32), jax.ShapeDtypeStruct((1, Q_LORA), F32),
           jax.ShapeDtypeStruct((1, KV_LORA), F32), jax.ShapeDtypeStruct((1, D_POOL), F32),
           jax.ShapeDtypeStruct((1, 128), F32)],
        scratch_shapes=piece_buf(CHIPS, F32) + piece_buf(CHIPS, F32) + piece_buf(3, BF16) + piece_buf(3, BF16)
        + [pltpu.VMEM((VEC_ROWS, D), F32)] + sm_buf() + sm_buf() + sm_buf(3) + [pltpu.VMEM((VEC_ROWS, D), F32)]
        + [pltpu.SemaphoreType.DMA((CHIPS * _NP,))]
        + dma(CHIPS * _NP) + dma(3 * _NP) + dma(_NP) + dma(2) + dma(3 * _NSP) + dma(_NSP),
        compiler_params=_cparams(),
    )(dwin, dwq, dwkv, dmeta4, dpw, dg, dgf, dgq, dgkv, dps, loss)


def _adamw_math(w, g, m, v):
    m = B1 * m + (1.0 - B1) * g
    v = B2 * v + (1.0 - B2) * (g * g)
    m_hat = m / C1
    v_hat = v / C2
    delta = -LR * (m_hat / (jnp.sqrt(v_hat) + ADAM_EPS) + WD * w)
    return delta, m, v


def _adamw_rows(name, w, g, m, v, block_rows):
    rows, cols = w.shape

    def body(w_ref, g_ref, m_ref, v_ref, go_ref, d_ref, nm_ref, nv_ref):
        g = g_ref[...]
        go_ref[...] = g
        d_ref[...], nm_ref[...], nv_ref[...] = _adamw_math(w_ref[...], g, m_ref[...], v_ref[...])

    spec = pl.BlockSpec((block_rows, cols), lambda i: (i, 0))
    return pl.pallas_call(
        body,
        name=name,
        grid=(rows // block_rows,),
        in_specs=[spec] * 4,
        out_specs=[spec] * 4,
        out_shape=[jax.ShapeDtypeStruct(w.shape, F32)] * 4,
        compiler_params=_cparams(dimension_semantics=("arbitrary",)),
    )(w, g, m, v)


def _adamw_small(groups):
    n = len(groups)

    def body(*refs):
        ins, outs = refs[:4 * n], refs[4 * n:]
        for t in range(n):
            w_ref, g_ref, m_ref, v_ref = ins[4 * t:4 * t + 4]
            g = g_ref[0:w_ref.shape[0], :]
            outs[4 * t][...] = g
            outs[4 * t + 1][...], outs[4 * t + 2][...], outs[4 * t + 3][...] = _adamw_math(
                w_ref[...], g, m_ref[...], v_ref[...])

    vm = pl.BlockSpec(memory_space=pltpu.VMEM)
    flat = [a for grp in groups for a in grp]
    outs = pl.pallas_call(
        body,
        name="adamw_small",
        in_specs=[vm] * (4 * n),
        out_specs=[vm] * (4 * n),
        out_shape=[jax.ShapeDtypeStruct(grp[0].shape, F32) for grp in groups for _ in range(4)],
        compiler_params=_cparams(),
    )(*flat)
    return [tuple(outs[4 * t:4 * t + 4]) for t in range(n)]


def _rope_tables():
    half = QK_ROPE // 2
    f32 = np.float32
    inv_freq = (f32(1.0) / (f32(ROPE_THETA) ** (np.arange(half, dtype=f32) / f32(half)))).astype(f32)
    pos = np.arange(N, dtype=f32) - f32(PAD)
    ang = (pos[:, None] * inv_freq[None, :]).astype(f32)
    cos, sin = np.cos(ang).astype(f32), np.sin(ang).astype(f32)
    zero = np.zeros((N, 128 - QK_ROPE), f32)
    return jnp.asarray(np.concatenate([cos, cos, zero], axis=1)), jnp.asarray(np.concatenate([-sin, sin, zero], axis=1))


def kernel(x, meta_tokens, norm_g, w_in, q_norm_g, w_q_b, kv_norm_g, w_kv_b, pool_w, pool_scale, w_out, final_norm_g, loss_target, m_meta_tokens, m_norm_g, m_w_in, m_q_norm_g, m_w_q_b, m_kv_norm_g, m_w_kv_b, m_pool_w, m_pool_scale, m_w_out, m_final_norm_g, v_meta_tokens, v_norm_g, v_w_in, v_q_norm_g, v_w_q_b, v_kv_norm_g, v_w_kv_b, v_pool_w, v_pool_scale, v_w_out, v_final_norm_g):
    tr = lambda a: a[0].T
    win, wq, wkv, h, tgt = _gather_weights(tr(w_in), tr(w_q_b), w_kv_b[0], meta_tokens, x[0], loss_target[0])
    cosf, sinf = _rope_tables()
    gf = final_norm_g.reshape(1, D)

    part = _local_step(h, tgt, norm_g, win, q_norm_g, wq, kv_norm_g, wkv, pool_w[0], pool_scale, w_out[0], gf, cosf, sinf)

    pw2 = lambda a: a.reshape(len(POOL_WINDOWS) * POOL_GROUP, POOL_GROUP)
    gwinT, gwqT, gwkv, gmeta, gpw, gg, ggf, ggq, ggkv, gps, gloss = _reduce_grads(
        part["dwin"], part["dwq"], part["dwkv"], part["dmeta"], pw2(part["dpw"]), part["dg"],
        part["dgf"], part["dgq"], part["dgkv"], part["dps"], part["loss"])
    gwout = part["gwout"]

    r_in = _adamw_rows("adamw_w_in", tr(w_in), gwinT, tr(m_w_in), tr(v_w_in), 248)
    r_out = _adamw_rows("adamw_w_out", w_out[0], gwout, m_w_out[0], v_w_out[0], 128)
    fn2 = lambda a: a.reshape(1, D)
    r_meta, r_norm, r_gq, r_wq, r_gkv, r_wkv, r_pw, r_ps, r_fn = _adamw_small([
        (meta_tokens, gmeta, m_meta_tokens, v_meta_tokens),
        (norm_g, gg, m_norm_g, v_norm_g),
        (q_norm_g, ggq, m_q_norm_g, v_q_norm_g),
        (tr(w_q_b), gwqT, tr(m_w_q_b), tr(v_w_q_b)),
        (kv_norm_g, ggkv, m_kv_norm_g, v_kv_norm_g),
        (w_kv_b[0], gwkv, m_w_kv_b[0], v_w_kv_b[0]),
        (pw2(pool_w), gpw, pw2(m_pool_w), pw2(v_pool_w)),
        (pool_scale, gps, m_pool_scale, v_pool_scale),
        (fn2(final_norm_g), ggf, fn2(m_final_norm_g), fn2(v_final_norm_g)),
    ])
    untr = lambda a: a.T[None]
    pw4 = lambda a: a.reshape(1, len(POOL_WINDOWS), POOL_GROUP, POOL_GROUP)
    per_kind = [[
        r_meta[kind], r_norm[kind], untr(r_in[kind]), r_gq[kind], untr(r_wq[kind]), r_gkv[kind], r_wkv[kind][None],
        pw4(r_pw[kind]), r_ps[kind], r_out[kind][None], r_fn[kind].reshape(D),
    ] for kind in range(4)]
    return (gloss[0, 0], part["gx"][None], *per_kind[0], *per_kind[1], *per_kind[2], *per_kind[3])
```

```python
import jax
import jax.numpy as jnp
import numpy as np
from jax import lax
from jax.experimental import pallas as pl
from jax.experimental.pallas import tpu as pltpu

F32 = jnp.float32
BF16 = jnp.bfloat16

D = 1024
S = 2048
N_META = 16
PAD = 112
HEAD_ROWS = PAD + N_META
N = HEAD_ROWS + S
D_POOL = 512
POOL_WINDOWS = (2, 4, 8, 16)
POOL_GROUP = 128
HALO = 16
HEADS = 4
QK_NOPE = 128
QK_ROPE = 64
QK = QK_NOPE + QK_ROPE
V_HEAD = 128
Q_LORA = 256
KV_LORA = 128
D_IN = 1984
EPS = 1e-6
ROPE_THETA = 10000.0
SCALE = QK ** -0.5
CHIPS = 4

ROWS_FWD = 544
ROWS_MID = 544
ROWS_BWD = 544
TK = 128
TQ = 256
NQ = S // TQ
HEADS_PER_STEP_BWD = 2

O_PI, O_PG, O_CQ, O_CKV, O_KR, O_AG = 0, 512, 1024, 1280, 1408, 1472
O_KR_END = O_KR + 128
SHARD_IN = D_IN // CHIPS
SHARD_OUT = D // CHIPS

LR, B1, B2, ADAM_EPS, WD, STEP = 0.001, 0.9, 0.999, 1e-08, 0.01, 10
C1 = 1.0 - B1**STEP
C2 = 1.0 - B2**STEP

VMEM_LIMIT = 60 * 1024 * 1024
MESH = pl.DeviceIdType.MESH
NEG = -1e30

VEC_ROWS = 8
V_GQ, V_GKV, V_PS, V_LOSS = 0, 256, 384, 896


def _cparams(**kw):
    return pltpu.CompilerParams(vmem_limit_bytes=VMEM_LIMIT, **kw)


def _nt(a, b):
    return lax.dot_general(a, b, (((1,), (1,)), ((), ())), preferred_element_type=F32)


def _tn(a, b):
    return lax.dot_general(a, b, (((0,), (0,)), ((), ())), preferred_element_type=F32)


def _nn(a, b):
    return jnp.dot(a, b, preferred_element_type=F32)


def _swap64(t):
    return pltpu.roll(t, 32, 1) + pltpu.roll(t, 96, 1)


def _sigmoid(x):
    return 1.0 / (1.0 + jnp.exp(-x))


def _low_lanes():
    return (lax.broadcasted_iota(jnp.int32, (1, 128), 1) < QK_ROPE).astype(F32)


def _rows(w, rows):
    return pl.BlockSpec((rows, w), lambda i: (i, 0))


def _const(*shape):
    return pl.BlockSpec(shape, lambda *_: (0,) * len(shape), pipeline_mode=pl.Buffered(1))


STAT_GROUPS = HEADS // HEADS_PER_STEP_BWD


def _stat_slot(head):
    return head // HEADS_PER_STEP_BWD, head % HEADS_PER_STEP_BWD


def _attn_tiles():
    return [(0, TK, TK)] + [(TK + TQ * t, TQ, TK + TQ * (t + 1)) for t in range(NQ)]


def _masked_scores(q, k, rows, klen):
    s = _nt(q, k)
    col = lax.broadcasted_iota(jnp.int32, (1, TK), 1)
    head_bias = jnp.where(col >= PAD, 0.0, NEG)
    if klen == TK:
        return s + head_bias
    r = lax.broadcasted_iota(jnp.int32, (rows, 1), 0) >> 6
    c = lax.broadcasted_iota(jnp.int32, (1, rows), 1) >> 6
    diag_bias = jnp.where(c <= r, 0.0, NEG)
    parts = [s[:, 0:TK] + head_bias]
    if klen - rows > TK:
        parts.append(s[:, TK:klen - rows])
    parts.append(s[:, klen - rows:klen] + diag_bias)
    return jnp.concatenate(parts, axis=1)


def _fwd_in(h, norm_g, win, gq, wq, gkv, wkv, cosf, sinf):
    tr = ROWS_FWD

    def body(h_ref, g_ref, win_ref, gq_ref, wq_ref, gkv_ref, wkv_ref, cos_ref, sin_ref,
             pi_ref, pg_ref, cq_ref, ckv_ref, ag_ref, q_ref, k_ref, v_ref):
        h = h_ref[...]
        r = lax.rsqrt(jnp.mean(h * h, axis=-1, keepdims=True) + EPS)
        hn = ((h * r) * g_ref[...]).astype(BF16)
        u = _nt(hn, win_ref[0:O_KR_END, :])
        pi_ref[...] = u[:, O_PI:O_PG]
        pg_ref[...] = u[:, O_PG:O_CQ]
        cq = u[:, O_CQ:O_CKV]
        ckv = u[:, O_CKV:O_KR]
        cq_ref[...] = cq
        ckv_ref[...] = ckv
        ag_ref[...] = _nt(hn, win_ref[O_AG:D_IN, :])
        cosv = cos_ref[...]
        sinv = sin_ref[...]
        kr = u[:, O_KR:O_KR_END] * _low_lanes()
        kr = (kr * cosv + _swap64(kr) * sinv).astype(BF16)
        rq = lax.rsqrt(jnp.mean(cq * cq, axis=-1, keepdims=True) + EPS)
        cqn = ((cq * rq) * gq_ref[...]).astype(BF16)
        rkv = lax.rsqrt(jnp.mean(ckv * ckv, axis=-1, keepdims=True) + EPS)
        ckvn = ((ckv * rkv) * gkv_ref[...]).astype(BF16)
        for hd in range(HEADS):
            qh = _nt(cqn, wq_ref[hd]) * SCALE
            z = qh[:, QK_NOPE:]
            q_ref[hd, :, 0:QK_NOPE] = qh[:, 0:QK_NOPE].astype(BF16)
            q_ref[hd, :, QK_NOPE:] = (z * cosv + _swap64(z) * sinv).astype(BF16)
            kvh = _nn(ckvn, wkv_ref[hd])
            k_ref[hd, :, 0:QK_NOPE] = kvh[:, 0:QK_NOPE].astype(BF16)
            k_ref[hd, :, QK_NOPE:] = kr
            v_ref[hd] = kvh[:, QK_NOPE:].astype(BF16)

    head = lambda w: pl.BlockSpec((HEADS, tr, w), lambda i: (0, i, 0))
    return pl.pallas_call(
        body,
        name="fwd_in",
        grid=(N // tr,),
        in_specs=[
            _rows(D, tr), _const(1, D), _const(D_IN, D), _const(1, Q_LORA), _const(HEADS, 256, Q_LORA),
            _const(1, KV_LORA), _const(HEADS, KV_LORA, 256), _rows(128, tr), _rows(128, tr),
        ],
        out_specs=[_rows(D_POOL, tr), _rows(D_POOL, tr), _rows(Q_LORA, tr), _rows(KV_LORA, tr), _rows(D_POOL, tr),
                   head(256), head(256), head(V_HEAD)],
        out_shape=[
            jax.ShapeDtypeStruct((N, D_POOL), F32), jax.ShapeDtypeStruct((N, D_POOL), F32),
            jax.ShapeDtypeStruct((N, Q_LORA), F32), jax.ShapeDtypeStruct((N, KV_LORA), F32),
            jax.ShapeDtypeStruct((N, D_POOL), F32),
            jax.ShapeDtypeStruct((HEADS, N, 256), BF16), jax.ShapeDtypeStruct((HEADS, N, 256), BF16),
            jax.ShapeDtypeStruct((HEADS, N, V_HEAD), BF16),
        ],
        compiler_params=_cparams(dimension_semantics=("arbitrary",)),
    )(h, norm_g, win, gq, wq, gkv, wkv, cosf, sinf)


def _attn_fwd(q, k, v, wout_s):
    tiles = _attn_tiles()
    n_t = len(tiles)
    half = SHARD_OUT // 2
    fwd_step = n_t - 2

    def body(q_hbm, k_hbm, v_hbm, wout_ref, o_hbm, lse_ref, wout_o, q_buf, k_buf, v_buf, o_buf, s_wout, in_sems, out_sems,
             ici_send, ici_recv, fwd_send, fwd_recv, own_sem):
        step = pl.program_id(0)
        x, y, c = lax.axis_index("x"), lax.axis_index("y"), lax.axis_index("c")
        me = 2 * x + y

        def chip_of(rel):
            fx, fy = _CHIP_RELS[rel]
            return 2 * (x ^ fx) + (y ^ fy)

        def place(chip, core):
            return wout_o.at[pl.ds(pl.multiple_of(SHARD_OUT * chip + half * core, half), half), :]

        def ici_copy(rel, src_chip, to):
            return _remote(s_wout.at[pl.ds(pl.multiple_of(half * c, half), half), :], place(src_chip, c),
                           ici_send.at[rel - 1], ici_recv.at[rel - 1], to)

        def fwd_copy(rel, core, to):
            spot = place(chip_of(rel), core)
            return _remote(spot, spot, fwd_send.at[rel - 1], fwd_recv.at[rel - 1], to)

        own = pltpu.make_async_copy(s_wout, wout_o.at[pl.ds(pl.multiple_of(SHARD_OUT * me, SHARD_OUT), SHARD_OUT), :], own_sem)

        @pl.when(step == 0)
        def _():
            s_wout[...] = wout_ref[...].astype(BF16)
            own.start()
            for rel in (1, 2, 3):
                fx, fy = _CHIP_RELS[rel]
                ici_copy(rel, me, (x ^ fx, y ^ fy, c)).start()

        @pl.when(step == fwd_step)
        def _():
            for rel in (1, 2, 3):
                ici_copy(rel, chip_of(rel), (x, y, c)).wait_recv()
                fwd_copy(rel, c, (x, y, 1 - c)).start()

        def finish_wout():
            for rel in (1, 2, 3):
                fwd_copy(rel, 1 - c, (x, y, c)).wait_recv()
            for rel in (1, 2, 3):
                ici_copy(rel, me, (x, y, c)).wait_send()
                fwd_copy(rel, c, (x, y, c)).wait_send()
            own.wait()

        def loads(idx):
            q0, rows, _ = tiles[idx]
            rs = pl.ds(q0, rows)
            return [pltpu.make_async_copy(src.at[:, rs, :], dst.at[:, rs, :], in_sems.at[a, idx % 2])
                    for a, (src, dst) in enumerate(((q_hbm, q_buf), (k_hbm, k_buf), (v_hbm, v_buf)))]

        def store(idx):
            q0, rows, _ = tiles[idx]
            return pltpu.make_async_copy(o_buf.at[idx % 2, pl.ds(0, rows), :], o_hbm.at[pl.ds(q0, rows), :],
                                         out_sems.at[idx % 2])

        @pl.when(step == 0)
        def _():
            lse_ref[...] = jnp.zeros_like(lse_ref)
            for cp in loads(0):
                cp.start()

        for idx, (q0, rows, klen) in enumerate(tiles):
            @pl.when(step == idx)
            def _(idx=idx, q0=q0, rows=rows, klen=klen):
                for cp in loads(idx):
                    cp.wait()
                if idx + 1 < n_t:
                    for cp in loads(idx + 1):
                        cp.start()
                if idx >= 2:
                    store(idx - 2).wait()
                for hd in range(HEADS):
                    s = _masked_scores(q_buf[hd, q0:q0 + rows, :], k_buf[hd, 0:klen, :], rows, klen)
                    m = jnp.max(s, axis=-1, keepdims=True)
                    p = jnp.exp(s - m)
                    l = jnp.sum(p, axis=-1, keepdims=True)
                    o_buf[idx % 2, 0:rows, hd * V_HEAD:(hd + 1) * V_HEAD] = _nn(p.astype(BF16), v_buf[hd, 0:klen, :]) / l
                    grp, lane = _stat_slot(hd)
                    lse_ref[grp, q0:q0 + rows, lane:lane + 1] = m + jnp.log(l)
                store(idx).start()
                if idx == n_t - 1:
                    store(idx - 1).wait()
                    store(idx).wait()
                    finish_wout()

    hbm = pl.BlockSpec(memory_space=pl.ANY)
    return pl.pallas_call(
        body,
        name="attn_fwd",
        grid=(n_t,),
        in_specs=[hbm, hbm, hbm, _const(SHARD_OUT, D)],
        out_specs=[hbm, _const(STAT_GROUPS, N, 128), hbm],
        out_shape=[jax.ShapeDtypeStruct((N, HEADS * V_HEAD), F32), jax.ShapeDtypeStruct((STAT_GROUPS, N, 128), F32),
                   jax.ShapeDtypeStruct((D, D), BF16)],
        scratch_shapes=[pltpu.VMEM((HEADS, N, 256), BF16), pltpu.VMEM((HEADS, N, 256), BF16),
                        pltpu.VMEM((HEADS, N, V_HEAD), BF16), pltpu.VMEM((2, TQ, HEADS * V_HEAD), F32),
                        pltpu.VMEM((SHARD_OUT, D), BF16),
                        pltpu.SemaphoreType.DMA((3, 2)), pltpu.SemaphoreType.DMA((2,))]
        + [pltpu.SemaphoreType.DMA((3,))] * 4 + [pltpu.SemaphoreType.DMA],
        compiler_params=_cparams(dimension_semantics=("arbitrary",)),
    )(q, k, v, wout_s)


def _inv_count(row0, rows, w):
    row = row0 + lax.broadcasted_iota(jnp.int32, (rows, 1), 0)
    return 1.0 / jnp.clip(row - (PAD - 1), 1, w).astype(F32)


def _mid(h, tgt, pool_in, pool_gate, attn_gate, attn, pool_w, pool_scale, wout, gf):
    tr = ROWS_MID
    per = tr // HALO
    ng = len(POOL_WINDOWS)

    def body(h_ref, t_ref, pin_ref, halo_ref, pg_ref, ag_ref, at_ref, pw_ref, ps_ref, wout_ref, gf_ref,
             dh2_ref, do_ref, delta_ref, dag_ref, dpg_ref, dpl_ref, dwout_ref, dpw_ref, dps_ref, dgf_ref, loss_ref):
        i = pl.program_id(0)

        @pl.when(i == 0)
        def _():
            dwout_ref[...] = jnp.zeros_like(dwout_ref)
            dpw_ref[...] = jnp.zeros_like(dpw_ref)
            dps_ref[...] = jnp.zeros_like(dps_ref)
            dgf_ref[...] = jnp.zeros_like(dgf_ref)
            loss_ref[...] = jnp.zeros_like(loss_ref)

        row0 = i * tr
        real = (row0 + lax.broadcasted_iota(jnp.int32, (tr, 1), 0)) >= HEAD_ROWS
        h = h_ref[...]

        halo = jnp.where(i > 0, halo_ref[...], 0.0)
        ext = jnp.concatenate([halo, pin_ref[...]], axis=0)
        pooled = []
        for g, w in enumerate(POOL_WINDOWS):
            e = ext[:, g * POOL_GROUP:(g + 1) * POOL_GROUP]
            acc = e
            shift = 1
            while shift < w:
                acc = acc + pltpu.roll(acc, shift, 0)
                shift *= 2
            pooled.append((acc[HALO:] * _inv_count(row0, tr, w) - e[HALO:]).astype(BF16))
        pw = [pw_ref[g].astype(BF16) for g in range(ng)]
        mixed = jnp.concatenate([_nn(pooled[g], pw[g]) for g in range(ng)], axis=1)
        ps = ps_ref[...]
        mixed_s = mixed * ps
        pg = pg_ref[...]
        sig_p = _sigmoid(pg)
        silu_p = pg * sig_p
        pool_out = (silu_p * mixed_s).astype(BF16)
        ag = ag_ref[...]
        sig_a = _sigmoid(ag)
        silu_a = ag * sig_a
        at = at_ref[...]
        attn_out = (silu_a * at).astype(BF16)
        mix = _nn(pool_out, wout_ref[0:D_POOL, :]) + _nn(attn_out, wout_ref[D_POOL:D, :])
        h2 = h + mix

        r2 = lax.rsqrt(jnp.mean(h2 * h2, axis=-1, keepdims=True) + EPS)
        n2 = h2 * r2
        gfv = gf_ref[...]
        err = jnp.where(real, n2 * gfv - t_ref[...], 0.0)
        loss_ref[...] += jnp.sum(jnp.sum(err * err, axis=-1, keepdims=True), axis=0, keepdims=True) * (0.5 / D)
        dy = err * (1.0 / D)
        dgf_ref[...] += jnp.sum(dy * n2, axis=0, keepdims=True)
        dn = dy * gfv
        dh2 = r2 * (dn - n2 * jnp.mean(dn * n2, axis=-1, keepdims=True))
        dh2_ref[...] = dh2
        dh2b = dh2.astype(BF16)

        dwout_ref[0:D_POOL, :] += _tn(pool_out, dh2b)
        dwout_ref[D_POOL:D, :] += _tn(attn_out, dh2b)
        dcat = _nt(dh2b, wout_ref[...])
        dpo = dcat[:, 0:D_POOL]
        dao = dcat[:, D_POOL:D]
        do = dao * silu_a
        prod = do * at
        delta_ref[...] = jnp.zeros_like(delta_ref)
        for hd in range(HEADS):
            grp, lane = _stat_slot(hd)
            cols = slice(hd * V_HEAD, (hd + 1) * V_HEAD)
            do_ref[grp, :, lane * V_HEAD:(lane + 1) * V_HEAD] = do[:, cols].astype(BF16)
            delta_ref[grp, :, lane:lane + 1] = jnp.sum(prod[:, cols], axis=-1, keepdims=True)
        dag_ref[...] = (dao * at * (sig_a * (1.0 + ag * (1.0 - sig_a)))).astype(BF16)
        dmixed_s = dpo * silu_p
        dpg_ref[...] = (dpo * mixed_s * (sig_p * (1.0 + pg * (1.0 - sig_p)))).astype(BF16)
        dps_ref[...] += jnp.sum(dmixed_s * mixed, axis=0, keepdims=True)
        dmixed = (dmixed_s * ps).astype(BF16)
        dpl = []
        for g in range(ng):
            dm = dmixed[:, g * POOL_GROUP:(g + 1) * POOL_GROUP]
            dpl.append(_nt(dm, pw[g]))
            dpw_ref[g] += _tn(pooled[g], dm)
        dpl_ref[...] = jnp.concatenate(dpl, axis=1)

    halo_spec = pl.BlockSpec((HALO, D_POOL), lambda i: (jnp.maximum(i * per - 1, 0), 0))
    return pl.pallas_call(
        body,
        name="mid",
        grid=(N // tr,),
        in_specs=[
            _rows(D, tr), _rows(D, tr), _rows(D_POOL, tr), halo_spec, _rows(D_POOL, tr), _rows(D_POOL, tr),
            _rows(D_POOL, tr), _const(ng, POOL_GROUP, POOL_GROUP), _const(1, D_POOL), _const(D, D), _const(1, D),
        ],
        out_specs=[
            _rows(D, tr), pl.BlockSpec((STAT_GROUPS, tr, HEADS_PER_STEP_BWD * V_HEAD), lambda i: (0, i, 0)),
            pl.BlockSpec((STAT_GROUPS, tr, 128), lambda i: (0, i, 0)),
            _rows(D_POOL, tr), _rows(D_POOL, tr), _rows(D_POOL, tr),
            _const(D, D), _const(ng, POOL_GROUP, POOL_GROUP), _const(1, D_POOL), _const(1, D), _const(1, 128),
        ],
        out_shape=[
            jax.ShapeDtypeStruct((N, D), F32), jax.ShapeDtypeStruct((STAT_GROUPS, N, HEADS_PER_STEP_BWD * V_HEAD), BF16),
            jax.ShapeDtypeStruct((STAT_GROUPS, N, 128), F32),
            jax.ShapeDtypeStruct((N, D_POOL), BF16), jax.ShapeDtypeStruct((N, D_POOL), BF16),
            jax.ShapeDtypeStruct((N, D_POOL), F32), jax.ShapeDtypeStruct((D, D), F32),
            jax.ShapeDtypeStruct((ng, POOL_GROUP, POOL_GROUP), F32),
            jax.ShapeDtypeStruct((1, D_POOL), F32), jax.ShapeDtypeStruct((1, D), F32), jax.ShapeDtypeStruct((1, 128), F32),
        ],
        compiler_params=_cparams(dimension_semantics=("arbitrary",)),
    )(h, tgt, pool_in, pool_in, pool_gate, attn_gate, attn, pool_w, pool_scale, wout, gf)


def _unrope(dy, cosv, sinv):
    return dy * cosv + _swap64(dy * sinv) * _low_lanes()


def _attn_bwd(q, k, v, do, lse, delta, cosf, sinf, dwout):
    tiles = _attn_tiles()
    hp = HEADS_PER_STEP_BWD
    n_g = HEADS // hp
    n_t = len(tiles)
    half = SHARD_OUT // 2
    send_at, sum_at = (0, 2), (n_g - 1, n_t // 2)

    def body(q_hbm, k_hbm, v_hbm, do_hbm, lse_ref, delta_ref, cos_ref, sin_ref, dwout_hbm, dq_hbm, dkv_ref, dkr_ref,
             gwout_ref, q_buf, k_buf, v_buf, do_buf, dq_buf, dk_acc, dv_acc, own_w, sib_w, stage_w, recv_w, gw_buf,
             in_sems, out_sems, ow_sems, d2d_send, d2d_recv, ici_send, ici_recv, fin_send, fin_recv):
        grp = pl.program_id(0)
        step = pl.program_id(1)
        heads = pl.ds(grp * hp, hp)
        x, y, c = lax.axis_index("x"), lax.axis_index("y"), lax.axis_index("c")
        sibling = (x, y, 1 - c)

        def chip_of(rel):
            fx, fy = _CHIP_RELS[rel]
            return 2 * (x ^ fx) + (y ^ fy)

        def piece(chip, core):
            return dwout_hbm.at[pl.ds(pl.multiple_of(SHARD_OUT * chip + half * core, half), half), :]

        def own_load(rel):
            return pltpu.make_async_copy(piece(chip_of(rel), c), own_w.at[rel], ow_sems.at[rel])

        def d2d_copy(rel):
            return _remote(piece(chip_of(rel), 1 - c), sib_w.at[rel], d2d_send.at[rel], d2d_recv.at[rel], sibling)

        def ici_copy(rel):
            fx, fy = _CHIP_RELS[rel]
            return _remote(stage_w.at[rel - 1], recv_w.at[rel - 1], ici_send.at[rel - 1], ici_recv.at[rel - 1],
                           (x ^ fx, y ^ fy, c))

        def fin_copy(core):
            spot = gw_buf.at[pl.ds(pl.multiple_of(half * core, half), half), :]
            return _remote(spot, spot, fin_send.at[0], fin_recv.at[0], sibling)

        @pl.when((grp == 0) & (step == 0))
        def _():
            for rel in (1, 2, 3, 0):
                d2d_copy(rel).start()
                own_load(rel).start()

        @pl.when((grp == send_at[0]) & (step == send_at[1]))
        def _():
            for rel in (1, 2, 3):
                own_load(rel).wait()
                d2d_copy(rel).wait_recv()
                stage_w[rel - 1] = (own_w[rel] + sib_w[rel]).astype(BF16)
                ici_copy(rel).start()

        @pl.when((grp == sum_at[0]) & (step == sum_at[1]))
        def _():
            own_load(0).wait()
            d2d_copy(0).wait_recv()
            total = own_w[0] + sib_w[0]
            for rel in (1, 2, 3):
                ici_copy(rel).wait_recv()
                total = total + recv_w[rel - 1].astype(F32)
            gw_buf[pl.ds(pl.multiple_of(half * c, half), half), :] = total
            fin_copy(c).start()

        def finish_dwout():
            fin_copy(1 - c).wait_recv()
            for rel in (0, 1, 2, 3):
                d2d_copy(rel).wait_send()
            for rel in (1, 2, 3):
                ici_copy(rel).wait_send()
            fin_copy(c).wait_send()
            gwout_ref[...] = gw_buf[...]

        def loads(g, idx):
            q0, rows, _ = tiles[idx]
            rs = pl.ds(q0, rows)
            par = (g * n_t + idx) % 2
            hs = pl.ds(g * hp, hp)
            pairs = ((q_hbm.at[hs, rs, :], q_buf.at[:, rs, :]), (k_hbm.at[hs, rs, :], k_buf.at[:, rs, :]),
                     (v_hbm.at[hs, rs, :], v_buf.at[:, rs, :]), (do_hbm.at[g, rs, :], do_buf.at[rs, :]))
            return [pltpu.make_async_copy(src, dst, in_sems.at[a, par]) for a, (src, dst) in enumerate(pairs)]

        def store(idx):
            q0, rows, _ = tiles[idx]
            return pltpu.make_async_copy(dq_buf.at[idx % 2, :, pl.ds(0, rows), :], dq_hbm.at[heads, pl.ds(q0, rows), :],
                                         out_sems.at[idx % 2])

        @pl.when(step == 0)
        def _():
            dk_acc[...] = jnp.zeros_like(dk_acc)
            dv_acc[...] = jnp.zeros_like(dv_acc)

        @pl.when((step == 0) & (grp == 0))
        def _():
            dkr_ref[...] = jnp.zeros_like(dkr_ref)
            for cp in loads(grp, 0):
                cp.start()

        for idx, (q0, rows, klen) in enumerate(tiles):
            @pl.when(step == idx)
            def _(idx=idx, q0=q0, rows=rows, klen=klen):
                for cp in loads(grp, idx):
                    cp.wait()
                if idx + 1 < n_t:
                    for cp in loads(grp, idx + 1):
                        cp.start()
                if idx >= 2:
                    store(idx - 2).wait()
                qs = pl.ds(q0, rows)
                for hd in range(hp):
                    qv = q_buf[hd, qs, :]
                    kv = k_buf[hd, 0:klen, :]
                    p = jnp.exp(_masked_scores(qv, kv, rows, klen) - lse_ref[0, qs, hd:hd + 1])
                    dob = do_buf[qs, hd * V_HEAD:(hd + 1) * V_HEAD]
                    ds = (p * (_nt(dob, v_buf[hd, 0:klen, :]) - delta_ref[0, qs, hd:hd + 1])).astype(BF16)
                    dq = _nn(ds, kv) * SCALE
                    dq_buf[idx % 2, hd, 0:rows, 0:QK_NOPE] = dq[:, 0:QK_NOPE].astype(BF16)
                    dq_buf[idx % 2, hd, 0:rows, QK_NOPE:] = _unrope(dq[:, QK_NOPE:], cos_ref[qs, :], sin_ref[qs, :]).astype(BF16)
                    dk_acc[hd, 0:klen, :] += _tn(ds, qv)
                    dv_acc[hd, 0:klen, :] += _tn(p.astype(BF16), dob)
                store(idx).start()

        @pl.when(step == n_t - 1)
        def _():
            @pl.when(grp + 1 < n_g)
            def _():
                for cp in loads(grp + 1, 0):
                    cp.start()

            for hd in range(hp):
                dkv_ref[hd, :, 0:QK_NOPE] = dk_acc[hd, :, 0:QK_NOPE].astype(BF16)
                dkv_ref[hd, :, QK_NOPE:] = dv_acc[hd].astype(BF16)
                dkr_ref[...] += dk_acc[hd, :, QK_NOPE:]
            store(n_t - 2).wait()
            store(n_t - 1).wait()

            @pl.when(grp == n_g - 1)
            def _():
                finish_dwout()

    hbm = pl.BlockSpec(memory_space=pl.ANY)
    stat = pl.BlockSpec((1, N, 128), lambda g, t: (g, 0, 0), pipeline_mode=pl.Buffered(1))
    piece_f32 = lambda lead: pltpu.VMEM((lead, half, D), F32)
    piece_bf16 = lambda lead: pltpu.VMEM((lead, half, D), BF16)
    return pl.pallas_call(
        body,
        name="attn_bwd",
        grid=(n_g, n_t),
        in_specs=[hbm, hbm, hbm, hbm, stat, stat, _const(N, 128), _const(N, 128), hbm],
        out_specs=[hbm, pl.BlockSpec((hp, N, 256), lambda g, t: (g, 0, 0), pipeline_mode=pl.Buffered(1)), _const(N, 128),
                   _const(SHARD_OUT, D)],
        out_shape=[
            jax.ShapeDtypeStruct((HEADS, N, 256), BF16), jax.ShapeDtypeStruct((HEADS, N, 256), BF16),
            jax.ShapeDtypeStruct((N, 128), F32), jax.ShapeDtypeStruct((SHARD_OUT, D), F32),
        ],
        scratch_shapes=[pltpu.VMEM((hp, N, 256), BF16), pltpu.VMEM((hp, N, 256), BF16), pltpu.VMEM((hp, N, V_HEAD), BF16),
                        pltpu.VMEM((N, hp * V_HEAD), BF16), pltpu.VMEM((2, hp, TQ, 256), BF16),
                        pltpu.VMEM((hp, N, 256), F32), pltpu.VMEM((hp, N, V_HEAD), F32),
                        piece_f32(CHIPS), piece_f32(CHIPS), piece_bf16(3), piece_bf16(3), pltpu.VMEM((SHARD_OUT, D), F32),
                        pltpu.SemaphoreType.DMA((4, 2)), pltpu.SemaphoreType.DMA((2,)), pltpu.SemaphoreType.DMA((CHIPS,)),
                        pltpu.SemaphoreType.DMA((CHIPS,)), pltpu.SemaphoreType.DMA((CHIPS,)),
                        pltpu.SemaphoreType.DMA((3,)), pltpu.SemaphoreType.DMA((3,)),
                        pltpu.SemaphoreType.DMA((1,)), pltpu.SemaphoreType.DMA((1,))],
        compiler_params=_cparams(dimension_semantics=("arbitrary", "arbitrary")),
    )(q, k, v, do, lse, delta, cosf, sinf, dwout)


def _bwd_in(h, dh2, dq, dkv, dkr, cq, ckv, dpl, dpg, dag, norm_g, win, gq, wq, gkv, wkv, cosf, sinf, adam_out):
    tr = ROWS_BWD
    nb = N // tr
    per = tr // HALO
    lead = HEAD_ROWS
    adam_rows = SHARD_OUT // nb

    def body(h_ref, dh2_ref, dq_ref, dkv_ref, dkr_ref, cq_ref, ckv_ref, dpl_ref, halo_ref, dpg_ref, dag_ref,
             g_ref, win_ref, gq_ref, wq_ref, gkv_ref, wkv_ref, cos_ref, sin_ref, aw_ref, ag_ref, am_ref, av_ref,
             gx_ref, dmeta_ref, dwin_ref, dwq_ref, dwkv_ref, dg_ref, dgq_ref, dgkv_ref, ago_ref, ad_ref, anm_ref, anv_ref,
             dh_buf, gx_sem):
        i = pl.program_id(0)
        grad_out = ag_ref[...]
        ago_ref[...] = grad_out
        ad_ref[...], anm_ref[...], anv_ref[...] = _adamw_math(aw_ref[...], grad_out, am_ref[...], av_ref[...])

        @pl.when(i == 0)
        def _():
            dwin_ref[...] = jnp.zeros_like(dwin_ref)
            dwq_ref[...] = jnp.zeros_like(dwq_ref)
            dwkv_ref[...] = jnp.zeros_like(dwkv_ref)
            dg_ref[...] = jnp.zeros_like(dg_ref)
            dgq_ref[...] = jnp.zeros_like(dgq_ref)
            dgkv_ref[...] = jnp.zeros_like(dgkv_ref)

        row0 = i * tr
        h = h_ref[...]
        r = lax.rsqrt(jnp.mean(h * h, axis=-1, keepdims=True) + EPS)
        n = h * r
        gv = g_ref[...]
        hn = (n * gv).astype(BF16)
        cq = cq_ref[...]
        rq = lax.rsqrt(jnp.mean(cq * cq, axis=-1, keepdims=True) + EPS)
        nq = cq * rq
        gqv = gq_ref[...]
        cqn = (nq * gqv).astype(BF16)
        dcqn = jnp.zeros((tr, Q_LORA), F32)
        for hd in range(HEADS):
            dqf = dq_ref[hd]
            dcqn = dcqn + _nn(dqf, wq_ref[hd])
            dwq_ref[hd] += _tn(dqf, cqn)
        dgq_ref[...] += jnp.sum(dcqn * nq, axis=0, keepdims=True)
        dnq = dcqn * gqv
        dcq = rq * (dnq - nq * jnp.mean(dnq * nq, axis=-1, keepdims=True))

        ckv = ckv_ref[...]
        rkv = lax.rsqrt(jnp.mean(ckv * ckv, axis=-1, keepdims=True) + EPS)
        nkv = ckv * rkv
        gkvv = gkv_ref[...]
        ckvn = (nkv * gkvv).astype(BF16)
        dckvn = jnp.zeros((tr, KV_LORA), F32)
        for hd in range(HEADS):
            dkv = dkv_ref[hd]
            dckvn = dckvn + _nt(dkv, wkv_ref[hd])
            dwkv_ref[hd] += _tn(ckvn, dkv)
        dgkv_ref[...] += jnp.sum(dckvn * nkv, axis=0, keepdims=True)
        dnkv = dckvn * gkvv
        dckv = rkv * (dnkv - nkv * jnp.mean(dnkv * nkv, axis=-1, keepdims=True))
        dkr = _unrope(dkr_ref[...], cos_ref[...], sin_ref[...])

        cur = dpl_ref[...]
        halo = jnp.where(i < nb - 1, halo_ref[...], 0.0)
        dpi = []
        for g, w in enumerate(POOL_WINDOWS):
            sl = slice(g * POOL_GROUP, (g + 1) * POOL_GROUP)
            a = jnp.concatenate([cur[:, sl] * _inv_count(row0, tr, w), halo[:, sl] * _inv_count(row0 + tr, HALO, w)], axis=0)
            acc = a
            shift = 1
            while shift < w:
                acc = acc + pltpu.roll(acc, tr + HALO - shift, 0)
                shift *= 2
            dpi.append(acc[0:tr] - cur[:, sl])

        du = jnp.concatenate([t.astype(BF16) for t in dpi] + [dpg_ref[...]] + [t.astype(BF16) for t in (dcq, dckv, dkr)],
                             axis=1)
        dagb = dag_ref[...]
        dwin_ref[0:O_KR_END, :] += _tn(du, hn)
        dwin_ref[O_AG:D_IN, :] += _tn(dagb, hn)
        dhn = _nn(du, win_ref[0:O_KR_END, :]) + _nn(dagb, win_ref[O_AG:D_IN, :])
        dg_ref[...] += jnp.sum(dhn * n, axis=0, keepdims=True)
        dn = dhn * gv
        dh = dh2_ref[...] + r * (dn - n * jnp.mean(dn * n, axis=-1, keepdims=True))

        first = pltpu.make_async_copy(dh_buf.at[pl.ds(lead, tr - lead), :], gx_ref.at[pl.ds(0, tr - lead), :], gx_sem)
        later = lambda step: pltpu.make_async_copy(
            dh_buf, gx_ref.at[pl.ds(pl.multiple_of(step * tr - lead, 16), tr), :], gx_sem)

        @pl.when(i == 1)
        def _():
            first.wait()

        @pl.when(i > 1)
        def _():
            later(i - 1).wait()

        dh_buf[...] = dh

        @pl.when(i == 0)
        def _():
            first.start()
            for chip in range(CHIPS):
                dmeta_ref[chip] = dh[PAD:HEAD_ROWS, chip * 256:(chip + 1) * 256]

        @pl.when(i > 0)
        def _():
            later(i).start()

        @pl.when(i == nb - 1)
        def _():
            later(i).wait()

    head = lambda w: pl.BlockSpec((HEADS, tr, w), lambda i: (0, i, 0))
    halo_spec = pl.BlockSpec((HALO, D_POOL), lambda i: (jnp.minimum((i + 1) * per, N // HALO - 1), 0))
    return pl.pallas_call(
        body,
        name="bwd_in",
        grid=(nb,),
        in_specs=[
            _rows(D, tr), _rows(D, tr), head(256), head(256), _rows(128, tr), _rows(Q_LORA, tr), _rows(KV_LORA, tr),
            _rows(D_POOL, tr), halo_spec, _rows(D_POOL, tr), _rows(D_POOL, tr),
            _const(1, D), _const(D_IN, D), _const(1, Q_LORA), _const(HEADS, 256, Q_LORA),
            _const(1, KV_LORA), _const(HEADS, KV_LORA, 256), _rows(128, tr), _rows(128, tr),
        ] + [_rows(D, adam_rows)] * 4,
        out_specs=[
            pl.BlockSpec(memory_space=pl.ANY), _const(CHIPS, N_META, 256), _const(D_IN, D), _const(HEADS, 256, Q_LORA),
            _const(HEADS, KV_LORA, 256), _const(1, D), _const(1, Q_LORA), _const(1, KV_LORA),
        ] + [_rows(D, adam_rows)] * 4,
        out_shape=[
            jax.ShapeDtypeStruct((S, D), F32), jax.ShapeDtypeStruct((CHIPS, N_META, 256), F32),
            jax.ShapeDtypeStruct((D_IN, D), F32), jax.ShapeDtypeStruct((HEADS, 256, Q_LORA), F32),
            jax.ShapeDtypeStruct((HEADS, KV_LORA, 256), F32),
            jax.ShapeDtypeStruct((1, D), F32), jax.ShapeDtypeStruct((1, Q_LORA), F32), jax.ShapeDtypeStruct((1, KV_LORA), F32),
        ] + [jax.ShapeDtypeStruct((SHARD_OUT, D), F32)] * 4,
        scratch_shapes=[pltpu.VMEM((tr, D), F32), pltpu.SemaphoreType.DMA],
        compiler_params=_cparams(dimension_semantics=("arbitrary",)),
    )(h, dh2, dq, dkv, dkr, cq, ckv, dpl, dpl, dpg, dag, norm_g, win, gq, wq, gkv, wkv, cosf, sinf, *adam_out)


def _local_step(h, tgt, norm_g, win, gq, wq, gkv, wkv, pool_w, pool_scale, wout_s, m_wout_s, v_wout_s, gf, cosf, sinf):
    pool_in, pool_gate, cq, ckv, attn_gate, q, k, v = _fwd_in(h, norm_g, win, gq, wq, gkv, wkv, cosf, sinf)
    attn, lse, wout = _attn_fwd(q, k, v, wout_s)
    dh2, do, delta, dag, dpg, dpl, dwout, dpw, dps, dgf, loss = _mid(
        h, tgt, pool_in, pool_gate, attn_gate, attn, pool_w, pool_scale, wout, gf)
    dq, dkv, dkr, gwout = _attn_bwd(q, k, v, do, lse, delta, cosf, sinf, dwout)
    gx, dmeta, dwin, dwq, dwkv, dg, dgq, dgkv, *r_out = _bwd_in(
        h, dh2, dq, dkv, dkr, cq, ckv, dpl, dpg, dag, norm_g, win, gq, wq, gkv, wkv, cosf, sinf,
        (wout_s, gwout, m_wout_s, v_wout_s))
    return dict(gx=gx, dmeta=dmeta, dwin=dwin, dwq=dwq, dwkv=dwkv, r_out=tuple(r_out), dg=dg, dgq=dgq, dgkv=dgkv,
                dpw=dpw, dps=dps, dgf=dgf, loss=loss)


_CHIP_RELS = ((0, 0), (1, 0), (0, 1), (1, 1))

_ARR_ROWS = (SHARD_IN, SHARD_OUT, 256, KV_LORA, N_META)
_ARR_COLS = (D, D, Q_LORA, 256, 256)
_PIECES = (
    (0, 0, 256, 0), (0, 256, SHARD_IN - 256, 1),
    (1, 0, 128, 0), (1, 128, 128, 1),
    (2, 0, 128, 0), (2, 128, 128, 1),
    (3, 0, 64, 0), (3, 64, 64, 1),
    (4, 0, N_META, 0),
)
_NP = len(_PIECES)
_PIECE_MAX = (256, 128, 128, 64, N_META)


def _gathered_at(refs, arr, chip, r0, n):
    if arr in (0, 1):
        return refs[arr].at[pl.ds(pl.multiple_of(_ARR_ROWS[arr] * chip + r0, 16), n), :]
    return refs[arr].at[chip, pl.ds(r0, n), :]


def _remote(src, dst, send_sem, recv_sem, to):
    return pltpu.make_async_remote_copy(src_ref=src, dst_ref=dst, send_sem=send_sem, recv_sem=recv_sem,
                                        device_id=to, device_id_type=MESH)


def _gather_weights(winT_s, wqT_s, wkv_s, meta_s, x2, tgt2):
    arrays = (0, 2, 3, 4)

    def body(win_ref, wq_ref, wkv_ref, meta_ref, x_ref, t_ref, win_o, wq_o, wkv_o, h_o, tp_o,
             s_win, s_wq, s_wkv, meta_all, head_buf, x_buf, t_buf, ici_send, ici_recv, fwd_send, fwd_recv,
             loc_sems, own_sems):
        x, y, c = lax.axis_index("x"), lax.axis_index("y"), lax.axis_index("c")
        me = 2 * x + y
        stage = (s_win, None, s_wq, s_wkv, meta_ref)
        outs = (win_o, None, wq_o, wkv_o, meta_all)

        frames = pl.ds(HEAD_ROWS, S)
        loads = [pltpu.make_async_copy(x_ref, x_buf, loc_sems.at[0]), pltpu.make_async_copy(t_ref, t_buf, loc_sems.at[1])]
        local = [pltpu.make_async_copy(x_buf, h_o.at[frames, :], loc_sems.at[0]),
                 pltpu.make_async_copy(t_buf, tp_o.at[frames, :], loc_sems.at[1])]
        for cp in loads:
            cp.start()

        s_win[...] = win_ref[...].astype(BF16)
        s_wq[0:QK, :] = wq_ref[...].astype(BF16)
        s_wq[QK:256, :] = jnp.zeros((256 - QK, Q_LORA), BF16)
        s_wkv[...] = wkv_ref[...].astype(BF16)

        def chip_of(rel):
            fx, fy = _CHIP_RELS[rel]
            return 2 * (x ^ fx) + (y ^ fy)

        def same_core_of(rel):
            fx, fy = _CHIP_RELS[rel]
            return (x ^ fx, y ^ fy, c)

        def ici_copy(rel, i, src_chip, to):
            arr, r0, n, _ = _PIECES[i]
            k = (rel - 1) * _NP + i
            return _remote(stage[arr].at[pl.ds(r0, n), :], _gathered_at(outs, arr, src_chip, r0, n),
                           ici_send.at[k], ici_recv.at[k], to)

        def fwd_copy(rel, i, to):
            arr, r0, n, _ = _PIECES[i]
            k = (rel - 1) * _NP + i
            place = _gathered_at(outs, arr, chip_of(rel), r0, n)
            return _remote(place, place, fwd_send.at[k], fwd_recv.at[k], to)

        for core in (0, 1):
            @pl.when(c == core)
            def _(core=core):
                mine = [i for i in range(_NP) if _PIECES[i][3] == core and _PIECES[i][0] in arrays]
                theirs = [i for i in range(_NP) if _PIECES[i][3] != core and _PIECES[i][0] in arrays]
                sends = [ici_copy(rel, i, me, same_core_of(rel)) for rel in (1, 2, 3) for i in mine]
                for cp in sends:
                    cp.start()
                for ld, st in zip(loads, local):
                    ld.wait()
                    st.start()
                own = [pltpu.make_async_copy(stage[arr], _gathered_at(outs, arr, me, 0, _ARR_ROWS[arr]), own_sems.at[arr])
                       for arr in arrays if arr != 4]
                for cp in own:
                    cp.start()
                meta_all[me] = meta_ref[...]
                for rel in (1, 2, 3):
                    for i in mine:
                        ici_copy(rel, i, chip_of(rel), (x, y, c)).wait_recv()
                        fwd = fwd_copy(rel, i, (x, y, 1 - c))
                        fwd.start()
                        sends.append(fwd)
                for rel in (1, 2, 3):
                    for i in theirs:
                        fwd_copy(rel, i, (x, y, c)).wait_recv()
                for cp in sends:
                    cp.wait_send()
                for cp in own:
                    cp.wait()

        head_buf[...] = jnp.zeros_like(head_buf)
        zeros = pltpu.make_async_copy(head_buf, tp_o.at[pl.ds(0, HEAD_ROWS), :], loc_sems.at[2])
        zeros.start()
        zeros.wait()
        for chip in range(CHIPS):
            head_buf[PAD:HEAD_ROWS, chip * 256:(chip + 1) * 256] = meta_all[chip]
        head = pltpu.make_async_copy(head_buf, h_o.at[pl.ds(0, HEAD_ROWS), :], loc_sems.at[2])
        head.start()
        head.wait()
        for cp in local:
            cp.wait()

    vm = pl.BlockSpec(memory_space=pltpu.VMEM)
    hbm = pl.BlockSpec(memory_space=pl.ANY)
    return pl.pallas_call(
        body,
        name="gather_weights",
        in_specs=[vm] * 4 + [hbm] * 2,
        out_specs=[hbm] * 5,
        out_shape=[
            jax.ShapeDtypeStruct((D_IN, D), BF16),
            jax.ShapeDtypeStruct((CHIPS, 256, Q_LORA), BF16), jax.ShapeDtypeStruct((CHIPS, KV_LORA, 256), BF16),
            jax.ShapeDtypeStruct((N, D), F32), jax.ShapeDtypeStruct((N, D), F32),
        ],
        scratch_shapes=[pltpu.VMEM((_ARR_ROWS[a], _ARR_COLS[a]), BF16) for a in (0, 2, 3)]
        + [pltpu.VMEM((CHIPS, N_META, 256), F32), pltpu.VMEM((HEAD_ROWS, D), F32), pltpu.VMEM((S, D), F32),
           pltpu.VMEM((S, D), F32)]
        + [pltpu.SemaphoreType.DMA((3 * _NP,))] * 4 + [pltpu.SemaphoreType.DMA((3,)), pltpu.SemaphoreType.DMA((4,))],
        compiler_params=_cparams(),
    )(winT_s, wqT_s, wkv_s, meta_s, x2, tgt2)


_SM_ROWS = (len(POOL_WINDOWS) * POOL_GROUP, VEC_ROWS)
_SM_COLS = (POOL_GROUP, D)
_SM_PIECES = ((0, 0, 256, 0), (0, 256, 256, 1), (1, 0, VEC_ROWS, 0))
_NSP = len(_SM_PIECES)


def _reduce_grads(dwin, dwq, dwkv, dmeta4, dpw, dg, dgf, dgq, dgkv, dps, loss):
    arrays = (0, 2, 3, 4)

    def body(dwin_ref, dwq_ref, dwkv_ref, dmeta_ref, dpw_ref, dg_ref, dgf_ref, dgq_ref, dgkv_ref, dps_ref,
             loss_ref, gwin_o, gwq_o, gwkv_o, gmeta_o, gpw_o, gg_o, ggf_o, ggq_o, ggkv_o, gps_o, gloss_o,
             ow0, ow2, ow3, ow4, sb0, sb2, sb3, sb4, st0, st2, st3, st4, rc0, rc2, rc3, rc4,
             vec, sm_sb0, sm_sb1, sm_cs0, sm_cs1, sm_rc0, sm_rc1, vec_fin,
             own_sems, d2d_send, d2d_recv, ici_send, ici_recv, fin_send, fin_recv,
             swap_send, swap_recv, smi_send, smi_recv, smf_send, smf_recv):
        x, y, c = lax.axis_index("x"), lax.axis_index("y"), lax.axis_index("c")
        me = 2 * x + y
        grads = (dwin_ref, None, dwq_ref, dwkv_ref, dmeta_ref)
        outs = (gwin_o, None, gwq_o, gwkv_o, gmeta_o)
        own_buf = (ow0, None, ow2, ow3, ow4)
        sib_buf = (sb0, None, sb2, sb3, sb4)
        stage = (st0, None, st2, st3, st4)
        recv = (rc0, None, rc2, rc3, rc4)
        sm_mine = (dpw_ref, vec)
        sm_sib = (sm_sb0, sm_sb1)
        sm_chip = (sm_cs0, sm_cs1)
        sm_recv = (sm_rc0, sm_rc1)
        sm_out = (gpw_o, vec_fin)
        sibling = (x, y, 1 - c)

        def chip_of(rel):
            fx, fy = _CHIP_RELS[rel]
            return 2 * (x ^ fx) + (y ^ fy)

        def same_core_of(rel):
            fx, fy = _CHIP_RELS[rel]
            return (x ^ fx, y ^ fy, c)

        def slot(bufs, i, idx):
            arr, _, n, _ = _PIECES[i]
            return bufs[arr].at[idx, pl.ds(0, n), :]

        def own_load(rel, i):
            arr, r0, n, _ = _PIECES[i]
            return pltpu.make_async_copy(_gathered_at(grads, arr, chip_of(rel), r0, n), slot(own_buf, i, rel),
                                         own_sems.at[rel * _NP + i])

        def d2d_copy(rel, i):
            arr, r0, n, _ = _PIECES[i]
            k = rel * _NP + i
            return _remote(_gathered_at(grads, arr, chip_of(rel), r0, n), slot(sib_buf, i, rel),
                           d2d_send.at[k], d2d_recv.at[k], sibling)

        def ici_copy(rel, i):
            k = (rel - 1) * _NP + i
            return _remote(slot(stage, i, rel - 1), slot(recv, i, rel - 1), ici_send.at[k], ici_recv.at[k],
                           same_core_of(rel))

        def fin_copy(i):
            arr, r0, n, _ = _PIECES[i]
            place = outs[arr].at[pl.ds(r0, n), :]
            return _remote(place, place, fin_send.at[i], fin_recv.at[i], sibling)

        def sm_ici_copy(rel, j):
            blk, r0, n, _ = _SM_PIECES[j]
            k = (rel - 1) * _NSP + j
            return _remote(sm_chip[blk].at[pl.ds(r0, n), :], sm_recv[blk].at[rel - 1, pl.ds(r0, n), :],
                           smi_send.at[k], smi_recv.at[k], same_core_of(rel))

        def sm_fin_copy(j):
            blk, r0, n, _ = _SM_PIECES[j]
            place = sm_out[blk].at[pl.ds(r0, n), :]
            return _remote(place, place, smf_send.at[j], smf_recv.at[j], sibling)

        vec[...] = jnp.zeros_like(vec)
        vec[0:1, :] = dg_ref[...]
        vec[1:2, :] = dgf_ref[...]
        vec[2:3, V_GQ:V_GQ + Q_LORA] = dgq_ref[...]
        vec[2:3, V_GKV:V_GKV + KV_LORA] = dgkv_ref[...]
        vec[2:3, V_PS:V_PS + D_POOL] = dps_ref[...]
        vec[2:3, V_LOSS:D] = loss_ref[...]
        swaps = [_remote(sm_mine[b], sm_sib[b], swap_send.at[b], swap_recv.at[b], sibling) for b in (0, 1)]
        for cp in swaps:
            cp.start()

        for core in (0, 1):
            @pl.when(c == core)
            def _(core=core):
                mine = [i for i in range(_NP) if _PIECES[i][3] == core and _PIECES[i][0] in arrays]
                theirs = [i for i in range(_NP) if _PIECES[i][3] != core and _PIECES[i][0] in arrays]
                sm_mine_p = [j for j in range(_NSP) if _SM_PIECES[j][3] == core]
                sm_theirs_p = [j for j in range(_NSP) if _SM_PIECES[j][3] != core]
                sends = list(swaps)

                for rel in (1, 2, 3, 0):
                    for i in theirs:
                        cp = d2d_copy(rel, i)
                        cp.start()
                        sends.append(cp)
                    for i in mine:
                        own_load(rel, i).start()

                for b in (0, 1):
                    swaps[b].wait_recv()
                    sm_chip[b][...] = sm_mine[b][...] + sm_sib[b][...]
                for rel in (1, 2, 3):
                    for j in sm_mine_p:
                        cp = sm_ici_copy(rel, j)
                        cp.start()
                        sends.append(cp)

                for rel in (1, 2, 3):
                    for i in mine:
                        arr, r0, n, _ = _PIECES[i]
                        own_load(rel, i).wait()
                        d2d_copy(rel, i).wait_recv()
                        total = slot(own_buf, i, rel)[...] + slot(sib_buf, i, rel)[...]
                        slot(stage, i, rel - 1)[...] = total.astype(stage[arr].dtype)
                        cp = ici_copy(rel, i)
                        cp.start()
                        sends.append(cp)

                for i in mine:
                    arr, r0, n, _ = _PIECES[i]
                    own_load(0, i).wait()
                    d2d_copy(0, i).wait_recv()
                    total = slot(own_buf, i, 0)[...] + slot(sib_buf, i, 0)[...]
                    for rel in (1, 2, 3):
                        ici_copy(rel, i).wait_recv()
                        total = total + slot(recv, i, rel - 1)[...].astype(F32)
                    outs[arr][pl.ds(r0, n), :] = total
                    cp = fin_copy(i)
                    cp.start()
                    sends.append(cp)

                for j in sm_mine_p:
                    blk, r0, n, _ = _SM_PIECES[j]
                    for rel in (1, 2, 3):
                        sm_ici_copy(rel, j).wait_recv()
                    total = jnp.zeros((n, _SM_COLS[blk]), F32)
                    for chip in range(CHIPS):
                        flips = chip ^ me
                        rel = jnp.where(flips == 2, 1, jnp.where(flips == 1, 2, flips))
                        theirs_rows = sm_recv[blk][jnp.maximum(rel - 1, 0), pl.ds(r0, n), :]
                        total = total + jnp.where(rel == 0, sm_chip[blk][pl.ds(r0, n), :], theirs_rows)
                    sm_out[blk][pl.ds(r0, n), :] = total
                    cp = sm_fin_copy(j)
                    cp.start()
                    sends.append(cp)

                for i in theirs:
                    fin_copy(i).wait_recv()
                for j in sm_theirs_p:
                    sm_fin_copy(j).wait_recv()
                for cp in sends:
                    cp.wait_send()

        gg_o[...] = vec_fin[0:1, :]
        ggf_o[...] = vec_fin[1:2, :]
        ggq_o[...] = vec_fin[2:3, V_GQ:V_GQ + Q_LORA]
        ggkv_o[...] = vec_fin[2:3, V_GKV:V_GKV + KV_LORA]
        gps_o[...] = vec_fin[2:3, V_PS:V_PS + D_POOL]
        gloss_o[...] = vec_fin[2:3, V_LOSS:D]

    vm = pl.BlockSpec(memory_space=pltpu.VMEM)
    piece_buf = lambda lead, dtype: [pltpu.VMEM((lead, _PIECE_MAX[a], _ARR_COLS[a]), F32 if a == 4 else dtype)
                                     for a in arrays]
    sm_buf = lambda *lead: [pltpu.VMEM(lead + (_SM_ROWS[b], _SM_COLS[b]), F32) for b in (0, 1)]
    dma = lambda n: [pltpu.SemaphoreType.DMA((n,))] * 2
    return pl.pallas_call(
        body,
        name="reduce_grads",
        in_specs=[pl.BlockSpec(memory_space=pl.ANY)] * 3 + [vm] * 8,
        out_specs=[vm] * 11,
        out_shape=[jax.ShapeDtypeStruct((_ARR_ROWS[a], _ARR_COLS[a]), F32) for a in arrays]
        + [jax.ShapeDtypeStruct((_SM_ROWS[0], _SM_COLS[0]), F32), jax.ShapeDtypeStruct((1, D), F32),
           jax.ShapeDtypeStruct((1, D), F32), jax.ShapeDtypeStruct((1, Q_LORA), F32),
           jax.ShapeDtypeStruct((1, KV_LORA), F32), jax.ShapeDtypeStruct((1, D_POOL), F32),
           jax.ShapeDtypeStruct((1, 128), F32)],
        scratch_shapes=piece_buf(CHIPS, F32) + piece_buf(CHIPS, F32) + piece_buf(3, BF16) + piece_buf(3, BF16)
        + [pltpu.VMEM((VEC_ROWS, D), F32)] + sm_buf() + sm_buf() + sm_buf(3) + [pltpu.VMEM((VEC_ROWS, D), F32)]
        + [pltpu.SemaphoreType.DMA((CHIPS * _NP,))]
        + dma(CHIPS * _NP) + dma(3 * _NP) + dma(_NP) + dma(2) + dma(3 * _NSP) + dma(_NSP),
        compiler_params=_cparams(),
    )(dwin, dwq, dwkv, dmeta4, dpw, dg, dgf, dgq, dgkv, dps, loss)


def _adamw_math(w, g, m, v):
    m = B1 * m + (1.0 - B1) * g
    v = B2 * v + (1.0 - B2) * (g * g)
    m_hat = m / C1
    v_hat = v / C2
    delta = -LR * (m_hat / (jnp.sqrt(v_hat) + ADAM_EPS) + WD * w)
    return delta, m, v


def _adamw_rows(name, w, g, m, v, block_rows):
    rows, cols = w.shape

    def body(w_ref, g_ref, m_ref, v_ref, go_ref, d_ref, nm_ref, nv_ref):
        g = g_ref[...]
        go_ref[...] = g
        d_ref[...], nm_ref[...], nv_ref[...] = _adamw_math(w_ref[...], g, m_ref[...], v_ref[...])

    spec = pl.BlockSpec((block_rows, cols), lambda i: (i, 0))
    return pl.pallas_call(
        body,
        name=name,
        grid=(rows // block_rows,),
        in_specs=[spec] * 4,
        out_specs=[spec] * 4,
        out_shape=[jax.ShapeDtypeStruct(w.shape, F32)] * 4,
        compiler_params=_cparams(dimension_semantics=("arbitrary",)),
    )(w, g, m, v)


def _adamw_small(groups):
    n = len(groups)

    def body(*refs):
        ins, outs = refs[:4 * n], refs[4 * n:]
        for t in range(n):
            w_ref, g_ref, m_ref, v_ref = ins[4 * t:4 * t + 4]
            g = g_ref[0:w_ref.shape[0], :]
            outs[4 * t][...] = g
            outs[4 * t + 1][...], outs[4 * t + 2][...], outs[4 * t + 3][...] = _adamw_math(
                w_ref[...], g, m_ref[...], v_ref[...])

    vm = pl.BlockSpec(memory_space=pltpu.VMEM)
    flat = [a for grp in groups for a in grp]
    outs = pl.pallas_call(
        body,
        name="adamw_small",
        in_specs=[vm] * (4 * n),
        out_specs=[vm] * (4 * n),
        out_shape=[jax.ShapeDtypeStruct(grp[0].shape, F32) for grp in groups for _ in range(4)],
        compiler_params=_cparams(),
    )(*flat)
    return [tuple(outs[4 * t:4 * t + 4]) for t in range(n)]


def _rope_tables():
    half = QK_ROPE // 2
    f32 = np.float32
    inv_freq = (f32(1.0) / (f32(ROPE_THETA) ** (np.arange(half, dtype=f32) / f32(half)))).astype(f32)
    pos = np.arange(N, dtype=f32) - f32(PAD)
    ang = (pos[:, None] * inv_freq[None, :]).astype(f32)
    cos, sin = np.cos(ang).astype(f32), np.sin(ang).astype(f32)
    zero = np.zeros((N, 128 - QK_ROPE), f32)
    return jnp.asarray(np.concatenate([cos, cos, zero], axis=1)), jnp.asarray(np.concatenate([-sin, sin, zero], axis=1))


def kernel(x, meta_tokens, norm_g, w_in, q_norm_g, w_q_b, kv_norm_g, w_kv_b, pool_w, pool_scale, w_out, final_norm_g, loss_target, m_meta_tokens, m_norm_g, m_w_in, m_q_norm_g, m_w_q_b, m_kv_norm_g, m_w_kv_b, m_pool_w, m_pool_scale, m_w_out, m_final_norm_g, v_meta_tokens, v_norm_g, v_w_in, v_q_norm_g, v_w_q_b, v_kv_norm_g, v_w_kv_b, v_pool_w, v_pool_scale, v_w_out, v_final_norm_g):
    tr = lambda a: a[0].T
    win, wq, wkv, h, tgt = _gather_weights(tr(w_in), tr(w_q_b), w_kv_b[0], meta_tokens, x[0], loss_target[0])
    cosf, sinf = _rope_tables()
    gf = final_norm_g.reshape(1, D)

    part = _local_step(h, tgt, norm_g, win, q_norm_g, wq, kv_norm_g, wkv, pool_w[0], pool_scale, w_out[0], m_w_out[0],
                       v_w_out[0], gf, cosf, sinf)

    pw2 = lambda a: a.reshape(len(POOL_WINDOWS) * POOL_GROUP, POOL_GROUP)
    gwinT, gwqT, gwkv, gmeta, gpw, gg, ggf, ggq, ggkv, gps, gloss = _reduce_grads(
        part["dwin"], part["dwq"], part["dwkv"], part["dmeta"], pw2(part["dpw"]), part["dg"],
        part["dgf"], part["dgq"], part["dgkv"], part["dps"], part["loss"])

    r_in = _adamw_rows("adamw_w_in", tr(w_in), gwinT, tr(m_w_in), tr(v_w_in), 248)
    r_out = part["r_out"]
    fn2 = lambda a: a.reshape(1, D)
    r_meta, r_norm, r_gq, r_wq, r_gkv, r_wkv, r_pw, r_ps, r_fn = _adamw_small([
        (meta_tokens, gmeta, m_meta_tokens, v_meta_tokens),
        (norm_g, gg, m_norm_g, v_norm_g),
        (q_norm_g, ggq, m_q_norm_g, v_q_norm_g),
        (tr(w_q_b), gwqT, tr(m_w_q_b), tr(v_w_q_b)),
        (kv_norm_g, ggkv, m_kv_norm_g, v_kv_norm_g),
        (w_kv_b[0], gwkv, m_w_kv_b[0], v_w_kv_b[0]),
        (pw2(pool_w), gpw, pw2(m_pool_w), pw2(v_pool_w)),
        (pool_scale, gps, m_pool_scale, v_pool_scale),
        (fn2(final_norm_g), ggf, fn2(m_final_norm_g), fn2(v_final_norm_g)),
    ])
    untr = lambda a: a.T[None]
    pw4 = lambda a: a.reshape(1, len(POOL_WINDOWS), POOL_GROUP, POOL_GROUP)
    per_kind = [[
        r_meta[kind], r_norm[kind], untr(r_in[kind]), r_gq[kind], untr(r_wq[kind]), r_gkv[kind], r_wkv[kind][None],
        pw4(r_pw[kind]), r_ps[kind], r_out[kind][None], r_fn[kind].reshape(D),
    ] for kind in range(4)]
    return (gloss[0, 0], part["gx"][None], *per_kind[0], *per_kind[1], *per_kind[2], *per_kind[3])
```

```python
import jax
import jax.numpy as jnp
import numpy as np
from jax import lax
from jax.experimental import pallas as pl
from jax.experimental.pallas import tpu as pltpu

F32 = jnp.float32
BF16 = jnp.bfloat16

D = 1024
S = 2048
N_META = 16
PAD = 112
HEAD_ROWS = PAD + N_META
N = HEAD_ROWS + S
D_POOL = 512
POOL_WINDOWS = (2, 4, 8, 16)
POOL_GROUP = 128
HALO = 16
HEADS = 4
QK_NOPE = 128
QK_ROPE = 64
QK = QK_NOPE + QK_ROPE
V_HEAD = 128
Q_LORA = 256
KV_LORA = 128
D_IN = 1984
EPS = 1e-6
ROPE_THETA = 10000.0
SCALE = QK ** -0.5
CHIPS = 4

ROWS_FWD = 544
ROWS_MID = 544
ROWS_BWD = 544
TK = 128
TQ = 256
NQ = S // TQ
HEADS_PER_STEP_BWD = 2

O_PI, O_PG, O_CQ, O_CKV, O_KR, O_AG = 0, 512, 1024, 1280, 1408, 1472
O_KR_END = O_KR + 128
SHARD_IN = D_IN // CHIPS
SHARD_OUT = D // CHIPS

LR, B1, B2, ADAM_EPS, WD, STEP = 0.001, 0.9, 0.999, 1e-08, 0.01, 10
C1 = 1.0 - B1**STEP
C2 = 1.0 - B2**STEP

VMEM_LIMIT = 60 * 1024 * 1024
MESH = pl.DeviceIdType.MESH
NEG = -1e30

VEC_ROWS = 8
V_GQ, V_GKV, V_PS, V_LOSS = 0, 256, 384, 896


def _cparams(**kw):
    return pltpu.CompilerParams(vmem_limit_bytes=VMEM_LIMIT, **kw)


def _nt(a, b):
    return lax.dot_general(a, b, (((1,), (1,)), ((), ())), preferred_element_type=F32)


def _tn(a, b):
    return lax.dot_general(a, b, (((0,), (0,)), ((), ())), preferred_element_type=F32)


def _nn(a, b):
    return jnp.dot(a, b, preferred_element_type=F32)


def _swap64(t):
    return pltpu.roll(t, 32, 1) + pltpu.roll(t, 96, 1)


def _sigmoid(x):
    return 1.0 / (1.0 + jnp.exp(-x))


def _low_lanes():
    return (lax.broadcasted_iota(jnp.int32, (1, 128), 1) < QK_ROPE).astype(F32)


def _rows(w, rows):
    return pl.BlockSpec((rows, w), lambda i: (i, 0))


def _const(*shape):
    return pl.BlockSpec(shape, lambda *_: (0,) * len(shape), pipeline_mode=pl.Buffered(1))


STAT_GROUPS = HEADS // HEADS_PER_STEP_BWD


def _stat_slot(head):
    return head // HEADS_PER_STEP_BWD, head % HEADS_PER_STEP_BWD


def _peer_barrier(x, y, c):
    barrier = pltpu.get_barrier_semaphore()
    peers = [(x, y, 1 - c)] + [(x ^ fx, y ^ fy, c) for fx, fy in _CHIP_RELS[1:]]
    for peer in peers:
        pl.semaphore_signal(barrier, inc=1, device_id=peer, device_id_type=MESH)
    pl.semaphore_wait(barrier, len(peers))


def _attn_tiles():
    return [(0, TK, TK)] + [(TK + TQ * t, TQ, TK + TQ * (t + 1)) for t in range(NQ)]


def _masked_scores(q, k, rows, klen):
    s = _nt(q, k)
    col = lax.broadcasted_iota(jnp.int32, (1, TK), 1)
    head_bias = jnp.where(col >= PAD, 0.0, NEG)
    if klen == TK:
        return s + head_bias
    r = lax.broadcasted_iota(jnp.int32, (rows, 1), 0) >> 6
    c = lax.broadcasted_iota(jnp.int32, (1, rows), 1) >> 6
    diag_bias = jnp.where(c <= r, 0.0, NEG)
    parts = [s[:, 0:TK] + head_bias]
    if klen - rows > TK:
        parts.append(s[:, TK:klen - rows])
    parts.append(s[:, klen - rows:klen] + diag_bias)
    return jnp.concatenate(parts, axis=1)


def _fwd_in(h, norm_g, win, gq, wq, gkv, wkv, cosf, sinf):
    tr = ROWS_FWD

    def body(h_ref, g_ref, win_ref, gq_ref, wq_ref, gkv_ref, wkv_ref, cos_ref, sin_ref,
             pi_ref, pg_ref, cq_ref, ckv_ref, ag_ref, q_ref, k_ref, v_ref):
        h = h_ref[...]
        r = lax.rsqrt(jnp.mean(h * h, axis=-1, keepdims=True) + EPS)
        hn = ((h * r) * g_ref[...]).astype(BF16)
        u = _nt(hn, win_ref[0:O_KR_END, :])
        pi_ref[...] = u[:, O_PI:O_PG]
        pg_ref[...] = u[:, O_PG:O_CQ]
        cq = u[:, O_CQ:O_CKV]
        ckv = u[:, O_CKV:O_KR]
        cq_ref[...] = cq
        ckv_ref[...] = ckv
        ag_ref[...] = _nt(hn, win_ref[O_AG:D_IN, :])
        cosv = cos_ref[...]
        sinv = sin_ref[...]
        kr = u[:, O_KR:O_KR_END] * _low_lanes()
        kr = (kr * cosv + _swap64(kr) * sinv).astype(BF16)
        rq = lax.rsqrt(jnp.mean(cq * cq, axis=-1, keepdims=True) + EPS)
        cqn = ((cq * rq) * gq_ref[...]).astype(BF16)
        rkv = lax.rsqrt(jnp.mean(ckv * ckv, axis=-1, keepdims=True) + EPS)
        ckvn = ((ckv * rkv) * gkv_ref[...]).astype(BF16)
        for hd in range(HEADS):
            qh = _nt(cqn, wq_ref[hd]) * SCALE
            z = qh[:, QK_NOPE:]
            q_ref[hd, :, 0:QK_NOPE] = qh[:, 0:QK_NOPE].astype(BF16)
            q_ref[hd, :, QK_NOPE:] = (z * cosv + _swap64(z) * sinv).astype(BF16)
            kvh = _nn(ckvn, wkv_ref[hd])
            k_ref[hd, :, 0:QK_NOPE] = kvh[:, 0:QK_NOPE].astype(BF16)
            k_ref[hd, :, QK_NOPE:] = kr
            v_ref[hd] = kvh[:, QK_NOPE:].astype(BF16)

    head = lambda w: pl.BlockSpec((HEADS, tr, w), lambda i: (0, i, 0))
    return pl.pallas_call(
        body,
        name="fwd_in",
        grid=(N // tr,),
        in_specs=[
            _rows(D, tr), _const(1, D), _const(D_IN, D), _const(1, Q_LORA), _const(HEADS, 256, Q_LORA),
            _const(1, KV_LORA), _const(HEADS, KV_LORA, 256), _rows(128, tr), _rows(128, tr),
        ],
        out_specs=[_rows(D_POOL, tr), _rows(D_POOL, tr), _rows(Q_LORA, tr), _rows(KV_LORA, tr), _rows(D_POOL, tr),
                   head(256), head(256), head(V_HEAD)],
        out_shape=[
            jax.ShapeDtypeStruct((N, D_POOL), F32), jax.ShapeDtypeStruct((N, D_POOL), F32),
            jax.ShapeDtypeStruct((N, Q_LORA), F32), jax.ShapeDtypeStruct((N, KV_LORA), F32),
            jax.ShapeDtypeStruct((N, D_POOL), F32),
            jax.ShapeDtypeStruct((HEADS, N, 256), BF16), jax.ShapeDtypeStruct((HEADS, N, 256), BF16),
            jax.ShapeDtypeStruct((HEADS, N, V_HEAD), BF16),
        ],
        compiler_params=_cparams(dimension_semantics=("arbitrary",)),
    )(h, norm_g, win, gq, wq, gkv, wkv, cosf, sinf)


def _attn_fwd(q, k, v, wout_s):
    tiles = _attn_tiles()
    n_t = len(tiles)
    half = SHARD_OUT // 2
    fwd_step = n_t - 2

    def body(q_hbm, k_hbm, v_hbm, wout_ref, o_hbm, lse_ref, wout_o, q_buf, k_buf, v_buf, o_buf, s_wout, in_sems, out_sems,
             ici_send, ici_recv, fwd_send, fwd_recv, own_sem):
        step = pl.program_id(0)
        x, y, c = lax.axis_index("x"), lax.axis_index("y"), lax.axis_index("c")
        me = 2 * x + y

        def chip_of(rel):
            fx, fy = _CHIP_RELS[rel]
            return 2 * (x ^ fx) + (y ^ fy)

        def place(chip, core):
            return wout_o.at[pl.ds(pl.multiple_of(SHARD_OUT * chip + half * core, half), half), :]

        def ici_copy(rel, src_chip, to):
            return _remote(s_wout.at[pl.ds(pl.multiple_of(half * c, half), half), :], place(src_chip, c),
                           ici_send.at[rel - 1], ici_recv.at[rel - 1], to)

        def fwd_copy(rel, core, to):
            spot = place(chip_of(rel), core)
            return _remote(spot, spot, fwd_send.at[rel - 1], fwd_recv.at[rel - 1], to)

        own = pltpu.make_async_copy(s_wout, wout_o.at[pl.ds(pl.multiple_of(SHARD_OUT * me, SHARD_OUT), SHARD_OUT), :], own_sem)

        @pl.when(step == 0)
        def _():
            _peer_barrier(x, y, c)
            s_wout[...] = wout_ref[...].astype(BF16)
            own.start()
            for rel in (1, 2, 3):
                fx, fy = _CHIP_RELS[rel]
                ici_copy(rel, me, (x ^ fx, y ^ fy, c)).start()

        @pl.when(step == fwd_step)
        def _():
            for rel in (1, 2, 3):
                ici_copy(rel, chip_of(rel), (x, y, c)).wait_recv()
                fwd_copy(rel, c, (x, y, 1 - c)).start()

        def finish_wout():
            for rel in (1, 2, 3):
                fwd_copy(rel, 1 - c, (x, y, c)).wait_recv()
            for rel in (1, 2, 3):
                ici_copy(rel, me, (x, y, c)).wait_send()
                fwd_copy(rel, c, (x, y, c)).wait_send()
            own.wait()

        def loads(idx):
            q0, rows, _ = tiles[idx]
            rs = pl.ds(q0, rows)
            return [pltpu.make_async_copy(src.at[:, rs, :], dst.at[:, rs, :], in_sems.at[a, idx % 2])
                    for a, (src, dst) in enumerate(((q_hbm, q_buf), (k_hbm, k_buf), (v_hbm, v_buf)))]

        def store(idx):
            q0, rows, _ = tiles[idx]
            return pltpu.make_async_copy(o_buf.at[idx % 2, pl.ds(0, rows), :], o_hbm.at[pl.ds(q0, rows), :],
                                         out_sems.at[idx % 2])

        @pl.when(step == 0)
        def _():
            lse_ref[...] = jnp.zeros_like(lse_ref)
            for cp in loads(0):
                cp.start()

        for idx, (q0, rows, klen) in enumerate(tiles):
            @pl.when(step == idx)
            def _(idx=idx, q0=q0, rows=rows, klen=klen):
                for cp in loads(idx):
                    cp.wait()
                if idx + 1 < n_t:
                    for cp in loads(idx + 1):
                        cp.start()
                if idx >= 2:
                    store(idx - 2).wait()
                for hd in range(HEADS):
                    s = _masked_scores(q_buf[hd, q0:q0 + rows, :], k_buf[hd, 0:klen, :], rows, klen)
                    m = jnp.max(s, axis=-1, keepdims=True)
                    p = jnp.exp(s - m)
                    l = jnp.sum(p, axis=-1, keepdims=True)
                    o_buf[idx % 2, 0:rows, hd * V_HEAD:(hd + 1) * V_HEAD] = _nn(p.astype(BF16), v_buf[hd, 0:klen, :]) / l
                    grp, lane = _stat_slot(hd)
                    lse_ref[grp, q0:q0 + rows, lane:lane + 1] = m + jnp.log(l)
                store(idx).start()
                if idx == n_t - 1:
                    store(idx - 1).wait()
                    store(idx).wait()
                    finish_wout()

    hbm = pl.BlockSpec(memory_space=pl.ANY)
    return pl.pallas_call(
        body,
        name="attn_fwd",
        grid=(n_t,),
        in_specs=[hbm, hbm, hbm, _const(SHARD_OUT, D)],
        out_specs=[hbm, _const(STAT_GROUPS, N, 128), hbm],
        out_shape=[jax.ShapeDtypeStruct((N, HEADS * V_HEAD), F32), jax.ShapeDtypeStruct((STAT_GROUPS, N, 128), F32),
                   jax.ShapeDtypeStruct((D, D), BF16)],
        scratch_shapes=[pltpu.VMEM((HEADS, N, 256), BF16), pltpu.VMEM((HEADS, N, 256), BF16),
                        pltpu.VMEM((HEADS, N, V_HEAD), BF16), pltpu.VMEM((2, TQ, HEADS * V_HEAD), F32),
                        pltpu.VMEM((SHARD_OUT, D), BF16),
                        pltpu.SemaphoreType.DMA((3, 2)), pltpu.SemaphoreType.DMA((2,))]
        + [pltpu.SemaphoreType.DMA((3,))] * 4 + [pltpu.SemaphoreType.DMA],
        compiler_params=_cparams(dimension_semantics=("arbitrary",), collective_id=1),
    )(q, k, v, wout_s)


def _inv_count(row0, rows, w):
    row = row0 + lax.broadcasted_iota(jnp.int32, (rows, 1), 0)
    return 1.0 / jnp.clip(row - (PAD - 1), 1, w).astype(F32)


def _mid(h, tgt, pool_in, pool_gate, attn_gate, attn, pool_w, pool_scale, wout, gf):
    tr = ROWS_MID
    per = tr // HALO
    ng = len(POOL_WINDOWS)

    def body(h_ref, t_ref, pin_ref, halo_ref, pg_ref, ag_ref, at_ref, pw_ref, ps_ref, wout_ref, gf_ref,
             dh2_ref, do_ref, delta_ref, dag_ref, dpg_ref, dpl_ref, dwout_ref, dpw_ref, dps_ref, dgf_ref, loss_ref):
        i = pl.program_id(0)

        @pl.when(i == 0)
        def _():
            dwout_ref[...] = jnp.zeros_like(dwout_ref)
            dpw_ref[...] = jnp.zeros_like(dpw_ref)
            dps_ref[...] = jnp.zeros_like(dps_ref)
            dgf_ref[...] = jnp.zeros_like(dgf_ref)
            loss_ref[...] = jnp.zeros_like(loss_ref)

        row0 = i * tr
        real = (row0 + lax.broadcasted_iota(jnp.int32, (tr, 1), 0)) >= HEAD_ROWS
        h = h_ref[...]

        halo = jnp.where(i > 0, halo_ref[...], 0.0)
        ext = jnp.concatenate([halo, pin_ref[...]], axis=0)
        pooled = []
        for g, w in enumerate(POOL_WINDOWS):
            e = ext[:, g * POOL_GROUP:(g + 1) * POOL_GROUP]
            acc = e
            shift = 1
            while shift < w:
                acc = acc + pltpu.roll(acc, shift, 0)
                shift *= 2
            pooled.append((acc[HALO:] * _inv_count(row0, tr, w) - e[HALO:]).astype(BF16))
        pw = [pw_ref[g].astype(BF16) for g in range(ng)]
        mixed = jnp.concatenate([_nn(pooled[g], pw[g]) for g in range(ng)], axis=1)
        ps = ps_ref[...]
        mixed_s = mixed * ps
        pg = pg_ref[...]
        sig_p = _sigmoid(pg)
        silu_p = pg * sig_p
        pool_out = (silu_p * mixed_s).astype(BF16)
        ag = ag_ref[...]
        sig_a = _sigmoid(ag)
        silu_a = ag * sig_a
        at = at_ref[...]
        attn_out = (silu_a * at).astype(BF16)
        mix = _nn(pool_out, wout_ref[0:D_POOL, :]) + _nn(attn_out, wout_ref[D_POOL:D, :])
        h2 = h + mix

        r2 = lax.rsqrt(jnp.mean(h2 * h2, axis=-1, keepdims=True) + EPS)
        n2 = h2 * r2
        gfv = gf_ref[...]
        err = jnp.where(real, n2 * gfv - t_ref[...], 0.0)
        loss_ref[...] += jnp.sum(jnp.sum(err * err, axis=-1, keepdims=True), axis=0, keepdims=True) * (0.5 / D)
        dy = err * (1.0 / D)
        dgf_ref[...] += jnp.sum(dy * n2, axis=0, keepdims=True)
        dn = dy * gfv
        dh2 = r2 * (dn - n2 * jnp.mean(dn * n2, axis=-1, keepdims=True))
        dh2_ref[...] = dh2
        dh2b = dh2.astype(BF16)

        dwout_ref[0:D_POOL, :] += _tn(pool_out, dh2b)
        dwout_ref[D_POOL:D, :] += _tn(attn_out, dh2b)
        dcat = _nt(dh2b, wout_ref[...])
        dpo = dcat[:, 0:D_POOL]
        dao = dcat[:, D_POOL:D]
        do = dao * silu_a
        prod = do * at
        delta_ref[...] = jnp.zeros_like(delta_ref)
        for hd in range(HEADS):
            grp, lane = _stat_slot(hd)
            cols = slice(hd * V_HEAD, (hd + 1) * V_HEAD)
            do_ref[grp, :, lane * V_HEAD:(lane + 1) * V_HEAD] = do[:, cols].astype(BF16)
            delta_ref[grp, :, lane:lane + 1] = jnp.sum(prod[:, cols], axis=-1, keepdims=True)
        dag_ref[...] = (dao * at * (sig_a * (1.0 + ag * (1.0 - sig_a)))).astype(BF16)
        dmixed_s = dpo * silu_p
        dpg_ref[...] = (dpo * mixed_s * (sig_p * (1.0 + pg * (1.0 - sig_p)))).astype(BF16)
        dps_ref[...] += jnp.sum(dmixed_s * mixed, axis=0, keepdims=True)
        dmixed = (dmixed_s * ps).astype(BF16)
        dpl = []
        for g in range(ng):
            dm = dmixed[:, g * POOL_GROUP:(g + 1) * POOL_GROUP]
            dpl.append(_nt(dm, pw[g]))
            dpw_ref[g] += _tn(pooled[g], dm)
        dpl_ref[...] = jnp.concatenate(dpl, axis=1)

    halo_spec = pl.BlockSpec((HALO, D_POOL), lambda i: (jnp.maximum(i * per - 1, 0), 0))
    return pl.pallas_call(
        body,
        name="mid",
        grid=(N // tr,),
        in_specs=[
            _rows(D, tr), _rows(D, tr), _rows(D_POOL, tr), halo_spec, _rows(D_POOL, tr), _rows(D_POOL, tr),
            _rows(D_POOL, tr), _const(ng, POOL_GROUP, POOL_GROUP), _const(1, D_POOL), _const(D, D), _const(1, D),
        ],
        out_specs=[
            _rows(D, tr), pl.BlockSpec((STAT_GROUPS, tr, HEADS_PER_STEP_BWD * V_HEAD), lambda i: (0, i, 0)),
            pl.BlockSpec((STAT_GROUPS, tr, 128), lambda i: (0, i, 0)),
            _rows(D_POOL, tr), _rows(D_POOL, tr), _rows(D_POOL, tr),
            _const(D, D), _const(ng, POOL_GROUP, POOL_GROUP), _const(1, D_POOL), _const(1, D), _const(1, 128),
        ],
        out_shape=[
            jax.ShapeDtypeStruct((N, D), F32), jax.ShapeDtypeStruct((STAT_GROUPS, N, HEADS_PER_STEP_BWD * V_HEAD), BF16),
            jax.ShapeDtypeStruct((STAT_GROUPS, N, 128), F32),
            jax.ShapeDtypeStruct((N, D_POOL), BF16), jax.ShapeDtypeStruct((N, D_POOL), BF16),
            jax.ShapeDtypeStruct((N, D_POOL), F32), jax.ShapeDtypeStruct((D, D), F32),
            jax.ShapeDtypeStruct((ng, POOL_GROUP, POOL_GROUP), F32),
            jax.ShapeDtypeStruct((1, D_POOL), F32), jax.ShapeDtypeStruct((1, D), F32), jax.ShapeDtypeStruct((1, 128), F32),
        ],
        compiler_params=_cparams(dimension_semantics=("arbitrary",)),
    )(h, tgt, pool_in, pool_in, pool_gate, attn_gate, attn, pool_w, pool_scale, wout, gf)


def _unrope(dy, cosv, sinv):
    return dy * cosv + _swap64(dy * sinv) * _low_lanes()


def _attn_bwd(q, k, v, do, lse, delta, cosf, sinf, dwout):
    tiles = _attn_tiles()
    hp = HEADS_PER_STEP_BWD
    n_g = HEADS // hp
    n_t = len(tiles)
    half = SHARD_OUT // 2
    send_at, sum_at = (0, 2), (n_g - 1, n_t // 2)

    def body(q_hbm, k_hbm, v_hbm, do_hbm, lse_ref, delta_ref, cos_ref, sin_ref, dwout_hbm, dq_hbm, dkv_ref, dkr_ref,
             gwout_ref, q_buf, k_buf, v_buf, do_buf, dq_buf, dk_acc, dv_acc, own_w, sib_w, stage_w, recv_w, gw_buf,
             in_sems, out_sems, ow_sems, d2d_send, d2d_recv, ici_send, ici_recv, fin_send, fin_recv):
        grp = pl.program_id(0)
        step = pl.program_id(1)
        heads = pl.ds(grp * hp, hp)
        x, y, c = lax.axis_index("x"), lax.axis_index("y"), lax.axis_index("c")
        sibling = (x, y, 1 - c)

        def chip_of(rel):
            fx, fy = _CHIP_RELS[rel]
            return 2 * (x ^ fx) + (y ^ fy)

        def piece(chip, core):
            return dwout_hbm.at[pl.ds(pl.multiple_of(SHARD_OUT * chip + half * core, half), half), :]

        def own_load(rel):
            return pltpu.make_async_copy(piece(chip_of(rel), c), own_w.at[rel], ow_sems.at[rel])

        def d2d_copy(rel):
            return _remote(piece(chip_of(rel), 1 - c), sib_w.at[rel], d2d_send.at[rel], d2d_recv.at[rel], sibling)

        def ici_copy(rel):
            fx, fy = _CHIP_RELS[rel]
            return _remote(stage_w.at[rel - 1], recv_w.at[rel - 1], ici_send.at[rel - 1], ici_recv.at[rel - 1],
                           (x ^ fx, y ^ fy, c))

        def fin_copy(core):
            spot = gw_buf.at[pl.ds(pl.multiple_of(half * core, half), half), :]
            return _remote(spot, spot, fin_send.at[0], fin_recv.at[0], sibling)

        @pl.when((grp == 0) & (step == 0))
        def _():
            _peer_barrier(x, y, c)
            for rel in (1, 2, 3, 0):
                d2d_copy(rel).start()
                own_load(rel).start()

        @pl.when((grp == send_at[0]) & (step == send_at[1]))
        def _():
            for rel in (1, 2, 3):
                own_load(rel).wait()
                d2d_copy(rel).wait_recv()
                stage_w[rel - 1] = (own_w[rel] + sib_w[rel]).astype(BF16)
                ici_copy(rel).start()

        @pl.when((grp == sum_at[0]) & (step == sum_at[1]))
        def _():
            own_load(0).wait()
            d2d_copy(0).wait_recv()
            total = own_w[0] + sib_w[0]
            for rel in (1, 2, 3):
                ici_copy(rel).wait_recv()
                total = total + recv_w[rel - 1].astype(F32)
            gw_buf[pl.ds(pl.multiple_of(half * c, half), half), :] = total
            fin_copy(c).start()

        def finish_dwout():
            fin_copy(1 - c).wait_recv()
            for rel in (0, 1, 2, 3):
                d2d_copy(rel).wait_send()
            for rel in (1, 2, 3):
                ici_copy(rel).wait_send()
            fin_copy(c).wait_send()
            gwout_ref[...] = gw_buf[...]

        def loads(g, idx):
            q0, rows, _ = tiles[idx]
            rs = pl.ds(q0, rows)
            par = (g * n_t + idx) % 2
            hs = pl.ds(g * hp, hp)
            pairs = ((q_hbm.at[hs, rs, :], q_buf.at[:, rs, :]), (k_hbm.at[hs, rs, :], k_buf.at[:, rs, :]),
                     (v_hbm.at[hs, rs, :], v_buf.at[:, rs, :]), (do_hbm.at[g, rs, :], do_buf.at[rs, :]))
            return [pltpu.make_async_copy(src, dst, in_sems.at[a, par]) for a, (src, dst) in enumerate(pairs)]

        def store(idx):
            q0, rows, _ = tiles[idx]
            return pltpu.make_async_copy(dq_buf.at[idx % 2, :, pl.ds(0, rows), :], dq_hbm.at[heads, pl.ds(q0, rows), :],
                                         out_sems.at[idx % 2])

        @pl.when(step == 0)
        def _():
            dk_acc[...] = jnp.zeros_like(dk_acc)
            dv_acc[...] = jnp.zeros_like(dv_acc)

        @pl.when((step == 0) & (grp == 0))
        def _():
            dkr_ref[...] = jnp.zeros_like(dkr_ref)
            for cp in loads(grp, 0):
                cp.start()

        for idx, (q0, rows, klen) in enumerate(tiles):
            @pl.when(step == idx)
            def _(idx=idx, q0=q0, rows=rows, klen=klen):
                for cp in loads(grp, idx):
                    cp.wait()
                if idx + 1 < n_t:
                    for cp in loads(grp, idx + 1):
                        cp.start()
                if idx >= 2:
                    store(idx - 2).wait()
                qs = pl.ds(q0, rows)
                for hd in range(hp):
                    qv = q_buf[hd, qs, :]
                    kv = k_buf[hd, 0:klen, :]
                    p = jnp.exp(_masked_scores(qv, kv, rows, klen) - lse_ref[0, qs, hd:hd + 1])
                    dob = do_buf[qs, hd * V_HEAD:(hd + 1) * V_HEAD]
                    ds = (p * (_nt(dob, v_buf[hd, 0:klen, :]) - delta_ref[0, qs, hd:hd + 1])).astype(BF16)
                    dq = _nn(ds, kv) * SCALE
                    dq_buf[idx % 2, hd, 0:rows, 0:QK_NOPE] = dq[:, 0:QK_NOPE].astype(BF16)
                    dq_buf[idx % 2, hd, 0:rows, QK_NOPE:] = _unrope(dq[:, QK_NOPE:], cos_ref[qs, :], sin_ref[qs, :]).astype(BF16)
                    dk_acc[hd, 0:klen, :] += _tn(ds, qv)
                    dv_acc[hd, 0:klen, :] += _tn(p.astype(BF16), dob)
                store(idx).start()

        @pl.when(step == n_t - 1)
        def _():
            @pl.when(grp + 1 < n_g)
            def _():
                for cp in loads(grp + 1, 0):
                    cp.start()

            for hd in range(hp):
                dkv_ref[hd, :, 0:QK_NOPE] = dk_acc[hd, :, 0:QK_NOPE].astype(BF16)
                dkv_ref[hd, :, QK_NOPE:] = dv_acc[hd].astype(BF16)
                dkr_ref[...] += dk_acc[hd, :, QK_NOPE:]
            store(n_t - 2).wait()
            store(n_t - 1).wait()

            @pl.when(grp == n_g - 1)
            def _():
                finish_dwout()

    hbm = pl.BlockSpec(memory_space=pl.ANY)
    stat = pl.BlockSpec((1, N, 128), lambda g, t: (g, 0, 0), pipeline_mode=pl.Buffered(1))
    piece_f32 = lambda lead: pltpu.VMEM((lead, half, D), F32)
    piece_bf16 = lambda lead: pltpu.VMEM((lead, half, D), BF16)
    return pl.pallas_call(
        body,
        name="attn_bwd",
        grid=(n_g, n_t),
        in_specs=[hbm, hbm, hbm, hbm, stat, stat, _const(N, 128), _const(N, 128), hbm],
        out_specs=[hbm, pl.BlockSpec((hp, N, 256), lambda g, t: (g, 0, 0), pipeline_mode=pl.Buffered(1)), _const(N, 128),
                   _const(SHARD_OUT, D)],
        out_shape=[
            jax.ShapeDtypeStruct((HEADS, N, 256), BF16), jax.ShapeDtypeStruct((HEADS, N, 256), BF16),
            jax.ShapeDtypeStruct((N, 128), F32), jax.ShapeDtypeStruct((SHARD_OUT, D), F32),
        ],
        scratch_shapes=[pltpu.VMEM((hp, N, 256), BF16), pltpu.VMEM((hp, N, 256), BF16), pltpu.VMEM((hp, N, V_HEAD), BF16),
                        pltpu.VMEM((N, hp * V_HEAD), BF16), pltpu.VMEM((2, hp, TQ, 256), BF16),
                        pltpu.VMEM((hp, N, 256), F32), pltpu.VMEM((hp, N, V_HEAD), F32),
                        piece_f32(CHIPS), piece_f32(CHIPS), piece_bf16(3), piece_bf16(3), pltpu.VMEM((SHARD_OUT, D), F32),
                        pltpu.SemaphoreType.DMA((4, 2)), pltpu.SemaphoreType.DMA((2,)), pltpu.SemaphoreType.DMA((CHIPS,)),
                        pltpu.SemaphoreType.DMA((CHIPS,)), pltpu.SemaphoreType.DMA((CHIPS,)),
                        pltpu.SemaphoreType.DMA((3,)), pltpu.SemaphoreType.DMA((3,)),
                        pltpu.SemaphoreType.DMA((1,)), pltpu.SemaphoreType.DMA((1,))],
        compiler_params=_cparams(dimension_semantics=("arbitrary", "arbitrary"), collective_id=2),
    )(q, k, v, do, lse, delta, cosf, sinf, dwout)


def _bwd_in(h, dh2, dq, dkv, dkr, cq, ckv, dpl, dpg, dag, norm_g, win, gq, wq, gkv, wkv, cosf, sinf, adam_out):
    tr = ROWS_BWD
    nb = N // tr
    per = tr // HALO
    lead = HEAD_ROWS
    adam_rows = SHARD_OUT // nb

    def body(h_ref, dh2_ref, dq_ref, dkv_ref, dkr_ref, cq_ref, ckv_ref, dpl_ref, halo_ref, dpg_ref, dag_ref,
             g_ref, win_ref, gq_ref, wq_ref, gkv_ref, wkv_ref, cos_ref, sin_ref, aw_ref, ag_ref, am_ref, av_ref,
             gx_ref, dmeta_ref, dwin_ref, dwq_ref, dwkv_ref, dg_ref, dgq_ref, dgkv_ref, ago_ref, ad_ref, anm_ref, anv_ref,
             dh_buf, gx_sem):
        i = pl.program_id(0)
        grad_out = ag_ref[...]
        ago_ref[...] = grad_out
        ad_ref[...], anm_ref[...], anv_ref[...] = _adamw_math(aw_ref[...], grad_out, am_ref[...], av_ref[...])

        @pl.when(i == 0)
        def _():
            dwin_ref[...] = jnp.zeros_like(dwin_ref)
            dwq_ref[...] = jnp.zeros_like(dwq_ref)
            dwkv_ref[...] = jnp.zeros_like(dwkv_ref)
            dg_ref[...] = jnp.zeros_like(dg_ref)
            dgq_ref[...] = jnp.zeros_like(dgq_ref)
            dgkv_ref[...] = jnp.zeros_like(dgkv_ref)

        row0 = i * tr
        h = h_ref[...]
        r = lax.rsqrt(jnp.mean(h * h, axis=-1, keepdims=True) + EPS)
        n = h * r
        gv = g_ref[...]
        hn = (n * gv).astype(BF16)
        cq = cq_ref[...]
        rq = lax.rsqrt(jnp.mean(cq * cq, axis=-1, keepdims=True) + EPS)
        nq = cq * rq
        gqv = gq_ref[...]
        cqn = (nq * gqv).astype(BF16)
        dcqn = jnp.zeros((tr, Q_LORA), F32)
        for hd in range(HEADS):
            dqf = dq_ref[hd]
            dcqn = dcqn + _nn(dqf, wq_ref[hd])
            dwq_ref[hd] += _tn(dqf, cqn)
        dgq_ref[...] += jnp.sum(dcqn * nq, axis=0, keepdims=True)
        dnq = dcqn * gqv
        dcq = rq * (dnq - nq * jnp.mean(dnq * nq, axis=-1, keepdims=True))

        ckv = ckv_ref[...]
        rkv = lax.rsqrt(jnp.mean(ckv * ckv, axis=-1, keepdims=True) + EPS)
        nkv = ckv * rkv
        gkvv = gkv_ref[...]
        ckvn = (nkv * gkvv).astype(BF16)
        dckvn = jnp.zeros((tr, KV_LORA), F32)
        for hd in range(HEADS):
            dkv = dkv_ref[hd]
            dckvn = dckvn + _nt(dkv, wkv_ref[hd])
            dwkv_ref[hd] += _tn(ckvn, dkv)
        dgkv_ref[...] += jnp.sum(dckvn * nkv, axis=0, keepdims=True)
        dnkv = dckvn * gkvv
        dckv = rkv * (dnkv - nkv * jnp.mean(dnkv * nkv, axis=-1, keepdims=True))
        dkr = _unrope(dkr_ref[...], cos_ref[...], sin_ref[...])

        cur = dpl_ref[...]
        halo = jnp.where(i < nb - 1, halo_ref[...], 0.0)
        dpi = []
        for g, w in enumerate(POOL_WINDOWS):
            sl = slice(g * POOL_GROUP, (g + 1) * POOL_GROUP)
            a = jnp.concatenate([cur[:, sl] * _inv_count(row0, tr, w), halo[:, sl] * _inv_count(row0 + tr, HALO, w)], axis=0)
            acc = a
            shift = 1
            while shift < w:
                acc = acc + pltpu.roll(acc, tr + HALO - shift, 0)
                shift *= 2
            dpi.append(acc[0:tr] - cur[:, sl])

        du = jnp.concatenate([t.astype(BF16) for t in dpi] + [dpg_ref[...]] + [t.astype(BF16) for t in (dcq, dckv, dkr)],
                             axis=1)
        dagb = dag_ref[...]
        dwin_ref[0:O_KR_END, :] += _tn(du, hn)
        dwin_ref[O_AG:D_IN, :] += _tn(dagb, hn)
        dhn = _nn(du, win_ref[0:O_KR_END, :]) + _nn(dagb, win_ref[O_AG:D_IN, :])
        dg_ref[...] += jnp.sum(dhn * n, axis=0, keepdims=True)
        dn = dhn * gv
        dh = dh2_ref[...] + r * (dn - n * jnp.mean(dn * n, axis=-1, keepdims=True))

        first = pltpu.make_async_copy(dh_buf.at[pl.ds(lead, tr - lead), :], gx_ref.at[pl.ds(0, tr - lead), :], gx_sem)
        later = lambda step: pltpu.make_async_copy(
            dh_buf, gx_ref.at[pl.ds(pl.multiple_of(step * tr - lead, 16), tr), :], gx_sem)

        @pl.when(i == 1)
        def _():
            first.wait()

        @pl.when(i > 1)
        def _():
            later(i - 1).wait()

        dh_buf[...] = dh

        @pl.when(i == 0)
        def _():
            first.start()
            for chip in range(CHIPS):
                dmeta_ref[chip] = dh[PAD:HEAD_ROWS, chip * 256:(chip + 1) * 256]

        @pl.when(i > 0)
        def _():
            later(i).start()

        @pl.when(i == nb - 1)
        def _():
            later(i).wait()

    head = lambda w: pl.BlockSpec((HEADS, tr, w), lambda i: (0, i, 0))
    halo_spec = pl.BlockSpec((HALO, D_POOL), lambda i: (jnp.minimum((i + 1) * per, N // HALO - 1), 0))
    return pl.pallas_call(
        body,
        name="bwd_in",
        grid=(nb,),
        in_specs=[
            _rows(D, tr), _rows(D, tr), head(256), head(256), _rows(128, tr), _rows(Q_LORA, tr), _rows(KV_LORA, tr),
            _rows(D_POOL, tr), halo_spec, _rows(D_POOL, tr), _rows(D_POOL, tr),
            _const(1, D), _const(D_IN, D), _const(1, Q_LORA), _const(HEADS, 256, Q_LORA),
            _const(1, KV_LORA), _const(HEADS, KV_LORA, 256), _rows(128, tr), _rows(128, tr),
        ] + [_rows(D, adam_rows)] * 4,
        out_specs=[
            pl.BlockSpec(memory_space=pl.ANY), _const(CHIPS, N_META, 256), _const(D_IN, D), _const(HEADS, 256, Q_LORA),
            _const(HEADS, KV_LORA, 256), _const(1, D), _const(1, Q_LORA), _const(1, KV_LORA),
        ] + [_rows(D, adam_rows)] * 4,
        out_shape=[
            jax.ShapeDtypeStruct((S, D), F32), jax.ShapeDtypeStruct((CHIPS, N_META, 256), F32),
            jax.ShapeDtypeStruct((D_IN, D), F32), jax.ShapeDtypeStruct((HEADS, 256, Q_LORA), F32),
            jax.ShapeDtypeStruct((HEADS, KV_LORA, 256), F32),
            jax.ShapeDtypeStruct((1, D), F32), jax.ShapeDtypeStruct((1, Q_LORA), F32), jax.ShapeDtypeStruct((1, KV_LORA), F32),
        ] + [jax.ShapeDtypeStruct((SHARD_OUT, D), F32)] * 4,
        scratch_shapes=[pltpu.VMEM((tr, D), F32), pltpu.SemaphoreType.DMA],
        compiler_params=_cparams(dimension_semantics=("arbitrary",)),
    )(h, dh2, dq, dkv, dkr, cq, ckv, dpl, dpl, dpg, dag, norm_g, win, gq, wq, gkv, wkv, cosf, sinf, *adam_out)


def _local_step(h, tgt, norm_g, win, gq, wq, gkv, wkv, pool_w, pool_scale, wout_s, m_wout_s, v_wout_s, gf, cosf, sinf):
    pool_in, pool_gate, cq, ckv, attn_gate, q, k, v = _fwd_in(h, norm_g, win, gq, wq, gkv, wkv, cosf, sinf)
    attn, lse, wout = _attn_fwd(q, k, v, wout_s)
    dh2, do, delta, dag, dpg, dpl, dwout, dpw, dps, dgf, loss = _mid(
        h, tgt, pool_in, pool_gate, attn_gate, attn, pool_w, pool_scale, wout, gf)
    dq, dkv, dkr, gwout = _attn_bwd(q, k, v, do, lse, delta, cosf, sinf, dwout)
    gx, dmeta, dwin, dwq, dwkv, dg, dgq, dgkv, *r_out = _bwd_in(
        h, dh2, dq, dkv, dkr, cq, ckv, dpl, dpg, dag, norm_g, win, gq, wq, gkv, wkv, cosf, sinf,
        (wout_s, gwout, m_wout_s, v_wout_s))
    return dict(gx=gx, dmeta=dmeta, dwin=dwin, dwq=dwq, dwkv=dwkv, r_out=tuple(r_out), dg=dg, dgq=dgq, dgkv=dgkv,
                dpw=dpw, dps=dps, dgf=dgf, loss=loss)


_CHIP_RELS = ((0, 0), (1, 0), (0, 1), (1, 1))

_ARR_ROWS = (SHARD_IN, SHARD_OUT, 256, KV_LORA, N_META)
_ARR_COLS = (D, D, Q_LORA, 256, 256)
_PIECES = (
    (0, 0, 256, 0), (0, 256, SHARD_IN - 256, 1),
    (1, 0, 128, 0), (1, 128, 128, 1),
    (2, 0, 128, 0), (2, 128, 128, 1),
    (3, 0, 64, 0), (3, 64, 64, 1),
    (4, 0, N_META, 0),
)
_NP = len(_PIECES)
_PIECE_MAX = (256, 128, 128, 64, N_META)


def _gathered_at(refs, arr, chip, r0, n):
    if arr in (0, 1):
        return refs[arr].at[pl.ds(pl.multiple_of(_ARR_ROWS[arr] * chip + r0, 16), n), :]
    return refs[arr].at[chip, pl.ds(r0, n), :]


def _remote(src, dst, send_sem, recv_sem, to):
    return pltpu.make_async_remote_copy(src_ref=src, dst_ref=dst, send_sem=send_sem, recv_sem=recv_sem,
                                        device_id=to, device_id_type=MESH)


def _gather_weights(winT_s, wqT_s, wkv_s, meta_s, x2, tgt2):
    arrays = (0, 2, 3, 4)

    def body(win_ref, wq_ref, wkv_ref, meta_ref, x_ref, t_ref, win_o, wq_o, wkv_o, h_o, tp_o,
             s_win, s_wq, s_wkv, meta_all, head_buf, x_buf, t_buf, ici_send, ici_recv, fwd_send, fwd_recv,
             loc_sems, own_sems):
        x, y, c = lax.axis_index("x"), lax.axis_index("y"), lax.axis_index("c")
        me = 2 * x + y
        stage = (s_win, None, s_wq, s_wkv, meta_ref)
        outs = (win_o, None, wq_o, wkv_o, meta_all)

        _peer_barrier(x, y, c)

        frames = pl.ds(HEAD_ROWS, S)
        loads = [pltpu.make_async_copy(x_ref, x_buf, loc_sems.at[0]), pltpu.make_async_copy(t_ref, t_buf, loc_sems.at[1])]
        local = [pltpu.make_async_copy(x_buf, h_o.at[frames, :], loc_sems.at[0]),
                 pltpu.make_async_copy(t_buf, tp_o.at[frames, :], loc_sems.at[1])]
        for cp in loads:
            cp.start()

        s_win[...] = win_ref[...].astype(BF16)
        s_wq[0:QK, :] = wq_ref[...].astype(BF16)
        s_wq[QK:256, :] = jnp.zeros((256 - QK, Q_LORA), BF16)
        s_wkv[...] = wkv_ref[...].astype(BF16)

        def chip_of(rel):
            fx, fy = _CHIP_RELS[rel]
            return 2 * (x ^ fx) + (y ^ fy)

        def same_core_of(rel):
            fx, fy = _CHIP_RELS[rel]
            return (x ^ fx, y ^ fy, c)

        def ici_copy(rel, i, src_chip, to):
            arr, r0, n, _ = _PIECES[i]
            k = (rel - 1) * _NP + i
            return _remote(stage[arr].at[pl.ds(r0, n), :], _gathered_at(outs, arr, src_chip, r0, n),
                           ici_send.at[k], ici_recv.at[k], to)

        def fwd_copy(rel, i, to):
            arr, r0, n, _ = _PIECES[i]
            k = (rel - 1) * _NP + i
            place = _gathered_at(outs, arr, chip_of(rel), r0, n)
            return _remote(place, place, fwd_send.at[k], fwd_recv.at[k], to)

        for core in (0, 1):
            @pl.when(c == core)
            def _(core=core):
                mine = [i for i in range(_NP) if _PIECES[i][3] == core and _PIECES[i][0] in arrays]
                theirs = [i for i in range(_NP) if _PIECES[i][3] != core and _PIECES[i][0] in arrays]
                sends = [ici_copy(rel, i, me, same_core_of(rel)) for rel in (1, 2, 3) for i in mine]
                for cp in sends:
                    cp.start()
                for ld, st in zip(loads, local):
                    ld.wait()
                    st.start()
                own = [pltpu.make_async_copy(stage[arr], _gathered_at(outs, arr, me, 0, _ARR_ROWS[arr]), own_sems.at[arr])
                       for arr in arrays if arr != 4]
                for cp in own:
                    cp.start()
                meta_all[me] = meta_ref[...]
                for rel in (1, 2, 3):
                    for i in mine:
                        ici_copy(rel, i, chip_of(rel), (x, y, c)).wait_recv()
                        fwd = fwd_copy(rel, i, (x, y, 1 - c))
                        fwd.start()
                        sends.append(fwd)
                for rel in (1, 2, 3):
                    for i in theirs:
                        fwd_copy(rel, i, (x, y, c)).wait_recv()
                for cp in sends:
                    cp.wait_send()
                for cp in own:
                    cp.wait()

        head_buf[...] = jnp.zeros_like(head_buf)
        zeros = pltpu.make_async_copy(head_buf, tp_o.at[pl.ds(0, HEAD_ROWS), :], loc_sems.at[2])
        zeros.start()
        zeros.wait()
        for chip in range(CHIPS):
            head_buf[PAD:HEAD_ROWS, chip * 256:(chip + 1) * 256] = meta_all[chip]
        head = pltpu.make_async_copy(head_buf, h_o.at[pl.ds(0, HEAD_ROWS), :], loc_sems.at[2])
        head.start()
        head.wait()
        for cp in local:
            cp.wait()

    vm = pl.BlockSpec(memory_space=pltpu.VMEM)
    hbm = pl.BlockSpec(memory_space=pl.ANY)
    return pl.pallas_call(
        body,
        name="gather_weights",
        in_specs=[vm] * 4 + [hbm] * 2,
        out_specs=[hbm] * 5,
        out_shape=[
            jax.ShapeDtypeStruct((D_IN, D), BF16),
            jax.ShapeDtypeStruct((CHIPS, 256, Q_LORA), BF16), jax.ShapeDtypeStruct((CHIPS, KV_LORA, 256), BF16),
            jax.ShapeDtypeStruct((N, D), F32), jax.ShapeDtypeStruct((N, D), F32),
        ],
        scratch_shapes=[pltpu.VMEM((_ARR_ROWS[a], _ARR_COLS[a]), BF16) for a in (0, 2, 3)]
        + [pltpu.VMEM((CHIPS, N_META, 256), F32), pltpu.VMEM((HEAD_ROWS, D), F32), pltpu.VMEM((S, D), F32),
           pltpu.VMEM((S, D), F32)]
        + [pltpu.SemaphoreType.DMA((3 * _NP,))] * 4 + [pltpu.SemaphoreType.DMA((3,)), pltpu.SemaphoreType.DMA((4,))],
        compiler_params=_cparams(collective_id=0),
    )(winT_s, wqT_s, wkv_s, meta_s, x2, tgt2)


_SM_ROWS = (len(POOL_WINDOWS) * POOL_GROUP, VEC_ROWS)
_SM_COLS = (POOL_GROUP, D)
_SM_PIECES = ((0, 0, 256, 0), (0, 256, 256, 1), (1, 0, VEC_ROWS, 0))
_NSP = len(_SM_PIECES)


def _reduce_grads(dwin, dwq, dwkv, dmeta4, dpw, dg, dgf, dgq, dgkv, dps, loss):
    arrays = (0, 2, 3, 4)

    def body(dwin_ref, dwq_ref, dwkv_ref, dmeta_ref, dpw_ref, dg_ref, dgf_ref, dgq_ref, dgkv_ref, dps_ref,
             loss_ref, gwin_o, gwq_o, gwkv_o, gmeta_o, gpw_o, gg_o, ggf_o, ggq_o, ggkv_o, gps_o, gloss_o,
             ow0, ow2, ow3, ow4, sb0, sb2, sb3, sb4, st0, st2, st3, st4, rc0, rc2, rc3, rc4,
             vec, sm_sb0, sm_sb1, sm_cs0, sm_cs1, sm_rc0, sm_rc1, vec_fin,
             own_sems, d2d_send, d2d_recv, ici_send, ici_recv, fin_send, fin_recv,
             swap_send, swap_recv, smi_send, smi_recv, smf_send, smf_recv):
        x, y, c = lax.axis_index("x"), lax.axis_index("y"), lax.axis_index("c")
        me = 2 * x + y
        grads = (dwin_ref, None, dwq_ref, dwkv_ref, dmeta_ref)
        outs = (gwin_o, None, gwq_o, gwkv_o, gmeta_o)
        own_buf = (ow0, None, ow2, ow3, ow4)
        sib_buf = (sb0, None, sb2, sb3, sb4)
        stage = (st0, None, st2, st3, st4)
        recv = (rc0, None, rc2, rc3, rc4)
        sm_mine = (dpw_ref, vec)
        sm_sib = (sm_sb0, sm_sb1)
        sm_chip = (sm_cs0, sm_cs1)
        sm_recv = (sm_rc0, sm_rc1)
        sm_out = (gpw_o, vec_fin)
        sibling = (x, y, 1 - c)

        def chip_of(rel):
            fx, fy = _CHIP_RELS[rel]
            return 2 * (x ^ fx) + (y ^ fy)

        def same_core_of(rel):
            fx, fy = _CHIP_RELS[rel]
            return (x ^ fx, y ^ fy, c)

        def slot(bufs, i, idx):
            arr, _, n, _ = _PIECES[i]
            return bufs[arr].at[idx, pl.ds(0, n), :]

        def own_load(rel, i):
            arr, r0, n, _ = _PIECES[i]
            return pltpu.make_async_copy(_gathered_at(grads, arr, chip_of(rel), r0, n), slot(own_buf, i, rel),
                                         own_sems.at[rel * _NP + i])

        def d2d_copy(rel, i):
            arr, r0, n, _ = _PIECES[i]
            k = rel * _NP + i
            return _remote(_gathered_at(grads, arr, chip_of(rel), r0, n), slot(sib_buf, i, rel),
                           d2d_send.at[k], d2d_recv.at[k], sibling)

        def ici_copy(rel, i):
            k = (rel - 1) * _NP + i
            return _remote(slot(stage, i, rel - 1), slot(recv, i, rel - 1), ici_send.at[k], ici_recv.at[k],
                           same_core_of(rel))

        def fin_copy(i):
            arr, r0, n, _ = _PIECES[i]
            place = outs[arr].at[pl.ds(r0, n), :]
            return _remote(place, place, fin_send.at[i], fin_recv.at[i], sibling)

        def sm_ici_copy(rel, j):
            blk, r0, n, _ = _SM_PIECES[j]
            k = (rel - 1) * _NSP + j
            return _remote(sm_chip[blk].at[pl.ds(r0, n), :], sm_recv[blk].at[rel - 1, pl.ds(r0, n), :],
                           smi_send.at[k], smi_recv.at[k], same_core_of(rel))

        def sm_fin_copy(j):
            blk, r0, n, _ = _SM_PIECES[j]
            place = sm_out[blk].at[pl.ds(r0, n), :]
            return _remote(place, place, smf_send.at[j], smf_recv.at[j], sibling)

        _peer_barrier(x, y, c)

        vec[...] = jnp.zeros_like(vec)
        vec[0:1, :] = dg_ref[...]
        vec[1:2, :] = dgf_ref[...]
        vec[2:3, V_GQ:V_GQ + Q_LORA] = dgq_ref[...]
        vec[2:3, V_GKV:V_GKV + KV_LORA] = dgkv_ref[...]
        vec[2:3, V_PS:V_PS + D_POOL] = dps_ref[...]
        vec[2:3, V_LOSS:D] = loss_ref[...]
        swaps = [_remote(sm_mine[b], sm_sib[b], swap_send.at[b], swap_recv.at[b], sibling) for b in (0, 1)]
        for cp in swaps:
            cp.start()

        for core in (0, 1):
            @pl.when(c == core)
            def _(core=core):
                mine = [i for i in range(_NP) if _PIECES[i][3] == core and _PIECES[i][0] in arrays]
                theirs = [i for i in range(_NP) if _PIECES[i][3] != core and _PIECES[i][0] in arrays]
                sm_mine_p = [j for j in range(_NSP) if _SM_PIECES[j][3] == core]
                sm_theirs_p = [j for j in range(_NSP) if _SM_PIECES[j][3] != core]
                sends = list(swaps)

                for rel in (1, 2, 3, 0):
                    for i in theirs:
                        cp = d2d_copy(rel, i)
                        cp.start()
                        sends.append(cp)
                    for i in mine:
                        own_load(rel, i).start()

                for b in (0, 1):
                    swaps[b].wait_recv()
                    sm_chip[b][...] = sm_mine[b][...] + sm_sib[b][...]
                for rel in (1, 2, 3):
                    for j in sm_mine_p:
                        cp = sm_ici_copy(rel, j)
                        cp.start()
                        sends.append(cp)

                for rel in (1, 2, 3):
                    for i in mine:
                        arr, r0, n, _ = _PIECES[i]
                        own_load(rel, i).wait()
                        d2d_copy(rel, i).wait_recv()
                        total = slot(own_buf, i, rel)[...] + slot(sib_buf, i, rel)[...]
                        slot(stage, i, rel - 1)[...] = total.astype(stage[arr].dtype)
                        cp = ici_copy(rel, i)
                        cp.start()
                        sends.append(cp)

                for i in mine:
                    arr, r0, n, _ = _PIECES[i]
                    own_load(0, i).wait()
                    d2d_copy(0, i).wait_recv()
                    total = slot(own_buf, i, 0)[...] + slot(sib_buf, i, 0)[...]
                    for rel in (1, 2, 3):
                        ici_copy(rel, i).wait_recv()
                        total = total + slot(recv, i, rel - 1)[...].astype(F32)
                    outs[arr][pl.ds(r0, n), :] = total
                    cp = fin_copy(i)
                    cp.start()
                    sends.append(cp)

                for j in sm_mine_p:
                    blk, r0, n, _ = _SM_PIECES[j]
                    for rel in (1, 2, 3):
                        sm_ici_copy(rel, j).wait_recv()
                    total = jnp.zeros((n, _SM_COLS[blk]), F32)
                    for chip in range(CHIPS):
                        flips = chip ^ me
                        rel = jnp.where(flips == 2, 1, jnp.where(flips == 1, 2, flips))
                        theirs_rows = sm_recv[blk][jnp.maximum(rel - 1, 0), pl.ds(r0, n), :]
                        total = total + jnp.where(rel == 0, sm_chip[blk][pl.ds(r0, n), :], theirs_rows)
                    sm_out[blk][pl.ds(r0, n), :] = total
                    cp = sm_fin_copy(j)
                    cp.start()
                    sends.append(cp)

                for i in theirs:
                    fin_copy(i).wait_recv()
                for j in sm_theirs_p:
                    sm_fin_copy(j).wait_recv()
                for cp in sends:
                    cp.wait_send()

        gg_o[...] = vec_fin[0:1, :]
        ggf_o[...] = vec_fin[1:2, :]
        ggq_o[...] = vec_fin[2:3, V_GQ:V_GQ + Q_LORA]
        ggkv_o[...] = vec_fin[2:3, V_GKV:V_GKV + KV_LORA]
        gps_o[...] = vec_fin[2:3, V_PS:V_PS + D_POOL]
        gloss_o[...] = vec_fin[2:3, V_LOSS:D]

    vm = pl.BlockSpec(memory_space=pltpu.VMEM)
    piece_buf = lambda lead, dtype: [pltpu.VMEM((lead, _PIECE_MAX[a], _ARR_COLS[a]), F32 if a == 4 else dtype)
                                     for a in arrays]
    sm_buf = lambda *lead: [pltpu.VMEM(lead + (_SM_ROWS[b], _SM_COLS[b]), F32) for b in (0, 1)]
    dma = lambda n: [pltpu.SemaphoreType.DMA((n,))] * 2
    return pl.pallas_call(
        body,
        name="reduce_grads",
        in_specs=[pl.BlockSpec(memory_space=pl.ANY)] * 3 + [vm] * 8,
        out_specs=[vm] * 11,
        out_shape=[jax.ShapeDtypeStruct((_ARR_ROWS[a], _ARR_COLS[a]), F32) for a in arrays]
        + [jax.ShapeDtypeStruct((_SM_ROWS[0], _SM_COLS[0]), F32), jax.ShapeDtypeStruct((1, D), F32),
           jax.ShapeDtypeStruct((1, D), F32), jax.ShapeDtypeStruct((1, Q_LORA), F32),
           jax.ShapeDtypeStruct((1, KV_LORA), F32), jax.ShapeDtypeStruct((1, D_POOL), F32),
           jax.ShapeDtypeStruct((1, 128), F32)],
        scratch_shapes=piece_buf(CHIPS, F32) + piece_buf(CHIPS, F32) + piece_buf(3, BF16) + piece_buf(3, BF16)
        + [pltpu.VMEM((VEC_ROWS, D), F32)] + sm_buf() + sm_buf() + sm_buf(3) + [pltpu.VMEM((VEC_ROWS, D), F32)]
        + [pltpu.SemaphoreType.DMA((CHIPS * _NP,))]
        + dma(CHIPS * _NP) + dma(3 * _NP) + dma(_NP) + dma(2) + dma(3 * _NSP) + dma(_NSP),
        compiler_params=_cparams(collective_id=3),
    )(dwin, dwq, dwkv, dmeta4, dpw, dg, dgf, dgq, dgkv, dps, loss)


def _adamw_math(w, g, m, v):
    m = B1 * m + (1.0 - B1) * g
    v = B2 * v + (1.0 - B2) * (g * g)
    m_hat = m / C1
    v_hat = v / C2
    delta = -LR * (m_hat / (jnp.sqrt(v_hat) + ADAM_EPS) + WD * w)
    return delta, m, v


def _adamw_rows(name, w, g, m, v, block_rows):
    rows, cols = w.shape

    def body(w_ref, g_ref, m_ref, v_ref, go_ref, d_ref, nm_ref, nv_ref):
        g = g_ref[...]
        go_ref[...] = g
        d_ref[...], nm_ref[...], nv_ref[...] = _adamw_math(w_ref[...], g, m_ref[...], v_ref[...])

    spec = pl.BlockSpec((block_rows, cols), lambda i: (i, 0))
    return pl.pallas_call(
        body,
        name=name,
        grid=(rows // block_rows,),
        in_specs=[spec] * 4,
        out_specs=[spec] * 4,
        out_shape=[jax.ShapeDtypeStruct(w.shape, F32)] * 4,
        compiler_params=_cparams(dimension_semantics=("arbitrary",)),
    )(w, g, m, v)


def _adamw_small(groups):
    n = len(groups)

    def body(*refs):
        ins, outs = refs[:4 * n], refs[4 * n:]
        for t in range(n):
            w_ref, g_ref, m_ref, v_ref = ins[4 * t:4 * t + 4]
            g = g_ref[0:w_ref.shape[0], :]
            outs[4 * t][...] = g
            outs[4 * t + 1][...], outs[4 * t + 2][...], outs[4 * t + 3][...] = _adamw_math(
                w_ref[...], g, m_ref[...], v_ref[...])

    vm = pl.BlockSpec(memory_space=pltpu.VMEM)
    flat = [a for grp in groups for a in grp]
    outs = pl.pallas_call(
        body,
        name="adamw_small",
        in_specs=[vm] * (4 * n),
        out_specs=[vm] * (4 * n),
        out_shape=[jax.ShapeDtypeStruct(grp[0].shape, F32) for grp in groups for _ in range(4)],
        compiler_params=_cparams(),
    )(*flat)
    return [tuple(outs[4 * t:4 * t + 4]) for t in range(n)]


def _rope_tables():
    half = QK_ROPE // 2
    f32 = np.float32
    inv_freq = (f32(1.0) / (f32(ROPE_THETA) ** (np.arange(half, dtype=f32) / f32(half)))).astype(f32)
    pos = np.arange(N, dtype=f32) - f32(PAD)
    ang = (pos[:, None] * inv_freq[None, :]).astype(f32)
    cos, sin = np.cos(ang).astype(f32), np.sin(ang).astype(f32)
    zero = np.zeros((N, 128 - QK_ROPE), f32)
    return jnp.asarray(np.concatenate([cos, cos, zero], axis=1)), jnp.asarray(np.concatenate([-sin, sin, zero], axis=1))


def kernel(x, meta_tokens, norm_g, w_in, q_norm_g, w_q_b, kv_norm_g, w_kv_b, pool_w, pool_scale, w_out, final_norm_g, loss_target, m_meta_tokens, m_norm_g, m_w_in, m_q_norm_g, m_w_q_b, m_kv_norm_g, m_w_kv_b, m_pool_w, m_pool_scale, m_w_out, m_final_norm_g, v_meta_tokens, v_norm_g, v_w_in, v_q_norm_g, v_w_q_b, v_kv_norm_g, v_w_kv_b, v_pool_w, v_pool_scale, v_w_out, v_final_norm_g):
    tr = lambda a: a[0].T
    win, wq, wkv, h, tgt = _gather_weights(tr(w_in), tr(w_q_b), w_kv_b[0], meta_tokens, x[0], loss_target[0])
    cosf, sinf = _rope_tables()
    gf = final_norm_g.reshape(1, D)

    part = _local_step(h, tgt, norm_g, win, q_norm_g, wq, kv_norm_g, wkv, pool_w[0], pool_scale, w_out[0], m_w_out[0],
                       v_w_out[0], gf, cosf, sinf)

    pw2 = lambda a: a.reshape(len(POOL_WINDOWS) * POOL_GROUP, POOL_GROUP)
    gwinT, gwqT, gwkv, gmeta, gpw, gg, ggf, ggq, ggkv, gps, gloss = _reduce_grads(
        part["dwin"], part["dwq"], part["dwkv"], part["dmeta"], pw2(part["dpw"]), part["dg"],
        part["dgf"], part["dgq"], part["dgkv"], part["dps"], part["loss"])

    r_in = _adamw_rows("adamw_w_in", tr(w_in), gwinT, tr(m_w_in), tr(v_w_in), 248)
    r_out = part["r_out"]
    fn2 = lambda a: a.reshape(1, D)
    r_meta, r_norm, r_gq, r_wq, r_gkv, r_wkv, r_pw, r_ps, r_fn = _adamw_small([
        (meta_tokens, gmeta, m_meta_tokens, v_meta_tokens),
        (norm_g, gg, m_norm_g, v_norm_g),
        (q_norm_g, ggq, m_q_norm_g, v_q_norm_g),
        (tr(w_q_b), gwqT, tr(m_w_q_b), tr(v_w_q_b)),
        (kv_norm_g, ggkv, m_kv_norm_g, v_kv_norm_g),
        (w_kv_b[0], gwkv, m_w_kv_b[0], v_w_kv_b[0]),
        (pw2(pool_w), gpw, pw2(m_pool_w), pw2(v_pool_w)),
        (pool_scale, gps, m_pool_scale, v_pool_scale),
        (fn2(final_norm_g), ggf, fn2(m_final_norm_g), fn2(v_final_norm_g)),
    ])
    untr = lambda a: a.T[None]
    pw4 = lambda a: a.reshape(1, len(POOL_WINDOWS), POOL_GROUP, POOL_GROUP)
    per_kind = [[
        r_meta[kind], r_norm[kind], untr(r_in[kind]), r_gq[kind], untr(r_wq[kind]), r_gkv[kind], r_wkv[kind][None],
        pw4(r_pw[kind]), r_ps[kind], r_out[kind][None], r_fn[kind].reshape(D),
    ] for kind in range(4)]
    return (gloss[0, 0], part["gx"][None], *per_kind[0], *per_kind[1], *per_kind[2], *per_kind[3])
```

```python
import jax
import jax.numpy as jnp
import numpy as np
from jax import lax
from jax.experimental import pallas as pl
from jax.experimental.pallas import tpu as pltpu

F32 = jnp.float32
BF16 = jnp.bfloat16

D = 1024
S = 2048
N_META = 16
PAD = 112
HEAD_ROWS = PAD + N_META
N = HEAD_ROWS + S
D_POOL = 512
POOL_WINDOWS = (2, 4, 8, 16)
POOL_GROUP = 128
HALO = 16
HEADS = 4
QK_NOPE = 128
QK_ROPE = 64
QK = QK_NOPE + QK_ROPE
V_HEAD = 128
Q_LORA = 256
KV_LORA = 128
D_IN = 1984
EPS = 1e-6
ROPE_THETA = 10000.0
SCALE = QK ** -0.5
CHIPS = 4

ROWS_FWD = 544
ROWS_MID = 544
ROWS_BWD = 544
TK = 128
TQ = 256
NQ = S // TQ
HEADS_PER_STEP_BWD = 2

O_PI, O_PG, O_CQ, O_CKV, O_KR, O_AG = 0, 512, 1024, 1280, 1408, 1472
O_KR_END = O_KR + 128
SHARD_IN = D_IN // CHIPS
SHARD_OUT = D // CHIPS

LR, B1, B2, ADAM_EPS, WD, STEP = 0.001, 0.9, 0.999, 1e-08, 0.01, 10
C1 = 1.0 - B1**STEP
C2 = 1.0 - B2**STEP

VMEM_LIMIT = 60 * 1024 * 1024
MESH = pl.DeviceIdType.MESH
NEG = -1e30

VEC_ROWS = 8
V_GQ, V_GKV, V_PS, V_LOSS = 0, 256, 384, 896


def _cparams(**kw):
    return pltpu.CompilerParams(vmem_limit_bytes=VMEM_LIMIT, **kw)


def _nt(a, b):
    return lax.dot_general(a, b, (((1,), (1,)), ((), ())), preferred_element_type=F32)


def _tn(a, b):
    return lax.dot_general(a, b, (((0,), (0,)), ((), ())), preferred_element_type=F32)


def _nn(a, b):
    return jnp.dot(a, b, preferred_element_type=F32)


def _swap64(t):
    return pltpu.roll(t, 32, 1) + pltpu.roll(t, 96, 1)


def _sigmoid(x):
    return 1.0 / (1.0 + jnp.exp(-x))


def _low_lanes():
    return (lax.broadcasted_iota(jnp.int32, (1, 128), 1) < QK_ROPE).astype(F32)


def _rows(w, rows):
    return pl.BlockSpec((rows, w), lambda i: (i, 0))


def _const(*shape):
    return pl.BlockSpec(shape, lambda *_: (0,) * len(shape), pipeline_mode=pl.Buffered(1))


STAT_GROUPS = HEADS // HEADS_PER_STEP_BWD


def _stat_slot(head):
    return head // HEADS_PER_STEP_BWD, head % HEADS_PER_STEP_BWD


def _peer_barrier(x, y, c):
    barrier = pltpu.get_barrier_semaphore()
    peers = [(x, y, 1 - c)] + [(x ^ fx, y ^ fy, c) for fx, fy in _CHIP_RELS[1:]]
    for peer in peers:
        pl.semaphore_signal(barrier, inc=1, device_id=peer, device_id_type=MESH)
    pl.semaphore_wait(barrier, len(peers))


def _attn_tiles():
    return [(0, TK, TK)] + [(TK + TQ * t, TQ, TK + TQ * (t + 1)) for t in range(NQ)]


def _masked_scores(q, k, rows, klen):
    s = _nt(q, k)
    col = lax.broadcasted_iota(jnp.int32, (1, TK), 1)
    head_bias = jnp.where(col >= PAD, 0.0, NEG)
    if klen == TK:
        return s + head_bias
    r = lax.broadcasted_iota(jnp.int32, (rows, 1), 0) >> 6
    c = lax.broadcasted_iota(jnp.int32, (1, rows), 1) >> 6
    diag_bias = jnp.where(c <= r, 0.0, NEG)
    parts = [s[:, 0:TK] + head_bias]
    if klen - rows > TK:
        parts.append(s[:, TK:klen - rows])
    parts.append(s[:, klen - rows:klen] + diag_bias)
    return jnp.concatenate(parts, axis=1)


def _fwd_in(h, norm_g, win, gq, wq, gkv, wkv, cosf, sinf):
    tr = ROWS_FWD

    def body(h_ref, g_ref, win_ref, gq_ref, wq_ref, gkv_ref, wkv_ref, cos_ref, sin_ref,
             pi_ref, pg_ref, cq_ref, ckv_ref, ag_ref, q_ref, k_ref, v_ref):
        h = h_ref[...]
        r = lax.rsqrt(jnp.mean(h * h, axis=-1, keepdims=True) + EPS)
        hn = ((h * r) * g_ref[...]).astype(BF16)
        u = _nt(hn, win_ref[0:O_KR_END, :])
        pi_ref[...] = u[:, O_PI:O_PG]
        pg_ref[...] = u[:, O_PG:O_CQ]
        cq = u[:, O_CQ:O_CKV]
        ckv = u[:, O_CKV:O_KR]
        cq_ref[...] = cq
        ckv_ref[...] = ckv
        ag_ref[...] = _nt(hn, win_ref[O_AG:D_IN, :])
        cosv = cos_ref[...]
        sinv = sin_ref[...]
        kr = u[:, O_KR:O_KR_END] * _low_lanes()
        kr = (kr * cosv + _swap64(kr) * sinv).astype(BF16)
        rq = lax.rsqrt(jnp.mean(cq * cq, axis=-1, keepdims=True) + EPS)
        cqn = ((cq * rq) * gq_ref[...]).astype(BF16)
        rkv = lax.rsqrt(jnp.mean(ckv * ckv, axis=-1, keepdims=True) + EPS)
        ckvn = ((ckv * rkv) * gkv_ref[...]).astype(BF16)
        for hd in range(HEADS):
            qh = _nt(cqn, wq_ref[hd]) * SCALE
            z = qh[:, QK_NOPE:]
            q_ref[hd, :, 0:QK_NOPE] = qh[:, 0:QK_NOPE].astype(BF16)
            q_ref[hd, :, QK_NOPE:] = (z * cosv + _swap64(z) * sinv).astype(BF16)
            kvh = _nn(ckvn, wkv_ref[hd])
            k_ref[hd, :, 0:QK_NOPE] = kvh[:, 0:QK_NOPE].astype(BF16)
            k_ref[hd, :, QK_NOPE:] = kr
            v_ref[hd] = kvh[:, QK_NOPE:].astype(BF16)

    head = lambda w: pl.BlockSpec((HEADS, tr, w), lambda i: (0, i, 0))
    return pl.pallas_call(
        body,
        name="fwd_in",
        grid=(N // tr,),
        in_specs=[
            _rows(D, tr), _const(1, D), _const(D_IN, D), _const(1, Q_LORA), _const(HEADS, 256, Q_LORA),
            _const(1, KV_LORA), _const(HEADS, KV_LORA, 256), _rows(128, tr), _rows(128, tr),
        ],
        out_specs=[_rows(D_POOL, tr), _rows(D_POOL, tr), _rows(Q_LORA, tr), _rows(KV_LORA, tr), _rows(D_POOL, tr),
                   head(256), head(256), head(V_HEAD)],
        out_shape=[
            jax.ShapeDtypeStruct((N, D_POOL), F32), jax.ShapeDtypeStruct((N, D_POOL), F32),
            jax.ShapeDtypeStruct((N, Q_LORA), F32), jax.ShapeDtypeStruct((N, KV_LORA), F32),
            jax.ShapeDtypeStruct((N, D_POOL), F32),
            jax.ShapeDtypeStruct((HEADS, N, 256), BF16), jax.ShapeDtypeStruct((HEADS, N, 256), BF16),
            jax.ShapeDtypeStruct((HEADS, N, V_HEAD), BF16),
        ],
        compiler_params=_cparams(dimension_semantics=("arbitrary",)),
    )(h, norm_g, win, gq, wq, gkv, wkv, cosf, sinf)


def _attn_fwd(q, k, v, wout_s):
    tiles = _attn_tiles()
    n_t = len(tiles)
    half = SHARD_OUT // 2
    fwd_step = n_t - 2

    def body(q_hbm, k_hbm, v_hbm, wout_ref, o_hbm, lse_ref, wout_o, q_buf, k_buf, v_buf, o_buf, s_wout, in_sems, out_sems,
             ici_send, ici_recv, fwd_send, fwd_recv, own_sem):
        step = pl.program_id(0)
        x, y, c = lax.axis_index("x"), lax.axis_index("y"), lax.axis_index("c")
        me = 2 * x + y

        def chip_of(rel):
            fx, fy = _CHIP_RELS[rel]
            return 2 * (x ^ fx) + (y ^ fy)

        def place(chip, core):
            return wout_o.at[pl.ds(pl.multiple_of(SHARD_OUT * chip + half * core, half), half), :]

        def ici_copy(rel, src_chip, to):
            return _remote(s_wout.at[pl.ds(pl.multiple_of(half * c, half), half), :], place(src_chip, c),
                           ici_send.at[rel - 1], ici_recv.at[rel - 1], to)

        def fwd_copy(rel, core, to):
            spot = place(chip_of(rel), core)
            return _remote(spot, spot, fwd_send.at[rel - 1], fwd_recv.at[rel - 1], to)

        own = pltpu.make_async_copy(s_wout, wout_o.at[pl.ds(pl.multiple_of(SHARD_OUT * me, SHARD_OUT), SHARD_OUT), :], own_sem)

        @pl.when(step == 0)
        def _():
            _peer_barrier(x, y, c)
            s_wout[...] = wout_ref[...].astype(BF16)
            own.start()
            for rel in (1, 2, 3):
                fx, fy = _CHIP_RELS[rel]
                ici_copy(rel, me, (x ^ fx, y ^ fy, c)).start()

        @pl.when(step == fwd_step)
        def _():
            for rel in (1, 2, 3):
                ici_copy(rel, chip_of(rel), (x, y, c)).wait_recv()
                fwd_copy(rel, c, (x, y, 1 - c)).start()

        def finish_wout():
            for rel in (1, 2, 3):
                fwd_copy(rel, 1 - c, (x, y, c)).wait_recv()
            for rel in (1, 2, 3):
                ici_copy(rel, me, (x, y, c)).wait_send()
                fwd_copy(rel, c, (x, y, c)).wait_send()
            own.wait()

        def loads(idx):
            q0, rows, _ = tiles[idx]
            rs = pl.ds(q0, rows)
            return [pltpu.make_async_copy(src.at[:, rs, :], dst.at[:, rs, :], in_sems.at[a, idx % 2])
                    for a, (src, dst) in enumerate(((q_hbm, q_buf), (k_hbm, k_buf), (v_hbm, v_buf)))]

        def store(idx):
            q0, rows, _ = tiles[idx]
            return pltpu.make_async_copy(o_buf.at[idx % 2, pl.ds(0, rows), :], o_hbm.at[pl.ds(q0, rows), :],
                                         out_sems.at[idx % 2])

        @pl.when(step == 0)
        def _():
            lse_ref[...] = jnp.zeros_like(lse_ref)
            for cp in loads(0):
                cp.start()

        for idx, (q0, rows, klen) in enumerate(tiles):
            @pl.when(step == idx)
            def _(idx=idx, q0=q0, rows=rows, klen=klen):
                for cp in loads(idx):
                    cp.wait()
                if idx + 1 < n_t:
                    for cp in loads(idx + 1):
                        cp.start()
                if idx >= 2:
                    store(idx - 2).wait()
                for hd in range(HEADS):
                    s = _masked_scores(q_buf[hd, q0:q0 + rows, :], k_buf[hd, 0:klen, :], rows, klen)
                    m = jnp.max(s, axis=-1, keepdims=True)
                    p = jnp.exp(s - m)
                    l = jnp.sum(p, axis=-1, keepdims=True)
                    o_buf[idx % 2, 0:rows, hd * V_HEAD:(hd + 1) * V_HEAD] = _nn(p.astype(BF16), v_buf[hd, 0:klen, :]) / l
                    grp, lane = _stat_slot(hd)
                    lse_ref[grp, q0:q0 + rows, lane:lane + 1] = m + jnp.log(l)
                store(idx).start()
                if idx == n_t - 1:
                    store(idx - 1).wait()
                    store(idx).wait()
                    finish_wout()

    hbm = pl.BlockSpec(memory_space=pl.ANY)
    return pl.pallas_call(
        body,
        name="attn_fwd",
        grid=(n_t,),
        in_specs=[hbm, hbm, hbm, _const(SHARD_OUT, D)],
        out_specs=[hbm, _const(STAT_GROUPS, N, 128), hbm],
        out_shape=[jax.ShapeDtypeStruct((N, HEADS * V_HEAD), F32), jax.ShapeDtypeStruct((STAT_GROUPS, N, 128), F32),
                   jax.ShapeDtypeStruct((D, D), BF16)],
        scratch_shapes=[pltpu.VMEM((HEADS, N, 256), BF16), pltpu.VMEM((HEADS, N, 256), BF16),
                        pltpu.VMEM((HEADS, N, V_HEAD), BF16), pltpu.VMEM((2, TQ, HEADS * V_HEAD), F32),
                        pltpu.VMEM((SHARD_OUT, D), BF16),
                        pltpu.SemaphoreType.DMA((3, 2)), pltpu.SemaphoreType.DMA((2,))]
        + [pltpu.SemaphoreType.DMA((3,))] * 4 + [pltpu.SemaphoreType.DMA],
        compiler_params=_cparams(dimension_semantics=("arbitrary",), collective_id=1),
    )(q, k, v, wout_s)


def _inv_count(row0, rows, w):
    row = row0 + lax.broadcasted_iota(jnp.int32, (rows, 1), 0)
    return 1.0 / jnp.clip(row - (PAD - 1), 1, w).astype(F32)


def _mid(h, tgt, pool_in, pool_gate, attn_gate, attn, pool_w, pool_scale, wout, gf):
    tr = ROWS_MID
    per = tr // HALO
    ng = len(POOL_WINDOWS)

    def body(h_ref, t_ref, pin_ref, halo_ref, pg_ref, ag_ref, at_ref, pw_ref, ps_ref, wout_ref, gf_ref,
             dh2_ref, do_ref, delta_ref, dag_ref, dpg_ref, dpl_ref, dwout_ref, dpw_ref, dps_ref, dgf_ref, loss_ref):
        i = pl.program_id(0)

        @pl.when(i == 0)
        def _():
            dwout_ref[...] = jnp.zeros_like(dwout_ref)
            dpw_ref[...] = jnp.zeros_like(dpw_ref)
            dps_ref[...] = jnp.zeros_like(dps_ref)
            dgf_ref[...] = jnp.zeros_like(dgf_ref)
            loss_ref[...] = jnp.zeros_like(loss_ref)

        row0 = i * tr
        real = (row0 + lax.broadcasted_iota(jnp.int32, (tr, 1), 0)) >= HEAD_ROWS
        h = h_ref[...]

        halo = jnp.where(i > 0, halo_ref[...], 0.0)
        ext = jnp.concatenate([halo, pin_ref[...]], axis=0)
        pooled = []
        for g, w in enumerate(POOL_WINDOWS):
            e = ext[:, g * POOL_GROUP:(g + 1) * POOL_GROUP]
            acc = e
            shift = 1
            while shift < w:
                acc = acc + pltpu.roll(acc, shift, 0)
                shift *= 2
            pooled.append((acc[HALO:] * _inv_count(row0, tr, w) - e[HALO:]).astype(BF16))
        pw = [pw_ref[g].astype(BF16) for g in range(ng)]
        mixed = jnp.concatenate([_nn(pooled[g], pw[g]) for g in range(ng)], axis=1)
        ps = ps_ref[...]
        mixed_s = mixed * ps
        pg = pg_ref[...]
        sig_p = _sigmoid(pg)
        silu_p = pg * sig_p
        pool_out = (silu_p * mixed_s).astype(BF16)
        ag = ag_ref[...]
        sig_a = _sigmoid(ag)
        silu_a = ag * sig_a
        at = at_ref[...]
        attn_out = (silu_a * at).astype(BF16)
        cat = jnp.concatenate([pool_out, attn_out], axis=1)
        h2 = h + _nn(cat, wout_ref[...])

        r2 = lax.rsqrt(jnp.mean(h2 * h2, axis=-1, keepdims=True) + EPS)
        n2 = h2 * r2
        gfv = gf_ref[...]
        err = jnp.where(real, n2 * gfv - t_ref[...], 0.0)
        loss_ref[...] += jnp.sum(jnp.sum(err * err, axis=-1, keepdims=True), axis=0, keepdims=True) * (0.5 / D)
        dy = err * (1.0 / D)
        dgf_ref[...] += jnp.sum(dy * n2, axis=0, keepdims=True)
        dn = dy * gfv
        dh2 = r2 * (dn - n2 * jnp.mean(dn * n2, axis=-1, keepdims=True))
        dh2_ref[...] = dh2
        dh2b = dh2.astype(BF16)

        dwout_ref[...] += _tn(cat, dh2b)
        dcat = _nt(dh2b, wout_ref[...])
        dpo = dcat[:, 0:D_POOL]
        dao = dcat[:, D_POOL:D]
        do = dao * silu_a
        prod = do * at
        delta_ref[...] = jnp.zeros_like(delta_ref)
        for hd in range(HEADS):
            grp, lane = _stat_slot(hd)
            cols = slice(hd * V_HEAD, (hd + 1) * V_HEAD)
            do_ref[grp, :, lane * V_HEAD:(lane + 1) * V_HEAD] = do[:, cols].astype(BF16)
            delta_ref[grp, :, lane:lane + 1] = jnp.sum(prod[:, cols], axis=-1, keepdims=True)
        dag_ref[...] = (dao * at * (sig_a * (1.0 + ag * (1.0 - sig_a)))).astype(BF16)
        dmixed_s = dpo * silu_p
        dpg_ref[...] = (dpo * mixed_s * (sig_p * (1.0 + pg * (1.0 - sig_p)))).astype(BF16)
        dps_ref[...] += jnp.sum(dmixed_s * mixed, axis=0, keepdims=True)
        dmixed = (dmixed_s * ps).astype(BF16)
        dpl = []
        for g in range(ng):
            dm = dmixed[:, g * POOL_GROUP:(g + 1) * POOL_GROUP]
            dpl.append(_nt(dm, pw[g]))
            dpw_ref[g] += _tn(pooled[g], dm)
        dpl_ref[...] = jnp.concatenate(dpl, axis=1)

    halo_spec = pl.BlockSpec((HALO, D_POOL), lambda i: (jnp.maximum(i * per - 1, 0), 0))
    return pl.pallas_call(
        body,
        name="mid",
        grid=(N // tr,),
        in_specs=[
            _rows(D, tr), _rows(D, tr), _rows(D_POOL, tr), halo_spec, _rows(D_POOL, tr), _rows(D_POOL, tr),
            _rows(D_POOL, tr), _const(ng, POOL_GROUP, POOL_GROUP), _const(1, D_POOL), _const(D, D), _const(1, D),
        ],
        out_specs=[
            _rows(D, tr), pl.BlockSpec((STAT_GROUPS, tr, HEADS_PER_STEP_BWD * V_HEAD), lambda i: (0, i, 0)),
            pl.BlockSpec((STAT_GROUPS, tr, 128), lambda i: (0, i, 0)),
            _rows(D_POOL, tr), _rows(D_POOL, tr), _rows(D_POOL, tr),
            _const(D, D), _const(ng, POOL_GROUP, POOL_GROUP), _const(1, D_POOL), _const(1, D), _const(1, 128),
        ],
        out_shape=[
            jax.ShapeDtypeStruct((N, D), F32), jax.ShapeDtypeStruct((STAT_GROUPS, N, HEADS_PER_STEP_BWD * V_HEAD), BF16),
            jax.ShapeDtypeStruct((STAT_GROUPS, N, 128), F32),
            jax.ShapeDtypeStruct((N, D_POOL), BF16), jax.ShapeDtypeStruct((N, D_POOL), BF16),
            jax.ShapeDtypeStruct((N, D_POOL), F32), jax.ShapeDtypeStruct((D, D), F32),
            jax.ShapeDtypeStruct((ng, POOL_GROUP, POOL_GROUP), F32),
            jax.ShapeDtypeStruct((1, D_POOL), F32), jax.ShapeDtypeStruct((1, D), F32), jax.ShapeDtypeStruct((1, 128), F32),
        ],
        compiler_params=_cparams(dimension_semantics=("arbitrary",)),
    )(h, tgt, pool_in, pool_in, pool_gate, attn_gate, attn, pool_w, pool_scale, wout, gf)


def _unrope(dy, cosv, sinv):
    return dy * cosv + _swap64(dy * sinv) * _low_lanes()


def _attn_bwd(q, k, v, do, lse, delta, cosf, sinf, dwout):
    tiles = _attn_tiles()
    hp = HEADS_PER_STEP_BWD
    n_g = HEADS // hp
    n_t = len(tiles)
    half = SHARD_OUT // 2
    send_at, sum_at = (0, 2), (n_g - 1, n_t // 2)

    def body(q_hbm, k_hbm, v_hbm, do_hbm, lse_ref, delta_ref, cos_ref, sin_ref, dwout_hbm, dq_hbm, dkv_ref, dkr_ref,
             gwout_ref, q_buf, k_buf, v_buf, do_buf, dq_buf, dk_acc, dv_acc, own_w, sib_w, stage_w, recv_w, gw_buf,
             in_sems, out_sems, ow_sems, d2d_send, d2d_recv, ici_send, ici_recv, fin_send, fin_recv):
        grp = pl.program_id(0)
        step = pl.program_id(1)
        heads = pl.ds(grp * hp, hp)
        x, y, c = lax.axis_index("x"), lax.axis_index("y"), lax.axis_index("c")
        sibling = (x, y, 1 - c)

        def chip_of(rel):
            fx, fy = _CHIP_RELS[rel]
            return 2 * (x ^ fx) + (y ^ fy)

        def piece(chip, core):
            return dwout_hbm.at[pl.ds(pl.multiple_of(SHARD_OUT * chip + half * core, half), half), :]

        def own_load(rel):
            return pltpu.make_async_copy(piece(chip_of(rel), c), own_w.at[rel], ow_sems.at[rel])

        def d2d_copy(rel):
            return _remote(piece(chip_of(rel), 1 - c), sib_w.at[rel], d2d_send.at[rel], d2d_recv.at[rel], sibling)

        def ici_copy(rel):
            fx, fy = _CHIP_RELS[rel]
            return _remote(stage_w.at[rel - 1], recv_w.at[rel - 1], ici_send.at[rel - 1], ici_recv.at[rel - 1],
                           (x ^ fx, y ^ fy, c))

        def fin_copy(core):
            spot = gw_buf.at[pl.ds(pl.multiple_of(half * core, half), half), :]
            return _remote(spot, spot, fin_send.at[0], fin_recv.at[0], sibling)

        @pl.when((grp == 0) & (step == 0))
        def _():
            _peer_barrier(x, y, c)
            for rel in (1, 2, 3, 0):
                d2d_copy(rel).start()
                own_load(rel).start()

        @pl.when((grp == send_at[0]) & (step == send_at[1]))
        def _():
            for rel in (1, 2, 3):
                own_load(rel).wait()
                d2d_copy(rel).wait_recv()
                stage_w[rel - 1] = (own_w[rel] + sib_w[rel]).astype(BF16)
                ici_copy(rel).start()

        @pl.when((grp == sum_at[0]) & (step == sum_at[1]))
        def _():
            own_load(0).wait()
            d2d_copy(0).wait_recv()
            total = own_w[0] + sib_w[0]
            for rel in (1, 2, 3):
                ici_copy(rel).wait_recv()
                total = total + recv_w[rel - 1].astype(F32)
            gw_buf[pl.ds(pl.multiple_of(half * c, half), half), :] = total
            fin_copy(c).start()

        def finish_dwout():
            fin_copy(1 - c).wait_recv()
            for rel in (0, 1, 2, 3):
                d2d_copy(rel).wait_send()
            for rel in (1, 2, 3):
                ici_copy(rel).wait_send()
            fin_copy(c).wait_send()
            gwout_ref[...] = gw_buf[...]

        def loads(g, idx):
            q0, rows, _ = tiles[idx]
            rs = pl.ds(q0, rows)
            par = (g * n_t + idx) % 2
            hs = pl.ds(g * hp, hp)
            pairs = ((q_hbm.at[hs, rs, :], q_buf.at[:, rs, :]), (k_hbm.at[hs, rs, :], k_buf.at[:, rs, :]),
                     (v_hbm.at[hs, rs, :], v_buf.at[:, rs, :]), (do_hbm.at[g, rs, :], do_buf.at[rs, :]))
            return [pltpu.make_async_copy(src, dst, in_sems.at[a, par]) for a, (src, dst) in enumerate(pairs)]

        def store(idx):
            q0, rows, _ = tiles[idx]
            return pltpu.make_async_copy(dq_buf.at[idx % 2, :, pl.ds(0, rows), :], dq_hbm.at[heads, pl.ds(q0, rows), :],
                                         out_sems.at[idx % 2])

        @pl.when(step == 0)
        def _():
            dk_acc[...] = jnp.zeros_like(dk_acc)
            dv_acc[...] = jnp.zeros_like(dv_acc)

        @pl.when((step == 0) & (grp == 0))
        def _():
            dkr_ref[...] = jnp.zeros_like(dkr_ref)
            for cp in loads(grp, 0):
                cp.start()

        for idx, (q0, rows, klen) in enumerate(tiles):
            @pl.when(step == idx)
            def _(idx=idx, q0=q0, rows=rows, klen=klen):
                for cp in loads(grp, idx):
                    cp.wait()
                if idx + 1 < n_t:
                    for cp in loads(grp, idx + 1):
                        cp.start()
                if idx >= 2:
                    store(idx - 2).wait()
                qs = pl.ds(q0, rows)
                for hd in range(hp):
                    qv = q_buf[hd, qs, :]
                    kv = k_buf[hd, 0:klen, :]
                    p = jnp.exp(_masked_scores(qv, kv, rows, klen) - lse_ref[0, qs, hd:hd + 1])
                    dob = do_buf[qs, hd * V_HEAD:(hd + 1) * V_HEAD]
                    ds = (p * (_nt(dob, v_buf[hd, 0:klen, :]) - delta_ref[0, qs, hd:hd + 1])).astype(BF16)
                    dq = _nn(ds, kv) * SCALE
                    dq_buf[idx % 2, hd, 0:rows, 0:QK_NOPE] = dq[:, 0:QK_NOPE].astype(BF16)
                    dq_buf[idx % 2, hd, 0:rows, QK_NOPE:] = _unrope(dq[:, QK_NOPE:], cos_ref[qs, :], sin_ref[qs, :]).astype(BF16)
                    dk_acc[hd, 0:klen, :] += _tn(ds, qv)
                    dv_acc[hd, 0:klen, :] += _tn(p.astype(BF16), dob)
                store(idx).start()

        @pl.when(step == n_t - 1)
        def _():
            @pl.when(grp + 1 < n_g)
            def _():
                for cp in loads(grp + 1, 0):
                    cp.start()

            for hd in range(hp):
                dkv_ref[hd, :, 0:QK_NOPE] = dk_acc[hd, :, 0:QK_NOPE].astype(BF16)
                dkv_ref[hd, :, QK_NOPE:] = dv_acc[hd].astype(BF16)
                dkr_ref[...] += dk_acc[hd, :, QK_NOPE:]
            store(n_t - 2).wait()
            store(n_t - 1).wait()

            @pl.when(grp == n_g - 1)
            def _():
                finish_dwout()

    hbm = pl.BlockSpec(memory_space=pl.ANY)
    stat = pl.BlockSpec((1, N, 128), lambda g, t: (g, 0, 0), pipeline_mode=pl.Buffered(1))
    piece_f32 = lambda lead: pltpu.VMEM((lead, half, D), F32)
    piece_bf16 = lambda lead: pltpu.VMEM((lead, half, D), BF16)
    return pl.pallas_call(
        body,
        name="attn_bwd",
        grid=(n_g, n_t),
        in_specs=[hbm, hbm, hbm, hbm, stat, stat, _const(N, 128), _const(N, 128), hbm],
        out_specs=[hbm, pl.BlockSpec((hp, N, 256), lambda g, t: (g, 0, 0), pipeline_mode=pl.Buffered(1)), _const(N, 128),
                   _const(SHARD_OUT, D)],
        out_shape=[
            jax.ShapeDtypeStruct((HEADS, N, 256), BF16), jax.ShapeDtypeStruct((HEADS, N, 256), BF16),
            jax.ShapeDtypeStruct((N, 128), F32), jax.ShapeDtypeStruct((SHARD_OUT, D), F32),
        ],
        scratch_shapes=[pltpu.VMEM((hp, N, 256), BF16), pltpu.VMEM((hp, N, 256), BF16), pltpu.VMEM((hp, N, V_HEAD), BF16),
                        pltpu.VMEM((N, hp * V_HEAD), BF16), pltpu.VMEM((2, hp, TQ, 256), BF16),
                        pltpu.VMEM((hp, N, 256), F32), pltpu.VMEM((hp, N, V_HEAD), F32),
                        piece_f32(CHIPS), piece_f32(CHIPS), piece_bf16(3), piece_bf16(3), pltpu.VMEM((SHARD_OUT, D), F32),
                        pltpu.SemaphoreType.DMA((4, 2)), pltpu.SemaphoreType.DMA((2,)), pltpu.SemaphoreType.DMA((CHIPS,)),
                        pltpu.SemaphoreType.DMA((CHIPS,)), pltpu.SemaphoreType.DMA((CHIPS,)),
                        pltpu.SemaphoreType.DMA((3,)), pltpu.SemaphoreType.DMA((3,)),
                        pltpu.SemaphoreType.DMA((1,)), pltpu.SemaphoreType.DMA((1,))],
        compiler_params=_cparams(dimension_semantics=("arbitrary", "arbitrary"), collective_id=2),
    )(q, k, v, do, lse, delta, cosf, sinf, dwout)


def _bwd_in(h, dh2, dq, dkv, dkr, cq, ckv, dpl, dpg, dag, norm_g, win, gq, wq, gkv, wkv, cosf, sinf, adam_out):
    tr = ROWS_BWD
    nb = N // tr
    per = tr // HALO
    lead = HEAD_ROWS
    adam_rows = SHARD_OUT // nb

    def body(h_ref, dh2_ref, dq_ref, dkv_ref, dkr_ref, cq_ref, ckv_ref, dpl_ref, halo_ref, dpg_ref, dag_ref,
             g_ref, win_ref, gq_ref, wq_ref, gkv_ref, wkv_ref, cos_ref, sin_ref, aw_ref, ag_ref, am_ref, av_ref,
             gx_ref, dmeta_ref, du_ref, hn_ref, dwq_ref, dwkv_ref, dg_ref, dgq_ref, dgkv_ref, ago_ref, ad_ref, anm_ref, anv_ref,
             dh_buf, gx_sem):
        i = pl.program_id(0)
        grad_out = ag_ref[...]
        ago_ref[...] = grad_out
        ad_ref[...], anm_ref[...], anv_ref[...] = _adamw_math(aw_ref[...], grad_out, am_ref[...], av_ref[...])

        @pl.when(i == 0)
        def _():
            dwq_ref[...] = jnp.zeros_like(dwq_ref)
            dwkv_ref[...] = jnp.zeros_like(dwkv_ref)
            dg_ref[...] = jnp.zeros_like(dg_ref)
            dgq_ref[...] = jnp.zeros_like(dgq_ref)
            dgkv_ref[...] = jnp.zeros_like(dgkv_ref)

        row0 = i * tr
        h = h_ref[...]
        r = lax.rsqrt(jnp.mean(h * h, axis=-1, keepdims=True) + EPS)
        n = h * r
        gv = g_ref[...]
        hn = (n * gv).astype(BF16)
        cq = cq_ref[...]
        rq = lax.rsqrt(jnp.mean(cq * cq, axis=-1, keepdims=True) + EPS)
        nq = cq * rq
        gqv = gq_ref[...]
        cqn = (nq * gqv).astype(BF16)
        dcqn = jnp.zeros((tr, Q_LORA), F32)
        for hd in range(HEADS):
            dqf = dq_ref[hd]
            dcqn = dcqn + _nn(dqf, wq_ref[hd])
            dwq_ref[hd] += _tn(dqf, cqn)
        dgq_ref[...] += jnp.sum(dcqn * nq, axis=0, keepdims=True)
        dnq = dcqn * gqv
        dcq = rq * (dnq - nq * jnp.mean(dnq * nq, axis=-1, keepdims=True))

        ckv = ckv_ref[...]
        rkv = lax.rsqrt(jnp.mean(ckv * ckv, axis=-1, keepdims=True) + EPS)
        nkv = ckv * rkv
        gkvv = gkv_ref[...]
        ckvn = (nkv * gkvv).astype(BF16)
        dckvn = jnp.zeros((tr, KV_LORA), F32)
        for hd in range(HEADS):
            dkv = dkv_ref[hd]
            dckvn = dckvn + _nt(dkv, wkv_ref[hd])
            dwkv_ref[hd] += _tn(ckvn, dkv)
        dgkv_ref[...] += jnp.sum(dckvn * nkv, axis=0, keepdims=True)
        dnkv = dckvn * gkvv
        dckv = rkv * (dnkv - nkv * jnp.mean(dnkv * nkv, axis=-1, keepdims=True))
        dkr = _unrope(dkr_ref[...], cos_ref[...], sin_ref[...])

        cur = dpl_ref[...]
        halo = jnp.where(i < nb - 1, halo_ref[...], 0.0)
        dpi = []
        for g, w in enumerate(POOL_WINDOWS):
            sl = slice(g * POOL_GROUP, (g + 1) * POOL_GROUP)
            a = jnp.concatenate([cur[:, sl] * _inv_count(row0, tr, w), halo[:, sl] * _inv_count(row0 + tr, HALO, w)], axis=0)
            acc = a
            shift = 1
            while shift < w:
                acc = acc + pltpu.roll(acc, tr + HALO - shift, 0)
                shift *= 2
            dpi.append(acc[0:tr] - cur[:, sl])

        du = jnp.concatenate([t.astype(BF16) for t in dpi] + [dpg_ref[...]] + [t.astype(BF16) for t in (dcq, dckv, dkr)],
                             axis=1)
        dagb = dag_ref[...]
        du_ref[...] = du
        hn_ref[...] = hn
        dhn = _nn(du, win_ref[0:O_KR_END, :]) + _nn(dagb, win_ref[O_AG:D_IN, :])
        dg_ref[...] += jnp.sum(dhn * n, axis=0, keepdims=True)
        dn = dhn * gv
        dh = dh2_ref[...] + r * (dn - n * jnp.mean(dn * n, axis=-1, keepdims=True))

        first = pltpu.make_async_copy(dh_buf.at[pl.ds(lead, tr - lead), :], gx_ref.at[pl.ds(0, tr - lead), :], gx_sem)
        later = lambda step: pltpu.make_async_copy(
            dh_buf, gx_ref.at[pl.ds(pl.multiple_of(step * tr - lead, 16), tr), :], gx_sem)

        @pl.when(i == 1)
        def _():
            first.wait()

        @pl.when(i > 1)
        def _():
            later(i - 1).wait()

        dh_buf[...] = dh

        @pl.when(i == 0)
        def _():
            first.start()
            for chip in range(CHIPS):
                dmeta_ref[chip] = dh[PAD:HEAD_ROWS, chip * 256:(chip + 1) * 256]

        @pl.when(i > 0)
        def _():
            later(i).start()

        @pl.when(i == nb - 1)
        def _():
            later(i).wait()

    head = lambda w: pl.BlockSpec((HEADS, tr, w), lambda i: (0, i, 0))
    halo_spec = pl.BlockSpec((HALO, D_POOL), lambda i: (jnp.minimum((i + 1) * per, N // HALO - 1), 0))
    return pl.pallas_call(
        body,
        name="bwd_in",
        grid=(nb,),
        in_specs=[
            _rows(D, tr), _rows(D, tr), head(256), head(256), _rows(128, tr), _rows(Q_LORA, tr), _rows(KV_LORA, tr),
            _rows(D_POOL, tr), halo_spec, _rows(D_POOL, tr), _rows(D_POOL, tr),
            _const(1, D), _const(D_IN, D), _const(1, Q_LORA), _const(HEADS, 256, Q_LORA),
            _const(1, KV_LORA), _const(HEADS, KV_LORA, 256), _rows(128, tr), _rows(128, tr),
        ] + [_rows(D, adam_rows)] * 4,
        out_specs=[
            pl.BlockSpec(memory_space=pl.ANY), _const(CHIPS, N_META, 256), _rows(O_KR_END, tr), _rows(D, tr),
            _const(HEADS, 256, Q_LORA),
            _const(HEADS, KV_LORA, 256), _const(1, D), _const(1, Q_LORA), _const(1, KV_LORA),
        ] + [_rows(D, adam_rows)] * 4,
        out_shape=[
            jax.ShapeDtypeStruct((S, D), F32), jax.ShapeDtypeStruct((CHIPS, N_META, 256), F32),
            jax.ShapeDtypeStruct((N, O_KR_END), BF16), jax.ShapeDtypeStruct((N, D), BF16),
            jax.ShapeDtypeStruct((HEADS, 256, Q_LORA), F32),
            jax.ShapeDtypeStruct((HEADS, KV_LORA, 256), F32),
            jax.ShapeDtypeStruct((1, D), F32), jax.ShapeDtypeStruct((1, Q_LORA), F32), jax.ShapeDtypeStruct((1, KV_LORA), F32),
        ] + [jax.ShapeDtypeStruct((SHARD_OUT, D), F32)] * 4,
        scratch_shapes=[pltpu.VMEM((tr, D), F32), pltpu.SemaphoreType.DMA],
        compiler_params=_cparams(dimension_semantics=("arbitrary",)),
    )(h, dh2, dq, dkv, dkr, cq, ckv, dpl, dpl, dpg, dag, norm_g, win, gq, wq, gkv, wkv, cosf, sinf, *adam_out)


def _local_step(h, tgt, norm_g, win, gq, wq, gkv, wkv, pool_w, pool_scale, wout_s, m_wout_s, v_wout_s, gf, cosf, sinf):
    pool_in, pool_gate, cq, ckv, attn_gate, q, k, v = _fwd_in(h, norm_g, win, gq, wq, gkv, wkv, cosf, sinf)
    attn, lse, wout = _attn_fwd(q, k, v, wout_s)
    dh2, do, delta, dag, dpg, dpl, dwout, dpw, dps, dgf, loss = _mid(
        h, tgt, pool_in, pool_gate, attn_gate, attn, pool_w, pool_scale, wout, gf)
    dq, dkv, dkr, gwout = _attn_bwd(q, k, v, do, lse, delta, cosf, sinf, dwout)
    gx, dmeta, du, hn, dwq, dwkv, dg, dgq, dgkv, *r_out = _bwd_in(
        h, dh2, dq, dkv, dkr, cq, ckv, dpl, dpg, dag, norm_g, win, gq, wq, gkv, wkv, cosf, sinf,
        (wout_s, gwout, m_wout_s, v_wout_s))
    return dict(gx=gx, dmeta=dmeta, du=du, dag=dag, hn=hn, dwq=dwq, dwkv=dwkv, r_out=tuple(r_out), dg=dg, dgq=dgq,
                dgkv=dgkv, dpw=dpw, dps=dps, dgf=dgf, loss=loss)


_CHIP_RELS = ((0, 0), (1, 0), (0, 1), (1, 1))

_ARR_ROWS = (SHARD_IN, SHARD_OUT, 256, KV_LORA, N_META)
_ARR_COLS = (D, D, Q_LORA, 256, 256)
_PIECES = (
    (0, 0, 256, 0), (0, 256, SHARD_IN - 256, 1),
    (1, 0, 128, 0), (1, 128, 128, 1),
    (2, 0, 128, 0), (2, 128, 128, 1),
    (3, 0, 64, 0), (3, 64, 64, 1),
    (4, 0, N_META, 0),
)
_NP = len(_PIECES)
_PIECE_MAX = (256, 128, 128, 64, N_META)


def _gathered_at(refs, arr, chip, r0, n):
    if arr in (0, 1):
        return refs[arr].at[pl.ds(pl.multiple_of(_ARR_ROWS[arr] * chip + r0, 16), n), :]
    return refs[arr].at[chip, pl.ds(r0, n), :]


def _remote(src, dst, send_sem, recv_sem, to):
    return pltpu.make_async_remote_copy(src_ref=src, dst_ref=dst, send_sem=send_sem, recv_sem=recv_sem,
                                        device_id=to, device_id_type=MESH)


def _gather_weights(winT_s, wqT_s, wkv_s, meta_s, x2, tgt2):
    arrays = (0, 2, 3, 4)

    def body(win_ref, wq_ref, wkv_ref, meta_ref, x_ref, t_ref, win_o, wq_o, wkv_o, h_o, tp_o,
             s_win, s_wq, s_wkv, meta_all, head_buf, x_buf, t_buf, ici_send, ici_recv, fwd_send, fwd_recv,
             loc_sems, own_sems):
        x, y, c = lax.axis_index("x"), lax.axis_index("y"), lax.axis_index("c")
        me = 2 * x + y
        stage = (s_win, None, s_wq, s_wkv, meta_ref)
        outs = (win_o, None, wq_o, wkv_o, meta_all)

        _peer_barrier(x, y, c)

        frames = pl.ds(HEAD_ROWS, S)
        loads = [pltpu.make_async_copy(x_ref, x_buf, loc_sems.at[0]), pltpu.make_async_copy(t_ref, t_buf, loc_sems.at[1])]
        local = [pltpu.make_async_copy(x_buf, h_o.at[frames, :], loc_sems.at[0]),
                 pltpu.make_async_copy(t_buf, tp_o.at[frames, :], loc_sems.at[1])]
        for cp in loads:
            cp.start()

        s_win[...] = win_ref[...].astype(BF16)
        s_wq[0:QK, :] = wq_ref[...].astype(BF16)
        s_wq[QK:256, :] = jnp.zeros((256 - QK, Q_LORA), BF16)
        s_wkv[...] = wkv_ref[...].astype(BF16)

        def chip_of(rel):
            fx, fy = _CHIP_RELS[rel]
            return 2 * (x ^ fx) + (y ^ fy)

        def same_core_of(rel):
            fx, fy = _CHIP_RELS[rel]
            return (x ^ fx, y ^ fy, c)

        def ici_copy(rel, i, src_chip, to):
            arr, r0, n, _ = _PIECES[i]
            k = (rel - 1) * _NP + i
            return _remote(stage[arr].at[pl.ds(r0, n), :], _gathered_at(outs, arr, src_chip, r0, n),
                           ici_send.at[k], ici_recv.at[k], to)

        def fwd_copy(rel, i, to):
            arr, r0, n, _ = _PIECES[i]
            k = (rel - 1) * _NP + i
            place = _gathered_at(outs, arr, chip_of(rel), r0, n)
            return _remote(place, place, fwd_send.at[k], fwd_recv.at[k], to)

        for core in (0, 1):
            @pl.when(c == core)
            def _(core=core):
                mine = [i for i in range(_NP) if _PIECES[i][3] == core and _PIECES[i][0] in arrays]
                theirs = [i for i in range(_NP) if _PIECES[i][3] != core and _PIECES[i][0] in arrays]
                sends = [ici_copy(rel, i, me, same_core_of(rel)) for rel in (1, 2, 3) for i in mine]
                for cp in sends:
                    cp.start()
                for ld, st in zip(loads, local):
                    ld.wait()
                    st.start()
                own = [pltpu.make_async_copy(stage[arr], _gathered_at(outs, arr, me, 0, _ARR_ROWS[arr]), own_sems.at[arr])
                       for arr in arrays if arr != 4]
                for cp in own:
                    cp.start()
                meta_all[me] = meta_ref[...]
                for rel in (1, 2, 3):
                    for i in mine:
                        ici_copy(rel, i, chip_of(rel), (x, y, c)).wait_recv()
                        fwd = fwd_copy(rel, i, (x, y, 1 - c))
                        fwd.start()
                        sends.append(fwd)
                for rel in (1, 2, 3):
                    for i in theirs:
                        fwd_copy(rel, i, (x, y, c)).wait_recv()
                for cp in sends:
                    cp.wait_send()
                for cp in own:
                    cp.wait()

        head_buf[...] = jnp.zeros_like(head_buf)
        zeros = pltpu.make_async_copy(head_buf, tp_o.at[pl.ds(0, HEAD_ROWS), :], loc_sems.at[2])
        zeros.start()
        zeros.wait()
        for chip in range(CHIPS):
            head_buf[PAD:HEAD_ROWS, chip * 256:(chip + 1) * 256] = meta_all[chip]
        head = pltpu.make_async_copy(head_buf, h_o.at[pl.ds(0, HEAD_ROWS), :], loc_sems.at[2])
        head.start()
        head.wait()
        for cp in local:
            cp.wait()

    vm = pl.BlockSpec(memory_space=pltpu.VMEM)
    hbm = pl.BlockSpec(memory_space=pl.ANY)
    return pl.pallas_call(
        body,
        name="gather_weights",
        in_specs=[vm] * 4 + [hbm] * 2,
        out_specs=[hbm] * 5,
        out_shape=[
            jax.ShapeDtypeStruct((D_IN, D), BF16),
            jax.ShapeDtypeStruct((CHIPS, 256, Q_LORA), BF16), jax.ShapeDtypeStruct((CHIPS, KV_LORA, 256), BF16),
            jax.ShapeDtypeStruct((N, D), F32), jax.ShapeDtypeStruct((N, D), F32),
        ],
        scratch_shapes=[pltpu.VMEM((_ARR_ROWS[a], _ARR_COLS[a]), BF16) for a in (0, 2, 3)]
        + [pltpu.VMEM((CHIPS, N_META, 256), F32), pltpu.VMEM((HEAD_ROWS, D), F32), pltpu.VMEM((S, D), F32),
           pltpu.VMEM((S, D), F32)]
        + [pltpu.SemaphoreType.DMA((3 * _NP,))] * 4 + [pltpu.SemaphoreType.DMA((3,)), pltpu.SemaphoreType.DMA((4,))],
        compiler_params=_cparams(collective_id=0),
    )(winT_s, wqT_s, wkv_s, meta_s, x2, tgt2)


_SM_ROWS = (len(POOL_WINDOWS) * POOL_GROUP, VEC_ROWS)
_SM_COLS = (POOL_GROUP, D)
_SM_PIECES = ((0, 0, 256, 0), (0, 256, 256, 1), (1, 0, VEC_ROWS, 0))
_NSP = len(_SM_PIECES)


def _reduce_grads(du, dag, hn, dwq, dwkv, dmeta4, dpw, dg, dgf, dgq, dgkv, dps, loss):
    arrays = (0, 2, 3, 4)
    loaded = (2, 3, 4)
    blocks = ([(0, 256), (256, 512)], [(512, 768), (768, 1024)], [(1024, 1280), (1280, O_AG), (O_AG, O_AG + 256)],
              [(O_AG + 256, D_IN)])

    def body(du_hbm, dag_hbm, hn_hbm, dwq_ref, dwkv_ref, dmeta_ref, dpw_ref, dg_ref, dgf_ref, dgq_ref, dgkv_ref, dps_ref,
             loss_ref, gwin_o, gwq_o, gwkv_o, gmeta_o, gpw_o, gg_o, ggf_o, ggq_o, ggkv_o, gps_o, gloss_o,
             ow2, ow3, ow4, sb0, sb2, sb3, sb4, st0, st2, st3, st4, rc0, rc2, rc3, rc4,
             vec, sm_sb0, sm_sb1, sm_cs0, sm_cs1, sm_rc0, sm_rc1, vec_fin, du_v, dag_v, hn_v, dwin_buf, own0,
             own_sems, d2d_send, d2d_recv, ici_send, ici_recv, fin_send, fin_recv,
             swap_send, swap_recv, smi_send, smi_recv, smf_send, smf_recv, ld_sems):
        x, y, c = lax.axis_index("x"), lax.axis_index("y"), lax.axis_index("c")
        me = 2 * x + y
        operands = [pltpu.make_async_copy(src, dst, ld_sems.at[t])
                    for t, (src, dst) in enumerate(((du_hbm, du_v), (dag_hbm, dag_v), (hn_hbm, hn_v)))]
        for cp in operands:
            cp.start()
        grads = (None, None, dwq_ref, dwkv_ref, dmeta_ref)
        outs = (gwin_o, None, gwq_o, gwkv_o, gmeta_o)
        own_buf = (None, None, ow2, ow3, ow4)
        sib_buf = (sb0, None, sb2, sb3, sb4)
        stage = (st0, None, st2, st3, st4)
        recv = (rc0, None, rc2, rc3, rc4)
        sm_mine = (dpw_ref, vec)
        sm_sib = (sm_sb0, sm_sb1)
        sm_chip = (sm_cs0, sm_cs1)
        sm_recv = (sm_rc0, sm_rc1)
        sm_out = (gpw_o, vec_fin)
        sibling = (x, y, 1 - c)

        def chip_of(rel):
            fx, fy = _CHIP_RELS[rel]
            return 2 * (x ^ fx) + (y ^ fy)

        def same_core_of(rel):
            fx, fy = _CHIP_RELS[rel]
            return (x ^ fx, y ^ fy, c)

        def slot(bufs, i, idx):
            arr, _, n, _ = _PIECES[i]
            return bufs[arr].at[idx, pl.ds(0, n), :]

        def own_load(rel, i):
            arr, r0, n, _ = _PIECES[i]
            return pltpu.make_async_copy(_gathered_at(grads, arr, chip_of(rel), r0, n), slot(own_buf, i, rel),
                                         own_sems.at[rel * _NP + i])

        def d2d_copy(rel, i):
            arr, r0, n, _ = _PIECES[i]
            k = rel * _NP + i
            return _remote(_gathered_at(grads, arr, chip_of(rel), r0, n), slot(sib_buf, i, rel),
                           d2d_send.at[k], d2d_recv.at[k], sibling)

        def ici_copy(rel, i):
            k = (rel - 1) * _NP + i
            return _remote(slot(stage, i, rel - 1), slot(recv, i, rel - 1), ici_send.at[k], ici_recv.at[k],
                           same_core_of(rel))

        def fin_copy(i):
            arr, r0, n, _ = _PIECES[i]
            place = outs[arr].at[pl.ds(r0, n), :]
            return _remote(place, place, fin_send.at[i], fin_recv.at[i], sibling)

        def sm_ici_copy(rel, j):
            blk, r0, n, _ = _SM_PIECES[j]
            k = (rel - 1) * _NSP + j
            return _remote(sm_chip[blk].at[pl.ds(r0, n), :], sm_recv[blk].at[rel - 1, pl.ds(r0, n), :],
                           smi_send.at[k], smi_recv.at[k], same_core_of(rel))

        def sm_fin_copy(j):
            blk, r0, n, _ = _SM_PIECES[j]
            place = sm_out[blk].at[pl.ds(r0, n), :]
            return _remote(place, place, smf_send.at[j], smf_recv.at[j], sibling)

        _peer_barrier(x, y, c)

        vec[...] = jnp.zeros_like(vec)
        vec[0:1, :] = dg_ref[...]
        vec[1:2, :] = dgf_ref[...]
        vec[2:3, V_GQ:V_GQ + Q_LORA] = dgq_ref[...]
        vec[2:3, V_GKV:V_GKV + KV_LORA] = dgkv_ref[...]
        vec[2:3, V_PS:V_PS + D_POOL] = dps_ref[...]
        vec[2:3, V_LOSS:D] = loss_ref[...]
        swaps = [_remote(sm_mine[b], sm_sib[b], swap_send.at[b], swap_recv.at[b], sibling) for b in (0, 1)]
        for cp in swaps:
            cp.start()

        for core in (0, 1):
            @pl.when(c == core)
            def _(core=core):
                mine = [i for i in range(_NP) if _PIECES[i][3] == core and _PIECES[i][0] in loaded]
                theirs = [i for i in range(_NP) if _PIECES[i][3] != core and _PIECES[i][0] in loaded]
                i0 = next(i for i in range(_NP) if _PIECES[i][0] == 0 and _PIECES[i][3] == core)
                j0 = next(i for i in range(_NP) if _PIECES[i][0] == 0 and _PIECES[i][3] != core)
                sm_mine_p = [j for j in range(_NSP) if _SM_PIECES[j][3] == core]
                sm_theirs_p = [j for j in range(_NSP) if _SM_PIECES[j][3] != core]
                sends = list(swaps)

                for rel in (1, 2, 3, 0):
                    for i in theirs:
                        cp = d2d_copy(rel, i)
                        cp.start()
                        sends.append(cp)
                    for i in mine:
                        own_load(rel, i).start()

                def rel_of(chip):
                    flips = chip ^ me
                    return jnp.where(flips == 2, 1, jnp.where(flips == 1, 2, flips))

                def shard_rows(chip, i):
                    return pl.ds(SHARD_IN * chip + _PIECES[i][1], _PIECES[i][2])

                def d2d0(chip, i):
                    rel = rel_of(chip)
                    return _remote(dwin_buf.at[shard_rows(chip, i), :], slot(sib_buf, i, rel),
                                   d2d_send.at[rel * _NP + i], d2d_recv.at[rel * _NP + i], sibling)

                def ici0(chip):
                    slot_idx = jnp.maximum(rel_of(chip) - 1, 0)
                    return _remote(slot(stage, i0, slot_idx), slot(recv, i0, slot_idx), ici_send.at[slot_idx * _NP + i0],
                                   ici_recv.at[slot_idx * _NP + i0], (chip // 2, chip % 2, c))

                def settle(chip):
                    d2d0(chip, i0).wait_recv()
                    total = dwin_buf[shard_rows(chip, i0), :] + slot(sib_buf, i0, rel_of(chip))[...]

                    @pl.when(chip != me)
                    def _():
                        slot(stage, i0, jnp.maximum(rel_of(chip) - 1, 0))[...] = total.astype(BF16)
                        ici0(chip).start()

                    @pl.when(chip == me)
                    def _():
                        own0[0:_PIECES[i0][2], :] = total

                for cp in operands:
                    cp.wait()
                for chip in range(CHIPS):
                    for lo, hi in blocks[chip]:
                        if lo < O_AG:
                            dwin_buf[lo:hi, :] = _tn(du_v[:, lo:lo + 256], hn_v[...])[0:hi - lo, :]
                        else:
                            dwin_buf[lo:hi, :] = _tn(dag_v[:, lo - O_AG:hi - O_AG], hn_v[...])
                    cp = d2d0(chip, j0)
                    cp.start()
                    sends.append(cp)
                    if chip > 0:
                        settle(chip - 1)
                settle(CHIPS - 1)

                for rel in (1, 2, 3):
                    for i in mine:
                        arr, r0, n, _ = _PIECES[i]
                        own_load(rel, i).wait()
                        d2d_copy(rel, i).wait_recv()
                        total = slot(own_buf, i, rel)[...] + slot(sib_buf, i, rel)[...]
                        slot(stage, i, rel - 1)[...] = total.astype(stage[arr].dtype)
                        cp = ici_copy(rel, i)
                        cp.start()
                        sends.append(cp)

                for b in (0, 1):
                    swaps[b].wait_recv()
                    sm_chip[b][...] = sm_mine[b][...] + sm_sib[b][...]
                for rel in (1, 2, 3):
                    for j in sm_mine_p:
                        cp = sm_ici_copy(rel, j)
                        cp.start()
                        sends.append(cp)

                for i in mine:
                    arr, r0, n, _ = _PIECES[i]
                    own_load(0, i).wait()
                    d2d_copy(0, i).wait_recv()
                    total = slot(own_buf, i, 0)[...] + slot(sib_buf, i, 0)[...]
                    for rel in (1, 2, 3):
                        ici_copy(rel, i).wait_recv()
                        total = total + slot(recv, i, rel - 1)[...].astype(F32)
                    outs[arr][pl.ds(r0, n), :] = total
                    cp = fin_copy(i)
                    cp.start()
                    sends.append(cp)
                total = own0[0:_PIECES[i0][2], :]
                for rel in (1, 2, 3):
                    ici_copy(rel, i0).wait_recv()
                    total = total + slot(recv, i0, rel - 1)[...].astype(F32)
                outs[0][pl.ds(_PIECES[i0][1], _PIECES[i0][2]), :] = total
                cp = fin_copy(i0)
                cp.start()
                sends.append(cp)

                for j in sm_mine_p:
                    blk, r0, n, _ = _SM_PIECES[j]
                    for rel in (1, 2, 3):
                        sm_ici_copy(rel, j).wait_recv()
                    total = jnp.zeros((n, _SM_COLS[blk]), F32)
                    for chip in range(CHIPS):
                        flips = chip ^ me
                        rel = jnp.where(flips == 2, 1, jnp.where(flips == 1, 2, flips))
                        theirs_rows = sm_recv[blk][jnp.maximum(rel - 1, 0), pl.ds(r0, n), :]
                        total = total + jnp.where(rel == 0, sm_chip[blk][pl.ds(r0, n), :], theirs_rows)
                    sm_out[blk][pl.ds(r0, n), :] = total
                    cp = sm_fin_copy(j)
                    cp.start()
                    sends.append(cp)

                for i in theirs + [j0]:
                    fin_copy(i).wait_recv()
                for j in sm_theirs_p:
                    sm_fin_copy(j).wait_recv()
                for cp in sends:
                    cp.wait_send()
                for chip in range(CHIPS):
                    @pl.when(chip != me)
                    def _(chip=chip):
                        ici0(chip).wait_send()

        gg_o[...] = vec_fin[0:1, :]
        ggf_o[...] = vec_fin[1:2, :]
        ggq_o[...] = vec_fin[2:3, V_GQ:V_GQ + Q_LORA]
        ggkv_o[...] = vec_fin[2:3, V_GKV:V_GKV + KV_LORA]
        gps_o[...] = vec_fin[2:3, V_PS:V_PS + D_POOL]
        gloss_o[...] = vec_fin[2:3, V_LOSS:D]

    vm = pl.BlockSpec(memory_space=pltpu.VMEM)
    piece_buf = lambda lead, dtype, which=arrays: [
        pltpu.VMEM((lead, _PIECE_MAX[a], _ARR_COLS[a]), F32 if a == 4 else dtype) for a in which]
    sm_buf = lambda *lead: [pltpu.VMEM(lead + (_SM_ROWS[b], _SM_COLS[b]), F32) for b in (0, 1)]
    dma = lambda n: [pltpu.SemaphoreType.DMA((n,))] * 2
    return pl.pallas_call(
        body,
        name="reduce_grads",
        in_specs=[pl.BlockSpec(memory_space=pl.ANY)] * 5 + [vm] * 8,
        out_specs=[vm] * 11,
        out_shape=[jax.ShapeDtypeStruct((_ARR_ROWS[a], _ARR_COLS[a]), F32) for a in arrays]
        + [jax.ShapeDtypeStruct((_SM_ROWS[0], _SM_COLS[0]), F32), jax.ShapeDtypeStruct((1, D), F32),
           jax.ShapeDtypeStruct((1, D), F32), jax.ShapeDtypeStruct((1, Q_LORA), F32),
           jax.ShapeDtypeStruct((1, KV_LORA), F32), jax.ShapeDtypeStruct((1, D_POOL), F32),
           jax.ShapeDtypeStruct((1, 128), F32)],
        scratch_shapes=piece_buf(CHIPS, F32, loaded) + piece_buf(CHIPS, F32) + piece_buf(3, BF16) + piece_buf(3, BF16)
        + [pltpu.VMEM((VEC_ROWS, D), F32)] + sm_buf() + sm_buf() + sm_buf(3) + [pltpu.VMEM((VEC_ROWS, D), F32)]
        + [pltpu.VMEM((N, O_KR_END), BF16), pltpu.VMEM((N, D_POOL), BF16), pltpu.VMEM((N, D), BF16),
           pltpu.VMEM((D_IN, D), F32), pltpu.VMEM((_PIECE_MAX[0], D), F32)]
        + [pltpu.SemaphoreType.DMA((CHIPS * _NP,))]
        + dma(CHIPS * _NP) + dma(3 * _NP) + dma(_NP) + dma(2) + dma(3 * _NSP) + dma(_NSP)
        + [pltpu.SemaphoreType.DMA((3,))],
        compiler_params=_cparams(collective_id=3),
    )(du, dag, hn, dwq, dwkv, dmeta4, dpw, dg, dgf, dgq, dgkv, dps, loss)


def _adamw_math(w, g, m, v):
    m = B1 * m + (1.0 - B1) * g
    v = B2 * v + (1.0 - B2) * (g * g)
    m_hat = m / C1
    v_hat = v / C2
    delta = -LR * (m_hat / (jnp.sqrt(v_hat) + ADAM_EPS) + WD * w)
    return delta, m, v


def _adamw_rows(name, w, g, m, v, block_rows):
    rows, cols = w.shape

    def body(w_ref, g_ref, m_ref, v_ref, go_ref, d_ref, nm_ref, nv_ref):
        g = g_ref[...]
        go_ref[...] = g
        d_ref[...], nm_ref[...], nv_ref[...] = _adamw_math(w_ref[...], g, m_ref[...], v_ref[...])

    spec = pl.BlockSpec((block_rows, cols), lambda i: (i, 0))
    return pl.pallas_call(
        body,
        name=name,
        grid=(rows // block_rows,),
        in_specs=[spec] * 4,
        out_specs=[spec] * 4,
        out_shape=[jax.ShapeDtypeStruct(w.shape, F32)] * 4,
        compiler_params=_cparams(dimension_semantics=("arbitrary",)),
    )(w, g, m, v)


def _adamw_small(groups):
    n = len(groups)

    def body(*refs):
        ins, outs = refs[:4 * n], refs[4 * n:]
        for t in range(n):
            w_ref, g_ref, m_ref, v_ref = ins[4 * t:4 * t + 4]
            g = g_ref[0:w_ref.shape[0], :]
            outs[4 * t][...] = g
            outs[4 * t + 1][...], outs[4 * t + 2][...], outs[4 * t + 3][...] = _adamw_math(
                w_ref[...], g, m_ref[...], v_ref[...])

    vm = pl.BlockSpec(memory_space=pltpu.VMEM)
    flat = [a for grp in groups for a in grp]
    outs = pl.pallas_call(
        body,
        name="adamw_small",
        in_specs=[vm] * (4 * n),
        out_specs=[vm] * (4 * n),
        out_shape=[jax.ShapeDtypeStruct(grp[0].shape, F32) for grp in groups for _ in range(4)],
        compiler_params=_cparams(),
    )(*flat)
    return [tuple(outs[4 * t:4 * t + 4]) for t in range(n)]


def _rope_tables():
    half = QK_ROPE // 2
    f32 = np.float32
    inv_freq = (f32(1.0) / (f32(ROPE_THETA) ** (np.arange(half, dtype=f32) / f32(half)))).astype(f32)
    pos = np.arange(N, dtype=f32) - f32(PAD)
    ang = (pos[:, None] * inv_freq[None, :]).astype(f32)
    cos, sin = np.cos(ang).astype(f32), np.sin(ang).astype(f32)
    zero = np.zeros((N, 128 - QK_ROPE), f32)
    return jnp.asarray(np.concatenate([cos, cos, zero], axis=1)), jnp.asarray(np.concatenate([-sin, sin, zero], axis=1))


def kernel(x, meta_tokens, norm_g, w_in, q_norm_g, w_q_b, kv_norm_g, w_kv_b, pool_w, pool_scale, w_out, final_norm_g, loss_target, m_meta_tokens, m_norm_g, m_w_in, m_q_norm_g, m_w_q_b, m_kv_norm_g, m_w_kv_b, m_pool_w, m_pool_scale, m_w_out, m_final_norm_g, v_meta_tokens, v_norm_g, v_w_in, v_q_norm_g, v_w_q_b, v_kv_norm_g, v_w_kv_b, v_pool_w, v_pool_scale, v_w_out, v_final_norm_g):
    tr = lambda a: a[0].T
    win, wq, wkv, h, tgt = _gather_weights(tr(w_in), tr(w_q_b), w_kv_b[0], meta_tokens, x[0], loss_target[0])
    cosf, sinf = _rope_tables()
    gf = final_norm_g.reshape(1, D)

    part = _local_step(h, tgt, norm_g, win, q_norm_g, wq, kv_norm_g, wkv, pool_w[0], pool_scale, w_out[0], m_w_out[0],
                       v_w_out[0], gf, cosf, sinf)

    pw2 = lambda a: a.reshape(len(POOL_WINDOWS) * POOL_GROUP, POOL_GROUP)
    gwinT, gwqT, gwkv, gmeta, gpw, gg, ggf, ggq, ggkv, gps, gloss = _reduce_grads(
        part["du"], part["dag"], part["hn"], part["dwq"], part["dwkv"], part["dmeta"], pw2(part["dpw"]), part["dg"],
        part["dgf"], part["dgq"], part["dgkv"], part["dps"], part["loss"])

    r_in = _adamw_rows("adamw_w_in", tr(w_in), gwinT, tr(m_w_in), tr(v_w_in), 248)
    r_out = part["r_out"]
    fn2 = lambda a: a.reshape(1, D)
    r_meta, r_norm, r_gq, r_wq, r_gkv, r_wkv, r_pw, r_ps, r_fn = _adamw_small([
        (meta_tokens, gmeta, m_meta_tokens, v_meta_tokens),
        (norm_g, gg, m_norm_g, v_norm_g),
        (q_norm_g, ggq, m_q_norm_g, v_q_norm_g),
        (tr(w_q_b), gwqT, tr(m_w_q_b), tr(v_w_q_b)),
        (kv_norm_g, ggkv, m_kv_norm_g, v_kv_norm_g),
        (w_kv_b[0], gwkv, m_w_kv_b[0], v_w_kv_b[0]),
        (pw2(pool_w), gpw, pw2(m_pool_w), pw2(v_pool_w)),
        (pool_scale, gps, m_pool_scale, v_pool_scale),
        (fn2(final_norm_g), ggf, fn2(m_final_norm_g), fn2(v_final_norm_g)),
    ])
    untr = lambda a: a.T[None]
    pw4 = lambda a: a.reshape(1, len(POOL_WINDOWS), POOL_GROUP, POOL_GROUP)
    per_kind = [[
        r_meta[kind], r_norm[kind], untr(r_in[kind]), r_gq[kind], untr(r_wq[kind]), r_gkv[kind], r_wkv[kind][None],
        pw4(r_pw[kind]), r_ps[kind], r_out[kind][None], r_fn[kind].reshape(D),
    ] for kind in range(4)]
    return (gloss[0, 0], part["gx"][None], *per_kind[0], *per_kind[1], *per_kind[2], *per_kind[3])
```

```python
import jax
import jax.numpy as jnp
import numpy as np
from jax import lax
from jax.experimental import pallas as pl
from jax.experimental.pallas import tpu as pltpu

F32 = jnp.float32
BF16 = jnp.bfloat16

D = 1024
S = 2048
N_META = 16
PAD = 112
HEAD_ROWS = PAD + N_META
N = HEAD_ROWS + S
D_POOL = 512
POOL_WINDOWS = (2, 4, 8, 16)
POOL_GROUP = 128
HALO = 16
HEADS = 4
QK_NOPE = 128
QK_ROPE = 64
QK = QK_NOPE + QK_ROPE
V_HEAD = 128
Q_LORA = 256
KV_LORA = 128
D_IN = 1984
EPS = 1e-6
ROPE_THETA = 10000.0
SCALE = QK ** -0.5
CHIPS = 4

ROWS_FWD = 544
ROWS_MID = 544
ROWS_BWD = 544
TK = 128
TQ = 256
NQ = S // TQ
HEADS_PER_STEP_BWD = 2

O_PI, O_PG, O_CQ, O_CKV, O_KR, O_AG = 0, 512, 1024, 1280, 1408, 1472
O_KR_END = O_KR + 128
SHARD_IN = D_IN // CHIPS
SHARD_OUT = D // CHIPS

LR, B1, B2, ADAM_EPS, WD, STEP = 0.001, 0.9, 0.999, 1e-08, 0.01, 10
C1 = 1.0 - B1**STEP
C2 = 1.0 - B2**STEP

VMEM_LIMIT = 60 * 1024 * 1024
MESH = pl.DeviceIdType.MESH
NEG = -1e30

VEC_ROWS = 8
V_GQ, V_GKV, V_PS, V_LOSS = 0, 256, 384, 896


def _cparams(**kw):
    return pltpu.CompilerParams(vmem_limit_bytes=VMEM_LIMIT, **kw)


def _nt(a, b):
    return lax.dot_general(a, b, (((1,), (1,)), ((), ())), preferred_element_type=F32)


def _tn(a, b):
    return lax.dot_general(a, b, (((0,), (0,)), ((), ())), preferred_element_type=F32)


def _nn(a, b):
    return jnp.dot(a, b, preferred_element_type=F32)


def _swap64(t):
    return pltpu.roll(t, 32, 1) + pltpu.roll(t, 96, 1)


def _sigmoid(x):
    return 1.0 / (1.0 + jnp.exp(-x))


def _low_lanes():
    return (lax.broadcasted_iota(jnp.int32, (1, 128), 1) < QK_ROPE).astype(F32)


def _rows(w, rows):
    return pl.BlockSpec((rows, w), lambda i: (i, 0))


def _const(*shape):
    return pl.BlockSpec(shape, lambda *_: (0,) * len(shape), pipeline_mode=pl.Buffered(1))


STAT_GROUPS = HEADS // HEADS_PER_STEP_BWD


def _stat_slot(head):
    return head // HEADS_PER_STEP_BWD, head % HEADS_PER_STEP_BWD


def _peer_barrier(x, y, c):
    barrier = pltpu.get_barrier_semaphore()
    peers = [(x, y, 1 - c)] + [(x ^ fx, y ^ fy, c) for fx, fy in _CHIP_RELS[1:]]
    for peer in peers:
        pl.semaphore_signal(barrier, inc=1, device_id=peer, device_id_type=MESH)
    pl.semaphore_wait(barrier, len(peers))


def _attn_tiles():
    return [(0, TK, TK)] + [(TK + TQ * t, TQ, TK + TQ * (t + 1)) for t in range(NQ)]


def _masked_scores(q, k, rows, klen):
    s = _nt(q, k)
    col = lax.broadcasted_iota(jnp.int32, (1, TK), 1)
    head_bias = jnp.where(col >= PAD, 0.0, NEG)
    if klen == TK:
        return s + head_bias
    r = lax.broadcasted_iota(jnp.int32, (rows, 1), 0) >> 6
    c = lax.broadcasted_iota(jnp.int32, (1, rows), 1) >> 6
    diag_bias = jnp.where(c <= r, 0.0, NEG)
    parts = [s[:, 0:TK] + head_bias]
    if klen - rows > TK:
        parts.append(s[:, TK:klen - rows])
    parts.append(s[:, klen - rows:klen] + diag_bias)
    return jnp.concatenate(parts, axis=1)


def _fwd_in(h, norm_g, win, gq, wq, gkv, wkv, cosf, sinf):
    tr = ROWS_FWD

    def body(h_ref, g_ref, win_ref, gq_ref, wq_ref, gkv_ref, wkv_ref, cos_ref, sin_ref,
             pi_ref, pg_ref, cq_ref, ckv_ref, ag_ref, q_ref, k_ref, v_ref):
        h = h_ref[...]
        r = lax.rsqrt(jnp.mean(h * h, axis=-1, keepdims=True) + EPS)
        hn = ((h * r) * g_ref[...]).astype(BF16)
        u = _nt(hn, win_ref[0:O_KR_END, :])
        pi_ref[...] = u[:, O_PI:O_PG]
        pg_ref[...] = u[:, O_PG:O_CQ]
        cq = u[:, O_CQ:O_CKV]
        ckv = u[:, O_CKV:O_KR]
        cq_ref[...] = cq
        ckv_ref[...] = ckv
        ag_ref[...] = _nt(hn, win_ref[O_AG:D_IN, :])
        cosv = cos_ref[...]
        sinv = sin_ref[...]
        kr = u[:, O_KR:O_KR_END] * _low_lanes()
        kr = (kr * cosv + _swap64(kr) * sinv).astype(BF16)
        rq = lax.rsqrt(jnp.mean(cq * cq, axis=-1, keepdims=True) + EPS)
        cqn = ((cq * rq) * gq_ref[...]).astype(BF16)
        rkv = lax.rsqrt(jnp.mean(ckv * ckv, axis=-1, keepdims=True) + EPS)
        ckvn = ((ckv * rkv) * gkv_ref[...]).astype(BF16)
        for hd in range(HEADS):
            qh = _nt(cqn, wq_ref[hd]) * SCALE
            z = qh[:, QK_NOPE:]
            q_ref[hd, :, 0:QK_NOPE] = qh[:, 0:QK_NOPE].astype(BF16)
            q_ref[hd, :, QK_NOPE:] = (z * cosv + _swap64(z) * sinv).astype(BF16)
            kvh = _nn(ckvn, wkv_ref[hd])
            k_ref[hd, :, 0:QK_NOPE] = kvh[:, 0:QK_NOPE].astype(BF16)
            k_ref[hd, :, QK_NOPE:] = kr
            v_ref[hd] = kvh[:, QK_NOPE:].astype(BF16)

    head = lambda w: pl.BlockSpec((HEADS, tr, w), lambda i: (0, i, 0))
    return pl.pallas_call(
        body,
        name="fwd_in",
        grid=(N // tr,),
        in_specs=[
            _rows(D, tr), _const(1, D), _const(D_IN, D), _const(1, Q_LORA), _const(HEADS, 256, Q_LORA),
            _const(1, KV_LORA), _const(HEADS, KV_LORA, 256), _rows(128, tr), _rows(128, tr),
        ],
        out_specs=[_rows(D_POOL, tr), _rows(D_POOL, tr), _rows(Q_LORA, tr), _rows(KV_LORA, tr), _rows(D_POOL, tr),
                   head(256), head(256), head(V_HEAD)],
        out_shape=[
            jax.ShapeDtypeStruct((N, D_POOL), F32), jax.ShapeDtypeStruct((N, D_POOL), F32),
            jax.ShapeDtypeStruct((N, Q_LORA), F32), jax.ShapeDtypeStruct((N, KV_LORA), F32),
            jax.ShapeDtypeStruct((N, D_POOL), F32),
            jax.ShapeDtypeStruct((HEADS, N, 256), BF16), jax.ShapeDtypeStruct((HEADS, N, 256), BF16),
            jax.ShapeDtypeStruct((HEADS, N, V_HEAD), BF16),
        ],
        compiler_params=_cparams(dimension_semantics=("arbitrary",)),
    )(h, norm_g, win, gq, wq, gkv, wkv, cosf, sinf)


def _attn_fwd(q, k, v, wout_s):
    tiles = _attn_tiles()
    n_t = len(tiles)
    half = SHARD_OUT // 2
    fwd_step = n_t - 2

    def body(q_hbm, k_hbm, v_hbm, wout_ref, o_hbm, lse_ref, wout_o, q_buf, k_buf, v_buf, o_buf, s_wout, in_sems, out_sems,
             ici_send, ici_recv, fwd_send, fwd_recv, own_sem):
        step = pl.program_id(0)
        x, y, c = lax.axis_index("x"), lax.axis_index("y"), lax.axis_index("c")
        me = 2 * x + y

        def chip_of(rel):
            fx, fy = _CHIP_RELS[rel]
            return 2 * (x ^ fx) + (y ^ fy)

        def place(chip, core):
            return wout_o.at[pl.ds(pl.multiple_of(SHARD_OUT * chip + half * core, half), half), :]

        def ici_copy(rel, src_chip, to):
            return _remote(s_wout.at[pl.ds(pl.multiple_of(half * c, half), half), :], place(src_chip, c),
                           ici_send.at[rel - 1], ici_recv.at[rel - 1], to)

        def fwd_copy(rel, core, to):
            spot = place(chip_of(rel), core)
            return _remote(spot, spot, fwd_send.at[rel - 1], fwd_recv.at[rel - 1], to)

        own = pltpu.make_async_copy(s_wout, wout_o.at[pl.ds(pl.multiple_of(SHARD_OUT * me, SHARD_OUT), SHARD_OUT), :], own_sem)

        @pl.when(step == 0)
        def _():
            _peer_barrier(x, y, c)
            s_wout[...] = wout_ref[...].astype(BF16)
            own.start()
            for rel in (1, 2, 3):
                fx, fy = _CHIP_RELS[rel]
                ici_copy(rel, me, (x ^ fx, y ^ fy, c)).start()

        @pl.when(step == fwd_step)
        def _():
            for rel in (1, 2, 3):
                ici_copy(rel, chip_of(rel), (x, y, c)).wait_recv()
                fwd_copy(rel, c, (x, y, 1 - c)).start()

        def finish_wout():
            for rel in (1, 2, 3):
                fwd_copy(rel, 1 - c, (x, y, c)).wait_recv()
            for rel in (1, 2, 3):
                ici_copy(rel, me, (x, y, c)).wait_send()
                fwd_copy(rel, c, (x, y, c)).wait_send()
            own.wait()

        def loads(idx):
            q0, rows, _ = tiles[idx]
            rs = pl.ds(q0, rows)
            return [pltpu.make_async_copy(src.at[:, rs, :], dst.at[:, rs, :], in_sems.at[a, idx % 2])
                    for a, (src, dst) in enumerate(((q_hbm, q_buf), (k_hbm, k_buf), (v_hbm, v_buf)))]

        def store(idx):
            q0, rows, _ = tiles[idx]
            return pltpu.make_async_copy(o_buf.at[idx % 2, pl.ds(0, rows), :], o_hbm.at[pl.ds(q0, rows), :],
                                         out_sems.at[idx % 2])

        @pl.when(step == 0)
        def _():
            lse_ref[...] = jnp.zeros_like(lse_ref)
            for cp in loads(0):
                cp.start()

        for idx, (q0, rows, klen) in enumerate(tiles):
            @pl.when(step == idx)
            def _(idx=idx, q0=q0, rows=rows, klen=klen):
                for cp in loads(idx):
                    cp.wait()
                if idx + 1 < n_t:
                    for cp in loads(idx + 1):
                        cp.start()
                if idx >= 2:
                    store(idx - 2).wait()
                for hd in range(HEADS):
                    s = _masked_scores(q_buf[hd, q0:q0 + rows, :], k_buf[hd, 0:klen, :], rows, klen)
                    m = jnp.max(s, axis=-1, keepdims=True)
                    p = jnp.exp(s - m)
                    l = jnp.sum(p, axis=-1, keepdims=True)
                    o_buf[idx % 2, 0:rows, hd * V_HEAD:(hd + 1) * V_HEAD] = _nn(p.astype(BF16), v_buf[hd, 0:klen, :]) / l
                    grp, lane = _stat_slot(hd)
                    lse_ref[grp, q0:q0 + rows, lane:lane + 1] = m + jnp.log(l)
                store(idx).start()
                if idx == n_t - 1:
                    store(idx - 1).wait()
                    store(idx).wait()
                    finish_wout()

    hbm = pl.BlockSpec(memory_space=pl.ANY)
    return pl.pallas_call(
        body,
        name="attn_fwd",
        grid=(n_t,),
        in_specs=[hbm, hbm, hbm, _const(SHARD_OUT, D)],
        out_specs=[hbm, _const(STAT_GROUPS, N, 128), hbm],
        out_shape=[jax.ShapeDtypeStruct((N, HEADS * V_HEAD), F32), jax.ShapeDtypeStruct((STAT_GROUPS, N, 128), F32),
                   jax.ShapeDtypeStruct((D, D), BF16)],
        scratch_shapes=[pltpu.VMEM((HEADS, N, 256), BF16), pltpu.VMEM((HEADS, N, 256), BF16),
                        pltpu.VMEM((HEADS, N, V_HEAD), BF16), pltpu.VMEM((2, TQ, HEADS * V_HEAD), F32),
                        pltpu.VMEM((SHARD_OUT, D), BF16),
                        pltpu.SemaphoreType.DMA((3, 2)), pltpu.SemaphoreType.DMA((2,))]
        + [pltpu.SemaphoreType.DMA((3,))] * 4 + [pltpu.SemaphoreType.DMA],
        compiler_params=_cparams(dimension_semantics=("arbitrary",), collective_id=1),
    )(q, k, v, wout_s)


def _inv_count(row0, rows, w):
    row = row0 + lax.broadcasted_iota(jnp.int32, (rows, 1), 0)
    return 1.0 / jnp.clip(row - (PAD - 1), 1, w).astype(F32)


def _mid(h, tgt, pool_in, pool_gate, attn_gate, attn, pool_w, pool_scale, wout, gf):
    tr = ROWS_MID
    per = tr // HALO
    ng = len(POOL_WINDOWS)

    def body(h_ref, t_ref, pin_ref, halo_ref, pg_ref, ag_ref, at_ref, pw_ref, ps_ref, wout_ref, gf_ref,
             dh2_ref, do_ref, delta_ref, dag_ref, dpg_ref, dpl_ref, dwout_ref, dpw_ref, dps_ref, dgf_ref, loss_ref):
        i = pl.program_id(0)

        @pl.when(i == 0)
        def _():
            dwout_ref[...] = jnp.zeros_like(dwout_ref)
            dpw_ref[...] = jnp.zeros_like(dpw_ref)
            dps_ref[...] = jnp.zeros_like(dps_ref)
            dgf_ref[...] = jnp.zeros_like(dgf_ref)
            loss_ref[...] = jnp.zeros_like(loss_ref)

        row0 = i * tr
        real = (row0 + lax.broadcasted_iota(jnp.int32, (tr, 1), 0)) >= HEAD_ROWS
        h = h_ref[...]

        halo = jnp.where(i > 0, halo_ref[...], 0.0)
        ext = jnp.concatenate([halo, pin_ref[...]], axis=0)
        pooled = []
        for g, w in enumerate(POOL_WINDOWS):
            e = ext[:, g * POOL_GROUP:(g + 1) * POOL_GROUP]
            acc = e
            shift = 1
            while shift < w:
                acc = acc + pltpu.roll(acc, shift, 0)
                shift *= 2
            pooled.append((acc[HALO:] * _inv_count(row0, tr, w) - e[HALO:]).astype(BF16))
        pw = [pw_ref[g].astype(BF16) for g in range(ng)]
        mixed = jnp.concatenate([_nn(pooled[g], pw[g]) for g in range(ng)], axis=1)
        ps = ps_ref[...]
        mixed_s = mixed * ps
        pg = pg_ref[...]
        sig_p = _sigmoid(pg)
        silu_p = pg * sig_p
        pool_out = (silu_p * mixed_s).astype(BF16)
        ag = ag_ref[...]
        sig_a = _sigmoid(ag)
        silu_a = ag * sig_a
        at = at_ref[...]
        attn_out = (silu_a * at).astype(BF16)
        cat = jnp.concatenate([pool_out, attn_out], axis=1)
        h2 = h + _nn(cat, wout_ref[...])

        r2 = lax.rsqrt(jnp.mean(h2 * h2, axis=-1, keepdims=True) + EPS)
        n2 = h2 * r2
        gfv = gf_ref[...]
        err = jnp.where(real, n2 * gfv - t_ref[...], 0.0)
        loss_ref[...] += jnp.sum(jnp.sum(err * err, axis=-1, keepdims=True), axis=0, keepdims=True) * (0.5 / D)
        dy = err * (1.0 / D)
        dgf_ref[...] += jnp.sum(dy * n2, axis=0, keepdims=True)
        dn = dy * gfv
        dh2 = r2 * (dn - n2 * jnp.mean(dn * n2, axis=-1, keepdims=True))
        dh2_ref[...] = dh2
        dh2b = dh2.astype(BF16)

        dwout_ref[...] += _tn(cat, dh2b)
        dcat = _nt(dh2b, wout_ref[...])
        dpo = dcat[:, 0:D_POOL]
        dao = dcat[:, D_POOL:D]
        do = dao * silu_a
        prod = do * at
        delta_ref[...] = jnp.zeros_like(delta_ref)
        for hd in range(HEADS):
            grp, lane = _stat_slot(hd)
            cols = slice(hd * V_HEAD, (hd + 1) * V_HEAD)
            do_ref[grp, :, lane * V_HEAD:(lane + 1) * V_HEAD] = do[:, cols].astype(BF16)
            delta_ref[grp, :, lane:lane + 1] = jnp.sum(prod[:, cols], axis=-1, keepdims=True)
        dag_ref[...] = (dao * at * (sig_a * (1.0 + ag * (1.0 - sig_a)))).astype(BF16)
        dmixed_s = dpo * silu_p
        dpg_ref[...] = (dpo * mixed_s * (sig_p * (1.0 + pg * (1.0 - sig_p)))).astype(BF16)
        dps_ref[...] += jnp.sum(dmixed_s * mixed, axis=0, keepdims=True)
        dmixed = (dmixed_s * ps).astype(BF16)
        dpl = []
        for g in range(ng):
            dm = dmixed[:, g * POOL_GROUP:(g + 1) * POOL_GROUP]
            dpl.append(_nt(dm, pw[g]))
            dpw_ref[g] += _tn(pooled[g], dm)
        dpl_ref[...] = jnp.concatenate(dpl, axis=1)

    halo_spec = pl.BlockSpec((HALO, D_POOL), lambda i: (jnp.maximum(i * per - 1, 0), 0))
    return pl.pallas_call(
        body,
        name="mid",
        grid=(N // tr,),
        in_specs=[
            _rows(D, tr), _rows(D, tr), _rows(D_POOL, tr), halo_spec, _rows(D_POOL, tr), _rows(D_POOL, tr),
            _rows(D_POOL, tr), _const(ng, POOL_GROUP, POOL_GROUP), _const(1, D_POOL), _const(D, D), _const(1, D),
        ],
        out_specs=[
            _rows(D, tr), pl.BlockSpec((STAT_GROUPS, tr, HEADS_PER_STEP_BWD * V_HEAD), lambda i: (0, i, 0)),
            pl.BlockSpec((STAT_GROUPS, tr, 128), lambda i: (0, i, 0)),
            _rows(D_POOL, tr), _rows(D_POOL, tr), _rows(D_POOL, tr),
            _const(D, D), _const(ng, POOL_GROUP, POOL_GROUP), _const(1, D_POOL), _const(1, D), _const(1, 128),
        ],
        out_shape=[
            jax.ShapeDtypeStruct((N, D), F32), jax.ShapeDtypeStruct((STAT_GROUPS, N, HEADS_PER_STEP_BWD * V_HEAD), BF16),
            jax.ShapeDtypeStruct((STAT_GROUPS, N, 128), F32),
            jax.ShapeDtypeStruct((N, D_POOL), BF16), jax.ShapeDtypeStruct((N, D_POOL), BF16),
            jax.ShapeDtypeStruct((N, D_POOL), F32), jax.ShapeDtypeStruct((D, D), F32),
            jax.ShapeDtypeStruct((ng, POOL_GROUP, POOL_GROUP), F32),
            jax.ShapeDtypeStruct((1, D_POOL), F32), jax.ShapeDtypeStruct((1, D), F32), jax.ShapeDtypeStruct((1, 128), F32),
        ],
        compiler_params=_cparams(dimension_semantics=("arbitrary",)),
    )(h, tgt, pool_in, pool_in, pool_gate, attn_gate, attn, pool_w, pool_scale, wout, gf)


def _unrope(dy, cosv, sinv):
    return dy * cosv + _swap64(dy * sinv) * _low_lanes()


def _attn_bwd(q, k, v, do, lse, delta, cosf, sinf, dwout):
    tiles = _attn_tiles()
    hp = HEADS_PER_STEP_BWD
    n_g = HEADS // hp
    n_t = len(tiles)
    half = SHARD_OUT // 2
    send_at, sum_at = (0, 2), (n_g - 1, n_t // 2)

    def body(q_hbm, k_hbm, v_hbm, do_hbm, lse_ref, delta_ref, cos_ref, sin_ref, dwout_hbm, dq_hbm, dkv_ref, dkr_ref,
             gwout_ref, q_buf, k_buf, v_buf, do_buf, dq_buf, dk_acc, dv_acc, own_w, sib_w, stage_w, recv_w, gw_buf,
             in_sems, out_sems, ow_sems, d2d_send, d2d_recv, ici_send, ici_recv, fin_send, fin_recv):
        grp = pl.program_id(0)
        step = pl.program_id(1)
        heads = pl.ds(grp * hp, hp)
        x, y, c = lax.axis_index("x"), lax.axis_index("y"), lax.axis_index("c")
        sibling = (x, y, 1 - c)

        def chip_of(rel):
            fx, fy = _CHIP_RELS[rel]
            return 2 * (x ^ fx) + (y ^ fy)

        def piece(chip, core):
            return dwout_hbm.at[pl.ds(pl.multiple_of(SHARD_OUT * chip + half * core, half), half), :]

        def own_load(rel):
            return pltpu.make_async_copy(piece(chip_of(rel), c), own_w.at[rel], ow_sems.at[rel])

        def d2d_copy(rel):
            return _remote(piece(chip_of(rel), 1 - c), sib_w.at[rel], d2d_send.at[rel], d2d_recv.at[rel], sibling)

        def ici_copy(rel):
            fx, fy = _CHIP_RELS[rel]
            return _remote(stage_w.at[rel - 1], recv_w.at[rel - 1], ici_send.at[rel - 1], ici_recv.at[rel - 1],
                           (x ^ fx, y ^ fy, c))

        def fin_copy(core):
            spot = gw_buf.at[pl.ds(pl.multiple_of(half * core, half), half), :]
            return _remote(spot, spot, fin_send.at[0], fin_recv.at[0], sibling)

        @pl.when((grp == 0) & (step == 0))
        def _():
            _peer_barrier(x, y, c)
            for rel in (1, 2, 3, 0):
                d2d_copy(rel).start()
                own_load(rel).start()

        @pl.when((grp == send_at[0]) & (step == send_at[1]))
        def _():
            for rel in (1, 2, 3):
                own_load(rel).wait()
                d2d_copy(rel).wait_recv()
                stage_w[rel - 1] = (own_w[rel] + sib_w[rel]).astype(BF16)
                ici_copy(rel).start()

        @pl.when((grp == sum_at[0]) & (step == sum_at[1]))
        def _():
            own_load(0).wait()
            d2d_copy(0).wait_recv()
            total = own_w[0] + sib_w[0]
            for rel in (1, 2, 3):
                ici_copy(rel).wait_recv()
                total = total + recv_w[rel - 1].astype(F32)
            gw_buf[pl.ds(pl.multiple_of(half * c, half), half), :] = total
            fin_copy(c).start()

        def finish_dwout():
            fin_copy(1 - c).wait_recv()
            for rel in (0, 1, 2, 3):
                d2d_copy(rel).wait_send()
            for rel in (1, 2, 3):
                ici_copy(rel).wait_send()
            fin_copy(c).wait_send()
            gwout_ref[...] = gw_buf[...]

        def loads(g, idx):
            q0, rows, _ = tiles[idx]
            rs = pl.ds(q0, rows)
            par = (g * n_t + idx) % 2
            hs = pl.ds(g * hp, hp)
            pairs = ((q_hbm.at[hs, rs, :], q_buf.at[:, rs, :]), (k_hbm.at[hs, rs, :], k_buf.at[:, rs, :]),
                     (v_hbm.at[hs, rs, :], v_buf.at[:, rs, :]), (do_hbm.at[g, rs, :], do_buf.at[rs, :]))
            return [pltpu.make_async_copy(src, dst, in_sems.at[a, par]) for a, (src, dst) in enumerate(pairs)]

        def store(idx):
            q0, rows, _ = tiles[idx]
            return pltpu.make_async_copy(dq_buf.at[idx % 2, :, pl.ds(0, rows), :], dq_hbm.at[heads, pl.ds(q0, rows), :],
                                         out_sems.at[idx % 2])

        @pl.when(step == 0)
        def _():
            dk_acc[...] = jnp.zeros_like(dk_acc)
            dv_acc[...] = jnp.zeros_like(dv_acc)

        @pl.when((step == 0) & (grp == 0))
        def _():
            dkr_ref[...] = jnp.zeros_like(dkr_ref)
            for cp in loads(grp, 0):
                cp.start()

        for idx, (q0, rows, klen) in enumerate(tiles):
            @pl.when(step == idx)
            def _(idx=idx, q0=q0, rows=rows, klen=klen):
                for cp in loads(grp, idx):
                    cp.wait()
                if idx + 1 < n_t:
                    for cp in loads(grp, idx + 1):
                        cp.start()
                if idx >= 2:
                    store(idx - 2).wait()
                qs = pl.ds(q0, rows)
                for hd in range(hp):
                    qv = q_buf[hd, qs, :]
                    kv = k_buf[hd, 0:klen, :]
                    p = jnp.exp(_masked_scores(qv, kv, rows, klen) - lse_ref[0, qs, hd:hd + 1])
                    dob = do_buf[qs, hd * V_HEAD:(hd + 1) * V_HEAD]
                    ds = (p * (_nt(dob, v_buf[hd, 0:klen, :]) - delta_ref[0, qs, hd:hd + 1])).astype(BF16)
                    dq = _nn(ds, kv) * SCALE
                    dq_buf[idx % 2, hd, 0:rows, 0:QK_NOPE] = dq[:, 0:QK_NOPE].astype(BF16)
                    dq_buf[idx % 2, hd, 0:rows, QK_NOPE:] = _unrope(dq[:, QK_NOPE:], cos_ref[qs, :], sin_ref[qs, :]).astype(BF16)
                    dk_acc[hd, 0:klen, :] += _tn(ds, qv)
                    dv_acc[hd, 0:klen, :] += _tn(p.astype(BF16), dob)
                store(idx).start()

        @pl.when(step == n_t - 1)
        def _():
            @pl.when(grp + 1 < n_g)
            def _():
                for cp in loads(grp + 1, 0):
                    cp.start()

            for hd in range(hp):
                dkv_ref[hd, :, 0:QK_NOPE] = dk_acc[hd, :, 0:QK_NOPE].astype(BF16)
                dkv_ref[hd, :, QK_NOPE:] = dv_acc[hd].astype(BF16)
                dkr_ref[...] += dk_acc[hd, :, QK_NOPE:]
            store(n_t - 2).wait()
            store(n_t - 1).wait()

            @pl.when(grp == n_g - 1)
            def _():
                finish_dwout()

    hbm = pl.BlockSpec(memory_space=pl.ANY)
    stat = pl.BlockSpec((1, N, 128), lambda g, t: (g, 0, 0), pipeline_mode=pl.Buffered(1))
    piece_f32 = lambda lead: pltpu.VMEM((lead, half, D), F32)
    piece_bf16 = lambda lead: pltpu.VMEM((lead, half, D), BF16)
    return pl.pallas_call(
        body,
        name="attn_bwd",
        grid=(n_g, n_t),
        in_specs=[hbm, hbm, hbm, hbm, stat, stat, _const(N, 128), _const(N, 128), hbm],
        out_specs=[hbm, pl.BlockSpec((hp, N, 256), lambda g, t: (g, 0, 0), pipeline_mode=pl.Buffered(1)), _const(N, 128),
                   _const(SHARD_OUT, D)],
        out_shape=[
            jax.ShapeDtypeStruct((HEADS, N, 256), BF16), jax.ShapeDtypeStruct((HEADS, N, 256), BF16),
            jax.ShapeDtypeStruct((N, 128), F32), jax.ShapeDtypeStruct((SHARD_OUT, D), F32),
        ],
        scratch_shapes=[pltpu.VMEM((hp, N, 256), BF16), pltpu.VMEM((hp, N, 256), BF16), pltpu.VMEM((hp, N, V_HEAD), BF16),
                        pltpu.VMEM((N, hp * V_HEAD), BF16), pltpu.VMEM((2, hp, TQ, 256), BF16),
                        pltpu.VMEM((hp, N, 256), F32), pltpu.VMEM((hp, N, V_HEAD), F32),
                        piece_f32(CHIPS), piece_f32(CHIPS), piece_bf16(3), piece_bf16(3), pltpu.VMEM((SHARD_OUT, D), F32),
                        pltpu.SemaphoreType.DMA((4, 2)), pltpu.SemaphoreType.DMA((2,)), pltpu.SemaphoreType.DMA((CHIPS,)),
                        pltpu.SemaphoreType.DMA((CHIPS,)), pltpu.SemaphoreType.DMA((CHIPS,)),
                        pltpu.SemaphoreType.DMA((3,)), pltpu.SemaphoreType.DMA((3,)),
                        pltpu.SemaphoreType.DMA((1,)), pltpu.SemaphoreType.DMA((1,))],
        compiler_params=_cparams(dimension_semantics=("arbitrary", "arbitrary"), collective_id=2),
    )(q, k, v, do, lse, delta, cosf, sinf, dwout)


def _bwd_in(h, dh2, dq, dkv, dkr, cq, ckv, dpl, dpg, dag, norm_g, win, gq, wq, gkv, wkv, cosf, sinf, adam_out):
    tr = ROWS_BWD
    nb = N // tr
    per = tr // HALO
    lead = HEAD_ROWS
    adam_rows = SHARD_OUT // nb

    def body(h_ref, dh2_ref, dq_ref, dkv_ref, dkr_ref, cq_ref, ckv_ref, dpl_ref, halo_ref, dpg_ref, dag_ref,
             g_ref, win_ref, gq_ref, wq_ref, gkv_ref, wkv_ref, cos_ref, sin_ref, aw_ref, ag_ref, am_ref, av_ref,
             gx_ref, dmeta_ref, du_ref, hn_ref, dwq_ref, dwkv_ref, dg_ref, dgq_ref, dgkv_ref, ago_ref, ad_ref, anm_ref, anv_ref,
             dh_buf, gx_sem):
        i = pl.program_id(0)
        grad_out = ag_ref[...]
        ago_ref[...] = grad_out
        ad_ref[...], anm_ref[...], anv_ref[...] = _adamw_math(aw_ref[...], grad_out, am_ref[...], av_ref[...])

        @pl.when(i == 0)
        def _():
            dwq_ref[...] = jnp.zeros_like(dwq_ref)
            dwkv_ref[...] = jnp.zeros_like(dwkv_ref)
            dg_ref[...] = jnp.zeros_like(dg_ref)
            dgq_ref[...] = jnp.zeros_like(dgq_ref)
            dgkv_ref[...] = jnp.zeros_like(dgkv_ref)

        row0 = i * tr
        h = h_ref[...]
        r = lax.rsqrt(jnp.mean(h * h, axis=-1, keepdims=True) + EPS)
        n = h * r
        gv = g_ref[...]
        hn = (n * gv).astype(BF16)
        cq = cq_ref[...]
        rq = lax.rsqrt(jnp.mean(cq * cq, axis=-1, keepdims=True) + EPS)
        nq = cq * rq
        gqv = gq_ref[...]
        cqn = (nq * gqv).astype(BF16)
        dcqn = jnp.zeros((tr, Q_LORA), F32)
        for hd in range(HEADS):
            dqf = dq_ref[hd]
            dcqn = dcqn + _nn(dqf, wq_ref[hd])
            dwq_ref[hd] += _tn(dqf, cqn)
        dgq_ref[...] += jnp.sum(dcqn * nq, axis=0, keepdims=True)
        dnq = dcqn * gqv
        dcq = rq * (dnq - nq * jnp.mean(dnq * nq, axis=-1, keepdims=True))

        ckv = ckv_ref[...]
        rkv = lax.rsqrt(jnp.mean(ckv * ckv, axis=-1, keepdims=True) + EPS)
        nkv = ckv * rkv
        gkvv = gkv_ref[...]
        ckvn = (nkv * gkvv).astype(BF16)
        dckvn = jnp.zeros((tr, KV_LORA), F32)
        for hd in range(HEADS):
            dkv = dkv_ref[hd]
            dckvn = dckvn + _nt(dkv, wkv_ref[hd])
            dwkv_ref[hd] += _tn(ckvn, dkv)
        dgkv_ref[...] += jnp.sum(dckvn * nkv, axis=0, keepdims=True)
        dnkv = dckvn * gkvv
        dckv = rkv * (dnkv - nkv * jnp.mean(dnkv * nkv, axis=-1, keepdims=True))
        dkr = _unrope(dkr_ref[...], cos_ref[...], sin_ref[...])

        cur = dpl_ref[...]
        halo = jnp.where(i < nb - 1, halo_ref[...], 0.0)
        dpi = []
        for g, w in enumerate(POOL_WINDOWS):
            sl = slice(g * POOL_GROUP, (g + 1) * POOL_GROUP)
            a = jnp.concatenate([cur[:, sl] * _inv_count(row0, tr, w), halo[:, sl] * _inv_count(row0 + tr, HALO, w)], axis=0)
            acc = a
            shift = 1
            while shift < w:
                acc = acc + pltpu.roll(acc, tr + HALO - shift, 0)
                shift *= 2
            dpi.append(acc[0:tr] - cur[:, sl])

        du = jnp.concatenate([t.astype(BF16) for t in dpi] + [dpg_ref[...]] + [t.astype(BF16) for t in (dcq, dckv, dkr)],
                             axis=1)
        dagb = dag_ref[...]
        du_ref[...] = du
        hn_ref[...] = hn
        dhn = _nn(du, win_ref[0:O_KR_END, :]) + _nn(dagb, win_ref[O_AG:D_IN, :])
        dg_ref[...] += jnp.sum(dhn * n, axis=0, keepdims=True)
        dn = dhn * gv
        dh = dh2_ref[...] + r * (dn - n * jnp.mean(dn * n, axis=-1, keepdims=True))

        first = pltpu.make_async_copy(dh_buf.at[pl.ds(lead, tr - lead), :], gx_ref.at[pl.ds(0, tr - lead), :], gx_sem)
        later = lambda step: pltpu.make_async_copy(
            dh_buf, gx_ref.at[pl.ds(pl.multiple_of(step * tr - lead, 16), tr), :], gx_sem)

        @pl.when(i == 1)
        def _():
            first.wait()

        @pl.when(i > 1)
        def _():
            later(i - 1).wait()

        dh_buf[...] = dh

        @pl.when(i == 0)
        def _():
            first.start()
            for chip in range(CHIPS):
                dmeta_ref[chip] = dh[PAD:HEAD_ROWS, chip * 256:(chip + 1) * 256]

        @pl.when(i > 0)
        def _():
            later(i).start()

        @pl.when(i == nb - 1)
        def _():
            later(i).wait()

    head = lambda w: pl.BlockSpec((HEADS, tr, w), lambda i: (0, i, 0))
    halo_spec = pl.BlockSpec((HALO, D_POOL), lambda i: (jnp.minimum((i + 1) * per, N // HALO - 1), 0))
    return pl.pallas_call(
        body,
        name="bwd_in",
        grid=(nb,),
        in_specs=[
            _rows(D, tr), _rows(D, tr), head(256), head(256), _rows(128, tr), _rows(Q_LORA, tr), _rows(KV_LORA, tr),
            _rows(D_POOL, tr), halo_spec, _rows(D_POOL, tr), _rows(D_POOL, tr),
            _const(1, D), _const(D_IN, D), _const(1, Q_LORA), _const(HEADS, 256, Q_LORA),
            _const(1, KV_LORA), _const(HEADS, KV_LORA, 256), _rows(128, tr), _rows(128, tr),
        ] + [_rows(D, adam_rows)] * 4,
        out_specs=[
            pl.BlockSpec(memory_space=pl.ANY), _const(CHIPS, N_META, 256), _rows(O_KR_END, tr), _rows(D, tr),
            _const(HEADS, 256, Q_LORA),
            _const(HEADS, KV_LORA, 256), _const(1, D), _const(1, Q_LORA), _const(1, KV_LORA),
        ] + [_rows(D, adam_rows)] * 4,
        out_shape=[
            jax.ShapeDtypeStruct((S, D), F32), jax.ShapeDtypeStruct((CHIPS, N_META, 256), F32),
            jax.ShapeDtypeStruct((N, O_KR_END), BF16), jax.ShapeDtypeStruct((N, D), BF16),
            jax.ShapeDtypeStruct((HEADS, 256, Q_LORA), F32),
            jax.ShapeDtypeStruct((HEADS, KV_LORA, 256), F32),
            jax.ShapeDtypeStruct((1, D), F32), jax.ShapeDtypeStruct((1, Q_LORA), F32), jax.ShapeDtypeStruct((1, KV_LORA), F32),
        ] + [jax.ShapeDtypeStruct((SHARD_OUT, D), F32)] * 4,
        scratch_shapes=[pltpu.VMEM((tr, D), F32), pltpu.SemaphoreType.DMA],
        compiler_params=_cparams(dimension_semantics=("arbitrary",)),
    )(h, dh2, dq, dkv, dkr, cq, ckv, dpl, dpl, dpg, dag, norm_g, win, gq, wq, gkv, wkv, cosf, sinf, *adam_out)


def _local_step(h, tgt, norm_g, win, gq, wq, gkv, wkv, pool_w, pool_scale, wout_s, m_wout_s, v_wout_s, gf, cosf, sinf):
    pool_in, pool_gate, cq, ckv, attn_gate, q, k, v = _fwd_in(h, norm_g, win, gq, wq, gkv, wkv, cosf, sinf)
    attn, lse, wout = _attn_fwd(q, k, v, wout_s)
    dh2, do, delta, dag, dpg, dpl, dwout, dpw, dps, dgf, loss = _mid(
        h, tgt, pool_in, pool_gate, attn_gate, attn, pool_w, pool_scale, wout, gf)
    dq, dkv, dkr, gwout = _attn_bwd(q, k, v, do, lse, delta, cosf, sinf, dwout)
    gx, dmeta, du, hn, dwq, dwkv, dg, dgq, dgkv, *r_out = _bwd_in(
        h, dh2, dq, dkv, dkr, cq, ckv, dpl, dpg, dag, norm_g, win, gq, wq, gkv, wkv, cosf, sinf,
        (wout_s, gwout, m_wout_s, v_wout_s))
    return dict(gx=gx, dmeta=dmeta, du=du, dag=dag, hn=hn, dwq=dwq, dwkv=dwkv, r_out=tuple(r_out), dg=dg, dgq=dgq,
                dgkv=dgkv, dpw=dpw, dps=dps, dgf=dgf, loss=loss)


_CHIP_RELS = ((0, 0), (1, 0), (0, 1), (1, 1))

_ARR_ROWS = (SHARD_IN, SHARD_OUT, 256, KV_LORA, N_META)
_ARR_COLS = (D, D, Q_LORA, 256, 256)
_PIECES = (
    (0, 0, 256, 0), (0, 256, SHARD_IN - 256, 1),
    (1, 0, 128, 0), (1, 128, 128, 1),
    (2, 0, 128, 0), (2, 128, 128, 1),
    (3, 0, 64, 0), (3, 64, 64, 1),
    (4, 0, N_META, 0),
)
_NP = len(_PIECES)
_PIECE_MAX = (256, 128, 128, 64, N_META)


def _gathered_at(refs, arr, chip, r0, n):
    if arr in (0, 1):
        return refs[arr].at[pl.ds(pl.multiple_of(_ARR_ROWS[arr] * chip + r0, 16), n), :]
    return refs[arr].at[chip, pl.ds(r0, n), :]


def _remote(src, dst, send_sem, recv_sem, to):
    return pltpu.make_async_remote_copy(src_ref=src, dst_ref=dst, send_sem=send_sem, recv_sem=recv_sem,
                                        device_id=to, device_id_type=MESH)


def _gather_weights(winT_s, wqT_s, wkv_s, meta_s, x2, tgt2):
    arrays = (0, 2, 3, 4)

    def body(win_ref, wq_ref, wkv_ref, meta_ref, x_ref, t_ref, win_o, wq_o, wkv_o, h_o, tp_o,
             s_win, s_wq, s_wkv, meta_all, head_buf, x_buf, t_buf, ici_send, ici_recv, fwd_send, fwd_recv,
             loc_sems, own_sems):
        x, y, c = lax.axis_index("x"), lax.axis_index("y"), lax.axis_index("c")
        me = 2 * x + y
        stage = (s_win, None, s_wq, s_wkv, meta_ref)
        outs = (win_o, None, wq_o, wkv_o, meta_all)

        _peer_barrier(x, y, c)

        frames = pl.ds(HEAD_ROWS, S)
        loads = [pltpu.make_async_copy(x_ref, x_buf, loc_sems.at[0]), pltpu.make_async_copy(t_ref, t_buf, loc_sems.at[1])]
        local = [pltpu.make_async_copy(x_buf, h_o.at[frames, :], loc_sems.at[0]),
                 pltpu.make_async_copy(t_buf, tp_o.at[frames, :], loc_sems.at[1])]
        for cp in loads:
            cp.start()

        s_win[...] = win_ref[...].astype(BF16)
        s_wq[0:QK, :] = wq_ref[...].astype(BF16)
        s_wq[QK:256, :] = jnp.zeros((256 - QK, Q_LORA), BF16)
        s_wkv[...] = wkv_ref[...].astype(BF16)

        def chip_of(rel):
            fx, fy = _CHIP_RELS[rel]
            return 2 * (x ^ fx) + (y ^ fy)

        def same_core_of(rel):
            fx, fy = _CHIP_RELS[rel]
            return (x ^ fx, y ^ fy, c)

        def ici_copy(rel, i, src_chip, to):
            arr, r0, n, _ = _PIECES[i]
            k = (rel - 1) * _NP + i
            return _remote(stage[arr].at[pl.ds(r0, n), :], _gathered_at(outs, arr, src_chip, r0, n),
                           ici_send.at[k], ici_recv.at[k], to)

        def fwd_copy(rel, i, to):
            arr, r0, n, _ = _PIECES[i]
            k = (rel - 1) * _NP + i
            place = _gathered_at(outs, arr, chip_of(rel), r0, n)
            return _remote(place, place, fwd_send.at[k], fwd_recv.at[k], to)

        for core in (0, 1):
            @pl.when(c == core)
            def _(core=core):
                mine = [i for i in range(_NP) if _PIECES[i][3] == core and _PIECES[i][0] in arrays]
                theirs = [i for i in range(_NP) if _PIECES[i][3] != core and _PIECES[i][0] in arrays]
                sends = [ici_copy(rel, i, me, same_core_of(rel)) for rel in (1, 2, 3) for i in mine]
                for cp in sends:
                    cp.start()
                for ld, st in zip(loads, local):
                    ld.wait()
                    st.start()
                own = [pltpu.make_async_copy(stage[arr], _gathered_at(outs, arr, me, 0, _ARR_ROWS[arr]), own_sems.at[arr])
                       for arr in arrays if arr != 4]
                for cp in own:
                    cp.start()
                meta_all[me] = meta_ref[...]
                for rel in (1, 2, 3):
                    for i in mine:
                        ici_copy(rel, i, chip_of(rel), (x, y, c)).wait_recv()
                        fwd = fwd_copy(rel, i, (x, y, 1 - c))
                        fwd.start()
                        sends.append(fwd)
                for rel in (1, 2, 3):
                    for i in theirs:
                        fwd_copy(rel, i, (x, y, c)).wait_recv()
                for cp in sends:
                    cp.wait_send()
                for cp in own:
                    cp.wait()

        head_buf[...] = jnp.zeros_like(head_buf)
        zeros = pltpu.make_async_copy(head_buf, tp_o.at[pl.ds(0, HEAD_ROWS), :], loc_sems.at[2])
        zeros.start()
        zeros.wait()
        for chip in range(CHIPS):
            head_buf[PAD:HEAD_ROWS, chip * 256:(chip + 1) * 256] = meta_all[chip]
        head = pltpu.make_async_copy(head_buf, h_o.at[pl.ds(0, HEAD_ROWS), :], loc_sems.at[2])
        head.start()
        head.wait()
        for cp in local:
            cp.wait()

    vm = pl.BlockSpec(memory_space=pltpu.VMEM)
    hbm = pl.BlockSpec(memory_space=pl.ANY)
    return pl.pallas_call(
        body,
        name="gather_weights",
        in_specs=[vm] * 4 + [hbm] * 2,
        out_specs=[hbm] * 5,
        out_shape=[
            jax.ShapeDtypeStruct((D_IN, D), BF16),
            jax.ShapeDtypeStruct((CHIPS, 256, Q_LORA), BF16), jax.ShapeDtypeStruct((CHIPS, KV_LORA, 256), BF16),
            jax.ShapeDtypeStruct((N, D), F32), jax.ShapeDtypeStruct((N, D), F32),
        ],
        scratch_shapes=[pltpu.VMEM((_ARR_ROWS[a], _ARR_COLS[a]), BF16) for a in (0, 2, 3)]
        + [pltpu.VMEM((CHIPS, N_META, 256), F32), pltpu.VMEM((HEAD_ROWS, D), F32), pltpu.VMEM((S, D), F32),
           pltpu.VMEM((S, D), F32)]
        + [pltpu.SemaphoreType.DMA((3 * _NP,))] * 4 + [pltpu.SemaphoreType.DMA((3,)), pltpu.SemaphoreType.DMA((4,))],
        compiler_params=_cparams(collective_id=0),
    )(winT_s, wqT_s, wkv_s, meta_s, x2, tgt2)


_SM_ROWS = (len(POOL_WINDOWS) * POOL_GROUP, VEC_ROWS)
_SM_COLS = (POOL_GROUP, D)
_SM_PIECES = ((0, 0, 256, 0), (0, 256, 256, 1), (1, 0, VEC_ROWS, 0))
_NSP = len(_SM_PIECES)


def _reduce_grads(du, dag, hn, dwq, dwkv, dmeta4, dpw, dg, dgf, dgq, dgkv, dps, loss):
    arrays = (0, 2, 3, 4)
    loaded = (2, 3, 4)
    N_DU_BLOCKS = O_KR_END // 256
    N_DAG_BLOCKS = D_POOL // 256
    blocks = ((0, 1), (1, 2, 3), (3, 4, 5, 6), (6, 7))

    def body(du_hbm, dag_hbm, hn_hbm, dwq_ref, dwkv_ref, dmeta_ref, dpw_ref, dg_ref, dgf_ref, dgq_ref, dgkv_ref, dps_ref,
             loss_ref, gwin_o, gwq_o, gwkv_o, gmeta_o, gpw_o, gg_o, ggf_o, ggq_o, ggkv_o, gps_o, gloss_o,
             ow2, ow3, ow4, sb0, sb2, sb3, sb4, st0, st2, st3, st4, rc0, rc2, rc3, rc4,
             vec, sm_sb0, sm_sb1, sm_cs0, sm_cs1, sm_rc0, sm_rc1, vec_fin, du_v, dag_v, hn_v, dwin_buf, own0,
             own_sems, d2d_send, d2d_recv, ici_send, ici_recv, fin_send, fin_recv,
             swap_send, swap_recv, smi_send, smi_recv, smf_send, smf_recv, ld_sems):
        x, y, c = lax.axis_index("x"), lax.axis_index("y"), lax.axis_index("c")
        me = 2 * x + y
        col_block = lambda b: pl.ds(256 * b, 256)
        operands = [pltpu.make_async_copy(hn_hbm, hn_v, ld_sems.at[0])]
        operands += [pltpu.make_async_copy(du_hbm.at[:, col_block(b)], du_v.at[b], ld_sems.at[1 + b])
                     for b in range(N_DU_BLOCKS)]
        operands += [pltpu.make_async_copy(dag_hbm.at[:, col_block(b)], dag_v.at[b], ld_sems.at[1 + N_DU_BLOCKS + b])
                     for b in range(N_DAG_BLOCKS)]
        for cp in operands:
            cp.start()
        grads = (None, None, dwq_ref, dwkv_ref, dmeta_ref)
        outs = (gwin_o, None, gwq_o, gwkv_o, gmeta_o)
        own_buf = (None, None, ow2, ow3, ow4)
        sib_buf = (sb0, None, sb2, sb3, sb4)
        stage = (st0, None, st2, st3, st4)
        recv = (rc0, None, rc2, rc3, rc4)
        sm_mine = (dpw_ref, vec)
        sm_sib = (sm_sb0, sm_sb1)
        sm_chip = (sm_cs0, sm_cs1)
        sm_recv = (sm_rc0, sm_rc1)
        sm_out = (gpw_o, vec_fin)
        sibling = (x, y, 1 - c)

        def chip_of(rel):
            fx, fy = _CHIP_RELS[rel]
            return 2 * (x ^ fx) + (y ^ fy)

        def same_core_of(rel):
            fx, fy = _CHIP_RELS[rel]
            return (x ^ fx, y ^ fy, c)

        def slot(bufs, i, idx):
            arr, _, n, _ = _PIECES[i]
            return bufs[arr].at[idx, pl.ds(0, n), :]

        def own_load(rel, i):
            arr, r0, n, _ = _PIECES[i]
            return pltpu.make_async_copy(_gathered_at(grads, arr, chip_of(rel), r0, n), slot(own_buf, i, rel),
                                         own_sems.at[rel * _NP + i])

        def d2d_copy(rel, i):
            arr, r0, n, _ = _PIECES[i]
            k = rel * _NP + i
            return _remote(_gathered_at(grads, arr, chip_of(rel), r0, n), slot(sib_buf, i, rel),
                           d2d_send.at[k], d2d_recv.at[k], sibling)

        def ici_copy(rel, i):
            k = (rel - 1) * _NP + i
            return _remote(slot(stage, i, rel - 1), slot(recv, i, rel - 1), ici_send.at[k], ici_recv.at[k],
                           same_core_of(rel))

        def fin_copy(i):
            arr, r0, n, _ = _PIECES[i]
            place = outs[arr].at[pl.ds(r0, n), :]
            return _remote(place, place, fin_send.at[i], fin_recv.at[i], sibling)

        def sm_ici_copy(rel, j):
            blk, r0, n, _ = _SM_PIECES[j]
            k = (rel - 1) * _NSP + j
            return _remote(sm_chip[blk].at[pl.ds(r0, n), :], sm_recv[blk].at[rel - 1, pl.ds(r0, n), :],
                           smi_send.at[k], smi_recv.at[k], same_core_of(rel))

        def sm_fin_copy(j):
            blk, r0, n, _ = _SM_PIECES[j]
            place = sm_out[blk].at[pl.ds(r0, n), :]
            return _remote(place, place, smf_send.at[j], smf_recv.at[j], sibling)

        _peer_barrier(x, y, c)

        vec[...] = jnp.zeros_like(vec)
        vec[0:1, :] = dg_ref[...]
        vec[1:2, :] = dgf_ref[...]
        vec[2:3, V_GQ:V_GQ + Q_LORA] = dgq_ref[...]
        vec[2:3, V_GKV:V_GKV + KV_LORA] = dgkv_ref[...]
        vec[2:3, V_PS:V_PS + D_POOL] = dps_ref[...]
        vec[2:3, V_LOSS:D] = loss_ref[...]
        swaps = [_remote(sm_mine[b], sm_sib[b], swap_send.at[b], swap_recv.at[b], sibling) for b in (0, 1)]
        for cp in swaps:
            cp.start()

        def pieces_of(core):
            mine = [i for i in range(_NP) if _PIECES[i][3] == core and _PIECES[i][0] in loaded]
            theirs = [i for i in range(_NP) if _PIECES[i][3] != core and _PIECES[i][0] in loaded]
            i0 = next(i for i in range(_NP) if _PIECES[i][0] == 0 and _PIECES[i][3] == core)
            j0 = next(i for i in range(_NP) if _PIECES[i][0] == 0 and _PIECES[i][3] != core)
            return mine, theirs, i0, j0

        for core in (0, 1):
            @pl.when(c == core)
            def _(core=core):
                mine, theirs, _, _ = pieces_of(core)
                for rel in (1, 2, 3, 0):
                    for i in theirs:
                        d2d_copy(rel, i).start()
                    for i in mine:
                        own_load(rel, i).start()

        def form(b):
            operands[1 + b].wait()
            if b < N_DU_BLOCKS:
                lo, hi = 256 * b, min(256 * b + 256, O_AG)
                dwin_buf[lo:hi, :] = _tn(du_v[b], hn_v[...])[0:hi - lo, :]
            else:
                lo = O_AG + 256 * (b - N_DU_BLOCKS)
                dwin_buf[lo:lo + 256, :] = _tn(dag_v[b - N_DU_BLOCKS], hn_v[...])

        def shard_rows(chip, i):
            return pl.ds(SHARD_IN * chip + _PIECES[i][1], _PIECES[i][2])

        def d2d0(chip, rel, i):
            return _remote(dwin_buf.at[shard_rows(chip, i), :], slot(sib_buf, i, rel),
                           d2d_send.at[rel * _NP + i], d2d_recv.at[rel * _NP + i], sibling)

        def ici0(chip, rel, i):
            return _remote(slot(stage, i, rel - 1), slot(recv, i, rel - 1), ici_send.at[(rel - 1) * _NP + i],
                           ici_recv.at[(rel - 1) * _NP + i], (chip // 2, chip % 2, c))

        def settle(chip, rel, i):
            d2d0(chip, rel, i).wait_recv()
            total = dwin_buf[shard_rows(chip, i), :] + slot(sib_buf, i, rel)[...]
            if rel == 0:
                own0[0:_PIECES[i][2], :] = total
            else:
                slot(stage, i, rel - 1)[...] = total.astype(BF16)
                ici0(chip, rel, i).start()

        def shard_order(chip):
            return ((chip ^ 2, 1), (chip ^ 1, 2), (chip ^ 3, 3), (chip, 0))

        for my_chip in range(CHIPS):
            @pl.when(me == my_chip)
            def _(my_chip=my_chip):
                operands[0].wait()
                done, prev = set(), None
                for chip, rel in shard_order(my_chip) + ((None, None),):
                    if chip is not None:
                        for b in sorted(set(blocks[chip]) - done):
                            form(b)
                            done.add(b)
                    for core in (0, 1):
                        @pl.when(c == core)
                        def _(core=core, chip=chip, rel=rel, prev=prev):
                            _, _, i0, j0 = pieces_of(core)
                            if chip is not None:
                                d2d0(chip, rel, j0).start()
                            if prev is not None:
                                settle(prev[0], prev[1], i0)
                    prev = (chip, rel) if chip is not None else None

        for core in (0, 1):
            @pl.when(c == core)
            def _(core=core):
                mine, theirs, i0, j0 = pieces_of(core)
                sm_mine_p = [j for j in range(_NSP) if _SM_PIECES[j][3] == core]
                sm_theirs_p = [j for j in range(_NSP) if _SM_PIECES[j][3] != core]
                sends = list(swaps) + [d2d_copy(rel, i) for rel in (1, 2, 3, 0) for i in theirs]

                for rel in (1, 2, 3):
                    for i in mine:
                        arr, r0, n, _ = _PIECES[i]
                        own_load(rel, i).wait()
                        d2d_copy(rel, i).wait_recv()
                        total = slot(own_buf, i, rel)[...] + slot(sib_buf, i, rel)[...]
                        slot(stage, i, rel - 1)[...] = total.astype(stage[arr].dtype)
                        cp = ici_copy(rel, i)
                        cp.start()
                        sends.append(cp)

                for b in (0, 1):
                    swaps[b].wait_recv()
                    sm_chip[b][...] = sm_mine[b][...] + sm_sib[b][...]
                for rel in (1, 2, 3):
                    for j in sm_mine_p:
                        cp = sm_ici_copy(rel, j)
                        cp.start()
                        sends.append(cp)

                for i in mine:
                    arr, r0, n, _ = _PIECES[i]
                    own_load(0, i).wait()
                    d2d_copy(0, i).wait_recv()
                    total = slot(own_buf, i, 0)[...] + slot(sib_buf, i, 0)[...]
                    for rel in (1, 2, 3):
                        ici_copy(rel, i).wait_recv()
                        total = total + slot(recv, i, rel - 1)[...].astype(F32)
                    outs[arr][pl.ds(r0, n), :] = total
                    cp = fin_copy(i)
                    cp.start()
                    sends.append(cp)
                total = own0[0:_PIECES[i0][2], :]
                for rel in (1, 2, 3):
                    ici_copy(rel, i0).wait_recv()
                    total = total + slot(recv, i0, rel - 1)[...].astype(F32)
                outs[0][pl.ds(_PIECES[i0][1], _PIECES[i0][2]), :] = total
                cp = fin_copy(i0)
                cp.start()
                sends.append(cp)

                for j in sm_mine_p:
                    blk, r0, n, _ = _SM_PIECES[j]
                    for rel in (1, 2, 3):
                        sm_ici_copy(rel, j).wait_recv()
                    total = jnp.zeros((n, _SM_COLS[blk]), F32)
                    for chip in range(CHIPS):
                        flips = chip ^ me
                        rel = jnp.where(flips == 2, 1, jnp.where(flips == 1, 2, flips))
                        theirs_rows = sm_recv[blk][jnp.maximum(rel - 1, 0), pl.ds(r0, n), :]
                        total = total + jnp.where(rel == 0, sm_chip[blk][pl.ds(r0, n), :], theirs_rows)
                    sm_out[blk][pl.ds(r0, n), :] = total
                    cp = sm_fin_copy(j)
                    cp.start()
                    sends.append(cp)

                for i in theirs + [j0]:
                    fin_copy(i).wait_recv()
                for j in sm_theirs_p:
                    sm_fin_copy(j).wait_recv()
                for cp in sends:
                    cp.wait_send()
                for my_chip in range(CHIPS):
                    @pl.when(me == my_chip)
                    def _(my_chip=my_chip):
                        for chip, rel in shard_order(my_chip):
                            d2d0(chip, rel, j0).wait_send()
                            if rel != 0:
                                ici0(chip, rel, i0).wait_send()

        gg_o[...] = vec_fin[0:1, :]
        ggf_o[...] = vec_fin[1:2, :]
        ggq_o[...] = vec_fin[2:3, V_GQ:V_GQ + Q_LORA]
        ggkv_o[...] = vec_fin[2:3, V_GKV:V_GKV + KV_LORA]
        gps_o[...] = vec_fin[2:3, V_PS:V_PS + D_POOL]
        gloss_o[...] = vec_fin[2:3, V_LOSS:D]

    vm = pl.BlockSpec(memory_space=pltpu.VMEM)
    piece_buf = lambda lead, dtype, which=arrays: [
        pltpu.VMEM((lead, _PIECE_MAX[a], _ARR_COLS[a]), F32 if a == 4 else dtype) for a in which]
    sm_buf = lambda *lead: [pltpu.VMEM(lead + (_SM_ROWS[b], _SM_COLS[b]), F32) for b in (0, 1)]
    dma = lambda n: [pltpu.SemaphoreType.DMA((n,))] * 2
    return pl.pallas_call(
        body,
        name="reduce_grads",
        in_specs=[pl.BlockSpec(memory_space=pl.ANY)] * 5 + [vm] * 8,
        out_specs=[vm] * 11,
        out_shape=[jax.ShapeDtypeStruct((_ARR_ROWS[a], _ARR_COLS[a]), F32) for a in arrays]
        + [jax.ShapeDtypeStruct((_SM_ROWS[0], _SM_COLS[0]), F32), jax.ShapeDtypeStruct((1, D), F32),
           jax.ShapeDtypeStruct((1, D), F32), jax.ShapeDtypeStruct((1, Q_LORA), F32),
           jax.ShapeDtypeStruct((1, KV_LORA), F32), jax.ShapeDtypeStruct((1, D_POOL), F32),
           jax.ShapeDtypeStruct((1, 128), F32)],
        scratch_shapes=piece_buf(CHIPS, F32, loaded) + piece_buf(CHIPS, F32) + piece_buf(3, BF16) + piece_buf(3, BF16)
        + [pltpu.VMEM((VEC_ROWS, D), F32)] + sm_buf() + sm_buf() + sm_buf(3) + [pltpu.VMEM((VEC_ROWS, D), F32)]
        + [pltpu.VMEM((O_KR_END // 256, N, 256), BF16), pltpu.VMEM((D_POOL // 256, N, 256), BF16), pltpu.VMEM((N, D), BF16),
           pltpu.VMEM((D_IN, D), F32), pltpu.VMEM((_PIECE_MAX[0], D), F32)]
        + [pltpu.SemaphoreType.DMA((CHIPS * _NP,))]
        + dma(CHIPS * _NP) + dma(3 * _NP) + dma(_NP) + dma(2) + dma(3 * _NSP) + dma(_NSP)
        + [pltpu.SemaphoreType.DMA((1 + (O_KR_END + D_POOL) // 256,))],
        compiler_params=_cparams(collective_id=3),
    )(du, dag, hn, dwq, dwkv, dmeta4, dpw, dg, dgf, dgq, dgkv, dps, loss)


def _adamw_math(w, g, m, v):
    m = B1 * m + (1.0 - B1) * g
    v = B2 * v + (1.0 - B2) * (g * g)
    m_hat = m / C1
    v_hat = v / C2
    delta = -LR * (m_hat / (jnp.sqrt(v_hat) + ADAM_EPS) + WD * w)
    return delta, m, v


def _adamw_rows(name, w, g, m, v, block_rows):
    rows, cols = w.shape

    def body(w_ref, g_ref, m_ref, v_ref, go_ref, d_ref, nm_ref, nv_ref):
        g = g_ref[...]
        go_ref[...] = g
        d_ref[...], nm_ref[...], nv_ref[...] = _adamw_math(w_ref[...], g, m_ref[...], v_ref[...])

    spec = pl.BlockSpec((block_rows, cols), lambda i: (i, 0))
    return pl.pallas_call(
        body,
        name=name,
        grid=(rows // block_rows,),
        in_specs=[spec] * 4,
        out_specs=[spec] * 4,
        out_shape=[jax.ShapeDtypeStruct(w.shape, F32)] * 4,
        compiler_params=_cparams(dimension_semantics=("arbitrary",)),
    )(w, g, m, v)


def _adamw_small(groups):
    n = len(groups)

    def body(*refs):
        ins, outs = refs[:4 * n], refs[4 * n:]
        for t in range(n):
            w_ref, g_ref, m_ref, v_ref = ins[4 * t:4 * t + 4]
            g = g_ref[0:w_ref.shape[0], :]
            outs[4 * t][...] = g
            outs[4 * t + 1][...], outs[4 * t + 2][...], outs[4 * t + 3][...] = _adamw_math(
                w_ref[...], g, m_ref[...], v_ref[...])

    vm = pl.BlockSpec(memory_space=pltpu.VMEM)
    flat = [a for grp in groups for a in grp]
    outs = pl.pallas_call(
        body,
        name="adamw_small",
        in_specs=[vm] * (4 * n),
        out_specs=[vm] * (4 * n),
        out_shape=[jax.ShapeDtypeStruct(grp[0].shape, F32) for grp in groups for _ in range(4)],
        compiler_params=_cparams(),
    )(*flat)
    return [tuple(outs[4 * t:4 * t + 4]) for t in range(n)]


def _rope_tables():
    half = QK_ROPE // 2
    f32 = np.float32
    inv_freq = (f32(1.0) / (f32(ROPE_THETA) ** (np.arange(half, dtype=f32) / f32(half)))).astype(f32)
    pos = np.arange(N, dtype=f32) - f32(PAD)
    ang = (pos[:, None] * inv_freq[None, :]).astype(f32)
    cos, sin = np.cos(ang).astype(f32), np.sin(ang).astype(f32)
    zero = np.zeros((N, 128 - QK_ROPE), f32)
    return jnp.asarray(np.concatenate([cos, cos, zero], axis=1)), jnp.asarray(np.concatenate([-sin, sin, zero], axis=1))


def kernel(x, meta_tokens, norm_g, w_in, q_norm_g, w_q_b, kv_norm_g, w_kv_b, pool_w, pool_scale, w_out, final_norm_g, loss_target, m_meta_tokens, m_norm_g, m_w_in, m_q_norm_g, m_w_q_b, m_kv_norm_g, m_w_kv_b, m_pool_w, m_pool_scale, m_w_out, m_final_norm_g, v_meta_tokens, v_norm_g, v_w_in, v_q_norm_g, v_w_q_b, v_kv_norm_g, v_w_kv_b, v_pool_w, v_pool_scale, v_w_out, v_final_norm_g):
    tr = lambda a: a[0].T
    win, wq, wkv, h, tgt = _gather_weights(tr(w_in), tr(w_q_b), w_kv_b[0], meta_tokens, x[0], loss_target[0])
    cosf, sinf = _rope_tables()
    gf = final_norm_g.reshape(1, D)

    part = _local_step(h, tgt, norm_g, win, q_norm_g, wq, kv_norm_g, wkv, pool_w[0], pool_scale, w_out[0], m_w_out[0],
                       v_w_out[0], gf, cosf, sinf)

    pw2 = lambda a: a.reshape(len(POOL_WINDOWS) * POOL_GROUP, POOL_GROUP)
    gwinT, gwqT, gwkv, gmeta, gpw, gg, ggf, ggq, ggkv, gps, gloss = _reduce_grads(
        part["du"], part["dag"], part["hn"], part["dwq"], part["dwkv"], part["dmeta"], pw2(part["dpw"]), part["dg"],
        part["dgf"], part["dgq"], part["dgkv"], part["dps"], part["loss"])

    r_in = _adamw_rows("adamw_w_in", tr(w_in), gwinT, tr(m_w_in), tr(v_w_in), 248)
    r_out = part["r_out"]
    fn2 = lambda a: a.reshape(1, D)
    r_meta, r_norm, r_gq, r_wq, r_gkv, r_wkv, r_pw, r_ps, r_fn = _adamw_small([
        (meta_tokens, gmeta, m_meta_tokens, v_meta_tokens),
        (norm_g, gg, m_norm_g, v_norm_g),
        (q_norm_g, ggq, m_q_norm_g, v_q_norm_g),
        (tr(w_q_b), gwqT, tr(m_w_q_b), tr(v_w_q_b)),
        (kv_norm_g, ggkv, m_kv_norm_g, v_kv_norm_g),
        (w_kv_b[0], gwkv, m_w_kv_b[0], v_w_kv_b[0]),
        (pw2(pool_w), gpw, pw2(m_pool_w), pw2(v_pool_w)),
        (pool_scale, gps, m_pool_scale, v_pool_scale),
        (fn2(final_norm_g), ggf, fn2(m_final_norm_g), fn2(v_final_norm_g)),
    ])
    untr = lambda a: a.T[None]
    pw4 = lambda a: a.reshape(1, len(POOL_WINDOWS), POOL_GROUP, POOL_GROUP)
    per_kind = [[
        r_meta[kind], r_norm[kind], untr(r_in[kind]), r_gq[kind], untr(r_wq[kind]), r_gkv[kind], r_wkv[kind][None],
        pw4(r_pw[kind]), r_ps[kind], r_out[kind][None], r_fn[kind].reshape(D),
    ] for kind in range(4)]
    return (gloss[0, 0], part["gx"][None], *per_kind[0], *per_kind[1], *per_kind[2], *per_kind[3])
```

```python
import jax
import jax.numpy as jnp
import numpy as np
from jax import lax
from jax.experimental import pallas as pl
from jax.experimental.pallas import tpu as pltpu

F32 = jnp.float32
BF16 = jnp.bfloat16

D = 1024
S = 2048
N_META = 16
PAD = 112
HEAD_ROWS = PAD + N_META
N = HEAD_ROWS + S
D_POOL = 512
POOL_WINDOWS = (2, 4, 8, 16)
POOL_GROUP = 128
HALO = 16
HEADS = 4
QK_NOPE = 128
QK_ROPE = 64
QK = QK_NOPE + QK_ROPE
V_HEAD = 128
Q_LORA = 256
KV_LORA = 128
D_IN = 1984
EPS = 1e-6
ROPE_THETA = 10000.0
SCALE = QK ** -0.5
CHIPS = 4

ROWS_FWD = 544
ROWS_MID = 544
ROWS_BWD = 544
TK = 128
TQ = 256
NQ = S // TQ
HEADS_PER_STEP_BWD = 2

O_PI, O_PG, O_CQ, O_CKV, O_KR, O_AG = 0, 512, 1024, 1280, 1408, 1472
O_KR_END = O_KR + 128
SHARD_IN = D_IN // CHIPS
SHARD_OUT = D // CHIPS

LR, B1, B2, ADAM_EPS, WD, STEP = 0.001, 0.9, 0.999, 1e-08, 0.01, 10
C1 = 1.0 - B1**STEP
C2 = 1.0 - B2**STEP

VMEM_LIMIT = 60 * 1024 * 1024
MESH = pl.DeviceIdType.MESH
NEG = -1e30

VEC_ROWS = 8
V_GQ, V_GKV, V_PS, V_LOSS = 0, 256, 384, 896


def _cparams(**kw):
    return pltpu.CompilerParams(vmem_limit_bytes=VMEM_LIMIT, **kw)


def _nt(a, b):
    return lax.dot_general(a, b, (((1,), (1,)), ((), ())), preferred_element_type=F32)


def _tn(a, b):
    return lax.dot_general(a, b, (((0,), (0,)), ((), ())), preferred_element_type=F32)


def _nn(a, b):
    return jnp.dot(a, b, preferred_element_type=F32)


def _swap64(t):
    return pltpu.roll(t, 32, 1) + pltpu.roll(t, 96, 1)


def _sigmoid(x):
    return 1.0 / (1.0 + jnp.exp(-x))


def _low_lanes():
    return (lax.broadcasted_iota(jnp.int32, (1, 128), 1) < QK_ROPE).astype(F32)


def _rows(w, rows):
    return pl.BlockSpec((rows, w), lambda i: (i, 0))


def _const(*shape):
    return pl.BlockSpec(shape, lambda *_: (0,) * len(shape), pipeline_mode=pl.Buffered(1))


STAT_GROUPS = HEADS // HEADS_PER_STEP_BWD


def _stat_slot(head):
    return head // HEADS_PER_STEP_BWD, head % HEADS_PER_STEP_BWD


def _peer_barrier(x, y, c):
    barrier = pltpu.get_barrier_semaphore()
    peers = [(x, y, 1 - c)] + [(x ^ fx, y ^ fy, c) for fx, fy in _CHIP_RELS[1:]]
    for peer in peers:
        pl.semaphore_signal(barrier, inc=1, device_id=peer, device_id_type=MESH)
    pl.semaphore_wait(barrier, len(peers))


def _attn_tiles():
    return [(0, TK, TK)] + [(TK + TQ * t, TQ, TK + TQ * (t + 1)) for t in range(NQ)]


def _masked_scores(q, k, rows, klen):
    s = _nt(q, k)
    col = lax.broadcasted_iota(jnp.int32, (1, TK), 1)
    head_bias = jnp.where(col >= PAD, 0.0, NEG)
    if klen == TK:
        return s + head_bias
    r = lax.broadcasted_iota(jnp.int32, (rows, 1), 0) >> 6
    c = lax.broadcasted_iota(jnp.int32, (1, rows), 1) >> 6
    diag_bias = jnp.where(c <= r, 0.0, NEG)
    parts = [s[:, 0:TK] + head_bias]
    if klen - rows > TK:
        parts.append(s[:, TK:klen - rows])
    parts.append(s[:, klen - rows:klen] + diag_bias)
    return jnp.concatenate(parts, axis=1)


def _fwd_in(h, norm_g, win, gq, wq, gkv, wkv, cosf, sinf):
    tr = ROWS_FWD

    def body(h_ref, g_ref, win_ref, gq_ref, wq_ref, gkv_ref, wkv_ref, cos_ref, sin_ref,
             pi_ref, pg_ref, cq_ref, ckv_ref, ag_ref, q_ref, k_ref, v_ref):
        h = h_ref[...]
        r = lax.rsqrt(jnp.mean(h * h, axis=-1, keepdims=True) + EPS)
        hn = ((h * r) * g_ref[...]).astype(BF16)
        u = _nt(hn, win_ref[0:O_KR_END, :])
        pi_ref[...] = u[:, O_PI:O_PG]
        pg_ref[...] = u[:, O_PG:O_CQ]
        cq = u[:, O_CQ:O_CKV]
        ckv = u[:, O_CKV:O_KR]
        cq_ref[...] = cq
        ckv_ref[...] = ckv
        ag_ref[...] = _nt(hn, win_ref[O_AG:D_IN, :])
        cosv = cos_ref[...]
        sinv = sin_ref[...]
        kr = u[:, O_KR:O_KR_END] * _low_lanes()
        kr = (kr * cosv + _swap64(kr) * sinv).astype(BF16)
        rq = lax.rsqrt(jnp.mean(cq * cq, axis=-1, keepdims=True) + EPS)
        cqn = ((cq * rq) * gq_ref[...]).astype(BF16)
        rkv = lax.rsqrt(jnp.mean(ckv * ckv, axis=-1, keepdims=True) + EPS)
        ckvn = ((ckv * rkv) * gkv_ref[...]).astype(BF16)
        for hd in range(HEADS):
            qh = _nt(cqn, wq_ref[hd]) * SCALE
            z = qh[:, QK_NOPE:]
            q_ref[hd, :, 0:QK_NOPE] = qh[:, 0:QK_NOPE].astype(BF16)
            q_ref[hd, :, QK_NOPE:] = (z * cosv + _swap64(z) * sinv).astype(BF16)
            kvh = _nn(ckvn, wkv_ref[hd])
            k_ref[hd, :, 0:QK_NOPE] = kvh[:, 0:QK_NOPE].astype(BF16)
            k_ref[hd, :, QK_NOPE:] = kr
            v_ref[hd] = kvh[:, QK_NOPE:].astype(BF16)

    head = lambda w: pl.BlockSpec((HEADS, tr, w), lambda i: (0, i, 0))
    return pl.pallas_call(
        body,
        name="fwd_in",
        grid=(N // tr,),
        in_specs=[
            _rows(D, tr), _const(1, D), _const(D_IN, D), _const(1, Q_LORA), _const(HEADS, 256, Q_LORA),
            _const(1, KV_LORA), _const(HEADS, KV_LORA, 256), _rows(128, tr), _rows(128, tr),
        ],
        out_specs=[_rows(D_POOL, tr), _rows(D_POOL, tr), _rows(Q_LORA, tr), _rows(KV_LORA, tr), _rows(D_POOL, tr),
                   head(256), head(256), head(V_HEAD)],
        out_shape=[
            jax.ShapeDtypeStruct((N, D_POOL), F32), jax.ShapeDtypeStruct((N, D_POOL), F32),
            jax.ShapeDtypeStruct((N, Q_LORA), F32), jax.ShapeDtypeStruct((N, KV_LORA), F32),
            jax.ShapeDtypeStruct((N, D_POOL), F32),
            jax.ShapeDtypeStruct((HEADS, N, 256), BF16), jax.ShapeDtypeStruct((HEADS, N, 256), BF16),
            jax.ShapeDtypeStruct((HEADS, N, V_HEAD), BF16),
        ],
        compiler_params=_cparams(dimension_semantics=("arbitrary",)),
    )(h, norm_g, win, gq, wq, gkv, wkv, cosf, sinf)


def _attn_fwd(q, k, v, wout_s):
    tiles = _attn_tiles()
    n_t = len(tiles)
    half = SHARD_OUT // 2
    fwd_step = n_t - 2

    def body(q_hbm, k_hbm, v_hbm, wout_ref, o_hbm, lse_ref, wout_o, q_buf, k_buf, v_buf, o_buf, s_wout, in_sems, out_sems,
             ici_send, ici_recv, fwd_send, fwd_recv, own_sem):
        step = pl.program_id(0)
        x, y, c = lax.axis_index("x"), lax.axis_index("y"), lax.axis_index("c")
        me = 2 * x + y

        def chip_of(rel):
            fx, fy = _CHIP_RELS[rel]
            return 2 * (x ^ fx) + (y ^ fy)

        def place(chip, core):
            return wout_o.at[pl.ds(pl.multiple_of(SHARD_OUT * chip + half * core, half), half), :]

        def ici_copy(rel, src_chip, to):
            return _remote(s_wout.at[pl.ds(pl.multiple_of(half * c, half), half), :], place(src_chip, c),
                           ici_send.at[rel - 1], ici_recv.at[rel - 1], to)

        def fwd_copy(rel, core, to):
            spot = place(chip_of(rel), core)
            return _remote(spot, spot, fwd_send.at[rel - 1], fwd_recv.at[rel - 1], to)

        own = pltpu.make_async_copy(s_wout, wout_o.at[pl.ds(pl.multiple_of(SHARD_OUT * me, SHARD_OUT), SHARD_OUT), :], own_sem)

        @pl.when(step == 0)
        def _():
            _peer_barrier(x, y, c)
            s_wout[...] = wout_ref[...].astype(BF16)
            own.start()
            for rel in (1, 2, 3):
                fx, fy = _CHIP_RELS[rel]
                ici_copy(rel, me, (x ^ fx, y ^ fy, c)).start()

        @pl.when(step == fwd_step)
        def _():
            for rel in (1, 2, 3):
                ici_copy(rel, chip_of(rel), (x, y, c)).wait_recv()
                fwd_copy(rel, c, (x, y, 1 - c)).start()

        def finish_wout():
            for rel in (1, 2, 3):
                fwd_copy(rel, 1 - c, (x, y, c)).wait_recv()
            for rel in (1, 2, 3):
                ici_copy(rel, me, (x, y, c)).wait_send()
                fwd_copy(rel, c, (x, y, c)).wait_send()
            own.wait()

        def loads(idx):
            q0, rows, _ = tiles[idx]
            rs = pl.ds(q0, rows)
            return [pltpu.make_async_copy(src.at[:, rs, :], dst.at[:, rs, :], in_sems.at[a, idx % 2])
                    for a, (src, dst) in enumerate(((q_hbm, q_buf), (k_hbm, k_buf), (v_hbm, v_buf)))]

        def store(idx):
            q0, rows, _ = tiles[idx]
            return pltpu.make_async_copy(o_buf.at[idx % 2, pl.ds(0, rows), :], o_hbm.at[pl.ds(q0, rows), :],
                                         out_sems.at[idx % 2])

        @pl.when(step == 0)
        def _():
            lse_ref[...] = jnp.zeros_like(lse_ref)
            for cp in loads(0):
                cp.start()

        for idx, (q0, rows, klen) in enumerate(tiles):
            @pl.when(step == idx)
            def _(idx=idx, q0=q0, rows=rows, klen=klen):
                for cp in loads(idx):
                    cp.wait()
                if idx + 1 < n_t:
                    for cp in loads(idx + 1):
                        cp.start()
                if idx >= 2:
                    store(idx - 2).wait()
                for hd in range(HEADS):
                    s = _masked_scores(q_buf[hd, q0:q0 + rows, :], k_buf[hd, 0:klen, :], rows, klen)
                    m = jnp.max(s, axis=-1, keepdims=True)
                    p = jnp.exp(s - m)
                    l = jnp.sum(p, axis=-1, keepdims=True)
                    o_buf[idx % 2, 0:rows, hd * V_HEAD:(hd + 1) * V_HEAD] = _nn(p.astype(BF16), v_buf[hd, 0:klen, :]) / l
                    grp, lane = _stat_slot(hd)
                    lse_ref[grp, q0:q0 + rows, lane:lane + 1] = m + jnp.log(l)
                store(idx).start()
                if idx == n_t - 1:
                    store(idx - 1).wait()
                    store(idx).wait()
                    finish_wout()

    hbm = pl.BlockSpec(memory_space=pl.ANY)
    return pl.pallas_call(
        body,
        name="attn_fwd",
        grid=(n_t,),
        in_specs=[hbm, hbm, hbm, _const(SHARD_OUT, D)],
        out_specs=[hbm, _const(STAT_GROUPS, N, 128), hbm],
        out_shape=[jax.ShapeDtypeStruct((N, HEADS * V_HEAD), F32), jax.ShapeDtypeStruct((STAT_GROUPS, N, 128), F32),
                   jax.ShapeDtypeStruct((D, D), BF16)],
        scratch_shapes=[pltpu.VMEM((HEADS, N, 256), BF16), pltpu.VMEM((HEADS, N, 256), BF16),
                        pltpu.VMEM((HEADS, N, V_HEAD), BF16), pltpu.VMEM((2, TQ, HEADS * V_HEAD), F32),
                        pltpu.VMEM((SHARD_OUT, D), BF16),
                        pltpu.SemaphoreType.DMA((3, 2)), pltpu.SemaphoreType.DMA((2,))]
        + [pltpu.SemaphoreType.DMA((3,))] * 4 + [pltpu.SemaphoreType.DMA],
        compiler_params=_cparams(dimension_semantics=("arbitrary",), collective_id=1),
    )(q, k, v, wout_s)


def _inv_count(row0, rows, w):
    row = row0 + lax.broadcasted_iota(jnp.int32, (rows, 1), 0)
    return 1.0 / jnp.clip(row - (PAD - 1), 1, w).astype(F32)


def _mid(h, tgt, pool_in, pool_gate, attn_gate, attn, pool_w, pool_scale, wout, gf):
    tr = ROWS_MID
    per = tr // HALO
    ng = len(POOL_WINDOWS)

    def body(h_ref, t_ref, pin_ref, halo_ref, pg_ref, ag_ref, at_ref, pw_ref, ps_ref, wout_ref, gf_ref,
             dh2_ref, do_ref, delta_ref, dag_ref, dpg_ref, dpl_ref, dwout_ref, dpw_ref, dps_ref, dgf_ref, loss_ref):
        i = pl.program_id(0)

        @pl.when(i == 0)
        def _():
            dwout_ref[...] = jnp.zeros_like(dwout_ref)
            dpw_ref[...] = jnp.zeros_like(dpw_ref)
            dps_ref[...] = jnp.zeros_like(dps_ref)
            dgf_ref[...] = jnp.zeros_like(dgf_ref)
            loss_ref[...] = jnp.zeros_like(loss_ref)

        row0 = i * tr
        real = (row0 + lax.broadcasted_iota(jnp.int32, (tr, 1), 0)) >= HEAD_ROWS
        h = h_ref[...]

        halo = jnp.where(i > 0, halo_ref[...], 0.0)
        ext = jnp.concatenate([halo, pin_ref[...]], axis=0)
        pooled = []
        for g, w in enumerate(POOL_WINDOWS):
            e = ext[:, g * POOL_GROUP:(g + 1) * POOL_GROUP]
            acc = e
            shift = 1
            while shift < w:
                acc = acc + pltpu.roll(acc, shift, 0)
                shift *= 2
            pooled.append((acc[HALO:] * _inv_count(row0, tr, w) - e[HALO:]).astype(BF16))
        pw = [pw_ref[g].astype(BF16) for g in range(ng)]
        mixed = jnp.concatenate([_nn(pooled[g], pw[g]) for g in range(ng)], axis=1)
        ps = ps_ref[...]
        mixed_s = mixed * ps
        pg = pg_ref[...]
        sig_p = _sigmoid(pg)
        silu_p = pg * sig_p
        pool_out = (silu_p * mixed_s).astype(BF16)
        ag = ag_ref[...]
        sig_a = _sigmoid(ag)
        silu_a = ag * sig_a
        at = at_ref[...]
        attn_out = (silu_a * at).astype(BF16)
        cat = jnp.concatenate([pool_out, attn_out], axis=1)
        h2 = h + _nn(cat, wout_ref[...])

        r2 = lax.rsqrt(jnp.mean(h2 * h2, axis=-1, keepdims=True) + EPS)
        n2 = h2 * r2
        gfv = gf_ref[...]
        err = jnp.where(real, n2 * gfv - t_ref[...], 0.0)
        loss_ref[...] += jnp.sum(jnp.sum(err * err, axis=-1, keepdims=True), axis=0, keepdims=True) * (0.5 / D)
        dy = err * (1.0 / D)
        dgf_ref[...] += jnp.sum(dy * n2, axis=0, keepdims=True)
        dn = dy * gfv
        dh2 = r2 * (dn - n2 * jnp.mean(dn * n2, axis=-1, keepdims=True))
        dh2_ref[...] = dh2
        dh2b = dh2.astype(BF16)

        dwout_ref[...] += _tn(cat, dh2b)
        dcat = _nt(dh2b, wout_ref[...])
        dpo = dcat[:, 0:D_POOL]
        dao = dcat[:, D_POOL:D]
        do = dao * silu_a
        prod = do * at
        delta_ref[...] = jnp.zeros_like(delta_ref)
        for hd in range(HEADS):
            grp, lane = _stat_slot(hd)
            cols = slice(hd * V_HEAD, (hd + 1) * V_HEAD)
            do_ref[grp, :, lane * V_HEAD:(lane + 1) * V_HEAD] = do[:, cols].astype(BF16)
            delta_ref[grp, :, lane:lane + 1] = jnp.sum(prod[:, cols], axis=-1, keepdims=True)
        dag_ref[...] = (dao * at * (sig_a * (1.0 + ag * (1.0 - sig_a)))).astype(BF16)
        dmixed_s = dpo * silu_p
        dpg_ref[...] = (dpo * mixed_s * (sig_p * (1.0 + pg * (1.0 - sig_p)))).astype(BF16)
        dps_ref[...] += jnp.sum(dmixed_s * mixed, axis=0, keepdims=True)
        dmixed = (dmixed_s * ps).astype(BF16)
        dpl = []
        for g in range(ng):
            dm = dmixed[:, g * POOL_GROUP:(g + 1) * POOL_GROUP]
            dpl.append(_nt(dm, pw[g]))
            dpw_ref[g] += _tn(pooled[g], dm)
        dpl_ref[...] = jnp.concatenate(dpl, axis=1)

    halo_spec = pl.BlockSpec((HALO, D_POOL), lambda i: (jnp.maximum(i * per - 1, 0), 0))
    return pl.pallas_call(
        body,
        name="mid",
        grid=(N // tr,),
        in_specs=[
            _rows(D, tr), _rows(D, tr), _rows(D_POOL, tr), halo_spec, _rows(D_POOL, tr), _rows(D_POOL, tr),
            _rows(D_POOL, tr), _const(ng, POOL_GROUP, POOL_GROUP), _const(1, D_POOL), _const(D, D), _const(1, D),
        ],
        out_specs=[
            _rows(D, tr), pl.BlockSpec((STAT_GROUPS, tr, HEADS_PER_STEP_BWD * V_HEAD), lambda i: (0, i, 0)),
            pl.BlockSpec((STAT_GROUPS, tr, 128), lambda i: (0, i, 0)),
            _rows(D_POOL, tr), _rows(D_POOL, tr), _rows(D_POOL, tr),
            _const(D, D), _const(ng, POOL_GROUP, POOL_GROUP), _const(1, D_POOL), _const(1, D), _const(1, 128),
        ],
        out_shape=[
            jax.ShapeDtypeStruct((N, D), F32), jax.ShapeDtypeStruct((STAT_GROUPS, N, HEADS_PER_STEP_BWD * V_HEAD), BF16),
            jax.ShapeDtypeStruct((STAT_GROUPS, N, 128), F32),
            jax.ShapeDtypeStruct((N, D_POOL), BF16), jax.ShapeDtypeStruct((N, D_POOL), BF16),
            jax.ShapeDtypeStruct((N, D_POOL), F32), jax.ShapeDtypeStruct((D, D), F32),
            jax.ShapeDtypeStruct((ng, POOL_GROUP, POOL_GROUP), F32),
            jax.ShapeDtypeStruct((1, D_POOL), F32), jax.ShapeDtypeStruct((1, D), F32), jax.ShapeDtypeStruct((1, 128), F32),
        ],
        compiler_params=_cparams(dimension_semantics=("arbitrary",)),
    )(h, tgt, pool_in, pool_in, pool_gate, attn_gate, attn, pool_w, pool_scale, wout, gf)


def _unrope(dy, cosv, sinv):
    return dy * cosv + _swap64(dy * sinv) * _low_lanes()


def _attn_bwd(q, k, v, do, lse, delta, cosf, sinf, dwout):
    tiles = _attn_tiles()
    hp = HEADS_PER_STEP_BWD
    n_g = HEADS // hp
    n_t = len(tiles)
    half = SHARD_OUT // 2
    send_at, sum_at = (0, 2), (n_g - 1, n_t // 2)

    def body(q_hbm, k_hbm, v_hbm, do_hbm, lse_ref, delta_ref, cos_ref, sin_ref, dwout_hbm, dq_hbm, dkv_ref, dkr_ref,
             gwout_ref, q_buf, k_buf, v_buf, do_buf, dq_buf, dk_acc, dv_acc, own_w, sib_w, stage_w, recv_w, gw_buf,
             in_sems, out_sems, ow_sems, d2d_send, d2d_recv, ici_send, ici_recv, fin_send, fin_recv):
        grp = pl.program_id(0)
        step = pl.program_id(1)
        heads = pl.ds(grp * hp, hp)
        x, y, c = lax.axis_index("x"), lax.axis_index("y"), lax.axis_index("c")
        sibling = (x, y, 1 - c)

        def chip_of(rel):
            fx, fy = _CHIP_RELS[rel]
            return 2 * (x ^ fx) + (y ^ fy)

        def piece(chip, core):
            return dwout_hbm.at[pl.ds(pl.multiple_of(SHARD_OUT * chip + half * core, half), half), :]

        def own_load(rel):
            return pltpu.make_async_copy(piece(chip_of(rel), c), own_w.at[rel], ow_sems.at[rel])

        def d2d_copy(rel):
            return _remote(piece(chip_of(rel), 1 - c), sib_w.at[rel], d2d_send.at[rel], d2d_recv.at[rel], sibling)

        def ici_copy(rel):
            fx, fy = _CHIP_RELS[rel]
            return _remote(stage_w.at[rel - 1], recv_w.at[rel - 1], ici_send.at[rel - 1], ici_recv.at[rel - 1],
                           (x ^ fx, y ^ fy, c))

        def fin_copy(core):
            spot = gw_buf.at[pl.ds(pl.multiple_of(half * core, half), half), :]
            return _remote(spot, spot, fin_send.at[0], fin_recv.at[0], sibling)

        @pl.when((grp == 0) & (step == 0))
        def _():
            _peer_barrier(x, y, c)
            for rel in (1, 2, 3, 0):
                d2d_copy(rel).start()
                own_load(rel).start()

        @pl.when((grp == send_at[0]) & (step == send_at[1]))
        def _():
            for rel in (1, 2, 3):
                own_load(rel).wait()
                d2d_copy(rel).wait_recv()
                stage_w[rel - 1] = (own_w[rel] + sib_w[rel]).astype(BF16)
                ici_copy(rel).start()

        @pl.when((grp == sum_at[0]) & (step == sum_at[1]))
        def _():
            own_load(0).wait()
            d2d_copy(0).wait_recv()
            total = own_w[0] + sib_w[0]
            for rel in (1, 2, 3):
                ici_copy(rel).wait_recv()
                total = total + recv_w[rel - 1].astype(F32)
            gw_buf[pl.ds(pl.multiple_of(half * c, half), half), :] = total
            fin_copy(c).start()

        def finish_dwout():
            fin_copy(1 - c).wait_recv()
            for rel in (0, 1, 2, 3):
                d2d_copy(rel).wait_send()
            for rel in (1, 2, 3):
                ici_copy(rel).wait_send()
            fin_copy(c).wait_send()
            gwout_ref[...] = gw_buf[...]

        def loads(g, idx):
            q0, rows, _ = tiles[idx]
            rs = pl.ds(q0, rows)
            par = (g * n_t + idx) % 2
            hs = pl.ds(g * hp, hp)
            pairs = ((q_hbm.at[hs, rs, :], q_buf.at[:, rs, :]), (k_hbm.at[hs, rs, :], k_buf.at[:, rs, :]),
                     (v_hbm.at[hs, rs, :], v_buf.at[:, rs, :]), (do_hbm.at[g, rs, :], do_buf.at[rs, :]))
            return [pltpu.make_async_copy(src, dst, in_sems.at[a, par]) for a, (src, dst) in enumerate(pairs)]

        def store(idx):
            q0, rows, _ = tiles[idx]
            return pltpu.make_async_copy(dq_buf.at[idx % 2, :, pl.ds(0, rows), :], dq_hbm.at[heads, pl.ds(q0, rows), :],
                                         out_sems.at[idx % 2])

        @pl.when(step == 0)
        def _():
            dk_acc[...] = jnp.zeros_like(dk_acc)
            dv_acc[...] = jnp.zeros_like(dv_acc)

        @pl.when((step == 0) & (grp == 0))
        def _():
            dkr_ref[...] = jnp.zeros_like(dkr_ref)
            for cp in loads(grp, 0):
                cp.start()

        for idx, (q0, rows, klen) in enumerate(tiles):
            @pl.when(step == idx)
            def _(idx=idx, q0=q0, rows=rows, klen=klen):
                for cp in loads(grp, idx):
                    cp.wait()
                if idx + 1 < n_t:
                    for cp in loads(grp, idx + 1):
                        cp.start()
                if idx >= 2:
                    store(idx - 2).wait()
                qs = pl.ds(q0, rows)
                for hd in range(hp):
                    qv = q_buf[hd, qs, :]
                    kv = k_buf[hd, 0:klen, :]
                    p = jnp.exp(_masked_scores(qv, kv, rows, klen) - lse_ref[0, qs, hd:hd + 1])
                    dob = do_buf[qs, hd * V_HEAD:(hd + 1) * V_HEAD]
                    ds = (p * (_nt(dob, v_buf[hd, 0:klen, :]) - delta_ref[0, qs, hd:hd + 1])).astype(BF16)
                    dq = _nn(ds, kv) * SCALE
                    dq_buf[idx % 2, hd, 0:rows, 0:QK_NOPE] = dq[:, 0:QK_NOPE].astype(BF16)
                    dq_buf[idx % 2, hd, 0:rows, QK_NOPE:] = _unrope(dq[:, QK_NOPE:], cos_ref[qs, :], sin_ref[qs, :]).astype(BF16)
                    dk_acc[hd, 0:klen, :] += _tn(ds, qv)
                    dv_acc[hd, 0:klen, :] += _tn(p.astype(BF16), dob)
                store(idx).start()

        @pl.when(step == n_t - 1)
        def _():
            @pl.when(grp + 1 < n_g)
            def _():
                for cp in loads(grp + 1, 0):
                    cp.start()

            for hd in range(hp):
                dkv_ref[hd, :, 0:QK_NOPE] = dk_acc[hd, :, 0:QK_NOPE].astype(BF16)
                dkv_ref[hd, :, QK_NOPE:] = dv_acc[hd].astype(BF16)
                dkr_ref[...] += dk_acc[hd, :, QK_NOPE:]
            store(n_t - 2).wait()
            store(n_t - 1).wait()

            @pl.when(grp == n_g - 1)
            def _():
                finish_dwout()

    hbm = pl.BlockSpec(memory_space=pl.ANY)
    stat = pl.BlockSpec((1, N, 128), lambda g, t: (g, 0, 0), pipeline_mode=pl.Buffered(1))
    piece_f32 = lambda lead: pltpu.VMEM((lead, half, D), F32)
    piece_bf16 = lambda lead: pltpu.VMEM((lead, half, D), BF16)
    return pl.pallas_call(
        body,
        name="attn_bwd",
        grid=(n_g, n_t),
        in_specs=[hbm, hbm, hbm, hbm, stat, stat, _const(N, 128), _const(N, 128), hbm],
        out_specs=[hbm, pl.BlockSpec((hp, N, 256), lambda g, t: (g, 0, 0), pipeline_mode=pl.Buffered(1)), _const(N, 128),
                   _const(SHARD_OUT, D)],
        out_shape=[
            jax.ShapeDtypeStruct((HEADS, N, 256), BF16), jax.ShapeDtypeStruct((HEADS, N, 256), BF16),
            jax.ShapeDtypeStruct((N, 128), F32), jax.ShapeDtypeStruct((SHARD_OUT, D), F32),
        ],
        scratch_shapes=[pltpu.VMEM((hp, N, 256), BF16), pltpu.VMEM((hp, N, 256), BF16), pltpu.VMEM((hp, N, V_HEAD), BF16),
                        pltpu.VMEM((N, hp * V_HEAD), BF16), pltpu.VMEM((2, hp, TQ, 256), BF16),
                        pltpu.VMEM((hp, N, 256), F32), pltpu.VMEM((hp, N, V_HEAD), F32),
                        piece_f32(CHIPS), piece_f32(CHIPS), piece_bf16(3), piece_bf16(3), pltpu.VMEM((SHARD_OUT, D), F32),
                        pltpu.SemaphoreType.DMA((4, 2)), pltpu.SemaphoreType.DMA((2,)), pltpu.SemaphoreType.DMA((CHIPS,)),
                        pltpu.SemaphoreType.DMA((CHIPS,)), pltpu.SemaphoreType.DMA((CHIPS,)),
                        pltpu.SemaphoreType.DMA((3,)), pltpu.SemaphoreType.DMA((3,)),
                        pltpu.SemaphoreType.DMA((1,)), pltpu.SemaphoreType.DMA((1,))],
        compiler_params=_cparams(dimension_semantics=("arbitrary", "arbitrary"), collective_id=2),
    )(q, k, v, do, lse, delta, cosf, sinf, dwout)


def _bwd_in(h, dh2, dq, dkv, dkr, cq, ckv, dpl, dpg, dag, norm_g, win, gq, wq, gkv, wkv, cosf, sinf, adam_out):
    tr = ROWS_BWD
    nb = N // tr
    per = tr // HALO
    lead = HEAD_ROWS
    adam_rows = SHARD_OUT // nb

    def body(h_ref, dh2_ref, dq_ref, dkv_ref, dkr_ref, cq_ref, ckv_ref, dpl_ref, halo_ref, dpg_ref, dag_ref,
             g_ref, win_ref, gq_ref, wq_ref, gkv_ref, wkv_ref, cos_ref, sin_ref, aw_ref, ag_ref, am_ref, av_ref,
             gx_ref, dmeta_ref, dwin_ref, dwq_ref, dwkv_ref, dg_ref, dgq_ref, dgkv_ref, ago_ref, ad_ref, anm_ref, anv_ref,
             dh_buf, gx_sem):
        i = pl.program_id(0)
        grad_out = ag_ref[...]
        ago_ref[...] = grad_out
        ad_ref[...], anm_ref[...], anv_ref[...] = _adamw_math(aw_ref[...], grad_out, am_ref[...], av_ref[...])

        @pl.when(i == 0)
        def _():
            dwin_ref[...] = jnp.zeros_like(dwin_ref)
            dwq_ref[...] = jnp.zeros_like(dwq_ref)
            dwkv_ref[...] = jnp.zeros_like(dwkv_ref)
            dg_ref[...] = jnp.zeros_like(dg_ref)
            dgq_ref[...] = jnp.zeros_like(dgq_ref)
            dgkv_ref[...] = jnp.zeros_like(dgkv_ref)

        row0 = i * tr
        h = h_ref[...]
        r = lax.rsqrt(jnp.mean(h * h, axis=-1, keepdims=True) + EPS)
        n = h * r
        gv = g_ref[...]
        hn = (n * gv).astype(BF16)
        cq = cq_ref[...]
        rq = lax.rsqrt(jnp.mean(cq * cq, axis=-1, keepdims=True) + EPS)
        nq = cq * rq
        gqv = gq_ref[...]
        cqn = (nq * gqv).astype(BF16)
        dcqn = jnp.zeros((tr, Q_LORA), F32)
        for hd in range(HEADS):
            dqf = dq_ref[hd]
            dcqn = dcqn + _nn(dqf, wq_ref[hd])
            dwq_ref[hd] += _tn(dqf, cqn)
        dgq_ref[...] += jnp.sum(dcqn * nq, axis=0, keepdims=True)
        dnq = dcqn * gqv
        dcq = rq * (dnq - nq * jnp.mean(dnq * nq, axis=-1, keepdims=True))

        ckv = ckv_ref[...]
        rkv = lax.rsqrt(jnp.mean(ckv * ckv, axis=-1, keepdims=True) + EPS)
        nkv = ckv * rkv
        gkvv = gkv_ref[...]
        ckvn = (nkv * gkvv).astype(BF16)
        dckvn = jnp.zeros((tr, KV_LORA), F32)
        for hd in range(HEADS):
            dkv = dkv_ref[hd]
            dckvn = dckvn + _nt(dkv, wkv_ref[hd])
            dwkv_ref[hd] += _tn(ckvn, dkv)
        dgkv_ref[...] += jnp.sum(dckvn * nkv, axis=0, keepdims=True)
        dnkv = dckvn * gkvv
        dckv = rkv * (dnkv - nkv * jnp.mean(dnkv * nkv, axis=-1, keepdims=True))
        dkr = _unrope(dkr_ref[...], cos_ref[...], sin_ref[...])

        cur = dpl_ref[...]
        halo = jnp.where(i < nb - 1, halo_ref[...], 0.0)
        dpi = []
        for g, w in enumerate(POOL_WINDOWS):
            sl = slice(g * POOL_GROUP, (g + 1) * POOL_GROUP)
            a = jnp.concatenate([cur[:, sl] * _inv_count(row0, tr, w), halo[:, sl] * _inv_count(row0 + tr, HALO, w)], axis=0)
            acc = a
            shift = 1
            while shift < w:
                acc = acc + pltpu.roll(acc, tr + HALO - shift, 0)
                shift *= 2
            dpi.append(acc[0:tr] - cur[:, sl])

        du = jnp.concatenate([t.astype(BF16) for t in dpi] + [dpg_ref[...]] + [t.astype(BF16) for t in (dcq, dckv, dkr)],
                             axis=1)
        dagb = dag_ref[...]
        dwin_ref[0:O_KR_END, :] += _tn(du, hn)
        dwin_ref[O_AG:D_IN, :] += _tn(dagb, hn)
        dhn = _nn(du, win_ref[0:O_KR_END, :]) + _nn(dagb, win_ref[O_AG:D_IN, :])
        dg_ref[...] += jnp.sum(dhn * n, axis=0, keepdims=True)
        dn = dhn * gv
        dh = dh2_ref[...] + r * (dn - n * jnp.mean(dn * n, axis=-1, keepdims=True))

        first = pltpu.make_async_copy(dh_buf.at[pl.ds(lead, tr - lead), :], gx_ref.at[pl.ds(0, tr - lead), :], gx_sem)
        later = lambda step: pltpu.make_async_copy(
            dh_buf, gx_ref.at[pl.ds(pl.multiple_of(step * tr - lead, 16), tr), :], gx_sem)

        @pl.when(i == 1)
        def _():
            first.wait()

        @pl.when(i > 1)
        def _():
            later(i - 1).wait()

        dh_buf[...] = dh

        @pl.when(i == 0)
        def _():
            first.start()
            for chip in range(CHIPS):
                dmeta_ref[chip] = dh[PAD:HEAD_ROWS, chip * 256:(chip + 1) * 256]

        @pl.when(i > 0)
        def _():
            later(i).start()

        @pl.when(i == nb - 1)
        def _():
            later(i).wait()

    head = lambda w: pl.BlockSpec((HEADS, tr, w), lambda i: (0, i, 0))
    halo_spec = pl.BlockSpec((HALO, D_POOL), lambda i: (jnp.minimum((i + 1) * per, N // HALO - 1), 0))
    return pl.pallas_call(
        body,
        name="bwd_in",
        grid=(nb,),
        in_specs=[
            _rows(D, tr), _rows(D, tr), head(256), head(256), _rows(128, tr), _rows(Q_LORA, tr), _rows(KV_LORA, tr),
            _rows(D_POOL, tr), halo_spec, _rows(D_POOL, tr), _rows(D_POOL, tr),
            _const(1, D), _const(D_IN, D), _const(1, Q_LORA), _const(HEADS, 256, Q_LORA),
            _const(1, KV_LORA), _const(HEADS, KV_LORA, 256), _rows(128, tr), _rows(128, tr),
        ] + [_rows(D, adam_rows)] * 4,
        out_specs=[
            pl.BlockSpec(memory_space=pl.ANY), _const(CHIPS, N_META, 256), _const(D_IN, D), _const(HEADS, 256, Q_LORA),
            _const(HEADS, KV_LORA, 256), _const(1, D), _const(1, Q_LORA), _const(1, KV_LORA),
        ] + [_rows(D, adam_rows)] * 4,
        out_shape=[
            jax.ShapeDtypeStruct((S, D), F32), jax.ShapeDtypeStruct((CHIPS, N_META, 256), F32),
            jax.ShapeDtypeStruct((D_IN, D), F32), jax.ShapeDtypeStruct((HEADS, 256, Q_LORA), F32),
            jax.ShapeDtypeStruct((HEADS, KV_LORA, 256), F32),
            jax.ShapeDtypeStruct((1, D), F32), jax.ShapeDtypeStruct((1, Q_LORA), F32), jax.ShapeDtypeStruct((1, KV_LORA), F32),
        ] + [jax.ShapeDtypeStruct((SHARD_OUT, D), F32)] * 4,
        scratch_shapes=[pltpu.VMEM((tr, D), F32), pltpu.SemaphoreType.DMA],
        compiler_params=_cparams(dimension_semantics=("arbitrary",)),
    )(h, dh2, dq, dkv, dkr, cq, ckv, dpl, dpl, dpg, dag, norm_g, win, gq, wq, gkv, wkv, cosf, sinf, *adam_out)


def _local_step(h, tgt, norm_g, win, gq, wq, gkv, wkv, pool_w, pool_scale, wout_s, m_wout_s, v_wout_s, gf, cosf, sinf):
    pool_in, pool_gate, cq, ckv, attn_gate, q, k, v = _fwd_in(h, norm_g, win, gq, wq, gkv, wkv, cosf, sinf)
    attn, lse, wout = _attn_fwd(q, k, v, wout_s)
    dh2, do, delta, dag, dpg, dpl, dwout, dpw, dps, dgf, loss = _mid(
        h, tgt, pool_in, pool_gate, attn_gate, attn, pool_w, pool_scale, wout, gf)
    dq, dkv, dkr, gwout = _attn_bwd(q, k, v, do, lse, delta, cosf, sinf, dwout)
    gx, dmeta, dwin, dwq, dwkv, dg, dgq, dgkv, *r_out = _bwd_in(
        h, dh2, dq, dkv, dkr, cq, ckv, dpl, dpg, dag, norm_g, win, gq, wq, gkv, wkv, cosf, sinf,
        (wout_s, gwout, m_wout_s, v_wout_s))
    return dict(gx=gx, dmeta=dmeta, dwin=dwin, dwq=dwq, dwkv=dwkv, r_out=tuple(r_out), dg=dg, dgq=dgq, dgkv=dgkv,
                dpw=dpw, dps=dps, dgf=dgf, loss=loss)


_CHIP_RELS = ((0, 0), (1, 0), (0, 1), (1, 1))

_ARR_ROWS = (SHARD_IN, SHARD_OUT, 256, KV_LORA, N_META)
_ARR_COLS = (D, D, Q_LORA, 256, 256)
_PIECES = (
    (0, 0, 256, 0), (0, 256, SHARD_IN - 256, 1),
    (1, 0, 128, 0), (1, 128, 128, 1),
    (2, 0, 128, 0), (2, 128, 128, 1),
    (3, 0, 64, 0), (3, 64, 64, 1),
    (4, 0, N_META, 0),
)
_NP = len(_PIECES)
_PIECE_MAX = (256, 128, 128, 64, N_META)


def _gathered_at(refs, arr, chip, r0, n):
    if arr in (0, 1):
        return refs[arr].at[pl.ds(pl.multiple_of(_ARR_ROWS[arr] * chip + r0, 16), n), :]
    return refs[arr].at[chip, pl.ds(r0, n), :]


def _remote(src, dst, send_sem, recv_sem, to):
    return pltpu.make_async_remote_copy(src_ref=src, dst_ref=dst, send_sem=send_sem, recv_sem=recv_sem,
                                        device_id=to, device_id_type=MESH)


def _gather_weights(winT_s, wqT_s, wkv_s, meta_s, x2, tgt2):
    arrays = (0, 2, 3, 4)

    def body(win_ref, wq_ref, wkv_ref, meta_ref, x_ref, t_ref, win_o, wq_o, wkv_o, h_o, tp_o,
             s_win, s_wq, s_wkv, meta_all, head_buf, x_buf, t_buf, ici_send, ici_recv, fwd_send, fwd_recv,
             loc_sems, own_sems):
        x, y, c = lax.axis_index("x"), lax.axis_index("y"), lax.axis_index("c")
        me = 2 * x + y
        stage = (s_win, None, s_wq, s_wkv, meta_ref)
        outs = (win_o, None, wq_o, wkv_o, meta_all)

        _peer_barrier(x, y, c)

        frames = pl.ds(HEAD_ROWS, S)
        loads = [pltpu.make_async_copy(x_ref, x_buf, loc_sems.at[0]), pltpu.make_async_copy(t_ref, t_buf, loc_sems.at[1])]
        local = [pltpu.make_async_copy(x_buf, h_o.at[frames, :], loc_sems.at[0]),
                 pltpu.make_async_copy(t_buf, tp_o.at[frames, :], loc_sems.at[1])]
        for cp in loads:
            cp.start()

        s_win[...] = win_ref[...].astype(BF16)
        s_wq[0:QK, :] = wq_ref[...].astype(BF16)
        s_wq[QK:256, :] = jnp.zeros((256 - QK, Q_LORA), BF16)
        s_wkv[...] = wkv_ref[...].astype(BF16)

        def chip_of(rel):
            fx, fy = _CHIP_RELS[rel]
            return 2 * (x ^ fx) + (y ^ fy)

        def same_core_of(rel):
            fx, fy = _CHIP_RELS[rel]
            return (x ^ fx, y ^ fy, c)

        def ici_copy(rel, i, src_chip, to):
            arr, r0, n, _ = _PIECES[i]
            k = (rel - 1) * _NP + i
            return _remote(stage[arr].at[pl.ds(r0, n), :], _gathered_at(outs, arr, src_chip, r0, n),
                           ici_send.at[k], ici_recv.at[k], to)

        def fwd_copy(rel, i, to):
            arr, r0, n, _ = _PIECES[i]
            k = (rel - 1) * _NP + i
            place = _gathered_at(outs, arr, chip_of(rel), r0, n)
            return _remote(place, place, fwd_send.at[k], fwd_recv.at[k], to)

        for core in (0, 1):
            @pl.when(c == core)
            def _(core=core):
                mine = [i for i in range(_NP) if _PIECES[i][3] == core and _PIECES[i][0] in arrays]
                theirs = [i for i in range(_NP) if _PIECES[i][3] != core and _PIECES[i][0] in arrays]
                sends = [ici_copy(rel, i, me, same_core_of(rel)) for rel in (1, 2, 3) for i in mine]
                for cp in sends:
                    cp.start()
                for ld, st in zip(loads, local):
                    ld.wait()
                    st.start()
                own = [pltpu.make_async_copy(stage[arr], _gathered_at(outs, arr, me, 0, _ARR_ROWS[arr]), own_sems.at[arr])
                       for arr in arrays if arr != 4]
                for cp in own:
                    cp.start()
                meta_all[me] = meta_ref[...]
                for rel in (1, 2, 3):
                    for i in mine:
                        ici_copy(rel, i, chip_of(rel), (x, y, c)).wait_recv()
                        fwd = fwd_copy(rel, i, (x, y, 1 - c))
                        fwd.start()
                        sends.append(fwd)
                for rel in (1, 2, 3):
                    for i in theirs:
                        fwd_copy(rel, i, (x, y, c)).wait_recv()
                for cp in sends:
                    cp.wait_send()
                for cp in own:
                    cp.wait()

        head_buf[...] = jnp.zeros_like(head_buf)
        zeros = pltpu.make_async_copy(head_buf, tp_o.at[pl.ds(0, HEAD_ROWS), :], loc_sems.at[2])
        zeros.start()
        zeros.wait()
        for chip in range(CHIPS):
            head_buf[PAD:HEAD_ROWS, chip * 256:(chip + 1) * 256] = meta_all[chip]
        head = pltpu.make_async_copy(head_buf, h_o.at[pl.ds(0, HEAD_ROWS), :], loc_sems.at[2])
        head.start()
        head.wait()
        for cp in local:
            cp.wait()

    vm = pl.BlockSpec(memory_space=pltpu.VMEM)
    hbm = pl.BlockSpec(memory_space=pl.ANY)
    return pl.pallas_call(
        body,
        name="gather_weights",
        in_specs=[vm] * 4 + [hbm] * 2,
        out_specs=[hbm] * 5,
        out_shape=[
            jax.ShapeDtypeStruct((D_IN, D), BF16),
            jax.ShapeDtypeStruct((CHIPS, 256, Q_LORA), BF16), jax.ShapeDtypeStruct((CHIPS, KV_LORA, 256), BF16),
            jax.ShapeDtypeStruct((N, D), F32), jax.ShapeDtypeStruct((N, D), F32),
        ],
        scratch_shapes=[pltpu.VMEM((_ARR_ROWS[a], _ARR_COLS[a]), BF16) for a in (0, 2, 3)]
        + [pltpu.VMEM((CHIPS, N_META, 256), F32), pltpu.VMEM((HEAD_ROWS, D), F32), pltpu.VMEM((S, D), F32),
           pltpu.VMEM((S, D), F32)]
        + [pltpu.SemaphoreType.DMA((3 * _NP,))] * 4 + [pltpu.SemaphoreType.DMA((3,)), pltpu.SemaphoreType.DMA((4,))],
        compiler_params=_cparams(collective_id=0),
    )(winT_s, wqT_s, wkv_s, meta_s, x2, tgt2)


_SM_ROWS = (len(POOL_WINDOWS) * POOL_GROUP, VEC_ROWS)
_SM_COLS = (POOL_GROUP, D)
_SM_PIECES = ((0, 0, 256, 0), (0, 256, 256, 1), (1, 0, VEC_ROWS, 0))
_NSP = len(_SM_PIECES)


def _reduce_grads(dwin, dwq, dwkv, dmeta4, dpw, dg, dgf, dgq, dgkv, dps, loss):
    arrays = (0, 2, 3, 4)

    def body(dwin_ref, dwq_ref, dwkv_ref, dmeta_ref, dpw_ref, dg_ref, dgf_ref, dgq_ref, dgkv_ref, dps_ref,
             loss_ref, gwin_o, gwq_o, gwkv_o, gmeta_o, gpw_o, gg_o, ggf_o, ggq_o, ggkv_o, gps_o, gloss_o,
             ow0, ow2, ow3, ow4, sb0, sb2, sb3, sb4, st0, st2, st3, st4, rc0, rc2, rc3, rc4,
             vec, sm_sb0, sm_sb1, sm_cs0, sm_cs1, sm_rc0, sm_rc1, vec_fin,
             own_sems, d2d_send, d2d_recv, ici_send, ici_recv, fin_send, fin_recv,
             swap_send, swap_recv, smi_send, smi_recv, smf_send, smf_recv):
        x, y, c = lax.axis_index("x"), lax.axis_index("y"), lax.axis_index("c")
        me = 2 * x + y
        grads = (dwin_ref, None, dwq_ref, dwkv_ref, dmeta_ref)
        outs = (gwin_o, None, gwq_o, gwkv_o, gmeta_o)
        own_buf = (ow0, None, ow2, ow3, ow4)
        sib_buf = (sb0, None, sb2, sb3, sb4)
        stage = (st0, None, st2, st3, st4)
        recv = (rc0, None, rc2, rc3, rc4)
        sm_mine = (dpw_ref, vec)
        sm_sib = (sm_sb0, sm_sb1)
        sm_chip = (sm_cs0, sm_cs1)
        sm_recv = (sm_rc0, sm_rc1)
        sm_out = (gpw_o, vec_fin)
        sibling = (x, y, 1 - c)

        def chip_of(rel):
            fx, fy = _CHIP_RELS[rel]
            return 2 * (x ^ fx) + (y ^ fy)

        def same_core_of(rel):
            fx, fy = _CHIP_RELS[rel]
            return (x ^ fx, y ^ fy, c)

        def slot(bufs, i, idx):
            arr, _, n, _ = _PIECES[i]
            return bufs[arr].at[idx, pl.ds(0, n), :]

        def own_load(rel, i):
            arr, r0, n, _ = _PIECES[i]
            return pltpu.make_async_copy(_gathered_at(grads, arr, chip_of(rel), r0, n), slot(own_buf, i, rel),
                                         own_sems.at[rel * _NP + i])

        def d2d_copy(rel, i):
            arr, r0, n, _ = _PIECES[i]
            k = rel * _NP + i
            return _remote(_gathered_at(grads, arr, chip_of(rel), r0, n), slot(sib_buf, i, rel),
                           d2d_send.at[k], d2d_recv.at[k], sibling)

        def ici_copy(rel, i):
            k = (rel - 1) * _NP + i
            return _remote(slot(stage, i, rel - 1), slot(recv, i, rel - 1), ici_send.at[k], ici_recv.at[k],
                           same_core_of(rel))

        def fin_copy(i):
            arr, r0, n, _ = _PIECES[i]
            place = outs[arr].at[pl.ds(r0, n), :]
            return _remote(place, place, fin_send.at[i], fin_recv.at[i], sibling)

        def sm_ici_copy(rel, j):
            blk, r0, n, _ = _SM_PIECES[j]
            k = (rel - 1) * _NSP + j
            return _remote(sm_chip[blk].at[pl.ds(r0, n), :], sm_recv[blk].at[rel - 1, pl.ds(r0, n), :],
                           smi_send.at[k], smi_recv.at[k], same_core_of(rel))

        def sm_fin_copy(j):
            blk, r0, n, _ = _SM_PIECES[j]
            place = sm_out[blk].at[pl.ds(r0, n), :]
            return _remote(place, place, smf_send.at[j], smf_recv.at[j], sibling)

        _peer_barrier(x, y, c)

        vec[...] = jnp.zeros_like(vec)
        vec[0:1, :] = dg_ref[...]
        vec[1:2, :] = dgf_ref[...]
        vec[2:3, V_GQ:V_GQ + Q_LORA] = dgq_ref[...]
        vec[2:3, V_GKV:V_GKV + KV_LORA] = dgkv_ref[...]
        vec[2:3, V_PS:V_PS + D_POOL] = dps_ref[...]
        vec[2:3, V_LOSS:D] = loss_ref[...]
        swaps = [_remote(sm_mine[b], sm_sib[b], swap_send.at[b], swap_recv.at[b], sibling) for b in (0, 1)]
        for cp in swaps:
            cp.start()

        for core in (0, 1):
            @pl.when(c == core)
            def _(core=core):
                mine = [i for i in range(_NP) if _PIECES[i][3] == core and _PIECES[i][0] in arrays]
                theirs = [i for i in range(_NP) if _PIECES[i][3] != core and _PIECES[i][0] in arrays]
                sm_mine_p = [j for j in range(_NSP) if _SM_PIECES[j][3] == core]
                sm_theirs_p = [j for j in range(_NSP) if _SM_PIECES[j][3] != core]
                sends = list(swaps)

                for rel in (1, 2, 3, 0):
                    for i in theirs:
                        cp = d2d_copy(rel, i)
                        cp.start()
                        sends.append(cp)
                    for i in mine:
                        own_load(rel, i).start()

                for rel in (1, 2, 3):
                    for i in mine:
                        arr, r0, n, _ = _PIECES[i]
                        own_load(rel, i).wait()
                        d2d_copy(rel, i).wait_recv()
                        total = slot(own_buf, i, rel)[...] + slot(sib_buf, i, rel)[...]
                        slot(stage, i, rel - 1)[...] = total.astype(stage[arr].dtype)
                        cp = ici_copy(rel, i)
                        cp.start()
                        sends.append(cp)

                for b in (0, 1):
                    swaps[b].wait_recv()
                    sm_chip[b][...] = sm_mine[b][...] + sm_sib[b][...]
                for rel in (1, 2, 3):
                    for j in sm_mine_p:
                        cp = sm_ici_copy(rel, j)
                        cp.start()
                        sends.append(cp)

                for i in mine:
                    arr, r0, n, _ = _PIECES[i]
                    own_load(0, i).wait()
                    d2d_copy(0, i).wait_recv()
                    total = slot(own_buf, i, 0)[...] + slot(sib_buf, i, 0)[...]
                    for rel in (1, 2, 3):
                        ici_copy(rel, i).wait_recv()
                        total = total + slot(recv, i, rel - 1)[...].astype(F32)
                    outs[arr][pl.ds(r0, n), :] = total
                    cp = fin_copy(i)
                    cp.start()
                    sends.append(cp)

                for j in sm_mine_p:
                    blk, r0, n, _ = _SM_PIECES[j]
                    for rel in (1, 2, 3):
                        sm_ici_copy(rel, j).wait_recv()
                    total = jnp.zeros((n, _SM_COLS[blk]), F32)
                    for chip in range(CHIPS):
                        flips = chip ^ me
                        rel = jnp.where(flips == 2, 1, jnp.where(flips == 1, 2, flips))
                        theirs_rows = sm_recv[blk][jnp.maximum(rel - 1, 0), pl.ds(r0, n), :]
                        total = total + jnp.where(rel == 0, sm_chip[blk][pl.ds(r0, n), :], theirs_rows)
                    sm_out[blk][pl.ds(r0, n), :] = total
                    cp = sm_fin_copy(j)
                    cp.start()
                    sends.append(cp)

                for i in theirs:
                    fin_copy(i).wait_recv()
                for j in sm_theirs_p:
                    sm_fin_copy(j).wait_recv()
                for cp in sends:
                    cp.wait_send()

        gg_o[...] = vec_fin[0:1, :]
        ggf_o[...] = vec_fin[1:2, :]
        ggq_o[...] = vec_fin[2:3, V_GQ:V_GQ + Q_LORA]
        ggkv_o[...] = vec_fin[2:3, V_GKV:V_GKV + KV_LORA]
        gps_o[...] = vec_fin[2:3, V_PS:V_PS + D_POOL]
        gloss_o[...] = vec_fin[2:3, V_LOSS:D]

    vm = pl.BlockSpec(memory_space=pltpu.VMEM)
    piece_buf = lambda lead, dtype: [pltpu.VMEM((lead, _PIECE_MAX[a], _ARR_COLS[a]), F32 if a == 4 else dtype)
                                     for a in arrays]
    sm_buf = lambda *lead: [pltpu.VMEM(lead + (_SM_ROWS[b], _SM_COLS[b]), F32) for b in (0, 1)]
    dma = lambda n: [pltpu.SemaphoreType.DMA((n,))] * 2
    return pl.pallas_call(
        body,
        name="reduce_grads",
        in_specs=[pl.BlockSpec(memory_space=pl.ANY)] * 3 + [vm] * 8,
        out_specs=[vm] * 11,
        out_shape=[jax.ShapeDtypeStruct((_ARR_ROWS[a], _ARR_COLS[a]), F32) for a in arrays]
        + [jax.ShapeDtypeStruct((_SM_ROWS[0], _SM_COLS[0]), F32), jax.ShapeDtypeStruct((1, D), F32),
           jax.ShapeDtypeStruct((1, D), F32), jax.ShapeDtypeStruct((1, Q_LORA), F32),
           jax.ShapeDtypeStruct((1, KV_LORA), F32), jax.ShapeDtypeStruct((1, D_POOL), F32),
           jax.ShapeDtypeStruct((1, 128), F32)],
        scratch_shapes=piece_buf(CHIPS, F32) + piece_buf(CHIPS, F32) + piece_buf(3, BF16) + piece_buf(3, BF16)
        + [pltpu.VMEM((VEC_ROWS, D), F32)] + sm_buf() + sm_buf() + sm_buf(3) + [pltpu.VMEM((VEC_ROWS, D), F32)]
        + [pltpu.SemaphoreType.DMA((CHIPS * _NP,))]
        + dma(CHIPS * _NP) + dma(3 * _NP) + dma(_NP) + dma(2) + dma(3 * _NSP) + dma(_NSP),
        compiler_params=_cparams(collective_id=3),
    )(dwin, dwq, dwkv, dmeta4, dpw, dg, dgf, dgq, dgkv, dps, loss)


def _adamw_math(w, g, m, v):
    m = B1 * m + (1.0 - B1) * g
    v = B2 * v + (1.0 - B2) * (g * g)
    m_hat = m / C1
    v_hat = v / C2
    delta = -LR * (m_hat / (jnp.sqrt(v_hat) + ADAM_EPS) + WD * w)
    return delta, m, v


def _adamw_rows(name, w, g, m, v, block_rows):
    rows, cols = w.shape

    def body(w_ref, g_ref, m_ref, v_ref, go_ref, d_ref, nm_ref, nv_ref):
        g = g_ref[...]
        go_ref[...] = g
        d_ref[...], nm_ref[...], nv_ref[...] = _adamw_math(w_ref[...], g, m_ref[...], v_ref[...])

    spec = pl.BlockSpec((block_rows, cols), lambda i: (i, 0))
    return pl.pallas_call(
        body,
        name=name,
        grid=(rows // block_rows,),
        in_specs=[spec] * 4,
        out_specs=[spec] * 4,
        out_shape=[jax.ShapeDtypeStruct(w.shape, F32)] * 4,
        compiler_params=_cparams(dimension_semantics=("arbitrary",)),
    )(w, g, m, v)


def _adamw_small(groups):
    n = len(groups)

    def body(*refs):
        ins, outs = refs[:4 * n], refs[4 * n:]
        for t in range(n):
            w_ref, g_ref, m_ref, v_ref = ins[4 * t:4 * t + 4]
            g = g_ref[0:w_ref.shape[0], :]
            outs[4 * t][...] = g
            outs[4 * t + 1][...], outs[4 * t + 2][...], outs[4 * t + 3][...] = _adamw_math(
                w_ref[...], g, m_ref[...], v_ref[...])

    vm = pl.BlockSpec(memory_space=pltpu.VMEM)
    flat = [a for grp in groups for a in grp]
    outs = pl.pallas_call(
        body,
        name="adamw_small",
        in_specs=[vm] * (4 * n),
        out_specs=[vm] * (4 * n),
        out_shape=[jax.ShapeDtypeStruct(grp[0].shape, F32) for grp in groups for _ in range(4)],
        compiler_params=_cparams(),
    )(*flat)
    return [tuple(outs[4 * t:4 * t + 4]) for t in range(n)]


def _rope_tables():
    half = QK_ROPE // 2
    f32 = np.float32
    inv_freq = (f32(1.0) / (f32(ROPE_THETA) ** (np.arange(half, dtype=f32) / f32(half)))).astype(f32)
    pos = np.arange(N, dtype=f32) - f32(PAD)
    ang = (pos[:, None] * inv_freq[None, :]).astype(f32)
    cos, sin = np.cos(ang).astype(f32), np.sin(ang).astype(f32)
    zero = np.zeros((N, 128 - QK_ROPE), f32)
    return jnp.asarray(np.concatenate([cos, cos, zero], axis=1)), jnp.asarray(np.concatenate([-sin, sin, zero], axis=1))


def kernel(x, meta_tokens, norm_g, w_in, q_norm_g, w_q_b, kv_norm_g, w_kv_b, pool_w, pool_scale, w_out, final_norm_g, loss_target, m_meta_tokens, m_norm_g, m_w_in, m_q_norm_g, m_w_q_b, m_kv_norm_g, m_w_kv_b, m_pool_w, m_pool_scale, m_w_out, m_final_norm_g, v_meta_tokens, v_norm_g, v_w_in, v_q_norm_g, v_w_q_b, v_kv_norm_g, v_w_kv_b, v_pool_w, v_pool_scale, v_w_out, v_final_norm_g):
    tr = lambda a: a[0].T
    win, wq, wkv, h, tgt = _gather_weights(tr(w_in), tr(w_q_b), w_kv_b[0], meta_tokens, x[0], loss_target[0])
    cosf, sinf = _rope_tables()
    gf = final_norm_g.reshape(1, D)

    part = _local_step(h, tgt, norm_g, win, q_norm_g, wq, kv_norm_g, wkv, pool_w[0], pool_scale, w_out[0], m_w_out[0],
                       v_w_out[0], gf, cosf, sinf)

    pw2 = lambda a: a.reshape(len(POOL_WINDOWS) * POOL_GROUP, POOL_GROUP)
    gwinT, gwqT, gwkv, gmeta, gpw, gg, ggf, ggq, ggkv, gps, gloss = _reduce_grads(
        part["dwin"], part["dwq"], part["dwkv"], part["dmeta"], pw2(part["dpw"]), part["dg"],
        part["dgf"], part["dgq"], part["dgkv"], part["dps"], part["loss"])

    r_in = _adamw_rows("adamw_w_in", tr(w_in), gwinT, tr(m_w_in), tr(v_w_in), 248)
    r_out = part["r_out"]
    fn2 = lambda a: a.reshape(1, D)
    r_meta, r_norm, r_gq, r_wq, r_gkv, r_wkv, r_pw, r_ps, r_fn = _adamw_small([
        (meta_tokens, gmeta, m_meta_tokens, v_meta_tokens),
        (norm_g, gg, m_norm_g, v_norm_g),
        (q_norm_g, ggq, m_q_norm_g, v_q_norm_g),
        (tr(w_q_b), gwqT, tr(m_w_q_b), tr(v_w_q_b)),
        (kv_norm_g, ggkv, m_kv_norm_g, v_kv_norm_g),
        (w_kv_b[0], gwkv, m_w_kv_b[0], v_w_kv_b[0]),
        (pw2(pool_w), gpw, pw2(m_pool_w), pw2(v_pool_w)),
        (pool_scale, gps, m_pool_scale, v_pool_scale),
        (fn2(final_norm_g), ggf, fn2(m_final_norm_g), fn2(v_final_norm_g)),
    ])
    untr = lambda a: a.T[None]
    pw4 = lambda a: a.reshape(1, len(POOL_WINDOWS), POOL_GROUP, POOL_GROUP)
    per_kind = [[
        r_meta[kind], r_norm[kind], untr(r_in[kind]), r_gq[kind], untr(r_wq[kind]), r_gkv[kind], r_wkv[kind][None],
        pw4(r_pw[kind]), r_ps[kind], r_out[kind][None], r_fn[kind].reshape(D),
    ] for kind in range(4)]
    return (gloss[0, 0], part["gx"][None], *per_kind[0], *per_kind[1], *per_kind[2], *per_kind[3])
```

```python
import jax
import jax.numpy as jnp
import numpy as np
from jax import lax
from jax.experimental import pallas as pl
from jax.experimental.pallas import tpu as pltpu

F32 = jnp.float32
BF16 = jnp.bfloat16

D = 1024
S = 2048
N_META = 16
PAD = 112
HEAD_ROWS = PAD + N_META
N = HEAD_ROWS + S
D_POOL = 512
POOL_WINDOWS = (2, 4, 8, 16)
POOL_GROUP = 128
HALO = 16
HEADS = 4
QK_NOPE = 128
QK_ROPE = 64
QK = QK_NOPE + QK_ROPE
V_HEAD = 128
Q_LORA = 256
KV_LORA = 128
D_IN = 1984
EPS = 1e-6
ROPE_THETA = 10000.0
SCALE = QK ** -0.5
CHIPS = 4

ROWS_FWD = 544
ROWS_MID = 544
ROWS_BWD = 544
TK = 128
TQ = 256
NQ = S // TQ
HEADS_PER_STEP_BWD = 2

O_PI, O_PG, O_CQ, O_CKV, O_KR, O_AG = 0, 512, 1024, 1280, 1408, 1472
O_KR_END = O_KR + 128
SHARD_IN = D_IN // CHIPS
SHARD_OUT = D // CHIPS

LR, B1, B2, ADAM_EPS, WD, STEP = 0.001, 0.9, 0.999, 1e-08, 0.01, 10
C1 = 1.0 - B1**STEP
C2 = 1.0 - B2**STEP

VMEM_LIMIT = 60 * 1024 * 1024
MESH = pl.DeviceIdType.MESH
NEG = -1e30

VEC_ROWS = 8
V_GQ, V_GKV, V_PS, V_LOSS = 0, 256, 384, 896


def _cparams(**kw):
    return pltpu.CompilerParams(vmem_limit_bytes=VMEM_LIMIT, **kw)


def _nt(a, b):
    return lax.dot_general(a, b, (((1,), (1,)), ((), ())), preferred_element_type=F32)


def _tn(a, b):
    return lax.dot_general(a, b, (((0,), (0,)), ((), ())), preferred_element_type=F32)


def _nn(a, b):
    return jnp.dot(a, b, preferred_element_type=F32)


def _swap64(t):
    return pltpu.roll(t, 32, 1) + pltpu.roll(t, 96, 1)


def _sigmoid(x):
    return 1.0 / (1.0 + jnp.exp(-x))


def _low_lanes():
    return (lax.broadcasted_iota(jnp.int32, (1, 128), 1) < QK_ROPE).astype(F32)


def _rows(w, rows):
    return pl.BlockSpec((rows, w), lambda i: (i, 0))


def _const(*shape):
    return pl.BlockSpec(shape, lambda *_: (0,) * len(shape), pipeline_mode=pl.Buffered(1))


STAT_GROUPS = HEADS // HEADS_PER_STEP_BWD


def _stat_slot(head):
    return head // HEADS_PER_STEP_BWD, head % HEADS_PER_STEP_BWD


def _peer_barrier(x, y, c):
    barrier = pltpu.get_barrier_semaphore()
    peers = [(x, y, 1 - c)] + [(x ^ fx, y ^ fy, c) for fx, fy in _CHIP_RELS[1:]]
    for peer in peers:
        pl.semaphore_signal(barrier, inc=1, device_id=peer, device_id_type=MESH)
    pl.semaphore_wait(barrier, len(peers))


def _attn_tiles():
    return [(0, TK, TK)] + [(TK + TQ * t, TQ, TK + TQ * (t + 1)) for t in range(NQ)]


def _masked_scores(q, k, rows, klen):
    s = _nt(q, k)
    col = lax.broadcasted_iota(jnp.int32, (1, TK), 1)
    head_bias = jnp.where(col >= PAD, 0.0, NEG)
    if klen == TK:
        return s + head_bias
    r = lax.broadcasted_iota(jnp.int32, (rows, 1), 0) >> 6
    c = lax.broadcasted_iota(jnp.int32, (1, rows), 1) >> 6
    diag_bias = jnp.where(c <= r, 0.0, NEG)
    parts = [s[:, 0:TK] + head_bias]
    if klen - rows > TK:
        parts.append(s[:, TK:klen - rows])
    parts.append(s[:, klen - rows:klen] + diag_bias)
    return jnp.concatenate(parts, axis=1)


def _fwd_in(h, norm_g, win, gq, wq, gkv, wkv, cosf, sinf):
    tr = ROWS_FWD

    def body(h_ref, g_ref, win_ref, gq_ref, wq_ref, gkv_ref, wkv_ref, cos_ref, sin_ref,
             pi_ref, pg_ref, cq_ref, ckv_ref, ag_ref, q_ref, k_ref, v_ref):
        h = h_ref[...]
        r = lax.rsqrt(jnp.mean(h * h, axis=-1, keepdims=True) + EPS)
        hn = ((h * r) * g_ref[...]).astype(BF16)
        u = _nt(hn, win_ref[0:O_KR_END, :])
        pi_ref[...] = u[:, O_PI:O_PG]
        pg_ref[...] = u[:, O_PG:O_CQ]
        cq = u[:, O_CQ:O_CKV]
        ckv = u[:, O_CKV:O_KR]
        cq_ref[...] = cq
        ckv_ref[...] = ckv
        ag_ref[...] = _nt(hn, win_ref[O_AG:D_IN, :])
        cosv = cos_ref[...]
        sinv = sin_ref[...]
        kr = u[:, O_KR:O_KR_END] * _low_lanes()
        kr = (kr * cosv + _swap64(kr) * sinv).astype(BF16)
        rq = lax.rsqrt(jnp.mean(cq * cq, axis=-1, keepdims=True) + EPS)
        cqn = ((cq * rq) * gq_ref[...]).astype(BF16)
        rkv = lax.rsqrt(jnp.mean(ckv * ckv, axis=-1, keepdims=True) + EPS)
        ckvn = ((ckv * rkv) * gkv_ref[...]).astype(BF16)
        for hd in range(HEADS):
            qh = _nt(cqn, wq_ref[hd]) * SCALE
            z = qh[:, QK_NOPE:]
            q_ref[hd, :, 0:QK_NOPE] = qh[:, 0:QK_NOPE].astype(BF16)
            q_ref[hd, :, QK_NOPE:] = (z * cosv + _swap64(z) * sinv).astype(BF16)
            kvh = _nn(ckvn, wkv_ref[hd])
            k_ref[hd, :, 0:QK_NOPE] = kvh[:, 0:QK_NOPE].astype(BF16)
            k_ref[hd, :, QK_NOPE:] = kr
            v_ref[hd] = kvh[:, QK_NOPE:].astype(BF16)

    head = lambda w: pl.BlockSpec((HEADS, tr, w), lambda i: (0, i, 0))
    return pl.pallas_call(
        body,
        name="fwd_in",
        grid=(N // tr,),
        in_specs=[
            _rows(D, tr), _const(1, D), _const(D_IN, D), _const(1, Q_LORA), _const(HEADS, 256, Q_LORA),
            _const(1, KV_LORA), _const(HEADS, KV_LORA, 256), _rows(128, tr), _rows(128, tr),
        ],
        out_specs=[_rows(D_POOL, tr), _rows(D_POOL, tr), _rows(Q_LORA, tr), _rows(KV_LORA, tr), _rows(D_POOL, tr),
                   head(256), head(256), head(V_HEAD)],
        out_shape=[
            jax.ShapeDtypeStruct((N, D_POOL), F32), jax.ShapeDtypeStruct((N, D_POOL), F32),
            jax.ShapeDtypeStruct((N, Q_LORA), F32), jax.ShapeDtypeStruct((N, KV_LORA), F32),
            jax.ShapeDtypeStruct((N, D_POOL), F32),
            jax.ShapeDtypeStruct((HEADS, N, 256), BF16), jax.ShapeDtypeStruct((HEADS, N, 256), BF16),
            jax.ShapeDtypeStruct((HEADS, N, V_HEAD), BF16),
        ],
        compiler_params=_cparams(dimension_semantics=("arbitrary",)),
    )(h, norm_g, win, gq, wq, gkv, wkv, cosf, sinf)


def _attn_fwd(q, k, v, wout_s):
    tiles = _attn_tiles()
    n_t = len(tiles)
    half = SHARD_OUT // 2
    fwd_step = n_t - 2

    def body(q_hbm, k_hbm, v_hbm, wout_ref, o_hbm, lse_ref, wout_o, q_buf, k_buf, v_buf, o_buf, s_wout, in_sems, out_sems,
             ici_send, ici_recv, fwd_send, fwd_recv, own_sem):
        step = pl.program_id(0)
        x, y, c = lax.axis_index("x"), lax.axis_index("y"), lax.axis_index("c")
        me = 2 * x + y

        def chip_of(rel):
            fx, fy = _CHIP_RELS[rel]
            return 2 * (x ^ fx) + (y ^ fy)

        def place(chip, core):
            return wout_o.at[pl.ds(pl.multiple_of(SHARD_OUT * chip + half * core, half), half), :]

        def ici_copy(rel, src_chip, to):
            return _remote(s_wout.at[pl.ds(pl.multiple_of(half * c, half), half), :], place(src_chip, c),
                           ici_send.at[rel - 1], ici_recv.at[rel - 1], to)

        def fwd_copy(rel, core, to):
            spot = place(chip_of(rel), core)
            return _remote(spot, spot, fwd_send.at[rel - 1], fwd_recv.at[rel - 1], to)

        own = pltpu.make_async_copy(s_wout, wout_o.at[pl.ds(pl.multiple_of(SHARD_OUT * me, SHARD_OUT), SHARD_OUT), :], own_sem)

        @pl.when(step == 0)
        def _():
            _peer_barrier(x, y, c)
            s_wout[...] = wout_ref[...].astype(BF16)
            own.start()
            for rel in (1, 2, 3):
                fx, fy = _CHIP_RELS[rel]
                ici_copy(rel, me, (x ^ fx, y ^ fy, c)).start()

        @pl.when(step == fwd_step)
        def _():
            for rel in (1, 2, 3):
                ici_copy(rel, chip_of(rel), (x, y, c)).wait_recv()
                fwd_copy(rel, c, (x, y, 1 - c)).start()

        def finish_wout():
            for rel in (1, 2, 3):
                fwd_copy(rel, 1 - c, (x, y, c)).wait_recv()
            for rel in (1, 2, 3):
                ici_copy(rel, me, (x, y, c)).wait_send()
                fwd_copy(rel, c, (x, y, c)).wait_send()
            own.wait()

        def loads(idx):
            q0, rows, _ = tiles[idx]
            rs = pl.ds(q0, rows)
            return [pltpu.make_async_copy(src.at[:, rs, :], dst.at[:, rs, :], in_sems.at[a, idx % 2])
                    for a, (src, dst) in enumerate(((q_hbm, q_buf), (k_hbm, k_buf), (v_hbm, v_buf)))]

        def store(idx):
            q0, rows, _ = tiles[idx]
            return pltpu.make_async_copy(o_buf.at[idx % 2, pl.ds(0, rows), :], o_hbm.at[pl.ds(q0, rows), :],
                                         out_sems.at[idx % 2])

        @pl.when(step == 0)
        def _():
            lse_ref[...] = jnp.zeros_like(lse_ref)
            for cp in loads(0):
                cp.start()

        for idx, (q0, rows, klen) in enumerate(tiles):
            @pl.when(step == idx)
            def _(idx=idx, q0=q0, rows=rows, klen=klen):
                for cp in loads(idx):
                    cp.wait()
                if idx + 1 < n_t:
                    for cp in loads(idx + 1):
                        cp.start()
                if idx >= 2:
                    store(idx - 2).wait()
                for hd in range(HEADS):
                    s = _masked_scores(q_buf[hd, q0:q0 + rows, :], k_buf[hd, 0:klen, :], rows, klen)
                    m = jnp.max(s, axis=-1, keepdims=True)
                    p = jnp.exp(s - m)
                    l = jnp.sum(p, axis=-1, keepdims=True)
                    o_buf[idx % 2, 0:rows, hd * V_HEAD:(hd + 1) * V_HEAD] = _nn(p.astype(BF16), v_buf[hd, 0:klen, :]) / l
                    grp, lane = _stat_slot(hd)
                    lse_ref[grp, q0:q0 + rows, lane:lane + 1] = m + jnp.log(l)
                store(idx).start()
                if idx == n_t - 1:
                    store(idx - 1).wait()
                    store(idx).wait()
                    finish_wout()

    hbm = pl.BlockSpec(memory_space=pl.ANY)
    return pl.pallas_call(
        body,
        name="attn_fwd",
        grid=(n_t,),
        in_specs=[hbm, hbm, hbm, _const(SHARD_OUT, D)],
        out_specs=[hbm, _const(STAT_GROUPS, N, 128), hbm],
        out_shape=[jax.ShapeDtypeStruct((N, HEADS * V_HEAD), F32), jax.ShapeDtypeStruct((STAT_GROUPS, N, 128), F32),
                   jax.ShapeDtypeStruct((D, D), BF16)],
        scratch_shapes=[pltpu.VMEM((HEADS, N, 256), BF16), pltpu.VMEM((HEADS, N, 256), BF16),
                        pltpu.VMEM((HEADS, N, V_HEAD), BF16), pltpu.VMEM((2, TQ, HEADS * V_HEAD), F32),
                        pltpu.VMEM((SHARD_OUT, D), BF16),
                        pltpu.SemaphoreType.DMA((3, 2)), pltpu.SemaphoreType.DMA((2,))]
        + [pltpu.SemaphoreType.DMA((3,))] * 4 + [pltpu.SemaphoreType.DMA],
        compiler_params=_cparams(dimension_semantics=("arbitrary",), collective_id=1),
    )(q, k, v, wout_s)


def _inv_count(row0, rows, w):
    row = row0 + lax.broadcasted_iota(jnp.int32, (rows, 1), 0)
    return 1.0 / jnp.clip(row - (PAD - 1), 1, w).astype(F32)


def _mid(h, tgt, pool_in, pool_gate, attn_gate, attn, pool_w, pool_scale, wout, gf):
    tr = ROWS_MID
    per = tr // HALO
    ng = len(POOL_WINDOWS)

    def body(h_ref, t_ref, pin_ref, halo_ref, pg_ref, ag_ref, at_ref, pw_ref, ps_ref, wout_ref, gf_ref,
             dh2_ref, do_ref, delta_ref, dag_ref, dpg_ref, dpl_ref, dwout_ref, dpw_ref, dps_ref, dgf_ref, loss_ref):
        i = pl.program_id(0)

        @pl.when(i == 0)
        def _():
            dwout_ref[...] = jnp.zeros_like(dwout_ref)
            dpw_ref[...] = jnp.zeros_like(dpw_ref)
            dps_ref[...] = jnp.zeros_like(dps_ref)
            dgf_ref[...] = jnp.zeros_like(dgf_ref)
            loss_ref[...] = jnp.zeros_like(loss_ref)

        row0 = i * tr
        real = (row0 + lax.broadcasted_iota(jnp.int32, (tr, 1), 0)) >= HEAD_ROWS
        h = h_ref[...]

        halo = jnp.where(i > 0, halo_ref[...], 0.0)
        ext = jnp.concatenate([halo, pin_ref[...]], axis=0)
        pooled = []
        for g, w in enumerate(POOL_WINDOWS):
            e = ext[:, g * POOL_GROUP:(g + 1) * POOL_GROUP]
            acc = e
            shift = 1
            while shift < w:
                acc = acc + pltpu.roll(acc, shift, 0)
                shift *= 2
            pooled.append((acc[HALO:] * _inv_count(row0, tr, w) - e[HALO:]).astype(BF16))
        pw = [pw_ref[g].astype(BF16) for g in range(ng)]
        mixed = jnp.concatenate([_nn(pooled[g], pw[g]) for g in range(ng)], axis=1)
        ps = ps_ref[...]
        mixed_s = mixed * ps
        pg = pg_ref[...]
        sig_p = _sigmoid(pg)
        silu_p = pg * sig_p
        pool_out = (silu_p * mixed_s).astype(BF16)
        ag = ag_ref[...]
        sig_a = _sigmoid(ag)
        silu_a = ag * sig_a
        at = at_ref[...]
        attn_out = (silu_a * at).astype(BF16)
        cat = jnp.concatenate([pool_out, attn_out], axis=1)
        h2 = h + _nn(cat, wout_ref[...])

        r2 = lax.rsqrt(jnp.mean(h2 * h2, axis=-1, keepdims=True) + EPS)
        n2 = h2 * r2
        gfv = gf_ref[...]
        err = jnp.where(real, n2 * gfv - t_ref[...], 0.0)
        loss_ref[...] += jnp.sum(jnp.sum(err * err, axis=-1, keepdims=True), axis=0, keepdims=True) * (0.5 / D)
        dy = err * (1.0 / D)
        dgf_ref[...] += jnp.sum(dy * n2, axis=0, keepdims=True)
        dn = dy * gfv
        dh2 = r2 * (dn - n2 * jnp.mean(dn * n2, axis=-1, keepdims=True))
        dh2_ref[...] = dh2
        dh2b = dh2.astype(BF16)

        dwout_ref[...] += _tn(cat, dh2b)
        dcat = _nt(dh2b, wout_ref[...])
        dpo = dcat[:, 0:D_POOL]
        dao = dcat[:, D_POOL:D]
        do = dao * silu_a
        prod = do * at
        delta_ref[...] = jnp.zeros_like(delta_ref)
        for hd in range(HEADS):
            grp, lane = _stat_slot(hd)
            cols = slice(hd * V_HEAD, (hd + 1) * V_HEAD)
            do_ref[grp, :, lane * V_HEAD:(lane + 1) * V_HEAD] = do[:, cols].astype(BF16)
            delta_ref[grp, :, lane:lane + 1] = jnp.sum(prod[:, cols], axis=-1, keepdims=True)
        dag_ref[...] = (dao * at * (sig_a * (1.0 + ag * (1.0 - sig_a)))).astype(BF16)
        dmixed_s = dpo * silu_p
        dpg_ref[...] = (dpo * mixed_s * (sig_p * (1.0 + pg * (1.0 - sig_p)))).astype(BF16)
        dps_ref[...] += jnp.sum(dmixed_s * mixed, axis=0, keepdims=True)
        dmixed = (dmixed_s * ps).astype(BF16)
        dpl = []
        for g in range(ng):
            dm = dmixed[:, g * POOL_GROUP:(g + 1) * POOL_GROUP]
            dpl.append(_nt(dm, pw[g]))
            dpw_ref[g] += _tn(pooled[g], dm)
        dpl_ref[...] = jnp.concatenate(dpl, axis=1)

    halo_spec = pl.BlockSpec((HALO, D_POOL), lambda i: (jnp.maximum(i * per - 1, 0), 0))
    return pl.pallas_call(
        body,
        name="mid",
        grid=(N // tr,),
        in_specs=[
            _rows(D, tr), _rows(D, tr), _rows(D_POOL, tr), halo_spec, _rows(D_POOL, tr), _rows(D_POOL, tr),
            _rows(D_POOL, tr), _const(ng, POOL_GROUP, POOL_GROUP), _const(1, D_POOL), _const(D, D), _const(1, D),
        ],
        out_specs=[
            _rows(D, tr), pl.BlockSpec((STAT_GROUPS, tr, HEADS_PER_STEP_BWD * V_HEAD), lambda i: (0, i, 0)),
            pl.BlockSpec((STAT_GROUPS, tr, 128), lambda i: (0, i, 0)),
            _rows(D_POOL, tr), _rows(D_POOL, tr), _rows(D_POOL, tr),
            _const(D, D), _const(ng, POOL_GROUP, POOL_GROUP), _const(1, D_POOL), _const(1, D), _const(1, 128),
        ],
        out_shape=[
            jax.ShapeDtypeStruct((N, D), F32), jax.ShapeDtypeStruct((STAT_GROUPS, N, HEADS_PER_STEP_BWD * V_HEAD), BF16),
            jax.ShapeDtypeStruct((STAT_GROUPS, N, 128), F32),
            jax.ShapeDtypeStruct((N, D_POOL), BF16), jax.ShapeDtypeStruct((N, D_POOL), BF16),
            jax.ShapeDtypeStruct((N, D_POOL), F32), jax.ShapeDtypeStruct((D, D), F32),
            jax.ShapeDtypeStruct((ng, POOL_GROUP, POOL_GROUP), F32),
            jax.ShapeDtypeStruct((1, D_POOL), F32), jax.ShapeDtypeStruct((1, D), F32), jax.ShapeDtypeStruct((1, 128), F32),
        ],
        compiler_params=_cparams(dimension_semantics=("arbitrary",)),
    )(h, tgt, pool_in, pool_in, pool_gate, attn_gate, attn, pool_w, pool_scale, wout, gf)


def _unrope(dy, cosv, sinv):
    return dy * cosv + _swap64(dy * sinv) * _low_lanes()


def _attn_bwd(q, k, v, do, lse, delta, cosf, sinf, dwout):
    tiles = _attn_tiles()
    hp = HEADS_PER_STEP_BWD
    n_g = HEADS // hp
    n_t = len(tiles)
    half = SHARD_OUT // 2
    send_at, sum_at = (0, 2), (n_g - 1, n_t // 2)

    def body(q_hbm, k_hbm, v_hbm, do_hbm, lse_ref, delta_ref, cos_ref, sin_ref, dwout_hbm, dq_hbm, dkv_ref, dkr_ref,
             gwout_ref, q_buf, k_buf, v_buf, do_buf, dq_buf, dk_acc, dv_acc, own_w, sib_w, stage_w, recv_w, gw_buf,
             in_sems, out_sems, ow_sems, d2d_send, d2d_recv, ici_send, ici_recv, fin_send, fin_recv):
        grp = pl.program_id(0)
        step = pl.program_id(1)
        heads = pl.ds(grp * hp, hp)
        x, y, c = lax.axis_index("x"), lax.axis_index("y"), lax.axis_index("c")
        sibling = (x, y, 1 - c)

        def chip_of(rel):
            fx, fy = _CHIP_RELS[rel]
            return 2 * (x ^ fx) + (y ^ fy)

        def piece(chip, core):
            return dwout_hbm.at[pl.ds(pl.multiple_of(SHARD_OUT * chip + half * core, half), half), :]

        def own_load(rel):
            return pltpu.make_async_copy(piece(chip_of(rel), c), own_w.at[rel], ow_sems.at[rel])

        def d2d_copy(rel):
            return _remote(piece(chip_of(rel), 1 - c), sib_w.at[rel], d2d_send.at[rel], d2d_recv.at[rel], sibling)

        def ici_copy(rel):
            fx, fy = _CHIP_RELS[rel]
            return _remote(stage_w.at[rel - 1], recv_w.at[rel - 1], ici_send.at[rel - 1], ici_recv.at[rel - 1],
                           (x ^ fx, y ^ fy, c))

        def fin_copy(core):
            spot = gw_buf.at[pl.ds(pl.multiple_of(half * core, half), half), :]
            return _remote(spot, spot, fin_send.at[0], fin_recv.at[0], sibling)

        @pl.when((grp == 0) & (step == 0))
        def _():
            _peer_barrier(x, y, c)
            for rel in (1, 2, 3, 0):
                d2d_copy(rel).start()
                own_load(rel).start()

        @pl.when((grp == send_at[0]) & (step == send_at[1]))
        def _():
            for rel in (1, 2, 3):
                own_load(rel).wait()
                d2d_copy(rel).wait_recv()
                stage_w[rel - 1] = (own_w[rel] + sib_w[rel]).astype(BF16)
                ici_copy(rel).start()

        @pl.when((grp == sum_at[0]) & (step == sum_at[1]))
        def _():
            own_load(0).wait()
            d2d_copy(0).wait_recv()
            total = own_w[0] + sib_w[0]
            for rel in (1, 2, 3):
                ici_copy(rel).wait_recv()
                total = total + recv_w[rel - 1].astype(F32)
            gw_buf[pl.ds(pl.multiple_of(half * c, half), half), :] = total
            fin_copy(c).start()

        def finish_dwout():
            fin_copy(1 - c).wait_recv()
            for rel in (0, 1, 2, 3):
                d2d_copy(rel).wait_send()
            for rel in (1, 2, 3):
                ici_copy(rel).wait_send()
            fin_copy(c).wait_send()
            gwout_ref[...] = gw_buf[...]

        def loads(g, idx):
            q0, rows, _ = tiles[idx]
            rs = pl.ds(q0, rows)
            par = (g * n_t + idx) % 2
            hs = pl.ds(g * hp, hp)
            pairs = ((q_hbm.at[hs, rs, :], q_buf.at[:, rs, :]), (k_hbm.at[hs, rs, :], k_buf.at[:, rs, :]),
                     (v_hbm.at[hs, rs, :], v_buf.at[:, rs, :]), (do_hbm.at[g, rs, :], do_buf.at[rs, :]))
            return [pltpu.make_async_copy(src, dst, in_sems.at[a, par]) for a, (src, dst) in enumerate(pairs)]

        def store(idx):
            q0, rows, _ = tiles[idx]
            return pltpu.make_async_copy(dq_buf.at[idx % 2, :, pl.ds(0, rows), :], dq_hbm.at[heads, pl.ds(q0, rows), :],
                                         out_sems.at[idx % 2])

        @pl.when(step == 0)
        def _():
            dk_acc[...] = jnp.zeros_like(dk_acc)
            dv_acc[...] = jnp.zeros_like(dv_acc)

        @pl.when((step == 0) & (grp == 0))
        def _():
            dkr_ref[...] = jnp.zeros_like(dkr_ref)
            for cp in loads(grp, 0):
                cp.start()

        for idx, (q0, rows, klen) in enumerate(tiles):
            @pl.when(step == idx)
            def _(idx=idx, q0=q0, rows=rows, klen=klen):
                for cp in loads(grp, idx):
                    cp.wait()
                if idx + 1 < n_t:
                    for cp in loads(grp, idx + 1):
                        cp.start()
                if idx >= 2:
                    store(idx - 2).wait()
                qs = pl.ds(q0, rows)
                for hd in range(hp):
                    qv = q_buf[hd, qs, :]
                    kv = k_buf[hd, 0:klen, :]
                    p = jnp.exp(_masked_scores(qv, kv, rows, klen) - lse_ref[0, qs, hd:hd + 1])
                    dob = do_buf[qs, hd * V_HEAD:(hd + 1) * V_HEAD]
                    ds = (p * (_nt(dob, v_buf[hd, 0:klen, :]) - delta_ref[0, qs, hd:hd + 1])).astype(BF16)
                    dq = _nn(ds, kv) * SCALE
                    dq_buf[idx % 2, hd, 0:rows, 0:QK_NOPE] = dq[:, 0:QK_NOPE].astype(BF16)
                    dq_buf[idx % 2, hd, 0:rows, QK_NOPE:] = _unrope(dq[:, QK_NOPE:], cos_ref[qs, :], sin_ref[qs, :]).astype(BF16)
                    dk_acc[hd, 0:klen, :] += _tn(ds, qv)
                    dv_acc[hd, 0:klen, :] += _tn(p.astype(BF16), dob)
                store(idx).start()

        @pl.when(step == n_t - 1)
        def _():
            @pl.when(grp + 1 < n_g)
            def _():
                for cp in loads(grp + 1, 0):
                    cp.start()

            for hd in range(hp):
                dkv_ref[hd, :, 0:QK_NOPE] = dk_acc[hd, :, 0:QK_NOPE].astype(BF16)
                dkv_ref[hd, :, QK_NOPE:] = dv_acc[hd].astype(BF16)
                dkr_ref[...] += dk_acc[hd, :, QK_NOPE:]
            store(n_t - 2).wait()
            store(n_t - 1).wait()

            @pl.when(grp == n_g - 1)
            def _():
                finish_dwout()

    hbm = pl.BlockSpec(memory_space=pl.ANY)
    stat = pl.BlockSpec((1, N, 128), lambda g, t: (g, 0, 0), pipeline_mode=pl.Buffered(1))
    piece_f32 = lambda lead: pltpu.VMEM((lead, half, D), F32)
    piece_bf16 = lambda lead: pltpu.VMEM((lead, half, D), BF16)
    return pl.pallas_call(
        body,
        name="attn_bwd",
        grid=(n_g, n_t),
        in_specs=[hbm, hbm, hbm, hbm, stat, stat, _const(N, 128), _const(N, 128), hbm],
        out_specs=[hbm, pl.BlockSpec((hp, N, 256), lambda g, t: (g, 0, 0), pipeline_mode=pl.Buffered(1)), _const(N, 128),
                   _const(SHARD_OUT, D)],
        out_shape=[
            jax.ShapeDtypeStruct((HEADS, N, 256), BF16), jax.ShapeDtypeStruct((HEADS, N, 256), BF16),
            jax.ShapeDtypeStruct((N, 128), F32), jax.ShapeDtypeStruct((SHARD_OUT, D), F32),
        ],
        scratch_shapes=[pltpu.VMEM((hp, N, 256), BF16), pltpu.VMEM((hp, N, 256), BF16), pltpu.VMEM((hp, N, V_HEAD), BF16),
                        pltpu.VMEM((N, hp * V_HEAD), BF16), pltpu.VMEM((2, hp, TQ, 256), BF16),
                        pltpu.VMEM((hp, N, 256), F32), pltpu.VMEM((hp, N, V_HEAD), F32),
                        piece_f32(CHIPS), piece_f32(CHIPS), piece_bf16(3), piece_bf16(3), pltpu.VMEM((SHARD_OUT, D), F32),
                        pltpu.SemaphoreType.DMA((4, 2)), pltpu.SemaphoreType.DMA((2,)), pltpu.SemaphoreType.DMA((CHIPS,)),
                        pltpu.SemaphoreType.DMA((CHIPS,)), pltpu.SemaphoreType.DMA((CHIPS,)),
                        pltpu.SemaphoreType.DMA((3,)), pltpu.SemaphoreType.DMA((3,)),
                        pltpu.SemaphoreType.DMA((1,)), pltpu.SemaphoreType.DMA((1,))],
        compiler_params=_cparams(dimension_semantics=("arbitrary", "arbitrary"), collective_id=2),
    )(q, k, v, do, lse, delta, cosf, sinf, dwout)


def _bwd_in(h, dq, dkv, dkr, cq, ckv, dpl, dpg, dag, norm_g, gq, wq, gkv, wkv, cosf, sinf, adam_out):
    tr = ROWS_BWD
    nb = N // tr
    per = tr // HALO
    adam_rows = SHARD_OUT // nb

    def body(h_ref, dq_ref, dkv_ref, dkr_ref, cq_ref, ckv_ref, dpl_ref, halo_ref, dpg_ref, dag_ref,
             g_ref, gq_ref, wq_ref, gkv_ref, wkv_ref, cos_ref, sin_ref, aw_ref, ag_ref, am_ref, av_ref,
             du_ref, dwin_ref, dwq_ref, dwkv_ref, dgq_ref, dgkv_ref, ago_ref, ad_ref, anm_ref, anv_ref):
        i = pl.program_id(0)
        grad_out = ag_ref[...]
        ago_ref[...] = grad_out
        ad_ref[...], anm_ref[...], anv_ref[...] = _adamw_math(aw_ref[...], grad_out, am_ref[...], av_ref[...])

        @pl.when(i == 0)
        def _():
            dwin_ref[...] = jnp.zeros_like(dwin_ref)
            dwq_ref[...] = jnp.zeros_like(dwq_ref)
            dwkv_ref[...] = jnp.zeros_like(dwkv_ref)
            dgq_ref[...] = jnp.zeros_like(dgq_ref)
            dgkv_ref[...] = jnp.zeros_like(dgkv_ref)

        row0 = i * tr
        h = h_ref[...]
        r = lax.rsqrt(jnp.mean(h * h, axis=-1, keepdims=True) + EPS)
        n = h * r
        gv = g_ref[...]
        hn = (n * gv).astype(BF16)
        cq = cq_ref[...]
        rq = lax.rsqrt(jnp.mean(cq * cq, axis=-1, keepdims=True) + EPS)
        nq = cq * rq
        gqv = gq_ref[...]
        cqn = (nq * gqv).astype(BF16)
        dcqn = jnp.zeros((tr, Q_LORA), F32)
        for hd in range(HEADS):
            dqf = dq_ref[hd]
            dcqn = dcqn + _nn(dqf, wq_ref[hd])
            dwq_ref[hd] += _tn(dqf, cqn)
        dgq_ref[...] += jnp.sum(dcqn * nq, axis=0, keepdims=True)
        dnq = dcqn * gqv
        dcq = rq * (dnq - nq * jnp.mean(dnq * nq, axis=-1, keepdims=True))

        ckv = ckv_ref[...]
        rkv = lax.rsqrt(jnp.mean(ckv * ckv, axis=-1, keepdims=True) + EPS)
        nkv = ckv * rkv
        gkvv = gkv_ref[...]
        ckvn = (nkv * gkvv).astype(BF16)
        dckvn = jnp.zeros((tr, KV_LORA), F32)
        for hd in range(HEADS):
            dkv = dkv_ref[hd]
            dckvn = dckvn + _nt(dkv, wkv_ref[hd])
            dwkv_ref[hd] += _tn(ckvn, dkv)
        dgkv_ref[...] += jnp.sum(dckvn * nkv, axis=0, keepdims=True)
        dnkv = dckvn * gkvv
        dckv = rkv * (dnkv - nkv * jnp.mean(dnkv * nkv, axis=-1, keepdims=True))
        dkr = _unrope(dkr_ref[...], cos_ref[...], sin_ref[...])

        cur = dpl_ref[...]
        halo = jnp.where(i < nb - 1, halo_ref[...], 0.0)
        dpi = []
        for g, w in enumerate(POOL_WINDOWS):
            sl = slice(g * POOL_GROUP, (g + 1) * POOL_GROUP)
            a = jnp.concatenate([cur[:, sl] * _inv_count(row0, tr, w), halo[:, sl] * _inv_count(row0 + tr, HALO, w)], axis=0)
            acc = a
            shift = 1
            while shift < w:
                acc = acc + pltpu.roll(acc, tr + HALO - shift, 0)
                shift *= 2
            dpi.append(acc[0:tr] - cur[:, sl])

        du = jnp.concatenate([t.astype(BF16) for t in dpi] + [dpg_ref[...]] + [t.astype(BF16) for t in (dcq, dckv, dkr)],
                             axis=1)
        dwin_ref[0:O_KR_END, :] += _tn(du, hn)
        dwin_ref[O_AG:D_IN, :] += _tn(dag_ref[...], hn)
        du_ref[...] = du

    head = lambda w: pl.BlockSpec((HEADS, tr, w), lambda i: (0, i, 0))
    halo_spec = pl.BlockSpec((HALO, D_POOL), lambda i: (jnp.minimum((i + 1) * per, N // HALO - 1), 0))
    return pl.pallas_call(
        body,
        name="bwd_in",
        grid=(nb,),
        in_specs=[
            _rows(D, tr), head(256), head(256), _rows(128, tr), _rows(Q_LORA, tr), _rows(KV_LORA, tr),
            _rows(D_POOL, tr), halo_spec, _rows(D_POOL, tr), _rows(D_POOL, tr),
            _const(1, D), _const(1, Q_LORA), _const(HEADS, 256, Q_LORA),
            _const(1, KV_LORA), _const(HEADS, KV_LORA, 256), _rows(128, tr), _rows(128, tr),
        ] + [_rows(D, adam_rows)] * 4,
        out_specs=[
            _rows(O_KR_END, tr), _const(D_IN, D), _const(HEADS, 256, Q_LORA),
            _const(HEADS, KV_LORA, 256), _const(1, Q_LORA), _const(1, KV_LORA),
        ] + [_rows(D, adam_rows)] * 4,
        out_shape=[
            jax.ShapeDtypeStruct((N, O_KR_END), BF16),
            jax.ShapeDtypeStruct((D_IN, D), F32), jax.ShapeDtypeStruct((HEADS, 256, Q_LORA), F32),
            jax.ShapeDtypeStruct((HEADS, KV_LORA, 256), F32),
            jax.ShapeDtypeStruct((1, Q_LORA), F32), jax.ShapeDtypeStruct((1, KV_LORA), F32),
        ] + [jax.ShapeDtypeStruct((SHARD_OUT, D), F32)] * 4,
        compiler_params=_cparams(dimension_semantics=("arbitrary",)),
    )(h, dq, dkv, dkr, cq, ckv, dpl, dpl, dpg, dag, norm_g, gq, wq, gkv, wkv, cosf, sinf, *adam_out)


def _local_step(h, tgt, norm_g, win, gq, wq, gkv, wkv, pool_w, pool_scale, wout_s, m_wout_s, v_wout_s, gf, cosf, sinf):
    pool_in, pool_gate, cq, ckv, attn_gate, q, k, v = _fwd_in(h, norm_g, win, gq, wq, gkv, wkv, cosf, sinf)
    attn, lse, wout = _attn_fwd(q, k, v, wout_s)
    dh2, do, delta, dag, dpg, dpl, dwout, dpw, dps, dgf, loss = _mid(
        h, tgt, pool_in, pool_gate, attn_gate, attn, pool_w, pool_scale, wout, gf)
    dq, dkv, dkr, gwout = _attn_bwd(q, k, v, do, lse, delta, cosf, sinf, dwout)
    du, dwin, dwq, dwkv, dgq, dgkv, *r_out = _bwd_in(
        h, dq, dkv, dkr, cq, ckv, dpl, dpg, dag, norm_g, gq, wq, gkv, wkv, cosf, sinf,
        (wout_s, gwout, m_wout_s, v_wout_s))
    return dict(du=du, dag=dag, dh2=dh2, dwin=dwin, dwq=dwq, dwkv=dwkv, r_out=tuple(r_out), dgq=dgq, dgkv=dgkv,
                dpw=dpw, dps=dps, dgf=dgf, loss=loss)


_CHIP_RELS = ((0, 0), (1, 0), (0, 1), (1, 1))

_ARR_ROWS = (SHARD_IN, SHARD_OUT, 256, KV_LORA, N_META)
_ARR_COLS = (D, D, Q_LORA, 256, 256)
_PIECES = (
    (0, 0, 256, 0), (0, 256, SHARD_IN - 256, 1),
    (1, 0, 128, 0), (1, 128, 128, 1),
    (2, 0, 128, 0), (2, 128, 128, 1),
    (3, 0, 64, 0), (3, 64, 64, 1),
    (4, 0, N_META, 0),
)
_NP = len(_PIECES)
_PIECE_MAX = (256, 128, 128, 64, N_META)


def _gathered_at(refs, arr, chip, r0, n):
    if arr in (0, 1):
        return refs[arr].at[pl.ds(pl.multiple_of(_ARR_ROWS[arr] * chip + r0, 16), n), :]
    return refs[arr].at[chip, pl.ds(r0, n), :]


def _remote(src, dst, send_sem, recv_sem, to):
    return pltpu.make_async_remote_copy(src_ref=src, dst_ref=dst, send_sem=send_sem, recv_sem=recv_sem,
                                        device_id=to, device_id_type=MESH)


def _gather_weights(winT_s, wqT_s, wkv_s, meta_s, x2, tgt2):
    arrays = (0, 2, 3, 4)

    def body(win_ref, wq_ref, wkv_ref, meta_ref, x_ref, t_ref, win_o, wq_o, wkv_o, h_o, tp_o,
             s_win, s_wq, s_wkv, meta_all, head_buf, x_buf, t_buf, ici_send, ici_recv, fwd_send, fwd_recv,
             loc_sems, own_sems):
        x, y, c = lax.axis_index("x"), lax.axis_index("y"), lax.axis_index("c")
        me = 2 * x + y
        stage = (s_win, None, s_wq, s_wkv, meta_ref)
        outs = (win_o, None, wq_o, wkv_o, meta_all)

        _peer_barrier(x, y, c)

        frames = pl.ds(HEAD_ROWS, S)
        loads = [pltpu.make_async_copy(x_ref, x_buf, loc_sems.at[0]), pltpu.make_async_copy(t_ref, t_buf, loc_sems.at[1])]
        local = [pltpu.make_async_copy(x_buf, h_o.at[frames, :], loc_sems.at[0]),
                 pltpu.make_async_copy(t_buf, tp_o.at[frames, :], loc_sems.at[1])]
        for cp in loads:
            cp.start()

        s_win[...] = win_ref[...].astype(BF16)
        s_wq[0:QK, :] = wq_ref[...].astype(BF16)
        s_wq[QK:256, :] = jnp.zeros((256 - QK, Q_LORA), BF16)
        s_wkv[...] = wkv_ref[...].astype(BF16)

        def chip_of(rel):
            fx, fy = _CHIP_RELS[rel]
            return 2 * (x ^ fx) + (y ^ fy)

        def same_core_of(rel):
            fx, fy = _CHIP_RELS[rel]
            return (x ^ fx, y ^ fy, c)

        def ici_copy(rel, i, src_chip, to):
            arr, r0, n, _ = _PIECES[i]
            k = (rel - 1) * _NP + i
            return _remote(stage[arr].at[pl.ds(r0, n), :], _gathered_at(outs, arr, src_chip, r0, n),
                           ici_send.at[k], ici_recv.at[k], to)

        def fwd_copy(rel, i, to):
            arr, r0, n, _ = _PIECES[i]
            k = (rel - 1) * _NP + i
            place = _gathered_at(outs, arr, chip_of(rel), r0, n)
            return _remote(place, place, fwd_send.at[k], fwd_recv.at[k], to)

        for core in (0, 1):
            @pl.when(c == core)
            def _(core=core):
                mine = [i for i in range(_NP) if _PIECES[i][3] == core and _PIECES[i][0] in arrays]
                theirs = [i for i in range(_NP) if _PIECES[i][3] != core and _PIECES[i][0] in arrays]
                sends = [ici_copy(rel, i, me, same_core_of(rel)) for rel in (1, 2, 3) for i in mine]
                for cp in sends:
                    cp.start()
                for ld, st in zip(loads, local):
                    ld.wait()
                    st.start()
                own = [pltpu.make_async_copy(stage[arr], _gathered_at(outs, arr, me, 0, _ARR_ROWS[arr]), own_sems.at[arr])
                       for arr in arrays if arr != 4]
                for cp in own:
                    cp.start()
                meta_all[me] = meta_ref[...]
                for rel in (1, 2, 3):
                    for i in mine:
                        ici_copy(rel, i, chip_of(rel), (x, y, c)).wait_recv()
                        fwd = fwd_copy(rel, i, (x, y, 1 - c))
                        fwd.start()
                        sends.append(fwd)
                for rel in (1, 2, 3):
                    for i in theirs:
                        fwd_copy(rel, i, (x, y, c)).wait_recv()
                for cp in sends:
                    cp.wait_send()
                for cp in own:
                    cp.wait()

        head_buf[...] = jnp.zeros_like(head_buf)
        zeros = pltpu.make_async_copy(head_buf, tp_o.at[pl.ds(0, HEAD_ROWS), :], loc_sems.at[2])
        zeros.start()
        zeros.wait()
        for chip in range(CHIPS):
            head_buf[PAD:HEAD_ROWS, chip * 256:(chip + 1) * 256] = meta_all[chip]
        head = pltpu.make_async_copy(head_buf, h_o.at[pl.ds(0, HEAD_ROWS), :], loc_sems.at[2])
        head.start()
        head.wait()
        for cp in local:
            cp.wait()

    vm = pl.BlockSpec(memory_space=pltpu.VMEM)
    hbm = pl.BlockSpec(memory_space=pl.ANY)
    return pl.pallas_call(
        body,
        name="gather_weights",
        in_specs=[vm] * 4 + [hbm] * 2,
        out_specs=[hbm] * 5,
        out_shape=[
            jax.ShapeDtypeStruct((D_IN, D), BF16),
            jax.ShapeDtypeStruct((CHIPS, 256, Q_LORA), BF16), jax.ShapeDtypeStruct((CHIPS, KV_LORA, 256), BF16),
            jax.ShapeDtypeStruct((N, D), F32), jax.ShapeDtypeStruct((N, D), F32),
        ],
        scratch_shapes=[pltpu.VMEM((_ARR_ROWS[a], _ARR_COLS[a]), BF16) for a in (0, 2, 3)]
        + [pltpu.VMEM((CHIPS, N_META, 256), F32), pltpu.VMEM((HEAD_ROWS, D), F32), pltpu.VMEM((S, D), F32),
           pltpu.VMEM((S, D), F32)]
        + [pltpu.SemaphoreType.DMA((3 * _NP,))] * 4 + [pltpu.SemaphoreType.DMA((3,)), pltpu.SemaphoreType.DMA((4,))],
        compiler_params=_cparams(collective_id=0),
    )(winT_s, wqT_s, wkv_s, meta_s, x2, tgt2)


_SM_ROWS = (len(POOL_WINDOWS) * POOL_GROUP, VEC_ROWS)
_SM_COLS = (POOL_GROUP, D)
_SM_PIECES = ((0, 0, 256, 0), (0, 256, 256, 1), (1, 0, VEC_ROWS, 0))
_NSP = len(_SM_PIECES)


def _reduce_grads(dwin, dwq, dwkv, du, dag, h, dh2, win, norm_g, dpw, dgf, dgq, dgkv, dps, loss):
    arrays = (0, 2, 3, 4)
    early = (0, 2, 3)
    tr = ROWS_BWD
    nb = N // tr
    lead = HEAD_ROWS

    def body(dwin_ref, dwq_ref, dwkv_ref, du_hbm, dag_hbm, h_hbm, dh2_hbm, win_hbm, g_ref, dpw_ref, dgf_ref, dgq_ref,
             dgkv_ref, dps_ref, loss_ref,
             gwin_o, gwq_o, gwkv_o, gmeta_o, gpw_o, gg_o, ggf_o, ggq_o, ggkv_o, gps_o, gloss_o, gx_hbm,
             ow0, ow2, ow3, ow4, sb0, sb2, sb3, sb4, st0, st2, st3, st4, rc0, rc2, rc3, rc4,
             vec, sm_sb0, sm_sb1, sm_cs0, sm_cs1, sm_rc0, sm_rc1, vec_fin,
             win_v, du_t, dag_t, h_t, dh2_t, dh_buf, dmeta_v,
             own_sems, d2d_send, d2d_recv, ici_send, ici_recv, fin_send, fin_recv,
             swap_send, swap_recv, smi_send, smi_recv, smf_send, smf_recv, ld_sems, gx_sems):
        x, y, c = lax.axis_index("x"), lax.axis_index("y"), lax.axis_index("c")
        me = 2 * x + y
        grads = (dwin_ref, None, dwq_ref, dwkv_ref, dmeta_v)
        outs = (gwin_o, None, gwq_o, gwkv_o, gmeta_o)
        own_buf = (ow0, None, ow2, ow3, ow4)
        sib_buf = (sb0, None, sb2, sb3, sb4)
        stage = (st0, None, st2, st3, st4)
        recv = (rc0, None, rc2, rc3, rc4)
        sm_mine = (dpw_ref, vec)
        sm_sib = (sm_sb0, sm_sb1)
        sm_chip = (sm_cs0, sm_cs1)
        sm_recv = (sm_rc0, sm_rc1)
        sm_out = (gpw_o, vec_fin)
        sibling = (x, y, 1 - c)

        def chip_of(rel):
            fx, fy = _CHIP_RELS[rel]
            return 2 * (x ^ fx) + (y ^ fy)

        def same_core_of(rel):
            fx, fy = _CHIP_RELS[rel]
            return (x ^ fx, y ^ fy, c)

        def slot(bufs, i, idx):
            arr, _, n, _ = _PIECES[i]
            return bufs[arr].at[idx, pl.ds(0, n), :]

        def own_load(rel, i):
            arr, r0, n, _ = _PIECES[i]
            return pltpu.make_async_copy(_gathered_at(grads, arr, chip_of(rel), r0, n), slot(own_buf, i, rel),
                                         own_sems.at[rel * _NP + i])

        def d2d_copy(rel, i):
            arr, r0, n, _ = _PIECES[i]
            k = rel * _NP + i
            return _remote(_gathered_at(grads, arr, chip_of(rel), r0, n), slot(sib_buf, i, rel),
                           d2d_send.at[k], d2d_recv.at[k], sibling)

        def ici_copy(rel, i):
            k = (rel - 1) * _NP + i
            return _remote(slot(stage, i, rel - 1), slot(recv, i, rel - 1), ici_send.at[k], ici_recv.at[k],
                           same_core_of(rel))

        def fin_copy(i):
            arr, r0, n, _ = _PIECES[i]
            place = outs[arr].at[pl.ds(r0, n), :]
            return _remote(place, place, fin_send.at[i], fin_recv.at[i], sibling)

        def sm_ici_copy(rel, j):
            blk, r0, n, _ = _SM_PIECES[j]
            k = (rel - 1) * _NSP + j
            return _remote(sm_chip[blk].at[pl.ds(r0, n), :], sm_recv[blk].at[rel - 1, pl.ds(r0, n), :],
                           smi_send.at[k], smi_recv.at[k], same_core_of(rel))

        def sm_fin_copy(j):
            blk, r0, n, _ = _SM_PIECES[j]
            place = sm_out[blk].at[pl.ds(r0, n), :]
            return _remote(place, place, smf_send.at[j], smf_recv.at[j], sibling)

        def pieces_of(core, which):
            return ([i for i in range(_NP) if _PIECES[i][3] == core and _PIECES[i][0] in which],
                    [i for i in range(_NP) if _PIECES[i][3] != core and _PIECES[i][0] in which])

        def hand_over(mine, theirs):
            for rel in (1, 2, 3, 0):
                for i in theirs:
                    d2d_copy(rel, i).start()
                for i in mine:
                    own_load(rel, i).start()

        def send_on(mine):
            for rel in (1, 2, 3):
                for i in mine:
                    own_load(rel, i).wait()
                    d2d_copy(rel, i).wait_recv()
                    total = slot(own_buf, i, rel)[...] + slot(sib_buf, i, rel)[...]
                    slot(stage, i, rel - 1)[...] = total.astype(stage[_PIECES[i][0]].dtype)
                    ici_copy(rel, i).start()

        win_load = pltpu.make_async_copy(win_hbm, win_v, ld_sems.at[4, 0])
        win_load.start()

        def tile_loads(t):
            rows, s = pl.ds(t * tr, tr), t % 2
            pairs = ((du_hbm, du_t), (dag_hbm, dag_t), (h_hbm, h_t), (dh2_hbm, dh2_t))
            return [pltpu.make_async_copy(src.at[rows, :], dst.at[s], ld_sems.at[a, s]) for a, (src, dst) in enumerate(pairs)]

        def gx_store(t):
            if t == 0:
                return pltpu.make_async_copy(dh_buf.at[0, pl.ds(lead, tr - lead), :], gx_hbm.at[pl.ds(0, tr - lead), :],
                                             gx_sems.at[0])
            return pltpu.make_async_copy(dh_buf.at[t % 2], gx_hbm.at[pl.ds(t * tr - lead, tr), :], gx_sems.at[t % 2])

        for cp in tile_loads(0):
            cp.start()

        _peer_barrier(x, y, c)

        for core in (0, 1):
            @pl.when(c == core)
            def _(core=core):
                mine, theirs = pieces_of(core, early)
                hand_over(mine, theirs)
                send_on(mine)

        gv = g_ref[...]
        dg_acc = jnp.zeros((1, D), F32)
        for t in range(nb):
            s = t % 2
            for cp in tile_loads(t):
                cp.wait()
            if t + 1 < nb:
                for cp in tile_loads(t + 1):
                    cp.start()
            if t == 0:
                win_load.wait()
            hv = h_t[s]
            r = lax.rsqrt(jnp.mean(hv * hv, axis=-1, keepdims=True) + EPS)
            nrm = hv * r
            dhn = _nn(du_t[s], win_v[0:O_KR_END, :]) + _nn(dag_t[s], win_v[O_AG:D_IN, :])
            dg_acc = dg_acc + jnp.sum(dhn * nrm, axis=0, keepdims=True)
            dn = dhn * gv
            dh = dh2_t[s] + r * (dn - nrm * jnp.mean(dn * nrm, axis=-1, keepdims=True))
            if t >= 2:
                gx_store(t - 2).wait()
            dh_buf[s] = dh
            gx_store(t).start()
            if t == 0:
                for chip in range(CHIPS):
                    dmeta_v[chip] = dh[PAD:HEAD_ROWS, chip * 256:(chip + 1) * 256]
        gx_store(nb - 2).wait()
        gx_store(nb - 1).wait()

        vec[...] = jnp.zeros_like(vec)
        vec[0:1, :] = dg_acc
        vec[1:2, :] = dgf_ref[...]
        vec[2:3, V_GQ:V_GQ + Q_LORA] = dgq_ref[...]
        vec[2:3, V_GKV:V_GKV + KV_LORA] = dgkv_ref[...]
        vec[2:3, V_PS:V_PS + D_POOL] = dps_ref[...]
        vec[2:3, V_LOSS:D] = loss_ref[...]
        swaps = [_remote(sm_mine[b], sm_sib[b], swap_send.at[b], swap_recv.at[b], sibling) for b in (0, 1)]
        for cp in swaps:
            cp.start()

        for core in (0, 1):
            @pl.when(c == core)
            def _(core=core):
                mine, theirs = pieces_of(core, arrays)
                late_mine, late_theirs = pieces_of(core, (4,))
                sm_mine_p = [j for j in range(_NSP) if _SM_PIECES[j][3] == core]
                sm_theirs_p = [j for j in range(_NSP) if _SM_PIECES[j][3] != core]
                sends = (list(swaps) + [d2d_copy(rel, i) for rel in (1, 2, 3, 0) for i in theirs]
                         + [ici_copy(rel, i) for rel in (1, 2, 3) for i in mine])

                hand_over(late_mine, late_theirs)

                for b in (0, 1):
                    swaps[b].wait_recv()
                    sm_chip[b][...] = sm_mine[b][...] + sm_sib[b][...]
                for rel in (1, 2, 3):
                    for j in sm_mine_p:
                        cp = sm_ici_copy(rel, j)
                        cp.start()
                        sends.append(cp)

                send_on(late_mine)

                for i in mine:
                    arr, r0, n, _ = _PIECES[i]
                    own_load(0, i).wait()
                    d2d_copy(0, i).wait_recv()
                    total = slot(own_buf, i, 0)[...] + slot(sib_buf, i, 0)[...]
                    for rel in (1, 2, 3):
                        ici_copy(rel, i).wait_recv()
                        total = total + slot(recv, i, rel - 1)[...].astype(F32)
                    outs[arr][pl.ds(r0, n), :] = total
                    cp = fin_copy(i)
                    cp.start()
                    sends.append(cp)

                for j in sm_mine_p:
                    blk, r0, n, _ = _SM_PIECES[j]
                    for rel in (1, 2, 3):
                        sm_ici_copy(rel, j).wait_recv()
                    total = jnp.zeros((n, _SM_COLS[blk]), F32)
                    for chip in range(CHIPS):
                        flips = chip ^ me
                        rel = jnp.where(flips == 2, 1, jnp.where(flips == 1, 2, flips))
                        theirs_rows = sm_recv[blk][jnp.maximum(rel - 1, 0), pl.ds(r0, n), :]
                        total = total + jnp.where(rel == 0, sm_chip[blk][pl.ds(r0, n), :], theirs_rows)
                    sm_out[blk][pl.ds(r0, n), :] = total
                    cp = sm_fin_copy(j)
                    cp.start()
                    sends.append(cp)

                for i in theirs:
                    fin_copy(i).wait_recv()
                for j in sm_theirs_p:
                    sm_fin_copy(j).wait_recv()
                for cp in sends:
                    cp.wait_send()

        gg_o[...] = vec_fin[0:1, :]
        ggf_o[...] = vec_fin[1:2, :]
        ggq_o[...] = vec_fin[2:3, V_GQ:V_GQ + Q_LORA]
        ggkv_o[...] = vec_fin[2:3, V_GKV:V_GKV + KV_LORA]
        gps_o[...] = vec_fin[2:3, V_PS:V_PS + D_POOL]
        gloss_o[...] = vec_fin[2:3, V_LOSS:D]

    vm = pl.BlockSpec(memory_space=pltpu.VMEM)
    hbm = pl.BlockSpec(memory_space=pl.ANY)
    piece_buf = lambda lead, dtype: [pltpu.VMEM((lead, _PIECE_MAX[a], _ARR_COLS[a]), F32 if a == 4 else dtype)
                                     for a in arrays]
    sm_buf = lambda *lead: [pltpu.VMEM(lead + (_SM_ROWS[b], _SM_COLS[b]), F32) for b in (0, 1)]
    dma = lambda n: [pltpu.SemaphoreType.DMA((n,))] * 2
    return pl.pallas_call(
        body,
        name="reduce_grads",
        in_specs=[hbm] * 8 + [vm] * 7,
        out_specs=[vm] * 11 + [hbm],
        out_shape=[jax.ShapeDtypeStruct((_ARR_ROWS[a], _ARR_COLS[a]), F32) for a in arrays]
        + [jax.ShapeDtypeStruct((_SM_ROWS[0], _SM_COLS[0]), F32), jax.ShapeDtypeStruct((1, D), F32),
           jax.ShapeDtypeStruct((1, D), F32), jax.ShapeDtypeStruct((1, Q_LORA), F32),
           jax.ShapeDtypeStruct((1, KV_LORA), F32), jax.ShapeDtypeStruct((1, D_POOL), F32),
           jax.ShapeDtypeStruct((1, 128), F32), jax.ShapeDtypeStruct((S, D), F32)],
        scratch_shapes=piece_buf(CHIPS, F32) + piece_buf(CHIPS, F32) + piece_buf(3, BF16) + piece_buf(3, BF16)
        + [pltpu.VMEM((VEC_ROWS, D), F32)] + sm_buf() + sm_buf() + sm_buf(3) + [pltpu.VMEM((VEC_ROWS, D), F32)]
        + [pltpu.VMEM((D_IN, D), BF16), pltpu.VMEM((2, tr, O_KR_END), BF16), pltpu.VMEM((2, tr, D_POOL), BF16),
           pltpu.VMEM((2, tr, D), F32), pltpu.VMEM((2, tr, D), F32), pltpu.VMEM((2, tr, D), F32),
           pltpu.VMEM((CHIPS, N_META, 256), F32)]
        + [pltpu.SemaphoreType.DMA((CHIPS * _NP,))]
        + dma(CHIPS * _NP) + dma(3 * _NP) + dma(_NP) + dma(2) + dma(3 * _NSP) + dma(_NSP)
        + [pltpu.SemaphoreType.DMA((5, 2)), pltpu.SemaphoreType.DMA((2,))],
        compiler_params=_cparams(collective_id=3),
    )(dwin, dwq, dwkv, du, dag, h, dh2, win, norm_g, dpw, dgf, dgq, dgkv, dps, loss)


def _adamw_math(w, g, m, v):
    m = B1 * m + (1.0 - B1) * g
    v = B2 * v + (1.0 - B2) * (g * g)
    m_hat = m / C1
    v_hat = v / C2
    delta = -LR * (m_hat / (jnp.sqrt(v_hat) + ADAM_EPS) + WD * w)
    return delta, m, v


def _adamw_rows(name, w, g, m, v, block_rows):
    rows, cols = w.shape

    def body(w_ref, g_ref, m_ref, v_ref, go_ref, d_ref, nm_ref, nv_ref):
        g = g_ref[...]
        go_ref[...] = g
        d_ref[...], nm_ref[...], nv_ref[...] = _adamw_math(w_ref[...], g, m_ref[...], v_ref[...])

    spec = pl.BlockSpec((block_rows, cols), lambda i: (i, 0))
    return pl.pallas_call(
        body,
        name=name,
        grid=(rows // block_rows,),
        in_specs=[spec] * 4,
        out_specs=[spec] * 4,
        out_shape=[jax.ShapeDtypeStruct(w.shape, F32)] * 4,
        compiler_params=_cparams(dimension_semantics=("arbitrary",)),
    )(w, g, m, v)


def _adamw_small(groups):
    n = len(groups)

    def body(*refs):
        ins, outs = refs[:4 * n], refs[4 * n:]
        for t in range(n):
            w_ref, g_ref, m_ref, v_ref = ins[4 * t:4 * t + 4]
            g = g_ref[0:w_ref.shape[0], :]
            outs[4 * t][...] = g
            outs[4 * t + 1][...], outs[4 * t + 2][...], outs[4 * t + 3][...] = _adamw_math(
                w_ref[...], g, m_ref[...], v_ref[...])

    vm = pl.BlockSpec(memory_space=pltpu.VMEM)
    flat = [a for grp in groups for a in grp]
    outs = pl.pallas_call(
        body,
        name="adamw_small",
        in_specs=[vm] * (4 * n),
        out_specs=[vm] * (4 * n),
        out_shape=[jax.ShapeDtypeStruct(grp[0].shape, F32) for grp in groups for _ in range(4)],
        compiler_params=_cparams(),
    )(*flat)
    return [tuple(outs[4 * t:4 * t + 4]) for t in range(n)]


def _rope_tables():
    half = QK_ROPE // 2
    f32 = np.float32
    inv_freq = (f32(1.0) / (f32(ROPE_THETA) ** (np.arange(half, dtype=f32) / f32(half)))).astype(f32)
    pos = np.arange(N, dtype=f32) - f32(PAD)
    ang = (pos[:, None] * inv_freq[None, :]).astype(f32)
    cos, sin = np.cos(ang).astype(f32), np.sin(ang).astype(f32)
    zero = np.zeros((N, 128 - QK_ROPE), f32)
    return jnp.asarray(np.concatenate([cos, cos, zero], axis=1)), jnp.asarray(np.concatenate([-sin, sin, zero], axis=1))


def kernel(x, meta_tokens, norm_g, w_in, q_norm_g, w_q_b, kv_norm_g, w_kv_b, pool_w, pool_scale, w_out, final_norm_g, loss_target, m_meta_tokens, m_norm_g, m_w_in, m_q_norm_g, m_w_q_b, m_kv_norm_g, m_w_kv_b, m_pool_w, m_pool_scale, m_w_out, m_final_norm_g, v_meta_tokens, v_norm_g, v_w_in, v_q_norm_g, v_w_q_b, v_kv_norm_g, v_w_kv_b, v_pool_w, v_pool_scale, v_w_out, v_final_norm_g):
    tr = lambda a: a[0].T
    win, wq, wkv, h, tgt = _gather_weights(tr(w_in), tr(w_q_b), w_kv_b[0], meta_tokens, x[0], loss_target[0])
    cosf, sinf = _rope_tables()
    gf = final_norm_g.reshape(1, D)

    part = _local_step(h, tgt, norm_g, win, q_norm_g, wq, kv_norm_g, wkv, pool_w[0], pool_scale, w_out[0], m_w_out[0],
                       v_w_out[0], gf, cosf, sinf)

    pw2 = lambda a: a.reshape(len(POOL_WINDOWS) * POOL_GROUP, POOL_GROUP)
    gwinT, gwqT, gwkv, gmeta, gpw, gg, ggf, ggq, ggkv, gps, gloss, gx = _reduce_grads(
        part["dwin"], part["dwq"], part["dwkv"], part["du"], part["dag"], h, part["dh2"], win, norm_g, pw2(part["dpw"]),
        part["dgf"], part["dgq"], part["dgkv"], part["dps"], part["loss"])

    r_in = _adamw_rows("adamw_w_in", tr(w_in), gwinT, tr(m_w_in), tr(v_w_in), 248)
    r_out = part["r_out"]
    fn2 = lambda a: a.reshape(1, D)
    r_meta, r_norm, r_gq, r_wq, r_gkv, r_wkv, r_pw, r_ps, r_fn = _adamw_small([
        (meta_tokens, gmeta, m_meta_tokens, v_meta_tokens),
        (norm_g, gg, m_norm_g, v_norm_g),
        (q_norm_g, ggq, m_q_norm_g, v_q_norm_g),
        (tr(w_q_b), gwqT, tr(m_w_q_b), tr(v_w_q_b)),
        (kv_norm_g, ggkv, m_kv_norm_g, v_kv_norm_g),
        (w_kv_b[0], gwkv, m_w_kv_b[0], v_w_kv_b[0]),
        (pw2(pool_w), gpw, pw2(m_pool_w), pw2(v_pool_w)),
        (pool_scale, gps, m_pool_scale, v_pool_scale),
        (fn2(final_norm_g), ggf, fn2(m_final_norm_g), fn2(v_final_norm_g)),
    ])
    untr = lambda a: a.T[None]
    pw4 = lambda a: a.reshape(1, len(POOL_WINDOWS), POOL_GROUP, POOL_GROUP)
    per_kind = [[
        r_meta[kind], r_norm[kind], untr(r_in[kind]), r_gq[kind], untr(r_wq[kind]), r_gkv[kind], r_wkv[kind][None],
        pw4(r_pw[kind]), r_ps[kind], r_out[kind][None], r_fn[kind].reshape(D),
    ] for kind in range(4)]
    return (gloss[0, 0], gx[None], *per_kind[0], *per_kind[1], *per_kind[2], *per_kind[3])
```

```python
import jax
import jax.numpy as jnp
import numpy as np
from jax import lax
from jax.experimental import pallas as pl
from jax.experimental.pallas import tpu as pltpu

F32 = jnp.float32
BF16 = jnp.bfloat16

D = 1024
S = 2048
N_META = 16
PAD = 112
HEAD_ROWS = PAD + N_META
N = HEAD_ROWS + S
D_POOL = 512
POOL_WINDOWS = (2, 4, 8, 16)
POOL_GROUP = 128
HALO = 16
HEADS = 4
QK_NOPE = 128
QK_ROPE = 64
QK = QK_NOPE + QK_ROPE
V_HEAD = 128
Q_LORA = 256
KV_LORA = 128
D_IN = 1984
EPS = 1e-6
ROPE_THETA = 10000.0
SCALE = QK ** -0.5
CHIPS = 4

ROWS_FWD = 544
ROWS_MID = 544
ROWS_BWD = 544
TK = 128
TQ = 256
NQ = S // TQ
HEADS_PER_STEP_BWD = 2

O_PI, O_PG, O_CQ, O_CKV, O_KR, O_AG = 0, 512, 1024, 1280, 1408, 1472
O_KR_END = O_KR + 128
SHARD_IN = D_IN // CHIPS
SHARD_OUT = D // CHIPS

LR, B1, B2, ADAM_EPS, WD, STEP = 0.001, 0.9, 0.999, 1e-08, 0.01, 10
C1 = 1.0 - B1**STEP
C2 = 1.0 - B2**STEP

VMEM_LIMIT = 60 * 1024 * 1024
MESH = pl.DeviceIdType.MESH
NEG = -1e30

VEC_ROWS = 8
V_GQ, V_GKV, V_PS, V_LOSS = 0, 256, 384, 896


def _cparams(**kw):
    return pltpu.CompilerParams(vmem_limit_bytes=VMEM_LIMIT, **kw)


def _nt(a, b):
    return lax.dot_general(a, b, (((1,), (1,)), ((), ())), preferred_element_type=F32)


def _tn(a, b):
    return lax.dot_general(a, b, (((0,), (0,)), ((), ())), preferred_element_type=F32)


def _nn(a, b):
    return jnp.dot(a, b, preferred_element_type=F32)


def _swap64(t):
    return pltpu.roll(t, 32, 1) + pltpu.roll(t, 96, 1)


def _sigmoid(x):
    return 1.0 / (1.0 + jnp.exp(-x))


def _low_lanes():
    return (lax.broadcasted_iota(jnp.int32, (1, 128), 1) < QK_ROPE).astype(F32)


def _rows(w, rows):
    return pl.BlockSpec((rows, w), lambda i: (i, 0))


def _const(*shape):
    return pl.BlockSpec(shape, lambda *_: (0,) * len(shape), pipeline_mode=pl.Buffered(1))


STAT_GROUPS = HEADS // HEADS_PER_STEP_BWD


def _stat_slot(head):
    return head // HEADS_PER_STEP_BWD, head % HEADS_PER_STEP_BWD


def _peer_barrier(x, y, c):
    barrier = pltpu.get_barrier_semaphore()
    peers = [(x, y, 1 - c)] + [(x ^ fx, y ^ fy, c) for fx, fy in _CHIP_RELS[1:]]
    for peer in peers:
        pl.semaphore_signal(barrier, inc=1, device_id=peer, device_id_type=MESH)
    pl.semaphore_wait(barrier, len(peers))


def _attn_tiles():
    return [(0, TK, TK)] + [(TK + TQ * t, TQ, TK + TQ * (t + 1)) for t in range(NQ)]


def _masked_scores(q, k, rows, klen):
    s = _nt(q, k)
    col = lax.broadcasted_iota(jnp.int32, (1, TK), 1)
    head_bias = jnp.where(col >= PAD, 0.0, NEG)
    if klen == TK:
        return s + head_bias
    r = lax.broadcasted_iota(jnp.int32, (rows, 1), 0) >> 6
    c = lax.broadcasted_iota(jnp.int32, (1, rows), 1) >> 6
    diag_bias = jnp.where(c <= r, 0.0, NEG)
    parts = [s[:, 0:TK] + head_bias]
    if klen - rows > TK:
        parts.append(s[:, TK:klen - rows])
    parts.append(s[:, klen - rows:klen] + diag_bias)
    return jnp.concatenate(parts, axis=1)


def _fwd_in(h, norm_g, win, gq, wq, gkv, wkv, cosf, sinf):
    tr = ROWS_FWD

    def body(h_ref, g_ref, win_ref, gq_ref, wq_ref, gkv_ref, wkv_ref, cos_ref, sin_ref,
             pi_ref, pg_ref, cq_ref, ckv_ref, ag_ref, q_ref, k_ref, v_ref):
        h = h_ref[...]
        r = lax.rsqrt(jnp.mean(h * h, axis=-1, keepdims=True) + EPS)
        hn = ((h * r) * g_ref[...]).astype(BF16)
        u = _nt(hn, win_ref[0:O_KR_END, :])
        pi_ref[...] = u[:, O_PI:O_PG]
        pg_ref[...] = u[:, O_PG:O_CQ]
        cq = u[:, O_CQ:O_CKV]
        ckv = u[:, O_CKV:O_KR]
        cq_ref[...] = cq
        ckv_ref[...] = ckv
        ag_ref[...] = _nt(hn, win_ref[O_AG:D_IN, :])
        cosv = cos_ref[...]
        sinv = sin_ref[...]
        kr = u[:, O_KR:O_KR_END] * _low_lanes()
        kr = (kr * cosv + _swap64(kr) * sinv).astype(BF16)
        rq = lax.rsqrt(jnp.mean(cq * cq, axis=-1, keepdims=True) + EPS)
        cqn = ((cq * rq) * gq_ref[...]).astype(BF16)
        rkv = lax.rsqrt(jnp.mean(ckv * ckv, axis=-1, keepdims=True) + EPS)
        ckvn = ((ckv * rkv) * gkv_ref[...]).astype(BF16)
        for hd in range(HEADS):
            qh = _nt(cqn, wq_ref[hd]) * SCALE
            z = qh[:, QK_NOPE:]
            q_ref[hd, :, 0:QK_NOPE] = qh[:, 0:QK_NOPE].astype(BF16)
            q_ref[hd, :, QK_NOPE:] = (z * cosv + _swap64(z) * sinv).astype(BF16)
            kvh = _nn(ckvn, wkv_ref[hd])
            k_ref[hd, :, 0:QK_NOPE] = kvh[:, 0:QK_NOPE].astype(BF16)
            k_ref[hd, :, QK_NOPE:] = kr
            v_ref[hd] = kvh[:, QK_NOPE:].astype(BF16)

    head = lambda w: pl.BlockSpec((HEADS, tr, w), lambda i: (0, i, 0))
    return pl.pallas_call(
        body,
        name="fwd_in",
        grid=(N // tr,),
        in_specs=[
            _rows(D, tr), _const(1, D), _const(D_IN, D), _const(1, Q_LORA), _const(HEADS, 256, Q_LORA),
            _const(1, KV_LORA), _const(HEADS, KV_LORA, 256), _rows(128, tr), _rows(128, tr),
        ],
        out_specs=[_rows(D_POOL, tr), _rows(D_POOL, tr), _rows(Q_LORA, tr), _rows(KV_LORA, tr), _rows(D_POOL, tr),
                   head(256), head(256), head(V_HEAD)],
        out_shape=[
            jax.ShapeDtypeStruct((N, D_POOL), F32), jax.ShapeDtypeStruct((N, D_POOL), F32),
            jax.ShapeDtypeStruct((N, Q_LORA), F32), jax.ShapeDtypeStruct((N, KV_LORA), F32),
            jax.ShapeDtypeStruct((N, D_POOL), F32),
            jax.ShapeDtypeStruct((HEADS, N, 256), BF16), jax.ShapeDtypeStruct((HEADS, N, 256), BF16),
            jax.ShapeDtypeStruct((HEADS, N, V_HEAD), BF16),
        ],
        compiler_params=_cparams(dimension_semantics=("arbitrary",)),
    )(h, norm_g, win, gq, wq, gkv, wkv, cosf, sinf)


def _attn_fwd(q, k, v, wout_s):
    tiles = _attn_tiles()
    n_t = len(tiles)
    half = SHARD_OUT // 2
    fwd_step = n_t - 2

    def body(q_hbm, k_hbm, v_hbm, wout_ref, o_hbm, lse_ref, wout_o, q_buf, k_buf, v_buf, o_buf, s_wout, in_sems, out_sems,
             ici_send, ici_recv, fwd_send, fwd_recv, own_sem):
        step = pl.program_id(0)
        x, y, c = lax.axis_index("x"), lax.axis_index("y"), lax.axis_index("c")
        me = 2 * x + y

        def chip_of(rel):
            fx, fy = _CHIP_RELS[rel]
            return 2 * (x ^ fx) + (y ^ fy)

        def place(chip, core):
            return wout_o.at[pl.ds(pl.multiple_of(SHARD_OUT * chip + half * core, half), half), :]

        def ici_copy(rel, src_chip, to):
            return _remote(s_wout.at[pl.ds(pl.multiple_of(half * c, half), half), :], place(src_chip, c),
                           ici_send.at[rel - 1], ici_recv.at[rel - 1], to)

        def fwd_copy(rel, core, to):
            spot = place(chip_of(rel), core)
            return _remote(spot, spot, fwd_send.at[rel - 1], fwd_recv.at[rel - 1], to)

        own = pltpu.make_async_copy(s_wout, wout_o.at[pl.ds(pl.multiple_of(SHARD_OUT * me, SHARD_OUT), SHARD_OUT), :], own_sem)

        @pl.when(step == 0)
        def _():
            _peer_barrier(x, y, c)
            s_wout[...] = wout_ref[...].astype(BF16)
            own.start()
            for rel in (1, 2, 3):
                fx, fy = _CHIP_RELS[rel]
                ici_copy(rel, me, (x ^ fx, y ^ fy, c)).start()

        @pl.when(step == fwd_step)
        def _():
            for rel in (1, 2, 3):
                ici_copy(rel, chip_of(rel), (x, y, c)).wait_recv()
                fwd_copy(rel, c, (x, y, 1 - c)).start()

        def finish_wout():
            for rel in (1, 2, 3):
                fwd_copy(rel, 1 - c, (x, y, c)).wait_recv()
            for rel in (1, 2, 3):
                ici_copy(rel, me, (x, y, c)).wait_send()
                fwd_copy(rel, c, (x, y, c)).wait_send()
            own.wait()

        def loads(idx):
            q0, rows, _ = tiles[idx]
            rs = pl.ds(q0, rows)
            return [pltpu.make_async_copy(src.at[:, rs, :], dst.at[:, rs, :], in_sems.at[a, idx % 2])
                    for a, (src, dst) in enumerate(((q_hbm, q_buf), (k_hbm, k_buf), (v_hbm, v_buf)))]

        def store(idx):
            q0, rows, _ = tiles[idx]
            return pltpu.make_async_copy(o_buf.at[idx % 2, pl.ds(0, rows), :], o_hbm.at[pl.ds(q0, rows), :],
                                         out_sems.at[idx % 2])

        @pl.when(step == 0)
        def _():
            lse_ref[...] = jnp.zeros_like(lse_ref)
            for cp in loads(0):
                cp.start()

        for idx, (q0, rows, klen) in enumerate(tiles):
            @pl.when(step == idx)
            def _(idx=idx, q0=q0, rows=rows, klen=klen):
                for cp in loads(idx):
                    cp.wait()
                if idx + 1 < n_t:
                    for cp in loads(idx + 1):
                        cp.start()
                if idx >= 2:
                    store(idx - 2).wait()
                for hd in range(HEADS):
                    s = _masked_scores(q_buf[hd, q0:q0 + rows, :], k_buf[hd, 0:klen, :], rows, klen)
                    m = jnp.max(s, axis=-1, keepdims=True)
                    p = jnp.exp(s - m)
                    l = jnp.sum(p, axis=-1, keepdims=True)
                    o_buf[idx % 2, 0:rows, hd * V_HEAD:(hd + 1) * V_HEAD] = _nn(p.astype(BF16), v_buf[hd, 0:klen, :]) / l
                    grp, lane = _stat_slot(hd)
                    lse_ref[grp, q0:q0 + rows, lane:lane + 1] = m + jnp.log(l)
                store(idx).start()
                if idx == n_t - 1:
                    store(idx - 1).wait()
                    store(idx).wait()
                    finish_wout()

    hbm = pl.BlockSpec(memory_space=pl.ANY)
    return pl.pallas_call(
        body,
        name="attn_fwd",
        grid=(n_t,),
        in_specs=[hbm, hbm, hbm, _const(SHARD_OUT, D)],
        out_specs=[hbm, _const(STAT_GROUPS, N, 128), hbm],
        out_shape=[jax.ShapeDtypeStruct((N, HEADS * V_HEAD), F32), jax.ShapeDtypeStruct((STAT_GROUPS, N, 128), F32),
                   jax.ShapeDtypeStruct((D, D), BF16)],
        scratch_shapes=[pltpu.VMEM((HEADS, N, 256), BF16), pltpu.VMEM((HEADS, N, 256), BF16),
                        pltpu.VMEM((HEADS, N, V_HEAD), BF16), pltpu.VMEM((2, TQ, HEADS * V_HEAD), F32),
                        pltpu.VMEM((SHARD_OUT, D), BF16),
                        pltpu.SemaphoreType.DMA((3, 2)), pltpu.SemaphoreType.DMA((2,))]
        + [pltpu.SemaphoreType.DMA((3,))] * 4 + [pltpu.SemaphoreType.DMA],
        compiler_params=_cparams(dimension_semantics=("arbitrary",), collective_id=1),
    )(q, k, v, wout_s)


def _inv_count(row0, rows, w):
    row = row0 + lax.broadcasted_iota(jnp.int32, (rows, 1), 0)
    return 1.0 / jnp.clip(row - (PAD - 1), 1, w).astype(F32)


def _mid(h, tgt, pool_in, pool_gate, attn_gate, attn, pool_w, pool_scale, wout, gf):
    tr = ROWS_MID
    per = tr // HALO
    ng = len(POOL_WINDOWS)

    def body(h_ref, t_ref, pin_ref, halo_ref, pg_ref, ag_ref, at_ref, pw_ref, ps_ref, wout_ref, gf_ref,
             dh2_ref, do_ref, delta_ref, dag_ref, dpg_ref, dpl_ref, dwout_ref, dpw_ref, dps_ref, dgf_ref, loss_ref):
        i = pl.program_id(0)

        @pl.when(i == 0)
        def _():
            dwout_ref[...] = jnp.zeros_like(dwout_ref)
            dpw_ref[...] = jnp.zeros_like(dpw_ref)
            dps_ref[...] = jnp.zeros_like(dps_ref)
            dgf_ref[...] = jnp.zeros_like(dgf_ref)
            loss_ref[...] = jnp.zeros_like(loss_ref)

        row0 = i * tr
        real = (row0 + lax.broadcasted_iota(jnp.int32, (tr, 1), 0)) >= HEAD_ROWS
        h = h_ref[...]

        halo = jnp.where(i > 0, halo_ref[...], 0.0)
        ext = jnp.concatenate([halo, pin_ref[...]], axis=0)
        pooled = []
        for g, w in enumerate(POOL_WINDOWS):
            e = ext[:, g * POOL_GROUP:(g + 1) * POOL_GROUP]
            acc = e
            shift = 1
            while shift < w:
                acc = acc + pltpu.roll(acc, shift, 0)
                shift *= 2
            pooled.append((acc[HALO:] * _inv_count(row0, tr, w) - e[HALO:]).astype(BF16))
        pw = [pw_ref[g].astype(BF16) for g in range(ng)]
        mixed = jnp.concatenate([_nn(pooled[g], pw[g]) for g in range(ng)], axis=1)
        ps = ps_ref[...]
        mixed_s = mixed * ps
        pg = pg_ref[...]
        sig_p = _sigmoid(pg)
        silu_p = pg * sig_p
        pool_out = (silu_p * mixed_s).astype(BF16)
        ag = ag_ref[...]
        sig_a = _sigmoid(ag)
        silu_a = ag * sig_a
        at = at_ref[...]
        attn_out = (silu_a * at).astype(BF16)
        cat = jnp.concatenate([pool_out, attn_out], axis=1)
        h2 = h + _nn(cat, wout_ref[...])

        r2 = lax.rsqrt(jnp.mean(h2 * h2, axis=-1, keepdims=True) + EPS)
        n2 = h2 * r2
        gfv = gf_ref[...]
        err = jnp.where(real, n2 * gfv - t_ref[...], 0.0)
        loss_ref[...] += jnp.sum(jnp.sum(err * err, axis=-1, keepdims=True), axis=0, keepdims=True) * (0.5 / D)
        dy = err * (1.0 / D)
        dgf_ref[...] += jnp.sum(dy * n2, axis=0, keepdims=True)
        dn = dy * gfv
        dh2 = r2 * (dn - n2 * jnp.mean(dn * n2, axis=-1, keepdims=True))
        dh2_ref[...] = dh2
        dh2b = dh2.astype(BF16)

        dwout_ref[...] += _tn(cat, dh2b)
        dcat = _nt(dh2b, wout_ref[...])
        dpo = dcat[:, 0:D_POOL]
        dao = dcat[:, D_POOL:D]
        do = dao * silu_a
        prod = do * at
        delta_ref[...] = jnp.zeros_like(delta_ref)
        for hd in range(HEADS):
            grp, lane = _stat_slot(hd)
            cols = slice(hd * V_HEAD, (hd + 1) * V_HEAD)
            do_ref[grp, :, lane * V_HEAD:(lane + 1) * V_HEAD] = do[:, cols].astype(BF16)
            delta_ref[grp, :, lane:lane + 1] = jnp.sum(prod[:, cols], axis=-1, keepdims=True)
        dag_ref[...] = (dao * at * (sig_a * (1.0 + ag * (1.0 - sig_a)))).astype(BF16)
        dmixed_s = dpo * silu_p
        dpg_ref[...] = (dpo * mixed_s * (sig_p * (1.0 + pg * (1.0 - sig_p)))).astype(BF16)
        dps_ref[...] += jnp.sum(dmixed_s * mixed, axis=0, keepdims=True)
        dmixed = (dmixed_s * ps).astype(BF16)
        dpl = []
        for g in range(ng):
            dm = dmixed[:, g * POOL_GROUP:(g + 1) * POOL_GROUP]
            dpl.append(_nt(dm, pw[g]))
            dpw_ref[g] += _tn(pooled[g], dm)
        dpl_ref[...] = jnp.concatenate(dpl, axis=1)

    halo_spec = pl.BlockSpec((HALO, D_POOL), lambda i: (jnp.maximum(i * per - 1, 0), 0))
    return pl.pallas_call(
        body,
        name="mid",
        grid=(N // tr,),
        in_specs=[
            _rows(D, tr), _rows(D, tr), _rows(D_POOL, tr), halo_spec, _rows(D_POOL, tr), _rows(D_POOL, tr),
            _rows(D_POOL, tr), _const(ng, POOL_GROUP, POOL_GROUP), _const(1, D_POOL), _const(D, D), _const(1, D),
        ],
        out_specs=[
            _rows(D, tr), pl.BlockSpec((STAT_GROUPS, tr, HEADS_PER_STEP_BWD * V_HEAD), lambda i: (0, i, 0)),
            pl.BlockSpec((STAT_GROUPS, tr, 128), lambda i: (0, i, 0)),
            _rows(D_POOL, tr), _rows(D_POOL, tr), _rows(D_POOL, tr),
            _const(D, D), _const(ng, POOL_GROUP, POOL_GROUP), _const(1, D_POOL), _const(1, D), _const(1, 128),
        ],
        out_shape=[
            jax.ShapeDtypeStruct((N, D), F32), jax.ShapeDtypeStruct((STAT_GROUPS, N, HEADS_PER_STEP_BWD * V_HEAD), BF16),
            jax.ShapeDtypeStruct((STAT_GROUPS, N, 128), F32),
            jax.ShapeDtypeStruct((N, D_POOL), BF16), jax.ShapeDtypeStruct((N, D_POOL), BF16),
            jax.ShapeDtypeStruct((N, D_POOL), F32), jax.ShapeDtypeStruct((D, D), F32),
            jax.ShapeDtypeStruct((ng, POOL_GROUP, POOL_GROUP), F32),
            jax.ShapeDtypeStruct((1, D_POOL), F32), jax.ShapeDtypeStruct((1, D), F32), jax.ShapeDtypeStruct((1, 128), F32),
        ],
        compiler_params=_cparams(dimension_semantics=("arbitrary",)),
    )(h, tgt, pool_in, pool_in, pool_gate, attn_gate, attn, pool_w, pool_scale, wout, gf)


def _unrope(dy, cosv, sinv):
    return dy * cosv + _swap64(dy * sinv) * _low_lanes()


def _attn_bwd(q, k, v, do, lse, delta, cosf, sinf, dwout):
    tiles = _attn_tiles()
    hp = HEADS_PER_STEP_BWD
    n_g = HEADS // hp
    n_t = len(tiles)
    half = SHARD_OUT // 2
    send_at, sum_at = (0, 2), (n_g - 1, n_t // 2)

    def body(q_hbm, k_hbm, v_hbm, do_hbm, lse_ref, delta_ref, cos_ref, sin_ref, dwout_hbm, dq_hbm, dkv_ref, dkr_ref,
             gwout_ref, q_buf, k_buf, v_buf, do_buf, dq_buf, dk_acc, dv_acc, own_w, sib_w, stage_w, recv_w, gw_buf,
             in_sems, out_sems, ow_sems, d2d_send, d2d_recv, ici_send, ici_recv, fin_send, fin_recv):
        grp = pl.program_id(0)
        step = pl.program_id(1)
        heads = pl.ds(grp * hp, hp)
        x, y, c = lax.axis_index("x"), lax.axis_index("y"), lax.axis_index("c")
        sibling = (x, y, 1 - c)

        def chip_of(rel):
            fx, fy = _CHIP_RELS[rel]
            return 2 * (x ^ fx) + (y ^ fy)

        def piece(chip, core):
            return dwout_hbm.at[pl.ds(pl.multiple_of(SHARD_OUT * chip + half * core, half), half), :]

        def own_load(rel):
            return pltpu.make_async_copy(piece(chip_of(rel), c), own_w.at[rel], ow_sems.at[rel])

        def d2d_copy(rel):
            return _remote(piece(chip_of(rel), 1 - c), sib_w.at[rel], d2d_send.at[rel], d2d_recv.at[rel], sibling)

        def ici_copy(rel):
            fx, fy = _CHIP_RELS[rel]
            return _remote(stage_w.at[rel - 1], recv_w.at[rel - 1], ici_send.at[rel - 1], ici_recv.at[rel - 1],
                           (x ^ fx, y ^ fy, c))

        def fin_copy(core):
            spot = gw_buf.at[pl.ds(pl.multiple_of(half * core, half), half), :]
            return _remote(spot, spot, fin_send.at[0], fin_recv.at[0], sibling)

        @pl.when((grp == 0) & (step == 0))
        def _():
            _peer_barrier(x, y, c)
            for rel in (1, 2, 3, 0):
                d2d_copy(rel).start()
                own_load(rel).start()

        @pl.when((grp == send_at[0]) & (step == send_at[1]))
        def _():
            for rel in (1, 2, 3):
                own_load(rel).wait()
                d2d_copy(rel).wait_recv()
                stage_w[rel - 1] = (own_w[rel] + sib_w[rel]).astype(BF16)
                ici_copy(rel).start()

        @pl.when((grp == sum_at[0]) & (step == sum_at[1]))
        def _():
            own_load(0).wait()
            d2d_copy(0).wait_recv()
            total = own_w[0] + sib_w[0]
            for rel in (1, 2, 3):
                ici_copy(rel).wait_recv()
                total = total + recv_w[rel - 1].astype(F32)
            gw_buf[pl.ds(pl.multiple_of(half * c, half), half), :] = total
            fin_copy(c).start()

        def finish_dwout():
            fin_copy(1 - c).wait_recv()
            for rel in (0, 1, 2, 3):
                d2d_copy(rel).wait_send()
            for rel in (1, 2, 3):
                ici_copy(rel).wait_send()
            fin_copy(c).wait_send()
            gwout_ref[...] = gw_buf[...]

        def loads(g, idx):
            q0, rows, _ = tiles[idx]
            rs = pl.ds(q0, rows)
            par = (g * n_t + idx) % 2
            hs = pl.ds(g * hp, hp)
            pairs = ((q_hbm.at[hs, rs, :], q_buf.at[:, rs, :]), (k_hbm.at[hs, rs, :], k_buf.at[:, rs, :]),
                     (v_hbm.at[hs, rs, :], v_buf.at[:, rs, :]), (do_hbm.at[g, rs, :], do_buf.at[rs, :]))
            return [pltpu.make_async_copy(src, dst, in_sems.at[a, par]) for a, (src, dst) in enumerate(pairs)]

        def store(idx):
            q0, rows, _ = tiles[idx]
            return pltpu.make_async_copy(dq_buf.at[idx % 2, :, pl.ds(0, rows), :], dq_hbm.at[heads, pl.ds(q0, rows), :],
                                         out_sems.at[idx % 2])

        @pl.when(step == 0)
        def _():
            dk_acc[...] = jnp.zeros_like(dk_acc)
            dv_acc[...] = jnp.zeros_like(dv_acc)

        @pl.when((step == 0) & (grp == 0))
        def _():
            dkr_ref[...] = jnp.zeros_like(dkr_ref)
            for cp in loads(grp, 0):
                cp.start()

        for idx, (q0, rows, klen) in enumerate(tiles):
            @pl.when(step == idx)
            def _(idx=idx, q0=q0, rows=rows, klen=klen):
                for cp in loads(grp, idx):
                    cp.wait()
                if idx + 1 < n_t:
                    for cp in loads(grp, idx + 1):
                        cp.start()
                if idx >= 2:
                    store(idx - 2).wait()
                qs = pl.ds(q0, rows)
                for hd in range(hp):
                    qv = q_buf[hd, qs, :]
                    kv = k_buf[hd, 0:klen, :]
                    p = jnp.exp(_masked_scores(qv, kv, rows, klen) - lse_ref[0, qs, hd:hd + 1])
                    dob = do_buf[qs, hd * V_HEAD:(hd + 1) * V_HEAD]
                    ds = (p * (_nt(dob, v_buf[hd, 0:klen, :]) - delta_ref[0, qs, hd:hd + 1])).astype(BF16)
                    dq = _nn(ds, kv) * SCALE
                    dq_buf[idx % 2, hd, 0:rows, 0:QK_NOPE] = dq[:, 0:QK_NOPE].astype(BF16)
                    dq_buf[idx % 2, hd, 0:rows, QK_NOPE:] = _unrope(dq[:, QK_NOPE:], cos_ref[qs, :], sin_ref[qs, :]).astype(BF16)
                    dk_acc[hd, 0:klen, :] += _tn(ds, qv)
                    dv_acc[hd, 0:klen, :] += _tn(p.astype(BF16), dob)
                store(idx).start()

        @pl.when(step == n_t - 1)
        def _():
            @pl.when(grp + 1 < n_g)
            def _():
                for cp in loads(grp + 1, 0):
                    cp.start()

            for hd in range(hp):
                dkv_ref[hd, :, 0:QK_NOPE] = dk_acc[hd, :, 0:QK_NOPE].astype(BF16)
                dkv_ref[hd, :, QK_NOPE:] = dv_acc[hd].astype(BF16)
                dkr_ref[...] += dk_acc[hd, :, QK_NOPE:]
            store(n_t - 2).wait()
            store(n_t - 1).wait()

            @pl.when(grp == n_g - 1)
            def _():
                finish_dwout()

    hbm = pl.BlockSpec(memory_space=pl.ANY)
    stat = pl.BlockSpec((1, N, 128), lambda g, t: (g, 0, 0), pipeline_mode=pl.Buffered(1))
    piece_f32 = lambda lead: pltpu.VMEM((lead, half, D), F32)
    piece_bf16 = lambda lead: pltpu.VMEM((lead, half, D), BF16)
    return pl.pallas_call(
        body,
        name="attn_bwd",
        grid=(n_g, n_t),
        in_specs=[hbm, hbm, hbm, hbm, stat, stat, _const(N, 128), _const(N, 128), hbm],
        out_specs=[hbm, pl.BlockSpec((hp, N, 256), lambda g, t: (g, 0, 0), pipeline_mode=pl.Buffered(1)), _const(N, 128),
                   _const(SHARD_OUT, D)],
        out_shape=[
            jax.ShapeDtypeStruct((HEADS, N, 256), BF16), jax.ShapeDtypeStruct((HEADS, N, 256), BF16),
            jax.ShapeDtypeStruct((N, 128), F32), jax.ShapeDtypeStruct((SHARD_OUT, D), F32),
        ],
        scratch_shapes=[pltpu.VMEM((hp, N, 256), BF16), pltpu.VMEM((hp, N, 256), BF16), pltpu.VMEM((hp, N, V_HEAD), BF16),
                        pltpu.VMEM((N, hp * V_HEAD), BF16), pltpu.VMEM((2, hp, TQ, 256), BF16),
                        pltpu.VMEM((hp, N, 256), F32), pltpu.VMEM((hp, N, V_HEAD), F32),
                        piece_f32(CHIPS), piece_f32(CHIPS), piece_bf16(3), piece_bf16(3), pltpu.VMEM((SHARD_OUT, D), F32),
                        pltpu.SemaphoreType.DMA((4, 2)), pltpu.SemaphoreType.DMA((2,)), pltpu.SemaphoreType.DMA((CHIPS,)),
                        pltpu.SemaphoreType.DMA((CHIPS,)), pltpu.SemaphoreType.DMA((CHIPS,)),
                        pltpu.SemaphoreType.DMA((3,)), pltpu.SemaphoreType.DMA((3,)),
                        pltpu.SemaphoreType.DMA((1,)), pltpu.SemaphoreType.DMA((1,))],
        compiler_params=_cparams(dimension_semantics=("arbitrary", "arbitrary"), collective_id=2),
    )(q, k, v, do, lse, delta, cosf, sinf, dwout)


def _bwd_in(h, dq, dkv, dkr, cq, ckv, dpl, dpg, dag, norm_g, gq, wq, gkv, wkv, cosf, sinf, adam_out):
    tr = ROWS_BWD
    nb = N // tr
    per = tr // HALO
    adam_rows = SHARD_OUT // nb

    def body(h_ref, dq_ref, dkv_ref, dkr_ref, cq_ref, ckv_ref, dpl_ref, halo_ref, dpg_ref, dag_ref,
             g_ref, gq_ref, wq_ref, gkv_ref, wkv_ref, cos_ref, sin_ref, aw_ref, ag_ref, am_ref, av_ref,
             du_ref, dwin_ref, dwq_ref, dwkv_ref, dgq_ref, dgkv_ref, ago_ref, ad_ref, anm_ref, anv_ref):
        i = pl.program_id(0)
        grad_out = ag_ref[...]
        ago_ref[...] = grad_out
        ad_ref[...], anm_ref[...], anv_ref[...] = _adamw_math(aw_ref[...], grad_out, am_ref[...], av_ref[...])

        @pl.when(i == 0)
        def _():
            dwin_ref[...] = jnp.zeros_like(dwin_ref)
            dwq_ref[...] = jnp.zeros_like(dwq_ref)
            dwkv_ref[...] = jnp.zeros_like(dwkv_ref)
            dgq_ref[...] = jnp.zeros_like(dgq_ref)
            dgkv_ref[...] = jnp.zeros_like(dgkv_ref)

        row0 = i * tr
        h = h_ref[...]
        r = lax.rsqrt(jnp.mean(h * h, axis=-1, keepdims=True) + EPS)
        n = h * r
        gv = g_ref[...]
        hn = (n * gv).astype(BF16)
        cq = cq_ref[...]
        rq = lax.rsqrt(jnp.mean(cq * cq, axis=-1, keepdims=True) + EPS)
        nq = cq * rq
        gqv = gq_ref[...]
        cqn = (nq * gqv).astype(BF16)
        dcqn = jnp.zeros((tr, Q_LORA), F32)
        for hd in range(HEADS):
            dqf = dq_ref[hd]
            dcqn = dcqn + _nn(dqf, wq_ref[hd])
            dwq_ref[hd] += _tn(dqf, cqn)
        dgq_ref[...] += jnp.sum(dcqn * nq, axis=0, keepdims=True)
        dnq = dcqn * gqv
        dcq = rq * (dnq - nq * jnp.mean(dnq * nq, axis=-1, keepdims=True))

        ckv = ckv_ref[...]
        rkv = lax.rsqrt(jnp.mean(ckv * ckv, axis=-1, keepdims=True) + EPS)
        nkv = ckv * rkv
        gkvv = gkv_ref[...]
        ckvn = (nkv * gkvv).astype(BF16)
        dckvn = jnp.zeros((tr, KV_LORA), F32)
        for hd in range(HEADS):
            dkv = dkv_ref[hd]
            dckvn = dckvn + _nt(dkv, wkv_ref[hd])
            dwkv_ref[hd] += _tn(ckvn, dkv)
        dgkv_ref[...] += jnp.sum(dckvn * nkv, axis=0, keepdims=True)
        dnkv = dckvn * gkvv
        dckv = rkv * (dnkv - nkv * jnp.mean(dnkv * nkv, axis=-1, keepdims=True))
        dkr = _unrope(dkr_ref[...], cos_ref[...], sin_ref[...])

        cur = dpl_ref[...]
        halo = jnp.where(i < nb - 1, halo_ref[...], 0.0)
        dpi = []
        for g, w in enumerate(POOL_WINDOWS):
            sl = slice(g * POOL_GROUP, (g + 1) * POOL_GROUP)
            a = jnp.concatenate([cur[:, sl] * _inv_count(row0, tr, w), halo[:, sl] * _inv_count(row0 + tr, HALO, w)], axis=0)
            acc = a
            shift = 1
            while shift < w:
                acc = acc + pltpu.roll(acc, tr + HALO - shift, 0)
                shift *= 2
            dpi.append(acc[0:tr] - cur[:, sl])

        du = jnp.concatenate([t.astype(BF16) for t in dpi] + [dpg_ref[...]] + [t.astype(BF16) for t in (dcq, dckv, dkr)],
                             axis=1)
        dwin_ref[0:O_KR_END, :] += _tn(du, hn)
        dwin_ref[O_AG:D_IN, :] += _tn(dag_ref[...], hn)
        du_ref[...] = du

    head = lambda w: pl.BlockSpec((HEADS, tr, w), lambda i: (0, i, 0))
    halo_spec = pl.BlockSpec((HALO, D_POOL), lambda i: (jnp.minimum((i + 1) * per, N // HALO - 1), 0))
    return pl.pallas_call(
        body,
        name="bwd_in",
        grid=(nb,),
        in_specs=[
            _rows(D, tr), head(256), head(256), _rows(128, tr), _rows(Q_LORA, tr), _rows(KV_LORA, tr),
            _rows(D_POOL, tr), halo_spec, _rows(D_POOL, tr), _rows(D_POOL, tr),
            _const(1, D), _const(1, Q_LORA), _const(HEADS, 256, Q_LORA),
            _const(1, KV_LORA), _const(HEADS, KV_LORA, 256), _rows(128, tr), _rows(128, tr),
        ] + [_rows(D, adam_rows)] * 4,
        out_specs=[
            _rows(O_KR_END, tr), _const(D_IN, D), _const(HEADS, 256, Q_LORA),
            _const(HEADS, KV_LORA, 256), _const(1, Q_LORA), _const(1, KV_LORA),
        ] + [_rows(D, adam_rows)] * 4,
        out_shape=[
            jax.ShapeDtypeStruct((N, O_KR_END), BF16),
            jax.ShapeDtypeStruct((D_IN, D), F32), jax.ShapeDtypeStruct((HEADS, 256, Q_LORA), F32),
            jax.ShapeDtypeStruct((HEADS, KV_LORA, 256), F32),
            jax.ShapeDtypeStruct((1, Q_LORA), F32), jax.ShapeDtypeStruct((1, KV_LORA), F32),
        ] + [jax.ShapeDtypeStruct((SHARD_OUT, D), F32)] * 4,
        compiler_params=_cparams(dimension_semantics=("arbitrary",)),
    )(h, dq, dkv, dkr, cq, ckv, dpl, dpl, dpg, dag, norm_g, gq, wq, gkv, wkv, cosf, sinf, *adam_out)


def _local_step(h, tgt, norm_g, win, gq, wq, gkv, wkv, pool_w, pool_scale, wout_s, m_wout_s, v_wout_s, gf, cosf, sinf):
    pool_in, pool_gate, cq, ckv, attn_gate, q, k, v = _fwd_in(h, norm_g, win, gq, wq, gkv, wkv, cosf, sinf)
    attn, lse, wout = _attn_fwd(q, k, v, wout_s)
    dh2, do, delta, dag, dpg, dpl, dwout, dpw, dps, dgf, loss = _mid(
        h, tgt, pool_in, pool_gate, attn_gate, attn, pool_w, pool_scale, wout, gf)
    dq, dkv, dkr, gwout = _attn_bwd(q, k, v, do, lse, delta, cosf, sinf, dwout)
    du, dwin, dwq, dwkv, dgq, dgkv, *r_out = _bwd_in(
        h, dq, dkv, dkr, cq, ckv, dpl, dpg, dag, norm_g, gq, wq, gkv, wkv, cosf, sinf,
        (wout_s, gwout, m_wout_s, v_wout_s))
    return dict(du=du, dag=dag, dh2=dh2, dwin=dwin, dwq=dwq, dwkv=dwkv, r_out=tuple(r_out), dgq=dgq, dgkv=dgkv,
                dpw=dpw, dps=dps, dgf=dgf, loss=loss)


_CHIP_RELS = ((0, 0), (1, 0), (0, 1), (1, 1))

_ARR_ROWS = (SHARD_IN, SHARD_OUT, 256, KV_LORA, N_META)
_ARR_COLS = (D, D, Q_LORA, 256, 256)
_PIECES = (
    (0, 0, 256, 0), (0, 256, SHARD_IN - 256, 1),
    (1, 0, 128, 0), (1, 128, 128, 1),
    (2, 0, 128, 0), (2, 128, 128, 1),
    (3, 0, 64, 0), (3, 64, 64, 1),
    (4, 0, N_META, 0),
)
_NP = len(_PIECES)
_PIECE_MAX = (256, 128, 128, 64, N_META)


def _gathered_at(refs, arr, chip, r0, n):
    if arr in (0, 1):
        return refs[arr].at[pl.ds(pl.multiple_of(_ARR_ROWS[arr] * chip + r0, 16), n), :]
    return refs[arr].at[chip, pl.ds(r0, n), :]


def _remote(src, dst, send_sem, recv_sem, to):
    return pltpu.make_async_remote_copy(src_ref=src, dst_ref=dst, send_sem=send_sem, recv_sem=recv_sem,
                                        device_id=to, device_id_type=MESH)


def _gather_weights(winT_s, wqT_s, wkv_s, meta_s, x2, tgt2):
    arrays = (0, 2, 3, 4)

    def body(win_ref, wq_ref, wkv_ref, meta_ref, x_ref, t_ref, win_o, wq_o, wkv_o, h_o, tp_o,
             s_win, s_wq, s_wkv, meta_all, head_buf, x_buf, t_buf, ici_send, ici_recv, fwd_send, fwd_recv,
             loc_sems, own_sems):
        x, y, c = lax.axis_index("x"), lax.axis_index("y"), lax.axis_index("c")
        me = 2 * x + y
        stage = (s_win, None, s_wq, s_wkv, meta_ref)
        outs = (win_o, None, wq_o, wkv_o, meta_all)

        _peer_barrier(x, y, c)

        frames = pl.ds(HEAD_ROWS, S)
        loads = [pltpu.make_async_copy(x_ref, x_buf, loc_sems.at[0]), pltpu.make_async_copy(t_ref, t_buf, loc_sems.at[1])]
        local = [pltpu.make_async_copy(x_buf, h_o.at[frames, :], loc_sems.at[0]),
                 pltpu.make_async_copy(t_buf, tp_o.at[frames, :], loc_sems.at[1])]
        for cp in loads:
            cp.start()

        s_win[...] = win_ref[...].astype(BF16)
        s_wq[0:QK, :] = wq_ref[...].astype(BF16)
        s_wq[QK:256, :] = jnp.zeros((256 - QK, Q_LORA), BF16)
        s_wkv[...] = wkv_ref[...].astype(BF16)

        def chip_of(rel):
            fx, fy = _CHIP_RELS[rel]
            return 2 * (x ^ fx) + (y ^ fy)

        def same_core_of(rel):
            fx, fy = _CHIP_RELS[rel]
            return (x ^ fx, y ^ fy, c)

        def ici_copy(rel, i, src_chip, to):
            arr, r0, n, _ = _PIECES[i]
            k = (rel - 1) * _NP + i
            return _remote(stage[arr].at[pl.ds(r0, n), :], _gathered_at(outs, arr, src_chip, r0, n),
                           ici_send.at[k], ici_recv.at[k], to)

        def fwd_copy(rel, i, to):
            arr, r0, n, _ = _PIECES[i]
            k = (rel - 1) * _NP + i
            place = _gathered_at(outs, arr, chip_of(rel), r0, n)
            return _remote(place, place, fwd_send.at[k], fwd_recv.at[k], to)

        for core in (0, 1):
            @pl.when(c == core)
            def _(core=core):
                mine = [i for i in range(_NP) if _PIECES[i][3] == core and _PIECES[i][0] in arrays]
                theirs = [i for i in range(_NP) if _PIECES[i][3] != core and _PIECES[i][0] in arrays]
                sends = [ici_copy(rel, i, me, same_core_of(rel)) for rel in (1, 2, 3) for i in mine]
                for cp in sends:
                    cp.start()
                for ld, st in zip(loads, local):
                    ld.wait()
                    st.start()
                own = [pltpu.make_async_copy(stage[arr], _gathered_at(outs, arr, me, 0, _ARR_ROWS[arr]), own_sems.at[arr])
                       for arr in arrays if arr != 4]
                for cp in own:
                    cp.start()
                meta_all[me] = meta_ref[...]
                for rel in (1, 2, 3):
                    for i in mine:
                        ici_copy(rel, i, chip_of(rel), (x, y, c)).wait_recv()
                        fwd = fwd_copy(rel, i, (x, y, 1 - c))
                        fwd.start()
                        sends.append(fwd)
                for rel in (1, 2, 3):
                    for i in theirs:
                        fwd_copy(rel, i, (x, y, c)).wait_recv()
                for cp in sends:
                    cp.wait_send()
                for cp in own:
                    cp.wait()

        head_buf[...] = jnp.zeros_like(head_buf)
        zeros = pltpu.make_async_copy(head_buf, tp_o.at[pl.ds(0, HEAD_ROWS), :], loc_sems.at[2])
        zeros.start()
        zeros.wait()
        for chip in range(CHIPS):
            head_buf[PAD:HEAD_ROWS, chip * 256:(chip + 1) * 256] = meta_all[chip]
        head = pltpu.make_async_copy(head_buf, h_o.at[pl.ds(0, HEAD_ROWS), :], loc_sems.at[2])
        head.start()
        head.wait()
        for cp in local:
            cp.wait()

    vm = pl.BlockSpec(memory_space=pltpu.VMEM)
    hbm = pl.BlockSpec(memory_space=pl.ANY)
    return pl.pallas_call(
        body,
        name="gather_weights",
        in_specs=[vm] * 4 + [hbm] * 2,
        out_specs=[hbm] * 5,
        out_shape=[
            jax.ShapeDtypeStruct((D_IN, D), BF16),
            jax.ShapeDtypeStruct((CHIPS, 256, Q_LORA), BF16), jax.ShapeDtypeStruct((CHIPS, KV_LORA, 256), BF16),
            jax.ShapeDtypeStruct((N, D), F32), jax.ShapeDtypeStruct((N, D), F32),
        ],
        scratch_shapes=[pltpu.VMEM((_ARR_ROWS[a], _ARR_COLS[a]), BF16) for a in (0, 2, 3)]
        + [pltpu.VMEM((CHIPS, N_META, 256), F32), pltpu.VMEM((HEAD_ROWS, D), F32), pltpu.VMEM((S, D), F32),
           pltpu.VMEM((S, D), F32)]
        + [pltpu.SemaphoreType.DMA((3 * _NP,))] * 4 + [pltpu.SemaphoreType.DMA((3,)), pltpu.SemaphoreType.DMA((4,))],
        compiler_params=_cparams(collective_id=0),
    )(winT_s, wqT_s, wkv_s, meta_s, x2, tgt2)


_SM_ROWS = (len(POOL_WINDOWS) * POOL_GROUP, VEC_ROWS)
_SM_COLS = (POOL_GROUP, D)
_SM_PIECES = ((0, 0, 256, 0), (0, 256, 256, 1), (1, 0, VEC_ROWS, 0))
_NSP = len(_SM_PIECES)


def _reduce_grads(dwin, dwq, dwkv, du, dag, h, dh2, win, norm_g, dpw, dgf, dgq, dgkv, dps, loss):
    arrays = (0, 2, 3, 4)
    early = (0, 2, 3)
    tr = ROWS_BWD
    nb = N // tr
    lead = HEAD_ROWS

    def body(dwin_ref, dwq_ref, dwkv_ref, du_hbm, dag_hbm, h_hbm, dh2_hbm, win_hbm, g_ref, dpw_ref, dgf_ref, dgq_ref,
             dgkv_ref, dps_ref, loss_ref,
             gwin_o, gwq_o, gwkv_o, gmeta_o, gpw_o, gg_o, ggf_o, ggq_o, ggkv_o, gps_o, gloss_o, gx_hbm,
             ow0, ow2, ow3, ow4, sb0, sb2, sb3, sb4, st0, st2, st3, st4, rc0, rc2, rc3, rc4,
             vec, sm_sb0, sm_sb1, sm_cs0, sm_cs1, sm_rc0, sm_rc1, vec_fin,
             win_v, du_t, dag_t, h_t, dh2_t, dh_buf, dmeta_v,
             own_sems, d2d_send, d2d_recv, ici_send, ici_recv, fin_send, fin_recv,
             swap_send, swap_recv, smi_send, smi_recv, smf_send, smf_recv, ld_sems, gx_sems):
        x, y, c = lax.axis_index("x"), lax.axis_index("y"), lax.axis_index("c")
        me = 2 * x + y
        grads = (dwin_ref, None, dwq_ref, dwkv_ref, dmeta_v)
        outs = (gwin_o, None, gwq_o, gwkv_o, gmeta_o)
        own_buf = (ow0, None, ow2, ow3, ow4)
        sib_buf = (sb0, None, sb2, sb3, sb4)
        stage = (st0, None, st2, st3, st4)
        recv = (rc0, None, rc2, rc3, rc4)
        sm_mine = (dpw_ref, vec)
        sm_sib = (sm_sb0, sm_sb1)
        sm_chip = (sm_cs0, sm_cs1)
        sm_recv = (sm_rc0, sm_rc1)
        sm_out = (gpw_o, vec_fin)
        sibling = (x, y, 1 - c)

        def chip_of(rel):
            fx, fy = _CHIP_RELS[rel]
            return 2 * (x ^ fx) + (y ^ fy)

        def same_core_of(rel):
            fx, fy = _CHIP_RELS[rel]
            return (x ^ fx, y ^ fy, c)

        def slot(bufs, i, idx):
            arr, _, n, _ = _PIECES[i]
            return bufs[arr].at[idx, pl.ds(0, n), :]

        def own_load(rel, i):
            arr, r0, n, _ = _PIECES[i]
            return pltpu.make_async_copy(_gathered_at(grads, arr, chip_of(rel), r0, n), slot(own_buf, i, rel),
                                         own_sems.at[rel * _NP + i])

        def d2d_copy(rel, i):
            arr, r0, n, _ = _PIECES[i]
            k = rel * _NP + i
            return _remote(_gathered_at(grads, arr, chip_of(rel), r0, n), slot(sib_buf, i, rel),
                           d2d_send.at[k], d2d_recv.at[k], sibling)

        def ici_copy(rel, i):
            k = (rel - 1) * _NP + i
            return _remote(slot(stage, i, rel - 1), slot(recv, i, rel - 1), ici_send.at[k], ici_recv.at[k],
                           same_core_of(rel))

        def fin_copy(i):
            arr, r0, n, _ = _PIECES[i]
            place = outs[arr].at[pl.ds(r0, n), :]
            return _remote(place, place, fin_send.at[i], fin_recv.at[i], sibling)

        def sm_ici_copy(rel, j):
            blk, r0, n, _ = _SM_PIECES[j]
            k = (rel - 1) * _NSP + j
            return _remote(sm_chip[blk].at[pl.ds(r0, n), :], sm_recv[blk].at[rel - 1, pl.ds(r0, n), :],
                           smi_send.at[k], smi_recv.at[k], same_core_of(rel))

        def sm_fin_copy(j):
            blk, r0, n, _ = _SM_PIECES[j]
            place = sm_out[blk].at[pl.ds(r0, n), :]
            return _remote(place, place, smf_send.at[j], smf_recv.at[j], sibling)

        def pieces_of(core, which):
            return ([i for i in range(_NP) if _PIECES[i][3] == core and _PIECES[i][0] in which],
                    [i for i in range(_NP) if _PIECES[i][3] != core and _PIECES[i][0] in which])

        def hand_over(mine, theirs):
            for rel in (1, 2, 3, 0):
                for i in theirs:
                    d2d_copy(rel, i).start()
                for i in mine:
                    own_load(rel, i).start()

        def send_on(mine):
            for rel in (1, 2, 3):
                for i in mine:
                    own_load(rel, i).wait()
                    d2d_copy(rel, i).wait_recv()
                    total = slot(own_buf, i, rel)[...] + slot(sib_buf, i, rel)[...]
                    slot(stage, i, rel - 1)[...] = total.astype(stage[_PIECES[i][0]].dtype)
                    ici_copy(rel, i).start()

        win_load = pltpu.make_async_copy(win_hbm, win_v, ld_sems.at[4, 0])
        win_load.start()

        def tile_loads(t):
            rows, s = pl.ds(t * tr, tr), t % 2
            pairs = ((du_hbm, du_t), (dag_hbm, dag_t), (h_hbm, h_t), (dh2_hbm, dh2_t))
            return [pltpu.make_async_copy(src.at[rows, :], dst.at[s], ld_sems.at[a, s]) for a, (src, dst) in enumerate(pairs)]

        def gx_store(t):
            if t == 0:
                return pltpu.make_async_copy(dh_buf.at[0, pl.ds(lead, tr - lead), :], gx_hbm.at[pl.ds(0, tr - lead), :],
                                             gx_sems.at[0])
            return pltpu.make_async_copy(dh_buf.at[t % 2], gx_hbm.at[pl.ds(t * tr - lead, tr), :], gx_sems.at[t % 2])

        for cp in tile_loads(0):
            cp.start()

        _peer_barrier(x, y, c)

        for core in (0, 1):
            @pl.when(c == core)
            def _(core=core):
                mine, theirs = pieces_of(core, early)
                hand_over(mine, theirs)
                send_on(mine)

        gv = g_ref[...]
        dg_acc = jnp.zeros((1, D), F32)
        for t in range(nb):
            s = t % 2
            for cp in tile_loads(t):
                cp.wait()
            if t + 1 < nb:
                for cp in tile_loads(t + 1):
                    cp.start()
            if t == 0:
                win_load.wait()
            hv = h_t[s]
            r = lax.rsqrt(jnp.mean(hv * hv, axis=-1, keepdims=True) + EPS)
            nrm = hv * r
            dhn = _nn(du_t[s], win_v[0:O_KR_END, :]) + _nn(dag_t[s], win_v[O_AG:D_IN, :])
            dg_acc = dg_acc + jnp.sum(dhn * nrm, axis=0, keepdims=True)
            dn = dhn * gv
            dh = dh2_t[s] + r * (dn - nrm * jnp.mean(dn * nrm, axis=-1, keepdims=True))
            if t >= 2:
                gx_store(t - 2).wait()
            dh_buf[s] = dh
            gx_store(t).start()
            if t == 0:
                for chip in range(CHIPS):
                    dmeta_v[chip] = dh[PAD:HEAD_ROWS, chip * 256:(chip + 1) * 256]
        gx_store(nb - 2).wait()
        gx_store(nb - 1).wait()

        vec[...] = jnp.zeros_like(vec)
        vec[0:1, :] = dg_acc
        vec[1:2, :] = dgf_ref[...]
        vec[2:3, V_GQ:V_GQ + Q_LORA] = dgq_ref[...]
        vec[2:3, V_GKV:V_GKV + KV_LORA] = dgkv_ref[...]
        vec[2:3, V_PS:V_PS + D_POOL] = dps_ref[...]
        vec[2:3, V_LOSS:D] = loss_ref[...]
        swaps = [_remote(sm_mine[b], sm_sib[b], swap_send.at[b], swap_recv.at[b], sibling) for b in (0, 1)]
        for cp in swaps:
            cp.start()

        for core in (0, 1):
            @pl.when(c == core)
            def _(core=core):
                mine, theirs = pieces_of(core, arrays)
                late_mine, late_theirs = pieces_of(core, (4,))
                sm_mine_p = [j for j in range(_NSP) if _SM_PIECES[j][3] == core]
                sm_theirs_p = [j for j in range(_NSP) if _SM_PIECES[j][3] != core]
                sends = (list(swaps) + [d2d_copy(rel, i) for rel in (1, 2, 3, 0) for i in theirs]
                         + [ici_copy(rel, i) for rel in (1, 2, 3) for i in mine])

                hand_over(late_mine, late_theirs)

                for b in (0, 1):
                    swaps[b].wait_recv()
                    sm_chip[b][...] = sm_mine[b][...] + sm_sib[b][...]
                for rel in (1, 2, 3):
                    for j in sm_mine_p:
                        cp = sm_ici_copy(rel, j)
                        cp.start()
                        sends.append(cp)

                send_on(late_mine)

                for i in mine:
                    arr, r0, n, _ = _PIECES[i]
                    own_load(0, i).wait()
                    d2d_copy(0, i).wait_recv()
                    total = slot(own_buf, i, 0)[...] + slot(sib_buf, i, 0)[...]
                    for rel in (1, 2, 3):
                        ici_copy(rel, i).wait_recv()
                        total = total + slot(recv, i, rel - 1)[...].astype(F32)
                    outs[arr][pl.ds(r0, n), :] = total
                    cp = fin_copy(i)
                    cp.start()
                    sends.append(cp)

                for j in sm_mine_p:
                    blk, r0, n, _ = _SM_PIECES[j]
                    for rel in (1, 2, 3):
                        sm_ici_copy(rel, j).wait_recv()
                    total = jnp.zeros((n, _SM_COLS[blk]), F32)
                    for chip in range(CHIPS):
                        flips = chip ^ me
                        rel = jnp.where(flips == 2, 1, jnp.where(flips == 1, 2, flips))
                        theirs_rows = sm_recv[blk][jnp.maximum(rel - 1, 0), pl.ds(r0, n), :]
                        total = total + jnp.where(rel == 0, sm_chip[blk][pl.ds(r0, n), :], theirs_rows)
                    sm_out[blk][pl.ds(r0, n), :] = total
                    cp = sm_fin_copy(j)
                    cp.start()
                    sends.append(cp)

                for i in theirs:
                    fin_copy(i).wait_recv()
                for j in sm_theirs_p:
                    sm_fin_copy(j).wait_recv()
                for cp in sends:
                    cp.wait_send()

        gg_o[...] = vec_fin[0:1, :]
        ggf_o[...] = vec_fin[1:2, :]
        ggq_o[...] = vec_fin[2:3, V_GQ:V_GQ + Q_LORA]
        ggkv_o[...] = vec_fin[2:3, V_GKV:V_GKV + KV_LORA]
        gps_o[...] = vec_fin[2:3, V_PS:V_PS + D_POOL]
        gloss_o[...] = vec_fin[2:3, V_LOSS:D]

    vm = pl.BlockSpec(memory_space=pltpu.VMEM)
    hbm = pl.BlockSpec(memory_space=pl.ANY)
    piece_buf = lambda lead, dtype: [pltpu.VMEM((lead, _PIECE_MAX[a], _ARR_COLS[a]), F32 if a == 4 else dtype)
                                     for a in arrays]
    sm_buf = lambda *lead: [pltpu.VMEM(lead + (_SM_ROWS[b], _SM_COLS[b]), F32) for b in (0, 1)]
    dma = lambda n: [pltpu.SemaphoreType.DMA((n,))] * 2
    return pl.pallas_call(
        body,
        name="reduce_grads",
        in_specs=[hbm] * 8 + [vm] * 7,
        out_specs=[vm] * 11 + [hbm],
        out_shape=[jax.ShapeDtypeStruct((_ARR_ROWS[a], _ARR_COLS[a]), F32) for a in arrays]
        + [jax.ShapeDtypeStruct((_SM_ROWS[0], _SM_COLS[0]), F32), jax.ShapeDtypeStruct((1, D), F32),
           jax.ShapeDtypeStruct((1, D), F32), jax.ShapeDtypeStruct((1, Q_LORA), F32),
           jax.ShapeDtypeStruct((1, KV_LORA), F32), jax.ShapeDtypeStruct((1, D_POOL), F32),
           jax.ShapeDtypeStruct((1, 128), F32), jax.ShapeDtypeStruct((S, D), F32)],
        scratch_shapes=piece_buf(CHIPS, F32) + piece_buf(CHIPS, F32) + piece_buf(3, BF16) + piece_buf(3, BF16)
        + [pltpu.VMEM((VEC_ROWS, D), F32)] + sm_buf() + sm_buf() + sm_buf(3) + [pltpu.VMEM((VEC_ROWS, D), F32)]
        + [pltpu.VMEM((D_IN, D), BF16), pltpu.VMEM((2, tr, O_KR_END), BF16), pltpu.VMEM((2, tr, D_POOL), BF16),
           pltpu.VMEM((2, tr, D), F32), pltpu.VMEM((2, tr, D), F32), pltpu.VMEM((2, tr, D), F32),
           pltpu.VMEM((CHIPS, N_META, 256), F32)]
        + [pltpu.SemaphoreType.DMA((CHIPS * _NP,))]
        + dma(CHIPS * _NP) + dma(3 * _NP) + dma(_NP) + dma(2) + dma(3 * _NSP) + dma(_NSP)
        + [pltpu.SemaphoreType.DMA((5, 2)), pltpu.SemaphoreType.DMA((2,))],
        compiler_params=_cparams(collective_id=3),
    )(dwin, dwq, dwkv, du, dag, h, dh2, win, norm_g, dpw, dgf, dgq, dgkv, dps, loss)


def _adamw_math(w, g, m, v):
    m = B1 * m + (1.0 - B1) * g
    v = B2 * v + (1.0 - B2) * (g * g)
    m_hat = m / C1
    v_hat = v / C2
    delta = -LR * (m_hat / (jnp.sqrt(v_hat) + ADAM_EPS) + WD * w)
    return delta, m, v


def _adamw_rows(name, w, g, m, v, block_rows, carried):
    rows, cols = w.shape

    def body(w_ref, g_ref, m_ref, v_ref, carried_in, go_ref, d_ref, nm_ref, nv_ref, carried_out):
        g = g_ref[...]
        go_ref[...] = g
        d_ref[...], nm_ref[...], nv_ref[...] = _adamw_math(w_ref[...], g, m_ref[...], v_ref[...])

    spec = pl.BlockSpec((block_rows, cols), lambda i: (i, 0))
    hbm = pl.BlockSpec(memory_space=pl.ANY)
    return pl.pallas_call(
        body,
        name=name,
        grid=(rows // block_rows,),
        in_specs=[spec] * 4 + [hbm],
        out_specs=[spec] * 4 + [hbm],
        out_shape=[jax.ShapeDtypeStruct(w.shape, F32)] * 4 + [jax.ShapeDtypeStruct(carried.shape, carried.dtype)],
        input_output_aliases={4: 4},
        compiler_params=_cparams(dimension_semantics=("arbitrary",)),
    )(w, g, m, v, carried)


def _adamw_small(groups):
    n = len(groups)

    def body(*refs):
        ins, outs = refs[:4 * n], refs[4 * n:]
        for t in range(n):
            w_ref, g_ref, m_ref, v_ref = ins[4 * t:4 * t + 4]
            g = g_ref[0:w_ref.shape[0], :]
            outs[4 * t][...] = g
            outs[4 * t + 1][...], outs[4 * t + 2][...], outs[4 * t + 3][...] = _adamw_math(
                w_ref[...], g, m_ref[...], v_ref[...])

    vm = pl.BlockSpec(memory_space=pltpu.VMEM)
    flat = [a for grp in groups for a in grp]
    outs = pl.pallas_call(
        body,
        name="adamw_small",
        in_specs=[vm] * (4 * n),
        out_specs=[vm] * (4 * n),
        out_shape=[jax.ShapeDtypeStruct(grp[0].shape, F32) for grp in groups for _ in range(4)],
        compiler_params=_cparams(),
    )(*flat)
    return [tuple(outs[4 * t:4 * t + 4]) for t in range(n)]


def _rope_tables():
    half = QK_ROPE // 2
    f32 = np.float32
    inv_freq = (f32(1.0) / (f32(ROPE_THETA) ** (np.arange(half, dtype=f32) / f32(half)))).astype(f32)
    pos = np.arange(N, dtype=f32) - f32(PAD)
    ang = (pos[:, None] * inv_freq[None, :]).astype(f32)
    cos, sin = np.cos(ang).astype(f32), np.sin(ang).astype(f32)
    zero = np.zeros((N, 128 - QK_ROPE), f32)
    return jnp.asarray(np.concatenate([cos, cos, zero], axis=1)), jnp.asarray(np.concatenate([-sin, sin, zero], axis=1))


def kernel(x, meta_tokens, norm_g, w_in, q_norm_g, w_q_b, kv_norm_g, w_kv_b, pool_w, pool_scale, w_out, final_norm_g, loss_target, m_meta_tokens, m_norm_g, m_w_in, m_q_norm_g, m_w_q_b, m_kv_norm_g, m_w_kv_b, m_pool_w, m_pool_scale, m_w_out, m_final_norm_g, v_meta_tokens, v_norm_g, v_w_in, v_q_norm_g, v_w_q_b, v_kv_norm_g, v_w_kv_b, v_pool_w, v_pool_scale, v_w_out, v_final_norm_g):
    tr = lambda a: a[0].T
    win, wq, wkv, h, tgt = _gather_weights(tr(w_in), tr(w_q_b), w_kv_b[0], meta_tokens, x[0], loss_target[0])
    cosf, sinf = _rope_tables()
    gf = final_norm_g.reshape(1, D)

    part = _local_step(h, tgt, norm_g, win, q_norm_g, wq, kv_norm_g, wkv, pool_w[0], pool_scale, w_out[0], m_w_out[0],
                       v_w_out[0], gf, cosf, sinf)

    pw2 = lambda a: a.reshape(len(POOL_WINDOWS) * POOL_GROUP, POOL_GROUP)
    gwinT, gwqT, gwkv, gmeta, gpw, gg, ggf, ggq, ggkv, gps, gloss, gx = _reduce_grads(
        part["dwin"], part["dwq"], part["dwkv"], part["du"], part["dag"], h, part["dh2"], win, norm_g, pw2(part["dpw"]),
        part["dgf"], part["dgq"], part["dgkv"], part["dps"], part["loss"])

    *r_in, gx = _adamw_rows("adamw_w_in", tr(w_in), gwinT, tr(m_w_in), tr(v_w_in), 248, gx)
    r_out = part["r_out"]
    fn2 = lambda a: a.reshape(1, D)
    r_meta, r_norm, r_gq, r_wq, r_gkv, r_wkv, r_pw, r_ps, r_fn = _adamw_small([
        (meta_tokens, gmeta, m_meta_tokens, v_meta_tokens),
        (norm_g, gg, m_norm_g, v_norm_g),
        (q_norm_g, ggq, m_q_norm_g, v_q_norm_g),
        (tr(w_q_b), gwqT, tr(m_w_q_b), tr(v_w_q_b)),
        (kv_norm_g, ggkv, m_kv_norm_g, v_kv_norm_g),
        (w_kv_b[0], gwkv, m_w_kv_b[0], v_w_kv_b[0]),
        (pw2(pool_w), gpw, pw2(m_pool_w), pw2(v_pool_w)),
        (pool_scale, gps, m_pool_scale, v_pool_scale),
        (fn2(final_norm_g), ggf, fn2(m_final_norm_g), fn2(v_final_norm_g)),
    ])
    untr = lambda a: a.T[None]
    pw4 = lambda a: a.reshape(1, len(POOL_WINDOWS), POOL_GROUP, POOL_GROUP)
    per_kind = [[
        r_meta[kind], r_norm[kind], untr(r_in[kind]), r_gq[kind], untr(r_wq[kind]), r_gkv[kind], r_wkv[kind][None],
        pw4(r_pw[kind]), r_ps[kind], r_out[kind][None], r_fn[kind].reshape(D),
    ] for kind in range(4)]
    return (gloss[0, 0], gx[None], *per_kind[0], *per_kind[1], *per_kind[2], *per_kind[3])
```

```python
import jax
import jax.numpy as jnp
import numpy as np
from jax import lax
from jax.experimental import pallas as pl
from jax.experimental.pallas import tpu as pltpu

F32 = jnp.float32
BF16 = jnp.bfloat16

D = 1024
S = 2048
N_META = 16
PAD = 112
HEAD_ROWS = PAD + N_META
N = HEAD_ROWS + S
D_POOL = 512
POOL_WINDOWS = (2, 4, 8, 16)
POOL_GROUP = 128
HALO = 16
HEADS = 4
QK_NOPE = 128
QK_ROPE = 64
QK = QK_NOPE + QK_ROPE
V_HEAD = 128
Q_LORA = 256
KV_LORA = 128
D_IN = 1984
EPS = 1e-6
ROPE_THETA = 10000.0
SCALE = QK ** -0.5
CHIPS = 4

ROWS_FWD = 544
ROWS_MID = 544
ROWS_BWD = 544
TK = 128
TQ = 256
NQ = S // TQ
HEADS_PER_STEP_BWD = 2

O_PI, O_PG, O_CQ, O_CKV, O_KR, O_AG = 0, 512, 1024, 1280, 1408, 1472
O_KR_END = O_KR + 128
SHARD_IN = D_IN // CHIPS
SHARD_OUT = D // CHIPS

LR, B1, B2, ADAM_EPS, WD, STEP = 0.001, 0.9, 0.999, 1e-08, 0.01, 10
C1 = 1.0 - B1**STEP
C2 = 1.0 - B2**STEP

VMEM_LIMIT = 60 * 1024 * 1024
MESH = pl.DeviceIdType.MESH
NEG = -1e30

VEC_ROWS = 8
V_GQ, V_GKV, V_PS, V_LOSS = 0, 256, 384, 896


def _cparams(**kw):
    return pltpu.CompilerParams(vmem_limit_bytes=VMEM_LIMIT, **kw)


def _nt(a, b):
    return lax.dot_general(a, b, (((1,), (1,)), ((), ())), preferred_element_type=F32)


def _tn(a, b):
    return lax.dot_general(a, b, (((0,), (0,)), ((), ())), preferred_element_type=F32)


def _nn(a, b):
    return jnp.dot(a, b, preferred_element_type=F32)


def _swap64(t):
    return pltpu.roll(t, 32, 1) + pltpu.roll(t, 96, 1)


def _sigmoid(x):
    return 1.0 / (1.0 + jnp.exp(-x))


def _low_lanes():
    return (lax.broadcasted_iota(jnp.int32, (1, 128), 1) < QK_ROPE).astype(F32)


def _rows(w, rows):
    return pl.BlockSpec((rows, w), lambda i: (i, 0))


def _const(*shape):
    return pl.BlockSpec(shape, lambda *_: (0,) * len(shape), pipeline_mode=pl.Buffered(1))


STAT_GROUPS = HEADS // HEADS_PER_STEP_BWD


def _stat_slot(head):
    return head // HEADS_PER_STEP_BWD, head % HEADS_PER_STEP_BWD


N_PEERS = 4


def _peer_signal(x, y, c):
    barrier = pltpu.get_barrier_semaphore()
    peers = [(x, y, 1 - c)] + [(x ^ fx, y ^ fy, c) for fx, fy in _CHIP_RELS[1:]]
    assert len(peers) == N_PEERS
    for peer in peers:
        pl.semaphore_signal(barrier, inc=1, device_id=peer, device_id_type=MESH)


def _peer_wait():
    pl.semaphore_wait(pltpu.get_barrier_semaphore(), N_PEERS)


def _attn_tiles():
    return [(0, TK, TK)] + [(TK + TQ * t, TQ, TK + TQ * (t + 1)) for t in range(NQ)]


def _masked_scores(q, k, rows, klen):
    s = _nt(q, k)
    col = lax.broadcasted_iota(jnp.int32, (1, TK), 1)
    head_bias = jnp.where(col >= PAD, 0.0, NEG)
    if klen == TK:
        return s + head_bias
    r = lax.broadcasted_iota(jnp.int32, (rows, 1), 0) >> 6
    c = lax.broadcasted_iota(jnp.int32, (1, rows), 1) >> 6
    diag_bias = jnp.where(c <= r, 0.0, NEG)
    parts = [s[:, 0:TK] + head_bias]
    if klen - rows > TK:
        parts.append(s[:, TK:klen - rows])
    parts.append(s[:, klen - rows:klen] + diag_bias)
    return jnp.concatenate(parts, axis=1)


def _fwd_in(h, norm_g, win, gq, wq, gkv, wkv, cosf, sinf):
    tr = ROWS_FWD

    def body(h_ref, g_ref, win_ref, gq_ref, wq_ref, gkv_ref, wkv_ref, cos_ref, sin_ref,
             pi_ref, pg_ref, cq_ref, ckv_ref, ag_ref, q_ref, k_ref, v_ref):
        h = h_ref[...]
        r = lax.rsqrt(jnp.mean(h * h, axis=-1, keepdims=True) + EPS)
        hn = ((h * r) * g_ref[...]).astype(BF16)
        u = _nt(hn, win_ref[0:O_KR_END, :])
        pi_ref[...] = u[:, O_PI:O_PG]
        pg_ref[...] = u[:, O_PG:O_CQ]
        cq = u[:, O_CQ:O_CKV]
        ckv = u[:, O_CKV:O_KR]
        cq_ref[...] = cq
        ckv_ref[...] = ckv
        ag_ref[...] = _nt(hn, win_ref[O_AG:D_IN, :])
        cosv = cos_ref[...]
        sinv = sin_ref[...]
        kr = u[:, O_KR:O_KR_END] * _low_lanes()
        kr = (kr * cosv + _swap64(kr) * sinv).astype(BF16)
        rq = lax.rsqrt(jnp.mean(cq * cq, axis=-1, keepdims=True) + EPS)
        cqn = ((cq * rq) * gq_ref[...]).astype(BF16)
        rkv = lax.rsqrt(jnp.mean(ckv * ckv, axis=-1, keepdims=True) + EPS)
        ckvn = ((ckv * rkv) * gkv_ref[...]).astype(BF16)
        for hd in range(HEADS):
            qh = _nt(cqn, wq_ref[hd]) * SCALE
            z = qh[:, QK_NOPE:]
            q_ref[hd, :, 0:QK_NOPE] = qh[:, 0:QK_NOPE].astype(BF16)
            q_ref[hd, :, QK_NOPE:] = (z * cosv + _swap64(z) * sinv).astype(BF16)
            kvh = _nn(ckvn, wkv_ref[hd])
            k_ref[hd, :, 0:QK_NOPE] = kvh[:, 0:QK_NOPE].astype(BF16)
            k_ref[hd, :, QK_NOPE:] = kr
            v_ref[hd] = kvh[:, QK_NOPE:].astype(BF16)

    head = lambda w: pl.BlockSpec((HEADS, tr, w), lambda i: (0, i, 0))
    return pl.pallas_call(
        body,
        name="fwd_in",
        grid=(N // tr,),
        in_specs=[
            _rows(D, tr), _const(1, D), _const(D_IN, D), _const(1, Q_LORA), _const(HEADS, 256, Q_LORA),
            _const(1, KV_LORA), _const(HEADS, KV_LORA, 256), _rows(128, tr), _rows(128, tr),
        ],
        out_specs=[_rows(D_POOL, tr), _rows(D_POOL, tr), _rows(Q_LORA, tr), _rows(KV_LORA, tr), _rows(D_POOL, tr),
                   head(256), head(256), head(V_HEAD)],
        out_shape=[
            jax.ShapeDtypeStruct((N, D_POOL), F32), jax.ShapeDtypeStruct((N, D_POOL), F32),
            jax.ShapeDtypeStruct((N, Q_LORA), F32), jax.ShapeDtypeStruct((N, KV_LORA), F32),
            jax.ShapeDtypeStruct((N, D_POOL), F32),
            jax.ShapeDtypeStruct((HEADS, N, 256), BF16), jax.ShapeDtypeStruct((HEADS, N, 256), BF16),
            jax.ShapeDtypeStruct((HEADS, N, V_HEAD), BF16),
        ],
        compiler_params=_cparams(dimension_semantics=("arbitrary",)),
    )(h, norm_g, win, gq, wq, gkv, wkv, cosf, sinf)


def _attn_fwd(q, k, v, wout_s):
    tiles = _attn_tiles()
    n_t = len(tiles)
    half = SHARD_OUT // 2
    send_step = 2
    fwd_step = n_t - 2

    def body(q_hbm, k_hbm, v_hbm, wout_ref, o_hbm, lse_ref, wout_o, q_buf, k_buf, v_buf, o_buf, s_wout, in_sems, out_sems,
             ici_send, ici_recv, fwd_send, fwd_recv, own_sem):
        step = pl.program_id(0)
        x, y, c = lax.axis_index("x"), lax.axis_index("y"), lax.axis_index("c")
        me = 2 * x + y

        def chip_of(rel):
            fx, fy = _CHIP_RELS[rel]
            return 2 * (x ^ fx) + (y ^ fy)

        def place(chip, core):
            return wout_o.at[pl.ds(pl.multiple_of(SHARD_OUT * chip + half * core, half), half), :]

        def ici_copy(rel, src_chip, to):
            return _remote(s_wout.at[pl.ds(pl.multiple_of(half * c, half), half), :], place(src_chip, c),
                           ici_send.at[rel - 1], ici_recv.at[rel - 1], to)

        def fwd_copy(rel, core, to):
            spot = place(chip_of(rel), core)
            return _remote(spot, spot, fwd_send.at[rel - 1], fwd_recv.at[rel - 1], to)

        own = pltpu.make_async_copy(s_wout, wout_o.at[pl.ds(pl.multiple_of(SHARD_OUT * me, SHARD_OUT), SHARD_OUT), :], own_sem)

        @pl.when(step == 0)
        def _():
            _peer_signal(x, y, c)
            s_wout[...] = wout_ref[...].astype(BF16)
            own.start()

        @pl.when(step == send_step)
        def _():
            _peer_wait()
            for rel in (1, 2, 3):
                fx, fy = _CHIP_RELS[rel]
                ici_copy(rel, me, (x ^ fx, y ^ fy, c)).start()

        @pl.when(step == fwd_step)
        def _():
            for rel in (1, 2, 3):
                ici_copy(rel, chip_of(rel), (x, y, c)).wait_recv()
                fwd_copy(rel, c, (x, y, 1 - c)).start()

        def finish_wout():
            for rel in (1, 2, 3):
                fwd_copy(rel, 1 - c, (x, y, c)).wait_recv()
            for rel in (1, 2, 3):
                ici_copy(rel, me, (x, y, c)).wait_send()
                fwd_copy(rel, c, (x, y, c)).wait_send()
            own.wait()

        def loads(idx):
            q0, rows, _ = tiles[idx]
            rs = pl.ds(q0, rows)
            return [pltpu.make_async_copy(src.at[:, rs, :], dst.at[:, rs, :], in_sems.at[a, idx % 2])
                    for a, (src, dst) in enumerate(((q_hbm, q_buf), (k_hbm, k_buf), (v_hbm, v_buf)))]

        def store(idx):
            q0, rows, _ = tiles[idx]
            return pltpu.make_async_copy(o_buf.at[idx % 2, pl.ds(0, rows), :], o_hbm.at[pl.ds(q0, rows), :],
                                         out_sems.at[idx % 2])

        @pl.when(step == 0)
        def _():
            lse_ref[...] = jnp.zeros_like(lse_ref)
            for cp in loads(0):
                cp.start()

        for idx, (q0, rows, klen) in enumerate(tiles):
            @pl.when(step == idx)
            def _(idx=idx, q0=q0, rows=rows, klen=klen):
                for cp in loads(idx):
                    cp.wait()
                if idx + 1 < n_t:
                    for cp in loads(idx + 1):
                        cp.start()
                if idx >= 2:
                    store(idx - 2).wait()
                for hd in range(HEADS):
                    s = _masked_scores(q_buf[hd, q0:q0 + rows, :], k_buf[hd, 0:klen, :], rows, klen)
                    m = jnp.max(s, axis=-1, keepdims=True)
                    p = jnp.exp(s - m)
                    l = jnp.sum(p, axis=-1, keepdims=True)
                    o_buf[idx % 2, 0:rows, hd * V_HEAD:(hd + 1) * V_HEAD] = _nn(p.astype(BF16), v_buf[hd, 0:klen, :]) / l
                    grp, lane = _stat_slot(hd)
                    lse_ref[grp, q0:q0 + rows, lane:lane + 1] = m + jnp.log(l)
                store(idx).start()
                if idx == n_t - 1:
                    store(idx - 1).wait()
                    store(idx).wait()
                    finish_wout()

    hbm = pl.BlockSpec(memory_space=pl.ANY)
    return pl.pallas_call(
        body,
        name="attn_fwd",
        grid=(n_t,),
        in_specs=[hbm, hbm, hbm, _const(SHARD_OUT, D)],
        out_specs=[hbm, _const(STAT_GROUPS, N, 128), hbm],
        out_shape=[jax.ShapeDtypeStruct((N, HEADS * V_HEAD), F32), jax.ShapeDtypeStruct((STAT_GROUPS, N, 128), F32),
                   jax.ShapeDtypeStruct((D, D), BF16)],
        scratch_shapes=[pltpu.VMEM((HEADS, N, 256), BF16), pltpu.VMEM((HEADS, N, 256), BF16),
                        pltpu.VMEM((HEADS, N, V_HEAD), BF16), pltpu.VMEM((2, TQ, HEADS * V_HEAD), F32),
                        pltpu.VMEM((SHARD_OUT, D), BF16),
                        pltpu.SemaphoreType.DMA((3, 2)), pltpu.SemaphoreType.DMA((2,))]
        + [pltpu.SemaphoreType.DMA((3,))] * 4 + [pltpu.SemaphoreType.DMA],
        compiler_params=_cparams(dimension_semantics=("arbitrary",), collective_id=1),
    )(q, k, v, wout_s)


def _inv_count(row0, rows, w):
    row = row0 + lax.broadcasted_iota(jnp.int32, (rows, 1), 0)
    return 1.0 / jnp.clip(row - (PAD - 1), 1, w).astype(F32)


def _mid(h, tgt, pool_in, pool_gate, attn_gate, attn, pool_w, pool_scale, wout, gf):
    tr = ROWS_MID
    per = tr // HALO
    ng = len(POOL_WINDOWS)

    def body(h_ref, t_ref, pin_ref, halo_ref, pg_ref, ag_ref, at_ref, pw_ref, ps_ref, wout_ref, gf_ref,
             dh2_ref, do_ref, delta_ref, dag_ref, dpg_ref, dpl_ref, dwout_ref, dpw_ref, dps_ref, dgf_ref, loss_ref):
        i = pl.program_id(0)

        @pl.when(i == 0)
        def _():
            dwout_ref[...] = jnp.zeros_like(dwout_ref)
            dpw_ref[...] = jnp.zeros_like(dpw_ref)
            dps_ref[...] = jnp.zeros_like(dps_ref)
            dgf_ref[...] = jnp.zeros_like(dgf_ref)
            loss_ref[...] = jnp.zeros_like(loss_ref)

        row0 = i * tr
        real = (row0 + lax.broadcasted_iota(jnp.int32, (tr, 1), 0)) >= HEAD_ROWS
        h = h_ref[...]

        halo = jnp.where(i > 0, halo_ref[...], 0.0)
        ext = jnp.concatenate([halo, pin_ref[...]], axis=0)
        pooled = []
        for g, w in enumerate(POOL_WINDOWS):
            e = ext[:, g * POOL_GROUP:(g + 1) * POOL_GROUP]
            acc = e
            shift = 1
            while shift < w:
                acc = acc + pltpu.roll(acc, shift, 0)
                shift *= 2
            pooled.append((acc[HALO:] * _inv_count(row0, tr, w) - e[HALO:]).astype(BF16))
        pw = [pw_ref[g].astype(BF16) for g in range(ng)]
        mixed = jnp.concatenate([_nn(pooled[g], pw[g]) for g in range(ng)], axis=1)
        ps = ps_ref[...]
        mixed_s = mixed * ps
        pg = pg_ref[...]
        sig_p = _sigmoid(pg)
        silu_p = pg * sig_p
        pool_out = (silu_p * mixed_s).astype(BF16)
        ag = ag_ref[...]
        sig_a = _sigmoid(ag)
        silu_a = ag * sig_a
        at = at_ref[...]
        attn_out = (silu_a * at).astype(BF16)
        cat = jnp.concatenate([pool_out, attn_out], axis=1)
        h2 = h + _nn(cat, wout_ref[...])

        r2 = lax.rsqrt(jnp.mean(h2 * h2, axis=-1, keepdims=True) + EPS)
        n2 = h2 * r2
        gfv = gf_ref[...]
        err = jnp.where(real, n2 * gfv - t_ref[...], 0.0)
        loss_ref[...] += jnp.sum(jnp.sum(err * err, axis=-1, keepdims=True), axis=0, keepdims=True) * (0.5 / D)
        dy = err * (1.0 / D)
        dgf_ref[...] += jnp.sum(dy * n2, axis=0, keepdims=True)
        dn = dy * gfv
        dh2 = r2 * (dn - n2 * jnp.mean(dn * n2, axis=-1, keepdims=True))
        dh2_ref[...] = dh2
        dh2b = dh2.astype(BF16)

        dwout_ref[...] += _tn(cat, dh2b)
        dcat = _nt(dh2b, wout_ref[...])
        dpo = dcat[:, 0:D_POOL]
        dao = dcat[:, D_POOL:D]
        do = dao * silu_a
        prod = do * at
        delta_ref[...] = jnp.zeros_like(delta_ref)
        for hd in range(HEADS):
            grp, lane = _stat_slot(hd)
            cols = slice(hd * V_HEAD, (hd + 1) * V_HEAD)
            do_ref[grp, :, lane * V_HEAD:(lane + 1) * V_HEAD] = do[:, cols].astype(BF16)
            delta_ref[grp, :, lane:lane + 1] = jnp.sum(prod[:, cols], axis=-1, keepdims=True)
        dag_ref[...] = (dao * at * (sig_a * (1.0 + ag * (1.0 - sig_a)))).astype(BF16)
        dmixed_s = dpo * silu_p
        dpg_ref[...] = (dpo * mixed_s * (sig_p * (1.0 + pg * (1.0 - sig_p)))).astype(BF16)
        dps_ref[...] += jnp.sum(dmixed_s * mixed, axis=0, keepdims=True)
        dmixed = (dmixed_s * ps).astype(BF16)
        dpl = []
        for g in range(ng):
            dm = dmixed[:, g * POOL_GROUP:(g + 1) * POOL_GROUP]
            dpl.append(_nt(dm, pw[g]))
            dpw_ref[g] += _tn(pooled[g], dm)
        dpl_ref[...] = jnp.concatenate(dpl, axis=1)

    halo_spec = pl.BlockSpec((HALO, D_POOL), lambda i: (jnp.maximum(i * per - 1, 0), 0))
    return pl.pallas_call(
        body,
        name="mid",
        grid=(N // tr,),
        in_specs=[
            _rows(D, tr), _rows(D, tr), _rows(D_POOL, tr), halo_spec, _rows(D_POOL, tr), _rows(D_POOL, tr),
            _rows(D_POOL, tr), _const(ng, POOL_GROUP, POOL_GROUP), _const(1, D_POOL), _const(D, D), _const(1, D),
        ],
        out_specs=[
            _rows(D, tr), pl.BlockSpec((STAT_GROUPS, tr, HEADS_PER_STEP_BWD * V_HEAD), lambda i: (0, i, 0)),
            pl.BlockSpec((STAT_GROUPS, tr, 128), lambda i: (0, i, 0)),
            _rows(D_POOL, tr), _rows(D_POOL, tr), _rows(D_POOL, tr),
            _const(D, D), _const(ng, POOL_GROUP, POOL_GROUP), _const(1, D_POOL), _const(1, D), _const(1, 128),
        ],
        out_shape=[
            jax.ShapeDtypeStruct((N, D), F32), jax.ShapeDtypeStruct((STAT_GROUPS, N, HEADS_PER_STEP_BWD * V_HEAD), BF16),
            jax.ShapeDtypeStruct((STAT_GROUPS, N, 128), F32),
            jax.ShapeDtypeStruct((N, D_POOL), BF16), jax.ShapeDtypeStruct((N, D_POOL), BF16),
            jax.ShapeDtypeStruct((N, D_POOL), F32), jax.ShapeDtypeStruct((D, D), F32),
            jax.ShapeDtypeStruct((ng, POOL_GROUP, POOL_GROUP), F32),
            jax.ShapeDtypeStruct((1, D_POOL), F32), jax.ShapeDtypeStruct((1, D), F32), jax.ShapeDtypeStruct((1, 128), F32),
        ],
        compiler_params=_cparams(dimension_semantics=("arbitrary",)),
    )(h, tgt, pool_in, pool_in, pool_gate, attn_gate, attn, pool_w, pool_scale, wout, gf)


def _unrope(dy, cosv, sinv):
    return dy * cosv + _swap64(dy * sinv) * _low_lanes()


def _attn_bwd(q, k, v, do, lse, delta, cosf, sinf, dwout):
    tiles = _attn_tiles()
    hp = HEADS_PER_STEP_BWD
    n_g = HEADS // hp
    n_t = len(tiles)
    half = SHARD_OUT // 2
    swap_at, send_at, sum_at = (0, 3), (0, 5), (n_g - 1, n_t // 2)

    def body(q_hbm, k_hbm, v_hbm, do_hbm, lse_ref, delta_ref, cos_ref, sin_ref, dwout_hbm, dq_hbm, dkv_ref, dkr_ref,
             gwout_ref, q_buf, k_buf, v_buf, do_buf, dq_buf, dk_acc, dv_acc, own_w, sib_w, stage_w, recv_w, gw_buf,
             in_sems, out_sems, ow_sems, d2d_send, d2d_recv, ici_send, ici_recv, fin_send, fin_recv):
        grp = pl.program_id(0)
        step = pl.program_id(1)
        heads = pl.ds(grp * hp, hp)
        x, y, c = lax.axis_index("x"), lax.axis_index("y"), lax.axis_index("c")
        sibling = (x, y, 1 - c)

        def chip_of(rel):
            fx, fy = _CHIP_RELS[rel]
            return 2 * (x ^ fx) + (y ^ fy)

        def piece(chip, core):
            return dwout_hbm.at[pl.ds(pl.multiple_of(SHARD_OUT * chip + half * core, half), half), :]

        def own_load(rel):
            return pltpu.make_async_copy(piece(chip_of(rel), c), own_w.at[rel], ow_sems.at[rel])

        def d2d_copy(rel):
            return _remote(piece(chip_of(rel), 1 - c), sib_w.at[rel], d2d_send.at[rel], d2d_recv.at[rel], sibling)

        def ici_copy(rel):
            fx, fy = _CHIP_RELS[rel]
            return _remote(stage_w.at[rel - 1], recv_w.at[rel - 1], ici_send.at[rel - 1], ici_recv.at[rel - 1],
                           (x ^ fx, y ^ fy, c))

        def fin_copy(core):
            spot = gw_buf.at[pl.ds(pl.multiple_of(half * core, half), half), :]
            return _remote(spot, spot, fin_send.at[0], fin_recv.at[0], sibling)

        @pl.when((grp == 0) & (step == 0))
        def _():
            _peer_signal(x, y, c)
            for rel in (1, 2, 3, 0):
                own_load(rel).start()

        @pl.when((grp == swap_at[0]) & (step == swap_at[1]))
        def _():
            _peer_wait()
            for rel in (1, 2, 3, 0):
                d2d_copy(rel).start()

        @pl.when((grp == send_at[0]) & (step == send_at[1]))
        def _():
            for rel in (1, 2, 3):
                own_load(rel).wait()
                d2d_copy(rel).wait_recv()
                stage_w[rel - 1] = (own_w[rel] + sib_w[rel]).astype(BF16)
                ici_copy(rel).start()

        @pl.when((grp == sum_at[0]) & (step == sum_at[1]))
        def _():
            own_load(0).wait()
            d2d_copy(0).wait_recv()
            total = own_w[0] + sib_w[0]
            for rel in (1, 2, 3):
                ici_copy(rel).wait_recv()
                total = total + recv_w[rel - 1].astype(F32)
            gw_buf[pl.ds(pl.multiple_of(half * c, half), half), :] = total
            fin_copy(c).start()

        def finish_dwout():
            fin_copy(1 - c).wait_recv()
            for rel in (0, 1, 2, 3):
                d2d_copy(rel).wait_send()
            for rel in (1, 2, 3):
                ici_copy(rel).wait_send()
            fin_copy(c).wait_send()
            gwout_ref[...] = gw_buf[...]

        def loads(g, idx):
            q0, rows, _ = tiles[idx]
            rs = pl.ds(q0, rows)
            par = (g * n_t + idx) % 2
            hs = pl.ds(g * hp, hp)
            pairs = ((q_hbm.at[hs, rs, :], q_buf.at[:, rs, :]), (k_hbm.at[hs, rs, :], k_buf.at[:, rs, :]),
                     (v_hbm.at[hs, rs, :], v_buf.at[:, rs, :]), (do_hbm.at[g, rs, :], do_buf.at[rs, :]))
            return [pltpu.make_async_copy(src, dst, in_sems.at[a, par]) for a, (src, dst) in enumerate(pairs)]

        def store(idx):
            q0, rows, _ = tiles[idx]
            return pltpu.make_async_copy(dq_buf.at[idx % 2, :, pl.ds(0, rows), :], dq_hbm.at[heads, pl.ds(q0, rows), :],
                                         out_sems.at[idx % 2])

        @pl.when(step == 0)
        def _():
            dk_acc[...] = jnp.zeros_like(dk_acc)
            dv_acc[...] = jnp.zeros_like(dv_acc)

        @pl.when((step == 0) & (grp == 0))
        def _():
            dkr_ref[...] = jnp.zeros_like(dkr_ref)
            for cp in loads(grp, 0):
                cp.start()

        for idx, (q0, rows, klen) in enumerate(tiles):
            @pl.when(step == idx)
            def _(idx=idx, q0=q0, rows=rows, klen=klen):
                for cp in loads(grp, idx):
                    cp.wait()
                if idx + 1 < n_t:
                    for cp in loads(grp, idx + 1):
                        cp.start()
                if idx >= 2:
                    store(idx - 2).wait()
                qs = pl.ds(q0, rows)
                for hd in range(hp):
                    qv = q_buf[hd, qs, :]
                    kv = k_buf[hd, 0:klen, :]
                    p = jnp.exp(_masked_scores(qv, kv, rows, klen) - lse_ref[0, qs, hd:hd + 1])
                    dob = do_buf[qs, hd * V_HEAD:(hd + 1) * V_HEAD]
                    ds = (p * (_nt(dob, v_buf[hd, 0:klen, :]) - delta_ref[0, qs, hd:hd + 1])).astype(BF16)
                    dq = _nn(ds, kv) * SCALE
                    dq_buf[idx % 2, hd, 0:rows, 0:QK_NOPE] = dq[:, 0:QK_NOPE].astype(BF16)
                    dq_buf[idx % 2, hd, 0:rows, QK_NOPE:] = _unrope(dq[:, QK_NOPE:], cos_ref[qs, :], sin_ref[qs, :]).astype(BF16)
                    dk_acc[hd, 0:klen, :] += _tn(ds, qv)
                    dv_acc[hd, 0:klen, :] += _tn(p.astype(BF16), dob)
                store(idx).start()

        @pl.when(step == n_t - 1)
        def _():
            @pl.when(grp + 1 < n_g)
            def _():
                for cp in loads(grp + 1, 0):
                    cp.start()

            for hd in range(hp):
                dkv_ref[hd, :, 0:QK_NOPE] = dk_acc[hd, :, 0:QK_NOPE].astype(BF16)
                dkv_ref[hd, :, QK_NOPE:] = dv_acc[hd].astype(BF16)
                dkr_ref[...] += dk_acc[hd, :, QK_NOPE:]
            store(n_t - 2).wait()
            store(n_t - 1).wait()

            @pl.when(grp == n_g - 1)
            def _():
                finish_dwout()

    hbm = pl.BlockSpec(memory_space=pl.ANY)
    stat = pl.BlockSpec((1, N, 128), lambda g, t: (g, 0, 0), pipeline_mode=pl.Buffered(1))
    piece_f32 = lambda lead: pltpu.VMEM((lead, half, D), F32)
    piece_bf16 = lambda lead: pltpu.VMEM((lead, half, D), BF16)
    return pl.pallas_call(
        body,
        name="attn_bwd",
        grid=(n_g, n_t),
        in_specs=[hbm, hbm, hbm, hbm, stat, stat, _const(N, 128), _const(N, 128), hbm],
        out_specs=[hbm, pl.BlockSpec((hp, N, 256), lambda g, t: (g, 0, 0), pipeline_mode=pl.Buffered(1)), _const(N, 128),
                   _const(SHARD_OUT, D)],
        out_shape=[
            jax.ShapeDtypeStruct((HEADS, N, 256), BF16), jax.ShapeDtypeStruct((HEADS, N, 256), BF16),
            jax.ShapeDtypeStruct((N, 128), F32), jax.ShapeDtypeStruct((SHARD_OUT, D), F32),
        ],
        scratch_shapes=[pltpu.VMEM((hp, N, 256), BF16), pltpu.VMEM((hp, N, 256), BF16), pltpu.VMEM((hp, N, V_HEAD), BF16),
                        pltpu.VMEM((N, hp * V_HEAD), BF16), pltpu.VMEM((2, hp, TQ, 256), BF16),
                        pltpu.VMEM((hp, N, 256), F32), pltpu.VMEM((hp, N, V_HEAD), F32),
                        piece_f32(CHIPS), piece_f32(CHIPS), piece_bf16(3), piece_bf16(3), pltpu.VMEM((SHARD_OUT, D), F32),
                        pltpu.SemaphoreType.DMA((4, 2)), pltpu.SemaphoreType.DMA((2,)), pltpu.SemaphoreType.DMA((CHIPS,)),
                        pltpu.SemaphoreType.DMA((CHIPS,)), pltpu.SemaphoreType.DMA((CHIPS,)),
                        pltpu.SemaphoreType.DMA((3,)), pltpu.SemaphoreType.DMA((3,)),
                        pltpu.SemaphoreType.DMA((1,)), pltpu.SemaphoreType.DMA((1,))],
        compiler_params=_cparams(dimension_semantics=("arbitrary", "arbitrary"), collective_id=2),
    )(q, k, v, do, lse, delta, cosf, sinf, dwout)


def _bwd_in(h, dh2, dq, dkv, dkr, cq, ckv, dpl, dpg, dag, norm_g, win, gq, wq, gkv, wkv, cosf, sinf, adam_out):
    tr = ROWS_BWD
    nb = N // tr
    per = tr // HALO
    lead = HEAD_ROWS
    adam_rows = SHARD_OUT // nb

    def body(h_ref, dh2_ref, dq_ref, dkv_ref, dkr_ref, cq_ref, ckv_ref, dpl_ref, halo_ref, dpg_ref, dag_ref,
             g_ref, win_ref, gq_ref, wq_ref, gkv_ref, wkv_ref, cos_ref, sin_ref, aw_ref, ag_ref, am_ref, av_ref,
             gx_ref, dmeta_ref, du_ref, hn_ref, dwq_ref, dwkv_ref, dg_ref, dgq_ref, dgkv_ref, ago_ref, ad_ref, anm_ref, anv_ref,
             dh_buf, gx_sem):
        i = pl.program_id(0)
        grad_out = ag_ref[...]
        ago_ref[...] = grad_out
        ad_ref[...], anm_ref[...], anv_ref[...] = _adamw_math(aw_ref[...], grad_out, am_ref[...], av_ref[...])

        @pl.when(i == 0)
        def _():
            dwq_ref[...] = jnp.zeros_like(dwq_ref)
            dwkv_ref[...] = jnp.zeros_like(dwkv_ref)
            dg_ref[...] = jnp.zeros_like(dg_ref)
            dgq_ref[...] = jnp.zeros_like(dgq_ref)
            dgkv_ref[...] = jnp.zeros_like(dgkv_ref)

        row0 = i * tr
        h = h_ref[...]
        r = lax.rsqrt(jnp.mean(h * h, axis=-1, keepdims=True) + EPS)
        n = h * r
        gv = g_ref[...]
        hn = (n * gv).astype(BF16)
        cq = cq_ref[...]
        rq = lax.rsqrt(jnp.mean(cq * cq, axis=-1, keepdims=True) + EPS)
        nq = cq * rq
        gqv = gq_ref[...]
        cqn = (nq * gqv).astype(BF16)
        dcqn = jnp.zeros((tr, Q_LORA), F32)
        for hd in range(HEADS):
            dqf = dq_ref[hd]
            dcqn = dcqn + _nn(dqf, wq_ref[hd])
            dwq_ref[hd] += _tn(dqf, cqn)
        dgq_ref[...] += jnp.sum(dcqn * nq, axis=0, keepdims=True)
        dnq = dcqn * gqv
        dcq = rq * (dnq - nq * jnp.mean(dnq * nq, axis=-1, keepdims=True))

        ckv = ckv_ref[...]
        rkv = lax.rsqrt(jnp.mean(ckv * ckv, axis=-1, keepdims=True) + EPS)
        nkv = ckv * rkv
        gkvv = gkv_ref[...]
        ckvn = (nkv * gkvv).astype(BF16)
        dckvn = jnp.zeros((tr, KV_LORA), F32)
        for hd in range(HEADS):
            dkv = dkv_ref[hd]
            dckvn = dckvn + _nt(dkv, wkv_ref[hd])
            dwkv_ref[hd] += _tn(ckvn, dkv)
        dgkv_ref[...] += jnp.sum(dckvn * nkv, axis=0, keepdims=True)
        dnkv = dckvn * gkvv
        dckv = rkv * (dnkv - nkv * jnp.mean(dnkv * nkv, axis=-1, keepdims=True))
        dkr = _unrope(dkr_ref[...], cos_ref[...], sin_ref[...])

        cur = dpl_ref[...]
        halo = jnp.where(i < nb - 1, halo_ref[...], 0.0)
        dpi = []
        for g, w in enumerate(POOL_WINDOWS):
            sl = slice(g * POOL_GROUP, (g + 1) * POOL_GROUP)
            a = jnp.concatenate([cur[:, sl] * _inv_count(row0, tr, w), halo[:, sl] * _inv_count(row0 + tr, HALO, w)], axis=0)
            acc = a
            shift = 1
            while shift < w:
                acc = acc + pltpu.roll(acc, tr + HALO - shift, 0)
                shift *= 2
            dpi.append(acc[0:tr] - cur[:, sl])

        du = jnp.concatenate([t.astype(BF16) for t in dpi] + [dpg_ref[...]] + [t.astype(BF16) for t in (dcq, dckv, dkr)],
                             axis=1)
        dagb = dag_ref[...]
        du_ref[...] = du
        hn_ref[...] = hn
        dhn = _nn(du, win_ref[0:O_KR_END, :]) + _nn(dagb, win_ref[O_AG:D_IN, :])
        dg_ref[...] += jnp.sum(dhn * n, axis=0, keepdims=True)
        dn = dhn * gv
        dh = dh2_ref[...] + r * (dn - n * jnp.mean(dn * n, axis=-1, keepdims=True))

        first = pltpu.make_async_copy(dh_buf.at[pl.ds(lead, tr - lead), :], gx_ref.at[pl.ds(0, tr - lead), :], gx_sem)
        later = lambda step: pltpu.make_async_copy(
            dh_buf, gx_ref.at[pl.ds(pl.multiple_of(step * tr - lead, 16), tr), :], gx_sem)

        @pl.when(i == 1)
        def _():
            first.wait()

        @pl.when(i > 1)
        def _():
            later(i - 1).wait()

        dh_buf[...] = dh

        @pl.when(i == 0)
        def _():
            first.start()
            for chip in range(CHIPS):
                dmeta_ref[chip] = dh[PAD:HEAD_ROWS, chip * 256:(chip + 1) * 256]

        @pl.when(i > 0)
        def _():
            later(i).start()

        @pl.when(i == nb - 1)
        def _():
            later(i).wait()

    head = lambda w: pl.BlockSpec((HEADS, tr, w), lambda i: (0, i, 0))
    halo_spec = pl.BlockSpec((HALO, D_POOL), lambda i: (jnp.minimum((i + 1) * per, N // HALO - 1), 0))
    return pl.pallas_call(
        body,
        name="bwd_in",
        grid=(nb,),
        in_specs=[
            _rows(D, tr), _rows(D, tr), head(256), head(256), _rows(128, tr), _rows(Q_LORA, tr), _rows(KV_LORA, tr),
            _rows(D_POOL, tr), halo_spec, _rows(D_POOL, tr), _rows(D_POOL, tr),
            _const(1, D), _const(D_IN, D), _const(1, Q_LORA), _const(HEADS, 256, Q_LORA),
            _const(1, KV_LORA), _const(HEADS, KV_LORA, 256), _rows(128, tr), _rows(128, tr),
        ] + [_rows(D, adam_rows)] * 4,
        out_specs=[
            pl.BlockSpec(memory_space=pl.ANY), _const(CHIPS, N_META, 256), _rows(O_KR_END, tr), _rows(D, tr),
            _const(HEADS, 256, Q_LORA),
            _const(HEADS, KV_LORA, 256), _const(1, D), _const(1, Q_LORA), _const(1, KV_LORA),
        ] + [_rows(D, adam_rows)] * 4,
        out_shape=[
            jax.ShapeDtypeStruct((S, D), F32), jax.ShapeDtypeStruct((CHIPS, N_META, 256), F32),
            jax.ShapeDtypeStruct((N, O_KR_END), BF16), jax.ShapeDtypeStruct((N, D), BF16),
            jax.ShapeDtypeStruct((HEADS, 256, Q_LORA), F32),
            jax.ShapeDtypeStruct((HEADS, KV_LORA, 256), F32),
            jax.ShapeDtypeStruct((1, D), F32), jax.ShapeDtypeStruct((1, Q_LORA), F32), jax.ShapeDtypeStruct((1, KV_LORA), F32),
        ] + [jax.ShapeDtypeStruct((SHARD_OUT, D), F32)] * 4,
        scratch_shapes=[pltpu.VMEM((tr, D), F32), pltpu.SemaphoreType.DMA],
        compiler_params=_cparams(dimension_semantics=("arbitrary",)),
    )(h, dh2, dq, dkv, dkr, cq, ckv, dpl, dpl, dpg, dag, norm_g, win, gq, wq, gkv, wkv, cosf, sinf, *adam_out)


def _local_step(h, tgt, norm_g, win, gq, wq, gkv, wkv, pool_w, pool_scale, wout_s, m_wout_s, v_wout_s, gf, cosf, sinf):
    pool_in, pool_gate, cq, ckv, attn_gate, q, k, v = _fwd_in(h, norm_g, win, gq, wq, gkv, wkv, cosf, sinf)
    attn, lse, wout = _attn_fwd(q, k, v, wout_s)
    dh2, do, delta, dag, dpg, dpl, dwout, dpw, dps, dgf, loss = _mid(
        h, tgt, pool_in, pool_gate, attn_gate, attn, pool_w, pool_scale, wout, gf)
    dq, dkv, dkr, gwout = _attn_bwd(q, k, v, do, lse, delta, cosf, sinf, dwout)
    gx, dmeta, du, hn, dwq, dwkv, dg, dgq, dgkv, *r_out = _bwd_in(
        h, dh2, dq, dkv, dkr, cq, ckv, dpl, dpg, dag, norm_g, win, gq, wq, gkv, wkv, cosf, sinf,
        (wout_s, gwout, m_wout_s, v_wout_s))
    return dict(gx=gx, dmeta=dmeta, du=du, dag=dag, hn=hn, dwq=dwq, dwkv=dwkv, r_out=tuple(r_out), dg=dg, dgq=dgq,
                dgkv=dgkv, dpw=dpw, dps=dps, dgf=dgf, loss=loss)


_CHIP_RELS = ((0, 0), (1, 0), (0, 1), (1, 1))

_ARR_ROWS = (SHARD_IN, SHARD_OUT, 256, KV_LORA, N_META)
_ARR_COLS = (D, D, Q_LORA, 256, 256)
_PIECES = (
    (0, 0, 256, 0), (0, 256, SHARD_IN - 256, 1),
    (1, 0, 128, 0), (1, 128, 128, 1),
    (2, 0, 128, 0), (2, 128, 128, 1),
    (3, 0, 64, 0), (3, 64, 64, 1),
    (4, 0, N_META, 0),
)
_NP = len(_PIECES)
_PIECE_MAX = (256, 128, 128, 64, N_META)


def _gathered_at(refs, arr, chip, r0, n):
    if arr in (0, 1):
        return refs[arr].at[pl.ds(pl.multiple_of(_ARR_ROWS[arr] * chip + r0, 16), n), :]
    return refs[arr].at[chip, pl.ds(r0, n), :]


def _remote(src, dst, send_sem, recv_sem, to):
    return pltpu.make_async_remote_copy(src_ref=src, dst_ref=dst, send_sem=send_sem, recv_sem=recv_sem,
                                        device_id=to, device_id_type=MESH)


def _gather_weights(winT_s, wqT_s, wkv_s, meta_s, x2, tgt2):
    arrays = (0, 2, 3, 4)

    def body(win_ref, wq_ref, wkv_ref, meta_ref, x_ref, t_ref, win_o, wq_o, wkv_o, h_o, tp_o,
             s_win, s_wq, s_wkv, meta_all, head_buf, x_buf, t_buf, ici_send, ici_recv, fwd_send, fwd_recv,
             loc_sems, own_sems):
        x, y, c = lax.axis_index("x"), lax.axis_index("y"), lax.axis_index("c")
        me = 2 * x + y
        stage = (s_win, None, s_wq, s_wkv, meta_ref)
        outs = (win_o, None, wq_o, wkv_o, meta_all)

        _peer_signal(x, y, c)

        frames = pl.ds(HEAD_ROWS, S)
        loads = [pltpu.make_async_copy(x_ref, x_buf, loc_sems.at[0]), pltpu.make_async_copy(t_ref, t_buf, loc_sems.at[1])]
        local = [pltpu.make_async_copy(x_buf, h_o.at[frames, :], loc_sems.at[0]),
                 pltpu.make_async_copy(t_buf, tp_o.at[frames, :], loc_sems.at[1])]
        for cp in loads:
            cp.start()

        s_win[...] = win_ref[...].astype(BF16)
        s_wq[0:QK, :] = wq_ref[...].astype(BF16)
        s_wq[QK:256, :] = jnp.zeros((256 - QK, Q_LORA), BF16)
        s_wkv[...] = wkv_ref[...].astype(BF16)
        head_buf[...] = jnp.zeros_like(head_buf)
        zeros = pltpu.make_async_copy(head_buf, tp_o.at[pl.ds(0, HEAD_ROWS), :], loc_sems.at[2])
        zeros.start()

        def chip_of(rel):
            fx, fy = _CHIP_RELS[rel]
            return 2 * (x ^ fx) + (y ^ fy)

        def same_core_of(rel):
            fx, fy = _CHIP_RELS[rel]
            return (x ^ fx, y ^ fy, c)

        def ici_copy(rel, i, src_chip, to):
            arr, r0, n, _ = _PIECES[i]
            k = (rel - 1) * _NP + i
            return _remote(stage[arr].at[pl.ds(r0, n), :], _gathered_at(outs, arr, src_chip, r0, n),
                           ici_send.at[k], ici_recv.at[k], to)

        def fwd_copy(rel, i, to):
            arr, r0, n, _ = _PIECES[i]
            k = (rel - 1) * _NP + i
            place = _gathered_at(outs, arr, chip_of(rel), r0, n)
            return _remote(place, place, fwd_send.at[k], fwd_recv.at[k], to)

        _peer_wait()
        for core in (0, 1):
            @pl.when(c == core)
            def _(core=core):
                mine = [i for i in range(_NP) if _PIECES[i][3] == core and _PIECES[i][0] in arrays]
                theirs = [i for i in range(_NP) if _PIECES[i][3] != core and _PIECES[i][0] in arrays]
                sends = [ici_copy(rel, i, me, same_core_of(rel)) for rel in (1, 2, 3) for i in mine]
                for cp in sends:
                    cp.start()
                for ld, st in zip(loads, local):
                    ld.wait()
                    st.start()
                own = [pltpu.make_async_copy(stage[arr], _gathered_at(outs, arr, me, 0, _ARR_ROWS[arr]), own_sems.at[arr])
                       for arr in arrays if arr != 4]
                for cp in own:
                    cp.start()
                meta_all[me] = meta_ref[...]
                for rel in (1, 2, 3):
                    for i in mine:
                        ici_copy(rel, i, chip_of(rel), (x, y, c)).wait_recv()
                        fwd = fwd_copy(rel, i, (x, y, 1 - c))
                        fwd.start()
                        sends.append(fwd)
                for rel in (1, 2, 3):
                    for i in theirs:
                        fwd_copy(rel, i, (x, y, c)).wait_recv()
                for cp in sends:
                    cp.wait_send()
                for cp in own:
                    cp.wait()

        zeros.wait()
        for chip in range(CHIPS):
            head_buf[PAD:HEAD_ROWS, chip * 256:(chip + 1) * 256] = meta_all[chip]
        head = pltpu.make_async_copy(head_buf, h_o.at[pl.ds(0, HEAD_ROWS), :], loc_sems.at[2])
        head.start()
        head.wait()
        for cp in local:
            cp.wait()

    vm = pl.BlockSpec(memory_space=pltpu.VMEM)
    hbm = pl.BlockSpec(memory_space=pl.ANY)
    return pl.pallas_call(
        body,
        name="gather_weights",
        in_specs=[vm] * 4 + [hbm] * 2,
        out_specs=[hbm] * 5,
        out_shape=[
            jax.ShapeDtypeStruct((D_IN, D), BF16),
            jax.ShapeDtypeStruct((CHIPS, 256, Q_LORA), BF16), jax.ShapeDtypeStruct((CHIPS, KV_LORA, 256), BF16),
            jax.ShapeDtypeStruct((N, D), F32), jax.ShapeDtypeStruct((N, D), F32),
        ],
        scratch_shapes=[pltpu.VMEM((_ARR_ROWS[a], _ARR_COLS[a]), BF16) for a in (0, 2, 3)]
        + [pltpu.VMEM((CHIPS, N_META, 256), F32), pltpu.VMEM((HEAD_ROWS, D), F32), pltpu.VMEM((S, D), F32),
           pltpu.VMEM((S, D), F32)]
        + [pltpu.SemaphoreType.DMA((3 * _NP,))] * 4 + [pltpu.SemaphoreType.DMA((3,)), pltpu.SemaphoreType.DMA((4,))],
        compiler_params=_cparams(collective_id=0),
    )(winT_s, wqT_s, wkv_s, meta_s, x2, tgt2)


_SM_ROWS = (len(POOL_WINDOWS) * POOL_GROUP, VEC_ROWS)
_SM_COLS = (POOL_GROUP, D)
_SM_PIECES = ((0, 0, 256, 0), (0, 256, 256, 1), (1, 0, VEC_ROWS, 0))
_NSP = len(_SM_PIECES)


def _reduce_grads(du, dag, hn, dwq, dwkv, dmeta4, dpw, dg, dgf, dgq, dgkv, dps, loss):
    arrays = (0, 2, 3, 4)
    loaded = (2, 3, 4)
    blocks = ([(0, 256), (256, 512)], [(512, 768), (768, 1024)], [(1024, 1280), (1280, O_AG), (O_AG, O_AG + 256)],
              [(O_AG + 256, D_IN)])

    def body(du_hbm, dag_hbm, hn_hbm, dwq_ref, dwkv_ref, dmeta_ref, dpw_ref, dg_ref, dgf_ref, dgq_ref, dgkv_ref, dps_ref,
             loss_ref, gwin_o, gwq_o, gwkv_o, gmeta_o, gpw_o, gg_o, ggf_o, ggq_o, ggkv_o, gps_o, gloss_o,
             ow2, ow3, ow4, sb0, sb2, sb3, sb4, st0, st2, st3, st4, rc0, rc2, rc3, rc4,
             vec, sm_sb0, sm_sb1, sm_cs0, sm_cs1, sm_rc0, sm_rc1, vec_fin, du_v, dag_v, hn_v, dwin_buf, own0,
             own_sems, d2d_send, d2d_recv, ici_send, ici_recv, fin_send, fin_recv,
             swap_send, swap_recv, smi_send, smi_recv, smf_send, smf_recv, ld_sems):
        x, y, c = lax.axis_index("x"), lax.axis_index("y"), lax.axis_index("c")
        me = 2 * x + y
        _peer_signal(x, y, c)
        operands = [pltpu.make_async_copy(src, dst, ld_sems.at[t])
                    for t, (src, dst) in enumerate(((du_hbm, du_v), (dag_hbm, dag_v), (hn_hbm, hn_v)))]
        for cp in operands:
            cp.start()
        grads = (None, None, dwq_ref, dwkv_ref, dmeta_ref)
        outs = (gwin_o, None, gwq_o, gwkv_o, gmeta_o)
        own_buf = (None, None, ow2, ow3, ow4)
        sib_buf = (sb0, None, sb2, sb3, sb4)
        stage = (st0, None, st2, st3, st4)
        recv = (rc0, None, rc2, rc3, rc4)
        sm_mine = (dpw_ref, vec)
        sm_sib = (sm_sb0, sm_sb1)
        sm_chip = (sm_cs0, sm_cs1)
        sm_recv = (sm_rc0, sm_rc1)
        sm_out = (gpw_o, vec_fin)
        sibling = (x, y, 1 - c)

        def chip_of(rel):
            fx, fy = _CHIP_RELS[rel]
            return 2 * (x ^ fx) + (y ^ fy)

        def same_core_of(rel):
            fx, fy = _CHIP_RELS[rel]
            return (x ^ fx, y ^ fy, c)

        def slot(bufs, i, idx):
            arr, _, n, _ = _PIECES[i]
            return bufs[arr].at[idx, pl.ds(0, n), :]

        def own_load(rel, i):
            arr, r0, n, _ = _PIECES[i]
            return pltpu.make_async_copy(_gathered_at(grads, arr, chip_of(rel), r0, n), slot(own_buf, i, rel),
                                         own_sems.at[rel * _NP + i])

        def d2d_copy(rel, i):
            arr, r0, n, _ = _PIECES[i]
            k = rel * _NP + i
            return _remote(_gathered_at(grads, arr, chip_of(rel), r0, n), slot(sib_buf, i, rel),
                           d2d_send.at[k], d2d_recv.at[k], sibling)

        def ici_copy(rel, i):
            k = (rel - 1) * _NP + i
            return _remote(slot(stage, i, rel - 1), slot(recv, i, rel - 1), ici_send.at[k], ici_recv.at[k],
                           same_core_of(rel))

        def fin_copy(i):
            arr, r0, n, _ = _PIECES[i]
            place = outs[arr].at[pl.ds(r0, n), :]
            return _remote(place, place, fin_send.at[i], fin_recv.at[i], sibling)

        def sm_ici_copy(rel, j):
            blk, r0, n, _ = _SM_PIECES[j]
            k = (rel - 1) * _NSP + j
            return _remote(sm_chip[blk].at[pl.ds(r0, n), :], sm_recv[blk].at[rel - 1, pl.ds(r0, n), :],
                           smi_send.at[k], smi_recv.at[k], same_core_of(rel))

        def sm_fin_copy(j):
            blk, r0, n, _ = _SM_PIECES[j]
            place = sm_out[blk].at[pl.ds(r0, n), :]
            return _remote(place, place, smf_send.at[j], smf_recv.at[j], sibling)

        vec[...] = jnp.zeros_like(vec)
        vec[0:1, :] = dg_ref[...]
        vec[1:2, :] = dgf_ref[...]
        vec[2:3, V_GQ:V_GQ + Q_LORA] = dgq_ref[...]
        vec[2:3, V_GKV:V_GKV + KV_LORA] = dgkv_ref[...]
        vec[2:3, V_PS:V_PS + D_POOL] = dps_ref[...]
        vec[2:3, V_LOSS:D] = loss_ref[...]
        _peer_wait()
        swaps = [_remote(sm_mine[b], sm_sib[b], swap_send.at[b], swap_recv.at[b], sibling) for b in (0, 1)]
        for cp in swaps:
            cp.start()

        for core in (0, 1):
            @pl.when(c == core)
            def _(core=core):
                mine = [i for i in range(_NP) if _PIECES[i][3] == core and _PIECES[i][0] in loaded]
                theirs = [i for i in range(_NP) if _PIECES[i][3] != core and _PIECES[i][0] in loaded]
                i0 = next(i for i in range(_NP) if _PIECES[i][0] == 0 and _PIECES[i][3] == core)
                j0 = next(i for i in range(_NP) if _PIECES[i][0] == 0 and _PIECES[i][3] != core)
                sm_mine_p = [j for j in range(_NSP) if _SM_PIECES[j][3] == core]
                sm_theirs_p = [j for j in range(_NSP) if _SM_PIECES[j][3] != core]
                sends = list(swaps)

                for rel in (1, 2, 3, 0):
                    for i in theirs:
                        cp = d2d_copy(rel, i)
                        cp.start()
                        sends.append(cp)
                    for i in mine:
                        own_load(rel, i).start()

                def rel_of(chip):
                    flips = chip ^ me
                    return jnp.where(flips == 2, 1, jnp.where(flips == 1, 2, flips))

                def shard_rows(chip, i):
                    return pl.ds(SHARD_IN * chip + _PIECES[i][1], _PIECES[i][2])

                def d2d0(chip, i):
                    rel = rel_of(chip)
                    return _remote(dwin_buf.at[shard_rows(chip, i), :], slot(sib_buf, i, rel),
                                   d2d_send.at[rel * _NP + i], d2d_recv.at[rel * _NP + i], sibling)

                def ici0(chip):
                    slot_idx = jnp.maximum(rel_of(chip) - 1, 0)
                    return _remote(slot(stage, i0, slot_idx), slot(recv, i0, slot_idx), ici_send.at[slot_idx * _NP + i0],
                                   ici_recv.at[slot_idx * _NP + i0], (chip // 2, chip % 2, c))

                def settle(chip):
                    d2d0(chip, i0).wait_recv()
                    total = dwin_buf[shard_rows(chip, i0), :] + slot(sib_buf, i0, rel_of(chip))[...]

                    @pl.when(chip != me)
                    def _():
                        slot(stage, i0, jnp.maximum(rel_of(chip) - 1, 0))[...] = total.astype(BF16)
                        ici0(chip).start()

                    @pl.when(chip == me)
                    def _():
                        own0[0:_PIECES[i0][2], :] = total

                for cp in operands:
                    cp.wait()
                for chip in range(CHIPS):
                    for lo, hi in blocks[chip]:
                        if lo < O_AG:
                            dwin_buf[lo:hi, :] = _tn(du_v[:, lo:lo + 256], hn_v[...])[0:hi - lo, :]
                        else:
                            dwin_buf[lo:hi, :] = _tn(dag_v[:, lo - O_AG:hi - O_AG], hn_v[...])
                    cp = d2d0(chip, j0)
                    cp.start()
                    sends.append(cp)
                    if chip > 0:
                        settle(chip - 1)
                settle(CHIPS - 1)

                for rel in (1, 2, 3):
                    for i in mine:
                        arr, r0, n, _ = _PIECES[i]
                        own_load(rel, i).wait()
                        d2d_copy(rel, i).wait_recv()
                        total = slot(own_buf, i, rel)[...] + slot(sib_buf, i, rel)[...]
                        slot(stage, i, rel - 1)[...] = total.astype(stage[arr].dtype)
                        cp = ici_copy(rel, i)
                        cp.start()
                        sends.append(cp)

                for b in (0, 1):
                    swaps[b].wait_recv()
                    sm_chip[b][...] = sm_mine[b][...] + sm_sib[b][...]
                for rel in (1, 2, 3):
                    for j in sm_mine_p:
                        cp = sm_ici_copy(rel, j)
                        cp.start()
                        sends.append(cp)

                for i in mine:
                    arr, r0, n, _ = _PIECES[i]
                    own_load(0, i).wait()
                    d2d_copy(0, i).wait_recv()
                    total = slot(own_buf, i, 0)[...] + slot(sib_buf, i, 0)[...]
                    for rel in (1, 2, 3):
                        ici_copy(rel, i).wait_recv()
                        total = total + slot(recv, i, rel - 1)[...].astype(F32)
                    outs[arr][pl.ds(r0, n), :] = total
                    cp = fin_copy(i)
                    cp.start()
                    sends.append(cp)
                total = own0[0:_PIECES[i0][2], :]
                for rel in (1, 2, 3):
                    ici_copy(rel, i0).wait_recv()
                    total = total + slot(recv, i0, rel - 1)[...].astype(F32)
                outs[0][pl.ds(_PIECES[i0][1], _PIECES[i0][2]), :] = total
                cp = fin_copy(i0)
                cp.start()
                sends.append(cp)

                for j in sm_mine_p:
                    blk, r0, n, _ = _SM_PIECES[j]
                    for rel in (1, 2, 3):
                        sm_ici_copy(rel, j).wait_recv()
                    total = jnp.zeros((n, _SM_COLS[blk]), F32)
                    for chip in range(CHIPS):
                        flips = chip ^ me
                        rel = jnp.where(flips == 2, 1, jnp.where(flips == 1, 2, flips))
                        theirs_rows = sm_recv[blk][jnp.maximum(rel - 1, 0), pl.ds(r0, n), :]
                        total = total + jnp.where(rel == 0, sm_chip[blk][pl.ds(r0, n), :], theirs_rows)
                    sm_out[blk][pl.ds(r0, n), :] = total
                    cp = sm_fin_copy(j)
                    cp.start()
                    sends.append(cp)

                for i in theirs + [j0]:
                    fin_copy(i).wait_recv()
                for j in sm_theirs_p:
                    sm_fin_copy(j).wait_recv()
                for cp in sends:
                    cp.wait_send()
                for chip in range(CHIPS):
                    @pl.when(chip != me)
                    def _(chip=chip):
                        ici0(chip).wait_send()

        gg_o[...] = vec_fin[0:1, :]
        ggf_o[...] = vec_fin[1:2, :]
        ggq_o[...] = vec_fin[2:3, V_GQ:V_GQ + Q_LORA]
        ggkv_o[...] = vec_fin[2:3, V_GKV:V_GKV + KV_LORA]
        gps_o[...] = vec_fin[2:3, V_PS:V_PS + D_POOL]
        gloss_o[...] = vec_fin[2:3, V_LOSS:D]

    vm = pl.BlockSpec(memory_space=pltpu.VMEM)
    piece_buf = lambda lead, dtype, which=arrays: [
        pltpu.VMEM((lead, _PIECE_MAX[a], _ARR_COLS[a]), F32 if a == 4 else dtype) for a in which]
    sm_buf = lambda *lead: [pltpu.VMEM(lead + (_SM_ROWS[b], _SM_COLS[b]), F32) for b in (0, 1)]
    dma = lambda n: [pltpu.SemaphoreType.DMA((n,))] * 2
    return pl.pallas_call(
        body,
        name="reduce_grads",
        in_specs=[pl.BlockSpec(memory_space=pl.ANY)] * 5 + [vm] * 8,
        out_specs=[vm] * 11,
        out_shape=[jax.ShapeDtypeStruct((_ARR_ROWS[a], _ARR_COLS[a]), F32) for a in arrays]
        + [jax.ShapeDtypeStruct((_SM_ROWS[0], _SM_COLS[0]), F32), jax.ShapeDtypeStruct((1, D), F32),
           jax.ShapeDtypeStruct((1, D), F32), jax.ShapeDtypeStruct((1, Q_LORA), F32),
           jax.ShapeDtypeStruct((1, KV_LORA), F32), jax.ShapeDtypeStruct((1, D_POOL), F32),
           jax.ShapeDtypeStruct((1, 128), F32)],
        scratch_shapes=piece_buf(CHIPS, F32, loaded) + piece_buf(CHIPS, F32) + piece_buf(3, BF16) + piece_buf(3, BF16)
        + [pltpu.VMEM((VEC_ROWS, D), F32)] + sm_buf() + sm_buf() + sm_buf(3) + [pltpu.VMEM((VEC_ROWS, D), F32)]
        + [pltpu.VMEM((N, O_KR_END), BF16), pltpu.VMEM((N, D_POOL), BF16), pltpu.VMEM((N, D), BF16),
           pltpu.VMEM((D_IN, D), F32), pltpu.VMEM((_PIECE_MAX[0], D), F32)]
        + [pltpu.SemaphoreType.DMA((CHIPS * _NP,))]
        + dma(CHIPS * _NP) + dma(3 * _NP) + dma(_NP) + dma(2) + dma(3 * _NSP) + dma(_NSP)
        + [pltpu.SemaphoreType.DMA((3,))],
        compiler_params=_cparams(collective_id=3),
    )(du, dag, hn, dwq, dwkv, dmeta4, dpw, dg, dgf, dgq, dgkv, dps, loss)


def _adamw_math(w, g, m, v):
    m = B1 * m + (1.0 - B1) * g
    v = B2 * v + (1.0 - B2) * (g * g)
    m_hat = m / C1
    v_hat = v / C2
    delta = -LR * (m_hat / (jnp.sqrt(v_hat) + ADAM_EPS) + WD * w)
    return delta, m, v


def _adamw_rows(name, w, g, m, v, block_rows):
    rows, cols = w.shape

    def body(w_ref, g_ref, m_ref, v_ref, go_ref, d_ref, nm_ref, nv_ref):
        g = g_ref[...]
        go_ref[...] = g
        d_ref[...], nm_ref[...], nv_ref[...] = _adamw_math(w_ref[...], g, m_ref[...], v_ref[...])

    spec = pl.BlockSpec((block_rows, cols), lambda i: (i, 0))
    return pl.pallas_call(
        body,
        name=name,
        grid=(rows // block_rows,),
        in_specs=[spec] * 4,
        out_specs=[spec] * 4,
        out_shape=[jax.ShapeDtypeStruct(w.shape, F32)] * 4,
        compiler_params=_cparams(dimension_semantics=("arbitrary",)),
    )(w, g, m, v)


def _adamw_small(groups):
    n = len(groups)

    def body(*refs):
        ins, outs = refs[:4 * n], refs[4 * n:]
        for t in range(n):
            w_ref, g_ref, m_ref, v_ref = ins[4 * t:4 * t + 4]
            g = g_ref[0:w_ref.shape[0], :]
            outs[4 * t][...] = g
            outs[4 * t + 1][...], outs[4 * t + 2][...], outs[4 * t + 3][...] = _adamw_math(
                w_ref[...], g, m_ref[...], v_ref[...])

    vm = pl.BlockSpec(memory_space=pltpu.VMEM)
    flat = [a for grp in groups for a in grp]
    outs = pl.pallas_call(
        body,
        name="adamw_small",
        in_specs=[vm] * (4 * n),
        out_specs=[vm] * (4 * n),
        out_shape=[jax.ShapeDtypeStruct(grp[0].shape, F32) for grp in groups for _ in range(4)],
        compiler_params=_cparams(),
    )(*flat)
    return [tuple(outs[4 * t:4 * t + 4]) for t in range(n)]


def _rope_tables():
    half = QK_ROPE // 2
    f32 = np.float32
    inv_freq = (f32(1.0) / (f32(ROPE_THETA) ** (np.arange(half, dtype=f32) / f32(half)))).astype(f32)
    pos = np.arange(N, dtype=f32) - f32(PAD)
    ang = (pos[:, None] * inv_freq[None, :]).astype(f32)
    cos, sin = np.cos(ang).astype(f32), np.sin(ang).astype(f32)
    zero = np.zeros((N, 128 - QK_ROPE), f32)
    return jnp.asarray(np.concatenate([cos, cos, zero], axis=1)), jnp.asarray(np.concatenate([-sin, sin, zero], axis=1))


def kernel(x, meta_tokens, norm_g, w_in, q_norm_g, w_q_b, kv_norm_g, w_kv_b, pool_w, pool_scale, w_out, final_norm_g, loss_target, m_meta_tokens, m_norm_g, m_w_in, m_q_norm_g, m_w_q_b, m_kv_norm_g, m_w_kv_b, m_pool_w, m_pool_scale, m_w_out, m_final_norm_g, v_meta_tokens, v_norm_g, v_w_in, v_q_norm_g, v_w_q_b, v_kv_norm_g, v_w_kv_b, v_pool_w, v_pool_scale, v_w_out, v_final_norm_g):
    tr = lambda a: a[0].T
    win, wq, wkv, h, tgt = _gather_weights(tr(w_in), tr(w_q_b), w_kv_b[0], meta_tokens, x[0], loss_target[0])
    cosf, sinf = _rope_tables()
    gf = final_norm_g.reshape(1, D)

    part = _local_step(h, tgt, norm_g, win, q_norm_g, wq, kv_norm_g, wkv, pool_w[0], pool_scale, w_out[0], m_w_out[0],
                       v_w_out[0], gf, cosf, sinf)

    pw2 = lambda a: a.reshape(len(POOL_WINDOWS) * POOL_GROUP, POOL_GROUP)
    gwinT, gwqT, gwkv, gmeta, gpw, gg, ggf, ggq, ggkv, gps, gloss = _reduce_grads(
        part["du"], part["dag"], part["hn"], part["dwq"], part["dwkv"], part["dmeta"], pw2(part["dpw"]), part["dg"],
        part["dgf"], part["dgq"], part["dgkv"], part["dps"], part["loss"])

    r_in = _adamw_rows("adamw_w_in", tr(w_in), gwinT, tr(m_w_in), tr(v_w_in), 248)
    r_out = part["r_out"]
    fn2 = lambda a: a.reshape(1, D)
    r_meta, r_norm, r_gq, r_wq, r_gkv, r_wkv, r_pw, r_ps, r_fn = _adamw_small([
        (meta_tokens, gmeta, m_meta_tokens, v_meta_tokens),
        (norm_g, gg, m_norm_g, v_norm_g),
        (q_norm_g, ggq, m_q_norm_g, v_q_norm_g),
        (tr(w_q_b), gwqT, tr(m_w_q_b), tr(v_w_q_b)),
        (kv_norm_g, ggkv, m_kv_norm_g, v_kv_norm_g),
        (w_kv_b[0], gwkv, m_w_kv_b[0], v_w_kv_b[0]),
        (pw2(pool_w), gpw, pw2(m_pool_w), pw2(v_pool_w)),
        (pool_scale, gps, m_pool_scale, v_pool_scale),
        (fn2(final_norm_g), ggf, fn2(m_final_norm_g), fn2(v_final_norm_g)),
    ])
    untr = lambda a: a.T[None]
    pw4 = lambda a: a.reshape(1, len(POOL_WINDOWS), POOL_GROUP, POOL_GROUP)
    per_kind = [[
        r_meta[kind], r_norm[kind], untr(r_in[kind]), r_gq[kind], untr(r_wq[kind]), r_gkv[kind], r_wkv[kind][None],
        pw4(r_pw[kind]), r_ps[kind], r_out[kind][None], r_fn[kind].reshape(D),
    ] for kind in range(4)]
    return (gloss[0, 0], part["gx"][None], *per_kind[0], *per_kind[1], *per_kind[2], *per_kind[3])
```

```python
import jax
import jax.numpy as jnp
import numpy as np
from jax import lax
from jax.experimental import pallas as pl
from jax.experimental.pallas import tpu as pltpu

F32 = jnp.float32
BF16 = jnp.bfloat16

D = 1024
S = 2048
N_META = 16
PAD = 112
HEAD_ROWS = PAD + N_META
N = HEAD_ROWS + S
D_POOL = 512
POOL_WINDOWS = (2, 4, 8, 16)
POOL_GROUP = 128
HALO = 16
HEADS = 4
QK_NOPE = 128
QK_ROPE = 64
QK = QK_NOPE + QK_ROPE
V_HEAD = 128
Q_LORA = 256
KV_LORA = 128
D_IN = 1984
EPS = 1e-6
ROPE_THETA = 10000.0
SCALE = QK ** -0.5
CHIPS = 4

ROWS_FWD = 544
ROWS_MID = 544
ROWS_BWD = 544
TK = 128
TQ = 256
NQ = S // TQ
HEADS_PER_STEP_BWD = 4

O_PI, O_PG, O_CQ, O_CKV, O_KR, O_AG = 0, 512, 1024, 1280, 1408, 1472
O_KR_END = O_KR + 128
SHARD_IN = D_IN // CHIPS
SHARD_OUT = D // CHIPS

LR, B1, B2, ADAM_EPS, WD, STEP = 0.001, 0.9, 0.999, 1e-08, 0.01, 10
C1 = 1.0 - B1**STEP
C2 = 1.0 - B2**STEP

VMEM_LIMIT = 60 * 1024 * 1024
MESH = pl.DeviceIdType.MESH
NEG = -1e30

VEC_ROWS = 8
V_GQ, V_GKV, V_PS, V_LOSS = 0, 256, 384, 896


def _cparams(**kw):
    return pltpu.CompilerParams(vmem_limit_bytes=VMEM_LIMIT, **kw)


def _nt(a, b):
    return lax.dot_general(a, b, (((1,), (1,)), ((), ())), preferred_element_type=F32)


def _tn(a, b):
    return lax.dot_general(a, b, (((0,), (0,)), ((), ())), preferred_element_type=F32)


def _nn(a, b):
    return jnp.dot(a, b, preferred_element_type=F32)


def _swap64(t):
    return pltpu.roll(t, 32, 1) + pltpu.roll(t, 96, 1)


def _sigmoid(x):
    return 1.0 / (1.0 + jnp.exp(-x))


def _low_lanes():
    return (lax.broadcasted_iota(jnp.int32, (1, 128), 1) < QK_ROPE).astype(F32)


def _rows(w, rows):
    return pl.BlockSpec((rows, w), lambda i: (i, 0))


def _const(*shape):
    return pl.BlockSpec(shape, lambda *_: (0,) * len(shape), pipeline_mode=pl.Buffered(1))


STAT_GROUPS = HEADS // HEADS_PER_STEP_BWD


def _stat_slot(head):
    return head // HEADS_PER_STEP_BWD, head % HEADS_PER_STEP_BWD


N_PEERS = 4


def _peer_signal(x, y, c):
    barrier = pltpu.get_barrier_semaphore()
    peers = [(x, y, 1 - c)] + [(x ^ fx, y ^ fy, c) for fx, fy in _CHIP_RELS[1:]]
    assert len(peers) == N_PEERS
    for peer in peers:
        pl.semaphore_signal(barrier, inc=1, device_id=peer, device_id_type=MESH)


def _peer_wait():
    pl.semaphore_wait(pltpu.get_barrier_semaphore(), N_PEERS)


def _attn_tiles():
    return [(0, TK, TK)] + [(TK + TQ * t, TQ, TK + TQ * (t + 1)) for t in range(NQ)]


def _masked_scores(q, k, rows, klen):
    s = _nt(q, k)
    col = lax.broadcasted_iota(jnp.int32, (1, TK), 1)
    head_bias = jnp.where(col >= PAD, 0.0, NEG)
    if klen == TK:
        return s + head_bias
    r = lax.broadcasted_iota(jnp.int32, (rows, 1), 0) >> 6
    c = lax.broadcasted_iota(jnp.int32, (1, rows), 1) >> 6
    diag_bias = jnp.where(c <= r, 0.0, NEG)
    parts = [s[:, 0:TK] + head_bias]
    if klen - rows > TK:
        parts.append(s[:, TK:klen - rows])
    parts.append(s[:, klen - rows:klen] + diag_bias)
    return jnp.concatenate(parts, axis=1)


def _fwd_in(h, norm_g, win, gq, wq, gkv, wkv, cosf, sinf):
    tr = ROWS_FWD

    def body(h_ref, g_ref, win_ref, gq_ref, wq_ref, gkv_ref, wkv_ref, cos_ref, sin_ref,
             pi_ref, pg_ref, cq_ref, ckv_ref, ag_ref, q_ref, k_ref, v_ref):
        h = h_ref[...]
        r = lax.rsqrt(jnp.mean(h * h, axis=-1, keepdims=True) + EPS)
        hn = ((h * r) * g_ref[...]).astype(BF16)
        u = _nt(hn, win_ref[0:O_KR_END, :])
        pi_ref[...] = u[:, O_PI:O_PG]
        pg_ref[...] = u[:, O_PG:O_CQ]
        cq = u[:, O_CQ:O_CKV]
        ckv = u[:, O_CKV:O_KR]
        cq_ref[...] = cq
        ckv_ref[...] = ckv
        ag_ref[...] = _nt(hn, win_ref[O_AG:D_IN, :])
        cosv = cos_ref[...]
        sinv = sin_ref[...]
        kr = u[:, O_KR:O_KR_END] * _low_lanes()
        kr = (kr * cosv + _swap64(kr) * sinv).astype(BF16)
        rq = lax.rsqrt(jnp.mean(cq * cq, axis=-1, keepdims=True) + EPS)
        cqn = ((cq * rq) * gq_ref[...]).astype(BF16)
        rkv = lax.rsqrt(jnp.mean(ckv * ckv, axis=-1, keepdims=True) + EPS)
        ckvn = ((ckv * rkv) * gkv_ref[...]).astype(BF16)
        for hd in range(HEADS):
            qh = _nt(cqn, wq_ref[hd]) * SCALE
            z = qh[:, QK_NOPE:]
            q_ref[hd, :, 0:QK_NOPE] = qh[:, 0:QK_NOPE].astype(BF16)
            q_ref[hd, :, QK_NOPE:] = (z * cosv + _swap64(z) * sinv).astype(BF16)
            kvh = _nn(ckvn, wkv_ref[hd])
            k_ref[hd, :, 0:QK_NOPE] = kvh[:, 0:QK_NOPE].astype(BF16)
            k_ref[hd, :, QK_NOPE:] = kr
            v_ref[hd] = kvh[:, QK_NOPE:].astype(BF16)

    head = lambda w: pl.BlockSpec((HEADS, tr, w), lambda i: (0, i, 0))
    return pl.pallas_call(
        body,
        name="fwd_in",
        grid=(N // tr,),
        in_specs=[
            _rows(D, tr), _const(1, D), _const(D_IN, D), _const(1, Q_LORA), _const(HEADS, 256, Q_LORA),
            _const(1, KV_LORA), _const(HEADS, KV_LORA, 256), _rows(128, tr), _rows(128, tr),
        ],
        out_specs=[_rows(D_POOL, tr), _rows(D_POOL, tr), _rows(Q_LORA, tr), _rows(KV_LORA, tr), _rows(D_POOL, tr),
                   head(256), head(256), head(V_HEAD)],
        out_shape=[
            jax.ShapeDtypeStruct((N, D_POOL), F32), jax.ShapeDtypeStruct((N, D_POOL), F32),
            jax.ShapeDtypeStruct((N, Q_LORA), F32), jax.ShapeDtypeStruct((N, KV_LORA), F32),
            jax.ShapeDtypeStruct((N, D_POOL), F32),
            jax.ShapeDtypeStruct((HEADS, N, 256), BF16), jax.ShapeDtypeStruct((HEADS, N, 256), BF16),
            jax.ShapeDtypeStruct((HEADS, N, V_HEAD), BF16),
        ],
        compiler_params=_cparams(dimension_semantics=("arbitrary",)),
    )(h, norm_g, win, gq, wq, gkv, wkv, cosf, sinf)


def _attn_fwd(q, k, v, wout_s):
    tiles = _attn_tiles()
    n_t = len(tiles)
    half = SHARD_OUT // 2
    send_step = 2
    fwd_step = n_t - 2

    def body(q_hbm, k_hbm, v_hbm, wout_ref, o_hbm, lse_ref, wout_o, q_buf, k_buf, v_buf, o_buf, s_wout, in_sems, out_sems,
             ici_send, ici_recv, fwd_send, fwd_recv, own_sem):
        step = pl.program_id(0)
        x, y, c = lax.axis_index("x"), lax.axis_index("y"), lax.axis_index("c")
        me = 2 * x + y

        def chip_of(rel):
            fx, fy = _CHIP_RELS[rel]
            return 2 * (x ^ fx) + (y ^ fy)

        def place(chip, core):
            return wout_o.at[pl.ds(pl.multiple_of(SHARD_OUT * chip + half * core, half), half), :]

        def ici_copy(rel, src_chip, to):
            return _remote(s_wout.at[pl.ds(pl.multiple_of(half * c, half), half), :], place(src_chip, c),
                           ici_send.at[rel - 1], ici_recv.at[rel - 1], to)

        def fwd_copy(rel, core, to):
            spot = place(chip_of(rel), core)
            return _remote(spot, spot, fwd_send.at[rel - 1], fwd_recv.at[rel - 1], to)

        own = pltpu.make_async_copy(s_wout, wout_o.at[pl.ds(pl.multiple_of(SHARD_OUT * me, SHARD_OUT), SHARD_OUT), :], own_sem)

        @pl.when(step == 0)
        def _():
            _peer_signal(x, y, c)
            s_wout[...] = wout_ref[...].astype(BF16)
            own.start()

        @pl.when(step == send_step)
        def _():
            _peer_wait()
            for rel in (1, 2, 3):
                fx, fy = _CHIP_RELS[rel]
                ici_copy(rel, me, (x ^ fx, y ^ fy, c)).start()

        @pl.when(step == fwd_step)
        def _():
            for rel in (1, 2, 3):
                ici_copy(rel, chip_of(rel), (x, y, c)).wait_recv()
                fwd_copy(rel, c, (x, y, 1 - c)).start()

        def finish_wout():
            for rel in (1, 2, 3):
                fwd_copy(rel, 1 - c, (x, y, c)).wait_recv()
            for rel in (1, 2, 3):
                ici_copy(rel, me, (x, y, c)).wait_send()
                fwd_copy(rel, c, (x, y, c)).wait_send()
            own.wait()

        def loads(idx):
            q0, rows, _ = tiles[idx]
            rs = pl.ds(q0, rows)
            return [pltpu.make_async_copy(src.at[:, rs, :], dst.at[:, rs, :], in_sems.at[a, idx % 2])
                    for a, (src, dst) in enumerate(((q_hbm, q_buf), (k_hbm, k_buf), (v_hbm, v_buf)))]

        def store(idx):
            q0, rows, _ = tiles[idx]
            return pltpu.make_async_copy(o_buf.at[idx % 2, pl.ds(0, rows), :], o_hbm.at[pl.ds(q0, rows), :],
                                         out_sems.at[idx % 2])

        @pl.when(step == 0)
        def _():
            lse_ref[...] = jnp.zeros_like(lse_ref)
            for cp in loads(0):
                cp.start()

        for idx, (q0, rows, klen) in enumerate(tiles):
            @pl.when(step == idx)
            def _(idx=idx, q0=q0, rows=rows, klen=klen):
                for cp in loads(idx):
                    cp.wait()
                if idx + 1 < n_t:
                    for cp in loads(idx + 1):
                        cp.start()
                if idx >= 2:
                    store(idx - 2).wait()
                for hd in range(HEADS):
                    s = _masked_scores(q_buf[hd, q0:q0 + rows, :], k_buf[hd, 0:klen, :], rows, klen)
                    m = jnp.max(s, axis=-1, keepdims=True)
                    p = jnp.exp(s - m)
                    l = jnp.sum(p, axis=-1, keepdims=True)
                    o_buf[idx % 2, 0:rows, hd * V_HEAD:(hd + 1) * V_HEAD] = _nn(p.astype(BF16), v_buf[hd, 0:klen, :]) / l
                    grp, lane = _stat_slot(hd)
                    lse_ref[grp, q0:q0 + rows, lane:lane + 1] = m + jnp.log(l)
                store(idx).start()
                if idx == n_t - 1:
                    store(idx - 1).wait()
                    store(idx).wait()
                    finish_wout()

    hbm = pl.BlockSpec(memory_space=pl.ANY)
    return pl.pallas_call(
        body,
        name="attn_fwd",
        grid=(n_t,),
        in_specs=[hbm, hbm, hbm, _const(SHARD_OUT, D)],
        out_specs=[hbm, _const(STAT_GROUPS, N, 128), hbm],
        out_shape=[jax.ShapeDtypeStruct((N, HEADS * V_HEAD), F32), jax.ShapeDtypeStruct((STAT_GROUPS, N, 128), F32),
                   jax.ShapeDtypeStruct((D, D), BF16)],
        scratch_shapes=[pltpu.VMEM((HEADS, N, 256), BF16), pltpu.VMEM((HEADS, N, 256), BF16),
                        pltpu.VMEM((HEADS, N, V_HEAD), BF16), pltpu.VMEM((2, TQ, HEADS * V_HEAD), F32),
                        pltpu.VMEM((SHARD_OUT, D), BF16),
                        pltpu.SemaphoreType.DMA((3, 2)), pltpu.SemaphoreType.DMA((2,))]
        + [pltpu.SemaphoreType.DMA((3,))] * 4 + [pltpu.SemaphoreType.DMA],
        compiler_params=_cparams(dimension_semantics=("arbitrary",), collective_id=1),
    )(q, k, v, wout_s)


def _inv_count(row0, rows, w):
    row = row0 + lax.broadcasted_iota(jnp.int32, (rows, 1), 0)
    return 1.0 / jnp.clip(row - (PAD - 1), 1, w).astype(F32)


def _mid(h, tgt, pool_in, pool_gate, attn_gate, attn, pool_w, pool_scale, wout, gf):
    tr = ROWS_MID
    per = tr // HALO
    ng = len(POOL_WINDOWS)

    def body(h_ref, t_ref, pin_ref, halo_ref, pg_ref, ag_ref, at_ref, pw_ref, ps_ref, wout_ref, gf_ref,
             dh2_ref, do_ref, delta_ref, dag_ref, dpg_ref, dpl_ref, dwout_ref, dpw_ref, dps_ref, dgf_ref, loss_ref):
        i = pl.program_id(0)

        @pl.when(i == 0)
        def _():
            dwout_ref[...] = jnp.zeros_like(dwout_ref)
            dpw_ref[...] = jnp.zeros_like(dpw_ref)
            dps_ref[...] = jnp.zeros_like(dps_ref)
            dgf_ref[...] = jnp.zeros_like(dgf_ref)
            loss_ref[...] = jnp.zeros_like(loss_ref)

        row0 = i * tr
        real = (row0 + lax.broadcasted_iota(jnp.int32, (tr, 1), 0)) >= HEAD_ROWS
        h = h_ref[...]

        halo = jnp.where(i > 0, halo_ref[...], 0.0)
        ext = jnp.concatenate([halo, pin_ref[...]], axis=0)
        pooled = []
        for g, w in enumerate(POOL_WINDOWS):
            e = ext[:, g * POOL_GROUP:(g + 1) * POOL_GROUP]
            acc = e
            shift = 1
            while shift < w:
                acc = acc + pltpu.roll(acc, shift, 0)
                shift *= 2
            pooled.append((acc[HALO:] * _inv_count(row0, tr, w) - e[HALO:]).astype(BF16))
        pw = [pw_ref[g].astype(BF16) for g in range(ng)]
        mixed = jnp.concatenate([_nn(pooled[g], pw[g]) for g in range(ng)], axis=1)
        ps = ps_ref[...]
        mixed_s = mixed * ps
        pg = pg_ref[...]
        sig_p = _sigmoid(pg)
        silu_p = pg * sig_p
        pool_out = (silu_p * mixed_s).astype(BF16)
        ag = ag_ref[...]
        sig_a = _sigmoid(ag)
        silu_a = ag * sig_a
        at = at_ref[...]
        attn_out = (silu_a * at).astype(BF16)
        cat = jnp.concatenate([pool_out, attn_out], axis=1)
        h2 = h + _nn(cat, wout_ref[...])

        r2 = lax.rsqrt(jnp.mean(h2 * h2, axis=-1, keepdims=True) + EPS)
        n2 = h2 * r2
        gfv = gf_ref[...]
        err = jnp.where(real, n2 * gfv - t_ref[...], 0.0)
        loss_ref[...] += jnp.sum(jnp.sum(err * err, axis=-1, keepdims=True), axis=0, keepdims=True) * (0.5 / D)
        dy = err * (1.0 / D)
        dgf_ref[...] += jnp.sum(dy * n2, axis=0, keepdims=True)
        dn = dy * gfv
        dh2 = r2 * (dn - n2 * jnp.mean(dn * n2, axis=-1, keepdims=True))
        dh2_ref[...] = dh2
        dh2b = dh2.astype(BF16)

        dwout_ref[...] += _tn(cat, dh2b)
        dcat = _nt(dh2b, wout_ref[...])
        dpo = dcat[:, 0:D_POOL]
        dao = dcat[:, D_POOL:D]
        do = dao * silu_a
        prod = do * at
        delta_ref[...] = jnp.zeros_like(delta_ref)
        for hd in range(HEADS):
            grp, lane = _stat_slot(hd)
            cols = slice(hd * V_HEAD, (hd + 1) * V_HEAD)
            do_ref[grp, :, lane * V_HEAD:(lane + 1) * V_HEAD] = do[:, cols].astype(BF16)
            delta_ref[grp, :, lane:lane + 1] = jnp.sum(prod[:, cols], axis=-1, keepdims=True)
        dag_ref[...] = (dao * at * (sig_a * (1.0 + ag * (1.0 - sig_a)))).astype(BF16)
        dmixed_s = dpo * silu_p
        dpg_ref[...] = (dpo * mixed_s * (sig_p * (1.0 + pg * (1.0 - sig_p)))).astype(BF16)
        dps_ref[...] += jnp.sum(dmixed_s * mixed, axis=0, keepdims=True)
        dmixed = (dmixed_s * ps).astype(BF16)
        dpl = []
        for g in range(ng):
            dm = dmixed[:, g * POOL_GROUP:(g + 1) * POOL_GROUP]
            dpl.append(_nt(dm, pw[g]))
            dpw_ref[g] += _tn(pooled[g], dm)
        dpl_ref[...] = jnp.concatenate(dpl, axis=1)

    halo_spec = pl.BlockSpec((HALO, D_POOL), lambda i: (jnp.maximum(i * per - 1, 0), 0))
    return pl.pallas_call(
        body,
        name="mid",
        grid=(N // tr,),
        in_specs=[
            _rows(D, tr), _rows(D, tr), _rows(D_POOL, tr), halo_spec, _rows(D_POOL, tr), _rows(D_POOL, tr),
            _rows(D_POOL, tr), _const(ng, POOL_GROUP, POOL_GROUP), _const(1, D_POOL), _const(D, D), _const(1, D),
        ],
        out_specs=[
            _rows(D, tr), pl.BlockSpec((STAT_GROUPS, tr, HEADS_PER_STEP_BWD * V_HEAD), lambda i: (0, i, 0)),
            pl.BlockSpec((STAT_GROUPS, tr, 128), lambda i: (0, i, 0)),
            _rows(D_POOL, tr), _rows(D_POOL, tr), _rows(D_POOL, tr),
            _const(D, D), _const(ng, POOL_GROUP, POOL_GROUP), _const(1, D_POOL), _const(1, D), _const(1, 128),
        ],
        out_shape=[
            jax.ShapeDtypeStruct((N, D), F32), jax.ShapeDtypeStruct((STAT_GROUPS, N, HEADS_PER_STEP_BWD * V_HEAD), BF16),
            jax.ShapeDtypeStruct((STAT_GROUPS, N, 128), F32),
            jax.ShapeDtypeStruct((N, D_POOL), BF16), jax.ShapeDtypeStruct((N, D_POOL), BF16),
            jax.ShapeDtypeStruct((N, D_POOL), F32), jax.ShapeDtypeStruct((D, D), F32),
            jax.ShapeDtypeStruct((ng, POOL_GROUP, POOL_GROUP), F32),
            jax.ShapeDtypeStruct((1, D_POOL), F32), jax.ShapeDtypeStruct((1, D), F32), jax.ShapeDtypeStruct((1, 128), F32),
        ],
        compiler_params=_cparams(dimension_semantics=("arbitrary",)),
    )(h, tgt, pool_in, pool_in, pool_gate, attn_gate, attn, pool_w, pool_scale, wout, gf)


def _unrope(dy, cosv, sinv):
    return dy * cosv + _swap64(dy * sinv) * _low_lanes()


def _attn_bwd(q, k, v, do, lse, delta, cosf, sinf, dwout):
    tiles = _attn_tiles()
    hp = HEADS_PER_STEP_BWD
    n_g = HEADS // hp
    n_t = len(tiles)
    half = SHARD_OUT // 2
    swap_at, send_at, sum_at = (0, 2), (0, 3), (n_g - 1, n_t - 3)

    def body(q_hbm, k_hbm, v_hbm, do_hbm, lse_ref, delta_ref, cos_ref, sin_ref, dwout_hbm, dq_hbm, dkv_ref, dkr_ref,
             gwout_ref, q_buf, k_buf, v_buf, do_buf, dq_buf, dk_acc, dv_acc, own_w, sib_w, stage_w, recv_w, gw_buf,
             in_sems, out_sems, ow_sems, d2d_send, d2d_recv, ici_send, ici_recv, fin_send, fin_recv):
        grp = pl.program_id(0)
        step = pl.program_id(1)
        heads = pl.ds(grp * hp, hp)
        x, y, c = lax.axis_index("x"), lax.axis_index("y"), lax.axis_index("c")
        sibling = (x, y, 1 - c)

        def chip_of(rel):
            fx, fy = _CHIP_RELS[rel]
            return 2 * (x ^ fx) + (y ^ fy)

        def piece(chip, core):
            return dwout_hbm.at[pl.ds(pl.multiple_of(SHARD_OUT * chip + half * core, half), half), :]

        def own_load(rel):
            return pltpu.make_async_copy(piece(chip_of(rel), c), own_w.at[rel], ow_sems.at[rel])

        def d2d_copy(rel):
            return _remote(piece(chip_of(rel), 1 - c), sib_w.at[rel], d2d_send.at[rel], d2d_recv.at[rel], sibling)

        def ici_copy(rel):
            fx, fy = _CHIP_RELS[rel]
            return _remote(stage_w.at[rel - 1], recv_w.at[rel - 1], ici_send.at[rel - 1], ici_recv.at[rel - 1],
                           (x ^ fx, y ^ fy, c))

        def fin_copy(core):
            spot = gw_buf.at[pl.ds(pl.multiple_of(half * core, half), half), :]
            return _remote(spot, spot, fin_send.at[0], fin_recv.at[0], sibling)

        @pl.when((grp == 0) & (step == 0))
        def _():
            _peer_signal(x, y, c)
            for rel in (1, 2, 3, 0):
                own_load(rel).start()

        @pl.when((grp == swap_at[0]) & (step == swap_at[1]))
        def _():
            _peer_wait()
            for rel in (1, 2, 3, 0):
                d2d_copy(rel).start()

        @pl.when((grp == send_at[0]) & (step == send_at[1]))
        def _():
            for rel in (1, 2, 3):
                own_load(rel).wait()
                d2d_copy(rel).wait_recv()
                stage_w[rel - 1] = (own_w[rel] + sib_w[rel]).astype(BF16)
                ici_copy(rel).start()

        @pl.when((grp == sum_at[0]) & (step == sum_at[1]))
        def _():
            own_load(0).wait()
            d2d_copy(0).wait_recv()
            total = own_w[0] + sib_w[0]
            for rel in (1, 2, 3):
                ici_copy(rel).wait_recv()
                total = total + recv_w[rel - 1].astype(F32)
            gw_buf[pl.ds(pl.multiple_of(half * c, half), half), :] = total
            fin_copy(c).start()

        def finish_dwout():
            fin_copy(1 - c).wait_recv()
            for rel in (0, 1, 2, 3):
                d2d_copy(rel).wait_send()
            for rel in (1, 2, 3):
                ici_copy(rel).wait_send()
            fin_copy(c).wait_send()
            gwout_ref[...] = gw_buf[...]

        def loads(g, idx):
            q0, rows, _ = tiles[idx]
            rs = pl.ds(q0, rows)
            par = (g * n_t + idx) % 2
            hs = pl.ds(g * hp, hp)
            pairs = ((q_hbm.at[hs, rs, :], q_buf.at[:, rs, :]), (k_hbm.at[hs, rs, :], k_buf.at[:, rs, :]),
                     (v_hbm.at[hs, rs, :], v_buf.at[:, rs, :]), (do_hbm.at[g, rs, :], do_buf.at[rs, :]))
            return [pltpu.make_async_copy(src, dst, in_sems.at[a, par]) for a, (src, dst) in enumerate(pairs)]

        def store(idx):
            q0, rows, _ = tiles[idx]
            return pltpu.make_async_copy(dq_buf.at[idx % 2, :, pl.ds(0, rows), :], dq_hbm.at[heads, pl.ds(q0, rows), :],
                                         out_sems.at[idx % 2])

        @pl.when(step == 0)
        def _():
            dk_acc[...] = jnp.zeros_like(dk_acc)
            dv_acc[...] = jnp.zeros_like(dv_acc)

        @pl.when((step == 0) & (grp == 0))
        def _():
            dkr_ref[...] = jnp.zeros_like(dkr_ref)
            for cp in loads(grp, 0):
                cp.start()

        for idx, (q0, rows, klen) in enumerate(tiles):
            @pl.when(step == idx)
            def _(idx=idx, q0=q0, rows=rows, klen=klen):
                for cp in loads(grp, idx):
                    cp.wait()
                if idx + 1 < n_t:
                    for cp in loads(grp, idx + 1):
                        cp.start()
                if idx >= 2:
                    store(idx - 2).wait()
                qs = pl.ds(q0, rows)
                for hd in range(hp):
                    qv = q_buf[hd, qs, :]
                    kv = k_buf[hd, 0:klen, :]
                    p = jnp.exp(_masked_scores(qv, kv, rows, klen) - lse_ref[0, qs, hd:hd + 1])
                    dob = do_buf[qs, hd * V_HEAD:(hd + 1) * V_HEAD]
                    ds = (p * (_nt(dob, v_buf[hd, 0:klen, :]) - delta_ref[0, qs, hd:hd + 1])).astype(BF16)
                    dq = _nn(ds, kv) * SCALE
                    dq_buf[idx % 2, hd, 0:rows, 0:QK_NOPE] = dq[:, 0:QK_NOPE].astype(BF16)
                    dq_buf[idx % 2, hd, 0:rows, QK_NOPE:] = _unrope(dq[:, QK_NOPE:], cos_ref[qs, :], sin_ref[qs, :]).astype(BF16)
                    dk_acc[hd, 0:klen, :] += _tn(ds, qv)
                    dv_acc[hd, 0:klen, :] += _tn(p.astype(BF16), dob)
                store(idx).start()

        @pl.when(step == n_t - 1)
        def _():
            @pl.when(grp + 1 < n_g)
            def _():
                for cp in loads(grp + 1, 0):
                    cp.start()

            for hd in range(hp):
                dkv_ref[hd, :, 0:QK_NOPE] = dk_acc[hd, :, 0:QK_NOPE].astype(BF16)
                dkv_ref[hd, :, QK_NOPE:] = dv_acc[hd].astype(BF16)
                dkr_ref[...] += dk_acc[hd, :, QK_NOPE:]
            store(n_t - 2).wait()
            store(n_t - 1).wait()

            @pl.when(grp == n_g - 1)
            def _():
                finish_dwout()

    hbm = pl.BlockSpec(memory_space=pl.ANY)
    stat = pl.BlockSpec((1, N, 128), lambda g, t: (g, 0, 0), pipeline_mode=pl.Buffered(1))
    piece_f32 = lambda lead: pltpu.VMEM((lead, half, D), F32)
    piece_bf16 = lambda lead: pltpu.VMEM((lead, half, D), BF16)
    return pl.pallas_call(
        body,
        name="attn_bwd",
        grid=(n_g, n_t),
        in_specs=[hbm, hbm, hbm, hbm, stat, stat, _const(N, 128), _const(N, 128), hbm],
        out_specs=[hbm, pl.BlockSpec((hp, N, 256), lambda g, t: (g, 0, 0), pipeline_mode=pl.Buffered(1)), _const(N, 128),
                   _const(SHARD_OUT, D)],
        out_shape=[
            jax.ShapeDtypeStruct((HEADS, N, 256), BF16), jax.ShapeDtypeStruct((HEADS, N, 256), BF16),
            jax.ShapeDtypeStruct((N, 128), F32), jax.ShapeDtypeStruct((SHARD_OUT, D), F32),
        ],
        scratch_shapes=[pltpu.VMEM((hp, N, 256), BF16), pltpu.VMEM((hp, N, 256), BF16), pltpu.VMEM((hp, N, V_HEAD), BF16),
                        pltpu.VMEM((N, hp * V_HEAD), BF16), pltpu.VMEM((2, hp, TQ, 256), BF16),
                        pltpu.VMEM((hp, N, 256), F32), pltpu.VMEM((hp, N, V_HEAD), F32),
                        piece_f32(CHIPS), piece_f32(CHIPS), piece_bf16(3), piece_bf16(3), pltpu.VMEM((SHARD_OUT, D), F32),
                        pltpu.SemaphoreType.DMA((4, 2)), pltpu.SemaphoreType.DMA((2,)), pltpu.SemaphoreType.DMA((CHIPS,)),
                        pltpu.SemaphoreType.DMA((CHIPS,)), pltpu.SemaphoreType.DMA((CHIPS,)),
                        pltpu.SemaphoreType.DMA((3,)), pltpu.SemaphoreType.DMA((3,)),
                        pltpu.SemaphoreType.DMA((1,)), pltpu.SemaphoreType.DMA((1,))],
        compiler_params=_cparams(dimension_semantics=("arbitrary", "arbitrary"), collective_id=2),
    )(q, k, v, do, lse, delta, cosf, sinf, dwout)


def _bwd_in(h, dh2, dq, dkv, dkr, cq, ckv, dpl, dpg, dag, norm_g, win, gq, wq, gkv, wkv, cosf, sinf, adam_out):
    tr = ROWS_BWD
    nb = N // tr
    per = tr // HALO
    lead = HEAD_ROWS
    adam_rows = SHARD_OUT // nb

    def body(h_ref, dh2_ref, dq_ref, dkv_ref, dkr_ref, cq_ref, ckv_ref, dpl_ref, halo_ref, dpg_ref, dag_ref,
             g_ref, win_ref, gq_ref, wq_ref, gkv_ref, wkv_ref, cos_ref, sin_ref, aw_ref, ag_ref, am_ref, av_ref,
             gx_ref, dmeta_ref, du_ref, hn_ref, dwq_ref, dwkv_ref, dg_ref, dgq_ref, dgkv_ref, ago_ref, ad_ref, anm_ref, anv_ref,
             dh_buf, gx_sem):
        i = pl.program_id(0)
        grad_out = ag_ref[...]
        ago_ref[...] = grad_out
        ad_ref[...], anm_ref[...], anv_ref[...] = _adamw_math(aw_ref[...], grad_out, am_ref[...], av_ref[...])

        @pl.when(i == 0)
        def _():
            dwq_ref[...] = jnp.zeros_like(dwq_ref)
            dwkv_ref[...] = jnp.zeros_like(dwkv_ref)
            dg_ref[...] = jnp.zeros_like(dg_ref)
            dgq_ref[...] = jnp.zeros_like(dgq_ref)
            dgkv_ref[...] = jnp.zeros_like(dgkv_ref)

        row0 = i * tr
        h = h_ref[...]
        r = lax.rsqrt(jnp.mean(h * h, axis=-1, keepdims=True) + EPS)
        n = h * r
        gv = g_ref[...]
        hn = (n * gv).astype(BF16)
        cq = cq_ref[...]
        rq = lax.rsqrt(jnp.mean(cq * cq, axis=-1, keepdims=True) + EPS)
        nq = cq * rq
        gqv = gq_ref[...]
        cqn = (nq * gqv).astype(BF16)
        dcqn = jnp.zeros((tr, Q_LORA), F32)
        for hd in range(HEADS):
            dqf = dq_ref[hd]
            dcqn = dcqn + _nn(dqf, wq_ref[hd])
            dwq_ref[hd] += _tn(dqf, cqn)
        dgq_ref[...] += jnp.sum(dcqn * nq, axis=0, keepdims=True)
        dnq = dcqn * gqv
        dcq = rq * (dnq - nq * jnp.mean(dnq * nq, axis=-1, keepdims=True))

        ckv = ckv_ref[...]
        rkv = lax.rsqrt(jnp.mean(ckv * ckv, axis=-1, keepdims=True) + EPS)
        nkv = ckv * rkv
        gkvv = gkv_ref[...]
        ckvn = (nkv * gkvv).astype(BF16)
        dckvn = jnp.zeros((tr, KV_LORA), F32)
        for hd in range(HEADS):
            dkv = dkv_ref[hd]
            dckvn = dckvn + _nt(dkv, wkv_ref[hd])
            dwkv_ref[hd] += _tn(ckvn, dkv)
        dgkv_ref[...] += jnp.sum(dckvn * nkv, axis=0, keepdims=True)
        dnkv = dckvn * gkvv
        dckv = rkv * (dnkv - nkv * jnp.mean(dnkv * nkv, axis=-1, keepdims=True))
        dkr = _unrope(dkr_ref[...], cos_ref[...], sin_ref[...])

        cur = dpl_ref[...]
        halo = jnp.where(i < nb - 1, halo_ref[...], 0.0)
        dpi = []
        for g, w in enumerate(POOL_WINDOWS):
            sl = slice(g * POOL_GROUP, (g + 1) * POOL_GROUP)
            a = jnp.concatenate([cur[:, sl] * _inv_count(row0, tr, w), halo[:, sl] * _inv_count(row0 + tr, HALO, w)], axis=0)
            acc = a
            shift = 1
            while shift < w:
                acc = acc + pltpu.roll(acc, tr + HALO - shift, 0)
                shift *= 2
            dpi.append(acc[0:tr] - cur[:, sl])

        du = jnp.concatenate([t.astype(BF16) for t in dpi] + [dpg_ref[...]] + [t.astype(BF16) for t in (dcq, dckv, dkr)],
                             axis=1)
        dagb = dag_ref[...]
        du_ref[...] = du
        hn_ref[...] = hn
        dhn = _nn(du, win_ref[0:O_KR_END, :]) + _nn(dagb, win_ref[O_AG:D_IN, :])
        dg_ref[...] += jnp.sum(dhn * n, axis=0, keepdims=True)
        dn = dhn * gv
        dh = dh2_ref[...] + r * (dn - n * jnp.mean(dn * n, axis=-1, keepdims=True))

        first = pltpu.make_async_copy(dh_buf.at[pl.ds(lead, tr - lead), :], gx_ref.at[pl.ds(0, tr - lead), :], gx_sem)
        later = lambda step: pltpu.make_async_copy(
            dh_buf, gx_ref.at[pl.ds(pl.multiple_of(step * tr - lead, 16), tr), :], gx_sem)

        @pl.when(i == 1)
        def _():
            first.wait()

        @pl.when(i > 1)
        def _():
            later(i - 1).wait()

        dh_buf[...] = dh

        @pl.when(i == 0)
        def _():
            first.start()
            for chip in range(CHIPS):
                dmeta_ref[chip] = dh[PAD:HEAD_ROWS, chip * 256:(chip + 1) * 256]

        @pl.when(i > 0)
        def _():
            later(i).start()

        @pl.when(i == nb - 1)
        def _():
            later(i).wait()

    head = lambda w: pl.BlockSpec((HEADS, tr, w), lambda i: (0, i, 0))
    halo_spec = pl.BlockSpec((HALO, D_POOL), lambda i: (jnp.minimum((i + 1) * per, N // HALO - 1), 0))
    return pl.pallas_call(
        body,
        name="bwd_in",
        grid=(nb,),
        in_specs=[
            _rows(D, tr), _rows(D, tr), head(256), head(256), _rows(128, tr), _rows(Q_LORA, tr), _rows(KV_LORA, tr),
            _rows(D_POOL, tr), halo_spec, _rows(D_POOL, tr), _rows(D_POOL, tr),
            _const(1, D), _const(D_IN, D), _const(1, Q_LORA), _const(HEADS, 256, Q_LORA),
            _const(1, KV_LORA), _const(HEADS, KV_LORA, 256), _rows(128, tr), _rows(128, tr),
        ] + [_rows(D, adam_rows)] * 4,
        out_specs=[
            pl.BlockSpec(memory_space=pl.ANY), _const(CHIPS, N_META, 256), _rows(O_KR_END, tr), _rows(D, tr),
            _const(HEADS, 256, Q_LORA),
            _const(HEADS, KV_LORA, 256), _const(1, D), _const(1, Q_LORA), _const(1, KV_LORA),
        ] + [_rows(D, adam_rows)] * 4,
        out_shape=[
            jax.ShapeDtypeStruct((S, D), F32), jax.ShapeDtypeStruct((CHIPS, N_META, 256), F32),
            jax.ShapeDtypeStruct((N, O_KR_END), BF16), jax.ShapeDtypeStruct((N, D), BF16),
            jax.ShapeDtypeStruct((HEADS, 256, Q_LORA), F32),
            jax.ShapeDtypeStruct((HEADS, KV_LORA, 256), F32),
            jax.ShapeDtypeStruct((1, D), F32), jax.ShapeDtypeStruct((1, Q_LORA), F32), jax.ShapeDtypeStruct((1, KV_LORA), F32),
        ] + [jax.ShapeDtypeStruct((SHARD_OUT, D), F32)] * 4,
        scratch_shapes=[pltpu.VMEM((tr, D), F32), pltpu.SemaphoreType.DMA],
        compiler_params=_cparams(dimension_semantics=("arbitrary",)),
    )(h, dh2, dq, dkv, dkr, cq, ckv, dpl, dpl, dpg, dag, norm_g, win, gq, wq, gkv, wkv, cosf, sinf, *adam_out)


def _local_step(h, tgt, norm_g, win, gq, wq, gkv, wkv, pool_w, pool_scale, wout_s, m_wout_s, v_wout_s, gf, cosf, sinf):
    pool_in, pool_gate, cq, ckv, attn_gate, q, k, v = _fwd_in(h, norm_g, win, gq, wq, gkv, wkv, cosf, sinf)
    attn, lse, wout = _attn_fwd(q, k, v, wout_s)
    dh2, do, delta, dag, dpg, dpl, dwout, dpw, dps, dgf, loss = _mid(
        h, tgt, pool_in, pool_gate, attn_gate, attn, pool_w, pool_scale, wout, gf)
    dq, dkv, dkr, gwout = _attn_bwd(q, k, v, do, lse, delta, cosf, sinf, dwout)
    gx, dmeta, du, hn, dwq, dwkv, dg, dgq, dgkv, *r_out = _bwd_in(
        h, dh2, dq, dkv, dkr, cq, ckv, dpl, dpg, dag, norm_g, win, gq, wq, gkv, wkv, cosf, sinf,
        (wout_s, gwout, m_wout_s, v_wout_s))
    return dict(gx=gx, dmeta=dmeta, du=du, dag=dag, hn=hn, dwq=dwq, dwkv=dwkv, r_out=tuple(r_out), dg=dg, dgq=dgq,
                dgkv=dgkv, dpw=dpw, dps=dps, dgf=dgf, loss=loss)


_CHIP_RELS = ((0, 0), (1, 0), (0, 1), (1, 1))

_ARR_ROWS = (SHARD_IN, SHARD_OUT, 256, KV_LORA, N_META)
_ARR_COLS = (D, D, Q_LORA, 256, 256)
_PIECES = (
    (0, 0, 256, 0), (0, 256, SHARD_IN - 256, 1),
    (1, 0, 128, 0), (1, 128, 128, 1),
    (2, 0, 128, 0), (2, 128, 128, 1),
    (3, 0, 64, 0), (3, 64, 64, 1),
    (4, 0, N_META, 0),
)
_NP = len(_PIECES)
_PIECE_MAX = (256, 128, 128, 64, N_META)


def _gathered_at(refs, arr, chip, r0, n):
    if arr in (0, 1):
        return refs[arr].at[pl.ds(pl.multiple_of(_ARR_ROWS[arr] * chip + r0, 16), n), :]
    return refs[arr].at[chip, pl.ds(r0, n), :]


def _remote(src, dst, send_sem, recv_sem, to):
    return pltpu.make_async_remote_copy(src_ref=src, dst_ref=dst, send_sem=send_sem, recv_sem=recv_sem,
                                        device_id=to, device_id_type=MESH)


def _gather_weights(winT_s, wqT_s, wkv_s, meta_s, x2, tgt2):
    arrays = (0, 2, 3, 4)

    def body(win_ref, wq_ref, wkv_ref, meta_ref, x_ref, t_ref, win_o, wq_o, wkv_o, h_o, tp_o,
             s_win, s_wq, s_wkv, meta_all, head_buf, x_buf, t_buf, ici_send, ici_recv, fwd_send, fwd_recv,
             loc_sems, own_sems):
        x, y, c = lax.axis_index("x"), lax.axis_index("y"), lax.axis_index("c")
        me = 2 * x + y
        stage = (s_win, None, s_wq, s_wkv, meta_ref)
        outs = (win_o, None, wq_o, wkv_o, meta_all)

        _peer_signal(x, y, c)

        frames = pl.ds(HEAD_ROWS, S)
        loads = [pltpu.make_async_copy(x_ref, x_buf, loc_sems.at[0]), pltpu.make_async_copy(t_ref, t_buf, loc_sems.at[1])]
        local = [pltpu.make_async_copy(x_buf, h_o.at[frames, :], loc_sems.at[0]),
                 pltpu.make_async_copy(t_buf, tp_o.at[frames, :], loc_sems.at[1])]
        for cp in loads:
            cp.start()

        s_win[...] = win_ref[...].astype(BF16)
        s_wq[0:QK, :] = wq_ref[...].astype(BF16)
        s_wq[QK:256, :] = jnp.zeros((256 - QK, Q_LORA), BF16)
        s_wkv[...] = wkv_ref[...].astype(BF16)
        head_buf[...] = jnp.zeros_like(head_buf)
        zeros = pltpu.make_async_copy(head_buf, tp_o.at[pl.ds(0, HEAD_ROWS), :], loc_sems.at[2])
        zeros.start()

        def chip_of(rel):
            fx, fy = _CHIP_RELS[rel]
            return 2 * (x ^ fx) + (y ^ fy)

        def same_core_of(rel):
            fx, fy = _CHIP_RELS[rel]
            return (x ^ fx, y ^ fy, c)

        def ici_copy(rel, i, src_chip, to):
            arr, r0, n, _ = _PIECES[i]
            k = (rel - 1) * _NP + i
            return _remote(stage[arr].at[pl.ds(r0, n), :], _gathered_at(outs, arr, src_chip, r0, n),
                           ici_send.at[k], ici_recv.at[k], to)

        def fwd_copy(rel, i, to):
            arr, r0, n, _ = _PIECES[i]
            k = (rel - 1) * _NP + i
            place = _gathered_at(outs, arr, chip_of(rel), r0, n)
            return _remote(place, place, fwd_send.at[k], fwd_recv.at[k], to)

        _peer_wait()
        for core in (0, 1):
            @pl.when(c == core)
            def _(core=core):
                mine = [i for i in range(_NP) if _PIECES[i][3] == core and _PIECES[i][0] in arrays]
                theirs = [i for i in range(_NP) if _PIECES[i][3] != core and _PIECES[i][0] in arrays]
                sends = [ici_copy(rel, i, me, same_core_of(rel)) for rel in (1, 2, 3) for i in mine]
                for cp in sends:
                    cp.start()
                for ld, st in zip(loads, local):
                    ld.wait()
                    st.start()
                own = [pltpu.make_async_copy(stage[arr], _gathered_at(outs, arr, me, 0, _ARR_ROWS[arr]), own_sems.at[arr])
                       for arr in arrays if arr != 4]
                for cp in own:
                    cp.start()
                meta_all[me] = meta_ref[...]
                for rel in (1, 2, 3):
                    for i in mine:
                        ici_copy(rel, i, chip_of(rel), (x, y, c)).wait_recv()
                        fwd = fwd_copy(rel, i, (x, y, 1 - c))
                        fwd.start()
                        sends.append(fwd)
                for rel in (1, 2, 3):
                    for i in theirs:
                        fwd_copy(rel, i, (x, y, c)).wait_recv()
                for cp in sends:
                    cp.wait_send()
                for cp in own:
                    cp.wait()

        zeros.wait()
        for chip in range(CHIPS):
            head_buf[PAD:HEAD_ROWS, chip * 256:(chip + 1) * 256] = meta_all[chip]
        head = pltpu.make_async_copy(head_buf, h_o.at[pl.ds(0, HEAD_ROWS), :], loc_sems.at[2])
        head.start()
        head.wait()
        for cp in local:
            cp.wait()

    vm = pl.BlockSpec(memory_space=pltpu.VMEM)
    hbm = pl.BlockSpec(memory_space=pl.ANY)
    return pl.pallas_call(
        body,
        name="gather_weights",
        in_specs=[vm] * 4 + [hbm] * 2,
        out_specs=[hbm] * 5,
        out_shape=[
            jax.ShapeDtypeStruct((D_IN, D), BF16),
            jax.ShapeDtypeStruct((CHIPS, 256, Q_LORA), BF16), jax.ShapeDtypeStruct((CHIPS, KV_LORA, 256), BF16),
            jax.ShapeDtypeStruct((N, D), F32), jax.ShapeDtypeStruct((N, D), F32),
        ],
        scratch_shapes=[pltpu.VMEM((_ARR_ROWS[a], _ARR_COLS[a]), BF16) for a in (0, 2, 3)]
        + [pltpu.VMEM((CHIPS, N_META, 256), F32), pltpu.VMEM((HEAD_ROWS, D), F32), pltpu.VMEM((S, D), F32),
           pltpu.VMEM((S, D), F32)]
        + [pltpu.SemaphoreType.DMA((3 * _NP,))] * 4 + [pltpu.SemaphoreType.DMA((3,)), pltpu.SemaphoreType.DMA((4,))],
        compiler_params=_cparams(collective_id=0),
    )(winT_s, wqT_s, wkv_s, meta_s, x2, tgt2)


_SM_ROWS = (len(POOL_WINDOWS) * POOL_GROUP, VEC_ROWS)
_SM_COLS = (POOL_GROUP, D)
_SM_PIECES = ((0, 0, 256, 0), (0, 256, 256, 1), (1, 0, VEC_ROWS, 0))
_NSP = len(_SM_PIECES)


def _reduce_grads(du, dag, hn, dwq, dwkv, dmeta4, dpw, dg, dgf, dgq, dgkv, dps, loss):
    arrays = (0, 2, 3, 4)
    loaded = (2, 3, 4)
    blocks = ([(0, 256), (256, 512)], [(512, 768), (768, 1024)], [(1024, 1280), (1280, O_AG), (O_AG, O_AG + 256)],
              [(O_AG + 256, D_IN)])

    def body(du_hbm, dag_hbm, hn_hbm, dwq_ref, dwkv_ref, dmeta_ref, dpw_ref, dg_ref, dgf_ref, dgq_ref, dgkv_ref, dps_ref,
             loss_ref, gwin_o, gwq_o, gwkv_o, gmeta_o, gpw_o, gg_o, ggf_o, ggq_o, ggkv_o, gps_o, gloss_o,
             ow2, ow3, ow4, sb0, sb2, sb3, sb4, st0, st2, st3, st4, rc0, rc2, rc3, rc4,
             vec, sm_sb0, sm_sb1, sm_cs0, sm_cs1, sm_rc0, sm_rc1, vec_fin, du_v, dag_v, hn_v, dwin_buf, own0,
             own_sems, d2d_send, d2d_recv, ici_send, ici_recv, fin_send, fin_recv,
             swap_send, swap_recv, smi_send, smi_recv, smf_send, smf_recv, ld_sems):
        x, y, c = lax.axis_index("x"), lax.axis_index("y"), lax.axis_index("c")
        me = 2 * x + y
        _peer_signal(x, y, c)
        operands = [pltpu.make_async_copy(src, dst, ld_sems.at[t])
                    for t, (src, dst) in enumerate(((du_hbm, du_v), (dag_hbm, dag_v), (hn_hbm, hn_v)))]
        for cp in operands:
            cp.start()
        grads = (None, None, dwq_ref, dwkv_ref, dmeta_ref)
        outs = (gwin_o, None, gwq_o, gwkv_o, gmeta_o)
        own_buf = (None, None, ow2, ow3, ow4)
        sib_buf = (sb0, None, sb2, sb3, sb4)
        stage = (st0, None, st2, st3, st4)
        recv = (rc0, None, rc2, rc3, rc4)
        sm_mine = (dpw_ref, vec)
        sm_sib = (sm_sb0, sm_sb1)
        sm_chip = (sm_cs0, sm_cs1)
        sm_recv = (sm_rc0, sm_rc1)
        sm_out = (gpw_o, vec_fin)
        sibling = (x, y, 1 - c)

        def chip_of(rel):
            fx, fy = _CHIP_RELS[rel]
            return 2 * (x ^ fx) + (y ^ fy)

        def same_core_of(rel):
            fx, fy = _CHIP_RELS[rel]
            return (x ^ fx, y ^ fy, c)

        def slot(bufs, i, idx):
            arr, _, n, _ = _PIECES[i]
            return bufs[arr].at[idx, pl.ds(0, n), :]

        def own_load(rel, i):
            arr, r0, n, _ = _PIECES[i]
            return pltpu.make_async_copy(_gathered_at(grads, arr, chip_of(rel), r0, n), slot(own_buf, i, rel),
                                         own_sems.at[rel * _NP + i])

        def d2d_copy(rel, i):
            arr, r0, n, _ = _PIECES[i]
            k = rel * _NP + i
            return _remote(_gathered_at(grads, arr, chip_of(rel), r0, n), slot(sib_buf, i, rel),
                           d2d_send.at[k], d2d_recv.at[k], sibling)

        def ici_copy(rel, i):
            k = (rel - 1) * _NP + i
            return _remote(slot(stage, i, rel - 1), slot(recv, i, rel - 1), ici_send.at[k], ici_recv.at[k],
                           same_core_of(rel))

        def fin_copy(i):
            arr, r0, n, _ = _PIECES[i]
            place = outs[arr].at[pl.ds(r0, n), :]
            return _remote(place, place, fin_send.at[i], fin_recv.at[i], sibling)

        def sm_ici_copy(rel, j):
            blk, r0, n, _ = _SM_PIECES[j]
            k = (rel - 1) * _NSP + j
            return _remote(sm_chip[blk].at[pl.ds(r0, n), :], sm_recv[blk].at[rel - 1, pl.ds(r0, n), :],
                           smi_send.at[k], smi_recv.at[k], same_core_of(rel))

        def sm_fin_copy(j):
            blk, r0, n, _ = _SM_PIECES[j]
            place = sm_out[blk].at[pl.ds(r0, n), :]
            return _remote(place, place, smf_send.at[j], smf_recv.at[j], sibling)

        vec[...] = jnp.zeros_like(vec)
        vec[0:1, :] = dg_ref[...]
        vec[1:2, :] = dgf_ref[...]
        vec[2:3, V_GQ:V_GQ + Q_LORA] = dgq_ref[...]
        vec[2:3, V_GKV:V_GKV + KV_LORA] = dgkv_ref[...]
        vec[2:3, V_PS:V_PS + D_POOL] = dps_ref[...]
        vec[2:3, V_LOSS:D] = loss_ref[...]
        _peer_wait()
        swaps = [_remote(sm_mine[b], sm_sib[b], swap_send.at[b], swap_recv.at[b], sibling) for b in (0, 1)]
        for cp in swaps:
            cp.start()

        for core in (0, 1):
            @pl.when(c == core)
            def _(core=core):
                mine = [i for i in range(_NP) if _PIECES[i][3] == core and _PIECES[i][0] in loaded]
                theirs = [i for i in range(_NP) if _PIECES[i][3] != core and _PIECES[i][0] in loaded]
                i0 = next(i for i in range(_NP) if _PIECES[i][0] == 0 and _PIECES[i][3] == core)
                j0 = next(i for i in range(_NP) if _PIECES[i][0] == 0 and _PIECES[i][3] != core)
                sm_mine_p = [j for j in range(_NSP) if _SM_PIECES[j][3] == core]
                sm_theirs_p = [j for j in range(_NSP) if _SM_PIECES[j][3] != core]
                sends = list(swaps)

                for rel in (1, 2, 3, 0):
                    for i in theirs:
                        cp = d2d_copy(rel, i)
                        cp.start()
                        sends.append(cp)
                    for i in mine:
                        own_load(rel, i).start()

                def rel_of(chip):
                    flips = chip ^ me
                    return jnp.where(flips == 2, 1, jnp.where(flips == 1, 2, flips))

                def shard_rows(chip, i):
                    return pl.ds(SHARD_IN * chip + _PIECES[i][1], _PIECES[i][2])

                def d2d0(chip, i):
                    rel = rel_of(chip)
                    return _remote(dwin_buf.at[shard_rows(chip, i), :], slot(sib_buf, i, rel),
                                   d2d_send.at[rel * _NP + i], d2d_recv.at[rel * _NP + i], sibling)

                def ici0(chip):
                    slot_idx = jnp.maximum(rel_of(chip) - 1, 0)
                    return _remote(slot(stage, i0, slot_idx), slot(recv, i0, slot_idx), ici_send.at[slot_idx * _NP + i0],
                                   ici_recv.at[slot_idx * _NP + i0], (chip // 2, chip % 2, c))

                def settle(chip):
                    d2d0(chip, i0).wait_recv()
                    total = dwin_buf[shard_rows(chip, i0), :] + slot(sib_buf, i0, rel_of(chip))[...]

                    @pl.when(chip != me)
                    def _():
                        slot(stage, i0, jnp.maximum(rel_of(chip) - 1, 0))[...] = total.astype(BF16)
                        ici0(chip).start()

                    @pl.when(chip == me)
                    def _():
                        own0[0:_PIECES[i0][2], :] = total

                for cp in operands:
                    cp.wait()
                for chip in range(CHIPS):
                    for lo, hi in blocks[chip]:
                        if lo < O_AG:
                            dwin_buf[lo:hi, :] = _tn(du_v[:, lo:lo + 256], hn_v[...])[0:hi - lo, :]
                        else:
                            dwin_buf[lo:hi, :] = _tn(dag_v[:, lo - O_AG:hi - O_AG], hn_v[...])
                    cp = d2d0(chip, j0)
                    cp.start()
                    sends.append(cp)
                    if chip > 0:
                        settle(chip - 1)
                settle(CHIPS - 1)

                for rel in (1, 2, 3):
                    for i in mine:
                        arr, r0, n, _ = _PIECES[i]
                        own_load(rel, i).wait()
                        d2d_copy(rel, i).wait_recv()
                        total = slot(own_buf, i, rel)[...] + slot(sib_buf, i, rel)[...]
                        slot(stage, i, rel - 1)[...] = total.astype(stage[arr].dtype)
                        cp = ici_copy(rel, i)
                        cp.start()
                        sends.append(cp)

                for b in (0, 1):
                    swaps[b].wait_recv()
                    sm_chip[b][...] = sm_mine[b][...] + sm_sib[b][...]
                for rel in (1, 2, 3):
                    for j in sm_mine_p:
                        cp = sm_ici_copy(rel, j)
                        cp.start()
                        sends.append(cp)

                for i in mine:
                    arr, r0, n, _ = _PIECES[i]
                    own_load(0, i).wait()
                    d2d_copy(0, i).wait_recv()
                    total = slot(own_buf, i, 0)[...] + slot(sib_buf, i, 0)[...]
                    for rel in (1, 2, 3):
                        ici_copy(rel, i).wait_recv()
                        total = total + slot(recv, i, rel - 1)[...].astype(F32)
                    outs[arr][pl.ds(r0, n), :] = total
                    cp = fin_copy(i)
                    cp.start()
                    sends.append(cp)
                total = own0[0:_PIECES[i0][2], :]
                for rel in (1, 2, 3):
                    ici_copy(rel, i0).wait_recv()
                    total = total + slot(recv, i0, rel - 1)[...].astype(F32)
                outs[0][pl.ds(_PIECES[i0][1], _PIECES[i0][2]), :] = total
                cp = fin_copy(i0)
                cp.start()
                sends.append(cp)

                for j in sm_mine_p:
                    blk, r0, n, _ = _SM_PIECES[j]
                    for rel in (1, 2, 3):
                        sm_ici_copy(rel, j).wait_recv()
                    total = jnp.zeros((n, _SM_COLS[blk]), F32)
                    for chip in range(CHIPS):
                        flips = chip ^ me
                        rel = jnp.where(flips == 2, 1, jnp.where(flips == 1, 2, flips))
                        theirs_rows = sm_recv[blk][jnp.maximum(rel - 1, 0), pl.ds(r0, n), :]
                        total = total + jnp.where(rel == 0, sm_chip[blk][pl.ds(r0, n), :], theirs_rows)
                    sm_out[blk][pl.ds(r0, n), :] = total
                    cp = sm_fin_copy(j)
                    cp.start()
                    sends.append(cp)

                for i in theirs + [j0]:
                    fin_copy(i).wait_recv()
                for j in sm_theirs_p:
                    sm_fin_copy(j).wait_recv()
                for cp in sends:
                    cp.wait_send()
                for chip in range(CHIPS):
                    @pl.when(chip != me)
                    def _(chip=chip):
                        ici0(chip).wait_send()

        gg_o[...] = vec_fin[0:1, :]
        ggf_o[...] = vec_fin[1:2, :]
        ggq_o[...] = vec_fin[2:3, V_GQ:V_GQ + Q_LORA]
        ggkv_o[...] = vec_fin[2:3, V_GKV:V_GKV + KV_LORA]
        gps_o[...] = vec_fin[2:3, V_PS:V_PS + D_POOL]
        gloss_o[...] = vec_fin[2:3, V_LOSS:D]

    vm = pl.BlockSpec(memory_space=pltpu.VMEM)
    piece_buf = lambda lead, dtype, which=arrays: [
        pltpu.VMEM((lead, _PIECE_MAX[a], _ARR_COLS[a]), F32 if a == 4 else dtype) for a in which]
    sm_buf = lambda *lead: [pltpu.VMEM(lead + (_SM_ROWS[b], _SM_COLS[b]), F32) for b in (0, 1)]
    dma = lambda n: [pltpu.SemaphoreType.DMA((n,))] * 2
    return pl.pallas_call(
        body,
        name="reduce_grads",
        in_specs=[pl.BlockSpec(memory_space=pl.ANY)] * 5 + [vm] * 8,
        out_specs=[vm] * 11,
        out_shape=[jax.ShapeDtypeStruct((_ARR_ROWS[a], _ARR_COLS[a]), F32) for a in arrays]
        + [jax.ShapeDtypeStruct((_SM_ROWS[0], _SM_COLS[0]), F32), jax.ShapeDtypeStruct((1, D), F32),
           jax.ShapeDtypeStruct((1, D), F32), jax.ShapeDtypeStruct((1, Q_LORA), F32),
           jax.ShapeDtypeStruct((1, KV_LORA), F32), jax.ShapeDtypeStruct((1, D_POOL), F32),
           jax.ShapeDtypeStruct((1, 128), F32)],
        scratch_shapes=piece_buf(CHIPS, F32, loaded) + piece_buf(CHIPS, F32) + piece_buf(3, BF16) + piece_buf(3, BF16)
        + [pltpu.VMEM((VEC_ROWS, D), F32)] + sm_buf() + sm_buf() + sm_buf(3) + [pltpu.VMEM((VEC_ROWS, D), F32)]
        + [pltpu.VMEM((N, O_KR_END), BF16), pltpu.VMEM((N, D_POOL), BF16), pltpu.VMEM((N, D), BF16),
           pltpu.VMEM((D_IN, D), F32), pltpu.VMEM((_PIECE_MAX[0], D), F32)]
        + [pltpu.SemaphoreType.DMA((CHIPS * _NP,))]
        + dma(CHIPS * _NP) + dma(3 * _NP) + dma(_NP) + dma(2) + dma(3 * _NSP) + dma(_NSP)
        + [pltpu.SemaphoreType.DMA((3,))],
        compiler_params=_cparams(collective_id=3),
    )(du, dag, hn, dwq, dwkv, dmeta4, dpw, dg, dgf, dgq, dgkv, dps, loss)


def _adamw_math(w, g, m, v):
    m = B1 * m + (1.0 - B1) * g
    v = B2 * v + (1.0 - B2) * (g * g)
    m_hat = m / C1
    v_hat = v / C2
    delta = -LR * (m_hat / (jnp.sqrt(v_hat) + ADAM_EPS) + WD * w)
    return delta, m, v


def _adamw_rows(name, w, g, m, v, block_rows):
    rows, cols = w.shape

    def body(w_ref, g_ref, m_ref, v_ref, go_ref, d_ref, nm_ref, nv_ref):
        g = g_ref[...]
        go_ref[...] = g
        d_ref[...], nm_ref[...], nv_ref[...] = _adamw_math(w_ref[...], g, m_ref[...], v_ref[...])

    spec = pl.BlockSpec((block_rows, cols), lambda i: (i, 0))
    return pl.pallas_call(
        body,
        name=name,
        grid=(rows // block_rows,),
        in_specs=[spec] * 4,
        out_specs=[spec] * 4,
        out_shape=[jax.ShapeDtypeStruct(w.shape, F32)] * 4,
        compiler_params=_cparams(dimension_semantics=("arbitrary",)),
    )(w, g, m, v)


def _adamw_small(groups):
    n = len(groups)

    def body(*refs):
        ins, outs = refs[:4 * n], refs[4 * n:]
        for t in range(n):
            w_ref, g_ref, m_ref, v_ref = ins[4 * t:4 * t + 4]
            g = g_ref[0:w_ref.shape[0], :]
            outs[4 * t][...] = g
            outs[4 * t + 1][...], outs[4 * t + 2][...], outs[4 * t + 3][...] = _adamw_math(
                w_ref[...], g, m_ref[...], v_ref[...])

    vm = pl.BlockSpec(memory_space=pltpu.VMEM)
    flat = [a for grp in groups for a in grp]
    outs = pl.pallas_call(
        body,
        name="adamw_small",
        in_specs=[vm] * (4 * n),
        out_specs=[vm] * (4 * n),
        out_shape=[jax.ShapeDtypeStruct(grp[0].shape, F32) for grp in groups for _ in range(4)],
        compiler_params=_cparams(),
    )(*flat)
    return [tuple(outs[4 * t:4 * t + 4]) for t in range(n)]


def _rope_tables():
    half = QK_ROPE // 2
    f32 = np.float32
    inv_freq = (f32(1.0) / (f32(ROPE_THETA) ** (np.arange(half, dtype=f32) / f32(half)))).astype(f32)
    pos = np.arange(N, dtype=f32) - f32(PAD)
    ang = (pos[:, None] * inv_freq[None, :]).astype(f32)
    cos, sin = np.cos(ang).astype(f32), np.sin(ang).astype(f32)
    zero = np.zeros((N, 128 - QK_ROPE), f32)
    return jnp.asarray(np.concatenate([cos, cos, zero], axis=1)), jnp.asarray(np.concatenate([-sin, sin, zero], axis=1))


def kernel(x, meta_tokens, norm_g, w_in, q_norm_g, w_q_b, kv_norm_g, w_kv_b, pool_w, pool_scale, w_out, final_norm_g, loss_target, m_meta_tokens, m_norm_g, m_w_in, m_q_norm_g, m_w_q_b, m_kv_norm_g, m_w_kv_b, m_pool_w, m_pool_scale, m_w_out, m_final_norm_g, v_meta_tokens, v_norm_g, v_w_in, v_q_norm_g, v_w_q_b, v_kv_norm_g, v_w_kv_b, v_pool_w, v_pool_scale, v_w_out, v_final_norm_g):
    tr = lambda a: a[0].T
    win, wq, wkv, h, tgt = _gather_weights(tr(w_in), tr(w_q_b), w_kv_b[0], meta_tokens, x[0], loss_target[0])
    cosf, sinf = _rope_tables()
    gf = final_norm_g.reshape(1, D)

    part = _local_step(h, tgt, norm_g, win, q_norm_g, wq, kv_norm_g, wkv, pool_w[0], pool_scale, w_out[0], m_w_out[0],
                       v_w_out[0], gf, cosf, sinf)

    pw2 = lambda a: a.reshape(len(POOL_WINDOWS) * POOL_GROUP, POOL_GROUP)
    gwinT, gwqT, gwkv, gmeta, gpw, gg, ggf, ggq, ggkv, gps, gloss = _reduce_grads(
        part["du"], part["dag"], part["hn"], part["dwq"], part["dwkv"], part["dmeta"], pw2(part["dpw"]), part["dg"],
        part["dgf"], part["dgq"], part["dgkv"], part["dps"], part["loss"])

    r_in = _adamw_rows("adamw_w_in", tr(w_in), gwinT, tr(m_w_in), tr(v_w_in), 248)
    r_out = part["r_out"]
    fn2 = lambda a: a.reshape(1, D)
    r_meta, r_norm, r_gq, r_wq, r_gkv, r_wkv, r_pw, r_ps, r_fn = _adamw_small([
        (meta_tokens, gmeta, m_meta_tokens, v_meta_tokens),
        (norm_g, gg, m_norm_g, v_norm_g),
        (q_norm_g, ggq, m_q_norm_g, v_q_norm_g),
        (tr(w_q_b), gwqT, tr(m_w_q_b), tr(v_w_q_b)),
        (kv_norm_g, ggkv, m_kv_norm_g, v_kv_norm_g),
        (w_kv_b[0], gwkv, m_w_kv_b[0], v_w_kv_b[0]),
        (pw2(pool_w), gpw, pw2(m_pool_w), pw2(v_pool_w)),
        (pool_scale, gps, m_pool_scale, v_pool_scale),
        (fn2(final_norm_g), ggf, fn2(m_final_norm_g), fn2(v_final_norm_g)),
    ])
    untr = lambda a: a.T[None]
    pw4 = lambda a: a.reshape(1, len(POOL_WINDOWS), POOL_GROUP, POOL_GROUP)
    per_kind = [[
        r_meta[kind], r_norm[kind], untr(r_in[kind]), r_gq[kind], untr(r_wq[kind]), r_gkv[kind], r_wkv[kind][None],
        pw4(r_pw[kind]), r_ps[kind], r_out[kind][None], r_fn[kind].reshape(D),
    ] for kind in range(4)]
    return (gloss[0, 0], part["gx"][None], *per_kind[0], *per_kind[1], *per_kind[2], *per_kind[3])
```

```python
import jax
import jax.numpy as jnp
import numpy as np
from jax import lax
from jax.experimental import pallas as pl
from jax.experimental.pallas import tpu as pltpu

F32 = jnp.float32
BF16 = jnp.bfloat16

D = 1024
S = 2048
N_META = 16
PAD = 112
HEAD_ROWS = PAD + N_META
N = HEAD_ROWS + S
D_POOL = 512
POOL_WINDOWS = (2, 4, 8, 16)
POOL_GROUP = 128
HALO = 16
HEADS = 4
QK_NOPE = 128
QK_ROPE = 64
QK = QK_NOPE + QK_ROPE
V_HEAD = 128
Q_LORA = 256
KV_LORA = 128
D_IN = 1984
EPS = 1e-6
ROPE_THETA = 10000.0
SCALE = QK ** -0.5
CHIPS = 4

ROWS_FWD = 544
ROWS_MID = 544
ROWS_BWD = 544
TK = 128
TQ = 256
NQ = S // TQ
HEADS_PER_STEP_BWD = 2

O_PI, O_PG, O_CQ, O_CKV, O_KR, O_AG = 0, 512, 1024, 1280, 1408, 1472
O_KR_END = O_KR + 128
DU_SLABS = O_KR_END // D_POOL
SHARD_IN = D_IN // CHIPS
SHARD_OUT = D // CHIPS

LR, B1, B2, ADAM_EPS, WD, STEP = 0.001, 0.9, 0.999, 1e-08, 0.01, 10
C1 = 1.0 - B1**STEP
C2 = 1.0 - B2**STEP

VMEM_LIMIT = 60 * 1024 * 1024
MESH = pl.DeviceIdType.MESH
NEG = -1e30

VEC_ROWS = 8
V_GQ, V_GKV, V_PS, V_LOSS = 0, 256, 384, 896


def _cparams(**kw):
    return pltpu.CompilerParams(vmem_limit_bytes=VMEM_LIMIT, **kw)


def _nt(a, b):
    return lax.dot_general(a, b, (((1,), (1,)), ((), ())), preferred_element_type=F32)


def _tn(a, b):
    return lax.dot_general(a, b, (((0,), (0,)), ((), ())), preferred_element_type=F32)


def _nn(a, b):
    return jnp.dot(a, b, preferred_element_type=F32)


def _swap64(t):
    return pltpu.roll(t, 32, 1) + pltpu.roll(t, 96, 1)


def _sigmoid(x):
    return 1.0 / (1.0 + jnp.exp(-x))


def _low_lanes():
    return (lax.broadcasted_iota(jnp.int32, (1, 128), 1) < QK_ROPE).astype(F32)


def _rows(w, rows):
    return pl.BlockSpec((rows, w), lambda i: (i, 0))


def _const(*shape):
    return pl.BlockSpec(shape, lambda *_: (0,) * len(shape), pipeline_mode=pl.Buffered(1))


STAT_GROUPS = HEADS // HEADS_PER_STEP_BWD


def _stat_slot(head):
    return head // HEADS_PER_STEP_BWD, head % HEADS_PER_STEP_BWD


N_PEERS = 4


def _peer_signal(x, y, c):
    barrier = pltpu.get_barrier_semaphore()
    peers = [(x, y, 1 - c)] + [(x ^ fx, y ^ fy, c) for fx, fy in _CHIP_RELS[1:]]
    assert len(peers) == N_PEERS
    for peer in peers:
        pl.semaphore_signal(barrier, inc=1, device_id=peer, device_id_type=MESH)


def _peer_wait():
    pl.semaphore_wait(pltpu.get_barrier_semaphore(), N_PEERS)


def _attn_tiles():
    return [(0, TK, TK)] + [(TK + TQ * t, TQ, TK + TQ * (t + 1)) for t in range(NQ)]


def _masked_scores(q, k, rows, klen):
    s = _nt(q, k)
    col = lax.broadcasted_iota(jnp.int32, (1, TK), 1)
    head_bias = jnp.where(col >= PAD, 0.0, NEG)
    if klen == TK:
        return s + head_bias
    r = lax.broadcasted_iota(jnp.int32, (rows, 1), 0) >> 6
    c = lax.broadcasted_iota(jnp.int32, (1, rows), 1) >> 6
    diag_bias = jnp.where(c <= r, 0.0, NEG)
    parts = [s[:, 0:TK] + head_bias]
    if klen - rows > TK:
        parts.append(s[:, TK:klen - rows])
    parts.append(s[:, klen - rows:klen] + diag_bias)
    return jnp.concatenate(parts, axis=1)


def _fwd_in(h, norm_g, win, gq, wq, gkv, wkv, cosf, sinf):
    tr = ROWS_FWD

    def body(h_ref, g_ref, win_ref, gq_ref, wq_ref, gkv_ref, wkv_ref, cos_ref, sin_ref,
             pi_ref, pg_ref, cq_ref, ckv_ref, ag_ref, q_ref, k_ref, v_ref):
        h = h_ref[...]
        r = lax.rsqrt(jnp.mean(h * h, axis=-1, keepdims=True) + EPS)
        hn = ((h * r) * g_ref[...]).astype(BF16)
        u = _nt(hn, win_ref[0:O_KR_END, :])
        pi_ref[...] = u[:, O_PI:O_PG]
        pg_ref[...] = u[:, O_PG:O_CQ]
        cq = u[:, O_CQ:O_CKV]
        ckv = u[:, O_CKV:O_KR]
        cq_ref[...] = cq
        ckv_ref[...] = ckv
        ag_ref[...] = _nt(hn, win_ref[O_AG:D_IN, :])
        cosv = cos_ref[...]
        sinv = sin_ref[...]
        kr = u[:, O_KR:O_KR_END] * _low_lanes()
        kr = (kr * cosv + _swap64(kr) * sinv).astype(BF16)
        rq = lax.rsqrt(jnp.mean(cq * cq, axis=-1, keepdims=True) + EPS)
        cqn = ((cq * rq) * gq_ref[...]).astype(BF16)
        rkv = lax.rsqrt(jnp.mean(ckv * ckv, axis=-1, keepdims=True) + EPS)
        ckvn = ((ckv * rkv) * gkv_ref[...]).astype(BF16)
        for hd in range(HEADS):
            qh = _nt(cqn, wq_ref[hd]) * SCALE
            z = qh[:, QK_NOPE:]
            q_ref[hd, :, 0:QK_NOPE] = qh[:, 0:QK_NOPE].astype(BF16)
            q_ref[hd, :, QK_NOPE:] = (z * cosv + _swap64(z) * sinv).astype(BF16)
            kvh = _nn(ckvn, wkv_ref[hd])
            k_ref[hd, :, 0:QK_NOPE] = kvh[:, 0:QK_NOPE].astype(BF16)
            k_ref[hd, :, QK_NOPE:] = kr
            v_ref[hd] = kvh[:, QK_NOPE:].astype(BF16)

    head = lambda w: pl.BlockSpec((HEADS, tr, w), lambda i: (0, i, 0))
    return pl.pallas_call(
        body,
        name="fwd_in",
        grid=(N // tr,),
        in_specs=[
            _rows(D, tr), _const(1, D), _const(D_IN, D), _const(1, Q_LORA), _const(HEADS, 256, Q_LORA),
            _const(1, KV_LORA), _const(HEADS, KV_LORA, 256), _rows(128, tr), _rows(128, tr),
        ],
        out_specs=[_rows(D_POOL, tr), _rows(D_POOL, tr), _rows(Q_LORA, tr), _rows(KV_LORA, tr), _rows(D_POOL, tr),
                   head(256), head(256), head(V_HEAD)],
        out_shape=[
            jax.ShapeDtypeStruct((N, D_POOL), F32), jax.ShapeDtypeStruct((N, D_POOL), F32),
            jax.ShapeDtypeStruct((N, Q_LORA), F32), jax.ShapeDtypeStruct((N, KV_LORA), F32),
            jax.ShapeDtypeStruct((N, D_POOL), F32),
            jax.ShapeDtypeStruct((HEADS, N, 256), BF16), jax.ShapeDtypeStruct((HEADS, N, 256), BF16),
            jax.ShapeDtypeStruct((HEADS, N, V_HEAD), BF16),
        ],
        compiler_params=_cparams(dimension_semantics=("arbitrary",)),
    )(h, norm_g, win, gq, wq, gkv, wkv, cosf, sinf)


def _attn_fwd(q, k, v, wout_s):
    tiles = _attn_tiles()
    n_t = len(tiles)
    half = SHARD_OUT // 2
    send_step = 2
    fwd_step = n_t - 2

    def body(q_hbm, k_hbm, v_hbm, wout_ref, o_hbm, lse_ref, wout_o, q_buf, k_buf, v_buf, o_buf, s_wout, in_sems, out_sems,
             ici_send, ici_recv, fwd_send, fwd_recv, own_sem):
        step = pl.program_id(0)
        x, y, c = lax.axis_index("x"), lax.axis_index("y"), lax.axis_index("c")
        me = 2 * x + y

        def chip_of(rel):
            fx, fy = _CHIP_RELS[rel]
            return 2 * (x ^ fx) + (y ^ fy)

        def place(chip, core):
            return wout_o.at[pl.ds(pl.multiple_of(SHARD_OUT * chip + half * core, half), half), :]

        def ici_copy(rel, src_chip, to):
            return _remote(s_wout.at[pl.ds(pl.multiple_of(half * c, half), half), :], place(src_chip, c),
                           ici_send.at[rel - 1], ici_recv.at[rel - 1], to)

        def fwd_copy(rel, core, to):
            spot = place(chip_of(rel), core)
            return _remote(spot, spot, fwd_send.at[rel - 1], fwd_recv.at[rel - 1], to)

        own = pltpu.make_async_copy(s_wout, wout_o.at[pl.ds(pl.multiple_of(SHARD_OUT * me, SHARD_OUT), SHARD_OUT), :], own_sem)

        @pl.when(step == 0)
        def _():
            _peer_signal(x, y, c)
            s_wout[...] = wout_ref[...].astype(BF16)
            own.start()

        @pl.when(step == send_step)
        def _():
            _peer_wait()
            for rel in (1, 2, 3):
                fx, fy = _CHIP_RELS[rel]
                ici_copy(rel, me, (x ^ fx, y ^ fy, c)).start()

        @pl.when(step == fwd_step)
        def _():
            for rel in (1, 2, 3):
                ici_copy(rel, chip_of(rel), (x, y, c)).wait_recv()
                fwd_copy(rel, c, (x, y, 1 - c)).start()

        def finish_wout():
            for rel in (1, 2, 3):
                fwd_copy(rel, 1 - c, (x, y, c)).wait_recv()
            for rel in (1, 2, 3):
                ici_copy(rel, me, (x, y, c)).wait_send()
                fwd_copy(rel, c, (x, y, c)).wait_send()
            own.wait()

        def loads(idx):
            q0, rows, _ = tiles[idx]
            rs = pl.ds(q0, rows)
            return [pltpu.make_async_copy(src.at[:, rs, :], dst.at[:, rs, :], in_sems.at[a, idx % 2])
                    for a, (src, dst) in enumerate(((q_hbm, q_buf), (k_hbm, k_buf), (v_hbm, v_buf)))]

        def store(idx):
            q0, rows, _ = tiles[idx]
            return pltpu.make_async_copy(o_buf.at[idx % 2, pl.ds(0, rows), :], o_hbm.at[pl.ds(q0, rows), :],
                                         out_sems.at[idx % 2])

        @pl.when(step == 0)
        def _():
            lse_ref[...] = jnp.zeros_like(lse_ref)
            for cp in loads(0):
                cp.start()

        for idx, (q0, rows, klen) in enumerate(tiles):
            @pl.when(step == idx)
            def _(idx=idx, q0=q0, rows=rows, klen=klen):
                for cp in loads(idx):
                    cp.wait()
                if idx + 1 < n_t:
                    for cp in loads(idx + 1):
                        cp.start()
                if idx >= 2:
                    store(idx - 2).wait()
                for hd in range(HEADS):
                    s = _masked_scores(q_buf[hd, q0:q0 + rows, :], k_buf[hd, 0:klen, :], rows, klen)
                    m = jnp.max(s, axis=-1, keepdims=True)
                    p = jnp.exp(s - m)
                    l = jnp.sum(p, axis=-1, keepdims=True)
                    o_buf[idx % 2, 0:rows, hd * V_HEAD:(hd + 1) * V_HEAD] = _nn(p.astype(BF16), v_buf[hd, 0:klen, :]) / l
                    grp, lane = _stat_slot(hd)
                    lse_ref[grp, q0:q0 + rows, lane:lane + 1] = m + jnp.log(l)
                store(idx).start()
                if idx == n_t - 1:
                    store(idx - 1).wait()
                    store(idx).wait()
                    finish_wout()

    hbm = pl.BlockSpec(memory_space=pl.ANY)
    return pl.pallas_call(
        body,
        name="attn_fwd",
        grid=(n_t,),
        in_specs=[hbm, hbm, hbm, _const(SHARD_OUT, D)],
        out_specs=[hbm, _const(STAT_GROUPS, N, 128), hbm],
        out_shape=[jax.ShapeDtypeStruct((N, HEADS * V_HEAD), F32), jax.ShapeDtypeStruct((STAT_GROUPS, N, 128), F32),
                   jax.ShapeDtypeStruct((D, D), BF16)],
        scratch_shapes=[pltpu.VMEM((HEADS, N, 256), BF16), pltpu.VMEM((HEADS, N, 256), BF16),
                        pltpu.VMEM((HEADS, N, V_HEAD), BF16), pltpu.VMEM((2, TQ, HEADS * V_HEAD), F32),
                        pltpu.VMEM((SHARD_OUT, D), BF16),
                        pltpu.SemaphoreType.DMA((3, 2)), pltpu.SemaphoreType.DMA((2,))]
        + [pltpu.SemaphoreType.DMA((3,))] * 4 + [pltpu.SemaphoreType.DMA],
        compiler_params=_cparams(dimension_semantics=("arbitrary",), collective_id=1),
    )(q, k, v, wout_s)


def _inv_count(row0, rows, w):
    row = row0 + lax.broadcasted_iota(jnp.int32, (rows, 1), 0)
    return 1.0 / jnp.clip(row - (PAD - 1), 1, w).astype(F32)


def _mid(h, tgt, pool_in, pool_gate, attn_gate, attn, pool_w, pool_scale, wout, gf):
    tr = ROWS_MID
    per = tr // HALO
    ng = len(POOL_WINDOWS)

    def body(h_ref, t_ref, pin_ref, halo_ref, pg_ref, ag_ref, at_ref, pw_ref, ps_ref, wout_ref, gf_ref,
             dh2_ref, do_ref, delta_ref, dag_ref, dpg_ref, dpl_ref, dwout_ref, dpw_ref, dps_ref, dgf_ref, loss_ref):
        i = pl.program_id(0)

        @pl.when(i == 0)
        def _():
            dwout_ref[...] = jnp.zeros_like(dwout_ref)
            dpw_ref[...] = jnp.zeros_like(dpw_ref)
            dps_ref[...] = jnp.zeros_like(dps_ref)
            dgf_ref[...] = jnp.zeros_like(dgf_ref)
            loss_ref[...] = jnp.zeros_like(loss_ref)

        row0 = i * tr
        real = (row0 + lax.broadcasted_iota(jnp.int32, (tr, 1), 0)) >= HEAD_ROWS
        h = h_ref[...]

        halo = jnp.where(i > 0, halo_ref[...], 0.0)
        ext = jnp.concatenate([halo, pin_ref[...]], axis=0)
        pooled = []
        for g, w in enumerate(POOL_WINDOWS):
            e = ext[:, g * POOL_GROUP:(g + 1) * POOL_GROUP]
            acc = e
            shift = 1
            while shift < w:
                acc = acc + pltpu.roll(acc, shift, 0)
                shift *= 2
            pooled.append((acc[HALO:] * _inv_count(row0, tr, w) - e[HALO:]).astype(BF16))
        pw = [pw_ref[g].astype(BF16) for g in range(ng)]
        mixed = jnp.concatenate([_nn(pooled[g], pw[g]) for g in range(ng)], axis=1)
        ps = ps_ref[...]
        mixed_s = mixed * ps
        pg = pg_ref[...]
        sig_p = _sigmoid(pg)
        silu_p = pg * sig_p
        pool_out = (silu_p * mixed_s).astype(BF16)
        ag = ag_ref[...]
        sig_a = _sigmoid(ag)
        silu_a = ag * sig_a
        at = at_ref[...]
        attn_out = (silu_a * at).astype(BF16)
        cat = jnp.concatenate([pool_out, attn_out], axis=1)
        h2 = h + _nn(cat, wout_ref[...])

        r2 = lax.rsqrt(jnp.mean(h2 * h2, axis=-1, keepdims=True) + EPS)
        n2 = h2 * r2
        gfv = gf_ref[...]
        err = jnp.where(real, n2 * gfv - t_ref[...], 0.0)
        loss_ref[...] += jnp.sum(jnp.sum(err * err, axis=-1, keepdims=True), axis=0, keepdims=True) * (0.5 / D)
        dy = err * (1.0 / D)
        dgf_ref[...] += jnp.sum(dy * n2, axis=0, keepdims=True)
        dn = dy * gfv
        dh2 = r2 * (dn - n2 * jnp.mean(dn * n2, axis=-1, keepdims=True))
        dh2_ref[...] = dh2
        dh2b = dh2.astype(BF16)

        dwout_ref[...] += _tn(cat, dh2b)
        dcat = _nt(dh2b, wout_ref[...])
        dpo = dcat[:, 0:D_POOL]
        dao = dcat[:, D_POOL:D]
        do = dao * silu_a
        prod = do * at
        delta_ref[...] = jnp.zeros_like(delta_ref)
        for hd in range(HEADS):
            grp, lane = _stat_slot(hd)
            cols = slice(hd * V_HEAD, (hd + 1) * V_HEAD)
            do_ref[grp, :, lane * V_HEAD:(lane + 1) * V_HEAD] = do[:, cols].astype(BF16)
            delta_ref[grp, :, lane:lane + 1] = jnp.sum(prod[:, cols], axis=-1, keepdims=True)
        dag_ref[...] = (dao * at * (sig_a * (1.0 + ag * (1.0 - sig_a)))).astype(BF16)
        dmixed_s = dpo * silu_p
        dpg_ref[...] = (dpo * mixed_s * (sig_p * (1.0 + pg * (1.0 - sig_p)))).astype(BF16)
        dps_ref[...] += jnp.sum(dmixed_s * mixed, axis=0, keepdims=True)
        dmixed = (dmixed_s * ps).astype(BF16)
        dpl = []
        for g in range(ng):
            dm = dmixed[:, g * POOL_GROUP:(g + 1) * POOL_GROUP]
            dpl.append(_nt(dm, pw[g]))
            dpw_ref[g] += _tn(pooled[g], dm)
        dpl_ref[...] = jnp.concatenate(dpl, axis=1)

    halo_spec = pl.BlockSpec((HALO, D_POOL), lambda i: (jnp.maximum(i * per - 1, 0), 0))
    return pl.pallas_call(
        body,
        name="mid",
        grid=(N // tr,),
        in_specs=[
            _rows(D, tr), _rows(D, tr), _rows(D_POOL, tr), halo_spec, _rows(D_POOL, tr), _rows(D_POOL, tr),
            _rows(D_POOL, tr), _const(ng, POOL_GROUP, POOL_GROUP), _const(1, D_POOL), _const(D, D), _const(1, D),
        ],
        out_specs=[
            _rows(D, tr), pl.BlockSpec((STAT_GROUPS, tr, HEADS_PER_STEP_BWD * V_HEAD), lambda i: (0, i, 0)),
            pl.BlockSpec((STAT_GROUPS, tr, 128), lambda i: (0, i, 0)),
            _rows(D_POOL, tr), _rows(D_POOL, tr), _rows(D_POOL, tr),
            _const(D, D), _const(ng, POOL_GROUP, POOL_GROUP), _const(1, D_POOL), _const(1, D), _const(1, 128),
        ],
        out_shape=[
            jax.ShapeDtypeStruct((N, D), F32), jax.ShapeDtypeStruct((STAT_GROUPS, N, HEADS_PER_STEP_BWD * V_HEAD), BF16),
            jax.ShapeDtypeStruct((STAT_GROUPS, N, 128), F32),
            jax.ShapeDtypeStruct((N, D_POOL), BF16), jax.ShapeDtypeStruct((N, D_POOL), BF16),
            jax.ShapeDtypeStruct((N, D_POOL), F32), jax.ShapeDtypeStruct((D, D), F32),
            jax.ShapeDtypeStruct((ng, POOL_GROUP, POOL_GROUP), F32),
            jax.ShapeDtypeStruct((1, D_POOL), F32), jax.ShapeDtypeStruct((1, D), F32), jax.ShapeDtypeStruct((1, 128), F32),
        ],
        compiler_params=_cparams(dimension_semantics=("arbitrary",)),
    )(h, tgt, pool_in, pool_in, pool_gate, attn_gate, attn, pool_w, pool_scale, wout, gf)


def _unrope(dy, cosv, sinv):
    return dy * cosv + _swap64(dy * sinv) * _low_lanes()


def _attn_bwd(q, k, v, do, lse, delta, cosf, sinf, dwout):
    tiles = _attn_tiles()
    hp = HEADS_PER_STEP_BWD
    n_g = HEADS // hp
    n_t = len(tiles)
    half = SHARD_OUT // 2
    swap_at, send_at, sum_at = (0, 3), (0, 5), (n_g - 1, n_t // 2)

    def body(q_hbm, k_hbm, v_hbm, do_hbm, lse_ref, delta_ref, cos_ref, sin_ref, dwout_hbm, dq_hbm, dkv_ref, dkr_ref,
             gwout_ref, q_buf, k_buf, v_buf, do_buf, dq_buf, dk_acc, dv_acc, own_w, sib_w, stage_w, recv_w, gw_buf,
             in_sems, out_sems, ow_sems, d2d_send, d2d_recv, ici_send, ici_recv, fin_send, fin_recv):
        grp = pl.program_id(0)
        step = pl.program_id(1)
        heads = pl.ds(grp * hp, hp)
        x, y, c = lax.axis_index("x"), lax.axis_index("y"), lax.axis_index("c")
        sibling = (x, y, 1 - c)

        def chip_of(rel):
            fx, fy = _CHIP_RELS[rel]
            return 2 * (x ^ fx) + (y ^ fy)

        def piece(chip, core):
            return dwout_hbm.at[pl.ds(pl.multiple_of(SHARD_OUT * chip + half * core, half), half), :]

        def own_load(rel):
            return pltpu.make_async_copy(piece(chip_of(rel), c), own_w.at[rel], ow_sems.at[rel])

        def d2d_copy(rel):
            return _remote(piece(chip_of(rel), 1 - c), sib_w.at[rel], d2d_send.at[rel], d2d_recv.at[rel], sibling)

        def ici_copy(rel):
            fx, fy = _CHIP_RELS[rel]
            return _remote(stage_w.at[rel - 1], recv_w.at[rel - 1], ici_send.at[rel - 1], ici_recv.at[rel - 1],
                           (x ^ fx, y ^ fy, c))

        def fin_copy(core):
            spot = gw_buf.at[pl.ds(pl.multiple_of(half * core, half), half), :]
            return _remote(spot, spot, fin_send.at[0], fin_recv.at[0], sibling)

        @pl.when((grp == 0) & (step == 0))
        def _():
            _peer_signal(x, y, c)
            for rel in (1, 2, 3, 0):
                own_load(rel).start()

        @pl.when((grp == swap_at[0]) & (step == swap_at[1]))
        def _():
            _peer_wait()
            for rel in (1, 2, 3, 0):
                d2d_copy(rel).start()

        @pl.when((grp == send_at[0]) & (step == send_at[1]))
        def _():
            for rel in (1, 2, 3):
                own_load(rel).wait()
                d2d_copy(rel).wait_recv()
                stage_w[rel - 1] = (own_w[rel] + sib_w[rel]).astype(BF16)
                ici_copy(rel).start()

        @pl.when((grp == sum_at[0]) & (step == sum_at[1]))
        def _():
            own_load(0).wait()
            d2d_copy(0).wait_recv()
            total = own_w[0] + sib_w[0]
            for rel in (1, 2, 3):
                ici_copy(rel).wait_recv()
                total = total + recv_w[rel - 1].astype(F32)
            gw_buf[pl.ds(pl.multiple_of(half * c, half), half), :] = total
            fin_copy(c).start()

        def finish_dwout():
            fin_copy(1 - c).wait_recv()
            for rel in (0, 1, 2, 3):
                d2d_copy(rel).wait_send()
            for rel in (1, 2, 3):
                ici_copy(rel).wait_send()
            fin_copy(c).wait_send()
            gwout_ref[...] = gw_buf[...]

        def loads(g, idx):
            q0, rows, _ = tiles[idx]
            rs = pl.ds(q0, rows)
            par = (g * n_t + idx) % 2
            hs = pl.ds(g * hp, hp)
            pairs = ((q_hbm.at[hs, rs, :], q_buf.at[:, rs, :]), (k_hbm.at[hs, rs, :], k_buf.at[:, rs, :]),
                     (v_hbm.at[hs, rs, :], v_buf.at[:, rs, :]), (do_hbm.at[g, rs, :], do_buf.at[rs, :]))
            return [pltpu.make_async_copy(src, dst, in_sems.at[a, par]) for a, (src, dst) in enumerate(pairs)]

        def store(idx):
            q0, rows, _ = tiles[idx]
            return pltpu.make_async_copy(dq_buf.at[idx % 2, :, pl.ds(0, rows), :], dq_hbm.at[heads, pl.ds(q0, rows), :],
                                         out_sems.at[idx % 2])

        @pl.when(step == 0)
        def _():
            dk_acc[...] = jnp.zeros_like(dk_acc)
            dv_acc[...] = jnp.zeros_like(dv_acc)

        @pl.when((step == 0) & (grp == 0))
        def _():
            dkr_ref[...] = jnp.zeros_like(dkr_ref)
            for cp in loads(grp, 0):
                cp.start()

        for idx, (q0, rows, klen) in enumerate(tiles):
            @pl.when(step == idx)
            def _(idx=idx, q0=q0, rows=rows, klen=klen):
                for cp in loads(grp, idx):
                    cp.wait()
                if idx + 1 < n_t:
                    for cp in loads(grp, idx + 1):
                        cp.start()
                if idx >= 2:
                    store(idx - 2).wait()
                qs = pl.ds(q0, rows)
                for hd in range(hp):
                    qv = q_buf[hd, qs, :]
                    kv = k_buf[hd, 0:klen, :]
                    p = jnp.exp(_masked_scores(qv, kv, rows, klen) - lse_ref[0, qs, hd:hd + 1])
                    dob = do_buf[qs, hd * V_HEAD:(hd + 1) * V_HEAD]
                    ds = (p * (_nt(dob, v_buf[hd, 0:klen, :]) - delta_ref[0, qs, hd:hd + 1])).astype(BF16)
                    dq = _nn(ds, kv) * SCALE
                    dq_buf[idx % 2, hd, 0:rows, 0:QK_NOPE] = dq[:, 0:QK_NOPE].astype(BF16)
                    dq_buf[idx % 2, hd, 0:rows, QK_NOPE:] = _unrope(dq[:, QK_NOPE:], cos_ref[qs, :], sin_ref[qs, :]).astype(BF16)
                    dk_acc[hd, 0:klen, :] += _tn(ds, qv)
                    dv_acc[hd, 0:klen, :] += _tn(p.astype(BF16), dob)
                store(idx).start()

        @pl.when(step == n_t - 1)
        def _():
            @pl.when(grp + 1 < n_g)
            def _():
                for cp in loads(grp + 1, 0):
                    cp.start()

            for hd in range(hp):
                dkv_ref[hd, :, 0:QK_NOPE] = dk_acc[hd, :, 0:QK_NOPE].astype(BF16)
                dkv_ref[hd, :, QK_NOPE:] = dv_acc[hd].astype(BF16)
                dkr_ref[...] += dk_acc[hd, :, QK_NOPE:]
            store(n_t - 2).wait()
            store(n_t - 1).wait()

            @pl.when(grp == n_g - 1)
            def _():
                finish_dwout()

    hbm = pl.BlockSpec(memory_space=pl.ANY)
    stat = pl.BlockSpec((1, N, 128), lambda g, t: (g, 0, 0), pipeline_mode=pl.Buffered(1))
    piece_f32 = lambda lead: pltpu.VMEM((lead, half, D), F32)
    piece_bf16 = lambda lead: pltpu.VMEM((lead, half, D), BF16)
    return pl.pallas_call(
        body,
        name="attn_bwd",
        grid=(n_g, n_t),
        in_specs=[hbm, hbm, hbm, hbm, stat, stat, _const(N, 128), _const(N, 128), hbm],
        out_specs=[hbm, pl.BlockSpec((hp, N, 256), lambda g, t: (g, 0, 0), pipeline_mode=pl.Buffered(1)), _const(N, 128),
                   _const(SHARD_OUT, D)],
        out_shape=[
            jax.ShapeDtypeStruct((HEADS, N, 256), BF16), jax.ShapeDtypeStruct((HEADS, N, 256), BF16),
            jax.ShapeDtypeStruct((N, 128), F32), jax.ShapeDtypeStruct((SHARD_OUT, D), F32),
        ],
        scratch_shapes=[pltpu.VMEM((hp, N, 256), BF16), pltpu.VMEM((hp, N, 256), BF16), pltpu.VMEM((hp, N, V_HEAD), BF16),
                        pltpu.VMEM((N, hp * V_HEAD), BF16), pltpu.VMEM((2, hp, TQ, 256), BF16),
                        pltpu.VMEM((hp, N, 256), F32), pltpu.VMEM((hp, N, V_HEAD), F32),
                        piece_f32(CHIPS), piece_f32(CHIPS), piece_bf16(3), piece_bf16(3), pltpu.VMEM((SHARD_OUT, D), F32),
                        pltpu.SemaphoreType.DMA((4, 2)), pltpu.SemaphoreType.DMA((2,)), pltpu.SemaphoreType.DMA((CHIPS,)),
                        pltpu.SemaphoreType.DMA((CHIPS,)), pltpu.SemaphoreType.DMA((CHIPS,)),
                        pltpu.SemaphoreType.DMA((3,)), pltpu.SemaphoreType.DMA((3,)),
                        pltpu.SemaphoreType.DMA((1,)), pltpu.SemaphoreType.DMA((1,))],
        compiler_params=_cparams(dimension_semantics=("arbitrary", "arbitrary"), collective_id=2),
    )(q, k, v, do, lse, delta, cosf, sinf, dwout)


def _bwd_in(h, dh2, dq, dkv, dkr, cq, ckv, dpl, dpg, dag, norm_g, win, gq, wq, gkv, wkv, cosf, sinf, adam_out):
    tr = ROWS_BWD
    nb = N // tr
    per = tr // HALO
    lead = HEAD_ROWS
    adam_rows = SHARD_OUT // nb

    def body(h_ref, dh2_ref, dq_ref, dkv_ref, dkr_ref, cq_ref, ckv_ref, dpl_ref, halo_ref, dpg_ref, dag_ref,
             g_ref, win_ref, gq_ref, wq_ref, gkv_ref, wkv_ref, cos_ref, sin_ref, aw_ref, ag_ref, am_ref, av_ref,
             gx_ref, dmeta_ref, du_ref, hn_ref, dwq_ref, dwkv_ref, dg_ref, dgq_ref, dgkv_ref, ago_ref, ad_ref, anm_ref, anv_ref,
             dh_buf, gx_sem):
        i = pl.program_id(0)
        grad_out = ag_ref[...]
        ago_ref[...] = grad_out
        ad_ref[...], anm_ref[...], anv_ref[...] = _adamw_math(aw_ref[...], grad_out, am_ref[...], av_ref[...])

        @pl.when(i == 0)
        def _():
            dwq_ref[...] = jnp.zeros_like(dwq_ref)
            dwkv_ref[...] = jnp.zeros_like(dwkv_ref)
            dg_ref[...] = jnp.zeros_like(dg_ref)
            dgq_ref[...] = jnp.zeros_like(dgq_ref)
            dgkv_ref[...] = jnp.zeros_like(dgkv_ref)

        row0 = i * tr
        h = h_ref[...]
        r = lax.rsqrt(jnp.mean(h * h, axis=-1, keepdims=True) + EPS)
        n = h * r
        gv = g_ref[...]
        hn = (n * gv).astype(BF16)
        cq = cq_ref[...]
        rq = lax.rsqrt(jnp.mean(cq * cq, axis=-1, keepdims=True) + EPS)
        nq = cq * rq
        gqv = gq_ref[...]
        cqn = (nq * gqv).astype(BF16)
        dcqn = jnp.zeros((tr, Q_LORA), F32)
        for hd in range(HEADS):
            dqf = dq_ref[hd]
            dcqn = dcqn + _nn(dqf, wq_ref[hd])
            dwq_ref[hd] += _tn(dqf, cqn)
        dgq_ref[...] += jnp.sum(dcqn * nq, axis=0, keepdims=True)
        dnq = dcqn * gqv
        dcq = rq * (dnq - nq * jnp.mean(dnq * nq, axis=-1, keepdims=True))

        ckv = ckv_ref[...]
        rkv = lax.rsqrt(jnp.mean(ckv * ckv, axis=-1, keepdims=True) + EPS)
        nkv = ckv * rkv
        gkvv = gkv_ref[...]
        ckvn = (nkv * gkvv).astype(BF16)
        dckvn = jnp.zeros((tr, KV_LORA), F32)
        for hd in range(HEADS):
            dkv = dkv_ref[hd]
            dckvn = dckvn + _nt(dkv, wkv_ref[hd])
            dwkv_ref[hd] += _tn(ckvn, dkv)
        dgkv_ref[...] += jnp.sum(dckvn * nkv, axis=0, keepdims=True)
        dnkv = dckvn * gkvv
        dckv = rkv * (dnkv - nkv * jnp.mean(dnkv * nkv, axis=-1, keepdims=True))
        dkr = _unrope(dkr_ref[...], cos_ref[...], sin_ref[...])

        cur = dpl_ref[...]
        halo = jnp.where(i < nb - 1, halo_ref[...], 0.0)
        dpi = []
        for g, w in enumerate(POOL_WINDOWS):
            sl = slice(g * POOL_GROUP, (g + 1) * POOL_GROUP)
            a = jnp.concatenate([cur[:, sl] * _inv_count(row0, tr, w), halo[:, sl] * _inv_count(row0 + tr, HALO, w)], axis=0)
            acc = a
            shift = 1
            while shift < w:
                acc = acc + pltpu.roll(acc, tr + HALO - shift, 0)
                shift *= 2
            dpi.append(acc[0:tr] - cur[:, sl])

        du = jnp.concatenate([t.astype(BF16) for t in dpi] + [dpg_ref[...]] + [t.astype(BF16) for t in (dcq, dckv, dkr)],
                             axis=1)
        dagb = dag_ref[...]
        for j in range(DU_SLABS):
            du_ref[j] = du[:, j * D_POOL:(j + 1) * D_POOL]
        hn_ref[...] = hn
        dhn = _nn(du, win_ref[0:O_KR_END, :]) + _nn(dagb, win_ref[O_AG:D_IN, :])
        dg_ref[...] += jnp.sum(dhn * n, axis=0, keepdims=True)
        dn = dhn * gv
        dh = dh2_ref[...] + r * (dn - n * jnp.mean(dn * n, axis=-1, keepdims=True))

        first = pltpu.make_async_copy(dh_buf.at[pl.ds(lead, tr - lead), :], gx_ref.at[pl.ds(0, tr - lead), :], gx_sem)
        later = lambda step: pltpu.make_async_copy(
            dh_buf, gx_ref.at[pl.ds(pl.multiple_of(step * tr - lead, 16), tr), :], gx_sem)

        @pl.when(i == 1)
        def _():
            first.wait()

        @pl.when(i > 1)
        def _():
            later(i - 1).wait()

        dh_buf[...] = dh

        @pl.when(i == 0)
        def _():
            first.start()
            for chip in range(CHIPS):
                dmeta_ref[chip] = dh[PAD:HEAD_ROWS, chip * 256:(chip + 1) * 256]

        @pl.when(i > 0)
        def _():
            later(i).start()

        @pl.when(i == nb - 1)
        def _():
            later(i).wait()

    head = lambda w: pl.BlockSpec((HEADS, tr, w), lambda i: (0, i, 0))
    halo_spec = pl.BlockSpec((HALO, D_POOL), lambda i: (jnp.minimum((i + 1) * per, N // HALO - 1), 0))
    return pl.pallas_call(
        body,
        name="bwd_in",
        grid=(nb,),
        in_specs=[
            _rows(D, tr), _rows(D, tr), head(256), head(256), _rows(128, tr), _rows(Q_LORA, tr), _rows(KV_LORA, tr),
            _rows(D_POOL, tr), halo_spec, _rows(D_POOL, tr), _rows(D_POOL, tr),
            _const(1, D), _const(D_IN, D), _const(1, Q_LORA), _const(HEADS, 256, Q_LORA),
            _const(1, KV_LORA), _const(HEADS, KV_LORA, 256), _rows(128, tr), _rows(128, tr),
        ] + [_rows(D, adam_rows)] * 4,
        out_specs=[
            pl.BlockSpec(memory_space=pl.ANY), _const(CHIPS, N_META, 256),
            pl.BlockSpec((DU_SLABS, tr, D_POOL), lambda i: (0, i, 0)), _rows(D, tr),
            _const(HEADS, 256, Q_LORA),
            _const(HEADS, KV_LORA, 256), _const(1, D), _const(1, Q_LORA), _const(1, KV_LORA),
        ] + [_rows(D, adam_rows)] * 4,
        out_shape=[
            jax.ShapeDtypeStruct((S, D), F32), jax.ShapeDtypeStruct((CHIPS, N_META, 256), F32),
            jax.ShapeDtypeStruct((DU_SLABS, N, D_POOL), BF16), jax.ShapeDtypeStruct((N, D), BF16),
            jax.ShapeDtypeStruct((HEADS, 256, Q_LORA), F32),
            jax.ShapeDtypeStruct((HEADS, KV_LORA, 256), F32),
            jax.ShapeDtypeStruct((1, D), F32), jax.ShapeDtypeStruct((1, Q_LORA), F32), jax.ShapeDtypeStruct((1, KV_LORA), F32),
        ] + [jax.ShapeDtypeStruct((SHARD_OUT, D), F32)] * 4,
        scratch_shapes=[pltpu.VMEM((tr, D), F32), pltpu.SemaphoreType.DMA],
        compiler_params=_cparams(dimension_semantics=("arbitrary",)),
    )(h, dh2, dq, dkv, dkr, cq, ckv, dpl, dpl, dpg, dag, norm_g, win, gq, wq, gkv, wkv, cosf, sinf, *adam_out)


def _local_step(h, tgt, norm_g, win, gq, wq, gkv, wkv, pool_w, pool_scale, wout_s, m_wout_s, v_wout_s, gf, cosf, sinf):
    pool_in, pool_gate, cq, ckv, attn_gate, q, k, v = _fwd_in(h, norm_g, win, gq, wq, gkv, wkv, cosf, sinf)
    attn, lse, wout = _attn_fwd(q, k, v, wout_s)
    dh2, do, delta, dag, dpg, dpl, dwout, dpw, dps, dgf, loss = _mid(
        h, tgt, pool_in, pool_gate, attn_gate, attn, pool_w, pool_scale, wout, gf)
    dq, dkv, dkr, gwout = _attn_bwd(q, k, v, do, lse, delta, cosf, sinf, dwout)
    gx, dmeta, du, hn, dwq, dwkv, dg, dgq, dgkv, *r_out = _bwd_in(
        h, dh2, dq, dkv, dkr, cq, ckv, dpl, dpg, dag, norm_g, win, gq, wq, gkv, wkv, cosf, sinf,
        (wout_s, gwout, m_wout_s, v_wout_s))
    return dict(gx=gx, dmeta=dmeta, du=du, dag=dag, hn=hn, dwq=dwq, dwkv=dwkv, r_out=tuple(r_out), dg=dg, dgq=dgq,
                dgkv=dgkv, dpw=dpw, dps=dps, dgf=dgf, loss=loss)


_CHIP_RELS = ((0, 0), (1, 0), (0, 1), (1, 1))

_ARR_ROWS = (SHARD_IN, SHARD_OUT, 256, KV_LORA, N_META)
_ARR_COLS = (D, D, Q_LORA, 256, 256)
_PIECES = (
    (0, 0, 256, 0), (0, 256, SHARD_IN - 256, 1),
    (1, 0, 128, 0), (1, 128, 128, 1),
    (2, 0, 128, 0), (2, 128, 128, 1),
    (3, 0, 64, 0), (3, 64, 64, 1),
    (4, 0, N_META, 0),
)
_NP = len(_PIECES)
_PIECE_MAX = (256, 128, 128, 64, N_META)


def _gathered_at(refs, arr, chip, r0, n):
    if arr in (0, 1):
        return refs[arr].at[pl.ds(pl.multiple_of(_ARR_ROWS[arr] * chip + r0, 16), n), :]
    return refs[arr].at[chip, pl.ds(r0, n), :]


def _remote(src, dst, send_sem, recv_sem, to):
    return pltpu.make_async_remote_copy(src_ref=src, dst_ref=dst, send_sem=send_sem, recv_sem=recv_sem,
                                        device_id=to, device_id_type=MESH)


def _gather_weights(winT_s, wqT_s, wkv_s, meta_s, x2, tgt2):
    arrays = (0, 2, 3, 4)

    def body(win_ref, wq_ref, wkv_ref, meta_ref, x_ref, t_ref, win_o, wq_o, wkv_o, h_o, tp_o,
             s_win, s_wq, s_wkv, meta_all, head_buf, x_buf, t_buf, ici_send, ici_recv, fwd_send, fwd_recv,
             loc_sems, own_sems):
        x, y, c = lax.axis_index("x"), lax.axis_index("y"), lax.axis_index("c")
        me = 2 * x + y
        stage = (s_win, None, s_wq, s_wkv, meta_ref)
        outs = (win_o, None, wq_o, wkv_o, meta_all)

        _peer_signal(x, y, c)

        frames = pl.ds(HEAD_ROWS, S)
        loads = [pltpu.make_async_copy(x_ref, x_buf, loc_sems.at[0]), pltpu.make_async_copy(t_ref, t_buf, loc_sems.at[1])]
        local = [pltpu.make_async_copy(x_buf, h_o.at[frames, :], loc_sems.at[0]),
                 pltpu.make_async_copy(t_buf, tp_o.at[frames, :], loc_sems.at[1])]
        for cp in loads:
            cp.start()

        s_win[...] = win_ref[...].astype(BF16)
        s_wq[0:QK, :] = wq_ref[...].astype(BF16)
        s_wq[QK:256, :] = jnp.zeros((256 - QK, Q_LORA), BF16)
        s_wkv[...] = wkv_ref[...].astype(BF16)
        head_buf[...] = jnp.zeros_like(head_buf)
        zeros = pltpu.make_async_copy(head_buf, tp_o.at[pl.ds(0, HEAD_ROWS), :], loc_sems.at[2])
        zeros.start()

        def chip_of(rel):
            fx, fy = _CHIP_RELS[rel]
            return 2 * (x ^ fx) + (y ^ fy)

        def same_core_of(rel):
            fx, fy = _CHIP_RELS[rel]
            return (x ^ fx, y ^ fy, c)

        def ici_copy(rel, i, src_chip, to):
            arr, r0, n, _ = _PIECES[i]
            k = (rel - 1) * _NP + i
            return _remote(stage[arr].at[pl.ds(r0, n), :], _gathered_at(outs, arr, src_chip, r0, n),
                           ici_send.at[k], ici_recv.at[k], to)

        def fwd_copy(rel, i, to):
            arr, r0, n, _ = _PIECES[i]
            k = (rel - 1) * _NP + i
            place = _gathered_at(outs, arr, chip_of(rel), r0, n)
            return _remote(place, place, fwd_send.at[k], fwd_recv.at[k], to)

        _peer_wait()
        for core in (0, 1):
            @pl.when(c == core)
            def _(core=core):
                mine = [i for i in range(_NP) if _PIECES[i][3] == core and _PIECES[i][0] in arrays]
                theirs = [i for i in range(_NP) if _PIECES[i][3] != core and _PIECES[i][0] in arrays]
                sends = [ici_copy(rel, i, me, same_core_of(rel)) for rel in (1, 2, 3) for i in mine]
                for cp in sends:
                    cp.start()
                for ld, st in zip(loads, local):
                    ld.wait()
                    st.start()
                own = [pltpu.make_async_copy(stage[arr], _gathered_at(outs, arr, me, 0, _ARR_ROWS[arr]), own_sems.at[arr])
                       for arr in arrays if arr != 4]
                for cp in own:
                    cp.start()
                meta_all[me] = meta_ref[...]
                for rel in (1, 2, 3):
                    for i in mine:
                        ici_copy(rel, i, chip_of(rel), (x, y, c)).wait_recv()
                        fwd = fwd_copy(rel, i, (x, y, 1 - c))
                        fwd.start()
                        sends.append(fwd)
                for rel in (1, 2, 3):
                    for i in theirs:
                        fwd_copy(rel, i, (x, y, c)).wait_recv()
                for cp in sends:
                    cp.wait_send()
                for cp in own:
                    cp.wait()

        zeros.wait()
        for chip in range(CHIPS):
            head_buf[PAD:HEAD_ROWS, chip * 256:(chip + 1) * 256] = meta_all[chip]
        head = pltpu.make_async_copy(head_buf, h_o.at[pl.ds(0, HEAD_ROWS), :], loc_sems.at[2])
        head.start()
        head.wait()
        for cp in local:
            cp.wait()

    vm = pl.BlockSpec(memory_space=pltpu.VMEM)
    hbm = pl.BlockSpec(memory_space=pl.ANY)
    return pl.pallas_call(
        body,
        name="gather_weights",
        in_specs=[vm] * 4 + [hbm] * 2,
        out_specs=[hbm] * 5,
        out_shape=[
            jax.ShapeDtypeStruct((D_IN, D), BF16),
            jax.ShapeDtypeStruct((CHIPS, 256, Q_LORA), BF16), jax.ShapeDtypeStruct((CHIPS, KV_LORA, 256), BF16),
            jax.ShapeDtypeStruct((N, D), F32), jax.ShapeDtypeStruct((N, D), F32),
        ],
        scratch_shapes=[pltpu.VMEM((_ARR_ROWS[a], _ARR_COLS[a]), BF16) for a in (0, 2, 3)]
        + [pltpu.VMEM((CHIPS, N_META, 256), F32), pltpu.VMEM((HEAD_ROWS, D), F32), pltpu.VMEM((S, D), F32),
           pltpu.VMEM((S, D), F32)]
        + [pltpu.SemaphoreType.DMA((3 * _NP,))] * 4 + [pltpu.SemaphoreType.DMA((3,)), pltpu.SemaphoreType.DMA((4,))],
        compiler_params=_cparams(collective_id=0),
    )(winT_s, wqT_s, wkv_s, meta_s, x2, tgt2)


_SM_ROWS = (len(POOL_WINDOWS) * POOL_GROUP, VEC_ROWS)
_SM_COLS = (POOL_GROUP, D)
_SM_PIECES = ((0, 0, 256, 0), (0, 256, 256, 1), (1, 0, VEC_ROWS, 0))
_NSP = len(_SM_PIECES)


def _reduce_grads(du, dag, hn, dwq, dwkv, dmeta4, dpw, dg, dgf, dgq, dgkv, dps, loss):
    arrays = (0, 2, 3, 4)
    loaded = (2, 3, 4)
    blocks = ([(0, 256), (256, 512)], [(512, 768), (768, 1024)], [(1024, 1280), (1280, O_AG), (O_AG, O_AG + 256)],
              [(O_AG + 256, D_IN)])

    def body(du_hbm, dag_hbm, hn_hbm, dwq_ref, dwkv_ref, dmeta_ref, dpw_ref, dg_ref, dgf_ref, dgq_ref, dgkv_ref, dps_ref,
             loss_ref, gwin_o, gwq_o, gwkv_o, gmeta_o, gpw_o, gg_o, ggf_o, ggq_o, ggkv_o, gps_o, gloss_o,
             ow2, ow3, ow4, sb0, sb2, sb3, sb4, st0, st2, st3, st4, rc0, rc2, rc3, rc4,
             vec, sm_sb0, sm_sb1, sm_cs0, sm_cs1, sm_rc0, sm_rc1, vec_fin, du_v, dag_v, hn_v, dwin_buf, own0,
             own_sems, d2d_send, d2d_recv, ici_send, ici_recv, fin_send, fin_recv,
             swap_send, swap_recv, smi_send, smi_recv, smf_send, smf_recv, ld_sems):
        x, y, c = lax.axis_index("x"), lax.axis_index("y"), lax.axis_index("c")
        me = 2 * x + y
        _peer_signal(x, y, c)
        operands = [pltpu.make_async_copy(src, dst, ld_sems.at[t]) for t, (src, dst) in enumerate(
            [(hn_hbm, hn_v)] + [(du_hbm.at[j], du_v.at[j]) for j in range(DU_SLABS)] + [(dag_hbm, dag_v)])]
        for cp in operands[0:2]:
            cp.start()
        grads = (None, None, dwq_ref, dwkv_ref, dmeta_ref)
        outs = (gwin_o, None, gwq_o, gwkv_o, gmeta_o)
        own_buf = (None, None, ow2, ow3, ow4)
        sib_buf = (sb0, None, sb2, sb3, sb4)
        stage = (st0, None, st2, st3, st4)
        recv = (rc0, None, rc2, rc3, rc4)
        sm_mine = (dpw_ref, vec)
        sm_sib = (sm_sb0, sm_sb1)
        sm_chip = (sm_cs0, sm_cs1)
        sm_recv = (sm_rc0, sm_rc1)
        sm_out = (gpw_o, vec_fin)
        sibling = (x, y, 1 - c)

        def chip_of(rel):
            fx, fy = _CHIP_RELS[rel]
            return 2 * (x ^ fx) + (y ^ fy)

        def same_core_of(rel):
            fx, fy = _CHIP_RELS[rel]
            return (x ^ fx, y ^ fy, c)

        def slot(bufs, i, idx):
            arr, _, n, _ = _PIECES[i]
            return bufs[arr].at[idx, pl.ds(0, n), :]

        def own_load(rel, i):
            arr, r0, n, _ = _PIECES[i]
            return pltpu.make_async_copy(_gathered_at(grads, arr, chip_of(rel), r0, n), slot(own_buf, i, rel),
                                         own_sems.at[rel * _NP + i])

        def d2d_copy(rel, i):
            arr, r0, n, _ = _PIECES[i]
            k = rel * _NP + i
            return _remote(_gathered_at(grads, arr, chip_of(rel), r0, n), slot(sib_buf, i, rel),
                           d2d_send.at[k], d2d_recv.at[k], sibling)

        def ici_copy(rel, i):
            k = (rel - 1) * _NP + i
            return _remote(slot(stage, i, rel - 1), slot(recv, i, rel - 1), ici_send.at[k], ici_recv.at[k],
                           same_core_of(rel))

        def fin_copy(i):
            arr, r0, n, _ = _PIECES[i]
            place = outs[arr].at[pl.ds(r0, n), :]
            return _remote(place, place, fin_send.at[i], fin_recv.at[i], sibling)

        def sm_ici_copy(rel, j):
            blk, r0, n, _ = _SM_PIECES[j]
            k = (rel - 1) * _NSP + j
            return _remote(sm_chip[blk].at[pl.ds(r0, n), :], sm_recv[blk].at[rel - 1, pl.ds(r0, n), :],
                           smi_send.at[k], smi_recv.at[k], same_core_of(rel))

        def sm_fin_copy(j):
            blk, r0, n, _ = _SM_PIECES[j]
            place = sm_out[blk].at[pl.ds(r0, n), :]
            return _remote(place, place, smf_send.at[j], smf_recv.at[j], sibling)

        vec[...] = jnp.zeros_like(vec)
        vec[0:1, :] = dg_ref[...]
        vec[1:2, :] = dgf_ref[...]
        vec[2:3, V_GQ:V_GQ + Q_LORA] = dgq_ref[...]
        vec[2:3, V_GKV:V_GKV + KV_LORA] = dgkv_ref[...]
        vec[2:3, V_PS:V_PS + D_POOL] = dps_ref[...]
        vec[2:3, V_LOSS:D] = loss_ref[...]
        _peer_wait()
        swaps = [_remote(sm_mine[b], sm_sib[b], swap_send.at[b], swap_recv.at[b], sibling) for b in (0, 1)]
        for cp in swaps:
            cp.start()

        for core in (0, 1):
            @pl.when(c == core)
            def _(core=core):
                mine = [i for i in range(_NP) if _PIECES[i][3] == core and _PIECES[i][0] in loaded]
                theirs = [i for i in range(_NP) if _PIECES[i][3] != core and _PIECES[i][0] in loaded]
                i0 = next(i for i in range(_NP) if _PIECES[i][0] == 0 and _PIECES[i][3] == core)
                j0 = next(i for i in range(_NP) if _PIECES[i][0] == 0 and _PIECES[i][3] != core)
                sm_mine_p = [j for j in range(_NSP) if _SM_PIECES[j][3] == core]
                sm_theirs_p = [j for j in range(_NSP) if _SM_PIECES[j][3] != core]
                sends = list(swaps)

                for rel in (1, 2, 3, 0):
                    for i in theirs:
                        cp = d2d_copy(rel, i)
                        cp.start()
                        sends.append(cp)
                    for i in mine:
                        own_load(rel, i).start()

                def rel_of(chip):
                    flips = chip ^ me
                    return jnp.where(flips == 2, 1, jnp.where(flips == 1, 2, flips))

                def shard_rows(chip, i):
                    return pl.ds(SHARD_IN * chip + _PIECES[i][1], _PIECES[i][2])

                def d2d0(chip, i):
                    rel = rel_of(chip)
                    return _remote(dwin_buf.at[shard_rows(chip, i), :], slot(sib_buf, i, rel),
                                   d2d_send.at[rel * _NP + i], d2d_recv.at[rel * _NP + i], sibling)

                def ici0(chip):
                    slot_idx = jnp.maximum(rel_of(chip) - 1, 0)
                    return _remote(slot(stage, i0, slot_idx), slot(recv, i0, slot_idx), ici_send.at[slot_idx * _NP + i0],
                                   ici_recv.at[slot_idx * _NP + i0], (chip // 2, chip % 2, c))

                def settle(chip):
                    d2d0(chip, i0).wait_recv()
                    total = dwin_buf[shard_rows(chip, i0), :] + slot(sib_buf, i0, rel_of(chip))[...]

                    @pl.when(chip != me)
                    def _():
                        slot(stage, i0, jnp.maximum(rel_of(chip) - 1, 0))[...] = total.astype(BF16)
                        ici0(chip).start()

                    @pl.when(chip == me)
                    def _():
                        own0[0:_PIECES[i0][2], :] = total

                for cp in operands[0:2]:
                    cp.wait()
                for cp in operands[2:]:
                    cp.start()
                needs = {1: operands[2:3], 2: operands[3:]}
                for chip in range(CHIPS):
                    for cp in needs.get(chip, ()):
                        cp.wait()
                    for lo, hi in blocks[chip]:
                        if lo < O_AG:
                            col = lo % D_POOL
                            dwin_buf[lo:hi, :] = _tn(du_v[lo // D_POOL, :, col:col + 256], hn_v[...])[0:hi - lo, :]
                        else:
                            dwin_buf[lo:hi, :] = _tn(dag_v[:, lo - O_AG:hi - O_AG], hn_v[...])
                    cp = d2d0(chip, j0)
                    cp.start()
                    sends.append(cp)
                    if chip > 0:
                        settle(chip - 1)
                settle(CHIPS - 1)

                for rel in (1, 2, 3):
                    for i in mine:
                        arr, r0, n, _ = _PIECES[i]
                        own_load(rel, i).wait()
                        d2d_copy(rel, i).wait_recv()
                        total = slot(own_buf, i, rel)[...] + slot(sib_buf, i, rel)[...]
                        slot(stage, i, rel - 1)[...] = total.astype(stage[arr].dtype)
                        cp = ici_copy(rel, i)
                        cp.start()
                        sends.append(cp)

                for b in (0, 1):
                    swaps[b].wait_recv()
                    sm_chip[b][...] = sm_mine[b][...] + sm_sib[b][...]
                for rel in (1, 2, 3):
                    for j in sm_mine_p:
                        cp = sm_ici_copy(rel, j)
                        cp.start()
                        sends.append(cp)

                for i in mine:
                    arr, r0, n, _ = _PIECES[i]
                    own_load(0, i).wait()
                    d2d_copy(0, i).wait_recv()
                    total = slot(own_buf, i, 0)[...] + slot(sib_buf, i, 0)[...]
                    for rel in (1, 2, 3):
                        ici_copy(rel, i).wait_recv()
                        total = total + slot(recv, i, rel - 1)[...].astype(F32)
                    outs[arr][pl.ds(r0, n), :] = total
                    cp = fin_copy(i)
                    cp.start()
                    sends.append(cp)
                total = own0[0:_PIECES[i0][2], :]
                for rel in (1, 2, 3):
                    ici_copy(rel, i0).wait_recv()
                    total = total + slot(recv, i0, rel - 1)[...].astype(F32)
                outs[0][pl.ds(_PIECES[i0][1], _PIECES[i0][2]), :] = total
                cp = fin_copy(i0)
                cp.start()
                sends.append(cp)

                for j in sm_mine_p:
                    blk, r0, n, _ = _SM_PIECES[j]
                    for rel in (1, 2, 3):
                        sm_ici_copy(rel, j).wait_recv()
                    total = jnp.zeros((n, _SM_COLS[blk]), F32)
                    for chip in range(CHIPS):
                        flips = chip ^ me
                        rel = jnp.where(flips == 2, 1, jnp.where(flips == 1, 2, flips))
                        theirs_rows = sm_recv[blk][jnp.maximum(rel - 1, 0), pl.ds(r0, n), :]
                        total = total + jnp.where(rel == 0, sm_chip[blk][pl.ds(r0, n), :], theirs_rows)
                    sm_out[blk][pl.ds(r0, n), :] = total
                    cp = sm_fin_copy(j)
                    cp.start()
                    sends.append(cp)

                for i in theirs + [j0]:
                    fin_copy(i).wait_recv()
                for j in sm_theirs_p:
                    sm_fin_copy(j).wait_recv()
                for cp in sends:
                    cp.wait_send()
                for chip in range(CHIPS):
                    @pl.when(chip != me)
                    def _(chip=chip):
                        ici0(chip).wait_send()

        gg_o[...] = vec_fin[0:1, :]
        ggf_o[...] = vec_fin[1:2, :]
        ggq_o[...] = vec_fin[2:3, V_GQ:V_GQ + Q_LORA]
        ggkv_o[...] = vec_fin[2:3, V_GKV:V_GKV + KV_LORA]
        gps_o[...] = vec_fin[2:3, V_PS:V_PS + D_POOL]
        gloss_o[...] = vec_fin[2:3, V_LOSS:D]

    vm = pl.BlockSpec(memory_space=pltpu.VMEM)
    piece_buf = lambda lead, dtype, which=arrays: [
        pltpu.VMEM((lead, _PIECE_MAX[a], _ARR_COLS[a]), F32 if a == 4 else dtype) for a in which]
    sm_buf = lambda *lead: [pltpu.VMEM(lead + (_SM_ROWS[b], _SM_COLS[b]), F32) for b in (0, 1)]
    dma = lambda n: [pltpu.SemaphoreType.DMA((n,))] * 2
    return pl.pallas_call(
        body,
        name="reduce_grads",
        in_specs=[pl.BlockSpec(memory_space=pl.ANY)] * 5 + [vm] * 8,
        out_specs=[vm] * 11,
        out_shape=[jax.ShapeDtypeStruct((_ARR_ROWS[a], _ARR_COLS[a]), F32) for a in arrays]
        + [jax.ShapeDtypeStruct((_SM_ROWS[0], _SM_COLS[0]), F32), jax.ShapeDtypeStruct((1, D), F32),
           jax.ShapeDtypeStruct((1, D), F32), jax.ShapeDtypeStruct((1, Q_LORA), F32),
           jax.ShapeDtypeStruct((1, KV_LORA), F32), jax.ShapeDtypeStruct((1, D_POOL), F32),
           jax.ShapeDtypeStruct((1, 128), F32)],
        scratch_shapes=piece_buf(CHIPS, F32, loaded) + piece_buf(CHIPS, F32) + piece_buf(3, BF16) + piece_buf(3, BF16)
        + [pltpu.VMEM((VEC_ROWS, D), F32)] + sm_buf() + sm_buf() + sm_buf(3) + [pltpu.VMEM((VEC_ROWS, D), F32)]
        + [pltpu.VMEM((DU_SLABS, N, D_POOL), BF16), pltpu.VMEM((N, D_POOL), BF16), pltpu.VMEM((N, D), BF16),
           pltpu.VMEM((D_IN, D), F32), pltpu.VMEM((_PIECE_MAX[0], D), F32)]
        + [pltpu.SemaphoreType.DMA((CHIPS * _NP,))]
        + dma(CHIPS * _NP) + dma(3 * _NP) + dma(_NP) + dma(2) + dma(3 * _NSP) + dma(_NSP)
        + [pltpu.SemaphoreType.DMA((DU_SLABS + 2,))],
        compiler_params=_cparams(collective_id=3),
    )(du, dag, hn, dwq, dwkv, dmeta4, dpw, dg, dgf, dgq, dgkv, dps, loss)


def _adamw_math(w, g, m, v):
    m = B1 * m + (1.0 - B1) * g
    v = B2 * v + (1.0 - B2) * (g * g)
    m_hat = m / C1
    v_hat = v / C2
    delta = -LR * (m_hat / (jnp.sqrt(v_hat) + ADAM_EPS) + WD * w)
    return delta, m, v


def _adamw_rows(name, w, g, m, v, block_rows):
    rows, cols = w.shape

    def body(w_ref, g_ref, m_ref, v_ref, go_ref, d_ref, nm_ref, nv_ref):
        g = g_ref[...]
        go_ref[...] = g
        d_ref[...], nm_ref[...], nv_ref[...] = _adamw_math(w_ref[...], g, m_ref[...], v_ref[...])

    spec = pl.BlockSpec((block_rows, cols), lambda i: (i, 0))
    return pl.pallas_call(
        body,
        name=name,
        grid=(rows // block_rows,),
        in_specs=[spec] * 4,
        out_specs=[spec] * 4,
        out_shape=[jax.ShapeDtypeStruct(w.shape, F32)] * 4,
        compiler_params=_cparams(dimension_semantics=("arbitrary",)),
    )(w, g, m, v)


def _adamw_small(groups):
    n = len(groups)

    def body(*refs):
        ins, outs = refs[:4 * n], refs[4 * n:]
        for t in range(n):
            w_ref, g_ref, m_ref, v_ref = ins[4 * t:4 * t + 4]
            g = g_ref[0:w_ref.shape[0], :]
            outs[4 * t][...] = g
            outs[4 * t + 1][...], outs[4 * t + 2][...], outs[4 * t + 3][...] = _adamw_math(
                w_ref[...], g, m_ref[...], v_ref[...])

    vm = pl.BlockSpec(memory_space=pltpu.VMEM)
    flat = [a for grp in groups for a in grp]
    outs = pl.pallas_call(
        body,
        name="adamw_small",
        in_specs=[vm] * (4 * n),
        out_specs=[vm] * (4 * n),
        out_shape=[jax.ShapeDtypeStruct(grp[0].shape, F32) for grp in groups for _ in range(4)],
        compiler_params=_cparams(),
    )(*flat)
    return [tuple(outs[4 * t:4 * t + 4]) for t in range(n)]


def _rope_tables():
    half = QK_ROPE // 2
    f32 = np.float32
    inv_freq = (f32(1.0) / (f32(ROPE_THETA) ** (np.arange(half, dtype=f32) / f32(half)))).astype(f32)
    pos = np.arange(N, dtype=f32) - f32(PAD)
    ang = (pos[:, None] * inv_freq[None, :]).astype(f32)
    cos, sin = np.cos(ang).astype(f32), np.sin(ang).astype(f32)
    zero = np.zeros((N, 128 - QK_ROPE), f32)
    return jnp.asarray(np.concatenate([cos, cos, zero], axis=1)), jnp.asarray(np.concatenate([-sin, sin, zero], axis=1))


def kernel(x, meta_tokens, norm_g, w_in, q_norm_g, w_q_b, kv_norm_g, w_kv_b, pool_w, pool_scale, w_out, final_norm_g, loss_target, m_meta_tokens, m_norm_g, m_w_in, m_q_norm_g, m_w_q_b, m_kv_norm_g, m_w_kv_b, m_pool_w, m_pool_scale, m_w_out, m_final_norm_g, v_meta_tokens, v_norm_g, v_w_in, v_q_norm_g, v_w_q_b, v_kv_norm_g, v_w_kv_b, v_pool_w, v_pool_scale, v_w_out, v_final_norm_g):
    tr = lambda a: a[0].T
    win, wq, wkv, h, tgt = _gather_weights(tr(w_in), tr(w_q_b), w_kv_b[0], meta_tokens, x[0], loss_target[0])
    cosf, sinf = _rope_tables()
    gf = final_norm_g.reshape(1, D)

    part = _local_step(h, tgt, norm_g, win, q_norm_g, wq, kv_norm_g, wkv, pool_w[0], pool_scale, w_out[0], m_w_out[0],
                       v_w_out[0], gf, cosf, sinf)

    pw2 = lambda a: a.reshape(len(POOL_WINDOWS) * POOL_GROUP, POOL_GROUP)
    gwinT, gwqT, gwkv, gmeta, gpw, gg, ggf, ggq, ggkv, gps, gloss = _reduce_grads(
        part["du"], part["dag"], part["hn"], part["dwq"], part["dwkv"], part["dmeta"], pw2(part["dpw"]), part["dg"],
        part["dgf"], part["dgq"], part["dgkv"], part["dps"], part["loss"])

    r_in = _adamw_rows("adamw_w_in", tr(w_in), gwinT, tr(m_w_in), tr(v_w_in), 248)
    r_out = part["r_out"]
    fn2 = lambda a: a.reshape(1, D)
    r_meta, r_norm, r_gq, r_wq, r_gkv, r_wkv, r_pw, r_ps, r_fn = _adamw_small([
        (meta_tokens, gmeta, m_meta_tokens, v_meta_tokens),
        (norm_g, gg, m_norm_g, v_norm_g),
        (q_norm_g, ggq, m_q_norm_g, v_q_norm_g),
        (tr(w_q_b), gwqT, tr(m_w_q_b), tr(v_w_q_b)),
        (kv_norm_g, ggkv, m_kv_norm_g, v_kv_norm_g),
        (w_kv_b[0], gwkv, m_w_kv_b[0], v_w_kv_b[0]),
        (pw2(pool_w), gpw, pw2(m_pool_w), pw2(v_pool_w)),
        (pool_scale, gps, m_pool_scale, v_pool_scale),
        (fn2(final_norm_g), ggf, fn2(m_final_norm_g), fn2(v_final_norm_g)),
    ])
    untr = lambda a: a.T[None]
    pw4 = lambda a: a.reshape(1, len(POOL_WINDOWS), POOL_GROUP, POOL_GROUP)
    per_kind = [[
        r_meta[kind], r_norm[kind], untr(r_in[kind]), r_gq[kind], untr(r_wq[kind]), r_gkv[kind], r_wkv[kind][None],
        pw4(r_pw[kind]), r_ps[kind], r_out[kind][None], r_fn[kind].reshape(D),
    ] for kind in range(4)]
    return (gloss[0, 0], part["gx"][None], *per_kind[0], *per_kind[1], *per_kind[2], *per_kind[3])
```

```python
import jax
import jax.numpy as jnp
import numpy as np
from jax import lax
from jax.experimental import pallas as pl
from jax.experimental.pallas import tpu as pltpu

F32 = jnp.float32
BF16 = jnp.bfloat16

D = 1024
S = 2048
N_META = 16
PAD = 112
HEAD_ROWS = PAD + N_META
N = HEAD_ROWS + S
D_POOL = 512
POOL_WINDOWS = (2, 4, 8, 16)
POOL_GROUP = 128
HALO = 16
HEADS = 4
QK_NOPE = 128
QK_ROPE = 64
QK = QK_NOPE + QK_ROPE
V_HEAD = 128
Q_LORA = 256
KV_LORA = 128
D_IN = 1984
EPS = 1e-6
ROPE_THETA = 10000.0
SCALE = QK ** -0.5
CHIPS = 4

ROWS_FWD = 544
ROWS_MID = 544
ROWS_BWD = 544
TK = 128
TQ = 256
NQ = S // TQ
HEADS_PER_STEP_BWD = 2

O_PI, O_PG, O_CQ, O_CKV, O_KR, O_AG = 0, 512, 1024, 1280, 1408, 1472
O_KR_END = O_KR + 128
SHARD_IN = D_IN // CHIPS
SHARD_PAD = 512
SHARD_OUT = D // CHIPS

LR, B1, B2, ADAM_EPS, WD, STEP = 0.001, 0.9, 0.999, 1e-08, 0.01, 10
C1 = 1.0 - B1**STEP
C2 = 1.0 - B2**STEP

VMEM_LIMIT = 60 * 1024 * 1024
MESH = pl.DeviceIdType.MESH
NEG = -1e30

VEC_ROWS = 8
V_GQ, V_GKV, V_PS, V_LOSS = 0, 256, 384, 896


def _cparams(**kw):
    return pltpu.CompilerParams(vmem_limit_bytes=VMEM_LIMIT, **kw)


def _nt(a, b):
    return lax.dot_general(a, b, (((1,), (1,)), ((), ())), preferred_element_type=F32)


def _tn(a, b):
    return lax.dot_general(a, b, (((0,), (0,)), ((), ())), preferred_element_type=F32)


def _nn(a, b):
    return jnp.dot(a, b, preferred_element_type=F32)


def _swap64(t):
    return pltpu.roll(t, 32, 1) + pltpu.roll(t, 96, 1)


def _sigmoid(x):
    return 1.0 / (1.0 + jnp.exp(-x))


def _low_lanes():
    return (lax.broadcasted_iota(jnp.int32, (1, 128), 1) < QK_ROPE).astype(F32)


def _rows(w, rows):
    return pl.BlockSpec((rows, w), lambda i: (i, 0))


def _const(*shape):
    return pl.BlockSpec(shape, lambda *_: (0,) * len(shape), pipeline_mode=pl.Buffered(1))


STAT_GROUPS = HEADS // HEADS_PER_STEP_BWD


def _stat_slot(head):
    return head // HEADS_PER_STEP_BWD, head % HEADS_PER_STEP_BWD


N_PEERS = 4


def _peer_signal(x, y, c):
    barrier = pltpu.get_barrier_semaphore()
    peers = [(x, y, 1 - c)] + [(x ^ fx, y ^ fy, c) for fx, fy in _CHIP_RELS[1:]]
    assert len(peers) == N_PEERS
    for peer in peers:
        pl.semaphore_signal(barrier, inc=1, device_id=peer, device_id_type=MESH)


def _peer_wait():
    pl.semaphore_wait(pltpu.get_barrier_semaphore(), N_PEERS)


def _attn_tiles():
    return [(0, TK, TK)] + [(TK + TQ * t, TQ, TK + TQ * (t + 1)) for t in range(NQ)]


def _masked_scores(q, k, rows, klen):
    s = _nt(q, k)
    col = lax.broadcasted_iota(jnp.int32, (1, TK), 1)
    head_bias = jnp.where(col >= PAD, 0.0, NEG)
    if klen == TK:
        return s + head_bias
    r = lax.broadcasted_iota(jnp.int32, (rows, 1), 0) >> 6
    c = lax.broadcasted_iota(jnp.int32, (1, rows), 1) >> 6
    diag_bias = jnp.where(c <= r, 0.0, NEG)
    parts = [s[:, 0:TK] + head_bias]
    if klen - rows > TK:
        parts.append(s[:, TK:klen - rows])
    parts.append(s[:, klen - rows:klen] + diag_bias)
    return jnp.concatenate(parts, axis=1)


def _fwd_in(h, norm_g, win, gq, wq, gkv, wkv, cosf, sinf):
    tr = ROWS_FWD

    def body(h_ref, g_ref, win_ref, gq_ref, wq_ref, gkv_ref, wkv_ref, cos_ref, sin_ref,
             pi_ref, pg_ref, cq_ref, ckv_ref, ag_ref, q_ref, k_ref, v_ref):
        h = h_ref[...]
        r = lax.rsqrt(jnp.mean(h * h, axis=-1, keepdims=True) + EPS)
        hn = ((h * r) * g_ref[...]).astype(BF16)
        u = _nt(hn, win_ref[0:O_KR_END, :])
        pi_ref[...] = u[:, O_PI:O_PG]
        pg_ref[...] = u[:, O_PG:O_CQ]
        cq = u[:, O_CQ:O_CKV]
        ckv = u[:, O_CKV:O_KR]
        cq_ref[...] = cq
        ckv_ref[...] = ckv
        ag_ref[...] = _nt(hn, win_ref[O_AG:D_IN, :])
        cosv = cos_ref[...]
        sinv = sin_ref[...]
        kr = u[:, O_KR:O_KR_END] * _low_lanes()
        kr = (kr * cosv + _swap64(kr) * sinv).astype(BF16)
        rq = lax.rsqrt(jnp.mean(cq * cq, axis=-1, keepdims=True) + EPS)
        cqn = ((cq * rq) * gq_ref[...]).astype(BF16)
        rkv = lax.rsqrt(jnp.mean(ckv * ckv, axis=-1, keepdims=True) + EPS)
        ckvn = ((ckv * rkv) * gkv_ref[...]).astype(BF16)
        for hd in range(HEADS):
            qh = _nt(cqn, wq_ref[hd]) * SCALE
            z = qh[:, QK_NOPE:]
            q_ref[hd, :, 0:QK_NOPE] = qh[:, 0:QK_NOPE].astype(BF16)
            q_ref[hd, :, QK_NOPE:] = (z * cosv + _swap64(z) * sinv).astype(BF16)
            kvh = _nn(ckvn, wkv_ref[hd])
            k_ref[hd, :, 0:QK_NOPE] = kvh[:, 0:QK_NOPE].astype(BF16)
            k_ref[hd, :, QK_NOPE:] = kr
            v_ref[hd] = kvh[:, QK_NOPE:].astype(BF16)

    head = lambda w: pl.BlockSpec((HEADS, tr, w), lambda i: (0, i, 0))
    return pl.pallas_call(
        body,
        name="fwd_in",
        grid=(N // tr,),
        in_specs=[
            _rows(D, tr), _const(1, D), _const(D_IN, D), _const(1, Q_LORA), _const(HEADS, 256, Q_LORA),
            _const(1, KV_LORA), _const(HEADS, KV_LORA, 256), _rows(128, tr), _rows(128, tr),
        ],
        out_specs=[_rows(D_POOL, tr), _rows(D_POOL, tr), _rows(Q_LORA, tr), _rows(KV_LORA, tr), _rows(D_POOL, tr),
                   head(256), head(256), head(V_HEAD)],
        out_shape=[
            jax.ShapeDtypeStruct((N, D_POOL), F32), jax.ShapeDtypeStruct((N, D_POOL), F32),
            jax.ShapeDtypeStruct((N, Q_LORA), F32), jax.ShapeDtypeStruct((N, KV_LORA), F32),
            jax.ShapeDtypeStruct((N, D_POOL), F32),
            jax.ShapeDtypeStruct((HEADS, N, 256), BF16), jax.ShapeDtypeStruct((HEADS, N, 256), BF16),
            jax.ShapeDtypeStruct((HEADS, N, V_HEAD), BF16),
        ],
        compiler_params=_cparams(dimension_semantics=("arbitrary",)),
    )(h, norm_g, win, gq, wq, gkv, wkv, cosf, sinf)


def _attn_fwd(q, k, v, wout_s):
    tiles = _attn_tiles()
    n_t = len(tiles)
    half = SHARD_OUT // 2
    send_step = 2
    fwd_step = n_t - 2

    def body(q_hbm, k_hbm, v_hbm, wout_ref, o_hbm, lse_ref, wout_o, q_buf, k_buf, v_buf, o_buf, s_wout, in_sems, out_sems,
             ici_send, ici_recv, fwd_send, fwd_recv, own_sem):
        step = pl.program_id(0)
        x, y, c = lax.axis_index("x"), lax.axis_index("y"), lax.axis_index("c")
        me = 2 * x + y

        def chip_of(rel):
            fx, fy = _CHIP_RELS[rel]
            return 2 * (x ^ fx) + (y ^ fy)

        def place(chip, core):
            return wout_o.at[pl.ds(pl.multiple_of(SHARD_OUT * chip + half * core, half), half), :]

        def ici_copy(rel, src_chip, to):
            return _remote(s_wout.at[pl.ds(pl.multiple_of(half * c, half), half), :], place(src_chip, c),
                           ici_send.at[rel - 1], ici_recv.at[rel - 1], to)

        def fwd_copy(rel, core, to):
            spot = place(chip_of(rel), core)
            return _remote(spot, spot, fwd_send.at[rel - 1], fwd_recv.at[rel - 1], to)

        own = pltpu.make_async_copy(s_wout, wout_o.at[pl.ds(pl.multiple_of(SHARD_OUT * me, SHARD_OUT), SHARD_OUT), :], own_sem)

        @pl.when(step == 0)
        def _():
            _peer_signal(x, y, c)
            s_wout[...] = wout_ref[...].astype(BF16)
            own.start()

        @pl.when(step == send_step)
        def _():
            _peer_wait()
            for rel in (1, 2, 3):
                fx, fy = _CHIP_RELS[rel]
                ici_copy(rel, me, (x ^ fx, y ^ fy, c)).start()

        @pl.when(step == fwd_step)
        def _():
            for rel in (1, 2, 3):
                ici_copy(rel, chip_of(rel), (x, y, c)).wait_recv()
                fwd_copy(rel, c, (x, y, 1 - c)).start()

        def finish_wout():
            for rel in (1, 2, 3):
                fwd_copy(rel, 1 - c, (x, y, c)).wait_recv()
            for rel in (1, 2, 3):
                ici_copy(rel, me, (x, y, c)).wait_send()
                fwd_copy(rel, c, (x, y, c)).wait_send()
            own.wait()

        def loads(idx):
            q0, rows, _ = tiles[idx]
            rs = pl.ds(q0, rows)
            return [pltpu.make_async_copy(src.at[:, rs, :], dst.at[:, rs, :], in_sems.at[a, idx % 2])
                    for a, (src, dst) in enumerate(((q_hbm, q_buf), (k_hbm, k_buf), (v_hbm, v_buf)))]

        def store(idx):
            q0, rows, _ = tiles[idx]
            return pltpu.make_async_copy(o_buf.at[idx % 2, pl.ds(0, rows), :], o_hbm.at[pl.ds(q0, rows), :],
                                         out_sems.at[idx % 2])

        @pl.when(step == 0)
        def _():
            lse_ref[...] = jnp.zeros_like(lse_ref)
            for cp in loads(0):
                cp.start()

        for idx, (q0, rows, klen) in enumerate(tiles):
            @pl.when(step == idx)
            def _(idx=idx, q0=q0, rows=rows, klen=klen):
                for cp in loads(idx):
                    cp.wait()
                if idx + 1 < n_t:
                    for cp in loads(idx + 1):
                        cp.start()
                if idx >= 2:
                    store(idx - 2).wait()
                for hd in range(HEADS):
                    s = _masked_scores(q_buf[hd, q0:q0 + rows, :], k_buf[hd, 0:klen, :], rows, klen)
                    m = jnp.max(s, axis=-1, keepdims=True)
                    p = jnp.exp(s - m)
                    l = jnp.sum(p, axis=-1, keepdims=True)
                    o_buf[idx % 2, 0:rows, hd * V_HEAD:(hd + 1) * V_HEAD] = _nn(p.astype(BF16), v_buf[hd, 0:klen, :]) / l
                    grp, lane = _stat_slot(hd)
                    lse_ref[grp, q0:q0 + rows, lane:lane + 1] = m + jnp.log(l)
                store(idx).start()
                if idx == n_t - 1:
                    store(idx - 1).wait()
                    store(idx).wait()
                    finish_wout()

    hbm = pl.BlockSpec(memory_space=pl.ANY)
    return pl.pallas_call(
        body,
        name="attn_fwd",
        grid=(n_t,),
        in_specs=[hbm, hbm, hbm, _const(SHARD_OUT, D)],
        out_specs=[hbm, _const(STAT_GROUPS, N, 128), hbm],
        out_shape=[jax.ShapeDtypeStruct((N, HEADS * V_HEAD), F32), jax.ShapeDtypeStruct((STAT_GROUPS, N, 128), F32),
                   jax.ShapeDtypeStruct((D, D), BF16)],
        scratch_shapes=[pltpu.VMEM((HEADS, N, 256), BF16), pltpu.VMEM((HEADS, N, 256), BF16),
                        pltpu.VMEM((HEADS, N, V_HEAD), BF16), pltpu.VMEM((2, TQ, HEADS * V_HEAD), F32),
                        pltpu.VMEM((SHARD_OUT, D), BF16),
                        pltpu.SemaphoreType.DMA((3, 2)), pltpu.SemaphoreType.DMA((2,))]
        + [pltpu.SemaphoreType.DMA((3,))] * 4 + [pltpu.SemaphoreType.DMA],
        compiler_params=_cparams(dimension_semantics=("arbitrary",), collective_id=1),
    )(q, k, v, wout_s)


def _inv_count(row0, rows, w):
    row = row0 + lax.broadcasted_iota(jnp.int32, (rows, 1), 0)
    return 1.0 / jnp.clip(row - (PAD - 1), 1, w).astype(F32)


def _mid(h, tgt, pool_in, pool_gate, attn_gate, attn, pool_w, pool_scale, wout, gf):
    tr = ROWS_MID
    per = tr // HALO
    ng = len(POOL_WINDOWS)

    def body(h_ref, t_ref, pin_ref, halo_ref, pg_ref, ag_ref, at_ref, pw_ref, ps_ref, wout_ref, gf_ref,
             dh2_ref, do_ref, delta_ref, dag_ref, dpg_ref, dpl_ref, dwout_ref, dpw_ref, dps_ref, dgf_ref, loss_ref):
        i = pl.program_id(0)

        @pl.when(i == 0)
        def _():
            dwout_ref[...] = jnp.zeros_like(dwout_ref)
            dpw_ref[...] = jnp.zeros_like(dpw_ref)
            dps_ref[...] = jnp.zeros_like(dps_ref)
            dgf_ref[...] = jnp.zeros_like(dgf_ref)
            loss_ref[...] = jnp.zeros_like(loss_ref)

        row0 = i * tr
        real = (row0 + lax.broadcasted_iota(jnp.int32, (tr, 1), 0)) >= HEAD_ROWS
        h = h_ref[...]

        halo = jnp.where(i > 0, halo_ref[...], 0.0)
        ext = jnp.concatenate([halo, pin_ref[...]], axis=0)
        pooled = []
        for g, w in enumerate(POOL_WINDOWS):
            e = ext[:, g * POOL_GROUP:(g + 1) * POOL_GROUP]
            acc = e
            shift = 1
            while shift < w:
                acc = acc + pltpu.roll(acc, shift, 0)
                shift *= 2
            pooled.append((acc[HALO:] * _inv_count(row0, tr, w) - e[HALO:]).astype(BF16))
        pw = [pw_ref[g].astype(BF16) for g in range(ng)]
        mixed = jnp.concatenate([_nn(pooled[g], pw[g]) for g in range(ng)], axis=1)
        ps = ps_ref[...]
        mixed_s = mixed * ps
        pg = pg_ref[...]
        sig_p = _sigmoid(pg)
        silu_p = pg * sig_p
        pool_out = (silu_p * mixed_s).astype(BF16)
        ag = ag_ref[...]
        sig_a = _sigmoid(ag)
        silu_a = ag * sig_a
        at = at_ref[...]
        attn_out = (silu_a * at).astype(BF16)
        cat = jnp.concatenate([pool_out, attn_out], axis=1)
        h2 = h + _nn(cat, wout_ref[...])

        r2 = lax.rsqrt(jnp.mean(h2 * h2, axis=-1, keepdims=True) + EPS)
        n2 = h2 * r2
        gfv = gf_ref[...]
        err = jnp.where(real, n2 * gfv - t_ref[...], 0.0)
        loss_ref[...] += jnp.sum(jnp.sum(err * err, axis=-1, keepdims=True), axis=0, keepdims=True) * (0.5 / D)
        dy = err * (1.0 / D)
        dgf_ref[...] += jnp.sum(dy * n2, axis=0, keepdims=True)
        dn = dy * gfv
        dh2 = r2 * (dn - n2 * jnp.mean(dn * n2, axis=-1, keepdims=True))
        dh2_ref[...] = dh2
        dh2b = dh2.astype(BF16)

        dwout_ref[...] += _tn(cat, dh2b)
        dcat = _nt(dh2b, wout_ref[...])
        dpo = dcat[:, 0:D_POOL]
        dao = dcat[:, D_POOL:D]
        do = dao * silu_a
        prod = do * at
        delta_ref[...] = jnp.zeros_like(delta_ref)
        for hd in range(HEADS):
            grp, lane = _stat_slot(hd)
            cols = slice(hd * V_HEAD, (hd + 1) * V_HEAD)
            do_ref[grp, :, lane * V_HEAD:(lane + 1) * V_HEAD] = do[:, cols].astype(BF16)
            delta_ref[grp, :, lane:lane + 1] = jnp.sum(prod[:, cols], axis=-1, keepdims=True)
        dag_ref[...] = (dao * at * (sig_a * (1.0 + ag * (1.0 - sig_a)))).astype(BF16)
        dmixed_s = dpo * silu_p
        dpg_ref[...] = (dpo * mixed_s * (sig_p * (1.0 + pg * (1.0 - sig_p)))).astype(BF16)
        dps_ref[...] += jnp.sum(dmixed_s * mixed, axis=0, keepdims=True)
        dmixed = (dmixed_s * ps).astype(BF16)
        dpl = []
        for g in range(ng):
            dm = dmixed[:, g * POOL_GROUP:(g + 1) * POOL_GROUP]
            dpl.append(_nt(dm, pw[g]))
            dpw_ref[g] += _tn(pooled[g], dm)
        dpl_ref[...] = jnp.concatenate(dpl, axis=1)

    halo_spec = pl.BlockSpec((HALO, D_POOL), lambda i: (jnp.maximum(i * per - 1, 0), 0))
    return pl.pallas_call(
        body,
        name="mid",
        grid=(N // tr,),
        in_specs=[
            _rows(D, tr), _rows(D, tr), _rows(D_POOL, tr), halo_spec, _rows(D_POOL, tr), _rows(D_POOL, tr),
            _rows(D_POOL, tr), _const(ng, POOL_GROUP, POOL_GROUP), _const(1, D_POOL), _const(D, D), _const(1, D),
        ],
        out_specs=[
            _rows(D, tr), pl.BlockSpec((STAT_GROUPS, tr, HEADS_PER_STEP_BWD * V_HEAD), lambda i: (0, i, 0)),
            pl.BlockSpec((STAT_GROUPS, tr, 128), lambda i: (0, i, 0)),
            _rows(D_POOL, tr), _rows(D_POOL, tr), _rows(D_POOL, tr),
            _const(D, D), _const(ng, POOL_GROUP, POOL_GROUP), _const(1, D_POOL), _const(1, D), _const(1, 128),
        ],
        out_shape=[
            jax.ShapeDtypeStruct((N, D), F32), jax.ShapeDtypeStruct((STAT_GROUPS, N, HEADS_PER_STEP_BWD * V_HEAD), BF16),
            jax.ShapeDtypeStruct((STAT_GROUPS, N, 128), F32),
            jax.ShapeDtypeStruct((N, D_POOL), BF16), jax.ShapeDtypeStruct((N, D_POOL), BF16),
            jax.ShapeDtypeStruct((N, D_POOL), F32), jax.ShapeDtypeStruct((D, D), F32),
            jax.ShapeDtypeStruct((ng, POOL_GROUP, POOL_GROUP), F32),
            jax.ShapeDtypeStruct((1, D_POOL), F32), jax.ShapeDtypeStruct((1, D), F32), jax.ShapeDtypeStruct((1, 128), F32),
        ],
        compiler_params=_cparams(dimension_semantics=("arbitrary",)),
    )(h, tgt, pool_in, pool_in, pool_gate, attn_gate, attn, pool_w, pool_scale, wout, gf)


def _unrope(dy, cosv, sinv):
    return dy * cosv + _swap64(dy * sinv) * _low_lanes()


def _attn_bwd(q, k, v, do, lse, delta, cosf, sinf, dwout):
    tiles = _attn_tiles()
    hp = HEADS_PER_STEP_BWD
    n_g = HEADS // hp
    n_t = len(tiles)
    half = SHARD_OUT // 2
    swap_at, send_at, sum_at = (0, 3), (0, 5), (n_g - 1, n_t // 2)

    def body(q_hbm, k_hbm, v_hbm, do_hbm, lse_ref, delta_ref, cos_ref, sin_ref, dwout_hbm, dq_hbm, dkv_ref, dkr_ref,
             gwout_ref, q_buf, k_buf, v_buf, do_buf, dq_buf, dk_acc, dv_acc, own_w, sib_w, stage_w, recv_w, gw_buf,
             in_sems, out_sems, ow_sems, d2d_send, d2d_recv, ici_send, ici_recv, fin_send, fin_recv):
        grp = pl.program_id(0)
        step = pl.program_id(1)
        heads = pl.ds(grp * hp, hp)
        x, y, c = lax.axis_index("x"), lax.axis_index("y"), lax.axis_index("c")
        sibling = (x, y, 1 - c)

        def chip_of(rel):
            fx, fy = _CHIP_RELS[rel]
            return 2 * (x ^ fx) + (y ^ fy)

        def piece(chip, core):
            return dwout_hbm.at[pl.ds(pl.multiple_of(SHARD_OUT * chip + half * core, half), half), :]

        def own_load(rel):
            return pltpu.make_async_copy(piece(chip_of(rel), c), own_w.at[rel], ow_sems.at[rel])

        def d2d_copy(rel):
            return _remote(piece(chip_of(rel), 1 - c), sib_w.at[rel], d2d_send.at[rel], d2d_recv.at[rel], sibling)

        def ici_copy(rel):
            fx, fy = _CHIP_RELS[rel]
            return _remote(stage_w.at[rel - 1], recv_w.at[rel - 1], ici_send.at[rel - 1], ici_recv.at[rel - 1],
                           (x ^ fx, y ^ fy, c))

        def fin_copy(core):
            spot = gw_buf.at[pl.ds(pl.multiple_of(half * core, half), half), :]
            return _remote(spot, spot, fin_send.at[0], fin_recv.at[0], sibling)

        @pl.when((grp == 0) & (step == 0))
        def _():
            _peer_signal(x, y, c)
            for rel in (1, 2, 3, 0):
                own_load(rel).start()

        @pl.when((grp == swap_at[0]) & (step == swap_at[1]))
        def _():
            _peer_wait()
            for rel in (1, 2, 3, 0):
                d2d_copy(rel).start()

        @pl.when((grp == send_at[0]) & (step == send_at[1]))
        def _():
            for rel in (1, 2, 3):
                own_load(rel).wait()
                d2d_copy(rel).wait_recv()
                stage_w[rel - 1] = (own_w[rel] + sib_w[rel]).astype(BF16)
                ici_copy(rel).start()

        @pl.when((grp == sum_at[0]) & (step == sum_at[1]))
        def _():
            own_load(0).wait()
            d2d_copy(0).wait_recv()
            total = own_w[0] + sib_w[0]
            for rel in (1, 2, 3):
                ici_copy(rel).wait_recv()
                total = total + recv_w[rel - 1].astype(F32)
            gw_buf[pl.ds(pl.multiple_of(half * c, half), half), :] = total
            fin_copy(c).start()

        def finish_dwout():
            fin_copy(1 - c).wait_recv()
            for rel in (0, 1, 2, 3):
                d2d_copy(rel).wait_send()
            for rel in (1, 2, 3):
                ici_copy(rel).wait_send()
            fin_copy(c).wait_send()
            gwout_ref[...] = gw_buf[...]

        def loads(g, idx):
            q0, rows, _ = tiles[idx]
            rs = pl.ds(q0, rows)
            par = (g * n_t + idx) % 2
            hs = pl.ds(g * hp, hp)
            pairs = ((q_hbm.at[hs, rs, :], q_buf.at[:, rs, :]), (k_hbm.at[hs, rs, :], k_buf.at[:, rs, :]),
                     (v_hbm.at[hs, rs, :], v_buf.at[:, rs, :]), (do_hbm.at[g, rs, :], do_buf.at[rs, :]))
            return [pltpu.make_async_copy(src, dst, in_sems.at[a, par]) for a, (src, dst) in enumerate(pairs)]

        def store(idx):
            q0, rows, _ = tiles[idx]
            return pltpu.make_async_copy(dq_buf.at[idx % 2, :, pl.ds(0, rows), :], dq_hbm.at[heads, pl.ds(q0, rows), :],
                                         out_sems.at[idx % 2])

        @pl.when(step == 0)
        def _():
            dk_acc[...] = jnp.zeros_like(dk_acc)
            dv_acc[...] = jnp.zeros_like(dv_acc)

        @pl.when((step == 0) & (grp == 0))
        def _():
            dkr_ref[...] = jnp.zeros_like(dkr_ref)
            for cp in loads(grp, 0):
                cp.start()

        for idx, (q0, rows, klen) in enumerate(tiles):
            @pl.when(step == idx)
            def _(idx=idx, q0=q0, rows=rows, klen=klen):
                for cp in loads(grp, idx):
                    cp.wait()
                if idx + 1 < n_t:
                    for cp in loads(grp, idx + 1):
                        cp.start()
                if idx >= 2:
                    store(idx - 2).wait()
                qs = pl.ds(q0, rows)
                for hd in range(hp):
                    qv = q_buf[hd, qs, :]
                    kv = k_buf[hd, 0:klen, :]
                    p = jnp.exp(_masked_scores(qv, kv, rows, klen) - lse_ref[0, qs, hd:hd + 1])
                    dob = do_buf[qs, hd * V_HEAD:(hd + 1) * V_HEAD]
                    ds = (p * (_nt(dob, v_buf[hd, 0:klen, :]) - delta_ref[0, qs, hd:hd + 1])).astype(BF16)
                    dq = _nn(ds, kv) * SCALE
                    dq_buf[idx % 2, hd, 0:rows, 0:QK_NOPE] = dq[:, 0:QK_NOPE].astype(BF16)
                    dq_buf[idx % 2, hd, 0:rows, QK_NOPE:] = _unrope(dq[:, QK_NOPE:], cos_ref[qs, :], sin_ref[qs, :]).astype(BF16)
                    dk_acc[hd, 0:klen, :] += _tn(ds, qv)
                    dv_acc[hd, 0:klen, :] += _tn(p.astype(BF16), dob)
                store(idx).start()

        @pl.when(step == n_t - 1)
        def _():
            @pl.when(grp + 1 < n_g)
            def _():
                for cp in loads(grp + 1, 0):
                    cp.start()

            for hd in range(hp):
                dkv_ref[hd, :, 0:QK_NOPE] = dk_acc[hd, :, 0:QK_NOPE].astype(BF16)
                dkv_ref[hd, :, QK_NOPE:] = dv_acc[hd].astype(BF16)
                dkr_ref[...] += dk_acc[hd, :, QK_NOPE:]
            store(n_t - 2).wait()
            store(n_t - 1).wait()

            @pl.when(grp == n_g - 1)
            def _():
                finish_dwout()

    hbm = pl.BlockSpec(memory_space=pl.ANY)
    stat = pl.BlockSpec((1, N, 128), lambda g, t: (g, 0, 0), pipeline_mode=pl.Buffered(1))
    piece_f32 = lambda lead: pltpu.VMEM((lead, half, D), F32)
    piece_bf16 = lambda lead: pltpu.VMEM((lead, half, D), BF16)
    return pl.pallas_call(
        body,
        name="attn_bwd",
        grid=(n_g, n_t),
        in_specs=[hbm, hbm, hbm, hbm, stat, stat, _const(N, 128), _const(N, 128), hbm],
        out_specs=[hbm, pl.BlockSpec((hp, N, 256), lambda g, t: (g, 0, 0), pipeline_mode=pl.Buffered(1)), _const(N, 128),
                   _const(SHARD_OUT, D)],
        out_shape=[
            jax.ShapeDtypeStruct((HEADS, N, 256), BF16), jax.ShapeDtypeStruct((HEADS, N, 256), BF16),
            jax.ShapeDtypeStruct((N, 128), F32), jax.ShapeDtypeStruct((SHARD_OUT, D), F32),
        ],
        scratch_shapes=[pltpu.VMEM((hp, N, 256), BF16), pltpu.VMEM((hp, N, 256), BF16), pltpu.VMEM((hp, N, V_HEAD), BF16),
                        pltpu.VMEM((N, hp * V_HEAD), BF16), pltpu.VMEM((2, hp, TQ, 256), BF16),
                        pltpu.VMEM((hp, N, 256), F32), pltpu.VMEM((hp, N, V_HEAD), F32),
                        piece_f32(CHIPS), piece_f32(CHIPS), piece_bf16(3), piece_bf16(3), pltpu.VMEM((SHARD_OUT, D), F32),
                        pltpu.SemaphoreType.DMA((4, 2)), pltpu.SemaphoreType.DMA((2,)), pltpu.SemaphoreType.DMA((CHIPS,)),
                        pltpu.SemaphoreType.DMA((CHIPS,)), pltpu.SemaphoreType.DMA((CHIPS,)),
                        pltpu.SemaphoreType.DMA((3,)), pltpu.SemaphoreType.DMA((3,)),
                        pltpu.SemaphoreType.DMA((1,)), pltpu.SemaphoreType.DMA((1,))],
        compiler_params=_cparams(dimension_semantics=("arbitrary", "arbitrary"), collective_id=2),
    )(q, k, v, do, lse, delta, cosf, sinf, dwout)


def _bwd_in(h, dh2, dq, dkv, dkr, cq, ckv, dpl, dpg, dag, norm_g, win, gq, wq, gkv, wkv, cosf, sinf, adam_out):
    tr = ROWS_BWD
    nb = N // tr
    per = tr // HALO
    lead = HEAD_ROWS
    adam_rows = SHARD_OUT // nb

    def body(h_ref, dh2_ref, dq_ref, dkv_ref, dkr_ref, cq_ref, ckv_ref, dpl_ref, halo_ref, dpg_ref, dag_ref,
             g_ref, win_ref, gq_ref, wq_ref, gkv_ref, wkv_ref, cos_ref, sin_ref, aw_ref, ag_ref, am_ref, av_ref,
             gx_ref, dmeta_ref, dsl_ref, hn_ref, dwq_ref, dwkv_ref, dg_ref, dgq_ref, dgkv_ref, ago_ref, ad_ref, anm_ref, anv_ref,
             dh_buf, gx_sem):
        i = pl.program_id(0)
        grad_out = ag_ref[...]
        ago_ref[...] = grad_out
        ad_ref[...], anm_ref[...], anv_ref[...] = _adamw_math(aw_ref[...], grad_out, am_ref[...], av_ref[...])

        @pl.when(i == 0)
        def _():
            dwq_ref[...] = jnp.zeros_like(dwq_ref)
            dwkv_ref[...] = jnp.zeros_like(dwkv_ref)
            dg_ref[...] = jnp.zeros_like(dg_ref)
            dgq_ref[...] = jnp.zeros_like(dgq_ref)
            dgkv_ref[...] = jnp.zeros_like(dgkv_ref)

        row0 = i * tr
        h = h_ref[...]
        r = lax.rsqrt(jnp.mean(h * h, axis=-1, keepdims=True) + EPS)
        n = h * r
        gv = g_ref[...]
        hn = (n * gv).astype(BF16)
        cq = cq_ref[...]
        rq = lax.rsqrt(jnp.mean(cq * cq, axis=-1, keepdims=True) + EPS)
        nq = cq * rq
        gqv = gq_ref[...]
        cqn = (nq * gqv).astype(BF16)
        dcqn = jnp.zeros((tr, Q_LORA), F32)
        for hd in range(HEADS):
            dqf = dq_ref[hd]
            dcqn = dcqn + _nn(dqf, wq_ref[hd])
            dwq_ref[hd] += _tn(dqf, cqn)
        dgq_ref[...] += jnp.sum(dcqn * nq, axis=0, keepdims=True)
        dnq = dcqn * gqv
        dcq = rq * (dnq - nq * jnp.mean(dnq * nq, axis=-1, keepdims=True))

        ckv = ckv_ref[...]
        rkv = lax.rsqrt(jnp.mean(ckv * ckv, axis=-1, keepdims=True) + EPS)
        nkv = ckv * rkv
        gkvv = gkv_ref[...]
        ckvn = (nkv * gkvv).astype(BF16)
        dckvn = jnp.zeros((tr, KV_LORA), F32)
        for hd in range(HEADS):
            dkv = dkv_ref[hd]
            dckvn = dckvn + _nt(dkv, wkv_ref[hd])
            dwkv_ref[hd] += _tn(ckvn, dkv)
        dgkv_ref[...] += jnp.sum(dckvn * nkv, axis=0, keepdims=True)
        dnkv = dckvn * gkvv
        dckv = rkv * (dnkv - nkv * jnp.mean(dnkv * nkv, axis=-1, keepdims=True))
        dkr = _unrope(dkr_ref[...], cos_ref[...], sin_ref[...])

        cur = dpl_ref[...]
        halo = jnp.where(i < nb - 1, halo_ref[...], 0.0)
        dpi = []
        for g, w in enumerate(POOL_WINDOWS):
            sl = slice(g * POOL_GROUP, (g + 1) * POOL_GROUP)
            a = jnp.concatenate([cur[:, sl] * _inv_count(row0, tr, w), halo[:, sl] * _inv_count(row0 + tr, HALO, w)], axis=0)
            acc = a
            shift = 1
            while shift < w:
                acc = acc + pltpu.roll(acc, tr + HALO - shift, 0)
                shift *= 2
            dpi.append(acc[0:tr] - cur[:, sl])

        du = jnp.concatenate([t.astype(BF16) for t in dpi] + [dpg_ref[...]] + [t.astype(BF16) for t in (dcq, dckv, dkr)],
                             axis=1)
        dagb = dag_ref[...]
        by_row = jnp.concatenate(dpi + [dpg_ref[...].astype(F32), dcq, dckv, dkr[:, 0:QK_ROPE], dagb.astype(F32),
                                        jnp.zeros((tr, SHARD_PAD - SHARD_IN), F32)], axis=1)
        for chip in range(CHIPS):
            dsl_ref[chip] = by_row[:, SHARD_IN * chip:SHARD_IN * chip + SHARD_PAD].astype(BF16)
        hn_ref[...] = hn
        dhn = _nn(du, win_ref[0:O_KR_END, :]) + _nn(dagb, win_ref[O_AG:D_IN, :])
        dg_ref[...] += jnp.sum(dhn * n, axis=0, keepdims=True)
        dn = dhn * gv
        dh = dh2_ref[...] + r * (dn - n * jnp.mean(dn * n, axis=-1, keepdims=True))

        first = pltpu.make_async_copy(dh_buf.at[pl.ds(lead, tr - lead), :], gx_ref.at[pl.ds(0, tr - lead), :], gx_sem)
        later = lambda step: pltpu.make_async_copy(
            dh_buf, gx_ref.at[pl.ds(pl.multiple_of(step * tr - lead, 16), tr), :], gx_sem)

        @pl.when(i == 1)
        def _():
            first.wait()

        @pl.when(i > 1)
        def _():
            later(i - 1).wait()

        dh_buf[...] = dh

        @pl.when(i == 0)
        def _():
            first.start()
            for chip in range(CHIPS):
                dmeta_ref[chip] = dh[PAD:HEAD_ROWS, chip * 256:(chip + 1) * 256]

        @pl.when(i > 0)
        def _():
            later(i).start()

        @pl.when(i == nb - 1)
        def _():
            later(i).wait()

    head = lambda w: pl.BlockSpec((HEADS, tr, w), lambda i: (0, i, 0))
    halo_spec = pl.BlockSpec((HALO, D_POOL), lambda i: (jnp.minimum((i + 1) * per, N // HALO - 1), 0))
    return pl.pallas_call(
        body,
        name="bwd_in",
        grid=(nb,),
        in_specs=[
            _rows(D, tr), _rows(D, tr), head(256), head(256), _rows(128, tr), _rows(Q_LORA, tr), _rows(KV_LORA, tr),
            _rows(D_POOL, tr), halo_spec, _rows(D_POOL, tr), _rows(D_POOL, tr),
            _const(1, D), _const(D_IN, D), _const(1, Q_LORA), _const(HEADS, 256, Q_LORA),
            _const(1, KV_LORA), _const(HEADS, KV_LORA, 256), _rows(128, tr), _rows(128, tr),
        ] + [_rows(D, adam_rows)] * 4,
        out_specs=[
            pl.BlockSpec(memory_space=pl.ANY), _const(CHIPS, N_META, 256),
            pl.BlockSpec((CHIPS, tr, SHARD_PAD), lambda i: (0, i, 0)), _rows(D, tr),
            _const(HEADS, 256, Q_LORA),
            _const(HEADS, KV_LORA, 256), _const(1, D), _const(1, Q_LORA), _const(1, KV_LORA),
        ] + [_rows(D, adam_rows)] * 4,
        out_shape=[
            jax.ShapeDtypeStruct((S, D), F32), jax.ShapeDtypeStruct((CHIPS, N_META, 256), F32),
            jax.ShapeDtypeStruct((CHIPS, N, SHARD_PAD), BF16), jax.ShapeDtypeStruct((N, D), BF16),
            jax.ShapeDtypeStruct((HEADS, 256, Q_LORA), F32),
            jax.ShapeDtypeStruct((HEADS, KV_LORA, 256), F32),
            jax.ShapeDtypeStruct((1, D), F32), jax.ShapeDtypeStruct((1, Q_LORA), F32), jax.ShapeDtypeStruct((1, KV_LORA), F32),
        ] + [jax.ShapeDtypeStruct((SHARD_OUT, D), F32)] * 4,
        scratch_shapes=[pltpu.VMEM((tr, D), F32), pltpu.SemaphoreType.DMA],
        compiler_params=_cparams(dimension_semantics=("arbitrary",)),
    )(h, dh2, dq, dkv, dkr, cq, ckv, dpl, dpl, dpg, dag, norm_g, win, gq, wq, gkv, wkv, cosf, sinf, *adam_out)


def _local_step(h, tgt, norm_g, win, gq, wq, gkv, wkv, pool_w, pool_scale, wout_s, m_wout_s, v_wout_s, gf, cosf, sinf):
    pool_in, pool_gate, cq, ckv, attn_gate, q, k, v = _fwd_in(h, norm_g, win, gq, wq, gkv, wkv, cosf, sinf)
    attn, lse, wout = _attn_fwd(q, k, v, wout_s)
    dh2, do, delta, dag, dpg, dpl, dwout, dpw, dps, dgf, loss = _mid(
        h, tgt, pool_in, pool_gate, attn_gate, attn, pool_w, pool_scale, wout, gf)
    dq, dkv, dkr, gwout = _attn_bwd(q, k, v, do, lse, delta, cosf, sinf, dwout)
    gx, dmeta, dsl, hn, dwq, dwkv, dg, dgq, dgkv, *r_out = _bwd_in(
        h, dh2, dq, dkv, dkr, cq, ckv, dpl, dpg, dag, norm_g, win, gq, wq, gkv, wkv, cosf, sinf,
        (wout_s, gwout, m_wout_s, v_wout_s))
    return dict(gx=gx, dmeta=dmeta, dsl=dsl, hn=hn, dwq=dwq, dwkv=dwkv, r_out=tuple(r_out), dg=dg, dgq=dgq,
                dgkv=dgkv, dpw=dpw, dps=dps, dgf=dgf, loss=loss)


_CHIP_RELS = ((0, 0), (1, 0), (0, 1), (1, 1))

_ARR_ROWS = (SHARD_IN, SHARD_OUT, 256, KV_LORA, N_META)
_ARR_COLS = (D, D, Q_LORA, 256, 256)
_PIECES = (
    (0, 0, 256, 0), (0, 256, SHARD_IN - 256, 1),
    (1, 0, 128, 0), (1, 128, 128, 1),
    (2, 0, 128, 0), (2, 128, 128, 1),
    (3, 0, 64, 0), (3, 64, 64, 1),
    (4, 0, N_META, 0),
)
_NP = len(_PIECES)
_PIECE_MAX = (256, 128, 128, 64, N_META)


def _gathered_at(refs, arr, chip, r0, n):
    if arr in (0, 1):
        return refs[arr].at[pl.ds(pl.multiple_of(_ARR_ROWS[arr] * chip + r0, 16), n), :]
    return refs[arr].at[chip, pl.ds(r0, n), :]


def _remote(src, dst, send_sem, recv_sem, to):
    return pltpu.make_async_remote_copy(src_ref=src, dst_ref=dst, send_sem=send_sem, recv_sem=recv_sem,
                                        device_id=to, device_id_type=MESH)


def _gather_weights(winT_s, wqT_s, wkv_s, meta_s, x2, tgt2):
    arrays = (0, 2, 3, 4)

    def body(win_ref, wq_ref, wkv_ref, meta_ref, x_ref, t_ref, win_o, wq_o, wkv_o, h_o, tp_o,
             s_win, s_wq, s_wkv, meta_all, head_buf, x_buf, t_buf, ici_send, ici_recv, fwd_send, fwd_recv,
             loc_sems, own_sems):
        x, y, c = lax.axis_index("x"), lax.axis_index("y"), lax.axis_index("c")
        me = 2 * x + y
        stage = (s_win, None, s_wq, s_wkv, meta_ref)
        outs = (win_o, None, wq_o, wkv_o, meta_all)

        _peer_signal(x, y, c)

        frames = pl.ds(HEAD_ROWS, S)
        loads = [pltpu.make_async_copy(x_ref, x_buf, loc_sems.at[0]), pltpu.make_async_copy(t_ref, t_buf, loc_sems.at[1])]
        local = [pltpu.make_async_copy(x_buf, h_o.at[frames, :], loc_sems.at[0]),
                 pltpu.make_async_copy(t_buf, tp_o.at[frames, :], loc_sems.at[1])]
        for cp in loads:
            cp.start()

        s_win[...] = win_ref[...].astype(BF16)
        s_wq[0:QK, :] = wq_ref[...].astype(BF16)
        s_wq[QK:256, :] = jnp.zeros((256 - QK, Q_LORA), BF16)
        s_wkv[...] = wkv_ref[...].astype(BF16)
        head_buf[...] = jnp.zeros_like(head_buf)
        zeros = pltpu.make_async_copy(head_buf, tp_o.at[pl.ds(0, HEAD_ROWS), :], loc_sems.at[2])
        zeros.start()

        def chip_of(rel):
            fx, fy = _CHIP_RELS[rel]
            return 2 * (x ^ fx) + (y ^ fy)

        def same_core_of(rel):
            fx, fy = _CHIP_RELS[rel]
            return (x ^ fx, y ^ fy, c)

        def ici_copy(rel, i, src_chip, to):
            arr, r0, n, _ = _PIECES[i]
            k = (rel - 1) * _NP + i
            return _remote(stage[arr].at[pl.ds(r0, n), :], _gathered_at(outs, arr, src_chip, r0, n),
                           ici_send.at[k], ici_recv.at[k], to)

        def fwd_copy(rel, i, to):
            arr, r0, n, _ = _PIECES[i]
            k = (rel - 1) * _NP + i
            place = _gathered_at(outs, arr, chip_of(rel), r0, n)
            return _remote(place, place, fwd_send.at[k], fwd_recv.at[k], to)

        _peer_wait()
        for core in (0, 1):
            @pl.when(c == core)
            def _(core=core):
                mine = [i for i in range(_NP) if _PIECES[i][3] == core and _PIECES[i][0] in arrays]
                theirs = [i for i in range(_NP) if _PIECES[i][3] != core and _PIECES[i][0] in arrays]
                sends = [ici_copy(rel, i, me, same_core_of(rel)) for rel in (1, 2, 3) for i in mine]
                for cp in sends:
                    cp.start()
                for ld, st in zip(loads, local):
                    ld.wait()
                    st.start()
                own = [pltpu.make_async_copy(stage[arr], _gathered_at(outs, arr, me, 0, _ARR_ROWS[arr]), own_sems.at[arr])
                       for arr in arrays if arr != 4]
                for cp in own:
                    cp.start()
                meta_all[me] = meta_ref[...]
                for rel in (1, 2, 3):
                    for i in mine:
                        ici_copy(rel, i, chip_of(rel), (x, y, c)).wait_recv()
                        fwd = fwd_copy(rel, i, (x, y, 1 - c))
                        fwd.start()
                        sends.append(fwd)
                for rel in (1, 2, 3):
                    for i in theirs:
                        fwd_copy(rel, i, (x, y, c)).wait_recv()
                for cp in sends:
                    cp.wait_send()
                for cp in own:
                    cp.wait()

        zeros.wait()
        for chip in range(CHIPS):
            head_buf[PAD:HEAD_ROWS, chip * 256:(chip + 1) * 256] = meta_all[chip]
        head = pltpu.make_async_copy(head_buf, h_o.at[pl.ds(0, HEAD_ROWS), :], loc_sems.at[2])
        head.start()
        head.wait()
        for cp in local:
            cp.wait()

    vm = pl.BlockSpec(memory_space=pltpu.VMEM)
    hbm = pl.BlockSpec(memory_space=pl.ANY)
    return pl.pallas_call(
        body,
        name="gather_weights",
        in_specs=[vm] * 4 + [hbm] * 2,
        out_specs=[hbm] * 5,
        out_shape=[
            jax.ShapeDtypeStruct((D_IN, D), BF16),
            jax.ShapeDtypeStruct((CHIPS, 256, Q_LORA), BF16), jax.ShapeDtypeStruct((CHIPS, KV_LORA, 256), BF16),
            jax.ShapeDtypeStruct((N, D), F32), jax.ShapeDtypeStruct((N, D), F32),
        ],
        scratch_shapes=[pltpu.VMEM((_ARR_ROWS[a], _ARR_COLS[a]), BF16) for a in (0, 2, 3)]
        + [pltpu.VMEM((CHIPS, N_META, 256), F32), pltpu.VMEM((HEAD_ROWS, D), F32), pltpu.VMEM((S, D), F32),
           pltpu.VMEM((S, D), F32)]
        + [pltpu.SemaphoreType.DMA((3 * _NP,))] * 4 + [pltpu.SemaphoreType.DMA((3,)), pltpu.SemaphoreType.DMA((4,))],
        compiler_params=_cparams(collective_id=0),
    )(winT_s, wqT_s, wkv_s, meta_s, x2, tgt2)


_SM_ROWS = (len(POOL_WINDOWS) * POOL_GROUP, VEC_ROWS)
_SM_COLS = (POOL_GROUP, D)
_SM_PIECES = ((0, 0, 256, 0), (0, 256, 256, 1), (1, 0, VEC_ROWS, 0))
_NSP = len(_SM_PIECES)


def _reduce_grads(dsl, hn, dwq, dwkv, dmeta4, dpw, dg, dgf, dgq, dgkv, dps, loss):
    arrays = (0, 2, 3, 4)
    loaded = (2, 3, 4)
    shard_order = (1, 2, 3, 0)

    def body(dsl_hbm, hn_hbm, dwq_ref, dwkv_ref, dmeta_ref, dpw_ref, dg_ref, dgf_ref, dgq_ref, dgkv_ref, dps_ref,
             loss_ref, gwin_o, gwq_o, gwkv_o, gmeta_o, gpw_o, gg_o, ggf_o, ggq_o, ggkv_o, gps_o, gloss_o,
             ow2, ow3, ow4, sb0, sb2, sb3, sb4, st0, st2, st3, st4, rc0, rc2, rc3, rc4,
             vec, sm_sb0, sm_sb1, sm_cs0, sm_cs1, sm_rc0, sm_rc1, vec_fin, slab_v, hn_v, dwin_buf, own0,
             own_sems, d2d_send, d2d_recv, ici_send, ici_recv, fin_send, fin_recv,
             swap_send, swap_recv, smi_send, smi_recv, smf_send, smf_recv, ld_sems):
        x, y, c = lax.axis_index("x"), lax.axis_index("y"), lax.axis_index("c")
        me = 2 * x + y
        _peer_signal(x, y, c)
        grads = (None, None, dwq_ref, dwkv_ref, dmeta_ref)
        outs = (gwin_o, None, gwq_o, gwkv_o, gmeta_o)
        own_buf = (None, None, ow2, ow3, ow4)
        sib_buf = (sb0, None, sb2, sb3, sb4)
        stage = (st0, None, st2, st3, st4)
        recv = (rc0, None, rc2, rc3, rc4)
        sm_mine = (dpw_ref, vec)
        sm_sib = (sm_sb0, sm_sb1)
        sm_chip = (sm_cs0, sm_cs1)
        sm_recv = (sm_rc0, sm_rc1)
        sm_out = (gpw_o, vec_fin)
        sibling = (x, y, 1 - c)

        def chip_of(rel):
            fx, fy = _CHIP_RELS[rel]
            return 2 * (x ^ fx) + (y ^ fy)

        def same_core_of(rel):
            fx, fy = _CHIP_RELS[rel]
            return (x ^ fx, y ^ fy, c)

        hn_load = pltpu.make_async_copy(hn_hbm, hn_v, ld_sems.at[CHIPS])

        def slab_load(rel):
            return pltpu.make_async_copy(dsl_hbm.at[chip_of(rel)], slab_v.at[rel], ld_sems.at[rel])

        hn_load.start()
        slab_load(shard_order[0]).start()

        def slot(bufs, i, idx):
            arr, _, n, _ = _PIECES[i]
            return bufs[arr].at[idx, pl.ds(0, n), :]

        def own_load(rel, i):
            arr, r0, n, _ = _PIECES[i]
            return pltpu.make_async_copy(_gathered_at(grads, arr, chip_of(rel), r0, n), slot(own_buf, i, rel),
                                         own_sems.at[rel * _NP + i])

        def d2d_copy(rel, i):
            arr, r0, n, _ = _PIECES[i]
            k = rel * _NP + i
            return _remote(_gathered_at(grads, arr, chip_of(rel), r0, n), slot(sib_buf, i, rel),
                           d2d_send.at[k], d2d_recv.at[k], sibling)

        def ici_copy(rel, i):
            k = (rel - 1) * _NP + i
            return _remote(slot(stage, i, rel - 1), slot(recv, i, rel - 1), ici_send.at[k], ici_recv.at[k],
                           same_core_of(rel))

        def fin_copy(i):
            arr, r0, n, _ = _PIECES[i]
            place = outs[arr].at[pl.ds(r0, n), :]
            return _remote(place, place, fin_send.at[i], fin_recv.at[i], sibling)

        def sm_ici_copy(rel, j):
            blk, r0, n, _ = _SM_PIECES[j]
            k = (rel - 1) * _NSP + j
            return _remote(sm_chip[blk].at[pl.ds(r0, n), :], sm_recv[blk].at[rel - 1, pl.ds(r0, n), :],
                           smi_send.at[k], smi_recv.at[k], same_core_of(rel))

        def sm_fin_copy(j):
            blk, r0, n, _ = _SM_PIECES[j]
            place = sm_out[blk].at[pl.ds(r0, n), :]
            return _remote(place, place, smf_send.at[j], smf_recv.at[j], sibling)

        vec[...] = jnp.zeros_like(vec)
        vec[0:1, :] = dg_ref[...]
        vec[1:2, :] = dgf_ref[...]
        vec[2:3, V_GQ:V_GQ + Q_LORA] = dgq_ref[...]
        vec[2:3, V_GKV:V_GKV + KV_LORA] = dgkv_ref[...]
        vec[2:3, V_PS:V_PS + D_POOL] = dps_ref[...]
        vec[2:3, V_LOSS:D] = loss_ref[...]
        _peer_wait()
        swaps = [_remote(sm_mine[b], sm_sib[b], swap_send.at[b], swap_recv.at[b], sibling) for b in (0, 1)]
        for cp in swaps:
            cp.start()

        for core in (0, 1):
            @pl.when(c == core)
            def _(core=core):
                mine = [i for i in range(_NP) if _PIECES[i][3] == core and _PIECES[i][0] in loaded]
                theirs = [i for i in range(_NP) if _PIECES[i][3] != core and _PIECES[i][0] in loaded]
                i0 = next(i for i in range(_NP) if _PIECES[i][0] == 0 and _PIECES[i][3] == core)
                j0 = next(i for i in range(_NP) if _PIECES[i][0] == 0 and _PIECES[i][3] != core)
                sm_mine_p = [j for j in range(_NSP) if _SM_PIECES[j][3] == core]
                sm_theirs_p = [j for j in range(_NSP) if _SM_PIECES[j][3] != core]
                sends = list(swaps)

                for rel in (1, 2, 3, 0):
                    for i in theirs:
                        cp = d2d_copy(rel, i)
                        cp.start()
                        sends.append(cp)
                    for i in mine:
                        own_load(rel, i).start()

                def piece_rows(i):
                    return pl.ds(_PIECES[i][1], _PIECES[i][2])

                def d2d0(rel, i):
                    return _remote(dwin_buf.at[rel, piece_rows(i), :], slot(sib_buf, i, rel),
                                   d2d_send.at[rel * _NP + i], d2d_recv.at[rel * _NP + i], sibling)

                def settle(rel):
                    d2d0(rel, i0).wait_recv()
                    total = dwin_buf[rel, piece_rows(i0), :] + slot(sib_buf, i0, rel)[...]
                    if rel == 0:
                        own0[0:_PIECES[i0][2], :] = total
                    else:
                        slot(stage, i0, rel - 1)[...] = total.astype(BF16)
                        cp = ici_copy(rel, i0)
                        cp.start()
                        sends.append(cp)

                hn_load.wait()
                for n, rel in enumerate(shard_order):
                    slab_load(rel).wait()
                    if n == 0:
                        for later in shard_order[1:]:
                            slab_load(later).start()
                    dwin_buf[rel] = _tn(slab_v[rel], hn_v[...])
                    cp = d2d0(rel, j0)
                    cp.start()
                    sends.append(cp)
                    if n > 0:
                        settle(shard_order[n - 1])
                settle(shard_order[-1])

                for rel in (1, 2, 3):
                    for i in mine:
                        arr, r0, n, _ = _PIECES[i]
                        own_load(rel, i).wait()
                        d2d_copy(rel, i).wait_recv()
                        total = slot(own_buf, i, rel)[...] + slot(sib_buf, i, rel)[...]
                        slot(stage, i, rel - 1)[...] = total.astype(stage[arr].dtype)
                        cp = ici_copy(rel, i)
                        cp.start()
                        sends.append(cp)

                for b in (0, 1):
                    swaps[b].wait_recv()
                    sm_chip[b][...] = sm_mine[b][...] + sm_sib[b][...]
                for rel in (1, 2, 3):
                    for j in sm_mine_p:
                        cp = sm_ici_copy(rel, j)
                        cp.start()
                        sends.append(cp)

                for i in mine:
                    arr, r0, n, _ = _PIECES[i]
                    own_load(0, i).wait()
                    d2d_copy(0, i).wait_recv()
                    total = slot(own_buf, i, 0)[...] + slot(sib_buf, i, 0)[...]
                    for rel in (1, 2, 3):
                        ici_copy(rel, i).wait_recv()
                        total = total + slot(recv, i, rel - 1)[...].astype(F32)
                    outs[arr][pl.ds(r0, n), :] = total
                    cp = fin_copy(i)
                    cp.start()
                    sends.append(cp)
                total = own0[0:_PIECES[i0][2], :]
                for rel in (1, 2, 3):
                    ici_copy(rel, i0).wait_recv()
                    total = total + slot(recv, i0, rel - 1)[...].astype(F32)
                outs[0][pl.ds(_PIECES[i0][1], _PIECES[i0][2]), :] = total
                cp = fin_copy(i0)
                cp.start()
                sends.append(cp)

                for j in sm_mine_p:
                    blk, r0, n, _ = _SM_PIECES[j]
                    for rel in (1, 2, 3):
                        sm_ici_copy(rel, j).wait_recv()
                    total = jnp.zeros((n, _SM_COLS[blk]), F32)
                    for chip in range(CHIPS):
                        flips = chip ^ me
                        rel = jnp.where(flips == 2, 1, jnp.where(flips == 1, 2, flips))
                        theirs_rows = sm_recv[blk][jnp.maximum(rel - 1, 0), pl.ds(r0, n), :]
                        total = total + jnp.where(rel == 0, sm_chip[blk][pl.ds(r0, n), :], theirs_rows)
                    sm_out[blk][pl.ds(r0, n), :] = total
                    cp = sm_fin_copy(j)
                    cp.start()
                    sends.append(cp)

                for i in theirs + [j0]:
                    fin_copy(i).wait_recv()
                for j in sm_theirs_p:
                    sm_fin_copy(j).wait_recv()
                for cp in sends:
                    cp.wait_send()

        gg_o[...] = vec_fin[0:1, :]
        ggf_o[...] = vec_fin[1:2, :]
        ggq_o[...] = vec_fin[2:3, V_GQ:V_GQ + Q_LORA]
        ggkv_o[...] = vec_fin[2:3, V_GKV:V_GKV + KV_LORA]
        gps_o[...] = vec_fin[2:3, V_PS:V_PS + D_POOL]
        gloss_o[...] = vec_fin[2:3, V_LOSS:D]

    vm = pl.BlockSpec(memory_space=pltpu.VMEM)
    piece_buf = lambda lead, dtype, which=arrays: [
        pltpu.VMEM((lead, _PIECE_MAX[a], _ARR_COLS[a]), F32 if a == 4 else dtype) for a in which]
    sm_buf = lambda *lead: [pltpu.VMEM(lead + (_SM_ROWS[b], _SM_COLS[b]), F32) for b in (0, 1)]
    dma = lambda n: [pltpu.SemaphoreType.DMA((n,))] * 2
    return pl.pallas_call(
        body,
        name="reduce_grads",
        in_specs=[pl.BlockSpec(memory_space=pl.ANY)] * 4 + [vm] * 8,
        out_specs=[vm] * 11,
        out_shape=[jax.ShapeDtypeStruct((_ARR_ROWS[a], _ARR_COLS[a]), F32) for a in arrays]
        + [jax.ShapeDtypeStruct((_SM_ROWS[0], _SM_COLS[0]), F32), jax.ShapeDtypeStruct((1, D), F32),
           jax.ShapeDtypeStruct((1, D), F32), jax.ShapeDtypeStruct((1, Q_LORA), F32),
           jax.ShapeDtypeStruct((1, KV_LORA), F32), jax.ShapeDtypeStruct((1, D_POOL), F32),
           jax.ShapeDtypeStruct((1, 128), F32)],
        scratch_shapes=piece_buf(CHIPS, F32, loaded) + piece_buf(CHIPS, F32) + piece_buf(3, BF16) + piece_buf(3, BF16)
        + [pltpu.VMEM((VEC_ROWS, D), F32)] + sm_buf() + sm_buf() + sm_buf(3) + [pltpu.VMEM((VEC_ROWS, D), F32)]
        + [pltpu.VMEM((CHIPS, N, SHARD_PAD), BF16), pltpu.VMEM((N, D), BF16),
           pltpu.VMEM((CHIPS, SHARD_PAD, D), F32), pltpu.VMEM((_PIECE_MAX[0], D), F32)]
        + [pltpu.SemaphoreType.DMA((CHIPS * _NP,))]
        + dma(CHIPS * _NP) + dma(3 * _NP) + dma(_NP) + dma(2) + dma(3 * _NSP) + dma(_NSP)
        + [pltpu.SemaphoreType.DMA((CHIPS + 1,))],
        compiler_params=_cparams(collective_id=3),
    )(dsl, hn, dwq, dwkv, dmeta4, dpw, dg, dgf, dgq, dgkv, dps, loss)


def _adamw_math(w, g, m, v):
    m = B1 * m + (1.0 - B1) * g
    v = B2 * v + (1.0 - B2) * (g * g)
    m_hat = m / C1
    v_hat = v / C2
    delta = -LR * (m_hat / (jnp.sqrt(v_hat) + ADAM_EPS) + WD * w)
    return delta, m, v


def _adamw_rows(name, w, g, m, v, block_rows):
    rows, cols = w.shape

    def body(w_ref, g_ref, m_ref, v_ref, go_ref, d_ref, nm_ref, nv_ref):
        g = g_ref[...]
        go_ref[...] = g
        d_ref[...], nm_ref[...], nv_ref[...] = _adamw_math(w_ref[...], g, m_ref[...], v_ref[...])

    spec = pl.BlockSpec((block_rows, cols), lambda i: (i, 0))
    return pl.pallas_call(
        body,
        name=name,
        grid=(rows // block_rows,),
        in_specs=[spec] * 4,
        out_specs=[spec] * 4,
        out_shape=[jax.ShapeDtypeStruct(w.shape, F32)] * 4,
        compiler_params=_cparams(dimension_semantics=("arbitrary",)),
    )(w, g, m, v)


def _adamw_small(groups):
    n = len(groups)

    def body(*refs):
        ins, outs = refs[:4 * n], refs[4 * n:]
        for t in range(n):
            w_ref, g_ref, m_ref, v_ref = ins[4 * t:4 * t + 4]
            g = g_ref[0:w_ref.shape[0], :]
            outs[4 * t][...] = g
            outs[4 * t + 1][...], outs[4 * t + 2][...], outs[4 * t + 3][...] = _adamw_math(
                w_ref[...], g, m_ref[...], v_ref[...])

    vm = pl.BlockSpec(memory_space=pltpu.VMEM)
    flat = [a for grp in groups for a in grp]
    outs = pl.pallas_call(
        body,
        name="adamw_small",
        in_specs=[vm] * (4 * n),
        out_specs=[vm] * (4 * n),
        out_shape=[jax.ShapeDtypeStruct(grp[0].shape, F32) for grp in groups for _ in range(4)],
        compiler_params=_cparams(),
    )(*flat)
    return [tuple(outs[4 * t:4 * t + 4]) for t in range(n)]


def _rope_tables():
    half = QK_ROPE // 2
    f32 = np.float32
    inv_freq = (f32(1.0) / (f32(ROPE_THETA) ** (np.arange(half, dtype=f32) / f32(half)))).astype(f32)
    pos = np.arange(N, dtype=f32) - f32(PAD)
    ang = (pos[:, None] * inv_freq[None, :]).astype(f32)
    cos, sin = np.cos(ang).astype(f32), np.sin(ang).astype(f32)
    zero = np.zeros((N, 128 - QK_ROPE), f32)
    return jnp.asarray(np.concatenate([cos, cos, zero], axis=1)), jnp.asarray(np.concatenate([-sin, sin, zero], axis=1))


def kernel(x, meta_tokens, norm_g, w_in, q_norm_g, w_q_b, kv_norm_g, w_kv_b, pool_w, pool_scale, w_out, final_norm_g, loss_target, m_meta_tokens, m_norm_g, m_w_in, m_q_norm_g, m_w_q_b, m_kv_norm_g, m_w_kv_b, m_pool_w, m_pool_scale, m_w_out, m_final_norm_g, v_meta_tokens, v_norm_g, v_w_in, v_q_norm_g, v_w_q_b, v_kv_norm_g, v_w_kv_b, v_pool_w, v_pool_scale, v_w_out, v_final_norm_g):
    tr = lambda a: a[0].T
    win, wq, wkv, h, tgt = _gather_weights(tr(w_in), tr(w_q_b), w_kv_b[0], meta_tokens, x[0], loss_target[0])
    cosf, sinf = _rope_tables()
    gf = final_norm_g.reshape(1, D)

    part = _local_step(h, tgt, norm_g, win, q_norm_g, wq, kv_norm_g, wkv, pool_w[0], pool_scale, w_out[0], m_w_out[0],
                       v_w_out[0], gf, cosf, sinf)

    pw2 = lambda a: a.reshape(len(POOL_WINDOWS) * POOL_GROUP, POOL_GROUP)
    gwinT, gwqT, gwkv, gmeta, gpw, gg, ggf, ggq, ggkv, gps, gloss = _reduce_grads(
        part["dsl"], part["hn"], part["dwq"], part["dwkv"], part["dmeta"], pw2(part["dpw"]), part["dg"],
        part["dgf"], part["dgq"], part["dgkv"], part["dps"], part["loss"])

    r_in = _adamw_rows("adamw_w_in", tr(w_in), gwinT, tr(m_w_in), tr(v_w_in), 248)
    r_out = part["r_out"]
    fn2 = lambda a: a.reshape(1, D)
    r_meta, r_norm, r_gq, r_wq, r_gkv, r_wkv, r_pw, r_ps, r_fn = _adamw_small([
        (meta_tokens, gmeta, m_meta_tokens, v_meta_tokens),
        (norm_g, gg, m_norm_g, v_norm_g),
        (q_norm_g, ggq, m_q_norm_g, v_q_norm_g),
        (tr(w_q_b), gwqT, tr(m_w_q_b), tr(v_w_q_b)),
        (kv_norm_g, ggkv, m_kv_norm_g, v_kv_norm_g),
        (w_kv_b[0], gwkv, m_w_kv_b[0], v_w_kv_b[0]),
        (pw2(pool_w), gpw, pw2(m_pool_w), pw2(v_pool_w)),
        (pool_scale, gps, m_pool_scale, v_pool_scale),
        (fn2(final_norm_g), ggf, fn2(m_final_norm_g), fn2(v_final_norm_g)),
    ])
    untr = lambda a: a.T[None]
    pw4 = lambda a: a.reshape(1, len(POOL_WINDOWS), POOL_GROUP, POOL_GROUP)
    per_kind = [[
        r_meta[kind], r_norm[kind], untr(r_in[kind]), r_gq[kind], untr(r_wq[kind]), r_gkv[kind], r_wkv[kind][None],
        pw4(r_pw[kind]), r_ps[kind], r_out[kind][None], r_fn[kind].reshape(D),
    ] for kind in range(4)]
    return (gloss[0, 0], part["gx"][None], *per_kind[0], *per_kind[1], *per_kind[2], *per_kind[3])
```

```python
import jax
import jax.numpy as jnp
import numpy as np
from jax import lax
from jax.experimental import pallas as pl
from jax.experimental.pallas import tpu as pltpu

F32 = jnp.float32
BF16 = jnp.bfloat16

D = 1024
S = 2048
N_META = 16
PAD = 112
HEAD_ROWS = PAD + N_META
N = HEAD_ROWS + S
D_POOL = 512
POOL_WINDOWS = (2, 4, 8, 16)
POOL_GROUP = 128
HALO = 16
HEADS = 4
QK_NOPE = 128
QK_ROPE = 64
QK = QK_NOPE + QK_ROPE
V_HEAD = 128
Q_LORA = 256
KV_LORA = 128
D_IN = 1984
EPS = 1e-6
ROPE_THETA = 10000.0
SCALE = QK ** -0.5
CHIPS = 4

ROWS_FWD = 544
ROWS_MID = 544
ROWS_BWD = 544
TK = 128
TQ = 256
NQ = S // TQ
HEADS_PER_STEP_BWD = 2

O_PI, O_PG, O_CQ, O_CKV, O_KR, O_AG = 0, 512, 1024, 1280, 1408, 1472
O_KR_END = O_KR + 128
SHARD_IN = D_IN // CHIPS
SHARD_PAD = 512
SHARD_OUT = D // CHIPS

LR, B1, B2, ADAM_EPS, WD, STEP = 0.001, 0.9, 0.999, 1e-08, 0.01, 10
C1 = 1.0 - B1**STEP
C2 = 1.0 - B2**STEP

VMEM_LIMIT = 60 * 1024 * 1024
MESH = pl.DeviceIdType.MESH
NEG = -1e30

VEC_ROWS = 8
V_GQ, V_GKV, V_PS, V_LOSS = 0, 256, 384, 896


def _cparams(**kw):
    return pltpu.CompilerParams(vmem_limit_bytes=VMEM_LIMIT, **kw)


def _nt(a, b):
    return lax.dot_general(a, b, (((1,), (1,)), ((), ())), preferred_element_type=F32)


def _tn(a, b):
    return lax.dot_general(a, b, (((0,), (0,)), ((), ())), preferred_element_type=F32)


def _nn(a, b):
    return jnp.dot(a, b, preferred_element_type=F32)


def _swap64(t):
    return pltpu.roll(t, 32, 1) + pltpu.roll(t, 96, 1)


def _sigmoid(x):
    return 1.0 / (1.0 + jnp.exp(-x))


def _low_lanes():
    return (lax.broadcasted_iota(jnp.int32, (1, 128), 1) < QK_ROPE).astype(F32)


def _rows(w, rows):
    return pl.BlockSpec((rows, w), lambda i: (i, 0))


def _const(*shape):
    return pl.BlockSpec(shape, lambda *_: (0,) * len(shape), pipeline_mode=pl.Buffered(1))


STAT_GROUPS = HEADS // HEADS_PER_STEP_BWD


def _stat_slot(head):
    return head // HEADS_PER_STEP_BWD, head % HEADS_PER_STEP_BWD


N_PEERS = 4


def _peer_signal(x, y, c):
    barrier = pltpu.get_barrier_semaphore()
    peers = [(x, y, 1 - c)] + [(x ^ fx, y ^ fy, c) for fx, fy in _CHIP_RELS[1:]]
    assert len(peers) == N_PEERS
    for peer in peers:
        pl.semaphore_signal(barrier, inc=1, device_id=peer, device_id_type=MESH)


def _peer_wait():
    pl.semaphore_wait(pltpu.get_barrier_semaphore(), N_PEERS)


def _attn_tiles():
    return [(0, TK, TK)] + [(TK + TQ * t, TQ, TK + TQ * (t + 1)) for t in range(NQ)]


def _masked_scores(q, k, rows, klen):
    s = _nt(q, k)
    col = lax.broadcasted_iota(jnp.int32, (1, TK), 1)
    head_bias = jnp.where(col >= PAD, 0.0, NEG)
    if klen == TK:
        return s + head_bias
    r = lax.broadcasted_iota(jnp.int32, (rows, 1), 0) >> 6
    c = lax.broadcasted_iota(jnp.int32, (1, rows), 1) >> 6
    diag_bias = jnp.where(c <= r, 0.0, NEG)
    parts = [s[:, 0:TK] + head_bias]
    if klen - rows > TK:
        parts.append(s[:, TK:klen - rows])
    parts.append(s[:, klen - rows:klen] + diag_bias)
    return jnp.concatenate(parts, axis=1)


def _fwd_in(h, norm_g, win, gq, wq, gkv, wkv, cosf, sinf):
    tr = ROWS_FWD

    def body(h_ref, g_ref, win_ref, gq_ref, wq_ref, gkv_ref, wkv_ref, cos_ref, sin_ref,
             pi_ref, pg_ref, cq_ref, ckv_ref, ag_ref, q_ref, k_ref, v_ref):
        h = h_ref[...]
        r = lax.rsqrt(jnp.mean(h * h, axis=-1, keepdims=True) + EPS)
        hn = ((h * r) * g_ref[...]).astype(BF16)
        u = _nt(hn, win_ref[0:O_KR_END, :])
        pi_ref[...] = u[:, O_PI:O_PG]
        pg_ref[...] = u[:, O_PG:O_CQ]
        cq = u[:, O_CQ:O_CKV]
        ckv = u[:, O_CKV:O_KR]
        cq_ref[...] = cq
        ckv_ref[...] = ckv
        ag_ref[...] = _nt(hn, win_ref[O_AG:D_IN, :])
        cosv = cos_ref[...]
        sinv = sin_ref[...]
        kr = u[:, O_KR:O_KR_END] * _low_lanes()
        kr = (kr * cosv + _swap64(kr) * sinv).astype(BF16)
        rq = lax.rsqrt(jnp.mean(cq * cq, axis=-1, keepdims=True) + EPS)
        cqn = ((cq * rq) * gq_ref[...]).astype(BF16)
        rkv = lax.rsqrt(jnp.mean(ckv * ckv, axis=-1, keepdims=True) + EPS)
        ckvn = ((ckv * rkv) * gkv_ref[...]).astype(BF16)
        for hd in range(HEADS):
            qh = _nt(cqn, wq_ref[hd]) * SCALE
            z = qh[:, QK_NOPE:]
            q_ref[hd, :, 0:QK_NOPE] = qh[:, 0:QK_NOPE].astype(BF16)
            q_ref[hd, :, QK_NOPE:] = (z * cosv + _swap64(z) * sinv).astype(BF16)
            kvh = _nn(ckvn, wkv_ref[hd])
            k_ref[hd, :, 0:QK_NOPE] = kvh[:, 0:QK_NOPE].astype(BF16)
            k_ref[hd, :, QK_NOPE:] = kr
            v_ref[hd] = kvh[:, QK_NOPE:].astype(BF16)

    head = lambda w: pl.BlockSpec((HEADS, tr, w), lambda i: (0, i, 0))
    return pl.pallas_call(
        body,
        name="fwd_in",
        grid=(N // tr,),
        in_specs=[
            _rows(D, tr), _const(1, D), _const(D_IN, D), _const(1, Q_LORA), _const(HEADS, 256, Q_LORA),
            _const(1, KV_LORA), _const(HEADS, KV_LORA, 256), _rows(128, tr), _rows(128, tr),
        ],
        out_specs=[_rows(D_POOL, tr), _rows(D_POOL, tr), _rows(Q_LORA, tr), _rows(KV_LORA, tr), _rows(D_POOL, tr),
                   head(256), head(256), head(V_HEAD)],
        out_shape=[
            jax.ShapeDtypeStruct((N, D_POOL), F32), jax.ShapeDtypeStruct((N, D_POOL), F32),
            jax.ShapeDtypeStruct((N, Q_LORA), F32), jax.ShapeDtypeStruct((N, KV_LORA), F32),
            jax.ShapeDtypeStruct((N, D_POOL), F32),
            jax.ShapeDtypeStruct((HEADS, N, 256), BF16), jax.ShapeDtypeStruct((HEADS, N, 256), BF16),
            jax.ShapeDtypeStruct((HEADS, N, V_HEAD), BF16),
        ],
        compiler_params=_cparams(dimension_semantics=("arbitrary",)),
    )(h, norm_g, win, gq, wq, gkv, wkv, cosf, sinf)


def _attn_fwd(q, k, v, wout_s):
    tiles = _attn_tiles()
    n_t = len(tiles)
    half = SHARD_OUT // 2
    send_step = 2
    fwd_step = n_t - 2

    def body(q_hbm, k_hbm, v_hbm, wout_ref, o_hbm, lse_ref, wout_o, q_buf, k_buf, v_buf, o_buf, s_wout, in_sems, out_sems,
             ici_send, ici_recv, fwd_send, fwd_recv, own_sem):
        step = pl.program_id(0)
        x, y, c = lax.axis_index("x"), lax.axis_index("y"), lax.axis_index("c")
        me = 2 * x + y

        def chip_of(rel):
            fx, fy = _CHIP_RELS[rel]
            return 2 * (x ^ fx) + (y ^ fy)

        def place(chip, core):
            return wout_o.at[pl.ds(pl.multiple_of(SHARD_OUT * chip + half * core, half), half), :]

        def ici_copy(rel, src_chip, to):
            return _remote(s_wout.at[pl.ds(pl.multiple_of(half * c, half), half), :], place(src_chip, c),
                           ici_send.at[rel - 1], ici_recv.at[rel - 1], to)

        def fwd_copy(rel, core, to):
            spot = place(chip_of(rel), core)
            return _remote(spot, spot, fwd_send.at[rel - 1], fwd_recv.at[rel - 1], to)

        own = pltpu.make_async_copy(s_wout, wout_o.at[pl.ds(pl.multiple_of(SHARD_OUT * me, SHARD_OUT), SHARD_OUT), :], own_sem)

        @pl.when(step == 0)
        def _():
            _peer_signal(x, y, c)
            s_wout[...] = wout_ref[...].astype(BF16)
            own.start()

        @pl.when(step == send_step)
        def _():
            _peer_wait()
            for rel in (1, 2, 3):
                fx, fy = _CHIP_RELS[rel]
                ici_copy(rel, me, (x ^ fx, y ^ fy, c)).start()

        @pl.when(step == fwd_step)
        def _():
            for rel in (1, 2, 3):
                ici_copy(rel, chip_of(rel), (x, y, c)).wait_recv()
                fwd_copy(rel, c, (x, y, 1 - c)).start()

        def finish_wout():
            for rel in (1, 2, 3):
                fwd_copy(rel, 1 - c, (x, y, c)).wait_recv()
            for rel in (1, 2, 3):
                ici_copy(rel, me, (x, y, c)).wait_send()
                fwd_copy(rel, c, (x, y, c)).wait_send()
            own.wait()

        def loads(idx):
            q0, rows, _ = tiles[idx]
            rs = pl.ds(q0, rows)
            return [pltpu.make_async_copy(src.at[:, rs, :], dst.at[:, rs, :], in_sems.at[a, idx % 2])
                    for a, (src, dst) in enumerate(((q_hbm, q_buf), (k_hbm, k_buf), (v_hbm, v_buf)))]

        def store(idx):
            q0, rows, _ = tiles[idx]
            return pltpu.make_async_copy(o_buf.at[idx % 2, pl.ds(0, rows), :], o_hbm.at[pl.ds(q0, rows), :],
                                         out_sems.at[idx % 2])

        @pl.when(step == 0)
        def _():
            lse_ref[...] = jnp.zeros_like(lse_ref)
            for cp in loads(0):
                cp.start()

        for idx, (q0, rows, klen) in enumerate(tiles):
            @pl.when(step == idx)
            def _(idx=idx, q0=q0, rows=rows, klen=klen):
                for cp in loads(idx):
                    cp.wait()
                if idx + 1 < n_t:
                    for cp in loads(idx + 1):
                        cp.start()
                if idx >= 2:
                    store(idx - 2).wait()
                for hd in range(HEADS):
                    s = _masked_scores(q_buf[hd, q0:q0 + rows, :], k_buf[hd, 0:klen, :], rows, klen)
                    m = jnp.max(s, axis=-1, keepdims=True)
                    p = jnp.exp(s - m)
                    l = jnp.sum(p, axis=-1, keepdims=True)
                    o_buf[idx % 2, 0:rows, hd * V_HEAD:(hd + 1) * V_HEAD] = _nn(p.astype(BF16), v_buf[hd, 0:klen, :]) / l
                    grp, lane = _stat_slot(hd)
                    lse_ref[grp, q0:q0 + rows, lane:lane + 1] = m + jnp.log(l)
                store(idx).start()
                if idx == n_t - 1:
                    store(idx - 1).wait()
                    store(idx).wait()
                    finish_wout()

    hbm = pl.BlockSpec(memory_space=pl.ANY)
    return pl.pallas_call(
        body,
        name="attn_fwd",
        grid=(n_t,),
        in_specs=[hbm, hbm, hbm, _const(SHARD_OUT, D)],
        out_specs=[hbm, _const(STAT_GROUPS, N, 128), hbm],
        out_shape=[jax.ShapeDtypeStruct((N, HEADS * V_HEAD), F32), jax.ShapeDtypeStruct((STAT_GROUPS, N, 128), F32),
                   jax.ShapeDtypeStruct((D, D), BF16)],
        scratch_shapes=[pltpu.VMEM((HEADS, N, 256), BF16), pltpu.VMEM((HEADS, N, 256), BF16),
                        pltpu.VMEM((HEADS, N, V_HEAD), BF16), pltpu.VMEM((2, TQ, HEADS * V_HEAD), F32),
                        pltpu.VMEM((SHARD_OUT, D), BF16),
                        pltpu.SemaphoreType.DMA((3, 2)), pltpu.SemaphoreType.DMA((2,))]
        + [pltpu.SemaphoreType.DMA((3,))] * 4 + [pltpu.SemaphoreType.DMA],
        compiler_params=_cparams(dimension_semantics=("arbitrary",), collective_id=1),
    )(q, k, v, wout_s)


def _inv_count(row0, rows, w):
    row = row0 + lax.broadcasted_iota(jnp.int32, (rows, 1), 0)
    return 1.0 / jnp.clip(row - (PAD - 1), 1, w).astype(F32)


def _mid(h, tgt, pool_in, pool_gate, attn_gate, attn, pool_w, pool_scale, wout, gf):
    tr = ROWS_MID
    per = tr // HALO
    ng = len(POOL_WINDOWS)

    def body(h_ref, t_ref, pin_ref, halo_ref, pg_ref, ag_ref, at_ref, pw_ref, ps_ref, wout_ref, gf_ref,
             dh2_ref, do_ref, delta_ref, dag_ref, dpg_ref, dpl_ref, dwout_ref, dpw_ref, dps_ref, dgf_ref, loss_ref):
        i = pl.program_id(0)

        @pl.when(i == 0)
        def _():
            dwout_ref[...] = jnp.zeros_like(dwout_ref)
            dpw_ref[...] = jnp.zeros_like(dpw_ref)
            dps_ref[...] = jnp.zeros_like(dps_ref)
            dgf_ref[...] = jnp.zeros_like(dgf_ref)
            loss_ref[...] = jnp.zeros_like(loss_ref)

        row0 = i * tr
        real = (row0 + lax.broadcasted_iota(jnp.int32, (tr, 1), 0)) >= HEAD_ROWS
        h = h_ref[...]

        halo = jnp.where(i > 0, halo_ref[...], 0.0)
        ext = jnp.concatenate([halo, pin_ref[...]], axis=0)
        pooled = []
        for g, w in enumerate(POOL_WINDOWS):
            e = ext[:, g * POOL_GROUP:(g + 1) * POOL_GROUP]
            acc = e
            shift = 1
            while shift < w:
                acc = acc + pltpu.roll(acc, shift, 0)
                shift *= 2
            pooled.append((acc[HALO:] * _inv_count(row0, tr, w) - e[HALO:]).astype(BF16))
        pw = [pw_ref[g].astype(BF16) for g in range(ng)]
        mixed = jnp.concatenate([_nn(pooled[g], pw[g]) for g in range(ng)], axis=1)
        ps = ps_ref[...]
        mixed_s = mixed * ps
        pg = pg_ref[...]
        sig_p = _sigmoid(pg)
        silu_p = pg * sig_p
        pool_out = (silu_p * mixed_s).astype(BF16)
        ag = ag_ref[...]
        sig_a = _sigmoid(ag)
        silu_a = ag * sig_a
        at = at_ref[...]
        attn_out = (silu_a * at).astype(BF16)
        cat = jnp.concatenate([pool_out, attn_out], axis=1)
        h2 = h + _nn(cat, wout_ref[...])

        r2 = lax.rsqrt(jnp.mean(h2 * h2, axis=-1, keepdims=True) + EPS)
        n2 = h2 * r2
        gfv = gf_ref[...]
        err = jnp.where(real, n2 * gfv - t_ref[...], 0.0)
        loss_ref[...] += jnp.sum(jnp.sum(err * err, axis=-1, keepdims=True), axis=0, keepdims=True) * (0.5 / D)
        dy = err * (1.0 / D)
        dgf_ref[...] += jnp.sum(dy * n2, axis=0, keepdims=True)
        dn = dy * gfv
        dh2 = r2 * (dn - n2 * jnp.mean(dn * n2, axis=-1, keepdims=True))
        dh2_ref[...] = dh2
        dh2b = dh2.astype(BF16)

        dwout_ref[...] += _tn(cat, dh2b)
        dcat = _nt(dh2b, wout_ref[...])
        dpo = dcat[:, 0:D_POOL]
        dao = dcat[:, D_POOL:D]
        do = dao * silu_a
        prod = do * at
        delta_ref[...] = jnp.zeros_like(delta_ref)
        for hd in range(HEADS):
            grp, lane = _stat_slot(hd)
            cols = slice(hd * V_HEAD, (hd + 1) * V_HEAD)
            do_ref[grp, :, lane * V_HEAD:(lane + 1) * V_HEAD] = do[:, cols].astype(BF16)
            delta_ref[grp, :, lane:lane + 1] = jnp.sum(prod[:, cols], axis=-1, keepdims=True)
        dag_ref[...] = (dao * at * (sig_a * (1.0 + ag * (1.0 - sig_a)))).astype(BF16)
        dmixed_s = dpo * silu_p
        dpg_ref[...] = (dpo * mixed_s * (sig_p * (1.0 + pg * (1.0 - sig_p)))).astype(BF16)
        dps_ref[...] += jnp.sum(dmixed_s * mixed, axis=0, keepdims=True)
        dmixed = (dmixed_s * ps).astype(BF16)
        dpl = []
        for g in range(ng):
            dm = dmixed[:, g * POOL_GROUP:(g + 1) * POOL_GROUP]
            dpl.append(_nt(dm, pw[g]))
            dpw_ref[g] += _tn(pooled[g], dm)
        dpl_ref[...] = jnp.concatenate(dpl, axis=1)

    halo_spec = pl.BlockSpec((HALO, D_POOL), lambda i: (jnp.maximum(i * per - 1, 0), 0))
    return pl.pallas_call(
        body,
        name="mid",
        grid=(N // tr,),
        in_specs=[
            _rows(D, tr), _rows(D, tr), _rows(D_POOL, tr), halo_spec, _rows(D_POOL, tr), _rows(D_POOL, tr),
            _rows(D_POOL, tr), _const(ng, POOL_GROUP, POOL_GROUP), _const(1, D_POOL), _const(D, D), _const(1, D),
        ],
        out_specs=[
            _rows(D, tr), pl.BlockSpec((STAT_GROUPS, tr, HEADS_PER_STEP_BWD * V_HEAD), lambda i: (0, i, 0)),
            pl.BlockSpec((STAT_GROUPS, tr, 128), lambda i: (0, i, 0)),
            _rows(D_POOL, tr), _rows(D_POOL, tr), _rows(D_POOL, tr),
            _const(D, D), _const(ng, POOL_GROUP, POOL_GROUP), _const(1, D_POOL), _const(1, D), _const(1, 128),
        ],
        out_shape=[
            jax.ShapeDtypeStruct((N, D), F32), jax.ShapeDtypeStruct((STAT_GROUPS, N, HEADS_PER_STEP_BWD * V_HEAD), BF16),
            jax.ShapeDtypeStruct((STAT_GROUPS, N, 128), F32),
            jax.ShapeDtypeStruct((N, D_POOL), BF16), jax.ShapeDtypeStruct((N, D_POOL), BF16),
            jax.ShapeDtypeStruct((N, D_POOL), F32), jax.ShapeDtypeStruct((D, D), F32),
            jax.ShapeDtypeStruct((ng, POOL_GROUP, POOL_GROUP), F32),
            jax.ShapeDtypeStruct((1, D_POOL), F32), jax.ShapeDtypeStruct((1, D), F32), jax.ShapeDtypeStruct((1, 128), F32),
        ],
        compiler_params=_cparams(dimension_semantics=("arbitrary",)),
    )(h, tgt, pool_in, pool_in, pool_gate, attn_gate, attn, pool_w, pool_scale, wout, gf)


def _unrope(dy, cosv, sinv):
    return dy * cosv + _swap64(dy * sinv) * _low_lanes()


def _attn_bwd(q, k, v, do, lse, delta, cosf, sinf, dwout):
    tiles = _attn_tiles()
    hp = HEADS_PER_STEP_BWD
    n_g = HEADS // hp
    n_t = len(tiles)
    half = SHARD_OUT // 2
    swap_at, send_at, sum_at = (0, 3), (0, 5), (n_g - 1, n_t // 2)

    def body(q_hbm, k_hbm, v_hbm, do_hbm, lse_ref, delta_ref, cos_ref, sin_ref, dwout_hbm, dq_hbm, dkv_ref, dkr_ref,
             gwout_ref, q_buf, k_buf, v_buf, do_buf, dq_buf, dk_acc, dv_acc, own_w, sib_w, stage_w, recv_w, gw_buf,
             in_sems, out_sems, ow_sems, d2d_send, d2d_recv, ici_send, ici_recv, fin_send, fin_recv):
        grp = pl.program_id(0)
        step = pl.program_id(1)
        heads = pl.ds(grp * hp, hp)
        x, y, c = lax.axis_index("x"), lax.axis_index("y"), lax.axis_index("c")
        sibling = (x, y, 1 - c)

        def chip_of(rel):
            fx, fy = _CHIP_RELS[rel]
            return 2 * (x ^ fx) + (y ^ fy)

        def piece(chip, core):
            return dwout_hbm.at[pl.ds(pl.multiple_of(SHARD_OUT * chip + half * core, half), half), :]

        def own_load(rel):
            return pltpu.make_async_copy(piece(chip_of(rel), c), own_w.at[rel], ow_sems.at[rel])

        def d2d_copy(rel):
            return _remote(piece(chip_of(rel), 1 - c), sib_w.at[rel], d2d_send.at[rel], d2d_recv.at[rel], sibling)

        def ici_copy(rel):
            fx, fy = _CHIP_RELS[rel]
            return _remote(stage_w.at[rel - 1], recv_w.at[rel - 1], ici_send.at[rel - 1], ici_recv.at[rel - 1],
                           (x ^ fx, y ^ fy, c))

        def fin_copy(core):
            spot = gw_buf.at[pl.ds(pl.multiple_of(half * core, half), half), :]
            return _remote(spot, spot, fin_send.at[0], fin_recv.at[0], sibling)

        @pl.when((grp == 0) & (step == 0))
        def _():
            _peer_signal(x, y, c)
            for rel in (1, 2, 3, 0):
                own_load(rel).start()

        @pl.when((grp == swap_at[0]) & (step == swap_at[1]))
        def _():
            _peer_wait()
            for rel in (1, 2, 3, 0):
                d2d_copy(rel).start()

        @pl.when((grp == send_at[0]) & (step == send_at[1]))
        def _():
            for rel in (1, 2, 3):
                own_load(rel).wait()
                d2d_copy(rel).wait_recv()
                stage_w[rel - 1] = (own_w[rel] + sib_w[rel]).astype(BF16)
                ici_copy(rel).start()

        @pl.when((grp == sum_at[0]) & (step == sum_at[1]))
        def _():
            own_load(0).wait()
            d2d_copy(0).wait_recv()
            total = own_w[0] + sib_w[0]
            for rel in (1, 2, 3):
                ici_copy(rel).wait_recv()
                total = total + recv_w[rel - 1].astype(F32)
            gw_buf[pl.ds(pl.multiple_of(half * c, half), half), :] = total
            fin_copy(c).start()

        def finish_dwout():
            fin_copy(1 - c).wait_recv()
            for rel in (0, 1, 2, 3):
                d2d_copy(rel).wait_send()
            for rel in (1, 2, 3):
                ici_copy(rel).wait_send()
            fin_copy(c).wait_send()
            gwout_ref[...] = gw_buf[...]

        def loads(g, idx):
            q0, rows, _ = tiles[idx]
            rs = pl.ds(q0, rows)
            par = (g * n_t + idx) % 2
            hs = pl.ds(g * hp, hp)
            pairs = ((q_hbm.at[hs, rs, :], q_buf.at[:, rs, :]), (k_hbm.at[hs, rs, :], k_buf.at[:, rs, :]),
                     (v_hbm.at[hs, rs, :], v_buf.at[:, rs, :]), (do_hbm.at[g, rs, :], do_buf.at[rs, :]))
            return [pltpu.make_async_copy(src, dst, in_sems.at[a, par]) for a, (src, dst) in enumerate(pairs)]

        def store(idx):
            q0, rows, _ = tiles[idx]
            return pltpu.make_async_copy(dq_buf.at[idx % 2, :, pl.ds(0, rows), :], dq_hbm.at[heads, pl.ds(q0, rows), :],
                                         out_sems.at[idx % 2])

        @pl.when(step == 0)
        def _():
            dk_acc[...] = jnp.zeros_like(dk_acc)
            dv_acc[...] = jnp.zeros_like(dv_acc)

        @pl.when((step == 0) & (grp == 0))
        def _():
            dkr_ref[...] = jnp.zeros_like(dkr_ref)
            for cp in loads(grp, 0):
                cp.start()

        for idx, (q0, rows, klen) in enumerate(tiles):
            @pl.when(step == idx)
            def _(idx=idx, q0=q0, rows=rows, klen=klen):
                for cp in loads(grp, idx):
                    cp.wait()
                if idx + 1 < n_t:
                    for cp in loads(grp, idx + 1):
                        cp.start()
                if idx >= 2:
                    store(idx - 2).wait()
                qs = pl.ds(q0, rows)
                for hd in range(hp):
                    qv = q_buf[hd, qs, :]
                    kv = k_buf[hd, 0:klen, :]
                    p = jnp.exp(_masked_scores(qv, kv, rows, klen) - lse_ref[0, qs, hd:hd + 1])
                    dob = do_buf[qs, hd * V_HEAD:(hd + 1) * V_HEAD]
                    ds = (p * (_nt(dob, v_buf[hd, 0:klen, :]) - delta_ref[0, qs, hd:hd + 1])).astype(BF16)
                    dq = _nn(ds, kv) * SCALE
                    dq_buf[idx % 2, hd, 0:rows, 0:QK_NOPE] = dq[:, 0:QK_NOPE].astype(BF16)
                    dq_buf[idx % 2, hd, 0:rows, QK_NOPE:] = _unrope(dq[:, QK_NOPE:], cos_ref[qs, :], sin_ref[qs, :]).astype(BF16)
                    dk_acc[hd, 0:klen, :] += _tn(ds, qv)
                    dv_acc[hd, 0:klen, :] += _tn(p.astype(BF16), dob)
                store(idx).start()

        @pl.when(step == n_t - 1)
        def _():
            @pl.when(grp + 1 < n_g)
            def _():
                for cp in loads(grp + 1, 0):
                    cp.start()

            for hd in range(hp):
                dkv_ref[hd, :, 0:QK_NOPE] = dk_acc[hd, :, 0:QK_NOPE].astype(BF16)
                dkv_ref[hd, :, QK_NOPE:] = dv_acc[hd].astype(BF16)
                dkr_ref[...] += dk_acc[hd, :, QK_NOPE:]
            store(n_t - 2).wait()
            store(n_t - 1).wait()

            @pl.when(grp == n_g - 1)
            def _():
                finish_dwout()

    hbm = pl.BlockSpec(memory_space=pl.ANY)
    stat = pl.BlockSpec((1, N, 128), lambda g, t: (g, 0, 0), pipeline_mode=pl.Buffered(1))
    piece_f32 = lambda lead: pltpu.VMEM((lead, half, D), F32)
    piece_bf16 = lambda lead: pltpu.VMEM((lead, half, D), BF16)
    return pl.pallas_call(
        body,
        name="attn_bwd",
        grid=(n_g, n_t),
        in_specs=[hbm, hbm, hbm, hbm, stat, stat, _const(N, 128), _const(N, 128), hbm],
        out_specs=[hbm, pl.BlockSpec((hp, N, 256), lambda g, t: (g, 0, 0), pipeline_mode=pl.Buffered(1)), _const(N, 128),
                   _const(SHARD_OUT, D)],
        out_shape=[
            jax.ShapeDtypeStruct((HEADS, N, 256), BF16), jax.ShapeDtypeStruct((HEADS, N, 256), BF16),
            jax.ShapeDtypeStruct((N, 128), F32), jax.ShapeDtypeStruct((SHARD_OUT, D), F32),
        ],
        scratch_shapes=[pltpu.VMEM((hp, N, 256), BF16), pltpu.VMEM((hp, N, 256), BF16), pltpu.VMEM((hp, N, V_HEAD), BF16),
                        pltpu.VMEM((N, hp * V_HEAD), BF16), pltpu.VMEM((2, hp, TQ, 256), BF16),
                        pltpu.VMEM((hp, N, 256), F32), pltpu.VMEM((hp, N, V_HEAD), F32),
                        piece_f32(CHIPS), piece_f32(CHIPS), piece_bf16(3), piece_bf16(3), pltpu.VMEM((SHARD_OUT, D), F32),
                        pltpu.SemaphoreType.DMA((4, 2)), pltpu.SemaphoreType.DMA((2,)), pltpu.SemaphoreType.DMA((CHIPS,)),
                        pltpu.SemaphoreType.DMA((CHIPS,)), pltpu.SemaphoreType.DMA((CHIPS,)),
                        pltpu.SemaphoreType.DMA((3,)), pltpu.SemaphoreType.DMA((3,)),
                        pltpu.SemaphoreType.DMA((1,)), pltpu.SemaphoreType.DMA((1,))],
        compiler_params=_cparams(dimension_semantics=("arbitrary", "arbitrary"), collective_id=2),
    )(q, k, v, do, lse, delta, cosf, sinf, dwout)


def _bwd_in(h, dh2, dq, dkv, dkr, cq, ckv, dpl, dpg, dag, norm_g, win, gq, wq, gkv, wkv, cosf, sinf, adam_out):
    tr = ROWS_BWD
    nb = N // tr
    per = tr // HALO
    lead = HEAD_ROWS
    adam_rows = SHARD_OUT // nb

    def body(h_ref, dh2_ref, dq_ref, dkv_ref, dkr_ref, cq_ref, ckv_ref, dpl_ref, halo_ref, dpg_ref, dag_ref,
             g_ref, win_ref, gq_ref, wq_ref, gkv_ref, wkv_ref, cos_ref, sin_ref, aw_ref, ag_ref, am_ref, av_ref,
             gx_ref, dmeta_ref, dsl_ref, hn_ref, dwq_ref, dwkv_ref, dg_ref, dgq_ref, dgkv_ref, ago_ref, ad_ref, anm_ref, anv_ref,
             dh_buf, gx_sem):
        i = pl.program_id(0)
        grad_out = ag_ref[...]
        ago_ref[...] = grad_out
        ad_ref[...], anm_ref[...], anv_ref[...] = _adamw_math(aw_ref[...], grad_out, am_ref[...], av_ref[...])

        @pl.when(i == 0)
        def _():
            dwq_ref[...] = jnp.zeros_like(dwq_ref)
            dwkv_ref[...] = jnp.zeros_like(dwkv_ref)
            dg_ref[...] = jnp.zeros_like(dg_ref)
            dgq_ref[...] = jnp.zeros_like(dgq_ref)
            dgkv_ref[...] = jnp.zeros_like(dgkv_ref)

        row0 = i * tr
        h = h_ref[...]
        r = lax.rsqrt(jnp.mean(h * h, axis=-1, keepdims=True) + EPS)
        n = h * r
        gv = g_ref[...]
        hn = (n * gv).astype(BF16)
        cq = cq_ref[...]
        rq = lax.rsqrt(jnp.mean(cq * cq, axis=-1, keepdims=True) + EPS)
        nq = cq * rq
        gqv = gq_ref[...]
        cqn = (nq * gqv).astype(BF16)
        dcqn = jnp.zeros((tr, Q_LORA), F32)
        for hd in range(HEADS):
            dqf = dq_ref[hd]
            dcqn = dcqn + _nn(dqf, wq_ref[hd])
            dwq_ref[hd] += _tn(dqf, cqn)
        dgq_ref[...] += jnp.sum(dcqn * nq, axis=0, keepdims=True)
        dnq = dcqn * gqv
        dcq = rq * (dnq - nq * jnp.mean(dnq * nq, axis=-1, keepdims=True))

        ckv = ckv_ref[...]
        rkv = lax.rsqrt(jnp.mean(ckv * ckv, axis=-1, keepdims=True) + EPS)
        nkv = ckv * rkv
        gkvv = gkv_ref[...]
        ckvn = (nkv * gkvv).astype(BF16)
        dckvn = jnp.zeros((tr, KV_LORA), F32)
        for hd in range(HEADS):
            dkv = dkv_ref[hd]
            dckvn = dckvn + _nt(dkv, wkv_ref[hd])
            dwkv_ref[hd] += _tn(ckvn, dkv)
        dgkv_ref[...] += jnp.sum(dckvn * nkv, axis=0, keepdims=True)
        dnkv = dckvn * gkvv
        dckv = rkv * (dnkv - nkv * jnp.mean(dnkv * nkv, axis=-1, keepdims=True))
        dkr = _unrope(dkr_ref[...], cos_ref[...], sin_ref[...])

        cur = dpl_ref[...]
        halo = jnp.where(i < nb - 1, halo_ref[...], 0.0)
        dpi = []
        for g, w in enumerate(POOL_WINDOWS):
            sl = slice(g * POOL_GROUP, (g + 1) * POOL_GROUP)
            a = jnp.concatenate([cur[:, sl] * _inv_count(row0, tr, w), halo[:, sl] * _inv_count(row0 + tr, HALO, w)], axis=0)
            acc = a
            shift = 1
            while shift < w:
                acc = acc + pltpu.roll(acc, tr + HALO - shift, 0)
                shift *= 2
            dpi.append(acc[0:tr] - cur[:, sl])

        du = jnp.concatenate([t.astype(BF16) for t in dpi] + [dpg_ref[...]] + [t.astype(BF16) for t in (dcq, dckv, dkr)],
                             axis=1)
        dagb = dag_ref[...]
        by_row = jnp.concatenate(dpi + [dpg_ref[...].astype(F32), dcq, dckv, dkr[:, 0:QK_ROPE], dagb.astype(F32),
                                        jnp.zeros((tr, SHARD_PAD - SHARD_IN), F32)], axis=1)
        for chip in range(CHIPS):
            dsl_ref[chip] = by_row[:, SHARD_IN * chip:SHARD_IN * chip + SHARD_PAD].astype(BF16)
        hn_ref[...] = hn
        dhn = _nn(du, win_ref[0:O_KR_END, :]) + _nn(dagb, win_ref[O_AG:D_IN, :])
        dg_ref[...] += jnp.sum(dhn * n, axis=0, keepdims=True)
        dn = dhn * gv
        dh = dh2_ref[...] + r * (dn - n * jnp.mean(dn * n, axis=-1, keepdims=True))

        first = pltpu.make_async_copy(dh_buf.at[pl.ds(lead, tr - lead), :], gx_ref.at[pl.ds(0, tr - lead), :], gx_sem)
        later = lambda step: pltpu.make_async_copy(
            dh_buf, gx_ref.at[pl.ds(pl.multiple_of(step * tr - lead, 16), tr), :], gx_sem)

        @pl.when(i == 1)
        def _():
            first.wait()

        @pl.when(i > 1)
        def _():
            later(i - 1).wait()

        dh_buf[...] = dh

        @pl.when(i == 0)
        def _():
            first.start()
            for chip in range(CHIPS):
                dmeta_ref[chip] = dh[PAD:HEAD_ROWS, chip * 256:(chip + 1) * 256]

        @pl.when(i > 0)
        def _():
            later(i).start()

        @pl.when(i == nb - 1)
        def _():
            later(i).wait()

    head = lambda w: pl.BlockSpec((HEADS, tr, w), lambda i: (0, i, 0))
    halo_spec = pl.BlockSpec((HALO, D_POOL), lambda i: (jnp.minimum((i + 1) * per, N // HALO - 1), 0))
    return pl.pallas_call(
        body,
        name="bwd_in",
        grid=(nb,),
        in_specs=[
            _rows(D, tr), _rows(D, tr), head(256), head(256), _rows(128, tr), _rows(Q_LORA, tr), _rows(KV_LORA, tr),
            _rows(D_POOL, tr), halo_spec, _rows(D_POOL, tr), _rows(D_POOL, tr),
            _const(1, D), _const(D_IN, D), _const(1, Q_LORA), _const(HEADS, 256, Q_LORA),
            _const(1, KV_LORA), _const(HEADS, KV_LORA, 256), _rows(128, tr), _rows(128, tr),
        ] + [_rows(D, adam_rows)] * 4,
        out_specs=[
            pl.BlockSpec(memory_space=pl.ANY), _const(CHIPS, N_META, 256),
            pl.BlockSpec((CHIPS, tr, SHARD_PAD), lambda i: (0, i, 0)), _rows(D, tr),
            _const(HEADS, 256, Q_LORA),
            _const(HEADS, KV_LORA, 256), _const(1, D), _const(1, Q_LORA), _const(1, KV_LORA),
        ] + [_rows(D, adam_rows)] * 4,
        out_shape=[
            jax.ShapeDtypeStruct((S, D), F32), jax.ShapeDtypeStruct((CHIPS, N_META, 256), F32),
            jax.ShapeDtypeStruct((CHIPS, N, SHARD_PAD), BF16), jax.ShapeDtypeStruct((N, D), BF16),
            jax.ShapeDtypeStruct((HEADS, 256, Q_LORA), F32),
            jax.ShapeDtypeStruct((HEADS, KV_LORA, 256), F32),
            jax.ShapeDtypeStruct((1, D), F32), jax.ShapeDtypeStruct((1, Q_LORA), F32), jax.ShapeDtypeStruct((1, KV_LORA), F32),
        ] + [jax.ShapeDtypeStruct((SHARD_OUT, D), F32)] * 4,
        scratch_shapes=[pltpu.VMEM((tr, D), F32), pltpu.SemaphoreType.DMA],
        compiler_params=_cparams(dimension_semantics=("arbitrary",)),
    )(h, dh2, dq, dkv, dkr, cq, ckv, dpl, dpl, dpg, dag, norm_g, win, gq, wq, gkv, wkv, cosf, sinf, *adam_out)


def _local_step(h, tgt, norm_g, win, gq, wq, gkv, wkv, pool_w, pool_scale, wout_s, m_wout_s, v_wout_s, gf, cosf, sinf):
    pool_in, pool_gate, cq, ckv, attn_gate, q, k, v = _fwd_in(h, norm_g, win, gq, wq, gkv, wkv, cosf, sinf)
    attn, lse, wout = _attn_fwd(q, k, v, wout_s)
    dh2, do, delta, dag, dpg, dpl, dwout, dpw, dps, dgf, loss = _mid(
        h, tgt, pool_in, pool_gate, attn_gate, attn, pool_w, pool_scale, wout, gf)
    dq, dkv, dkr, gwout = _attn_bwd(q, k, v, do, lse, delta, cosf, sinf, dwout)
    gx, dmeta, dsl, hn, dwq, dwkv, dg, dgq, dgkv, *r_out = _bwd_in(
        h, dh2, dq, dkv, dkr, cq, ckv, dpl, dpg, dag, norm_g, win, gq, wq, gkv, wkv, cosf, sinf,
        (wout_s, gwout, m_wout_s, v_wout_s))
    return dict(gx=gx, dmeta=dmeta, dsl=dsl, hn=hn, dwq=dwq, dwkv=dwkv, r_out=tuple(r_out), dg=dg, dgq=dgq,
                dgkv=dgkv, dpw=dpw, dps=dps, dgf=dgf, loss=loss)


_CHIP_RELS = ((0, 0), (1, 0), (0, 1), (1, 1))

_ARR_ROWS = (SHARD_IN, SHARD_OUT, 256, KV_LORA, N_META)
_ARR_COLS = (D, D, Q_LORA, 256, 256)
_PIECES = (
    (0, 0, 256, 0), (0, 256, SHARD_IN - 256, 1),
    (1, 0, 128, 0), (1, 128, 128, 1),
    (2, 0, 128, 0), (2, 128, 128, 1),
    (3, 0, 64, 0), (3, 64, 64, 1),
    (4, 0, N_META, 0),
)
_NP = len(_PIECES)
_PIECE_MAX = (256, 128, 128, 64, N_META)


def _gathered_at(refs, arr, chip, r0, n):
    if arr in (0, 1):
        return refs[arr].at[pl.ds(pl.multiple_of(_ARR_ROWS[arr] * chip + r0, 16), n), :]
    return refs[arr].at[chip, pl.ds(r0, n), :]


def _remote(src, dst, send_sem, recv_sem, to):
    return pltpu.make_async_remote_copy(src_ref=src, dst_ref=dst, send_sem=send_sem, recv_sem=recv_sem,
                                        device_id=to, device_id_type=MESH)


def _gather_weights(winT_s, wqT_s, wkv_s, meta_s, x2, tgt2):
    arrays = (0, 2, 3, 4)

    def body(win_ref, wq_ref, wkv_ref, meta_ref, x_ref, t_ref, win_o, wq_o, wkv_o, h_o, tp_o,
             s_win, s_wq, s_wkv, meta_all, head_buf, x_buf, t_buf, ici_send, ici_recv, fwd_send, fwd_recv,
             loc_sems, own_sems):
        x, y, c = lax.axis_index("x"), lax.axis_index("y"), lax.axis_index("c")
        me = 2 * x + y
        stage = (s_win, None, s_wq, s_wkv, meta_ref)
        outs = (win_o, None, wq_o, wkv_o, meta_all)

        _peer_signal(x, y, c)

        frames = pl.ds(HEAD_ROWS, S)
        loads = [pltpu.make_async_copy(x_ref, x_buf, loc_sems.at[0]), pltpu.make_async_copy(t_ref, t_buf, loc_sems.at[1])]
        local = [pltpu.make_async_copy(x_buf, h_o.at[frames, :], loc_sems.at[0]),
                 pltpu.make_async_copy(t_buf, tp_o.at[frames, :], loc_sems.at[1])]
        for cp in loads:
            cp.start()

        s_win[...] = win_ref[...].astype(BF16)
        s_wq[0:QK, :] = wq_ref[...].astype(BF16)
        s_wq[QK:256, :] = jnp.zeros((256 - QK, Q_LORA), BF16)
        s_wkv[...] = wkv_ref[...].astype(BF16)
        head_buf[...] = jnp.zeros_like(head_buf)
        zeros = pltpu.make_async_copy(head_buf, tp_o.at[pl.ds(0, HEAD_ROWS), :], loc_sems.at[2])
        zeros.start()

        def chip_of(rel):
            fx, fy = _CHIP_RELS[rel]
            return 2 * (x ^ fx) + (y ^ fy)

        def same_core_of(rel):
            fx, fy = _CHIP_RELS[rel]
            return (x ^ fx, y ^ fy, c)

        def ici_copy(rel, i, src_chip, to):
            arr, r0, n, _ = _PIECES[i]
            k = (rel - 1) * _NP + i
            return _remote(stage[arr].at[pl.ds(r0, n), :], _gathered_at(outs, arr, src_chip, r0, n),
                           ici_send.at[k], ici_recv.at[k], to)

        def fwd_copy(rel, i, to):
            arr, r0, n, _ = _PIECES[i]
            k = (rel - 1) * _NP + i
            place = _gathered_at(outs, arr, chip_of(rel), r0, n)
            return _remote(place, place, fwd_send.at[k], fwd_recv.at[k], to)

        _peer_wait()
        for core in (0, 1):
            @pl.when(c == core)
            def _(core=core):
                mine = [i for i in range(_NP) if _PIECES[i][3] == core and _PIECES[i][0] in arrays]
                theirs = [i for i in range(_NP) if _PIECES[i][3] != core and _PIECES[i][0] in arrays]
                sends = [ici_copy(rel, i, me, same_core_of(rel)) for rel in (1, 2, 3) for i in mine]
                for cp in sends:
                    cp.start()
                for ld, st in zip(loads, local):
                    ld.wait()
                    st.start()
                own = [pltpu.make_async_copy(stage[arr], _gathered_at(outs, arr, me, 0, _ARR_ROWS[arr]), own_sems.at[arr])
                       for arr in arrays if arr != 4]
                for cp in own:
                    cp.start()
                meta_all[me] = meta_ref[...]
                for rel in (1, 2, 3):
                    for i in mine:
                        ici_copy(rel, i, chip_of(rel), (x, y, c)).wait_recv()
                        fwd = fwd_copy(rel, i, (x, y, 1 - c))
                        fwd.start()
                        sends.append(fwd)
                for rel in (1, 2, 3):
                    for i in theirs:
                        fwd_copy(rel, i, (x, y, c)).wait_recv()
                for cp in sends:
                    cp.wait_send()
                for cp in own:
                    cp.wait()

        zeros.wait()
        for chip in range(CHIPS):
            head_buf[PAD:HEAD_ROWS, chip * 256:(chip + 1) * 256] = meta_all[chip]
        head = pltpu.make_async_copy(head_buf, h_o.at[pl.ds(0, HEAD_ROWS), :], loc_sems.at[2])
        head.start()
        head.wait()
        for cp in local:
            cp.wait()

    vm = pl.BlockSpec(memory_space=pltpu.VMEM)
    hbm = pl.BlockSpec(memory_space=pl.ANY)
    return pl.pallas_call(
        body,
        name="gather_weights",
        in_specs=[vm] * 4 + [hbm] * 2,
        out_specs=[hbm] * 5,
        out_shape=[
            jax.ShapeDtypeStruct((D_IN, D), BF16),
            jax.ShapeDtypeStruct((CHIPS, 256, Q_LORA), BF16), jax.ShapeDtypeStruct((CHIPS, KV_LORA, 256), BF16),
            jax.ShapeDtypeStruct((N, D), F32), jax.ShapeDtypeStruct((N, D), F32),
        ],
        scratch_shapes=[pltpu.VMEM((_ARR_ROWS[a], _ARR_COLS[a]), BF16) for a in (0, 2, 3)]
        + [pltpu.VMEM((CHIPS, N_META, 256), F32), pltpu.VMEM((HEAD_ROWS, D), F32), pltpu.VMEM((S, D), F32),
           pltpu.VMEM((S, D), F32)]
        + [pltpu.SemaphoreType.DMA((3 * _NP,))] * 4 + [pltpu.SemaphoreType.DMA((3,)), pltpu.SemaphoreType.DMA((4,))],
        compiler_params=_cparams(collective_id=0),
    )(winT_s, wqT_s, wkv_s, meta_s, x2, tgt2)


_SM_ROWS = (len(POOL_WINDOWS) * POOL_GROUP, VEC_ROWS)
_SM_COLS = (POOL_GROUP, D)
_SM_PIECES = ((0, 0, 256, 0), (0, 256, 256, 1), (1, 0, VEC_ROWS, 0))
_NSP = len(_SM_PIECES)


def _reduce_grads(dsl, hn, dwq, dwkv, dmeta4, dpw, dg, dgf, dgq, dgkv, dps, loss):
    arrays = (0, 2, 3, 4)
    loaded = (2, 3, 4)
    shard_order = (1, 2, 3, 0)

    def body(dsl_hbm, hn_hbm, dwq_ref, dwkv_ref, dmeta_ref, dpw_ref, dg_ref, dgf_ref, dgq_ref, dgkv_ref, dps_ref,
             loss_ref, gwin_o, gwq_o, gwkv_o, gmeta_o, gpw_o, gg_o, ggf_o, ggq_o, ggkv_o, gps_o, gloss_o,
             ow2, ow3, ow4, sb0, sb2, sb3, sb4, st0, st2, st3, st4, rc0, rc2, rc3, rc4,
             vec, sm_sb0, sm_sb1, sm_cs0, sm_cs1, sm_rc0, sm_rc1, vec_fin, slab_v, hn_v, dwin_buf, own0,
             own_sems, d2d_send, d2d_recv, ici_send, ici_recv, fin_send, fin_recv,
             swap_send, swap_recv, smi_send, smi_recv, smf_send, smf_recv, ld_sems):
        x, y, c = lax.axis_index("x"), lax.axis_index("y"), lax.axis_index("c")
        me = 2 * x + y
        _peer_signal(x, y, c)
        grads = (None, None, dwq_ref, dwkv_ref, dmeta_ref)
        outs = (gwin_o, None, gwq_o, gwkv_o, gmeta_o)
        own_buf = (None, None, ow2, ow3, ow4)
        sib_buf = (sb0, None, sb2, sb3, sb4)
        stage = (st0, None, st2, st3, st4)
        recv = (rc0, None, rc2, rc3, rc4)
        sm_mine = (dpw_ref, vec)
        sm_sib = (sm_sb0, sm_sb1)
        sm_chip = (sm_cs0, sm_cs1)
        sm_recv = (sm_rc0, sm_rc1)
        sm_out = (gpw_o, vec_fin)
        sibling = (x, y, 1 - c)

        def chip_of(rel):
            fx, fy = _CHIP_RELS[rel]
            return 2 * (x ^ fx) + (y ^ fy)

        def same_core_of(rel):
            fx, fy = _CHIP_RELS[rel]
            return (x ^ fx, y ^ fy, c)

        hn_load = pltpu.make_async_copy(hn_hbm, hn_v, ld_sems.at[CHIPS])

        def slab_load(rel):
            return pltpu.make_async_copy(dsl_hbm.at[chip_of(rel)], slab_v.at[rel], ld_sems.at[rel])

        hn_load.start()
        slab_load(shard_order[0]).start()

        def slot(bufs, i, idx):
            arr, _, n, _ = _PIECES[i]
            return bufs[arr].at[idx, pl.ds(0, n), :]

        def own_load(rel, i):
            arr, r0, n, _ = _PIECES[i]
            return pltpu.make_async_copy(_gathered_at(grads, arr, chip_of(rel), r0, n), slot(own_buf, i, rel),
                                         own_sems.at[rel * _NP + i])

        def d2d_copy(rel, i):
            arr, r0, n, _ = _PIECES[i]
            k = rel * _NP + i
            return _remote(_gathered_at(grads, arr, chip_of(rel), r0, n), slot(sib_buf, i, rel),
                           d2d_send.at[k], d2d_recv.at[k], sibling)

        def ici_copy(rel, i):
            k = (rel - 1) * _NP + i
            return _remote(slot(stage, i, rel - 1), slot(recv, i, rel - 1), ici_send.at[k], ici_recv.at[k],
                           same_core_of(rel))

        def fin_copy(i):
            arr, r0, n, _ = _PIECES[i]
            place = outs[arr].at[pl.ds(r0, n), :]
            return _remote(place, place, fin_send.at[i], fin_recv.at[i], sibling)

        def sm_ici_copy(rel, j):
            blk, r0, n, _ = _SM_PIECES[j]
            k = (rel - 1) * _NSP + j
            return _remote(sm_chip[blk].at[pl.ds(r0, n), :], sm_recv[blk].at[rel - 1, pl.ds(r0, n), :],
                           smi_send.at[k], smi_recv.at[k], same_core_of(rel))

        def sm_fin_copy(j):
            blk, r0, n, _ = _SM_PIECES[j]
            place = sm_out[blk].at[pl.ds(r0, n), :]
            return _remote(place, place, smf_send.at[j], smf_recv.at[j], sibling)

        vec[...] = jnp.zeros_like(vec)
        vec[0:1, :] = dg_ref[...]
        vec[1:2, :] = dgf_ref[...]
        vec[2:3, V_GQ:V_GQ + Q_LORA] = dgq_ref[...]
        vec[2:3, V_GKV:V_GKV + KV_LORA] = dgkv_ref[...]
        vec[2:3, V_PS:V_PS + D_POOL] = dps_ref[...]
        vec[2:3, V_LOSS:D] = loss_ref[...]
        _peer_wait()
        swaps = [_remote(sm_mine[b], sm_sib[b], swap_send.at[b], swap_recv.at[b], sibling) for b in (0, 1)]
        for cp in swaps:
            cp.start()

        for core in (0, 1):
            @pl.when(c == core)
            def _(core=core):
                mine = [i for i in range(_NP) if _PIECES[i][3] == core and _PIECES[i][0] in loaded]
                theirs = [i for i in range(_NP) if _PIECES[i][3] != core and _PIECES[i][0] in loaded]
                i0 = next(i for i in range(_NP) if _PIECES[i][0] == 0 and _PIECES[i][3] == core)
                j0 = next(i for i in range(_NP) if _PIECES[i][0] == 0 and _PIECES[i][3] != core)
                sm_mine_p = [j for j in range(_NSP) if _SM_PIECES[j][3] == core]
                sm_theirs_p = [j for j in range(_NSP) if _SM_PIECES[j][3] != core]
                sends = list(swaps)

                for rel in (1, 2, 3, 0):
                    for i in theirs:
                        cp = d2d_copy(rel, i)
                        cp.start()
                        sends.append(cp)
                    for i in mine:
                        own_load(rel, i).start()

                def piece_rows(i):
                    return pl.ds(_PIECES[i][1], _PIECES[i][2])

                def form(rel, i):
                    r0, n = _PIECES[i][1], _PIECES[i][2]
                    dwin_buf[rel, r0:r0 + n, :] = _tn(slab_v[rel, :, r0:r0 + _PIECE_MAX[0]], hn_v[...])[0:n, :]

                def d2d0(rel, i):
                    return _remote(dwin_buf.at[rel, piece_rows(i), :], slot(sib_buf, i, rel),
                                   d2d_send.at[rel * _NP + i], d2d_recv.at[rel * _NP + i], sibling)

                def settle(rel):
                    d2d0(rel, i0).wait_recv()
                    total = dwin_buf[rel, piece_rows(i0), :] + slot(sib_buf, i0, rel)[...]
                    if rel == 0:
                        own0[0:_PIECES[i0][2], :] = total
                    else:
                        slot(stage, i0, rel - 1)[...] = total.astype(BF16)
                        cp = ici_copy(rel, i0)
                        cp.start()
                        sends.append(cp)

                hn_load.wait()
                for n, rel in enumerate(shard_order):
                    slab_load(rel).wait()
                    if n == 0:
                        for later in shard_order[1:]:
                            slab_load(later).start()
                    form(rel, j0)
                    cp = d2d0(rel, j0)
                    cp.start()
                    sends.append(cp)
                    if n > 0:
                        settle(shard_order[n - 1])
                    form(rel, i0)
                settle(shard_order[-1])

                for rel in (1, 2, 3):
                    for i in mine:
                        arr, r0, n, _ = _PIECES[i]
                        own_load(rel, i).wait()
                        d2d_copy(rel, i).wait_recv()
                        total = slot(own_buf, i, rel)[...] + slot(sib_buf, i, rel)[...]
                        slot(stage, i, rel - 1)[...] = total.astype(stage[arr].dtype)
                        cp = ici_copy(rel, i)
                        cp.start()
                        sends.append(cp)

                for b in (0, 1):
                    swaps[b].wait_recv()
                    sm_chip[b][...] = sm_mine[b][...] + sm_sib[b][...]
                for rel in (1, 2, 3):
                    for j in sm_mine_p:
                        cp = sm_ici_copy(rel, j)
                        cp.start()
                        sends.append(cp)

                for i in mine:
                    arr, r0, n, _ = _PIECES[i]
                    own_load(0, i).wait()
                    d2d_copy(0, i).wait_recv()
                    total = slot(own_buf, i, 0)[...] + slot(sib_buf, i, 0)[...]
                    for rel in (1, 2, 3):
                        ici_copy(rel, i).wait_recv()
                        total = total + slot(recv, i, rel - 1)[...].astype(F32)
                    outs[arr][pl.ds(r0, n), :] = total
                    cp = fin_copy(i)
                    cp.start()
                    sends.append(cp)
                total = own0[0:_PIECES[i0][2], :]
                for rel in (1, 2, 3):
                    ici_copy(rel, i0).wait_recv()
                    total = total + slot(recv, i0, rel - 1)[...].astype(F32)
                outs[0][pl.ds(_PIECES[i0][1], _PIECES[i0][2]), :] = total
                cp = fin_copy(i0)
                cp.start()
                sends.append(cp)

                for j in sm_mine_p:
                    blk, r0, n, _ = _SM_PIECES[j]
                    for rel in (1, 2, 3):
                        sm_ici_copy(rel, j).wait_recv()
                    total = jnp.zeros((n, _SM_COLS[blk]), F32)
                    for chip in range(CHIPS):
                        flips = chip ^ me
                        rel = jnp.where(flips == 2, 1, jnp.where(flips == 1, 2, flips))
                        theirs_rows = sm_recv[blk][jnp.maximum(rel - 1, 0), pl.ds(r0, n), :]
                        total = total + jnp.where(rel == 0, sm_chip[blk][pl.ds(r0, n), :], theirs_rows)
                    sm_out[blk][pl.ds(r0, n), :] = total
                    cp = sm_fin_copy(j)
                    cp.start()
                    sends.append(cp)

                for i in theirs + [j0]:
                    fin_copy(i).wait_recv()
                for j in sm_theirs_p:
                    sm_fin_copy(j).wait_recv()
                for cp in sends:
                    cp.wait_send()

        gg_o[...] = vec_fin[0:1, :]
        ggf_o[...] = vec_fin[1:2, :]
        ggq_o[...] = vec_fin[2:3, V_GQ:V_GQ + Q_LORA]
        ggkv_o[...] = vec_fin[2:3, V_GKV:V_GKV + KV_LORA]
        gps_o[...] = vec_fin[2:3, V_PS:V_PS + D_POOL]
        gloss_o[...] = vec_fin[2:3, V_LOSS:D]

    vm = pl.BlockSpec(memory_space=pltpu.VMEM)
    piece_buf = lambda lead, dtype, which=arrays: [
        pltpu.VMEM((lead, _PIECE_MAX[a], _ARR_COLS[a]), F32 if a == 4 else dtype) for a in which]
    sm_buf = lambda *lead: [pltpu.VMEM(lead + (_SM_ROWS[b], _SM_COLS[b]), F32) for b in (0, 1)]
    dma = lambda n: [pltpu.SemaphoreType.DMA((n,))] * 2
    return pl.pallas_call(
        body,
        name="reduce_grads",
        in_specs=[pl.BlockSpec(memory_space=pl.ANY)] * 4 + [vm] * 8,
        out_specs=[vm] * 11,
        out_shape=[jax.ShapeDtypeStruct((_ARR_ROWS[a], _ARR_COLS[a]), F32) for a in arrays]
        + [jax.ShapeDtypeStruct((_SM_ROWS[0], _SM_COLS[0]), F32), jax.ShapeDtypeStruct((1, D), F32),
           jax.ShapeDtypeStruct((1, D), F32), jax.ShapeDtypeStruct((1, Q_LORA), F32),
           jax.ShapeDtypeStruct((1, KV_LORA), F32), jax.ShapeDtypeStruct((1, D_POOL), F32),
           jax.ShapeDtypeStruct((1, 128), F32)],
        scratch_shapes=piece_buf(CHIPS, F32, loaded) + piece_buf(CHIPS, F32) + piece_buf(3, BF16) + piece_buf(3, BF16)
        + [pltpu.VMEM((VEC_ROWS, D), F32)] + sm_buf() + sm_buf() + sm_buf(3) + [pltpu.VMEM((VEC_ROWS, D), F32)]
        + [pltpu.VMEM((CHIPS, N, SHARD_PAD), BF16), pltpu.VMEM((N, D), BF16),
           pltpu.VMEM((CHIPS, SHARD_PAD, D), F32), pltpu.VMEM((_PIECE_MAX[0], D), F32)]
        + [pltpu.SemaphoreType.DMA((CHIPS * _NP,))]
        + dma(CHIPS * _NP) + dma(3 * _NP) + dma(_NP) + dma(2) + dma(3 * _NSP) + dma(_NSP)
        + [pltpu.SemaphoreType.DMA((CHIPS + 1,))],
        compiler_params=_cparams(collective_id=3),
    )(dsl, hn, dwq, dwkv, dmeta4, dpw, dg, dgf, dgq, dgkv, dps, loss)


def _adamw_math(w, g, m, v):
    m = B1 * m + (1.0 - B1) * g
    v = B2 * v + (1.0 - B2) * (g * g)
    m_hat = m / C1
    v_hat = v / C2
    delta = -LR * (m_hat / (jnp.sqrt(v_hat) + ADAM_EPS) + WD * w)
    return delta, m, v


def _adamw_rows(name, w, g, m, v, block_rows):
    rows, cols = w.shape

    def body(w_ref, g_ref, m_ref, v_ref, go_ref, d_ref, nm_ref, nv_ref):
        g = g_ref[...]
        go_ref[...] = g
        d_ref[...], nm_ref[...], nv_ref[...] = _adamw_math(w_ref[...], g, m_ref[...], v_ref[...])

    spec = pl.BlockSpec((block_rows, cols), lambda i: (i, 0))
    return pl.pallas_call(
        body,
        name=name,
        grid=(rows // block_rows,),
        in_specs=[spec] * 4,
        out_specs=[spec] * 4,
        out_shape=[jax.ShapeDtypeStruct(w.shape, F32)] * 4,
        compiler_params=_cparams(dimension_semantics=("arbitrary",)),
    )(w, g, m, v)


def _adamw_small(groups):
    n = len(groups)

    def body(*refs):
        ins, outs = refs[:4 * n], refs[4 * n:]
        for t in range(n):
            w_ref, g_ref, m_ref, v_ref = ins[4 * t:4 * t + 4]
            g = g_ref[0:w_ref.shape[0], :]
            outs[4 * t][...] = g
            outs[4 * t + 1][...], outs[4 * t + 2][...], outs[4 * t + 3][...] = _adamw_math(
                w_ref[...], g, m_ref[...], v_ref[...])

    vm = pl.BlockSpec(memory_space=pltpu.VMEM)
    flat = [a for grp in groups for a in grp]
    outs = pl.pallas_call(
        body,
        name="adamw_small",
        in_specs=[vm] * (4 * n),
        out_specs=[vm] * (4 * n),
        out_shape=[jax.ShapeDtypeStruct(grp[0].shape, F32) for grp in groups for _ in range(4)],
        compiler_params=_cparams(),
    )(*flat)
    return [tuple(outs[4 * t:4 * t + 4]) for t in range(n)]


def _rope_tables():
    half = QK_ROPE // 2
    f32 = np.float32
    inv_freq = (f32(1.0) / (f32(ROPE_THETA) ** (np.arange(half, dtype=f32) / f32(half)))).astype(f32)
    pos = np.arange(N, dtype=f32) - f32(PAD)
    ang = (pos[:, None] * inv_freq[None, :]).astype(f32)
    cos, sin = np.cos(ang).astype(f32), np.sin(ang).astype(f32)
    zero = np.zeros((N, 128 - QK_ROPE), f32)
    return jnp.asarray(np.concatenate([cos, cos, zero], axis=1)), jnp.asarray(np.concatenate([-sin, sin, zero], axis=1))


def kernel(x, meta_tokens, norm_g, w_in, q_norm_g, w_q_b, kv_norm_g, w_kv_b, pool_w, pool_scale, w_out, final_norm_g, loss_target, m_meta_tokens, m_norm_g, m_w_in, m_q_norm_g, m_w_q_b, m_kv_norm_g, m_w_kv_b, m_pool_w, m_pool_scale, m_w_out, m_final_norm_g, v_meta_tokens, v_norm_g, v_w_in, v_q_norm_g, v_w_q_b, v_kv_norm_g, v_w_kv_b, v_pool_w, v_pool_scale, v_w_out, v_final_norm_g):
    tr = lambda a: a[0].T
    win, wq, wkv, h, tgt = _gather_weights(tr(w_in), tr(w_q_b), w_kv_b[0], meta_tokens, x[0], loss_target[0])
    cosf, sinf = _rope_tables()
    gf = final_norm_g.reshape(1, D)

    part = _local_step(h, tgt, norm_g, win, q_norm_g, wq, kv_norm_g, wkv, pool_w[0], pool_scale, w_out[0], m_w_out[0],
                       v_w_out[0], gf, cosf, sinf)

    pw2 = lambda a: a.reshape(len(POOL_WINDOWS) * POOL_GROUP, POOL_GROUP)
    gwinT, gwqT, gwkv, gmeta, gpw, gg, ggf, ggq, ggkv, gps, gloss = _reduce_grads(
        part["dsl"], part["hn"], part["dwq"], part["dwkv"], part["dmeta"], pw2(part["dpw"]), part["dg"],
        part["dgf"], part["dgq"], part["dgkv"], part["dps"], part["loss"])

    r_in = _adamw_rows("adamw_w_in", tr(w_in), gwinT, tr(m_w_in), tr(v_w_in), 248)
    r_out = part["r_out"]
    fn2 = lambda a: a.reshape(1, D)
    r_meta, r_norm, r_gq, r_wq, r_gkv, r_wkv, r_pw, r_ps, r_fn = _adamw_small([
        (meta_tokens, gmeta, m_meta_tokens, v_meta_tokens),
        (norm_g, gg, m_norm_g, v_norm_g),
        (q_norm_g, ggq, m_q_norm_g, v_q_norm_g),
        (tr(w_q_b), gwqT, tr(m_w_q_b), tr(v_w_q_b)),
        (kv_norm_g, ggkv, m_kv_norm_g, v_kv_norm_g),
        (w_kv_b[0], gwkv, m_w_kv_b[0], v_w_kv_b[0]),
        (pw2(pool_w), gpw, pw2(m_pool_w), pw2(v_pool_w)),
        (pool_scale, gps, m_pool_scale, v_pool_scale),
        (fn2(final_norm_g), ggf, fn2(m_final_norm_g), fn2(v_final_norm_g)),
    ])
    untr = lambda a: a.T[None]
    pw4 = lambda a: a.reshape(1, len(POOL_WINDOWS), POOL_GROUP, POOL_GROUP)
    per_kind = [[
        r_meta[kind], r_norm[kind], untr(r_in[kind]), r_gq[kind], untr(r_wq[kind]), r_gkv[kind], r_wkv[kind][None],
        pw4(r_pw[kind]), r_ps[kind], r_out[kind][None], r_fn[kind].reshape(D),
    ] for kind in range(4)]
    return (gloss[0, 0], part["gx"][None], *per_kind[0], *per_kind[1], *per_kind[2], *per_kind[3])
```

```python
import jax
import jax.numpy as jnp
import numpy as np
from jax import lax
from jax.experimental import pallas as pl
from jax.experimental.pallas import tpu as pltpu

F32 = jnp.float32
BF16 = jnp.bfloat16

D = 1024
S = 2048
N_META = 16
PAD = 112
HEAD_ROWS = PAD + N_META
N = HEAD_ROWS + S
D_POOL = 512
POOL_WINDOWS = (2, 4, 8, 16)
POOL_GROUP = 128
HALO = 16
HEADS = 4
QK_NOPE = 128
QK_ROPE = 64
QK = QK_NOPE + QK_ROPE
V_HEAD = 128
Q_LORA = 256
KV_LORA = 128
D_IN = 1984
EPS = 1e-6
ROPE_THETA = 10000.0
SCALE = QK ** -0.5
CHIPS = 4

ROWS_FWD = 544
ROWS_MID = 544
ROWS_BWD = 544
TK = 128
TQ = 256
NQ = S // TQ
HEADS_PER_STEP_BWD = 2

O_PI, O_PG, O_CQ, O_CKV, O_KR, O_AG = 0, 512, 1024, 1280, 1408, 1472
O_KR_END = O_KR + 128
SHARD_IN = D_IN // CHIPS
SHARD_PAD = 512
SHARD_OUT = D // CHIPS

LR, B1, B2, ADAM_EPS, WD, STEP = 0.001, 0.9, 0.999, 1e-08, 0.01, 10
C1 = 1.0 - B1**STEP
C2 = 1.0 - B2**STEP

VMEM_LIMIT = 60 * 1024 * 1024
MESH = pl.DeviceIdType.MESH
NEG = -1e30

VEC_ROWS = 8
V_GQ, V_GKV, V_PS, V_LOSS = 0, 256, 384, 896


def _cparams(**kw):
    return pltpu.CompilerParams(vmem_limit_bytes=VMEM_LIMIT, **kw)


def _nt(a, b):
    return lax.dot_general(a, b, (((1,), (1,)), ((), ())), preferred_element_type=F32)


def _tn(a, b):
    return lax.dot_general(a, b, (((0,), (0,)), ((), ())), preferred_element_type=F32)


def _nn(a, b):
    return jnp.dot(a, b, preferred_element_type=F32)


def _swap64(t):
    return pltpu.roll(t, 32, 1) + pltpu.roll(t, 96, 1)


def _sigmoid(x):
    return 1.0 / (1.0 + jnp.exp(-x))


def _low_lanes():
    return (lax.broadcasted_iota(jnp.int32, (1, 128), 1) < QK_ROPE).astype(F32)


def _rows(w, rows):
    return pl.BlockSpec((rows, w), lambda i: (i, 0))


def _const(*shape):
    return pl.BlockSpec(shape, lambda *_: (0,) * len(shape), pipeline_mode=pl.Buffered(1))


STAT_GROUPS = HEADS // HEADS_PER_STEP_BWD


def _stat_slot(head):
    return head // HEADS_PER_STEP_BWD, head % HEADS_PER_STEP_BWD


N_PEERS = 4


def _peer_signal(x, y, c):
    barrier = pltpu.get_barrier_semaphore()
    peers = [(x, y, 1 - c)] + [(x ^ fx, y ^ fy, c) for fx, fy in _CHIP_RELS[1:]]
    assert len(peers) == N_PEERS
    for peer in peers:
        pl.semaphore_signal(barrier, inc=1, device_id=peer, device_id_type=MESH)


def _peer_wait():
    pl.semaphore_wait(pltpu.get_barrier_semaphore(), N_PEERS)


def _attn_tiles():
    return [(0, TK, TK)] + [(TK + TQ * t, TQ, TK + TQ * (t + 1)) for t in range(NQ)]


def _masked_scores(q, k, rows, klen):
    s = _nt(q, k)
    col = lax.broadcasted_iota(jnp.int32, (1, TK), 1)
    head_bias = jnp.where(col >= PAD, 0.0, NEG)
    if klen == TK:
        return s + head_bias
    r = lax.broadcasted_iota(jnp.int32, (rows, 1), 0) >> 6
    c = lax.broadcasted_iota(jnp.int32, (1, rows), 1) >> 6
    diag_bias = jnp.where(c <= r, 0.0, NEG)
    parts = [s[:, 0:TK] + head_bias]
    if klen - rows > TK:
        parts.append(s[:, TK:klen - rows])
    parts.append(s[:, klen - rows:klen] + diag_bias)
    return jnp.concatenate(parts, axis=1)


def _fwd_in(h, norm_g, win, gq, wq, gkv, wkv, cosf, sinf):
    tr = ROWS_FWD

    def body(h_ref, g_ref, win_ref, gq_ref, wq_ref, gkv_ref, wkv_ref, cos_ref, sin_ref,
             pi_ref, pg_ref, cq_ref, ckv_ref, ag_ref, q_ref, k_ref, v_ref, hn_ref):
        h = h_ref[...]
        r = lax.rsqrt(jnp.mean(h * h, axis=-1, keepdims=True) + EPS)
        hn = ((h * r) * g_ref[...]).astype(BF16)
        hn_ref[...] = hn
        u = _nt(hn, win_ref[0:O_KR_END, :])
        pi_ref[...] = u[:, O_PI:O_PG]
        pg_ref[...] = u[:, O_PG:O_CQ]
        cq = u[:, O_CQ:O_CKV]
        ckv = u[:, O_CKV:O_KR]
        cq_ref[...] = cq
        ckv_ref[...] = ckv
        ag_ref[...] = _nt(hn, win_ref[O_AG:D_IN, :])
        cosv = cos_ref[...]
        sinv = sin_ref[...]
        kr = u[:, O_KR:O_KR_END] * _low_lanes()
        kr = (kr * cosv + _swap64(kr) * sinv).astype(BF16)
        rq = lax.rsqrt(jnp.mean(cq * cq, axis=-1, keepdims=True) + EPS)
        cqn = ((cq * rq) * gq_ref[...]).astype(BF16)
        rkv = lax.rsqrt(jnp.mean(ckv * ckv, axis=-1, keepdims=True) + EPS)
        ckvn = ((ckv * rkv) * gkv_ref[...]).astype(BF16)
        for hd in range(HEADS):
            qh = _nt(cqn, wq_ref[hd]) * SCALE
            z = qh[:, QK_NOPE:]
            q_ref[hd, :, 0:QK_NOPE] = qh[:, 0:QK_NOPE].astype(BF16)
            q_ref[hd, :, QK_NOPE:] = (z * cosv + _swap64(z) * sinv).astype(BF16)
            kvh = _nn(ckvn, wkv_ref[hd])
            k_ref[hd, :, 0:QK_NOPE] = kvh[:, 0:QK_NOPE].astype(BF16)
            k_ref[hd, :, QK_NOPE:] = kr
            v_ref[hd] = kvh[:, QK_NOPE:].astype(BF16)

    head = lambda w: pl.BlockSpec((HEADS, tr, w), lambda i: (0, i, 0))
    return pl.pallas_call(
        body,
        name="fwd_in",
        grid=(N // tr,),
        in_specs=[
            _rows(D, tr), _const(1, D), _const(D_IN, D), _const(1, Q_LORA), _const(HEADS, 256, Q_LORA),
            _const(1, KV_LORA), _const(HEADS, KV_LORA, 256), _rows(128, tr), _rows(128, tr),
        ],
        out_specs=[_rows(D_POOL, tr), _rows(D_POOL, tr), _rows(Q_LORA, tr), _rows(KV_LORA, tr), _rows(D_POOL, tr),
                   head(256), head(256), head(V_HEAD), _rows(D, tr)],
        out_shape=[
            jax.ShapeDtypeStruct((N, D_POOL), F32), jax.ShapeDtypeStruct((N, D_POOL), F32),
            jax.ShapeDtypeStruct((N, Q_LORA), F32), jax.ShapeDtypeStruct((N, KV_LORA), F32),
            jax.ShapeDtypeStruct((N, D_POOL), F32),
            jax.ShapeDtypeStruct((HEADS, N, 256), BF16), jax.ShapeDtypeStruct((HEADS, N, 256), BF16),
            jax.ShapeDtypeStruct((HEADS, N, V_HEAD), BF16), jax.ShapeDtypeStruct((N, D), BF16),
        ],
        compiler_params=_cparams(dimension_semantics=("arbitrary",)),
    )(h, norm_g, win, gq, wq, gkv, wkv, cosf, sinf)


def _attn_fwd(q, k, v, wout_s):
    tiles = _attn_tiles()
    n_t = len(tiles)
    half = SHARD_OUT // 2
    send_step = 2
    fwd_step = n_t - 2

    def body(q_hbm, k_hbm, v_hbm, wout_ref, o_hbm, lse_ref, wout_o, q_buf, k_buf, v_buf, o_buf, s_wout, in_sems, out_sems,
             ici_send, ici_recv, fwd_send, fwd_recv, own_sem):
        step = pl.program_id(0)
        x, y, c = lax.axis_index("x"), lax.axis_index("y"), lax.axis_index("c")
        me = 2 * x + y

        def chip_of(rel):
            fx, fy = _CHIP_RELS[rel]
            return 2 * (x ^ fx) + (y ^ fy)

        def place(chip, core):
            return wout_o.at[pl.ds(pl.multiple_of(SHARD_OUT * chip + half * core, half), half), :]

        def ici_copy(rel, src_chip, to):
            return _remote(s_wout.at[pl.ds(pl.multiple_of(half * c, half), half), :], place(src_chip, c),
                           ici_send.at[rel - 1], ici_recv.at[rel - 1], to)

        def fwd_copy(rel, core, to):
            spot = place(chip_of(rel), core)
            return _remote(spot, spot, fwd_send.at[rel - 1], fwd_recv.at[rel - 1], to)

        own = pltpu.make_async_copy(s_wout, wout_o.at[pl.ds(pl.multiple_of(SHARD_OUT * me, SHARD_OUT), SHARD_OUT), :], own_sem)

        @pl.when(step == 0)
        def _():
            _peer_signal(x, y, c)
            s_wout[...] = wout_ref[...].astype(BF16)
            own.start()

        @pl.when(step == send_step)
        def _():
            _peer_wait()
            for rel in (1, 2, 3):
                fx, fy = _CHIP_RELS[rel]
                ici_copy(rel, me, (x ^ fx, y ^ fy, c)).start()

        @pl.when(step == fwd_step)
        def _():
            for rel in (1, 2, 3):
                ici_copy(rel, chip_of(rel), (x, y, c)).wait_recv()
                fwd_copy(rel, c, (x, y, 1 - c)).start()

        def finish_wout():
            for rel in (1, 2, 3):
                fwd_copy(rel, 1 - c, (x, y, c)).wait_recv()
            for rel in (1, 2, 3):
                ici_copy(rel, me, (x, y, c)).wait_send()
                fwd_copy(rel, c, (x, y, c)).wait_send()
            own.wait()

        def loads(idx):
            q0, rows, _ = tiles[idx]
            rs = pl.ds(q0, rows)
            return [pltpu.make_async_copy(src.at[:, rs, :], dst.at[:, rs, :], in_sems.at[a, idx % 2])
                    for a, (src, dst) in enumerate(((q_hbm, q_buf), (k_hbm, k_buf), (v_hbm, v_buf)))]

        def store(idx):
            q0, rows, _ = tiles[idx]
            return pltpu.make_async_copy(o_buf.at[idx % 2, pl.ds(0, rows), :], o_hbm.at[pl.ds(q0, rows), :],
                                         out_sems.at[idx % 2])

        @pl.when(step == 0)
        def _():
            lse_ref[...] = jnp.zeros_like(lse_ref)
            for cp in loads(0):
                cp.start()

        for idx, (q0, rows, klen) in enumerate(tiles):
            @pl.when(step == idx)
            def _(idx=idx, q0=q0, rows=rows, klen=klen):
                for cp in loads(idx):
                    cp.wait()
                if idx + 1 < n_t:
                    for cp in loads(idx + 1):
                        cp.start()
                if idx >= 2:
                    store(idx - 2).wait()
                for hd in range(HEADS):
                    s = _masked_scores(q_buf[hd, q0:q0 + rows, :], k_buf[hd, 0:klen, :], rows, klen)
                    m = jnp.max(s, axis=-1, keepdims=True)
                    p = jnp.exp(s - m)
                    l = jnp.sum(p, axis=-1, keepdims=True)
                    o_buf[idx % 2, 0:rows, hd * V_HEAD:(hd + 1) * V_HEAD] = _nn(p.astype(BF16), v_buf[hd, 0:klen, :]) / l
                    grp, lane = _stat_slot(hd)
                    lse_ref[grp, q0:q0 + rows, lane:lane + 1] = m + jnp.log(l)
                store(idx).start()
                if idx == n_t - 1:
                    store(idx - 1).wait()
                    store(idx).wait()
                    finish_wout()

    hbm = pl.BlockSpec(memory_space=pl.ANY)
    return pl.pallas_call(
        body,
        name="attn_fwd",
        grid=(n_t,),
        in_specs=[hbm, hbm, hbm, _const(SHARD_OUT, D)],
        out_specs=[hbm, _const(STAT_GROUPS, N, 128), hbm],
        out_shape=[jax.ShapeDtypeStruct((N, HEADS * V_HEAD), F32), jax.ShapeDtypeStruct((STAT_GROUPS, N, 128), F32),
                   jax.ShapeDtypeStruct((D, D), BF16)],
        scratch_shapes=[pltpu.VMEM((HEADS, N, 256), BF16), pltpu.VMEM((HEADS, N, 256), BF16),
                        pltpu.VMEM((HEADS, N, V_HEAD), BF16), pltpu.VMEM((2, TQ, HEADS * V_HEAD), F32),
                        pltpu.VMEM((SHARD_OUT, D), BF16),
                        pltpu.SemaphoreType.DMA((3, 2)), pltpu.SemaphoreType.DMA((2,))]
        + [pltpu.SemaphoreType.DMA((3,))] * 4 + [pltpu.SemaphoreType.DMA],
        compiler_params=_cparams(dimension_semantics=("arbitrary",), collective_id=1),
    )(q, k, v, wout_s)


def _inv_count(row0, rows, w):
    row = row0 + lax.broadcasted_iota(jnp.int32, (rows, 1), 0)
    return 1.0 / jnp.clip(row - (PAD - 1), 1, w).astype(F32)


def _mid(h, tgt, pool_in, pool_gate, attn_gate, attn, pool_w, pool_scale, wout, gf):
    tr = ROWS_MID
    per = tr // HALO
    ng = len(POOL_WINDOWS)

    def body(h_ref, t_ref, pin_ref, halo_ref, pg_ref, ag_ref, at_ref, pw_ref, ps_ref, wout_ref, gf_ref,
             dh2_ref, do_ref, delta_ref, dag_ref, dpg_ref, dpl_ref, dwout_ref, dpw_ref, dps_ref, dgf_ref, loss_ref):
        i = pl.program_id(0)

        @pl.when(i == 0)
        def _():
            dwout_ref[...] = jnp.zeros_like(dwout_ref)
            dpw_ref[...] = jnp.zeros_like(dpw_ref)
            dps_ref[...] = jnp.zeros_like(dps_ref)
            dgf_ref[...] = jnp.zeros_like(dgf_ref)
            loss_ref[...] = jnp.zeros_like(loss_ref)

        row0 = i * tr
        real = (row0 + lax.broadcasted_iota(jnp.int32, (tr, 1), 0)) >= HEAD_ROWS
        h = h_ref[...]

        halo = jnp.where(i > 0, halo_ref[...], 0.0)
        ext = jnp.concatenate([halo, pin_ref[...]], axis=0)
        pooled = []
        for g, w in enumerate(POOL_WINDOWS):
            e = ext[:, g * POOL_GROUP:(g + 1) * POOL_GROUP]
            acc = e
            shift = 1
            while shift < w:
                acc = acc + pltpu.roll(acc, shift, 0)
                shift *= 2
            pooled.append((acc[HALO:] * _inv_count(row0, tr, w) - e[HALO:]).astype(BF16))
        pw = [pw_ref[g].astype(BF16) for g in range(ng)]
        mixed = jnp.concatenate([_nn(pooled[g], pw[g]) for g in range(ng)], axis=1)
        ps = ps_ref[...]
        mixed_s = mixed * ps
        pg = pg_ref[...]
        sig_p = _sigmoid(pg)
        silu_p = pg * sig_p
        pool_out = (silu_p * mixed_s).astype(BF16)
        ag = ag_ref[...]
        sig_a = _sigmoid(ag)
        silu_a = ag * sig_a
        at = at_ref[...]
        attn_out = (silu_a * at).astype(BF16)
        cat = jnp.concatenate([pool_out, attn_out], axis=1)
        h2 = h + _nn(cat, wout_ref[...])

        r2 = lax.rsqrt(jnp.mean(h2 * h2, axis=-1, keepdims=True) + EPS)
        n2 = h2 * r2
        gfv = gf_ref[...]
        err = jnp.where(real, n2 * gfv - t_ref[...], 0.0)
        loss_ref[...] += jnp.sum(jnp.sum(err * err, axis=-1, keepdims=True), axis=0, keepdims=True) * (0.5 / D)
        dy = err * (1.0 / D)
        dgf_ref[...] += jnp.sum(dy * n2, axis=0, keepdims=True)
        dn = dy * gfv
        dh2 = r2 * (dn - n2 * jnp.mean(dn * n2, axis=-1, keepdims=True))
        dh2_ref[...] = dh2
        dh2b = dh2.astype(BF16)

        dwout_ref[...] += _tn(cat, dh2b)
        dcat = _nt(dh2b, wout_ref[...])
        dpo = dcat[:, 0:D_POOL]
        dao = dcat[:, D_POOL:D]
        do = dao * silu_a
        prod = do * at
        delta_ref[...] = jnp.zeros_like(delta_ref)
        for hd in range(HEADS):
            grp, lane = _stat_slot(hd)
            cols = slice(hd * V_HEAD, (hd + 1) * V_HEAD)
            do_ref[grp, :, lane * V_HEAD:(lane + 1) * V_HEAD] = do[:, cols].astype(BF16)
            delta_ref[grp, :, lane:lane + 1] = jnp.sum(prod[:, cols], axis=-1, keepdims=True)
        dag_ref[...] = (dao * at * (sig_a * (1.0 + ag * (1.0 - sig_a)))).astype(BF16)
        dmixed_s = dpo * silu_p
        dpg_ref[...] = (dpo * mixed_s * (sig_p * (1.0 + pg * (1.0 - sig_p)))).astype(BF16)
        dps_ref[...] += jnp.sum(dmixed_s * mixed, axis=0, keepdims=True)
        dmixed = (dmixed_s * ps).astype(BF16)
        dpl = []
        for g in range(ng):
            dm = dmixed[:, g * POOL_GROUP:(g + 1) * POOL_GROUP]
            dpl.append(_nt(dm, pw[g]))
            dpw_ref[g] += _tn(pooled[g], dm)
        dpl_ref[...] = jnp.concatenate(dpl, axis=1)

    halo_spec = pl.BlockSpec((HALO, D_POOL), lambda i: (jnp.maximum(i * per - 1, 0), 0))
    return pl.pallas_call(
        body,
        name="mid",
        grid=(N // tr,),
        in_specs=[
            _rows(D, tr), _rows(D, tr), _rows(D_POOL, tr), halo_spec, _rows(D_POOL, tr), _rows(D_POOL, tr),
            _rows(D_POOL, tr), _const(ng, POOL_GROUP, POOL_GROUP), _const(1, D_POOL), _const(D, D), _const(1, D),
        ],
        out_specs=[
            _rows(D, tr), pl.BlockSpec((STAT_GROUPS, tr, HEADS_PER_STEP_BWD * V_HEAD), lambda i: (0, i, 0)),
            pl.BlockSpec((STAT_GROUPS, tr, 128), lambda i: (0, i, 0)),
            _rows(D_POOL, tr), _rows(D_POOL, tr), _rows(D_POOL, tr),
            _const(D, D), _const(ng, POOL_GROUP, POOL_GROUP), _const(1, D_POOL), _const(1, D), _const(1, 128),
        ],
        out_shape=[
            jax.ShapeDtypeStruct((N, D), F32), jax.ShapeDtypeStruct((STAT_GROUPS, N, HEADS_PER_STEP_BWD * V_HEAD), BF16),
            jax.ShapeDtypeStruct((STAT_GROUPS, N, 128), F32),
            jax.ShapeDtypeStruct((N, D_POOL), BF16), jax.ShapeDtypeStruct((N, D_POOL), BF16),
            jax.ShapeDtypeStruct((N, D_POOL), F32), jax.ShapeDtypeStruct((D, D), F32),
            jax.ShapeDtypeStruct((ng, POOL_GROUP, POOL_GROUP), F32),
            jax.ShapeDtypeStruct((1, D_POOL), F32), jax.ShapeDtypeStruct((1, D), F32), jax.ShapeDtypeStruct((1, 128), F32),
        ],
        compiler_params=_cparams(dimension_semantics=("arbitrary",)),
    )(h, tgt, pool_in, pool_in, pool_gate, attn_gate, attn, pool_w, pool_scale, wout, gf)


def _unrope(dy, cosv, sinv):
    return dy * cosv + _swap64(dy * sinv) * _low_lanes()


def _attn_bwd(q, k, v, do, lse, delta, cosf, sinf, dwout):
    tiles = _attn_tiles()
    hp = HEADS_PER_STEP_BWD
    n_g = HEADS // hp
    n_t = len(tiles)
    half = SHARD_OUT // 2
    swap_at, send_at, sum_at = (0, 3), (0, 5), (n_g - 1, n_t // 2)

    def body(q_hbm, k_hbm, v_hbm, do_hbm, lse_ref, delta_ref, cos_ref, sin_ref, dwout_hbm, dq_hbm, dkv_ref, dkr_ref,
             gwout_ref, q_buf, k_buf, v_buf, do_buf, dq_buf, dk_acc, dv_acc, own_w, sib_w, stage_w, recv_w, gw_buf,
             in_sems, out_sems, ow_sems, d2d_send, d2d_recv, ici_send, ici_recv, fin_send, fin_recv):
        grp = pl.program_id(0)
        step = pl.program_id(1)
        heads = pl.ds(grp * hp, hp)
        x, y, c = lax.axis_index("x"), lax.axis_index("y"), lax.axis_index("c")
        sibling = (x, y, 1 - c)

        def chip_of(rel):
            fx, fy = _CHIP_RELS[rel]
            return 2 * (x ^ fx) + (y ^ fy)

        def piece(chip, core):
            return dwout_hbm.at[pl.ds(pl.multiple_of(SHARD_OUT * chip + half * core, half), half), :]

        def own_load(rel):
            return pltpu.make_async_copy(piece(chip_of(rel), c), own_w.at[rel], ow_sems.at[rel])

        def d2d_copy(rel):
            return _remote(piece(chip_of(rel), 1 - c), sib_w.at[rel], d2d_send.at[rel], d2d_recv.at[rel], sibling)

        def ici_copy(rel):
            fx, fy = _CHIP_RELS[rel]
            return _remote(stage_w.at[rel - 1], recv_w.at[rel - 1], ici_send.at[rel - 1], ici_recv.at[rel - 1],
                           (x ^ fx, y ^ fy, c))

        def fin_copy(core):
            spot = gw_buf.at[pl.ds(pl.multiple_of(half * core, half), half), :]
            return _remote(spot, spot, fin_send.at[0], fin_recv.at[0], sibling)

        @pl.when((grp == 0) & (step == 0))
        def _():
            _peer_signal(x, y, c)
            for rel in (1, 2, 3, 0):
                own_load(rel).start()

        @pl.when((grp == swap_at[0]) & (step == swap_at[1]))
        def _():
            _peer_wait()
            for rel in (1, 2, 3, 0):
                d2d_copy(rel).start()

        @pl.when((grp == send_at[0]) & (step == send_at[1]))
        def _():
            for rel in (1, 2, 3):
                own_load(rel).wait()
                d2d_copy(rel).wait_recv()
                stage_w[rel - 1] = (own_w[rel] + sib_w[rel]).astype(BF16)
                ici_copy(rel).start()

        @pl.when((grp == sum_at[0]) & (step == sum_at[1]))
        def _():
            own_load(0).wait()
            d2d_copy(0).wait_recv()
            total = own_w[0] + sib_w[0]
            for rel in (1, 2, 3):
                ici_copy(rel).wait_recv()
                total = total + recv_w[rel - 1].astype(F32)
            gw_buf[pl.ds(pl.multiple_of(half * c, half), half), :] = total
            fin_copy(c).start()

        def finish_dwout():
            fin_copy(1 - c).wait_recv()
            for rel in (0, 1, 2, 3):
                d2d_copy(rel).wait_send()
            for rel in (1, 2, 3):
                ici_copy(rel).wait_send()
            fin_copy(c).wait_send()
            gwout_ref[...] = gw_buf[...]

        def loads(g, idx):
            q0, rows, _ = tiles[idx]
            rs = pl.ds(q0, rows)
            par = (g * n_t + idx) % 2
            hs = pl.ds(g * hp, hp)
            pairs = ((q_hbm.at[hs, rs, :], q_buf.at[:, rs, :]), (k_hbm.at[hs, rs, :], k_buf.at[:, rs, :]),
                     (v_hbm.at[hs, rs, :], v_buf.at[:, rs, :]), (do_hbm.at[g, rs, :], do_buf.at[rs, :]))
            return [pltpu.make_async_copy(src, dst, in_sems.at[a, par]) for a, (src, dst) in enumerate(pairs)]

        def store(idx):
            q0, rows, _ = tiles[idx]
            return pltpu.make_async_copy(dq_buf.at[idx % 2, :, pl.ds(0, rows), :], dq_hbm.at[heads, pl.ds(q0, rows), :],
                                         out_sems.at[idx % 2])

        @pl.when(step == 0)
        def _():
            dk_acc[...] = jnp.zeros_like(dk_acc)
            dv_acc[...] = jnp.zeros_like(dv_acc)

        @pl.when((step == 0) & (grp == 0))
        def _():
            dkr_ref[...] = jnp.zeros_like(dkr_ref)
            for cp in loads(grp, 0):
                cp.start()

        for idx, (q0, rows, klen) in enumerate(tiles):
            @pl.when(step == idx)
            def _(idx=idx, q0=q0, rows=rows, klen=klen):
                for cp in loads(grp, idx):
                    cp.wait()
                if idx + 1 < n_t:
                    for cp in loads(grp, idx + 1):
                        cp.start()
                if idx >= 2:
                    store(idx - 2).wait()
                qs = pl.ds(q0, rows)
                for hd in range(hp):
                    qv = q_buf[hd, qs, :]
                    kv = k_buf[hd, 0:klen, :]
                    p = jnp.exp(_masked_scores(qv, kv, rows, klen) - lse_ref[0, qs, hd:hd + 1])
                    dob = do_buf[qs, hd * V_HEAD:(hd + 1) * V_HEAD]
                    ds = (p * (_nt(dob, v_buf[hd, 0:klen, :]) - delta_ref[0, qs, hd:hd + 1])).astype(BF16)
                    dq = _nn(ds, kv) * SCALE
                    dq_buf[idx % 2, hd, 0:rows, 0:QK_NOPE] = dq[:, 0:QK_NOPE].astype(BF16)
                    dq_buf[idx % 2, hd, 0:rows, QK_NOPE:] = _unrope(dq[:, QK_NOPE:], cos_ref[qs, :], sin_ref[qs, :]).astype(BF16)
                    dk_acc[hd, 0:klen, :] += _tn(ds, qv)
                    dv_acc[hd, 0:klen, :] += _tn(p.astype(BF16), dob)
                store(idx).start()

        @pl.when(step == n_t - 1)
        def _():
            @pl.when(grp + 1 < n_g)
            def _():
                for cp in loads(grp + 1, 0):
                    cp.start()

            for hd in range(hp):
                dkv_ref[hd, :, 0:QK_NOPE] = dk_acc[hd, :, 0:QK_NOPE].astype(BF16)
                dkv_ref[hd, :, QK_NOPE:] = dv_acc[hd].astype(BF16)
                dkr_ref[...] += dk_acc[hd, :, QK_NOPE:]
            store(n_t - 2).wait()
            store(n_t - 1).wait()

            @pl.when(grp == n_g - 1)
            def _():
                finish_dwout()

    hbm = pl.BlockSpec(memory_space=pl.ANY)
    stat = pl.BlockSpec((1, N, 128), lambda g, t: (g, 0, 0), pipeline_mode=pl.Buffered(1))
    piece_f32 = lambda lead: pltpu.VMEM((lead, half, D), F32)
    piece_bf16 = lambda lead: pltpu.VMEM((lead, half, D), BF16)
    return pl.pallas_call(
        body,
        name="attn_bwd",
        grid=(n_g, n_t),
        in_specs=[hbm, hbm, hbm, hbm, stat, stat, _const(N, 128), _const(N, 128), hbm],
        out_specs=[hbm, pl.BlockSpec((hp, N, 256), lambda g, t: (g, 0, 0), pipeline_mode=pl.Buffered(1)), _const(N, 128),
                   _const(SHARD_OUT, D)],
        out_shape=[
            jax.ShapeDtypeStruct((HEADS, N, 256), BF16), jax.ShapeDtypeStruct((HEADS, N, 256), BF16),
            jax.ShapeDtypeStruct((N, 128), F32), jax.ShapeDtypeStruct((SHARD_OUT, D), F32),
        ],
        scratch_shapes=[pltpu.VMEM((hp, N, 256), BF16), pltpu.VMEM((hp, N, 256), BF16), pltpu.VMEM((hp, N, V_HEAD), BF16),
                        pltpu.VMEM((N, hp * V_HEAD), BF16), pltpu.VMEM((2, hp, TQ, 256), BF16),
                        pltpu.VMEM((hp, N, 256), F32), pltpu.VMEM((hp, N, V_HEAD), F32),
                        piece_f32(CHIPS), piece_f32(CHIPS), piece_bf16(3), piece_bf16(3), pltpu.VMEM((SHARD_OUT, D), F32),
                        pltpu.SemaphoreType.DMA((4, 2)), pltpu.SemaphoreType.DMA((2,)), pltpu.SemaphoreType.DMA((CHIPS,)),
                        pltpu.SemaphoreType.DMA((CHIPS,)), pltpu.SemaphoreType.DMA((CHIPS,)),
                        pltpu.SemaphoreType.DMA((3,)), pltpu.SemaphoreType.DMA((3,)),
                        pltpu.SemaphoreType.DMA((1,)), pltpu.SemaphoreType.DMA((1,))],
        compiler_params=_cparams(dimension_semantics=("arbitrary", "arbitrary"), collective_id=2),
    )(q, k, v, do, lse, delta, cosf, sinf, dwout)


def _bwd_in(h, dh2, dq, dkv, dkr, cq, ckv, dpl, dpg, dag, norm_g, win, gq, wq, gkv, wkv, cosf, sinf, adam_out):
    tr = ROWS_BWD
    nb = N // tr
    per = tr // HALO
    lead = HEAD_ROWS
    adam_rows = SHARD_OUT // nb

    def body(h_ref, dh2_ref, dq_ref, dkv_ref, dkr_ref, cq_ref, ckv_ref, dpl_ref, halo_ref, dpg_ref, dag_ref,
             g_ref, win_ref, gq_ref, wq_ref, gkv_ref, wkv_ref, cos_ref, sin_ref, aw_ref, ag_ref, am_ref, av_ref,
             gx_ref, dmeta_ref, dsl_ref, dwq_ref, dwkv_ref, dg_ref, dgq_ref, dgkv_ref, ago_ref, ad_ref, anm_ref, anv_ref,
             dh_buf, gx_sem):
        i = pl.program_id(0)
        grad_out = ag_ref[...]
        ago_ref[...] = grad_out
        ad_ref[...], anm_ref[...], anv_ref[...] = _adamw_math(aw_ref[...], grad_out, am_ref[...], av_ref[...])

        @pl.when(i == 0)
        def _():
            dwq_ref[...] = jnp.zeros_like(dwq_ref)
            dwkv_ref[...] = jnp.zeros_like(dwkv_ref)
            dg_ref[...] = jnp.zeros_like(dg_ref)
            dgq_ref[...] = jnp.zeros_like(dgq_ref)
            dgkv_ref[...] = jnp.zeros_like(dgkv_ref)

        row0 = i * tr
        h = h_ref[...]
        r = lax.rsqrt(jnp.mean(h * h, axis=-1, keepdims=True) + EPS)
        n = h * r
        gv = g_ref[...]
        cq = cq_ref[...]
        rq = lax.rsqrt(jnp.mean(cq * cq, axis=-1, keepdims=True) + EPS)
        nq = cq * rq
        gqv = gq_ref[...]
        cqn = (nq * gqv).astype(BF16)
        dcqn = jnp.zeros((tr, Q_LORA), F32)
        for hd in range(HEADS):
            dqf = dq_ref[hd]
            dcqn = dcqn + _nn(dqf, wq_ref[hd])
            dwq_ref[hd] += _tn(dqf, cqn)
        dgq_ref[...] += jnp.sum(dcqn * nq, axis=0, keepdims=True)
        dnq = dcqn * gqv
        dcq = rq * (dnq - nq * jnp.mean(dnq * nq, axis=-1, keepdims=True))

        ckv = ckv_ref[...]
        rkv = lax.rsqrt(jnp.mean(ckv * ckv, axis=-1, keepdims=True) + EPS)
        nkv = ckv * rkv
        gkvv = gkv_ref[...]
        ckvn = (nkv * gkvv).astype(BF16)
        dckvn = jnp.zeros((tr, KV_LORA), F32)
        for hd in range(HEADS):
            dkv = dkv_ref[hd]
            dckvn = dckvn + _nt(dkv, wkv_ref[hd])
            dwkv_ref[hd] += _tn(ckvn, dkv)
        dgkv_ref[...] += jnp.sum(dckvn * nkv, axis=0, keepdims=True)
        dnkv = dckvn * gkvv
        dckv = rkv * (dnkv - nkv * jnp.mean(dnkv * nkv, axis=-1, keepdims=True))
        dkr = _unrope(dkr_ref[...], cos_ref[...], sin_ref[...])

        cur = dpl_ref[...]
        halo = jnp.where(i < nb - 1, halo_ref[...], 0.0)
        dpi = []
        for g, w in enumerate(POOL_WINDOWS):
            sl = slice(g * POOL_GROUP, (g + 1) * POOL_GROUP)
            a = jnp.concatenate([cur[:, sl] * _inv_count(row0, tr, w), halo[:, sl] * _inv_count(row0 + tr, HALO, w)], axis=0)
            acc = a
            shift = 1
            while shift < w:
                acc = acc + pltpu.roll(acc, tr + HALO - shift, 0)
                shift *= 2
            dpi.append(acc[0:tr] - cur[:, sl])

        du = jnp.concatenate([t.astype(BF16) for t in dpi] + [dpg_ref[...]] + [t.astype(BF16) for t in (dcq, dckv, dkr)],
                             axis=1)
        dagb = dag_ref[...]
        by_row = jnp.concatenate(dpi + [dpg_ref[...].astype(F32), dcq, dckv, dkr[:, 0:QK_ROPE], dagb.astype(F32),
                                        jnp.zeros((tr, SHARD_PAD - SHARD_IN), F32)], axis=1)
        for chip in range(CHIPS):
            dsl_ref[chip] = by_row[:, SHARD_IN * chip:SHARD_IN * chip + SHARD_PAD].astype(BF16)
        dhn = _nn(du, win_ref[0:O_KR_END, :]) + _nn(dagb, win_ref[O_AG:D_IN, :])
        dg_ref[...] += jnp.sum(dhn * n, axis=0, keepdims=True)
        dn = dhn * gv
        dh = dh2_ref[...] + r * (dn - n * jnp.mean(dn * n, axis=-1, keepdims=True))

        first = pltpu.make_async_copy(dh_buf.at[pl.ds(lead, tr - lead), :], gx_ref.at[pl.ds(0, tr - lead), :], gx_sem)
        later = lambda step: pltpu.make_async_copy(
            dh_buf, gx_ref.at[pl.ds(pl.multiple_of(step * tr - lead, 16), tr), :], gx_sem)

        @pl.when(i == 1)
        def _():
            first.wait()

        @pl.when(i > 1)
        def _():
            later(i - 1).wait()

        dh_buf[...] = dh

        @pl.when(i == 0)
        def _():
            first.start()
            for chip in range(CHIPS):
                dmeta_ref[chip] = dh[PAD:HEAD_ROWS, chip * 256:(chip + 1) * 256]

        @pl.when(i > 0)
        def _():
            later(i).start()

        @pl.when(i == nb - 1)
        def _():
            later(i).wait()

    head = lambda w: pl.BlockSpec((HEADS, tr, w), lambda i: (0, i, 0))
    halo_spec = pl.BlockSpec((HALO, D_POOL), lambda i: (jnp.minimum((i + 1) * per, N // HALO - 1), 0))
    return pl.pallas_call(
        body,
        name="bwd_in",
        grid=(nb,),
        in_specs=[
            _rows(D, tr), _rows(D, tr), head(256), head(256), _rows(128, tr), _rows(Q_LORA, tr), _rows(KV_LORA, tr),
            _rows(D_POOL, tr), halo_spec, _rows(D_POOL, tr), _rows(D_POOL, tr),
            _const(1, D), _const(D_IN, D), _const(1, Q_LORA), _const(HEADS, 256, Q_LORA),
            _const(1, KV_LORA), _const(HEADS, KV_LORA, 256), _rows(128, tr), _rows(128, tr),
        ] + [_rows(D, adam_rows)] * 4,
        out_specs=[
            pl.BlockSpec(memory_space=pl.ANY), _const(CHIPS, N_META, 256),
            pl.BlockSpec((CHIPS, tr, SHARD_PAD), lambda i: (0, i, 0)), _const(HEADS, 256, Q_LORA),
            _const(HEADS, KV_LORA, 256), _const(1, D), _const(1, Q_LORA), _const(1, KV_LORA),
        ] + [_rows(D, adam_rows)] * 4,
        out_shape=[
            jax.ShapeDtypeStruct((S, D), F32), jax.ShapeDtypeStruct((CHIPS, N_META, 256), F32),
            jax.ShapeDtypeStruct((CHIPS, N, SHARD_PAD), BF16), jax.ShapeDtypeStruct((HEADS, 256, Q_LORA), F32),
            jax.ShapeDtypeStruct((HEADS, KV_LORA, 256), F32),
            jax.ShapeDtypeStruct((1, D), F32), jax.ShapeDtypeStruct((1, Q_LORA), F32), jax.ShapeDtypeStruct((1, KV_LORA), F32),
        ] + [jax.ShapeDtypeStruct((SHARD_OUT, D), F32)] * 4,
        scratch_shapes=[pltpu.VMEM((tr, D), F32), pltpu.SemaphoreType.DMA],
        compiler_params=_cparams(dimension_semantics=("arbitrary",)),
    )(h, dh2, dq, dkv, dkr, cq, ckv, dpl, dpl, dpg, dag, norm_g, win, gq, wq, gkv, wkv, cosf, sinf, *adam_out)


def _local_step(h, tgt, norm_g, win, gq, wq, gkv, wkv, pool_w, pool_scale, wout_s, m_wout_s, v_wout_s, gf, cosf, sinf):
    pool_in, pool_gate, cq, ckv, attn_gate, q, k, v, hn = _fwd_in(h, norm_g, win, gq, wq, gkv, wkv, cosf, sinf)
    attn, lse, wout = _attn_fwd(q, k, v, wout_s)
    dh2, do, delta, dag, dpg, dpl, dwout, dpw, dps, dgf, loss = _mid(
        h, tgt, pool_in, pool_gate, attn_gate, attn, pool_w, pool_scale, wout, gf)
    dq, dkv, dkr, gwout = _attn_bwd(q, k, v, do, lse, delta, cosf, sinf, dwout)
    gx, dmeta, dsl, dwq, dwkv, dg, dgq, dgkv, *r_out = _bwd_in(
        h, dh2, dq, dkv, dkr, cq, ckv, dpl, dpg, dag, norm_g, win, gq, wq, gkv, wkv, cosf, sinf,
        (wout_s, gwout, m_wout_s, v_wout_s))
    return dict(gx=gx, dmeta=dmeta, dsl=dsl, hn=hn, dwq=dwq, dwkv=dwkv, r_out=tuple(r_out), dg=dg, dgq=dgq,
                dgkv=dgkv, dpw=dpw, dps=dps, dgf=dgf, loss=loss)


_CHIP_RELS = ((0, 0), (1, 0), (0, 1), (1, 1))

_ARR_ROWS = (SHARD_IN, SHARD_OUT, 256, KV_LORA, N_META)
_ARR_COLS = (D, D, Q_LORA, 256, 256)
_PIECES = (
    (0, 0, 256, 0), (0, 256, SHARD_IN - 256, 1),
    (1, 0, 128, 0), (1, 128, 128, 1),
    (2, 0, 128, 0), (2, 128, 128, 1),
    (3, 0, 64, 0), (3, 64, 64, 1),
    (4, 0, N_META, 0),
)
_NP = len(_PIECES)
_PIECE_MAX = (256, 128, 128, 64, N_META)


def _gathered_at(refs, arr, chip, r0, n):
    if arr in (0, 1):
        return refs[arr].at[pl.ds(pl.multiple_of(_ARR_ROWS[arr] * chip + r0, 16), n), :]
    return refs[arr].at[chip, pl.ds(r0, n), :]


def _remote(src, dst, send_sem, recv_sem, to):
    return pltpu.make_async_remote_copy(src_ref=src, dst_ref=dst, send_sem=send_sem, recv_sem=recv_sem,
                                        device_id=to, device_id_type=MESH)


def _gather_weights(winT_s, wqT_s, wkv_s, meta_s, x2, tgt2):
    arrays = (0, 2, 3, 4)

    def body(win_ref, wq_ref, wkv_ref, meta_ref, x_ref, t_ref, win_o, wq_o, wkv_o, h_o, tp_o,
             s_win, s_wq, s_wkv, meta_all, head_buf, x_buf, t_buf, ici_send, ici_recv, fwd_send, fwd_recv,
             loc_sems, own_sems):
        x, y, c = lax.axis_index("x"), lax.axis_index("y"), lax.axis_index("c")
        me = 2 * x + y
        stage = (s_win, None, s_wq, s_wkv, meta_ref)
        outs = (win_o, None, wq_o, wkv_o, meta_all)

        _peer_signal(x, y, c)

        frames = pl.ds(HEAD_ROWS, S)
        loads = [pltpu.make_async_copy(x_ref, x_buf, loc_sems.at[0]), pltpu.make_async_copy(t_ref, t_buf, loc_sems.at[1])]
        local = [pltpu.make_async_copy(x_buf, h_o.at[frames, :], loc_sems.at[0]),
                 pltpu.make_async_copy(t_buf, tp_o.at[frames, :], loc_sems.at[1])]
        for cp in loads:
            cp.start()

        s_win[...] = win_ref[...].astype(BF16)
        s_wq[0:QK, :] = wq_ref[...].astype(BF16)
        s_wq[QK:256, :] = jnp.zeros((256 - QK, Q_LORA), BF16)
        s_wkv[...] = wkv_ref[...].astype(BF16)
        head_buf[...] = jnp.zeros_like(head_buf)
        zeros = pltpu.make_async_copy(head_buf, tp_o.at[pl.ds(0, HEAD_ROWS), :], loc_sems.at[2])
        zeros.start()

        def chip_of(rel):
            fx, fy = _CHIP_RELS[rel]
            return 2 * (x ^ fx) + (y ^ fy)

        def same_core_of(rel):
            fx, fy = _CHIP_RELS[rel]
            return (x ^ fx, y ^ fy, c)

        def ici_copy(rel, i, src_chip, to):
            arr, r0, n, _ = _PIECES[i]
            k = (rel - 1) * _NP + i
            return _remote(stage[arr].at[pl.ds(r0, n), :], _gathered_at(outs, arr, src_chip, r0, n),
                           ici_send.at[k], ici_recv.at[k], to)

        def fwd_copy(rel, i, to):
            arr, r0, n, _ = _PIECES[i]
            k = (rel - 1) * _NP + i
            place = _gathered_at(outs, arr, chip_of(rel), r0, n)
            return _remote(place, place, fwd_send.at[k], fwd_recv.at[k], to)

        _peer_wait()
        for core in (0, 1):
            @pl.when(c == core)
            def _(core=core):
                mine = [i for i in range(_NP) if _PIECES[i][3] == core and _PIECES[i][0] in arrays]
                theirs = [i for i in range(_NP) if _PIECES[i][3] != core and _PIECES[i][0] in arrays]
                sends = [ici_copy(rel, i, me, same_core_of(rel)) for rel in (1, 2, 3) for i in mine]
                for cp in sends:
                    cp.start()
                for ld, st in zip(loads, local):
                    ld.wait()
                    st.start()
                own = [pltpu.make_async_copy(stage[arr], _gathered_at(outs, arr, me, 0, _ARR_ROWS[arr]), own_sems.at[arr])
                       for arr in arrays if arr != 4]
                for cp in own:
                    cp.start()
                meta_all[me] = meta_ref[...]
                for rel in (1, 2, 3):
                    for i in mine:
                        ici_copy(rel, i, chip_of(rel), (x, y, c)).wait_recv()
                        fwd = fwd_copy(rel, i, (x, y, 1 - c))
                        fwd.start()
                        sends.append(fwd)
                for rel in (1, 2, 3):
                    for i in theirs:
                        fwd_copy(rel, i, (x, y, c)).wait_recv()
                for cp in sends:
                    cp.wait_send()
                for cp in own:
                    cp.wait()

        zeros.wait()
        for chip in range(CHIPS):
            head_buf[PAD:HEAD_ROWS, chip * 256:(chip + 1) * 256] = meta_all[chip]
        head = pltpu.make_async_copy(head_buf, h_o.at[pl.ds(0, HEAD_ROWS), :], loc_sems.at[2])
        head.start()
        head.wait()
        for cp in local:
            cp.wait()

    vm = pl.BlockSpec(memory_space=pltpu.VMEM)
    hbm = pl.BlockSpec(memory_space=pl.ANY)
    return pl.pallas_call(
        body,
        name="gather_weights",
        in_specs=[vm] * 4 + [hbm] * 2,
        out_specs=[hbm] * 5,
        out_shape=[
            jax.ShapeDtypeStruct((D_IN, D), BF16),
            jax.ShapeDtypeStruct((CHIPS, 256, Q_LORA), BF16), jax.ShapeDtypeStruct((CHIPS, KV_LORA, 256), BF16),
            jax.ShapeDtypeStruct((N, D), F32), jax.ShapeDtypeStruct((N, D), F32),
        ],
        scratch_shapes=[pltpu.VMEM((_ARR_ROWS[a], _ARR_COLS[a]), BF16) for a in (0, 2, 3)]
        + [pltpu.VMEM((CHIPS, N_META, 256), F32), pltpu.VMEM((HEAD_ROWS, D), F32), pltpu.VMEM((S, D), F32),
           pltpu.VMEM((S, D), F32)]
        + [pltpu.SemaphoreType.DMA((3 * _NP,))] * 4 + [pltpu.SemaphoreType.DMA((3,)), pltpu.SemaphoreType.DMA((4,))],
        compiler_params=_cparams(collective_id=0),
    )(winT_s, wqT_s, wkv_s, meta_s, x2, tgt2)


_SM_ROWS = (len(POOL_WINDOWS) * POOL_GROUP, VEC_ROWS)
_SM_COLS = (POOL_GROUP, D)
_SM_PIECES = ((0, 0, 256, 0), (0, 256, 256, 1), (1, 0, VEC_ROWS, 0))
_NSP = len(_SM_PIECES)


def _reduce_grads(dsl, hn, dwq, dwkv, dmeta4, dpw, dg, dgf, dgq, dgkv, dps, loss):
    arrays = (0, 2, 3, 4)
    loaded = (2, 3, 4)
    shard_order = (1, 2, 3, 0)

    def body(dsl_hbm, hn_hbm, dwq_ref, dwkv_ref, dmeta_ref, dpw_ref, dg_ref, dgf_ref, dgq_ref, dgkv_ref, dps_ref,
             loss_ref, gwin_o, gwq_o, gwkv_o, gmeta_o, gpw_o, gg_o, ggf_o, ggq_o, ggkv_o, gps_o, gloss_o,
             ow2, ow3, ow4, sb0, sb2, sb3, sb4, st0, st2, st3, st4, rc0, rc2, rc3, rc4,
             vec, sm_sb0, sm_sb1, sm_cs0, sm_cs1, sm_rc0, sm_rc1, vec_fin, slab_v, hn_v, dwin_buf, own0,
             own_sems, d2d_send, d2d_recv, ici_send, ici_recv, fin_send, fin_recv,
             swap_send, swap_recv, smi_send, smi_recv, smf_send, smf_recv, ld_sems):
        x, y, c = lax.axis_index("x"), lax.axis_index("y"), lax.axis_index("c")
        me = 2 * x + y
        _peer_signal(x, y, c)
        grads = (None, None, dwq_ref, dwkv_ref, dmeta_ref)
        outs = (gwin_o, None, gwq_o, gwkv_o, gmeta_o)
        own_buf = (None, None, ow2, ow3, ow4)
        sib_buf = (sb0, None, sb2, sb3, sb4)
        stage = (st0, None, st2, st3, st4)
        recv = (rc0, None, rc2, rc3, rc4)
        sm_mine = (dpw_ref, vec)
        sm_sib = (sm_sb0, sm_sb1)
        sm_chip = (sm_cs0, sm_cs1)
        sm_recv = (sm_rc0, sm_rc1)
        sm_out = (gpw_o, vec_fin)
        sibling = (x, y, 1 - c)

        def chip_of(rel):
            fx, fy = _CHIP_RELS[rel]
            return 2 * (x ^ fx) + (y ^ fy)

        def same_core_of(rel):
            fx, fy = _CHIP_RELS[rel]
            return (x ^ fx, y ^ fy, c)

        hn_load = pltpu.make_async_copy(hn_hbm, hn_v, ld_sems.at[CHIPS])

        def slab_load(rel):
            return pltpu.make_async_copy(dsl_hbm.at[chip_of(rel)], slab_v.at[rel], ld_sems.at[rel])

        hn_load.start()
        slab_load(shard_order[0]).start()

        def slot(bufs, i, idx):
            arr, _, n, _ = _PIECES[i]
            return bufs[arr].at[idx, pl.ds(0, n), :]

        def own_load(rel, i):
            arr, r0, n, _ = _PIECES[i]
            return pltpu.make_async_copy(_gathered_at(grads, arr, chip_of(rel), r0, n), slot(own_buf, i, rel),
                                         own_sems.at[rel * _NP + i])

        def d2d_copy(rel, i):
            arr, r0, n, _ = _PIECES[i]
            k = rel * _NP + i
            return _remote(_gathered_at(grads, arr, chip_of(rel), r0, n), slot(sib_buf, i, rel),
                           d2d_send.at[k], d2d_recv.at[k], sibling)

        def ici_copy(rel, i):
            k = (rel - 1) * _NP + i
            return _remote(slot(stage, i, rel - 1), slot(recv, i, rel - 1), ici_send.at[k], ici_recv.at[k],
                           same_core_of(rel))

        def fin_copy(i):
            arr, r0, n, _ = _PIECES[i]
            place = outs[arr].at[pl.ds(r0, n), :]
            return _remote(place, place, fin_send.at[i], fin_recv.at[i], sibling)

        def sm_ici_copy(rel, j):
            blk, r0, n, _ = _SM_PIECES[j]
            k = (rel - 1) * _NSP + j
            return _remote(sm_chip[blk].at[pl.ds(r0, n), :], sm_recv[blk].at[rel - 1, pl.ds(r0, n), :],
                           smi_send.at[k], smi_recv.at[k], same_core_of(rel))

        def sm_fin_copy(j):
            blk, r0, n, _ = _SM_PIECES[j]
            place = sm_out[blk].at[pl.ds(r0, n), :]
            return _remote(place, place, smf_send.at[j], smf_recv.at[j], sibling)

        vec[...] = jnp.zeros_like(vec)
        vec[0:1, :] = dg_ref[...]
        vec[1:2, :] = dgf_ref[...]
        vec[2:3, V_GQ:V_GQ + Q_LORA] = dgq_ref[...]
        vec[2:3, V_GKV:V_GKV + KV_LORA] = dgkv_ref[...]
        vec[2:3, V_PS:V_PS + D_POOL] = dps_ref[...]
        vec[2:3, V_LOSS:D] = loss_ref[...]
        _peer_wait()
        swaps = [_remote(sm_mine[b], sm_sib[b], swap_send.at[b], swap_recv.at[b], sibling) for b in (0, 1)]
        for cp in swaps:
            cp.start()

        for core in (0, 1):
            @pl.when(c == core)
            def _(core=core):
                mine = [i for i in range(_NP) if _PIECES[i][3] == core and _PIECES[i][0] in loaded]
                theirs = [i for i in range(_NP) if _PIECES[i][3] != core and _PIECES[i][0] in loaded]
                i0 = next(i for i in range(_NP) if _PIECES[i][0] == 0 and _PIECES[i][3] == core)
                j0 = next(i for i in range(_NP) if _PIECES[i][0] == 0 and _PIECES[i][3] != core)
                sm_mine_p = [j for j in range(_NSP) if _SM_PIECES[j][3] == core]
                sm_theirs_p = [j for j in range(_NSP) if _SM_PIECES[j][3] != core]
                sends = list(swaps)

                for rel in (1, 2, 3, 0):
                    for i in theirs:
                        cp = d2d_copy(rel, i)
                        cp.start()
                        sends.append(cp)
                    for i in mine:
                        own_load(rel, i).start()

                def piece_rows(i):
                    return pl.ds(_PIECES[i][1], _PIECES[i][2])

                def form(rel, i):
                    r0, n = _PIECES[i][1], _PIECES[i][2]
                    dwin_buf[rel, r0:r0 + n, :] = _tn(slab_v[rel, :, r0:r0 + _PIECE_MAX[0]], hn_v[...])[0:n, :]

                def d2d0(rel, i):
                    return _remote(dwin_buf.at[rel, piece_rows(i), :], slot(sib_buf, i, rel),
                                   d2d_send.at[rel * _NP + i], d2d_recv.at[rel * _NP + i], sibling)

                def settle(rel):
                    d2d0(rel, i0).wait_recv()
                    total = dwin_buf[rel, piece_rows(i0), :] + slot(sib_buf, i0, rel)[...]
                    if rel == 0:
                        own0[0:_PIECES[i0][2], :] = total
                    else:
                        slot(stage, i0, rel - 1)[...] = total.astype(BF16)
                        cp = ici_copy(rel, i0)
                        cp.start()
                        sends.append(cp)

                hn_load.wait()
                for n, rel in enumerate(shard_order):
                    slab_load(rel).wait()
                    if n == 0:
                        for later in shard_order[1:]:
                            slab_load(later).start()
                    form(rel, j0)
                    cp = d2d0(rel, j0)
                    cp.start()
                    sends.append(cp)
                    if n > 0:
                        settle(shard_order[n - 1])
                    form(rel, i0)
                settle(shard_order[-1])

                for rel in (1, 2, 3):
                    for i in mine:
                        arr, r0, n, _ = _PIECES[i]
                        own_load(rel, i).wait()
                        d2d_copy(rel, i).wait_recv()
                        total = slot(own_buf, i, rel)[...] + slot(sib_buf, i, rel)[...]
                        slot(stage, i, rel - 1)[...] = total.astype(stage[arr].dtype)
                        cp = ici_copy(rel, i)
                        cp.start()
                        sends.append(cp)

                for b in (0, 1):
                    swaps[b].wait_recv()
                    sm_chip[b][...] = sm_mine[b][...] + sm_sib[b][...]
                for rel in (1, 2, 3):
                    for j in sm_mine_p:
                        cp = sm_ici_copy(rel, j)
                        cp.start()
                        sends.append(cp)

                for i in mine:
                    arr, r0, n, _ = _PIECES[i]
                    own_load(0, i).wait()
                    d2d_copy(0, i).wait_recv()
                    total = slot(own_buf, i, 0)[...] + slot(sib_buf, i, 0)[...]
                    for rel in (1, 2, 3):
                        ici_copy(rel, i).wait_recv()
                        total = total + slot(recv, i, rel - 1)[...].astype(F32)
                    outs[arr][pl.ds(r0, n), :] = total
                    cp = fin_copy(i)
                    cp.start()
                    sends.append(cp)
                total = own0[0:_PIECES[i0][2], :]
                for rel in (1, 2, 3):
                    ici_copy(rel, i0).wait_recv()
                    total = total + slot(recv, i0, rel - 1)[...].astype(F32)
                outs[0][pl.ds(_PIECES[i0][1], _PIECES[i0][2]), :] = total
                cp = fin_copy(i0)
                cp.start()
                sends.append(cp)

                for j in sm_mine_p:
                    blk, r0, n, _ = _SM_PIECES[j]
                    for rel in (1, 2, 3):
                        sm_ici_copy(rel, j).wait_recv()
                    total = jnp.zeros((n, _SM_COLS[blk]), F32)
                    for chip in range(CHIPS):
                        flips = chip ^ me
                        rel = jnp.where(flips == 2, 1, jnp.where(flips == 1, 2, flips))
                        theirs_rows = sm_recv[blk][jnp.maximum(rel - 1, 0), pl.ds(r0, n), :]
                        total = total + jnp.where(rel == 0, sm_chip[blk][pl.ds(r0, n), :], theirs_rows)
                    sm_out[blk][pl.ds(r0, n), :] = total
                    cp = sm_fin_copy(j)
                    cp.start()
                    sends.append(cp)

                for i in theirs + [j0]:
                    fin_copy(i).wait_recv()
                for j in sm_theirs_p:
                    sm_fin_copy(j).wait_recv()
                for cp in sends:
                    cp.wait_send()

        gg_o[...] = vec_fin[0:1, :]
        ggf_o[...] = vec_fin[1:2, :]
        ggq_o[...] = vec_fin[2:3, V_GQ:V_GQ + Q_LORA]
        ggkv_o[...] = vec_fin[2:3, V_GKV:V_GKV + KV_LORA]
        gps_o[...] = vec_fin[2:3, V_PS:V_PS + D_POOL]
        gloss_o[...] = vec_fin[2:3, V_LOSS:D]

    vm = pl.BlockSpec(memory_space=pltpu.VMEM)
    piece_buf = lambda lead, dtype, which=arrays: [
        pltpu.VMEM((lead, _PIECE_MAX[a], _ARR_COLS[a]), F32 if a == 4 else dtype) for a in which]
    sm_buf = lambda *lead: [pltpu.VMEM(lead + (_SM_ROWS[b], _SM_COLS[b]), F32) for b in (0, 1)]
    dma = lambda n: [pltpu.SemaphoreType.DMA((n,))] * 2
    return pl.pallas_call(
        body,
        name="reduce_grads",
        in_specs=[pl.BlockSpec(memory_space=pl.ANY)] * 4 + [vm] * 8,
        out_specs=[vm] * 11,
        out_shape=[jax.ShapeDtypeStruct((_ARR_ROWS[a], _ARR_COLS[a]), F32) for a in arrays]
        + [jax.ShapeDtypeStruct((_SM_ROWS[0], _SM_COLS[0]), F32), jax.ShapeDtypeStruct((1, D), F32),
           jax.ShapeDtypeStruct((1, D), F32), jax.ShapeDtypeStruct((1, Q_LORA), F32),
           jax.ShapeDtypeStruct((1, KV_LORA), F32), jax.ShapeDtypeStruct((1, D_POOL), F32),
           jax.ShapeDtypeStruct((1, 128), F32)],
        scratch_shapes=piece_buf(CHIPS, F32, loaded) + piece_buf(CHIPS, F32) + piece_buf(3, BF16) + piece_buf(3, BF16)
        + [pltpu.VMEM((VEC_ROWS, D), F32)] + sm_buf() + sm_buf() + sm_buf(3) + [pltpu.VMEM((VEC_ROWS, D), F32)]
        + [pltpu.VMEM((CHIPS, N, SHARD_PAD), BF16), pltpu.VMEM((N, D), BF16),
           pltpu.VMEM((CHIPS, SHARD_PAD, D), F32), pltpu.VMEM((_PIECE_MAX[0], D), F32)]
        + [pltpu.SemaphoreType.DMA((CHIPS * _NP,))]
        + dma(CHIPS * _NP) + dma(3 * _NP) + dma(_NP) + dma(2) + dma(3 * _NSP) + dma(_NSP)
        + [pltpu.SemaphoreType.DMA((CHIPS + 1,))],
        compiler_params=_cparams(collective_id=3),
    )(dsl, hn, dwq, dwkv, dmeta4, dpw, dg, dgf, dgq, dgkv, dps, loss)


def _adamw_math(w, g, m, v):
    m = B1 * m + (1.0 - B1) * g
    v = B2 * v + (1.0 - B2) * (g * g)
    m_hat = m / C1
    v_hat = v / C2
    delta = -LR * (m_hat / (jnp.sqrt(v_hat) + ADAM_EPS) + WD * w)
    return delta, m, v


def _adamw(big, block_rows, groups):
    rows, cols = big[0].shape
    n = len(groups)

    def body(*refs):
        w_ref, g_ref, m_ref, v_ref = refs[0:4]
        small_in = refs[4:4 + 4 * n]
        go_ref, d_ref, nm_ref, nv_ref = refs[4 + 4 * n:8 + 4 * n]
        small_out = refs[8 + 4 * n:]
        g = g_ref[...]
        go_ref[...] = g
        d_ref[...], nm_ref[...], nv_ref[...] = _adamw_math(w_ref[...], g, m_ref[...], v_ref[...])

        @pl.when(pl.program_id(0) == 0)
        def _():
            for t in range(n):
                sw_ref, sg_ref, sm_ref, sv_ref = small_in[4 * t:4 * t + 4]
                sg = sg_ref[0:sw_ref.shape[0], :]
                small_out[4 * t][...] = sg
                small_out[4 * t + 1][...], small_out[4 * t + 2][...], small_out[4 * t + 3][...] = _adamw_math(
                    sw_ref[...], sg, sm_ref[...], sv_ref[...])

    spec = pl.BlockSpec((block_rows, cols), lambda i: (i, 0))
    vm = pl.BlockSpec(memory_space=pltpu.VMEM)
    outs = pl.pallas_call(
        body,
        name="adamw",
        grid=(rows // block_rows,),
        in_specs=[spec] * 4 + [vm] * (4 * n),
        out_specs=[spec] * 4 + [vm] * (4 * n),
        out_shape=[jax.ShapeDtypeStruct(big[0].shape, F32)] * 4
        + [jax.ShapeDtypeStruct(grp[0].shape, F32) for grp in groups for _ in range(4)],
        compiler_params=_cparams(dimension_semantics=("arbitrary",)),
    )(*big, *[a for grp in groups for a in grp])
    return tuple(outs[0:4]), [tuple(outs[4 + 4 * t:8 + 4 * t]) for t in range(n)]


def _rope_tables():
    half = QK_ROPE // 2
    f32 = np.float32
    inv_freq = (f32(1.0) / (f32(ROPE_THETA) ** (np.arange(half, dtype=f32) / f32(half)))).astype(f32)
    pos = np.arange(N, dtype=f32) - f32(PAD)
    ang = (pos[:, None] * inv_freq[None, :]).astype(f32)
    cos, sin = np.cos(ang).astype(f32), np.sin(ang).astype(f32)
    zero = np.zeros((N, 128 - QK_ROPE), f32)
    return jnp.asarray(np.concatenate([cos, cos, zero], axis=1)), jnp.asarray(np.concatenate([-sin, sin, zero], axis=1))


def kernel(x, meta_tokens, norm_g, w_in, q_norm_g, w_q_b, kv_norm_g, w_kv_b, pool_w, pool_scale, w_out, final_norm_g, loss_target, m_meta_tokens, m_norm_g, m_w_in, m_q_norm_g, m_w_q_b, m_kv_norm_g, m_w_kv_b, m_pool_w, m_pool_scale, m_w_out, m_final_norm_g, v_meta_tokens, v_norm_g, v_w_in, v_q_norm_g, v_w_q_b, v_kv_norm_g, v_w_kv_b, v_pool_w, v_pool_scale, v_w_out, v_final_norm_g):
    tr = lambda a: a[0].T
    win, wq, wkv, h, tgt = _gather_weights(tr(w_in), tr(w_q_b), w_kv_b[0], meta_tokens, x[0], loss_target[0])
    cosf, sinf = _rope_tables()
    gf = final_norm_g.reshape(1, D)

    part = _local_step(h, tgt, norm_g, win, q_norm_g, wq, kv_norm_g, wkv, pool_w[0], pool_scale, w_out[0], m_w_out[0],
                       v_w_out[0], gf, cosf, sinf)

    pw2 = lambda a: a.reshape(len(POOL_WINDOWS) * POOL_GROUP, POOL_GROUP)
    gwinT, gwqT, gwkv, gmeta, gpw, gg, ggf, ggq, ggkv, gps, gloss = _reduce_grads(
        part["dsl"], part["hn"], part["dwq"], part["dwkv"], part["dmeta"], pw2(part["dpw"]), part["dg"],
        part["dgf"], part["dgq"], part["dgkv"], part["dps"], part["loss"])

    r_out = part["r_out"]
    fn2 = lambda a: a.reshape(1, D)
    r_in, (r_meta, r_norm, r_gq, r_wq, r_gkv, r_wkv, r_pw, r_ps, r_fn) = _adamw((tr(w_in), gwinT, tr(m_w_in), tr(v_w_in)), 248, [
        (meta_tokens, gmeta, m_meta_tokens, v_meta_tokens),
        (norm_g, gg, m_norm_g, v_norm_g),
        (q_norm_g, ggq, m_q_norm_g, v_q_norm_g),
        (tr(w_q_b), gwqT, tr(m_w_q_b), tr(v_w_q_b)),
        (kv_norm_g, ggkv, m_kv_norm_g, v_kv_norm_g),
        (w_kv_b[0], gwkv, m_w_kv_b[0], v_w_kv_b[0]),
        (pw2(pool_w), gpw, pw2(m_pool_w), pw2(v_pool_w)),
        (pool_scale, gps, m_pool_scale, v_pool_scale),
        (fn2(final_norm_g), ggf, fn2(m_final_norm_g), fn2(v_final_norm_g)),
    ])
    untr = lambda a: a.T[None]
    pw4 = lambda a: a.reshape(1, len(POOL_WINDOWS), POOL_GROUP, POOL_GROUP)
    per_kind = [[
        r_meta[kind], r_norm[kind], untr(r_in[kind]), r_gq[kind], untr(r_wq[kind]), r_gkv[kind], r_wkv[kind][None],
        pw4(r_pw[kind]), r_ps[kind], r_out[kind][None], r_fn[kind].reshape(D),
    ] for kind in range(4)]
    return (gloss[0, 0], part["gx"][None], *per_kind[0], *per_kind[1], *per_kind[2], *per_kind[3])
```

```python
import jax
import jax.numpy as jnp
import numpy as np
from jax import lax
from jax.experimental import pallas as pl
from jax.experimental.pallas import tpu as pltpu

F32 = jnp.float32
BF16 = jnp.bfloat16

D = 1024
S = 2048
N_META = 16
PAD = 112
HEAD_ROWS = PAD + N_META
N = HEAD_ROWS + S
D_POOL = 512
POOL_WINDOWS = (2, 4, 8, 16)
POOL_GROUP = 128
HALO = 16
HEADS = 4
QK_NOPE = 128
QK_ROPE = 64
QK = QK_NOPE + QK_ROPE
V_HEAD = 128
Q_LORA = 256
KV_LORA = 128
D_IN = 1984
EPS = 1e-6
ROPE_THETA = 10000.0
SCALE = QK ** -0.5
CHIPS = 4

ROWS_FWD = 544
ROWS_MID = 544
ROWS_BWD = 544
TK = 128
TQ = 256
NQ = S // TQ
HEADS_PER_STEP_BWD = 2

O_PI, O_PG, O_CQ, O_CKV, O_KR, O_AG = 0, 512, 1024, 1280, 1408, 1472
O_KR_END = O_KR + 128
SHARD_IN = D_IN // CHIPS
SHARD_PAD = 512
SHARD_OUT = D // CHIPS

LR, B1, B2, ADAM_EPS, WD, STEP = 0.001, 0.9, 0.999, 1e-08, 0.01, 10
C1 = 1.0 - B1**STEP
C2 = 1.0 - B2**STEP

VMEM_LIMIT = 60 * 1024 * 1024
MESH = pl.DeviceIdType.MESH
NEG = -1e30

VEC_ROWS = 8
V_GQ, V_GKV, V_PS, V_LOSS = 0, 256, 384, 896


def _cparams(**kw):
    return pltpu.CompilerParams(vmem_limit_bytes=VMEM_LIMIT, **kw)


def _nt(a, b):
    return lax.dot_general(a, b, (((1,), (1,)), ((), ())), preferred_element_type=F32)


def _tn(a, b):
    return lax.dot_general(a, b, (((0,), (0,)), ((), ())), preferred_element_type=F32)


def _nn(a, b):
    return jnp.dot(a, b, preferred_element_type=F32)


def _swap64(t):
    return pltpu.roll(t, 32, 1) + pltpu.roll(t, 96, 1)


def _sigmoid(x):
    return 1.0 / (1.0 + jnp.exp(-x))


def _low_lanes():
    return (lax.broadcasted_iota(jnp.int32, (1, 128), 1) < QK_ROPE).astype(F32)


def _rows(w, rows):
    return pl.BlockSpec((rows, w), lambda i: (i, 0))


def _const(*shape):
    return pl.BlockSpec(shape, lambda *_: (0,) * len(shape), pipeline_mode=pl.Buffered(1))


STAT_GROUPS = HEADS // HEADS_PER_STEP_BWD


def _stat_slot(head):
    return head // HEADS_PER_STEP_BWD, head % HEADS_PER_STEP_BWD


N_PEERS = 4
SEND_ORDER = (3, 1, 2)


def _peer_signal(x, y, c):
    barrier = pltpu.get_barrier_semaphore()
    peers = [(x, y, 1 - c)] + [(x ^ fx, y ^ fy, c) for fx, fy in _CHIP_RELS[1:]]
    assert len(peers) == N_PEERS
    for peer in peers:
        pl.semaphore_signal(barrier, inc=1, device_id=peer, device_id_type=MESH)


def _peer_wait():
    pl.semaphore_wait(pltpu.get_barrier_semaphore(), N_PEERS)


def _attn_tiles():
    return [(0, TK, TK)] + [(TK + TQ * t, TQ, TK + TQ * (t + 1)) for t in range(NQ)]


def _masked_scores(q, k, rows, klen):
    s = _nt(q, k)
    col = lax.broadcasted_iota(jnp.int32, (1, TK), 1)
    head_bias = jnp.where(col >= PAD, 0.0, NEG)
    if klen == TK:
        return s + head_bias
    r = lax.broadcasted_iota(jnp.int32, (rows, 1), 0) >> 6
    c = lax.broadcasted_iota(jnp.int32, (1, rows), 1) >> 6
    diag_bias = jnp.where(c <= r, 0.0, NEG)
    parts = [s[:, 0:TK] + head_bias]
    if klen - rows > TK:
        parts.append(s[:, TK:klen - rows])
    parts.append(s[:, klen - rows:klen] + diag_bias)
    return jnp.concatenate(parts, axis=1)


def _fwd_in(h, norm_g, win, gq, wq, gkv, wkv, cosf, sinf):
    tr = ROWS_FWD

    def body(h_ref, g_ref, win_ref, gq_ref, wq_ref, gkv_ref, wkv_ref, cos_ref, sin_ref,
             pi_ref, pg_ref, cq_ref, ckv_ref, ag_ref, q_ref, k_ref, v_ref, hn_ref):
        h = h_ref[...]
        r = lax.rsqrt(jnp.mean(h * h, axis=-1, keepdims=True) + EPS)
        hn = ((h * r) * g_ref[...]).astype(BF16)
        hn_ref[...] = hn
        u = _nt(hn, win_ref[0:O_KR_END, :])
        pi_ref[...] = u[:, O_PI:O_PG]
        pg_ref[...] = u[:, O_PG:O_CQ]
        cq = u[:, O_CQ:O_CKV]
        ckv = u[:, O_CKV:O_KR]
        cq_ref[...] = cq
        ckv_ref[...] = ckv
        ag_ref[...] = _nt(hn, win_ref[O_AG:D_IN, :])
        cosv = cos_ref[...]
        sinv = sin_ref[...]
        kr = u[:, O_KR:O_KR_END] * _low_lanes()
        kr = (kr * cosv + _swap64(kr) * sinv).astype(BF16)
        rq = lax.rsqrt(jnp.mean(cq * cq, axis=-1, keepdims=True) + EPS)
        cqn = ((cq * rq) * gq_ref[...]).astype(BF16)
        rkv = lax.rsqrt(jnp.mean(ckv * ckv, axis=-1, keepdims=True) + EPS)
        ckvn = ((ckv * rkv) * gkv_ref[...]).astype(BF16)
        for hd in range(HEADS):
            qh = _nt(cqn, wq_ref[hd]) * SCALE
            z = qh[:, QK_NOPE:]
            q_ref[hd, :, 0:QK_NOPE] = qh[:, 0:QK_NOPE].astype(BF16)
            q_ref[hd, :, QK_NOPE:] = (z * cosv + _swap64(z) * sinv).astype(BF16)
            kvh = _nn(ckvn, wkv_ref[hd])
            k_ref[hd, :, 0:QK_NOPE] = kvh[:, 0:QK_NOPE].astype(BF16)
            k_ref[hd, :, QK_NOPE:] = kr
            v_ref[hd] = kvh[:, QK_NOPE:].astype(BF16)

    head = lambda w: pl.BlockSpec((HEADS, tr, w), lambda i: (0, i, 0))
    return pl.pallas_call(
        body,
        name="fwd_in",
        grid=(N // tr,),
        in_specs=[
            _rows(D, tr), _const(1, D), _const(D_IN, D), _const(1, Q_LORA), _const(HEADS, 256, Q_LORA),
            _const(1, KV_LORA), _const(HEADS, KV_LORA, 256), _rows(128, tr), _rows(128, tr),
        ],
        out_specs=[_rows(D_POOL, tr), _rows(D_POOL, tr), _rows(Q_LORA, tr), _rows(KV_LORA, tr), _rows(D_POOL, tr),
                   head(256), head(256), head(V_HEAD), _rows(D, tr)],
        out_shape=[
            jax.ShapeDtypeStruct((N, D_POOL), F32), jax.ShapeDtypeStruct((N, D_POOL), F32),
            jax.ShapeDtypeStruct((N, Q_LORA), F32), jax.ShapeDtypeStruct((N, KV_LORA), F32),
            jax.ShapeDtypeStruct((N, D_POOL), F32),
            jax.ShapeDtypeStruct((HEADS, N, 256), BF16), jax.ShapeDtypeStruct((HEADS, N, 256), BF16),
            jax.ShapeDtypeStruct((HEADS, N, V_HEAD), BF16), jax.ShapeDtypeStruct((N, D), BF16),
        ],
        compiler_params=_cparams(dimension_semantics=("arbitrary",)),
    )(h, norm_g, win, gq, wq, gkv, wkv, cosf, sinf)


def _attn_fwd(q, k, v, wout_s):
    tiles = _attn_tiles()
    n_t = len(tiles)
    half = SHARD_OUT // 2
    send_step = 2
    fwd_step = n_t - 2

    def body(q_hbm, k_hbm, v_hbm, wout_ref, o_hbm, lse_ref, wout_o, q_buf, k_buf, v_buf, o_buf, s_wout, in_sems, out_sems,
             ici_send, ici_recv, fwd_send, fwd_recv, own_sem):
        step = pl.program_id(0)
        x, y, c = lax.axis_index("x"), lax.axis_index("y"), lax.axis_index("c")
        me = 2 * x + y

        def chip_of(rel):
            fx, fy = _CHIP_RELS[rel]
            return 2 * (x ^ fx) + (y ^ fy)

        def place(chip, core):
            return wout_o.at[pl.ds(pl.multiple_of(SHARD_OUT * chip + half * core, half), half), :]

        def ici_copy(rel, src_chip, to):
            return _remote(s_wout.at[pl.ds(pl.multiple_of(half * c, half), half), :], place(src_chip, c),
                           ici_send.at[rel - 1], ici_recv.at[rel - 1], to)

        def fwd_copy(rel, core, to):
            spot = place(chip_of(rel), core)
            return _remote(spot, spot, fwd_send.at[rel - 1], fwd_recv.at[rel - 1], to)

        own = pltpu.make_async_copy(s_wout, wout_o.at[pl.ds(pl.multiple_of(SHARD_OUT * me, SHARD_OUT), SHARD_OUT), :], own_sem)

        @pl.when(step == 0)
        def _():
            _peer_signal(x, y, c)
            s_wout[...] = wout_ref[...].astype(BF16)
            own.start()

        @pl.when(step == send_step)
        def _():
            _peer_wait()
            for rel in SEND_ORDER:
                fx, fy = _CHIP_RELS[rel]
                ici_copy(rel, me, (x ^ fx, y ^ fy, c)).start()

        @pl.when(step == fwd_step)
        def _():
            for rel in (1, 2, 3):
                ici_copy(rel, chip_of(rel), (x, y, c)).wait_recv()
                fwd_copy(rel, c, (x, y, 1 - c)).start()

        def finish_wout():
            for rel in (1, 2, 3):
                fwd_copy(rel, 1 - c, (x, y, c)).wait_recv()
            for rel in (1, 2, 3):
                ici_copy(rel, me, (x, y, c)).wait_send()
                fwd_copy(rel, c, (x, y, c)).wait_send()
            own.wait()

        def loads(idx):
            q0, rows, _ = tiles[idx]
            rs = pl.ds(q0, rows)
            return [pltpu.make_async_copy(src.at[:, rs, :], dst.at[:, rs, :], in_sems.at[a, idx % 2])
                    for a, (src, dst) in enumerate(((q_hbm, q_buf), (k_hbm, k_buf), (v_hbm, v_buf)))]

        def store(idx):
            q0, rows, _ = tiles[idx]
            return pltpu.make_async_copy(o_buf.at[idx % 2, pl.ds(0, rows), :], o_hbm.at[pl.ds(q0, rows), :],
                                         out_sems.at[idx % 2])

        @pl.when(step == 0)
        def _():
            lse_ref[...] = jnp.zeros_like(lse_ref)
            for cp in loads(0):
                cp.start()

        for idx, (q0, rows, klen) in enumerate(tiles):
            @pl.when(step == idx)
            def _(idx=idx, q0=q0, rows=rows, klen=klen):
                for cp in loads(idx):
                    cp.wait()
                if idx + 1 < n_t:
                    for cp in loads(idx + 1):
                        cp.start()
                if idx >= 2:
                    store(idx - 2).wait()
                for hd in range(HEADS):
                    s = _masked_scores(q_buf[hd, q0:q0 + rows, :], k_buf[hd, 0:klen, :], rows, klen)
                    m = jnp.max(s, axis=-1, keepdims=True)
                    p = jnp.exp(s - m)
                    l = jnp.sum(p, axis=-1, keepdims=True)
                    o_buf[idx % 2, 0:rows, hd * V_HEAD:(hd + 1) * V_HEAD] = _nn(p.astype(BF16), v_buf[hd, 0:klen, :]) / l
                    grp, lane = _stat_slot(hd)
                    lse_ref[grp, q0:q0 + rows, lane:lane + 1] = m + jnp.log(l)
                store(idx).start()
                if idx == n_t - 1:
                    store(idx - 1).wait()
                    store(idx).wait()
                    finish_wout()

    hbm = pl.BlockSpec(memory_space=pl.ANY)
    return pl.pallas_call(
        body,
        name="attn_fwd",
        grid=(n_t,),
        in_specs=[hbm, hbm, hbm, _const(SHARD_OUT, D)],
        out_specs=[hbm, _const(STAT_GROUPS, N, 128), hbm],
        out_shape=[jax.ShapeDtypeStruct((N, HEADS * V_HEAD), F32), jax.ShapeDtypeStruct((STAT_GROUPS, N, 128), F32),
                   jax.ShapeDtypeStruct((D, D), BF16)],
        scratch_shapes=[pltpu.VMEM((HEADS, N, 256), BF16), pltpu.VMEM((HEADS, N, 256), BF16),
                        pltpu.VMEM((HEADS, N, V_HEAD), BF16), pltpu.VMEM((2, TQ, HEADS * V_HEAD), F32),
                        pltpu.VMEM((SHARD_OUT, D), BF16),
                        pltpu.SemaphoreType.DMA((3, 2)), pltpu.SemaphoreType.DMA((2,))]
        + [pltpu.SemaphoreType.DMA((3,))] * 4 + [pltpu.SemaphoreType.DMA],
        compiler_params=_cparams(dimension_semantics=("arbitrary",), collective_id=1),
    )(q, k, v, wout_s)


def _inv_count(row0, rows, w):
    row = row0 + lax.broadcasted_iota(jnp.int32, (rows, 1), 0)
    return 1.0 / jnp.clip(row - (PAD - 1), 1, w).astype(F32)


def _mid(h, tgt, pool_in, pool_gate, attn_gate, attn, pool_w, pool_scale, wout, gf):
    tr = ROWS_MID
    per = tr // HALO
    ng = len(POOL_WINDOWS)

    def body(h_ref, t_ref, pin_ref, halo_ref, pg_ref, ag_ref, at_ref, pw_ref, ps_ref, wout_ref, gf_ref,
             dh2_ref, do_ref, delta_ref, dag_ref, dpg_ref, dpl_ref, dwout_ref, dpw_ref, dps_ref, dgf_ref, loss_ref):
        i = pl.program_id(0)

        @pl.when(i == 0)
        def _():
            dwout_ref[...] = jnp.zeros_like(dwout_ref)
            dpw_ref[...] = jnp.zeros_like(dpw_ref)
            dps_ref[...] = jnp.zeros_like(dps_ref)
            dgf_ref[...] = jnp.zeros_like(dgf_ref)
            loss_ref[...] = jnp.zeros_like(loss_ref)

        row0 = i * tr
        real = (row0 + lax.broadcasted_iota(jnp.int32, (tr, 1), 0)) >= HEAD_ROWS
        h = h_ref[...]

        halo = jnp.where(i > 0, halo_ref[...], 0.0)
        ext = jnp.concatenate([halo, pin_ref[...]], axis=0)
        pooled = []
        for g, w in enumerate(POOL_WINDOWS):
            e = ext[:, g * POOL_GROUP:(g + 1) * POOL_GROUP]
            acc = e
            shift = 1
            while shift < w:
                acc = acc + pltpu.roll(acc, shift, 0)
                shift *= 2
            pooled.append((acc[HALO:] * _inv_count(row0, tr, w) - e[HALO:]).astype(BF16))
        pw = [pw_ref[g].astype(BF16) for g in range(ng)]
        mixed = jnp.concatenate([_nn(pooled[g], pw[g]) for g in range(ng)], axis=1)
        ps = ps_ref[...]
        mixed_s = mixed * ps
        pg = pg_ref[...]
        sig_p = _sigmoid(pg)
        silu_p = pg * sig_p
        pool_out = (silu_p * mixed_s).astype(BF16)
        ag = ag_ref[...]
        sig_a = _sigmoid(ag)
        silu_a = ag * sig_a
        at = at_ref[...]
        attn_out = (silu_a * at).astype(BF16)
        cat = jnp.concatenate([pool_out, attn_out], axis=1)
        h2 = h + _nn(cat, wout_ref[...])

        r2 = lax.rsqrt(jnp.mean(h2 * h2, axis=-1, keepdims=True) + EPS)
        n2 = h2 * r2
        gfv = gf_ref[...]
        err = jnp.where(real, n2 * gfv - t_ref[...], 0.0)
        loss_ref[...] += jnp.sum(jnp.sum(err * err, axis=-1, keepdims=True), axis=0, keepdims=True) * (0.5 / D)
        dy = err * (1.0 / D)
        dgf_ref[...] += jnp.sum(dy * n2, axis=0, keepdims=True)
        dn = dy * gfv
        dh2 = r2 * (dn - n2 * jnp.mean(dn * n2, axis=-1, keepdims=True))
        dh2_ref[...] = dh2
        dh2b = dh2.astype(BF16)

        dwout_ref[...] += _tn(cat, dh2b)
        dcat = _nt(dh2b, wout_ref[...])
        dpo = dcat[:, 0:D_POOL]
        dao = dcat[:, D_POOL:D]
        do = dao * silu_a
        prod = do * at
        delta_ref[...] = jnp.zeros_like(delta_ref)
        for hd in range(HEADS):
            grp, lane = _stat_slot(hd)
            cols = slice(hd * V_HEAD, (hd + 1) * V_HEAD)
            do_ref[grp, :, lane * V_HEAD:(lane + 1) * V_HEAD] = do[:, cols].astype(BF16)
            delta_ref[grp, :, lane:lane + 1] = jnp.sum(prod[:, cols], axis=-1, keepdims=True)
        dag_ref[...] = (dao * at * (sig_a * (1.0 + ag * (1.0 - sig_a)))).astype(BF16)
        dmixed_s = dpo * silu_p
        dpg_ref[...] = (dpo * mixed_s * (sig_p * (1.0 + pg * (1.0 - sig_p)))).astype(BF16)
        dps_ref[...] += jnp.sum(dmixed_s * mixed, axis=0, keepdims=True)
        dmixed = (dmixed_s * ps).astype(BF16)
        dpl = []
        for g in range(ng):
            dm = dmixed[:, g * POOL_GROUP:(g + 1) * POOL_GROUP]
            dpl.append(_nt(dm, pw[g]))
            dpw_ref[g] += _tn(pooled[g], dm)
        dpl_ref[...] = jnp.concatenate(dpl, axis=1)

    halo_spec = pl.BlockSpec((HALO, D_POOL), lambda i: (jnp.maximum(i * per - 1, 0), 0))
    return pl.pallas_call(
        body,
        name="mid",
        grid=(N // tr,),
        in_specs=[
            _rows(D, tr), _rows(D, tr), _rows(D_POOL, tr), halo_spec, _rows(D_POOL, tr), _rows(D_POOL, tr),
            _rows(D_POOL, tr), _const(ng, POOL_GROUP, POOL_GROUP), _const(1, D_POOL), _const(D, D), _const(1, D),
        ],
        out_specs=[
            _rows(D, tr), pl.BlockSpec((STAT_GROUPS, tr, HEADS_PER_STEP_BWD * V_HEAD), lambda i: (0, i, 0)),
            pl.BlockSpec((STAT_GROUPS, tr, 128), lambda i: (0, i, 0)),
            _rows(D_POOL, tr), _rows(D_POOL, tr), _rows(D_POOL, tr),
            _const(D, D), _const(ng, POOL_GROUP, POOL_GROUP), _const(1, D_POOL), _const(1, D), _const(1, 128),
        ],
        out_shape=[
            jax.ShapeDtypeStruct((N, D), F32), jax.ShapeDtypeStruct((STAT_GROUPS, N, HEADS_PER_STEP_BWD * V_HEAD), BF16),
            jax.ShapeDtypeStruct((STAT_GROUPS, N, 128), F32),
            jax.ShapeDtypeStruct((N, D_POOL), BF16), jax.ShapeDtypeStruct((N, D_POOL), BF16),
            jax.ShapeDtypeStruct((N, D_POOL), F32), jax.ShapeDtypeStruct((D, D), F32),
            jax.ShapeDtypeStruct((ng, POOL_GROUP, POOL_GROUP), F32),
            jax.ShapeDtypeStruct((1, D_POOL), F32), jax.ShapeDtypeStruct((1, D), F32), jax.ShapeDtypeStruct((1, 128), F32),
        ],
        compiler_params=_cparams(dimension_semantics=("arbitrary",)),
    )(h, tgt, pool_in, pool_in, pool_gate, attn_gate, attn, pool_w, pool_scale, wout, gf)


def _unrope(dy, cosv, sinv):
    return dy * cosv + _swap64(dy * sinv) * _low_lanes()


def _attn_bwd(q, k, v, do, lse, delta, cosf, sinf, dwout):
    tiles = _attn_tiles()
    hp = HEADS_PER_STEP_BWD
    n_g = HEADS // hp
    n_t = len(tiles)
    half = SHARD_OUT // 2
    swap_at, send_at, sum_at = (0, 3), (0, 5), (n_g - 1, n_t // 2)

    def body(q_hbm, k_hbm, v_hbm, do_hbm, lse_ref, delta_ref, cos_ref, sin_ref, dwout_hbm, dq_hbm, dkv_ref, dkr_ref,
             gwout_ref, q_buf, k_buf, v_buf, do_buf, dq_buf, dk_acc, dv_acc, own_w, sib_w, stage_w, recv_w, gw_buf,
             in_sems, out_sems, ow_sems, d2d_send, d2d_recv, ici_send, ici_recv, fin_send, fin_recv):
        grp = pl.program_id(0)
        step = pl.program_id(1)
        heads = pl.ds(grp * hp, hp)
        x, y, c = lax.axis_index("x"), lax.axis_index("y"), lax.axis_index("c")
        sibling = (x, y, 1 - c)

        def chip_of(rel):
            fx, fy = _CHIP_RELS[rel]
            return 2 * (x ^ fx) + (y ^ fy)

        def piece(chip, core):
            return dwout_hbm.at[pl.ds(pl.multiple_of(SHARD_OUT * chip + half * core, half), half), :]

        def own_load(rel):
            return pltpu.make_async_copy(piece(chip_of(rel), c), own_w.at[rel], ow_sems.at[rel])

        def d2d_copy(rel):
            return _remote(piece(chip_of(rel), 1 - c), sib_w.at[rel], d2d_send.at[rel], d2d_recv.at[rel], sibling)

        def ici_copy(rel):
            fx, fy = _CHIP_RELS[rel]
            return _remote(stage_w.at[rel - 1], recv_w.at[rel - 1], ici_send.at[rel - 1], ici_recv.at[rel - 1],
                           (x ^ fx, y ^ fy, c))

        def fin_copy(core):
            spot = gw_buf.at[pl.ds(pl.multiple_of(half * core, half), half), :]
            return _remote(spot, spot, fin_send.at[0], fin_recv.at[0], sibling)

        @pl.when((grp == 0) & (step == 0))
        def _():
            _peer_signal(x, y, c)
            for rel in SEND_ORDER + (0,):
                own_load(rel).start()

        @pl.when((grp == swap_at[0]) & (step == swap_at[1]))
        def _():
            _peer_wait()
            for rel in SEND_ORDER + (0,):
                d2d_copy(rel).start()

        @pl.when((grp == send_at[0]) & (step == send_at[1]))
        def _():
            for rel in SEND_ORDER:
                own_load(rel).wait()
                d2d_copy(rel).wait_recv()
                stage_w[rel - 1] = (own_w[rel] + sib_w[rel]).astype(BF16)
                ici_copy(rel).start()

        @pl.when((grp == sum_at[0]) & (step == sum_at[1]))
        def _():
            own_load(0).wait()
            d2d_copy(0).wait_recv()
            total = own_w[0] + sib_w[0]
            for rel in (1, 2, 3):
                ici_copy(rel).wait_recv()
                total = total + recv_w[rel - 1].astype(F32)
            gw_buf[pl.ds(pl.multiple_of(half * c, half), half), :] = total
            fin_copy(c).start()

        def finish_dwout():
            fin_copy(1 - c).wait_recv()
            for rel in (0, 1, 2, 3):
                d2d_copy(rel).wait_send()
            for rel in (1, 2, 3):
                ici_copy(rel).wait_send()
            fin_copy(c).wait_send()
            gwout_ref[...] = gw_buf[...]

        def loads(g, idx):
            q0, rows, _ = tiles[idx]
            rs = pl.ds(q0, rows)
            par = (g * n_t + idx) % 2
            hs = pl.ds(g * hp, hp)
            pairs = ((q_hbm.at[hs, rs, :], q_buf.at[:, rs, :]), (k_hbm.at[hs, rs, :], k_buf.at[:, rs, :]),
                     (v_hbm.at[hs, rs, :], v_buf.at[:, rs, :]), (do_hbm.at[g, rs, :], do_buf.at[rs, :]))
            return [pltpu.make_async_copy(src, dst, in_sems.at[a, par]) for a, (src, dst) in enumerate(pairs)]

        def store(idx):
            q0, rows, _ = tiles[idx]
            return pltpu.make_async_copy(dq_buf.at[idx % 2, :, pl.ds(0, rows), :], dq_hbm.at[heads, pl.ds(q0, rows), :],
                                         out_sems.at[idx % 2])

        @pl.when(step == 0)
        def _():
            dk_acc[...] = jnp.zeros_like(dk_acc)
            dv_acc[...] = jnp.zeros_like(dv_acc)

        @pl.when((step == 0) & (grp == 0))
        def _():
            dkr_ref[...] = jnp.zeros_like(dkr_ref)
            for cp in loads(grp, 0):
                cp.start()

        for idx, (q0, rows, klen) in enumerate(tiles):
            @pl.when(step == idx)
            def _(idx=idx, q0=q0, rows=rows, klen=klen):
                for cp in loads(grp, idx):
                    cp.wait()
                if idx + 1 < n_t:
                    for cp in loads(grp, idx + 1):
                        cp.start()
                if idx >= 2:
                    store(idx - 2).wait()
                qs = pl.ds(q0, rows)
                for hd in range(hp):
                    qv = q_buf[hd, qs, :]
                    kv = k_buf[hd, 0:klen, :]
                    p = jnp.exp(_masked_scores(qv, kv, rows, klen) - lse_ref[0, qs, hd:hd + 1])
                    dob = do_buf[qs, hd * V_HEAD:(hd + 1) * V_HEAD]
                    ds = (p * (_nt(dob, v_buf[hd, 0:klen, :]) - delta_ref[0, qs, hd:hd + 1])).astype(BF16)
                    dq = _nn(ds, kv) * SCALE
                    dq_buf[idx % 2, hd, 0:rows, 0:QK_NOPE] = dq[:, 0:QK_NOPE].astype(BF16)
                    dq_buf[idx % 2, hd, 0:rows, QK_NOPE:] = _unrope(dq[:, QK_NOPE:], cos_ref[qs, :], sin_ref[qs, :]).astype(BF16)
                    dk_acc[hd, 0:klen, :] += _tn(ds, qv)
                    dv_acc[hd, 0:klen, :] += _tn(p.astype(BF16), dob)
                store(idx).start()

        @pl.when(step == n_t - 1)
        def _():
            @pl.when(grp + 1 < n_g)
            def _():
                for cp in loads(grp + 1, 0):
                    cp.start()

            for hd in range(hp):
                dkv_ref[hd, :, 0:QK_NOPE] = dk_acc[hd, :, 0:QK_NOPE].astype(BF16)
                dkv_ref[hd, :, QK_NOPE:] = dv_acc[hd].astype(BF16)
                dkr_ref[...] += dk_acc[hd, :, QK_NOPE:]
            store(n_t - 2).wait()
            store(n_t - 1).wait()

            @pl.when(grp == n_g - 1)
            def _():
                finish_dwout()

    hbm = pl.BlockSpec(memory_space=pl.ANY)
    stat = pl.BlockSpec((1, N, 128), lambda g, t: (g, 0, 0), pipeline_mode=pl.Buffered(1))
    piece_f32 = lambda lead: pltpu.VMEM((lead, half, D), F32)
    piece_bf16 = lambda lead: pltpu.VMEM((lead, half, D), BF16)
    return pl.pallas_call(
        body,
        name="attn_bwd",
        grid=(n_g, n_t),
        in_specs=[hbm, hbm, hbm, hbm, stat, stat, _const(N, 128), _const(N, 128), hbm],
        out_specs=[hbm, pl.BlockSpec((hp, N, 256), lambda g, t: (g, 0, 0), pipeline_mode=pl.Buffered(1)), _const(N, 128),
                   _const(SHARD_OUT, D)],
        out_shape=[
            jax.ShapeDtypeStruct((HEADS, N, 256), BF16), jax.ShapeDtypeStruct((HEADS, N, 256), BF16),
            jax.ShapeDtypeStruct((N, 128), F32), jax.ShapeDtypeStruct((SHARD_OUT, D), F32),
        ],
        scratch_shapes=[pltpu.VMEM((hp, N, 256), BF16), pltpu.VMEM((hp, N, 256), BF16), pltpu.VMEM((hp, N, V_HEAD), BF16),
                        pltpu.VMEM((N, hp * V_HEAD), BF16), pltpu.VMEM((2, hp, TQ, 256), BF16),
                        pltpu.VMEM((hp, N, 256), F32), pltpu.VMEM((hp, N, V_HEAD), F32),
                        piece_f32(CHIPS), piece_f32(CHIPS), piece_bf16(3), piece_bf16(3), pltpu.VMEM((SHARD_OUT, D), F32),
                        pltpu.SemaphoreType.DMA((4, 2)), pltpu.SemaphoreType.DMA((2,)), pltpu.SemaphoreType.DMA((CHIPS,)),
                        pltpu.SemaphoreType.DMA((CHIPS,)), pltpu.SemaphoreType.DMA((CHIPS,)),
                        pltpu.SemaphoreType.DMA((3,)), pltpu.SemaphoreType.DMA((3,)),
                        pltpu.SemaphoreType.DMA((1,)), pltpu.SemaphoreType.DMA((1,))],
        compiler_params=_cparams(dimension_semantics=("arbitrary", "arbitrary"), collective_id=2),
    )(q, k, v, do, lse, delta, cosf, sinf, dwout)


def _bwd_in(h, dh2, dq, dkv, dkr, cq, ckv, dpl, dpg, dag, norm_g, win, gq, wq, gkv, wkv, cosf, sinf, adam_out):
    tr = ROWS_BWD
    nb = N // tr
    per = tr // HALO
    lead = HEAD_ROWS
    adam_rows = SHARD_OUT // nb

    def body(h_ref, dh2_ref, dq_ref, dkv_ref, dkr_ref, cq_ref, ckv_ref, dpl_ref, halo_ref, dpg_ref, dag_ref,
             g_ref, win_ref, gq_ref, wq_ref, gkv_ref, wkv_ref, cos_ref, sin_ref, aw_ref, ag_ref, am_ref, av_ref,
             gx_ref, dmeta_ref, dsl_ref, dwq_ref, dwkv_ref, dg_ref, dgq_ref, dgkv_ref, ago_ref, ad_ref, anm_ref, anv_ref,
             dh_buf, gx_sem):
        i = pl.program_id(0)
        grad_out = ag_ref[...]
        ago_ref[...] = grad_out
        ad_ref[...], anm_ref[...], anv_ref[...] = _adamw_math(aw_ref[...], grad_out, am_ref[...], av_ref[...])

        @pl.when(i == 0)
        def _():
            dwq_ref[...] = jnp.zeros_like(dwq_ref)
            dwkv_ref[...] = jnp.zeros_like(dwkv_ref)
            dg_ref[...] = jnp.zeros_like(dg_ref)
            dgq_ref[...] = jnp.zeros_like(dgq_ref)
            dgkv_ref[...] = jnp.zeros_like(dgkv_ref)

        row0 = i * tr
        h = h_ref[...]
        r = lax.rsqrt(jnp.mean(h * h, axis=-1, keepdims=True) + EPS)
        n = h * r
        gv = g_ref[...]
        cq = cq_ref[...]
        rq = lax.rsqrt(jnp.mean(cq * cq, axis=-1, keepdims=True) + EPS)
        nq = cq * rq
        gqv = gq_ref[...]
        cqn = (nq * gqv).astype(BF16)
        dcqn = jnp.zeros((tr, Q_LORA), F32)
        for hd in range(HEADS):
            dqf = dq_ref[hd]
            dcqn = dcqn + _nn(dqf, wq_ref[hd])
            dwq_ref[hd] += _tn(dqf, cqn)
        dgq_ref[...] += jnp.sum(dcqn * nq, axis=0, keepdims=True)
        dnq = dcqn * gqv
        dcq = rq * (dnq - nq * jnp.mean(dnq * nq, axis=-1, keepdims=True))

        ckv = ckv_ref[...]
        rkv = lax.rsqrt(jnp.mean(ckv * ckv, axis=-1, keepdims=True) + EPS)
        nkv = ckv * rkv
        gkvv = gkv_ref[...]
        ckvn = (nkv * gkvv).astype(BF16)
        dckvn = jnp.zeros((tr, KV_LORA), F32)
        for hd in range(HEADS):
            dkv = dkv_ref[hd]
            dckvn = dckvn + _nt(dkv, wkv_ref[hd])
            dwkv_ref[hd] += _tn(ckvn, dkv)
        dgkv_ref[...] += jnp.sum(dckvn * nkv, axis=0, keepdims=True)
        dnkv = dckvn * gkvv
        dckv = rkv * (dnkv - nkv * jnp.mean(dnkv * nkv, axis=-1, keepdims=True))
        dkr = _unrope(dkr_ref[...], cos_ref[...], sin_ref[...])

        cur = dpl_ref[...]
        halo = jnp.where(i < nb - 1, halo_ref[...], 0.0)
        dpi = []
        for g, w in enumerate(POOL_WINDOWS):
            sl = slice(g * POOL_GROUP, (g + 1) * POOL_GROUP)
            a = jnp.concatenate([cur[:, sl] * _inv_count(row0, tr, w), halo[:, sl] * _inv_count(row0 + tr, HALO, w)], axis=0)
            acc = a
            shift = 1
            while shift < w:
                acc = acc + pltpu.roll(acc, tr + HALO - shift, 0)
                shift *= 2
            dpi.append(acc[0:tr] - cur[:, sl])

        du = jnp.concatenate([t.astype(BF16) for t in dpi] + [dpg_ref[...]] + [t.astype(BF16) for t in (dcq, dckv, dkr)],
                             axis=1)
        dagb = dag_ref[...]
        by_row = jnp.concatenate(dpi + [dpg_ref[...].astype(F32), dcq, dckv, dkr[:, 0:QK_ROPE], dagb.astype(F32),
                                        jnp.zeros((tr, SHARD_PAD - SHARD_IN), F32)], axis=1)
        for chip in range(CHIPS):
            dsl_ref[chip] = by_row[:, SHARD_IN * chip:SHARD_IN * chip + SHARD_PAD].astype(BF16)
        dhn = _nn(du, win_ref[0:O_KR_END, :]) + _nn(dagb, win_ref[O_AG:D_IN, :])
        dg_ref[...] += jnp.sum(dhn * n, axis=0, keepdims=True)
        dn = dhn * gv
        dh = dh2_ref[...] + r * (dn - n * jnp.mean(dn * n, axis=-1, keepdims=True))

        first = pltpu.make_async_copy(dh_buf.at[pl.ds(lead, tr - lead), :], gx_ref.at[pl.ds(0, tr - lead), :], gx_sem)
        later = lambda step: pltpu.make_async_copy(
            dh_buf, gx_ref.at[pl.ds(pl.multiple_of(step * tr - lead, 16), tr), :], gx_sem)

        @pl.when(i == 1)
        def _():
            first.wait()

        @pl.when(i > 1)
        def _():
            later(i - 1).wait()

        dh_buf[...] = dh

        @pl.when(i == 0)
        def _():
            first.start()
            for chip in range(CHIPS):
                dmeta_ref[chip] = dh[PAD:HEAD_ROWS, chip * 256:(chip + 1) * 256]

        @pl.when(i > 0)
        def _():
            later(i).start()

        @pl.when(i == nb - 1)
        def _():
            later(i).wait()

    head = lambda w: pl.BlockSpec((HEADS, tr, w), lambda i: (0, i, 0))
    halo_spec = pl.BlockSpec((HALO, D_POOL), lambda i: (jnp.minimum((i + 1) * per, N // HALO - 1), 0))
    return pl.pallas_call(
        body,
        name="bwd_in",
        grid=(nb,),
        in_specs=[
            _rows(D, tr), _rows(D, tr), head(256), head(256), _rows(128, tr), _rows(Q_LORA, tr), _rows(KV_LORA, tr),
            _rows(D_POOL, tr), halo_spec, _rows(D_POOL, tr), _rows(D_POOL, tr),
            _const(1, D), _const(D_IN, D), _const(1, Q_LORA), _const(HEADS, 256, Q_LORA),
            _const(1, KV_LORA), _const(HEADS, KV_LORA, 256), _rows(128, tr), _rows(128, tr),
        ] + [_rows(D, adam_rows)] * 4,
        out_specs=[
            pl.BlockSpec(memory_space=pl.ANY), _const(CHIPS, N_META, 256),
            pl.BlockSpec((CHIPS, tr, SHARD_PAD), lambda i: (0, i, 0)), _const(HEADS, 256, Q_LORA),
            _const(HEADS, KV_LORA, 256), _const(1, D), _const(1, Q_LORA), _const(1, KV_LORA),
        ] + [_rows(D, adam_rows)] * 4,
        out_shape=[
            jax.ShapeDtypeStruct((S, D), F32), jax.ShapeDtypeStruct((CHIPS, N_META, 256), F32),
            jax.ShapeDtypeStruct((CHIPS, N, SHARD_PAD), BF16), jax.ShapeDtypeStruct((HEADS, 256, Q_LORA), F32),
            jax.ShapeDtypeStruct((HEADS, KV_LORA, 256), F32),
            jax.ShapeDtypeStruct((1, D), F32), jax.ShapeDtypeStruct((1, Q_LORA), F32), jax.ShapeDtypeStruct((1, KV_LORA), F32),
        ] + [jax.ShapeDtypeStruct((SHARD_OUT, D), F32)] * 4,
        scratch_shapes=[pltpu.VMEM((tr, D), F32), pltpu.SemaphoreType.DMA],
        compiler_params=_cparams(dimension_semantics=("arbitrary",)),
    )(h, dh2, dq, dkv, dkr, cq, ckv, dpl, dpl, dpg, dag, norm_g, win, gq, wq, gkv, wkv, cosf, sinf, *adam_out)


def _local_step(h, tgt, norm_g, win, gq, wq, gkv, wkv, pool_w, pool_scale, wout_s, m_wout_s, v_wout_s, gf, cosf, sinf):
    pool_in, pool_gate, cq, ckv, attn_gate, q, k, v, hn = _fwd_in(h, norm_g, win, gq, wq, gkv, wkv, cosf, sinf)
    attn, lse, wout = _attn_fwd(q, k, v, wout_s)
    dh2, do, delta, dag, dpg, dpl, dwout, dpw, dps, dgf, loss = _mid(
        h, tgt, pool_in, pool_gate, attn_gate, attn, pool_w, pool_scale, wout, gf)
    dq, dkv, dkr, gwout = _attn_bwd(q, k, v, do, lse, delta, cosf, sinf, dwout)
    gx, dmeta, dsl, dwq, dwkv, dg, dgq, dgkv, *r_out = _bwd_in(
        h, dh2, dq, dkv, dkr, cq, ckv, dpl, dpg, dag, norm_g, win, gq, wq, gkv, wkv, cosf, sinf,
        (wout_s, gwout, m_wout_s, v_wout_s))
    return dict(gx=gx, dmeta=dmeta, dsl=dsl, hn=hn, dwq=dwq, dwkv=dwkv, r_out=tuple(r_out), dg=dg, dgq=dgq,
                dgkv=dgkv, dpw=dpw, dps=dps, dgf=dgf, loss=loss)


_CHIP_RELS = ((0, 0), (1, 0), (0, 1), (1, 1))

_ARR_ROWS = (SHARD_IN, SHARD_OUT, 256, KV_LORA, N_META)
_ARR_COLS = (D, D, Q_LORA, 256, 256)
_PIECES = (
    (0, 0, 256, 0), (0, 256, SHARD_IN - 256, 1),
    (1, 0, 128, 0), (1, 128, 128, 1),
    (2, 0, 128, 0), (2, 128, 128, 1),
    (3, 0, 64, 0), (3, 64, 64, 1),
    (4, 0, N_META, 0),
)
_NP = len(_PIECES)
_PIECE_MAX = (256, 128, 128, 64, N_META)


def _gathered_at(refs, arr, chip, r0, n):
    if arr in (0, 1):
        return refs[arr].at[pl.ds(pl.multiple_of(_ARR_ROWS[arr] * chip + r0, 16), n), :]
    return refs[arr].at[chip, pl.ds(r0, n), :]


def _remote(src, dst, send_sem, recv_sem, to):
    return pltpu.make_async_remote_copy(src_ref=src, dst_ref=dst, send_sem=send_sem, recv_sem=recv_sem,
                                        device_id=to, device_id_type=MESH)


def _gather_weights(winT_s, wqT_s, wkv_s, meta_s, x2, tgt2):
    arrays = (0, 2, 3, 4)

    def body(win_ref, wq_ref, wkv_ref, meta_ref, x_ref, t_ref, win_o, wq_o, wkv_o, h_o, tp_o,
             s_win, s_wq, s_wkv, meta_all, head_buf, x_buf, t_buf, ici_send, ici_recv, fwd_send, fwd_recv,
             loc_sems, own_sems):
        x, y, c = lax.axis_index("x"), lax.axis_index("y"), lax.axis_index("c")
        me = 2 * x + y
        stage = (s_win, None, s_wq, s_wkv, meta_ref)
        outs = (win_o, None, wq_o, wkv_o, meta_all)

        _peer_signal(x, y, c)

        frames = pl.ds(HEAD_ROWS, S)
        loads = [pltpu.make_async_copy(x_ref, x_buf, loc_sems.at[0]), pltpu.make_async_copy(t_ref, t_buf, loc_sems.at[1])]
        local = [pltpu.make_async_copy(x_buf, h_o.at[frames, :], loc_sems.at[0]),
                 pltpu.make_async_copy(t_buf, tp_o.at[frames, :], loc_sems.at[1])]
        for cp in loads:
            cp.start()

        s_win[...] = win_ref[...].astype(BF16)
        s_wq[0:QK, :] = wq_ref[...].astype(BF16)
        s_wq[QK:256, :] = jnp.zeros((256 - QK, Q_LORA), BF16)
        s_wkv[...] = wkv_ref[...].astype(BF16)
        head_buf[...] = jnp.zeros_like(head_buf)
        zeros = pltpu.make_async_copy(head_buf, tp_o.at[pl.ds(0, HEAD_ROWS), :], loc_sems.at[2])
        zeros.start()

        def chip_of(rel):
            fx, fy = _CHIP_RELS[rel]
            return 2 * (x ^ fx) + (y ^ fy)

        def same_core_of(rel):
            fx, fy = _CHIP_RELS[rel]
            return (x ^ fx, y ^ fy, c)

        def ici_copy(rel, i, src_chip, to):
            arr, r0, n, _ = _PIECES[i]
            k = (rel - 1) * _NP + i
            return _remote(stage[arr].at[pl.ds(r0, n), :], _gathered_at(outs, arr, src_chip, r0, n),
                           ici_send.at[k], ici_recv.at[k], to)

        def fwd_copy(rel, i, to):
            arr, r0, n, _ = _PIECES[i]
            k = (rel - 1) * _NP + i
            place = _gathered_at(outs, arr, chip_of(rel), r0, n)
            return _remote(place, place, fwd_send.at[k], fwd_recv.at[k], to)

        _peer_wait()
        for core in (0, 1):
            @pl.when(c == core)
            def _(core=core):
                mine = [i for i in range(_NP) if _PIECES[i][3] == core and _PIECES[i][0] in arrays]
                theirs = [i for i in range(_NP) if _PIECES[i][3] != core and _PIECES[i][0] in arrays]
                sends = [ici_copy(rel, i, me, same_core_of(rel)) for rel in SEND_ORDER for i in mine]
                for cp in sends:
                    cp.start()
                for ld, st in zip(loads, local):
                    ld.wait()
                    st.start()
                own = [pltpu.make_async_copy(stage[arr], _gathered_at(outs, arr, me, 0, _ARR_ROWS[arr]), own_sems.at[arr])
                       for arr in arrays if arr != 4]
                for cp in own:
                    cp.start()
                meta_all[me] = meta_ref[...]
                for rel in (1, 2, 3):
                    for i in mine:
                        ici_copy(rel, i, chip_of(rel), (x, y, c)).wait_recv()
                        fwd = fwd_copy(rel, i, (x, y, 1 - c))
                        fwd.start()
                        sends.append(fwd)
                for rel in (1, 2, 3):
                    for i in theirs:
                        fwd_copy(rel, i, (x, y, c)).wait_recv()
                for cp in sends:
                    cp.wait_send()
                for cp in own:
                    cp.wait()

        zeros.wait()
        for chip in range(CHIPS):
            head_buf[PAD:HEAD_ROWS, chip * 256:(chip + 1) * 256] = meta_all[chip]
        head = pltpu.make_async_copy(head_buf, h_o.at[pl.ds(0, HEAD_ROWS), :], loc_sems.at[2])
        head.start()
        head.wait()
        for cp in local:
            cp.wait()

    vm = pl.BlockSpec(memory_space=pltpu.VMEM)
    hbm = pl.BlockSpec(memory_space=pl.ANY)
    return pl.pallas_call(
        body,
        name="gather_weights",
        in_specs=[vm] * 4 + [hbm] * 2,
        out_specs=[hbm] * 5,
        out_shape=[
            jax.ShapeDtypeStruct((D_IN, D), BF16),
            jax.ShapeDtypeStruct((CHIPS, 256, Q_LORA), BF16), jax.ShapeDtypeStruct((CHIPS, KV_LORA, 256), BF16),
            jax.ShapeDtypeStruct((N, D), F32), jax.ShapeDtypeStruct((N, D), F32),
        ],
        scratch_shapes=[pltpu.VMEM((_ARR_ROWS[a], _ARR_COLS[a]), BF16) for a in (0, 2, 3)]
        + [pltpu.VMEM((CHIPS, N_META, 256), F32), pltpu.VMEM((HEAD_ROWS, D), F32), pltpu.VMEM((S, D), F32),
           pltpu.VMEM((S, D), F32)]
        + [pltpu.SemaphoreType.DMA((3 * _NP,))] * 4 + [pltpu.SemaphoreType.DMA((3,)), pltpu.SemaphoreType.DMA((4,))],
        compiler_params=_cparams(collective_id=0),
    )(winT_s, wqT_s, wkv_s, meta_s, x2, tgt2)


_SM_ROWS = (len(POOL_WINDOWS) * POOL_GROUP, VEC_ROWS)
_SM_COLS = (POOL_GROUP, D)
_SM_PIECES = ((0, 0, 256, 0), (0, 256, 256, 1), (1, 0, VEC_ROWS, 0))
_NSP = len(_SM_PIECES)


def _reduce_grads(dsl, hn, dwq, dwkv, dmeta4, dpw, dg, dgf, dgq, dgkv, dps, loss):
    arrays = (0, 2, 3, 4)
    loaded = (2, 3, 4)
    shard_order = SEND_ORDER + (0,)

    def body(dsl_hbm, hn_hbm, dwq_ref, dwkv_ref, dmeta_ref, dpw_ref, dg_ref, dgf_ref, dgq_ref, dgkv_ref, dps_ref,
             loss_ref, gwin_o, gwq_o, gwkv_o, gmeta_o, gpw_o, gg_o, ggf_o, ggq_o, ggkv_o, gps_o, gloss_o,
             ow2, ow3, ow4, sb0, sb2, sb3, sb4, st0, st2, st3, st4, rc0, rc2, rc3, rc4,
             vec, sm_sb0, sm_sb1, sm_cs0, sm_cs1, sm_rc0, sm_rc1, vec_fin, slab_v, hn_v, dwin_buf, own0,
             own_sems, d2d_send, d2d_recv, ici_send, ici_recv, fin_send, fin_recv,
             swap_send, swap_recv, smi_send, smi_recv, smf_send, smf_recv, ld_sems):
        x, y, c = lax.axis_index("x"), lax.axis_index("y"), lax.axis_index("c")
        me = 2 * x + y
        _peer_signal(x, y, c)
        grads = (None, None, dwq_ref, dwkv_ref, dmeta_ref)
        outs = (gwin_o, None, gwq_o, gwkv_o, gmeta_o)
        own_buf = (None, None, ow2, ow3, ow4)
        sib_buf = (sb0, None, sb2, sb3, sb4)
        stage = (st0, None, st2, st3, st4)
        recv = (rc0, None, rc2, rc3, rc4)
        sm_mine = (dpw_ref, vec)
        sm_sib = (sm_sb0, sm_sb1)
        sm_chip = (sm_cs0, sm_cs1)
        sm_recv = (sm_rc0, sm_rc1)
        sm_out = (gpw_o, vec_fin)
        sibling = (x, y, 1 - c)

        def chip_of(rel):
            fx, fy = _CHIP_RELS[rel]
            return 2 * (x ^ fx) + (y ^ fy)

        def same_core_of(rel):
            fx, fy = _CHIP_RELS[rel]
            return (x ^ fx, y ^ fy, c)

        hn_load = pltpu.make_async_copy(hn_hbm, hn_v, ld_sems.at[CHIPS])

        def slab_load(rel):
            return pltpu.make_async_copy(dsl_hbm.at[chip_of(rel)], slab_v.at[rel], ld_sems.at[rel])

        hn_load.start()
        slab_load(shard_order[0]).start()

        def slot(bufs, i, idx):
            arr, _, n, _ = _PIECES[i]
            return bufs[arr].at[idx, pl.ds(0, n), :]

        def own_load(rel, i):
            arr, r0, n, _ = _PIECES[i]
            return pltpu.make_async_copy(_gathered_at(grads, arr, chip_of(rel), r0, n), slot(own_buf, i, rel),
                                         own_sems.at[rel * _NP + i])

        def d2d_copy(rel, i):
            arr, r0, n, _ = _PIECES[i]
            k = rel * _NP + i
            return _remote(_gathered_at(grads, arr, chip_of(rel), r0, n), slot(sib_buf, i, rel),
                           d2d_send.at[k], d2d_recv.at[k], sibling)

        def ici_copy(rel, i):
            k = (rel - 1) * _NP + i
            return _remote(slot(stage, i, rel - 1), slot(recv, i, rel - 1), ici_send.at[k], ici_recv.at[k],
                           same_core_of(rel))

        def fin_copy(i):
            arr, r0, n, _ = _PIECES[i]
            place = outs[arr].at[pl.ds(r0, n), :]
            return _remote(place, place, fin_send.at[i], fin_recv.at[i], sibling)

        def sm_ici_copy(rel, j):
            blk, r0, n, _ = _SM_PIECES[j]
            k = (rel - 1) * _NSP + j
            return _remote(sm_chip[blk].at[pl.ds(r0, n), :], sm_recv[blk].at[rel - 1, pl.ds(r0, n), :],
                           smi_send.at[k], smi_recv.at[k], same_core_of(rel))

        def sm_fin_copy(j):
            blk, r0, n, _ = _SM_PIECES[j]
            place = sm_out[blk].at[pl.ds(r0, n), :]
            return _remote(place, place, smf_send.at[j], smf_recv.at[j], sibling)

        vec[...] = jnp.zeros_like(vec)
        vec[0:1, :] = dg_ref[...]
        vec[1:2, :] = dgf_ref[...]
        vec[2:3, V_GQ:V_GQ + Q_LORA] = dgq_ref[...]
        vec[2:3, V_GKV:V_GKV + KV_LORA] = dgkv_ref[...]
        vec[2:3, V_PS:V_PS + D_POOL] = dps_ref[...]
        vec[2:3, V_LOSS:D] = loss_ref[...]
        _peer_wait()
        swaps = [_remote(sm_mine[b], sm_sib[b], swap_send.at[b], swap_recv.at[b], sibling) for b in (0, 1)]
        for cp in swaps:
            cp.start()

        for core in (0, 1):
            @pl.when(c == core)
            def _(core=core):
                mine = [i for i in range(_NP) if _PIECES[i][3] == core and _PIECES[i][0] in loaded]
                theirs = [i for i in range(_NP) if _PIECES[i][3] != core and _PIECES[i][0] in loaded]
                i0 = next(i for i in range(_NP) if _PIECES[i][0] == 0 and _PIECES[i][3] == core)
                j0 = next(i for i in range(_NP) if _PIECES[i][0] == 0 and _PIECES[i][3] != core)
                sm_mine_p = [j for j in range(_NSP) if _SM_PIECES[j][3] == core]
                sm_theirs_p = [j for j in range(_NSP) if _SM_PIECES[j][3] != core]
                sends = list(swaps)

                for rel in shard_order:
                    for i in theirs:
                        cp = d2d_copy(rel, i)
                        cp.start()
                        sends.append(cp)
                    for i in mine:
                        own_load(rel, i).start()

                def piece_rows(i):
                    return pl.ds(_PIECES[i][1], _PIECES[i][2])

                def form(rel, i):
                    r0, n = _PIECES[i][1], _PIECES[i][2]
                    dwin_buf[rel, r0:r0 + n, :] = _tn(slab_v[rel, :, r0:r0 + _PIECE_MAX[0]], hn_v[...])[0:n, :]

                def d2d0(rel, i):
                    return _remote(dwin_buf.at[rel, piece_rows(i), :], slot(sib_buf, i, rel),
                                   d2d_send.at[rel * _NP + i], d2d_recv.at[rel * _NP + i], sibling)

                def settle(rel):
                    d2d0(rel, i0).wait_recv()
                    total = dwin_buf[rel, piece_rows(i0), :] + slot(sib_buf, i0, rel)[...]
                    if rel == 0:
                        own0[0:_PIECES[i0][2], :] = total
                    else:
                        slot(stage, i0, rel - 1)[...] = total.astype(BF16)
                        cp = ici_copy(rel, i0)
                        cp.start()
                        sends.append(cp)

                hn_load.wait()
                for n, rel in enumerate(shard_order):
                    slab_load(rel).wait()
                    if n == 0:
                        for later in shard_order[1:]:
                            slab_load(later).start()
                    form(rel, j0)
                    cp = d2d0(rel, j0)
                    cp.start()
                    sends.append(cp)
                    if n > 0:
                        settle(shard_order[n - 1])
                    form(rel, i0)
                settle(shard_order[-1])

                for rel in SEND_ORDER:
                    for i in mine:
                        arr, r0, n, _ = _PIECES[i]
                        own_load(rel, i).wait()
                        d2d_copy(rel, i).wait_recv()
                        total = slot(own_buf, i, rel)[...] + slot(sib_buf, i, rel)[...]
                        slot(stage, i, rel - 1)[...] = total.astype(stage[arr].dtype)
                        cp = ici_copy(rel, i)
                        cp.start()
                        sends.append(cp)

                for b in (0, 1):
                    swaps[b].wait_recv()
                    sm_chip[b][...] = sm_mine[b][...] + sm_sib[b][...]
                for rel in SEND_ORDER:
                    for j in sm_mine_p:
                        cp = sm_ici_copy(rel, j)
                        cp.start()
                        sends.append(cp)

                for i in mine:
                    arr, r0, n, _ = _PIECES[i]
                    own_load(0, i).wait()
                    d2d_copy(0, i).wait_recv()
                    total = slot(own_buf, i, 0)[...] + slot(sib_buf, i, 0)[...]
                    for rel in (1, 2, 3):
                        ici_copy(rel, i).wait_recv()
                        total = total + slot(recv, i, rel - 1)[...].astype(F32)
                    outs[arr][pl.ds(r0, n), :] = total
                    cp = fin_copy(i)
                    cp.start()
                    sends.append(cp)
                total = own0[0:_PIECES[i0][2], :]
                for rel in (1, 2, 3):
                    ici_copy(rel, i0).wait_recv()
                    total = total + slot(recv, i0, rel - 1)[...].astype(F32)
                outs[0][pl.ds(_PIECES[i0][1], _PIECES[i0][2]), :] = total
                cp = fin_copy(i0)
                cp.start()
                sends.append(cp)

                for j in sm_mine_p:
                    blk, r0, n, _ = _SM_PIECES[j]
                    for rel in (1, 2, 3):
                        sm_ici_copy(rel, j).wait_recv()
                    total = jnp.zeros((n, _SM_COLS[blk]), F32)
                    for chip in range(CHIPS):
                        flips = chip ^ me
                        rel = jnp.where(flips == 2, 1, jnp.where(flips == 1, 2, flips))
                        theirs_rows = sm_recv[blk][jnp.maximum(rel - 1, 0), pl.ds(r0, n), :]
                        total = total + jnp.where(rel == 0, sm_chip[blk][pl.ds(r0, n), :], theirs_rows)
                    sm_out[blk][pl.ds(r0, n), :] = total
                    cp = sm_fin_copy(j)
                    cp.start()
                    sends.append(cp)

                for i in theirs + [j0]:
                    fin_copy(i).wait_recv()
                for j in sm_theirs_p:
                    sm_fin_copy(j).wait_recv()
                for cp in sends:
                    cp.wait_send()

        gg_o[...] = vec_fin[0:1, :]
        ggf_o[...] = vec_fin[1:2, :]
        ggq_o[...] = vec_fin[2:3, V_GQ:V_GQ + Q_LORA]
        ggkv_o[...] = vec_fin[2:3, V_GKV:V_GKV + KV_LORA]
        gps_o[...] = vec_fin[2:3, V_PS:V_PS + D_POOL]
        gloss_o[...] = vec_fin[2:3, V_LOSS:D]

    vm = pl.BlockSpec(memory_space=pltpu.VMEM)
    piece_buf = lambda lead, dtype, which=arrays: [
        pltpu.VMEM((lead, _PIECE_MAX[a], _ARR_COLS[a]), F32 if a == 4 else dtype) for a in which]
    sm_buf = lambda *lead: [pltpu.VMEM(lead + (_SM_ROWS[b], _SM_COLS[b]), F32) for b in (0, 1)]
    dma = lambda n: [pltpu.SemaphoreType.DMA((n,))] * 2
    return pl.pallas_call(
        body,
        name="reduce_grads",
        in_specs=[pl.BlockSpec(memory_space=pl.ANY)] * 4 + [vm] * 8,
        out_specs=[vm] * 11,
        out_shape=[jax.ShapeDtypeStruct((_ARR_ROWS[a], _ARR_COLS[a]), F32) for a in arrays]
        + [jax.ShapeDtypeStruct((_SM_ROWS[0], _SM_COLS[0]), F32), jax.ShapeDtypeStruct((1, D), F32),
           jax.ShapeDtypeStruct((1, D), F32), jax.ShapeDtypeStruct((1, Q_LORA), F32),
           jax.ShapeDtypeStruct((1, KV_LORA), F32), jax.ShapeDtypeStruct((1, D_POOL), F32),
           jax.ShapeDtypeStruct((1, 128), F32)],
        scratch_shapes=piece_buf(CHIPS, F32, loaded) + piece_buf(CHIPS, F32) + piece_buf(3, BF16) + piece_buf(3, BF16)
        + [pltpu.VMEM((VEC_ROWS, D), F32)] + sm_buf() + sm_buf() + sm_buf(3) + [pltpu.VMEM((VEC_ROWS, D), F32)]
        + [pltpu.VMEM((CHIPS, N, SHARD_PAD), BF16), pltpu.VMEM((N, D), BF16),
           pltpu.VMEM((CHIPS, SHARD_PAD, D), F32), pltpu.VMEM((_PIECE_MAX[0], D), F32)]
        + [pltpu.SemaphoreType.DMA((CHIPS * _NP,))]
        + dma(CHIPS * _NP) + dma(3 * _NP) + dma(_NP) + dma(2) + dma(3 * _NSP) + dma(_NSP)
        + [pltpu.SemaphoreType.DMA((CHIPS + 1,))],
        compiler_params=_cparams(collective_id=3),
    )(dsl, hn, dwq, dwkv, dmeta4, dpw, dg, dgf, dgq, dgkv, dps, loss)


def _adamw_math(w, g, m, v):
    m = B1 * m + (1.0 - B1) * g
    v = B2 * v + (1.0 - B2) * (g * g)
    m_hat = m / C1
    v_hat = v / C2
    delta = -LR * (m_hat / (jnp.sqrt(v_hat) + ADAM_EPS) + WD * w)
    return delta, m, v


def _adamw(big, block_rows, groups):
    rows, cols = big[0].shape
    n = len(groups)

    def body(*refs):
        w_ref, g_ref, m_ref, v_ref = refs[0:4]
        small_in = refs[4:4 + 4 * n]
        go_ref, d_ref, nm_ref, nv_ref = refs[4 + 4 * n:8 + 4 * n]
        small_out = refs[8 + 4 * n:]
        g = g_ref[...]
        go_ref[...] = g
        d_ref[...], nm_ref[...], nv_ref[...] = _adamw_math(w_ref[...], g, m_ref[...], v_ref[...])

        @pl.when(pl.program_id(0) == 0)
        def _():
            for t in range(n):
                sw_ref, sg_ref, sm_ref, sv_ref = small_in[4 * t:4 * t + 4]
                sg = sg_ref[0:sw_ref.shape[0], :]
                small_out[4 * t][...] = sg
                small_out[4 * t + 1][...], small_out[4 * t + 2][...], small_out[4 * t + 3][...] = _adamw_math(
                    sw_ref[...], sg, sm_ref[...], sv_ref[...])

    spec = pl.BlockSpec((block_rows, cols), lambda i: (i, 0))
    vm = pl.BlockSpec(memory_space=pltpu.VMEM)
    outs = pl.pallas_call(
        body,
        name="adamw",
        grid=(rows // block_rows,),
        in_specs=[spec] * 4 + [vm] * (4 * n),
        out_specs=[spec] * 4 + [vm] * (4 * n),
        out_shape=[jax.ShapeDtypeStruct(big[0].shape, F32)] * 4
        + [jax.ShapeDtypeStruct(grp[0].shape, F32) for grp in groups for _ in range(4)],
        compiler_params=_cparams(dimension_semantics=("arbitrary",)),
    )(*big, *[a for grp in groups for a in grp])
    return tuple(outs[0:4]), [tuple(outs[4 + 4 * t:8 + 4 * t]) for t in range(n)]


def _rope_tables():
    half = QK_ROPE // 2
    f32 = np.float32
    inv_freq = (f32(1.0) / (f32(ROPE_THETA) ** (np.arange(half, dtype=f32) / f32(half)))).astype(f32)
    pos = np.arange(N, dtype=f32) - f32(PAD)
    ang = (pos[:, None] * inv_freq[None, :]).astype(f32)
    cos, sin = np.cos(ang).astype(f32), np.sin(ang).astype(f32)
    zero = np.zeros((N, 128 - QK_ROPE), f32)
    return jnp.asarray(np.concatenate([cos, cos, zero], axis=1)), jnp.asarray(np.concatenate([-sin, sin, zero], axis=1))


def kernel(x, meta_tokens, norm_g, w_in, q_norm_g, w_q_b, kv_norm_g, w_kv_b, pool_w, pool_scale, w_out, final_norm_g, loss_target, m_meta_tokens, m_norm_g, m_w_in, m_q_norm_g, m_w_q_b, m_kv_norm_g, m_w_kv_b, m_pool_w, m_pool_scale, m_w_out, m_final_norm_g, v_meta_tokens, v_norm_g, v_w_in, v_q_norm_g, v_w_q_b, v_kv_norm_g, v_w_kv_b, v_pool_w, v_pool_scale, v_w_out, v_final_norm_g):
    tr = lambda a: a[0].T
    win, wq, wkv, h, tgt = _gather_weights(tr(w_in), tr(w_q_b), w_kv_b[0], meta_tokens, x[0], loss_target[0])
    cosf, sinf = _rope_tables()
    gf = final_norm_g.reshape(1, D)

    part = _local_step(h, tgt, norm_g, win, q_norm_g, wq, kv_norm_g, wkv, pool_w[0], pool_scale, w_out[0], m_w_out[0],
                       v_w_out[0], gf, cosf, sinf)

    pw2 = lambda a: a.reshape(len(POOL_WINDOWS) * POOL_GROUP, POOL_GROUP)
    gwinT, gwqT, gwkv, gmeta, gpw, gg, ggf, ggq, ggkv, gps, gloss = _reduce_grads(
        part["dsl"], part["hn"], part["dwq"], part["dwkv"], part["dmeta"], pw2(part["dpw"]), part["dg"],
        part["dgf"], part["dgq"], part["dgkv"], part["dps"], part["loss"])

    r_out = part["r_out"]
    fn2 = lambda a: a.reshape(1, D)
    r_in, (r_meta, r_norm, r_gq, r_wq, r_gkv, r_wkv, r_pw, r_ps, r_fn) = _adamw((tr(w_in), gwinT, tr(m_w_in), tr(v_w_in)), 248, [
        (meta_tokens, gmeta, m_meta_tokens, v_meta_tokens),
        (norm_g, gg, m_norm_g, v_norm_g),
        (q_norm_g, ggq, m_q_norm_g, v_q_norm_g),
        (tr(w_q_b), gwqT, tr(m_w_q_b), tr(v_w_q_b)),
        (kv_norm_g, ggkv, m_kv_norm_g, v_kv_norm_g),
        (w_kv_b[0], gwkv, m_w_kv_b[0], v_w_kv_b[0]),
        (pw2(pool_w), gpw, pw2(m_pool_w), pw2(v_pool_w)),
        (pool_scale, gps, m_pool_scale, v_pool_scale),
        (fn2(final_norm_g), ggf, fn2(m_final_norm_g), fn2(v_final_norm_g)),
    ])
    untr = lambda a: a.T[None]
    pw4 = lambda a: a.reshape(1, len(POOL_WINDOWS), POOL_GROUP, POOL_GROUP)
    per_kind = [[
        r_meta[kind], r_norm[kind], untr(r_in[kind]), r_gq[kind], untr(r_wq[kind]), r_gkv[kind], r_wkv[kind][None],
        pw4(r_pw[kind]), r_ps[kind], r_out[kind][None], r_fn[kind].reshape(D),
    ] for kind in range(4)]
    return (gloss[0, 0], part["gx"][None], *per_kind[0], *per_kind[1], *per_kind[2], *per_kind[3])
```

```python
import jax
import jax.numpy as jnp
import numpy as np
from jax import lax
from jax.experimental import pallas as pl
from jax.experimental.pallas import tpu as pltpu

F32 = jnp.float32
BF16 = jnp.bfloat16

D = 1024
S = 2048
N_META = 16
PAD = 112
HEAD_ROWS = PAD + N_META
N = HEAD_ROWS + S
D_POOL = 512
POOL_WINDOWS = (2, 4, 8, 16)
POOL_GROUP = 128
HALO = 16
HEADS = 4
QK_NOPE = 128
QK_ROPE = 64
QK = QK_NOPE + QK_ROPE
V_HEAD = 128
Q_LORA = 256
KV_LORA = 128
D_IN = 1984
EPS = 1e-6
ROPE_THETA = 10000.0
SCALE = QK ** -0.5
CHIPS = 4

ROWS_FWD = 544
ROWS_MID = 544
ROWS_BWD = 544
TK = 128
TQ = 256
NQ = S // TQ
HEADS_PER_STEP_BWD = 2

O_PI, O_PG, O_CQ, O_CKV, O_KR, O_AG = 0, 512, 1024, 1280, 1408, 1472
O_KR_END = O_KR + 128
SHARD_IN = D_IN // CHIPS
SHARD_PAD = 512
SHARD_OUT = D // CHIPS

LR, B1, B2, ADAM_EPS, WD, STEP = 0.001, 0.9, 0.999, 1e-08, 0.01, 10
C1 = 1.0 - B1**STEP
C2 = 1.0 - B2**STEP

VMEM_LIMIT = 60 * 1024 * 1024
MESH = pl.DeviceIdType.MESH
NEG = -1e30

VEC_ROWS = 8
V_GQ, V_GKV, V_PS, V_LOSS = 0, 256, 384, 896


def _cparams(**kw):
    return pltpu.CompilerParams(vmem_limit_bytes=VMEM_LIMIT, **kw)


def _nt(a, b):
    return lax.dot_general(a, b, (((1,), (1,)), ((), ())), preferred_element_type=F32)


def _tn(a, b):
    return lax.dot_general(a, b, (((0,), (0,)), ((), ())), preferred_element_type=F32)


def _nn(a, b):
    return jnp.dot(a, b, preferred_element_type=F32)


def _swap64(t):
    return pltpu.roll(t, 32, 1) + pltpu.roll(t, 96, 1)


def _sigmoid(x):
    return 1.0 / (1.0 + jnp.exp(-x))


def _low_lanes():
    return (lax.broadcasted_iota(jnp.int32, (1, 128), 1) < QK_ROPE).astype(F32)


def _rows(w, rows):
    return pl.BlockSpec((rows, w), lambda i: (i, 0))


def _const(*shape):
    return pl.BlockSpec(shape, lambda *_: (0,) * len(shape), pipeline_mode=pl.Buffered(1))


STAT_GROUPS = HEADS // HEADS_PER_STEP_BWD


def _stat_slot(head):
    return head // HEADS_PER_STEP_BWD, head % HEADS_PER_STEP_BWD


N_PEERS = 4
SEND_ORDER = (3, 1, 2)


def _peer_signal(x, y, c):
    barrier = pltpu.get_barrier_semaphore()
    peers = [(x, y, 1 - c)] + [(x ^ fx, y ^ fy, c) for fx, fy in _CHIP_RELS[1:]]
    assert len(peers) == N_PEERS
    for peer in peers:
        pl.semaphore_signal(barrier, inc=1, device_id=peer, device_id_type=MESH)


def _peer_wait():
    pl.semaphore_wait(pltpu.get_barrier_semaphore(), N_PEERS)


def _attn_tiles():
    return [(0, TK, TK)] + [(TK + TQ * t, TQ, TK + TQ * (t + 1)) for t in range(NQ)]


def _masked_scores(q, k, rows, klen):
    s = _nt(q, k)
    col = lax.broadcasted_iota(jnp.int32, (1, TK), 1)
    head_bias = jnp.where(col >= PAD, 0.0, NEG)
    if klen == TK:
        return s + head_bias
    r = lax.broadcasted_iota(jnp.int32, (rows, 1), 0) >> 6
    c = lax.broadcasted_iota(jnp.int32, (1, rows), 1) >> 6
    diag_bias = jnp.where(c <= r, 0.0, NEG)
    parts = [s[:, 0:TK] + head_bias]
    if klen - rows > TK:
        parts.append(s[:, TK:klen - rows])
    parts.append(s[:, klen - rows:klen] + diag_bias)
    return jnp.concatenate(parts, axis=1)


def _fwd_in(h, norm_g, win, gq, wq, gkv, wkv, cosf, sinf):
    tr = ROWS_FWD

    def body(h_ref, g_ref, win_ref, gq_ref, wq_ref, gkv_ref, wkv_ref, cos_ref, sin_ref,
             pi_ref, pg_ref, cq_ref, ckv_ref, ag_ref, q_ref, k_ref, v_ref, hn_ref):
        h = h_ref[...]
        r = lax.rsqrt(jnp.mean(h * h, axis=-1, keepdims=True) + EPS)
        hn = ((h * r) * g_ref[...]).astype(BF16)
        hn_ref[...] = hn
        u = _nt(hn, win_ref[0:O_KR_END, :])
        pi_ref[...] = u[:, O_PI:O_PG]
        pg_ref[...] = u[:, O_PG:O_CQ]
        cq = u[:, O_CQ:O_CKV]
        ckv = u[:, O_CKV:O_KR]
        cq_ref[...] = cq
        ckv_ref[...] = ckv
        ag_ref[...] = _nt(hn, win_ref[O_AG:D_IN, :])
        cosv = cos_ref[...]
        sinv = sin_ref[...]
        kr = u[:, O_KR:O_KR_END] * _low_lanes()
        kr = (kr * cosv + _swap64(kr) * sinv).astype(BF16)
        rq = lax.rsqrt(jnp.mean(cq * cq, axis=-1, keepdims=True) + EPS)
        cqn = ((cq * rq) * gq_ref[...]).astype(BF16)
        rkv = lax.rsqrt(jnp.mean(ckv * ckv, axis=-1, keepdims=True) + EPS)
        ckvn = ((ckv * rkv) * gkv_ref[...]).astype(BF16)
        for hd in range(HEADS):
            qh = _nt(cqn, wq_ref[hd]) * SCALE
            z = qh[:, QK_NOPE:]
            q_ref[hd, :, 0:QK_NOPE] = qh[:, 0:QK_NOPE].astype(BF16)
            q_ref[hd, :, QK_NOPE:] = (z * cosv + _swap64(z) * sinv).astype(BF16)
            kvh = _nn(ckvn, wkv_ref[hd])
            k_ref[hd, :, 0:QK_NOPE] = kvh[:, 0:QK_NOPE].astype(BF16)
            k_ref[hd, :, QK_NOPE:] = kr
            v_ref[hd] = kvh[:, QK_NOPE:].astype(BF16)

    head = lambda w: pl.BlockSpec((HEADS, tr, w), lambda i: (0, i, 0))
    return pl.pallas_call(
        body,
        name="fwd_in",
        grid=(N // tr,),
        in_specs=[
            _rows(D, tr), _const(1, D), _const(D_IN, D), _const(1, Q_LORA), _const(HEADS, 256, Q_LORA),
            _const(1, KV_LORA), _const(HEADS, KV_LORA, 256), _rows(128, tr), _rows(128, tr),
        ],
        out_specs=[_rows(D_POOL, tr), _rows(D_POOL, tr), _rows(Q_LORA, tr), _rows(KV_LORA, tr), _rows(D_POOL, tr),
                   head(256), head(256), head(V_HEAD), _rows(D, tr)],
        out_shape=[
            jax.ShapeDtypeStruct((N, D_POOL), F32), jax.ShapeDtypeStruct((N, D_POOL), F32),
            jax.ShapeDtypeStruct((N, Q_LORA), F32), jax.ShapeDtypeStruct((N, KV_LORA), F32),
            jax.ShapeDtypeStruct((N, D_POOL), F32),
            jax.ShapeDtypeStruct((HEADS, N, 256), BF16), jax.ShapeDtypeStruct((HEADS, N, 256), BF16),
            jax.ShapeDtypeStruct((HEADS, N, V_HEAD), BF16), jax.ShapeDtypeStruct((N, D), BF16),
        ],
        compiler_params=_cparams(dimension_semantics=("arbitrary",)),
    )(h, norm_g, win, gq, wq, gkv, wkv, cosf, sinf)


def _attn_fwd(q, k, v, wout_s):
    tiles = _attn_tiles()
    n_t = len(tiles)
    half = SHARD_OUT // 2
    send_step = 2
    fwd_step = n_t - 2

    def body(q_hbm, k_hbm, v_hbm, wout_ref, o_hbm, lse_ref, wout_o, q_buf, k_buf, v_buf, o_buf, s_wout, in_sems, out_sems,
             ici_send, ici_recv, fwd_send, fwd_recv, own_sem):
        step = pl.program_id(0)
        x, y, c = lax.axis_index("x"), lax.axis_index("y"), lax.axis_index("c")
        me = 2 * x + y

        def chip_of(rel):
            fx, fy = _CHIP_RELS[rel]
            return 2 * (x ^ fx) + (y ^ fy)

        def place(chip, core):
            return wout_o.at[pl.ds(pl.multiple_of(SHARD_OUT * chip + half * core, half), half), :]

        def ici_copy(rel, src_chip, to):
            return _remote(s_wout.at[pl.ds(pl.multiple_of(half * c, half), half), :], place(src_chip, c),
                           ici_send.at[rel - 1], ici_recv.at[rel - 1], to)

        def fwd_copy(rel, core, to):
            spot = place(chip_of(rel), core)
            return _remote(spot, spot, fwd_send.at[rel - 1], fwd_recv.at[rel - 1], to)

        own = pltpu.make_async_copy(s_wout, wout_o.at[pl.ds(pl.multiple_of(SHARD_OUT * me, SHARD_OUT), SHARD_OUT), :], own_sem)

        @pl.when(step == 0)
        def _():
            _peer_signal(x, y, c)
            s_wout[...] = wout_ref[...].astype(BF16)
            own.start()

        @pl.when(step == send_step)
        def _():
            _peer_wait()
            for rel in SEND_ORDER:
                fx, fy = _CHIP_RELS[rel]
                ici_copy(rel, me, (x ^ fx, y ^ fy, c)).start()

        @pl.when(step == fwd_step)
        def _():
            for rel in (1, 2, 3):
                ici_copy(rel, chip_of(rel), (x, y, c)).wait_recv()
                fwd_copy(rel, c, (x, y, 1 - c)).start()

        def finish_wout():
            for rel in (1, 2, 3):
                fwd_copy(rel, 1 - c, (x, y, c)).wait_recv()
            for rel in (1, 2, 3):
                ici_copy(rel, me, (x, y, c)).wait_send()
                fwd_copy(rel, c, (x, y, c)).wait_send()
            own.wait()

        def loads(idx):
            q0, rows, _ = tiles[idx]
            rs = pl.ds(q0, rows)
            return [pltpu.make_async_copy(src.at[:, rs, :], dst.at[:, rs, :], in_sems.at[a, idx % 2])
                    for a, (src, dst) in enumerate(((q_hbm, q_buf), (k_hbm, k_buf), (v_hbm, v_buf)))]

        def store(idx):
            q0, rows, _ = tiles[idx]
            return pltpu.make_async_copy(o_buf.at[idx % 2, pl.ds(0, rows), :], o_hbm.at[pl.ds(q0, rows), :],
                                         out_sems.at[idx % 2])

        @pl.when(step == 0)
        def _():
            lse_ref[...] = jnp.zeros_like(lse_ref)
            for cp in loads(0):
                cp.start()

        for idx, (q0, rows, klen) in enumerate(tiles):
            @pl.when(step == idx)
            def _(idx=idx, q0=q0, rows=rows, klen=klen):
                for cp in loads(idx):
                    cp.wait()
                if idx + 1 < n_t:
                    for cp in loads(idx + 1):
                        cp.start()
                if idx >= 2:
                    store(idx - 2).wait()
                for hd in range(HEADS):
                    s = _masked_scores(q_buf[hd, q0:q0 + rows, :], k_buf[hd, 0:klen, :], rows, klen)
                    m = jnp.max(s, axis=-1, keepdims=True)
                    p = jnp.exp(s - m)
                    l = jnp.sum(p, axis=-1, keepdims=True)
                    o_buf[idx % 2, 0:rows, hd * V_HEAD:(hd + 1) * V_HEAD] = _nn(p.astype(BF16), v_buf[hd, 0:klen, :]) / l
                    grp, lane = _stat_slot(hd)
                    lse_ref[grp, q0:q0 + rows, lane:lane + 1] = m + jnp.log(l)
                store(idx).start()
                if idx == n_t - 1:
                    store(idx - 1).wait()
                    store(idx).wait()
                    finish_wout()

    hbm = pl.BlockSpec(memory_space=pl.ANY)
    return pl.pallas_call(
        body,
        name="attn_fwd",
        grid=(n_t,),
        in_specs=[hbm, hbm, hbm, _const(SHARD_OUT, D)],
        out_specs=[hbm, _const(STAT_GROUPS, N, 128), hbm],
        out_shape=[jax.ShapeDtypeStruct((N, HEADS * V_HEAD), F32), jax.ShapeDtypeStruct((STAT_GROUPS, N, 128), F32),
                   jax.ShapeDtypeStruct((D, D), BF16)],
        scratch_shapes=[pltpu.VMEM((HEADS, N, 256), BF16), pltpu.VMEM((HEADS, N, 256), BF16),
                        pltpu.VMEM((HEADS, N, V_HEAD), BF16), pltpu.VMEM((2, TQ, HEADS * V_HEAD), F32),
                        pltpu.VMEM((SHARD_OUT, D), BF16),
                        pltpu.SemaphoreType.DMA((3, 2)), pltpu.SemaphoreType.DMA((2,))]
        + [pltpu.SemaphoreType.DMA((3,))] * 4 + [pltpu.SemaphoreType.DMA],
        compiler_params=_cparams(dimension_semantics=("arbitrary",), collective_id=1),
    )(q, k, v, wout_s)


def _inv_count(row0, rows, w):
    row = row0 + lax.broadcasted_iota(jnp.int32, (rows, 1), 0)
    return 1.0 / jnp.clip(row - (PAD - 1), 1, w).astype(F32)


def _mid(h, tgt, pool_in, pool_gate, attn_gate, attn, pool_w, pool_scale, wout, gf):
    tr = ROWS_MID
    per = tr // HALO
    ng = len(POOL_WINDOWS)

    def body(h_ref, t_ref, pin_ref, halo_ref, pg_ref, ag_ref, at_ref, pw_ref, ps_ref, wout_ref, gf_ref,
             dh2_ref, do_ref, delta_ref, dag_ref, dpg_ref, dpl_ref, dwout_ref, dpw_ref, dps_ref, dgf_ref, loss_ref):
        i = pl.program_id(0)

        @pl.when(i == 0)
        def _():
            dwout_ref[...] = jnp.zeros_like(dwout_ref)
            dpw_ref[...] = jnp.zeros_like(dpw_ref)
            dps_ref[...] = jnp.zeros_like(dps_ref)
            dgf_ref[...] = jnp.zeros_like(dgf_ref)
            loss_ref[...] = jnp.zeros_like(loss_ref)

        row0 = i * tr
        real = (row0 + lax.broadcasted_iota(jnp.int32, (tr, 1), 0)) >= HEAD_ROWS
        h = h_ref[...]

        halo = jnp.where(i > 0, halo_ref[...], 0.0)
        ext = jnp.concatenate([halo, pin_ref[...]], axis=0)
        pooled = []
        for g, w in enumerate(POOL_WINDOWS):
            e = ext[:, g * POOL_GROUP:(g + 1) * POOL_GROUP]
            acc = e
            shift = 1
            while shift < w:
                acc = acc + pltpu.roll(acc, shift, 0)
                shift *= 2
            pooled.append((acc[HALO:] * _inv_count(row0, tr, w) - e[HALO:]).astype(BF16))
        pw = [pw_ref[g].astype(BF16) for g in range(ng)]
        mixed = jnp.concatenate([_nn(pooled[g], pw[g]) for g in range(ng)], axis=1)
        ps = ps_ref[...]
        mixed_s = mixed * ps
        pg = pg_ref[...]
        sig_p = _sigmoid(pg)
        silu_p = pg * sig_p
        pool_out = (silu_p * mixed_s).astype(BF16)
        ag = ag_ref[...]
        sig_a = _sigmoid(ag)
        silu_a = ag * sig_a
        at = at_ref[...]
        attn_out = (silu_a * at).astype(BF16)
        cat = jnp.concatenate([pool_out, attn_out], axis=1)
        h2 = h + _nn(cat, wout_ref[...])

        r2 = lax.rsqrt(jnp.mean(h2 * h2, axis=-1, keepdims=True) + EPS)
        n2 = h2 * r2
        gfv = gf_ref[...]
        err = jnp.where(real, n2 * gfv - t_ref[...], 0.0)
        loss_ref[...] += jnp.sum(jnp.sum(err * err, axis=-1, keepdims=True), axis=0, keepdims=True) * (0.5 / D)
        dy = err * (1.0 / D)
        dgf_ref[...] += jnp.sum(dy * n2, axis=0, keepdims=True)
        dn = dy * gfv
        dh2 = r2 * (dn - n2 * jnp.mean(dn * n2, axis=-1, keepdims=True))
        dh2_ref[...] = dh2
        dh2b = dh2.astype(BF16)

        dwout_ref[...] += _tn(cat, dh2b)
        dcat = _nt(dh2b, wout_ref[...])
        dpo = dcat[:, 0:D_POOL]
        dao = dcat[:, D_POOL:D]
        do = dao * silu_a
        prod = do * at
        delta_ref[...] = jnp.zeros_like(delta_ref)
        for hd in range(HEADS):
            grp, lane = _stat_slot(hd)
            cols = slice(hd * V_HEAD, (hd + 1) * V_HEAD)
            do_ref[grp, :, lane * V_HEAD:(lane + 1) * V_HEAD] = do[:, cols].astype(BF16)
            delta_ref[grp, :, lane:lane + 1] = jnp.sum(prod[:, cols], axis=-1, keepdims=True)
        dag_ref[...] = (dao * at * (sig_a * (1.0 + ag * (1.0 - sig_a)))).astype(BF16)
        dmixed_s = dpo * silu_p
        dpg_ref[...] = (dpo * mixed_s * (sig_p * (1.0 + pg * (1.0 - sig_p)))).astype(BF16)
        dps_ref[...] += jnp.sum(dmixed_s * mixed, axis=0, keepdims=True)
        dmixed = (dmixed_s * ps).astype(BF16)
        dpl = []
        for g in range(ng):
            dm = dmixed[:, g * POOL_GROUP:(g + 1) * POOL_GROUP]
            dpl.append(_nt(dm, pw[g]))
            dpw_ref[g] += _tn(pooled[g], dm)
        dpl_ref[...] = jnp.concatenate(dpl, axis=1)

    halo_spec = pl.BlockSpec((HALO, D_POOL), lambda i: (jnp.maximum(i * per - 1, 0), 0))
    return pl.pallas_call(
        body,
        name="mid",
        grid=(N // tr,),
        in_specs=[
            _rows(D, tr), _rows(D, tr), _rows(D_POOL, tr), halo_spec, _rows(D_POOL, tr), _rows(D_POOL, tr),
            _rows(D_POOL, tr), _const(ng, POOL_GROUP, POOL_GROUP), _const(1, D_POOL), _const(D, D), _const(1, D),
        ],
        out_specs=[
            _rows(D, tr), pl.BlockSpec((STAT_GROUPS, tr, HEADS_PER_STEP_BWD * V_HEAD), lambda i: (0, i, 0)),
            pl.BlockSpec((STAT_GROUPS, tr, 128), lambda i: (0, i, 0)),
            _rows(D_POOL, tr), _rows(D_POOL, tr), _rows(D_POOL, tr),
            _const(D, D), _const(ng, POOL_GROUP, POOL_GROUP), _const(1, D_POOL), _const(1, D), _const(1, 128),
        ],
        out_shape=[
            jax.ShapeDtypeStruct((N, D), F32), jax.ShapeDtypeStruct((STAT_GROUPS, N, HEADS_PER_STEP_BWD * V_HEAD), BF16),
            jax.ShapeDtypeStruct((STAT_GROUPS, N, 128), F32),
            jax.ShapeDtypeStruct((N, D_POOL), BF16), jax.ShapeDtypeStruct((N, D_POOL), BF16),
            jax.ShapeDtypeStruct((N, D_POOL), F32), jax.ShapeDtypeStruct((D, D), F32),
            jax.ShapeDtypeStruct((ng, POOL_GROUP, POOL_GROUP), F32),
            jax.ShapeDtypeStruct((1, D_POOL), F32), jax.ShapeDtypeStruct((1, D), F32), jax.ShapeDtypeStruct((1, 128), F32),
        ],
        compiler_params=_cparams(dimension_semantics=("arbitrary",)),
    )(h, tgt, pool_in, pool_in, pool_gate, attn_gate, attn, pool_w, pool_scale, wout, gf)


def _unrope(dy, cosv, sinv):
    return dy * cosv + _swap64(dy * sinv) * _low_lanes()


def _attn_bwd(q, k, v, do, lse, delta, cosf, sinf, dwout):
    tiles = _attn_tiles()
    hp = HEADS_PER_STEP_BWD
    n_g = HEADS // hp
    n_t = len(tiles)
    half = SHARD_OUT // 2
    swap_at, send_at, sum_at = (0, 3), (0, 5), (n_g - 1, n_t // 2)

    def body(q_hbm, k_hbm, v_hbm, do_hbm, lse_ref, delta_ref, cos_ref, sin_ref, dwout_hbm, dq_hbm, dkv_ref, dkr_ref,
             gwout_ref, q_buf, k_buf, v_buf, do_buf, dq_buf, dk_acc, dv_acc, own_w, sib_w, stage_w, recv_w, gw_buf,
             in_sems, out_sems, ow_sems, d2d_send, d2d_recv, ici_send, ici_recv, fin_send, fin_recv):
        grp = pl.program_id(0)
        step = pl.program_id(1)
        heads = pl.ds(grp * hp, hp)
        x, y, c = lax.axis_index("x"), lax.axis_index("y"), lax.axis_index("c")
        sibling = (x, y, 1 - c)

        def chip_of(rel):
            fx, fy = _CHIP_RELS[rel]
            return 2 * (x ^ fx) + (y ^ fy)

        def piece(chip, core):
            return dwout_hbm.at[pl.ds(pl.multiple_of(SHARD_OUT * chip + half * core, half), half), :]

        def own_load(rel):
            return pltpu.make_async_copy(piece(chip_of(rel), c), own_w.at[rel], ow_sems.at[rel])

        def d2d_copy(rel):
            return _remote(piece(chip_of(rel), 1 - c), sib_w.at[rel], d2d_send.at[rel], d2d_recv.at[rel], sibling)

        def ici_copy(rel):
            fx, fy = _CHIP_RELS[rel]
            return _remote(stage_w.at[rel - 1], recv_w.at[rel - 1], ici_send.at[rel - 1], ici_recv.at[rel - 1],
                           (x ^ fx, y ^ fy, c))

        def fin_copy(core):
            spot = gw_buf.at[pl.ds(pl.multiple_of(half * core, half), half), :]
            return _remote(spot, spot, fin_send.at[0], fin_recv.at[0], sibling)

        @pl.when((grp == 0) & (step == 0))
        def _():
            _peer_signal(x, y, c)
            for rel in SEND_ORDER + (0,):
                own_load(rel).start()

        @pl.when((grp == swap_at[0]) & (step == swap_at[1]))
        def _():
            _peer_wait()
            for rel in SEND_ORDER + (0,):
                d2d_copy(rel).start()

        @pl.when((grp == send_at[0]) & (step == send_at[1]))
        def _():
            for rel in SEND_ORDER:
                own_load(rel).wait()
                d2d_copy(rel).wait_recv()
                stage_w[rel - 1] = (own_w[rel] + sib_w[rel]).astype(BF16)
                ici_copy(rel).start()

        @pl.when((grp == sum_at[0]) & (step == sum_at[1]))
        def _():
            own_load(0).wait()
            d2d_copy(0).wait_recv()
            total = own_w[0] + sib_w[0]
            for rel in (1, 2, 3):
                ici_copy(rel).wait_recv()
                total = total + recv_w[rel - 1].astype(F32)
            gw_buf[pl.ds(pl.multiple_of(half * c, half), half), :] = total
            fin_copy(c).start()

        def finish_dwout():
            fin_copy(1 - c).wait_recv()
            for rel in (0, 1, 2, 3):
                d2d_copy(rel).wait_send()
            for rel in (1, 2, 3):
                ici_copy(rel).wait_send()
            fin_copy(c).wait_send()
            gwout_ref[...] = gw_buf[...]

        def loads(g, idx):
            q0, rows, _ = tiles[idx]
            rs = pl.ds(q0, rows)
            par = (g * n_t + idx) % 2
            hs = pl.ds(g * hp, hp)
            pairs = ((q_hbm.at[hs, rs, :], q_buf.at[:, rs, :]), (k_hbm.at[hs, rs, :], k_buf.at[:, rs, :]),
                     (v_hbm.at[hs, rs, :], v_buf.at[:, rs, :]), (do_hbm.at[g, rs, :], do_buf.at[rs, :]))
            return [pltpu.make_async_copy(src, dst, in_sems.at[a, par]) for a, (src, dst) in enumerate(pairs)]

        def store(idx):
            q0, rows, _ = tiles[idx]
            return pltpu.make_async_copy(dq_buf.at[idx % 2, :, pl.ds(0, rows), :], dq_hbm.at[heads, pl.ds(q0, rows), :],
                                         out_sems.at[idx % 2])

        @pl.when(step == 0)
        def _():
            dk_acc[...] = jnp.zeros_like(dk_acc)
            dv_acc[...] = jnp.zeros_like(dv_acc)

        @pl.when((step == 0) & (grp == 0))
        def _():
            dkr_ref[...] = jnp.zeros_like(dkr_ref)
            for cp in loads(grp, 0):
                cp.start()

        for idx, (q0, rows, klen) in enumerate(tiles):
            @pl.when(step == idx)
            def _(idx=idx, q0=q0, rows=rows, klen=klen):
                for cp in loads(grp, idx):
                    cp.wait()
                if idx + 1 < n_t:
                    for cp in loads(grp, idx + 1):
                        cp.start()
                if idx >= 2:
                    store(idx - 2).wait()
                qs = pl.ds(q0, rows)
                for hd in range(hp):
                    qv = q_buf[hd, qs, :]
                    kv = k_buf[hd, 0:klen, :]
                    p = jnp.exp(_masked_scores(qv, kv, rows, klen) - lse_ref[0, qs, hd:hd + 1])
                    dob = do_buf[qs, hd * V_HEAD:(hd + 1) * V_HEAD]
                    ds = (p * (_nt(dob, v_buf[hd, 0:klen, :]) - delta_ref[0, qs, hd:hd + 1])).astype(BF16)
                    dq = _nn(ds, kv) * SCALE
                    dq_buf[idx % 2, hd, 0:rows, 0:QK_NOPE] = dq[:, 0:QK_NOPE].astype(BF16)
                    dq_buf[idx % 2, hd, 0:rows, QK_NOPE:] = _unrope(dq[:, QK_NOPE:], cos_ref[qs, :], sin_ref[qs, :]).astype(BF16)
                    dk_acc[hd, 0:klen, :] += _tn(ds, qv)
                    dv_acc[hd, 0:klen, :] += _tn(p.astype(BF16), dob)
                store(idx).start()

        @pl.when(step == n_t - 1)
        def _():
            @pl.when(grp + 1 < n_g)
            def _():
                for cp in loads(grp + 1, 0):
                    cp.start()

            for hd in range(hp):
                dkv_ref[hd, :, 0:QK_NOPE] = dk_acc[hd, :, 0:QK_NOPE].astype(BF16)
                dkv_ref[hd, :, QK_NOPE:] = dv_acc[hd].astype(BF16)
                dkr_ref[...] += dk_acc[hd, :, QK_NOPE:]
            store(n_t - 2).wait()
            store(n_t - 1).wait()

            @pl.when(grp == n_g - 1)
            def _():
                finish_dwout()

    hbm = pl.BlockSpec(memory_space=pl.ANY)
    stat = pl.BlockSpec((1, N, 128), lambda g, t: (g, 0, 0), pipeline_mode=pl.Buffered(1))
    piece_f32 = lambda lead: pltpu.VMEM((lead, half, D), F32)
    piece_bf16 = lambda lead: pltpu.VMEM((lead, half, D), BF16)
    return pl.pallas_call(
        body,
        name="attn_bwd",
        grid=(n_g, n_t),
        in_specs=[hbm, hbm, hbm, hbm, stat, stat, _const(N, 128), _const(N, 128), hbm],
        out_specs=[hbm, pl.BlockSpec((hp, N, 256), lambda g, t: (g, 0, 0), pipeline_mode=pl.Buffered(1)), _const(N, 128),
                   _const(SHARD_OUT, D)],
        out_shape=[
            jax.ShapeDtypeStruct((HEADS, N, 256), BF16), jax.ShapeDtypeStruct((HEADS, N, 256), BF16),
            jax.ShapeDtypeStruct((N, 128), F32), jax.ShapeDtypeStruct((SHARD_OUT, D), F32),
        ],
        scratch_shapes=[pltpu.VMEM((hp, N, 256), BF16), pltpu.VMEM((hp, N, 256), BF16), pltpu.VMEM((hp, N, V_HEAD), BF16),
                        pltpu.VMEM((N, hp * V_HEAD), BF16), pltpu.VMEM((2, hp, TQ, 256), BF16),
                        pltpu.VMEM((hp, N, 256), F32), pltpu.VMEM((hp, N, V_HEAD), F32),
                        piece_f32(CHIPS), piece_f32(CHIPS), piece_bf16(3), piece_bf16(3), pltpu.VMEM((SHARD_OUT, D), F32),
                        pltpu.SemaphoreType.DMA((4, 2)), pltpu.SemaphoreType.DMA((2,)), pltpu.SemaphoreType.DMA((CHIPS,)),
                        pltpu.SemaphoreType.DMA((CHIPS,)), pltpu.SemaphoreType.DMA((CHIPS,)),
                        pltpu.SemaphoreType.DMA((3,)), pltpu.SemaphoreType.DMA((3,)),
                        pltpu.SemaphoreType.DMA((1,)), pltpu.SemaphoreType.DMA((1,))],
        compiler_params=_cparams(dimension_semantics=("arbitrary", "arbitrary"), collective_id=2),
    )(q, k, v, do, lse, delta, cosf, sinf, dwout)


def _bwd_in(h, dh2, dq, dkv, dkr, cq, ckv, dpl, dpg, dag, norm_g, win, gq, wq, gkv, wkv, cosf, sinf, adam_out):
    tr = ROWS_BWD
    nb = N // tr
    per = tr // HALO
    lead = HEAD_ROWS
    adam_rows = SHARD_OUT // nb

    def body(h_ref, dh2_ref, dq_ref, dkv_ref, dkr_ref, cq_ref, ckv_ref, dpl_ref, halo_ref, dpg_ref, dag_ref,
             g_ref, win_ref, gq_ref, wq_ref, gkv_ref, wkv_ref, cos_ref, sin_ref, aw_ref, ag_ref, am_ref, av_ref,
             gx_ref, dmeta_ref, dsl_ref, dwq_ref, dwkv_ref, dg_ref, dgq_ref, dgkv_ref, ago_ref, ad_ref, anm_ref, anv_ref,
             dh_buf, gx_sem):
        i = pl.program_id(0)
        grad_out = ag_ref[...]
        ago_ref[...] = grad_out
        ad_ref[...], anm_ref[...], anv_ref[...] = _adamw_math(aw_ref[...], grad_out, am_ref[...], av_ref[...])

        @pl.when(i == 0)
        def _():
            dwq_ref[...] = jnp.zeros_like(dwq_ref)
            dwkv_ref[...] = jnp.zeros_like(dwkv_ref)
            dg_ref[...] = jnp.zeros_like(dg_ref)
            dgq_ref[...] = jnp.zeros_like(dgq_ref)
            dgkv_ref[...] = jnp.zeros_like(dgkv_ref)

        row0 = i * tr
        h = h_ref[...]
        r = lax.rsqrt(jnp.mean(h * h, axis=-1, keepdims=True) + EPS)
        n = h * r
        gv = g_ref[...]
        cq = cq_ref[...]
        rq = lax.rsqrt(jnp.mean(cq * cq, axis=-1, keepdims=True) + EPS)
        nq = cq * rq
        gqv = gq_ref[...]
        cqn = (nq * gqv).astype(BF16)
        dcqn = jnp.zeros((tr, Q_LORA), F32)
        for hd in range(HEADS):
            dqf = dq_ref[hd]
            dcqn = dcqn + _nn(dqf, wq_ref[hd])
            dwq_ref[hd] += _tn(dqf, cqn)
        dgq_ref[...] += jnp.sum(dcqn * nq, axis=0, keepdims=True)
        dnq = dcqn * gqv
        dcq = rq * (dnq - nq * jnp.mean(dnq * nq, axis=-1, keepdims=True))

        ckv = ckv_ref[...]
        rkv = lax.rsqrt(jnp.mean(ckv * ckv, axis=-1, keepdims=True) + EPS)
        nkv = ckv * rkv
        gkvv = gkv_ref[...]
        ckvn = (nkv * gkvv).astype(BF16)
        dckvn = jnp.zeros((tr, KV_LORA), F32)
        for hd in range(HEADS):
            dkv = dkv_ref[hd]
            dckvn = dckvn + _nt(dkv, wkv_ref[hd])
            dwkv_ref[hd] += _tn(ckvn, dkv)
        dgkv_ref[...] += jnp.sum(dckvn * nkv, axis=0, keepdims=True)
        dnkv = dckvn * gkvv
        dckv = rkv * (dnkv - nkv * jnp.mean(dnkv * nkv, axis=-1, keepdims=True))
        dkr = _unrope(dkr_ref[...], cos_ref[...], sin_ref[...])

        cur = dpl_ref[...]
        halo = jnp.where(i < nb - 1, halo_ref[...], 0.0)
        dpi = []
        for g, w in enumerate(POOL_WINDOWS):
            sl = slice(g * POOL_GROUP, (g + 1) * POOL_GROUP)
            a = jnp.concatenate([cur[:, sl] * _inv_count(row0, tr, w), halo[:, sl] * _inv_count(row0 + tr, HALO, w)], axis=0)
            acc = a
            shift = 1
            while shift < w:
                acc = acc + pltpu.roll(acc, tr + HALO - shift, 0)
                shift *= 2
            dpi.append(acc[0:tr] - cur[:, sl])

        du = jnp.concatenate([t.astype(BF16) for t in dpi] + [dpg_ref[...]] + [t.astype(BF16) for t in (dcq, dckv, dkr)],
                             axis=1)
        dagb = dag_ref[...]
        by_row = jnp.concatenate(dpi + [dpg_ref[...].astype(F32), dcq, dckv, dkr[:, 0:QK_ROPE], dagb.astype(F32),
                                        jnp.zeros((tr, SHARD_PAD - SHARD_IN), F32)], axis=1)
        for chip in range(CHIPS):
            dsl_ref[chip] = by_row[:, SHARD_IN * chip:SHARD_IN * chip + SHARD_PAD].astype(BF16)
        dhn = _nn(du, win_ref[0:O_KR_END, :]) + _nn(dagb, win_ref[O_AG:D_IN, :])
        dg_ref[...] += jnp.sum(dhn * n, axis=0, keepdims=True)
        dn = dhn * gv
        dh = dh2_ref[...] + r * (dn - n * jnp.mean(dn * n, axis=-1, keepdims=True))

        first = pltpu.make_async_copy(dh_buf.at[pl.ds(lead, tr - lead), :], gx_ref.at[pl.ds(0, tr - lead), :], gx_sem)
        later = lambda step: pltpu.make_async_copy(
            dh_buf, gx_ref.at[pl.ds(pl.multiple_of(step * tr - lead, 16), tr), :], gx_sem)

        @pl.when(i == 1)
        def _():
            first.wait()

        @pl.when(i > 1)
        def _():
            later(i - 1).wait()

        dh_buf[...] = dh

        @pl.when(i == 0)
        def _():
            first.start()
            for chip in range(CHIPS):
                dmeta_ref[chip] = dh[PAD:HEAD_ROWS, chip * 256:(chip + 1) * 256]

        @pl.when(i > 0)
        def _():
            later(i).start()

        @pl.when(i == nb - 1)
        def _():
            later(i).wait()

    head = lambda w: pl.BlockSpec((HEADS, tr, w), lambda i: (0, i, 0))
    halo_spec = pl.BlockSpec((HALO, D_POOL), lambda i: (jnp.minimum((i + 1) * per, N // HALO - 1), 0))
    return pl.pallas_call(
        body,
        name="bwd_in",
        grid=(nb,),
        in_specs=[
            _rows(D, tr), _rows(D, tr), head(256), head(256), _rows(128, tr), _rows(Q_LORA, tr), _rows(KV_LORA, tr),
            _rows(D_POOL, tr), halo_spec, _rows(D_POOL, tr), _rows(D_POOL, tr),
            _const(1, D), _const(D_IN, D), _const(1, Q_LORA), _const(HEADS, 256, Q_LORA),
            _const(1, KV_LORA), _const(HEADS, KV_LORA, 256), _rows(128, tr), _rows(128, tr),
        ] + [_rows(D, adam_rows)] * 4,
        out_specs=[
            pl.BlockSpec(memory_space=pl.ANY), _const(CHIPS, N_META, 256),
            pl.BlockSpec((CHIPS, tr, SHARD_PAD), lambda i: (0, i, 0)), _const(HEADS, 256, Q_LORA),
            _const(HEADS, KV_LORA, 256), _const(1, D), _const(1, Q_LORA), _const(1, KV_LORA),
        ] + [_rows(D, adam_rows)] * 4,
        out_shape=[
            jax.ShapeDtypeStruct((S, D), F32), jax.ShapeDtypeStruct((CHIPS, N_META, 256), F32),
            jax.ShapeDtypeStruct((CHIPS, N, SHARD_PAD), BF16), jax.ShapeDtypeStruct((HEADS, 256, Q_LORA), F32),
            jax.ShapeDtypeStruct((HEADS, KV_LORA, 256), F32),
            jax.ShapeDtypeStruct((1, D), F32), jax.ShapeDtypeStruct((1, Q_LORA), F32), jax.ShapeDtypeStruct((1, KV_LORA), F32),
        ] + [jax.ShapeDtypeStruct((SHARD_OUT, D), F32)] * 4,
        scratch_shapes=[pltpu.VMEM((tr, D), F32), pltpu.SemaphoreType.DMA],
        compiler_params=_cparams(dimension_semantics=("arbitrary",)),
    )(h, dh2, dq, dkv, dkr, cq, ckv, dpl, dpl, dpg, dag, norm_g, win, gq, wq, gkv, wkv, cosf, sinf, *adam_out)


def _local_step(h, tgt, norm_g, win, gq, wq, gkv, wkv, pool_w, pool_scale, wout_s, m_wout_s, v_wout_s, gf, cosf, sinf):
    pool_in, pool_gate, cq, ckv, attn_gate, q, k, v, hn = _fwd_in(h, norm_g, win, gq, wq, gkv, wkv, cosf, sinf)
    attn, lse, wout = _attn_fwd(q, k, v, wout_s)
    dh2, do, delta, dag, dpg, dpl, dwout, dpw, dps, dgf, loss = _mid(
        h, tgt, pool_in, pool_gate, attn_gate, attn, pool_w, pool_scale, wout, gf)
    dq, dkv, dkr, gwout = _attn_bwd(q, k, v, do, lse, delta, cosf, sinf, dwout)
    gx, dmeta, dsl, dwq, dwkv, dg, dgq, dgkv, *r_out = _bwd_in(
        h, dh2, dq, dkv, dkr, cq, ckv, dpl, dpg, dag, norm_g, win, gq, wq, gkv, wkv, cosf, sinf,
        (wout_s, gwout, m_wout_s, v_wout_s))
    return dict(gx=gx, dmeta=dmeta, dsl=dsl, hn=hn, dwq=dwq, dwkv=dwkv, r_out=tuple(r_out), dg=dg, dgq=dgq,
                dgkv=dgkv, dpw=dpw, dps=dps, dgf=dgf, loss=loss)


_CHIP_RELS = ((0, 0), (1, 0), (0, 1), (1, 1))

_ARR_ROWS = (SHARD_IN, SHARD_OUT, 256, KV_LORA, N_META)
_ARR_COLS = (D, D, Q_LORA, 256, 256)
_PIECES = (
    (0, 0, 256, 0), (0, 256, SHARD_IN - 256, 1),
    (1, 0, 128, 0), (1, 128, 128, 1),
    (2, 0, 128, 0), (2, 128, 128, 1),
    (3, 0, 64, 0), (3, 64, 64, 1),
    (4, 0, N_META, 0),
)
_NP = len(_PIECES)
_PIECE_MAX = (256, 128, 128, 64, N_META)


def _gathered_at(refs, arr, chip, r0, n):
    if arr in (0, 1):
        return refs[arr].at[pl.ds(pl.multiple_of(_ARR_ROWS[arr] * chip + r0, 16), n), :]
    return refs[arr].at[chip, pl.ds(r0, n), :]


def _remote(src, dst, send_sem, recv_sem, to):
    return pltpu.make_async_remote_copy(src_ref=src, dst_ref=dst, send_sem=send_sem, recv_sem=recv_sem,
                                        device_id=to, device_id_type=MESH)


def _gather_weights(winT_s, wqT_s, wkv_s, meta_s, x2, tgt2):
    arrays = (0, 2, 3, 4)

    def body(win_ref, wq_ref, wkv_ref, meta_ref, x_ref, t_ref, win_o, wq_o, wkv_o, h_o, tp_o,
             s_win, s_wq, s_wkv, meta_all, head_buf, x_buf, t_buf, ici_send, ici_recv, fwd_send, fwd_recv,
             loc_sems, own_sems):
        x, y, c = lax.axis_index("x"), lax.axis_index("y"), lax.axis_index("c")
        me = 2 * x + y
        stage = (s_win, None, s_wq, s_wkv, meta_ref)
        outs = (win_o, None, wq_o, wkv_o, meta_all)

        _peer_signal(x, y, c)

        frames = pl.ds(HEAD_ROWS, S)
        loads = [pltpu.make_async_copy(x_ref, x_buf, loc_sems.at[0]), pltpu.make_async_copy(t_ref, t_buf, loc_sems.at[1])]
        local = [pltpu.make_async_copy(x_buf, h_o.at[frames, :], loc_sems.at[0]),
                 pltpu.make_async_copy(t_buf, tp_o.at[frames, :], loc_sems.at[1])]
        for cp in loads:
            cp.start()

        s_win[...] = win_ref[...].astype(BF16)
        s_wq[0:QK, :] = wq_ref[...].astype(BF16)
        s_wq[QK:256, :] = jnp.zeros((256 - QK, Q_LORA), BF16)
        s_wkv[...] = wkv_ref[...].astype(BF16)
        head_buf[...] = jnp.zeros_like(head_buf)
        zeros = pltpu.make_async_copy(head_buf, tp_o.at[pl.ds(0, HEAD_ROWS), :], loc_sems.at[2])
        zeros.start()

        def chip_of(rel):
            fx, fy = _CHIP_RELS[rel]
            return 2 * (x ^ fx) + (y ^ fy)

        def same_core_of(rel):
            fx, fy = _CHIP_RELS[rel]
            return (x ^ fx, y ^ fy, c)

        def ici_copy(rel, i, src_chip, to):
            arr, r0, n, _ = _PIECES[i]
            k = (rel - 1) * _NP + i
            return _remote(stage[arr].at[pl.ds(r0, n), :], _gathered_at(outs, arr, src_chip, r0, n),
                           ici_send.at[k], ici_recv.at[k], to)

        def fwd_copy(rel, i, to):
            arr, r0, n, _ = _PIECES[i]
            k = (rel - 1) * _NP + i
            place = _gathered_at(outs, arr, chip_of(rel), r0, n)
            return _remote(place, place, fwd_send.at[k], fwd_recv.at[k], to)

        _peer_wait()
        for core in (0, 1):
            @pl.when(c == core)
            def _(core=core):
                mine = [i for i in range(_NP) if _PIECES[i][3] == core and _PIECES[i][0] in arrays]
                theirs = [i for i in range(_NP) if _PIECES[i][3] != core and _PIECES[i][0] in arrays]
                sends = [ici_copy(rel, i, me, same_core_of(rel)) for rel in SEND_ORDER for i in mine]
                for cp in sends:
                    cp.start()
                for ld, st in zip(loads, local):
                    ld.wait()
                    st.start()
                own = [pltpu.make_async_copy(stage[arr], _gathered_at(outs, arr, me, 0, _ARR_ROWS[arr]), own_sems.at[arr])
                       for arr in arrays if arr != 4]
                for cp in own:
                    cp.start()
                meta_all[me] = meta_ref[...]
                for rel in SEND_ORDER:
                    for i in mine:
                        ici_copy(rel, i, chip_of(rel), (x, y, c)).wait_recv()
                        fwd = fwd_copy(rel, i, (x, y, 1 - c))
                        fwd.start()
                        sends.append(fwd)
                for rel in SEND_ORDER:
                    for i in theirs:
                        fwd_copy(rel, i, (x, y, c)).wait_recv()
                for cp in sends:
                    cp.wait_send()
                for cp in own:
                    cp.wait()

        zeros.wait()
        for chip in range(CHIPS):
            head_buf[PAD:HEAD_ROWS, chip * 256:(chip + 1) * 256] = meta_all[chip]
        head = pltpu.make_async_copy(head_buf, h_o.at[pl.ds(0, HEAD_ROWS), :], loc_sems.at[2])
        head.start()
        head.wait()
        for cp in local:
            cp.wait()

    vm = pl.BlockSpec(memory_space=pltpu.VMEM)
    hbm = pl.BlockSpec(memory_space=pl.ANY)
    return pl.pallas_call(
        body,
        name="gather_weights",
        in_specs=[vm] * 4 + [hbm] * 2,
        out_specs=[hbm] * 5,
        out_shape=[
            jax.ShapeDtypeStruct((D_IN, D), BF16),
            jax.ShapeDtypeStruct((CHIPS, 256, Q_LORA), BF16), jax.ShapeDtypeStruct((CHIPS, KV_LORA, 256), BF16),
            jax.ShapeDtypeStruct((N, D), F32), jax.ShapeDtypeStruct((N, D), F32),
        ],
        scratch_shapes=[pltpu.VMEM((_ARR_ROWS[a], _ARR_COLS[a]), BF16) for a in (0, 2, 3)]
        + [pltpu.VMEM((CHIPS, N_META, 256), F32), pltpu.VMEM((HEAD_ROWS, D), F32), pltpu.VMEM((S, D), F32),
           pltpu.VMEM((S, D), F32)]
        + [pltpu.SemaphoreType.DMA((3 * _NP,))] * 4 + [pltpu.SemaphoreType.DMA((3,)), pltpu.SemaphoreType.DMA((4,))],
        compiler_params=_cparams(collective_id=0),
    )(winT_s, wqT_s, wkv_s, meta_s, x2, tgt2)


_SM_ROWS = (len(POOL_WINDOWS) * POOL_GROUP, VEC_ROWS)
_SM_COLS = (POOL_GROUP, D)
_SM_PIECES = ((0, 0, 256, 0), (0, 256, 256, 1), (1, 0, VEC_ROWS, 0))
_NSP = len(_SM_PIECES)


def _reduce_grads(dsl, hn, dwq, dwkv, dmeta4, dpw, dg, dgf, dgq, dgkv, dps, loss):
    arrays = (0, 2, 3, 4)
    loaded = (2, 3, 4)
    shard_order = SEND_ORDER + (0,)

    def body(dsl_hbm, hn_hbm, dwq_ref, dwkv_ref, dmeta_ref, dpw_ref, dg_ref, dgf_ref, dgq_ref, dgkv_ref, dps_ref,
             loss_ref, gwin_o, gwq_o, gwkv_o, gmeta_o, gpw_o, gg_o, ggf_o, ggq_o, ggkv_o, gps_o, gloss_o,
             ow2, ow3, ow4, sb0, sb2, sb3, sb4, st0, st2, st3, st4, rc0, rc2, rc3, rc4,
             vec, sm_sb0, sm_sb1, sm_cs0, sm_cs1, sm_rc0, sm_rc1, vec_fin, slab_v, hn_v, dwin_buf, own0,
             own_sems, d2d_send, d2d_recv, ici_send, ici_recv, fin_send, fin_recv,
             swap_send, swap_recv, smi_send, smi_recv, smf_send, smf_recv, ld_sems):
        x, y, c = lax.axis_index("x"), lax.axis_index("y"), lax.axis_index("c")
        me = 2 * x + y
        _peer_signal(x, y, c)
        grads = (None, None, dwq_ref, dwkv_ref, dmeta_ref)
        outs = (gwin_o, None, gwq_o, gwkv_o, gmeta_o)
        own_buf = (None, None, ow2, ow3, ow4)
        sib_buf = (sb0, None, sb2, sb3, sb4)
        stage = (st0, None, st2, st3, st4)
        recv = (rc0, None, rc2, rc3, rc4)
        sm_mine = (dpw_ref, vec)
        sm_sib = (sm_sb0, sm_sb1)
        sm_chip = (sm_cs0, sm_cs1)
        sm_recv = (sm_rc0, sm_rc1)
        sm_out = (gpw_o, vec_fin)
        sibling = (x, y, 1 - c)

        def chip_of(rel):
            fx, fy = _CHIP_RELS[rel]
            return 2 * (x ^ fx) + (y ^ fy)

        def same_core_of(rel):
            fx, fy = _CHIP_RELS[rel]
            return (x ^ fx, y ^ fy, c)

        hn_load = pltpu.make_async_copy(hn_hbm, hn_v, ld_sems.at[CHIPS])

        def slab_load(rel):
            return pltpu.make_async_copy(dsl_hbm.at[chip_of(rel)], slab_v.at[rel], ld_sems.at[rel])

        hn_load.start()
        slab_load(shard_order[0]).start()

        def slot(bufs, i, idx):
            arr, _, n, _ = _PIECES[i]
            return bufs[arr].at[idx, pl.ds(0, n), :]

        def own_load(rel, i):
            arr, r0, n, _ = _PIECES[i]
            return pltpu.make_async_copy(_gathered_at(grads, arr, chip_of(rel), r0, n), slot(own_buf, i, rel),
                                         own_sems.at[rel * _NP + i])

        def d2d_copy(rel, i):
            arr, r0, n, _ = _PIECES[i]
            k = rel * _NP + i
            return _remote(_gathered_at(grads, arr, chip_of(rel), r0, n), slot(sib_buf, i, rel),
                           d2d_send.at[k], d2d_recv.at[k], sibling)

        def ici_copy(rel, i):
            k = (rel - 1) * _NP + i
            return _remote(slot(stage, i, rel - 1), slot(recv, i, rel - 1), ici_send.at[k], ici_recv.at[k],
                           same_core_of(rel))

        def fin_copy(i):
            arr, r0, n, _ = _PIECES[i]
            place = outs[arr].at[pl.ds(r0, n), :]
            return _remote(place, place, fin_send.at[i], fin_recv.at[i], sibling)

        def sm_ici_copy(rel, j):
            blk, r0, n, _ = _SM_PIECES[j]
            k = (rel - 1) * _NSP + j
            return _remote(sm_chip[blk].at[pl.ds(r0, n), :], sm_recv[blk].at[rel - 1, pl.ds(r0, n), :],
                           smi_send.at[k], smi_recv.at[k], same_core_of(rel))

        def sm_fin_copy(j):
            blk, r0, n, _ = _SM_PIECES[j]
            place = sm_out[blk].at[pl.ds(r0, n), :]
            return _remote(place, place, smf_send.at[j], smf_recv.at[j], sibling)

        vec[...] = jnp.zeros_like(vec)
        vec[0:1, :] = dg_ref[...]
        vec[1:2, :] = dgf_ref[...]
        vec[2:3, V_GQ:V_GQ + Q_LORA] = dgq_ref[...]
        vec[2:3, V_GKV:V_GKV + KV_LORA] = dgkv_ref[...]
        vec[2:3, V_PS:V_PS + D_POOL] = dps_ref[...]
        vec[2:3, V_LOSS:D] = loss_ref[...]
        _peer_wait()
        swaps = [_remote(sm_mine[b], sm_sib[b], swap_send.at[b], swap_recv.at[b], sibling) for b in (0, 1)]
        for cp in swaps:
            cp.start()

        for core in (0, 1):
            @pl.when(c == core)
            def _(core=core):
                mine = [i for i in range(_NP) if _PIECES[i][3] == core and _PIECES[i][0] in loaded]
                theirs = [i for i in range(_NP) if _PIECES[i][3] != core and _PIECES[i][0] in loaded]
                i0 = next(i for i in range(_NP) if _PIECES[i][0] == 0 and _PIECES[i][3] == core)
                j0 = next(i for i in range(_NP) if _PIECES[i][0] == 0 and _PIECES[i][3] != core)
                sm_mine_p = [j for j in range(_NSP) if _SM_PIECES[j][3] == core]
                sm_theirs_p = [j for j in range(_NSP) if _SM_PIECES[j][3] != core]
                sends = list(swaps)

                for rel in shard_order:
                    for i in theirs:
                        cp = d2d_copy(rel, i)
                        cp.start()
                        sends.append(cp)
                    for i in mine:
                        own_load(rel, i).start()

                def piece_rows(i):
                    return pl.ds(_PIECES[i][1], _PIECES[i][2])

                def form(rel, i):
                    r0, n = _PIECES[i][1], _PIECES[i][2]
                    dwin_buf[rel, r0:r0 + n, :] = _tn(slab_v[rel, :, r0:r0 + _PIECE_MAX[0]], hn_v[...])[0:n, :]

                def d2d0(rel, i):
                    return _remote(dwin_buf.at[rel, piece_rows(i), :], slot(sib_buf, i, rel),
                                   d2d_send.at[rel * _NP + i], d2d_recv.at[rel * _NP + i], sibling)

                def settle(rel):
                    d2d0(rel, i0).wait_recv()
                    total = dwin_buf[rel, piece_rows(i0), :] + slot(sib_buf, i0, rel)[...]
                    if rel == 0:
                        own0[0:_PIECES[i0][2], :] = total
                    else:
                        slot(stage, i0, rel - 1)[...] = total.astype(BF16)
                        cp = ici_copy(rel, i0)
                        cp.start()
                        sends.append(cp)

                hn_load.wait()
                for n, rel in enumerate(shard_order):
                    slab_load(rel).wait()
                    if n == 0:
                        for later in shard_order[1:]:
                            slab_load(later).start()
                    form(rel, j0)
                    cp = d2d0(rel, j0)
                    cp.start()
                    sends.append(cp)
                    if n > 0:
                        settle(shard_order[n - 1])
                    form(rel, i0)
                settle(shard_order[-1])

                for rel in SEND_ORDER:
                    for i in mine:
                        arr, r0, n, _ = _PIECES[i]
                        own_load(rel, i).wait()
                        d2d_copy(rel, i).wait_recv()
                        total = slot(own_buf, i, rel)[...] + slot(sib_buf, i, rel)[...]
                        slot(stage, i, rel - 1)[...] = total.astype(stage[arr].dtype)
                        cp = ici_copy(rel, i)
                        cp.start()
                        sends.append(cp)

                for b in (0, 1):
                    swaps[b].wait_recv()
                    sm_chip[b][...] = sm_mine[b][...] + sm_sib[b][...]
                for rel in SEND_ORDER:
                    for j in sm_mine_p:
                        cp = sm_ici_copy(rel, j)
                        cp.start()
                        sends.append(cp)

                for i in mine:
                    arr, r0, n, _ = _PIECES[i]
                    own_load(0, i).wait()
                    d2d_copy(0, i).wait_recv()
                    total = slot(own_buf, i, 0)[...] + slot(sib_buf, i, 0)[...]
                    for rel in (1, 2, 3):
                        ici_copy(rel, i).wait_recv()
                        total = total + slot(recv, i, rel - 1)[...].astype(F32)
                    outs[arr][pl.ds(r0, n), :] = total
                    cp = fin_copy(i)
                    cp.start()
                    sends.append(cp)
                total = own0[0:_PIECES[i0][2], :]
                for rel in (1, 2, 3):
                    ici_copy(rel, i0).wait_recv()
                    total = total + slot(recv, i0, rel - 1)[...].astype(F32)
                outs[0][pl.ds(_PIECES[i0][1], _PIECES[i0][2]), :] = total
                cp = fin_copy(i0)
                cp.start()
                sends.append(cp)

                for j in sm_mine_p:
                    blk, r0, n, _ = _SM_PIECES[j]
                    for rel in (1, 2, 3):
                        sm_ici_copy(rel, j).wait_recv()
                    total = jnp.zeros((n, _SM_COLS[blk]), F32)
                    for chip in range(CHIPS):
                        flips = chip ^ me
                        rel = jnp.where(flips == 2, 1, jnp.where(flips == 1, 2, flips))
                        theirs_rows = sm_recv[blk][jnp.maximum(rel - 1, 0), pl.ds(r0, n), :]
                        total = total + jnp.where(rel == 0, sm_chip[blk][pl.ds(r0, n), :], theirs_rows)
                    sm_out[blk][pl.ds(r0, n), :] = total
                    cp = sm_fin_copy(j)
                    cp.start()
                    sends.append(cp)

                for i in theirs + [j0]:
                    fin_copy(i).wait_recv()
                for j in sm_theirs_p:
                    sm_fin_copy(j).wait_recv()
                for cp in sends:
                    cp.wait_send()

        gg_o[...] = vec_fin[0:1, :]
        ggf_o[...] = vec_fin[1:2, :]
        ggq_o[...] = vec_fin[2:3, V_GQ:V_GQ + Q_LORA]
        ggkv_o[...] = vec_fin[2:3, V_GKV:V_GKV + KV_LORA]
        gps_o[...] = vec_fin[2:3, V_PS:V_PS + D_POOL]
        gloss_o[...] = vec_fin[2:3, V_LOSS:D]

    vm = pl.BlockSpec(memory_space=pltpu.VMEM)
    piece_buf = lambda lead, dtype, which=arrays: [
        pltpu.VMEM((lead, _PIECE_MAX[a], _ARR_COLS[a]), F32 if a == 4 else dtype) for a in which]
    sm_buf = lambda *lead: [pltpu.VMEM(lead + (_SM_ROWS[b], _SM_COLS[b]), F32) for b in (0, 1)]
    dma = lambda n: [pltpu.SemaphoreType.DMA((n,))] * 2
    return pl.pallas_call(
        body,
        name="reduce_grads",
        in_specs=[pl.BlockSpec(memory_space=pl.ANY)] * 4 + [vm] * 8,
        out_specs=[vm] * 11,
        out_shape=[jax.ShapeDtypeStruct((_ARR_ROWS[a], _ARR_COLS[a]), F32) for a in arrays]
        + [jax.ShapeDtypeStruct((_SM_ROWS[0], _SM_COLS[0]), F32), jax.ShapeDtypeStruct((1, D), F32),
           jax.ShapeDtypeStruct((1, D), F32), jax.ShapeDtypeStruct((1, Q_LORA), F32),
           jax.ShapeDtypeStruct((1, KV_LORA), F32), jax.ShapeDtypeStruct((1, D_POOL), F32),
           jax.ShapeDtypeStruct((1, 128), F32)],
        scratch_shapes=piece_buf(CHIPS, F32, loaded) + piece_buf(CHIPS, F32) + piece_buf(3, BF16) + piece_buf(3, BF16)
        + [pltpu.VMEM((VEC_ROWS, D), F32)] + sm_buf() + sm_buf() + sm_buf(3) + [pltpu.VMEM((VEC_ROWS, D), F32)]
        + [pltpu.VMEM((CHIPS, N, SHARD_PAD), BF16), pltpu.VMEM((N, D), BF16),
           pltpu.VMEM((CHIPS, SHARD_PAD, D), F32), pltpu.VMEM((_PIECE_MAX[0], D), F32)]
        + [pltpu.SemaphoreType.DMA((CHIPS * _NP,))]
        + dma(CHIPS * _NP) + dma(3 * _NP) + dma(_NP) + dma(2) + dma(3 * _NSP) + dma(_NSP)
        + [pltpu.SemaphoreType.DMA((CHIPS + 1,))],
        compiler_params=_cparams(collective_id=3),
    )(dsl, hn, dwq, dwkv, dmeta4, dpw, dg, dgf, dgq, dgkv, dps, loss)


def _adamw_math(w, g, m, v):
    m = B1 * m + (1.0 - B1) * g
    v = B2 * v + (1.0 - B2) * (g * g)
    m_hat = m / C1
    v_hat = v / C2
    delta = -LR * (m_hat / (jnp.sqrt(v_hat) + ADAM_EPS) + WD * w)
    return delta, m, v


def _adamw(big, block_rows, groups):
    rows, cols = big[0].shape
    n = len(groups)

    def body(*refs):
        w_ref, g_ref, m_ref, v_ref = refs[0:4]
        small_in = refs[4:4 + 4 * n]
        go_ref, d_ref, nm_ref, nv_ref = refs[4 + 4 * n:8 + 4 * n]
        small_out = refs[8 + 4 * n:]
        g = g_ref[...]
        go_ref[...] = g
        d_ref[...], nm_ref[...], nv_ref[...] = _adamw_math(w_ref[...], g, m_ref[...], v_ref[...])

        @pl.when(pl.program_id(0) == 0)
        def _():
            for t in range(n):
                sw_ref, sg_ref, sm_ref, sv_ref = small_in[4 * t:4 * t + 4]
                sg = sg_ref[0:sw_ref.shape[0], :]
                small_out[4 * t][...] = sg
                small_out[4 * t + 1][...], small_out[4 * t + 2][...], small_out[4 * t + 3][...] = _adamw_math(
                    sw_ref[...], sg, sm_ref[...], sv_ref[...])

    spec = pl.BlockSpec((block_rows, cols), lambda i: (i, 0))
    vm = pl.BlockSpec(memory_space=pltpu.VMEM)
    outs = pl.pallas_call(
        body,
        name="adamw",
        grid=(rows // block_rows,),
        in_specs=[spec] * 4 + [vm] * (4 * n),
        out_specs=[spec] * 4 + [vm] * (4 * n),
        out_shape=[jax.ShapeDtypeStruct(big[0].shape, F32)] * 4
        + [jax.ShapeDtypeStruct(grp[0].shape, F32) for grp in groups for _ in range(4)],
        compiler_params=_cparams(dimension_semantics=("arbitrary",)),
    )(*big, *[a for grp in groups for a in grp])
    return tuple(outs[0:4]), [tuple(outs[4 + 4 * t:8 + 4 * t]) for t in range(n)]


def _rope_tables():
    half = QK_ROPE // 2
    f32 = np.float32
    inv_freq = (f32(1.0) / (f32(ROPE_THETA) ** (np.arange(half, dtype=f32) / f32(half)))).astype(f32)
    pos = np.arange(N, dtype=f32) - f32(PAD)
    ang = (pos[:, None] * inv_freq[None, :]).astype(f32)
    cos, sin = np.cos(ang).astype(f32), np.sin(ang).astype(f32)
    zero = np.zeros((N, 128 - QK_ROPE), f32)
    return jnp.asarray(np.concatenate([cos, cos, zero], axis=1)), jnp.asarray(np.concatenate([-sin, sin, zero], axis=1))


def kernel(x, meta_tokens, norm_g, w_in, q_norm_g, w_q_b, kv_norm_g, w_kv_b, pool_w, pool_scale, w_out, final_norm_g, loss_target, m_meta_tokens, m_norm_g, m_w_in, m_q_norm_g, m_w_q_b, m_kv_norm_g, m_w_kv_b, m_pool_w, m_pool_scale, m_w_out, m_final_norm_g, v_meta_tokens, v_norm_g, v_w_in, v_q_norm_g, v_w_q_b, v_kv_norm_g, v_w_kv_b, v_pool_w, v_pool_scale, v_w_out, v_final_norm_g):
    tr = lambda a: a[0].T
    win, wq, wkv, h, tgt = _gather_weights(tr(w_in), tr(w_q_b), w_kv_b[0], meta_tokens, x[0], loss_target[0])
    cosf, sinf = _rope_tables()
    gf = final_norm_g.reshape(1, D)

    part = _local_step(h, tgt, norm_g, win, q_norm_g, wq, kv_norm_g, wkv, pool_w[0], pool_scale, w_out[0], m_w_out[0],
                       v_w_out[0], gf, cosf, sinf)

    pw2 = lambda a: a.reshape(len(POOL_WINDOWS) * POOL_GROUP, POOL_GROUP)
    gwinT, gwqT, gwkv, gmeta, gpw, gg, ggf, ggq, ggkv, gps, gloss = _reduce_grads(
        part["dsl"], part["hn"], part["dwq"], part["dwkv"], part["dmeta"], pw2(part["dpw"]), part["dg"],
        part["dgf"], part["dgq"], part["dgkv"], part["dps"], part["loss"])

    r_out = part["r_out"]
    fn2 = lambda a: a.reshape(1, D)
    r_in, (r_meta, r_norm, r_gq, r_wq, r_gkv, r_wkv, r_pw, r_ps, r_fn) = _adamw((tr(w_in), gwinT, tr(m_w_in), tr(v_w_in)), 248, [
        (meta_tokens, gmeta, m_meta_tokens, v_meta_tokens),
        (norm_g, gg, m_norm_g, v_norm_g),
        (q_norm_g, ggq, m_q_norm_g, v_q_norm_g),
        (tr(w_q_b), gwqT, tr(m_w_q_b), tr(v_w_q_b)),
        (kv_norm_g, ggkv, m_kv_norm_g, v_kv_norm_g),
        (w_kv_b[0], gwkv, m_w_kv_b[0], v_w_kv_b[0]),
        (pw2(pool_w), gpw, pw2(m_pool_w), pw2(v_pool_w)),
        (pool_scale, gps, m_pool_scale, v_pool_scale),
        (fn2(final_norm_g), ggf, fn2(m_final_norm_g), fn2(v_final_norm_g)),
    ])
    untr = lambda a: a.T[None]
    pw4 = lambda a: a.reshape(1, len(POOL_WINDOWS), POOL_GROUP, POOL_GROUP)
    per_kind = [[
        r_meta[kind], r_norm[kind], untr(r_in[kind]), r_gq[kind], untr(r_wq[kind]), r_gkv[kind], r_wkv[kind][None],
        pw4(r_pw[kind]), r_ps[kind], r_out[kind][None], r_fn[kind].reshape(D),
    ] for kind in range(4)]
    return (gloss[0, 0], part["gx"][None], *per_kind[0], *per_kind[1], *per_kind[2], *per_kind[3])
```

```python
import jax
import jax.numpy as jnp
import numpy as np
from jax import lax
from jax.experimental import pallas as pl
from jax.experimental.pallas import tpu as pltpu

F32 = jnp.float32
BF16 = jnp.bfloat16

D = 1024
S = 2048
N_META = 16
PAD = 112
HEAD_ROWS = PAD + N_META
N = HEAD_ROWS + S
D_POOL = 512
POOL_WINDOWS = (2, 4, 8, 16)
POOL_GROUP = 128
HALO = 16
HEADS = 4
QK_NOPE = 128
QK_ROPE = 64
QK = QK_NOPE + QK_ROPE
V_HEAD = 128
Q_LORA = 256
KV_LORA = 128
D_IN = 1984
EPS = 1e-6
ROPE_THETA = 10000.0
SCALE = QK ** -0.5
CHIPS = 4

ROWS_FWD = 544
ROWS_MID = 544
ROWS_BWD = 544
TK = 128
TQ = 256
NQ = S // TQ
HEADS_PER_STEP_BWD = 2

O_PI, O_PG, O_CQ, O_CKV, O_KR, O_AG = 0, 512, 1024, 1280, 1408, 1472
O_KR_END = O_KR + 128
SHARD_IN = D_IN // CHIPS
SHARD_PAD = 512
SHARD_OUT = D // CHIPS

LR, B1, B2, ADAM_EPS, WD, STEP = 0.001, 0.9, 0.999, 1e-08, 0.01, 10
C1 = 1.0 - B1**STEP
C2 = 1.0 - B2**STEP

VMEM_LIMIT = 60 * 1024 * 1024
MESH = pl.DeviceIdType.MESH
NEG = -1e30

VEC_ROWS = 8
V_GQ, V_GKV, V_PS, V_LOSS = 0, 256, 384, 896


def _cparams(**kw):
    return pltpu.CompilerParams(vmem_limit_bytes=VMEM_LIMIT, **kw)


def _nt(a, b):
    return lax.dot_general(a, b, (((1,), (1,)), ((), ())), preferred_element_type=F32)


def _tn(a, b):
    return lax.dot_general(a, b, (((0,), (0,)), ((), ())), preferred_element_type=F32)


def _nn(a, b):
    return jnp.dot(a, b, preferred_element_type=F32)


def _swap64(t):
    return pltpu.roll(t, 32, 1) + pltpu.roll(t, 96, 1)


def _sigmoid(x):
    return 1.0 / (1.0 + jnp.exp(-x))


def _low_lanes():
    return (lax.broadcasted_iota(jnp.int32, (1, 128), 1) < QK_ROPE).astype(F32)


def _rows(w, rows):
    return pl.BlockSpec((rows, w), lambda i: (i, 0))


def _const(*shape):
    return pl.BlockSpec(shape, lambda *_: (0,) * len(shape), pipeline_mode=pl.Buffered(1))


STAT_GROUPS = HEADS // HEADS_PER_STEP_BWD


def _stat_slot(head):
    return head // HEADS_PER_STEP_BWD, head % HEADS_PER_STEP_BWD


N_PEERS = 4
SEND_ORDER = (3, 1, 2)


def _peer_signal(x, y, c):
    barrier = pltpu.get_barrier_semaphore()
    peers = [(x, y, 1 - c)] + [(x ^ fx, y ^ fy, c) for fx, fy in _CHIP_RELS[1:]]
    assert len(peers) == N_PEERS
    for peer in peers:
        pl.semaphore_signal(barrier, inc=1, device_id=peer, device_id_type=MESH)


def _peer_wait():
    pl.semaphore_wait(pltpu.get_barrier_semaphore(), N_PEERS)


def _attn_tiles():
    return [(0, TK, TK)] + [(TK + TQ * t, TQ, TK + TQ * (t + 1)) for t in range(NQ)]


def _masked_scores(q, k, rows, klen):
    s = _nt(q, k)
    col = lax.broadcasted_iota(jnp.int32, (1, TK), 1)
    head_bias = jnp.where(col >= PAD, 0.0, NEG)
    if klen == TK:
        return s + head_bias
    r = lax.broadcasted_iota(jnp.int32, (rows, 1), 0) >> 6
    c = lax.broadcasted_iota(jnp.int32, (1, rows), 1) >> 6
    diag_bias = jnp.where(c <= r, 0.0, NEG)
    parts = [s[:, 0:TK] + head_bias]
    if klen - rows > TK:
        parts.append(s[:, TK:klen - rows])
    parts.append(s[:, klen - rows:klen] + diag_bias)
    return jnp.concatenate(parts, axis=1)


def _fwd_in(h, norm_g, win, gq, wq, gkv, wkv, cosf, sinf):
    tr = ROWS_FWD

    def body(h_ref, g_ref, win_ref, gq_ref, wq_ref, gkv_ref, wkv_ref, cos_ref, sin_ref,
             pi_ref, pg_ref, cq_ref, ckv_ref, ag_ref, q_ref, k_ref, v_ref, hn_ref):
        h = h_ref[...]
        r = lax.rsqrt(jnp.mean(h * h, axis=-1, keepdims=True) + EPS)
        hn = ((h * r) * g_ref[...]).astype(BF16)
        hn_ref[...] = hn
        u = _nt(hn, win_ref[0:O_KR_END, :])
        pi_ref[...] = u[:, O_PI:O_PG]
        pg_ref[...] = u[:, O_PG:O_CQ]
        cq = u[:, O_CQ:O_CKV]
        ckv = u[:, O_CKV:O_KR]
        cq_ref[...] = cq
        ckv_ref[...] = ckv
        ag_ref[...] = _nt(hn, win_ref[O_AG:D_IN, :])
        cosv = cos_ref[...]
        sinv = sin_ref[...]
        kr = u[:, O_KR:O_KR_END] * _low_lanes()
        kr = (kr * cosv + _swap64(kr) * sinv).astype(BF16)
        rq = lax.rsqrt(jnp.mean(cq * cq, axis=-1, keepdims=True) + EPS)
        cqn = ((cq * rq) * gq_ref[...]).astype(BF16)
        rkv = lax.rsqrt(jnp.mean(ckv * ckv, axis=-1, keepdims=True) + EPS)
        ckvn = ((ckv * rkv) * gkv_ref[...]).astype(BF16)
        for hd in range(HEADS):
            qh = _nt(cqn, wq_ref[hd]) * SCALE
            z = qh[:, QK_NOPE:]
            q_ref[hd, :, 0:QK_NOPE] = qh[:, 0:QK_NOPE].astype(BF16)
            q_ref[hd, :, QK_NOPE:] = (z * cosv + _swap64(z) * sinv).astype(BF16)
            kvh = _nn(ckvn, wkv_ref[hd])
            k_ref[hd, :, 0:QK_NOPE] = kvh[:, 0:QK_NOPE].astype(BF16)
            k_ref[hd, :, QK_NOPE:] = kr
            v_ref[hd] = kvh[:, QK_NOPE:].astype(BF16)

    head = lambda w: pl.BlockSpec((HEADS, tr, w), lambda i: (0, i, 0))
    return pl.pallas_call(
        body,
        name="fwd_in",
        grid=(N // tr,),
        in_specs=[
            _rows(D, tr), _const(1, D), _const(D_IN, D), _const(1, Q_LORA), _const(HEADS, 256, Q_LORA),
            _const(1, KV_LORA), _const(HEADS, KV_LORA, 256), _rows(128, tr), _rows(128, tr),
        ],
        out_specs=[_rows(D_POOL, tr), _rows(D_POOL, tr), _rows(Q_LORA, tr), _rows(KV_LORA, tr), _rows(D_POOL, tr),
                   head(256), head(256), head(V_HEAD), _rows(D, tr)],
        out_shape=[
            jax.ShapeDtypeStruct((N, D_POOL), F32), jax.ShapeDtypeStruct((N, D_POOL), F32),
            jax.ShapeDtypeStruct((N, Q_LORA), F32), jax.ShapeDtypeStruct((N, KV_LORA), F32),
            jax.ShapeDtypeStruct((N, D_POOL), F32),
            jax.ShapeDtypeStruct((HEADS, N, 256), BF16), jax.ShapeDtypeStruct((HEADS, N, 256), BF16),
            jax.ShapeDtypeStruct((HEADS, N, V_HEAD), BF16), jax.ShapeDtypeStruct((N, D), BF16),
        ],
        compiler_params=_cparams(dimension_semantics=("arbitrary",)),
    )(h, norm_g, win, gq, wq, gkv, wkv, cosf, sinf)


def _attn_fwd(q, k, v, wout_s):
    tiles = _attn_tiles()
    n_t = len(tiles)
    half = SHARD_OUT // 2
    send_step = 2
    fwd_step = n_t - 2

    def body(q_hbm, k_hbm, v_hbm, wout_ref, o_hbm, lse_ref, wout_o, q_buf, k_buf, v_buf, o_buf, s_wout, in_sems, out_sems,
             ici_send, ici_recv, fwd_send, fwd_recv, own_sem):
        step = pl.program_id(0)
        x, y, c = lax.axis_index("x"), lax.axis_index("y"), lax.axis_index("c")
        me = 2 * x + y

        def chip_of(rel):
            fx, fy = _CHIP_RELS[rel]
            return 2 * (x ^ fx) + (y ^ fy)

        def place(chip, core):
            return wout_o.at[pl.ds(pl.multiple_of(SHARD_OUT * chip + half * core, half), half), :]

        def ici_copy(rel, src_chip, to):
            return _remote(s_wout.at[pl.ds(pl.multiple_of(half * c, half), half), :], place(src_chip, c),
                           ici_send.at[rel - 1], ici_recv.at[rel - 1], to)

        def fwd_copy(rel, core, to):
            spot = place(chip_of(rel), core)
            return _remote(spot, spot, fwd_send.at[rel - 1], fwd_recv.at[rel - 1], to)

        own = pltpu.make_async_copy(s_wout, wout_o.at[pl.ds(pl.multiple_of(SHARD_OUT * me, SHARD_OUT), SHARD_OUT), :], own_sem)

        @pl.when(step == 0)
        def _():
            _peer_signal(x, y, c)
            s_wout[...] = wout_ref[...].astype(BF16)
            own.start()

        @pl.when(step == send_step)
        def _():
            _peer_wait()
            for rel in SEND_ORDER:
                fx, fy = _CHIP_RELS[rel]
                ici_copy(rel, me, (x ^ fx, y ^ fy, c)).start()

        @pl.when(step == fwd_step)
        def _():
            for rel in (1, 2, 3):
                ici_copy(rel, chip_of(rel), (x, y, c)).wait_recv()
                fwd_copy(rel, c, (x, y, 1 - c)).start()

        def finish_wout():
            for rel in (1, 2, 3):
                fwd_copy(rel, 1 - c, (x, y, c)).wait_recv()
            for rel in (1, 2, 3):
                ici_copy(rel, me, (x, y, c)).wait_send()
                fwd_copy(rel, c, (x, y, c)).wait_send()
            own.wait()

        def loads(idx):
            q0, rows, _ = tiles[idx]
            rs = pl.ds(q0, rows)
            return [pltpu.make_async_copy(src.at[:, rs, :], dst.at[:, rs, :], in_sems.at[a, idx % 2])
                    for a, (src, dst) in enumerate(((q_hbm, q_buf), (k_hbm, k_buf), (v_hbm, v_buf)))]

        def store(idx):
            q0, rows, _ = tiles[idx]
            return pltpu.make_async_copy(o_buf.at[idx % 2, pl.ds(0, rows), :], o_hbm.at[pl.ds(q0, rows), :],
                                         out_sems.at[idx % 2])

        @pl.when(step == 0)
        def _():
            lse_ref[...] = jnp.zeros_like(lse_ref)
            for cp in loads(0):
                cp.start()

        for idx, (q0, rows, klen) in enumerate(tiles):
            @pl.when(step == idx)
            def _(idx=idx, q0=q0, rows=rows, klen=klen):
                for cp in loads(idx):
                    cp.wait()
                if idx + 1 < n_t:
                    for cp in loads(idx + 1):
                        cp.start()
                if idx >= 2:
                    store(idx - 2).wait()
                for hd in range(HEADS):
                    s = _masked_scores(q_buf[hd, q0:q0 + rows, :], k_buf[hd, 0:klen, :], rows, klen)
                    m = jnp.max(s, axis=-1, keepdims=True)
                    p = jnp.exp(s - m)
                    l = jnp.sum(p, axis=-1, keepdims=True)
                    o_buf[idx % 2, 0:rows, hd * V_HEAD:(hd + 1) * V_HEAD] = _nn(p.astype(BF16), v_buf[hd, 0:klen, :]) / l
                    grp, lane = _stat_slot(hd)
                    lse_ref[grp, q0:q0 + rows, lane:lane + 1] = m + jnp.log(l)
                store(idx).start()
                if idx == n_t - 1:
                    store(idx - 1).wait()
                    store(idx).wait()
                    finish_wout()

    hbm = pl.BlockSpec(memory_space=pl.ANY)
    return pl.pallas_call(
        body,
        name="attn_fwd",
        grid=(n_t,),
        in_specs=[hbm, hbm, hbm, _const(SHARD_OUT, D)],
        out_specs=[hbm, _const(STAT_GROUPS, N, 128), hbm],
        out_shape=[jax.ShapeDtypeStruct((N, HEADS * V_HEAD), F32), jax.ShapeDtypeStruct((STAT_GROUPS, N, 128), F32),
                   jax.ShapeDtypeStruct((D, D), BF16)],
        scratch_shapes=[pltpu.VMEM((HEADS, N, 256), BF16), pltpu.VMEM((HEADS, N, 256), BF16),
                        pltpu.VMEM((HEADS, N, V_HEAD), BF16), pltpu.VMEM((2, TQ, HEADS * V_HEAD), F32),
                        pltpu.VMEM((SHARD_OUT, D), BF16),
                        pltpu.SemaphoreType.DMA((3, 2)), pltpu.SemaphoreType.DMA((2,))]
        + [pltpu.SemaphoreType.DMA((3,))] * 4 + [pltpu.SemaphoreType.DMA],
        compiler_params=_cparams(dimension_semantics=("arbitrary",), collective_id=1),
    )(q, k, v, wout_s)


def _inv_count(row0, rows, w):
    row = row0 + lax.broadcasted_iota(jnp.int32, (rows, 1), 0)
    return 1.0 / jnp.clip(row - (PAD - 1), 1, w).astype(F32)


def _mid(h, tgt, pool_in, pool_gate, attn_gate, attn, pool_w, pool_scale, wout, gf):
    tr = ROWS_MID
    per = tr // HALO
    ng = len(POOL_WINDOWS)

    def body(h_ref, t_ref, pin_ref, halo_ref, pg_ref, ag_ref, at_ref, pw_ref, ps_ref, wout_ref, gf_ref,
             dh2_ref, do_ref, delta_ref, dag_ref, dpg_ref, dpl_ref, dwout_ref, dpw_ref, dps_ref, dgf_ref, loss_ref):
        i = pl.program_id(0)

        @pl.when(i == 0)
        def _():
            dwout_ref[...] = jnp.zeros_like(dwout_ref)
            dpw_ref[...] = jnp.zeros_like(dpw_ref)
            dps_ref[...] = jnp.zeros_like(dps_ref)
            dgf_ref[...] = jnp.zeros_like(dgf_ref)
            loss_ref[...] = jnp.zeros_like(loss_ref)

        row0 = i * tr
        real = (row0 + lax.broadcasted_iota(jnp.int32, (tr, 1), 0)) >= HEAD_ROWS
        h = h_ref[...]

        halo = jnp.where(i > 0, halo_ref[...], 0.0)
        ext = jnp.concatenate([halo, pin_ref[...]], axis=0)
        pooled = []
        for g, w in enumerate(POOL_WINDOWS):
            e = ext[:, g * POOL_GROUP:(g + 1) * POOL_GROUP]
            acc = e
            shift = 1
            while shift < w:
                acc = acc + pltpu.roll(acc, shift, 0)
                shift *= 2
            pooled.append((acc[HALO:] * _inv_count(row0, tr, w) - e[HALO:]).astype(BF16))
        pw = [pw_ref[g].astype(BF16) for g in range(ng)]
        mixed = jnp.concatenate([_nn(pooled[g], pw[g]) for g in range(ng)], axis=1)
        ps = ps_ref[...]
        mixed_s = mixed * ps
        pg = pg_ref[...]
        sig_p = _sigmoid(pg)
        silu_p = pg * sig_p
        pool_out = (silu_p * mixed_s).astype(BF16)
        ag = ag_ref[...]
        sig_a = _sigmoid(ag)
        silu_a = ag * sig_a
        at = at_ref[...]
        attn_out = (silu_a * at).astype(BF16)
        cat = jnp.concatenate([pool_out, attn_out], axis=1)
        h2 = h + _nn(cat, wout_ref[...])

        r2 = lax.rsqrt(jnp.mean(h2 * h2, axis=-1, keepdims=True) + EPS)
        n2 = h2 * r2
        gfv = gf_ref[...]
        err = jnp.where(real, n2 * gfv - t_ref[...], 0.0)
        loss_ref[...] += jnp.sum(jnp.sum(err * err, axis=-1, keepdims=True), axis=0, keepdims=True) * (0.5 / D)
        dy = err * (1.0 / D)
        dgf_ref[...] += jnp.sum(dy * n2, axis=0, keepdims=True)
        dn = dy * gfv
        dh2 = r2 * (dn - n2 * jnp.mean(dn * n2, axis=-1, keepdims=True))
        dh2_ref[...] = dh2
        dh2b = dh2.astype(BF16)

        dwout_ref[...] += _tn(cat, dh2b)
        dcat = _nt(dh2b, wout_ref[...])
        dpo = dcat[:, 0:D_POOL]
        dao = dcat[:, D_POOL:D]
        do = dao * silu_a
        prod = do * at
        delta_ref[...] = jnp.zeros_like(delta_ref)
        for hd in range(HEADS):
            grp, lane = _stat_slot(hd)
            cols = slice(hd * V_HEAD, (hd + 1) * V_HEAD)
            do_ref[grp, :, lane * V_HEAD:(lane + 1) * V_HEAD] = do[:, cols].astype(BF16)
            delta_ref[grp, :, lane:lane + 1] = jnp.sum(prod[:, cols], axis=-1, keepdims=True)
        dag_ref[...] = (dao * at * (sig_a * (1.0 + ag * (1.0 - sig_a)))).astype(BF16)
        dmixed_s = dpo * silu_p
        dpg_ref[...] = (dpo * mixed_s * (sig_p * (1.0 + pg * (1.0 - sig_p)))).astype(BF16)
        dps_ref[...] += jnp.sum(dmixed_s * mixed, axis=0, keepdims=True)
        dmixed = (dmixed_s * ps).astype(BF16)
        dpl = []
        for g in range(ng):
            dm = dmixed[:, g * POOL_GROUP:(g + 1) * POOL_GROUP]
            dpl.append(_nt(dm, pw[g]))
            dpw_ref[g] += _tn(pooled[g], dm)
        dpl_ref[...] = jnp.concatenate(dpl, axis=1)

    halo_spec = pl.BlockSpec((HALO, D_POOL), lambda i: (jnp.maximum(i * per - 1, 0), 0))
    return pl.pallas_call(
        body,
        name="mid",
        grid=(N // tr,),
        in_specs=[
            _rows(D, tr), _rows(D, tr), _rows(D_POOL, tr), halo_spec, _rows(D_POOL, tr), _rows(D_POOL, tr),
            _rows(D_POOL, tr), _const(ng, POOL_GROUP, POOL_GROUP), _const(1, D_POOL), _const(D, D), _const(1, D),
        ],
        out_specs=[
            _rows(D, tr), pl.BlockSpec((STAT_GROUPS, tr, HEADS_PER_STEP_BWD * V_HEAD), lambda i: (0, i, 0)),
            pl.BlockSpec((STAT_GROUPS, tr, 128), lambda i: (0, i, 0)),
            _rows(D_POOL, tr), _rows(D_POOL, tr), _rows(D_POOL, tr),
            _const(D, D), _const(ng, POOL_GROUP, POOL_GROUP), _const(1, D_POOL), _const(1, D), _const(1, 128),
        ],
        out_shape=[
            jax.ShapeDtypeStruct((N, D), F32), jax.ShapeDtypeStruct((STAT_GROUPS, N, HEADS_PER_STEP_BWD * V_HEAD), BF16),
            jax.ShapeDtypeStruct((STAT_GROUPS, N, 128), F32),
            jax.ShapeDtypeStruct((N, D_POOL), BF16), jax.ShapeDtypeStruct((N, D_POOL), BF16),
            jax.ShapeDtypeStruct((N, D_POOL), F32), jax.ShapeDtypeStruct((D, D), F32),
            jax.ShapeDtypeStruct((ng, POOL_GROUP, POOL_GROUP), F32),
            jax.ShapeDtypeStruct((1, D_POOL), F32), jax.ShapeDtypeStruct((1, D), F32), jax.ShapeDtypeStruct((1, 128), F32),
        ],
        compiler_params=_cparams(dimension_semantics=("arbitrary",)),
    )(h, tgt, pool_in, pool_in, pool_gate, attn_gate, attn, pool_w, pool_scale, wout, gf)


def _unrope(dy, cosv, sinv):
    return dy * cosv + _swap64(dy * sinv) * _low_lanes()


def _attn_bwd(q, k, v, do, lse, delta, cosf, sinf, dwout):
    tiles = _attn_tiles()
    hp = HEADS_PER_STEP_BWD
    n_g = HEADS // hp
    n_t = len(tiles)
    half = SHARD_OUT // 2
    swap_at, send_at, sum_at = (0, 3), (0, 5), (n_g - 1, n_t // 2)

    def body(q_hbm, k_hbm, v_hbm, do_hbm, lse_ref, delta_ref, cos_ref, sin_ref, dwout_hbm, dq_hbm, dkv_ref, dkr_ref,
             gwout_ref, q_buf, k_buf, v_buf, do_buf, dq_buf, dk_acc, dv_acc, own_w, sib_w, stage_w, recv_w, gw_buf,
             in_sems, out_sems, ow_sems, d2d_send, d2d_recv, ici_send, ici_recv, fin_send, fin_recv):
        grp = pl.program_id(0)
        step = pl.program_id(1)
        heads = pl.ds(grp * hp, hp)
        x, y, c = lax.axis_index("x"), lax.axis_index("y"), lax.axis_index("c")
        sibling = (x, y, 1 - c)

        def chip_of(rel):
            fx, fy = _CHIP_RELS[rel]
            return 2 * (x ^ fx) + (y ^ fy)

        def piece(chip, core):
            return dwout_hbm.at[pl.ds(pl.multiple_of(SHARD_OUT * chip + half * core, half), half), :]

        def own_load(rel):
            return pltpu.make_async_copy(piece(chip_of(rel), c), own_w.at[rel], ow_sems.at[rel])

        def d2d_copy(rel):
            return _remote(piece(chip_of(rel), 1 - c), sib_w.at[rel], d2d_send.at[rel], d2d_recv.at[rel], sibling)

        def ici_copy(rel):
            fx, fy = _CHIP_RELS[rel]
            return _remote(stage_w.at[rel - 1], recv_w.at[rel - 1], ici_send.at[rel - 1], ici_recv.at[rel - 1],
                           (x ^ fx, y ^ fy, c))

        def fin_copy(core):
            spot = gw_buf.at[pl.ds(pl.multiple_of(half * core, half), half), :]
            return _remote(spot, spot, fin_send.at[0], fin_recv.at[0], sibling)

        @pl.when((grp == 0) & (step == 0))
        def _():
            _peer_signal(x, y, c)
            for rel in SEND_ORDER + (0,):
                own_load(rel).start()

        @pl.when((grp == swap_at[0]) & (step == swap_at[1]))
        def _():
            _peer_wait()
            for rel in SEND_ORDER + (0,):
                d2d_copy(rel).start()

        @pl.when((grp == send_at[0]) & (step == send_at[1]))
        def _():
            for rel in SEND_ORDER:
                own_load(rel).wait()
                d2d_copy(rel).wait_recv()
                stage_w[rel - 1] = (own_w[rel] + sib_w[rel]).astype(BF16)
                ici_copy(rel).start()

        @pl.when((grp == sum_at[0]) & (step == sum_at[1]))
        def _():
            own_load(0).wait()
            d2d_copy(0).wait_recv()
            total = own_w[0] + sib_w[0]
            for rel in (1, 2, 3):
                ici_copy(rel).wait_recv()
                total = total + recv_w[rel - 1].astype(F32)
            gw_buf[pl.ds(pl.multiple_of(half * c, half), half), :] = total
            fin_copy(c).start()

        def finish_dwout():
            fin_copy(1 - c).wait_recv()
            for rel in (0, 1, 2, 3):
                d2d_copy(rel).wait_send()
            for rel in (1, 2, 3):
                ici_copy(rel).wait_send()
            fin_copy(c).wait_send()
            gwout_ref[...] = gw_buf[...]

        def loads(g, idx):
            q0, rows, _ = tiles[idx]
            rs = pl.ds(q0, rows)
            par = (g * n_t + idx) % 2
            hs = pl.ds(g * hp, hp)
            pairs = ((q_hbm.at[hs, rs, :], q_buf.at[:, rs, :]), (k_hbm.at[hs, rs, :], k_buf.at[:, rs, :]),
                     (v_hbm.at[hs, rs, :], v_buf.at[:, rs, :]), (do_hbm.at[g, rs, :], do_buf.at[rs, :]))
            return [pltpu.make_async_copy(src, dst, in_sems.at[a, par]) for a, (src, dst) in enumerate(pairs)]

        def store(idx):
            q0, rows, _ = tiles[idx]
            return pltpu.make_async_copy(dq_buf.at[idx % 2, :, pl.ds(0, rows), :], dq_hbm.at[heads, pl.ds(q0, rows), :],
                                         out_sems.at[idx % 2])

        @pl.when(step == 0)
        def _():
            dk_acc[...] = jnp.zeros_like(dk_acc)
            dv_acc[...] = jnp.zeros_like(dv_acc)

        @pl.when((step == 0) & (grp == 0))
        def _():
            dkr_ref[...] = jnp.zeros_like(dkr_ref)
            for cp in loads(grp, 0):
                cp.start()

        for idx, (q0, rows, klen) in enumerate(tiles):
            @pl.when(step == idx)
            def _(idx=idx, q0=q0, rows=rows, klen=klen):
                for cp in loads(grp, idx):
                    cp.wait()
                if idx + 1 < n_t:
                    for cp in loads(grp, idx + 1):
                        cp.start()
                if idx >= 2:
                    store(idx - 2).wait()
                qs = pl.ds(q0, rows)
                for hd in range(hp):
                    qv = q_buf[hd, qs, :]
                    kv = k_buf[hd, 0:klen, :]
                    p = jnp.exp(_masked_scores(qv, kv, rows, klen) - lse_ref[0, qs, hd:hd + 1])
                    dob = do_buf[qs, hd * V_HEAD:(hd + 1) * V_HEAD]
                    ds = (p * (_nt(dob, v_buf[hd, 0:klen, :]) - delta_ref[0, qs, hd:hd + 1])).astype(BF16)
                    dq = _nn(ds, kv) * SCALE
                    dq_buf[idx % 2, hd, 0:rows, 0:QK_NOPE] = dq[:, 0:QK_NOPE].astype(BF16)
                    dq_buf[idx % 2, hd, 0:rows, QK_NOPE:] = _unrope(dq[:, QK_NOPE:], cos_ref[qs, :], sin_ref[qs, :]).astype(BF16)
                    dk_acc[hd, 0:klen, :] += _tn(ds, qv)
                    dv_acc[hd, 0:klen, :] += _tn(p.astype(BF16), dob)
                store(idx).start()

        @pl.when(step == n_t - 1)
        def _():
            @pl.when(grp + 1 < n_g)
            def _():
                for cp in loads(grp + 1, 0):
                    cp.start()

            for hd in range(hp):
                dkv_ref[hd, :, 0:QK_NOPE] = dk_acc[hd, :, 0:QK_NOPE].astype(BF16)
                dkv_ref[hd, :, QK_NOPE:] = dv_acc[hd].astype(BF16)
                dkr_ref[...] += dk_acc[hd, :, QK_NOPE:]
            store(n_t - 2).wait()
            store(n_t - 1).wait()

            @pl.when(grp == n_g - 1)
            def _():
                finish_dwout()

    hbm = pl.BlockSpec(memory_space=pl.ANY)
    stat = pl.BlockSpec((1, N, 128), lambda g, t: (g, 0, 0), pipeline_mode=pl.Buffered(1))
    piece_f32 = lambda lead: pltpu.VMEM((lead, half, D), F32)
    piece_bf16 = lambda lead: pltpu.VMEM((lead, half, D), BF16)
    return pl.pallas_call(
        body,
        name="attn_bwd",
        grid=(n_g, n_t),
        in_specs=[hbm, hbm, hbm, hbm, stat, stat, _const(N, 128), _const(N, 128), hbm],
        out_specs=[hbm, pl.BlockSpec((hp, N, 256), lambda g, t: (g, 0, 0), pipeline_mode=pl.Buffered(1)), _const(N, 128),
                   _const(SHARD_OUT, D)],
        out_shape=[
            jax.ShapeDtypeStruct((HEADS, N, 256), BF16), jax.ShapeDtypeStruct((HEADS, N, 256), BF16),
            jax.ShapeDtypeStruct((N, 128), F32), jax.ShapeDtypeStruct((SHARD_OUT, D), F32),
        ],
        scratch_shapes=[pltpu.VMEM((hp, N, 256), BF16), pltpu.VMEM((hp, N, 256), BF16), pltpu.VMEM((hp, N, V_HEAD), BF16),
                        pltpu.VMEM((N, hp * V_HEAD), BF16), pltpu.VMEM((2, hp, TQ, 256), BF16),
                        pltpu.VMEM((hp, N, 256), F32), pltpu.VMEM((hp, N, V_HEAD), F32),
                        piece_f32(CHIPS), piece_f32(CHIPS), piece_bf16(3), piece_bf16(3), pltpu.VMEM((SHARD_OUT, D), F32),
                        pltpu.SemaphoreType.DMA((4, 2)), pltpu.SemaphoreType.DMA((2,)), pltpu.SemaphoreType.DMA((CHIPS,)),
                        pltpu.SemaphoreType.DMA((CHIPS,)), pltpu.SemaphoreType.DMA((CHIPS,)),
                        pltpu.SemaphoreType.DMA((3,)), pltpu.SemaphoreType.DMA((3,)),
                        pltpu.SemaphoreType.DMA((1,)), pltpu.SemaphoreType.DMA((1,))],
        compiler_params=_cparams(dimension_semantics=("arbitrary", "arbitrary"), collective_id=2),
    )(q, k, v, do, lse, delta, cosf, sinf, dwout)


def _bwd_in(h, dh2, dq, dkv, dkr, cq, ckv, dpl, dpg, dag, norm_g, win, gq, wq, gkv, wkv, cosf, sinf, adam_out):
    tr = ROWS_BWD
    nb = N // tr
    per = tr // HALO
    lead = HEAD_ROWS
    adam_rows = SHARD_OUT // nb

    def body(h_ref, dh2_ref, dq_ref, dkv_ref, dkr_ref, cq_ref, ckv_ref, dpl_ref, halo_ref, dpg_ref, dag_ref,
             g_ref, win_ref, gq_ref, wq_ref, gkv_ref, wkv_ref, cos_ref, sin_ref, aw_ref, ag_ref, am_ref, av_ref,
             gx_ref, dmeta_ref, dsl_ref, dwq_ref, dwkv_ref, dg_ref, dgq_ref, dgkv_ref, ago_ref, ad_ref, anm_ref, anv_ref,
             dh_buf, gx_sem):
        i = pl.program_id(0)
        grad_out = ag_ref[...]
        ago_ref[...] = grad_out
        ad_ref[...], anm_ref[...], anv_ref[...] = _adamw_math(aw_ref[...], grad_out, am_ref[...], av_ref[...])

        @pl.when(i == 0)
        def _():
            dwq_ref[...] = jnp.zeros_like(dwq_ref)
            dwkv_ref[...] = jnp.zeros_like(dwkv_ref)
            dg_ref[...] = jnp.zeros_like(dg_ref)
            dgq_ref[...] = jnp.zeros_like(dgq_ref)
            dgkv_ref[...] = jnp.zeros_like(dgkv_ref)

        row0 = i * tr
        h = h_ref[...]
        r = lax.rsqrt(jnp.mean(h * h, axis=-1, keepdims=True) + EPS)
        n = h * r
        gv = g_ref[...]
        cq = cq_ref[...]
        rq = lax.rsqrt(jnp.mean(cq * cq, axis=-1, keepdims=True) + EPS)
        nq = cq * rq
        gqv = gq_ref[...]
        cqn = (nq * gqv).astype(BF16)
        dcqn = jnp.zeros((tr, Q_LORA), F32)
        for hd in range(HEADS):
            dqf = dq_ref[hd]
            dcqn = dcqn + _nn(dqf, wq_ref[hd])
            dwq_ref[hd] += _tn(dqf, cqn)
        dgq_ref[...] += jnp.sum(dcqn * nq, axis=0, keepdims=True)
        dnq = dcqn * gqv
        dcq = rq * (dnq - nq * jnp.mean(dnq * nq, axis=-1, keepdims=True))

        ckv = ckv_ref[...]
        rkv = lax.rsqrt(jnp.mean(ckv * ckv, axis=-1, keepdims=True) + EPS)
        nkv = ckv * rkv
        gkvv = gkv_ref[...]
        ckvn = (nkv * gkvv).astype(BF16)
        dckvn = jnp.zeros((tr, KV_LORA), F32)
        for hd in range(HEADS):
            dkv = dkv_ref[hd]
            dckvn = dckvn + _nt(dkv, wkv_ref[hd])
            dwkv_ref[hd] += _tn(ckvn, dkv)
        dgkv_ref[...] += jnp.sum(dckvn * nkv, axis=0, keepdims=True)
        dnkv = dckvn * gkvv
        dckv = rkv * (dnkv - nkv * jnp.mean(dnkv * nkv, axis=-1, keepdims=True))
        dkr = _unrope(dkr_ref[...], cos_ref[...], sin_ref[...])

        cur = dpl_ref[...]
        halo = jnp.where(i < nb - 1, halo_ref[...], 0.0)
        dpi = []
        for g, w in enumerate(POOL_WINDOWS):
            sl = slice(g * POOL_GROUP, (g + 1) * POOL_GROUP)
            a = jnp.concatenate([cur[:, sl] * _inv_count(row0, tr, w), halo[:, sl] * _inv_count(row0 + tr, HALO, w)], axis=0)
            acc = a
            shift = 1
            while shift < w:
                acc = acc + pltpu.roll(acc, tr + HALO - shift, 0)
                shift *= 2
            dpi.append(acc[0:tr] - cur[:, sl])

        du = jnp.concatenate([t.astype(BF16) for t in dpi] + [dpg_ref[...]] + [t.astype(BF16) for t in (dcq, dckv, dkr)],
                             axis=1)
        dagb = dag_ref[...]
        by_row = jnp.concatenate(dpi + [dpg_ref[...].astype(F32), dcq, dckv, dkr[:, 0:QK_ROPE], dagb.astype(F32),
                                        jnp.zeros((tr, SHARD_PAD - SHARD_IN), F32)], axis=1)
        for chip in range(CHIPS):
            dsl_ref[chip] = by_row[:, SHARD_IN * chip:SHARD_IN * chip + SHARD_PAD].astype(BF16)
        dhn = _nn(du, win_ref[0:O_KR_END, :]) + _nn(dagb, win_ref[O_AG:D_IN, :])
        dg_ref[...] += jnp.sum(dhn * n, axis=0, keepdims=True)
        dn = dhn * gv
        dh = dh2_ref[...] + r * (dn - n * jnp.mean(dn * n, axis=-1, keepdims=True))

        first = pltpu.make_async_copy(dh_buf.at[pl.ds(lead, tr - lead), :], gx_ref.at[pl.ds(0, tr - lead), :], gx_sem)
        later = lambda step: pltpu.make_async_copy(
            dh_buf, gx_ref.at[pl.ds(pl.multiple_of(step * tr - lead, 16), tr), :], gx_sem)

        @pl.when(i == 1)
        def _():
            first.wait()

        @pl.when(i > 1)
        def _():
            later(i - 1).wait()

        dh_buf[...] = dh

        @pl.when(i == 0)
        def _():
            first.start()
            for chip in range(CHIPS):
                dmeta_ref[chip] = dh[PAD:HEAD_ROWS, chip * 256:(chip + 1) * 256]

        @pl.when(i > 0)
        def _():
            later(i).start()

        @pl.when(i == nb - 1)
        def _():
            later(i).wait()

    head = lambda w: pl.BlockSpec((HEADS, tr, w), lambda i: (0, i, 0))
    halo_spec = pl.BlockSpec((HALO, D_POOL), lambda i: (jnp.minimum((i + 1) * per, N // HALO - 1), 0))
    return pl.pallas_call(
        body,
        name="bwd_in",
        grid=(nb,),
        in_specs=[
            _rows(D, tr), _rows(D, tr), head(256), head(256), _rows(128, tr), _rows(Q_LORA, tr), _rows(KV_LORA, tr),
            _rows(D_POOL, tr), halo_spec, _rows(D_POOL, tr), _rows(D_POOL, tr),
            _const(1, D), _const(D_IN, D), _const(1, Q_LORA), _const(HEADS, 256, Q_LORA),
            _const(1, KV_LORA), _const(HEADS, KV_LORA, 256), _rows(128, tr), _rows(128, tr),
        ] + [_rows(D, adam_rows)] * 4,
        out_specs=[
            pl.BlockSpec(memory_space=pl.ANY), _const(CHIPS, N_META, 256),
            pl.BlockSpec((CHIPS, tr, SHARD_PAD), lambda i: (0, i, 0)), _const(HEADS, 256, Q_LORA),
            _const(HEADS, KV_LORA, 256), _const(1, D), _const(1, Q_LORA), _const(1, KV_LORA),
        ] + [_rows(D, adam_rows)] * 4,
        out_shape=[
            jax.ShapeDtypeStruct((S, D), F32), jax.ShapeDtypeStruct((CHIPS, N_META, 256), F32),
            jax.ShapeDtypeStruct((CHIPS, N, SHARD_PAD), BF16), jax.ShapeDtypeStruct((HEADS, 256, Q_LORA), F32),
            jax.ShapeDtypeStruct((HEADS, KV_LORA, 256), F32),
            jax.ShapeDtypeStruct((1, D), F32), jax.ShapeDtypeStruct((1, Q_LORA), F32), jax.ShapeDtypeStruct((1, KV_LORA), F32),
        ] + [jax.ShapeDtypeStruct((SHARD_OUT, D), F32)] * 4,
        scratch_shapes=[pltpu.VMEM((tr, D), F32), pltpu.SemaphoreType.DMA],
        compiler_params=_cparams(dimension_semantics=("arbitrary",)),
    )(h, dh2, dq, dkv, dkr, cq, ckv, dpl, dpl, dpg, dag, norm_g, win, gq, wq, gkv, wkv, cosf, sinf, *adam_out)


def _local_step(h, tgt, norm_g, win, gq, wq, gkv, wkv, pool_w, pool_scale, wout_s, m_wout_s, v_wout_s, gf, cosf, sinf):
    pool_in, pool_gate, cq, ckv, attn_gate, q, k, v, hn = _fwd_in(h, norm_g, win, gq, wq, gkv, wkv, cosf, sinf)
    attn, lse, wout = _attn_fwd(q, k, v, wout_s)
    dh2, do, delta, dag, dpg, dpl, dwout, dpw, dps, dgf, loss = _mid(
        h, tgt, pool_in, pool_gate, attn_gate, attn, pool_w, pool_scale, wout, gf)
    dq, dkv, dkr, gwout = _attn_bwd(q, k, v, do, lse, delta, cosf, sinf, dwout)
    gx, dmeta, dsl, dwq, dwkv, dg, dgq, dgkv, *r_out = _bwd_in(
        h, dh2, dq, dkv, dkr, cq, ckv, dpl, dpg, dag, norm_g, win, gq, wq, gkv, wkv, cosf, sinf,
        (wout_s, gwout, m_wout_s, v_wout_s))
    return dict(gx=gx, dmeta=dmeta, dsl=dsl, hn=hn, dwq=dwq, dwkv=dwkv, r_out=tuple(r_out), dg=dg, dgq=dgq,
                dgkv=dgkv, dpw=dpw, dps=dps, dgf=dgf, loss=loss)


_CHIP_RELS = ((0, 0), (1, 0), (0, 1), (1, 1))

_ARR_ROWS = (SHARD_IN, SHARD_OUT, 256, KV_LORA, N_META)
_ARR_COLS = (D, D, Q_LORA, 256, 256)
_PIECES = (
    (0, 0, 256, 0), (0, 256, SHARD_IN - 256, 1),
    (1, 0, 128, 0), (1, 128, 128, 1),
    (2, 0, 128, 0), (2, 128, 128, 1),
    (3, 0, 64, 0), (3, 64, 64, 1),
    (4, 0, N_META, 0),
)
_NP = len(_PIECES)
_PIECE_MAX = (256, 128, 128, 64, N_META)


def _gathered_at(refs, arr, chip, r0, n):
    if arr in (0, 1):
        return refs[arr].at[pl.ds(pl.multiple_of(_ARR_ROWS[arr] * chip + r0, 16), n), :]
    return refs[arr].at[chip, pl.ds(r0, n), :]


def _remote(src, dst, send_sem, recv_sem, to):
    return pltpu.make_async_remote_copy(src_ref=src, dst_ref=dst, send_sem=send_sem, recv_sem=recv_sem,
                                        device_id=to, device_id_type=MESH)


def _gather_weights(winT_s, wqT_s, wkv_s, meta_s, x2, tgt2):
    arrays = (0, 2, 3, 4)

    def body(win_ref, wq_ref, wkv_ref, meta_ref, x_ref, t_ref, win_o, wq_o, wkv_o, h_o, tp_o,
             s_win, s_wq, s_wkv, meta_all, head_buf, x_buf, t_buf, ici_send, ici_recv, fwd_send, fwd_recv,
             loc_sems, own_sems):
        x, y, c = lax.axis_index("x"), lax.axis_index("y"), lax.axis_index("c")
        me = 2 * x + y
        stage = (s_win, None, s_wq, s_wkv, meta_ref)
        outs = (win_o, None, wq_o, wkv_o, meta_all)

        _peer_signal(x, y, c)

        frames = pl.ds(HEAD_ROWS, S)
        loads = [pltpu.make_async_copy(x_ref, x_buf, loc_sems.at[0]), pltpu.make_async_copy(t_ref, t_buf, loc_sems.at[1])]
        local = [pltpu.make_async_copy(x_buf, h_o.at[frames, :], loc_sems.at[0]),
                 pltpu.make_async_copy(t_buf, tp_o.at[frames, :], loc_sems.at[1])]
        for cp in loads:
            cp.start()

        s_win[...] = win_ref[...].astype(BF16)
        s_wq[0:QK, :] = wq_ref[...].astype(BF16)
        s_wq[QK:256, :] = jnp.zeros((256 - QK, Q_LORA), BF16)
        s_wkv[...] = wkv_ref[...].astype(BF16)
        head_buf[...] = jnp.zeros_like(head_buf)
        zeros = pltpu.make_async_copy(head_buf, tp_o.at[pl.ds(0, HEAD_ROWS), :], loc_sems.at[2])
        zeros.start()

        def chip_of(rel):
            fx, fy = _CHIP_RELS[rel]
            return 2 * (x ^ fx) + (y ^ fy)

        def same_core_of(rel):
            fx, fy = _CHIP_RELS[rel]
            return (x ^ fx, y ^ fy, c)

        def ici_copy(rel, i, src_chip, to):
            arr, r0, n, _ = _PIECES[i]
            k = (rel - 1) * _NP + i
            return _remote(stage[arr].at[pl.ds(r0, n), :], _gathered_at(outs, arr, src_chip, r0, n),
                           ici_send.at[k], ici_recv.at[k], to)

        def fwd_copy(rel, i, to):
            arr, r0, n, _ = _PIECES[i]
            k = (rel - 1) * _NP + i
            place = _gathered_at(outs, arr, chip_of(rel), r0, n)
            return _remote(place, place, fwd_send.at[k], fwd_recv.at[k], to)

        _peer_wait()
        for core in (0, 1):
            @pl.when(c == core)
            def _(core=core):
                mine = [i for i in range(_NP) if _PIECES[i][3] == core and _PIECES[i][0] in arrays]
                theirs = [i for i in range(_NP) if _PIECES[i][3] != core and _PIECES[i][0] in arrays]
                order = (1, 2, 3)
                sends = [ici_copy(rel, i, me, same_core_of(rel)) for rel in order for i in mine]
                for cp in sends:
                    cp.start()
                for ld, st in zip(loads, local):
                    ld.wait()
                    st.start()
                own = [pltpu.make_async_copy(stage[arr], _gathered_at(outs, arr, me, 0, _ARR_ROWS[arr]), own_sems.at[arr])
                       for arr in arrays if arr != 4]
                for cp in own:
                    cp.start()
                meta_all[me] = meta_ref[...]
                for rel in order:
                    for i in mine:
                        ici_copy(rel, i, chip_of(rel), (x, y, c)).wait_recv()
                        fwd = fwd_copy(rel, i, (x, y, 1 - c))
                        fwd.start()
                        sends.append(fwd)
                for rel in order:
                    for i in theirs:
                        fwd_copy(rel, i, (x, y, c)).wait_recv()
                for cp in sends:
                    cp.wait_send()
                for cp in own:
                    cp.wait()

        zeros.wait()
        for chip in range(CHIPS):
            head_buf[PAD:HEAD_ROWS, chip * 256:(chip + 1) * 256] = meta_all[chip]
        head = pltpu.make_async_copy(head_buf, h_o.at[pl.ds(0, HEAD_ROWS), :], loc_sems.at[2])
        head.start()
        head.wait()
        for cp in local:
            cp.wait()

    vm = pl.BlockSpec(memory_space=pltpu.VMEM)
    hbm = pl.BlockSpec(memory_space=pl.ANY)
    return pl.pallas_call(
        body,
        name="gather_weights",
        in_specs=[vm] * 4 + [hbm] * 2,
        out_specs=[hbm] * 5,
        out_shape=[
            jax.ShapeDtypeStruct((D_IN, D), BF16),
            jax.ShapeDtypeStruct((CHIPS, 256, Q_LORA), BF16), jax.ShapeDtypeStruct((CHIPS, KV_LORA, 256), BF16),
            jax.ShapeDtypeStruct((N, D), F32), jax.ShapeDtypeStruct((N, D), F32),
        ],
        scratch_shapes=[pltpu.VMEM((_ARR_ROWS[a], _ARR_COLS[a]), BF16) for a in (0, 2, 3)]
        + [pltpu.VMEM((CHIPS, N_META, 256), F32), pltpu.VMEM((HEAD_ROWS, D), F32), pltpu.VMEM((S, D), F32),
           pltpu.VMEM((S, D), F32)]
        + [pltpu.SemaphoreType.DMA((3 * _NP,))] * 4 + [pltpu.SemaphoreType.DMA((3,)), pltpu.SemaphoreType.DMA((4,))],
        compiler_params=_cparams(collective_id=0),
    )(winT_s, wqT_s, wkv_s, meta_s, x2, tgt2)


_SM_ROWS = (len(POOL_WINDOWS) * POOL_GROUP, VEC_ROWS)
_SM_COLS = (POOL_GROUP, D)
_SM_PIECES = ((0, 0, 256, 0), (0, 256, 256, 1), (1, 0, VEC_ROWS, 0))
_NSP = len(_SM_PIECES)


def _reduce_grads(dsl, hn, dwq, dwkv, dmeta4, dpw, dg, dgf, dgq, dgkv, dps, loss):
    arrays = (0, 2, 3, 4)
    loaded = (2, 3, 4)
    shard_order = SEND_ORDER + (0,)

    def body(dsl_hbm, hn_hbm, dwq_ref, dwkv_ref, dmeta_ref, dpw_ref, dg_ref, dgf_ref, dgq_ref, dgkv_ref, dps_ref,
             loss_ref, gwin_o, gwq_o, gwkv_o, gmeta_o, gpw_o, gg_o, ggf_o, ggq_o, ggkv_o, gps_o, gloss_o,
             ow2, ow3, ow4, sb0, sb2, sb3, sb4, st0, st2, st3, st4, rc0, rc2, rc3, rc4,
             vec, sm_sb0, sm_sb1, sm_cs0, sm_cs1, sm_rc0, sm_rc1, vec_fin, slab_v, hn_v, dwin_buf, own0,
             own_sems, d2d_send, d2d_recv, ici_send, ici_recv, fin_send, fin_recv,
             swap_send, swap_recv, smi_send, smi_recv, smf_send, smf_recv, ld_sems):
        x, y, c = lax.axis_index("x"), lax.axis_index("y"), lax.axis_index("c")
        me = 2 * x + y
        _peer_signal(x, y, c)
        grads = (None, None, dwq_ref, dwkv_ref, dmeta_ref)
        outs = (gwin_o, None, gwq_o, gwkv_o, gmeta_o)
        own_buf = (None, None, ow2, ow3, ow4)
        sib_buf = (sb0, None, sb2, sb3, sb4)
        stage = (st0, None, st2, st3, st4)
        recv = (rc0, None, rc2, rc3, rc4)
        sm_mine = (dpw_ref, vec)
        sm_sib = (sm_sb0, sm_sb1)
        sm_chip = (sm_cs0, sm_cs1)
        sm_recv = (sm_rc0, sm_rc1)
        sm_out = (gpw_o, vec_fin)
        sibling = (x, y, 1 - c)

        def chip_of(rel):
            fx, fy = _CHIP_RELS[rel]
            return 2 * (x ^ fx) + (y ^ fy)

        def same_core_of(rel):
            fx, fy = _CHIP_RELS[rel]
            return (x ^ fx, y ^ fy, c)

        hn_load = pltpu.make_async_copy(hn_hbm, hn_v, ld_sems.at[CHIPS])

        def slab_load(rel):
            return pltpu.make_async_copy(dsl_hbm.at[chip_of(rel)], slab_v.at[rel], ld_sems.at[rel])

        hn_load.start()
        slab_load(shard_order[0]).start()

        def slot(bufs, i, idx):
            arr, _, n, _ = _PIECES[i]
            return bufs[arr].at[idx, pl.ds(0, n), :]

        def own_load(rel, i):
            arr, r0, n, _ = _PIECES[i]
            return pltpu.make_async_copy(_gathered_at(grads, arr, chip_of(rel), r0, n), slot(own_buf, i, rel),
                                         own_sems.at[rel * _NP + i])

        def d2d_copy(rel, i):
            arr, r0, n, _ = _PIECES[i]
            k = rel * _NP + i
            return _remote(_gathered_at(grads, arr, chip_of(rel), r0, n), slot(sib_buf, i, rel),
                           d2d_send.at[k], d2d_recv.at[k], sibling)

        def ici_copy(rel, i):
            k = (rel - 1) * _NP + i
            return _remote(slot(stage, i, rel - 1), slot(recv, i, rel - 1), ici_send.at[k], ici_recv.at[k],
                           same_core_of(rel))

        def fin_copy(i):
            arr, r0, n, _ = _PIECES[i]
            place = outs[arr].at[pl.ds(r0, n), :]
            return _remote(place, place, fin_send.at[i], fin_recv.at[i], sibling)

        def sm_ici_copy(rel, j):
            blk, r0, n, _ = _SM_PIECES[j]
            k = (rel - 1) * _NSP + j
            return _remote(sm_chip[blk].at[pl.ds(r0, n), :], sm_recv[blk].at[rel - 1, pl.ds(r0, n), :],
                           smi_send.at[k], smi_recv.at[k], same_core_of(rel))

        def sm_fin_copy(j):
            blk, r0, n, _ = _SM_PIECES[j]
            place = sm_out[blk].at[pl.ds(r0, n), :]
            return _remote(place, place, smf_send.at[j], smf_recv.at[j], sibling)

        vec[...] = jnp.zeros_like(vec)
        vec[0:1, :] = dg_ref[...]
        vec[1:2, :] = dgf_ref[...]
        vec[2:3, V_GQ:V_GQ + Q_LORA] = dgq_ref[...]
        vec[2:3, V_GKV:V_GKV + KV_LORA] = dgkv_ref[...]
        vec[2:3, V_PS:V_PS + D_POOL] = dps_ref[...]
        vec[2:3, V_LOSS:D] = loss_ref[...]
        _peer_wait()
        swaps = [_remote(sm_mine[b], sm_sib[b], swap_send.at[b], swap_recv.at[b], sibling) for b in (0, 1)]
        for cp in swaps:
            cp.start()

        for core in (0, 1):
            @pl.when(c == core)
            def _(core=core):
                mine = [i for i in range(_NP) if _PIECES[i][3] == core and _PIECES[i][0] in loaded]
                theirs = [i for i in range(_NP) if _PIECES[i][3] != core and _PIECES[i][0] in loaded]
                i0 = next(i for i in range(_NP) if _PIECES[i][0] == 0 and _PIECES[i][3] == core)
                j0 = next(i for i in range(_NP) if _PIECES[i][0] == 0 and _PIECES[i][3] != core)
                sm_mine_p = [j for j in range(_NSP) if _SM_PIECES[j][3] == core]
                sm_theirs_p = [j for j in range(_NSP) if _SM_PIECES[j][3] != core]
                sends = list(swaps)

                for rel in shard_order:
                    for i in theirs:
                        cp = d2d_copy(rel, i)
                        cp.start()
                        sends.append(cp)
                    for i in mine:
                        own_load(rel, i).start()

                def piece_rows(i):
                    return pl.ds(_PIECES[i][1], _PIECES[i][2])

                def form(rel, i):
                    r0, n = _PIECES[i][1], _PIECES[i][2]
                    dwin_buf[rel, r0:r0 + n, :] = _tn(slab_v[rel, :, r0:r0 + _PIECE_MAX[0]], hn_v[...])[0:n, :]

                def d2d0(rel, i):
                    return _remote(dwin_buf.at[rel, piece_rows(i), :], slot(sib_buf, i, rel),
                                   d2d_send.at[rel * _NP + i], d2d_recv.at[rel * _NP + i], sibling)

                def settle(rel):
                    d2d0(rel, i0).wait_recv()
                    total = dwin_buf[rel, piece_rows(i0), :] + slot(sib_buf, i0, rel)[...]
                    if rel == 0:
                        own0[0:_PIECES[i0][2], :] = total
                    else:
                        slot(stage, i0, rel - 1)[...] = total.astype(BF16)
                        cp = ici_copy(rel, i0)
                        cp.start()
                        sends.append(cp)

                hn_load.wait()
                for n, rel in enumerate(shard_order):
                    slab_load(rel).wait()
                    if n == 0:
                        for later in shard_order[1:]:
                            slab_load(later).start()
                    form(rel, j0)
                    cp = d2d0(rel, j0)
                    cp.start()
                    sends.append(cp)
                    if n > 0:
                        settle(shard_order[n - 1])
                    form(rel, i0)
                settle(shard_order[-1])

                for rel in SEND_ORDER:
                    for i in mine:
                        arr, r0, n, _ = _PIECES[i]
                        own_load(rel, i).wait()
                        d2d_copy(rel, i).wait_recv()
                        total = slot(own_buf, i, rel)[...] + slot(sib_buf, i, rel)[...]
                        slot(stage, i, rel - 1)[...] = total.astype(stage[arr].dtype)
                        cp = ici_copy(rel, i)
                        cp.start()
                        sends.append(cp)

                for b in (0, 1):
                    swaps[b].wait_recv()
                    sm_chip[b][...] = sm_mine[b][...] + sm_sib[b][...]
                for rel in SEND_ORDER:
                    for j in sm_mine_p:
                        cp = sm_ici_copy(rel, j)
                        cp.start()
                        sends.append(cp)

                for i in mine:
                    arr, r0, n, _ = _PIECES[i]
                    own_load(0, i).wait()
                    d2d_copy(0, i).wait_recv()
                    total = slot(own_buf, i, 0)[...] + slot(sib_buf, i, 0)[...]
                    for rel in (1, 2, 3):
                        ici_copy(rel, i).wait_recv()
                        total = total + slot(recv, i, rel - 1)[...].astype(F32)
                    outs[arr][pl.ds(r0, n), :] = total
                    cp = fin_copy(i)
                    cp.start()
                    sends.append(cp)
                total = own0[0:_PIECES[i0][2], :]
                for rel in (1, 2, 3):
                    ici_copy(rel, i0).wait_recv()
                    total = total + slot(recv, i0, rel - 1)[...].astype(F32)
                outs[0][pl.ds(_PIECES[i0][1], _PIECES[i0][2]), :] = total
                cp = fin_copy(i0)
                cp.start()
                sends.append(cp)

                for j in sm_mine_p:
                    blk, r0, n, _ = _SM_PIECES[j]
                    for rel in (1, 2, 3):
                        sm_ici_copy(rel, j).wait_recv()
                    total = jnp.zeros((n, _SM_COLS[blk]), F32)
                    for chip in range(CHIPS):
                        flips = chip ^ me
                        rel = jnp.where(flips == 2, 1, jnp.where(flips == 1, 2, flips))
                        theirs_rows = sm_recv[blk][jnp.maximum(rel - 1, 0), pl.ds(r0, n), :]
                        total = total + jnp.where(rel == 0, sm_chip[blk][pl.ds(r0, n), :], theirs_rows)
                    sm_out[blk][pl.ds(r0, n), :] = total
                    cp = sm_fin_copy(j)
                    cp.start()
                    sends.append(cp)

                for i in theirs + [j0]:
                    fin_copy(i).wait_recv()
                for j in sm_theirs_p:
                    sm_fin_copy(j).wait_recv()
                for cp in sends:
                    cp.wait_send()

        gg_o[...] = vec_fin[0:1, :]
        ggf_o[...] = vec_fin[1:2, :]
        ggq_o[...] = vec_fin[2:3, V_GQ:V_GQ + Q_LORA]
        ggkv_o[...] = vec_fin[2:3, V_GKV:V_GKV + KV_LORA]
        gps_o[...] = vec_fin[2:3, V_PS:V_PS + D_POOL]
        gloss_o[...] = vec_fin[2:3, V_LOSS:D]

    vm = pl.BlockSpec(memory_space=pltpu.VMEM)
    piece_buf = lambda lead, dtype, which=arrays: [
        pltpu.VMEM((lead, _PIECE_MAX[a], _ARR_COLS[a]), F32 if a == 4 else dtype) for a in which]
    sm_buf = lambda *lead: [pltpu.VMEM(lead + (_SM_ROWS[b], _SM_COLS[b]), F32) for b in (0, 1)]
    dma = lambda n: [pltpu.SemaphoreType.DMA((n,))] * 2
    return pl.pallas_call(
        body,
        name="reduce_grads",
        in_specs=[pl.BlockSpec(memory_space=pl.ANY)] * 4 + [vm] * 8,
        out_specs=[vm] * 11,
        out_shape=[jax.ShapeDtypeStruct((_ARR_ROWS[a], _ARR_COLS[a]), F32) for a in arrays]
        + [jax.ShapeDtypeStruct((_SM_ROWS[0], _SM_COLS[0]), F32), jax.ShapeDtypeStruct((1, D), F32),
           jax.ShapeDtypeStruct((1, D), F32), jax.ShapeDtypeStruct((1, Q_LORA), F32),
           jax.ShapeDtypeStruct((1, KV_LORA), F32), jax.ShapeDtypeStruct((1, D_POOL), F32),
           jax.ShapeDtypeStruct((1, 128), F32)],
        scratch_shapes=piece_buf(CHIPS, F32, loaded) + piece_buf(CHIPS, F32) + piece_buf(3, BF16) + piece_buf(3, BF16)
        + [pltpu.VMEM((VEC_ROWS, D), F32)] + sm_buf() + sm_buf() + sm_buf(3) + [pltpu.VMEM((VEC_ROWS, D), F32)]
        + [pltpu.VMEM((CHIPS, N, SHARD_PAD), BF16), pltpu.VMEM((N, D), BF16),
           pltpu.VMEM((CHIPS, SHARD_PAD, D), F32), pltpu.VMEM((_PIECE_MAX[0], D), F32)]
        + [pltpu.SemaphoreType.DMA((CHIPS * _NP,))]
        + dma(CHIPS * _NP) + dma(3 * _NP) + dma(_NP) + dma(2) + dma(3 * _NSP) + dma(_NSP)
        + [pltpu.SemaphoreType.DMA((CHIPS + 1,))],
        compiler_params=_cparams(collective_id=3),
    )(dsl, hn, dwq, dwkv, dmeta4, dpw, dg, dgf, dgq, dgkv, dps, loss)


def _adamw_math(w, g, m, v):
    m = B1 * m + (1.0 - B1) * g
    v = B2 * v + (1.0 - B2) * (g * g)
    m_hat = m / C1
    v_hat = v / C2
    delta = -LR * (m_hat / (jnp.sqrt(v_hat) + ADAM_EPS) + WD * w)
    return delta, m, v


def _adamw(big, block_rows, groups):
    rows, cols = big[0].shape
    n = len(groups)

    def body(*refs):
        w_ref, g_ref, m_ref, v_ref = refs[0:4]
        small_in = refs[4:4 + 4 * n]
        go_ref, d_ref, nm_ref, nv_ref = refs[4 + 4 * n:8 + 4 * n]
        small_out = refs[8 + 4 * n:]
        g = g_ref[...]
        go_ref[...] = g
        d_ref[...], nm_ref[...], nv_ref[...] = _adamw_math(w_ref[...], g, m_ref[...], v_ref[...])

        @pl.when(pl.program_id(0) == 0)
        def _():
            for t in range(n):
                sw_ref, sg_ref, sm_ref, sv_ref = small_in[4 * t:4 * t + 4]
                sg = sg_ref[0:sw_ref.shape[0], :]
                small_out[4 * t][...] = sg
                small_out[4 * t + 1][...], small_out[4 * t + 2][...], small_out[4 * t + 3][...] = _adamw_math(
                    sw_ref[...], sg, sm_ref[...], sv_ref[...])

    spec = pl.BlockSpec((block_rows, cols), lambda i: (i, 0))
    vm = pl.BlockSpec(memory_space=pltpu.VMEM)
    outs = pl.pallas_call(
        body,
        name="adamw",
        grid=(rows // block_rows,),
        in_specs=[spec] * 4 + [vm] * (4 * n),
        out_specs=[spec] * 4 + [vm] * (4 * n),
        out_shape=[jax.ShapeDtypeStruct(big[0].shape, F32)] * 4
        + [jax.ShapeDtypeStruct(grp[0].shape, F32) for grp in groups for _ in range(4)],
        compiler_params=_cparams(dimension_semantics=("arbitrary",)),
    )(*big, *[a for grp in groups for a in grp])
    return tuple(outs[0:4]), [tuple(outs[4 + 4 * t:8 + 4 * t]) for t in range(n)]


def _rope_tables():
    half = QK_ROPE // 2
    f32 = np.float32
    inv_freq = (f32(1.0) / (f32(ROPE_THETA) ** (np.arange(half, dtype=f32) / f32(half)))).astype(f32)
    pos = np.arange(N, dtype=f32) - f32(PAD)
    ang = (pos[:, None] * inv_freq[None, :]).astype(f32)
    cos, sin = np.cos(ang).astype(f32), np.sin(ang).astype(f32)
    zero = np.zeros((N, 128 - QK_ROPE), f32)
    return jnp.asarray(np.concatenate([cos, cos, zero], axis=1)), jnp.asarray(np.concatenate([-sin, sin, zero], axis=1))


def kernel(x, meta_tokens, norm_g, w_in, q_norm_g, w_q_b, kv_norm_g, w_kv_b, pool_w, pool_scale, w_out, final_norm_g, loss_target, m_meta_tokens, m_norm_g, m_w_in, m_q_norm_g, m_w_q_b, m_kv_norm_g, m_w_kv_b, m_pool_w, m_pool_scale, m_w_out, m_final_norm_g, v_meta_tokens, v_norm_g, v_w_in, v_q_norm_g, v_w_q_b, v_kv_norm_g, v_w_kv_b, v_pool_w, v_pool_scale, v_w_out, v_final_norm_g):
    tr = lambda a: a[0].T
    win, wq, wkv, h, tgt = _gather_weights(tr(w_in), tr(w_q_b), w_kv_b[0], meta_tokens, x[0], loss_target[0])
    cosf, sinf = _rope_tables()
    gf = final_norm_g.reshape(1, D)

    part = _local_step(h, tgt, norm_g, win, q_norm_g, wq, kv_norm_g, wkv, pool_w[0], pool_scale, w_out[0], m_w_out[0],
                       v_w_out[0], gf, cosf, sinf)

    pw2 = lambda a: a.reshape(len(POOL_WINDOWS) * POOL_GROUP, POOL_GROUP)
    gwinT, gwqT, gwkv, gmeta, gpw, gg, ggf, ggq, ggkv, gps, gloss = _reduce_grads(
        part["dsl"], part["hn"], part["dwq"], part["dwkv"], part["dmeta"], pw2(part["dpw"]), part["dg"],
        part["dgf"], part["dgq"], part["dgkv"], part["dps"], part["loss"])

    r_out = part["r_out"]
    fn2 = lambda a: a.reshape(1, D)
    r_in, (r_meta, r_norm, r_gq, r_wq, r_gkv, r_wkv, r_pw, r_ps, r_fn) = _adamw((tr(w_in), gwinT, tr(m_w_in), tr(v_w_in)), 248, [
        (meta_tokens, gmeta, m_meta_tokens, v_meta_tokens),
        (norm_g, gg, m_norm_g, v_norm_g),
        (q_norm_g, ggq, m_q_norm_g, v_q_norm_g),
        (tr(w_q_b), gwqT, tr(m_w_q_b), tr(v_w_q_b)),
        (kv_norm_g, ggkv, m_kv_norm_g, v_kv_norm_g),
        (w_kv_b[0], gwkv, m_w_kv_b[0], v_w_kv_b[0]),
        (pw2(pool_w), gpw, pw2(m_pool_w), pw2(v_pool_w)),
        (pool_scale, gps, m_pool_scale, v_pool_scale),
        (fn2(final_norm_g), ggf, fn2(m_final_norm_g), fn2(v_final_norm_g)),
    ])
    untr = lambda a: a.T[None]
    pw4 = lambda a: a.reshape(1, len(POOL_WINDOWS), POOL_GROUP, POOL_GROUP)
    per_kind = [[
        r_meta[kind], r_norm[kind], untr(r_in[kind]), r_gq[kind], untr(r_wq[kind]), r_gkv[kind], r_wkv[kind][None],
        pw4(r_pw[kind]), r_ps[kind], r_out[kind][None], r_fn[kind].reshape(D),
    ] for kind in range(4)]
    return (gloss[0, 0], part["gx"][None], *per_kind[0], *per_kind[1], *per_kind[2], *per_kind[3])
```

```python
import jax
import jax.numpy as jnp
import numpy as np
from jax import lax
from jax.experimental import pallas as pl
from jax.experimental.pallas import tpu as pltpu

F32 = jnp.float32
BF16 = jnp.bfloat16

D = 1024
S = 2048
N_META = 16
PAD = 112
HEAD_ROWS = PAD + N_META
N = HEAD_ROWS + S
D_POOL = 512
POOL_WINDOWS = (2, 4, 8, 16)
POOL_GROUP = 128
HALO = 16
HEADS = 4
QK_NOPE = 128
QK_ROPE = 64
QK = QK_NOPE + QK_ROPE
V_HEAD = 128
Q_LORA = 256
KV_LORA = 128
D_IN = 1984
EPS = 1e-6
ROPE_THETA = 10000.0
SCALE = QK ** -0.5
CHIPS = 4

ROWS_FWD = 544
ROWS_MID = 544
ROWS_BWD = 544
TK = 128
TQ = 256
NQ = S // TQ
HEADS_PER_STEP_BWD = 2

O_PI, O_PG, O_CQ, O_CKV, O_KR, O_AG = 0, 512, 1024, 1280, 1408, 1472
O_KR_END = O_KR + 128
SHARD_IN = D_IN // CHIPS
SHARD_PAD = 512
SHARD_OUT = D // CHIPS

LR, B1, B2, ADAM_EPS, WD, STEP = 0.001, 0.9, 0.999, 1e-08, 0.01, 10
C1 = 1.0 - B1**STEP
C2 = 1.0 - B2**STEP

VMEM_LIMIT = 60 * 1024 * 1024
MESH = pl.DeviceIdType.MESH
NEG = -1e30

VEC_ROWS = 8
V_GQ, V_GKV, V_PS, V_LOSS = 0, 256, 384, 896


def _cparams(**kw):
    return pltpu.CompilerParams(vmem_limit_bytes=VMEM_LIMIT, **kw)


def _nt(a, b):
    return lax.dot_general(a, b, (((1,), (1,)), ((), ())), preferred_element_type=F32)


def _tn(a, b):
    return lax.dot_general(a, b, (((0,), (0,)), ((), ())), preferred_element_type=F32)


def _nn(a, b):
    return jnp.dot(a, b, preferred_element_type=F32)


def _swap64(t):
    return pltpu.roll(t, 32, 1) + pltpu.roll(t, 96, 1)


def _sigmoid(x):
    return 1.0 / (1.0 + jnp.exp(-x))


def _low_lanes():
    return (lax.broadcasted_iota(jnp.int32, (1, 128), 1) < QK_ROPE).astype(F32)


def _rows(w, rows):
    return pl.BlockSpec((rows, w), lambda i: (i, 0))


def _const(*shape):
    return pl.BlockSpec(shape, lambda *_: (0,) * len(shape), pipeline_mode=pl.Buffered(1))


STAT_GROUPS = HEADS // HEADS_PER_STEP_BWD


def _stat_slot(head):
    return head // HEADS_PER_STEP_BWD, head % HEADS_PER_STEP_BWD


N_PEERS = 4
SEND_ORDER = (3, 1, 2)


def _peer_signal(x, y, c):
    barrier = pltpu.get_barrier_semaphore()
    peers = [(x, y, 1 - c)] + [(x ^ fx, y ^ fy, c) for fx, fy in _CHIP_RELS[1:]]
    assert len(peers) == N_PEERS
    for peer in peers:
        pl.semaphore_signal(barrier, inc=1, device_id=peer, device_id_type=MESH)


def _peer_wait():
    pl.semaphore_wait(pltpu.get_barrier_semaphore(), N_PEERS)


def _attn_tiles():
    return [(0, TK, TK)] + [(TK + TQ * t, TQ, TK + TQ * (t + 1)) for t in range(NQ)]


def _masked_scores(q, k, rows, klen):
    s = _nt(q, k)
    col = lax.broadcasted_iota(jnp.int32, (1, TK), 1)
    head_bias = jnp.where(col >= PAD, 0.0, NEG)
    if klen == TK:
        return s + head_bias
    r = lax.broadcasted_iota(jnp.int32, (rows, 1), 0) >> 6
    c = lax.broadcasted_iota(jnp.int32, (1, rows), 1) >> 6
    diag_bias = jnp.where(c <= r, 0.0, NEG)
    parts = [s[:, 0:TK] + head_bias]
    if klen - rows > TK:
        parts.append(s[:, TK:klen - rows])
    parts.append(s[:, klen - rows:klen] + diag_bias)
    return jnp.concatenate(parts, axis=1)


def _fwd_in(h, norm_g, win, gq, wq, gkv, wkv, cosf, sinf):
    tr = ROWS_FWD

    def body(h_ref, g_ref, win_ref, gq_ref, wq_ref, gkv_ref, wkv_ref, cos_ref, sin_ref,
             pi_ref, pg_ref, cq_ref, ckv_ref, ag_ref, q_ref, k_ref, v_ref, hn_ref):
        h = h_ref[...]
        r = lax.rsqrt(jnp.mean(h * h, axis=-1, keepdims=True) + EPS)
        hn = ((h * r) * g_ref[...]).astype(BF16)
        hn_ref[...] = hn
        u = _nt(hn, win_ref[0:O_KR_END, :])
        pi_ref[...] = u[:, O_PI:O_PG]
        pg_ref[...] = u[:, O_PG:O_CQ]
        cq = u[:, O_CQ:O_CKV]
        ckv = u[:, O_CKV:O_KR]
        cq_ref[...] = cq
        ckv_ref[...] = ckv
        ag_ref[...] = _nt(hn, win_ref[O_AG:D_IN, :])
        cosv = cos_ref[...]
        sinv = sin_ref[...]
        kr = u[:, O_KR:O_KR_END] * _low_lanes()
        kr = (kr * cosv + _swap64(kr) * sinv).astype(BF16)
        rq = lax.rsqrt(jnp.mean(cq * cq, axis=-1, keepdims=True) + EPS)
        cqn = ((cq * rq) * gq_ref[...]).astype(BF16)
        rkv = lax.rsqrt(jnp.mean(ckv * ckv, axis=-1, keepdims=True) + EPS)
        ckvn = ((ckv * rkv) * gkv_ref[...]).astype(BF16)
        for hd in range(HEADS):
            qh = _nt(cqn, wq_ref[hd]) * SCALE
            z = qh[:, QK_NOPE:]
            q_ref[hd, :, 0:QK_NOPE] = qh[:, 0:QK_NOPE].astype(BF16)
            q_ref[hd, :, QK_NOPE:] = (z * cosv + _swap64(z) * sinv).astype(BF16)
            kvh = _nn(ckvn, wkv_ref[hd])
            k_ref[hd, :, 0:QK_NOPE] = kvh[:, 0:QK_NOPE].astype(BF16)
            k_ref[hd, :, QK_NOPE:] = kr
            v_ref[hd] = kvh[:, QK_NOPE:].astype(BF16)

    head = lambda w: pl.BlockSpec((HEADS, tr, w), lambda i: (0, i, 0))
    return pl.pallas_call(
        body,
        name="fwd_in",
        grid=(N // tr,),
        in_specs=[
            _rows(D, tr), _const(1, D), _const(D_IN, D), _const(1, Q_LORA), _const(HEADS, 256, Q_LORA),
            _const(1, KV_LORA), _const(HEADS, KV_LORA, 256), _rows(128, tr), _rows(128, tr),
        ],
        out_specs=[_rows(D_POOL, tr), _rows(D_POOL, tr), _rows(Q_LORA, tr), _rows(KV_LORA, tr), _rows(D_POOL, tr),
                   head(256), head(256), head(V_HEAD), _rows(D, tr)],
        out_shape=[
            jax.ShapeDtypeStruct((N, D_POOL), F32), jax.ShapeDtypeStruct((N, D_POOL), F32),
            jax.ShapeDtypeStruct((N, Q_LORA), F32), jax.ShapeDtypeStruct((N, KV_LORA), F32),
            jax.ShapeDtypeStruct((N, D_POOL), F32),
            jax.ShapeDtypeStruct((HEADS, N, 256), BF16), jax.ShapeDtypeStruct((HEADS, N, 256), BF16),
            jax.ShapeDtypeStruct((HEADS, N, V_HEAD), BF16), jax.ShapeDtypeStruct((N, D), BF16),
        ],
        compiler_params=_cparams(dimension_semantics=("arbitrary",)),
    )(h, norm_g, win, gq, wq, gkv, wkv, cosf, sinf)


def _attn_fwd(q, k, v, wout_s):
    tiles = _attn_tiles()
    n_t = len(tiles)
    half = SHARD_OUT // 2
    send_step = 2
    fwd_step = n_t - 2

    def body(q_hbm, k_hbm, v_hbm, wout_ref, o_hbm, lse_ref, wout_o, q_buf, k_buf, v_buf, o_buf, s_wout, in_sems, out_sems,
             ici_send, ici_recv, fwd_send, fwd_recv, own_sem):
        step = pl.program_id(0)
        x, y, c = lax.axis_index("x"), lax.axis_index("y"), lax.axis_index("c")
        me = 2 * x + y

        def chip_of(rel):
            fx, fy = _CHIP_RELS[rel]
            return 2 * (x ^ fx) + (y ^ fy)

        def place(chip, core):
            return wout_o.at[pl.ds(pl.multiple_of(SHARD_OUT * chip + half * core, half), half), :]

        def ici_copy(rel, src_chip, to):
            return _remote(s_wout.at[pl.ds(pl.multiple_of(half * c, half), half), :], place(src_chip, c),
                           ici_send.at[rel - 1], ici_recv.at[rel - 1], to)

        def fwd_copy(rel, core, to):
            spot = place(chip_of(rel), core)
            return _remote(spot, spot, fwd_send.at[rel - 1], fwd_recv.at[rel - 1], to)

        own = pltpu.make_async_copy(s_wout, wout_o.at[pl.ds(pl.multiple_of(SHARD_OUT * me, SHARD_OUT), SHARD_OUT), :], own_sem)

        @pl.when(step == 0)
        def _():
            _peer_signal(x, y, c)
            s_wout[...] = wout_ref[...].astype(BF16)
            own.start()

        @pl.when(step == send_step)
        def _():
            _peer_wait()
            for rel in SEND_ORDER:
                fx, fy = _CHIP_RELS[rel]
                ici_copy(rel, me, (x ^ fx, y ^ fy, c)).start()

        @pl.when(step == fwd_step)
        def _():
            for rel in (1, 2, 3):
                ici_copy(rel, chip_of(rel), (x, y, c)).wait_recv()
                fwd_copy(rel, c, (x, y, 1 - c)).start()

        def finish_wout():
            for rel in (1, 2, 3):
                fwd_copy(rel, 1 - c, (x, y, c)).wait_recv()
            for rel in (1, 2, 3):
                ici_copy(rel, me, (x, y, c)).wait_send()
                fwd_copy(rel, c, (x, y, c)).wait_send()
            own.wait()

        def loads(idx):
            q0, rows, _ = tiles[idx]
            rs = pl.ds(q0, rows)
            return [pltpu.make_async_copy(src.at[:, rs, :], dst.at[:, rs, :], in_sems.at[a, idx % 2])
                    for a, (src, dst) in enumerate(((q_hbm, q_buf), (k_hbm, k_buf), (v_hbm, v_buf)))]

        def store(idx):
            q0, rows, _ = tiles[idx]
            return pltpu.make_async_copy(o_buf.at[idx % 2, pl.ds(0, rows), :], o_hbm.at[pl.ds(q0, rows), :],
                                         out_sems.at[idx % 2])

        @pl.when(step == 0)
        def _():
            lse_ref[...] = jnp.zeros_like(lse_ref)
            for cp in loads(0):
                cp.start()

        for idx, (q0, rows, klen) in enumerate(tiles):
            @pl.when(step == idx)
            def _(idx=idx, q0=q0, rows=rows, klen=klen):
                for cp in loads(idx):
                    cp.wait()
                if idx + 1 < n_t:
                    for cp in loads(idx + 1):
                        cp.start()
                if idx >= 2:
                    store(idx - 2).wait()
                for hd in range(HEADS):
                    s = _masked_scores(q_buf[hd, q0:q0 + rows, :], k_buf[hd, 0:klen, :], rows, klen)
                    m = jnp.max(s, axis=-1, keepdims=True)
                    p = jnp.exp(s - m)
                    l = jnp.sum(p, axis=-1, keepdims=True)
                    o_buf[idx % 2, 0:rows, hd * V_HEAD:(hd + 1) * V_HEAD] = _nn(p.astype(BF16), v_buf[hd, 0:klen, :]) / l
                    grp, lane = _stat_slot(hd)
                    lse_ref[grp, q0:q0 + rows, lane:lane + 1] = m + jnp.log(l)
                store(idx).start()
                if idx == n_t - 1:
                    store(idx - 1).wait()
                    store(idx).wait()
                    finish_wout()

    hbm = pl.BlockSpec(memory_space=pl.ANY)
    return pl.pallas_call(
        body,
        name="attn_fwd",
        grid=(n_t,),
        in_specs=[hbm, hbm, hbm, _const(SHARD_OUT, D)],
        out_specs=[hbm, _const(STAT_GROUPS, N, 128), hbm],
        out_shape=[jax.ShapeDtypeStruct((N, HEADS * V_HEAD), F32), jax.ShapeDtypeStruct((STAT_GROUPS, N, 128), F32),
                   jax.ShapeDtypeStruct((D, D), BF16)],
        scratch_shapes=[pltpu.VMEM((HEADS, N, 256), BF16), pltpu.VMEM((HEADS, N, 256), BF16),
                        pltpu.VMEM((HEADS, N, V_HEAD), BF16), pltpu.VMEM((2, TQ, HEADS * V_HEAD), F32),
                        pltpu.VMEM((SHARD_OUT, D), BF16),
                        pltpu.SemaphoreType.DMA((3, 2)), pltpu.SemaphoreType.DMA((2,))]
        + [pltpu.SemaphoreType.DMA((3,))] * 4 + [pltpu.SemaphoreType.DMA],
        compiler_params=_cparams(dimension_semantics=("arbitrary",), collective_id=1),
    )(q, k, v, wout_s)


def _inv_count(row0, rows, w):
    row = row0 + lax.broadcasted_iota(jnp.int32, (rows, 1), 0)
    return 1.0 / jnp.clip(row - (PAD - 1), 1, w).astype(F32)


def _mid(h, tgt, pool_in, pool_gate, attn_gate, attn, pool_w, pool_scale, wout, gf):
    tr = ROWS_MID
    per = tr // HALO
    ng = len(POOL_WINDOWS)

    def body(h_ref, t_ref, pin_ref, halo_ref, pg_ref, ag_ref, at_ref, pw_ref, ps_ref, wout_ref, gf_ref,
             dh2_ref, do_ref, delta_ref, dag_ref, dpg_ref, dpl_ref, dwout_ref, dpw_ref, dps_ref, dgf_ref, loss_ref):
        i = pl.program_id(0)

        @pl.when(i == 0)
        def _():
            dwout_ref[...] = jnp.zeros_like(dwout_ref)
            dpw_ref[...] = jnp.zeros_like(dpw_ref)
            dps_ref[...] = jnp.zeros_like(dps_ref)
            dgf_ref[...] = jnp.zeros_like(dgf_ref)
            loss_ref[...] = jnp.zeros_like(loss_ref)

        row0 = i * tr
        real = (row0 + lax.broadcasted_iota(jnp.int32, (tr, 1), 0)) >= HEAD_ROWS
        h = h_ref[...]

        halo = jnp.where(i > 0, halo_ref[...], 0.0)
        ext = jnp.concatenate([halo, pin_ref[...]], axis=0)
        pooled = []
        for g, w in enumerate(POOL_WINDOWS):
            e = ext[:, g * POOL_GROUP:(g + 1) * POOL_GROUP]
            acc = e
            shift = 1
            while shift < w:
                acc = acc + pltpu.roll(acc, shift, 0)
                shift *= 2
            pooled.append((acc[HALO:] * _inv_count(row0, tr, w) - e[HALO:]).astype(BF16))
        pw = [pw_ref[g].astype(BF16) for g in range(ng)]
        mixed = jnp.concatenate([_nn(pooled[g], pw[g]) for g in range(ng)], axis=1)
        ps = ps_ref[...]
        mixed_s = mixed * ps
        pg = pg_ref[...]
        sig_p = _sigmoid(pg)
        silu_p = pg * sig_p
        pool_out = (silu_p * mixed_s).astype(BF16)
        ag = ag_ref[...]
        sig_a = _sigmoid(ag)
        silu_a = ag * sig_a
        at = at_ref[...]
        attn_out = (silu_a * at).astype(BF16)
        cat = jnp.concatenate([pool_out, attn_out], axis=1)
        h2 = h + _nn(cat, wout_ref[...])

        r2 = lax.rsqrt(jnp.mean(h2 * h2, axis=-1, keepdims=True) + EPS)
        n2 = h2 * r2
        gfv = gf_ref[...]
        err = jnp.where(real, n2 * gfv - t_ref[...], 0.0)
        loss_ref[...] += jnp.sum(jnp.sum(err * err, axis=-1, keepdims=True), axis=0, keepdims=True) * (0.5 / D)
        dy = err * (1.0 / D)
        dgf_ref[...] += jnp.sum(dy * n2, axis=0, keepdims=True)
        dn = dy * gfv
        dh2 = r2 * (dn - n2 * jnp.mean(dn * n2, axis=-1, keepdims=True))
        dh2_ref[...] = dh2
        dh2b = dh2.astype(BF16)

        dwout_ref[...] += _tn(cat, dh2b)
        dcat = _nt(dh2b, wout_ref[...])
        dpo = dcat[:, 0:D_POOL]
        dao = dcat[:, D_POOL:D]
        do = dao * silu_a
        prod = do * at
        delta_ref[...] = jnp.zeros_like(delta_ref)
        for hd in range(HEADS):
            grp, lane = _stat_slot(hd)
            cols = slice(hd * V_HEAD, (hd + 1) * V_HEAD)
            do_ref[grp, :, lane * V_HEAD:(lane + 1) * V_HEAD] = do[:, cols].astype(BF16)
            delta_ref[grp, :, lane:lane + 1] = jnp.sum(prod[:, cols], axis=-1, keepdims=True)
        dag_ref[...] = (dao * at * (sig_a * (1.0 + ag * (1.0 - sig_a)))).astype(BF16)
        dmixed_s = dpo * silu_p
        dpg_ref[...] = (dpo * mixed_s * (sig_p * (1.0 + pg * (1.0 - sig_p)))).astype(BF16)
        dps_ref[...] += jnp.sum(dmixed_s * mixed, axis=0, keepdims=True)
        dmixed = (dmixed_s * ps).astype(BF16)
        dpl = []
        for g in range(ng):
            dm = dmixed[:, g * POOL_GROUP:(g + 1) * POOL_GROUP]
            dpl.append(_nt(dm, pw[g]))
            dpw_ref[g] += _tn(pooled[g], dm)
        dpl_ref[...] = jnp.concatenate(dpl, axis=1)

    halo_spec = pl.BlockSpec((HALO, D_POOL), lambda i: (jnp.maximum(i * per - 1, 0), 0))
    return pl.pallas_call(
        body,
        name="mid",
        grid=(N // tr,),
        in_specs=[
            _rows(D, tr), _rows(D, tr), _rows(D_POOL, tr), halo_spec, _rows(D_POOL, tr), _rows(D_POOL, tr),
            _rows(D_POOL, tr), _const(ng, POOL_GROUP, POOL_GROUP), _const(1, D_POOL), _const(D, D), _const(1, D),
        ],
        out_specs=[
            _rows(D, tr), pl.BlockSpec((STAT_GROUPS, tr, HEADS_PER_STEP_BWD * V_HEAD), lambda i: (0, i, 0)),
            pl.BlockSpec((STAT_GROUPS, tr, 128), lambda i: (0, i, 0)),
            _rows(D_POOL, tr), _rows(D_POOL, tr), _rows(D_POOL, tr),
            _const(D, D), _const(ng, POOL_GROUP, POOL_GROUP), _const(1, D_POOL), _const(1, D), _const(1, 128),
        ],
        out_shape=[
            jax.ShapeDtypeStruct((N, D), F32), jax.ShapeDtypeStruct((STAT_GROUPS, N, HEADS_PER_STEP_BWD * V_HEAD), BF16),
            jax.ShapeDtypeStruct((STAT_GROUPS, N, 128), F32),
            jax.ShapeDtypeStruct((N, D_POOL), BF16), jax.ShapeDtypeStruct((N, D_POOL), BF16),
            jax.ShapeDtypeStruct((N, D_POOL), F32), jax.ShapeDtypeStruct((D, D), F32),
            jax.ShapeDtypeStruct((ng, POOL_GROUP, POOL_GROUP), F32),
            jax.ShapeDtypeStruct((1, D_POOL), F32), jax.ShapeDtypeStruct((1, D), F32), jax.ShapeDtypeStruct((1, 128), F32),
        ],
        compiler_params=_cparams(dimension_semantics=("arbitrary",)),
    )(h, tgt, pool_in, pool_in, pool_gate, attn_gate, attn, pool_w, pool_scale, wout, gf)


def _unrope(dy, cosv, sinv):
    return dy * cosv + _swap64(dy * sinv) * _low_lanes()


def _attn_bwd(q, k, v, do, lse, delta, cosf, sinf, dwout):
    tiles = _attn_tiles()
    hp = HEADS_PER_STEP_BWD
    n_g = HEADS // hp
    n_t = len(tiles)
    half = SHARD_OUT // 2
    swap_at, send_at, sum_at = (0, 3), (0, 5), (n_g - 1, n_t // 2)

    def body(q_hbm, k_hbm, v_hbm, do_hbm, lse_ref, delta_ref, cos_ref, sin_ref, dwout_hbm, dq_hbm, dkv_ref, dkr_ref,
             gwout_ref, q_buf, k_buf, v_buf, do_buf, dq_buf, dk_acc, dv_acc, own_w, sib_w, stage_w, recv_w, gw_buf,
             in_sems, out_sems, ow_sems, d2d_send, d2d_recv, ici_send, ici_recv, fin_send, fin_recv):
        grp = pl.program_id(0)
        step = pl.program_id(1)
        heads = pl.ds(grp * hp, hp)
        x, y, c = lax.axis_index("x"), lax.axis_index("y"), lax.axis_index("c")
        sibling = (x, y, 1 - c)

        def chip_of(rel):
            fx, fy = _CHIP_RELS[rel]
            return 2 * (x ^ fx) + (y ^ fy)

        def piece(chip, core):
            return dwout_hbm.at[pl.ds(pl.multiple_of(SHARD_OUT * chip + half * core, half), half), :]

        def own_load(rel):
            return pltpu.make_async_copy(piece(chip_of(rel), c), own_w.at[rel], ow_sems.at[rel])

        def d2d_copy(rel):
            return _remote(piece(chip_of(rel), 1 - c), sib_w.at[rel], d2d_send.at[rel], d2d_recv.at[rel], sibling)

        def ici_copy(rel):
            fx, fy = _CHIP_RELS[rel]
            return _remote(stage_w.at[rel - 1], recv_w.at[rel - 1], ici_send.at[rel - 1], ici_recv.at[rel - 1],
                           (x ^ fx, y ^ fy, c))

        def fin_copy(core):
            spot = gw_buf.at[pl.ds(pl.multiple_of(half * core, half), half), :]
            return _remote(spot, spot, fin_send.at[0], fin_recv.at[0], sibling)

        @pl.when((grp == 0) & (step == 0))
        def _():
            _peer_signal(x, y, c)
            for rel in SEND_ORDER + (0,):
                own_load(rel).start()

        @pl.when((grp == swap_at[0]) & (step == swap_at[1]))
        def _():
            _peer_wait()
            for rel in SEND_ORDER + (0,):
                d2d_copy(rel).start()

        @pl.when((grp == send_at[0]) & (step == send_at[1]))
        def _():
            for rel in SEND_ORDER:
                own_load(rel).wait()
                d2d_copy(rel).wait_recv()
                stage_w[rel - 1] = (own_w[rel] + sib_w[rel]).astype(BF16)
                ici_copy(rel).start()

        @pl.when((grp == sum_at[0]) & (step == sum_at[1]))
        def _():
            own_load(0).wait()
            d2d_copy(0).wait_recv()
            total = own_w[0] + sib_w[0]
            for rel in (1, 2, 3):
                ici_copy(rel).wait_recv()
                total = total + recv_w[rel - 1].astype(F32)
            gw_buf[pl.ds(pl.multiple_of(half * c, half), half), :] = total
            fin_copy(c).start()

        def finish_dwout():
            fin_copy(1 - c).wait_recv()
            for rel in (0, 1, 2, 3):
                d2d_copy(rel).wait_send()
            for rel in (1, 2, 3):
                ici_copy(rel).wait_send()
            fin_copy(c).wait_send()
            gwout_ref[...] = gw_buf[...]

        def loads(g, idx):
            q0, rows, _ = tiles[idx]
            rs = pl.ds(q0, rows)
            par = (g * n_t + idx) % 2
            hs = pl.ds(g * hp, hp)
            pairs = ((q_hbm.at[hs, rs, :], q_buf.at[:, rs, :]), (k_hbm.at[hs, rs, :], k_buf.at[:, rs, :]),
                     (v_hbm.at[hs, rs, :], v_buf.at[:, rs, :]), (do_hbm.at[g, rs, :], do_buf.at[rs, :]))
            return [pltpu.make_async_copy(src, dst, in_sems.at[a, par]) for a, (src, dst) in enumerate(pairs)]

        def store(idx):
            q0, rows, _ = tiles[idx]
            return pltpu.make_async_copy(dq_buf.at[idx % 2, :, pl.ds(0, rows), :], dq_hbm.at[heads, pl.ds(q0, rows), :],
                                         out_sems.at[idx % 2])

        @pl.when(step == 0)
        def _():
            dk_acc[...] = jnp.zeros_like(dk_acc)
            dv_acc[...] = jnp.zeros_like(dv_acc)

        @pl.when((step == 0) & (grp == 0))
        def _():
            dkr_ref[...] = jnp.zeros_like(dkr_ref)
            for cp in loads(grp, 0):
                cp.start()

        for idx, (q0, rows, klen) in enumerate(tiles):
            @pl.when(step == idx)
            def _(idx=idx, q0=q0, rows=rows, klen=klen):
                for cp in loads(grp, idx):
                    cp.wait()
                if idx + 1 < n_t:
                    for cp in loads(grp, idx + 1):
                        cp.start()
                if idx >= 2:
                    store(idx - 2).wait()
                qs = pl.ds(q0, rows)
                for hd in range(hp):
                    qv = q_buf[hd, qs, :]
                    kv = k_buf[hd, 0:klen, :]
                    p = jnp.exp(_masked_scores(qv, kv, rows, klen) - lse_ref[0, qs, hd:hd + 1])
                    dob = do_buf[qs, hd * V_HEAD:(hd + 1) * V_HEAD]
                    ds = (p * (_nt(dob, v_buf[hd, 0:klen, :]) - delta_ref[0, qs, hd:hd + 1])).astype(BF16)
                    dq = _nn(ds, kv) * SCALE
                    dq_buf[idx % 2, hd, 0:rows, 0:QK_NOPE] = dq[:, 0:QK_NOPE].astype(BF16)
                    dq_buf[idx % 2, hd, 0:rows, QK_NOPE:] = _unrope(dq[:, QK_NOPE:], cos_ref[qs, :], sin_ref[qs, :]).astype(BF16)
                    dk_acc[hd, 0:klen, :] += _tn(ds, qv)
                    dv_acc[hd, 0:klen, :] += _tn(p.astype(BF16), dob)
                store(idx).start()

        @pl.when(step == n_t - 1)
        def _():
            @pl.when(grp + 1 < n_g)
            def _():
                for cp in loads(grp + 1, 0):
                    cp.start()

            for hd in range(hp):
                dkv_ref[hd, :, 0:QK_NOPE] = dk_acc[hd, :, 0:QK_NOPE].astype(BF16)
                dkv_ref[hd, :, QK_NOPE:] = dv_acc[hd].astype(BF16)
                dkr_ref[...] += dk_acc[hd, :, QK_NOPE:]
            store(n_t - 2).wait()
            store(n_t - 1).wait()

            @pl.when(grp == n_g - 1)
            def _():
                finish_dwout()

    hbm = pl.BlockSpec(memory_space=pl.ANY)
    stat = pl.BlockSpec((1, N, 128), lambda g, t: (g, 0, 0), pipeline_mode=pl.Buffered(1))
    piece_f32 = lambda lead: pltpu.VMEM((lead, half, D), F32)
    piece_bf16 = lambda lead: pltpu.VMEM((lead, half, D), BF16)
    return pl.pallas_call(
        body,
        name="attn_bwd",
        grid=(n_g, n_t),
        in_specs=[hbm, hbm, hbm, hbm, stat, stat, _const(N, 128), _const(N, 128), hbm],
        out_specs=[hbm, pl.BlockSpec((hp, N, 256), lambda g, t: (g, 0, 0), pipeline_mode=pl.Buffered(1)), _const(N, 128),
                   _const(SHARD_OUT, D)],
        out_shape=[
            jax.ShapeDtypeStruct((HEADS, N, 256), BF16), jax.ShapeDtypeStruct((HEADS, N, 256), BF16),
            jax.ShapeDtypeStruct((N, 128), F32), jax.ShapeDtypeStruct((SHARD_OUT, D), F32),
        ],
        scratch_shapes=[pltpu.VMEM((hp, N, 256), BF16), pltpu.VMEM((hp, N, 256), BF16), pltpu.VMEM((hp, N, V_HEAD), BF16),
                        pltpu.VMEM((N, hp * V_HEAD), BF16), pltpu.VMEM((2, hp, TQ, 256), BF16),
                        pltpu.VMEM((hp, N, 256), F32), pltpu.VMEM((hp, N, V_HEAD), F32),
                        piece_f32(CHIPS), piece_f32(CHIPS), piece_bf16(3), piece_bf16(3), pltpu.VMEM((SHARD_OUT, D), F32),
                        pltpu.SemaphoreType.DMA((4, 2)), pltpu.SemaphoreType.DMA((2,)), pltpu.SemaphoreType.DMA((CHIPS,)),
                        pltpu.SemaphoreType.DMA((CHIPS,)), pltpu.SemaphoreType.DMA((CHIPS,)),
                        pltpu.SemaphoreType.DMA((3,)), pltpu.SemaphoreType.DMA((3,)),
                        pltpu.SemaphoreType.DMA((1,)), pltpu.SemaphoreType.DMA((1,))],
        compiler_params=_cparams(dimension_semantics=("arbitrary", "arbitrary"), collective_id=2),
    )(q, k, v, do, lse, delta, cosf, sinf, dwout)


def _bwd_in(h, dh2, dq, dkv, dkr, cq, ckv, dpl, dpg, dag, norm_g, win, gq, wq, gkv, wkv, cosf, sinf, adam_out):
    tr = ROWS_BWD
    nb = N // tr
    per = tr // HALO
    lead = HEAD_ROWS
    adam_rows = SHARD_OUT // nb

    def body(h_ref, dh2_ref, dq_ref, dkv_ref, dkr_ref, cq_ref, ckv_ref, dpl_ref, halo_ref, dpg_ref, dag_ref,
             g_ref, win_ref, gq_ref, wq_ref, gkv_ref, wkv_ref, cos_ref, sin_ref, aw_ref, ag_ref, am_ref, av_ref,
             gx_ref, dmeta_ref, dsl_ref, dwq_ref, dwkv_ref, dg_ref, dgq_ref, dgkv_ref, ago_ref, ad_ref, anm_ref, anv_ref,
             dh_buf, gx_sem):
        i = pl.program_id(0)
        grad_out = ag_ref[...]
        ago_ref[...] = grad_out
        ad_ref[...], anm_ref[...], anv_ref[...] = _adamw_math(aw_ref[...], grad_out, am_ref[...], av_ref[...])

        @pl.when(i == 0)
        def _():
            dwq_ref[...] = jnp.zeros_like(dwq_ref)
            dwkv_ref[...] = jnp.zeros_like(dwkv_ref)
            dg_ref[...] = jnp.zeros_like(dg_ref)
            dgq_ref[...] = jnp.zeros_like(dgq_ref)
            dgkv_ref[...] = jnp.zeros_like(dgkv_ref)

        row0 = i * tr
        h = h_ref[...]
        r = lax.rsqrt(jnp.mean(h * h, axis=-1, keepdims=True) + EPS)
        n = h * r
        gv = g_ref[...]
        cq = cq_ref[...]
        rq = lax.rsqrt(jnp.mean(cq * cq, axis=-1, keepdims=True) + EPS)
        nq = cq * rq
        gqv = gq_ref[...]
        cqn = (nq * gqv).astype(BF16)
        dcqn = jnp.zeros((tr, Q_LORA), F32)
        for hd in range(HEADS):
            dqf = dq_ref[hd]
            dcqn = dcqn + _nn(dqf, wq_ref[hd])
            dwq_ref[hd] += _tn(dqf, cqn)
        dgq_ref[...] += jnp.sum(dcqn * nq, axis=0, keepdims=True)
        dnq = dcqn * gqv
        dcq = rq * (dnq - nq * jnp.mean(dnq * nq, axis=-1, keepdims=True))

        ckv = ckv_ref[...]
        rkv = lax.rsqrt(jnp.mean(ckv * ckv, axis=-1, keepdims=True) + EPS)
        nkv = ckv * rkv
        gkvv = gkv_ref[...]
        ckvn = (nkv * gkvv).astype(BF16)
        dckvn = jnp.zeros((tr, KV_LORA), F32)
        for hd in range(HEADS):
            dkv = dkv_ref[hd]
            dckvn = dckvn + _nt(dkv, wkv_ref[hd])
            dwkv_ref[hd] += _tn(ckvn, dkv)
        dgkv_ref[...] += jnp.sum(dckvn * nkv, axis=0, keepdims=True)
        dnkv = dckvn * gkvv
        dckv = rkv * (dnkv - nkv * jnp.mean(dnkv * nkv, axis=-1, keepdims=True))
        dkr = _unrope(dkr_ref[...], cos_ref[...], sin_ref[...])

        cur = dpl_ref[...]
        halo = jnp.where(i < nb - 1, halo_ref[...], 0.0)
        dpi = []
        for g, w in enumerate(POOL_WINDOWS):
            sl = slice(g * POOL_GROUP, (g + 1) * POOL_GROUP)
            a = jnp.concatenate([cur[:, sl] * _inv_count(row0, tr, w), halo[:, sl] * _inv_count(row0 + tr, HALO, w)], axis=0)
            acc = a
            shift = 1
            while shift < w:
                acc = acc + pltpu.roll(acc, tr + HALO - shift, 0)
                shift *= 2
            dpi.append(acc[0:tr] - cur[:, sl])

        du = jnp.concatenate([t.astype(BF16) for t in dpi] + [dpg_ref[...]] + [t.astype(BF16) for t in (dcq, dckv, dkr)],
                             axis=1)
        dagb = dag_ref[...]
        by_row = jnp.concatenate(dpi + [dpg_ref[...].astype(F32), dcq, dckv, dkr[:, 0:QK_ROPE], dagb.astype(F32),
                                        jnp.zeros((tr, SHARD_PAD - SHARD_IN), F32)], axis=1)
        for chip in range(CHIPS):
            dsl_ref[chip] = by_row[:, SHARD_IN * chip:SHARD_IN * chip + SHARD_PAD].astype(BF16)
        dhn = _nn(du, win_ref[0:O_KR_END, :]) + _nn(dagb, win_ref[O_AG:D_IN, :])
        dg_ref[...] += jnp.sum(dhn * n, axis=0, keepdims=True)
        dn = dhn * gv
        dh = dh2_ref[...] + r * (dn - n * jnp.mean(dn * n, axis=-1, keepdims=True))

        first = pltpu.make_async_copy(dh_buf.at[pl.ds(lead, tr - lead), :], gx_ref.at[pl.ds(0, tr - lead), :], gx_sem)
        later = lambda step: pltpu.make_async_copy(
            dh_buf, gx_ref.at[pl.ds(pl.multiple_of(step * tr - lead, 16), tr), :], gx_sem)

        @pl.when(i == 1)
        def _():
            first.wait()

        @pl.when(i > 1)
        def _():
            later(i - 1).wait()

        dh_buf[...] = dh

        @pl.when(i == 0)
        def _():
            first.start()
            for chip in range(CHIPS):
                dmeta_ref[chip] = dh[PAD:HEAD_ROWS, chip * 256:(chip + 1) * 256]

        @pl.when(i > 0)
        def _():
            later(i).start()

        @pl.when(i == nb - 1)
        def _():
            later(i).wait()

    head = lambda w: pl.BlockSpec((HEADS, tr, w), lambda i: (0, i, 0))
    halo_spec = pl.BlockSpec((HALO, D_POOL), lambda i: (jnp.minimum((i + 1) * per, N // HALO - 1), 0))
    return pl.pallas_call(
        body,
        name="bwd_in",
        grid=(nb,),
        in_specs=[
            _rows(D, tr), _rows(D, tr), head(256), head(256), _rows(128, tr), _rows(Q_LORA, tr), _rows(KV_LORA, tr),
            _rows(D_POOL, tr), halo_spec, _rows(D_POOL, tr), _rows(D_POOL, tr),
            _const(1, D), _const(D_IN, D), _const(1, Q_LORA), _const(HEADS, 256, Q_LORA),
            _const(1, KV_LORA), _const(HEADS, KV_LORA, 256), _rows(128, tr), _rows(128, tr),
        ] + [_rows(D, adam_rows)] * 4,
        out_specs=[
            pl.BlockSpec(memory_space=pl.ANY), _const(CHIPS, N_META, 256),
            pl.BlockSpec((CHIPS, tr, SHARD_PAD), lambda i: (0, i, 0)), _const(HEADS, 256, Q_LORA),
            _const(HEADS, KV_LORA, 256), _const(1, D), _const(1, Q_LORA), _const(1, KV_LORA),
        ] + [_rows(D, adam_rows)] * 4,
        out_shape=[
            jax.ShapeDtypeStruct((S, D), F32), jax.ShapeDtypeStruct((CHIPS, N_META, 256), F32),
            jax.ShapeDtypeStruct((CHIPS, N, SHARD_PAD), BF16), jax.ShapeDtypeStruct((HEADS, 256, Q_LORA), F32),
            jax.ShapeDtypeStruct((HEADS, KV_LORA, 256), F32),
            jax.ShapeDtypeStruct((1, D), F32), jax.ShapeDtypeStruct((1, Q_LORA), F32), jax.ShapeDtypeStruct((1, KV_LORA), F32),
        ] + [jax.ShapeDtypeStruct((SHARD_OUT, D), F32)] * 4,
        scratch_shapes=[pltpu.VMEM((tr, D), F32), pltpu.SemaphoreType.DMA],
        compiler_params=_cparams(dimension_semantics=("arbitrary",)),
    )(h, dh2, dq, dkv, dkr, cq, ckv, dpl, dpl, dpg, dag, norm_g, win, gq, wq, gkv, wkv, cosf, sinf, *adam_out)


def _local_step(h, tgt, norm_g, win, gq, wq, gkv, wkv, pool_w, pool_scale, wout_s, m_wout_s, v_wout_s, gf, cosf, sinf):
    pool_in, pool_gate, cq, ckv, attn_gate, q, k, v, hn = _fwd_in(h, norm_g, win, gq, wq, gkv, wkv, cosf, sinf)
    attn, lse, wout = _attn_fwd(q, k, v, wout_s)
    dh2, do, delta, dag, dpg, dpl, dwout, dpw, dps, dgf, loss = _mid(
        h, tgt, pool_in, pool_gate, attn_gate, attn, pool_w, pool_scale, wout, gf)
    dq, dkv, dkr, gwout = _attn_bwd(q, k, v, do, lse, delta, cosf, sinf, dwout)
    gx, dmeta, dsl, dwq, dwkv, dg, dgq, dgkv, *r_out = _bwd_in(
        h, dh2, dq, dkv, dkr, cq, ckv, dpl, dpg, dag, norm_g, win, gq, wq, gkv, wkv, cosf, sinf,
        (wout_s, gwout, m_wout_s, v_wout_s))
    return dict(gx=gx, dmeta=dmeta, dsl=dsl, hn=hn, dwq=dwq, dwkv=dwkv, r_out=tuple(r_out), dg=dg, dgq=dgq,
                dgkv=dgkv, dpw=dpw, dps=dps, dgf=dgf, loss=loss)


_CHIP_RELS = ((0, 0), (1, 0), (0, 1), (1, 1))

_ARR_ROWS = (SHARD_IN, SHARD_OUT, 256, KV_LORA, N_META)
_ARR_COLS = (D, D, Q_LORA, 256, 256)
_PIECES = (
    (0, 0, 256, 0), (0, 256, SHARD_IN - 256, 1),
    (1, 0, 128, 0), (1, 128, 128, 1),
    (2, 0, 128, 0), (2, 128, 128, 1),
    (3, 0, 64, 0), (3, 64, 64, 1),
    (4, 0, N_META, 0),
)
_NP = len(_PIECES)
_PIECE_MAX = (256, 128, 128, 64, N_META)


def _gathered_at(refs, arr, chip, r0, n):
    if arr in (0, 1):
        return refs[arr].at[pl.ds(pl.multiple_of(_ARR_ROWS[arr] * chip + r0, 16), n), :]
    return refs[arr].at[chip, pl.ds(r0, n), :]


def _remote(src, dst, send_sem, recv_sem, to):
    return pltpu.make_async_remote_copy(src_ref=src, dst_ref=dst, send_sem=send_sem, recv_sem=recv_sem,
                                        device_id=to, device_id_type=MESH)


def _gather_weights(winT_s, wqT_s, wkv_s, meta_s, x2, tgt2):
    arrays = (0, 2, 3, 4)

    def body(win_ref, wq_ref, wkv_ref, meta_ref, x_ref, t_ref, win_o, wq_o, wkv_o, h_o, tp_o,
             s_win, s_wq, s_wkv, meta_all, head_buf, x_buf, t_buf, ici_send, ici_recv, fwd_send, fwd_recv,
             loc_sems, own_sems):
        x, y, c = lax.axis_index("x"), lax.axis_index("y"), lax.axis_index("c")
        me = 2 * x + y
        stage = (s_win, None, s_wq, s_wkv, meta_ref)
        outs = (win_o, None, wq_o, wkv_o, meta_all)

        _peer_signal(x, y, c)

        frames = pl.ds(HEAD_ROWS, S)
        loads = [pltpu.make_async_copy(x_ref, x_buf, loc_sems.at[0]), pltpu.make_async_copy(t_ref, t_buf, loc_sems.at[1])]
        local = [pltpu.make_async_copy(x_buf, h_o.at[frames, :], loc_sems.at[0]),
                 pltpu.make_async_copy(t_buf, tp_o.at[frames, :], loc_sems.at[1])]
        for cp in loads:
            cp.start()

        s_win[...] = win_ref[...].astype(BF16)
        s_wq[0:QK, :] = wq_ref[...].astype(BF16)
        s_wq[QK:256, :] = jnp.zeros((256 - QK, Q_LORA), BF16)
        s_wkv[...] = wkv_ref[...].astype(BF16)
        head_buf[...] = jnp.zeros_like(head_buf)
        zeros = pltpu.make_async_copy(head_buf, tp_o.at[pl.ds(0, HEAD_ROWS), :], loc_sems.at[2])
        zeros.start()

        def chip_of(rel):
            fx, fy = _CHIP_RELS[rel]
            return 2 * (x ^ fx) + (y ^ fy)

        def same_core_of(rel):
            fx, fy = _CHIP_RELS[rel]
            return (x ^ fx, y ^ fy, c)

        def ici_copy(rel, i, src_chip, to):
            arr, r0, n, _ = _PIECES[i]
            k = (rel - 1) * _NP + i
            return _remote(stage[arr].at[pl.ds(r0, n), :], _gathered_at(outs, arr, src_chip, r0, n),
                           ici_send.at[k], ici_recv.at[k], to)

        def fwd_copy(rel, i, to):
            arr, r0, n, _ = _PIECES[i]
            k = (rel - 1) * _NP + i
            place = _gathered_at(outs, arr, chip_of(rel), r0, n)
            return _remote(place, place, fwd_send.at[k], fwd_recv.at[k], to)

        _peer_wait()
        for core in (0, 1):
            @pl.when(c == core)
            def _(core=core):
                mine = [i for i in range(_NP) if _PIECES[i][3] == core and _PIECES[i][0] in arrays]
                theirs = [i for i in range(_NP) if _PIECES[i][3] != core and _PIECES[i][0] in arrays]
                order = (1, 2, 3)
                sends = [ici_copy(rel, i, me, same_core_of(rel)) for rel in order for i in mine]
                for cp in sends:
                    cp.start()
                for ld, st in zip(loads, local):
                    ld.wait()
                    st.start()
                own = [pltpu.make_async_copy(stage[arr], _gathered_at(outs, arr, me, 0, _ARR_ROWS[arr]), own_sems.at[arr])
                       for arr in arrays if arr != 4]
                for cp in own:
                    cp.start()
                meta_all[me] = meta_ref[...]
                for rel in order:
                    for i in mine:
                        ici_copy(rel, i, chip_of(rel), (x, y, c)).wait_recv()
                        fwd = fwd_copy(rel, i, (x, y, 1 - c))
                        fwd.start()
                        sends.append(fwd)
                for rel in order:
                    for i in theirs:
                        fwd_copy(rel, i, (x, y, c)).wait_recv()
                for cp in sends:
                    cp.wait_send()
                for cp in own:
                    cp.wait()

        zeros.wait()
        for chip in range(CHIPS):
            head_buf[PAD:HEAD_ROWS, chip * 256:(chip + 1) * 256] = meta_all[chip]
        head = pltpu.make_async_copy(head_buf, h_o.at[pl.ds(0, HEAD_ROWS), :], loc_sems.at[2])
        head.start()
        head.wait()
        for cp in local:
            cp.wait()

    vm = pl.BlockSpec(memory_space=pltpu.VMEM)
    hbm = pl.BlockSpec(memory_space=pl.ANY)
    return pl.pallas_call(
        body,
        name="gather_weights",
        in_specs=[vm] * 4 + [hbm] * 2,
        out_specs=[hbm] * 5,
        out_shape=[
            jax.ShapeDtypeStruct((D_IN, D), BF16),
            jax.ShapeDtypeStruct((CHIPS, 256, Q_LORA), BF16), jax.ShapeDtypeStruct((CHIPS, KV_LORA, 256), BF16),
            jax.ShapeDtypeStruct((N, D), F32), jax.ShapeDtypeStruct((N, D), F32),
        ],
        scratch_shapes=[pltpu.VMEM((_ARR_ROWS[a], _ARR_COLS[a]), BF16) for a in (0, 2, 3)]
        + [pltpu.VMEM((CHIPS, N_META, 256), F32), pltpu.VMEM((HEAD_ROWS, D), F32), pltpu.VMEM((S, D), F32),
           pltpu.VMEM((S, D), F32)]
        + [pltpu.SemaphoreType.DMA((3 * _NP,))] * 4 + [pltpu.SemaphoreType.DMA((3,)), pltpu.SemaphoreType.DMA((4,))],
        compiler_params=_cparams(collective_id=0),
    )(winT_s, wqT_s, wkv_s, meta_s, x2, tgt2)


_SM_ROWS = (len(POOL_WINDOWS) * POOL_GROUP, VEC_ROWS)
_SM_COLS = (POOL_GROUP, D)
_SM_PIECES = ((0, 0, 256, 0), (0, 256, 256, 1), (1, 0, VEC_ROWS, 0))
_NSP = len(_SM_PIECES)
W_IN_PART_ROWS = 128


def _reduce_grads(dsl, hn, dwq, dwkv, dmeta4, dpw, dg, dgf, dgq, dgkv, dps, loss):
    arrays = (0, 2, 3, 4)
    loaded = (2, 3, 4)
    shard_order = SEND_ORDER + (0,)

    def body(dsl_hbm, hn_hbm, dwq_ref, dwkv_ref, dmeta_ref, dpw_ref, dg_ref, dgf_ref, dgq_ref, dgkv_ref, dps_ref,
             loss_ref, gwin_o, gwq_o, gwkv_o, gmeta_o, gpw_o, gg_o, ggf_o, ggq_o, ggkv_o, gps_o, gloss_o,
             ow2, ow3, ow4, sb0, sb2, sb3, sb4, st0, st2, st3, st4, rc0, rc2, rc3, rc4,
             vec, sm_sb0, sm_sb1, sm_cs0, sm_cs1, sm_rc0, sm_rc1, vec_fin, slab_v, hn_v, dwin_buf, own0,
             own_sems, d2d_send, d2d_recv, ici_send, ici_recv, fin_send, fin_recv,
             swap_send, swap_recv, smi_send, smi_recv, smf_send, smf_recv, part_send, part_recv, ld_sems):
        x, y, c = lax.axis_index("x"), lax.axis_index("y"), lax.axis_index("c")
        me = 2 * x + y
        _peer_signal(x, y, c)
        grads = (None, None, dwq_ref, dwkv_ref, dmeta_ref)
        outs = (gwin_o, None, gwq_o, gwkv_o, gmeta_o)
        own_buf = (None, None, ow2, ow3, ow4)
        sib_buf = (sb0, None, sb2, sb3, sb4)
        stage = (st0, None, st2, st3, st4)
        recv = (rc0, None, rc2, rc3, rc4)
        sm_mine = (dpw_ref, vec)
        sm_sib = (sm_sb0, sm_sb1)
        sm_chip = (sm_cs0, sm_cs1)
        sm_recv = (sm_rc0, sm_rc1)
        sm_out = (gpw_o, vec_fin)
        sibling = (x, y, 1 - c)

        def chip_of(rel):
            fx, fy = _CHIP_RELS[rel]
            return 2 * (x ^ fx) + (y ^ fy)

        def same_core_of(rel):
            fx, fy = _CHIP_RELS[rel]
            return (x ^ fx, y ^ fy, c)

        hn_load = pltpu.make_async_copy(hn_hbm, hn_v, ld_sems.at[CHIPS])

        def slab_load(rel):
            return pltpu.make_async_copy(dsl_hbm.at[chip_of(rel)], slab_v.at[rel], ld_sems.at[rel])

        hn_load.start()
        slab_load(shard_order[0]).start()

        def slot(bufs, i, idx):
            arr, _, n, _ = _PIECES[i]
            return bufs[arr].at[idx, pl.ds(0, n), :]

        def own_load(rel, i):
            arr, r0, n, _ = _PIECES[i]
            return pltpu.make_async_copy(_gathered_at(grads, arr, chip_of(rel), r0, n), slot(own_buf, i, rel),
                                         own_sems.at[rel * _NP + i])

        def d2d_copy(rel, i):
            arr, r0, n, _ = _PIECES[i]
            k = rel * _NP + i
            return _remote(_gathered_at(grads, arr, chip_of(rel), r0, n), slot(sib_buf, i, rel),
                           d2d_send.at[k], d2d_recv.at[k], sibling)

        def ici_copy(rel, i):
            k = (rel - 1) * _NP + i
            return _remote(slot(stage, i, rel - 1), slot(recv, i, rel - 1), ici_send.at[k], ici_recv.at[k],
                           same_core_of(rel))

        def fin_copy(i):
            arr, r0, n, _ = _PIECES[i]
            place = outs[arr].at[pl.ds(r0, n), :]
            return _remote(place, place, fin_send.at[i], fin_recv.at[i], sibling)

        def sm_ici_copy(rel, j):
            blk, r0, n, _ = _SM_PIECES[j]
            k = (rel - 1) * _NSP + j
            return _remote(sm_chip[blk].at[pl.ds(r0, n), :], sm_recv[blk].at[rel - 1, pl.ds(r0, n), :],
                           smi_send.at[k], smi_recv.at[k], same_core_of(rel))

        def sm_fin_copy(j):
            blk, r0, n, _ = _SM_PIECES[j]
            place = sm_out[blk].at[pl.ds(r0, n), :]
            return _remote(place, place, smf_send.at[j], smf_recv.at[j], sibling)

        vec[...] = jnp.zeros_like(vec)
        vec[0:1, :] = dg_ref[...]
        vec[1:2, :] = dgf_ref[...]
        vec[2:3, V_GQ:V_GQ + Q_LORA] = dgq_ref[...]
        vec[2:3, V_GKV:V_GKV + KV_LORA] = dgkv_ref[...]
        vec[2:3, V_PS:V_PS + D_POOL] = dps_ref[...]
        vec[2:3, V_LOSS:D] = loss_ref[...]
        _peer_wait()
        swaps = [_remote(sm_mine[b], sm_sib[b], swap_send.at[b], swap_recv.at[b], sibling) for b in (0, 1)]
        for cp in swaps:
            cp.start()

        for core in (0, 1):
            @pl.when(c == core)
            def _(core=core):
                mine = [i for i in range(_NP) if _PIECES[i][3] == core and _PIECES[i][0] in loaded]
                theirs = [i for i in range(_NP) if _PIECES[i][3] != core and _PIECES[i][0] in loaded]
                i0 = next(i for i in range(_NP) if _PIECES[i][0] == 0 and _PIECES[i][3] == core)
                j0 = next(i for i in range(_NP) if _PIECES[i][0] == 0 and _PIECES[i][3] != core)
                sm_mine_p = [j for j in range(_NSP) if _SM_PIECES[j][3] == core]
                sm_theirs_p = [j for j in range(_NSP) if _SM_PIECES[j][3] != core]
                sends = list(swaps)

                for rel in shard_order:
                    for i in theirs:
                        cp = d2d_copy(rel, i)
                        cp.start()
                        sends.append(cp)
                    for i in mine:
                        own_load(rel, i).start()

                def piece_rows(i):
                    return pl.ds(_PIECES[i][1], _PIECES[i][2])

                def form(rel, i):
                    r0, n = _PIECES[i][1], _PIECES[i][2]
                    dwin_buf[rel, r0:r0 + n, :] = _tn(slab_v[rel, :, r0:r0 + _PIECE_MAX[0]], hn_v[...])[0:n, :]

                def d2d0(rel, i):
                    return _remote(dwin_buf.at[rel, piece_rows(i), :], slot(sib_buf, i, rel),
                                   d2d_send.at[rel * _NP + i], d2d_recv.at[rel * _NP + i], sibling)

                def ici_parts(rel):
                    cuts = (0, W_IN_PART_ROWS, _PIECES[i0][2])
                    return [_remote(stage[0].at[rel - 1, pl.ds(a, b - a), :], recv[0].at[rel - 1, pl.ds(a, b - a), :],
                                    part_send.at[2 * (rel - 1) + t], part_recv.at[2 * (rel - 1) + t], same_core_of(rel))
                            for t, (a, b) in enumerate(zip(cuts[:-1], cuts[1:]))]

                def settle(rel):
                    d2d0(rel, i0).wait_recv()
                    total = dwin_buf[rel, piece_rows(i0), :] + slot(sib_buf, i0, rel)[...]
                    if rel == 0:
                        own0[0:_PIECES[i0][2], :] = total
                    else:
                        slot(stage, i0, rel - 1)[...] = total.astype(BF16)
                        for cp in ici_parts(rel):
                            cp.start()
                            sends.append(cp)

                hn_load.wait()
                for n, rel in enumerate(shard_order):
                    slab_load(rel).wait()
                    if n == 0:
                        for later in shard_order[1:]:
                            slab_load(later).start()
                    form(rel, j0)
                    cp = d2d0(rel, j0)
                    cp.start()
                    sends.append(cp)
                    if n > 0:
                        settle(shard_order[n - 1])
                    form(rel, i0)
                settle(shard_order[-1])

                for rel in SEND_ORDER:
                    for i in mine:
                        arr, r0, n, _ = _PIECES[i]
                        own_load(rel, i).wait()
                        d2d_copy(rel, i).wait_recv()
                        total = slot(own_buf, i, rel)[...] + slot(sib_buf, i, rel)[...]
                        slot(stage, i, rel - 1)[...] = total.astype(stage[arr].dtype)
                        cp = ici_copy(rel, i)
                        cp.start()
                        sends.append(cp)

                for b in (0, 1):
                    swaps[b].wait_recv()
                    sm_chip[b][...] = sm_mine[b][...] + sm_sib[b][...]
                for rel in SEND_ORDER:
                    for j in sm_mine_p:
                        cp = sm_ici_copy(rel, j)
                        cp.start()
                        sends.append(cp)

                for i in mine:
                    arr, r0, n, _ = _PIECES[i]
                    own_load(0, i).wait()
                    d2d_copy(0, i).wait_recv()
                    total = slot(own_buf, i, 0)[...] + slot(sib_buf, i, 0)[...]
                    for rel in (1, 2, 3):
                        ici_copy(rel, i).wait_recv()
                        total = total + slot(recv, i, rel - 1)[...].astype(F32)
                    outs[arr][pl.ds(r0, n), :] = total
                    cp = fin_copy(i)
                    cp.start()
                    sends.append(cp)
                total = own0[0:_PIECES[i0][2], :]
                for rel in (1, 2, 3):
                    for cp in ici_parts(rel):
                        cp.wait_recv()
                    total = total + slot(recv, i0, rel - 1)[...].astype(F32)
                outs[0][pl.ds(_PIECES[i0][1], _PIECES[i0][2]), :] = total
                cp = fin_copy(i0)
                cp.start()
                sends.append(cp)

                for j in sm_mine_p:
                    blk, r0, n, _ = _SM_PIECES[j]
                    for rel in (1, 2, 3):
                        sm_ici_copy(rel, j).wait_recv()
                    total = jnp.zeros((n, _SM_COLS[blk]), F32)
                    for chip in range(CHIPS):
                        flips = chip ^ me
                        rel = jnp.where(flips == 2, 1, jnp.where(flips == 1, 2, flips))
                        theirs_rows = sm_recv[blk][jnp.maximum(rel - 1, 0), pl.ds(r0, n), :]
                        total = total + jnp.where(rel == 0, sm_chip[blk][pl.ds(r0, n), :], theirs_rows)
                    sm_out[blk][pl.ds(r0, n), :] = total
                    cp = sm_fin_copy(j)
                    cp.start()
                    sends.append(cp)

                for i in theirs + [j0]:
                    fin_copy(i).wait_recv()
                for j in sm_theirs_p:
                    sm_fin_copy(j).wait_recv()
                for cp in sends:
                    cp.wait_send()

        gg_o[...] = vec_fin[0:1, :]
        ggf_o[...] = vec_fin[1:2, :]
        ggq_o[...] = vec_fin[2:3, V_GQ:V_GQ + Q_LORA]
        ggkv_o[...] = vec_fin[2:3, V_GKV:V_GKV + KV_LORA]
        gps_o[...] = vec_fin[2:3, V_PS:V_PS + D_POOL]
        gloss_o[...] = vec_fin[2:3, V_LOSS:D]

    vm = pl.BlockSpec(memory_space=pltpu.VMEM)
    piece_buf = lambda lead, dtype, which=arrays: [
        pltpu.VMEM((lead, _PIECE_MAX[a], _ARR_COLS[a]), F32 if a == 4 else dtype) for a in which]
    sm_buf = lambda *lead: [pltpu.VMEM(lead + (_SM_ROWS[b], _SM_COLS[b]), F32) for b in (0, 1)]
    dma = lambda n: [pltpu.SemaphoreType.DMA((n,))] * 2
    return pl.pallas_call(
        body,
        name="reduce_grads",
        in_specs=[pl.BlockSpec(memory_space=pl.ANY)] * 4 + [vm] * 8,
        out_specs=[vm] * 11,
        out_shape=[jax.ShapeDtypeStruct((_ARR_ROWS[a], _ARR_COLS[a]), F32) for a in arrays]
        + [jax.ShapeDtypeStruct((_SM_ROWS[0], _SM_COLS[0]), F32), jax.ShapeDtypeStruct((1, D), F32),
           jax.ShapeDtypeStruct((1, D), F32), jax.ShapeDtypeStruct((1, Q_LORA), F32),
           jax.ShapeDtypeStruct((1, KV_LORA), F32), jax.ShapeDtypeStruct((1, D_POOL), F32),
           jax.ShapeDtypeStruct((1, 128), F32)],
        scratch_shapes=piece_buf(CHIPS, F32, loaded) + piece_buf(CHIPS, F32) + piece_buf(3, BF16) + piece_buf(3, BF16)
        + [pltpu.VMEM((VEC_ROWS, D), F32)] + sm_buf() + sm_buf() + sm_buf(3) + [pltpu.VMEM((VEC_ROWS, D), F32)]
        + [pltpu.VMEM((CHIPS, N, SHARD_PAD), BF16), pltpu.VMEM((N, D), BF16),
           pltpu.VMEM((CHIPS, SHARD_PAD, D), F32), pltpu.VMEM((_PIECE_MAX[0], D), F32)]
        + [pltpu.SemaphoreType.DMA((CHIPS * _NP,))]
        + dma(CHIPS * _NP) + dma(3 * _NP) + dma(_NP) + dma(2) + dma(3 * _NSP) + dma(_NSP) + dma(3 * 2)
        + [pltpu.SemaphoreType.DMA((CHIPS + 1,))],
        compiler_params=_cparams(collective_id=3),
    )(dsl, hn, dwq, dwkv, dmeta4, dpw, dg, dgf, dgq, dgkv, dps, loss)


def _adamw_math(w, g, m, v):
    m = B1 * m + (1.0 - B1) * g
    v = B2 * v + (1.0 - B2) * (g * g)
    m_hat = m / C1
    v_hat = v / C2
    delta = -LR * (m_hat / (jnp.sqrt(v_hat) + ADAM_EPS) + WD * w)
    return delta, m, v


def _adamw(big, block_rows, groups):
    rows, cols = big[0].shape
    n = len(groups)

    def body(*refs):
        w_ref, g_ref, m_ref, v_ref = refs[0:4]
        small_in = refs[4:4 + 4 * n]
        go_ref, d_ref, nm_ref, nv_ref = refs[4 + 4 * n:8 + 4 * n]
        small_out = refs[8 + 4 * n:]
        g = g_ref[...]
        go_ref[...] = g
        d_ref[...], nm_ref[...], nv_ref[...] = _adamw_math(w_ref[...], g, m_ref[...], v_ref[...])

        @pl.when(pl.program_id(0) == 0)
        def _():
            for t in range(n):
                sw_ref, sg_ref, sm_ref, sv_ref = small_in[4 * t:4 * t + 4]
                sg = sg_ref[0:sw_ref.shape[0], :]
                small_out[4 * t][...] = sg
                small_out[4 * t + 1][...], small_out[4 * t + 2][...], small_out[4 * t + 3][...] = _adamw_math(
                    sw_ref[...], sg, sm_ref[...], sv_ref[...])

    spec = pl.BlockSpec((block_rows, cols), lambda i: (i, 0))
    vm = pl.BlockSpec(memory_space=pltpu.VMEM)
    outs = pl.pallas_call(
        body,
        name="adamw",
        grid=(rows // block_rows,),
        in_specs=[spec] * 4 + [vm] * (4 * n),
        out_specs=[spec] * 4 + [vm] * (4 * n),
        out_shape=[jax.ShapeDtypeStruct(big[0].shape, F32)] * 4
        + [jax.ShapeDtypeStruct(grp[0].shape, F32) for grp in groups for _ in range(4)],
        compiler_params=_cparams(dimension_semantics=("arbitrary",)),
    )(*big, *[a for grp in groups for a in grp])
    return tuple(outs[0:4]), [tuple(outs[4 + 4 * t:8 + 4 * t]) for t in range(n)]


def _rope_tables():
    half = QK_ROPE // 2
    f32 = np.float32
    inv_freq = (f32(1.0) / (f32(ROPE_THETA) ** (np.arange(half, dtype=f32) / f32(half)))).astype(f32)
    pos = np.arange(N, dtype=f32) - f32(PAD)
    ang = (pos[:, None] * inv_freq[None, :]).astype(f32)
    cos, sin = np.cos(ang).astype(f32), np.sin(ang).astype(f32)
    zero = np.zeros((N, 128 - QK_ROPE), f32)
    return jnp.asarray(np.concatenate([cos, cos, zero], axis=1)), jnp.asarray(np.concatenate([-sin, sin, zero], axis=1))


def kernel(x, meta_tokens, norm_g, w_in, q_norm_g, w_q_b, kv_norm_g, w_kv_b, pool_w, pool_scale, w_out, final_norm_g, loss_target, m_meta_tokens, m_norm_g, m_w_in, m_q_norm_g, m_w_q_b, m_kv_norm_g, m_w_kv_b, m_pool_w, m_pool_scale, m_w_out, m_final_norm_g, v_meta_tokens, v_norm_g, v_w_in, v_q_norm_g, v_w_q_b, v_kv_norm_g, v_w_kv_b, v_pool_w, v_pool_scale, v_w_out, v_final_norm_g):
    tr = lambda a: a[0].T
    win, wq, wkv, h, tgt = _gather_weights(tr(w_in), tr(w_q_b), w_kv_b[0], meta_tokens, x[0], loss_target[0])
    cosf, sinf = _rope_tables()
    gf = final_norm_g.reshape(1, D)

    part = _local_step(h, tgt, norm_g, win, q_norm_g, wq, kv_norm_g, wkv, pool_w[0], pool_scale, w_out[0], m_w_out[0],
                       v_w_out[0], gf, cosf, sinf)

    pw2 = lambda a: a.reshape(len(POOL_WINDOWS) * POOL_GROUP, POOL_GROUP)
    gwinT, gwqT, gwkv, gmeta, gpw, gg, ggf, ggq, ggkv, gps, gloss = _reduce_grads(
        part["dsl"], part["hn"], part["dwq"], part["dwkv"], part["dmeta"], pw2(part["dpw"]), part["dg"],
        part["dgf"], part["dgq"], part["dgkv"], part["dps"], part["loss"])

    r_out = part["r_out"]
    fn2 = lambda a: a.reshape(1, D)
    r_in, (r_meta, r_norm, r_gq, r_wq, r_gkv, r_wkv, r_pw, r_ps, r_fn) = _adamw((tr(w_in), gwinT, tr(m_w_in), tr(v_w_in)), 248, [
        (meta_tokens, gmeta, m_meta_tokens, v_meta_tokens),
        (norm_g, gg, m_norm_g, v_norm_g),
        (q_norm_g, ggq, m_q_norm_g, v_q_norm_g),
        (tr(w_q_b), gwqT, tr(m_w_q_b), tr(v_w_q_b)),
        (kv_norm_g, ggkv, m_kv_norm_g, v_kv_norm_g),
        (w_kv_b[0], gwkv, m_w_kv_b[0], v_w_kv_b[0]),
        (pw2(pool_w), gpw, pw2(m_pool_w), pw2(v_pool_w)),
        (pool_scale, gps, m_pool_scale, v_pool_scale),
        (fn2(final_norm_g), ggf, fn2(m_final_norm_g), fn2(v_final_norm_g)),
    ])
    untr = lambda a: a.T[None]
    pw4 = lambda a: a.reshape(1, len(POOL_WINDOWS), POOL_GROUP, POOL_GROUP)
    per_kind = [[
        r_meta[kind], r_norm[kind], untr(r_in[kind]), r_gq[kind], untr(r_wq[kind]), r_gkv[kind], r_wkv[kind][None],
        pw4(r_pw[kind]), r_ps[kind], r_out[kind][None], r_fn[kind].reshape(D),
    ] for kind in range(4)]
    return (gloss[0, 0], part["gx"][None], *per_kind[0], *per_kind[1], *per_kind[2], *per_kind[3])
```

```python
import jax
import jax.numpy as jnp
import numpy as np
from jax import lax
from jax.experimental import pallas as pl
from jax.experimental.pallas import tpu as pltpu

F32 = jnp.float32
BF16 = jnp.bfloat16

D = 1024
S = 2048
N_META = 16
PAD = 112
HEAD_ROWS = PAD + N_META
N = HEAD_ROWS + S
D_POOL = 512
POOL_WINDOWS = (2, 4, 8, 16)
POOL_GROUP = 128
HALO = 16
HEADS = 4
QK_NOPE = 128
QK_ROPE = 64
QK = QK_NOPE + QK_ROPE
V_HEAD = 128
Q_LORA = 256
KV_LORA = 128
D_IN = 1984
EPS = 1e-6
ROPE_THETA = 10000.0
SCALE = QK ** -0.5
CHIPS = 4

ROWS_FWD = 544
ROWS_MID = 544
ROWS_BWD = 544
TK = 128
TQ = 256
NQ = S // TQ
HEADS_PER_STEP_BWD = 2

O_PI, O_PG, O_CQ, O_CKV, O_KR, O_AG = 0, 512, 1024, 1280, 1408, 1472
O_KR_END = O_KR + 128
SHARD_IN = D_IN // CHIPS
SHARD_PAD = 512
SHARD_OUT = D // CHIPS

LR, B1, B2, ADAM_EPS, WD, STEP = 0.001, 0.9, 0.999, 1e-08, 0.01, 10
C1 = 1.0 - B1**STEP
C2 = 1.0 - B2**STEP

VMEM_LIMIT = 60 * 1024 * 1024
MESH = pl.DeviceIdType.MESH
NEG = -1e30

VEC_ROWS = 8
V_GQ, V_GKV, V_PS, V_LOSS = 0, 256, 384, 896


def _cparams(**kw):
    return pltpu.CompilerParams(vmem_limit_bytes=VMEM_LIMIT, **kw)


def _nt(a, b):
    return lax.dot_general(a, b, (((1,), (1,)), ((), ())), preferred_element_type=F32)


def _tn(a, b):
    return lax.dot_general(a, b, (((0,), (0,)), ((), ())), preferred_element_type=F32)


def _nn(a, b):
    return jnp.dot(a, b, preferred_element_type=F32)


def _swap64(t):
    return pltpu.roll(t, 32, 1) + pltpu.roll(t, 96, 1)


def _sigmoid(x):
    return 1.0 / (1.0 + jnp.exp(-x))


def _low_lanes():
    return (lax.broadcasted_iota(jnp.int32, (1, 128), 1) < QK_ROPE).astype(F32)


def _rows(w, rows):
    return pl.BlockSpec((rows, w), lambda i: (i, 0))


def _const(*shape):
    return pl.BlockSpec(shape, lambda *_: (0,) * len(shape), pipeline_mode=pl.Buffered(1))


STAT_GROUPS = HEADS // HEADS_PER_STEP_BWD


def _stat_slot(head):
    return head // HEADS_PER_STEP_BWD, head % HEADS_PER_STEP_BWD


N_PEERS = 4
SEND_ORDER = (3, 1, 2)


def _peer_signal(x, y, c):
    barrier = pltpu.get_barrier_semaphore()
    peers = [(x, y, 1 - c)] + [(x ^ fx, y ^ fy, c) for fx, fy in _CHIP_RELS[1:]]
    assert len(peers) == N_PEERS
    for peer in peers:
        pl.semaphore_signal(barrier, inc=1, device_id=peer, device_id_type=MESH)


def _peer_wait():
    pl.semaphore_wait(pltpu.get_barrier_semaphore(), N_PEERS)


def _attn_tiles():
    return [(0, TK, TK)] + [(TK + TQ * t, TQ, TK + TQ * (t + 1)) for t in range(NQ)]


def _masked_scores(q, k, rows, klen):
    s = _nt(q, k)
    col = lax.broadcasted_iota(jnp.int32, (1, TK), 1)
    head_bias = jnp.where(col >= PAD, 0.0, NEG)
    if klen == TK:
        return s + head_bias
    r = lax.broadcasted_iota(jnp.int32, (rows, 1), 0) >> 6
    c = lax.broadcasted_iota(jnp.int32, (1, rows), 1) >> 6
    diag_bias = jnp.where(c <= r, 0.0, NEG)
    parts = [s[:, 0:TK] + head_bias]
    if klen - rows > TK:
        parts.append(s[:, TK:klen - rows])
    parts.append(s[:, klen - rows:klen] + diag_bias)
    return jnp.concatenate(parts, axis=1)


def _fwd_in(h, norm_g, win, gq, wq, gkv, wkv, cosf, sinf):
    tr = ROWS_FWD

    def body(h_ref, g_ref, win_ref, gq_ref, wq_ref, gkv_ref, wkv_ref, cos_ref, sin_ref,
             pi_ref, pg_ref, cq_ref, ckv_ref, ag_ref, q_ref, k_ref, v_ref, hn_ref):
        h = h_ref[...]
        r = lax.rsqrt(jnp.mean(h * h, axis=-1, keepdims=True) + EPS)
        hn = ((h * r) * g_ref[...]).astype(BF16)
        hn_ref[...] = hn
        u = _nt(hn, win_ref[0:O_KR_END, :])
        pi_ref[...] = u[:, O_PI:O_PG]
        pg_ref[...] = u[:, O_PG:O_CQ]
        cq = u[:, O_CQ:O_CKV]
        ckv = u[:, O_CKV:O_KR]
        cq_ref[...] = cq
        ckv_ref[...] = ckv
        ag_ref[...] = _nt(hn, win_ref[O_AG:D_IN, :])
        cosv = cos_ref[...]
        sinv = sin_ref[...]
        kr = u[:, O_KR:O_KR_END] * _low_lanes()
        kr = (kr * cosv + _swap64(kr) * sinv).astype(BF16)
        rq = lax.rsqrt(jnp.mean(cq * cq, axis=-1, keepdims=True) + EPS)
        cqn = ((cq * rq) * gq_ref[...]).astype(BF16)
        rkv = lax.rsqrt(jnp.mean(ckv * ckv, axis=-1, keepdims=True) + EPS)
        ckvn = ((ckv * rkv) * gkv_ref[...]).astype(BF16)
        for hd in range(HEADS):
            qh = _nt(cqn, wq_ref[hd]) * SCALE
            z = qh[:, QK_NOPE:]
            q_ref[hd, :, 0:QK_NOPE] = qh[:, 0:QK_NOPE].astype(BF16)
            q_ref[hd, :, QK_NOPE:] = (z * cosv + _swap64(z) * sinv).astype(BF16)
            kvh = _nn(ckvn, wkv_ref[hd])
            k_ref[hd, :, 0:QK_NOPE] = kvh[:, 0:QK_NOPE].astype(BF16)
            k_ref[hd, :, QK_NOPE:] = kr
            v_ref[hd] = kvh[:, QK_NOPE:].astype(BF16)

    head = lambda w: pl.BlockSpec((HEADS, tr, w), lambda i: (0, i, 0))
    return pl.pallas_call(
        body,
        name="fwd_in",
        grid=(N // tr,),
        in_specs=[
            _rows(D, tr), _const(1, D), _const(D_IN, D), _const(1, Q_LORA), _const(HEADS, 256, Q_LORA),
            _const(1, KV_LORA), _const(HEADS, KV_LORA, 256), _rows(128, tr), _rows(128, tr),
        ],
        out_specs=[_rows(D_POOL, tr), _rows(D_POOL, tr), _rows(Q_LORA, tr), _rows(KV_LORA, tr), _rows(D_POOL, tr),
                   head(256), head(256), head(V_HEAD), _rows(D, tr)],
        out_shape=[
            jax.ShapeDtypeStruct((N, D_POOL), F32), jax.ShapeDtypeStruct((N, D_POOL), F32),
            jax.ShapeDtypeStruct((N, Q_LORA), F32), jax.ShapeDtypeStruct((N, KV_LORA), F32),
            jax.ShapeDtypeStruct((N, D_POOL), F32),
            jax.ShapeDtypeStruct((HEADS, N, 256), BF16), jax.ShapeDtypeStruct((HEADS, N, 256), BF16),
            jax.ShapeDtypeStruct((HEADS, N, V_HEAD), BF16), jax.ShapeDtypeStruct((N, D), BF16),
        ],
        compiler_params=_cparams(dimension_semantics=("arbitrary",)),
    )(h, norm_g, win, gq, wq, gkv, wkv, cosf, sinf)


def _attn_fwd(q, k, v, wout_s):
    tiles = _attn_tiles()
    n_t = len(tiles)
    half = SHARD_OUT // 2
    send_step = 2
    fwd_step = n_t - 2

    def body(q_hbm, k_hbm, v_hbm, wout_ref, o_hbm, lse_ref, wout_o, q_buf, k_buf, v_buf, o_buf, s_wout, in_sems, out_sems,
             ici_send, ici_recv, fwd_send, fwd_recv, own_sem):
        step = pl.program_id(0)
        x, y, c = lax.axis_index("x"), lax.axis_index("y"), lax.axis_index("c")
        me = 2 * x + y

        def chip_of(rel):
            fx, fy = _CHIP_RELS[rel]
            return 2 * (x ^ fx) + (y ^ fy)

        def place(chip, core):
            return wout_o.at[pl.ds(pl.multiple_of(SHARD_OUT * chip + half * core, half), half), :]

        def ici_copy(rel, src_chip, to):
            return _remote(s_wout.at[pl.ds(pl.multiple_of(half * c, half), half), :], place(src_chip, c),
                           ici_send.at[rel - 1], ici_recv.at[rel - 1], to)

        def fwd_copy(rel, core, to):
            spot = place(chip_of(rel), core)
            return _remote(spot, spot, fwd_send.at[rel - 1], fwd_recv.at[rel - 1], to)

        own = pltpu.make_async_copy(s_wout, wout_o.at[pl.ds(pl.multiple_of(SHARD_OUT * me, SHARD_OUT), SHARD_OUT), :], own_sem)

        @pl.when(step == 0)
        def _():
            _peer_signal(x, y, c)
            s_wout[...] = wout_ref[...].astype(BF16)
            own.start()

        @pl.when(step == send_step)
        def _():
            _peer_wait()
            for rel in SEND_ORDER:
                fx, fy = _CHIP_RELS[rel]
                ici_copy(rel, me, (x ^ fx, y ^ fy, c)).start()

        @pl.when(step == fwd_step)
        def _():
            for rel in (1, 2, 3):
                ici_copy(rel, chip_of(rel), (x, y, c)).wait_recv()
                fwd_copy(rel, c, (x, y, 1 - c)).start()

        def finish_wout():
            for rel in (1, 2, 3):
                fwd_copy(rel, 1 - c, (x, y, c)).wait_recv()
            for rel in (1, 2, 3):
                ici_copy(rel, me, (x, y, c)).wait_send()
                fwd_copy(rel, c, (x, y, c)).wait_send()
            own.wait()

        def loads(idx):
            q0, rows, _ = tiles[idx]
            rs = pl.ds(q0, rows)
            return [pltpu.make_async_copy(src.at[:, rs, :], dst.at[:, rs, :], in_sems.at[a, idx % 2])
                    for a, (src, dst) in enumerate(((q_hbm, q_buf), (k_hbm, k_buf), (v_hbm, v_buf)))]

        def store(idx):
            q0, rows, _ = tiles[idx]
            return pltpu.make_async_copy(o_buf.at[idx % 2, pl.ds(0, rows), :], o_hbm.at[pl.ds(q0, rows), :],
                                         out_sems.at[idx % 2])

        @pl.when(step == 0)
        def _():
            lse_ref[...] = jnp.zeros_like(lse_ref)
            for cp in loads(0):
                cp.start()

        for idx, (q0, rows, klen) in enumerate(tiles):
            @pl.when(step == idx)
            def _(idx=idx, q0=q0, rows=rows, klen=klen):
                for cp in loads(idx):
                    cp.wait()
                if idx + 1 < n_t:
                    for cp in loads(idx + 1):
                        cp.start()
                if idx >= 2:
                    store(idx - 2).wait()
                for hd in range(HEADS):
                    s = _masked_scores(q_buf[hd, q0:q0 + rows, :], k_buf[hd, 0:klen, :], rows, klen)
                    m = jnp.max(s, axis=-1, keepdims=True)
                    p = jnp.exp(s - m)
                    l = jnp.sum(p, axis=-1, keepdims=True)
                    o_buf[idx % 2, 0:rows, hd * V_HEAD:(hd + 1) * V_HEAD] = _nn(p.astype(BF16), v_buf[hd, 0:klen, :]) / l
                    grp, lane = _stat_slot(hd)
                    lse_ref[grp, q0:q0 + rows, lane:lane + 1] = m + jnp.log(l)
                store(idx).start()
                if idx == n_t - 1:
                    store(idx - 1).wait()
                    store(idx).wait()
                    finish_wout()

    hbm = pl.BlockSpec(memory_space=pl.ANY)
    return pl.pallas_call(
        body,
        name="attn_fwd",
        grid=(n_t,),
        in_specs=[hbm, hbm, hbm, _const(SHARD_OUT, D)],
        out_specs=[hbm, _const(STAT_GROUPS, N, 128), hbm],
        out_shape=[jax.ShapeDtypeStruct((N, HEADS * V_HEAD), F32), jax.ShapeDtypeStruct((STAT_GROUPS, N, 128), F32),
                   jax.ShapeDtypeStruct((D, D), BF16)],
        scratch_shapes=[pltpu.VMEM((HEADS, N, 256), BF16), pltpu.VMEM((HEADS, N, 256), BF16),
                        pltpu.VMEM((HEADS, N, V_HEAD), BF16), pltpu.VMEM((2, TQ, HEADS * V_HEAD), F32),
                        pltpu.VMEM((SHARD_OUT, D), BF16),
                        pltpu.SemaphoreType.DMA((3, 2)), pltpu.SemaphoreType.DMA((2,))]
        + [pltpu.SemaphoreType.DMA((3,))] * 4 + [pltpu.SemaphoreType.DMA],
        compiler_params=_cparams(dimension_semantics=("arbitrary",), collective_id=1),
    )(q, k, v, wout_s)


def _inv_count(row0, rows, w):
    row = row0 + lax.broadcasted_iota(jnp.int32, (rows, 1), 0)
    return 1.0 / jnp.clip(row - (PAD - 1), 1, w).astype(F32)


def _mid(h, tgt, pool_in, pool_gate, attn_gate, attn, pool_w, pool_scale, wout, gf):
    tr = ROWS_MID
    per = tr // HALO
    ng = len(POOL_WINDOWS)

    def body(h_ref, t_ref, pin_ref, halo_ref, pg_ref, ag_ref, at_ref, pw_ref, ps_ref, wout_ref, gf_ref,
             dh2_ref, do_ref, delta_ref, dag_ref, dpg_ref, dpl_ref, dwout_ref, dpw_ref, dps_ref, dgf_ref, loss_ref):
        i = pl.program_id(0)

        @pl.when(i == 0)
        def _():
            dwout_ref[...] = jnp.zeros_like(dwout_ref)
            dpw_ref[...] = jnp.zeros_like(dpw_ref)
            dps_ref[...] = jnp.zeros_like(dps_ref)
            dgf_ref[...] = jnp.zeros_like(dgf_ref)
            loss_ref[...] = jnp.zeros_like(loss_ref)

        row0 = i * tr
        real = (row0 + lax.broadcasted_iota(jnp.int32, (tr, 1), 0)) >= HEAD_ROWS
        h = h_ref[...]

        halo = jnp.where(i > 0, halo_ref[...], 0.0)
        ext = jnp.concatenate([halo, pin_ref[...]], axis=0)
        pooled = []
        for g, w in enumerate(POOL_WINDOWS):
            e = ext[:, g * POOL_GROUP:(g + 1) * POOL_GROUP]
            acc = e
            shift = 1
            while shift < w:
                acc = acc + pltpu.roll(acc, shift, 0)
                shift *= 2
            pooled.append((acc[HALO:] * _inv_count(row0, tr, w) - e[HALO:]).astype(BF16))
        pw = [pw_ref[g].astype(BF16) for g in range(ng)]
        mixed = jnp.concatenate([_nn(pooled[g], pw[g]) for g in range(ng)], axis=1)
        ps = ps_ref[...]
        mixed_s = mixed * ps
        pg = pg_ref[...]
        sig_p = _sigmoid(pg)
        silu_p = pg * sig_p
        pool_out = (silu_p * mixed_s).astype(BF16)
        ag = ag_ref[...]
        sig_a = _sigmoid(ag)
        silu_a = ag * sig_a
        at = at_ref[...]
        attn_out = (silu_a * at).astype(BF16)
        cat = jnp.concatenate([pool_out, attn_out], axis=1)
        h2 = h + _nn(cat, wout_ref[...])

        r2 = lax.rsqrt(jnp.mean(h2 * h2, axis=-1, keepdims=True) + EPS)
        n2 = h2 * r2
        gfv = gf_ref[...]
        err = jnp.where(real, n2 * gfv - t_ref[...], 0.0)
        loss_ref[...] += jnp.sum(jnp.sum(err * err, axis=-1, keepdims=True), axis=0, keepdims=True) * (0.5 / D)
        dy = err * (1.0 / D)
        dgf_ref[...] += jnp.sum(dy * n2, axis=0, keepdims=True)
        dn = dy * gfv
        dh2 = r2 * (dn - n2 * jnp.mean(dn * n2, axis=-1, keepdims=True))
        dh2_ref[...] = dh2
        dh2b = dh2.astype(BF16)

        dwout_ref[...] += _tn(cat, dh2b)
        dcat = _nt(dh2b, wout_ref[...])
        dpo = dcat[:, 0:D_POOL]
        dao = dcat[:, D_POOL:D]
        do = dao * silu_a
        prod = do * at
        delta_ref[...] = jnp.zeros_like(delta_ref)
        for hd in range(HEADS):
            grp, lane = _stat_slot(hd)
            cols = slice(hd * V_HEAD, (hd + 1) * V_HEAD)
            do_ref[grp, :, lane * V_HEAD:(lane + 1) * V_HEAD] = do[:, cols].astype(BF16)
            delta_ref[grp, :, lane:lane + 1] = jnp.sum(prod[:, cols], axis=-1, keepdims=True)
        dag_ref[...] = (dao * at * (sig_a * (1.0 + ag * (1.0 - sig_a)))).astype(BF16)
        dmixed_s = dpo * silu_p
        dpg_ref[...] = (dpo * mixed_s * (sig_p * (1.0 + pg * (1.0 - sig_p)))).astype(BF16)
        dps_ref[...] += jnp.sum(dmixed_s * mixed, axis=0, keepdims=True)
        dmixed = (dmixed_s * ps).astype(BF16)
        dpl = []
        for g in range(ng):
            dm = dmixed[:, g * POOL_GROUP:(g + 1) * POOL_GROUP]
            dpl.append(_nt(dm, pw[g]))
            dpw_ref[g] += _tn(pooled[g], dm)
        dpl_ref[...] = jnp.concatenate(dpl, axis=1)

    halo_spec = pl.BlockSpec((HALO, D_POOL), lambda i: (jnp.maximum(i * per - 1, 0), 0))
    return pl.pallas_call(
        body,
        name="mid",
        grid=(N // tr,),
        in_specs=[
            _rows(D, tr), _rows(D, tr), _rows(D_POOL, tr), halo_spec, _rows(D_POOL, tr), _rows(D_POOL, tr),
            _rows(D_POOL, tr), _const(ng, POOL_GROUP, POOL_GROUP), _const(1, D_POOL), _const(D, D), _const(1, D),
        ],
        out_specs=[
            _rows(D, tr), pl.BlockSpec((STAT_GROUPS, tr, HEADS_PER_STEP_BWD * V_HEAD), lambda i: (0, i, 0)),
            pl.BlockSpec((STAT_GROUPS, tr, 128), lambda i: (0, i, 0)),
            _rows(D_POOL, tr), _rows(D_POOL, tr), _rows(D_POOL, tr),
            _const(D, D), _const(ng, POOL_GROUP, POOL_GROUP), _const(1, D_POOL), _const(1, D), _const(1, 128),
        ],
        out_shape=[
            jax.ShapeDtypeStruct((N, D), F32), jax.ShapeDtypeStruct((STAT_GROUPS, N, HEADS_PER_STEP_BWD * V_HEAD), BF16),
            jax.ShapeDtypeStruct((STAT_GROUPS, N, 128), F32),
            jax.ShapeDtypeStruct((N, D_POOL), BF16), jax.ShapeDtypeStruct((N, D_POOL), BF16),
            jax.ShapeDtypeStruct((N, D_POOL), F32), jax.ShapeDtypeStruct((D, D), F32),
            jax.ShapeDtypeStruct((ng, POOL_GROUP, POOL_GROUP), F32),
            jax.ShapeDtypeStruct((1, D_POOL), F32), jax.ShapeDtypeStruct((1, D), F32), jax.ShapeDtypeStruct((1, 128), F32),
        ],
        compiler_params=_cparams(dimension_semantics=("arbitrary",)),
    )(h, tgt, pool_in, pool_in, pool_gate, attn_gate, attn, pool_w, pool_scale, wout, gf)


def _unrope(dy, cosv, sinv):
    return dy * cosv + _swap64(dy * sinv) * _low_lanes()


def _attn_bwd(q, k, v, do, lse, delta, cosf, sinf, dwout):
    tiles = _attn_tiles()
    hp = HEADS_PER_STEP_BWD
    n_g = HEADS // hp
    n_t = len(tiles)
    half = SHARD_OUT // 2
    swap_at, send_at, sum_at = (0, 3), (0, 5), (n_g - 1, n_t // 2)

    def body(q_hbm, k_hbm, v_hbm, do_hbm, lse_ref, delta_ref, cos_ref, sin_ref, dwout_hbm, dq_hbm, dkv_ref, dkr_ref,
             gwout_ref, q_buf, k_buf, v_buf, do_buf, dq_buf, dk_acc, dv_acc, own_w, sib_w, stage_w, recv_w, gw_buf,
             in_sems, out_sems, ow_sems, d2d_send, d2d_recv, ici_send, ici_recv, fin_send, fin_recv):
        grp = pl.program_id(0)
        step = pl.program_id(1)
        heads = pl.ds(grp * hp, hp)
        x, y, c = lax.axis_index("x"), lax.axis_index("y"), lax.axis_index("c")
        sibling = (x, y, 1 - c)

        def chip_of(rel):
            fx, fy = _CHIP_RELS[rel]
            return 2 * (x ^ fx) + (y ^ fy)

        def piece(chip, core):
            return dwout_hbm.at[pl.ds(pl.multiple_of(SHARD_OUT * chip + half * core, half), half), :]

        def own_load(rel):
            return pltpu.make_async_copy(piece(chip_of(rel), c), own_w.at[rel], ow_sems.at[rel])

        def d2d_copy(rel):
            return _remote(piece(chip_of(rel), 1 - c), sib_w.at[rel], d2d_send.at[rel], d2d_recv.at[rel], sibling)

        def ici_copy(rel):
            fx, fy = _CHIP_RELS[rel]
            return _remote(stage_w.at[rel - 1], recv_w.at[rel - 1], ici_send.at[rel - 1], ici_recv.at[rel - 1],
                           (x ^ fx, y ^ fy, c))

        def fin_copy(core):
            spot = gw_buf.at[pl.ds(pl.multiple_of(half * core, half), half), :]
            return _remote(spot, spot, fin_send.at[0], fin_recv.at[0], sibling)

        @pl.when((grp == 0) & (step == 0))
        def _():
            _peer_signal(x, y, c)
            for rel in SEND_ORDER + (0,):
                own_load(rel).start()

        @pl.when((grp == swap_at[0]) & (step == swap_at[1]))
        def _():
            _peer_wait()
            for rel in SEND_ORDER + (0,):
                d2d_copy(rel).start()

        @pl.when((grp == send_at[0]) & (step == send_at[1]))
        def _():
            for rel in SEND_ORDER:
                own_load(rel).wait()
                d2d_copy(rel).wait_recv()
                stage_w[rel - 1] = (own_w[rel] + sib_w[rel]).astype(BF16)
                ici_copy(rel).start()

        @pl.when((grp == sum_at[0]) & (step == sum_at[1]))
        def _():
            own_load(0).wait()
            d2d_copy(0).wait_recv()
            total = own_w[0] + sib_w[0]
            for rel in (1, 2, 3):
                ici_copy(rel).wait_recv()
                total = total + recv_w[rel - 1].astype(F32)
            gw_buf[pl.ds(pl.multiple_of(half * c, half), half), :] = total
            fin_copy(c).start()

        def finish_dwout():
            fin_copy(1 - c).wait_recv()
            for rel in (0, 1, 2, 3):
                d2d_copy(rel).wait_send()
            for rel in (1, 2, 3):
                ici_copy(rel).wait_send()
            fin_copy(c).wait_send()
            gwout_ref[...] = gw_buf[...]

        def loads(g, idx):
            q0, rows, _ = tiles[idx]
            rs = pl.ds(q0, rows)
            par = (g * n_t + idx) % 2
            hs = pl.ds(g * hp, hp)
            pairs = ((q_hbm.at[hs, rs, :], q_buf.at[:, rs, :]), (k_hbm.at[hs, rs, :], k_buf.at[:, rs, :]),
                     (v_hbm.at[hs, rs, :], v_buf.at[:, rs, :]), (do_hbm.at[g, rs, :], do_buf.at[rs, :]))
            return [pltpu.make_async_copy(src, dst, in_sems.at[a, par]) for a, (src, dst) in enumerate(pairs)]

        def store(idx):
            q0, rows, _ = tiles[idx]
            return pltpu.make_async_copy(dq_buf.at[idx % 2, :, pl.ds(0, rows), :], dq_hbm.at[heads, pl.ds(q0, rows), :],
                                         out_sems.at[idx % 2])

        @pl.when(step == 0)
        def _():
            dk_acc[...] = jnp.zeros_like(dk_acc)
            dv_acc[...] = jnp.zeros_like(dv_acc)

        @pl.when((step == 0) & (grp == 0))
        def _():
            dkr_ref[...] = jnp.zeros_like(dkr_ref)
            for cp in loads(grp, 0):
                cp.start()

        for idx, (q0, rows, klen) in enumerate(tiles):
            @pl.when(step == idx)
            def _(idx=idx, q0=q0, rows=rows, klen=klen):
                for cp in loads(grp, idx):
                    cp.wait()
                if idx + 1 < n_t:
                    for cp in loads(grp, idx + 1):
                        cp.start()
                if idx >= 2:
                    store(idx - 2).wait()
                qs = pl.ds(q0, rows)
                for hd in range(hp):
                    qv = q_buf[hd, qs, :]
                    kv = k_buf[hd, 0:klen, :]
                    p = jnp.exp(_masked_scores(qv, kv, rows, klen) - lse_ref[0, qs, hd:hd + 1])
                    dob = do_buf[qs, hd * V_HEAD:(hd + 1) * V_HEAD]
                    ds = (p * (_nt(dob, v_buf[hd, 0:klen, :]) - delta_ref[0, qs, hd:hd + 1])).astype(BF16)
                    dq = _nn(ds, kv) * SCALE
                    dq_buf[idx % 2, hd, 0:rows, 0:QK_NOPE] = dq[:, 0:QK_NOPE].astype(BF16)
                    dq_buf[idx % 2, hd, 0:rows, QK_NOPE:] = _unrope(dq[:, QK_NOPE:], cos_ref[qs, :], sin_ref[qs, :]).astype(BF16)
                    dk_acc[hd, 0:klen, :] += _tn(ds, qv)
                    dv_acc[hd, 0:klen, :] += _tn(p.astype(BF16), dob)
                store(idx).start()

        @pl.when(step == n_t - 1)
        def _():
            @pl.when(grp + 1 < n_g)
            def _():
                for cp in loads(grp + 1, 0):
                    cp.start()

            for hd in range(hp):
                dkv_ref[hd, :, 0:QK_NOPE] = dk_acc[hd, :, 0:QK_NOPE].astype(BF16)
                dkv_ref[hd, :, QK_NOPE:] = dv_acc[hd].astype(BF16)
                dkr_ref[...] += dk_acc[hd, :, QK_NOPE:]
            store(n_t - 2).wait()
            store(n_t - 1).wait()

            @pl.when(grp == n_g - 1)
            def _():
                finish_dwout()

    hbm = pl.BlockSpec(memory_space=pl.ANY)
    stat = pl.BlockSpec((1, N, 128), lambda g, t: (g, 0, 0), pipeline_mode=pl.Buffered(1))
    piece_f32 = lambda lead: pltpu.VMEM((lead, half, D), F32)
    piece_bf16 = lambda lead: pltpu.VMEM((lead, half, D), BF16)
    return pl.pallas_call(
        body,
        name="attn_bwd",
        grid=(n_g, n_t),
        in_specs=[hbm, hbm, hbm, hbm, stat, stat, _const(N, 128), _const(N, 128), hbm],
        out_specs=[hbm, pl.BlockSpec((hp, N, 256), lambda g, t: (g, 0, 0), pipeline_mode=pl.Buffered(1)), _const(N, 128),
                   _const(SHARD_OUT, D)],
        out_shape=[
            jax.ShapeDtypeStruct((HEADS, N, 256), BF16), jax.ShapeDtypeStruct((HEADS, N, 256), BF16),
            jax.ShapeDtypeStruct((N, 128), F32), jax.ShapeDtypeStruct((SHARD_OUT, D), F32),
        ],
        scratch_shapes=[pltpu.VMEM((hp, N, 256), BF16), pltpu.VMEM((hp, N, 256), BF16), pltpu.VMEM((hp, N, V_HEAD), BF16),
                        pltpu.VMEM((N, hp * V_HEAD), BF16), pltpu.VMEM((2, hp, TQ, 256), BF16),
                        pltpu.VMEM((hp, N, 256), F32), pltpu.VMEM((hp, N, V_HEAD), F32),
                        piece_f32(CHIPS), piece_f32(CHIPS), piece_bf16(3), piece_bf16(3), pltpu.VMEM((SHARD_OUT, D), F32),
                        pltpu.SemaphoreType.DMA((4, 2)), pltpu.SemaphoreType.DMA((2,)), pltpu.SemaphoreType.DMA((CHIPS,)),
                        pltpu.SemaphoreType.DMA((CHIPS,)), pltpu.SemaphoreType.DMA((CHIPS,)),
                        pltpu.SemaphoreType.DMA((3,)), pltpu.SemaphoreType.DMA((3,)),
                        pltpu.SemaphoreType.DMA((1,)), pltpu.SemaphoreType.DMA((1,))],
        compiler_params=_cparams(dimension_semantics=("arbitrary", "arbitrary"), collective_id=2),
    )(q, k, v, do, lse, delta, cosf, sinf, dwout)


def _bwd_in(h, dh2, dq, dkv, dkr, cq, ckv, dpl, dpg, dag, norm_g, win, gq, wq, gkv, wkv, cosf, sinf, adam_out):
    tr = ROWS_BWD
    nb = N // tr
    per = tr // HALO
    lead = HEAD_ROWS
    adam_rows = SHARD_OUT // nb

    def body(h_ref, dh2_ref, dq_ref, dkv_ref, dkr_ref, cq_ref, ckv_ref, dpl_ref, halo_ref, dpg_ref, dag_ref,
             g_ref, win_ref, gq_ref, wq_ref, gkv_ref, wkv_ref, cos_ref, sin_ref, aw_ref, ag_ref, am_ref, av_ref,
             gx_ref, dmeta_ref, dsl_ref, dwq_ref, dwkv_ref, dg_ref, dgq_ref, dgkv_ref, ago_ref, ad_ref, anm_ref, anv_ref,
             dh_buf, gx_sem):
        i = pl.program_id(0)
        grad_out = ag_ref[...]
        ago_ref[...] = grad_out
        ad_ref[...], anm_ref[...], anv_ref[...] = _adamw_math(aw_ref[...], grad_out, am_ref[...], av_ref[...])

        @pl.when(i == 0)
        def _():
            dwq_ref[...] = jnp.zeros_like(dwq_ref)
            dwkv_ref[...] = jnp.zeros_like(dwkv_ref)
            dg_ref[...] = jnp.zeros_like(dg_ref)
            dgq_ref[...] = jnp.zeros_like(dgq_ref)
            dgkv_ref[...] = jnp.zeros_like(dgkv_ref)

        row0 = i * tr
        h = h_ref[...]
        r = lax.rsqrt(jnp.mean(h * h, axis=-1, keepdims=True) + EPS)
        n = h * r
        gv = g_ref[...]
        cq = cq_ref[...]
        rq = lax.rsqrt(jnp.mean(cq * cq, axis=-1, keepdims=True) + EPS)
        nq = cq * rq
        gqv = gq_ref[...]
        cqn = (nq * gqv).astype(BF16)
        dcqn = jnp.zeros((tr, Q_LORA), F32)
        for hd in range(HEADS):
            dqf = dq_ref[hd]
            dcqn = dcqn + _nn(dqf, wq_ref[hd])
            dwq_ref[hd] += _tn(dqf, cqn)
        dgq_ref[...] += jnp.sum(dcqn * nq, axis=0, keepdims=True)
        dnq = dcqn * gqv
        dcq = rq * (dnq - nq * jnp.mean(dnq * nq, axis=-1, keepdims=True))

        ckv = ckv_ref[...]
        rkv = lax.rsqrt(jnp.mean(ckv * ckv, axis=-1, keepdims=True) + EPS)
        nkv = ckv * rkv
        gkvv = gkv_ref[...]
        ckvn = (nkv * gkvv).astype(BF16)
        dckvn = jnp.zeros((tr, KV_LORA), F32)
        for hd in range(HEADS):
            dkv = dkv_ref[hd]
            dckvn = dckvn + _nt(dkv, wkv_ref[hd])
            dwkv_ref[hd] += _tn(ckvn, dkv)
        dgkv_ref[...] += jnp.sum(dckvn * nkv, axis=0, keepdims=True)
        dnkv = dckvn * gkvv
        dckv = rkv * (dnkv - nkv * jnp.mean(dnkv * nkv, axis=-1, keepdims=True))
        dkr = _unrope(dkr_ref[...], cos_ref[...], sin_ref[...])

        cur = dpl_ref[...]
        halo = jnp.where(i < nb - 1, halo_ref[...], 0.0)
        dpi = []
        for g, w in enumerate(POOL_WINDOWS):
            sl = slice(g * POOL_GROUP, (g + 1) * POOL_GROUP)
            a = jnp.concatenate([cur[:, sl] * _inv_count(row0, tr, w), halo[:, sl] * _inv_count(row0 + tr, HALO, w)], axis=0)
            acc = a
            shift = 1
            while shift < w:
                acc = acc + pltpu.roll(acc, tr + HALO - shift, 0)
                shift *= 2
            dpi.append(acc[0:tr] - cur[:, sl])

        du = jnp.concatenate([t.astype(BF16) for t in dpi] + [dpg_ref[...]] + [t.astype(BF16) for t in (dcq, dckv, dkr)],
                             axis=1)
        dagb = dag_ref[...]
        by_row = jnp.concatenate(dpi + [dpg_ref[...].astype(F32), dcq, dckv, dkr[:, 0:QK_ROPE], dagb.astype(F32),
                                        jnp.zeros((tr, SHARD_PAD - SHARD_IN), F32)], axis=1)
        for chip in range(CHIPS):
            dsl_ref[chip] = by_row[:, SHARD_IN * chip:SHARD_IN * chip + SHARD_PAD].astype(BF16)
        dhn = _nn(du, win_ref[0:O_KR_END, :]) + _nn(dagb, win_ref[O_AG:D_IN, :])
        dg_ref[...] += jnp.sum(dhn * n, axis=0, keepdims=True)
        dn = dhn * gv
        dh = dh2_ref[...] + r * (dn - n * jnp.mean(dn * n, axis=-1, keepdims=True))

        first = pltpu.make_async_copy(dh_buf.at[pl.ds(lead, tr - lead), :], gx_ref.at[pl.ds(0, tr - lead), :], gx_sem)
        later = lambda step: pltpu.make_async_copy(
            dh_buf, gx_ref.at[pl.ds(pl.multiple_of(step * tr - lead, 16), tr), :], gx_sem)

        @pl.when(i == 1)
        def _():
            first.wait()

        @pl.when(i > 1)
        def _():
            later(i - 1).wait()

        dh_buf[...] = dh

        @pl.when(i == 0)
        def _():
            first.start()
            for chip in range(CHIPS):
                dmeta_ref[chip] = dh[PAD:HEAD_ROWS, chip * 256:(chip + 1) * 256]

        @pl.when(i > 0)
        def _():
            later(i).start()

        @pl.when(i == nb - 1)
        def _():
            later(i).wait()

    head = lambda w: pl.BlockSpec((HEADS, tr, w), lambda i: (0, i, 0))
    halo_spec = pl.BlockSpec((HALO, D_POOL), lambda i: (jnp.minimum((i + 1) * per, N // HALO - 1), 0))
    return pl.pallas_call(
        body,
        name="bwd_in",
        grid=(nb,),
        in_specs=[
            _rows(D, tr), _rows(D, tr), head(256), head(256), _rows(128, tr), _rows(Q_LORA, tr), _rows(KV_LORA, tr),
            _rows(D_POOL, tr), halo_spec, _rows(D_POOL, tr), _rows(D_POOL, tr),
            _const(1, D), _const(D_IN, D), _const(1, Q_LORA), _const(HEADS, 256, Q_LORA),
            _const(1, KV_LORA), _const(HEADS, KV_LORA, 256), _rows(128, tr), _rows(128, tr),
        ] + [_rows(D, adam_rows)] * 4,
        out_specs=[
            pl.BlockSpec(memory_space=pl.ANY), _const(CHIPS, N_META, 256),
            pl.BlockSpec((CHIPS, tr, SHARD_PAD), lambda i: (0, i, 0)), _const(HEADS, 256, Q_LORA),
            _const(HEADS, KV_LORA, 256), _const(1, D), _const(1, Q_LORA), _const(1, KV_LORA),
        ] + [_rows(D, adam_rows)] * 4,
        out_shape=[
            jax.ShapeDtypeStruct((S, D), F32), jax.ShapeDtypeStruct((CHIPS, N_META, 256), F32),
            jax.ShapeDtypeStruct((CHIPS, N, SHARD_PAD), BF16), jax.ShapeDtypeStruct((HEADS, 256, Q_LORA), F32),
            jax.ShapeDtypeStruct((HEADS, KV_LORA, 256), F32),
            jax.ShapeDtypeStruct((1, D), F32), jax.ShapeDtypeStruct((1, Q_LORA), F32), jax.ShapeDtypeStruct((1, KV_LORA), F32),
        ] + [jax.ShapeDtypeStruct((SHARD_OUT, D), F32)] * 4,
        scratch_shapes=[pltpu.VMEM((tr, D), F32), pltpu.SemaphoreType.DMA],
        compiler_params=_cparams(dimension_semantics=("arbitrary",)),
    )(h, dh2, dq, dkv, dkr, cq, ckv, dpl, dpl, dpg, dag, norm_g, win, gq, wq, gkv, wkv, cosf, sinf, *adam_out)


def _local_step(h, tgt, norm_g, win, gq, wq, gkv, wkv, pool_w, pool_scale, wout_s, m_wout_s, v_wout_s, gf, cosf, sinf):
    pool_in, pool_gate, cq, ckv, attn_gate, q, k, v, hn = _fwd_in(h, norm_g, win, gq, wq, gkv, wkv, cosf, sinf)
    attn, lse, wout = _attn_fwd(q, k, v, wout_s)
    dh2, do, delta, dag, dpg, dpl, dwout, dpw, dps, dgf, loss = _mid(
        h, tgt, pool_in, pool_gate, attn_gate, attn, pool_w, pool_scale, wout, gf)
    dq, dkv, dkr, gwout = _attn_bwd(q, k, v, do, lse, delta, cosf, sinf, dwout)
    gx, dmeta, dsl, dwq, dwkv, dg, dgq, dgkv, *r_out = _bwd_in(
        h, dh2, dq, dkv, dkr, cq, ckv, dpl, dpg, dag, norm_g, win, gq, wq, gkv, wkv, cosf, sinf,
        (wout_s, gwout, m_wout_s, v_wout_s))
    return dict(gx=gx, dmeta=dmeta, dsl=dsl, hn=hn, dwq=dwq, dwkv=dwkv, r_out=tuple(r_out), dg=dg, dgq=dgq,
                dgkv=dgkv, dpw=dpw, dps=dps, dgf=dgf, loss=loss)


_CHIP_RELS = ((0, 0), (1, 0), (0, 1), (1, 1))

_ARR_ROWS = (SHARD_IN, SHARD_OUT, 256, KV_LORA, N_META)
_ARR_COLS = (D, D, Q_LORA, 256, 256)
_PIECES = (
    (0, 0, 256, 0), (0, 256, SHARD_IN - 256, 1),
    (1, 0, 128, 0), (1, 128, 128, 1),
    (2, 0, 128, 0), (2, 128, 128, 1),
    (3, 0, 64, 0), (3, 64, 64, 1),
    (4, 0, N_META, 0),
)
_NP = len(_PIECES)
_PIECE_MAX = (256, 128, 128, 64, N_META)


def _gathered_at(refs, arr, chip, r0, n):
    if arr in (0, 1):
        return refs[arr].at[pl.ds(pl.multiple_of(_ARR_ROWS[arr] * chip + r0, 16), n), :]
    return refs[arr].at[chip, pl.ds(r0, n), :]


def _remote(src, dst, send_sem, recv_sem, to):
    return pltpu.make_async_remote_copy(src_ref=src, dst_ref=dst, send_sem=send_sem, recv_sem=recv_sem,
                                        device_id=to, device_id_type=MESH)


def _gather_weights(winT_s, wqT_s, wkv_s, meta_s, x2, tgt2):
    arrays = (0, 2, 3, 4)

    def body(win_ref, wq_ref, wkv_ref, meta_ref, x_ref, t_ref, win_o, wq_o, wkv_o, h_o, tp_o,
             s_win, s_wq, s_wkv, meta_all, head_buf, x_buf, t_buf, ici_send, ici_recv, fwd_send, fwd_recv,
             loc_sems, own_sems):
        x, y, c = lax.axis_index("x"), lax.axis_index("y"), lax.axis_index("c")
        me = 2 * x + y
        stage = (s_win, None, s_wq, s_wkv, meta_ref)
        outs = (win_o, None, wq_o, wkv_o, meta_all)

        _peer_signal(x, y, c)

        frames = pl.ds(HEAD_ROWS, S)
        loads = [pltpu.make_async_copy(x_ref, x_buf, loc_sems.at[0]), pltpu.make_async_copy(t_ref, t_buf, loc_sems.at[1])]
        local = [pltpu.make_async_copy(x_buf, h_o.at[frames, :], loc_sems.at[0]),
                 pltpu.make_async_copy(t_buf, tp_o.at[frames, :], loc_sems.at[1])]
        for cp in loads:
            cp.start()

        s_win[...] = win_ref[...].astype(BF16)
        s_wq[0:QK, :] = wq_ref[...].astype(BF16)
        s_wq[QK:256, :] = jnp.zeros((256 - QK, Q_LORA), BF16)
        s_wkv[...] = wkv_ref[...].astype(BF16)
        head_buf[...] = jnp.zeros_like(head_buf)
        zeros = pltpu.make_async_copy(head_buf, tp_o.at[pl.ds(0, HEAD_ROWS), :], loc_sems.at[2])
        zeros.start()

        def chip_of(rel):
            fx, fy = _CHIP_RELS[rel]
            return 2 * (x ^ fx) + (y ^ fy)

        def same_core_of(rel):
            fx, fy = _CHIP_RELS[rel]
            return (x ^ fx, y ^ fy, c)

        def ici_copy(rel, i, src_chip, to):
            arr, r0, n, _ = _PIECES[i]
            k = (rel - 1) * _NP + i
            return _remote(stage[arr].at[pl.ds(r0, n), :], _gathered_at(outs, arr, src_chip, r0, n),
                           ici_send.at[k], ici_recv.at[k], to)

        def fwd_copy(rel, i, to):
            arr, r0, n, _ = _PIECES[i]
            k = (rel - 1) * _NP + i
            place = _gathered_at(outs, arr, chip_of(rel), r0, n)
            return _remote(place, place, fwd_send.at[k], fwd_recv.at[k], to)

        _peer_wait()
        for core in (0, 1):
            @pl.when(c == core)
            def _(core=core):
                mine = [i for i in range(_NP) if _PIECES[i][3] == core and _PIECES[i][0] in arrays]
                theirs = [i for i in range(_NP) if _PIECES[i][3] != core and _PIECES[i][0] in arrays]
                order = (1, 2, 3)
                sends = [ici_copy(rel, i, me, same_core_of(rel)) for rel in order for i in mine]
                for cp in sends:
                    cp.start()
                for ld, st in zip(loads, local):
                    ld.wait()
                    st.start()
                own = [pltpu.make_async_copy(stage[arr], _gathered_at(outs, arr, me, 0, _ARR_ROWS[arr]), own_sems.at[arr])
                       for arr in arrays if arr != 4]
                for cp in own:
                    cp.start()
                meta_all[me] = meta_ref[...]
                for rel in order:
                    for i in mine:
                        ici_copy(rel, i, chip_of(rel), (x, y, c)).wait_recv()
                        fwd = fwd_copy(rel, i, (x, y, 1 - c))
                        fwd.start()
                        sends.append(fwd)
                for rel in order:
                    for i in theirs:
                        fwd_copy(rel, i, (x, y, c)).wait_recv()
                for cp in sends:
                    cp.wait_send()
                for cp in own:
                    cp.wait()

        zeros.wait()
        for chip in range(CHIPS):
            head_buf[PAD:HEAD_ROWS, chip * 256:(chip + 1) * 256] = meta_all[chip]
        head = pltpu.make_async_copy(head_buf, h_o.at[pl.ds(0, HEAD_ROWS), :], loc_sems.at[2])
        head.start()
        head.wait()
        for cp in local:
            cp.wait()

    vm = pl.BlockSpec(memory_space=pltpu.VMEM)
    hbm = pl.BlockSpec(memory_space=pl.ANY)
    return pl.pallas_call(
        body,
        name="gather_weights",
        in_specs=[vm] * 4 + [hbm] * 2,
        out_specs=[hbm] * 5,
        out_shape=[
            jax.ShapeDtypeStruct((D_IN, D), BF16),
            jax.ShapeDtypeStruct((CHIPS, 256, Q_LORA), BF16), jax.ShapeDtypeStruct((CHIPS, KV_LORA, 256), BF16),
            jax.ShapeDtypeStruct((N, D), F32), jax.ShapeDtypeStruct((N, D), F32),
        ],
        scratch_shapes=[pltpu.VMEM((_ARR_ROWS[a], _ARR_COLS[a]), BF16) for a in (0, 2, 3)]
        + [pltpu.VMEM((CHIPS, N_META, 256), F32), pltpu.VMEM((HEAD_ROWS, D), F32), pltpu.VMEM((S, D), F32),
           pltpu.VMEM((S, D), F32)]
        + [pltpu.SemaphoreType.DMA((3 * _NP,))] * 4 + [pltpu.SemaphoreType.DMA((3,)), pltpu.SemaphoreType.DMA((4,))],
        compiler_params=_cparams(collective_id=0),
    )(winT_s, wqT_s, wkv_s, meta_s, x2, tgt2)


_SM_ROWS = (len(POOL_WINDOWS) * POOL_GROUP, VEC_ROWS)
_SM_COLS = (POOL_GROUP, D)
_SM_PIECES = ((0, 0, 256, 0), (0, 256, 256, 1), (1, 0, VEC_ROWS, 0))
_NSP = len(_SM_PIECES)


def _reduce_grads(dsl, hn, dwq, dwkv, dmeta4, dpw, dg, dgf, dgq, dgkv, dps, loss):
    arrays = (0, 2, 3, 4)
    loaded = (2, 3, 4)
    shard_order = SEND_ORDER + (0,)

    def body(dsl_hbm, hn_hbm, dwq_ref, dwkv_ref, dmeta_ref, dpw_ref, dg_ref, dgf_ref, dgq_ref, dgkv_ref, dps_ref,
             loss_ref, gwin_o, gwq_o, gwkv_o, gmeta_o, gpw_o, gg_o, ggf_o, ggq_o, ggkv_o, gps_o, gloss_o,
             ow2, ow3, ow4, sb0, sb2, sb3, sb4, st0, st2, st3, st4, rc0, rc2, rc3, rc4,
             vec, sm_sb0, sm_sb1, sm_cs0, sm_cs1, sm_rc0, sm_rc1, vec_fin, slab_v, hn_v, dwin_buf, own0,
             own_sems, d2d_send, d2d_recv, ici_send, ici_recv, fin_send, fin_recv,
             swap_send, swap_recv, smi_send, smi_recv, smf_send, smf_recv, ld_sems):
        x, y, c = lax.axis_index("x"), lax.axis_index("y"), lax.axis_index("c")
        me = 2 * x + y
        _peer_signal(x, y, c)
        grads = (None, None, dwq_ref, dwkv_ref, dmeta_ref)
        outs = (gwin_o, None, gwq_o, gwkv_o, gmeta_o)
        own_buf = (None, None, ow2, ow3, ow4)
        sib_buf = (sb0, None, sb2, sb3, sb4)
        stage = (st0, None, st2, st3, st4)
        recv = (rc0, None, rc2, rc3, rc4)
        sm_mine = (dpw_ref, vec)
        sm_sib = (sm_sb0, sm_sb1)
        sm_chip = (sm_cs0, sm_cs1)
        sm_recv = (sm_rc0, sm_rc1)
        sm_out = (gpw_o, vec_fin)
        sibling = (x, y, 1 - c)

        def chip_of(rel):
            fx, fy = _CHIP_RELS[rel]
            return 2 * (x ^ fx) + (y ^ fy)

        def same_core_of(rel):
            fx, fy = _CHIP_RELS[rel]
            return (x ^ fx, y ^ fy, c)

        hn_load = pltpu.make_async_copy(hn_hbm, hn_v, ld_sems.at[CHIPS])

        def slab_load(rel):
            return pltpu.make_async_copy(dsl_hbm.at[chip_of(rel)], slab_v.at[rel], ld_sems.at[rel])

        hn_load.start()
        slab_load(shard_order[0]).start()

        def slot(bufs, i, idx):
            arr, _, n, _ = _PIECES[i]
            return bufs[arr].at[idx, pl.ds(0, n), :]

        def own_load(rel, i):
            arr, r0, n, _ = _PIECES[i]
            return pltpu.make_async_copy(_gathered_at(grads, arr, chip_of(rel), r0, n), slot(own_buf, i, rel),
                                         own_sems.at[rel * _NP + i])

        def d2d_copy(rel, i):
            arr, r0, n, _ = _PIECES[i]
            k = rel * _NP + i
            return _remote(_gathered_at(grads, arr, chip_of(rel), r0, n), slot(sib_buf, i, rel),
                           d2d_send.at[k], d2d_recv.at[k], sibling)

        def ici_copy(rel, i):
            k = (rel - 1) * _NP + i
            return _remote(slot(stage, i, rel - 1), slot(recv, i, rel - 1), ici_send.at[k], ici_recv.at[k],
                           same_core_of(rel))

        def fin_copy(i):
            arr, r0, n, _ = _PIECES[i]
            place = outs[arr].at[pl.ds(r0, n), :]
            return _remote(place, place, fin_send.at[i], fin_recv.at[i], sibling)

        def sm_ici_copy(rel, j):
            blk, r0, n, _ = _SM_PIECES[j]
            k = (rel - 1) * _NSP + j
            return _remote(sm_chip[blk].at[pl.ds(r0, n), :], sm_recv[blk].at[rel - 1, pl.ds(r0, n), :],
                           smi_send.at[k], smi_recv.at[k], same_core_of(rel))

        def sm_fin_copy(j):
            blk, r0, n, _ = _SM_PIECES[j]
            place = sm_out[blk].at[pl.ds(r0, n), :]
            return _remote(place, place, smf_send.at[j], smf_recv.at[j], sibling)

        vec[...] = jnp.zeros_like(vec)
        vec[0:1, :] = dg_ref[...]
        vec[1:2, :] = dgf_ref[...]
        vec[2:3, V_GQ:V_GQ + Q_LORA] = dgq_ref[...]
        vec[2:3, V_GKV:V_GKV + KV_LORA] = dgkv_ref[...]
        vec[2:3, V_PS:V_PS + D_POOL] = dps_ref[...]
        vec[2:3, V_LOSS:D] = loss_ref[...]
        _peer_wait()
        swaps = [_remote(sm_mine[b], sm_sib[b], swap_send.at[b], swap_recv.at[b], sibling) for b in (0, 1)]
        for cp in swaps:
            cp.start()

        for core in (0, 1):
            @pl.when(c == core)
            def _(core=core):
                mine = [i for i in range(_NP) if _PIECES[i][3] == core and _PIECES[i][0] in loaded]
                theirs = [i for i in range(_NP) if _PIECES[i][3] != core and _PIECES[i][0] in loaded]
                i0 = next(i for i in range(_NP) if _PIECES[i][0] == 0 and _PIECES[i][3] == core)
                j0 = next(i for i in range(_NP) if _PIECES[i][0] == 0 and _PIECES[i][3] != core)
                sm_mine_p = [j for j in range(_NSP) if _SM_PIECES[j][3] == core]
                sm_theirs_p = [j for j in range(_NSP) if _SM_PIECES[j][3] != core]
                sends = list(swaps)

                for rel in shard_order:
                    for i in theirs:
                        cp = d2d_copy(rel, i)
                        cp.start()
                        sends.append(cp)
                    for i in mine:
                        own_load(rel, i).start()

                def piece_rows(i):
                    return pl.ds(_PIECES[i][1], _PIECES[i][2])

                def form(rel, i):
                    r0, n = _PIECES[i][1], _PIECES[i][2]
                    dwin_buf[rel, r0:r0 + n, :] = _tn(slab_v[rel, :, r0:r0 + _PIECE_MAX[0]], hn_v[...])[0:n, :]

                def d2d0(rel, i):
                    return _remote(dwin_buf.at[rel, piece_rows(i), :], slot(sib_buf, i, rel),
                                   d2d_send.at[rel * _NP + i], d2d_recv.at[rel * _NP + i], sibling)

                def settle(rel):
                    d2d0(rel, i0).wait_recv()
                    total = dwin_buf[rel, piece_rows(i0), :] + slot(sib_buf, i0, rel)[...]
                    if rel == 0:
                        own0[0:_PIECES[i0][2], :] = total
                    else:
                        slot(stage, i0, rel - 1)[...] = total.astype(BF16)
                        cp = ici_copy(rel, i0)
                        cp.start()
                        sends.append(cp)

                def send_the_rest():
                    for rel in SEND_ORDER:
                        for i in mine:
                            arr, r0, n, _ = _PIECES[i]
                            own_load(rel, i).wait()
                            d2d_copy(rel, i).wait_recv()
                            total = slot(own_buf, i, rel)[...] + slot(sib_buf, i, rel)[...]
                            slot(stage, i, rel - 1)[...] = total.astype(stage[arr].dtype)
                            cp = ici_copy(rel, i)
                            cp.start()
                            sends.append(cp)

                    for b in (0, 1):
                        swaps[b].wait_recv()
                        sm_chip[b][...] = sm_mine[b][...] + sm_sib[b][...]
                    for rel in SEND_ORDER:
                        for j in sm_mine_p:
                            cp = sm_ici_copy(rel, j)
                            cp.start()
                            sends.append(cp)

                hn_load.wait()
                for n, rel in enumerate(shard_order):
                    slab_load(rel).wait()
                    if n == 0:
                        for later in shard_order[1:]:
                            slab_load(later).start()
                    form(rel, j0)
                    cp = d2d0(rel, j0)
                    cp.start()
                    sends.append(cp)
                    if n > 0:
                        settle(shard_order[n - 1])
                    if n == 1:
                        send_the_rest()
                    form(rel, i0)
                settle(shard_order[-1])

                for j in sm_mine_p:
                    blk, r0, n, _ = _SM_PIECES[j]
                    for rel in (1, 2, 3):
                        sm_ici_copy(rel, j).wait_recv()
                    total = jnp.zeros((n, _SM_COLS[blk]), F32)
                    for chip in range(CHIPS):
                        flips = chip ^ me
                        rel = jnp.where(flips == 2, 1, jnp.where(flips == 1, 2, flips))
                        theirs_rows = sm_recv[blk][jnp.maximum(rel - 1, 0), pl.ds(r0, n), :]
                        total = total + jnp.where(rel == 0, sm_chip[blk][pl.ds(r0, n), :], theirs_rows)
                    sm_out[blk][pl.ds(r0, n), :] = total
                    cp = sm_fin_copy(j)
                    cp.start()
                    sends.append(cp)

                for i in mine:
                    arr, r0, n, _ = _PIECES[i]
                    own_load(0, i).wait()
                    d2d_copy(0, i).wait_recv()
                    total = slot(own_buf, i, 0)[...] + slot(sib_buf, i, 0)[...]
                    for rel in (1, 2, 3):
                        ici_copy(rel, i).wait_recv()
                        total = total + slot(recv, i, rel - 1)[...].astype(F32)
                    outs[arr][pl.ds(r0, n), :] = total
                    cp = fin_copy(i)
                    cp.start()
                    sends.append(cp)
                total = own0[0:_PIECES[i0][2], :]
                for rel in (1, 2, 3):
                    ici_copy(rel, i0).wait_recv()
                    total = total + slot(recv, i0, rel - 1)[...].astype(F32)
                outs[0][pl.ds(_PIECES[i0][1], _PIECES[i0][2]), :] = total
                cp = fin_copy(i0)
                cp.start()
                sends.append(cp)

                for i in theirs + [j0]:
                    fin_copy(i).wait_recv()
                for j in sm_theirs_p:
                    sm_fin_copy(j).wait_recv()
                for cp in sends:
                    cp.wait_send()

        gg_o[...] = vec_fin[0:1, :]
        ggf_o[...] = vec_fin[1:2, :]
        ggq_o[...] = vec_fin[2:3, V_GQ:V_GQ + Q_LORA]
        ggkv_o[...] = vec_fin[2:3, V_GKV:V_GKV + KV_LORA]
        gps_o[...] = vec_fin[2:3, V_PS:V_PS + D_POOL]
        gloss_o[...] = vec_fin[2:3, V_LOSS:D]

    vm = pl.BlockSpec(memory_space=pltpu.VMEM)
    piece_buf = lambda lead, dtype, which=arrays: [
        pltpu.VMEM((lead, _PIECE_MAX[a], _ARR_COLS[a]), F32 if a == 4 else dtype) for a in which]
    sm_buf = lambda *lead: [pltpu.VMEM(lead + (_SM_ROWS[b], _SM_COLS[b]), F32) for b in (0, 1)]
    dma = lambda n: [pltpu.SemaphoreType.DMA((n,))] * 2
    return pl.pallas_call(
        body,
        name="reduce_grads",
        in_specs=[pl.BlockSpec(memory_space=pl.ANY)] * 4 + [vm] * 8,
        out_specs=[vm] * 11,
        out_shape=[jax.ShapeDtypeStruct((_ARR_ROWS[a], _ARR_COLS[a]), F32) for a in arrays]
        + [jax.ShapeDtypeStruct((_SM_ROWS[0], _SM_COLS[0]), F32), jax.ShapeDtypeStruct((1, D), F32),
           jax.ShapeDtypeStruct((1, D), F32), jax.ShapeDtypeStruct((1, Q_LORA), F32),
           jax.ShapeDtypeStruct((1, KV_LORA), F32), jax.ShapeDtypeStruct((1, D_POOL), F32),
           jax.ShapeDtypeStruct((1, 128), F32)],
        scratch_shapes=piece_buf(CHIPS, F32, loaded) + piece_buf(CHIPS, F32) + piece_buf(3, BF16) + piece_buf(3, BF16)
        + [pltpu.VMEM((VEC_ROWS, D), F32)] + sm_buf() + sm_buf() + sm_buf(3) + [pltpu.VMEM((VEC_ROWS, D), F32)]
        + [pltpu.VMEM((CHIPS, N, SHARD_PAD), BF16), pltpu.VMEM((N, D), BF16),
           pltpu.VMEM((CHIPS, SHARD_PAD, D), F32), pltpu.VMEM((_PIECE_MAX[0], D), F32)]
        + [pltpu.SemaphoreType.DMA((CHIPS * _NP,))]
        + dma(CHIPS * _NP) + dma(3 * _NP) + dma(_NP) + dma(2) + dma(3 * _NSP) + dma(_NSP)
        + [pltpu.SemaphoreType.DMA((CHIPS + 1,))],
        compiler_params=_cparams(collective_id=3),
    )(dsl, hn, dwq, dwkv, dmeta4, dpw, dg, dgf, dgq, dgkv, dps, loss)


def _adamw_math(w, g, m, v):
    m = B1 * m + (1.0 - B1) * g
    v = B2 * v + (1.0 - B2) * (g * g)
    m_hat = m / C1
    v_hat = v / C2
    delta = -LR * (m_hat / (jnp.sqrt(v_hat) + ADAM_EPS) + WD * w)
    return delta, m, v


def _adamw(big, block_rows, groups):
    rows, cols = big[0].shape
    n = len(groups)

    def body(*refs):
        w_ref, g_ref, m_ref, v_ref = refs[0:4]
        small_in = refs[4:4 + 4 * n]
        go_ref, d_ref, nm_ref, nv_ref = refs[4 + 4 * n:8 + 4 * n]
        small_out = refs[8 + 4 * n:]
        g = g_ref[...]
        go_ref[...] = g
        d_ref[...], nm_ref[...], nv_ref[...] = _adamw_math(w_ref[...], g, m_ref[...], v_ref[...])

        @pl.when(pl.program_id(0) == 0)
        def _():
            for t in range(n):
                sw_ref, sg_ref, sm_ref, sv_ref = small_in[4 * t:4 * t + 4]
                sg = sg_ref[0:sw_ref.shape[0], :]
                small_out[4 * t][...] = sg
                small_out[4 * t + 1][...], small_out[4 * t + 2][...], small_out[4 * t + 3][...] = _adamw_math(
                    sw_ref[...], sg, sm_ref[...], sv_ref[...])

    spec = pl.BlockSpec((block_rows, cols), lambda i: (i, 0))
    vm = pl.BlockSpec(memory_space=pltpu.VMEM)
    outs = pl.pallas_call(
        body,
        name="adamw",
        grid=(rows // block_rows,),
        in_specs=[spec] * 4 + [vm] * (4 * n),
        out_specs=[spec] * 4 + [vm] * (4 * n),
        out_shape=[jax.ShapeDtypeStruct(big[0].shape, F32)] * 4
        + [jax.ShapeDtypeStruct(grp[0].shape, F32) for grp in groups for _ in range(4)],
        compiler_params=_cparams(dimension_semantics=("arbitrary",)),
    )(*big, *[a for grp in groups for a in grp])
    return tuple(outs[0:4]), [tuple(outs[4 + 4 * t:8 + 4 * t]) for t in range(n)]


def _rope_tables():
    half = QK_ROPE // 2
    f32 = np.float32
    inv_freq = (f32(1.0) / (f32(ROPE_THETA) ** (np.arange(half, dtype=f32) / f32(half)))).astype(f32)
    pos = np.arange(N, dtype=f32) - f32(PAD)
    ang = (pos[:, None] * inv_freq[None, :]).astype(f32)
    cos, sin = np.cos(ang).astype(f32), np.sin(ang).astype(f32)
    zero = np.zeros((N, 128 - QK_ROPE), f32)
    return jnp.asarray(np.concatenate([cos, cos, zero], axis=1)), jnp.asarray(np.concatenate([-sin, sin, zero], axis=1))


def kernel(x, meta_tokens, norm_g, w_in, q_norm_g, w_q_b, kv_norm_g, w_kv_b, pool_w, pool_scale, w_out, final_norm_g, loss_target, m_meta_tokens, m_norm_g, m_w_in, m_q_norm_g, m_w_q_b, m_kv_norm_g, m_w_kv_b, m_pool_w, m_pool_scale, m_w_out, m_final_norm_g, v_meta_tokens, v_norm_g, v_w_in, v_q_norm_g, v_w_q_b, v_kv_norm_g, v_w_kv_b, v_pool_w, v_pool_scale, v_w_out, v_final_norm_g):
    tr = lambda a: a[0].T
    win, wq, wkv, h, tgt = _gather_weights(tr(w_in), tr(w_q_b), w_kv_b[0], meta_tokens, x[0], loss_target[0])
    cosf, sinf = _rope_tables()
    gf = final_norm_g.reshape(1, D)

    part = _local_step(h, tgt, norm_g, win, q_norm_g, wq, kv_norm_g, wkv, pool_w[0], pool_scale, w_out[0], m_w_out[0],
                       v_w_out[0], gf, cosf, sinf)

    pw2 = lambda a: a.reshape(len(POOL_WINDOWS) * POOL_GROUP, POOL_GROUP)
    gwinT, gwqT, gwkv, gmeta, gpw, gg, ggf, ggq, ggkv, gps, gloss = _reduce_grads(
        part["dsl"], part["hn"], part["dwq"], part["dwkv"], part["dmeta"], pw2(part["dpw"]), part["dg"],
        part["dgf"], part["dgq"], part["dgkv"], part["dps"], part["loss"])

    r_out = part["r_out"]
    fn2 = lambda a: a.reshape(1, D)
    r_in, (r_meta, r_norm, r_gq, r_wq, r_gkv, r_wkv, r_pw, r_ps, r_fn) = _adamw((tr(w_in), gwinT, tr(m_w_in), tr(v_w_in)), 248, [
        (meta_tokens, gmeta, m_meta_tokens, v_meta_tokens),
        (norm_g, gg, m_norm_g, v_norm_g),
        (q_norm_g, ggq, m_q_norm_g, v_q_norm_g),
        (tr(w_q_b), gwqT, tr(m_w_q_b), tr(v_w_q_b)),
        (kv_norm_g, ggkv, m_kv_norm_g, v_kv_norm_g),
        (w_kv_b[0], gwkv, m_w_kv_b[0], v_w_kv_b[0]),
        (pw2(pool_w), gpw, pw2(m_pool_w), pw2(v_pool_w)),
        (pool_scale, gps, m_pool_scale, v_pool_scale),
        (fn2(final_norm_g), ggf, fn2(m_final_norm_g), fn2(v_final_norm_g)),
    ])
    untr = lambda a: a.T[None]
    pw4 = lambda a: a.reshape(1, len(POOL_WINDOWS), POOL_GROUP, POOL_GROUP)
    per_kind = [[
        r_meta[kind], r_norm[kind], untr(r_in[kind]), r_gq[kind], untr(r_wq[kind]), r_gkv[kind], r_wkv[kind][None],
        pw4(r_pw[kind]), r_ps[kind], r_out[kind][None], r_fn[kind].reshape(D),
    ] for kind in range(4)]
    return (gloss[0, 0], part["gx"][None], *per_kind[0], *per_kind[1], *per_kind[2], *per_kind[3])
```

```python
import jax
import jax.numpy as jnp
import numpy as np
from jax import lax
from jax.experimental import pallas as pl
from jax.experimental.pallas import tpu as pltpu

F32 = jnp.float32
BF16 = jnp.bfloat16

D = 1024
S = 2048
N_META = 16
PAD = 112
HEAD_ROWS = PAD + N_META
N = HEAD_ROWS + S
D_POOL = 512
POOL_WINDOWS = (2, 4, 8, 16)
POOL_GROUP = 128
HALO = 16
HEADS = 4
QK_NOPE = 128
QK_ROPE = 64
QK = QK_NOPE + QK_ROPE
V_HEAD = 128
Q_LORA = 256
KV_LORA = 128
D_IN = 1984
EPS = 1e-6
ROPE_THETA = 10000.0
SCALE = QK ** -0.5
CHIPS = 4

ROWS_FWD = 544
ROWS_MID = 544
ROWS_BWD = 544
TK = 128
TQ = 256
NQ = S // TQ
HEADS_PER_STEP_BWD = 2

O_PI, O_PG, O_CQ, O_CKV, O_KR, O_AG = 0, 512, 1024, 1280, 1408, 1472
O_KR_END = O_KR + 128
SHARD_IN = D_IN // CHIPS
SHARD_PAD = 512
SHARD_OUT = D // CHIPS

LR, B1, B2, ADAM_EPS, WD, STEP = 0.001, 0.9, 0.999, 1e-08, 0.01, 10
C1 = 1.0 - B1**STEP
C2 = 1.0 - B2**STEP

VMEM_LIMIT = 60 * 1024 * 1024
MESH = pl.DeviceIdType.MESH
NEG = -1e30

VEC_ROWS = 8
PW_ROWS = len(POOL_WINDOWS) * POOL_GROUP
V_GQ, V_GKV, V_PS, V_LOSS = 0, 256, 384, 896


def _cparams(**kw):
    return pltpu.CompilerParams(vmem_limit_bytes=VMEM_LIMIT, **kw)


def _nt(a, b):
    return lax.dot_general(a, b, (((1,), (1,)), ((), ())), preferred_element_type=F32)


def _tn(a, b):
    return lax.dot_general(a, b, (((0,), (0,)), ((), ())), preferred_element_type=F32)


def _nn(a, b):
    return jnp.dot(a, b, preferred_element_type=F32)


def _swap64(t):
    return pltpu.roll(t, 32, 1) + pltpu.roll(t, 96, 1)


def _sigmoid(x):
    return 1.0 / (1.0 + jnp.exp(-x))


def _low_lanes():
    return (lax.broadcasted_iota(jnp.int32, (1, 128), 1) < QK_ROPE).astype(F32)


def _rows(w, rows):
    return pl.BlockSpec((rows, w), lambda i: (i, 0))


def _const(*shape):
    return pl.BlockSpec(shape, lambda *_: (0,) * len(shape), pipeline_mode=pl.Buffered(1))


STAT_GROUPS = HEADS // HEADS_PER_STEP_BWD


def _stat_slot(head):
    return head // HEADS_PER_STEP_BWD, head % HEADS_PER_STEP_BWD


N_PEERS = 4
SEND_ORDER = (3, 1, 2)


def _peer_signal(x, y, c):
    barrier = pltpu.get_barrier_semaphore()
    peers = [(x, y, 1 - c)] + [(x ^ fx, y ^ fy, c) for fx, fy in _CHIP_RELS[1:]]
    assert len(peers) == N_PEERS
    for peer in peers:
        pl.semaphore_signal(barrier, inc=1, device_id=peer, device_id_type=MESH)


def _peer_wait():
    pl.semaphore_wait(pltpu.get_barrier_semaphore(), N_PEERS)


def _attn_tiles():
    return [(0, TK, TK)] + [(TK + TQ * t, TQ, TK + TQ * (t + 1)) for t in range(NQ)]


def _masked_scores(q, k, rows, klen):
    s = _nt(q, k)
    col = lax.broadcasted_iota(jnp.int32, (1, TK), 1)
    head_bias = jnp.where(col >= PAD, 0.0, NEG)
    if klen == TK:
        return s + head_bias
    r = lax.broadcasted_iota(jnp.int32, (rows, 1), 0) >> 6
    c = lax.broadcasted_iota(jnp.int32, (1, rows), 1) >> 6
    diag_bias = jnp.where(c <= r, 0.0, NEG)
    parts = [s[:, 0:TK] + head_bias]
    if klen - rows > TK:
        parts.append(s[:, TK:klen - rows])
    parts.append(s[:, klen - rows:klen] + diag_bias)
    return jnp.concatenate(parts, axis=1)


def _fwd_in(h, norm_g, win, gq, wq, gkv, wkv, cosf, sinf):
    tr = ROWS_FWD

    def body(h_ref, g_ref, win_ref, gq_ref, wq_ref, gkv_ref, wkv_ref, cos_ref, sin_ref,
             pi_ref, pg_ref, cq_ref, ckv_ref, ag_ref, q_ref, k_ref, v_ref, hn_ref):
        h = h_ref[...]
        r = lax.rsqrt(jnp.mean(h * h, axis=-1, keepdims=True) + EPS)
        hn = ((h * r) * g_ref[...]).astype(BF16)
        hn_ref[...] = hn
        u = _nt(hn, win_ref[0:O_KR_END, :])
        pi_ref[...] = u[:, O_PI:O_PG]
        pg_ref[...] = u[:, O_PG:O_CQ]
        cq = u[:, O_CQ:O_CKV]
        ckv = u[:, O_CKV:O_KR]
        cq_ref[...] = cq
        ckv_ref[...] = ckv
        ag_ref[...] = _nt(hn, win_ref[O_AG:D_IN, :])
        cosv = cos_ref[...]
        sinv = sin_ref[...]
        kr = u[:, O_KR:O_KR_END] * _low_lanes()
        kr = (kr * cosv + _swap64(kr) * sinv).astype(BF16)
        rq = lax.rsqrt(jnp.mean(cq * cq, axis=-1, keepdims=True) + EPS)
        cqn = ((cq * rq) * gq_ref[...]).astype(BF16)
        rkv = lax.rsqrt(jnp.mean(ckv * ckv, axis=-1, keepdims=True) + EPS)
        ckvn = ((ckv * rkv) * gkv_ref[...]).astype(BF16)
        for hd in range(HEADS):
            qh = _nt(cqn, wq_ref[hd]) * SCALE
            z = qh[:, QK_NOPE:]
            q_ref[hd, :, 0:QK_NOPE] = qh[:, 0:QK_NOPE].astype(BF16)
            q_ref[hd, :, QK_NOPE:] = (z * cosv + _swap64(z) * sinv).astype(BF16)
            kvh = _nn(ckvn, wkv_ref[hd])
            k_ref[hd, :, 0:QK_NOPE] = kvh[:, 0:QK_NOPE].astype(BF16)
            k_ref[hd, :, QK_NOPE:] = kr
            v_ref[hd] = kvh[:, QK_NOPE:].astype(BF16)

    head = lambda w: pl.BlockSpec((HEADS, tr, w), lambda i: (0, i, 0))
    return pl.pallas_call(
        body,
        name="fwd_in",
        grid=(N // tr,),
        in_specs=[
            _rows(D, tr), _const(1, D), _const(D_IN, D), _const(1, Q_LORA), _const(HEADS, 256, Q_LORA),
            _const(1, KV_LORA), _const(HEADS, KV_LORA, 256), _rows(128, tr), _rows(128, tr),
        ],
        out_specs=[_rows(D_POOL, tr), _rows(D_POOL, tr), _rows(Q_LORA, tr), _rows(KV_LORA, tr), _rows(D_POOL, tr),
                   head(256), head(256), head(V_HEAD), _rows(D, tr)],
        out_shape=[
            jax.ShapeDtypeStruct((N, D_POOL), F32), jax.ShapeDtypeStruct((N, D_POOL), F32),
            jax.ShapeDtypeStruct((N, Q_LORA), F32), jax.ShapeDtypeStruct((N, KV_LORA), F32),
            jax.ShapeDtypeStruct((N, D_POOL), F32),
            jax.ShapeDtypeStruct((HEADS, N, 256), BF16), jax.ShapeDtypeStruct((HEADS, N, 256), BF16),
            jax.ShapeDtypeStruct((HEADS, N, V_HEAD), BF16), jax.ShapeDtypeStruct((N, D), BF16),
        ],
        compiler_params=_cparams(dimension_semantics=("arbitrary",)),
    )(h, norm_g, win, gq, wq, gkv, wkv, cosf, sinf)


def _attn_fwd(q, k, v, wout_s):
    tiles = _attn_tiles()
    n_t = len(tiles)
    half = SHARD_OUT // 2
    send_step = 2
    fwd_step = n_t - 2

    def body(q_hbm, k_hbm, v_hbm, wout_ref, o_hbm, lse_ref, wout_o, q_buf, k_buf, v_buf, o_buf, s_wout, in_sems, out_sems,
             ici_send, ici_recv, fwd_send, fwd_recv, own_sem):
        step = pl.program_id(0)
        x, y, c = lax.axis_index("x"), lax.axis_index("y"), lax.axis_index("c")
        me = 2 * x + y

        def chip_of(rel):
            fx, fy = _CHIP_RELS[rel]
            return 2 * (x ^ fx) + (y ^ fy)

        def place(chip, core):
            return wout_o.at[pl.ds(pl.multiple_of(SHARD_OUT * chip + half * core, half), half), :]

        def ici_copy(rel, src_chip, to):
            return _remote(s_wout.at[pl.ds(pl.multiple_of(half * c, half), half), :], place(src_chip, c),
                           ici_send.at[rel - 1], ici_recv.at[rel - 1], to)

        def fwd_copy(rel, core, to):
            spot = place(chip_of(rel), core)
            return _remote(spot, spot, fwd_send.at[rel - 1], fwd_recv.at[rel - 1], to)

        own = pltpu.make_async_copy(s_wout, wout_o.at[pl.ds(pl.multiple_of(SHARD_OUT * me, SHARD_OUT), SHARD_OUT), :], own_sem)

        @pl.when(step == 0)
        def _():
            _peer_signal(x, y, c)
            s_wout[...] = wout_ref[...].astype(BF16)
            own.start()

        @pl.when(step == send_step)
        def _():
            _peer_wait()
            for rel in SEND_ORDER:
                fx, fy = _CHIP_RELS[rel]
                ici_copy(rel, me, (x ^ fx, y ^ fy, c)).start()

        @pl.when(step == fwd_step)
        def _():
            for rel in (1, 2, 3):
                ici_copy(rel, chip_of(rel), (x, y, c)).wait_recv()
                fwd_copy(rel, c, (x, y, 1 - c)).start()

        def finish_wout():
            for rel in (1, 2, 3):
                fwd_copy(rel, 1 - c, (x, y, c)).wait_recv()
            for rel in (1, 2, 3):
                ici_copy(rel, me, (x, y, c)).wait_send()
                fwd_copy(rel, c, (x, y, c)).wait_send()
            own.wait()

        def loads(idx):
            q0, rows, _ = tiles[idx]
            rs = pl.ds(q0, rows)
            return [pltpu.make_async_copy(src.at[:, rs, :], dst.at[:, rs, :], in_sems.at[a, idx % 2])
                    for a, (src, dst) in enumerate(((q_hbm, q_buf), (k_hbm, k_buf), (v_hbm, v_buf)))]

        def store(idx):
            q0, rows, _ = tiles[idx]
            return pltpu.make_async_copy(o_buf.at[idx % 2, pl.ds(0, rows), :], o_hbm.at[pl.ds(q0, rows), :],
                                         out_sems.at[idx % 2])

        @pl.when(step == 0)
        def _():
            lse_ref[...] = jnp.zeros_like(lse_ref)
            for cp in loads(0):
                cp.start()

        for idx, (q0, rows, klen) in enumerate(tiles):
            @pl.when(step == idx)
            def _(idx=idx, q0=q0, rows=rows, klen=klen):
                for cp in loads(idx):
                    cp.wait()
                if idx + 1 < n_t:
                    for cp in loads(idx + 1):
                        cp.start()
                if idx >= 2:
                    store(idx - 2).wait()
                for hd in range(HEADS):
                    s = _masked_scores(q_buf[hd, q0:q0 + rows, :], k_buf[hd, 0:klen, :], rows, klen)
                    m = jnp.max(s, axis=-1, keepdims=True)
                    p = jnp.exp(s - m)
                    l = jnp.sum(p, axis=-1, keepdims=True)
                    o_buf[idx % 2, 0:rows, hd * V_HEAD:(hd + 1) * V_HEAD] = _nn(p.astype(BF16), v_buf[hd, 0:klen, :]) / l
                    grp, lane = _stat_slot(hd)
                    lse_ref[grp, q0:q0 + rows, lane:lane + 1] = m + jnp.log(l)
                store(idx).start()
                if idx == n_t - 1:
                    store(idx - 1).wait()
                    store(idx).wait()
                    finish_wout()

    hbm = pl.BlockSpec(memory_space=pl.ANY)
    return pl.pallas_call(
        body,
        name="attn_fwd",
        grid=(n_t,),
        in_specs=[hbm, hbm, hbm, _const(SHARD_OUT, D)],
        out_specs=[hbm, _const(STAT_GROUPS, N, 128), hbm],
        out_shape=[jax.ShapeDtypeStruct((N, HEADS * V_HEAD), F32), jax.ShapeDtypeStruct((STAT_GROUPS, N, 128), F32),
                   jax.ShapeDtypeStruct((D, D), BF16)],
        scratch_shapes=[pltpu.VMEM((HEADS, N, 256), BF16), pltpu.VMEM((HEADS, N, 256), BF16),
                        pltpu.VMEM((HEADS, N, V_HEAD), BF16), pltpu.VMEM((2, TQ, HEADS * V_HEAD), F32),
                        pltpu.VMEM((SHARD_OUT, D), BF16),
                        pltpu.SemaphoreType.DMA((3, 2)), pltpu.SemaphoreType.DMA((2,))]
        + [pltpu.SemaphoreType.DMA((3,))] * 4 + [pltpu.SemaphoreType.DMA],
        compiler_params=_cparams(dimension_semantics=("arbitrary",), collective_id=1),
    )(q, k, v, wout_s)


def _inv_count(row0, rows, w):
    row = row0 + lax.broadcasted_iota(jnp.int32, (rows, 1), 0)
    return 1.0 / jnp.clip(row - (PAD - 1), 1, w).astype(F32)


def _mid(h, tgt, pool_in, pool_gate, attn_gate, attn, pool_w, pool_scale, wout, gf):
    tr = ROWS_MID
    per = tr // HALO
    ng = len(POOL_WINDOWS)

    def body(h_ref, t_ref, pin_ref, halo_ref, pg_ref, ag_ref, at_ref, pw_ref, ps_ref, wout_ref, gf_ref,
             dh2_ref, do_ref, delta_ref, dag_ref, dpg_ref, dpl_ref, dwout_ref, dpw_ref, dps_ref, dgf_ref, loss_ref):
        i = pl.program_id(0)

        @pl.when(i == 0)
        def _():
            dwout_ref[...] = jnp.zeros_like(dwout_ref)
            dpw_ref[...] = jnp.zeros_like(dpw_ref)
            dps_ref[...] = jnp.zeros_like(dps_ref)
            dgf_ref[...] = jnp.zeros_like(dgf_ref)
            loss_ref[...] = jnp.zeros_like(loss_ref)

        row0 = i * tr
        real = (row0 + lax.broadcasted_iota(jnp.int32, (tr, 1), 0)) >= HEAD_ROWS
        h = h_ref[...]

        halo = jnp.where(i > 0, halo_ref[...], 0.0)
        ext = jnp.concatenate([halo, pin_ref[...]], axis=0)
        pooled = []
        for g, w in enumerate(POOL_WINDOWS):
            e = ext[:, g * POOL_GROUP:(g + 1) * POOL_GROUP]
            acc = e
            shift = 1
            while shift < w:
                acc = acc + pltpu.roll(acc, shift, 0)
                shift *= 2
            pooled.append((acc[HALO:] * _inv_count(row0, tr, w) - e[HALO:]).astype(BF16))
        pw = [pw_ref[g].astype(BF16) for g in range(ng)]
        mixed = jnp.concatenate([_nn(pooled[g], pw[g]) for g in range(ng)], axis=1)
        ps = ps_ref[...]
        mixed_s = mixed * ps
        pg = pg_ref[...]
        sig_p = _sigmoid(pg)
        silu_p = pg * sig_p
        pool_out = (silu_p * mixed_s).astype(BF16)
        ag = ag_ref[...]
        sig_a = _sigmoid(ag)
        silu_a = ag * sig_a
        at = at_ref[...]
        attn_out = (silu_a * at).astype(BF16)
        cat = jnp.concatenate([pool_out, attn_out], axis=1)
        h2 = h + _nn(cat, wout_ref[...])

        r2 = lax.rsqrt(jnp.mean(h2 * h2, axis=-1, keepdims=True) + EPS)
        n2 = h2 * r2
        gfv = gf_ref[...]
        err = jnp.where(real, n2 * gfv - t_ref[...], 0.0)
        loss_ref[...] += jnp.sum(jnp.sum(err * err, axis=-1, keepdims=True), axis=0, keepdims=True) * (0.5 / D)
        dy = err * (1.0 / D)
        dgf_ref[...] += jnp.sum(dy * n2, axis=0, keepdims=True)
        dn = dy * gfv
        dh2 = r2 * (dn - n2 * jnp.mean(dn * n2, axis=-1, keepdims=True))
        dh2_ref[...] = dh2
        dh2b = dh2.astype(BF16)

        dwout_ref[...] += _tn(cat, dh2b)
        dcat = _nt(dh2b, wout_ref[...])
        dpo = dcat[:, 0:D_POOL]
        dao = dcat[:, D_POOL:D]
        do = dao * silu_a
        prod = do * at
        delta_ref[...] = jnp.zeros_like(delta_ref)
        for hd in range(HEADS):
            grp, lane = _stat_slot(hd)
            cols = slice(hd * V_HEAD, (hd + 1) * V_HEAD)
            do_ref[grp, :, lane * V_HEAD:(lane + 1) * V_HEAD] = do[:, cols].astype(BF16)
            delta_ref[grp, :, lane:lane + 1] = jnp.sum(prod[:, cols], axis=-1, keepdims=True)
        dag_ref[...] = (dao * at * (sig_a * (1.0 + ag * (1.0 - sig_a)))).astype(BF16)
        dmixed_s = dpo * silu_p
        dpg_ref[...] = (dpo * mixed_s * (sig_p * (1.0 + pg * (1.0 - sig_p)))).astype(BF16)
        dps_ref[...] += jnp.sum(dmixed_s * mixed, axis=0, keepdims=True)
        dmixed = (dmixed_s * ps).astype(BF16)
        dpl = []
        for g in range(ng):
            dm = dmixed[:, g * POOL_GROUP:(g + 1) * POOL_GROUP]
            dpl.append(_nt(dm, pw[g]))
            dpw_ref[g] += _tn(pooled[g], dm)
        dpl_ref[...] = jnp.concatenate(dpl, axis=1)

    halo_spec = pl.BlockSpec((HALO, D_POOL), lambda i: (jnp.maximum(i * per - 1, 0), 0))
    return pl.pallas_call(
        body,
        name="mid",
        grid=(N // tr,),
        in_specs=[
            _rows(D, tr), _rows(D, tr), _rows(D_POOL, tr), halo_spec, _rows(D_POOL, tr), _rows(D_POOL, tr),
            _rows(D_POOL, tr), _const(ng, POOL_GROUP, POOL_GROUP), _const(1, D_POOL), _const(D, D), _const(1, D),
        ],
        out_specs=[
            _rows(D, tr), pl.BlockSpec((STAT_GROUPS, tr, HEADS_PER_STEP_BWD * V_HEAD), lambda i: (0, i, 0)),
            pl.BlockSpec((STAT_GROUPS, tr, 128), lambda i: (0, i, 0)),
            _rows(D_POOL, tr), _rows(D_POOL, tr), _rows(D_POOL, tr),
            _const(D, D), _const(ng, POOL_GROUP, POOL_GROUP), _const(1, D_POOL), _const(1, D), _const(1, 128),
        ],
        out_shape=[
            jax.ShapeDtypeStruct((N, D), F32), jax.ShapeDtypeStruct((STAT_GROUPS, N, HEADS_PER_STEP_BWD * V_HEAD), BF16),
            jax.ShapeDtypeStruct((STAT_GROUPS, N, 128), F32),
            jax.ShapeDtypeStruct((N, D_POOL), BF16), jax.ShapeDtypeStruct((N, D_POOL), BF16),
            jax.ShapeDtypeStruct((N, D_POOL), F32), jax.ShapeDtypeStruct((D, D), F32),
            jax.ShapeDtypeStruct((ng, POOL_GROUP, POOL_GROUP), F32),
            jax.ShapeDtypeStruct((1, D_POOL), F32), jax.ShapeDtypeStruct((1, D), F32), jax.ShapeDtypeStruct((1, 128), F32),
        ],
        compiler_params=_cparams(dimension_semantics=("arbitrary",)),
    )(h, tgt, pool_in, pool_in, pool_gate, attn_gate, attn, pool_w, pool_scale, wout, gf)


def _unrope(dy, cosv, sinv):
    return dy * cosv + _swap64(dy * sinv) * _low_lanes()


def _attn_bwd(q, k, v, do, lse, delta, cosf, sinf, dwout, dpw):
    tiles = _attn_tiles()
    hp = HEADS_PER_STEP_BWD
    n_g = HEADS // hp
    n_t = len(tiles)
    half = SHARD_OUT // 2
    half_pw = PW_ROWS // 2
    swap_at, send_at, sum_at = (0, 3), (0, 5), (n_g - 1, n_t // 2)

    def body(q_hbm, k_hbm, v_hbm, do_hbm, lse_ref, delta_ref, cos_ref, sin_ref, dwout_hbm, dpw_ref, dq_hbm, dkv_ref, dkr_ref,
             gwout_ref, gpw_ref, q_buf, k_buf, v_buf, do_buf, dq_buf, dk_acc, dv_acc, own_w, sib_w, stage_w, recv_w, gw_buf,
             pw_sib, pw_chip, pw_recv, pw_buf,
             in_sems, out_sems, ow_sems, d2d_send, d2d_recv, ici_send, ici_recv, fin_send, fin_recv,
             pw_swap_send, pw_swap_recv, pw_ici_send, pw_ici_recv, pw_fin_send, pw_fin_recv):
        grp = pl.program_id(0)
        step = pl.program_id(1)
        heads = pl.ds(grp * hp, hp)
        x, y, c = lax.axis_index("x"), lax.axis_index("y"), lax.axis_index("c")
        me = 2 * x + y
        sibling = (x, y, 1 - c)

        def pw_rows(core):
            return pl.ds(pl.multiple_of(half_pw * core, half_pw), half_pw)

        def pw_swap():
            return _remote(dpw_ref, pw_sib, pw_swap_send.at[0], pw_swap_recv.at[0], sibling)

        def pw_ici(rel):
            fx, fy = _CHIP_RELS[rel]
            return _remote(pw_chip.at[pw_rows(c), :], pw_recv.at[rel - 1], pw_ici_send.at[rel - 1], pw_ici_recv.at[rel - 1],
                           (x ^ fx, y ^ fy, c))

        def pw_fin(core):
            spot = pw_buf.at[pw_rows(core), :]
            return _remote(spot, spot, pw_fin_send.at[0], pw_fin_recv.at[0], sibling)

        def chip_of(rel):
            fx, fy = _CHIP_RELS[rel]
            return 2 * (x ^ fx) + (y ^ fy)

        def piece(chip, core):
            return dwout_hbm.at[pl.ds(pl.multiple_of(SHARD_OUT * chip + half * core, half), half), :]

        def own_load(rel):
            return pltpu.make_async_copy(piece(chip_of(rel), c), own_w.at[rel], ow_sems.at[rel])

        def d2d_copy(rel):
            return _remote(piece(chip_of(rel), 1 - c), sib_w.at[rel], d2d_send.at[rel], d2d_recv.at[rel], sibling)

        def ici_copy(rel):
            fx, fy = _CHIP_RELS[rel]
            return _remote(stage_w.at[rel - 1], recv_w.at[rel - 1], ici_send.at[rel - 1], ici_recv.at[rel - 1],
                           (x ^ fx, y ^ fy, c))

        def fin_copy(core):
            spot = gw_buf.at[pl.ds(pl.multiple_of(half * core, half), half), :]
            return _remote(spot, spot, fin_send.at[0], fin_recv.at[0], sibling)

        @pl.when((grp == 0) & (step == 0))
        def _():
            _peer_signal(x, y, c)
            for rel in SEND_ORDER + (0,):
                own_load(rel).start()

        @pl.when((grp == swap_at[0]) & (step == swap_at[1]))
        def _():
            _peer_wait()
            pw_swap().start()
            for rel in SEND_ORDER + (0,):
                d2d_copy(rel).start()

        @pl.when((grp == send_at[0]) & (step == send_at[1]))
        def _():
            for rel in SEND_ORDER:
                own_load(rel).wait()
                d2d_copy(rel).wait_recv()
                stage_w[rel - 1] = (own_w[rel] + sib_w[rel]).astype(BF16)
                ici_copy(rel).start()
            pw_swap().wait_recv()
            pw_chip[...] = dpw_ref[...] + pw_sib[...]
            for rel in SEND_ORDER:
                pw_ici(rel).start()

        @pl.when((grp == sum_at[0]) & (step == sum_at[1]))
        def _():
            own_load(0).wait()
            d2d_copy(0).wait_recv()
            total = own_w[0] + sib_w[0]
            for rel in (1, 2, 3):
                ici_copy(rel).wait_recv()
                total = total + recv_w[rel - 1].astype(F32)
            gw_buf[pl.ds(pl.multiple_of(half * c, half), half), :] = total
            fin_copy(c).start()
            for rel in (1, 2, 3):
                pw_ici(rel).wait_recv()
            total = jnp.zeros((half_pw, POOL_GROUP), F32)
            for chip in range(CHIPS):
                flips = chip ^ me
                rel = jnp.where(flips == 2, 1, jnp.where(flips == 1, 2, flips))
                total = total + jnp.where(rel == 0, pw_chip[pw_rows(c), :], pw_recv[jnp.maximum(rel - 1, 0)])
            pw_buf[pw_rows(c), :] = total
            pw_fin(c).start()

        def finish_dwout():
            fin_copy(1 - c).wait_recv()
            pw_fin(1 - c).wait_recv()
            for rel in (0, 1, 2, 3):
                d2d_copy(rel).wait_send()
            for rel in (1, 2, 3):
                ici_copy(rel).wait_send()
                pw_ici(rel).wait_send()
            fin_copy(c).wait_send()
            pw_swap().wait_send()
            pw_fin(c).wait_send()
            gwout_ref[...] = gw_buf[...]
            gpw_ref[...] = pw_buf[...]

        def loads(g, idx):
            q0, rows, _ = tiles[idx]
            rs = pl.ds(q0, rows)
            par = (g * n_t + idx) % 2
            hs = pl.ds(g * hp, hp)
            pairs = ((q_hbm.at[hs, rs, :], q_buf.at[:, rs, :]), (k_hbm.at[hs, rs, :], k_buf.at[:, rs, :]),
                     (v_hbm.at[hs, rs, :], v_buf.at[:, rs, :]), (do_hbm.at[g, rs, :], do_buf.at[rs, :]))
            return [pltpu.make_async_copy(src, dst, in_sems.at[a, par]) for a, (src, dst) in enumerate(pairs)]

        def store(idx):
            q0, rows, _ = tiles[idx]
            return pltpu.make_async_copy(dq_buf.at[idx % 2, :, pl.ds(0, rows), :], dq_hbm.at[heads, pl.ds(q0, rows), :],
                                         out_sems.at[idx % 2])

        @pl.when(step == 0)
        def _():
            dk_acc[...] = jnp.zeros_like(dk_acc)
            dv_acc[...] = jnp.zeros_like(dv_acc)

        @pl.when((step == 0) & (grp == 0))
        def _():
            dkr_ref[...] = jnp.zeros_like(dkr_ref)
            for cp in loads(grp, 0):
                cp.start()

        for idx, (q0, rows, klen) in enumerate(tiles):
            @pl.when(step == idx)
            def _(idx=idx, q0=q0, rows=rows, klen=klen):
                for cp in loads(grp, idx):
                    cp.wait()
                if idx + 1 < n_t:
                    for cp in loads(grp, idx + 1):
                        cp.start()
                if idx >= 2:
                    store(idx - 2).wait()
                qs = pl.ds(q0, rows)
                for hd in range(hp):
                    qv = q_buf[hd, qs, :]
                    kv = k_buf[hd, 0:klen, :]
                    p = jnp.exp(_masked_scores(qv, kv, rows, klen) - lse_ref[0, qs, hd:hd + 1])
                    dob = do_buf[qs, hd * V_HEAD:(hd + 1) * V_HEAD]
                    ds = (p * (_nt(dob, v_buf[hd, 0:klen, :]) - delta_ref[0, qs, hd:hd + 1])).astype(BF16)
                    dq = _nn(ds, kv) * SCALE
                    dq_buf[idx % 2, hd, 0:rows, 0:QK_NOPE] = dq[:, 0:QK_NOPE].astype(BF16)
                    dq_buf[idx % 2, hd, 0:rows, QK_NOPE:] = _unrope(dq[:, QK_NOPE:], cos_ref[qs, :], sin_ref[qs, :]).astype(BF16)
                    dk_acc[hd, 0:klen, :] += _tn(ds, qv)
                    dv_acc[hd, 0:klen, :] += _tn(p.astype(BF16), dob)
                store(idx).start()

        @pl.when(step == n_t - 1)
        def _():
            @pl.when(grp + 1 < n_g)
            def _():
                for cp in loads(grp + 1, 0):
                    cp.start()

            for hd in range(hp):
                dkv_ref[hd, :, 0:QK_NOPE] = dk_acc[hd, :, 0:QK_NOPE].astype(BF16)
                dkv_ref[hd, :, QK_NOPE:] = dv_acc[hd].astype(BF16)
                dkr_ref[...] += dk_acc[hd, :, QK_NOPE:]
            store(n_t - 2).wait()
            store(n_t - 1).wait()

            @pl.when(grp == n_g - 1)
            def _():
                finish_dwout()

    hbm = pl.BlockSpec(memory_space=pl.ANY)
    stat = pl.BlockSpec((1, N, 128), lambda g, t: (g, 0, 0), pipeline_mode=pl.Buffered(1))
    piece_f32 = lambda lead: pltpu.VMEM((lead, half, D), F32)
    piece_bf16 = lambda lead: pltpu.VMEM((lead, half, D), BF16)
    return pl.pallas_call(
        body,
        name="attn_bwd",
        grid=(n_g, n_t),
        in_specs=[hbm, hbm, hbm, hbm, stat, stat, _const(N, 128), _const(N, 128), hbm, _const(PW_ROWS, POOL_GROUP)],
        out_specs=[hbm, pl.BlockSpec((hp, N, 256), lambda g, t: (g, 0, 0), pipeline_mode=pl.Buffered(1)), _const(N, 128),
                   _const(SHARD_OUT, D), _const(PW_ROWS, POOL_GROUP)],
        out_shape=[
            jax.ShapeDtypeStruct((HEADS, N, 256), BF16), jax.ShapeDtypeStruct((HEADS, N, 256), BF16),
            jax.ShapeDtypeStruct((N, 128), F32), jax.ShapeDtypeStruct((SHARD_OUT, D), F32),
            jax.ShapeDtypeStruct((PW_ROWS, POOL_GROUP), F32),
        ],
        scratch_shapes=[pltpu.VMEM((hp, N, 256), BF16), pltpu.VMEM((hp, N, 256), BF16), pltpu.VMEM((hp, N, V_HEAD), BF16),
                        pltpu.VMEM((N, hp * V_HEAD), BF16), pltpu.VMEM((2, hp, TQ, 256), BF16),
                        pltpu.VMEM((hp, N, 256), F32), pltpu.VMEM((hp, N, V_HEAD), F32),
                        piece_f32(CHIPS), piece_f32(CHIPS), piece_bf16(3), piece_bf16(3), pltpu.VMEM((SHARD_OUT, D), F32),
                        pltpu.VMEM((PW_ROWS, POOL_GROUP), F32), pltpu.VMEM((PW_ROWS, POOL_GROUP), F32),
                        pltpu.VMEM((3, half_pw, POOL_GROUP), F32), pltpu.VMEM((PW_ROWS, POOL_GROUP), F32),
                        pltpu.SemaphoreType.DMA((4, 2)), pltpu.SemaphoreType.DMA((2,)), pltpu.SemaphoreType.DMA((CHIPS,)),
                        pltpu.SemaphoreType.DMA((CHIPS,)), pltpu.SemaphoreType.DMA((CHIPS,)),
                        pltpu.SemaphoreType.DMA((3,)), pltpu.SemaphoreType.DMA((3,)),
                        pltpu.SemaphoreType.DMA((1,)), pltpu.SemaphoreType.DMA((1,)),
                        pltpu.SemaphoreType.DMA((1,)), pltpu.SemaphoreType.DMA((1,)),
                        pltpu.SemaphoreType.DMA((3,)), pltpu.SemaphoreType.DMA((3,)),
                        pltpu.SemaphoreType.DMA((1,)), pltpu.SemaphoreType.DMA((1,))],
        compiler_params=_cparams(dimension_semantics=("arbitrary", "arbitrary"), collective_id=2),
    )(q, k, v, do, lse, delta, cosf, sinf, dwout, dpw)


def _bwd_in(h, dh2, dq, dkv, dkr, cq, ckv, dpl, dpg, dag, norm_g, win, gq, wq, gkv, wkv, cosf, sinf, adam_out):
    tr = ROWS_BWD
    nb = N // tr
    per = tr // HALO
    lead = HEAD_ROWS
    adam_rows = SHARD_OUT // nb

    def body(h_ref, dh2_ref, dq_ref, dkv_ref, dkr_ref, cq_ref, ckv_ref, dpl_ref, halo_ref, dpg_ref, dag_ref,
             g_ref, win_ref, gq_ref, wq_ref, gkv_ref, wkv_ref, cos_ref, sin_ref, aw_ref, ag_ref, am_ref, av_ref,
             gx_ref, dmeta_ref, dsl_ref, dwq_ref, dwkv_ref, dg_ref, dgq_ref, dgkv_ref, ago_ref, ad_ref, anm_ref, anv_ref,
             dh_buf, gx_sem):
        i = pl.program_id(0)
        grad_out = ag_ref[...]
        ago_ref[...] = grad_out
        ad_ref[...], anm_ref[...], anv_ref[...] = _adamw_math(aw_ref[...], grad_out, am_ref[...], av_ref[...])

        @pl.when(i == 0)
        def _():
            dwq_ref[...] = jnp.zeros_like(dwq_ref)
            dwkv_ref[...] = jnp.zeros_like(dwkv_ref)
            dg_ref[...] = jnp.zeros_like(dg_ref)
            dgq_ref[...] = jnp.zeros_like(dgq_ref)
            dgkv_ref[...] = jnp.zeros_like(dgkv_ref)

        row0 = i * tr
        h = h_ref[...]
        r = lax.rsqrt(jnp.mean(h * h, axis=-1, keepdims=True) + EPS)
        n = h * r
        gv = g_ref[...]
        cq = cq_ref[...]
        rq = lax.rsqrt(jnp.mean(cq * cq, axis=-1, keepdims=True) + EPS)
        nq = cq * rq
        gqv = gq_ref[...]
        cqn = (nq * gqv).astype(BF16)
        dcqn = jnp.zeros((tr, Q_LORA), F32)
        for hd in range(HEADS):
            dqf = dq_ref[hd]
            dcqn = dcqn + _nn(dqf, wq_ref[hd])
            dwq_ref[hd] += _tn(dqf, cqn)
        dgq_ref[...] += jnp.sum(dcqn * nq, axis=0, keepdims=True)
        dnq = dcqn * gqv
        dcq = rq * (dnq - nq * jnp.mean(dnq * nq, axis=-1, keepdims=True))

        ckv = ckv_ref[...]
        rkv = lax.rsqrt(jnp.mean(ckv * ckv, axis=-1, keepdims=True) + EPS)
        nkv = ckv * rkv
        gkvv = gkv_ref[...]
        ckvn = (nkv * gkvv).astype(BF16)
        dckvn = jnp.zeros((tr, KV_LORA), F32)
        for hd in range(HEADS):
            dkv = dkv_ref[hd]
            dckvn = dckvn + _nt(dkv, wkv_ref[hd])
            dwkv_ref[hd] += _tn(ckvn, dkv)
        dgkv_ref[...] += jnp.sum(dckvn * nkv, axis=0, keepdims=True)
        dnkv = dckvn * gkvv
        dckv = rkv * (dnkv - nkv * jnp.mean(dnkv * nkv, axis=-1, keepdims=True))
        dkr = _unrope(dkr_ref[...], cos_ref[...], sin_ref[...])

        cur = dpl_ref[...]
        halo = jnp.where(i < nb - 1, halo_ref[...], 0.0)
        dpi = []
        for g, w in enumerate(POOL_WINDOWS):
            sl = slice(g * POOL_GROUP, (g + 1) * POOL_GROUP)
            a = jnp.concatenate([cur[:, sl] * _inv_count(row0, tr, w), halo[:, sl] * _inv_count(row0 + tr, HALO, w)], axis=0)
            acc = a
            shift = 1
            while shift < w:
                acc = acc + pltpu.roll(acc, tr + HALO - shift, 0)
                shift *= 2
            dpi.append(acc[0:tr] - cur[:, sl])

        du = jnp.concatenate([t.astype(BF16) for t in dpi] + [dpg_ref[...]] + [t.astype(BF16) for t in (dcq, dckv, dkr)],
                             axis=1)
        dagb = dag_ref[...]
        by_row = jnp.concatenate(dpi + [dpg_ref[...].astype(F32), dcq, dckv, dkr[:, 0:QK_ROPE], dagb.astype(F32),
                                        jnp.zeros((tr, SHARD_PAD - SHARD_IN), F32)], axis=1)
        for chip in range(CHIPS):
            dsl_ref[chip] = by_row[:, SHARD_IN * chip:SHARD_IN * chip + SHARD_PAD].astype(BF16)
        dhn = _nn(du, win_ref[0:O_KR_END, :]) + _nn(dagb, win_ref[O_AG:D_IN, :])
        dg_ref[...] += jnp.sum(dhn * n, axis=0, keepdims=True)
        dn = dhn * gv
        dh = dh2_ref[...] + r * (dn - n * jnp.mean(dn * n, axis=-1, keepdims=True))

        first = pltpu.make_async_copy(dh_buf.at[pl.ds(lead, tr - lead), :], gx_ref.at[pl.ds(0, tr - lead), :], gx_sem)
        later = lambda step: pltpu.make_async_copy(
            dh_buf, gx_ref.at[pl.ds(pl.multiple_of(step * tr - lead, 16), tr), :], gx_sem)

        @pl.when(i == 1)
        def _():
            first.wait()

        @pl.when(i > 1)
        def _():
            later(i - 1).wait()

        dh_buf[...] = dh

        @pl.when(i == 0)
        def _():
            first.start()
            for chip in range(CHIPS):
                dmeta_ref[chip] = dh[PAD:HEAD_ROWS, chip * 256:(chip + 1) * 256]

        @pl.when(i > 0)
        def _():
            later(i).start()

        @pl.when(i == nb - 1)
        def _():
            later(i).wait()

    head = lambda w: pl.BlockSpec((HEADS, tr, w), lambda i: (0, i, 0))
    halo_spec = pl.BlockSpec((HALO, D_POOL), lambda i: (jnp.minimum((i + 1) * per, N // HALO - 1), 0))
    return pl.pallas_call(
        body,
        name="bwd_in",
        grid=(nb,),
        in_specs=[
            _rows(D, tr), _rows(D, tr), head(256), head(256), _rows(128, tr), _rows(Q_LORA, tr), _rows(KV_LORA, tr),
            _rows(D_POOL, tr), halo_spec, _rows(D_POOL, tr), _rows(D_POOL, tr),
            _const(1, D), _const(D_IN, D), _const(1, Q_LORA), _const(HEADS, 256, Q_LORA),
            _const(1, KV_LORA), _const(HEADS, KV_LORA, 256), _rows(128, tr), _rows(128, tr),
        ] + [_rows(D, adam_rows)] * 4,
        out_specs=[
            pl.BlockSpec(memory_space=pl.ANY), _const(CHIPS, N_META, 256),
            pl.BlockSpec((CHIPS, tr, SHARD_PAD), lambda i: (0, i, 0)), _const(HEADS, 256, Q_LORA),
            _const(HEADS, KV_LORA, 256), _const(1, D), _const(1, Q_LORA), _const(1, KV_LORA),
        ] + [_rows(D, adam_rows)] * 4,
        out_shape=[
            jax.ShapeDtypeStruct((S, D), F32), jax.ShapeDtypeStruct((CHIPS, N_META, 256), F32),
            jax.ShapeDtypeStruct((CHIPS, N, SHARD_PAD), BF16), jax.ShapeDtypeStruct((HEADS, 256, Q_LORA), F32),
            jax.ShapeDtypeStruct((HEADS, KV_LORA, 256), F32),
            jax.ShapeDtypeStruct((1, D), F32), jax.ShapeDtypeStruct((1, Q_LORA), F32), jax.ShapeDtypeStruct((1, KV_LORA), F32),
        ] + [jax.ShapeDtypeStruct((SHARD_OUT, D), F32)] * 4,
        scratch_shapes=[pltpu.VMEM((tr, D), F32), pltpu.SemaphoreType.DMA],
        compiler_params=_cparams(dimension_semantics=("arbitrary",)),
    )(h, dh2, dq, dkv, dkr, cq, ckv, dpl, dpl, dpg, dag, norm_g, win, gq, wq, gkv, wkv, cosf, sinf, *adam_out)


def _local_step(h, tgt, norm_g, win, gq, wq, gkv, wkv, pool_w, pool_scale, wout_s, m_wout_s, v_wout_s, gf, cosf, sinf):
    pool_in, pool_gate, cq, ckv, attn_gate, q, k, v, hn = _fwd_in(h, norm_g, win, gq, wq, gkv, wkv, cosf, sinf)
    attn, lse, wout = _attn_fwd(q, k, v, wout_s)
    dh2, do, delta, dag, dpg, dpl, dwout, dpw, dps, dgf, loss = _mid(
        h, tgt, pool_in, pool_gate, attn_gate, attn, pool_w, pool_scale, wout, gf)
    dq, dkv, dkr, gwout, gpw = _attn_bwd(q, k, v, do, lse, delta, cosf, sinf, dwout, dpw.reshape(PW_ROWS, POOL_GROUP))
    gx, dmeta, dsl, dwq, dwkv, dg, dgq, dgkv, *r_out = _bwd_in(
        h, dh2, dq, dkv, dkr, cq, ckv, dpl, dpg, dag, norm_g, win, gq, wq, gkv, wkv, cosf, sinf,
        (wout_s, gwout, m_wout_s, v_wout_s))
    return dict(gx=gx, dmeta=dmeta, dsl=dsl, hn=hn, dwq=dwq, dwkv=dwkv, r_out=tuple(r_out), dg=dg, dgq=dgq,
                dgkv=dgkv, gpw=gpw, dps=dps, dgf=dgf, loss=loss)


_CHIP_RELS = ((0, 0), (1, 0), (0, 1), (1, 1))

_ARR_ROWS = (SHARD_IN, SHARD_OUT, 256, KV_LORA, N_META)
_ARR_COLS = (D, D, Q_LORA, 256, 256)
_PIECES = (
    (0, 0, 256, 0), (0, 256, SHARD_IN - 256, 1),
    (1, 0, 128, 0), (1, 128, 128, 1),
    (2, 0, 128, 0), (2, 128, 128, 1),
    (3, 0, 64, 0), (3, 64, 64, 1),
    (4, 0, N_META, 0),
)
_NP = len(_PIECES)
_PIECE_MAX = (256, 128, 128, 64, N_META)


def _gathered_at(refs, arr, chip, r0, n):
    if arr in (0, 1):
        return refs[arr].at[pl.ds(pl.multiple_of(_ARR_ROWS[arr] * chip + r0, 16), n), :]
    return refs[arr].at[chip, pl.ds(r0, n), :]


def _remote(src, dst, send_sem, recv_sem, to):
    return pltpu.make_async_remote_copy(src_ref=src, dst_ref=dst, send_sem=send_sem, recv_sem=recv_sem,
                                        device_id=to, device_id_type=MESH)


def _gather_weights(winT_s, wqT_s, wkv_s, meta_s, x2, tgt2):
    arrays = (0, 2, 3, 4)

    def body(win_ref, wq_ref, wkv_ref, meta_ref, x_ref, t_ref, win_o, wq_o, wkv_o, h_o, tp_o,
             s_win, s_wq, s_wkv, meta_all, head_buf, x_buf, t_buf, ici_send, ici_recv, fwd_send, fwd_recv,
             loc_sems, own_sems):
        x, y, c = lax.axis_index("x"), lax.axis_index("y"), lax.axis_index("c")
        me = 2 * x + y
        stage = (s_win, None, s_wq, s_wkv, meta_ref)
        outs = (win_o, None, wq_o, wkv_o, meta_all)

        _peer_signal(x, y, c)

        frames = pl.ds(HEAD_ROWS, S)
        loads = [pltpu.make_async_copy(x_ref, x_buf, loc_sems.at[0]), pltpu.make_async_copy(t_ref, t_buf, loc_sems.at[1])]
        local = [pltpu.make_async_copy(x_buf, h_o.at[frames, :], loc_sems.at[0]),
                 pltpu.make_async_copy(t_buf, tp_o.at[frames, :], loc_sems.at[1])]
        for cp in loads:
            cp.start()

        s_win[...] = win_ref[...].astype(BF16)
        s_wq[0:QK, :] = wq_ref[...].astype(BF16)
        s_wq[QK:256, :] = jnp.zeros((256 - QK, Q_LORA), BF16)
        s_wkv[...] = wkv_ref[...].astype(BF16)
        head_buf[...] = jnp.zeros_like(head_buf)
        zeros = pltpu.make_async_copy(head_buf, tp_o.at[pl.ds(0, HEAD_ROWS), :], loc_sems.at[2])
        zeros.start()

        def chip_of(rel):
            fx, fy = _CHIP_RELS[rel]
            return 2 * (x ^ fx) + (y ^ fy)

        def same_core_of(rel):
            fx, fy = _CHIP_RELS[rel]
            return (x ^ fx, y ^ fy, c)

        def ici_copy(rel, i, src_chip, to):
            arr, r0, n, _ = _PIECES[i]
            k = (rel - 1) * _NP + i
            return _remote(stage[arr].at[pl.ds(r0, n), :], _gathered_at(outs, arr, src_chip, r0, n),
                           ici_send.at[k], ici_recv.at[k], to)

        def fwd_copy(rel, i, to):
            arr, r0, n, _ = _PIECES[i]
            k = (rel - 1) * _NP + i
            place = _gathered_at(outs, arr, chip_of(rel), r0, n)
            return _remote(place, place, fwd_send.at[k], fwd_recv.at[k], to)

        _peer_wait()
        for core in (0, 1):
            @pl.when(c == core)
            def _(core=core):
                mine = [i for i in range(_NP) if _PIECES[i][3] == core and _PIECES[i][0] in arrays]
                theirs = [i for i in range(_NP) if _PIECES[i][3] != core and _PIECES[i][0] in arrays]
                order = (1, 2, 3)
                sends = [ici_copy(rel, i, me, same_core_of(rel)) for rel in order for i in mine]
                for cp in sends:
                    cp.start()
                for ld, st in zip(loads, local):
                    ld.wait()
                    st.start()
                own = [pltpu.make_async_copy(stage[arr], _gathered_at(outs, arr, me, 0, _ARR_ROWS[arr]), own_sems.at[arr])
                       for arr in arrays if arr != 4]
                for cp in own:
                    cp.start()
                meta_all[me] = meta_ref[...]
                for rel in order:
                    for i in mine:
                        ici_copy(rel, i, chip_of(rel), (x, y, c)).wait_recv()
                        fwd = fwd_copy(rel, i, (x, y, 1 - c))
                        fwd.start()
                        sends.append(fwd)
                for rel in order:
                    for i in theirs:
                        fwd_copy(rel, i, (x, y, c)).wait_recv()
                for cp in sends:
                    cp.wait_send()
                for cp in own:
                    cp.wait()

        zeros.wait()
        for chip in range(CHIPS):
            head_buf[PAD:HEAD_ROWS, chip * 256:(chip + 1) * 256] = meta_all[chip]
        head = pltpu.make_async_copy(head_buf, h_o.at[pl.ds(0, HEAD_ROWS), :], loc_sems.at[2])
        head.start()
        head.wait()
        for cp in local:
            cp.wait()

    vm = pl.BlockSpec(memory_space=pltpu.VMEM)
    hbm = pl.BlockSpec(memory_space=pl.ANY)
    return pl.pallas_call(
        body,
        name="gather_weights",
        in_specs=[vm] * 4 + [hbm] * 2,
        out_specs=[hbm] * 5,
        out_shape=[
            jax.ShapeDtypeStruct((D_IN, D), BF16),
            jax.ShapeDtypeStruct((CHIPS, 256, Q_LORA), BF16), jax.ShapeDtypeStruct((CHIPS, KV_LORA, 256), BF16),
            jax.ShapeDtypeStruct((N, D), F32), jax.ShapeDtypeStruct((N, D), F32),
        ],
        scratch_shapes=[pltpu.VMEM((_ARR_ROWS[a], _ARR_COLS[a]), BF16) for a in (0, 2, 3)]
        + [pltpu.VMEM((CHIPS, N_META, 256), F32), pltpu.VMEM((HEAD_ROWS, D), F32), pltpu.VMEM((S, D), F32),
           pltpu.VMEM((S, D), F32)]
        + [pltpu.SemaphoreType.DMA((3 * _NP,))] * 4 + [pltpu.SemaphoreType.DMA((3,)), pltpu.SemaphoreType.DMA((4,))],
        compiler_params=_cparams(collective_id=0),
    )(winT_s, wqT_s, wkv_s, meta_s, x2, tgt2)


_SM_ROWS = (VEC_ROWS,)
_SM_COLS = (D,)
_SM_PIECES = ((0, 0, VEC_ROWS, 0),)
_NSP = len(_SM_PIECES)
_NSB = len(_SM_ROWS)


def _reduce_grads(dsl, hn, dwq, dwkv, dmeta4, dg, dgf, dgq, dgkv, dps, loss):
    arrays = (0, 2, 3, 4)
    loaded = (2, 3, 4)
    shard_order = SEND_ORDER + (0,)

    def body(dsl_hbm, hn_hbm, dwq_ref, dwkv_ref, dmeta_ref, dg_ref, dgf_ref, dgq_ref, dgkv_ref, dps_ref,
             loss_ref, gwin_o, gwq_o, gwkv_o, gmeta_o, gg_o, ggf_o, ggq_o, ggkv_o, gps_o, gloss_o,
             ow2, ow3, ow4, sb0, sb2, sb3, sb4, st0, st2, st3, st4, rc0, rc2, rc3, rc4,
             vec, sm_sb0, sm_cs0, sm_rc0, vec_fin, slab_v, hn_v, dwin_buf, own0,
             own_sems, d2d_send, d2d_recv, ici_send, ici_recv, fin_send, fin_recv,
             swap_send, swap_recv, smi_send, smi_recv, smf_send, smf_recv, ld_sems):
        x, y, c = lax.axis_index("x"), lax.axis_index("y"), lax.axis_index("c")
        me = 2 * x + y
        _peer_signal(x, y, c)
        grads = (None, None, dwq_ref, dwkv_ref, dmeta_ref)
        outs = (gwin_o, None, gwq_o, gwkv_o, gmeta_o)
        own_buf = (None, None, ow2, ow3, ow4)
        sib_buf = (sb0, None, sb2, sb3, sb4)
        stage = (st0, None, st2, st3, st4)
        recv = (rc0, None, rc2, rc3, rc4)
        sm_mine = (vec,)
        sm_sib = (sm_sb0,)
        sm_chip = (sm_cs0,)
        sm_recv = (sm_rc0,)
        sm_out = (vec_fin,)
        sibling = (x, y, 1 - c)

        def chip_of(rel):
            fx, fy = _CHIP_RELS[rel]
            return 2 * (x ^ fx) + (y ^ fy)

        def same_core_of(rel):
            fx, fy = _CHIP_RELS[rel]
            return (x ^ fx, y ^ fy, c)

        hn_load = pltpu.make_async_copy(hn_hbm, hn_v, ld_sems.at[CHIPS])

        def slab_load(rel):
            return pltpu.make_async_copy(dsl_hbm.at[chip_of(rel)], slab_v.at[rel], ld_sems.at[rel])

        hn_load.start()
        slab_load(shard_order[0]).start()

        def slot(bufs, i, idx):
            arr, _, n, _ = _PIECES[i]
            return bufs[arr].at[idx, pl.ds(0, n), :]

        def own_load(rel, i):
            arr, r0, n, _ = _PIECES[i]
            return pltpu.make_async_copy(_gathered_at(grads, arr, chip_of(rel), r0, n), slot(own_buf, i, rel),
                                         own_sems.at[rel * _NP + i])

        def d2d_copy(rel, i):
            arr, r0, n, _ = _PIECES[i]
            k = rel * _NP + i
            return _remote(_gathered_at(grads, arr, chip_of(rel), r0, n), slot(sib_buf, i, rel),
                           d2d_send.at[k], d2d_recv.at[k], sibling)

        def ici_copy(rel, i):
            k = (rel - 1) * _NP + i
            return _remote(slot(stage, i, rel - 1), slot(recv, i, rel - 1), ici_send.at[k], ici_recv.at[k],
                           same_core_of(rel))

        def fin_copy(i):
            arr, r0, n, _ = _PIECES[i]
            place = outs[arr].at[pl.ds(r0, n), :]
            return _remote(place, place, fin_send.at[i], fin_recv.at[i], sibling)

        def sm_ici_copy(rel, j):
            blk, r0, n, _ = _SM_PIECES[j]
            k = (rel - 1) * _NSP + j
            return _remote(sm_chip[blk].at[pl.ds(r0, n), :], sm_recv[blk].at[rel - 1, pl.ds(r0, n), :],
                           smi_send.at[k], smi_recv.at[k], same_core_of(rel))

        def sm_fin_copy(j):
            blk, r0, n, _ = _SM_PIECES[j]
            place = sm_out[blk].at[pl.ds(r0, n), :]
            return _remote(place, place, smf_send.at[j], smf_recv.at[j], sibling)

        vec[...] = jnp.zeros_like(vec)
        vec[0:1, :] = dg_ref[...]
        vec[1:2, :] = dgf_ref[...]
        vec[2:3, V_GQ:V_GQ + Q_LORA] = dgq_ref[...]
        vec[2:3, V_GKV:V_GKV + KV_LORA] = dgkv_ref[...]
        vec[2:3, V_PS:V_PS + D_POOL] = dps_ref[...]
        vec[2:3, V_LOSS:D] = loss_ref[...]
        _peer_wait()
        swaps = [_remote(sm_mine[b], sm_sib[b], swap_send.at[b], swap_recv.at[b], sibling) for b in range(_NSB)]
        for cp in swaps:
            cp.start()

        for core in (0, 1):
            @pl.when(c == core)
            def _(core=core):
                mine = [i for i in range(_NP) if _PIECES[i][3] == core and _PIECES[i][0] in loaded]
                theirs = [i for i in range(_NP) if _PIECES[i][3] != core and _PIECES[i][0] in loaded]
                i0 = next(i for i in range(_NP) if _PIECES[i][0] == 0 and _PIECES[i][3] == core)
                j0 = next(i for i in range(_NP) if _PIECES[i][0] == 0 and _PIECES[i][3] != core)
                sm_mine_p = [j for j in range(_NSP) if _SM_PIECES[j][3] == core]
                sm_theirs_p = [j for j in range(_NSP) if _SM_PIECES[j][3] != core]
                sends = list(swaps)

                for rel in shard_order:
                    for i in theirs:
                        cp = d2d_copy(rel, i)
                        cp.start()
                        sends.append(cp)
                    for i in mine:
                        own_load(rel, i).start()

                def piece_rows(i):
                    return pl.ds(_PIECES[i][1], _PIECES[i][2])

                def form(rel, i):
                    r0, n = _PIECES[i][1], _PIECES[i][2]
                    dwin_buf[rel, r0:r0 + n, :] = _tn(slab_v[rel, :, r0:r0 + _PIECE_MAX[0]], hn_v[...])[0:n, :]

                def d2d0(rel, i):
                    return _remote(dwin_buf.at[rel, piece_rows(i), :], slot(sib_buf, i, rel),
                                   d2d_send.at[rel * _NP + i], d2d_recv.at[rel * _NP + i], sibling)

                def settle(rel):
                    d2d0(rel, i0).wait_recv()
                    total = dwin_buf[rel, piece_rows(i0), :] + slot(sib_buf, i0, rel)[...]
                    if rel == 0:
                        own0[0:_PIECES[i0][2], :] = total
                    else:
                        slot(stage, i0, rel - 1)[...] = total.astype(BF16)
                        cp = ici_copy(rel, i0)
                        cp.start()
                        sends.append(cp)

                hn_load.wait()
                for n, rel in enumerate(shard_order):
                    slab_load(rel).wait()
                    if n == 0:
                        for later in shard_order[1:]:
                            slab_load(later).start()
                    form(rel, j0)
                    cp = d2d0(rel, j0)
                    cp.start()
                    sends.append(cp)
                    if n > 0:
                        settle(shard_order[n - 1])
                    form(rel, i0)
                settle(shard_order[-1])

                for rel in SEND_ORDER:
                    for i in mine:
                        arr, r0, n, _ = _PIECES[i]
                        own_load(rel, i).wait()
                        d2d_copy(rel, i).wait_recv()
                        total = slot(own_buf, i, rel)[...] + slot(sib_buf, i, rel)[...]
                        slot(stage, i, rel - 1)[...] = total.astype(stage[arr].dtype)
                        cp = ici_copy(rel, i)
                        cp.start()
                        sends.append(cp)

                for b in range(_NSB):
                    swaps[b].wait_recv()
                    sm_chip[b][...] = sm_mine[b][...] + sm_sib[b][...]
                for rel in SEND_ORDER:
                    for j in sm_mine_p:
                        cp = sm_ici_copy(rel, j)
                        cp.start()
                        sends.append(cp)

                for i in mine:
                    arr, r0, n, _ = _PIECES[i]
                    own_load(0, i).wait()
                    d2d_copy(0, i).wait_recv()
                    total = slot(own_buf, i, 0)[...] + slot(sib_buf, i, 0)[...]
                    for rel in (1, 2, 3):
                        ici_copy(rel, i).wait_recv()
                        total = total + slot(recv, i, rel - 1)[...].astype(F32)
                    outs[arr][pl.ds(r0, n), :] = total
                    cp = fin_copy(i)
                    cp.start()
                    sends.append(cp)
                total = own0[0:_PIECES[i0][2], :]
                for rel in (1, 2, 3):
                    ici_copy(rel, i0).wait_recv()
                    total = total + slot(recv, i0, rel - 1)[...].astype(F32)
                outs[0][pl.ds(_PIECES[i0][1], _PIECES[i0][2]), :] = total
                cp = fin_copy(i0)
                cp.start()
                sends.append(cp)

                for j in sm_mine_p:
                    blk, r0, n, _ = _SM_PIECES[j]
                    for rel in (1, 2, 3):
                        sm_ici_copy(rel, j).wait_recv()
                    total = jnp.zeros((n, _SM_COLS[blk]), F32)
                    for chip in range(CHIPS):
                        flips = chip ^ me
                        rel = jnp.where(flips == 2, 1, jnp.where(flips == 1, 2, flips))
                        theirs_rows = sm_recv[blk][jnp.maximum(rel - 1, 0), pl.ds(r0, n), :]
                        total = total + jnp.where(rel == 0, sm_chip[blk][pl.ds(r0, n), :], theirs_rows)
                    sm_out[blk][pl.ds(r0, n), :] = total
                    cp = sm_fin_copy(j)
                    cp.start()
                    sends.append(cp)

                for i in theirs + [j0]:
                    fin_copy(i).wait_recv()
                for j in sm_theirs_p:
                    sm_fin_copy(j).wait_recv()
                for cp in sends:
                    cp.wait_send()

        gg_o[...] = vec_fin[0:1, :]
        ggf_o[...] = vec_fin[1:2, :]
        ggq_o[...] = vec_fin[2:3, V_GQ:V_GQ + Q_LORA]
        ggkv_o[...] = vec_fin[2:3, V_GKV:V_GKV + KV_LORA]
        gps_o[...] = vec_fin[2:3, V_PS:V_PS + D_POOL]
        gloss_o[...] = vec_fin[2:3, V_LOSS:D]

    vm = pl.BlockSpec(memory_space=pltpu.VMEM)
    piece_buf = lambda lead, dtype, which=arrays: [
        pltpu.VMEM((lead, _PIECE_MAX[a], _ARR_COLS[a]), F32 if a == 4 else dtype) for a in which]
    sm_buf = lambda *lead: [pltpu.VMEM(lead + (_SM_ROWS[b], _SM_COLS[b]), F32) for b in range(_NSB)]
    dma = lambda n: [pltpu.SemaphoreType.DMA((n,))] * 2
    return pl.pallas_call(
        body,
        name="reduce_grads",
        in_specs=[pl.BlockSpec(memory_space=pl.ANY)] * 4 + [vm] * 7,
        out_specs=[vm] * 10,
        out_shape=[jax.ShapeDtypeStruct((_ARR_ROWS[a], _ARR_COLS[a]), F32) for a in arrays]
        + [jax.ShapeDtypeStruct((1, D), F32),
           jax.ShapeDtypeStruct((1, D), F32), jax.ShapeDtypeStruct((1, Q_LORA), F32),
           jax.ShapeDtypeStruct((1, KV_LORA), F32), jax.ShapeDtypeStruct((1, D_POOL), F32),
           jax.ShapeDtypeStruct((1, 128), F32)],
        scratch_shapes=piece_buf(CHIPS, F32, loaded) + piece_buf(CHIPS, F32) + piece_buf(3, BF16) + piece_buf(3, BF16)
        + [pltpu.VMEM((VEC_ROWS, D), F32)] + sm_buf() + sm_buf() + sm_buf(3) + [pltpu.VMEM((VEC_ROWS, D), F32)]
        + [pltpu.VMEM((CHIPS, N, SHARD_PAD), BF16), pltpu.VMEM((N, D), BF16),
           pltpu.VMEM((CHIPS, SHARD_PAD, D), F32), pltpu.VMEM((_PIECE_MAX[0], D), F32)]
        + [pltpu.SemaphoreType.DMA((CHIPS * _NP,))]
        + dma(CHIPS * _NP) + dma(3 * _NP) + dma(_NP) + dma(_NSB) + dma(3 * _NSP) + dma(_NSP)
        + [pltpu.SemaphoreType.DMA((CHIPS + 1,))],
        compiler_params=_cparams(collective_id=3),
    )(dsl, hn, dwq, dwkv, dmeta4, dg, dgf, dgq, dgkv, dps, loss)


def _adamw_math(w, g, m, v):
    m = B1 * m + (1.0 - B1) * g
    v = B2 * v + (1.0 - B2) * (g * g)
    m_hat = m / C1
    v_hat = v / C2
    delta = -LR * (m_hat / (jnp.sqrt(v_hat) + ADAM_EPS) + WD * w)
    return delta, m, v


def _adamw(big, block_rows, groups):
    rows, cols = big[0].shape
    n = len(groups)

    def body(*refs):
        w_ref, g_ref, m_ref, v_ref = refs[0:4]
        small_in = refs[4:4 + 4 * n]
        go_ref, d_ref, nm_ref, nv_ref = refs[4 + 4 * n:8 + 4 * n]
        small_out = refs[8 + 4 * n:]
        g = g_ref[...]
        go_ref[...] = g
        d_ref[...], nm_ref[...], nv_ref[...] = _adamw_math(w_ref[...], g, m_ref[...], v_ref[...])

        @pl.when(pl.program_id(0) == 0)
        def _():
            for t in range(n):
                sw_ref, sg_ref, sm_ref, sv_ref = small_in[4 * t:4 * t + 4]
                sg = sg_ref[0:sw_ref.shape[0], :]
                small_out[4 * t][...] = sg
                small_out[4 * t + 1][...], small_out[4 * t + 2][...], small_out[4 * t + 3][...] = _adamw_math(
                    sw_ref[...], sg, sm_ref[...], sv_ref[...])

    spec = pl.BlockSpec((block_rows, cols), lambda i: (i, 0))
    vm = pl.BlockSpec(memory_space=pltpu.VMEM)
    outs = pl.pallas_call(
        body,
        name="adamw",
        grid=(rows // block_rows,),
        in_specs=[spec] * 4 + [vm] * (4 * n),
        out_specs=[spec] * 4 + [vm] * (4 * n),
        out_shape=[jax.ShapeDtypeStruct(big[0].shape, F32)] * 4
        + [jax.ShapeDtypeStruct(grp[0].shape, F32) for grp in groups for _ in range(4)],
        compiler_params=_cparams(dimension_semantics=("arbitrary",)),
    )(*big, *[a for grp in groups for a in grp])
    return tuple(outs[0:4]), [tuple(outs[4 + 4 * t:8 + 4 * t]) for t in range(n)]


def _rope_tables():
    half = QK_ROPE // 2
    f32 = np.float32
    inv_freq = (f32(1.0) / (f32(ROPE_THETA) ** (np.arange(half, dtype=f32) / f32(half)))).astype(f32)
    pos = np.arange(N, dtype=f32) - f32(PAD)
    ang = (pos[:, None] * inv_freq[None, :]).astype(f32)
    cos, sin = np.cos(ang).astype(f32), np.sin(ang).astype(f32)
    zero = np.zeros((N, 128 - QK_ROPE), f32)
    return jnp.asarray(np.concatenate([cos, cos, zero], axis=1)), jnp.asarray(np.concatenate([-sin, sin, zero], axis=1))


def kernel(x, meta_tokens, norm_g, w_in, q_norm_g, w_q_b, kv_norm_g, w_kv_b, pool_w, pool_scale, w_out, final_norm_g, loss_target, m_meta_tokens, m_norm_g, m_w_in, m_q_norm_g, m_w_q_b, m_kv_norm_g, m_w_kv_b, m_pool_w, m_pool_scale, m_w_out, m_final_norm_g, v_meta_tokens, v_norm_g, v_w_in, v_q_norm_g, v_w_q_b, v_kv_norm_g, v_w_kv_b, v_pool_w, v_pool_scale, v_w_out, v_final_norm_g):
    tr = lambda a: a[0].T
    win, wq, wkv, h, tgt = _gather_weights(tr(w_in), tr(w_q_b), w_kv_b[0], meta_tokens, x[0], loss_target[0])
    cosf, sinf = _rope_tables()
    gf = final_norm_g.reshape(1, D)

    part = _local_step(h, tgt, norm_g, win, q_norm_g, wq, kv_norm_g, wkv, pool_w[0], pool_scale, w_out[0], m_w_out[0],
                       v_w_out[0], gf, cosf, sinf)

    pw2 = lambda a: a.reshape(len(POOL_WINDOWS) * POOL_GROUP, POOL_GROUP)
    gpw = part["gpw"]
    gwinT, gwqT, gwkv, gmeta, gg, ggf, ggq, ggkv, gps, gloss = _reduce_grads(
        part["dsl"], part["hn"], part["dwq"], part["dwkv"], part["dmeta"], part["dg"],
        part["dgf"], part["dgq"], part["dgkv"], part["dps"], part["loss"])

    r_out = part["r_out"]
    fn2 = lambda a: a.reshape(1, D)
    r_in, (r_meta, r_norm, r_gq, r_wq, r_gkv, r_wkv, r_pw, r_ps, r_fn) = _adamw((tr(w_in), gwinT, tr(m_w_in), tr(v_w_in)), 248, [
        (meta_tokens, gmeta, m_meta_tokens, v_meta_tokens),
        (norm_g, gg, m_norm_g, v_norm_g),
        (q_norm_g, ggq, m_q_norm_g, v_q_norm_g),
        (tr(w_q_b), gwqT, tr(m_w_q_b), tr(v_w_q_b)),
        (kv_norm_g, ggkv, m_kv_norm_g, v_kv_norm_g),
        (w_kv_b[0], gwkv, m_w_kv_b[0], v_w_kv_b[0]),
        (pw2(pool_w), gpw, pw2(m_pool_w), pw2(v_pool_w)),
        (pool_scale, gps, m_pool_scale, v_pool_scale),
        (fn2(final_norm_g), ggf, fn2(m_final_norm_g), fn2(v_final_norm_g)),
    ])
    untr = lambda a: a.T[None]
    pw4 = lambda a: a.reshape(1, len(POOL_WINDOWS), POOL_GROUP, POOL_GROUP)
    per_kind = [[
        r_meta[kind], r_norm[kind], untr(r_in[kind]), r_gq[kind], untr(r_wq[kind]), r_gkv[kind], r_wkv[kind][None],
        pw4(r_pw[kind]), r_ps[kind], r_out[kind][None], r_fn[kind].reshape(D),
    ] for kind in range(4)]
    return (gloss[0, 0], part["gx"][None], *per_kind[0], *per_kind[1], *per_kind[2], *per_kind[3])
```

```python
import jax
import jax.numpy as jnp
import numpy as np
from jax import lax
from jax.experimental import pallas as pl
from jax.experimental.pallas import tpu as pltpu

F32 = jnp.float32
BF16 = jnp.bfloat16

D = 1024
S = 2048
N_META = 16
PAD = 112
HEAD_ROWS = PAD + N_META
N = HEAD_ROWS + S
D_POOL = 512
POOL_WINDOWS = (2, 4, 8, 16)
POOL_GROUP = 128
HALO = 16
HEADS = 4
QK_NOPE = 128
QK_ROPE = 64
QK = QK_NOPE + QK_ROPE
V_HEAD = 128
Q_LORA = 256
KV_LORA = 128
D_IN = 1984
EPS = 1e-6
ROPE_THETA = 10000.0
SCALE = QK ** -0.5
CHIPS = 4

ROWS_FWD = 544
ROWS_MID = 544
ROWS_BWD = 544
TK = 128
TQ = 256
NQ = S // TQ
HEADS_PER_STEP_BWD = 2

O_PI, O_PG, O_CQ, O_CKV, O_KR, O_AG = 0, 512, 1024, 1280, 1408, 1472
O_KR_END = O_KR + 128
SHARD_IN = D_IN // CHIPS
SHARD_PAD = 512
SHARD_OUT = D // CHIPS

LR, B1, B2, ADAM_EPS, WD, STEP = 0.001, 0.9, 0.999, 1e-08, 0.01, 10
C1 = 1.0 - B1**STEP
C2 = 1.0 - B2**STEP

VMEM_LIMIT = 60 * 1024 * 1024
MESH = pl.DeviceIdType.MESH
NEG = -1e30

VEC_ROWS = 8
PW_ROWS = len(POOL_WINDOWS) * POOL_GROUP
V_GQ, V_GKV, V_PS, V_LOSS = 0, 256, 384, 896


def _cparams(**kw):
    return pltpu.CompilerParams(vmem_limit_bytes=VMEM_LIMIT, **kw)


def _nt(a, b):
    return lax.dot_general(a, b, (((1,), (1,)), ((), ())), preferred_element_type=F32)


def _tn(a, b):
    return lax.dot_general(a, b, (((0,), (0,)), ((), ())), preferred_element_type=F32)


def _nn(a, b):
    return jnp.dot(a, b, preferred_element_type=F32)


def _swap64(t):
    return pltpu.roll(t, 32, 1) + pltpu.roll(t, 96, 1)


def _sigmoid(x):
    return 1.0 / (1.0 + jnp.exp(-x))


def _low_lanes():
    return (lax.broadcasted_iota(jnp.int32, (1, 128), 1) < QK_ROPE).astype(F32)


def _rows(w, rows):
    return pl.BlockSpec((rows, w), lambda i: (i, 0))


def _const(*shape):
    return pl.BlockSpec(shape, lambda *_: (0,) * len(shape), pipeline_mode=pl.Buffered(1))


STAT_GROUPS = HEADS // HEADS_PER_STEP_BWD


def _stat_slot(head):
    return head // HEADS_PER_STEP_BWD, head % HEADS_PER_STEP_BWD


N_PEERS = 4
SEND_ORDER = (3, 1, 2)


def _peer_signal(x, y, c):
    barrier = pltpu.get_barrier_semaphore()
    peers = [(x, y, 1 - c)] + [(x ^ fx, y ^ fy, c) for fx, fy in _CHIP_RELS[1:]]
    assert len(peers) == N_PEERS
    for peer in peers:
        pl.semaphore_signal(barrier, inc=1, device_id=peer, device_id_type=MESH)


def _peer_wait():
    pl.semaphore_wait(pltpu.get_barrier_semaphore(), N_PEERS)


def _attn_tiles():
    return [(0, TK, TK)] + [(TK + TQ * t, TQ, TK + TQ * (t + 1)) for t in range(NQ)]


def _masked_scores(q, k, rows, klen):
    s = _nt(q, k)
    col = lax.broadcasted_iota(jnp.int32, (1, TK), 1)
    head_bias = jnp.where(col >= PAD, 0.0, NEG)
    if klen == TK:
        return s + head_bias
    r = lax.broadcasted_iota(jnp.int32, (rows, 1), 0) >> 6
    c = lax.broadcasted_iota(jnp.int32, (1, rows), 1) >> 6
    diag_bias = jnp.where(c <= r, 0.0, NEG)
    parts = [s[:, 0:TK] + head_bias]
    if klen - rows > TK:
        parts.append(s[:, TK:klen - rows])
    parts.append(s[:, klen - rows:klen] + diag_bias)
    return jnp.concatenate(parts, axis=1)


def _fwd_in(h, norm_g, win, gq, wq, gkv, wkv, cosf, sinf):
    tr = ROWS_FWD

    def body(h_ref, g_ref, win_ref, gq_ref, wq_ref, gkv_ref, wkv_ref, cos_ref, sin_ref,
             pi_ref, pg_ref, cq_ref, ckv_ref, ag_ref, q_ref, k_ref, v_ref, hn_ref):
        h = h_ref[...]
        r = lax.rsqrt(jnp.mean(h * h, axis=-1, keepdims=True) + EPS)
        hn = ((h * r) * g_ref[...]).astype(BF16)
        hn_ref[...] = hn
        u = _nt(hn, win_ref[0:O_KR_END, :])
        pi_ref[...] = u[:, O_PI:O_PG]
        pg_ref[...] = u[:, O_PG:O_CQ]
        cq = u[:, O_CQ:O_CKV]
        ckv = u[:, O_CKV:O_KR]
        cq_ref[...] = cq
        ckv_ref[...] = ckv
        ag_ref[...] = _nt(hn, win_ref[O_AG:D_IN, :])
        cosv = cos_ref[...]
        sinv = sin_ref[...]
        kr = u[:, O_KR:O_KR_END] * _low_lanes()
        kr = (kr * cosv + _swap64(kr) * sinv).astype(BF16)
        rq = lax.rsqrt(jnp.mean(cq * cq, axis=-1, keepdims=True) + EPS)
        cqn = ((cq * rq) * gq_ref[...]).astype(BF16)
        rkv = lax.rsqrt(jnp.mean(ckv * ckv, axis=-1, keepdims=True) + EPS)
        ckvn = ((ckv * rkv) * gkv_ref[...]).astype(BF16)
        for hd in range(HEADS):
            qh = _nt(cqn, wq_ref[hd]) * SCALE
            z = qh[:, QK_NOPE:]
            q_ref[hd, :, 0:QK_NOPE] = qh[:, 0:QK_NOPE].astype(BF16)
            q_ref[hd, :, QK_NOPE:] = (z * cosv + _swap64(z) * sinv).astype(BF16)
            kvh = _nn(ckvn, wkv_ref[hd])
            k_ref[hd, :, 0:QK_NOPE] = kvh[:, 0:QK_NOPE].astype(BF16)
            k_ref[hd, :, QK_NOPE:] = kr
            v_ref[hd] = kvh[:, QK_NOPE:].astype(BF16)

    head = lambda w: pl.BlockSpec((HEADS, tr, w), lambda i: (0, i, 0))
    return pl.pallas_call(
        body,
        name="fwd_in",
        grid=(N // tr,),
        in_specs=[
            _rows(D, tr), _const(1, D), _const(D_IN, D), _const(1, Q_LORA), _const(HEADS, 256, Q_LORA),
            _const(1, KV_LORA), _const(HEADS, KV_LORA, 256), _rows(128, tr), _rows(128, tr),
        ],
        out_specs=[_rows(D_POOL, tr), _rows(D_POOL, tr), _rows(Q_LORA, tr), _rows(KV_LORA, tr), _rows(D_POOL, tr),
                   head(256), head(256), head(V_HEAD), _rows(D, tr)],
        out_shape=[
            jax.ShapeDtypeStruct((N, D_POOL), F32), jax.ShapeDtypeStruct((N, D_POOL), F32),
            jax.ShapeDtypeStruct((N, Q_LORA), F32), jax.ShapeDtypeStruct((N, KV_LORA), F32),
            jax.ShapeDtypeStruct((N, D_POOL), F32),
            jax.ShapeDtypeStruct((HEADS, N, 256), BF16), jax.ShapeDtypeStruct((HEADS, N, 256), BF16),
            jax.ShapeDtypeStruct((HEADS, N, V_HEAD), BF16), jax.ShapeDtypeStruct((N, D), BF16),
        ],
        compiler_params=_cparams(dimension_semantics=("arbitrary",)),
    )(h, norm_g, win, gq, wq, gkv, wkv, cosf, sinf)


def _attn_fwd(q, k, v, wout_s):
    tiles = _attn_tiles()
    n_t = len(tiles)
    half = SHARD_OUT // 2
    send_step = 2
    fwd_step = n_t - 2

    def body(q_hbm, k_hbm, v_hbm, wout_ref, o_hbm, lse_ref, wout_o, q_buf, k_buf, v_buf, o_buf, s_wout, in_sems, out_sems,
             ici_send, ici_recv, fwd_send, fwd_recv, own_sem):
        step = pl.program_id(0)
        x, y, c = lax.axis_index("x"), lax.axis_index("y"), lax.axis_index("c")
        me = 2 * x + y

        def chip_of(rel):
            fx, fy = _CHIP_RELS[rel]
            return 2 * (x ^ fx) + (y ^ fy)

        def place(chip, core):
            return wout_o.at[pl.ds(pl.multiple_of(SHARD_OUT * chip + half * core, half), half), :]

        def ici_copy(rel, src_chip, to):
            return _remote(s_wout.at[pl.ds(pl.multiple_of(half * c, half), half), :], place(src_chip, c),
                           ici_send.at[rel - 1], ici_recv.at[rel - 1], to)

        def fwd_copy(rel, core, to):
            spot = place(chip_of(rel), core)
            return _remote(spot, spot, fwd_send.at[rel - 1], fwd_recv.at[rel - 1], to)

        own = pltpu.make_async_copy(s_wout, wout_o.at[pl.ds(pl.multiple_of(SHARD_OUT * me, SHARD_OUT), SHARD_OUT), :], own_sem)

        @pl.when(step == 0)
        def _():
            _peer_signal(x, y, c)
            s_wout[...] = wout_ref[...].astype(BF16)
            own.start()

        @pl.when(step == send_step)
        def _():
            _peer_wait()
            for rel in SEND_ORDER:
                fx, fy = _CHIP_RELS[rel]
                ici_copy(rel, me, (x ^ fx, y ^ fy, c)).start()

        @pl.when(step == fwd_step)
        def _():
            for rel in (1, 2, 3):
                ici_copy(rel, chip_of(rel), (x, y, c)).wait_recv()
                fwd_copy(rel, c, (x, y, 1 - c)).start()

        def finish_wout():
            for rel in (1, 2, 3):
                fwd_copy(rel, 1 - c, (x, y, c)).wait_recv()
            for rel in (1, 2, 3):
                ici_copy(rel, me, (x, y, c)).wait_send()
                fwd_copy(rel, c, (x, y, c)).wait_send()
            own.wait()

        def loads(idx):
            q0, rows, _ = tiles[idx]
            rs = pl.ds(q0, rows)
            return [pltpu.make_async_copy(src.at[:, rs, :], dst.at[:, rs, :], in_sems.at[a, idx % 2])
                    for a, (src, dst) in enumerate(((q_hbm, q_buf), (k_hbm, k_buf), (v_hbm, v_buf)))]

        def store(idx):
            q0, rows, _ = tiles[idx]
            return pltpu.make_async_copy(o_buf.at[idx % 2, pl.ds(0, rows), :], o_hbm.at[pl.ds(q0, rows), :],
                                         out_sems.at[idx % 2])

        @pl.when(step == 0)
        def _():
            lse_ref[...] = jnp.zeros_like(lse_ref)
            for cp in loads(0):
                cp.start()

        for idx, (q0, rows, klen) in enumerate(tiles):
            @pl.when(step == idx)
            def _(idx=idx, q0=q0, rows=rows, klen=klen):
                for cp in loads(idx):
                    cp.wait()
                if idx + 1 < n_t:
                    for cp in loads(idx + 1):
                        cp.start()
                if idx >= 2:
                    store(idx - 2).wait()
                for hd in range(HEADS):
                    s = _masked_scores(q_buf[hd, q0:q0 + rows, :], k_buf[hd, 0:klen, :], rows, klen)
                    m = jnp.max(s, axis=-1, keepdims=True)
                    p = jnp.exp(s - m)
                    l = jnp.sum(p, axis=-1, keepdims=True)
                    o_buf[idx % 2, 0:rows, hd * V_HEAD:(hd + 1) * V_HEAD] = _nn(p.astype(BF16), v_buf[hd, 0:klen, :]) / l
                    grp, lane = _stat_slot(hd)
                    lse_ref[grp, q0:q0 + rows, lane:lane + 1] = m + jnp.log(l)
                store(idx).start()
                if idx == n_t - 1:
                    store(idx - 1).wait()
                    store(idx).wait()
                    finish_wout()

    hbm = pl.BlockSpec(memory_space=pl.ANY)
    return pl.pallas_call(
        body,
        name="attn_fwd",
        grid=(n_t,),
        in_specs=[hbm, hbm, hbm, _const(SHARD_OUT, D)],
        out_specs=[hbm, _const(STAT_GROUPS, N, 128), hbm],
        out_shape=[jax.ShapeDtypeStruct((N, HEADS * V_HEAD), F32), jax.ShapeDtypeStruct((STAT_GROUPS, N, 128), F32),
                   jax.ShapeDtypeStruct((D, D), BF16)],
        scratch_shapes=[pltpu.VMEM((HEADS, N, 256), BF16), pltpu.VMEM((HEADS, N, 256), BF16),
                        pltpu.VMEM((HEADS, N, V_HEAD), BF16), pltpu.VMEM((2, TQ, HEADS * V_HEAD), F32),
                        pltpu.VMEM((SHARD_OUT, D), BF16),
                        pltpu.SemaphoreType.DMA((3, 2)), pltpu.SemaphoreType.DMA((2,))]
        + [pltpu.SemaphoreType.DMA((3,))] * 4 + [pltpu.SemaphoreType.DMA],
        compiler_params=_cparams(dimension_semantics=("arbitrary",), collective_id=1),
    )(q, k, v, wout_s)


def _inv_count(row0, rows, w):
    row = row0 + lax.broadcasted_iota(jnp.int32, (rows, 1), 0)
    return 1.0 / jnp.clip(row - (PAD - 1), 1, w).astype(F32)


def _mid(h, tgt, pool_in, pool_gate, attn_gate, attn, pool_w, pool_scale, wout, gf):
    tr = ROWS_MID
    per = tr // HALO
    ng = len(POOL_WINDOWS)

    def body(h_ref, t_ref, pin_ref, halo_ref, pg_ref, ag_ref, at_ref, pw_ref, ps_ref, wout_ref, gf_ref,
             dh2_ref, do_ref, delta_ref, dag_ref, dpg_ref, dpl_ref, dwout_ref, dpw_ref, dps_ref, dgf_ref, loss_ref):
        i = pl.program_id(0)

        @pl.when(i == 0)
        def _():
            dwout_ref[...] = jnp.zeros_like(dwout_ref)
            dpw_ref[...] = jnp.zeros_like(dpw_ref)
            dps_ref[...] = jnp.zeros_like(dps_ref)
            dgf_ref[...] = jnp.zeros_like(dgf_ref)
            loss_ref[...] = jnp.zeros_like(loss_ref)

        row0 = i * tr
        real = (row0 + lax.broadcasted_iota(jnp.int32, (tr, 1), 0)) >= HEAD_ROWS
        h = h_ref[...]

        halo = jnp.where(i > 0, halo_ref[...], 0.0)
        ext = jnp.concatenate([halo, pin_ref[...]], axis=0)
        pooled = []
        for g, w in enumerate(POOL_WINDOWS):
            e = ext[:, g * POOL_GROUP:(g + 1) * POOL_GROUP]
            acc = e
            shift = 1
            while shift < w:
                acc = acc + pltpu.roll(acc, shift, 0)
                shift *= 2
            pooled.append((acc[HALO:] * _inv_count(row0, tr, w) - e[HALO:]).astype(BF16))
        pw = [pw_ref[g].astype(BF16) for g in range(ng)]
        mixed = jnp.concatenate([_nn(pooled[g], pw[g]) for g in range(ng)], axis=1)
        ps = ps_ref[...]
        mixed_s = mixed * ps
        pg = pg_ref[...]
        sig_p = _sigmoid(pg)
        silu_p = pg * sig_p
        pool_out = (silu_p * mixed_s).astype(BF16)
        ag = ag_ref[...]
        sig_a = _sigmoid(ag)
        silu_a = ag * sig_a
        at = at_ref[...]
        attn_out = (silu_a * at).astype(BF16)
        cat = jnp.concatenate([pool_out, attn_out], axis=1)
        h2 = h + _nn(cat, wout_ref[...])

        r2 = lax.rsqrt(jnp.mean(h2 * h2, axis=-1, keepdims=True) + EPS)
        n2 = h2 * r2
        gfv = gf_ref[...]
        err = jnp.where(real, n2 * gfv - t_ref[...], 0.0)
        loss_ref[...] += jnp.sum(jnp.sum(err * err, axis=-1, keepdims=True), axis=0, keepdims=True) * (0.5 / D)
        dy = err * (1.0 / D)
        dgf_ref[...] += jnp.sum(dy * n2, axis=0, keepdims=True)
        dn = dy * gfv
        dh2 = r2 * (dn - n2 * jnp.mean(dn * n2, axis=-1, keepdims=True))
        dh2_ref[...] = dh2
        dh2b = dh2.astype(BF16)

        dwout_ref[...] += _tn(cat, dh2b)
        dcat = _nt(dh2b, wout_ref[...])
        dpo = dcat[:, 0:D_POOL]
        dao = dcat[:, D_POOL:D]
        do = dao * silu_a
        prod = do * at
        delta_ref[...] = jnp.zeros_like(delta_ref)
        for hd in range(HEADS):
            grp, lane = _stat_slot(hd)
            cols = slice(hd * V_HEAD, (hd + 1) * V_HEAD)
            do_ref[grp, :, lane * V_HEAD:(lane + 1) * V_HEAD] = do[:, cols].astype(BF16)
            delta_ref[grp, :, lane:lane + 1] = jnp.sum(prod[:, cols], axis=-1, keepdims=True)
        dag_ref[...] = (dao * at * (sig_a * (1.0 + ag * (1.0 - sig_a)))).astype(BF16)
        dmixed_s = dpo * silu_p
        dpg_ref[...] = (dpo * mixed_s * (sig_p * (1.0 + pg * (1.0 - sig_p)))).astype(BF16)
        dps_ref[...] += jnp.sum(dmixed_s * mixed, axis=0, keepdims=True)
        dmixed = (dmixed_s * ps).astype(BF16)
        dpl = []
        for g in range(ng):
            dm = dmixed[:, g * POOL_GROUP:(g + 1) * POOL_GROUP]
            dpl.append(_nt(dm, pw[g]))
            dpw_ref[g] += _tn(pooled[g], dm)
        dpl_ref[...] = jnp.concatenate(dpl, axis=1)

    halo_spec = pl.BlockSpec((HALO, D_POOL), lambda i: (jnp.maximum(i * per - 1, 0), 0))
    return pl.pallas_call(
        body,
        name="mid",
        grid=(N // tr,),
        in_specs=[
            _rows(D, tr), _rows(D, tr), _rows(D_POOL, tr), halo_spec, _rows(D_POOL, tr), _rows(D_POOL, tr),
            _rows(D_POOL, tr), _const(ng, POOL_GROUP, POOL_GROUP), _const(1, D_POOL), _const(D, D), _const(1, D),
        ],
        out_specs=[
            _rows(D, tr), pl.BlockSpec((STAT_GROUPS, tr, HEADS_PER_STEP_BWD * V_HEAD), lambda i: (0, i, 0)),
            pl.BlockSpec((STAT_GROUPS, tr, 128), lambda i: (0, i, 0)),
            _rows(D_POOL, tr), _rows(D_POOL, tr), _rows(D_POOL, tr),
            _const(D, D), _const(ng, POOL_GROUP, POOL_GROUP), _const(1, D_POOL), _const(1, D), _const(1, 128),
        ],
        out_shape=[
            jax.ShapeDtypeStruct((N, D), F32), jax.ShapeDtypeStruct((STAT_GROUPS, N, HEADS_PER_STEP_BWD * V_HEAD), BF16),
            jax.ShapeDtypeStruct((STAT_GROUPS, N, 128), F32),
            jax.ShapeDtypeStruct((N, D_POOL), BF16), jax.ShapeDtypeStruct((N, D_POOL), BF16),
            jax.ShapeDtypeStruct((N, D_POOL), F32), jax.ShapeDtypeStruct((D, D), F32),
            jax.ShapeDtypeStruct((ng, POOL_GROUP, POOL_GROUP), F32),
            jax.ShapeDtypeStruct((1, D_POOL), F32), jax.ShapeDtypeStruct((1, D), F32), jax.ShapeDtypeStruct((1, 128), F32),
        ],
        compiler_params=_cparams(dimension_semantics=("arbitrary",)),
    )(h, tgt, pool_in, pool_in, pool_gate, attn_gate, attn, pool_w, pool_scale, wout, gf)


def _unrope(dy, cosv, sinv):
    return dy * cosv + _swap64(dy * sinv) * _low_lanes()


def _attn_bwd(q, k, v, do, lse, delta, cosf, sinf, dwout, dpw):
    tiles = _attn_tiles()
    hp = HEADS_PER_STEP_BWD
    n_g = HEADS // hp
    n_t = len(tiles)
    half = SHARD_OUT // 2
    half_pw = PW_ROWS // 2
    swap_at, send_at, sum_at = (0, 3), (0, 5), (n_g - 1, n_t // 2)

    def body(q_hbm, k_hbm, v_hbm, do_hbm, lse_ref, delta_ref, cos_ref, sin_ref, dwout_hbm, dpw_ref, dq_hbm, dkv_ref, dkr_ref,
             gwout_ref, gpw_ref, q_buf, k_buf, v_buf, do_buf, dq_buf, dk_acc, dv_acc, own_w, sib_w, stage_w, recv_w, gw_buf,
             pw_sib, pw_chip, pw_recv, pw_buf,
             in_sems, out_sems, ow_sems, d2d_send, d2d_recv, ici_send, ici_recv, fin_send, fin_recv,
             pw_swap_send, pw_swap_recv, pw_ici_send, pw_ici_recv, pw_fin_send, pw_fin_recv):
        grp = pl.program_id(0)
        step = pl.program_id(1)
        heads = pl.ds(grp * hp, hp)
        x, y, c = lax.axis_index("x"), lax.axis_index("y"), lax.axis_index("c")
        me = 2 * x + y
        sibling = (x, y, 1 - c)

        def pw_rows(core):
            return pl.ds(pl.multiple_of(half_pw * core, half_pw), half_pw)

        def pw_swap():
            return _remote(dpw_ref, pw_sib, pw_swap_send.at[0], pw_swap_recv.at[0], sibling)

        def pw_ici(rel):
            fx, fy = _CHIP_RELS[rel]
            return _remote(pw_chip.at[pw_rows(c), :], pw_recv.at[rel - 1], pw_ici_send.at[rel - 1], pw_ici_recv.at[rel - 1],
                           (x ^ fx, y ^ fy, c))

        def pw_fin(core):
            spot = pw_buf.at[pw_rows(core), :]
            return _remote(spot, spot, pw_fin_send.at[0], pw_fin_recv.at[0], sibling)

        def chip_of(rel):
            fx, fy = _CHIP_RELS[rel]
            return 2 * (x ^ fx) + (y ^ fy)

        def piece(chip, core):
            return dwout_hbm.at[pl.ds(pl.multiple_of(SHARD_OUT * chip + half * core, half), half), :]

        def own_load(rel):
            return pltpu.make_async_copy(piece(chip_of(rel), c), own_w.at[rel], ow_sems.at[rel])

        def d2d_copy(rel):
            return _remote(piece(chip_of(rel), 1 - c), sib_w.at[rel], d2d_send.at[rel], d2d_recv.at[rel], sibling)

        def ici_copy(rel):
            fx, fy = _CHIP_RELS[rel]
            return _remote(stage_w.at[rel - 1], recv_w.at[rel - 1], ici_send.at[rel - 1], ici_recv.at[rel - 1],
                           (x ^ fx, y ^ fy, c))

        def fin_copy(core):
            spot = gw_buf.at[pl.ds(pl.multiple_of(half * core, half), half), :]
            return _remote(spot, spot, fin_send.at[0], fin_recv.at[0], sibling)

        @pl.when((grp == 0) & (step == 0))
        def _():
            _peer_signal(x, y, c)
            for rel in SEND_ORDER + (0,):
                own_load(rel).start()

        @pl.when((grp == swap_at[0]) & (step == swap_at[1]))
        def _():
            _peer_wait()
            pw_swap().start()
            for rel in SEND_ORDER + (0,):
                d2d_copy(rel).start()

        @pl.when((grp == send_at[0]) & (step == send_at[1]))
        def _():
            for rel in SEND_ORDER:
                own_load(rel).wait()
                d2d_copy(rel).wait_recv()
                stage_w[rel - 1] = (own_w[rel] + sib_w[rel]).astype(BF16)
                ici_copy(rel).start()
            pw_swap().wait_recv()
            pw_chip[...] = dpw_ref[...] + pw_sib[...]
            for rel in SEND_ORDER:
                pw_ici(rel).start()

        @pl.when((grp == sum_at[0]) & (step == sum_at[1]))
        def _():
            own_load(0).wait()
            d2d_copy(0).wait_recv()
            total = own_w[0] + sib_w[0]
            for rel in (1, 2, 3):
                ici_copy(rel).wait_recv()
                total = total + recv_w[rel - 1].astype(F32)
            gw_buf[pl.ds(pl.multiple_of(half * c, half), half), :] = total
            fin_copy(c).start()
            for rel in (1, 2, 3):
                pw_ici(rel).wait_recv()
            total = jnp.zeros((half_pw, POOL_GROUP), F32)
            for chip in range(CHIPS):
                flips = chip ^ me
                rel = jnp.where(flips == 2, 1, jnp.where(flips == 1, 2, flips))
                total = total + jnp.where(rel == 0, pw_chip[pw_rows(c), :], pw_recv[jnp.maximum(rel - 1, 0)])
            pw_buf[pw_rows(c), :] = total
            pw_fin(c).start()

        def finish_dwout():
            fin_copy(1 - c).wait_recv()
            pw_fin(1 - c).wait_recv()
            for rel in (0, 1, 2, 3):
                d2d_copy(rel).wait_send()
            for rel in (1, 2, 3):
                ici_copy(rel).wait_send()
                pw_ici(rel).wait_send()
            fin_copy(c).wait_send()
            pw_swap().wait_send()
            pw_fin(c).wait_send()
            gwout_ref[...] = gw_buf[...]
            gpw_ref[...] = pw_buf[...]

        def loads(g, idx):
            q0, rows, _ = tiles[idx]
            rs = pl.ds(q0, rows)
            par = (g * n_t + idx) % 2
            hs = pl.ds(g * hp, hp)
            pairs = ((q_hbm.at[hs, rs, :], q_buf.at[:, rs, :]), (k_hbm.at[hs, rs, :], k_buf.at[:, rs, :]),
                     (v_hbm.at[hs, rs, :], v_buf.at[:, rs, :]), (do_hbm.at[g, rs, :], do_buf.at[rs, :]))
            return [pltpu.make_async_copy(src, dst, in_sems.at[a, par]) for a, (src, dst) in enumerate(pairs)]

        def store(idx):
            q0, rows, _ = tiles[idx]
            return pltpu.make_async_copy(dq_buf.at[idx % 2, :, pl.ds(0, rows), :], dq_hbm.at[heads, pl.ds(q0, rows), :],
                                         out_sems.at[idx % 2])

        @pl.when(step == 0)
        def _():
            dk_acc[...] = jnp.zeros_like(dk_acc)
            dv_acc[...] = jnp.zeros_like(dv_acc)

        @pl.when((step == 0) & (grp == 0))
        def _():
            dkr_ref[...] = jnp.zeros_like(dkr_ref)
            for cp in loads(grp, 0):
                cp.start()

        for idx, (q0, rows, klen) in enumerate(tiles):
            @pl.when(step == idx)
            def _(idx=idx, q0=q0, rows=rows, klen=klen):
                for cp in loads(grp, idx):
                    cp.wait()
                if idx + 1 < n_t:
                    for cp in loads(grp, idx + 1):
                        cp.start()
                if idx >= 2:
                    store(idx - 2).wait()
                qs = pl.ds(q0, rows)
                for hd in range(hp):
                    qv = q_buf[hd, qs, :]
                    kv = k_buf[hd, 0:klen, :]
                    p = jnp.exp(_masked_scores(qv, kv, rows, klen) - lse_ref[0, qs, hd:hd + 1])
                    dob = do_buf[qs, hd * V_HEAD:(hd + 1) * V_HEAD]
                    ds = (p * (_nt(dob, v_buf[hd, 0:klen, :]) - delta_ref[0, qs, hd:hd + 1])).astype(BF16)
                    dq = _nn(ds, kv) * SCALE
                    dq_buf[idx % 2, hd, 0:rows, 0:QK_NOPE] = dq[:, 0:QK_NOPE].astype(BF16)
                    dq_buf[idx % 2, hd, 0:rows, QK_NOPE:] = _unrope(dq[:, QK_NOPE:], cos_ref[qs, :], sin_ref[qs, :]).astype(BF16)
                    dk_acc[hd, 0:klen, :] += _tn(ds, qv)
                    dv_acc[hd, 0:klen, :] += _tn(p.astype(BF16), dob)
                store(idx).start()

        @pl.when(step == n_t - 1)
        def _():
            @pl.when(grp + 1 < n_g)
            def _():
                for cp in loads(grp + 1, 0):
                    cp.start()

            for hd in range(hp):
                dkv_ref[hd, :, 0:QK_NOPE] = dk_acc[hd, :, 0:QK_NOPE].astype(BF16)
                dkv_ref[hd, :, QK_NOPE:] = dv_acc[hd].astype(BF16)
                dkr_ref[...] += dk_acc[hd, :, QK_NOPE:]
            store(n_t - 2).wait()
            store(n_t - 1).wait()

            @pl.when(grp == n_g - 1)
            def _():
                finish_dwout()

    hbm = pl.BlockSpec(memory_space=pl.ANY)
    stat = pl.BlockSpec((1, N, 128), lambda g, t: (g, 0, 0), pipeline_mode=pl.Buffered(1))
    piece_f32 = lambda lead: pltpu.VMEM((lead, half, D), F32)
    piece_bf16 = lambda lead: pltpu.VMEM((lead, half, D), BF16)
    return pl.pallas_call(
        body,
        name="attn_bwd",
        grid=(n_g, n_t),
        in_specs=[hbm, hbm, hbm, hbm, stat, stat, _const(N, 128), _const(N, 128), hbm, _const(PW_ROWS, POOL_GROUP)],
        out_specs=[hbm, pl.BlockSpec((hp, N, 256), lambda g, t: (g, 0, 0), pipeline_mode=pl.Buffered(1)), _const(N, 128),
                   _const(SHARD_OUT, D), _const(PW_ROWS, POOL_GROUP)],
        out_shape=[
            jax.ShapeDtypeStruct((HEADS, N, 256), BF16), jax.ShapeDtypeStruct((HEADS, N, 256), BF16),
            jax.ShapeDtypeStruct((N, 128), F32), jax.ShapeDtypeStruct((SHARD_OUT, D), F32),
            jax.ShapeDtypeStruct((PW_ROWS, POOL_GROUP), F32),
        ],
        scratch_shapes=[pltpu.VMEM((hp, N, 256), BF16), pltpu.VMEM((hp, N, 256), BF16), pltpu.VMEM((hp, N, V_HEAD), BF16),
                        pltpu.VMEM((N, hp * V_HEAD), BF16), pltpu.VMEM((2, hp, TQ, 256), BF16),
                        pltpu.VMEM((hp, N, 256), F32), pltpu.VMEM((hp, N, V_HEAD), F32),
                        piece_f32(CHIPS), piece_f32(CHIPS), piece_bf16(3), piece_bf16(3), pltpu.VMEM((SHARD_OUT, D), F32),
                        pltpu.VMEM((PW_ROWS, POOL_GROUP), F32), pltpu.VMEM((PW_ROWS, POOL_GROUP), F32),
                        pltpu.VMEM((3, half_pw, POOL_GROUP), F32), pltpu.VMEM((PW_ROWS, POOL_GROUP), F32),
                        pltpu.SemaphoreType.DMA((4, 2)), pltpu.SemaphoreType.DMA((2,)), pltpu.SemaphoreType.DMA((CHIPS,)),
                        pltpu.SemaphoreType.DMA((CHIPS,)), pltpu.SemaphoreType.DMA((CHIPS,)),
                        pltpu.SemaphoreType.DMA((3,)), pltpu.SemaphoreType.DMA((3,)),
                        pltpu.SemaphoreType.DMA((1,)), pltpu.SemaphoreType.DMA((1,)),
                        pltpu.SemaphoreType.DMA((1,)), pltpu.SemaphoreType.DMA((1,)),
                        pltpu.SemaphoreType.DMA((3,)), pltpu.SemaphoreType.DMA((3,)),
                        pltpu.SemaphoreType.DMA((1,)), pltpu.SemaphoreType.DMA((1,))],
        compiler_params=_cparams(dimension_semantics=("arbitrary", "arbitrary"), collective_id=2),
    )(q, k, v, do, lse, delta, cosf, sinf, dwout, dpw)


def _bwd_in(h, dh2, dq, dkv, dkr, cq, ckv, dpl, dpg, dag, norm_g, win, gq, wq, gkv, wkv, cosf, sinf, adam_out):
    tr = ROWS_BWD
    nb = N // tr
    per = tr // HALO
    lead = HEAD_ROWS
    adam_rows = SHARD_OUT // nb

    def body(h_ref, dh2_ref, dq_ref, dkv_ref, dkr_ref, cq_ref, ckv_ref, dpl_ref, halo_ref, dpg_ref, dag_ref,
             g_ref, win_ref, gq_ref, wq_ref, gkv_ref, wkv_ref, cos_ref, sin_ref, aw_ref, ag_ref, am_ref, av_ref,
             gx_ref, dmeta_ref, dsl_ref, dwq_ref, dwkv_ref, dg_ref, dgq_ref, dgkv_ref, ago_ref, ad_ref, anm_ref, anv_ref,
             dh_buf, gx_sem):
        i = pl.program_id(0)
        grad_out = ag_ref[...]
        ago_ref[...] = grad_out
        ad_ref[...], anm_ref[...], anv_ref[...] = _adamw_math(aw_ref[...], grad_out, am_ref[...], av_ref[...])

        @pl.when(i == 0)
        def _():
            dwq_ref[...] = jnp.zeros_like(dwq_ref)
            dwkv_ref[...] = jnp.zeros_like(dwkv_ref)
            dg_ref[...] = jnp.zeros_like(dg_ref)
            dgq_ref[...] = jnp.zeros_like(dgq_ref)
            dgkv_ref[...] = jnp.zeros_like(dgkv_ref)

        row0 = i * tr
        h = h_ref[...]
        r = lax.rsqrt(jnp.mean(h * h, axis=-1, keepdims=True) + EPS)
        n = h * r
        gv = g_ref[...]
        cq = cq_ref[...]
        rq = lax.rsqrt(jnp.mean(cq * cq, axis=-1, keepdims=True) + EPS)
        nq = cq * rq
        gqv = gq_ref[...]
        cqn = (nq * gqv).astype(BF16)
        dcqn = jnp.zeros((tr, Q_LORA), F32)
        for hd in range(HEADS):
            dqf = dq_ref[hd]
            dcqn = dcqn + _nn(dqf, wq_ref[hd])
            dwq_ref[hd] += _tn(dqf, cqn)
        dgq_ref[...] += jnp.sum(dcqn * nq, axis=0, keepdims=True)
        dnq = dcqn * gqv
        dcq = rq * (dnq - nq * jnp.mean(dnq * nq, axis=-1, keepdims=True))

        ckv = ckv_ref[...]
        rkv = lax.rsqrt(jnp.mean(ckv * ckv, axis=-1, keepdims=True) + EPS)
        nkv = ckv * rkv
        gkvv = gkv_ref[...]
        ckvn = (nkv * gkvv).astype(BF16)
        dckvn = jnp.zeros((tr, KV_LORA), F32)
        for hd in range(HEADS):
            dkv = dkv_ref[hd]
            dckvn = dckvn + _nt(dkv, wkv_ref[hd])
            dwkv_ref[hd] += _tn(ckvn, dkv)
        dgkv_ref[...] += jnp.sum(dckvn * nkv, axis=0, keepdims=True)
        dnkv = dckvn * gkvv
        dckv = rkv * (dnkv - nkv * jnp.mean(dnkv * nkv, axis=-1, keepdims=True))
        dkr = _unrope(dkr_ref[...], cos_ref[...], sin_ref[...])

        cur = dpl_ref[...]
        halo = jnp.where(i < nb - 1, halo_ref[...], 0.0)
        dpi = []
        for g, w in enumerate(POOL_WINDOWS):
            sl = slice(g * POOL_GROUP, (g + 1) * POOL_GROUP)
            a = jnp.concatenate([cur[:, sl] * _inv_count(row0, tr, w), halo[:, sl] * _inv_count(row0 + tr, HALO, w)], axis=0)
            acc = a
            shift = 1
            while shift < w:
                acc = acc + pltpu.roll(acc, tr + HALO - shift, 0)
                shift *= 2
            dpi.append(acc[0:tr] - cur[:, sl])

        du = jnp.concatenate([t.astype(BF16) for t in dpi] + [dpg_ref[...]] + [t.astype(BF16) for t in (dcq, dckv, dkr)],
                             axis=1)
        dagb = dag_ref[...]
        by_row = jnp.concatenate(dpi + [dpg_ref[...].astype(F32), dcq, dckv, dkr[:, 0:QK_ROPE], dagb.astype(F32),
                                        jnp.zeros((tr, SHARD_PAD - SHARD_IN), F32)], axis=1)
        for chip in range(CHIPS):
            dsl_ref[chip] = by_row[:, SHARD_IN * chip:SHARD_IN * chip + SHARD_PAD].astype(BF16)
        dhn = _nn(du, win_ref[0:O_KR_END, :]) + _nn(dagb, win_ref[O_AG:D_IN, :])
        dg_ref[...] += jnp.sum(dhn * n, axis=0, keepdims=True)
        dn = dhn * gv
        dh = dh2_ref[...] + r * (dn - n * jnp.mean(dn * n, axis=-1, keepdims=True))

        first = pltpu.make_async_copy(dh_buf.at[pl.ds(lead, tr - lead), :], gx_ref.at[pl.ds(0, tr - lead), :], gx_sem)
        later = lambda step: pltpu.make_async_copy(
            dh_buf, gx_ref.at[pl.ds(pl.multiple_of(step * tr - lead, 16), tr), :], gx_sem)

        @pl.when(i == 1)
        def _():
            first.wait()

        @pl.when(i > 1)
        def _():
            later(i - 1).wait()

        dh_buf[...] = dh

        @pl.when(i == 0)
        def _():
            first.start()
            for chip in range(CHIPS):
                dmeta_ref[chip] = dh[PAD:HEAD_ROWS, chip * 256:(chip + 1) * 256]

        @pl.when(i > 0)
        def _():
            later(i).start()

        @pl.when(i == nb - 1)
        def _():
            later(i).wait()

    head = lambda w: pl.BlockSpec((HEADS, tr, w), lambda i: (0, i, 0))
    halo_spec = pl.BlockSpec((HALO, D_POOL), lambda i: (jnp.minimum((i + 1) * per, N // HALO - 1), 0))
    return pl.pallas_call(
        body,
        name="bwd_in",
        grid=(nb,),
        in_specs=[
            _rows(D, tr), _rows(D, tr), head(256), head(256), _rows(128, tr), _rows(Q_LORA, tr), _rows(KV_LORA, tr),
            _rows(D_POOL, tr), halo_spec, _rows(D_POOL, tr), _rows(D_POOL, tr),
            _const(1, D), _const(D_IN, D), _const(1, Q_LORA), _const(HEADS, 256, Q_LORA),
            _const(1, KV_LORA), _const(HEADS, KV_LORA, 256), _rows(128, tr), _rows(128, tr),
        ] + [_rows(D, adam_rows)] * 4,
        out_specs=[
            pl.BlockSpec(memory_space=pl.ANY), _const(CHIPS, N_META, 256),
            pl.BlockSpec((CHIPS, tr, SHARD_PAD), lambda i: (0, i, 0)), _const(HEADS, 256, Q_LORA),
            _const(HEADS, KV_LORA, 256), _const(1, D), _const(1, Q_LORA), _const(1, KV_LORA),
        ] + [_rows(D, adam_rows)] * 4,
        out_shape=[
            jax.ShapeDtypeStruct((S, D), F32), jax.ShapeDtypeStruct((CHIPS, N_META, 256), F32),
            jax.ShapeDtypeStruct((CHIPS, N, SHARD_PAD), BF16), jax.ShapeDtypeStruct((HEADS, 256, Q_LORA), F32),
            jax.ShapeDtypeStruct((HEADS, KV_LORA, 256), F32),
            jax.ShapeDtypeStruct((1, D), F32), jax.ShapeDtypeStruct((1, Q_LORA), F32), jax.ShapeDtypeStruct((1, KV_LORA), F32),
        ] + [jax.ShapeDtypeStruct((SHARD_OUT, D), F32)] * 4,
        scratch_shapes=[pltpu.VMEM((tr, D), F32), pltpu.SemaphoreType.DMA],
        compiler_params=_cparams(dimension_semantics=("arbitrary",)),
    )(h, dh2, dq, dkv, dkr, cq, ckv, dpl, dpl, dpg, dag, norm_g, win, gq, wq, gkv, wkv, cosf, sinf, *adam_out)


def _local_step(h, tgt, norm_g, win, gq, wq, gkv, wkv, pool_w, pool_scale, wout_s, m_wout_s, v_wout_s, gf, cosf, sinf):
    pool_in, pool_gate, cq, ckv, attn_gate, q, k, v, hn = _fwd_in(h, norm_g, win, gq, wq, gkv, wkv, cosf, sinf)
    attn, lse, wout = _attn_fwd(q, k, v, wout_s)
    dh2, do, delta, dag, dpg, dpl, dwout, dpw, dps, dgf, loss = _mid(
        h, tgt, pool_in, pool_gate, attn_gate, attn, pool_w, pool_scale, wout, gf)
    dq, dkv, dkr, gwout, gpw = _attn_bwd(q, k, v, do, lse, delta, cosf, sinf, dwout, dpw.reshape(PW_ROWS, POOL_GROUP))
    gx, dmeta, dsl, dwq, dwkv, dg, dgq, dgkv, *r_out = _bwd_in(
        h, dh2, dq, dkv, dkr, cq, ckv, dpl, dpg, dag, norm_g, win, gq, wq, gkv, wkv, cosf, sinf,
        (wout_s, gwout, m_wout_s, v_wout_s))
    return dict(gx=gx, dmeta=dmeta, dsl=dsl, hn=hn, dwq=dwq, dwkv=dwkv, r_out=tuple(r_out), dg=dg, dgq=dgq,
                dgkv=dgkv, gpw=gpw, dps=dps, dgf=dgf, loss=loss)


_CHIP_RELS = ((0, 0), (1, 0), (0, 1), (1, 1))

_ARR_ROWS = (SHARD_IN, SHARD_OUT, 256, KV_LORA, N_META)
_ARR_COLS = (D, D, Q_LORA, 256, 256)
_PIECES = (
    (0, 0, 256, 0), (0, 256, SHARD_IN - 256, 1),
    (1, 0, 128, 0), (1, 128, 128, 1),
    (2, 0, 128, 0), (2, 128, 128, 1),
    (3, 0, 64, 0), (3, 64, 64, 1),
    (4, 0, N_META, 0),
)
_NP = len(_PIECES)
_PIECE_MAX = (256, 128, 128, 64, N_META)


def _gathered_at(refs, arr, chip, r0, n):
    if arr in (0, 1):
        return refs[arr].at[pl.ds(pl.multiple_of(_ARR_ROWS[arr] * chip + r0, 16), n), :]
    return refs[arr].at[chip, pl.ds(r0, n), :]


def _remote(src, dst, send_sem, recv_sem, to):
    return pltpu.make_async_remote_copy(src_ref=src, dst_ref=dst, send_sem=send_sem, recv_sem=recv_sem,
                                        device_id=to, device_id_type=MESH)


def _gather_weights(winT_s, wqT_s, wkv_s, meta_s, x2, tgt2):
    arrays = (0, 2, 3, 4)

    def body(win_ref, wq_ref, wkv_ref, meta_ref, x_ref, t_ref, win_o, wq_o, wkv_o, h_o, tp_o,
             s_win, s_wq, s_wkv, meta_all, head_buf, x_buf, t_buf, ici_send, ici_recv, fwd_send, fwd_recv,
             loc_sems, own_sems):
        x, y, c = lax.axis_index("x"), lax.axis_index("y"), lax.axis_index("c")
        me = 2 * x + y
        stage = (s_win, None, s_wq, s_wkv, meta_ref)
        outs = (win_o, None, wq_o, wkv_o, meta_all)

        _peer_signal(x, y, c)

        frames = pl.ds(HEAD_ROWS, S)
        loads = [pltpu.make_async_copy(x_ref, x_buf, loc_sems.at[0]), pltpu.make_async_copy(t_ref, t_buf, loc_sems.at[1])]
        local = [pltpu.make_async_copy(x_buf, h_o.at[frames, :], loc_sems.at[0]),
                 pltpu.make_async_copy(t_buf, tp_o.at[frames, :], loc_sems.at[1])]
        for cp in loads:
            cp.start()

        s_win[...] = win_ref[...].astype(BF16)
        s_wq[0:QK, :] = wq_ref[...].astype(BF16)
        s_wq[QK:256, :] = jnp.zeros((256 - QK, Q_LORA), BF16)
        s_wkv[...] = wkv_ref[...].astype(BF16)
        head_buf[...] = jnp.zeros_like(head_buf)
        zeros = pltpu.make_async_copy(head_buf, tp_o.at[pl.ds(0, HEAD_ROWS), :], loc_sems.at[2])
        zeros.start()

        def chip_of(rel):
            fx, fy = _CHIP_RELS[rel]
            return 2 * (x ^ fx) + (y ^ fy)

        def same_core_of(rel):
            fx, fy = _CHIP_RELS[rel]
            return (x ^ fx, y ^ fy, c)

        def ici_copy(rel, i, src_chip, to):
            arr, r0, n, _ = _PIECES[i]
            k = (rel - 1) * _NP + i
            return _remote(stage[arr].at[pl.ds(r0, n), :], _gathered_at(outs, arr, src_chip, r0, n),
                           ici_send.at[k], ici_recv.at[k], to)

        def fwd_copy(rel, i, to):
            arr, r0, n, _ = _PIECES[i]
            k = (rel - 1) * _NP + i
            place = _gathered_at(outs, arr, chip_of(rel), r0, n)
            return _remote(place, place, fwd_send.at[k], fwd_recv.at[k], to)

        _peer_wait()
        for core in (0, 1):
            @pl.when(c == core)
            def _(core=core):
                mine = [i for i in range(_NP) if _PIECES[i][3] == core and _PIECES[i][0] in arrays]
                theirs = [i for i in range(_NP) if _PIECES[i][3] != core and _PIECES[i][0] in arrays]
                order = (1, 2, 3)
                sends = [ici_copy(rel, i, me, same_core_of(rel)) for rel in order for i in mine]
                for cp in sends:
                    cp.start()
                for ld, st in zip(loads, local):
                    ld.wait()
                    st.start()
                own = [pltpu.make_async_copy(stage[arr], _gathered_at(outs, arr, me, 0, _ARR_ROWS[arr]), own_sems.at[arr])
                       for arr in arrays if arr != 4]
                for cp in own:
                    cp.start()
                meta_all[me] = meta_ref[...]
                for rel in order:
                    for i in mine:
                        ici_copy(rel, i, chip_of(rel), (x, y, c)).wait_recv()
                        fwd = fwd_copy(rel, i, (x, y, 1 - c))
                        fwd.start()
                        sends.append(fwd)
                for rel in order:
                    for i in theirs:
                        fwd_copy(rel, i, (x, y, c)).wait_recv()
                for cp in sends:
                    cp.wait_send()
                for cp in own:
                    cp.wait()

        zeros.wait()
        for chip in range(CHIPS):
            head_buf[PAD:HEAD_ROWS, chip * 256:(chip + 1) * 256] = meta_all[chip]
        head = pltpu.make_async_copy(head_buf, h_o.at[pl.ds(0, HEAD_ROWS), :], loc_sems.at[2])
        head.start()
        head.wait()
        for cp in local:
            cp.wait()

    vm = pl.BlockSpec(memory_space=pltpu.VMEM)
    hbm = pl.BlockSpec(memory_space=pl.ANY)
    return pl.pallas_call(
        body,
        name="gather_weights",
        in_specs=[vm] * 4 + [hbm] * 2,
        out_specs=[hbm] * 5,
        out_shape=[
            jax.ShapeDtypeStruct((D_IN, D), BF16),
            jax.ShapeDtypeStruct((CHIPS, 256, Q_LORA), BF16), jax.ShapeDtypeStruct((CHIPS, KV_LORA, 256), BF16),
            jax.ShapeDtypeStruct((N, D), F32), jax.ShapeDtypeStruct((N, D), F32),
        ],
        scratch_shapes=[pltpu.VMEM((_ARR_ROWS[a], _ARR_COLS[a]), BF16) for a in (0, 2, 3)]
        + [pltpu.VMEM((CHIPS, N_META, 256), F32), pltpu.VMEM((HEAD_ROWS, D), F32), pltpu.VMEM((S, D), F32),
           pltpu.VMEM((S, D), F32)]
        + [pltpu.SemaphoreType.DMA((3 * _NP,))] * 4 + [pltpu.SemaphoreType.DMA((3,)), pltpu.SemaphoreType.DMA((4,))],
        compiler_params=_cparams(collective_id=0),
    )(winT_s, wqT_s, wkv_s, meta_s, x2, tgt2)


_SM_ROWS = (VEC_ROWS,)
_SM_COLS = (D,)
_SM_PIECES = ((0, 0, VEC_ROWS, 0),)
_NSP = len(_SM_PIECES)
_NSB = len(_SM_ROWS)


def _reduce_grads(dsl, hn, dwq, dwkv, dmeta4, dg, dgf, dgq, dgkv, dps, loss):
    arrays = (0, 2, 3, 4)
    loaded = (2, 3, 4)
    shard_order = SEND_ORDER + (0,)

    def body(dsl_hbm, hn_hbm, dwq_ref, dwkv_ref, dmeta_ref, dg_ref, dgf_ref, dgq_ref, dgkv_ref, dps_ref,
             loss_ref, gwin_o, gwq_o, gwkv_o, gmeta_o, gg_o, ggf_o, ggq_o, ggkv_o, gps_o, gloss_o,
             ow2, ow3, ow4, sb0, sb2, sb3, sb4, st0, st2, st3, st4, rc0, rc2, rc3, rc4,
             vec, sm_sb0, sm_cs0, sm_rc0, vec_fin, slab_v, hn_v, dwin_buf, own0,
             own_sems, d2d_send, d2d_recv, ici_send, ici_recv, fin_send, fin_recv,
             swap_send, swap_recv, smi_send, smi_recv, smf_send, smf_recv, ld_sems):
        x, y, c = lax.axis_index("x"), lax.axis_index("y"), lax.axis_index("c")
        me = 2 * x + y
        _peer_signal(x, y, c)
        grads = (None, None, dwq_ref, dwkv_ref, dmeta_ref)
        outs = (gwin_o, None, gwq_o, gwkv_o, gmeta_o)
        own_buf = (None, None, ow2, ow3, ow4)
        sib_buf = (sb0, None, sb2, sb3, sb4)
        stage = (st0, None, st2, st3, st4)
        recv = (rc0, None, rc2, rc3, rc4)
        sm_mine = (vec,)
        sm_sib = (sm_sb0,)
        sm_chip = (sm_cs0,)
        sm_recv = (sm_rc0,)
        sm_out = (vec_fin,)
        sibling = (x, y, 1 - c)

        def chip_of(rel):
            fx, fy = _CHIP_RELS[rel]
            return 2 * (x ^ fx) + (y ^ fy)

        def same_core_of(rel):
            fx, fy = _CHIP_RELS[rel]
            return (x ^ fx, y ^ fy, c)

        hn_load = pltpu.make_async_copy(hn_hbm, hn_v, ld_sems.at[CHIPS])

        def slab_load(rel):
            return pltpu.make_async_copy(dsl_hbm.at[chip_of(rel)], slab_v.at[rel], ld_sems.at[rel])

        hn_load.start()
        slab_load(shard_order[0]).start()

        def slot(bufs, i, idx):
            arr, _, n, _ = _PIECES[i]
            return bufs[arr].at[idx, pl.ds(0, n), :]

        def own_load(rel, i):
            arr, r0, n, _ = _PIECES[i]
            return pltpu.make_async_copy(_gathered_at(grads, arr, chip_of(rel), r0, n), slot(own_buf, i, rel),
                                         own_sems.at[rel * _NP + i])

        def d2d_copy(rel, i):
            arr, r0, n, _ = _PIECES[i]
            k = rel * _NP + i
            return _remote(_gathered_at(grads, arr, chip_of(rel), r0, n), slot(sib_buf, i, rel),
                           d2d_send.at[k], d2d_recv.at[k], sibling)

        def ici_copy(rel, i):
            k = (rel - 1) * _NP + i
            return _remote(slot(stage, i, rel - 1), slot(recv, i, rel - 1), ici_send.at[k], ici_recv.at[k],
                           same_core_of(rel))

        def fin_copy(i):
            arr, r0, n, _ = _PIECES[i]
            place = outs[arr].at[pl.ds(r0, n), :]
            return _remote(place, place, fin_send.at[i], fin_recv.at[i], sibling)

        def sm_ici_copy(rel, j):
            blk, r0, n, _ = _SM_PIECES[j]
            k = (rel - 1) * _NSP + j
            return _remote(sm_chip[blk].at[pl.ds(r0, n), :], sm_recv[blk].at[rel - 1, pl.ds(r0, n), :],
                           smi_send.at[k], smi_recv.at[k], same_core_of(rel))

        def sm_fin_copy(j):
            blk, r0, n, _ = _SM_PIECES[j]
            place = sm_out[blk].at[pl.ds(r0, n), :]
            return _remote(place, place, smf_send.at[j], smf_recv.at[j], sibling)

        vec[...] = jnp.zeros_like(vec)
        vec[0:1, :] = dg_ref[...]
        vec[1:2, :] = dgf_ref[...]
        vec[2:3, V_GQ:V_GQ + Q_LORA] = dgq_ref[...]
        vec[2:3, V_GKV:V_GKV + KV_LORA] = dgkv_ref[...]
        vec[2:3, V_PS:V_PS + D_POOL] = dps_ref[...]
        vec[2:3, V_LOSS:D] = loss_ref[...]
        _peer_wait()
        swaps = [_remote(sm_mine[b], sm_sib[b], swap_send.at[b], swap_recv.at[b], sibling) for b in range(_NSB)]
        for cp in swaps:
            cp.start()

        for core in (0, 1):
            @pl.when(c == core)
            def _(core=core):
                mine = [i for i in range(_NP) if _PIECES[i][3] == core and _PIECES[i][0] in loaded]
                theirs = [i for i in range(_NP) if _PIECES[i][3] != core and _PIECES[i][0] in loaded]
                i0 = next(i for i in range(_NP) if _PIECES[i][0] == 0 and _PIECES[i][3] == core)
                j0 = next(i for i in range(_NP) if _PIECES[i][0] == 0 and _PIECES[i][3] != core)
                sm_mine_p = [j for j in range(_NSP) if _SM_PIECES[j][3] == core]
                sm_theirs_p = [j for j in range(_NSP) if _SM_PIECES[j][3] != core]
                sends = list(swaps)

                for rel in shard_order:
                    for i in theirs:
                        cp = d2d_copy(rel, i)
                        cp.start()
                        sends.append(cp)
                    for i in mine:
                        own_load(rel, i).start()

                def piece_rows(i):
                    return pl.ds(_PIECES[i][1], _PIECES[i][2])

                def form(rel, i):
                    r0, n = _PIECES[i][1], _PIECES[i][2]
                    dwin_buf[rel, r0:r0 + n, :] = _tn(slab_v[rel, :, r0:r0 + _PIECE_MAX[0]], hn_v[...])[0:n, :]

                def d2d0(rel, i):
                    return _remote(dwin_buf.at[rel, piece_rows(i), :], slot(sib_buf, i, rel),
                                   d2d_send.at[rel * _NP + i], d2d_recv.at[rel * _NP + i], sibling)

                def settle(rel):
                    d2d0(rel, i0).wait_recv()
                    total = dwin_buf[rel, piece_rows(i0), :] + slot(sib_buf, i0, rel)[...]
                    if rel == 0:
                        own0[0:_PIECES[i0][2], :] = total
                    else:
                        slot(stage, i0, rel - 1)[...] = total.astype(BF16)
                        cp = ici_copy(rel, i0)
                        cp.start()
                        sends.append(cp)

                for rel in SEND_ORDER:
                    for i in mine:
                        arr, r0, n, _ = _PIECES[i]
                        own_load(rel, i).wait()
                        d2d_copy(rel, i).wait_recv()
                        total = slot(own_buf, i, rel)[...] + slot(sib_buf, i, rel)[...]
                        slot(stage, i, rel - 1)[...] = total.astype(stage[arr].dtype)
                        cp = ici_copy(rel, i)
                        cp.start()
                        sends.append(cp)

                for b in range(_NSB):
                    swaps[b].wait_recv()
                    sm_chip[b][...] = sm_mine[b][...] + sm_sib[b][...]
                for rel in SEND_ORDER:
                    for j in sm_mine_p:
                        cp = sm_ici_copy(rel, j)
                        cp.start()
                        sends.append(cp)

                hn_load.wait()
                for n, rel in enumerate(shard_order):
                    slab_load(rel).wait()
                    if n == 0:
                        for later in shard_order[1:]:
                            slab_load(later).start()
                    form(rel, j0)
                    cp = d2d0(rel, j0)
                    cp.start()
                    sends.append(cp)
                    if n > 0:
                        settle(shard_order[n - 1])
                    form(rel, i0)
                settle(shard_order[-1])

                for i in mine:
                    arr, r0, n, _ = _PIECES[i]
                    own_load(0, i).wait()
                    d2d_copy(0, i).wait_recv()
                    total = slot(own_buf, i, 0)[...] + slot(sib_buf, i, 0)[...]
                    for rel in (1, 2, 3):
                        ici_copy(rel, i).wait_recv()
                        total = total + slot(recv, i, rel - 1)[...].astype(F32)
                    outs[arr][pl.ds(r0, n), :] = total
                    cp = fin_copy(i)
                    cp.start()
                    sends.append(cp)
                total = own0[0:_PIECES[i0][2], :]
                for rel in (1, 2, 3):
                    ici_copy(rel, i0).wait_recv()
                    total = total + slot(recv, i0, rel - 1)[...].astype(F32)
                outs[0][pl.ds(_PIECES[i0][1], _PIECES[i0][2]), :] = total
                cp = fin_copy(i0)
                cp.start()
                sends.append(cp)

                for j in sm_mine_p:
                    blk, r0, n, _ = _SM_PIECES[j]
                    for rel in (1, 2, 3):
                        sm_ici_copy(rel, j).wait_recv()
                    total = jnp.zeros((n, _SM_COLS[blk]), F32)
                    for chip in range(CHIPS):
                        flips = chip ^ me
                        rel = jnp.where(flips == 2, 1, jnp.where(flips == 1, 2, flips))
                        theirs_rows = sm_recv[blk][jnp.maximum(rel - 1, 0), pl.ds(r0, n), :]
                        total = total + jnp.where(rel == 0, sm_chip[blk][pl.ds(r0, n), :], theirs_rows)
                    sm_out[blk][pl.ds(r0, n), :] = total
                    cp = sm_fin_copy(j)
                    cp.start()
                    sends.append(cp)

                for i in theirs + [j0]:
                    fin_copy(i).wait_recv()
                for j in sm_theirs_p:
                    sm_fin_copy(j).wait_recv()
                for cp in sends:
                    cp.wait_send()

        gg_o[...] = vec_fin[0:1, :]
        ggf_o[...] = vec_fin[1:2, :]
        ggq_o[...] = vec_fin[2:3, V_GQ:V_GQ + Q_LORA]
        ggkv_o[...] = vec_fin[2:3, V_GKV:V_GKV + KV_LORA]
        gps_o[...] = vec_fin[2:3, V_PS:V_PS + D_POOL]
        gloss_o[...] = vec_fin[2:3, V_LOSS:D]

    vm = pl.BlockSpec(memory_space=pltpu.VMEM)
    piece_buf = lambda lead, dtype, which=arrays: [
        pltpu.VMEM((lead, _PIECE_MAX[a], _ARR_COLS[a]), F32 if a == 4 else dtype) for a in which]
    sm_buf = lambda *lead: [pltpu.VMEM(lead + (_SM_ROWS[b], _SM_COLS[b]), F32) for b in range(_NSB)]
    dma = lambda n: [pltpu.SemaphoreType.DMA((n,))] * 2
    return pl.pallas_call(
        body,
        name="reduce_grads",
        in_specs=[pl.BlockSpec(memory_space=pl.ANY)] * 4 + [vm] * 7,
        out_specs=[vm] * 10,
        out_shape=[jax.ShapeDtypeStruct((_ARR_ROWS[a], _ARR_COLS[a]), F32) for a in arrays]
        + [jax.ShapeDtypeStruct((1, D), F32),
           jax.ShapeDtypeStruct((1, D), F32), jax.ShapeDtypeStruct((1, Q_LORA), F32),
           jax.ShapeDtypeStruct((1, KV_LORA), F32), jax.ShapeDtypeStruct((1, D_POOL), F32),
           jax.ShapeDtypeStruct((1, 128), F32)],
        scratch_shapes=piece_buf(CHIPS, F32, loaded) + piece_buf(CHIPS, F32) + piece_buf(3, BF16) + piece_buf(3, BF16)
        + [pltpu.VMEM((VEC_ROWS, D), F32)] + sm_buf() + sm_buf() + sm_buf(3) + [pltpu.VMEM((VEC_ROWS, D), F32)]
        + [pltpu.VMEM((CHIPS, N, SHARD_PAD), BF16), pltpu.VMEM((N, D), BF16),
           pltpu.VMEM((CHIPS, SHARD_PAD, D), F32), pltpu.VMEM((_PIECE_MAX[0], D), F32)]
        + [pltpu.SemaphoreType.DMA((CHIPS * _NP,))]
        + dma(CHIPS * _NP) + dma(3 * _NP) + dma(_NP) + dma(_NSB) + dma(3 * _NSP) + dma(_NSP)
        + [pltpu.SemaphoreType.DMA((CHIPS + 1,))],
        compiler_params=_cparams(collective_id=3),
    )(dsl, hn, dwq, dwkv, dmeta4, dg, dgf, dgq, dgkv, dps, loss)


def _adamw_math(w, g, m, v):
    m = B1 * m + (1.0 - B1) * g
    v = B2 * v + (1.0 - B2) * (g * g)
    m_hat = m / C1
    v_hat = v / C2
    delta = -LR * (m_hat / (jnp.sqrt(v_hat) + ADAM_EPS) + WD * w)
    return delta, m, v


def _adamw(big, block_rows, groups):
    rows, cols = big[0].shape
    n = len(groups)

    def body(*refs):
        w_ref, g_ref, m_ref, v_ref = refs[0:4]
        small_in = refs[4:4 + 4 * n]
        go_ref, d_ref, nm_ref, nv_ref = refs[4 + 4 * n:8 + 4 * n]
        small_out = refs[8 + 4 * n:]
        g = g_ref[...]
        go_ref[...] = g
        d_ref[...], nm_ref[...], nv_ref[...] = _adamw_math(w_ref[...], g, m_ref[...], v_ref[...])

        @pl.when(pl.program_id(0) == 0)
        def _():
            for t in range(n):
                sw_ref, sg_ref, sm_ref, sv_ref = small_in[4 * t:4 * t + 4]
                sg = sg_ref[0:sw_ref.shape[0], :]
                small_out[4 * t][...] = sg
                small_out[4 * t + 1][...], small_out[4 * t + 2][...], small_out[4 * t + 3][...] = _adamw_math(
                    sw_ref[...], sg, sm_ref[...], sv_ref[...])

    spec = pl.BlockSpec((block_rows, cols), lambda i: (i, 0))
    vm = pl.BlockSpec(memory_space=pltpu.VMEM)
    outs = pl.pallas_call(
        body,
        name="adamw",
        grid=(rows // block_rows,),
        in_specs=[spec] * 4 + [vm] * (4 * n),
        out_specs=[spec] * 4 + [vm] * (4 * n),
        out_shape=[jax.ShapeDtypeStruct(big[0].shape, F32)] * 4
        + [jax.ShapeDtypeStruct(grp[0].shape, F32) for grp in groups for _ in range(4)],
        compiler_params=_cparams(dimension_semantics=("arbitrary",)),
    )(*big, *[a for grp in groups for a in grp])
    return tuple(outs[0:4]), [tuple(outs[4 + 4 * t:8 + 4 * t]) for t in range(n)]


def _rope_tables():
    half = QK_ROPE // 2
    f32 = np.float32
    inv_freq = (f32(1.0) / (f32(ROPE_THETA) ** (np.arange(half, dtype=f32) / f32(half)))).astype(f32)
    pos = np.arange(N, dtype=f32) - f32(PAD)
    ang = (pos[:, None] * inv_freq[None, :]).astype(f32)
    cos, sin = np.cos(ang).astype(f32), np.sin(ang).astype(f32)
    zero = np.zeros((N, 128 - QK_ROPE), f32)
    return jnp.asarray(np.concatenate([cos, cos, zero], axis=1)), jnp.asarray(np.concatenate([-sin, sin, zero], axis=1))


def kernel(x, meta_tokens, norm_g, w_in, q_norm_g, w_q_b, kv_norm_g, w_kv_b, pool_w, pool_scale, w_out, final_norm_g, loss_target, m_meta_tokens, m_norm_g, m_w_in, m_q_norm_g, m_w_q_b, m_kv_norm_g, m_w_kv_b, m_pool_w, m_pool_scale, m_w_out, m_final_norm_g, v_meta_tokens, v_norm_g, v_w_in, v_q_norm_g, v_w_q_b, v_kv_norm_g, v_w_kv_b, v_pool_w, v_pool_scale, v_w_out, v_final_norm_g):
    tr = lambda a: a[0].T
    win, wq, wkv, h, tgt = _gather_weights(tr(w_in), tr(w_q_b), w_kv_b[0], meta_tokens, x[0], loss_target[0])
    cosf, sinf = _rope_tables()
    gf = final_norm_g.reshape(1, D)

    part = _local_step(h, tgt, norm_g, win, q_norm_g, wq, kv_norm_g, wkv, pool_w[0], pool_scale, w_out[0], m_w_out[0],
                       v_w_out[0], gf, cosf, sinf)

    pw2 = lambda a: a.reshape(len(POOL_WINDOWS) * POOL_GROUP, POOL_GROUP)
    gpw = part["gpw"]
    gwinT, gwqT, gwkv, gmeta, gg, ggf, ggq, ggkv, gps, gloss = _reduce_grads(
        part["dsl"], part["hn"], part["dwq"], part["dwkv"], part["dmeta"], part["dg"],
        part["dgf"], part["dgq"], part["dgkv"], part["dps"], part["loss"])

    r_out = part["r_out"]
    fn2 = lambda a: a.reshape(1, D)
    r_in, (r_meta, r_norm, r_gq, r_wq, r_gkv, r_wkv, r_pw, r_ps, r_fn) = _adamw((tr(w_in), gwinT, tr(m_w_in), tr(v_w_in)), 248, [
        (meta_tokens, gmeta, m_meta_tokens, v_meta_tokens),
        (norm_g, gg, m_norm_g, v_norm_g),
        (q_norm_g, ggq, m_q_norm_g, v_q_norm_g),
        (tr(w_q_b), gwqT, tr(m_w_q_b), tr(v_w_q_b)),
        (kv_norm_g, ggkv, m_kv_norm_g, v_kv_norm_g),
        (w_kv_b[0], gwkv, m_w_kv_b[0], v_w_kv_b[0]),
        (pw2(pool_w), gpw, pw2(m_pool_w), pw2(v_pool_w)),
        (pool_scale, gps, m_pool_scale, v_pool_scale),
        (fn2(final_norm_g), ggf, fn2(m_final_norm_g), fn2(v_final_norm_g)),
    ])
    untr = lambda a: a.T[None]
    pw4 = lambda a: a.reshape(1, len(POOL_WINDOWS), POOL_GROUP, POOL_GROUP)
    per_kind = [[
        r_meta[kind], r_norm[kind], untr(r_in[kind]), r_gq[kind], untr(r_wq[kind]), r_gkv[kind], r_wkv[kind][None],
        pw4(r_pw[kind]), r_ps[kind], r_out[kind][None], r_fn[kind].reshape(D),
    ] for kind in range(4)]
    return (gloss[0, 0], part["gx"][None], *per_kind[0], *per_kind[1], *per_kind[2], *per_kind[3])
```

```python
import jax
import jax.numpy as jnp
import numpy as np
from jax import lax
from jax.experimental import pallas as pl
from jax.experimental.pallas import tpu as pltpu

F32 = jnp.float32
BF16 = jnp.bfloat16

D = 1024
S = 2048
N_META = 16
PAD = 112
HEAD_ROWS = PAD + N_META
N = HEAD_ROWS + S
D_POOL = 512
POOL_WINDOWS = (2, 4, 8, 16)
POOL_GROUP = 128
HALO = 16
HEADS = 4
QK_NOPE = 128
QK_ROPE = 64
QK = QK_NOPE + QK_ROPE
V_HEAD = 128
Q_LORA = 256
KV_LORA = 128
D_IN = 1984
EPS = 1e-6
ROPE_THETA = 10000.0
SCALE = QK ** -0.5
CHIPS = 4

ROWS_FWD = 544
ROWS_MID = 544
ROWS_BWD = 544
TK = 128
TQ = 256
NQ = S // TQ
HEADS_PER_STEP_BWD = 2

O_PI, O_PG, O_CQ, O_CKV, O_KR, O_AG = 0, 512, 1024, 1280, 1408, 1472
O_KR_END = O_KR + 128
SHARD_IN = D_IN // CHIPS
SHARD_PAD = 512
SHARD_OUT = D // CHIPS

LR, B1, B2, ADAM_EPS, WD, STEP = 0.001, 0.9, 0.999, 1e-08, 0.01, 10
C1 = 1.0 - B1**STEP
C2 = 1.0 - B2**STEP

VMEM_LIMIT = 60 * 1024 * 1024
MESH = pl.DeviceIdType.MESH
NEG = -1e30

VEC_ROWS = 8
PW_ROWS = len(POOL_WINDOWS) * POOL_GROUP
V_GQ, V_GKV, V_PS, V_LOSS = 0, 256, 384, 896


def _cparams(**kw):
    return pltpu.CompilerParams(vmem_limit_bytes=VMEM_LIMIT, **kw)


def _nt(a, b):
    return lax.dot_general(a, b, (((1,), (1,)), ((), ())), preferred_element_type=F32)


def _tn(a, b):
    return lax.dot_general(a, b, (((0,), (0,)), ((), ())), preferred_element_type=F32)


def _nn(a, b):
    return jnp.dot(a, b, preferred_element_type=F32)


def _swap64(t):
    return pltpu.roll(t, 32, 1) + pltpu.roll(t, 96, 1)


def _sigmoid(x):
    return 1.0 / (1.0 + jnp.exp(-x))


def _low_lanes():
    return (lax.broadcasted_iota(jnp.int32, (1, 128), 1) < QK_ROPE).astype(F32)


def _rows(w, rows):
    return pl.BlockSpec((rows, w), lambda i: (i, 0))


def _const(*shape):
    return pl.BlockSpec(shape, lambda *_: (0,) * len(shape), pipeline_mode=pl.Buffered(1))


STAT_GROUPS = HEADS // HEADS_PER_STEP_BWD


def _stat_slot(head):
    return head // HEADS_PER_STEP_BWD, head % HEADS_PER_STEP_BWD


N_PEERS = 4
SEND_ORDER = (3, 1, 2)


def _peer_signal(x, y, c):
    barrier = pltpu.get_barrier_semaphore()
    peers = [(x, y, 1 - c)] + [(x ^ fx, y ^ fy, c) for fx, fy in _CHIP_RELS[1:]]
    assert len(peers) == N_PEERS
    for peer in peers:
        pl.semaphore_signal(barrier, inc=1, device_id=peer, device_id_type=MESH)


def _peer_wait():
    pl.semaphore_wait(pltpu.get_barrier_semaphore(), N_PEERS)


def _attn_tiles():
    return [(0, TK, TK)] + [(TK + TQ * t, TQ, TK + TQ * (t + 1)) for t in range(NQ)]


def _masked_scores(q, k, rows, klen):
    s = _nt(q, k)
    col = lax.broadcasted_iota(jnp.int32, (1, TK), 1)
    head_bias = jnp.where(col >= PAD, 0.0, NEG)
    if klen == TK:
        return s + head_bias
    r = lax.broadcasted_iota(jnp.int32, (rows, 1), 0) >> 6
    c = lax.broadcasted_iota(jnp.int32, (1, rows), 1) >> 6
    diag_bias = jnp.where(c <= r, 0.0, NEG)
    parts = [s[:, 0:TK] + head_bias]
    if klen - rows > TK:
        parts.append(s[:, TK:klen - rows])
    parts.append(s[:, klen - rows:klen] + diag_bias)
    return jnp.concatenate(parts, axis=1)


def _fwd_in(h, norm_g, win, gq, wq, gkv, wkv, cosf, sinf):
    tr = ROWS_FWD

    def body(h_ref, g_ref, win_ref, gq_ref, wq_ref, gkv_ref, wkv_ref, cos_ref, sin_ref,
             pi_ref, pg_ref, cq_ref, ckv_ref, ag_ref, q_ref, k_ref, v_ref, hn_ref):
        h = h_ref[...]
        r = lax.rsqrt(jnp.mean(h * h, axis=-1, keepdims=True) + EPS)
        hn = ((h * r) * g_ref[...]).astype(BF16)
        hn_ref[...] = hn
        u = _nt(hn, win_ref[0:O_KR_END, :])
        pi_ref[...] = u[:, O_PI:O_PG]
        pg_ref[...] = u[:, O_PG:O_CQ]
        cq = u[:, O_CQ:O_CKV]
        ckv = u[:, O_CKV:O_KR]
        cq_ref[...] = cq
        ckv_ref[...] = ckv
        ag_ref[...] = _nt(hn, win_ref[O_AG:D_IN, :])
        cosv = cos_ref[...]
        sinv = sin_ref[...]
        kr = u[:, O_KR:O_KR_END] * _low_lanes()
        kr = (kr * cosv + _swap64(kr) * sinv).astype(BF16)
        rq = lax.rsqrt(jnp.mean(cq * cq, axis=-1, keepdims=True) + EPS)
        cqn = ((cq * rq) * gq_ref[...]).astype(BF16)
        rkv = lax.rsqrt(jnp.mean(ckv * ckv, axis=-1, keepdims=True) + EPS)
        ckvn = ((ckv * rkv) * gkv_ref[...]).astype(BF16)
        for hd in range(HEADS):
            qh = _nt(cqn, wq_ref[hd]) * SCALE
            z = qh[:, QK_NOPE:]
            q_ref[hd, :, 0:QK_NOPE] = qh[:, 0:QK_NOPE].astype(BF16)
            q_ref[hd, :, QK_NOPE:] = (z * cosv + _swap64(z) * sinv).astype(BF16)
            kvh = _nn(ckvn, wkv_ref[hd])
            k_ref[hd, :, 0:QK_NOPE] = kvh[:, 0:QK_NOPE].astype(BF16)
            k_ref[hd, :, QK_NOPE:] = kr
            v_ref[hd] = kvh[:, QK_NOPE:].astype(BF16)

    head = lambda w: pl.BlockSpec((HEADS, tr, w), lambda i: (0, i, 0))
    return pl.pallas_call(
        body,
        name="fwd_in",
        grid=(N // tr,),
        in_specs=[
            _rows(D, tr), _const(1, D), _const(D_IN, D), _const(1, Q_LORA), _const(HEADS, 256, Q_LORA),
            _const(1, KV_LORA), _const(HEADS, KV_LORA, 256), _rows(128, tr), _rows(128, tr),
        ],
        out_specs=[_rows(D_POOL, tr), _rows(D_POOL, tr), _rows(Q_LORA, tr), _rows(KV_LORA, tr), _rows(D_POOL, tr),
                   head(256), head(256), head(V_HEAD), _rows(D, tr)],
        out_shape=[
            jax.ShapeDtypeStruct((N, D_POOL), F32), jax.ShapeDtypeStruct((N, D_POOL), F32),
            jax.ShapeDtypeStruct((N, Q_LORA), F32), jax.ShapeDtypeStruct((N, KV_LORA), F32),
            jax.ShapeDtypeStruct((N, D_POOL), F32),
            jax.ShapeDtypeStruct((HEADS, N, 256), BF16), jax.ShapeDtypeStruct((HEADS, N, 256), BF16),
            jax.ShapeDtypeStruct((HEADS, N, V_HEAD), BF16), jax.ShapeDtypeStruct((N, D), BF16),
        ],
        compiler_params=_cparams(dimension_semantics=("arbitrary",)),
    )(h, norm_g, win, gq, wq, gkv, wkv, cosf, sinf)


def _attn_fwd(q, k, v, wout_s):
    tiles = _attn_tiles()
    n_t = len(tiles)
    half = SHARD_OUT // 2
    send_step = 2
    fwd_step = n_t - 2

    def body(q_hbm, k_hbm, v_hbm, wout_ref, o_hbm, lse_ref, wout_o, q_buf, k_buf, v_buf, o_buf, s_wout, in_sems, out_sems,
             ici_send, ici_recv, fwd_send, fwd_recv, own_sem):
        step = pl.program_id(0)
        x, y, c = lax.axis_index("x"), lax.axis_index("y"), lax.axis_index("c")
        me = 2 * x + y

        def chip_of(rel):
            fx, fy = _CHIP_RELS[rel]
            return 2 * (x ^ fx) + (y ^ fy)

        def place(chip, core):
            return wout_o.at[pl.ds(pl.multiple_of(SHARD_OUT * chip + half * core, half), half), :]

        def ici_copy(rel, src_chip, to):
            return _remote(s_wout.at[pl.ds(pl.multiple_of(half * c, half), half), :], place(src_chip, c),
                           ici_send.at[rel - 1], ici_recv.at[rel - 1], to)

        def fwd_copy(rel, core, to):
            spot = place(chip_of(rel), core)
            return _remote(spot, spot, fwd_send.at[rel - 1], fwd_recv.at[rel - 1], to)

        own = pltpu.make_async_copy(s_wout, wout_o.at[pl.ds(pl.multiple_of(SHARD_OUT * me, SHARD_OUT), SHARD_OUT), :], own_sem)

        @pl.when(step == 0)
        def _():
            _peer_signal(x, y, c)
            s_wout[...] = wout_ref[...].astype(BF16)
            own.start()

        @pl.when(step == send_step)
        def _():
            _peer_wait()
            for rel in SEND_ORDER:
                fx, fy = _CHIP_RELS[rel]
                ici_copy(rel, me, (x ^ fx, y ^ fy, c)).start()

        @pl.when(step == fwd_step)
        def _():
            for rel in (1, 2, 3):
                ici_copy(rel, chip_of(rel), (x, y, c)).wait_recv()
                fwd_copy(rel, c, (x, y, 1 - c)).start()

        def finish_wout():
            for rel in (1, 2, 3):
                fwd_copy(rel, 1 - c, (x, y, c)).wait_recv()
            for rel in (1, 2, 3):
                ici_copy(rel, me, (x, y, c)).wait_send()
                fwd_copy(rel, c, (x, y, c)).wait_send()
            own.wait()

        def loads(idx):
            q0, rows, _ = tiles[idx]
            rs = pl.ds(q0, rows)
            return [pltpu.make_async_copy(src.at[:, rs, :], dst.at[:, rs, :], in_sems.at[a, idx % 2])
                    for a, (src, dst) in enumerate(((q_hbm, q_buf), (k_hbm, k_buf), (v_hbm, v_buf)))]

        def store(idx):
            q0, rows, _ = tiles[idx]
            return pltpu.make_async_copy(o_buf.at[idx % 2, pl.ds(0, rows), :], o_hbm.at[pl.ds(q0, rows), :],
                                         out_sems.at[idx % 2])

        @pl.when(step == 0)
        def _():
            lse_ref[...] = jnp.zeros_like(lse_ref)
            for cp in loads(0):
                cp.start()

        for idx, (q0, rows, klen) in enumerate(tiles):
            @pl.when(step == idx)
            def _(idx=idx, q0=q0, rows=rows, klen=klen):
                for cp in loads(idx):
                    cp.wait()
                if idx + 1 < n_t:
                    for cp in loads(idx + 1):
                        cp.start()
                if idx >= 2:
                    store(idx - 2).wait()
                for hd in range(HEADS):
                    s = _masked_scores(q_buf[hd, q0:q0 + rows, :], k_buf[hd, 0:klen, :], rows, klen)
                    m = jnp.max(s, axis=-1, keepdims=True)
                    p = jnp.exp(s - m)
                    l = jnp.sum(p, axis=-1, keepdims=True)
                    o_buf[idx % 2, 0:rows, hd * V_HEAD:(hd + 1) * V_HEAD] = _nn(p.astype(BF16), v_buf[hd, 0:klen, :]) / l
                    grp, lane = _stat_slot(hd)
                    lse_ref[grp, q0:q0 + rows, lane:lane + 1] = m + jnp.log(l)
                store(idx).start()
                if idx == n_t - 1:
                    store(idx - 1).wait()
                    store(idx).wait()
                    finish_wout()

    hbm = pl.BlockSpec(memory_space=pl.ANY)
    return pl.pallas_call(
        body,
        name="attn_fwd",
        grid=(n_t,),
        in_specs=[hbm, hbm, hbm, _const(SHARD_OUT, D)],
        out_specs=[hbm, _const(STAT_GROUPS, N, 128), hbm],
        out_shape=[jax.ShapeDtypeStruct((N, HEADS * V_HEAD), F32), jax.ShapeDtypeStruct((STAT_GROUPS, N, 128), F32),
                   jax.ShapeDtypeStruct((D, D), BF16)],
        scratch_shapes=[pltpu.VMEM((HEADS, N, 256), BF16), pltpu.VMEM((HEADS, N, 256), BF16),
                        pltpu.VMEM((HEADS, N, V_HEAD), BF16), pltpu.VMEM((2, TQ, HEADS * V_HEAD), F32),
                        pltpu.VMEM((SHARD_OUT, D), BF16),
                        pltpu.SemaphoreType.DMA((3, 2)), pltpu.SemaphoreType.DMA((2,))]
        + [pltpu.SemaphoreType.DMA((3,))] * 4 + [pltpu.SemaphoreType.DMA],
        compiler_params=_cparams(dimension_semantics=("arbitrary",), collective_id=1),
    )(q, k, v, wout_s)


def _inv_count(row0, rows, w):
    row = row0 + lax.broadcasted_iota(jnp.int32, (rows, 1), 0)
    return 1.0 / jnp.clip(row - (PAD - 1), 1, w).astype(F32)


def _mid(h, tgt, pool_in, pool_gate, attn_gate, attn, pool_w, pool_scale, wout, gf):
    tr = ROWS_MID
    per = tr // HALO
    ng = len(POOL_WINDOWS)

    def body(h_ref, t_ref, pin_ref, halo_ref, pg_ref, ag_ref, at_ref, pw_ref, ps_ref, wout_ref, gf_ref,
             dh2_ref, do_ref, delta_ref, dag_ref, dpg_ref, dpl_ref, dwout_ref, dpw_ref, dps_ref, dgf_ref, loss_ref):
        i = pl.program_id(0)

        @pl.when(i == 0)
        def _():
            dwout_ref[...] = jnp.zeros_like(dwout_ref)
            dpw_ref[...] = jnp.zeros_like(dpw_ref)
            dps_ref[...] = jnp.zeros_like(dps_ref)
            dgf_ref[...] = jnp.zeros_like(dgf_ref)
            loss_ref[...] = jnp.zeros_like(loss_ref)

        row0 = i * tr
        real = (row0 + lax.broadcasted_iota(jnp.int32, (tr, 1), 0)) >= HEAD_ROWS
        h = h_ref[...]

        halo = jnp.where(i > 0, halo_ref[...], 0.0)
        ext = jnp.concatenate([halo, pin_ref[...]], axis=0)
        pooled = []
        for g, w in enumerate(POOL_WINDOWS):
            e = ext[:, g * POOL_GROUP:(g + 1) * POOL_GROUP]
            acc = e
            shift = 1
            while shift < w:
                acc = acc + pltpu.roll(acc, shift, 0)
                shift *= 2
            pooled.append((acc[HALO:] * _inv_count(row0, tr, w) - e[HALO:]).astype(BF16))
        pw = [pw_ref[g].astype(BF16) for g in range(ng)]
        mixed = jnp.concatenate([_nn(pooled[g], pw[g]) for g in range(ng)], axis=1)
        ps = ps_ref[...]
        mixed_s = mixed * ps
        pg = pg_ref[...]
        sig_p = _sigmoid(pg)
        silu_p = pg * sig_p
        pool_out = (silu_p * mixed_s).astype(BF16)
        ag = ag_ref[...]
        sig_a = _sigmoid(ag)
        silu_a = ag * sig_a
        at = at_ref[...]
        attn_out = (silu_a * at).astype(BF16)
        cat = jnp.concatenate([pool_out, attn_out], axis=1)
        h2 = h + _nn(cat, wout_ref[...])

        r2 = lax.rsqrt(jnp.mean(h2 * h2, axis=-1, keepdims=True) + EPS)
        n2 = h2 * r2
        gfv = gf_ref[...]
        err = jnp.where(real, n2 * gfv - t_ref[...], 0.0)
        loss_ref[...] += jnp.sum(jnp.sum(err * err, axis=-1, keepdims=True), axis=0, keepdims=True) * (0.5 / D)
        dy = err * (1.0 / D)
        dgf_ref[...] += jnp.sum(dy * n2, axis=0, keepdims=True)
        dn = dy * gfv
        dh2 = r2 * (dn - n2 * jnp.mean(dn * n2, axis=-1, keepdims=True))
        dh2_ref[...] = dh2
        dh2b = dh2.astype(BF16)

        dwout_ref[...] += _tn(cat, dh2b)
        dcat = _nt(dh2b, wout_ref[...])
        dpo = dcat[:, 0:D_POOL]
        dao = dcat[:, D_POOL:D]
        do = dao * silu_a
        prod = do * at
        delta_ref[...] = jnp.zeros_like(delta_ref)
        for hd in range(HEADS):
            grp, lane = _stat_slot(hd)
            cols = slice(hd * V_HEAD, (hd + 1) * V_HEAD)
            do_ref[grp, :, lane * V_HEAD:(lane + 1) * V_HEAD] = do[:, cols].astype(BF16)
            delta_ref[grp, :, lane:lane + 1] = jnp.sum(prod[:, cols], axis=-1, keepdims=True)
        dag_ref[...] = (dao * at * (sig_a * (1.0 + ag * (1.0 - sig_a)))).astype(BF16)
        dmixed_s = dpo * silu_p
        dpg_ref[...] = (dpo * mixed_s * (sig_p * (1.0 + pg * (1.0 - sig_p)))).astype(BF16)
        dps_ref[...] += jnp.sum(dmixed_s * mixed, axis=0, keepdims=True)
        dmixed = (dmixed_s * ps).astype(BF16)
        dpl = []
        for g in range(ng):
            dm = dmixed[:, g * POOL_GROUP:(g + 1) * POOL_GROUP]
            dpl.append(_nt(dm, pw[g]))
            dpw_ref[g] += _tn(pooled[g], dm)
        dpl_ref[...] = jnp.concatenate(dpl, axis=1)

    halo_spec = pl.BlockSpec((HALO, D_POOL), lambda i: (jnp.maximum(i * per - 1, 0), 0))
    return pl.pallas_call(
        body,
        name="mid",
        grid=(N // tr,),
        in_specs=[
            _rows(D, tr), _rows(D, tr), _rows(D_POOL, tr), halo_spec, _rows(D_POOL, tr), _rows(D_POOL, tr),
            _rows(D_POOL, tr), _const(ng, POOL_GROUP, POOL_GROUP), _const(1, D_POOL), _const(D, D), _const(1, D),
        ],
        out_specs=[
            _rows(D, tr), pl.BlockSpec((STAT_GROUPS, tr, HEADS_PER_STEP_BWD * V_HEAD), lambda i: (0, i, 0)),
            pl.BlockSpec((STAT_GROUPS, tr, 128), lambda i: (0, i, 0)),
            _rows(D_POOL, tr), _rows(D_POOL, tr), _rows(D_POOL, tr),
            _const(D, D), _const(ng, POOL_GROUP, POOL_GROUP), _const(1, D_POOL), _const(1, D), _const(1, 128),
        ],
        out_shape=[
            jax.ShapeDtypeStruct((N, D), F32), jax.ShapeDtypeStruct((STAT_GROUPS, N, HEADS_PER_STEP_BWD * V_HEAD), BF16),
            jax.ShapeDtypeStruct((STAT_GROUPS, N, 128), F32),
            jax.ShapeDtypeStruct((N, D_POOL), BF16), jax.ShapeDtypeStruct((N, D_POOL), BF16),
            jax.ShapeDtypeStruct((N, D_POOL), F32), jax.ShapeDtypeStruct((D, D), F32),
            jax.ShapeDtypeStruct((ng, POOL_GROUP, POOL_GROUP), F32),
            jax.ShapeDtypeStruct((1, D_POOL), F32), jax.ShapeDtypeStruct((1, D), F32), jax.ShapeDtypeStruct((1, 128), F32),
        ],
        compiler_params=_cparams(dimension_semantics=("arbitrary",)),
    )(h, tgt, pool_in, pool_in, pool_gate, attn_gate, attn, pool_w, pool_scale, wout, gf)


def _unrope(dy, cosv, sinv):
    return dy * cosv + _swap64(dy * sinv) * _low_lanes()


def _attn_bwd(q, k, v, do, lse, delta, cosf, sinf, dwout, dpw):
    tiles = _attn_tiles()
    hp = HEADS_PER_STEP_BWD
    n_g = HEADS // hp
    n_t = len(tiles)
    half = SHARD_OUT // 2
    half_pw = PW_ROWS // 2
    swap_at, send_at, sum_at = (0, 3), (0, 5), (n_g - 1, n_t // 2)

    def body(q_hbm, k_hbm, v_hbm, do_hbm, lse_ref, delta_ref, cos_ref, sin_ref, dwout_hbm, dpw_ref, dq_hbm, dkv_ref, dkr_ref,
             gwout_ref, gpw_ref, q_buf, k_buf, v_buf, do_buf, dq_buf, dk_acc, dv_acc, own_w, sib_w, stage_w, recv_w, gw_buf,
             pw_sib, pw_chip, pw_recv, pw_buf,
             in_sems, out_sems, ow_sems, d2d_send, d2d_recv, ici_send, ici_recv, fin_send, fin_recv,
             pw_swap_send, pw_swap_recv, pw_ici_send, pw_ici_recv, pw_fin_send, pw_fin_recv):
        grp = pl.program_id(0)
        step = pl.program_id(1)
        heads = pl.ds(grp * hp, hp)
        x, y, c = lax.axis_index("x"), lax.axis_index("y"), lax.axis_index("c")
        me = 2 * x + y
        sibling = (x, y, 1 - c)

        def pw_rows(core):
            return pl.ds(pl.multiple_of(half_pw * core, half_pw), half_pw)

        def pw_swap():
            return _remote(dpw_ref, pw_sib, pw_swap_send.at[0], pw_swap_recv.at[0], sibling)

        def pw_ici(rel):
            fx, fy = _CHIP_RELS[rel]
            return _remote(pw_chip.at[pw_rows(c), :], pw_recv.at[rel - 1], pw_ici_send.at[rel - 1], pw_ici_recv.at[rel - 1],
                           (x ^ fx, y ^ fy, c))

        def pw_fin(core):
            spot = pw_buf.at[pw_rows(core), :]
            return _remote(spot, spot, pw_fin_send.at[0], pw_fin_recv.at[0], sibling)

        def chip_of(rel):
            fx, fy = _CHIP_RELS[rel]
            return 2 * (x ^ fx) + (y ^ fy)

        def piece(chip, core):
            return dwout_hbm.at[pl.ds(pl.multiple_of(SHARD_OUT * chip + half * core, half), half), :]

        def own_load(rel):
            return pltpu.make_async_copy(piece(chip_of(rel), c), own_w.at[rel], ow_sems.at[rel])

        def d2d_copy(rel):
            return _remote(piece(chip_of(rel), 1 - c), sib_w.at[rel], d2d_send.at[rel], d2d_recv.at[rel], sibling)

        def ici_copy(rel):
            fx, fy = _CHIP_RELS[rel]
            return _remote(stage_w.at[rel - 1], recv_w.at[rel - 1], ici_send.at[rel - 1], ici_recv.at[rel - 1],
                           (x ^ fx, y ^ fy, c))

        def fin_copy(core):
            spot = gw_buf.at[pl.ds(pl.multiple_of(half * core, half), half), :]
            return _remote(spot, spot, fin_send.at[0], fin_recv.at[0], sibling)

        @pl.when((grp == 0) & (step == 0))
        def _():
            _peer_signal(x, y, c)
            for rel in SEND_ORDER + (0,):
                own_load(rel).start()

        @pl.when((grp == swap_at[0]) & (step == swap_at[1]))
        def _():
            _peer_wait()
            pw_swap().start()
            for rel in SEND_ORDER + (0,):
                d2d_copy(rel).start()

        @pl.when((grp == send_at[0]) & (step == send_at[1]))
        def _():
            for rel in SEND_ORDER:
                own_load(rel).wait()
                d2d_copy(rel).wait_recv()
                stage_w[rel - 1] = (own_w[rel] + sib_w[rel]).astype(BF16)
                ici_copy(rel).start()
            pw_swap().wait_recv()
            pw_chip[...] = dpw_ref[...] + pw_sib[...]
            for rel in SEND_ORDER:
                pw_ici(rel).start()

        @pl.when((grp == sum_at[0]) & (step == sum_at[1]))
        def _():
            own_load(0).wait()
            d2d_copy(0).wait_recv()
            total = own_w[0] + sib_w[0]
            for rel in (1, 2, 3):
                ici_copy(rel).wait_recv()
                total = total + recv_w[rel - 1].astype(F32)
            gw_buf[pl.ds(pl.multiple_of(half * c, half), half), :] = total
            fin_copy(c).start()
            for rel in (1, 2, 3):
                pw_ici(rel).wait_recv()
            total = jnp.zeros((half_pw, POOL_GROUP), F32)
            for chip in range(CHIPS):
                flips = chip ^ me
                rel = jnp.where(flips == 2, 1, jnp.where(flips == 1, 2, flips))
                total = total + jnp.where(rel == 0, pw_chip[pw_rows(c), :], pw_recv[jnp.maximum(rel - 1, 0)])
            pw_buf[pw_rows(c), :] = total
            pw_fin(c).start()

        def finish_dwout():
            fin_copy(1 - c).wait_recv()
            pw_fin(1 - c).wait_recv()
            for rel in (0, 1, 2, 3):
                d2d_copy(rel).wait_send()
            for rel in (1, 2, 3):
                ici_copy(rel).wait_send()
                pw_ici(rel).wait_send()
            fin_copy(c).wait_send()
            pw_swap().wait_send()
            pw_fin(c).wait_send()
            gwout_ref[...] = gw_buf[...]
            gpw_ref[...] = pw_buf[...]

        def loads(g, idx):
            q0, rows, _ = tiles[idx]
            rs = pl.ds(q0, rows)
            par = (g * n_t + idx) % 2
            hs = pl.ds(g * hp, hp)
            pairs = ((q_hbm.at[hs, rs, :], q_buf.at[:, rs, :]), (k_hbm.at[hs, rs, :], k_buf.at[:, rs, :]),
                     (v_hbm.at[hs, rs, :], v_buf.at[:, rs, :]), (do_hbm.at[g, rs, :], do_buf.at[rs, :]))
            return [pltpu.make_async_copy(src, dst, in_sems.at[a, par]) for a, (src, dst) in enumerate(pairs)]

        def store(idx):
            q0, rows, _ = tiles[idx]
            return pltpu.make_async_copy(dq_buf.at[idx % 2, :, pl.ds(0, rows), :], dq_hbm.at[heads, pl.ds(q0, rows), :],
                                         out_sems.at[idx % 2])

        @pl.when(step == 0)
        def _():
            dk_acc[...] = jnp.zeros_like(dk_acc)
            dv_acc[...] = jnp.zeros_like(dv_acc)

        @pl.when((step == 0) & (grp == 0))
        def _():
            dkr_ref[...] = jnp.zeros_like(dkr_ref)
            for cp in loads(grp, 0):
                cp.start()

        for idx, (q0, rows, klen) in enumerate(tiles):
            @pl.when(step == idx)
            def _(idx=idx, q0=q0, rows=rows, klen=klen):
                for cp in loads(grp, idx):
                    cp.wait()
                if idx + 1 < n_t:
                    for cp in loads(grp, idx + 1):
                        cp.start()
                if idx >= 2:
                    store(idx - 2).wait()
                qs = pl.ds(q0, rows)
                for hd in range(hp):
                    qv = q_buf[hd, qs, :]
                    kv = k_buf[hd, 0:klen, :]
                    p = jnp.exp(_masked_scores(qv, kv, rows, klen) - lse_ref[0, qs, hd:hd + 1])
                    dob = do_buf[qs, hd * V_HEAD:(hd + 1) * V_HEAD]
                    ds = (p * (_nt(dob, v_buf[hd, 0:klen, :]) - delta_ref[0, qs, hd:hd + 1])).astype(BF16)
                    dq = _nn(ds, kv) * SCALE
                    dq_buf[idx % 2, hd, 0:rows, 0:QK_NOPE] = dq[:, 0:QK_NOPE].astype(BF16)
                    dq_buf[idx % 2, hd, 0:rows, QK_NOPE:] = _unrope(dq[:, QK_NOPE:], cos_ref[qs, :], sin_ref[qs, :]).astype(BF16)
                    dk_acc[hd, 0:klen, :] += _tn(ds, qv)
                    dv_acc[hd, 0:klen, :] += _tn(p.astype(BF16), dob)
                store(idx).start()

        @pl.when(step == n_t - 1)
        def _():
            @pl.when(grp + 1 < n_g)
            def _():
                for cp in loads(grp + 1, 0):
                    cp.start()

            for hd in range(hp):
                dkv_ref[hd, :, 0:QK_NOPE] = dk_acc[hd, :, 0:QK_NOPE].astype(BF16)
                dkv_ref[hd, :, QK_NOPE:] = dv_acc[hd].astype(BF16)
                dkr_ref[...] += dk_acc[hd, :, QK_NOPE:]
            store(n_t - 2).wait()
            store(n_t - 1).wait()

            @pl.when(grp == n_g - 1)
            def _():
                finish_dwout()

    hbm = pl.BlockSpec(memory_space=pl.ANY)
    stat = pl.BlockSpec((1, N, 128), lambda g, t: (g, 0, 0), pipeline_mode=pl.Buffered(1))
    piece_f32 = lambda lead: pltpu.VMEM((lead, half, D), F32)
    piece_bf16 = lambda lead: pltpu.VMEM((lead, half, D), BF16)
    return pl.pallas_call(
        body,
        name="attn_bwd",
        grid=(n_g, n_t),
        in_specs=[hbm, hbm, hbm, hbm, stat, stat, _const(N, 128), _const(N, 128), hbm, _const(PW_ROWS, POOL_GROUP)],
        out_specs=[hbm, pl.BlockSpec((hp, N, 256), lambda g, t: (g, 0, 0), pipeline_mode=pl.Buffered(1)), _const(N, 128),
                   _const(SHARD_OUT, D), _const(PW_ROWS, POOL_GROUP)],
        out_shape=[
            jax.ShapeDtypeStruct((HEADS, N, 256), BF16), jax.ShapeDtypeStruct((HEADS, N, 256), BF16),
            jax.ShapeDtypeStruct((N, 128), F32), jax.ShapeDtypeStruct((SHARD_OUT, D), F32),
            jax.ShapeDtypeStruct((PW_ROWS, POOL_GROUP), F32),
        ],
        scratch_shapes=[pltpu.VMEM((hp, N, 256), BF16), pltpu.VMEM((hp, N, 256), BF16), pltpu.VMEM((hp, N, V_HEAD), BF16),
                        pltpu.VMEM((N, hp * V_HEAD), BF16), pltpu.VMEM((2, hp, TQ, 256), BF16),
                        pltpu.VMEM((hp, N, 256), F32), pltpu.VMEM((hp, N, V_HEAD), F32),
                        piece_f32(CHIPS), piece_f32(CHIPS), piece_bf16(3), piece_bf16(3), pltpu.VMEM((SHARD_OUT, D), F32),
                        pltpu.VMEM((PW_ROWS, POOL_GROUP), F32), pltpu.VMEM((PW_ROWS, POOL_GROUP), F32),
                        pltpu.VMEM((3, half_pw, POOL_GROUP), F32), pltpu.VMEM((PW_ROWS, POOL_GROUP), F32),
                        pltpu.SemaphoreType.DMA((4, 2)), pltpu.SemaphoreType.DMA((2,)), pltpu.SemaphoreType.DMA((CHIPS,)),
                        pltpu.SemaphoreType.DMA((CHIPS,)), pltpu.SemaphoreType.DMA((CHIPS,)),
                        pltpu.SemaphoreType.DMA((3,)), pltpu.SemaphoreType.DMA((3,)),
                        pltpu.SemaphoreType.DMA((1,)), pltpu.SemaphoreType.DMA((1,)),
                        pltpu.SemaphoreType.DMA((1,)), pltpu.SemaphoreType.DMA((1,)),
                        pltpu.SemaphoreType.DMA((3,)), pltpu.SemaphoreType.DMA((3,)),
                        pltpu.SemaphoreType.DMA((1,)), pltpu.SemaphoreType.DMA((1,))],
        compiler_params=_cparams(dimension_semantics=("arbitrary", "arbitrary"), collective_id=2),
    )(q, k, v, do, lse, delta, cosf, sinf, dwout, dpw)


def _bwd_in(h, dh2, dq, dkv, dkr, cq, ckv, dpl, dpg, dag, norm_g, win, gq, wq, gkv, wkv, cosf, sinf, adam_out):
    tr = ROWS_BWD
    nb = N // tr
    per = tr // HALO
    lead = HEAD_ROWS
    adam_rows = SHARD_OUT // nb

    def body(h_ref, dh2_ref, dq_ref, dkv_ref, dkr_ref, cq_ref, ckv_ref, dpl_ref, halo_ref, dpg_ref, dag_ref,
             g_ref, win_ref, gq_ref, wq_ref, gkv_ref, wkv_ref, cos_ref, sin_ref, aw_ref, ag_ref, am_ref, av_ref,
             gx_ref, dmeta_ref, dsl_ref, dwq_ref, dwkv_ref, dg_ref, dgq_ref, dgkv_ref, ago_ref, ad_ref, anm_ref, anv_ref,
             dh_buf, gx_sem):
        i = pl.program_id(0)
        grad_out = ag_ref[...]
        ago_ref[...] = grad_out
        ad_ref[...], anm_ref[...], anv_ref[...] = _adamw_math(aw_ref[...], grad_out, am_ref[...], av_ref[...])

        @pl.when(i == 0)
        def _():
            dwq_ref[...] = jnp.zeros_like(dwq_ref)
            dwkv_ref[...] = jnp.zeros_like(dwkv_ref)
            dg_ref[...] = jnp.zeros_like(dg_ref)
            dgq_ref[...] = jnp.zeros_like(dgq_ref)
            dgkv_ref[...] = jnp.zeros_like(dgkv_ref)

        row0 = i * tr
        h = h_ref[...]
        r = lax.rsqrt(jnp.mean(h * h, axis=-1, keepdims=True) + EPS)
        n = h * r
        gv = g_ref[...]
        cq = cq_ref[...]
        rq = lax.rsqrt(jnp.mean(cq * cq, axis=-1, keepdims=True) + EPS)
        nq = cq * rq
        gqv = gq_ref[...]
        cqn = (nq * gqv).astype(BF16)
        dcqn = jnp.zeros((tr, Q_LORA), F32)
        for hd in range(HEADS):
            dqf = dq_ref[hd]
            dcqn = dcqn + _nn(dqf, wq_ref[hd])
            dwq_ref[hd] += _tn(dqf, cqn)
        dgq_ref[...] += jnp.sum(dcqn * nq, axis=0, keepdims=True)
        dnq = dcqn * gqv
        dcq = rq * (dnq - nq * jnp.mean(dnq * nq, axis=-1, keepdims=True))

        ckv = ckv_ref[...]
        rkv = lax.rsqrt(jnp.mean(ckv * ckv, axis=-1, keepdims=True) + EPS)
        nkv = ckv * rkv
        gkvv = gkv_ref[...]
        ckvn = (nkv * gkvv).astype(BF16)
        dckvn = jnp.zeros((tr, KV_LORA), F32)
        for hd in range(HEADS):
            dkv = dkv_ref[hd]
            dckvn = dckvn + _nt(dkv, wkv_ref[hd])
            dwkv_ref[hd] += _tn(ckvn, dkv)
        dgkv_ref[...] += jnp.sum(dckvn * nkv, axis=0, keepdims=True)
        dnkv = dckvn * gkvv
        dckv = rkv * (dnkv - nkv * jnp.mean(dnkv * nkv, axis=-1, keepdims=True))
        dkr = _unrope(dkr_ref[...], cos_ref[...], sin_ref[...])

        cur = dpl_ref[...]
        halo = jnp.where(i < nb - 1, halo_ref[...], 0.0)
        dpi = []
        for g, w in enumerate(POOL_WINDOWS):
            sl = slice(g * POOL_GROUP, (g + 1) * POOL_GROUP)
            a = jnp.concatenate([cur[:, sl] * _inv_count(row0, tr, w), halo[:, sl] * _inv_count(row0 + tr, HALO, w)], axis=0)
            acc = a
            shift = 1
            while shift < w:
                acc = acc + pltpu.roll(acc, tr + HALO - shift, 0)
                shift *= 2
            dpi.append(acc[0:tr] - cur[:, sl])

        du = jnp.concatenate([t.astype(BF16) for t in dpi] + [dpg_ref[...]] + [t.astype(BF16) for t in (dcq, dckv, dkr)],
                             axis=1)
        dagb = dag_ref[...]
        by_row = jnp.concatenate(dpi + [dpg_ref[...].astype(F32), dcq, dckv, dkr[:, 0:QK_ROPE], dagb.astype(F32),
                                        jnp.zeros((tr, SHARD_PAD - SHARD_IN), F32)], axis=1)
        for chip in range(CHIPS):
            dsl_ref[chip] = by_row[:, SHARD_IN * chip:SHARD_IN * chip + SHARD_PAD].astype(BF16)
        dhn = _nn(du, win_ref[0:O_KR_END, :]) + _nn(dagb, win_ref[O_AG:D_IN, :])
        dg_ref[...] += jnp.sum(dhn * n, axis=0, keepdims=True)
        dn = dhn * gv
        dh = dh2_ref[...] + r * (dn - n * jnp.mean(dn * n, axis=-1, keepdims=True))

        first = pltpu.make_async_copy(dh_buf.at[pl.ds(lead, tr - lead), :], gx_ref.at[pl.ds(0, tr - lead), :], gx_sem)
        later = lambda step: pltpu.make_async_copy(
            dh_buf, gx_ref.at[pl.ds(pl.multiple_of(step * tr - lead, 16), tr), :], gx_sem)

        @pl.when(i == 1)
        def _():
            first.wait()

        @pl.when(i > 1)
        def _():
            later(i - 1).wait()

        dh_buf[...] = dh

        @pl.when(i == 0)
        def _():
            first.start()
            for chip in range(CHIPS):
                dmeta_ref[chip] = dh[PAD:HEAD_ROWS, chip * 256:(chip + 1) * 256]

        @pl.when(i > 0)
        def _():
            later(i).start()

        @pl.when(i == nb - 1)
        def _():
            later(i).wait()

    head = lambda w: pl.BlockSpec((HEADS, tr, w), lambda i: (0, i, 0))
    halo_spec = pl.BlockSpec((HALO, D_POOL), lambda i: (jnp.minimum((i + 1) * per, N // HALO - 1), 0))
    return pl.pallas_call(
        body,
        name="bwd_in",
        grid=(nb,),
        in_specs=[
            _rows(D, tr), _rows(D, tr), head(256), head(256), _rows(128, tr), _rows(Q_LORA, tr), _rows(KV_LORA, tr),
            _rows(D_POOL, tr), halo_spec, _rows(D_POOL, tr), _rows(D_POOL, tr),
            _const(1, D), _const(D_IN, D), _const(1, Q_LORA), _const(HEADS, 256, Q_LORA),
            _const(1, KV_LORA), _const(HEADS, KV_LORA, 256), _rows(128, tr), _rows(128, tr),
        ] + [_rows(D, adam_rows)] * 4,
        out_specs=[
            pl.BlockSpec(memory_space=pl.ANY), _const(CHIPS, N_META, 256),
            pl.BlockSpec((CHIPS, tr, SHARD_PAD), lambda i: (0, i, 0)), _const(HEADS, 256, Q_LORA),
            _const(HEADS, KV_LORA, 256), _const(1, D), _const(1, Q_LORA), _const(1, KV_LORA),
        ] + [_rows(D, adam_rows)] * 4,
        out_shape=[
            jax.ShapeDtypeStruct((S, D), F32), jax.ShapeDtypeStruct((CHIPS, N_META, 256), F32),
            jax.ShapeDtypeStruct((CHIPS, N, SHARD_PAD), BF16), jax.ShapeDtypeStruct((HEADS, 256, Q_LORA), F32),
            jax.ShapeDtypeStruct((HEADS, KV_LORA, 256), F32),
            jax.ShapeDtypeStruct((1, D), F32), jax.ShapeDtypeStruct((1, Q_LORA), F32), jax.ShapeDtypeStruct((1, KV_LORA), F32),
        ] + [jax.ShapeDtypeStruct((SHARD_OUT, D), F32)] * 4,
        scratch_shapes=[pltpu.VMEM((tr, D), F32), pltpu.SemaphoreType.DMA],
        compiler_params=_cparams(dimension_semantics=("arbitrary",)),
    )(h, dh2, dq, dkv, dkr, cq, ckv, dpl, dpl, dpg, dag, norm_g, win, gq, wq, gkv, wkv, cosf, sinf, *adam_out)


def _local_step(h, tgt, norm_g, win, gq, wq, gkv, wkv, pool_w, pool_scale, wout_s, m_wout_s, v_wout_s, gf, cosf, sinf):
    pool_in, pool_gate, cq, ckv, attn_gate, q, k, v, hn = _fwd_in(h, norm_g, win, gq, wq, gkv, wkv, cosf, sinf)
    attn, lse, wout = _attn_fwd(q, k, v, wout_s)
    dh2, do, delta, dag, dpg, dpl, dwout, dpw, dps, dgf, loss = _mid(
        h, tgt, pool_in, pool_gate, attn_gate, attn, pool_w, pool_scale, wout, gf)
    dq, dkv, dkr, gwout, gpw = _attn_bwd(q, k, v, do, lse, delta, cosf, sinf, dwout, dpw.reshape(PW_ROWS, POOL_GROUP))
    gx, dmeta, dsl, dwq, dwkv, dg, dgq, dgkv, *r_out = _bwd_in(
        h, dh2, dq, dkv, dkr, cq, ckv, dpl, dpg, dag, norm_g, win, gq, wq, gkv, wkv, cosf, sinf,
        (wout_s, gwout, m_wout_s, v_wout_s))
    return dict(gx=gx, dmeta=dmeta, dsl=dsl, hn=hn, dwq=dwq, dwkv=dwkv, r_out=tuple(r_out), dg=dg, dgq=dgq,
                dgkv=dgkv, gpw=gpw, dps=dps, dgf=dgf, loss=loss)


_CHIP_RELS = ((0, 0), (1, 0), (0, 1), (1, 1))

_ARR_ROWS = (SHARD_IN, SHARD_OUT, 256, KV_LORA, N_META)
_ARR_COLS = (D, D, Q_LORA, 256, 256)
_PIECES = (
    (0, 0, 256, 0), (0, 256, SHARD_IN - 256, 1),
    (1, 0, 128, 0), (1, 128, 128, 1),
    (2, 0, 128, 0), (2, 128, 128, 1),
    (3, 0, 64, 0), (3, 64, 64, 1),
    (4, 0, N_META, 0),
)
_NP = len(_PIECES)
_PIECE_MAX = (256, 128, 128, 64, N_META)


def _gathered_at(refs, arr, chip, r0, n):
    if arr in (0, 1):
        return refs[arr].at[pl.ds(pl.multiple_of(_ARR_ROWS[arr] * chip + r0, 16), n), :]
    return refs[arr].at[chip, pl.ds(r0, n), :]


def _remote(src, dst, send_sem, recv_sem, to):
    return pltpu.make_async_remote_copy(src_ref=src, dst_ref=dst, send_sem=send_sem, recv_sem=recv_sem,
                                        device_id=to, device_id_type=MESH)


def _gather_weights(winT_s, wqT_s, wkv_s, meta_s, x2, tgt2):
    arrays = (0, 2, 3, 4)

    def body(win_ref, wq_ref, wkv_ref, meta_ref, x_ref, t_ref, win_o, wq_o, wkv_o, h_o, tp_o,
             s_win, s_wq, s_wkv, meta_all, head_buf, x_buf, t_buf, ici_send, ici_recv, fwd_send, fwd_recv,
             loc_sems, own_sems):
        x, y, c = lax.axis_index("x"), lax.axis_index("y"), lax.axis_index("c")
        me = 2 * x + y
        stage = (s_win, None, s_wq, s_wkv, meta_ref)
        outs = (win_o, None, wq_o, wkv_o, meta_all)

        _peer_signal(x, y, c)

        frames = pl.ds(HEAD_ROWS, S)
        loads = [pltpu.make_async_copy(x_ref, x_buf, loc_sems.at[0]), pltpu.make_async_copy(t_ref, t_buf, loc_sems.at[1])]
        local = [pltpu.make_async_copy(x_buf, h_o.at[frames, :], loc_sems.at[0]),
                 pltpu.make_async_copy(t_buf, tp_o.at[frames, :], loc_sems.at[1])]
        for cp in loads:
            cp.start()

        s_win[...] = win_ref[...].astype(BF16)
        s_wq[0:QK, :] = wq_ref[...].astype(BF16)
        s_wq[QK:256, :] = jnp.zeros((256 - QK, Q_LORA), BF16)
        s_wkv[...] = wkv_ref[...].astype(BF16)
        head_buf[...] = jnp.zeros_like(head_buf)
        zeros = pltpu.make_async_copy(head_buf, tp_o.at[pl.ds(0, HEAD_ROWS), :], loc_sems.at[2])
        zeros.start()

        def chip_of(rel):
            fx, fy = _CHIP_RELS[rel]
            return 2 * (x ^ fx) + (y ^ fy)

        def same_core_of(rel):
            fx, fy = _CHIP_RELS[rel]
            return (x ^ fx, y ^ fy, c)

        def ici_copy(rel, i, src_chip, to):
            arr, r0, n, _ = _PIECES[i]
            k = (rel - 1) * _NP + i
            return _remote(stage[arr].at[pl.ds(r0, n), :], _gathered_at(outs, arr, src_chip, r0, n),
                           ici_send.at[k], ici_recv.at[k], to)

        def fwd_copy(rel, i, to):
            arr, r0, n, _ = _PIECES[i]
            k = (rel - 1) * _NP + i
            place = _gathered_at(outs, arr, chip_of(rel), r0, n)
            return _remote(place, place, fwd_send.at[k], fwd_recv.at[k], to)

        _peer_wait()
        for core in (0, 1):
            @pl.when(c == core)
            def _(core=core):
                mine = [i for i in range(_NP) if _PIECES[i][3] == core and _PIECES[i][0] in arrays]
                theirs = [i for i in range(_NP) if _PIECES[i][3] != core and _PIECES[i][0] in arrays]
                order = (1, 2, 3)
                sends = [ici_copy(rel, i, me, same_core_of(rel)) for rel in order for i in mine]
                for cp in sends:
                    cp.start()
                for ld, st in zip(loads, local):
                    ld.wait()
                    st.start()
                own = [pltpu.make_async_copy(stage[arr], _gathered_at(outs, arr, me, 0, _ARR_ROWS[arr]), own_sems.at[arr])
                       for arr in arrays if arr != 4]
                for cp in own:
                    cp.start()
                meta_all[me] = meta_ref[...]
                for rel in order:
                    for i in mine:
                        ici_copy(rel, i, chip_of(rel), (x, y, c)).wait_recv()
                        fwd = fwd_copy(rel, i, (x, y, 1 - c))
                        fwd.start()
                        sends.append(fwd)
                for rel in order:
                    for i in theirs:
                        fwd_copy(rel, i, (x, y, c)).wait_recv()
                for cp in sends:
                    cp.wait_send()
                for cp in own:
                    cp.wait()

        zeros.wait()
        for chip in range(CHIPS):
            head_buf[PAD:HEAD_ROWS, chip * 256:(chip + 1) * 256] = meta_all[chip]
        head = pltpu.make_async_copy(head_buf, h_o.at[pl.ds(0, HEAD_ROWS), :], loc_sems.at[2])
        head.start()
        head.wait()
        for cp in local:
            cp.wait()

    vm = pl.BlockSpec(memory_space=pltpu.VMEM)
    hbm = pl.BlockSpec(memory_space=pl.ANY)
    return pl.pallas_call(
        body,
        name="gather_weights",
        in_specs=[vm] * 4 + [hbm] * 2,
        out_specs=[hbm] * 5,
        out_shape=[
            jax.ShapeDtypeStruct((D_IN, D), BF16),
            jax.ShapeDtypeStruct((CHIPS, 256, Q_LORA), BF16), jax.ShapeDtypeStruct((CHIPS, KV_LORA, 256), BF16),
            jax.ShapeDtypeStruct((N, D), F32), jax.ShapeDtypeStruct((N, D), F32),
        ],
        scratch_shapes=[pltpu.VMEM((_ARR_ROWS[a], _ARR_COLS[a]), BF16) for a in (0, 2, 3)]
        + [pltpu.VMEM((CHIPS, N_META, 256), F32), pltpu.VMEM((HEAD_ROWS, D), F32), pltpu.VMEM((S, D), F32),
           pltpu.VMEM((S, D), F32)]
        + [pltpu.SemaphoreType.DMA((3 * _NP,))] * 4 + [pltpu.SemaphoreType.DMA((3,)), pltpu.SemaphoreType.DMA((4,))],
        compiler_params=_cparams(collective_id=0),
    )(winT_s, wqT_s, wkv_s, meta_s, x2, tgt2)


_SM_ROWS = (VEC_ROWS,)
_SM_COLS = (D,)
_SM_PIECES = ((0, 0, VEC_ROWS, 0),)
_NSP = len(_SM_PIECES)
_NSB = len(_SM_ROWS)


def _reduce_grads(dsl, hn, dwq, dwkv, dmeta4, dg, dgf, dgq, dgkv, dps, loss):
    arrays = (0, 2, 3, 4)
    loaded = (2, 3, 4)
    shard_order = SEND_ORDER + (0,)

    def body(dsl_hbm, hn_hbm, dwq_ref, dwkv_ref, dmeta_ref, dg_ref, dgf_ref, dgq_ref, dgkv_ref, dps_ref,
             loss_ref, gwin_o, gwq_o, gwkv_o, gmeta_o, gg_o, ggf_o, ggq_o, ggkv_o, gps_o, gloss_o,
             ow2, ow3, ow4, sb0, sb2, sb3, sb4, st0, st2, st3, st4, rc0, rc2, rc3, rc4,
             vec, sm_sb0, sm_cs0, sm_rc0, vec_fin, slab_v, hn_v, dwin_buf, own0,
             own_sems, d2d_send, d2d_recv, ici_send, ici_recv, fin_send, fin_recv,
             swap_send, swap_recv, smi_send, smi_recv, smf_send, smf_recv, ld_sems):
        x, y, c = lax.axis_index("x"), lax.axis_index("y"), lax.axis_index("c")
        me = 2 * x + y
        _peer_signal(x, y, c)
        grads = (None, None, dwq_ref, dwkv_ref, dmeta_ref)
        outs = (gwin_o, None, gwq_o, gwkv_o, gmeta_o)
        own_buf = (None, None, ow2, ow3, ow4)
        sib_buf = (sb0, None, sb2, sb3, sb4)
        stage = (st0, None, st2, st3, st4)
        recv = (rc0, None, rc2, rc3, rc4)
        sm_mine = (vec,)
        sm_sib = (sm_sb0,)
        sm_chip = (sm_cs0,)
        sm_recv = (sm_rc0,)
        sm_out = (vec_fin,)
        sibling = (x, y, 1 - c)

        def chip_of(rel):
            fx, fy = _CHIP_RELS[rel]
            return 2 * (x ^ fx) + (y ^ fy)

        def same_core_of(rel):
            fx, fy = _CHIP_RELS[rel]
            return (x ^ fx, y ^ fy, c)

        hn_load = pltpu.make_async_copy(hn_hbm, hn_v, ld_sems.at[CHIPS])

        def slab_load(rel):
            return pltpu.make_async_copy(dsl_hbm.at[chip_of(rel)], slab_v.at[rel], ld_sems.at[rel])

        hn_load.start()
        slab_load(shard_order[0]).start()

        def slot(bufs, i, idx):
            arr, _, n, _ = _PIECES[i]
            return bufs[arr].at[idx, pl.ds(0, n), :]

        def own_load(rel, i):
            arr, r0, n, _ = _PIECES[i]
            return pltpu.make_async_copy(_gathered_at(grads, arr, chip_of(rel), r0, n), slot(own_buf, i, rel),
                                         own_sems.at[rel * _NP + i])

        def d2d_copy(rel, i):
            arr, r0, n, _ = _PIECES[i]
            k = rel * _NP + i
            return _remote(_gathered_at(grads, arr, chip_of(rel), r0, n), slot(sib_buf, i, rel),
                           d2d_send.at[k], d2d_recv.at[k], sibling)

        def ici_copy(rel, i):
            k = (rel - 1) * _NP + i
            return _remote(slot(stage, i, rel - 1), slot(recv, i, rel - 1), ici_send.at[k], ici_recv.at[k],
                           same_core_of(rel))

        def fin_copy(i):
            arr, r0, n, _ = _PIECES[i]
            place = outs[arr].at[pl.ds(r0, n), :]
            return _remote(place, place, fin_send.at[i], fin_recv.at[i], sibling)

        def sm_ici_copy(rel, j):
            blk, r0, n, _ = _SM_PIECES[j]
            k = (rel - 1) * _NSP + j
            return _remote(sm_chip[blk].at[pl.ds(r0, n), :], sm_recv[blk].at[rel - 1, pl.ds(r0, n), :],
                           smi_send.at[k], smi_recv.at[k], same_core_of(rel))

        def sm_fin_copy(j):
            blk, r0, n, _ = _SM_PIECES[j]
            place = sm_out[blk].at[pl.ds(r0, n), :]
            return _remote(place, place, smf_send.at[j], smf_recv.at[j], sibling)

        vec[...] = jnp.zeros_like(vec)
        vec[0:1, :] = dg_ref[...]
        vec[1:2, :] = dgf_ref[...]
        vec[2:3, V_GQ:V_GQ + Q_LORA] = dgq_ref[...]
        vec[2:3, V_GKV:V_GKV + KV_LORA] = dgkv_ref[...]
        vec[2:3, V_PS:V_PS + D_POOL] = dps_ref[...]
        vec[2:3, V_LOSS:D] = loss_ref[...]
        _peer_wait()
        swaps = [_remote(sm_mine[b], sm_sib[b], swap_send.at[b], swap_recv.at[b], sibling) for b in range(_NSB)]
        for cp in swaps:
            cp.start()

        for core in (0, 1):
            @pl.when(c == core)
            def _(core=core):
                mine = [i for i in range(_NP) if _PIECES[i][3] == core and _PIECES[i][0] in loaded]
                theirs = [i for i in range(_NP) if _PIECES[i][3] != core and _PIECES[i][0] in loaded]
                i0 = next(i for i in range(_NP) if _PIECES[i][0] == 0 and _PIECES[i][3] == core)
                j0 = next(i for i in range(_NP) if _PIECES[i][0] == 0 and _PIECES[i][3] != core)
                sm_mine_p = [j for j in range(_NSP) if _SM_PIECES[j][3] == core]
                sm_theirs_p = [j for j in range(_NSP) if _SM_PIECES[j][3] != core]
                sends = list(swaps)

                for rel in shard_order:
                    for i in theirs:
                        cp = d2d_copy(rel, i)
                        cp.start()
                        sends.append(cp)
                    for i in mine:
                        own_load(rel, i).start()

                def piece_rows(i):
                    return pl.ds(_PIECES[i][1], _PIECES[i][2])

                def form(rel, i):
                    r0, n = _PIECES[i][1], _PIECES[i][2]
                    dwin_buf[rel, r0:r0 + n, :] = _tn(slab_v[rel, :, r0:r0 + _PIECE_MAX[0]], hn_v[...])[0:n, :]

                def d2d0(rel, i):
                    return _remote(dwin_buf.at[rel, piece_rows(i), :], slot(sib_buf, i, rel),
                                   d2d_send.at[rel * _NP + i], d2d_recv.at[rel * _NP + i], sibling)

                def settle(rel):
                    d2d0(rel, i0).wait_recv()
                    total = dwin_buf[rel, piece_rows(i0), :] + slot(sib_buf, i0, rel)[...]
                    if rel == 0:
                        own0[0:_PIECES[i0][2], :] = total
                    else:
                        slot(stage, i0, rel - 1)[...] = total.astype(BF16)
                        cp = ici_copy(rel, i0)
                        cp.start()
                        sends.append(cp)

                for rel in SEND_ORDER:
                    for i in mine:
                        arr, r0, n, _ = _PIECES[i]
                        own_load(rel, i).wait()
                        d2d_copy(rel, i).wait_recv()
                        total = slot(own_buf, i, rel)[...] + slot(sib_buf, i, rel)[...]
                        slot(stage, i, rel - 1)[...] = total.astype(stage[arr].dtype)
                        cp = ici_copy(rel, i)
                        cp.start()
                        sends.append(cp)

                for b in range(_NSB):
                    swaps[b].wait_recv()
                    sm_chip[b][...] = sm_mine[b][...] + sm_sib[b][...]
                for rel in SEND_ORDER:
                    for j in sm_mine_p:
                        cp = sm_ici_copy(rel, j)
                        cp.start()
                        sends.append(cp)

                hn_load.wait()
                for n, rel in enumerate(shard_order):
                    slab_load(rel).wait()
                    if n == 0:
                        for later in shard_order[1:]:
                            slab_load(later).start()
                    form(rel, j0)
                    cp = d2d0(rel, j0)
                    cp.start()
                    sends.append(cp)
                    form(rel, i0)
                    settle(rel)

                for i in mine:
                    arr, r0, n, _ = _PIECES[i]
                    own_load(0, i).wait()
                    d2d_copy(0, i).wait_recv()
                    total = slot(own_buf, i, 0)[...] + slot(sib_buf, i, 0)[...]
                    for rel in (1, 2, 3):
                        ici_copy(rel, i).wait_recv()
                        total = total + slot(recv, i, rel - 1)[...].astype(F32)
                    outs[arr][pl.ds(r0, n), :] = total
                    cp = fin_copy(i)
                    cp.start()
                    sends.append(cp)
                total = own0[0:_PIECES[i0][2], :]
                for rel in (1, 2, 3):
                    ici_copy(rel, i0).wait_recv()
                    total = total + slot(recv, i0, rel - 1)[...].astype(F32)
                outs[0][pl.ds(_PIECES[i0][1], _PIECES[i0][2]), :] = total
                cp = fin_copy(i0)
                cp.start()
                sends.append(cp)

                for j in sm_mine_p:
                    blk, r0, n, _ = _SM_PIECES[j]
                    for rel in (1, 2, 3):
                        sm_ici_copy(rel, j).wait_recv()
                    total = jnp.zeros((n, _SM_COLS[blk]), F32)
                    for chip in range(CHIPS):
                        flips = chip ^ me
                        rel = jnp.where(flips == 2, 1, jnp.where(flips == 1, 2, flips))
                        theirs_rows = sm_recv[blk][jnp.maximum(rel - 1, 0), pl.ds(r0, n), :]
                        total = total + jnp.where(rel == 0, sm_chip[blk][pl.ds(r0, n), :], theirs_rows)
                    sm_out[blk][pl.ds(r0, n), :] = total
                    cp = sm_fin_copy(j)
                    cp.start()
                    sends.append(cp)

                for i in theirs + [j0]:
                    fin_copy(i).wait_recv()
                for j in sm_theirs_p:
                    sm_fin_copy(j).wait_recv()
                for cp in sends:
                    cp.wait_send()

        gg_o[...] = vec_fin[0:1, :]
        ggf_o[...] = vec_fin[1:2, :]
        ggq_o[...] = vec_fin[2:3, V_GQ:V_GQ + Q_LORA]
        ggkv_o[...] = vec_fin[2:3, V_GKV:V_GKV + KV_LORA]
        gps_o[...] = vec_fin[2:3, V_PS:V_PS + D_POOL]
        gloss_o[...] = vec_fin[2:3, V_LOSS:D]

    vm = pl.BlockSpec(memory_space=pltpu.VMEM)
    piece_buf = lambda lead, dtype, which=arrays: [
        pltpu.VMEM((lead, _PIECE_MAX[a], _ARR_COLS[a]), F32 if a == 4 else dtype) for a in which]
    sm_buf = lambda *lead: [pltpu.VMEM(lead + (_SM_ROWS[b], _SM_COLS[b]), F32) for b in range(_NSB)]
    dma = lambda n: [pltpu.SemaphoreType.DMA((n,))] * 2
    return pl.pallas_call(
        body,
        name="reduce_grads",
        in_specs=[pl.BlockSpec(memory_space=pl.ANY)] * 4 + [vm] * 7,
        out_specs=[vm] * 10,
        out_shape=[jax.ShapeDtypeStruct((_ARR_ROWS[a], _ARR_COLS[a]), F32) for a in arrays]
        + [jax.ShapeDtypeStruct((1, D), F32),
           jax.ShapeDtypeStruct((1, D), F32), jax.ShapeDtypeStruct((1, Q_LORA), F32),
           jax.ShapeDtypeStruct((1, KV_LORA), F32), jax.ShapeDtypeStruct((1, D_POOL), F32),
           jax.ShapeDtypeStruct((1, 128), F32)],
        scratch_shapes=piece_buf(CHIPS, F32, loaded) + piece_buf(CHIPS, F32) + piece_buf(3, BF16) + piece_buf(3, BF16)
        + [pltpu.VMEM((VEC_ROWS, D), F32)] + sm_buf() + sm_buf() + sm_buf(3) + [pltpu.VMEM((VEC_ROWS, D), F32)]
        + [pltpu.VMEM((CHIPS, N, SHARD_PAD), BF16), pltpu.VMEM((N, D), BF16),
           pltpu.VMEM((CHIPS, SHARD_PAD, D), F32), pltpu.VMEM((_PIECE_MAX[0], D), F32)]
        + [pltpu.SemaphoreType.DMA((CHIPS * _NP,))]
        + dma(CHIPS * _NP) + dma(3 * _NP) + dma(_NP) + dma(_NSB) + dma(3 * _NSP) + dma(_NSP)
        + [pltpu.SemaphoreType.DMA((CHIPS + 1,))],
        compiler_params=_cparams(collective_id=3),
    )(dsl, hn, dwq, dwkv, dmeta4, dg, dgf, dgq, dgkv, dps, loss)


def _adamw_math(w, g, m, v):
    m = B1 * m + (1.0 - B1) * g
    v = B2 * v + (1.0 - B2) * (g * g)
    m_hat = m / C1
    v_hat = v / C2
    delta = -LR * (m_hat / (jnp.sqrt(v_hat) + ADAM_EPS) + WD * w)
    return delta, m, v


def _adamw(big, block_rows, groups):
    rows, cols = big[0].shape
    n = len(groups)

    def body(*refs):
        w_ref, g_ref, m_ref, v_ref = refs[0:4]
        small_in = refs[4:4 + 4 * n]
        go_ref, d_ref, nm_ref, nv_ref = refs[4 + 4 * n:8 + 4 * n]
        small_out = refs[8 + 4 * n:]
        g = g_ref[...]
        go_ref[...] = g
        d_ref[...], nm_ref[...], nv_ref[...] = _adamw_math(w_ref[...], g, m_ref[...], v_ref[...])

        @pl.when(pl.program_id(0) == 0)
        def _():
            for t in range(n):
                sw_ref, sg_ref, sm_ref, sv_ref = small_in[4 * t:4 * t + 4]
                sg = sg_ref[0:sw_ref.shape[0], :]
                small_out[4 * t][...] = sg
                small_out[4 * t + 1][...], small_out[4 * t + 2][...], small_out[4 * t + 3][...] = _adamw_math(
                    sw_ref[...], sg, sm_ref[...], sv_ref[...])

    spec = pl.BlockSpec((block_rows, cols), lambda i: (i, 0))
    vm = pl.BlockSpec(memory_space=pltpu.VMEM)
    outs = pl.pallas_call(
        body,
        name="adamw",
        grid=(rows // block_rows,),
        in_specs=[spec] * 4 + [vm] * (4 * n),
        out_specs=[spec] * 4 + [vm] * (4 * n),
        out_shape=[jax.ShapeDtypeStruct(big[0].shape, F32)] * 4
        + [jax.ShapeDtypeStruct(grp[0].shape, F32) for grp in groups for _ in range(4)],
        compiler_params=_cparams(dimension_semantics=("arbitrary",)),
    )(*big, *[a for grp in groups for a in grp])
    return tuple(outs[0:4]), [tuple(outs[4 + 4 * t:8 + 4 * t]) for t in range(n)]


def _rope_tables():
    half = QK_ROPE // 2
    f32 = np.float32
    inv_freq = (f32(1.0) / (f32(ROPE_THETA) ** (np.arange(half, dtype=f32) / f32(half)))).astype(f32)
    pos = np.arange(N, dtype=f32) - f32(PAD)
    ang = (pos[:, None] * inv_freq[None, :]).astype(f32)
    cos, sin = np.cos(ang).astype(f32), np.sin(ang).astype(f32)
    zero = np.zeros((N, 128 - QK_ROPE), f32)
    return jnp.asarray(np.concatenate([cos, cos, zero], axis=1)), jnp.asarray(np.concatenate([-sin, sin, zero], axis=1))


def kernel(x, meta_tokens, norm_g, w_in, q_norm_g, w_q_b, kv_norm_g, w_kv_b, pool_w, pool_scale, w_out, final_norm_g, loss_target, m_meta_tokens, m_norm_g, m_w_in, m_q_norm_g, m_w_q_b, m_kv_norm_g, m_w_kv_b, m_pool_w, m_pool_scale, m_w_out, m_final_norm_g, v_meta_tokens, v_norm_g, v_w_in, v_q_norm_g, v_w_q_b, v_kv_norm_g, v_w_kv_b, v_pool_w, v_pool_scale, v_w_out, v_final_norm_g):
    tr = lambda a: a[0].T
    win, wq, wkv, h, tgt = _gather_weights(tr(w_in), tr(w_q_b), w_kv_b[0], meta_tokens, x[0], loss_target[0])
    cosf, sinf = _rope_tables()
    gf = final_norm_g.reshape(1, D)

    part = _local_step(h, tgt, norm_g, win, q_norm_g, wq, kv_norm_g, wkv, pool_w[0], pool_scale, w_out[0], m_w_out[0],
                       v_w_out[0], gf, cosf, sinf)

    pw2 = lambda a: a.reshape(len(POOL_WINDOWS) * POOL_GROUP, POOL_GROUP)
    gpw = part["gpw"]
    gwinT, gwqT, gwkv, gmeta, gg, ggf, ggq, ggkv, gps, gloss = _reduce_grads(
        part["dsl"], part["hn"], part["dwq"], part["dwkv"], part["dmeta"], part["dg"],
        part["dgf"], part["dgq"], part["dgkv"], part["dps"], part["loss"])

    r_out = part["r_out"]
    fn2 = lambda a: a.reshape(1, D)
    r_in, (r_meta, r_norm, r_gq, r_wq, r_gkv, r_wkv, r_pw, r_ps, r_fn) = _adamw((tr(w_in), gwinT, tr(m_w_in), tr(v_w_in)), 248, [
        (meta_tokens, gmeta, m_meta_tokens, v_meta_tokens),
        (norm_g, gg, m_norm_g, v_norm_g),
        (q_norm_g, ggq, m_q_norm_g, v_q_norm_g),
        (tr(w_q_b), gwqT, tr(m_w_q_b), tr(v_w_q_b)),
        (kv_norm_g, ggkv, m_kv_norm_g, v_kv_norm_g),
        (w_kv_b[0], gwkv, m_w_kv_b[0], v_w_kv_b[0]),
        (pw2(pool_w), gpw, pw2(m_pool_w), pw2(v_pool_w)),
        (pool_scale, gps, m_pool_scale, v_pool_scale),
        (fn2(final_norm_g), ggf, fn2(m_final_norm_g), fn2(v_final_norm_g)),
    ])
    untr = lambda a: a.T[None]
    pw4 = lambda a: a.reshape(1, len(POOL_WINDOWS), POOL_GROUP, POOL_GROUP)
    per_kind = [[
        r_meta[kind], r_norm[kind], untr(r_in[kind]), r_gq[kind], untr(r_wq[kind]), r_gkv[kind], r_wkv[kind][None],
        pw4(r_pw[kind]), r_ps[kind], r_out[kind][None], r_fn[kind].reshape(D),
    ] for kind in range(4)]
    return (gloss[0, 0], part["gx"][None], *per_kind[0], *per_kind[1], *per_kind[2], *per_kind[3])
```

```python
import jax
import jax.numpy as jnp
import numpy as np
from jax import lax
from jax.experimental import pallas as pl
from jax.experimental.pallas import tpu as pltpu

F32 = jnp.float32
BF16 = jnp.bfloat16

D = 1024
S = 2048
N_META = 16
PAD = 112
HEAD_ROWS = PAD + N_META
N = HEAD_ROWS + S
D_POOL = 512
POOL_WINDOWS = (2, 4, 8, 16)
POOL_GROUP = 128
HALO = 16
HEADS = 4
QK_NOPE = 128
QK_ROPE = 64
QK = QK_NOPE + QK_ROPE
V_HEAD = 128
Q_LORA = 256
KV_LORA = 128
D_IN = 1984
EPS = 1e-6
ROPE_THETA = 10000.0
SCALE = QK ** -0.5
CHIPS = 4

ROWS_FWD = 544
ROWS_MID = 544
ROWS_BWD = 544
TK = 128
TQ = 256
NQ = S // TQ
HEADS_PER_STEP_BWD = 2

O_PI, O_PG, O_CQ, O_CKV, O_KR, O_AG = 0, 512, 1024, 1280, 1408, 1472
O_KR_END = O_KR + 128
SHARD_IN = D_IN // CHIPS
SHARD_PAD = 512
SHARD_OUT = D // CHIPS

LR, B1, B2, ADAM_EPS, WD, STEP = 0.001, 0.9, 0.999, 1e-08, 0.01, 10
C1 = 1.0 - B1**STEP
C2 = 1.0 - B2**STEP

VMEM_LIMIT = 60 * 1024 * 1024
MESH = pl.DeviceIdType.MESH
NEG = -1e30

VEC_ROWS = 8
PW_ROWS = len(POOL_WINDOWS) * POOL_GROUP
V_GQ, V_GKV, V_PS, V_LOSS = 0, 256, 384, 896


def _cparams(**kw):
    return pltpu.CompilerParams(vmem_limit_bytes=VMEM_LIMIT, **kw)


def _nt(a, b):
    return lax.dot_general(a, b, (((1,), (1,)), ((), ())), preferred_element_type=F32)


def _tn(a, b):
    return lax.dot_general(a, b, (((0,), (0,)), ((), ())), preferred_element_type=F32)


def _nn(a, b):
    return jnp.dot(a, b, preferred_element_type=F32)


def _swap64(t):
    return pltpu.roll(t, 32, 1) + pltpu.roll(t, 96, 1)


def _sigmoid(x):
    return 1.0 / (1.0 + jnp.exp(-x))


def _low_lanes():
    return (lax.broadcasted_iota(jnp.int32, (1, 128), 1) < QK_ROPE).astype(F32)


def _rows(w, rows):
    return pl.BlockSpec((rows, w), lambda i: (i, 0))


def _const(*shape):
    return pl.BlockSpec(shape, lambda *_: (0,) * len(shape), pipeline_mode=pl.Buffered(1))


STAT_GROUPS = HEADS // HEADS_PER_STEP_BWD


def _stat_slot(head):
    return head // HEADS_PER_STEP_BWD, head % HEADS_PER_STEP_BWD


N_PEERS = 4
SEND_ORDER = (3, 1, 2)


def _peer_signal(x, y, c):
    barrier = pltpu.get_barrier_semaphore()
    peers = [(x, y, 1 - c)] + [(x ^ fx, y ^ fy, c) for fx, fy in _CHIP_RELS[1:]]
    assert len(peers) == N_PEERS
    for peer in peers:
        pl.semaphore_signal(barrier, inc=1, device_id=peer, device_id_type=MESH)


def _peer_wait():
    pl.semaphore_wait(pltpu.get_barrier_semaphore(), N_PEERS)


def _attn_tiles():
    return [(0, TK, TK)] + [(TK + TQ * t, TQ, TK + TQ * (t + 1)) for t in range(NQ)]


def _masked_scores(q, k, rows, klen):
    s = _nt(q, k)
    col = lax.broadcasted_iota(jnp.int32, (1, TK), 1)
    head_bias = jnp.where(col >= PAD, 0.0, NEG)
    if klen == TK:
        return s + head_bias
    r = lax.broadcasted_iota(jnp.int32, (rows, 1), 0) >> 6
    c = lax.broadcasted_iota(jnp.int32, (1, rows), 1) >> 6
    diag_bias = jnp.where(c <= r, 0.0, NEG)
    parts = [s[:, 0:TK] + head_bias]
    if klen - rows > TK:
        parts.append(s[:, TK:klen - rows])
    parts.append(s[:, klen - rows:klen] + diag_bias)
    return jnp.concatenate(parts, axis=1)


def _fwd_in(h, norm_g, win, gq, wq, gkv, wkv, cosf, sinf):
    tr = ROWS_FWD

    def body(h_ref, g_ref, win_ref, gq_ref, wq_ref, gkv_ref, wkv_ref, cos_ref, sin_ref,
             pi_ref, pg_ref, cq_ref, ckv_ref, ag_ref, q_ref, k_ref, v_ref, hn_ref):
        h = h_ref[...]
        r = lax.rsqrt(jnp.mean(h * h, axis=-1, keepdims=True) + EPS)
        hn = ((h * r) * g_ref[...]).astype(BF16)
        hn_ref[...] = hn
        u = _nt(hn, win_ref[0:O_KR_END, :])
        pi_ref[...] = u[:, O_PI:O_PG]
        pg_ref[...] = u[:, O_PG:O_CQ]
        cq = u[:, O_CQ:O_CKV]
        ckv = u[:, O_CKV:O_KR]
        cq_ref[...] = cq
        ckv_ref[...] = ckv
        ag_ref[...] = _nt(hn, win_ref[O_AG:D_IN, :])
        cosv = cos_ref[...]
        sinv = sin_ref[...]
        kr = u[:, O_KR:O_KR_END] * _low_lanes()
        kr = (kr * cosv + _swap64(kr) * sinv).astype(BF16)
        rq = lax.rsqrt(jnp.mean(cq * cq, axis=-1, keepdims=True) + EPS)
        cqn = ((cq * rq) * gq_ref[...]).astype(BF16)
        rkv = lax.rsqrt(jnp.mean(ckv * ckv, axis=-1, keepdims=True) + EPS)
        ckvn = ((ckv * rkv) * gkv_ref[...]).astype(BF16)
        for hd in range(HEADS):
            qh = _nt(cqn, wq_ref[hd]) * SCALE
            z = qh[:, QK_NOPE:]
            q_ref[hd, :, 0:QK_NOPE] = qh[:, 0:QK_NOPE].astype(BF16)
            q_ref[hd, :, QK_NOPE:] = (z * cosv + _swap64(z) * sinv).astype(BF16)
            kvh = _nn(ckvn, wkv_ref[hd])
            k_ref[hd, :, 0:QK_NOPE] = kvh[:, 0:QK_NOPE].astype(BF16)
            k_ref[hd, :, QK_NOPE:] = kr
            v_ref[hd] = kvh[:, QK_NOPE:].astype(BF16)

    head = lambda w: pl.BlockSpec((HEADS, tr, w), lambda i: (0, i, 0))
    return pl.pallas_call(
        body,
        name="fwd_in",
        grid=(N // tr,),
        in_specs=[
            _rows(D, tr), _const(1, D), _const(D_IN, D), _const(1, Q_LORA), _const(HEADS, 256, Q_LORA),
            _const(1, KV_LORA), _const(HEADS, KV_LORA, 256), _rows(128, tr), _rows(128, tr),
        ],
        out_specs=[_rows(D_POOL, tr), _rows(D_POOL, tr), _rows(Q_LORA, tr), _rows(KV_LORA, tr), _rows(D_POOL, tr),
                   head(256), head(256), head(V_HEAD), _rows(D, tr)],
        out_shape=[
            jax.ShapeDtypeStruct((N, D_POOL), F32), jax.ShapeDtypeStruct((N, D_POOL), F32),
            jax.ShapeDtypeStruct((N, Q_LORA), F32), jax.ShapeDtypeStruct((N, KV_LORA), F32),
            jax.ShapeDtypeStruct((N, D_POOL), F32),
            jax.ShapeDtypeStruct((HEADS, N, 256), BF16), jax.ShapeDtypeStruct((HEADS, N, 256), BF16),
            jax.ShapeDtypeStruct((HEADS, N, V_HEAD), BF16), jax.ShapeDtypeStruct((N, D), BF16),
        ],
        compiler_params=_cparams(dimension_semantics=("arbitrary",)),
    )(h, norm_g, win, gq, wq, gkv, wkv, cosf, sinf)


def _attn_fwd(q, k, v, wout_s):
    tiles = _attn_tiles()
    n_t = len(tiles)
    half = SHARD_OUT // 2
    send_step = 2
    fwd_step = n_t - 2

    def body(q_hbm, k_hbm, v_hbm, wout_ref, o_hbm, lse_ref, wout_o, q_buf, k_buf, v_buf, o_buf, s_wout, in_sems, out_sems,
             ici_send, ici_recv, fwd_send, fwd_recv, own_sem):
        step = pl.program_id(0)
        x, y, c = lax.axis_index("x"), lax.axis_index("y"), lax.axis_index("c")
        me = 2 * x + y

        def chip_of(rel):
            fx, fy = _CHIP_RELS[rel]
            return 2 * (x ^ fx) + (y ^ fy)

        def place(chip, core):
            return wout_o.at[pl.ds(pl.multiple_of(SHARD_OUT * chip + half * core, half), half), :]

        def ici_copy(rel, src_chip, to):
            return _remote(s_wout.at[pl.ds(pl.multiple_of(half * c, half), half), :], place(src_chip, c),
                           ici_send.at[rel - 1], ici_recv.at[rel - 1], to)

        def fwd_copy(rel, core, to):
            spot = place(chip_of(rel), core)
            return _remote(spot, spot, fwd_send.at[rel - 1], fwd_recv.at[rel - 1], to)

        own = pltpu.make_async_copy(s_wout, wout_o.at[pl.ds(pl.multiple_of(SHARD_OUT * me, SHARD_OUT), SHARD_OUT), :], own_sem)

        @pl.when(step == 0)
        def _():
            _peer_signal(x, y, c)
            s_wout[...] = wout_ref[...].astype(BF16)
            own.start()

        @pl.when(step == send_step)
        def _():
            _peer_wait()
            for rel in SEND_ORDER:
                fx, fy = _CHIP_RELS[rel]
                ici_copy(rel, me, (x ^ fx, y ^ fy, c)).start()

        @pl.when(step == fwd_step)
        def _():
            for rel in (1, 2, 3):
                ici_copy(rel, chip_of(rel), (x, y, c)).wait_recv()
                fwd_copy(rel, c, (x, y, 1 - c)).start()

        def finish_wout():
            for rel in (1, 2, 3):
                fwd_copy(rel, 1 - c, (x, y, c)).wait_recv()
            for rel in (1, 2, 3):
                ici_copy(rel, me, (x, y, c)).wait_send()
                fwd_copy(rel, c, (x, y, c)).wait_send()
            own.wait()

        def loads(idx):
            q0, rows, _ = tiles[idx]
            rs = pl.ds(q0, rows)
            return [pltpu.make_async_copy(src.at[:, rs, :], dst.at[:, rs, :], in_sems.at[a, idx % 2])
                    for a, (src, dst) in enumerate(((q_hbm, q_buf), (k_hbm, k_buf), (v_hbm, v_buf)))]

        def store(idx):
            q0, rows, _ = tiles[idx]
            return pltpu.make_async_copy(o_buf.at[idx % 2, pl.ds(0, rows), :], o_hbm.at[pl.ds(q0, rows), :],
                                         out_sems.at[idx % 2])

        @pl.when(step == 0)
        def _():
            lse_ref[...] = jnp.zeros_like(lse_ref)
            for cp in loads(0):
                cp.start()

        for idx, (q0, rows, klen) in enumerate(tiles):
            @pl.when(step == idx)
            def _(idx=idx, q0=q0, rows=rows, klen=klen):
                for cp in loads(idx):
                    cp.wait()
                if idx + 1 < n_t:
                    for cp in loads(idx + 1):
                        cp.start()
                if idx >= 2:
                    store(idx - 2).wait()
                for hd in range(HEADS):
                    s = _masked_scores(q_buf[hd, q0:q0 + rows, :], k_buf[hd, 0:klen, :], rows, klen)
                    m = jnp.max(s, axis=-1, keepdims=True)
                    p = jnp.exp(s - m)
                    l = jnp.sum(p, axis=-1, keepdims=True)
                    o_buf[idx % 2, 0:rows, hd * V_HEAD:(hd + 1) * V_HEAD] = _nn(p.astype(BF16), v_buf[hd, 0:klen, :]) / l
                    grp, lane = _stat_slot(hd)
                    lse_ref[grp, q0:q0 + rows, lane:lane + 1] = m + jnp.log(l)
                store(idx).start()
                if idx == n_t - 1:
                    store(idx - 1).wait()
                    store(idx).wait()
                    finish_wout()

    hbm = pl.BlockSpec(memory_space=pl.ANY)
    return pl.pallas_call(
        body,
        name="attn_fwd",
        grid=(n_t,),
        in_specs=[hbm, hbm, hbm, _const(SHARD_OUT, D)],
        out_specs=[hbm, _const(STAT_GROUPS, N, 128), hbm],
        out_shape=[jax.ShapeDtypeStruct((N, HEADS * V_HEAD), F32), jax.ShapeDtypeStruct((STAT_GROUPS, N, 128), F32),
                   jax.ShapeDtypeStruct((D, D), BF16)],
        scratch_shapes=[pltpu.VMEM((HEADS, N, 256), BF16), pltpu.VMEM((HEADS, N, 256), BF16),
                        pltpu.VMEM((HEADS, N, V_HEAD), BF16), pltpu.VMEM((2, TQ, HEADS * V_HEAD), F32),
                        pltpu.VMEM((SHARD_OUT, D), BF16),
                        pltpu.SemaphoreType.DMA((3, 2)), pltpu.SemaphoreType.DMA((2,))]
        + [pltpu.SemaphoreType.DMA((3,))] * 4 + [pltpu.SemaphoreType.DMA],
        compiler_params=_cparams(dimension_semantics=("arbitrary",), collective_id=1),
    )(q, k, v, wout_s)


def _inv_count(row0, rows, w):
    row = row0 + lax.broadcasted_iota(jnp.int32, (rows, 1), 0)
    return 1.0 / jnp.clip(row - (PAD - 1), 1, w).astype(F32)


def _mid(h, tgt, pool_in, pool_gate, attn_gate, attn, pool_w, pool_scale, wout, gf):
    tr = ROWS_MID
    per = tr // HALO
    ng = len(POOL_WINDOWS)

    def body(h_ref, t_ref, pin_ref, halo_ref, pg_ref, ag_ref, at_ref, pw_ref, ps_ref, wout_ref, gf_ref,
             dh2_ref, do_ref, delta_ref, dag_ref, dpg_ref, dpl_ref, dwout_ref, dpw_ref, dps_ref, dgf_ref, loss_ref):
        i = pl.program_id(0)

        @pl.when(i == 0)
        def _():
            dwout_ref[...] = jnp.zeros_like(dwout_ref)
            dpw_ref[...] = jnp.zeros_like(dpw_ref)
            dps_ref[...] = jnp.zeros_like(dps_ref)
            dgf_ref[...] = jnp.zeros_like(dgf_ref)
            loss_ref[...] = jnp.zeros_like(loss_ref)

        row0 = i * tr
        real = (row0 + lax.broadcasted_iota(jnp.int32, (tr, 1), 0)) >= HEAD_ROWS
        h = h_ref[...]

        halo = jnp.where(i > 0, halo_ref[...], 0.0)
        ext = jnp.concatenate([halo, pin_ref[...]], axis=0)
        pooled = []
        for g, w in enumerate(POOL_WINDOWS):
            e = ext[:, g * POOL_GROUP:(g + 1) * POOL_GROUP]
            acc = e
            shift = 1
            while shift < w:
                acc = acc + pltpu.roll(acc, shift, 0)
                shift *= 2
            pooled.append((acc[HALO:] * _inv_count(row0, tr, w) - e[HALO:]).astype(BF16))
        pw = [pw_ref[g].astype(BF16) for g in range(ng)]
        mixed = jnp.concatenate([_nn(pooled[g], pw[g]) for g in range(ng)], axis=1)
        ps = ps_ref[...]
        mixed_s = mixed * ps
        pg = pg_ref[...]
        sig_p = _sigmoid(pg)
        silu_p = pg * sig_p
        pool_out = (silu_p * mixed_s).astype(BF16)
        ag = ag_ref[...]
        sig_a = _sigmoid(ag)
        silu_a = ag * sig_a
        at = at_ref[...]
        attn_out = (silu_a * at).astype(BF16)
        cat = jnp.concatenate([pool_out, attn_out], axis=1)
        h2 = h + _nn(cat, wout_ref[...])

        r2 = lax.rsqrt(jnp.mean(h2 * h2, axis=-1, keepdims=True) + EPS)
        n2 = h2 * r2
        gfv = gf_ref[...]
        err = jnp.where(real, n2 * gfv - t_ref[...], 0.0)
        loss_ref[...] += jnp.sum(jnp.sum(err * err, axis=-1, keepdims=True), axis=0, keepdims=True) * (0.5 / D)
        dy = err * (1.0 / D)
        dgf_ref[...] += jnp.sum(dy * n2, axis=0, keepdims=True)
        dn = dy * gfv
        dh2 = r2 * (dn - n2 * jnp.mean(dn * n2, axis=-1, keepdims=True))
        dh2_ref[...] = dh2
        dh2b = dh2.astype(BF16)

        dwout_ref[...] += _tn(cat, dh2b)
        dcat = _nt(dh2b, wout_ref[...])
        dpo = dcat[:, 0:D_POOL]
        dao = dcat[:, D_POOL:D]
        do = dao * silu_a
        prod = do * at
        delta_ref[...] = jnp.zeros_like(delta_ref)
        for hd in range(HEADS):
            grp, lane = _stat_slot(hd)
            cols = slice(hd * V_HEAD, (hd + 1) * V_HEAD)
            do_ref[grp, :, lane * V_HEAD:(lane + 1) * V_HEAD] = do[:, cols].astype(BF16)
            delta_ref[grp, :, lane:lane + 1] = jnp.sum(prod[:, cols], axis=-1, keepdims=True)
        dag_ref[...] = (dao * at * (sig_a * (1.0 + ag * (1.0 - sig_a)))).astype(BF16)
        dmixed_s = dpo * silu_p
        dpg_ref[...] = (dpo * mixed_s * (sig_p * (1.0 + pg * (1.0 - sig_p)))).astype(BF16)
        dps_ref[...] += jnp.sum(dmixed_s * mixed, axis=0, keepdims=True)
        dmixed = (dmixed_s * ps).astype(BF16)
        dpl = []
        for g in range(ng):
            dm = dmixed[:, g * POOL_GROUP:(g + 1) * POOL_GROUP]
            dpl.append(_nt(dm, pw[g]))
            dpw_ref[g] += _tn(pooled[g], dm)
        dpl_ref[...] = jnp.concatenate(dpl, axis=1)

    halo_spec = pl.BlockSpec((HALO, D_POOL), lambda i: (jnp.maximum(i * per - 1, 0), 0))
    return pl.pallas_call(
        body,
        name="mid",
        grid=(N // tr,),
        in_specs=[
            _rows(D, tr), _rows(D, tr), _rows(D_POOL, tr), halo_spec, _rows(D_POOL, tr), _rows(D_POOL, tr),
            _rows(D_POOL, tr), _const(ng, POOL_GROUP, POOL_GROUP), _const(1, D_POOL), _const(D, D), _const(1, D),
        ],
        out_specs=[
            _rows(D, tr), pl.BlockSpec((STAT_GROUPS, tr, HEADS_PER_STEP_BWD * V_HEAD), lambda i: (0, i, 0)),
            pl.BlockSpec((STAT_GROUPS, tr, 128), lambda i: (0, i, 0)),
            _rows(D_POOL, tr), _rows(D_POOL, tr), _rows(D_POOL, tr),
            _const(D, D), _const(ng, POOL_GROUP, POOL_GROUP), _const(1, D_POOL), _const(1, D), _const(1, 128),
        ],
        out_shape=[
            jax.ShapeDtypeStruct((N, D), F32), jax.ShapeDtypeStruct((STAT_GROUPS, N, HEADS_PER_STEP_BWD * V_HEAD), BF16),
            jax.ShapeDtypeStruct((STAT_GROUPS, N, 128), F32),
            jax.ShapeDtypeStruct((N, D_POOL), BF16), jax.ShapeDtypeStruct((N, D_POOL), BF16),
            jax.ShapeDtypeStruct((N, D_POOL), F32), jax.ShapeDtypeStruct((D, D), F32),
            jax.ShapeDtypeStruct((ng, POOL_GROUP, POOL_GROUP), F32),
            jax.ShapeDtypeStruct((1, D_POOL), F32), jax.ShapeDtypeStruct((1, D), F32), jax.ShapeDtypeStruct((1, 128), F32),
        ],
        compiler_params=_cparams(dimension_semantics=("arbitrary",)),
    )(h, tgt, pool_in, pool_in, pool_gate, attn_gate, attn, pool_w, pool_scale, wout, gf)


def _unrope(dy, cosv, sinv):
    return dy * cosv + _swap64(dy * sinv) * _low_lanes()


def _attn_bwd(q, k, v, do, lse, delta, cosf, sinf, dwout, dpw):
    tiles = _attn_tiles()
    hp = HEADS_PER_STEP_BWD
    n_g = HEADS // hp
    n_t = len(tiles)
    half = SHARD_OUT // 2
    half_pw = PW_ROWS // 2
    swap_at, send_at, sum_at = (0, 3), (0, 5), (n_g - 1, n_t // 2)

    def body(q_hbm, k_hbm, v_hbm, do_hbm, lse_ref, delta_ref, cos_ref, sin_ref, dwout_hbm, dpw_ref, dq_hbm, dkv_ref, dkr_ref,
             gwout_ref, gpw_ref, q_buf, k_buf, v_buf, do_buf, dq_buf, dk_acc, dv_acc, own_w, sib_w, stage_w, recv_w, gw_buf,
             pw_sib, pw_chip, pw_recv, pw_buf,
             in_sems, out_sems, ow_sems, d2d_send, d2d_recv, ici_send, ici_recv, fin_send, fin_recv,
             pw_swap_send, pw_swap_recv, pw_ici_send, pw_ici_recv, pw_fin_send, pw_fin_recv):
        grp = pl.program_id(0)
        step = pl.program_id(1)
        heads = pl.ds(grp * hp, hp)
        x, y, c = lax.axis_index("x"), lax.axis_index("y"), lax.axis_index("c")
        me = 2 * x + y
        sibling = (x, y, 1 - c)

        def pw_rows(core):
            return pl.ds(pl.multiple_of(half_pw * core, half_pw), half_pw)

        def pw_swap():
            return _remote(dpw_ref, pw_sib, pw_swap_send.at[0], pw_swap_recv.at[0], sibling)

        def pw_ici(rel):
            fx, fy = _CHIP_RELS[rel]
            return _remote(pw_chip.at[pw_rows(c), :], pw_recv.at[rel - 1], pw_ici_send.at[rel - 1], pw_ici_recv.at[rel - 1],
                           (x ^ fx, y ^ fy, c))

        def pw_fin(core):
            spot = pw_buf.at[pw_rows(core), :]
            return _remote(spot, spot, pw_fin_send.at[0], pw_fin_recv.at[0], sibling)

        def chip_of(rel):
            fx, fy = _CHIP_RELS[rel]
            return 2 * (x ^ fx) + (y ^ fy)

        def piece(chip, core):
            return dwout_hbm.at[pl.ds(pl.multiple_of(SHARD_OUT * chip + half * core, half), half), :]

        def own_load(rel):
            return pltpu.make_async_copy(piece(chip_of(rel), c), own_w.at[rel], ow_sems.at[rel])

        def d2d_copy(rel):
            return _remote(piece(chip_of(rel), 1 - c), sib_w.at[rel], d2d_send.at[rel], d2d_recv.at[rel], sibling)

        def ici_copy(rel):
            fx, fy = _CHIP_RELS[rel]
            return _remote(stage_w.at[rel - 1], recv_w.at[rel - 1], ici_send.at[rel - 1], ici_recv.at[rel - 1],
                           (x ^ fx, y ^ fy, c))

        def fin_copy(core):
            spot = gw_buf.at[pl.ds(pl.multiple_of(half * core, half), half), :]
            return _remote(spot, spot, fin_send.at[0], fin_recv.at[0], sibling)

        @pl.when((grp == 0) & (step == 0))
        def _():
            _peer_signal(x, y, c)
            for rel in SEND_ORDER + (0,):
                own_load(rel).start()

        @pl.when((grp == swap_at[0]) & (step == swap_at[1]))
        def _():
            _peer_wait()
            pw_swap().start()
            for rel in SEND_ORDER + (0,):
                d2d_copy(rel).start()

        @pl.when((grp == send_at[0]) & (step == send_at[1]))
        def _():
            for rel in SEND_ORDER:
                own_load(rel).wait()
                d2d_copy(rel).wait_recv()
                stage_w[rel - 1] = (own_w[rel] + sib_w[rel]).astype(BF16)
                ici_copy(rel).start()
            pw_swap().wait_recv()
            pw_chip[...] = dpw_ref[...] + pw_sib[...]
            for rel in SEND_ORDER:
                pw_ici(rel).start()

        @pl.when((grp == sum_at[0]) & (step == sum_at[1]))
        def _():
            own_load(0).wait()
            d2d_copy(0).wait_recv()
            total = own_w[0] + sib_w[0]
            for rel in (1, 2, 3):
                ici_copy(rel).wait_recv()
                total = total + recv_w[rel - 1].astype(F32)
            gw_buf[pl.ds(pl.multiple_of(half * c, half), half), :] = total
            fin_copy(c).start()
            for rel in (1, 2, 3):
                pw_ici(rel).wait_recv()
            total = jnp.zeros((half_pw, POOL_GROUP), F32)
            for chip in range(CHIPS):
                flips = chip ^ me
                rel = jnp.where(flips == 2, 1, jnp.where(flips == 1, 2, flips))
                total = total + jnp.where(rel == 0, pw_chip[pw_rows(c), :], pw_recv[jnp.maximum(rel - 1, 0)])
            pw_buf[pw_rows(c), :] = total
            pw_fin(c).start()

        def finish_dwout():
            fin_copy(1 - c).wait_recv()
            pw_fin(1 - c).wait_recv()
            for rel in (0, 1, 2, 3):
                d2d_copy(rel).wait_send()
            for rel in (1, 2, 3):
                ici_copy(rel).wait_send()
                pw_ici(rel).wait_send()
            fin_copy(c).wait_send()
            pw_swap().wait_send()
            pw_fin(c).wait_send()
            gwout_ref[...] = gw_buf[...]
            gpw_ref[...] = pw_buf[...]

        def loads(g, idx):
            q0, rows, _ = tiles[idx]
            rs = pl.ds(q0, rows)
            par = (g * n_t + idx) % 2
            hs = pl.ds(g * hp, hp)
            pairs = ((q_hbm.at[hs, rs, :], q_buf.at[:, rs, :]), (k_hbm.at[hs, rs, :], k_buf.at[:, rs, :]),
                     (v_hbm.at[hs, rs, :], v_buf.at[:, rs, :]), (do_hbm.at[g, rs, :], do_buf.at[rs, :]))
            return [pltpu.make_async_copy(src, dst, in_sems.at[a, par]) for a, (src, dst) in enumerate(pairs)]

        def store(idx):
            q0, rows, _ = tiles[idx]
            return pltpu.make_async_copy(dq_buf.at[idx % 2, :, pl.ds(0, rows), :], dq_hbm.at[heads, pl.ds(q0, rows), :],
                                         out_sems.at[idx % 2])

        @pl.when(step == 0)
        def _():
            dk_acc[...] = jnp.zeros_like(dk_acc)
            dv_acc[...] = jnp.zeros_like(dv_acc)

        @pl.when((step == 0) & (grp == 0))
        def _():
            dkr_ref[...] = jnp.zeros_like(dkr_ref)
            for cp in loads(grp, 0):
                cp.start()

        for idx, (q0, rows, klen) in enumerate(tiles):
            @pl.when(step == idx)
            def _(idx=idx, q0=q0, rows=rows, klen=klen):
                for cp in loads(grp, idx):
                    cp.wait()
                if idx + 1 < n_t:
                    for cp in loads(grp, idx + 1):
                        cp.start()
                if idx >= 2:
                    store(idx - 2).wait()
                qs = pl.ds(q0, rows)
                for hd in range(hp):
                    qv = q_buf[hd, qs, :]
                    kv = k_buf[hd, 0:klen, :]
                    p = jnp.exp(_masked_scores(qv, kv, rows, klen) - lse_ref[0, qs, hd:hd + 1])
                    dob = do_buf[qs, hd * V_HEAD:(hd + 1) * V_HEAD]
                    ds = (p * (_nt(dob, v_buf[hd, 0:klen, :]) - delta_ref[0, qs, hd:hd + 1])).astype(BF16)
                    dq = _nn(ds, kv) * SCALE
                    dq_buf[idx % 2, hd, 0:rows, 0:QK_NOPE] = dq[:, 0:QK_NOPE].astype(BF16)
                    dq_buf[idx % 2, hd, 0:rows, QK_NOPE:] = _unrope(dq[:, QK_NOPE:], cos_ref[qs, :], sin_ref[qs, :]).astype(BF16)
                    dk_acc[hd, 0:klen, :] += _tn(ds, qv)
                    dv_acc[hd, 0:klen, :] += _tn(p.astype(BF16), dob)
                store(idx).start()

        @pl.when(step == n_t - 1)
        def _():
            @pl.when(grp + 1 < n_g)
            def _():
                for cp in loads(grp + 1, 0):
                    cp.start()

            for hd in range(hp):
                dkv_ref[hd, :, 0:QK_NOPE] = dk_acc[hd, :, 0:QK_NOPE].astype(BF16)
                dkv_ref[hd, :, QK_NOPE:] = dv_acc[hd].astype(BF16)
                dkr_ref[...] += dk_acc[hd, :, QK_NOPE:]
            store(n_t - 2).wait()
            store(n_t - 1).wait()

            @pl.when(grp == n_g - 1)
            def _():
                dkr_ref[...] = _unrope(dkr_ref[...], cos_ref[...], sin_ref[...])
                finish_dwout()

    hbm = pl.BlockSpec(memory_space=pl.ANY)
    stat = pl.BlockSpec((1, N, 128), lambda g, t: (g, 0, 0), pipeline_mode=pl.Buffered(1))
    piece_f32 = lambda lead: pltpu.VMEM((lead, half, D), F32)
    piece_bf16 = lambda lead: pltpu.VMEM((lead, half, D), BF16)
    return pl.pallas_call(
        body,
        name="attn_bwd",
        grid=(n_g, n_t),
        in_specs=[hbm, hbm, hbm, hbm, stat, stat, _const(N, 128), _const(N, 128), hbm, _const(PW_ROWS, POOL_GROUP)],
        out_specs=[hbm, pl.BlockSpec((hp, N, 256), lambda g, t: (g, 0, 0), pipeline_mode=pl.Buffered(1)), _const(N, 128),
                   _const(SHARD_OUT, D), _const(PW_ROWS, POOL_GROUP)],
        out_shape=[
            jax.ShapeDtypeStruct((HEADS, N, 256), BF16), jax.ShapeDtypeStruct((HEADS, N, 256), BF16),
            jax.ShapeDtypeStruct((N, 128), F32), jax.ShapeDtypeStruct((SHARD_OUT, D), F32),
            jax.ShapeDtypeStruct((PW_ROWS, POOL_GROUP), F32),
        ],
        scratch_shapes=[pltpu.VMEM((hp, N, 256), BF16), pltpu.VMEM((hp, N, 256), BF16), pltpu.VMEM((hp, N, V_HEAD), BF16),
                        pltpu.VMEM((N, hp * V_HEAD), BF16), pltpu.VMEM((2, hp, TQ, 256), BF16),
                        pltpu.VMEM((hp, N, 256), F32), pltpu.VMEM((hp, N, V_HEAD), F32),
                        piece_f32(CHIPS), piece_f32(CHIPS), piece_bf16(3), piece_bf16(3), pltpu.VMEM((SHARD_OUT, D), F32),
                        pltpu.VMEM((PW_ROWS, POOL_GROUP), F32), pltpu.VMEM((PW_ROWS, POOL_GROUP), F32),
                        pltpu.VMEM((3, half_pw, POOL_GROUP), F32), pltpu.VMEM((PW_ROWS, POOL_GROUP), F32),
                        pltpu.SemaphoreType.DMA((4, 2)), pltpu.SemaphoreType.DMA((2,)), pltpu.SemaphoreType.DMA((CHIPS,)),
                        pltpu.SemaphoreType.DMA((CHIPS,)), pltpu.SemaphoreType.DMA((CHIPS,)),
                        pltpu.SemaphoreType.DMA((3,)), pltpu.SemaphoreType.DMA((3,)),
                        pltpu.SemaphoreType.DMA((1,)), pltpu.SemaphoreType.DMA((1,)),
                        pltpu.SemaphoreType.DMA((1,)), pltpu.SemaphoreType.DMA((1,)),
                        pltpu.SemaphoreType.DMA((3,)), pltpu.SemaphoreType.DMA((3,)),
                        pltpu.SemaphoreType.DMA((1,)), pltpu.SemaphoreType.DMA((1,))],
        compiler_params=_cparams(dimension_semantics=("arbitrary", "arbitrary"), collective_id=2),
    )(q, k, v, do, lse, delta, cosf, sinf, dwout, dpw)


def _bwd_in(h, dh2, dq, dkv, dkr, cq, ckv, dpl, dpg, dag, norm_g, win, gq, wq, gkv, wkv, adam_out):
    tr = ROWS_BWD
    nb = N // tr
    per = tr // HALO
    lead = HEAD_ROWS
    adam_rows = SHARD_OUT // nb

    def body(h_ref, dh2_ref, dq_ref, dkv_ref, dkr_ref, cq_ref, ckv_ref, dpl_ref, halo_ref, dpg_ref, dag_ref,
             g_ref, win_ref, gq_ref, wq_ref, gkv_ref, wkv_ref, aw_ref, ag_ref, am_ref, av_ref,
             gx_ref, dmeta_ref, dsl_ref, dwq_ref, dwkv_ref, dg_ref, dgq_ref, dgkv_ref, ago_ref, ad_ref, anm_ref, anv_ref,
             dh_buf, gx_sem):
        i = pl.program_id(0)
        grad_out = ag_ref[...]
        ago_ref[...] = grad_out
        ad_ref[...], anm_ref[...], anv_ref[...] = _adamw_math(aw_ref[...], grad_out, am_ref[...], av_ref[...])

        @pl.when(i == 0)
        def _():
            dwq_ref[...] = jnp.zeros_like(dwq_ref)
            dwkv_ref[...] = jnp.zeros_like(dwkv_ref)
            dg_ref[...] = jnp.zeros_like(dg_ref)
            dgq_ref[...] = jnp.zeros_like(dgq_ref)
            dgkv_ref[...] = jnp.zeros_like(dgkv_ref)

        row0 = i * tr
        h = h_ref[...]
        r = lax.rsqrt(jnp.mean(h * h, axis=-1, keepdims=True) + EPS)
        n = h * r
        gv = g_ref[...]
        cq = cq_ref[...]
        rq = lax.rsqrt(jnp.mean(cq * cq, axis=-1, keepdims=True) + EPS)
        nq = cq * rq
        gqv = gq_ref[...]
        cqn = (nq * gqv).astype(BF16)
        dcqn = jnp.zeros((tr, Q_LORA), F32)
        for hd in range(HEADS):
            dqf = dq_ref[hd]
            dcqn = dcqn + _nn(dqf, wq_ref[hd])
            dwq_ref[hd] += _tn(dqf, cqn)
        dgq_ref[...] += jnp.sum(dcqn * nq, axis=0, keepdims=True)
        dnq = dcqn * gqv
        dcq = rq * (dnq - nq * jnp.mean(dnq * nq, axis=-1, keepdims=True))

        ckv = ckv_ref[...]
        rkv = lax.rsqrt(jnp.mean(ckv * ckv, axis=-1, keepdims=True) + EPS)
        nkv = ckv * rkv
        gkvv = gkv_ref[...]
        ckvn = (nkv * gkvv).astype(BF16)
        dckvn = jnp.zeros((tr, KV_LORA), F32)
        for hd in range(HEADS):
            dkv = dkv_ref[hd]
            dckvn = dckvn + _nt(dkv, wkv_ref[hd])
            dwkv_ref[hd] += _tn(ckvn, dkv)
        dgkv_ref[...] += jnp.sum(dckvn * nkv, axis=0, keepdims=True)
        dnkv = dckvn * gkvv
        dckv = rkv * (dnkv - nkv * jnp.mean(dnkv * nkv, axis=-1, keepdims=True))
        dkr = dkr_ref[...]

        cur = dpl_ref[...]
        halo = jnp.where(i < nb - 1, halo_ref[...], 0.0)
        dpi = []
        for g, w in enumerate(POOL_WINDOWS):
            sl = slice(g * POOL_GROUP, (g + 1) * POOL_GROUP)
            a = jnp.concatenate([cur[:, sl] * _inv_count(row0, tr, w), halo[:, sl] * _inv_count(row0 + tr, HALO, w)], axis=0)
            acc = a
            shift = 1
            while shift < w:
                acc = acc + pltpu.roll(acc, tr + HALO - shift, 0)
                shift *= 2
            dpi.append(acc[0:tr] - cur[:, sl])

        du = jnp.concatenate([t.astype(BF16) for t in dpi] + [dpg_ref[...]] + [t.astype(BF16) for t in (dcq, dckv, dkr)],
                             axis=1)
        dagb = dag_ref[...]
        by_row = jnp.concatenate(dpi + [dpg_ref[...].astype(F32), dcq, dckv, dkr[:, 0:QK_ROPE], dagb.astype(F32),
                                        jnp.zeros((tr, SHARD_PAD - SHARD_IN), F32)], axis=1)
        for chip in range(CHIPS):
            dsl_ref[chip] = by_row[:, SHARD_IN * chip:SHARD_IN * chip + SHARD_PAD].astype(BF16)
        dhn = _nn(du, win_ref[0:O_KR_END, :]) + _nn(dagb, win_ref[O_AG:D_IN, :])
        dg_ref[...] += jnp.sum(dhn * n, axis=0, keepdims=True)
        dn = dhn * gv
        dh = dh2_ref[...] + r * (dn - n * jnp.mean(dn * n, axis=-1, keepdims=True))

        first = pltpu.make_async_copy(dh_buf.at[pl.ds(lead, tr - lead), :], gx_ref.at[pl.ds(0, tr - lead), :], gx_sem)
        later = lambda step: pltpu.make_async_copy(
            dh_buf, gx_ref.at[pl.ds(pl.multiple_of(step * tr - lead, 16), tr), :], gx_sem)

        @pl.when(i == 1)
        def _():
            first.wait()

        @pl.when(i > 1)
        def _():
            later(i - 1).wait()

        dh_buf[...] = dh

        @pl.when(i == 0)
        def _():
            first.start()
            for chip in range(CHIPS):
                dmeta_ref[chip] = dh[PAD:HEAD_ROWS, chip * 256:(chip + 1) * 256]

        @pl.when(i > 0)
        def _():
            later(i).start()

        @pl.when(i == nb - 1)
        def _():
            later(i).wait()

    head = lambda w: pl.BlockSpec((HEADS, tr, w), lambda i: (0, i, 0))
    halo_spec = pl.BlockSpec((HALO, D_POOL), lambda i: (jnp.minimum((i + 1) * per, N // HALO - 1), 0))
    return pl.pallas_call(
        body,
        name="bwd_in",
        grid=(nb,),
        in_specs=[
            _rows(D, tr), _rows(D, tr), head(256), head(256), _rows(128, tr), _rows(Q_LORA, tr), _rows(KV_LORA, tr),
            _rows(D_POOL, tr), halo_spec, _rows(D_POOL, tr), _rows(D_POOL, tr),
            _const(1, D), _const(D_IN, D), _const(1, Q_LORA), _const(HEADS, 256, Q_LORA),
            _const(1, KV_LORA), _const(HEADS, KV_LORA, 256),
        ] + [_rows(D, adam_rows)] * 4,
        out_specs=[
            pl.BlockSpec(memory_space=pl.ANY), _const(CHIPS, N_META, 256),
            pl.BlockSpec((CHIPS, tr, SHARD_PAD), lambda i: (0, i, 0)), _const(HEADS, 256, Q_LORA),
            _const(HEADS, KV_LORA, 256), _const(1, D), _const(1, Q_LORA), _const(1, KV_LORA),
        ] + [_rows(D, adam_rows)] * 4,
        out_shape=[
            jax.ShapeDtypeStruct((S, D), F32), jax.ShapeDtypeStruct((CHIPS, N_META, 256), F32),
            jax.ShapeDtypeStruct((CHIPS, N, SHARD_PAD), BF16), jax.ShapeDtypeStruct((HEADS, 256, Q_LORA), F32),
            jax.ShapeDtypeStruct((HEADS, KV_LORA, 256), F32),
            jax.ShapeDtypeStruct((1, D), F32), jax.ShapeDtypeStruct((1, Q_LORA), F32), jax.ShapeDtypeStruct((1, KV_LORA), F32),
        ] + [jax.ShapeDtypeStruct((SHARD_OUT, D), F32)] * 4,
        scratch_shapes=[pltpu.VMEM((tr, D), F32), pltpu.SemaphoreType.DMA],
        compiler_params=_cparams(dimension_semantics=("arbitrary",)),
    )(h, dh2, dq, dkv, dkr, cq, ckv, dpl, dpl, dpg, dag, norm_g, win, gq, wq, gkv, wkv, *adam_out)


def _local_step(h, tgt, norm_g, win, gq, wq, gkv, wkv, pool_w, pool_scale, wout_s, m_wout_s, v_wout_s, gf, cosf, sinf):
    pool_in, pool_gate, cq, ckv, attn_gate, q, k, v, hn = _fwd_in(h, norm_g, win, gq, wq, gkv, wkv, cosf, sinf)
    attn, lse, wout = _attn_fwd(q, k, v, wout_s)
    dh2, do, delta, dag, dpg, dpl, dwout, dpw, dps, dgf, loss = _mid(
        h, tgt, pool_in, pool_gate, attn_gate, attn, pool_w, pool_scale, wout, gf)
    dq, dkv, dkr, gwout, gpw = _attn_bwd(q, k, v, do, lse, delta, cosf, sinf, dwout, dpw.reshape(PW_ROWS, POOL_GROUP))
    gx, dmeta, dsl, dwq, dwkv, dg, dgq, dgkv, *r_out = _bwd_in(
        h, dh2, dq, dkv, dkr, cq, ckv, dpl, dpg, dag, norm_g, win, gq, wq, gkv, wkv,
        (wout_s, gwout, m_wout_s, v_wout_s))
    return dict(gx=gx, dmeta=dmeta, dsl=dsl, hn=hn, dwq=dwq, dwkv=dwkv, r_out=tuple(r_out), dg=dg, dgq=dgq,
                dgkv=dgkv, gpw=gpw, dps=dps, dgf=dgf, loss=loss)


_CHIP_RELS = ((0, 0), (1, 0), (0, 1), (1, 1))

_ARR_ROWS = (SHARD_IN, SHARD_OUT, 256, KV_LORA, N_META)
_ARR_COLS = (D, D, Q_LORA, 256, 256)
_PIECES = (
    (0, 0, 256, 0), (0, 256, SHARD_IN - 256, 1),
    (1, 0, 128, 0), (1, 128, 128, 1),
    (2, 0, 128, 0), (2, 128, 128, 1),
    (3, 0, 64, 0), (3, 64, 64, 1),
    (4, 0, N_META, 0),
)
_NP = len(_PIECES)
_PIECE_MAX = (256, 128, 128, 64, N_META)


def _gathered_at(refs, arr, chip, r0, n):
    if arr in (0, 1):
        return refs[arr].at[pl.ds(pl.multiple_of(_ARR_ROWS[arr] * chip + r0, 16), n), :]
    return refs[arr].at[chip, pl.ds(r0, n), :]


def _remote(src, dst, send_sem, recv_sem, to):
    return pltpu.make_async_remote_copy(src_ref=src, dst_ref=dst, send_sem=send_sem, recv_sem=recv_sem,
                                        device_id=to, device_id_type=MESH)


def _gather_weights(winT_s, wqT_s, wkv_s, meta_s, x2, tgt2):
    arrays = (0, 2, 3, 4)

    def body(win_ref, wq_ref, wkv_ref, meta_ref, x_ref, t_ref, win_o, wq_o, wkv_o, h_o, tp_o,
             s_win, s_wq, s_wkv, meta_all, head_buf, x_buf, t_buf, ici_send, ici_recv, fwd_send, fwd_recv,
             loc_sems, own_sems):
        x, y, c = lax.axis_index("x"), lax.axis_index("y"), lax.axis_index("c")
        me = 2 * x + y
        stage = (s_win, None, s_wq, s_wkv, meta_ref)
        outs = (win_o, None, wq_o, wkv_o, meta_all)

        _peer_signal(x, y, c)

        frames = pl.ds(HEAD_ROWS, S)
        loads = [pltpu.make_async_copy(x_ref, x_buf, loc_sems.at[0]), pltpu.make_async_copy(t_ref, t_buf, loc_sems.at[1])]
        local = [pltpu.make_async_copy(x_buf, h_o.at[frames, :], loc_sems.at[0]),
                 pltpu.make_async_copy(t_buf, tp_o.at[frames, :], loc_sems.at[1])]
        for cp in loads:
            cp.start()

        s_win[...] = win_ref[...].astype(BF16)
        s_wq[0:QK, :] = wq_ref[...].astype(BF16)
        s_wq[QK:256, :] = jnp.zeros((256 - QK, Q_LORA), BF16)
        s_wkv[...] = wkv_ref[...].astype(BF16)
        head_buf[...] = jnp.zeros_like(head_buf)
        zeros = pltpu.make_async_copy(head_buf, tp_o.at[pl.ds(0, HEAD_ROWS), :], loc_sems.at[2])
        zeros.start()

        def chip_of(rel):
            fx, fy = _CHIP_RELS[rel]
            return 2 * (x ^ fx) + (y ^ fy)

        def same_core_of(rel):
            fx, fy = _CHIP_RELS[rel]
            return (x ^ fx, y ^ fy, c)

        def ici_copy(rel, i, src_chip, to):
            arr, r0, n, _ = _PIECES[i]
            k = (rel - 1) * _NP + i
            return _remote(stage[arr].at[pl.ds(r0, n), :], _gathered_at(outs, arr, src_chip, r0, n),
                           ici_send.at[k], ici_recv.at[k], to)

        def fwd_copy(rel, i, to):
            arr, r0, n, _ = _PIECES[i]
            k = (rel - 1) * _NP + i
            place = _gathered_at(outs, arr, chip_of(rel), r0, n)
            return _remote(place, place, fwd_send.at[k], fwd_recv.at[k], to)

        _peer_wait()
        for core in (0, 1):
            @pl.when(c == core)
            def _(core=core):
                mine = [i for i in range(_NP) if _PIECES[i][3] == core and _PIECES[i][0] in arrays]
                theirs = [i for i in range(_NP) if _PIECES[i][3] != core and _PIECES[i][0] in arrays]
                order = (1, 2, 3)
                sends = [ici_copy(rel, i, me, same_core_of(rel)) for rel in order for i in mine]
                for cp in sends:
                    cp.start()
                for ld, st in zip(loads, local):
                    ld.wait()
                    st.start()
                own = [pltpu.make_async_copy(stage[arr], _gathered_at(outs, arr, me, 0, _ARR_ROWS[arr]), own_sems.at[arr])
                       for arr in arrays if arr != 4]
                for cp in own:
                    cp.start()
                meta_all[me] = meta_ref[...]
                for rel in order:
                    for i in mine:
                        ici_copy(rel, i, chip_of(rel), (x, y, c)).wait_recv()
                        fwd = fwd_copy(rel, i, (x, y, 1 - c))
                        fwd.start()
                        sends.append(fwd)
                for rel in order:
                    for i in theirs:
                        fwd_copy(rel, i, (x, y, c)).wait_recv()
                for cp in sends:
                    cp.wait_send()
                for cp in own:
                    cp.wait()

        zeros.wait()
        for chip in range(CHIPS):
            head_buf[PAD:HEAD_ROWS, chip * 256:(chip + 1) * 256] = meta_all[chip]
        head = pltpu.make_async_copy(head_buf, h_o.at[pl.ds(0, HEAD_ROWS), :], loc_sems.at[2])
        head.start()
        head.wait()
        for cp in local:
            cp.wait()

    vm = pl.BlockSpec(memory_space=pltpu.VMEM)
    hbm = pl.BlockSpec(memory_space=pl.ANY)
    return pl.pallas_call(
        body,
        name="gather_weights",
        in_specs=[vm] * 4 + [hbm] * 2,
        out_specs=[hbm] * 5,
        out_shape=[
            jax.ShapeDtypeStruct((D_IN, D), BF16),
            jax.ShapeDtypeStruct((CHIPS, 256, Q_LORA), BF16), jax.ShapeDtypeStruct((CHIPS, KV_LORA, 256), BF16),
            jax.ShapeDtypeStruct((N, D), F32), jax.ShapeDtypeStruct((N, D), F32),
        ],
        scratch_shapes=[pltpu.VMEM((_ARR_ROWS[a], _ARR_COLS[a]), BF16) for a in (0, 2, 3)]
        + [pltpu.VMEM((CHIPS, N_META, 256), F32), pltpu.VMEM((HEAD_ROWS, D), F32), pltpu.VMEM((S, D), F32),
           pltpu.VMEM((S, D), F32)]
        + [pltpu.SemaphoreType.DMA((3 * _NP,))] * 4 + [pltpu.SemaphoreType.DMA((3,)), pltpu.SemaphoreType.DMA((4,))],
        compiler_params=_cparams(collective_id=0),
    )(winT_s, wqT_s, wkv_s, meta_s, x2, tgt2)


_SM_ROWS = (VEC_ROWS,)
_SM_COLS = (D,)
_SM_PIECES = ((0, 0, VEC_ROWS, 0),)
_NSP = len(_SM_PIECES)
_NSB = len(_SM_ROWS)


def _reduce_grads(dsl, hn, dwq, dwkv, dmeta4, dg, dgf, dgq, dgkv, dps, loss):
    arrays = (0, 2, 3, 4)
    loaded = (2, 3, 4)
    shard_order = SEND_ORDER + (0,)

    def body(dsl_hbm, hn_hbm, dwq_ref, dwkv_ref, dmeta_ref, dg_ref, dgf_ref, dgq_ref, dgkv_ref, dps_ref,
             loss_ref, gwin_o, gwq_o, gwkv_o, gmeta_o, gg_o, ggf_o, ggq_o, ggkv_o, gps_o, gloss_o,
             ow2, ow3, ow4, sb0, sb2, sb3, sb4, st0, st2, st3, st4, rc0, rc2, rc3, rc4,
             vec, sm_sb0, sm_cs0, sm_rc0, vec_fin, slab_v, hn_v, dwin_buf, own0,
             own_sems, d2d_send, d2d_recv, ici_send, ici_recv, fin_send, fin_recv,
             swap_send, swap_recv, smi_send, smi_recv, smf_send, smf_recv, ld_sems):
        x, y, c = lax.axis_index("x"), lax.axis_index("y"), lax.axis_index("c")
        me = 2 * x + y
        _peer_signal(x, y, c)
        grads = (None, None, dwq_ref, dwkv_ref, dmeta_ref)
        outs = (gwin_o, None, gwq_o, gwkv_o, gmeta_o)
        own_buf = (None, None, ow2, ow3, ow4)
        sib_buf = (sb0, None, sb2, sb3, sb4)
        stage = (st0, None, st2, st3, st4)
        recv = (rc0, None, rc2, rc3, rc4)
        sm_mine = (vec,)
        sm_sib = (sm_sb0,)
        sm_chip = (sm_cs0,)
        sm_recv = (sm_rc0,)
        sm_out = (vec_fin,)
        sibling = (x, y, 1 - c)

        def chip_of(rel):
            fx, fy = _CHIP_RELS[rel]
            return 2 * (x ^ fx) + (y ^ fy)

        def same_core_of(rel):
            fx, fy = _CHIP_RELS[rel]
            return (x ^ fx, y ^ fy, c)

        hn_load = pltpu.make_async_copy(hn_hbm, hn_v, ld_sems.at[CHIPS])

        def slab_load(rel):
            return pltpu.make_async_copy(dsl_hbm.at[chip_of(rel)], slab_v.at[rel], ld_sems.at[rel])

        hn_load.start()
        slab_load(shard_order[0]).start()

        def slot(bufs, i, idx):
            arr, _, n, _ = _PIECES[i]
            return bufs[arr].at[idx, pl.ds(0, n), :]

        def own_load(rel, i):
            arr, r0, n, _ = _PIECES[i]
            return pltpu.make_async_copy(_gathered_at(grads, arr, chip_of(rel), r0, n), slot(own_buf, i, rel),
                                         own_sems.at[rel * _NP + i])

        def d2d_copy(rel, i):
            arr, r0, n, _ = _PIECES[i]
            k = rel * _NP + i
            return _remote(_gathered_at(grads, arr, chip_of(rel), r0, n), slot(sib_buf, i, rel),
                           d2d_send.at[k], d2d_recv.at[k], sibling)

        def ici_copy(rel, i):
            k = (rel - 1) * _NP + i
            return _remote(slot(stage, i, rel - 1), slot(recv, i, rel - 1), ici_send.at[k], ici_recv.at[k],
                           same_core_of(rel))

        def fin_copy(i):
            arr, r0, n, _ = _PIECES[i]
            place = outs[arr].at[pl.ds(r0, n), :]
            return _remote(place, place, fin_send.at[i], fin_recv.at[i], sibling)

        def sm_ici_copy(rel, j):
            blk, r0, n, _ = _SM_PIECES[j]
            k = (rel - 1) * _NSP + j
            return _remote(sm_chip[blk].at[pl.ds(r0, n), :], sm_recv[blk].at[rel - 1, pl.ds(r0, n), :],
                           smi_send.at[k], smi_recv.at[k], same_core_of(rel))

        def sm_fin_copy(j):
            blk, r0, n, _ = _SM_PIECES[j]
            place = sm_out[blk].at[pl.ds(r0, n), :]
            return _remote(place, place, smf_send.at[j], smf_recv.at[j], sibling)

        vec[...] = jnp.zeros_like(vec)
        vec[0:1, :] = dg_ref[...]
        vec[1:2, :] = dgf_ref[...]
        vec[2:3, V_GQ:V_GQ + Q_LORA] = dgq_ref[...]
        vec[2:3, V_GKV:V_GKV + KV_LORA] = dgkv_ref[...]
        vec[2:3, V_PS:V_PS + D_POOL] = dps_ref[...]
        vec[2:3, V_LOSS:D] = loss_ref[...]
        _peer_wait()
        swaps = [_remote(sm_mine[b], sm_sib[b], swap_send.at[b], swap_recv.at[b], sibling) for b in range(_NSB)]
        for cp in swaps:
            cp.start()

        for core in (0, 1):
            @pl.when(c == core)
            def _(core=core):
                mine = [i for i in range(_NP) if _PIECES[i][3] == core and _PIECES[i][0] in loaded]
                theirs = [i for i in range(_NP) if _PIECES[i][3] != core and _PIECES[i][0] in loaded]
                i0 = next(i for i in range(_NP) if _PIECES[i][0] == 0 and _PIECES[i][3] == core)
                j0 = next(i for i in range(_NP) if _PIECES[i][0] == 0 and _PIECES[i][3] != core)
                sm_mine_p = [j for j in range(_NSP) if _SM_PIECES[j][3] == core]
                sm_theirs_p = [j for j in range(_NSP) if _SM_PIECES[j][3] != core]
                sends = list(swaps)

                for rel in shard_order:
                    for i in theirs:
                        cp = d2d_copy(rel, i)
                        cp.start()
                        sends.append(cp)
                    for i in mine:
                        own_load(rel, i).start()

                def piece_rows(i):
                    return pl.ds(_PIECES[i][1], _PIECES[i][2])

                def form(rel, i):
                    r0, n = _PIECES[i][1], _PIECES[i][2]
                    dwin_buf[rel, r0:r0 + n, :] = _tn(slab_v[rel, :, r0:r0 + _PIECE_MAX[0]], hn_v[...])[0:n, :]

                def d2d0(rel, i):
                    return _remote(dwin_buf.at[rel, piece_rows(i), :], slot(sib_buf, i, rel),
                                   d2d_send.at[rel * _NP + i], d2d_recv.at[rel * _NP + i], sibling)

                def settle(rel):
                    d2d0(rel, i0).wait_recv()
                    total = dwin_buf[rel, piece_rows(i0), :] + slot(sib_buf, i0, rel)[...]
                    if rel == 0:
                        own0[0:_PIECES[i0][2], :] = total
                    else:
                        slot(stage, i0, rel - 1)[...] = total.astype(BF16)
                        cp = ici_copy(rel, i0)
                        cp.start()
                        sends.append(cp)

                for rel in SEND_ORDER:
                    for i in mine:
                        arr, r0, n, _ = _PIECES[i]
                        own_load(rel, i).wait()
                        d2d_copy(rel, i).wait_recv()
                        total = slot(own_buf, i, rel)[...] + slot(sib_buf, i, rel)[...]
                        slot(stage, i, rel - 1)[...] = total.astype(stage[arr].dtype)
                        cp = ici_copy(rel, i)
                        cp.start()
                        sends.append(cp)

                for b in range(_NSB):
                    swaps[b].wait_recv()
                    sm_chip[b][...] = sm_mine[b][...] + sm_sib[b][...]
                for rel in SEND_ORDER:
                    for j in sm_mine_p:
                        cp = sm_ici_copy(rel, j)
                        cp.start()
                        sends.append(cp)

                hn_load.wait()
                for n, rel in enumerate(shard_order):
                    slab_load(rel).wait()
                    if n == 0:
                        for later in shard_order[1:]:
                            slab_load(later).start()
                    form(rel, j0)
                    cp = d2d0(rel, j0)
                    cp.start()
                    sends.append(cp)
                    form(rel, i0)
                    settle(rel)

                for i in mine:
                    arr, r0, n, _ = _PIECES[i]
                    own_load(0, i).wait()
                    d2d_copy(0, i).wait_recv()
                    total = slot(own_buf, i, 0)[...] + slot(sib_buf, i, 0)[...]
                    for rel in (1, 2, 3):
                        ici_copy(rel, i).wait_recv()
                        total = total + slot(recv, i, rel - 1)[...].astype(F32)
                    outs[arr][pl.ds(r0, n), :] = total
                    cp = fin_copy(i)
                    cp.start()
                    sends.append(cp)
                total = own0[0:_PIECES[i0][2], :]
                for rel in (1, 2, 3):
                    ici_copy(rel, i0).wait_recv()
                    total = total + slot(recv, i0, rel - 1)[...].astype(F32)
                outs[0][pl.ds(_PIECES[i0][1], _PIECES[i0][2]), :] = total
                cp = fin_copy(i0)
                cp.start()
                sends.append(cp)

                for j in sm_mine_p:
                    blk, r0, n, _ = _SM_PIECES[j]
                    for rel in (1, 2, 3):
                        sm_ici_copy(rel, j).wait_recv()
                    total = jnp.zeros((n, _SM_COLS[blk]), F32)
                    for chip in range(CHIPS):
                        flips = chip ^ me
                        rel = jnp.where(flips == 2, 1, jnp.where(flips == 1, 2, flips))
                        theirs_rows = sm_recv[blk][jnp.maximum(rel - 1, 0), pl.ds(r0, n), :]
                        total = total + jnp.where(rel == 0, sm_chip[blk][pl.ds(r0, n), :], theirs_rows)
                    sm_out[blk][pl.ds(r0, n), :] = total
                    cp = sm_fin_copy(j)
                    cp.start()
                    sends.append(cp)

                for i in theirs + [j0]:
                    fin_copy(i).wait_recv()
                for j in sm_theirs_p:
                    sm_fin_copy(j).wait_recv()
                for cp in sends:
                    cp.wait_send()

        gg_o[...] = vec_fin[0:1, :]
        ggf_o[...] = vec_fin[1:2, :]
        ggq_o[...] = vec_fin[2:3, V_GQ:V_GQ + Q_LORA]
        ggkv_o[...] = vec_fin[2:3, V_GKV:V_GKV + KV_LORA]
        gps_o[...] = vec_fin[2:3, V_PS:V_PS + D_POOL]
        gloss_o[...] = vec_fin[2:3, V_LOSS:D]

    vm = pl.BlockSpec(memory_space=pltpu.VMEM)
    piece_buf = lambda lead, dtype, which=arrays: [
        pltpu.VMEM((lead, _PIECE_MAX[a], _ARR_COLS[a]), F32 if a == 4 else dtype) for a in which]
    sm_buf = lambda *lead: [pltpu.VMEM(lead + (_SM_ROWS[b], _SM_COLS[b]), F32) for b in range(_NSB)]
    dma = lambda n: [pltpu.SemaphoreType.DMA((n,))] * 2
    return pl.pallas_call(
        body,
        name="reduce_grads",
        in_specs=[pl.BlockSpec(memory_space=pl.ANY)] * 4 + [vm] * 7,
        out_specs=[vm] * 10,
        out_shape=[jax.ShapeDtypeStruct((_ARR_ROWS[a], _ARR_COLS[a]), F32) for a in arrays]
        + [jax.ShapeDtypeStruct((1, D), F32),
           jax.ShapeDtypeStruct((1, D), F32), jax.ShapeDtypeStruct((1, Q_LORA), F32),
           jax.ShapeDtypeStruct((1, KV_LORA), F32), jax.ShapeDtypeStruct((1, D_POOL), F32),
           jax.ShapeDtypeStruct((1, 128), F32)],
        scratch_shapes=piece_buf(CHIPS, F32, loaded) + piece_buf(CHIPS, F32) + piece_buf(3, BF16) + piece_buf(3, BF16)
        + [pltpu.VMEM((VEC_ROWS, D), F32)] + sm_buf() + sm_buf() + sm_buf(3) + [pltpu.VMEM((VEC_ROWS, D), F32)]
        + [pltpu.VMEM((CHIPS, N, SHARD_PAD), BF16), pltpu.VMEM((N, D), BF16),
           pltpu.VMEM((CHIPS, SHARD_PAD, D), F32), pltpu.VMEM((_PIECE_MAX[0], D), F32)]
        + [pltpu.SemaphoreType.DMA((CHIPS * _NP,))]
        + dma(CHIPS * _NP) + dma(3 * _NP) + dma(_NP) + dma(_NSB) + dma(3 * _NSP) + dma(_NSP)
        + [pltpu.SemaphoreType.DMA((CHIPS + 1,))],
        compiler_params=_cparams(collective_id=3),
    )(dsl, hn, dwq, dwkv, dmeta4, dg, dgf, dgq, dgkv, dps, loss)


def _adamw_math(w, g, m, v):
    m = B1 * m + (1.0 - B1) * g
    v = B2 * v + (1.0 - B2) * (g * g)
    m_hat = m / C1
    v_hat = v / C2
    delta = -LR * (m_hat / (jnp.sqrt(v_hat) + ADAM_EPS) + WD * w)
    return delta, m, v


def _adamw(big, block_rows, groups):
    rows, cols = big[0].shape
    n = len(groups)

    def body(*refs):
        w_ref, g_ref, m_ref, v_ref = refs[0:4]
        small_in = refs[4:4 + 4 * n]
        go_ref, d_ref, nm_ref, nv_ref = refs[4 + 4 * n:8 + 4 * n]
        small_out = refs[8 + 4 * n:]
        g = g_ref[...]
        go_ref[...] = g
        d_ref[...], nm_ref[...], nv_ref[...] = _adamw_math(w_ref[...], g, m_ref[...], v_ref[...])

        @pl.when(pl.program_id(0) == 0)
        def _():
            for t in range(n):
                sw_ref, sg_ref, sm_ref, sv_ref = small_in[4 * t:4 * t + 4]
                sg = sg_ref[0:sw_ref.shape[0], :]
                small_out[4 * t][...] = sg
                small_out[4 * t + 1][...], small_out[4 * t + 2][...], small_out[4 * t + 3][...] = _adamw_math(
                    sw_ref[...], sg, sm_ref[...], sv_ref[...])

    spec = pl.BlockSpec((block_rows, cols), lambda i: (i, 0))
    vm = pl.BlockSpec(memory_space=pltpu.VMEM)
    outs = pl.pallas_call(
        body,
        name="adamw",
        grid=(rows // block_rows,),
        in_specs=[spec] * 4 + [vm] * (4 * n),
        out_specs=[spec] * 4 + [vm] * (4 * n),
        out_shape=[jax.ShapeDtypeStruct(big[0].shape, F32)] * 4
        + [jax.ShapeDtypeStruct(grp[0].shape, F32) for grp in groups for _ in range(4)],
        compiler_params=_cparams(dimension_semantics=("arbitrary",)),
    )(*big, *[a for grp in groups for a in grp])
    return tuple(outs[0:4]), [tuple(outs[4 + 4 * t:8 + 4 * t]) for t in range(n)]


def _rope_tables():
    half = QK_ROPE // 2
    f32 = np.float32
    inv_freq = (f32(1.0) / (f32(ROPE_THETA) ** (np.arange(half, dtype=f32) / f32(half)))).astype(f32)
    pos = np.arange(N, dtype=f32) - f32(PAD)
    ang = (pos[:, None] * inv_freq[None, :]).astype(f32)
    cos, sin = np.cos(ang).astype(f32), np.sin(ang).astype(f32)
    zero = np.zeros((N, 128 - QK_ROPE), f32)
    return jnp.asarray(np.concatenate([cos, cos, zero], axis=1)), jnp.asarray(np.concatenate([-sin, sin, zero], axis=1))


def kernel(x, meta_tokens, norm_g, w_in, q_norm_g, w_q_b, kv_norm_g, w_kv_b, pool_w, pool_scale, w_out, final_norm_g, loss_target, m_meta_tokens, m_norm_g, m_w_in, m_q_norm_g, m_w_q_b, m_kv_norm_g, m_w_kv_b, m_pool_w, m_pool_scale, m_w_out, m_final_norm_g, v_meta_tokens, v_norm_g, v_w_in, v_q_norm_g, v_w_q_b, v_kv_norm_g, v_w_kv_b, v_pool_w, v_pool_scale, v_w_out, v_final_norm_g):
    tr = lambda a: a[0].T
    win, wq, wkv, h, tgt = _gather_weights(tr(w_in), tr(w_q_b), w_kv_b[0], meta_tokens, x[0], loss_target[0])
    cosf, sinf = _rope_tables()
    gf = final_norm_g.reshape(1, D)

    part = _local_step(h, tgt, norm_g, win, q_norm_g, wq, kv_norm_g, wkv, pool_w[0], pool_scale, w_out[0], m_w_out[0],
                       v_w_out[0], gf, cosf, sinf)

    pw2 = lambda a: a.reshape(len(POOL_WINDOWS) * POOL_GROUP, POOL_GROUP)
    gpw = part["gpw"]
    gwinT, gwqT, gwkv, gmeta, gg, ggf, ggq, ggkv, gps, gloss = _reduce_grads(
        part["dsl"], part["hn"], part["dwq"], part["dwkv"], part["dmeta"], part["dg"],
        part["dgf"], part["dgq"], part["dgkv"], part["dps"], part["loss"])

    r_out = part["r_out"]
    fn2 = lambda a: a.reshape(1, D)
    r_in, (r_meta, r_norm, r_gq, r_wq, r_gkv, r_wkv, r_pw, r_ps, r_fn) = _adamw((tr(w_in), gwinT, tr(m_w_in), tr(v_w_in)), 248, [
        (meta_tokens, gmeta, m_meta_tokens, v_meta_tokens),
        (norm_g, gg, m_norm_g, v_norm_g),
        (q_norm_g, ggq, m_q_norm_g, v_q_norm_g),
        (tr(w_q_b), gwqT, tr(m_w_q_b), tr(v_w_q_b)),
        (kv_norm_g, ggkv, m_kv_norm_g, v_kv_norm_g),
        (w_kv_b[0], gwkv, m_w_kv_b[0], v_w_kv_b[0]),
        (pw2(pool_w), gpw, pw2(m_pool_w), pw2(v_pool_w)),
        (pool_scale, gps, m_pool_scale, v_pool_scale),
        (fn2(final_norm_g), ggf, fn2(m_final_norm_g), fn2(v_final_norm_g)),
    ])
    untr = lambda a: a.T[None]
    pw4 = lambda a: a.reshape(1, len(POOL_WINDOWS), POOL_GROUP, POOL_GROUP)
    per_kind = [[
        r_meta[kind], r_norm[kind], untr(r_in[kind]), r_gq[kind], untr(r_wq[kind]), r_gkv[kind], r_wkv[kind][None],
        pw4(r_pw[kind]), r_ps[kind], r_out[kind][None], r_fn[kind].reshape(D),
    ] for kind in range(4)]
    return (gloss[0, 0], part["gx"][None], *per_kind[0], *per_kind[1], *per_kind[2], *per_kind[3])
```

```python
import jax
import jax.numpy as jnp
import numpy as np
from jax import lax
from jax.experimental import pallas as pl
from jax.experimental.pallas import tpu as pltpu

F32 = jnp.float32
BF16 = jnp.bfloat16

D = 1024
S = 2048
N_META = 16
PAD = 112
HEAD_ROWS = PAD + N_META
N = HEAD_ROWS + S
D_POOL = 512
POOL_WINDOWS = (2, 4, 8, 16)
POOL_GROUP = 128
HALO = 16
HEADS = 4
QK_NOPE = 128
QK_ROPE = 64
QK = QK_NOPE + QK_ROPE
V_HEAD = 128
Q_LORA = 256
KV_LORA = 128
D_IN = 1984
EPS = 1e-6
ROPE_THETA = 10000.0
SCALE = QK ** -0.5
CHIPS = 4

ROWS_FWD = 544
ROWS_MID = 544
ROWS_BWD = 272
TK = 128
TQ = 256
NQ = S // TQ
HEADS_PER_STEP_BWD = 2

O_PI, O_PG, O_CQ, O_CKV, O_KR, O_AG = 0, 512, 1024, 1280, 1408, 1472
O_KR_END = O_KR + 128
SHARD_IN = D_IN // CHIPS
SHARD_PAD = 512
SHARD_OUT = D // CHIPS

LR, B1, B2, ADAM_EPS, WD, STEP = 0.001, 0.9, 0.999, 1e-08, 0.01, 10
C1 = 1.0 - B1**STEP
C2 = 1.0 - B2**STEP

VMEM_LIMIT = 60 * 1024 * 1024
MESH = pl.DeviceIdType.MESH
NEG = -1e30

VEC_ROWS = 8
PW_ROWS = len(POOL_WINDOWS) * POOL_GROUP
V_GQ, V_GKV, V_PS, V_LOSS = 0, 256, 384, 896


def _cparams(**kw):
    return pltpu.CompilerParams(vmem_limit_bytes=VMEM_LIMIT, **kw)


def _nt(a, b):
    return lax.dot_general(a, b, (((1,), (1,)), ((), ())), preferred_element_type=F32)


def _tn(a, b):
    return lax.dot_general(a, b, (((0,), (0,)), ((), ())), preferred_element_type=F32)


def _nn(a, b):
    return jnp.dot(a, b, preferred_element_type=F32)


def _swap64(t):
    return pltpu.roll(t, 32, 1) + pltpu.roll(t, 96, 1)


def _sigmoid(x):
    return 1.0 / (1.0 + jnp.exp(-x))


def _low_lanes():
    return (lax.broadcasted_iota(jnp.int32, (1, 128), 1) < QK_ROPE).astype(F32)


def _rows(w, rows):
    return pl.BlockSpec((rows, w), lambda i: (i, 0))


def _const(*shape):
    return pl.BlockSpec(shape, lambda *_: (0,) * len(shape), pipeline_mode=pl.Buffered(1))


STAT_GROUPS = HEADS // HEADS_PER_STEP_BWD


def _stat_slot(head):
    return head // HEADS_PER_STEP_BWD, head % HEADS_PER_STEP_BWD


N_PEERS = 4
SEND_ORDER = (3, 1, 2)


def _peer_signal(x, y, c):
    barrier = pltpu.get_barrier_semaphore()
    peers = [(x, y, 1 - c)] + [(x ^ fx, y ^ fy, c) for fx, fy in _CHIP_RELS[1:]]
    assert len(peers) == N_PEERS
    for peer in peers:
        pl.semaphore_signal(barrier, inc=1, device_id=peer, device_id_type=MESH)


def _peer_wait():
    pl.semaphore_wait(pltpu.get_barrier_semaphore(), N_PEERS)


def _attn_tiles():
    return [(0, TK, TK)] + [(TK + TQ * t, TQ, TK + TQ * (t + 1)) for t in range(NQ)]


def _masked_scores(q, k, rows, klen):
    s = _nt(q, k)
    col = lax.broadcasted_iota(jnp.int32, (1, TK), 1)
    head_bias = jnp.where(col >= PAD, 0.0, NEG)
    if klen == TK:
        return s + head_bias
    r = lax.broadcasted_iota(jnp.int32, (rows, 1), 0) >> 6
    c = lax.broadcasted_iota(jnp.int32, (1, rows), 1) >> 6
    diag_bias = jnp.where(c <= r, 0.0, NEG)
    parts = [s[:, 0:TK] + head_bias]
    if klen - rows > TK:
        parts.append(s[:, TK:klen - rows])
    parts.append(s[:, klen - rows:klen] + diag_bias)
    return jnp.concatenate(parts, axis=1)


def _fwd_in(h, norm_g, win, gq, wq, gkv, wkv, cosf, sinf):
    tr = ROWS_FWD

    def body(h_ref, g_ref, win_ref, gq_ref, wq_ref, gkv_ref, wkv_ref, cos_ref, sin_ref,
             pi_ref, pg_ref, cq_ref, ckv_ref, ag_ref, q_ref, k_ref, v_ref, hn_ref):
        h = h_ref[...]
        r = lax.rsqrt(jnp.mean(h * h, axis=-1, keepdims=True) + EPS)
        hn = ((h * r) * g_ref[...]).astype(BF16)
        hn_ref[...] = hn
        u = _nt(hn, win_ref[0:O_KR_END, :])
        pi_ref[...] = u[:, O_PI:O_PG]
        pg_ref[...] = u[:, O_PG:O_CQ]
        cq = u[:, O_CQ:O_CKV]
        ckv = u[:, O_CKV:O_KR]
        cq_ref[...] = cq
        ckv_ref[...] = ckv
        ag_ref[...] = _nt(hn, win_ref[O_AG:D_IN, :])
        cosv = cos_ref[...]
        sinv = sin_ref[...]
        kr = u[:, O_KR:O_KR_END] * _low_lanes()
        kr = (kr * cosv + _swap64(kr) * sinv).astype(BF16)
        rq = lax.rsqrt(jnp.mean(cq * cq, axis=-1, keepdims=True) + EPS)
        cqn = ((cq * rq) * gq_ref[...]).astype(BF16)
        rkv = lax.rsqrt(jnp.mean(ckv * ckv, axis=-1, keepdims=True) + EPS)
        ckvn = ((ckv * rkv) * gkv_ref[...]).astype(BF16)
        for hd in range(HEADS):
            qh = _nt(cqn, wq_ref[hd]) * SCALE
            z = qh[:, QK_NOPE:]
            q_ref[hd, :, 0:QK_NOPE] = qh[:, 0:QK_NOPE].astype(BF16)
            q_ref[hd, :, QK_NOPE:] = (z * cosv + _swap64(z) * sinv).astype(BF16)
            kvh = _nn(ckvn, wkv_ref[hd])
            k_ref[hd, :, 0:QK_NOPE] = kvh[:, 0:QK_NOPE].astype(BF16)
            k_ref[hd, :, QK_NOPE:] = kr
            v_ref[hd] = kvh[:, QK_NOPE:].astype(BF16)

    head = lambda w: pl.BlockSpec((HEADS, tr, w), lambda i: (0, i, 0))
    return pl.pallas_call(
        body,
        name="fwd_in",
        grid=(N // tr,),
        in_specs=[
            _rows(D, tr), _const(1, D), _const(D_IN, D), _const(1, Q_LORA), _const(HEADS, 256, Q_LORA),
            _const(1, KV_LORA), _const(HEADS, KV_LORA, 256), _rows(128, tr), _rows(128, tr),
        ],
        out_specs=[_rows(D_POOL, tr), _rows(D_POOL, tr), _rows(Q_LORA, tr), _rows(KV_LORA, tr), _rows(D_POOL, tr),
                   head(256), head(256), head(V_HEAD), _rows(D, tr)],
        out_shape=[
            jax.ShapeDtypeStruct((N, D_POOL), F32), jax.ShapeDtypeStruct((N, D_POOL), F32),
            jax.ShapeDtypeStruct((N, Q_LORA), F32), jax.ShapeDtypeStruct((N, KV_LORA), F32),
            jax.ShapeDtypeStruct((N, D_POOL), F32),
            jax.ShapeDtypeStruct((HEADS, N, 256), BF16), jax.ShapeDtypeStruct((HEADS, N, 256), BF16),
            jax.ShapeDtypeStruct((HEADS, N, V_HEAD), BF16), jax.ShapeDtypeStruct((N, D), BF16),
        ],
        compiler_params=_cparams(dimension_semantics=("arbitrary",)),
    )(h, norm_g, win, gq, wq, gkv, wkv, cosf, sinf)


def _attn_fwd(q, k, v, wout_s):
    tiles = _attn_tiles()
    n_t = len(tiles)
    half = SHARD_OUT // 2
    send_step = 2
    fwd_step = n_t - 2

    def body(q_hbm, k_hbm, v_hbm, wout_ref, o_hbm, lse_ref, wout_o, q_buf, k_buf, v_buf, o_buf, s_wout, in_sems, out_sems,
             ici_send, ici_recv, fwd_send, fwd_recv, own_sem):
        step = pl.program_id(0)
        x, y, c = lax.axis_index("x"), lax.axis_index("y"), lax.axis_index("c")
        me = 2 * x + y

        def chip_of(rel):
            fx, fy = _CHIP_RELS[rel]
            return 2 * (x ^ fx) + (y ^ fy)

        def place(chip, core):
            return wout_o.at[pl.ds(pl.multiple_of(SHARD_OUT * chip + half * core, half), half), :]

        def ici_copy(rel, src_chip, to):
            return _remote(s_wout.at[pl.ds(pl.multiple_of(half * c, half), half), :], place(src_chip, c),
                           ici_send.at[rel - 1], ici_recv.at[rel - 1], to)

        def fwd_copy(rel, core, to):
            spot = place(chip_of(rel), core)
            return _remote(spot, spot, fwd_send.at[rel - 1], fwd_recv.at[rel - 1], to)

        own = pltpu.make_async_copy(s_wout, wout_o.at[pl.ds(pl.multiple_of(SHARD_OUT * me, SHARD_OUT), SHARD_OUT), :], own_sem)

        @pl.when(step == 0)
        def _():
            _peer_signal(x, y, c)
            s_wout[...] = wout_ref[...].astype(BF16)
            own.start()

        @pl.when(step == send_step)
        def _():
            _peer_wait()
            for rel in SEND_ORDER:
                fx, fy = _CHIP_RELS[rel]
                ici_copy(rel, me, (x ^ fx, y ^ fy, c)).start()

        @pl.when(step == fwd_step)
        def _():
            for rel in (1, 2, 3):
                ici_copy(rel, chip_of(rel), (x, y, c)).wait_recv()
                fwd_copy(rel, c, (x, y, 1 - c)).start()

        def finish_wout():
            for rel in (1, 2, 3):
                fwd_copy(rel, 1 - c, (x, y, c)).wait_recv()
            for rel in (1, 2, 3):
                ici_copy(rel, me, (x, y, c)).wait_send()
                fwd_copy(rel, c, (x, y, c)).wait_send()
            own.wait()

        def loads(idx):
            q0, rows, _ = tiles[idx]
            rs = pl.ds(q0, rows)
            return [pltpu.make_async_copy(src.at[:, rs, :], dst.at[:, rs, :], in_sems.at[a, idx % 2])
                    for a, (src, dst) in enumerate(((q_hbm, q_buf), (k_hbm, k_buf), (v_hbm, v_buf)))]

        def store(idx):
            q0, rows, _ = tiles[idx]
            return pltpu.make_async_copy(o_buf.at[idx % 2, pl.ds(0, rows), :], o_hbm.at[pl.ds(q0, rows), :],
                                         out_sems.at[idx % 2])

        @pl.when(step == 0)
        def _():
            lse_ref[...] = jnp.zeros_like(lse_ref)
            for cp in loads(0):
                cp.start()

        for idx, (q0, rows, klen) in enumerate(tiles):
            @pl.when(step == idx)
            def _(idx=idx, q0=q0, rows=rows, klen=klen):
                for cp in loads(idx):
                    cp.wait()
                if idx + 1 < n_t:
                    for cp in loads(idx + 1):
                        cp.start()
                if idx >= 2:
                    store(idx - 2).wait()
                for hd in range(HEADS):
                    s = _masked_scores(q_buf[hd, q0:q0 + rows, :], k_buf[hd, 0:klen, :], rows, klen)
                    m = jnp.max(s, axis=-1, keepdims=True)
                    p = jnp.exp(s - m)
                    l = jnp.sum(p, axis=-1, keepdims=True)
                    o_buf[idx % 2, 0:rows, hd * V_HEAD:(hd + 1) * V_HEAD] = _nn(p.astype(BF16), v_buf[hd, 0:klen, :]) / l
                    grp, lane = _stat_slot(hd)
                    lse_ref[grp, q0:q0 + rows, lane:lane + 1] = m + jnp.log(l)
                store(idx).start()
                if idx == n_t - 1:
                    store(idx - 1).wait()
                    store(idx).wait()
                    finish_wout()

    hbm = pl.BlockSpec(memory_space=pl.ANY)
    return pl.pallas_call(
        body,
        name="attn_fwd",
        grid=(n_t,),
        in_specs=[hbm, hbm, hbm, _const(SHARD_OUT, D)],
        out_specs=[hbm, _const(STAT_GROUPS, N, 128), hbm],
        out_shape=[jax.ShapeDtypeStruct((N, HEADS * V_HEAD), F32), jax.ShapeDtypeStruct((STAT_GROUPS, N, 128), F32),
                   jax.ShapeDtypeStruct((D, D), BF16)],
        scratch_shapes=[pltpu.VMEM((HEADS, N, 256), BF16), pltpu.VMEM((HEADS, N, 256), BF16),
                        pltpu.VMEM((HEADS, N, V_HEAD), BF16), pltpu.VMEM((2, TQ, HEADS * V_HEAD), F32),
                        pltpu.VMEM((SHARD_OUT, D), BF16),
                        pltpu.SemaphoreType.DMA((3, 2)), pltpu.SemaphoreType.DMA((2,))]
        + [pltpu.SemaphoreType.DMA((3,))] * 4 + [pltpu.SemaphoreType.DMA],
        compiler_params=_cparams(dimension_semantics=("arbitrary",), collective_id=1),
    )(q, k, v, wout_s)


def _inv_count(row0, rows, w):
    row = row0 + lax.broadcasted_iota(jnp.int32, (rows, 1), 0)
    return 1.0 / jnp.clip(row - (PAD - 1), 1, w).astype(F32)


def _mid(h, tgt, pool_in, pool_gate, attn_gate, attn, pool_w, pool_scale, wout, gf):
    tr = ROWS_MID
    per = tr // HALO
    ng = len(POOL_WINDOWS)

    def body(h_ref, t_ref, pin_ref, halo_ref, pg_ref, ag_ref, at_ref, pw_ref, ps_ref, wout_ref, gf_ref,
             dh2_ref, do_ref, delta_ref, dag_ref, dpg_ref, dpl_ref, dwout_ref, dpw_ref, dps_ref, dgf_ref, loss_ref):
        i = pl.program_id(0)

        @pl.when(i == 0)
        def _():
            dwout_ref[...] = jnp.zeros_like(dwout_ref)
            dpw_ref[...] = jnp.zeros_like(dpw_ref)
            dps_ref[...] = jnp.zeros_like(dps_ref)
            dgf_ref[...] = jnp.zeros_like(dgf_ref)
            loss_ref[...] = jnp.zeros_like(loss_ref)

        row0 = i * tr
        real = (row0 + lax.broadcasted_iota(jnp.int32, (tr, 1), 0)) >= HEAD_ROWS
        h = h_ref[...]

        halo = jnp.where(i > 0, halo_ref[...], 0.0)
        ext = jnp.concatenate([halo, pin_ref[...]], axis=0)
        pooled = []
        for g, w in enumerate(POOL_WINDOWS):
            e = ext[:, g * POOL_GROUP:(g + 1) * POOL_GROUP]
            acc = e
            shift = 1
            while shift < w:
                acc = acc + pltpu.roll(acc, shift, 0)
                shift *= 2
            pooled.append((acc[HALO:] * _inv_count(row0, tr, w) - e[HALO:]).astype(BF16))
        pw = [pw_ref[g].astype(BF16) for g in range(ng)]
        mixed = jnp.concatenate([_nn(pooled[g], pw[g]) for g in range(ng)], axis=1)
        ps = ps_ref[...]
        mixed_s = mixed * ps
        pg = pg_ref[...]
        sig_p = _sigmoid(pg)
        silu_p = pg * sig_p
        pool_out = (silu_p * mixed_s).astype(BF16)
        ag = ag_ref[...]
        sig_a = _sigmoid(ag)
        silu_a = ag * sig_a
        at = at_ref[...]
        attn_out = (silu_a * at).astype(BF16)
        cat = jnp.concatenate([pool_out, attn_out], axis=1)
        h2 = h + _nn(cat, wout_ref[...])

        r2 = lax.rsqrt(jnp.mean(h2 * h2, axis=-1, keepdims=True) + EPS)
        n2 = h2 * r2
        gfv = gf_ref[...]
        err = jnp.where(real, n2 * gfv - t_ref[...], 0.0)
        loss_ref[...] += jnp.sum(jnp.sum(err * err, axis=-1, keepdims=True), axis=0, keepdims=True) * (0.5 / D)
        dy = err * (1.0 / D)
        dgf_ref[...] += jnp.sum(dy * n2, axis=0, keepdims=True)
        dn = dy * gfv
        dh2 = r2 * (dn - n2 * jnp.mean(dn * n2, axis=-1, keepdims=True))
        dh2_ref[...] = dh2
        dh2b = dh2.astype(BF16)

        dwout_ref[...] += _tn(cat, dh2b)
        dcat = _nt(dh2b, wout_ref[...])
        dpo = dcat[:, 0:D_POOL]
        dao = dcat[:, D_POOL:D]
        do = dao * silu_a
        prod = do * at
        delta_ref[...] = jnp.zeros_like(delta_ref)
        for hd in range(HEADS):
            grp, lane = _stat_slot(hd)
            cols = slice(hd * V_HEAD, (hd + 1) * V_HEAD)
            do_ref[grp, :, lane * V_HEAD:(lane + 1) * V_HEAD] = do[:, cols].astype(BF16)
            delta_ref[grp, :, lane:lane + 1] = jnp.sum(prod[:, cols], axis=-1, keepdims=True)
        dag_ref[...] = (dao * at * (sig_a * (1.0 + ag * (1.0 - sig_a)))).astype(BF16)
        dmixed_s = dpo * silu_p
        dpg_ref[...] = (dpo * mixed_s * (sig_p * (1.0 + pg * (1.0 - sig_p)))).astype(BF16)
        dps_ref[...] += jnp.sum(dmixed_s * mixed, axis=0, keepdims=True)
        dmixed = (dmixed_s * ps).astype(BF16)
        dpl = []
        for g in range(ng):
            dm = dmixed[:, g * POOL_GROUP:(g + 1) * POOL_GROUP]
            dpl.append(_nt(dm, pw[g]))
            dpw_ref[g] += _tn(pooled[g], dm)
        dpl_ref[...] = jnp.concatenate(dpl, axis=1)

    halo_spec = pl.BlockSpec((HALO, D_POOL), lambda i: (jnp.maximum(i * per - 1, 0), 0))
    return pl.pallas_call(
        body,
        name="mid",
        grid=(N // tr,),
        in_specs=[
            _rows(D, tr), _rows(D, tr), _rows(D_POOL, tr), halo_spec, _rows(D_POOL, tr), _rows(D_POOL, tr),
            _rows(D_POOL, tr), _const(ng, POOL_GROUP, POOL_GROUP), _const(1, D_POOL), _const(D, D), _const(1, D),
        ],
        out_specs=[
            _rows(D, tr), pl.BlockSpec((STAT_GROUPS, tr, HEADS_PER_STEP_BWD * V_HEAD), lambda i: (0, i, 0)),
            pl.BlockSpec((STAT_GROUPS, tr, 128), lambda i: (0, i, 0)),
            _rows(D_POOL, tr), _rows(D_POOL, tr), _rows(D_POOL, tr),
            _const(D, D), _const(ng, POOL_GROUP, POOL_GROUP), _const(1, D_POOL), _const(1, D), _const(1, 128),
        ],
        out_shape=[
            jax.ShapeDtypeStruct((N, D), F32), jax.ShapeDtypeStruct((STAT_GROUPS, N, HEADS_PER_STEP_BWD * V_HEAD), BF16),
            jax.ShapeDtypeStruct((STAT_GROUPS, N, 128), F32),
            jax.ShapeDtypeStruct((N, D_POOL), BF16), jax.ShapeDtypeStruct((N, D_POOL), BF16),
            jax.ShapeDtypeStruct((N, D_POOL), F32), jax.ShapeDtypeStruct((D, D), F32),
            jax.ShapeDtypeStruct((ng, POOL_GROUP, POOL_GROUP), F32),
            jax.ShapeDtypeStruct((1, D_POOL), F32), jax.ShapeDtypeStruct((1, D), F32), jax.ShapeDtypeStruct((1, 128), F32),
        ],
        compiler_params=_cparams(dimension_semantics=("arbitrary",)),
    )(h, tgt, pool_in, pool_in, pool_gate, attn_gate, attn, pool_w, pool_scale, wout, gf)


def _unrope(dy, cosv, sinv):
    return dy * cosv + _swap64(dy * sinv) * _low_lanes()


def _attn_bwd(q, k, v, do, lse, delta, cosf, sinf, dwout, dpw):
    tiles = _attn_tiles()
    hp = HEADS_PER_STEP_BWD
    n_g = HEADS // hp
    n_t = len(tiles)
    half = SHARD_OUT // 2
    half_pw = PW_ROWS // 2
    swap_at, send_at, sum_at = (0, 3), (0, 5), (n_g - 1, n_t // 2)

    def body(q_hbm, k_hbm, v_hbm, do_hbm, lse_ref, delta_ref, cos_ref, sin_ref, dwout_hbm, dpw_ref, dq_hbm, dkv_ref, dkr_ref,
             gwout_ref, gpw_ref, q_buf, k_buf, v_buf, do_buf, dq_buf, dk_acc, dv_acc, own_w, sib_w, stage_w, recv_w, gw_buf,
             pw_sib, pw_chip, pw_recv, pw_buf,
             in_sems, out_sems, ow_sems, d2d_send, d2d_recv, ici_send, ici_recv, fin_send, fin_recv,
             pw_swap_send, pw_swap_recv, pw_ici_send, pw_ici_recv, pw_fin_send, pw_fin_recv):
        grp = pl.program_id(0)
        step = pl.program_id(1)
        heads = pl.ds(grp * hp, hp)
        x, y, c = lax.axis_index("x"), lax.axis_index("y"), lax.axis_index("c")
        me = 2 * x + y
        sibling = (x, y, 1 - c)

        def pw_rows(core):
            return pl.ds(pl.multiple_of(half_pw * core, half_pw), half_pw)

        def pw_swap():
            return _remote(dpw_ref, pw_sib, pw_swap_send.at[0], pw_swap_recv.at[0], sibling)

        def pw_ici(rel):
            fx, fy = _CHIP_RELS[rel]
            return _remote(pw_chip.at[pw_rows(c), :], pw_recv.at[rel - 1], pw_ici_send.at[rel - 1], pw_ici_recv.at[rel - 1],
                           (x ^ fx, y ^ fy, c))

        def pw_fin(core):
            spot = pw_buf.at[pw_rows(core), :]
            return _remote(spot, spot, pw_fin_send.at[0], pw_fin_recv.at[0], sibling)

        def chip_of(rel):
            fx, fy = _CHIP_RELS[rel]
            return 2 * (x ^ fx) + (y ^ fy)

        def piece(chip, core):
            return dwout_hbm.at[pl.ds(pl.multiple_of(SHARD_OUT * chip + half * core, half), half), :]

        def own_load(rel):
            return pltpu.make_async_copy(piece(chip_of(rel), c), own_w.at[rel], ow_sems.at[rel])

        def d2d_copy(rel):
            return _remote(piece(chip_of(rel), 1 - c), sib_w.at[rel], d2d_send.at[rel], d2d_recv.at[rel], sibling)

        def ici_copy(rel):
            fx, fy = _CHIP_RELS[rel]
            return _remote(stage_w.at[rel - 1], recv_w.at[rel - 1], ici_send.at[rel - 1], ici_recv.at[rel - 1],
                           (x ^ fx, y ^ fy, c))

        def fin_copy(core):
            spot = gw_buf.at[pl.ds(pl.multiple_of(half * core, half), half), :]
            return _remote(spot, spot, fin_send.at[0], fin_recv.at[0], sibling)

        @pl.when((grp == 0) & (step == 0))
        def _():
            _peer_signal(x, y, c)
            for rel in SEND_ORDER + (0,):
                own_load(rel).start()

        @pl.when((grp == swap_at[0]) & (step == swap_at[1]))
        def _():
            _peer_wait()
            pw_swap().start()
            for rel in SEND_ORDER + (0,):
                d2d_copy(rel).start()

        @pl.when((grp == send_at[0]) & (step == send_at[1]))
        def _():
            for rel in SEND_ORDER:
                own_load(rel).wait()
                d2d_copy(rel).wait_recv()
                stage_w[rel - 1] = (own_w[rel] + sib_w[rel]).astype(BF16)
                ici_copy(rel).start()
            pw_swap().wait_recv()
            pw_chip[...] = dpw_ref[...] + pw_sib[...]
            for rel in SEND_ORDER:
                pw_ici(rel).start()

        @pl.when((grp == sum_at[0]) & (step == sum_at[1]))
        def _():
            own_load(0).wait()
            d2d_copy(0).wait_recv()
            total = own_w[0] + sib_w[0]
            for rel in (1, 2, 3):
                ici_copy(rel).wait_recv()
                total = total + recv_w[rel - 1].astype(F32)
            gw_buf[pl.ds(pl.multiple_of(half * c, half), half), :] = total
            fin_copy(c).start()
            for rel in (1, 2, 3):
                pw_ici(rel).wait_recv()
            total = jnp.zeros((half_pw, POOL_GROUP), F32)
            for chip in range(CHIPS):
                flips = chip ^ me
                rel = jnp.where(flips == 2, 1, jnp.where(flips == 1, 2, flips))
                total = total + jnp.where(rel == 0, pw_chip[pw_rows(c), :], pw_recv[jnp.maximum(rel - 1, 0)])
            pw_buf[pw_rows(c), :] = total
            pw_fin(c).start()

        def finish_dwout():
            fin_copy(1 - c).wait_recv()
            pw_fin(1 - c).wait_recv()
            for rel in (0, 1, 2, 3):
                d2d_copy(rel).wait_send()
            for rel in (1, 2, 3):
                ici_copy(rel).wait_send()
                pw_ici(rel).wait_send()
            fin_copy(c).wait_send()
            pw_swap().wait_send()
            pw_fin(c).wait_send()
            gwout_ref[...] = gw_buf[...]
            gpw_ref[...] = pw_buf[...]

        def loads(g, idx):
            q0, rows, _ = tiles[idx]
            rs = pl.ds(q0, rows)
            par = (g * n_t + idx) % 2
            hs = pl.ds(g * hp, hp)
            pairs = ((q_hbm.at[hs, rs, :], q_buf.at[:, rs, :]), (k_hbm.at[hs, rs, :], k_buf.at[:, rs, :]),
                     (v_hbm.at[hs, rs, :], v_buf.at[:, rs, :]), (do_hbm.at[g, rs, :], do_buf.at[rs, :]))
            return [pltpu.make_async_copy(src, dst, in_sems.at[a, par]) for a, (src, dst) in enumerate(pairs)]

        def store(idx):
            q0, rows, _ = tiles[idx]
            return pltpu.make_async_copy(dq_buf.at[idx % 2, :, pl.ds(0, rows), :], dq_hbm.at[heads, pl.ds(q0, rows), :],
                                         out_sems.at[idx % 2])

        @pl.when(step == 0)
        def _():
            dk_acc[...] = jnp.zeros_like(dk_acc)
            dv_acc[...] = jnp.zeros_like(dv_acc)

        @pl.when((step == 0) & (grp == 0))
        def _():
            dkr_ref[...] = jnp.zeros_like(dkr_ref)
            for cp in loads(grp, 0):
                cp.start()

        for idx, (q0, rows, klen) in enumerate(tiles):
            @pl.when(step == idx)
            def _(idx=idx, q0=q0, rows=rows, klen=klen):
                for cp in loads(grp, idx):
                    cp.wait()
                if idx + 1 < n_t:
                    for cp in loads(grp, idx + 1):
                        cp.start()
                if idx >= 2:
                    store(idx - 2).wait()
                qs = pl.ds(q0, rows)
                for hd in range(hp):
                    qv = q_buf[hd, qs, :]
                    kv = k_buf[hd, 0:klen, :]
                    p = jnp.exp(_masked_scores(qv, kv, rows, klen) - lse_ref[0, qs, hd:hd + 1])
                    dob = do_buf[qs, hd * V_HEAD:(hd + 1) * V_HEAD]
                    ds = (p * (_nt(dob, v_buf[hd, 0:klen, :]) - delta_ref[0, qs, hd:hd + 1])).astype(BF16)
                    dq = _nn(ds, kv) * SCALE
                    dq_buf[idx % 2, hd, 0:rows, 0:QK_NOPE] = dq[:, 0:QK_NOPE].astype(BF16)
                    dq_buf[idx % 2, hd, 0:rows, QK_NOPE:] = _unrope(dq[:, QK_NOPE:], cos_ref[qs, :], sin_ref[qs, :]).astype(BF16)
                    dk_acc[hd, 0:klen, :] += _tn(ds, qv)
                    dv_acc[hd, 0:klen, :] += _tn(p.astype(BF16), dob)
                store(idx).start()

        @pl.when(step == n_t - 1)
        def _():
            @pl.when(grp + 1 < n_g)
            def _():
                for cp in loads(grp + 1, 0):
                    cp.start()

            for hd in range(hp):
                dkv_ref[hd, :, 0:QK_NOPE] = dk_acc[hd, :, 0:QK_NOPE].astype(BF16)
                dkv_ref[hd, :, QK_NOPE:] = dv_acc[hd].astype(BF16)
                dkr_ref[...] += dk_acc[hd, :, QK_NOPE:]
            store(n_t - 2).wait()
            store(n_t - 1).wait()

            @pl.when(grp == n_g - 1)
            def _():
                finish_dwout()

    hbm = pl.BlockSpec(memory_space=pl.ANY)
    stat = pl.BlockSpec((1, N, 128), lambda g, t: (g, 0, 0), pipeline_mode=pl.Buffered(1))
    piece_f32 = lambda lead: pltpu.VMEM((lead, half, D), F32)
    piece_bf16 = lambda lead: pltpu.VMEM((lead, half, D), BF16)
    return pl.pallas_call(
        body,
        name="attn_bwd",
        grid=(n_g, n_t),
        in_specs=[hbm, hbm, hbm, hbm, stat, stat, _const(N, 128), _const(N, 128), hbm, _const(PW_ROWS, POOL_GROUP)],
        out_specs=[hbm, pl.BlockSpec((hp, N, 256), lambda g, t: (g, 0, 0), pipeline_mode=pl.Buffered(1)), _const(N, 128),
                   _const(SHARD_OUT, D), _const(PW_ROWS, POOL_GROUP)],
        out_shape=[
            jax.ShapeDtypeStruct((HEADS, N, 256), BF16), jax.ShapeDtypeStruct((HEADS, N, 256), BF16),
            jax.ShapeDtypeStruct((N, 128), F32), jax.ShapeDtypeStruct((SHARD_OUT, D), F32),
            jax.ShapeDtypeStruct((PW_ROWS, POOL_GROUP), F32),
        ],
        scratch_shapes=[pltpu.VMEM((hp, N, 256), BF16), pltpu.VMEM((hp, N, 256), BF16), pltpu.VMEM((hp, N, V_HEAD), BF16),
                        pltpu.VMEM((N, hp * V_HEAD), BF16), pltpu.VMEM((2, hp, TQ, 256), BF16),
                        pltpu.VMEM((hp, N, 256), F32), pltpu.VMEM((hp, N, V_HEAD), F32),
                        piece_f32(CHIPS), piece_f32(CHIPS), piece_bf16(3), piece_bf16(3), pltpu.VMEM((SHARD_OUT, D), F32),
                        pltpu.VMEM((PW_ROWS, POOL_GROUP), F32), pltpu.VMEM((PW_ROWS, POOL_GROUP), F32),
                        pltpu.VMEM((3, half_pw, POOL_GROUP), F32), pltpu.VMEM((PW_ROWS, POOL_GROUP), F32),
                        pltpu.SemaphoreType.DMA((4, 2)), pltpu.SemaphoreType.DMA((2,)), pltpu.SemaphoreType.DMA((CHIPS,)),
                        pltpu.SemaphoreType.DMA((CHIPS,)), pltpu.SemaphoreType.DMA((CHIPS,)),
                        pltpu.SemaphoreType.DMA((3,)), pltpu.SemaphoreType.DMA((3,)),
                        pltpu.SemaphoreType.DMA((1,)), pltpu.SemaphoreType.DMA((1,)),
                        pltpu.SemaphoreType.DMA((1,)), pltpu.SemaphoreType.DMA((1,)),
                        pltpu.SemaphoreType.DMA((3,)), pltpu.SemaphoreType.DMA((3,)),
                        pltpu.SemaphoreType.DMA((1,)), pltpu.SemaphoreType.DMA((1,))],
        compiler_params=_cparams(dimension_semantics=("arbitrary", "arbitrary"), collective_id=2),
    )(q, k, v, do, lse, delta, cosf, sinf, dwout, dpw)


def _bwd_in(h, dh2, dq, dkv, dkr, cq, ckv, dpl, dpg, dag, norm_g, win, gq, wq, gkv, wkv, cosf, sinf, adam_out):
    tr = ROWS_BWD
    nb = N // tr
    per = tr // HALO
    lead = HEAD_ROWS
    adam_rows = SHARD_OUT // nb

    def body(h_ref, dh2_ref, dq_ref, dkv_ref, dkr_ref, cq_ref, ckv_ref, dpl_ref, halo_ref, dpg_ref, dag_ref,
             g_ref, win_ref, gq_ref, wq_ref, gkv_ref, wkv_ref, cos_ref, sin_ref, aw_ref, ag_ref, am_ref, av_ref,
             gx_ref, dmeta_ref, dsl_ref, dwq_ref, dwkv_ref, dg_ref, dgq_ref, dgkv_ref, ago_ref, ad_ref, anm_ref, anv_ref,
             dh_buf, gx_sem):
        i = pl.program_id(0)
        grad_out = ag_ref[...]
        ago_ref[...] = grad_out
        ad_ref[...], anm_ref[...], anv_ref[...] = _adamw_math(aw_ref[...], grad_out, am_ref[...], av_ref[...])

        @pl.when(i == 0)
        def _():
            dwq_ref[...] = jnp.zeros_like(dwq_ref)
            dwkv_ref[...] = jnp.zeros_like(dwkv_ref)
            dg_ref[...] = jnp.zeros_like(dg_ref)
            dgq_ref[...] = jnp.zeros_like(dgq_ref)
            dgkv_ref[...] = jnp.zeros_like(dgkv_ref)

        row0 = i * tr
        h = h_ref[...]
        r = lax.rsqrt(jnp.mean(h * h, axis=-1, keepdims=True) + EPS)
        n = h * r
        gv = g_ref[...]
        cq = cq_ref[...]
        rq = lax.rsqrt(jnp.mean(cq * cq, axis=-1, keepdims=True) + EPS)
        nq = cq * rq
        gqv = gq_ref[...]
        cqn = (nq * gqv).astype(BF16)
        dcqn = jnp.zeros((tr, Q_LORA), F32)
        for hd in range(HEADS):
            dqf = dq_ref[hd]
            dcqn = dcqn + _nn(dqf, wq_ref[hd])
            dwq_ref[hd] += _tn(dqf, cqn)
        dgq_ref[...] += jnp.sum(dcqn * nq, axis=0, keepdims=True)
        dnq = dcqn * gqv
        dcq = rq * (dnq - nq * jnp.mean(dnq * nq, axis=-1, keepdims=True))

        ckv = ckv_ref[...]
        rkv = lax.rsqrt(jnp.mean(ckv * ckv, axis=-1, keepdims=True) + EPS)
        nkv = ckv * rkv
        gkvv = gkv_ref[...]
        ckvn = (nkv * gkvv).astype(BF16)
        dckvn = jnp.zeros((tr, KV_LORA), F32)
        for hd in range(HEADS):
            dkv = dkv_ref[hd]
            dckvn = dckvn + _nt(dkv, wkv_ref[hd])
            dwkv_ref[hd] += _tn(ckvn, dkv)
        dgkv_ref[...] += jnp.sum(dckvn * nkv, axis=0, keepdims=True)
        dnkv = dckvn * gkvv
        dckv = rkv * (dnkv - nkv * jnp.mean(dnkv * nkv, axis=-1, keepdims=True))
        dkr = _unrope(dkr_ref[...], cos_ref[...], sin_ref[...])

        cur = dpl_ref[...]
        halo = jnp.where(i < nb - 1, halo_ref[...], 0.0)
        dpi = []
        for g, w in enumerate(POOL_WINDOWS):
            sl = slice(g * POOL_GROUP, (g + 1) * POOL_GROUP)
            a = jnp.concatenate([cur[:, sl] * _inv_count(row0, tr, w), halo[:, sl] * _inv_count(row0 + tr, HALO, w)], axis=0)
            acc = a
            shift = 1
            while shift < w:
                acc = acc + pltpu.roll(acc, tr + HALO - shift, 0)
                shift *= 2
            dpi.append(acc[0:tr] - cur[:, sl])

        du = jnp.concatenate([t.astype(BF16) for t in dpi] + [dpg_ref[...]] + [t.astype(BF16) for t in (dcq, dckv, dkr)],
                             axis=1)
        dagb = dag_ref[...]
        by_row = jnp.concatenate(dpi + [dpg_ref[...].astype(F32), dcq, dckv, dkr[:, 0:QK_ROPE], dagb.astype(F32),
                                        jnp.zeros((tr, SHARD_PAD - SHARD_IN), F32)], axis=1)
        for chip in range(CHIPS):
            dsl_ref[chip] = by_row[:, SHARD_IN * chip:SHARD_IN * chip + SHARD_PAD].astype(BF16)
        dhn = _nn(du, win_ref[0:O_KR_END, :]) + _nn(dagb, win_ref[O_AG:D_IN, :])
        dg_ref[...] += jnp.sum(dhn * n, axis=0, keepdims=True)
        dn = dhn * gv
        dh = dh2_ref[...] + r * (dn - n * jnp.mean(dn * n, axis=-1, keepdims=True))

        first = pltpu.make_async_copy(dh_buf.at[pl.ds(lead, tr - lead), :], gx_ref.at[pl.ds(0, tr - lead), :], gx_sem)
        later = lambda step: pltpu.make_async_copy(
            dh_buf, gx_ref.at[pl.ds(pl.multiple_of(step * tr - lead, 16), tr), :], gx_sem)

        @pl.when(i == 1)
        def _():
            first.wait()

        @pl.when(i > 1)
        def _():
            later(i - 1).wait()

        dh_buf[...] = dh

        @pl.when(i == 0)
        def _():
            first.start()
            for chip in range(CHIPS):
                dmeta_ref[chip] = dh[PAD:HEAD_ROWS, chip * 256:(chip + 1) * 256]

        @pl.when(i > 0)
        def _():
            later(i).start()

        @pl.when(i == nb - 1)
        def _():
            later(i).wait()

    head = lambda w: pl.BlockSpec((HEADS, tr, w), lambda i: (0, i, 0))
    halo_spec = pl.BlockSpec((HALO, D_POOL), lambda i: (jnp.minimum((i + 1) * per, N // HALO - 1), 0))
    return pl.pallas_call(
        body,
        name="bwd_in",
        grid=(nb,),
        in_specs=[
            _rows(D, tr), _rows(D, tr), head(256), head(256), _rows(128, tr), _rows(Q_LORA, tr), _rows(KV_LORA, tr),
            _rows(D_POOL, tr), halo_spec, _rows(D_POOL, tr), _rows(D_POOL, tr),
            _const(1, D), _const(D_IN, D), _const(1, Q_LORA), _const(HEADS, 256, Q_LORA),
            _const(1, KV_LORA), _const(HEADS, KV_LORA, 256), _rows(128, tr), _rows(128, tr),
        ] + [_rows(D, adam_rows)] * 4,
        out_specs=[
            pl.BlockSpec(memory_space=pl.ANY), _const(CHIPS, N_META, 256),
            pl.BlockSpec((CHIPS, tr, SHARD_PAD), lambda i: (0, i, 0)), _const(HEADS, 256, Q_LORA),
            _const(HEADS, KV_LORA, 256), _const(1, D), _const(1, Q_LORA), _const(1, KV_LORA),
        ] + [_rows(D, adam_rows)] * 4,
        out_shape=[
            jax.ShapeDtypeStruct((S, D), F32), jax.ShapeDtypeStruct((CHIPS, N_META, 256), F32),
            jax.ShapeDtypeStruct((CHIPS, N, SHARD_PAD), BF16), jax.ShapeDtypeStruct((HEADS, 256, Q_LORA), F32),
            jax.ShapeDtypeStruct((HEADS, KV_LORA, 256), F32),
            jax.ShapeDtypeStruct((1, D), F32), jax.ShapeDtypeStruct((1, Q_LORA), F32), jax.ShapeDtypeStruct((1, KV_LORA), F32),
        ] + [jax.ShapeDtypeStruct((SHARD_OUT, D), F32)] * 4,
        scratch_shapes=[pltpu.VMEM((tr, D), F32), pltpu.SemaphoreType.DMA],
        compiler_params=_cparams(dimension_semantics=("arbitrary",)),
    )(h, dh2, dq, dkv, dkr, cq, ckv, dpl, dpl, dpg, dag, norm_g, win, gq, wq, gkv, wkv, cosf, sinf, *adam_out)


def _local_step(h, tgt, norm_g, win, gq, wq, gkv, wkv, pool_w, pool_scale, wout_s, m_wout_s, v_wout_s, gf, cosf, sinf):
    pool_in, pool_gate, cq, ckv, attn_gate, q, k, v, hn = _fwd_in(h, norm_g, win, gq, wq, gkv, wkv, cosf, sinf)
    attn, lse, wout = _attn_fwd(q, k, v, wout_s)
    dh2, do, delta, dag, dpg, dpl, dwout, dpw, dps, dgf, loss = _mid(
        h, tgt, pool_in, pool_gate, attn_gate, attn, pool_w, pool_scale, wout, gf)
    dq, dkv, dkr, gwout, gpw = _attn_bwd(q, k, v, do, lse, delta, cosf, sinf, dwout, dpw.reshape(PW_ROWS, POOL_GROUP))
    gx, dmeta, dsl, dwq, dwkv, dg, dgq, dgkv, *r_out = _bwd_in(
        h, dh2, dq, dkv, dkr, cq, ckv, dpl, dpg, dag, norm_g, win, gq, wq, gkv, wkv, cosf, sinf,
        (wout_s, gwout, m_wout_s, v_wout_s))
    return dict(gx=gx, dmeta=dmeta, dsl=dsl, hn=hn, dwq=dwq, dwkv=dwkv, r_out=tuple(r_out), dg=dg, dgq=dgq,
                dgkv=dgkv, gpw=gpw, dps=dps, dgf=dgf, loss=loss)


_CHIP_RELS = ((0, 0), (1, 0), (0, 1), (1, 1))

_ARR_ROWS = (SHARD_IN, SHARD_OUT, 256, KV_LORA, N_META)
_ARR_COLS = (D, D, Q_LORA, 256, 256)
_PIECES = (
    (0, 0, 256, 0), (0, 256, SHARD_IN - 256, 1),
    (1, 0, 128, 0), (1, 128, 128, 1),
    (2, 0, 128, 0), (2, 128, 128, 1),
    (3, 0, 64, 0), (3, 64, 64, 1),
    (4, 0, N_META, 0),
)
_NP = len(_PIECES)
_PIECE_MAX = (256, 128, 128, 64, N_META)


def _gathered_at(refs, arr, chip, r0, n):
    if arr in (0, 1):
        return refs[arr].at[pl.ds(pl.multiple_of(_ARR_ROWS[arr] * chip + r0, 16), n), :]
    return refs[arr].at[chip, pl.ds(r0, n), :]


def _remote(src, dst, send_sem, recv_sem, to):
    return pltpu.make_async_remote_copy(src_ref=src, dst_ref=dst, send_sem=send_sem, recv_sem=recv_sem,
                                        device_id=to, device_id_type=MESH)


def _gather_weights(winT_s, wqT_s, wkv_s, meta_s, x2, tgt2):
    arrays = (0, 2, 3, 4)

    def body(win_ref, wq_ref, wkv_ref, meta_ref, x_ref, t_ref, win_o, wq_o, wkv_o, h_o, tp_o,
             s_win, s_wq, s_wkv, meta_all, head_buf, x_buf, t_buf, ici_send, ici_recv, fwd_send, fwd_recv,
             loc_sems, own_sems):
        x, y, c = lax.axis_index("x"), lax.axis_index("y"), lax.axis_index("c")
        me = 2 * x + y
        stage = (s_win, None, s_wq, s_wkv, meta_ref)
        outs = (win_o, None, wq_o, wkv_o, meta_all)

        _peer_signal(x, y, c)

        frames = pl.ds(HEAD_ROWS, S)
        loads = [pltpu.make_async_copy(x_ref, x_buf, loc_sems.at[0]), pltpu.make_async_copy(t_ref, t_buf, loc_sems.at[1])]
        local = [pltpu.make_async_copy(x_buf, h_o.at[frames, :], loc_sems.at[0]),
                 pltpu.make_async_copy(t_buf, tp_o.at[frames, :], loc_sems.at[1])]
        for cp in loads:
            cp.start()

        s_win[...] = win_ref[...].astype(BF16)
        s_wq[0:QK, :] = wq_ref[...].astype(BF16)
        s_wq[QK:256, :] = jnp.zeros((256 - QK, Q_LORA), BF16)
        s_wkv[...] = wkv_ref[...].astype(BF16)
        head_buf[...] = jnp.zeros_like(head_buf)
        zeros = pltpu.make_async_copy(head_buf, tp_o.at[pl.ds(0, HEAD_ROWS), :], loc_sems.at[2])
        zeros.start()

        def chip_of(rel):
            fx, fy = _CHIP_RELS[rel]
            return 2 * (x ^ fx) + (y ^ fy)

        def same_core_of(rel):
            fx, fy = _CHIP_RELS[rel]
            return (x ^ fx, y ^ fy, c)

        def ici_copy(rel, i, src_chip, to):
            arr, r0, n, _ = _PIECES[i]
            k = (rel - 1) * _NP + i
            return _remote(stage[arr].at[pl.ds(r0, n), :], _gathered_at(outs, arr, src_chip, r0, n),
                           ici_send.at[k], ici_recv.at[k], to)

        def fwd_copy(rel, i, to):
            arr, r0, n, _ = _PIECES[i]
            k = (rel - 1) * _NP + i
            place = _gathered_at(outs, arr, chip_of(rel), r0, n)
            return _remote(place, place, fwd_send.at[k], fwd_recv.at[k], to)

        _peer_wait()
        for core in (0, 1):
            @pl.when(c == core)
            def _(core=core):
                mine = [i for i in range(_NP) if _PIECES[i][3] == core and _PIECES[i][0] in arrays]
                theirs = [i for i in range(_NP) if _PIECES[i][3] != core and _PIECES[i][0] in arrays]
                order = (1, 2, 3)
                sends = [ici_copy(rel, i, me, same_core_of(rel)) for rel in order for i in mine]
                for cp in sends:
                    cp.start()
                for ld, st in zip(loads, local):
                    ld.wait()
                    st.start()
                own = [pltpu.make_async_copy(stage[arr], _gathered_at(outs, arr, me, 0, _ARR_ROWS[arr]), own_sems.at[arr])
                       for arr in arrays if arr != 4]
                for cp in own:
                    cp.start()
                meta_all[me] = meta_ref[...]
                for rel in order:
                    for i in mine:
                        ici_copy(rel, i, chip_of(rel), (x, y, c)).wait_recv()
                        fwd = fwd_copy(rel, i, (x, y, 1 - c))
                        fwd.start()
                        sends.append(fwd)
                for rel in order:
                    for i in theirs:
                        fwd_copy(rel, i, (x, y, c)).wait_recv()
                for cp in sends:
                    cp.wait_send()
                for cp in own:
                    cp.wait()

        zeros.wait()
        for chip in range(CHIPS):
            head_buf[PAD:HEAD_ROWS, chip * 256:(chip + 1) * 256] = meta_all[chip]
        head = pltpu.make_async_copy(head_buf, h_o.at[pl.ds(0, HEAD_ROWS), :], loc_sems.at[2])
        head.start()
        head.wait()
        for cp in local:
            cp.wait()

    vm = pl.BlockSpec(memory_space=pltpu.VMEM)
    hbm = pl.BlockSpec(memory_space=pl.ANY)
    return pl.pallas_call(
        body,
        name="gather_weights",
        in_specs=[vm] * 4 + [hbm] * 2,
        out_specs=[hbm] * 5,
        out_shape=[
            jax.ShapeDtypeStruct((D_IN, D), BF16),
            jax.ShapeDtypeStruct((CHIPS, 256, Q_LORA), BF16), jax.ShapeDtypeStruct((CHIPS, KV_LORA, 256), BF16),
            jax.ShapeDtypeStruct((N, D), F32), jax.ShapeDtypeStruct((N, D), F32),
        ],
        scratch_shapes=[pltpu.VMEM((_ARR_ROWS[a], _ARR_COLS[a]), BF16) for a in (0, 2, 3)]
        + [pltpu.VMEM((CHIPS, N_META, 256), F32), pltpu.VMEM((HEAD_ROWS, D), F32), pltpu.VMEM((S, D), F32),
           pltpu.VMEM((S, D), F32)]
        + [pltpu.SemaphoreType.DMA((3 * _NP,))] * 4 + [pltpu.SemaphoreType.DMA((3,)), pltpu.SemaphoreType.DMA((4,))],
        compiler_params=_cparams(collective_id=0),
    )(winT_s, wqT_s, wkv_s, meta_s, x2, tgt2)


_SM_ROWS = (VEC_ROWS,)
_SM_COLS = (D,)
_SM_PIECES = ((0, 0, VEC_ROWS, 0),)
_NSP = len(_SM_PIECES)
_NSB = len(_SM_ROWS)


def _reduce_grads(dsl, hn, dwq, dwkv, dmeta4, dg, dgf, dgq, dgkv, dps, loss):
    arrays = (0, 2, 3, 4)
    loaded = (2, 3, 4)
    shard_order = SEND_ORDER + (0,)

    def body(dsl_hbm, hn_hbm, dwq_ref, dwkv_ref, dmeta_ref, dg_ref, dgf_ref, dgq_ref, dgkv_ref, dps_ref,
             loss_ref, gwin_o, gwq_o, gwkv_o, gmeta_o, gg_o, ggf_o, ggq_o, ggkv_o, gps_o, gloss_o,
             ow2, ow3, ow4, sb0, sb2, sb3, sb4, st0, st2, st3, st4, rc0, rc2, rc3, rc4,
             vec, sm_sb0, sm_cs0, sm_rc0, vec_fin, slab_v, hn_v, dwin_buf, own0,
             own_sems, d2d_send, d2d_recv, ici_send, ici_recv, fin_send, fin_recv,
             swap_send, swap_recv, smi_send, smi_recv, smf_send, smf_recv, ld_sems):
        x, y, c = lax.axis_index("x"), lax.axis_index("y"), lax.axis_index("c")
        me = 2 * x + y
        _peer_signal(x, y, c)
        grads = (None, None, dwq_ref, dwkv_ref, dmeta_ref)
        outs = (gwin_o, None, gwq_o, gwkv_o, gmeta_o)
        own_buf = (None, None, ow2, ow3, ow4)
        sib_buf = (sb0, None, sb2, sb3, sb4)
        stage = (st0, None, st2, st3, st4)
        recv = (rc0, None, rc2, rc3, rc4)
        sm_mine = (vec,)
        sm_sib = (sm_sb0,)
        sm_chip = (sm_cs0,)
        sm_recv = (sm_rc0,)
        sm_out = (vec_fin,)
        sibling = (x, y, 1 - c)

        def chip_of(rel):
            fx, fy = _CHIP_RELS[rel]
            return 2 * (x ^ fx) + (y ^ fy)

        def same_core_of(rel):
            fx, fy = _CHIP_RELS[rel]
            return (x ^ fx, y ^ fy, c)

        hn_load = pltpu.make_async_copy(hn_hbm, hn_v, ld_sems.at[CHIPS])

        def slab_load(rel):
            return pltpu.make_async_copy(dsl_hbm.at[chip_of(rel)], slab_v.at[rel], ld_sems.at[rel])

        hn_load.start()
        slab_load(shard_order[0]).start()

        def slot(bufs, i, idx):
            arr, _, n, _ = _PIECES[i]
            return bufs[arr].at[idx, pl.ds(0, n), :]

        def own_load(rel, i):
            arr, r0, n, _ = _PIECES[i]
            return pltpu.make_async_copy(_gathered_at(grads, arr, chip_of(rel), r0, n), slot(own_buf, i, rel),
                                         own_sems.at[rel * _NP + i])

        def d2d_copy(rel, i):
            arr, r0, n, _ = _PIECES[i]
            k = rel * _NP + i
            return _remote(_gathered_at(grads, arr, chip_of(rel), r0, n), slot(sib_buf, i, rel),
                           d2d_send.at[k], d2d_recv.at[k], sibling)

        def ici_copy(rel, i):
            k = (rel - 1) * _NP + i
            return _remote(slot(stage, i, rel - 1), slot(recv, i, rel - 1), ici_send.at[k], ici_recv.at[k],
                           same_core_of(rel))

        def fin_copy(i):
            arr, r0, n, _ = _PIECES[i]
            place = outs[arr].at[pl.ds(r0, n), :]
            return _remote(place, place, fin_send.at[i], fin_recv.at[i], sibling)

        def sm_ici_copy(rel, j):
            blk, r0, n, _ = _SM_PIECES[j]
            k = (rel - 1) * _NSP + j
            return _remote(sm_chip[blk].at[pl.ds(r0, n), :], sm_recv[blk].at[rel - 1, pl.ds(r0, n), :],
                           smi_send.at[k], smi_recv.at[k], same_core_of(rel))

        def sm_fin_copy(j):
            blk, r0, n, _ = _SM_PIECES[j]
            place = sm_out[blk].at[pl.ds(r0, n), :]
            return _remote(place, place, smf_send.at[j], smf_recv.at[j], sibling)

        vec[...] = jnp.zeros_like(vec)
        vec[0:1, :] = dg_ref[...]
        vec[1:2, :] = dgf_ref[...]
        vec[2:3, V_GQ:V_GQ + Q_LORA] = dgq_ref[...]
        vec[2:3, V_GKV:V_GKV + KV_LORA] = dgkv_ref[...]
        vec[2:3, V_PS:V_PS + D_POOL] = dps_ref[...]
        vec[2:3, V_LOSS:D] = loss_ref[...]
        _peer_wait()
        swaps = [_remote(sm_mine[b], sm_sib[b], swap_send.at[b], swap_recv.at[b], sibling) for b in range(_NSB)]
        for cp in swaps:
            cp.start()

        for core in (0, 1):
            @pl.when(c == core)
            def _(core=core):
                mine = [i for i in range(_NP) if _PIECES[i][3] == core and _PIECES[i][0] in loaded]
                theirs = [i for i in range(_NP) if _PIECES[i][3] != core and _PIECES[i][0] in loaded]
                i0 = next(i for i in range(_NP) if _PIECES[i][0] == 0 and _PIECES[i][3] == core)
                j0 = next(i for i in range(_NP) if _PIECES[i][0] == 0 and _PIECES[i][3] != core)
                sm_mine_p = [j for j in range(_NSP) if _SM_PIECES[j][3] == core]
                sm_theirs_p = [j for j in range(_NSP) if _SM_PIECES[j][3] != core]
                sends = list(swaps)

                for rel in shard_order:
                    for i in theirs:
                        cp = d2d_copy(rel, i)
                        cp.start()
                        sends.append(cp)
                    for i in mine:
                        own_load(rel, i).start()

                def piece_rows(i):
                    return pl.ds(_PIECES[i][1], _PIECES[i][2])

                def form(rel, i):
                    r0, n = _PIECES[i][1], _PIECES[i][2]
                    dwin_buf[rel, r0:r0 + n, :] = _tn(slab_v[rel, :, r0:r0 + _PIECE_MAX[0]], hn_v[...])[0:n, :]

                def d2d0(rel, i):
                    return _remote(dwin_buf.at[rel, piece_rows(i), :], slot(sib_buf, i, rel),
                                   d2d_send.at[rel * _NP + i], d2d_recv.at[rel * _NP + i], sibling)

                def settle(rel):
                    d2d0(rel, i0).wait_recv()
                    total = dwin_buf[rel, piece_rows(i0), :] + slot(sib_buf, i0, rel)[...]
                    if rel == 0:
                        own0[0:_PIECES[i0][2], :] = total
                    else:
                        slot(stage, i0, rel - 1)[...] = total.astype(BF16)
                        cp = ici_copy(rel, i0)
                        cp.start()
                        sends.append(cp)

                for rel in SEND_ORDER:
                    for i in mine:
                        arr, r0, n, _ = _PIECES[i]
                        own_load(rel, i).wait()
                        d2d_copy(rel, i).wait_recv()
                        total = slot(own_buf, i, rel)[...] + slot(sib_buf, i, rel)[...]
                        slot(stage, i, rel - 1)[...] = total.astype(stage[arr].dtype)
                        cp = ici_copy(rel, i)
                        cp.start()
                        sends.append(cp)

                for b in range(_NSB):
                    swaps[b].wait_recv()
                    sm_chip[b][...] = sm_mine[b][...] + sm_sib[b][...]
                for rel in SEND_ORDER:
                    for j in sm_mine_p:
                        cp = sm_ici_copy(rel, j)
                        cp.start()
                        sends.append(cp)

                hn_load.wait()
                for n, rel in enumerate(shard_order):
                    slab_load(rel).wait()
                    if n == 0:
                        for later in shard_order[1:]:
                            slab_load(later).start()
                    form(rel, j0)
                    cp = d2d0(rel, j0)
                    cp.start()
                    sends.append(cp)
                    form(rel, i0)
                    settle(rel)

                for i in mine:
                    arr, r0, n, _ = _PIECES[i]
                    own_load(0, i).wait()
                    d2d_copy(0, i).wait_recv()
                    total = slot(own_buf, i, 0)[...] + slot(sib_buf, i, 0)[...]
                    for rel in (1, 2, 3):
                        ici_copy(rel, i).wait_recv()
                        total = total + slot(recv, i, rel - 1)[...].astype(F32)
                    outs[arr][pl.ds(r0, n), :] = total
                    cp = fin_copy(i)
                    cp.start()
                    sends.append(cp)
                total = own0[0:_PIECES[i0][2], :]
                for rel in (1, 2, 3):
                    ici_copy(rel, i0).wait_recv()
                    total = total + slot(recv, i0, rel - 1)[...].astype(F32)
                outs[0][pl.ds(_PIECES[i0][1], _PIECES[i0][2]), :] = total
                cp = fin_copy(i0)
                cp.start()
                sends.append(cp)

                for j in sm_mine_p:
                    blk, r0, n, _ = _SM_PIECES[j]
                    for rel in (1, 2, 3):
                        sm_ici_copy(rel, j).wait_recv()
                    total = jnp.zeros((n, _SM_COLS[blk]), F32)
                    for chip in range(CHIPS):
                        flips = chip ^ me
                        rel = jnp.where(flips == 2, 1, jnp.where(flips == 1, 2, flips))
                        theirs_rows = sm_recv[blk][jnp.maximum(rel - 1, 0), pl.ds(r0, n), :]
                        total = total + jnp.where(rel == 0, sm_chip[blk][pl.ds(r0, n), :], theirs_rows)
                    sm_out[blk][pl.ds(r0, n), :] = total
                    cp = sm_fin_copy(j)
                    cp.start()
                    sends.append(cp)

                for i in theirs + [j0]:
                    fin_copy(i).wait_recv()
                for j in sm_theirs_p:
                    sm_fin_copy(j).wait_recv()
                for cp in sends:
                    cp.wait_send()

        gg_o[...] = vec_fin[0:1, :]
        ggf_o[...] = vec_fin[1:2, :]
        ggq_o[...] = vec_fin[2:3, V_GQ:V_GQ + Q_LORA]
        ggkv_o[...] = vec_fin[2:3, V_GKV:V_GKV + KV_LORA]
        gps_o[...] = vec_fin[2:3, V_PS:V_PS + D_POOL]
        gloss_o[...] = vec_fin[2:3, V_LOSS:D]

    vm = pl.BlockSpec(memory_space=pltpu.VMEM)
    piece_buf = lambda lead, dtype, which=arrays: [
        pltpu.VMEM((lead, _PIECE_MAX[a], _ARR_COLS[a]), F32 if a == 4 else dtype) for a in which]
    sm_buf = lambda *lead: [pltpu.VMEM(lead + (_SM_ROWS[b], _SM_COLS[b]), F32) for b in range(_NSB)]
    dma = lambda n: [pltpu.SemaphoreType.DMA((n,))] * 2
    return pl.pallas_call(
        body,
        name="reduce_grads",
        in_specs=[pl.BlockSpec(memory_space=pl.ANY)] * 4 + [vm] * 7,
        out_specs=[vm] * 10,
        out_shape=[jax.ShapeDtypeStruct((_ARR_ROWS[a], _ARR_COLS[a]), F32) for a in arrays]
        + [jax.ShapeDtypeStruct((1, D), F32),
           jax.ShapeDtypeStruct((1, D), F32), jax.ShapeDtypeStruct((1, Q_LORA), F32),
           jax.ShapeDtypeStruct((1, KV_LORA), F32), jax.ShapeDtypeStruct((1, D_POOL), F32),
           jax.ShapeDtypeStruct((1, 128), F32)],
        scratch_shapes=piece_buf(CHIPS, F32, loaded) + piece_buf(CHIPS, F32) + piece_buf(3, BF16) + piece_buf(3, BF16)
        + [pltpu.VMEM((VEC_ROWS, D), F32)] + sm_buf() + sm_buf() + sm_buf(3) + [pltpu.VMEM((VEC_ROWS, D), F32)]
        + [pltpu.VMEM((CHIPS, N, SHARD_PAD), BF16), pltpu.VMEM((N, D), BF16),
           pltpu.VMEM((CHIPS, SHARD_PAD, D), F32), pltpu.VMEM((_PIECE_MAX[0], D), F32)]
        + [pltpu.SemaphoreType.DMA((CHIPS * _NP,))]
        + dma(CHIPS * _NP) + dma(3 * _NP) + dma(_NP) + dma(_NSB) + dma(3 * _NSP) + dma(_NSP)
        + [pltpu.SemaphoreType.DMA((CHIPS + 1,))],
        compiler_params=_cparams(collective_id=3),
    )(dsl, hn, dwq, dwkv, dmeta4, dg, dgf, dgq, dgkv, dps, loss)


def _adamw_math(w, g, m, v):
    m = B1 * m + (1.0 - B1) * g
    v = B2 * v + (1.0 - B2) * (g * g)
    m_hat = m / C1
    v_hat = v / C2
    delta = -LR * (m_hat / (jnp.sqrt(v_hat) + ADAM_EPS) + WD * w)
    return delta, m, v


def _adamw(big, block_rows, groups):
    rows, cols = big[0].shape
    n = len(groups)

    def body(*refs):
        w_ref, g_ref, m_ref, v_ref = refs[0:4]
        small_in = refs[4:4 + 4 * n]
        go_ref, d_ref, nm_ref, nv_ref = refs[4 + 4 * n:8 + 4 * n]
        small_out = refs[8 + 4 * n:]
        g = g_ref[...]
        go_ref[...] = g
        d_ref[...], nm_ref[...], nv_ref[...] = _adamw_math(w_ref[...], g, m_ref[...], v_ref[...])

        @pl.when(pl.program_id(0) == 0)
        def _():
            for t in range(n):
                sw_ref, sg_ref, sm_ref, sv_ref = small_in[4 * t:4 * t + 4]
                sg = sg_ref[0:sw_ref.shape[0], :]
                small_out[4 * t][...] = sg
                small_out[4 * t + 1][...], small_out[4 * t + 2][...], small_out[4 * t + 3][...] = _adamw_math(
                    sw_ref[...], sg, sm_ref[...], sv_ref[...])

    spec = pl.BlockSpec((block_rows, cols), lambda i: (i, 0))
    vm = pl.BlockSpec(memory_space=pltpu.VMEM)
    outs = pl.pallas_call(
        body,
        name="adamw",
        grid=(rows // block_rows,),
        in_specs=[spec] * 4 + [vm] * (4 * n),
        out_specs=[spec] * 4 + [vm] * (4 * n),
        out_shape=[jax.ShapeDtypeStruct(big[0].shape, F32)] * 4
        + [jax.ShapeDtypeStruct(grp[0].shape, F32) for grp in groups for _ in range(4)],
        compiler_params=_cparams(dimension_semantics=("arbitrary",)),
    )(*big, *[a for grp in groups for a in grp])
    return tuple(outs[0:4]), [tuple(outs[4 + 4 * t:8 + 4 * t]) for t in range(n)]


def _rope_tables():
    half = QK_ROPE // 2
    f32 = np.float32
    inv_freq = (f32(1.0) / (f32(ROPE_THETA) ** (np.arange(half, dtype=f32) / f32(half)))).astype(f32)
    pos = np.arange(N, dtype=f32) - f32(PAD)
    ang = (pos[:, None] * inv_freq[None, :]).astype(f32)
    cos, sin = np.cos(ang).astype(f32), np.sin(ang).astype(f32)
    zero = np.zeros((N, 128 - QK_ROPE), f32)
    return jnp.asarray(np.concatenate([cos, cos, zero], axis=1)), jnp.asarray(np.concatenate([-sin, sin, zero], axis=1))


def kernel(x, meta_tokens, norm_g, w_in, q_norm_g, w_q_b, kv_norm_g, w_kv_b, pool_w, pool_scale, w_out, final_norm_g, loss_target, m_meta_tokens, m_norm_g, m_w_in, m_q_norm_g, m_w_q_b, m_kv_norm_g, m_w_kv_b, m_pool_w, m_pool_scale, m_w_out, m_final_norm_g, v_meta_tokens, v_norm_g, v_w_in, v_q_norm_g, v_w_q_b, v_kv_norm_g, v_w_kv_b, v_pool_w, v_pool_scale, v_w_out, v_final_norm_g):
    tr = lambda a: a[0].T
    win, wq, wkv, h, tgt = _gather_weights(tr(w_in), tr(w_q_b), w_kv_b[0], meta_tokens, x[0], loss_target[0])
    cosf, sinf = _rope_tables()
    gf = final_norm_g.reshape(1, D)

    part = _local_step(h, tgt, norm_g, win, q_norm_g, wq, kv_norm_g, wkv, pool_w[0], pool_scale, w_out[0], m_w_out[0],
                       v_w_out[0], gf, cosf, sinf)

    pw2 = lambda a: a.reshape(len(POOL_WINDOWS) * POOL_GROUP, POOL_GROUP)
    gpw = part["gpw"]
    gwinT, gwqT, gwkv, gmeta, gg, ggf, ggq, ggkv, gps, gloss = _reduce_grads(
        part["dsl"], part["hn"], part["dwq"], part["dwkv"], part["dmeta"], part["dg"],
        part["dgf"], part["dgq"], part["dgkv"], part["dps"], part["loss"])

    r_out = part["r_out"]
    fn2 = lambda a: a.reshape(1, D)
    r_in, (r_meta, r_norm, r_gq, r_wq, r_gkv, r_wkv, r_pw, r_ps, r_fn) = _adamw((tr(w_in), gwinT, tr(m_w_in), tr(v_w_in)), 248, [
        (meta_tokens, gmeta, m_meta_tokens, v_meta_tokens),
        (norm_g, gg, m_norm_g, v_norm_g),
        (q_norm_g, ggq, m_q_norm_g, v_q_norm_g),
        (tr(w_q_b), gwqT, tr(m_w_q_b), tr(v_w_q_b)),
        (kv_norm_g, ggkv, m_kv_norm_g, v_kv_norm_g),
        (w_kv_b[0], gwkv, m_w_kv_b[0], v_w_kv_b[0]),
        (pw2(pool_w), gpw, pw2(m_pool_w), pw2(v_pool_w)),
        (pool_scale, gps, m_pool_scale, v_pool_scale),
        (fn2(final_norm_g), ggf, fn2(m_final_norm_g), fn2(v_final_norm_g)),
    ])
    untr = lambda a: a.T[None]
    pw4 = lambda a: a.reshape(1, len(POOL_WINDOWS), POOL_GROUP, POOL_GROUP)
    per_kind = [[
        r_meta[kind], r_norm[kind], untr(r_in[kind]), r_gq[kind], untr(r_wq[kind]), r_gkv[kind], r_wkv[kind][None],
        pw4(r_pw[kind]), r_ps[kind], r_out[kind][None], r_fn[kind].reshape(D),
    ] for kind in range(4)]
    return (gloss[0, 0], part["gx"][None], *per_kind[0], *per_kind[1], *per_kind[2], *per_kind[3])
```

```python
import jax
import jax.numpy as jnp
import numpy as np
from jax import lax
from jax.experimental import pallas as pl
from jax.experimental.pallas import tpu as pltpu

F32 = jnp.float32
BF16 = jnp.bfloat16

D = 1024
S = 2048
N_META = 16
PAD = 112
HEAD_ROWS = PAD + N_META
N = HEAD_ROWS + S
D_POOL = 512
POOL_WINDOWS = (2, 4, 8, 16)
POOL_GROUP = 128
HALO = 16
HEADS = 4
QK_NOPE = 128
QK_ROPE = 64
QK = QK_NOPE + QK_ROPE
V_HEAD = 128
Q_LORA = 256
KV_LORA = 128
D_IN = 1984
EPS = 1e-6
ROPE_THETA = 10000.0
SCALE = QK ** -0.5
CHIPS = 4

ROWS_FWD = 544
ROWS_MID = 544
ROWS_BWD = 544
TK = 128
TQ = 256
NQ = S // TQ
HEADS_PER_STEP_BWD = 2

O_PI, O_PG, O_CQ, O_CKV, O_KR, O_AG = 0, 512, 1024, 1280, 1408, 1472
O_KR_END = O_KR + 128
SHARD_IN = D_IN // CHIPS
SHARD_PAD = 512
SHARD_OUT = D // CHIPS

LR, B1, B2, ADAM_EPS, WD, STEP = 0.001, 0.9, 0.999, 1e-08, 0.01, 10
C1 = 1.0 - B1**STEP
C2 = 1.0 - B2**STEP

VMEM_LIMIT = 60 * 1024 * 1024
MESH = pl.DeviceIdType.MESH
NEG = -1e30

VEC_ROWS = 8
PW_ROWS = len(POOL_WINDOWS) * POOL_GROUP
V_GQ, V_GKV, V_PS, V_LOSS = 0, 256, 384, 896


def _cparams(**kw):
    return pltpu.CompilerParams(vmem_limit_bytes=VMEM_LIMIT, **kw)


def _nt(a, b):
    return lax.dot_general(a, b, (((1,), (1,)), ((), ())), preferred_element_type=F32)


def _tn(a, b):
    return lax.dot_general(a, b, (((0,), (0,)), ((), ())), preferred_element_type=F32)


def _nn(a, b):
    return jnp.dot(a, b, preferred_element_type=F32)


def _swap64(t):
    return pltpu.roll(t, 32, 1) + pltpu.roll(t, 96, 1)


def _sigmoid(x):
    return 1.0 / (1.0 + jnp.exp(-x))


def _low_lanes():
    return (lax.broadcasted_iota(jnp.int32, (1, 128), 1) < QK_ROPE).astype(F32)


def _rows(w, rows):
    return pl.BlockSpec((rows, w), lambda i: (i, 0))


def _const(*shape):
    return pl.BlockSpec(shape, lambda *_: (0,) * len(shape), pipeline_mode=pl.Buffered(1))


STAT_GROUPS = HEADS // HEADS_PER_STEP_BWD


def _stat_slot(head):
    return head // HEADS_PER_STEP_BWD, head % HEADS_PER_STEP_BWD


N_PEERS = 4
SEND_ORDER = (3, 1, 2)


def _peer_signal(x, y, c):
    barrier = pltpu.get_barrier_semaphore()
    peers = [(x, y, 1 - c)] + [(x ^ fx, y ^ fy, c) for fx, fy in _CHIP_RELS[1:]]
    assert len(peers) == N_PEERS
    for peer in peers:
        pl.semaphore_signal(barrier, inc=1, device_id=peer, device_id_type=MESH)


def _peer_wait():
    pl.semaphore_wait(pltpu.get_barrier_semaphore(), N_PEERS)


def _attn_tiles():
    return [(0, TK, TK)] + [(TK + TQ * t, TQ, TK + TQ * (t + 1)) for t in range(NQ)]


def _masked_scores(q, k, rows, klen):
    s = _nt(q, k)
    col = lax.broadcasted_iota(jnp.int32, (1, TK), 1)
    head_bias = jnp.where(col >= PAD, 0.0, NEG)
    if klen == TK:
        return s + head_bias
    r = lax.broadcasted_iota(jnp.int32, (rows, 1), 0) >> 6
    c = lax.broadcasted_iota(jnp.int32, (1, rows), 1) >> 6
    diag_bias = jnp.where(c <= r, 0.0, NEG)
    parts = [s[:, 0:TK] + head_bias]
    if klen - rows > TK:
        parts.append(s[:, TK:klen - rows])
    parts.append(s[:, klen - rows:klen] + diag_bias)
    return jnp.concatenate(parts, axis=1)


def _fwd_in(h, norm_g, win, gq, wq, gkv, wkv, cosf, sinf):
    tr = ROWS_FWD

    def body(h_ref, g_ref, win_ref, gq_ref, wq_ref, gkv_ref, wkv_ref, cos_ref, sin_ref,
             pi_ref, pg_ref, cq_ref, ckv_ref, ag_ref, q_ref, k_ref, v_ref, hn_ref):
        h = h_ref[...]
        r = lax.rsqrt(jnp.mean(h * h, axis=-1, keepdims=True) + EPS)
        hn = ((h * r) * g_ref[...]).astype(BF16)
        hn_ref[...] = hn
        u = _nt(hn, win_ref[0:O_KR_END, :])
        pi_ref[...] = u[:, O_PI:O_PG]
        pg_ref[...] = u[:, O_PG:O_CQ]
        cq = u[:, O_CQ:O_CKV]
        ckv = u[:, O_CKV:O_KR]
        cq_ref[...] = cq
        ckv_ref[...] = ckv
        ag_ref[...] = _nt(hn, win_ref[O_AG:D_IN, :])
        cosv = cos_ref[...]
        sinv = sin_ref[...]
        kr = u[:, O_KR:O_KR_END] * _low_lanes()
        kr = (kr * cosv + _swap64(kr) * sinv).astype(BF16)
        rq = lax.rsqrt(jnp.mean(cq * cq, axis=-1, keepdims=True) + EPS)
        cqn = ((cq * rq) * gq_ref[...]).astype(BF16)
        rkv = lax.rsqrt(jnp.mean(ckv * ckv, axis=-1, keepdims=True) + EPS)
        ckvn = ((ckv * rkv) * gkv_ref[...]).astype(BF16)
        for hd in range(HEADS):
            qh = _nt(cqn, wq_ref[hd]) * SCALE
            z = qh[:, QK_NOPE:]
            q_ref[hd, :, 0:QK_NOPE] = qh[:, 0:QK_NOPE].astype(BF16)
            q_ref[hd, :, QK_NOPE:] = (z * cosv + _swap64(z) * sinv).astype(BF16)
            kvh = _nn(ckvn, wkv_ref[hd])
            k_ref[hd, :, 0:QK_NOPE] = kvh[:, 0:QK_NOPE].astype(BF16)
            k_ref[hd, :, QK_NOPE:] = kr
            v_ref[hd] = kvh[:, QK_NOPE:].astype(BF16)

    head = lambda w: pl.BlockSpec((HEADS, tr, w), lambda i: (0, i, 0))
    return pl.pallas_call(
        body,
        name="fwd_in",
        grid=(N // tr,),
        in_specs=[
            _rows(D, tr), _const(1, D), _const(D_IN, D), _const(1, Q_LORA), _const(HEADS, 256, Q_LORA),
            _const(1, KV_LORA), _const(HEADS, KV_LORA, 256), _rows(128, tr), _rows(128, tr),
        ],
        out_specs=[_rows(D_POOL, tr), _rows(D_POOL, tr), _rows(Q_LORA, tr), _rows(KV_LORA, tr), _rows(D_POOL, tr),
                   head(256), head(256), head(V_HEAD), _rows(D, tr)],
        out_shape=[
            jax.ShapeDtypeStruct((N, D_POOL), F32), jax.ShapeDtypeStruct((N, D_POOL), F32),
            jax.ShapeDtypeStruct((N, Q_LORA), F32), jax.ShapeDtypeStruct((N, KV_LORA), F32),
            jax.ShapeDtypeStruct((N, D_POOL), F32),
            jax.ShapeDtypeStruct((HEADS, N, 256), BF16), jax.ShapeDtypeStruct((HEADS, N, 256), BF16),
            jax.ShapeDtypeStruct((HEADS, N, V_HEAD), BF16), jax.ShapeDtypeStruct((N, D), BF16),
        ],
        compiler_params=_cparams(dimension_semantics=("arbitrary",)),
    )(h, norm_g, win, gq, wq, gkv, wkv, cosf, sinf)


def _attn_fwd(q, k, v, wout_s):
    tiles = _attn_tiles()
    n_t = len(tiles)
    half = SHARD_OUT // 2
    send_step = 2
    fwd_step = n_t - 2

    def body(q_hbm, k_hbm, v_hbm, wout_ref, o_hbm, lse_ref, wout_o, q_buf, k_buf, v_buf, o_buf, s_wout, in_sems, out_sems,
             ici_send, ici_recv, fwd_send, fwd_recv, own_sem):
        step = pl.program_id(0)
        x, y, c = lax.axis_index("x"), lax.axis_index("y"), lax.axis_index("c")
        me = 2 * x + y

        def chip_of(rel):
            fx, fy = _CHIP_RELS[rel]
            return 2 * (x ^ fx) + (y ^ fy)

        def place(chip, core):
            return wout_o.at[pl.ds(pl.multiple_of(SHARD_OUT * chip + half * core, half), half), :]

        def ici_copy(rel, src_chip, to):
            return _remote(s_wout.at[pl.ds(pl.multiple_of(half * c, half), half), :], place(src_chip, c),
                           ici_send.at[rel - 1], ici_recv.at[rel - 1], to)

        def fwd_copy(rel, core, to):
            spot = place(chip_of(rel), core)
            return _remote(spot, spot, fwd_send.at[rel - 1], fwd_recv.at[rel - 1], to)

        own = pltpu.make_async_copy(s_wout, wout_o.at[pl.ds(pl.multiple_of(SHARD_OUT * me, SHARD_OUT), SHARD_OUT), :], own_sem)

        @pl.when(step == 0)
        def _():
            _peer_signal(x, y, c)
            s_wout[...] = wout_ref[...].astype(BF16)
            own.start()

        @pl.when(step == send_step)
        def _():
            _peer_wait()
            for rel in SEND_ORDER:
                fx, fy = _CHIP_RELS[rel]
                ici_copy(rel, me, (x ^ fx, y ^ fy, c)).start()

        @pl.when(step == fwd_step)
        def _():
            for rel in (1, 2, 3):
                ici_copy(rel, chip_of(rel), (x, y, c)).wait_recv()
                fwd_copy(rel, c, (x, y, 1 - c)).start()

        def finish_wout():
            for rel in (1, 2, 3):
                fwd_copy(rel, 1 - c, (x, y, c)).wait_recv()
            for rel in (1, 2, 3):
                ici_copy(rel, me, (x, y, c)).wait_send()
                fwd_copy(rel, c, (x, y, c)).wait_send()
            own.wait()

        def loads(idx):
            q0, rows, _ = tiles[idx]
            rs = pl.ds(q0, rows)
            return [pltpu.make_async_copy(src.at[:, rs, :], dst.at[:, rs, :], in_sems.at[a, idx % 2])
                    for a, (src, dst) in enumerate(((q_hbm, q_buf), (k_hbm, k_buf), (v_hbm, v_buf)))]

        def store(idx):
            q0, rows, _ = tiles[idx]
            return pltpu.make_async_copy(o_buf.at[idx % 2, pl.ds(0, rows), :], o_hbm.at[pl.ds(q0, rows), :],
                                         out_sems.at[idx % 2])

        @pl.when(step == 0)
        def _():
            lse_ref[...] = jnp.zeros_like(lse_ref)
            for cp in loads(0):
                cp.start()

        for idx, (q0, rows, klen) in enumerate(tiles):
            @pl.when(step == idx)
            def _(idx=idx, q0=q0, rows=rows, klen=klen):
                for cp in loads(idx):
                    cp.wait()
                if idx + 1 < n_t:
                    for cp in loads(idx + 1):
                        cp.start()
                if idx >= 2:
                    store(idx - 2).wait()
                for hd in range(HEADS):
                    s = _masked_scores(q_buf[hd, q0:q0 + rows, :], k_buf[hd, 0:klen, :], rows, klen)
                    m = jnp.max(s, axis=-1, keepdims=True)
                    p = jnp.exp(s - m)
                    l = jnp.sum(p, axis=-1, keepdims=True)
                    o_buf[idx % 2, 0:rows, hd * V_HEAD:(hd + 1) * V_HEAD] = _nn(p.astype(BF16), v_buf[hd, 0:klen, :]) / l
                    grp, lane = _stat_slot(hd)
                    lse_ref[grp, q0:q0 + rows, lane:lane + 1] = m + jnp.log(l)
                store(idx).start()
                if idx == n_t - 1:
                    store(idx - 1).wait()
                    store(idx).wait()
                    finish_wout()

    hbm = pl.BlockSpec(memory_space=pl.ANY)
    return pl.pallas_call(
        body,
        name="attn_fwd",
        grid=(n_t,),
        in_specs=[hbm, hbm, hbm, _const(SHARD_OUT, D)],
        out_specs=[hbm, _const(STAT_GROUPS, N, 128), hbm],
        out_shape=[jax.ShapeDtypeStruct((N, HEADS * V_HEAD), F32), jax.ShapeDtypeStruct((STAT_GROUPS, N, 128), F32),
                   jax.ShapeDtypeStruct((D, D), BF16)],
        scratch_shapes=[pltpu.VMEM((HEADS, N, 256), BF16), pltpu.VMEM((HEADS, N, 256), BF16),
                        pltpu.VMEM((HEADS, N, V_HEAD), BF16), pltpu.VMEM((2, TQ, HEADS * V_HEAD), F32),
                        pltpu.VMEM((SHARD_OUT, D), BF16),
                        pltpu.SemaphoreType.DMA((3, 2)), pltpu.SemaphoreType.DMA((2,))]
        + [pltpu.SemaphoreType.DMA((3,))] * 4 + [pltpu.SemaphoreType.DMA],
        compiler_params=_cparams(dimension_semantics=("arbitrary",), collective_id=1),
    )(q, k, v, wout_s)


def _inv_count(row0, rows, w):
    row = row0 + lax.broadcasted_iota(jnp.int32, (rows, 1), 0)
    return 1.0 / jnp.clip(row - (PAD - 1), 1, w).astype(F32)


def _mid(h, tgt, pool_in, pool_gate, attn_gate, attn, pool_w, pool_scale, wout, gf):
    tr = ROWS_MID
    per = tr // HALO
    ng = len(POOL_WINDOWS)

    def body(h_ref, t_ref, pin_ref, halo_ref, pg_ref, ag_ref, at_ref, pw_ref, ps_ref, wout_ref, gf_ref,
             dh2_ref, do_ref, delta_ref, dag_ref, dpg_ref, dpl_ref, dwout_ref, dpw_ref, dps_ref, dgf_ref, loss_ref):
        i = pl.program_id(0)

        @pl.when(i == 0)
        def _():
            dwout_ref[...] = jnp.zeros_like(dwout_ref)
            dpw_ref[...] = jnp.zeros_like(dpw_ref)
            dps_ref[...] = jnp.zeros_like(dps_ref)
            dgf_ref[...] = jnp.zeros_like(dgf_ref)
            loss_ref[...] = jnp.zeros_like(loss_ref)

        row0 = i * tr
        real = (row0 + lax.broadcasted_iota(jnp.int32, (tr, 1), 0)) >= HEAD_ROWS
        h = h_ref[...]

        halo = jnp.where(i > 0, halo_ref[...], 0.0)
        ext = jnp.concatenate([halo, pin_ref[...]], axis=0)
        pooled = []
        for g, w in enumerate(POOL_WINDOWS):
            e = ext[:, g * POOL_GROUP:(g + 1) * POOL_GROUP]
            acc = e
            shift = 1
            while shift < w:
                acc = acc + pltpu.roll(acc, shift, 0)
                shift *= 2
            pooled.append((acc[HALO:] * _inv_count(row0, tr, w) - e[HALO:]).astype(BF16))
        pw = [pw_ref[g].astype(BF16) for g in range(ng)]
        mixed = jnp.concatenate([_nn(pooled[g], pw[g]) for g in range(ng)], axis=1)
        ps = ps_ref[...]
        mixed_s = mixed * ps
        pg = pg_ref[...]
        sig_p = _sigmoid(pg)
        silu_p = pg * sig_p
        pool_out = (silu_p * mixed_s).astype(BF16)
        ag = ag_ref[...]
        sig_a = _sigmoid(ag)
        silu_a = ag * sig_a
        at = at_ref[...]
        attn_out = (silu_a * at).astype(BF16)
        cat = jnp.concatenate([pool_out, attn_out], axis=1)
        h2 = h + _nn(cat, wout_ref[...])

        r2 = lax.rsqrt(jnp.mean(h2 * h2, axis=-1, keepdims=True) + EPS)
        n2 = h2 * r2
        gfv = gf_ref[...]
        err = jnp.where(real, n2 * gfv - t_ref[...], 0.0)
        loss_ref[...] += jnp.sum(jnp.sum(err * err, axis=-1, keepdims=True), axis=0, keepdims=True) * (0.5 / D)
        dy = err * (1.0 / D)
        dgf_ref[...] += jnp.sum(dy * n2, axis=0, keepdims=True)
        dn = dy * gfv
        dh2 = r2 * (dn - n2 * jnp.mean(dn * n2, axis=-1, keepdims=True))
        dh2_ref[...] = dh2
        dh2b = dh2.astype(BF16)

        dwout_ref[...] += _tn(cat, dh2b)
        dcat = _nt(dh2b, wout_ref[...])
        dpo = dcat[:, 0:D_POOL]
        dao = dcat[:, D_POOL:D]
        do = dao * silu_a
        prod = do * at
        delta_ref[...] = jnp.zeros_like(delta_ref)
        for hd in range(HEADS):
            grp, lane = _stat_slot(hd)
            cols = slice(hd * V_HEAD, (hd + 1) * V_HEAD)
            do_ref[grp, :, lane * V_HEAD:(lane + 1) * V_HEAD] = do[:, cols].astype(BF16)
            delta_ref[grp, :, lane:lane + 1] = jnp.sum(prod[:, cols], axis=-1, keepdims=True)
        dag_ref[...] = (dao * at * (sig_a * (1.0 + ag * (1.0 - sig_a)))).astype(BF16)
        dmixed_s = dpo * silu_p
        dpg_ref[...] = (dpo * mixed_s * (sig_p * (1.0 + pg * (1.0 - sig_p)))).astype(BF16)
        dps_ref[...] += jnp.sum(dmixed_s * mixed, axis=0, keepdims=True)
        dmixed = (dmixed_s * ps).astype(BF16)
        dpl = []
        for g in range(ng):
            dm = dmixed[:, g * POOL_GROUP:(g + 1) * POOL_GROUP]
            dpl.append(_nt(dm, pw[g]))
            dpw_ref[g] += _tn(pooled[g], dm)
        dpl_ref[...] = jnp.concatenate(dpl, axis=1)

    halo_spec = pl.BlockSpec((HALO, D_POOL), lambda i: (jnp.maximum(i * per - 1, 0), 0))
    return pl.pallas_call(
        body,
        name="mid",
        grid=(N // tr,),
        in_specs=[
            _rows(D, tr), _rows(D, tr), _rows(D_POOL, tr), halo_spec, _rows(D_POOL, tr), _rows(D_POOL, tr),
            _rows(D_POOL, tr), _const(ng, POOL_GROUP, POOL_GROUP), _const(1, D_POOL), _const(D, D), _const(1, D),
        ],
        out_specs=[
            _rows(D, tr), pl.BlockSpec((STAT_GROUPS, tr, HEADS_PER_STEP_BWD * V_HEAD), lambda i: (0, i, 0)),
            pl.BlockSpec((STAT_GROUPS, tr, 128), lambda i: (0, i, 0)),
            _rows(D_POOL, tr), _rows(D_POOL, tr), _rows(D_POOL, tr),
            _const(D, D), _const(ng, POOL_GROUP, POOL_GROUP), _const(1, D_POOL), _const(1, D), _const(1, 128),
        ],
        out_shape=[
            jax.ShapeDtypeStruct((N, D), F32), jax.ShapeDtypeStruct((STAT_GROUPS, N, HEADS_PER_STEP_BWD * V_HEAD), BF16),
            jax.ShapeDtypeStruct((STAT_GROUPS, N, 128), F32),
            jax.ShapeDtypeStruct((N, D_POOL), BF16), jax.ShapeDtypeStruct((N, D_POOL), BF16),
            jax.ShapeDtypeStruct((N, D_POOL), F32), jax.ShapeDtypeStruct((D, D), F32),
            jax.ShapeDtypeStruct((ng, POOL_GROUP, POOL_GROUP), F32),
            jax.ShapeDtypeStruct((1, D_POOL), F32), jax.ShapeDtypeStruct((1, D), F32), jax.ShapeDtypeStruct((1, 128), F32),
        ],
        compiler_params=_cparams(dimension_semantics=("arbitrary",)),
    )(h, tgt, pool_in, pool_in, pool_gate, attn_gate, attn, pool_w, pool_scale, wout, gf)


def _unrope(dy, cosv, sinv):
    return dy * cosv + _swap64(dy * sinv) * _low_lanes()


def _attn_bwd(q, k, v, do, lse, delta, cosf, sinf, dwout, dpw):
    tiles = _attn_tiles()
    hp = HEADS_PER_STEP_BWD
    n_g = HEADS // hp
    n_t = len(tiles)
    half = SHARD_OUT // 2
    half_pw = PW_ROWS // 2
    swap_at, send_at, sum_at = (0, 3), (0, 5), (n_g - 1, n_t // 2)

    def body(q_hbm, k_hbm, v_hbm, do_hbm, lse_ref, delta_ref, cos_ref, sin_ref, dwout_hbm, dpw_ref, dq_hbm, dkv_ref, dkr_ref,
             gwout_ref, gpw_ref, q_buf, k_buf, v_buf, do_buf, dq_buf, dk_acc, dv_acc, own_w, sib_w, stage_w, recv_w, gw_buf,
             pw_sib, pw_chip, pw_recv, pw_buf,
             in_sems, out_sems, ow_sems, d2d_send, d2d_recv, ici_send, ici_recv, fin_send, fin_recv,
             pw_swap_send, pw_swap_recv, pw_ici_send, pw_ici_recv, pw_fin_send, pw_fin_recv):
        grp = pl.program_id(0)
        step = pl.program_id(1)
        heads = pl.ds(grp * hp, hp)
        x, y, c = lax.axis_index("x"), lax.axis_index("y"), lax.axis_index("c")
        me = 2 * x + y
        sibling = (x, y, 1 - c)

        def pw_rows(core):
            return pl.ds(pl.multiple_of(half_pw * core, half_pw), half_pw)

        def pw_swap():
            return _remote(dpw_ref, pw_sib, pw_swap_send.at[0], pw_swap_recv.at[0], sibling)

        def pw_ici(rel):
            fx, fy = _CHIP_RELS[rel]
            return _remote(pw_chip.at[pw_rows(c), :], pw_recv.at[rel - 1], pw_ici_send.at[rel - 1], pw_ici_recv.at[rel - 1],
                           (x ^ fx, y ^ fy, c))

        def pw_fin(core):
            spot = pw_buf.at[pw_rows(core), :]
            return _remote(spot, spot, pw_fin_send.at[0], pw_fin_recv.at[0], sibling)

        def chip_of(rel):
            fx, fy = _CHIP_RELS[rel]
            return 2 * (x ^ fx) + (y ^ fy)

        def piece(chip, core):
            return dwout_hbm.at[pl.ds(pl.multiple_of(SHARD_OUT * chip + half * core, half), half), :]

        def own_load(rel):
            return pltpu.make_async_copy(piece(chip_of(rel), c), own_w.at[rel], ow_sems.at[rel])

        def d2d_copy(rel):
            return _remote(piece(chip_of(rel), 1 - c), sib_w.at[rel], d2d_send.at[rel], d2d_recv.at[rel], sibling)

        def ici_copy(rel):
            fx, fy = _CHIP_RELS[rel]
            return _remote(stage_w.at[rel - 1], recv_w.at[rel - 1], ici_send.at[rel - 1], ici_recv.at[rel - 1],
                           (x ^ fx, y ^ fy, c))

        def fin_copy(core):
            spot = gw_buf.at[pl.ds(pl.multiple_of(half * core, half), half), :]
            return _remote(spot, spot, fin_send.at[0], fin_recv.at[0], sibling)

        @pl.when((grp == 0) & (step == 0))
        def _():
            _peer_signal(x, y, c)
            for rel in SEND_ORDER + (0,):
                own_load(rel).start()

        @pl.when((grp == swap_at[0]) & (step == swap_at[1]))
        def _():
            _peer_wait()
            pw_swap().start()
            for rel in SEND_ORDER + (0,):
                d2d_copy(rel).start()

        @pl.when((grp == send_at[0]) & (step == send_at[1]))
        def _():
            for rel in SEND_ORDER:
                own_load(rel).wait()
                d2d_copy(rel).wait_recv()
                stage_w[rel - 1] = (own_w[rel] + sib_w[rel]).astype(BF16)
                ici_copy(rel).start()
            pw_swap().wait_recv()
            pw_chip[...] = dpw_ref[...] + pw_sib[...]
            for rel in SEND_ORDER:
                pw_ici(rel).start()

        @pl.when((grp == sum_at[0]) & (step == sum_at[1]))
        def _():
            own_load(0).wait()
            d2d_copy(0).wait_recv()
            total = own_w[0] + sib_w[0]
            for rel in (1, 2, 3):
                ici_copy(rel).wait_recv()
                total = total + recv_w[rel - 1].astype(F32)
            gw_buf[pl.ds(pl.multiple_of(half * c, half), half), :] = total
            fin_copy(c).start()
            for rel in (1, 2, 3):
                pw_ici(rel).wait_recv()
            total = jnp.zeros((half_pw, POOL_GROUP), F32)
            for chip in range(CHIPS):
                flips = chip ^ me
                rel = jnp.where(flips == 2, 1, jnp.where(flips == 1, 2, flips))
                total = total + jnp.where(rel == 0, pw_chip[pw_rows(c), :], pw_recv[jnp.maximum(rel - 1, 0)])
            pw_buf[pw_rows(c), :] = total
            pw_fin(c).start()

        def finish_dwout():
            fin_copy(1 - c).wait_recv()
            pw_fin(1 - c).wait_recv()
            for rel in (0, 1, 2, 3):
                d2d_copy(rel).wait_send()
            for rel in (1, 2, 3):
                ici_copy(rel).wait_send()
                pw_ici(rel).wait_send()
            fin_copy(c).wait_send()
            pw_swap().wait_send()
            pw_fin(c).wait_send()
            gwout_ref[...] = gw_buf[...]
            gpw_ref[...] = pw_buf[...]

        def loads(g, idx):
            q0, rows, _ = tiles[idx]
            rs = pl.ds(q0, rows)
            par = (g * n_t + idx) % 2
            hs = pl.ds(g * hp, hp)
            pairs = ((q_hbm.at[hs, rs, :], q_buf.at[:, rs, :]), (k_hbm.at[hs, rs, :], k_buf.at[:, rs, :]),
                     (v_hbm.at[hs, rs, :], v_buf.at[:, rs, :]), (do_hbm.at[g, rs, :], do_buf.at[rs, :]))
            return [pltpu.make_async_copy(src, dst, in_sems.at[a, par]) for a, (src, dst) in enumerate(pairs)]

        def store(idx):
            q0, rows, _ = tiles[idx]
            return pltpu.make_async_copy(dq_buf.at[idx % 2, :, pl.ds(0, rows), :], dq_hbm.at[heads, pl.ds(q0, rows), :],
                                         out_sems.at[idx % 2])

        @pl.when(step == 0)
        def _():
            dk_acc[...] = jnp.zeros_like(dk_acc)
            dv_acc[...] = jnp.zeros_like(dv_acc)

        @pl.when((step == 0) & (grp == 0))
        def _():
            dkr_ref[...] = jnp.zeros_like(dkr_ref)
            for cp in loads(grp, 0):
                cp.start()

        for idx, (q0, rows, klen) in enumerate(tiles):
            @pl.when(step == idx)
            def _(idx=idx, q0=q0, rows=rows, klen=klen):
                for cp in loads(grp, idx):
                    cp.wait()
                if idx + 1 < n_t:
                    for cp in loads(grp, idx + 1):
                        cp.start()
                if idx >= 2:
                    store(idx - 2).wait()
                qs = pl.ds(q0, rows)
                for hd in range(hp):
                    qv = q_buf[hd, qs, :]
                    kv = k_buf[hd, 0:klen, :]
                    p = jnp.exp(_masked_scores(qv, kv, rows, klen) - lse_ref[0, qs, hd:hd + 1])
                    dob = do_buf[qs, hd * V_HEAD:(hd + 1) * V_HEAD]
                    ds = (p * (_nt(dob, v_buf[hd, 0:klen, :]) - delta_ref[0, qs, hd:hd + 1])).astype(BF16)
                    dq = _nn(ds, kv) * SCALE
                    dq_buf[idx % 2, hd, 0:rows, 0:QK_NOPE] = dq[:, 0:QK_NOPE].astype(BF16)
                    dq_buf[idx % 2, hd, 0:rows, QK_NOPE:] = _unrope(dq[:, QK_NOPE:], cos_ref[qs, :], sin_ref[qs, :]).astype(BF16)
                    dk_acc[hd, 0:klen, :] += _tn(ds, qv)
                    dv_acc[hd, 0:klen, :] += _tn(p.astype(BF16), dob)
                store(idx).start()

        @pl.when(step == n_t - 1)
        def _():
            @pl.when(grp + 1 < n_g)
            def _():
                for cp in loads(grp + 1, 0):
                    cp.start()

            for hd in range(hp):
                dkv_ref[hd, :, 0:QK_NOPE] = dk_acc[hd, :, 0:QK_NOPE].astype(BF16)
                dkv_ref[hd, :, QK_NOPE:] = dv_acc[hd].astype(BF16)
                dkr_ref[...] += dk_acc[hd, :, QK_NOPE:]
            store(n_t - 2).wait()
            store(n_t - 1).wait()

            @pl.when(grp == n_g - 1)
            def _():
                finish_dwout()

    hbm = pl.BlockSpec(memory_space=pl.ANY)
    stat = pl.BlockSpec((1, N, 128), lambda g, t: (g, 0, 0), pipeline_mode=pl.Buffered(1))
    piece_f32 = lambda lead: pltpu.VMEM((lead, half, D), F32)
    piece_bf16 = lambda lead: pltpu.VMEM((lead, half, D), BF16)
    return pl.pallas_call(
        body,
        name="attn_bwd",
        grid=(n_g, n_t),
        in_specs=[hbm, hbm, hbm, hbm, stat, stat, _const(N, 128), _const(N, 128), hbm, _const(PW_ROWS, POOL_GROUP)],
        out_specs=[hbm, pl.BlockSpec((hp, N, 256), lambda g, t: (g, 0, 0), pipeline_mode=pl.Buffered(1)), _const(N, 128),
                   _const(SHARD_OUT, D), _const(PW_ROWS, POOL_GROUP)],
        out_shape=[
            jax.ShapeDtypeStruct((HEADS, N, 256), BF16), jax.ShapeDtypeStruct((HEADS, N, 256), BF16),
            jax.ShapeDtypeStruct((N, 128), F32), jax.ShapeDtypeStruct((SHARD_OUT, D), F32),
            jax.ShapeDtypeStruct((PW_ROWS, POOL_GROUP), F32),
        ],
        scratch_shapes=[pltpu.VMEM((hp, N, 256), BF16), pltpu.VMEM((hp, N, 256), BF16), pltpu.VMEM((hp, N, V_HEAD), BF16),
                        pltpu.VMEM((N, hp * V_HEAD), BF16), pltpu.VMEM((2, hp, TQ, 256), BF16),
                        pltpu.VMEM((hp, N, 256), F32), pltpu.VMEM((hp, N, V_HEAD), F32),
                        piece_f32(CHIPS), piece_f32(CHIPS), piece_bf16(3), piece_bf16(3), pltpu.VMEM((SHARD_OUT, D), F32),
                        pltpu.VMEM((PW_ROWS, POOL_GROUP), F32), pltpu.VMEM((PW_ROWS, POOL_GROUP), F32),
                        pltpu.VMEM((3, half_pw, POOL_GROUP), F32), pltpu.VMEM((PW_ROWS, POOL_GROUP), F32),
                        pltpu.SemaphoreType.DMA((4, 2)), pltpu.SemaphoreType.DMA((2,)), pltpu.SemaphoreType.DMA((CHIPS,)),
                        pltpu.SemaphoreType.DMA((CHIPS,)), pltpu.SemaphoreType.DMA((CHIPS,)),
                        pltpu.SemaphoreType.DMA((3,)), pltpu.SemaphoreType.DMA((3,)),
                        pltpu.SemaphoreType.DMA((1,)), pltpu.SemaphoreType.DMA((1,)),
                        pltpu.SemaphoreType.DMA((1,)), pltpu.SemaphoreType.DMA((1,)),
                        pltpu.SemaphoreType.DMA((3,)), pltpu.SemaphoreType.DMA((3,)),
                        pltpu.SemaphoreType.DMA((1,)), pltpu.SemaphoreType.DMA((1,))],
        compiler_params=_cparams(dimension_semantics=("arbitrary", "arbitrary"), collective_id=2),
    )(q, k, v, do, lse, delta, cosf, sinf, dwout, dpw)


def _bwd_in(h, dh2, dq, dkv, dkr, cq, ckv, dpl, dpg, dag, norm_g, win, gq, wq, gkv, wkv, cosf, sinf, adam_out):
    tr = ROWS_BWD
    nb = N // tr
    per = tr // HALO
    lead = HEAD_ROWS
    adam_rows = SHARD_OUT // nb

    def body(h_ref, dh2_ref, dq_ref, dkv_ref, dkr_ref, cq_ref, ckv_ref, dpl_ref, halo_ref, dpg_ref, dag_ref,
             g_ref, win_ref, gq_ref, wq_ref, gkv_ref, wkv_ref, cos_ref, sin_ref, aw_ref, ag_ref, am_ref, av_ref,
             gx_ref, dmeta_ref, dsl_ref, dwq_ref, dwkv_ref, dg_ref, dgq_ref, dgkv_ref, ago_ref, ad_ref, anm_ref, anv_ref,
             dh_buf, gx_sem):
        i = pl.program_id(0)
        grad_out = ag_ref[...]
        ago_ref[...] = grad_out
        ad_ref[...], anm_ref[...], anv_ref[...] = _adamw_math(aw_ref[...], grad_out, am_ref[...], av_ref[...])

        @pl.when(i == 0)
        def _():
            dwq_ref[...] = jnp.zeros_like(dwq_ref)
            dwkv_ref[...] = jnp.zeros_like(dwkv_ref)
            dg_ref[...] = jnp.zeros_like(dg_ref)
            dgq_ref[...] = jnp.zeros_like(dgq_ref)
            dgkv_ref[...] = jnp.zeros_like(dgkv_ref)

        row0 = i * tr
        h = h_ref[...]
        r = lax.rsqrt(jnp.mean(h * h, axis=-1, keepdims=True) + EPS)
        n = h * r
        gv = g_ref[...]
        cq = cq_ref[...]
        rq = lax.rsqrt(jnp.mean(cq * cq, axis=-1, keepdims=True) + EPS)
        nq = cq * rq
        gqv = gq_ref[...]
        cqn = (nq * gqv).astype(BF16)
        dcqn = jnp.zeros((tr, Q_LORA), F32)
        for hd in range(HEADS):
            dqf = dq_ref[hd]
            dcqn = dcqn + _nn(dqf, wq_ref[hd])
            dwq_ref[hd] += _tn(dqf, cqn)
        dgq_ref[...] += jnp.sum(dcqn * nq, axis=0, keepdims=True)
        dnq = dcqn * gqv
        dcq = rq * (dnq - nq * jnp.mean(dnq * nq, axis=-1, keepdims=True))

        ckv = ckv_ref[...]
        rkv = lax.rsqrt(jnp.mean(ckv * ckv, axis=-1, keepdims=True) + EPS)
        nkv = ckv * rkv
        gkvv = gkv_ref[...]
        ckvn = (nkv * gkvv).astype(BF16)
        dckvn = jnp.zeros((tr, KV_LORA), F32)
        for hd in range(HEADS):
            dkv = dkv_ref[hd]
            dckvn = dckvn + _nt(dkv, wkv_ref[hd])
            dwkv_ref[hd] += _tn(ckvn, dkv)
        dgkv_ref[...] += jnp.sum(dckvn * nkv, axis=0, keepdims=True)
        dnkv = dckvn * gkvv
        dckv = rkv * (dnkv - nkv * jnp.mean(dnkv * nkv, axis=-1, keepdims=True))
        dkr = _unrope(dkr_ref[...], cos_ref[...], sin_ref[...])

        cur = dpl_ref[...]
        halo = jnp.where(i < nb - 1, halo_ref[...], 0.0)
        dpi = []
        for g, w in enumerate(POOL_WINDOWS):
            sl = slice(g * POOL_GROUP, (g + 1) * POOL_GROUP)
            a = jnp.concatenate([cur[:, sl] * _inv_count(row0, tr, w), halo[:, sl] * _inv_count(row0 + tr, HALO, w)], axis=0)
            acc = a
            shift = 1
            while shift < w:
                acc = acc + pltpu.roll(acc, tr + HALO - shift, 0)
                shift *= 2
            dpi.append(acc[0:tr] - cur[:, sl])

        du = jnp.concatenate([t.astype(BF16) for t in dpi] + [dpg_ref[...]] + [t.astype(BF16) for t in (dcq, dckv, dkr)],
                             axis=1)
        dagb = dag_ref[...]
        by_row = jnp.concatenate(dpi + [dpg_ref[...].astype(F32), dcq, dckv, dkr[:, 0:QK_ROPE], dagb.astype(F32),
                                        jnp.zeros((tr, SHARD_PAD - SHARD_IN), F32)], axis=1)
        for chip in range(CHIPS):
            dsl_ref[chip] = by_row[:, SHARD_IN * chip:SHARD_IN * chip + SHARD_PAD].astype(BF16)
        dhn = _nn(du, win_ref[0:O_KR_END, :]) + _nn(dagb, win_ref[O_AG:D_IN, :])
        dg_ref[...] += jnp.sum(dhn * n, axis=0, keepdims=True)
        dn = dhn * gv
        dh = dh2_ref[...] + r * (dn - n * jnp.mean(dn * n, axis=-1, keepdims=True))

        first = pltpu.make_async_copy(dh_buf.at[pl.ds(lead, tr - lead), :], gx_ref.at[pl.ds(0, tr - lead), :], gx_sem)
        later = lambda step: pltpu.make_async_copy(
            dh_buf, gx_ref.at[pl.ds(pl.multiple_of(step * tr - lead, 16), tr), :], gx_sem)

        @pl.when(i == 1)
        def _():
            first.wait()

        @pl.when(i > 1)
        def _():
            later(i - 1).wait()

        dh_buf[...] = dh

        @pl.when(i == 0)
        def _():
            first.start()
            for chip in range(CHIPS):
                dmeta_ref[chip] = dh[PAD:HEAD_ROWS, chip * 256:(chip + 1) * 256]

        @pl.when(i > 0)
        def _():
            later(i).start()

        @pl.when(i == nb - 1)
        def _():
            later(i).wait()

    head = lambda w: pl.BlockSpec((HEADS, tr, w), lambda i: (0, i, 0))
    halo_spec = pl.BlockSpec((HALO, D_POOL), lambda i: (jnp.minimum((i + 1) * per, N // HALO - 1), 0))
    return pl.pallas_call(
        body,
        name="bwd_in",
        grid=(nb,),
        in_specs=[
            _rows(D, tr), _rows(D, tr), head(256), head(256), _rows(128, tr), _rows(Q_LORA, tr), _rows(KV_LORA, tr),
            _rows(D_POOL, tr), halo_spec, _rows(D_POOL, tr), _rows(D_POOL, tr),
            _const(1, D), _const(D_IN, D), _const(1, Q_LORA), _const(HEADS, 256, Q_LORA),
            _const(1, KV_LORA), _const(HEADS, KV_LORA, 256), _rows(128, tr), _rows(128, tr),
        ] + [_rows(D, adam_rows)] * 4,
        out_specs=[
            pl.BlockSpec(memory_space=pl.ANY), _const(CHIPS, N_META, 256),
            pl.BlockSpec((CHIPS, tr, SHARD_PAD), lambda i: (0, i, 0)), _const(HEADS, 256, Q_LORA),
            _const(HEADS, KV_LORA, 256), _const(1, D), _const(1, Q_LORA), _const(1, KV_LORA),
        ] + [_rows(D, adam_rows)] * 4,
        out_shape=[
            jax.ShapeDtypeStruct((S, D), F32), jax.ShapeDtypeStruct((CHIPS, N_META, 256), F32),
            jax.ShapeDtypeStruct((CHIPS, N, SHARD_PAD), BF16), jax.ShapeDtypeStruct((HEADS, 256, Q_LORA), F32),
            jax.ShapeDtypeStruct((HEADS, KV_LORA, 256), F32),
            jax.ShapeDtypeStruct((1, D), F32), jax.ShapeDtypeStruct((1, Q_LORA), F32), jax.ShapeDtypeStruct((1, KV_LORA), F32),
        ] + [jax.ShapeDtypeStruct((SHARD_OUT, D), F32)] * 4,
        scratch_shapes=[pltpu.VMEM((tr, D), F32), pltpu.SemaphoreType.DMA],
        compiler_params=_cparams(dimension_semantics=("arbitrary",)),
    )(h, dh2, dq, dkv, dkr, cq, ckv, dpl, dpl, dpg, dag, norm_g, win, gq, wq, gkv, wkv, cosf, sinf, *adam_out)


def _local_step(h, tgt, norm_g, win, gq, wq, gkv, wkv, pool_w, pool_scale, wout_s, m_wout_s, v_wout_s, gf, cosf, sinf):
    pool_in, pool_gate, cq, ckv, attn_gate, q, k, v, hn = _fwd_in(h, norm_g, win, gq, wq, gkv, wkv, cosf, sinf)
    attn, lse, wout = _attn_fwd(q, k, v, wout_s)
    dh2, do, delta, dag, dpg, dpl, dwout, dpw, dps, dgf, loss = _mid(
        h, tgt, pool_in, pool_gate, attn_gate, attn, pool_w, pool_scale, wout, gf)
    dq, dkv, dkr, gwout, gpw = _attn_bwd(q, k, v, do, lse, delta, cosf, sinf, dwout, dpw.reshape(PW_ROWS, POOL_GROUP))
    gx, dmeta, dsl, dwq, dwkv, dg, dgq, dgkv, *r_out = _bwd_in(
        h, dh2, dq, dkv, dkr, cq, ckv, dpl, dpg, dag, norm_g, win, gq, wq, gkv, wkv, cosf, sinf,
        (wout_s, gwout, m_wout_s, v_wout_s))
    return dict(gx=gx, dmeta=dmeta, dsl=dsl, hn=hn, dwq=dwq, dwkv=dwkv, r_out=tuple(r_out), dg=dg, dgq=dgq,
                dgkv=dgkv, gpw=gpw, dps=dps, dgf=dgf, loss=loss)


_CHIP_RELS = ((0, 0), (1, 0), (0, 1), (1, 1))

_ARR_ROWS = (SHARD_IN, SHARD_OUT, 256, KV_LORA, N_META)
_ARR_COLS = (D, D, Q_LORA, 256, 256)
_PIECES = (
    (0, 0, 256, 0), (0, 256, SHARD_IN - 256, 1),
    (1, 0, 128, 0), (1, 128, 128, 1),
    (2, 0, 128, 0), (2, 128, 128, 1),
    (3, 0, 64, 0), (3, 64, 64, 1),
    (4, 0, N_META, 0),
)
_NP = len(_PIECES)
_PIECE_MAX = (256, 128, 128, 64, N_META)


def _gathered_at(refs, arr, chip, r0, n):
    if arr in (0, 1):
        return refs[arr].at[pl.ds(pl.multiple_of(_ARR_ROWS[arr] * chip + r0, 16), n), :]
    return refs[arr].at[chip, pl.ds(r0, n), :]


def _remote(src, dst, send_sem, recv_sem, to):
    return pltpu.make_async_remote_copy(src_ref=src, dst_ref=dst, send_sem=send_sem, recv_sem=recv_sem,
                                        device_id=to, device_id_type=MESH)


def _gather_weights(winT_s, wqT_s, wkv_s, meta_s, x2, tgt2):
    arrays = (0, 2, 3, 4)

    def body(win_ref, wq_ref, wkv_ref, meta_ref, x_ref, t_ref, win_o, wq_o, wkv_o, h_o, tp_o,
             s_win, s_wq, s_wkv, meta_all, head_buf, x_buf, t_buf, ici_send, ici_recv, fwd_send, fwd_recv,
             loc_sems, own_sems):
        x, y, c = lax.axis_index("x"), lax.axis_index("y"), lax.axis_index("c")
        me = 2 * x + y
        stage = (s_win, None, s_wq, s_wkv, meta_ref)
        outs = (win_o, None, wq_o, wkv_o, meta_all)

        _peer_signal(x, y, c)

        frames = pl.ds(HEAD_ROWS, S)
        loads = [pltpu.make_async_copy(x_ref, x_buf, loc_sems.at[0]), pltpu.make_async_copy(t_ref, t_buf, loc_sems.at[1])]
        local = [pltpu.make_async_copy(x_buf, h_o.at[frames, :], loc_sems.at[0]),
                 pltpu.make_async_copy(t_buf, tp_o.at[frames, :], loc_sems.at[1])]
        for cp in loads:
            cp.start()

        s_win[...] = win_ref[...].astype(BF16)
        s_wq[0:QK, :] = wq_ref[...].astype(BF16)
        s_wq[QK:256, :] = jnp.zeros((256 - QK, Q_LORA), BF16)
        s_wkv[...] = wkv_ref[...].astype(BF16)
        head_buf[...] = jnp.zeros_like(head_buf)
        zeros = pltpu.make_async_copy(head_buf, tp_o.at[pl.ds(0, HEAD_ROWS), :], loc_sems.at[2])
        zeros.start()

        def chip_of(rel):
            fx, fy = _CHIP_RELS[rel]
            return 2 * (x ^ fx) + (y ^ fy)

        def same_core_of(rel):
            fx, fy = _CHIP_RELS[rel]
            return (x ^ fx, y ^ fy, c)

        def ici_copy(rel, i, src_chip, to):
            arr, r0, n, _ = _PIECES[i]
            k = (rel - 1) * _NP + i
            return _remote(stage[arr].at[pl.ds(r0, n), :], _gathered_at(outs, arr, src_chip, r0, n),
                           ici_send.at[k], ici_recv.at[k], to)

        def fwd_copy(rel, i, to):
            arr, r0, n, _ = _PIECES[i]
            k = (rel - 1) * _NP + i
            place = _gathered_at(outs, arr, chip_of(rel), r0, n)
            return _remote(place, place, fwd_send.at[k], fwd_recv.at[k], to)

        _peer_wait()
        for core in (0, 1):
            @pl.when(c == core)
            def _(core=core):
                mine = [i for i in range(_NP) if _PIECES[i][3] == core and _PIECES[i][0] in arrays]
                theirs = [i for i in range(_NP) if _PIECES[i][3] != core and _PIECES[i][0] in arrays]
                order = (1, 2, 3)
                sends = [ici_copy(rel, i, me, same_core_of(rel)) for rel in order for i in mine]
                for cp in sends:
                    cp.start()
                for ld, st in zip(loads, local):
                    ld.wait()
                    st.start()
                own = [pltpu.make_async_copy(stage[arr], _gathered_at(outs, arr, me, 0, _ARR_ROWS[arr]), own_sems.at[arr])
                       for arr in arrays if arr != 4]
                for cp in own:
                    cp.start()
                meta_all[me] = meta_ref[...]
                for rel in order:
                    for i in mine:
                        ici_copy(rel, i, chip_of(rel), (x, y, c)).wait_recv()
                        fwd = fwd_copy(rel, i, (x, y, 1 - c))
                        fwd.start()
                        sends.append(fwd)
                for rel in order:
                    for i in theirs:
                        fwd_copy(rel, i, (x, y, c)).wait_recv()
                for cp in sends:
                    cp.wait_send()
                for cp in own:
                    cp.wait()

        zeros.wait()
        for chip in range(CHIPS):
            head_buf[PAD:HEAD_ROWS, chip * 256:(chip + 1) * 256] = meta_all[chip]
        head = pltpu.make_async_copy(head_buf, h_o.at[pl.ds(0, HEAD_ROWS), :], loc_sems.at[2])
        head.start()
        head.wait()
        for cp in local:
            cp.wait()

    vm = pl.BlockSpec(memory_space=pltpu.VMEM)
    hbm = pl.BlockSpec(memory_space=pl.ANY)
    return pl.pallas_call(
        body,
        name="gather_weights",
        in_specs=[vm] * 4 + [hbm] * 2,
        out_specs=[hbm] * 5,
        out_shape=[
            jax.ShapeDtypeStruct((D_IN, D), BF16),
            jax.ShapeDtypeStruct((CHIPS, 256, Q_LORA), BF16), jax.ShapeDtypeStruct((CHIPS, KV_LORA, 256), BF16),
            jax.ShapeDtypeStruct((N, D), F32), jax.ShapeDtypeStruct((N, D), F32),
        ],
        scratch_shapes=[pltpu.VMEM((_ARR_ROWS[a], _ARR_COLS[a]), BF16) for a in (0, 2, 3)]
        + [pltpu.VMEM((CHIPS, N_META, 256), F32), pltpu.VMEM((HEAD_ROWS, D), F32), pltpu.VMEM((S, D), F32),
           pltpu.VMEM((S, D), F32)]
        + [pltpu.SemaphoreType.DMA((3 * _NP,))] * 4 + [pltpu.SemaphoreType.DMA((3,)), pltpu.SemaphoreType.DMA((4,))],
        compiler_params=_cparams(collective_id=0),
    )(winT_s, wqT_s, wkv_s, meta_s, x2, tgt2)


_SM_ROWS = (VEC_ROWS,)
_SM_COLS = (D,)
_SM_PIECES = ((0, 0, VEC_ROWS, 0),)
_NSP = len(_SM_PIECES)
_NSB = len(_SM_ROWS)


def _reduce_grads(dsl, hn, dwq, dwkv, dmeta4, dg, dgf, dgq, dgkv, dps, loss):
    arrays = (0, 2, 3, 4)
    loaded = (2, 3, 4)
    shard_order = SEND_ORDER + (0,)

    def body(dsl_hbm, hn_hbm, dwq_ref, dwkv_ref, dmeta_ref, dg_ref, dgf_ref, dgq_ref, dgkv_ref, dps_ref,
             loss_ref, gwin_o, gwq_o, gwkv_o, gmeta_o, gg_o, ggf_o, ggq_o, ggkv_o, gps_o, gloss_o,
             ow2, ow3, ow4, sb0, sb2, sb3, sb4, st0, st2, st3, st4, rc0, rc2, rc3, rc4,
             vec, sm_sb0, sm_cs0, sm_rc0, vec_fin, slab_v, hn_v, dwin_buf, own0,
             own_sems, d2d_send, d2d_recv, ici_send, ici_recv, fin_send, fin_recv,
             swap_send, swap_recv, smi_send, smi_recv, smf_send, smf_recv, ld_sems):
        x, y, c = lax.axis_index("x"), lax.axis_index("y"), lax.axis_index("c")
        me = 2 * x + y
        _peer_signal(x, y, c)
        grads = (None, None, dwq_ref, dwkv_ref, dmeta_ref)
        outs = (gwin_o, None, gwq_o, gwkv_o, gmeta_o)
        own_buf = (None, None, ow2, ow3, ow4)
        sib_buf = (sb0, None, sb2, sb3, sb4)
        stage = (st0, None, st2, st3, st4)
        recv = (rc0, None, rc2, rc3, rc4)
        sm_mine = (vec,)
        sm_sib = (sm_sb0,)
        sm_chip = (sm_cs0,)
        sm_recv = (sm_rc0,)
        sm_out = (vec_fin,)
        sibling = (x, y, 1 - c)

        def chip_of(rel):
            fx, fy = _CHIP_RELS[rel]
            return 2 * (x ^ fx) + (y ^ fy)

        def same_core_of(rel):
            fx, fy = _CHIP_RELS[rel]
            return (x ^ fx, y ^ fy, c)

        hn_load = pltpu.make_async_copy(hn_hbm, hn_v, ld_sems.at[CHIPS])

        def slab_load(rel):
            return pltpu.make_async_copy(dsl_hbm.at[chip_of(rel)], slab_v.at[rel], ld_sems.at[rel])

        hn_load.start()
        slab_load(shard_order[0]).start()

        def slot(bufs, i, idx):
            arr, _, n, _ = _PIECES[i]
            return bufs[arr].at[idx, pl.ds(0, n), :]

        def own_load(rel, i):
            arr, r0, n, _ = _PIECES[i]
            return pltpu.make_async_copy(_gathered_at(grads, arr, chip_of(rel), r0, n), slot(own_buf, i, rel),
                                         own_sems.at[rel * _NP + i])

        def d2d_copy(rel, i):
            arr, r0, n, _ = _PIECES[i]
            k = rel * _NP + i
            return _remote(_gathered_at(grads, arr, chip_of(rel), r0, n), slot(sib_buf, i, rel),
                           d2d_send.at[k], d2d_recv.at[k], sibling)

        def ici_copy(rel, i):
            k = (rel - 1) * _NP + i
            return _remote(slot(stage, i, rel - 1), slot(recv, i, rel - 1), ici_send.at[k], ici_recv.at[k],
                           same_core_of(rel))

        def fin_copy(i):
            arr, r0, n, _ = _PIECES[i]
            place = outs[arr].at[pl.ds(r0, n), :]
            return _remote(place, place, fin_send.at[i], fin_recv.at[i], sibling)

        def sm_ici_copy(rel, j):
            blk, r0, n, _ = _SM_PIECES[j]
            k = (rel - 1) * _NSP + j
            return _remote(sm_chip[blk].at[pl.ds(r0, n), :], sm_recv[blk].at[rel - 1, pl.ds(r0, n), :],
                           smi_send.at[k], smi_recv.at[k], same_core_of(rel))

        def sm_fin_copy(j):
            blk, r0, n, _ = _SM_PIECES[j]
            place = sm_out[blk].at[pl.ds(r0, n), :]
            return _remote(place, place, smf_send.at[j], smf_recv.at[j], sibling)

        vec[...] = jnp.zeros_like(vec)
        vec[0:1, :] = dg_ref[...]
        vec[1:2, :] = dgf_ref[...]
        vec[2:3, V_GQ:V_GQ + Q_LORA] = dgq_ref[...]
        vec[2:3, V_GKV:V_GKV + KV_LORA] = dgkv_ref[...]
        vec[2:3, V_PS:V_PS + D_POOL] = dps_ref[...]
        vec[2:3, V_LOSS:D] = loss_ref[...]
        _peer_wait()
        swaps = [_remote(sm_mine[b], sm_sib[b], swap_send.at[b], swap_recv.at[b], sibling) for b in range(_NSB)]
        for cp in swaps:
            cp.start()

        for core in (0, 1):
            @pl.when(c == core)
            def _(core=core):
                mine = [i for i in range(_NP) if _PIECES[i][3] == core and _PIECES[i][0] in loaded]
                theirs = [i for i in range(_NP) if _PIECES[i][3] != core and _PIECES[i][0] in loaded]
                i0 = next(i for i in range(_NP) if _PIECES[i][0] == 0 and _PIECES[i][3] == core)
                j0 = next(i for i in range(_NP) if _PIECES[i][0] == 0 and _PIECES[i][3] != core)
                sm_mine_p = [j for j in range(_NSP) if _SM_PIECES[j][3] == core]
                sm_theirs_p = [j for j in range(_NSP) if _SM_PIECES[j][3] != core]
                sends = list(swaps)

                for rel in shard_order:
                    for i in theirs:
                        cp = d2d_copy(rel, i)
                        cp.start()
                        sends.append(cp)
                    for i in mine:
                        own_load(rel, i).start()

                def piece_rows(i):
                    return pl.ds(_PIECES[i][1], _PIECES[i][2])

                def form(rel, i):
                    r0, n = _PIECES[i][1], _PIECES[i][2]
                    dwin_buf[rel, r0:r0 + n, :] = _tn(slab_v[rel, :, r0:r0 + _PIECE_MAX[0]], hn_v[...])[0:n, :]

                def d2d0(rel, i):
                    return _remote(dwin_buf.at[rel, piece_rows(i), :], slot(sib_buf, i, rel),
                                   d2d_send.at[rel * _NP + i], d2d_recv.at[rel * _NP + i], sibling)

                def settle(rel):
                    d2d0(rel, i0).wait_recv()
                    total = dwin_buf[rel, piece_rows(i0), :] + slot(sib_buf, i0, rel)[...]
                    if rel == 0:
                        own0[0:_PIECES[i0][2], :] = total
                    else:
                        slot(stage, i0, rel - 1)[...] = total.astype(BF16)
                        cp = ici_copy(rel, i0)
                        cp.start()
                        sends.append(cp)

                for rel in SEND_ORDER:
                    for i in mine:
                        arr, r0, n, _ = _PIECES[i]
                        own_load(rel, i).wait()
                        d2d_copy(rel, i).wait_recv()
                        total = slot(own_buf, i, rel)[...] + slot(sib_buf, i, rel)[...]
                        slot(stage, i, rel - 1)[...] = total.astype(stage[arr].dtype)
                        cp = ici_copy(rel, i)
                        cp.start()
                        sends.append(cp)

                for b in range(_NSB):
                    swaps[b].wait_recv()
                    sm_chip[b][...] = sm_mine[b][...] + sm_sib[b][...]
                for rel in SEND_ORDER:
                    for j in sm_mine_p:
                        cp = sm_ici_copy(rel, j)
                        cp.start()
                        sends.append(cp)

                hn_load.wait()
                for n, rel in enumerate(shard_order):
                    slab_load(rel).wait()
                    if n == 0:
                        for later in shard_order[1:]:
                            slab_load(later).start()
                    form(rel, j0)
                    cp = d2d0(rel, j0)
                    cp.start()
                    sends.append(cp)
                    form(rel, i0)
                    settle(rel)

                for i in mine:
                    arr, r0, n, _ = _PIECES[i]
                    own_load(0, i).wait()
                    d2d_copy(0, i).wait_recv()
                    total = slot(own_buf, i, 0)[...] + slot(sib_buf, i, 0)[...]
                    for rel in (1, 2, 3):
                        ici_copy(rel, i).wait_recv()
                        total = total + slot(recv, i, rel - 1)[...].astype(F32)
                    outs[arr][pl.ds(r0, n), :] = total
                    cp = fin_copy(i)
                    cp.start()
                    sends.append(cp)
                for j in sm_mine_p:
                    blk, r0, n, _ = _SM_PIECES[j]
                    for rel in (1, 2, 3):
                        sm_ici_copy(rel, j).wait_recv()
                    total = jnp.zeros((n, _SM_COLS[blk]), F32)
                    for chip in range(CHIPS):
                        flips = chip ^ me
                        rel = jnp.where(flips == 2, 1, jnp.where(flips == 1, 2, flips))
                        theirs_rows = sm_recv[blk][jnp.maximum(rel - 1, 0), pl.ds(r0, n), :]
                        total = total + jnp.where(rel == 0, sm_chip[blk][pl.ds(r0, n), :], theirs_rows)
                    sm_out[blk][pl.ds(r0, n), :] = total
                    cp = sm_fin_copy(j)
                    cp.start()
                    sends.append(cp)

                total = own0[0:_PIECES[i0][2], :]
                for rel in (1, 2, 3):
                    ici_copy(rel, i0).wait_recv()
                    total = total + slot(recv, i0, rel - 1)[...].astype(F32)
                outs[0][pl.ds(_PIECES[i0][1], _PIECES[i0][2]), :] = total
                cp = fin_copy(i0)
                cp.start()
                sends.append(cp)

                for i in theirs + [j0]:
                    fin_copy(i).wait_recv()
                for j in sm_theirs_p:
                    sm_fin_copy(j).wait_recv()
                for cp in sends:
                    cp.wait_send()

        gg_o[...] = vec_fin[0:1, :]
        ggf_o[...] = vec_fin[1:2, :]
        ggq_o[...] = vec_fin[2:3, V_GQ:V_GQ + Q_LORA]
        ggkv_o[...] = vec_fin[2:3, V_GKV:V_GKV + KV_LORA]
        gps_o[...] = vec_fin[2:3, V_PS:V_PS + D_POOL]
        gloss_o[...] = vec_fin[2:3, V_LOSS:D]

    vm = pl.BlockSpec(memory_space=pltpu.VMEM)
    piece_buf = lambda lead, dtype, which=arrays: [
        pltpu.VMEM((lead, _PIECE_MAX[a], _ARR_COLS[a]), F32 if a == 4 else dtype) for a in which]
    sm_buf = lambda *lead: [pltpu.VMEM(lead + (_SM_ROWS[b], _SM_COLS[b]), F32) for b in range(_NSB)]
    dma = lambda n: [pltpu.SemaphoreType.DMA((n,))] * 2
    return pl.pallas_call(
        body,
        name="reduce_grads",
        in_specs=[pl.BlockSpec(memory_space=pl.ANY)] * 4 + [vm] * 7,
        out_specs=[vm] * 10,
        out_shape=[jax.ShapeDtypeStruct((_ARR_ROWS[a], _ARR_COLS[a]), F32) for a in arrays]
        + [jax.ShapeDtypeStruct((1, D), F32),
           jax.ShapeDtypeStruct((1, D), F32), jax.ShapeDtypeStruct((1, Q_LORA), F32),
           jax.ShapeDtypeStruct((1, KV_LORA), F32), jax.ShapeDtypeStruct((1, D_POOL), F32),
           jax.ShapeDtypeStruct((1, 128), F32)],
        scratch_shapes=piece_buf(CHIPS, F32, loaded) + piece_buf(CHIPS, F32) + piece_buf(3, BF16) + piece_buf(3, BF16)
        + [pltpu.VMEM((VEC_ROWS, D), F32)] + sm_buf() + sm_buf() + sm_buf(3) + [pltpu.VMEM((VEC_ROWS, D), F32)]
        + [pltpu.VMEM((CHIPS, N, SHARD_PAD), BF16), pltpu.VMEM((N, D), BF16),
           pltpu.VMEM((CHIPS, SHARD_PAD, D), F32), pltpu.VMEM((_PIECE_MAX[0], D), F32)]
        + [pltpu.SemaphoreType.DMA((CHIPS * _NP,))]
        + dma(CHIPS * _NP) + dma(3 * _NP) + dma(_NP) + dma(_NSB) + dma(3 * _NSP) + dma(_NSP)
        + [pltpu.SemaphoreType.DMA((CHIPS + 1,))],
        compiler_params=_cparams(collective_id=3),
    )(dsl, hn, dwq, dwkv, dmeta4, dg, dgf, dgq, dgkv, dps, loss)


def _adamw_math(w, g, m, v):
    m = B1 * m + (1.0 - B1) * g
    v = B2 * v + (1.0 - B2) * (g * g)
    m_hat = m / C1
    v_hat = v / C2
    delta = -LR * (m_hat / (jnp.sqrt(v_hat) + ADAM_EPS) + WD * w)
    return delta, m, v


def _adamw(big, block_rows, groups):
    rows, cols = big[0].shape
    n = len(groups)

    def body(*refs):
        w_ref, g_ref, m_ref, v_ref = refs[0:4]
        small_in = refs[4:4 + 4 * n]
        go_ref, d_ref, nm_ref, nv_ref = refs[4 + 4 * n:8 + 4 * n]
        small_out = refs[8 + 4 * n:]
        g = g_ref[...]
        go_ref[...] = g
        d_ref[...], nm_ref[...], nv_ref[...] = _adamw_math(w_ref[...], g, m_ref[...], v_ref[...])

        @pl.when(pl.program_id(0) == 0)
        def _():
            for t in range(n):
                sw_ref, sg_ref, sm_ref, sv_ref = small_in[4 * t:4 * t + 4]
                sg = sg_ref[0:sw_ref.shape[0], :]
                small_out[4 * t][...] = sg
                small_out[4 * t + 1][...], small_out[4 * t + 2][...], small_out[4 * t + 3][...] = _adamw_math(
                    sw_ref[...], sg, sm_ref[...], sv_ref[...])

    spec = pl.BlockSpec((block_rows, cols), lambda i: (i, 0))
    vm = pl.BlockSpec(memory_space=pltpu.VMEM)
    outs = pl.pallas_call(
        body,
        name="adamw",
        grid=(rows // block_rows,),
        in_specs=[spec] * 4 + [vm] * (4 * n),
        out_specs=[spec] * 4 + [vm] * (4 * n),
        out_shape=[jax.ShapeDtypeStruct(big[0].shape, F32)] * 4
        + [jax.ShapeDtypeStruct(grp[0].shape, F32) for grp in groups for _ in range(4)],
        compiler_params=_cparams(dimension_semantics=("arbitrary",)),
    )(*big, *[a for grp in groups for a in grp])
    return tuple(outs[0:4]), [tuple(outs[4 + 4 * t:8 + 4 * t]) for t in range(n)]


def _rope_tables():
    half = QK_ROPE // 2
    f32 = np.float32
    inv_freq = (f32(1.0) / (f32(ROPE_THETA) ** (np.arange(half, dtype=f32) / f32(half)))).astype(f32)
    pos = np.arange(N, dtype=f32) - f32(PAD)
    ang = (pos[:, None] * inv_freq[None, :]).astype(f32)
    cos, sin = np.cos(ang).astype(f32), np.sin(ang).astype(f32)
    zero = np.zeros((N, 128 - QK_ROPE), f32)
    return jnp.asarray(np.concatenate([cos, cos, zero], axis=1)), jnp.asarray(np.concatenate([-sin, sin, zero], axis=1))


def kernel(x, meta_tokens, norm_g, w_in, q_norm_g, w_q_b, kv_norm_g, w_kv_b, pool_w, pool_scale, w_out, final_norm_g, loss_target, m_meta_tokens, m_norm_g, m_w_in, m_q_norm_g, m_w_q_b, m_kv_norm_g, m_w_kv_b, m_pool_w, m_pool_scale, m_w_out, m_final_norm_g, v_meta_tokens, v_norm_g, v_w_in, v_q_norm_g, v_w_q_b, v_kv_norm_g, v_w_kv_b, v_pool_w, v_pool_scale, v_w_out, v_final_norm_g):
    tr = lambda a: a[0].T
    win, wq, wkv, h, tgt = _gather_weights(tr(w_in), tr(w_q_b), w_kv_b[0], meta_tokens, x[0], loss_target[0])
    cosf, sinf = _rope_tables()
    gf = final_norm_g.reshape(1, D)

    part = _local_step(h, tgt, norm_g, win, q_norm_g, wq, kv_norm_g, wkv, pool_w[0], pool_scale, w_out[0], m_w_out[0],
                       v_w_out[0], gf, cosf, sinf)

    pw2 = lambda a: a.reshape(len(POOL_WINDOWS) * POOL_GROUP, POOL_GROUP)
    gpw = part["gpw"]
    gwinT, gwqT, gwkv, gmeta, gg, ggf, ggq, ggkv, gps, gloss = _reduce_grads(
        part["dsl"], part["hn"], part["dwq"], part["dwkv"], part["dmeta"], part["dg"],
        part["dgf"], part["dgq"], part["dgkv"], part["dps"], part["loss"])

    r_out = part["r_out"]
    fn2 = lambda a: a.reshape(1, D)
    r_in, (r_meta, r_norm, r_gq, r_wq, r_gkv, r_wkv, r_pw, r_ps, r_fn) = _adamw((tr(w_in), gwinT, tr(m_w_in), tr(v_w_in)), 248, [
        (meta_tokens, gmeta, m_meta_tokens, v_meta_tokens),
        (norm_g, gg, m_norm_g, v_norm_g),
        (q_norm_g, ggq, m_q_norm_g, v_q_norm_g),
        (tr(w_q_b), gwqT, tr(m_w_q_b), tr(v_w_q_b)),
        (kv_norm_g, ggkv, m_kv_norm_g, v_kv_norm_g),
        (w_kv_b[0], gwkv, m_w_kv_b[0], v_w_kv_b[0]),
        (pw2(pool_w), gpw, pw2(m_pool_w), pw2(v_pool_w)),
        (pool_scale, gps, m_pool_scale, v_pool_scale),
        (fn2(final_norm_g), ggf, fn2(m_final_norm_g), fn2(v_final_norm_g)),
    ])
    untr = lambda a: a.T[None]
    pw4 = lambda a: a.reshape(1, len(POOL_WINDOWS), POOL_GROUP, POOL_GROUP)
    per_kind = [[
        r_meta[kind], r_norm[kind], untr(r_in[kind]), r_gq[kind], untr(r_wq[kind]), r_gkv[kind], r_wkv[kind][None],
        pw4(r_pw[kind]), r_ps[kind], r_out[kind][None], r_fn[kind].reshape(D),
    ] for kind in range(4)]
    return (gloss[0, 0], part["gx"][None], *per_kind[0], *per_kind[1], *per_kind[2], *per_kind[3])
```

```python
import jax
import jax.numpy as jnp
import numpy as np
from jax import lax
from jax.experimental import pallas as pl
from jax.experimental.pallas import tpu as pltpu

F32 = jnp.float32
BF16 = jnp.bfloat16

D = 1024
S = 2048
N_META = 16
PAD = 112
HEAD_ROWS = PAD + N_META
N = HEAD_ROWS + S
D_POOL = 512
POOL_WINDOWS = (2, 4, 8, 16)
POOL_GROUP = 128
HALO = 16
HEADS = 4
QK_NOPE = 128
QK_ROPE = 64
QK = QK_NOPE + QK_ROPE
V_HEAD = 128
Q_LORA = 256
KV_LORA = 128
D_IN = 1984
EPS = 1e-6
ROPE_THETA = 10000.0
SCALE = QK ** -0.5
CHIPS = 4

ROWS_FWD = 544
ROWS_MID = 544
ROWS_BWD = 544
TK = 128
TQ = 256
NQ = S // TQ
HEADS_PER_STEP_BWD = 2

O_PI, O_PG, O_CQ, O_CKV, O_KR, O_AG = 0, 512, 1024, 1280, 1408, 1472
O_KR_END = O_KR + 128
SHARD_IN = D_IN // CHIPS
SHARD_PAD = 512
SHARD_OUT = D // CHIPS

LR, B1, B2, ADAM_EPS, WD, STEP = 0.001, 0.9, 0.999, 1e-08, 0.01, 10
C1 = 1.0 - B1**STEP
C2 = 1.0 - B2**STEP

VMEM_LIMIT = 60 * 1024 * 1024
MESH = pl.DeviceIdType.MESH
NEG = -1e30

VEC_ROWS = 8
PW_ROWS = len(POOL_WINDOWS) * POOL_GROUP
V_GQ, V_GKV, V_PS, V_LOSS = 0, 256, 384, 896


def _cparams(**kw):
    return pltpu.CompilerParams(vmem_limit_bytes=VMEM_LIMIT, **kw)


def _nt(a, b):
    return lax.dot_general(a, b, (((1,), (1,)), ((), ())), preferred_element_type=F32)


def _tn(a, b):
    return lax.dot_general(a, b, (((0,), (0,)), ((), ())), preferred_element_type=F32)


def _nn(a, b):
    return jnp.dot(a, b, preferred_element_type=F32)


def _swap64(t):
    return pltpu.roll(t, 32, 1) + pltpu.roll(t, 96, 1)


def _sigmoid(x):
    return 1.0 / (1.0 + jnp.exp(-x))


def _low_lanes():
    return (lax.broadcasted_iota(jnp.int32, (1, 128), 1) < QK_ROPE).astype(F32)


def _rows(w, rows):
    return pl.BlockSpec((rows, w), lambda i: (i, 0))


def _const(*shape):
    return pl.BlockSpec(shape, lambda *_: (0,) * len(shape), pipeline_mode=pl.Buffered(1))


STAT_GROUPS = HEADS // HEADS_PER_STEP_BWD


def _stat_slot(head):
    return head // HEADS_PER_STEP_BWD, head % HEADS_PER_STEP_BWD


N_PEERS = 4
SEND_ORDER = (3, 1, 2)


def _peer_signal(x, y, c):
    barrier = pltpu.get_barrier_semaphore()
    peers = [(x, y, 1 - c)] + [(x ^ fx, y ^ fy, c) for fx, fy in _CHIP_RELS[1:]]
    assert len(peers) == N_PEERS
    for peer in peers:
        pl.semaphore_signal(barrier, inc=1, device_id=peer, device_id_type=MESH)


def _peer_wait():
    pl.semaphore_wait(pltpu.get_barrier_semaphore(), N_PEERS)


def _attn_tiles():
    return [(0, TK, TK)] + [(TK + TQ * t, TQ, TK + TQ * (t + 1)) for t in range(NQ)]


def _masked_scores(q, k, rows, klen):
    s = _nt(q, k)
    col = lax.broadcasted_iota(jnp.int32, (1, TK), 1)
    head_bias = jnp.where(col >= PAD, 0.0, NEG)
    if klen == TK:
        return s + head_bias
    r = lax.broadcasted_iota(jnp.int32, (rows, 1), 0) >> 6
    c = lax.broadcasted_iota(jnp.int32, (1, rows), 1) >> 6
    diag_bias = jnp.where(c <= r, 0.0, NEG)
    parts = [s[:, 0:TK] + head_bias]
    if klen - rows > TK:
        parts.append(s[:, TK:klen - rows])
    parts.append(s[:, klen - rows:klen] + diag_bias)
    return jnp.concatenate(parts, axis=1)


def _fwd_in(h, norm_g, win, gq, wq, gkv, wkv, cosf, sinf):
    tr = ROWS_FWD

    def body(h_ref, g_ref, win_ref, gq_ref, wq_ref, gkv_ref, wkv_ref, cos_ref, sin_ref,
             pi_ref, pg_ref, cq_ref, ckv_ref, ag_ref, q_ref, k_ref, v_ref, hn_ref):
        h = h_ref[...]
        r = lax.rsqrt(jnp.mean(h * h, axis=-1, keepdims=True) + EPS)
        hn = ((h * r) * g_ref[...]).astype(BF16)
        hn_ref[...] = hn
        u = _nt(hn, win_ref[0:O_KR_END, :])
        pi_ref[...] = u[:, O_PI:O_PG]
        pg_ref[...] = u[:, O_PG:O_CQ]
        cq = u[:, O_CQ:O_CKV]
        ckv = u[:, O_CKV:O_KR]
        cq_ref[...] = cq
        ckv_ref[...] = ckv
        ag_ref[...] = _nt(hn, win_ref[O_AG:D_IN, :])
        cosv = cos_ref[...]
        sinv = sin_ref[...]
        kr = u[:, O_KR:O_KR_END] * _low_lanes()
        kr = (kr * cosv + _swap64(kr) * sinv).astype(BF16)
        rq = lax.rsqrt(jnp.mean(cq * cq, axis=-1, keepdims=True) + EPS)
        cqn = ((cq * rq) * gq_ref[...]).astype(BF16)
        rkv = lax.rsqrt(jnp.mean(ckv * ckv, axis=-1, keepdims=True) + EPS)
        ckvn = ((ckv * rkv) * gkv_ref[...]).astype(BF16)
        for hd in range(HEADS):
            qh = _nt(cqn, wq_ref[hd]) * SCALE
            z = qh[:, QK_NOPE:]
            q_ref[hd, :, 0:QK_NOPE] = qh[:, 0:QK_NOPE].astype(BF16)
            q_ref[hd, :, QK_NOPE:] = (z * cosv + _swap64(z) * sinv).astype(BF16)
            kvh = _nn(ckvn, wkv_ref[hd])
            k_ref[hd, :, 0:QK_NOPE] = kvh[:, 0:QK_NOPE].astype(BF16)
            k_ref[hd, :, QK_NOPE:] = kr
            v_ref[hd] = kvh[:, QK_NOPE:].astype(BF16)

    head = lambda w: pl.BlockSpec((HEADS, tr, w), lambda i: (0, i, 0))
    return pl.pallas_call(
        body,
        name="fwd_in",
        grid=(N // tr,),
        in_specs=[
            _rows(D, tr), _const(1, D), _const(D_IN, D), _const(1, Q_LORA), _const(HEADS, 256, Q_LORA),
            _const(1, KV_LORA), _const(HEADS, KV_LORA, 256), _rows(128, tr), _rows(128, tr),
        ],
        out_specs=[_rows(D_POOL, tr), _rows(D_POOL, tr), _rows(Q_LORA, tr), _rows(KV_LORA, tr), _rows(D_POOL, tr),
                   head(256), head(256), head(V_HEAD), _rows(D, tr)],
        out_shape=[
            jax.ShapeDtypeStruct((N, D_POOL), F32), jax.ShapeDtypeStruct((N, D_POOL), F32),
            jax.ShapeDtypeStruct((N, Q_LORA), F32), jax.ShapeDtypeStruct((N, KV_LORA), F32),
            jax.ShapeDtypeStruct((N, D_POOL), F32),
            jax.ShapeDtypeStruct((HEADS, N, 256), BF16), jax.ShapeDtypeStruct((HEADS, N, 256), BF16),
            jax.ShapeDtypeStruct((HEADS, N, V_HEAD), BF16), jax.ShapeDtypeStruct((N, D), BF16),
        ],
        compiler_params=_cparams(dimension_semantics=("arbitrary",)),
    )(h, norm_g, win, gq, wq, gkv, wkv, cosf, sinf)


def _attn_fwd(q, k, v, wout_s):
    tiles = _attn_tiles()
    n_t = len(tiles)
    half = SHARD_OUT // 2
    send_step = 2
    fwd_step = n_t - 2

    def body(q_hbm, k_hbm, v_hbm, wout_ref, o_hbm, lse_ref, wout_o, q_buf, k_buf, v_buf, o_buf, s_wout, in_sems, out_sems,
             ici_send, ici_recv, fwd_send, fwd_recv, own_sem):
        step = pl.program_id(0)
        x, y, c = lax.axis_index("x"), lax.axis_index("y"), lax.axis_index("c")
        me = 2 * x + y

        def chip_of(rel):
            fx, fy = _CHIP_RELS[rel]
            return 2 * (x ^ fx) + (y ^ fy)

        def place(chip, core):
            return wout_o.at[pl.ds(pl.multiple_of(SHARD_OUT * chip + half * core, half), half), :]

        def ici_copy(rel, src_chip, to):
            return _remote(s_wout.at[pl.ds(pl.multiple_of(half * c, half), half), :], place(src_chip, c),
                           ici_send.at[rel - 1], ici_recv.at[rel - 1], to)

        def fwd_copy(rel, core, to):
            spot = place(chip_of(rel), core)
            return _remote(spot, spot, fwd_send.at[rel - 1], fwd_recv.at[rel - 1], to)

        own = pltpu.make_async_copy(s_wout, wout_o.at[pl.ds(pl.multiple_of(SHARD_OUT * me, SHARD_OUT), SHARD_OUT), :], own_sem)

        @pl.when(step == 0)
        def _():
            _peer_signal(x, y, c)
            s_wout[...] = wout_ref[...].astype(BF16)
            own.start()

        @pl.when(step == send_step)
        def _():
            _peer_wait()
            for rel in SEND_ORDER:
                fx, fy = _CHIP_RELS[rel]
                ici_copy(rel, me, (x ^ fx, y ^ fy, c)).start()

        @pl.when(step == fwd_step)
        def _():
            for rel in (1, 2, 3):
                ici_copy(rel, chip_of(rel), (x, y, c)).wait_recv()
                fwd_copy(rel, c, (x, y, 1 - c)).start()

        def finish_wout():
            for rel in (1, 2, 3):
                fwd_copy(rel, 1 - c, (x, y, c)).wait_recv()
            for rel in (1, 2, 3):
                ici_copy(rel, me, (x, y, c)).wait_send()
                fwd_copy(rel, c, (x, y, c)).wait_send()
            own.wait()

        def loads(idx):
            q0, rows, _ = tiles[idx]
            rs = pl.ds(q0, rows)
            return [pltpu.make_async_copy(src.at[:, rs, :], dst.at[:, rs, :], in_sems.at[a, idx % 2])
                    for a, (src, dst) in enumerate(((q_hbm, q_buf), (k_hbm, k_buf), (v_hbm, v_buf)))]

        def store(idx):
            q0, rows, _ = tiles[idx]
            return pltpu.make_async_copy(o_buf.at[idx % 2, pl.ds(0, rows), :], o_hbm.at[pl.ds(q0, rows), :],
                                         out_sems.at[idx % 2])

        @pl.when(step == 0)
        def _():
            lse_ref[...] = jnp.zeros_like(lse_ref)
            for cp in loads(0):
                cp.start()

        for idx, (q0, rows, klen) in enumerate(tiles):
            @pl.when(step == idx)
            def _(idx=idx, q0=q0, rows=rows, klen=klen):
                for cp in loads(idx):
                    cp.wait()
                if idx + 1 < n_t:
                    for cp in loads(idx + 1):
                        cp.start()
                if idx >= 2:
                    store(idx - 2).wait()
                for hd in range(HEADS):
                    s = _masked_scores(q_buf[hd, q0:q0 + rows, :], k_buf[hd, 0:klen, :], rows, klen)
                    m = jnp.max(s, axis=-1, keepdims=True)
                    p = jnp.exp(s - m)
                    l = jnp.sum(p, axis=-1, keepdims=True)
                    o_buf[idx % 2, 0:rows, hd * V_HEAD:(hd + 1) * V_HEAD] = _nn(p.astype(BF16), v_buf[hd, 0:klen, :]) / l
                    grp, lane = _stat_slot(hd)
                    lse_ref[grp, q0:q0 + rows, lane:lane + 1] = m + jnp.log(l)
                store(idx).start()
                if idx == n_t - 1:
                    store(idx - 1).wait()
                    store(idx).wait()
                    finish_wout()

    hbm = pl.BlockSpec(memory_space=pl.ANY)
    return pl.pallas_call(
        body,
        name="attn_fwd",
        grid=(n_t,),
        in_specs=[hbm, hbm, hbm, _const(SHARD_OUT, D)],
        out_specs=[hbm, _const(STAT_GROUPS, N, 128), hbm],
        out_shape=[jax.ShapeDtypeStruct((N, HEADS * V_HEAD), F32), jax.ShapeDtypeStruct((STAT_GROUPS, N, 128), F32),
                   jax.ShapeDtypeStruct((D, D), BF16)],
        scratch_shapes=[pltpu.VMEM((HEADS, N, 256), BF16), pltpu.VMEM((HEADS, N, 256), BF16),
                        pltpu.VMEM((HEADS, N, V_HEAD), BF16), pltpu.VMEM((2, TQ, HEADS * V_HEAD), F32),
                        pltpu.VMEM((SHARD_OUT, D), BF16),
                        pltpu.SemaphoreType.DMA((3, 2)), pltpu.SemaphoreType.DMA((2,))]
        + [pltpu.SemaphoreType.DMA((3,))] * 4 + [pltpu.SemaphoreType.DMA],
        compiler_params=_cparams(dimension_semantics=("arbitrary",), collective_id=1),
    )(q, k, v, wout_s)


def _inv_count(row0, rows, w):
    row = row0 + lax.broadcasted_iota(jnp.int32, (rows, 1), 0)
    return 1.0 / jnp.clip(row - (PAD - 1), 1, w).astype(F32)


def _mid(h, tgt, pool_in, pool_gate, attn_gate, attn, pool_w, pool_scale, wout, gf):
    tr = ROWS_MID
    per = tr // HALO
    ng = len(POOL_WINDOWS)

    def body(h_ref, t_ref, pin_ref, halo_ref, pg_ref, ag_ref, at_ref, pw_ref, ps_ref, wout_ref, gf_ref,
             dh2_ref, do_ref, delta_ref, dag_ref, dpg_ref, dpl_ref, dwout_ref, dpw_ref, dps_ref, dgf_ref, loss_ref):
        i = pl.program_id(0)

        @pl.when(i == 0)
        def _():
            dwout_ref[...] = jnp.zeros_like(dwout_ref)
            dpw_ref[...] = jnp.zeros_like(dpw_ref)
            dps_ref[...] = jnp.zeros_like(dps_ref)
            dgf_ref[...] = jnp.zeros_like(dgf_ref)
            loss_ref[...] = jnp.zeros_like(loss_ref)

        row0 = i * tr
        real = (row0 + lax.broadcasted_iota(jnp.int32, (tr, 1), 0)) >= HEAD_ROWS
        h = h_ref[...]

        halo = jnp.where(i > 0, halo_ref[...], 0.0)
        ext = jnp.concatenate([halo, pin_ref[...]], axis=0)
        pooled = []
        for g, w in enumerate(POOL_WINDOWS):
            e = ext[:, g * POOL_GROUP:(g + 1) * POOL_GROUP]
            acc = e
            shift = 1
            while shift < w:
                acc = acc + pltpu.roll(acc, shift, 0)
                shift *= 2
            pooled.append((acc[HALO:] * _inv_count(row0, tr, w) - e[HALO:]).astype(BF16))
        pw = [pw_ref[g].astype(BF16) for g in range(ng)]
        mixed = jnp.concatenate([_nn(pooled[g], pw[g]) for g in range(ng)], axis=1)
        ps = ps_ref[...]
        mixed_s = mixed * ps
        pg = pg_ref[...]
        sig_p = _sigmoid(pg)
        silu_p = pg * sig_p
        pool_out = (silu_p * mixed_s).astype(BF16)
        ag = ag_ref[...]
        sig_a = _sigmoid(ag)
        silu_a = ag * sig_a
        at = at_ref[...]
        attn_out = (silu_a * at).astype(BF16)
        cat = jnp.concatenate([pool_out, attn_out], axis=1)
        h2 = h + _nn(cat, wout_ref[...])

        r2 = lax.rsqrt(jnp.mean(h2 * h2, axis=-1, keepdims=True) + EPS)
        n2 = h2 * r2
        gfv = gf_ref[...]
        err = jnp.where(real, n2 * gfv - t_ref[...], 0.0)
        loss_ref[...] += jnp.sum(jnp.sum(err * err, axis=-1, keepdims=True), axis=0, keepdims=True) * (0.5 / D)
        dy = err * (1.0 / D)
        dgf_ref[...] += jnp.sum(dy * n2, axis=0, keepdims=True)
        dn = dy * gfv
        dh2 = r2 * (dn - n2 * jnp.mean(dn * n2, axis=-1, keepdims=True))
        dh2_ref[...] = dh2
        dh2b = dh2.astype(BF16)

        dwout_ref[...] += _tn(cat, dh2b)
        dcat = _nt(dh2b, wout_ref[...])
        dpo = dcat[:, 0:D_POOL]
        dao = dcat[:, D_POOL:D]
        do = dao * silu_a
        prod = do * at
        delta_ref[...] = jnp.zeros_like(delta_ref)
        for hd in range(HEADS):
            grp, lane = _stat_slot(hd)
            cols = slice(hd * V_HEAD, (hd + 1) * V_HEAD)
            do_ref[grp, :, lane * V_HEAD:(lane + 1) * V_HEAD] = do[:, cols].astype(BF16)
            delta_ref[grp, :, lane:lane + 1] = jnp.sum(prod[:, cols], axis=-1, keepdims=True)
        dag_ref[...] = (dao * at * (sig_a * (1.0 + ag * (1.0 - sig_a)))).astype(BF16)
        dmixed_s = dpo * silu_p
        dpg_ref[...] = (dpo * mixed_s * (sig_p * (1.0 + pg * (1.0 - sig_p)))).astype(BF16)
        dps_ref[...] += jnp.sum(dmixed_s * mixed, axis=0, keepdims=True)
        dmixed = (dmixed_s * ps).astype(BF16)
        dpl = []
        for g in range(ng):
            dm = dmixed[:, g * POOL_GROUP:(g + 1) * POOL_GROUP]
            dpl.append(_nt(dm, pw[g]))
            dpw_ref[g] += _tn(pooled[g], dm)
        dpl_ref[...] = jnp.concatenate(dpl, axis=1)

    halo_spec = pl.BlockSpec((HALO, D_POOL), lambda i: (jnp.maximum(i * per - 1, 0), 0))
    return pl.pallas_call(
        body,
        name="mid",
        grid=(N // tr,),
        in_specs=[
            _rows(D, tr), _rows(D, tr), _rows(D_POOL, tr), halo_spec, _rows(D_POOL, tr), _rows(D_POOL, tr),
            _rows(D_POOL, tr), _const(ng, POOL_GROUP, POOL_GROUP), _const(1, D_POOL), _const(D, D), _const(1, D),
        ],
        out_specs=[
            _rows(D, tr), pl.BlockSpec((STAT_GROUPS, tr, HEADS_PER_STEP_BWD * V_HEAD), lambda i: (0, i, 0)),
            pl.BlockSpec((STAT_GROUPS, tr, 128), lambda i: (0, i, 0)),
            _rows(D_POOL, tr), _rows(D_POOL, tr), _rows(D_POOL, tr),
            _const(D, D), _const(ng, POOL_GROUP, POOL_GROUP), _const(1, D_POOL), _const(1, D), _const(1, 128),
        ],
        out_shape=[
            jax.ShapeDtypeStruct((N, D), F32), jax.ShapeDtypeStruct((STAT_GROUPS, N, HEADS_PER_STEP_BWD * V_HEAD), BF16),
            jax.ShapeDtypeStruct((STAT_GROUPS, N, 128), F32),
            jax.ShapeDtypeStruct((N, D_POOL), BF16), jax.ShapeDtypeStruct((N, D_POOL), BF16),
            jax.ShapeDtypeStruct((N, D_POOL), F32), jax.ShapeDtypeStruct((D, D), F32),
            jax.ShapeDtypeStruct((ng, POOL_GROUP, POOL_GROUP), F32),
            jax.ShapeDtypeStruct((1, D_POOL), F32), jax.ShapeDtypeStruct((1, D), F32), jax.ShapeDtypeStruct((1, 128), F32),
        ],
        compiler_params=_cparams(dimension_semantics=("arbitrary",)),
    )(h, tgt, pool_in, pool_in, pool_gate, attn_gate, attn, pool_w, pool_scale, wout, gf)


def _unrope(dy, cosv, sinv):
    return dy * cosv + _swap64(dy * sinv) * _low_lanes()


def _attn_bwd(q, k, v, do, lse, delta, cosf, sinf, dwout, dpw):
    tiles = _attn_tiles()
    hp = HEADS_PER_STEP_BWD
    n_g = HEADS // hp
    n_t = len(tiles)
    half = SHARD_OUT // 2
    half_pw = PW_ROWS // 2
    swap_at, send_at, sum_at = (0, 3), (0, 5), (n_g - 1, n_t // 2)

    def body(q_hbm, k_hbm, v_hbm, do_hbm, lse_ref, delta_ref, cos_ref, sin_ref, dwout_hbm, dpw_ref, dq_hbm, dkv_ref, dkr_ref,
             gwout_ref, gpw_ref, q_buf, k_buf, v_buf, do_buf, dq_buf, dk_acc, dv_acc, own_w, sib_w, stage_w, recv_w, gw_buf,
             pw_sib, pw_chip, pw_recv, pw_buf,
             in_sems, out_sems, ow_sems, d2d_send, d2d_recv, ici_send, ici_recv, fin_send, fin_recv,
             pw_swap_send, pw_swap_recv, pw_ici_send, pw_ici_recv, pw_fin_send, pw_fin_recv):
        grp = pl.program_id(0)
        step = pl.program_id(1)
        heads = pl.ds(grp * hp, hp)
        x, y, c = lax.axis_index("x"), lax.axis_index("y"), lax.axis_index("c")
        me = 2 * x + y
        sibling = (x, y, 1 - c)

        def pw_rows(core):
            return pl.ds(pl.multiple_of(half_pw * core, half_pw), half_pw)

        def pw_swap():
            return _remote(dpw_ref, pw_sib, pw_swap_send.at[0], pw_swap_recv.at[0], sibling)

        def pw_ici(rel):
            fx, fy = _CHIP_RELS[rel]
            return _remote(pw_chip.at[pw_rows(c), :], pw_recv.at[rel - 1], pw_ici_send.at[rel - 1], pw_ici_recv.at[rel - 1],
                           (x ^ fx, y ^ fy, c))

        def pw_fin(core):
            spot = pw_buf.at[pw_rows(core), :]
            return _remote(spot, spot, pw_fin_send.at[0], pw_fin_recv.at[0], sibling)

        def chip_of(rel):
            fx, fy = _CHIP_RELS[rel]
            return 2 * (x ^ fx) + (y ^ fy)

        def piece(chip, core):
            return dwout_hbm.at[pl.ds(pl.multiple_of(SHARD_OUT * chip + half * core, half), half), :]

        def own_load(rel):
            return pltpu.make_async_copy(piece(chip_of(rel), c), own_w.at[rel], ow_sems.at[rel])

        def d2d_copy(rel):
            return _remote(piece(chip_of(rel), 1 - c), sib_w.at[rel], d2d_send.at[rel], d2d_recv.at[rel], sibling)

        def ici_copy(rel):
            fx, fy = _CHIP_RELS[rel]
            return _remote(stage_w.at[rel - 1], recv_w.at[rel - 1], ici_send.at[rel - 1], ici_recv.at[rel - 1],
                           (x ^ fx, y ^ fy, c))

        def fin_copy(core):
            spot = gw_buf.at[pl.ds(pl.multiple_of(half * core, half), half), :]
            return _remote(spot, spot, fin_send.at[0], fin_recv.at[0], sibling)

        @pl.when((grp == 0) & (step == 0))
        def _():
            _peer_signal(x, y, c)
            for rel in SEND_ORDER + (0,):
                own_load(rel).start()

        @pl.when((grp == swap_at[0]) & (step == swap_at[1]))
        def _():
            _peer_wait()
            pw_swap().start()
            for rel in SEND_ORDER + (0,):
                d2d_copy(rel).start()

        @pl.when((grp == send_at[0]) & (step == send_at[1]))
        def _():
            for rel in SEND_ORDER:
                own_load(rel).wait()
                d2d_copy(rel).wait_recv()
                stage_w[rel - 1] = (own_w[rel] + sib_w[rel]).astype(BF16)
                ici_copy(rel).start()
            pw_swap().wait_recv()
            pw_chip[...] = dpw_ref[...] + pw_sib[...]
            for rel in SEND_ORDER:
                pw_ici(rel).start()

        @pl.when((grp == sum_at[0]) & (step == sum_at[1]))
        def _():
            own_load(0).wait()
            d2d_copy(0).wait_recv()
            total = own_w[0] + sib_w[0]
            for rel in (1, 2, 3):
                ici_copy(rel).wait_recv()
                total = total + recv_w[rel - 1].astype(F32)
            gw_buf[pl.ds(pl.multiple_of(half * c, half), half), :] = total
            fin_copy(c).start()
            for rel in (1, 2, 3):
                pw_ici(rel).wait_recv()
            total = jnp.zeros((half_pw, POOL_GROUP), F32)
            for chip in range(CHIPS):
                flips = chip ^ me
                rel = jnp.where(flips == 2, 1, jnp.where(flips == 1, 2, flips))
                total = total + jnp.where(rel == 0, pw_chip[pw_rows(c), :], pw_recv[jnp.maximum(rel - 1, 0)])
            pw_buf[pw_rows(c), :] = total
            pw_fin(c).start()

        def finish_dwout():
            fin_copy(1 - c).wait_recv()
            pw_fin(1 - c).wait_recv()
            for rel in (0, 1, 2, 3):
                d2d_copy(rel).wait_send()
            for rel in (1, 2, 3):
                ici_copy(rel).wait_send()
                pw_ici(rel).wait_send()
            fin_copy(c).wait_send()
            pw_swap().wait_send()
            pw_fin(c).wait_send()
            gwout_ref[...] = gw_buf[...]
            gpw_ref[...] = pw_buf[...]

        def loads(g, idx):
            q0, rows, _ = tiles[idx]
            rs = pl.ds(q0, rows)
            par = (g * n_t + idx) % 2
            hs = pl.ds(g * hp, hp)
            pairs = ((q_hbm.at[hs, rs, :], q_buf.at[:, rs, :]), (k_hbm.at[hs, rs, :], k_buf.at[:, rs, :]),
                     (v_hbm.at[hs, rs, :], v_buf.at[:, rs, :]), (do_hbm.at[g, rs, :], do_buf.at[rs, :]))
            return [pltpu.make_async_copy(src, dst, in_sems.at[a, par]) for a, (src, dst) in enumerate(pairs)]

        def store(idx):
            q0, rows, _ = tiles[idx]
            return pltpu.make_async_copy(dq_buf.at[idx % 2, :, pl.ds(0, rows), :], dq_hbm.at[heads, pl.ds(q0, rows), :],
                                         out_sems.at[idx % 2])

        @pl.when(step == 0)
        def _():
            dk_acc[...] = jnp.zeros_like(dk_acc)
            dv_acc[...] = jnp.zeros_like(dv_acc)

        @pl.when((step == 0) & (grp == 0))
        def _():
            dkr_ref[...] = jnp.zeros_like(dkr_ref)
            for cp in loads(grp, 0):
                cp.start()

        for idx, (q0, rows, klen) in enumerate(tiles):
            @pl.when(step == idx)
            def _(idx=idx, q0=q0, rows=rows, klen=klen):
                for cp in loads(grp, idx):
                    cp.wait()
                if idx + 1 < n_t:
                    for cp in loads(grp, idx + 1):
                        cp.start()
                if idx >= 2:
                    store(idx - 2).wait()
                qs = pl.ds(q0, rows)
                for hd in range(hp):
                    qv = q_buf[hd, qs, :]
                    kv = k_buf[hd, 0:klen, :]
                    p = jnp.exp(_masked_scores(qv, kv, rows, klen) - lse_ref[0, qs, hd:hd + 1])
                    dob = do_buf[qs, hd * V_HEAD:(hd + 1) * V_HEAD]
                    ds = (p * (_nt(dob, v_buf[hd, 0:klen, :]) - delta_ref[0, qs, hd:hd + 1])).astype(BF16)
                    dq = _nn(ds, kv) * SCALE
                    dq_buf[idx % 2, hd, 0:rows, 0:QK_NOPE] = dq[:, 0:QK_NOPE].astype(BF16)
                    dq_buf[idx % 2, hd, 0:rows, QK_NOPE:] = _unrope(dq[:, QK_NOPE:], cos_ref[qs, :], sin_ref[qs, :]).astype(BF16)
                    dk_acc[hd, 0:klen, :] += _tn(ds, qv)
                    dv_acc[hd, 0:klen, :] += _tn(p.astype(BF16), dob)
                store(idx).start()

        @pl.when(step == n_t - 1)
        def _():
            @pl.when(grp + 1 < n_g)
            def _():
                for cp in loads(grp + 1, 0):
                    cp.start()

            for hd in range(hp):
                dkv_ref[hd, :, 0:QK_NOPE] = dk_acc[hd, :, 0:QK_NOPE].astype(BF16)
                dkv_ref[hd, :, QK_NOPE:] = dv_acc[hd].astype(BF16)
                dkr_ref[...] += dk_acc[hd, :, QK_NOPE:]
            store(n_t - 2).wait()
            store(n_t - 1).wait()

            @pl.when(grp == n_g - 1)
            def _():
                finish_dwout()

    hbm = pl.BlockSpec(memory_space=pl.ANY)
    stat = pl.BlockSpec((1, N, 128), lambda g, t: (g, 0, 0), pipeline_mode=pl.Buffered(1))
    piece_f32 = lambda lead: pltpu.VMEM((lead, half, D), F32)
    piece_bf16 = lambda lead: pltpu.VMEM((lead, half, D), BF16)
    return pl.pallas_call(
        body,
        name="attn_bwd",
        grid=(n_g, n_t),
        in_specs=[hbm, hbm, hbm, hbm, stat, stat, _const(N, 128), _const(N, 128), hbm, _const(PW_ROWS, POOL_GROUP)],
        out_specs=[hbm, pl.BlockSpec((hp, N, 256), lambda g, t: (g, 0, 0), pipeline_mode=pl.Buffered(1)), _const(N, 128),
                   _const(SHARD_OUT, D), _const(PW_ROWS, POOL_GROUP)],
        out_shape=[
            jax.ShapeDtypeStruct((HEADS, N, 256), BF16), jax.ShapeDtypeStruct((HEADS, N, 256), BF16),
            jax.ShapeDtypeStruct((N, 128), F32), jax.ShapeDtypeStruct((SHARD_OUT, D), F32),
            jax.ShapeDtypeStruct((PW_ROWS, POOL_GROUP), F32),
        ],
        scratch_shapes=[pltpu.VMEM((hp, N, 256), BF16), pltpu.VMEM((hp, N, 256), BF16), pltpu.VMEM((hp, N, V_HEAD), BF16),
                        pltpu.VMEM((N, hp * V_HEAD), BF16), pltpu.VMEM((2, hp, TQ, 256), BF16),
                        pltpu.VMEM((hp, N, 256), F32), pltpu.VMEM((hp, N, V_HEAD), F32),
                        piece_f32(CHIPS), piece_f32(CHIPS), piece_bf16(3), piece_bf16(3), pltpu.VMEM((SHARD_OUT, D), F32),
                        pltpu.VMEM((PW_ROWS, POOL_GROUP), F32), pltpu.VMEM((PW_ROWS, POOL_GROUP), F32),
                        pltpu.VMEM((3, half_pw, POOL_GROUP), F32), pltpu.VMEM((PW_ROWS, POOL_GROUP), F32),
                        pltpu.SemaphoreType.DMA((4, 2)), pltpu.SemaphoreType.DMA((2,)), pltpu.SemaphoreType.DMA((CHIPS,)),
                        pltpu.SemaphoreType.DMA((CHIPS,)), pltpu.SemaphoreType.DMA((CHIPS,)),
                        pltpu.SemaphoreType.DMA((3,)), pltpu.SemaphoreType.DMA((3,)),
                        pltpu.SemaphoreType.DMA((1,)), pltpu.SemaphoreType.DMA((1,)),
                        pltpu.SemaphoreType.DMA((1,)), pltpu.SemaphoreType.DMA((1,)),
                        pltpu.SemaphoreType.DMA((3,)), pltpu.SemaphoreType.DMA((3,)),
                        pltpu.SemaphoreType.DMA((1,)), pltpu.SemaphoreType.DMA((1,))],
        compiler_params=_cparams(dimension_semantics=("arbitrary", "arbitrary"), collective_id=2),
    )(q, k, v, do, lse, delta, cosf, sinf, dwout, dpw)


def _bwd_in(h, dh2, dq, dkv, dkr, cq, ckv, dpl, dpg, dag, norm_g, win, gq, wq, gkv, wkv, cosf, sinf, adam_out):
    tr = ROWS_BWD
    nb = N // tr
    per = tr // HALO
    lead = HEAD_ROWS
    adam_rows = SHARD_OUT // nb

    def body(h_hbm, dh2_hbm, dq_ref, dkv_ref, dkr_ref, cq_ref, ckv_ref, dpl_ref, halo_ref, dpg_ref, dag_ref,
             g_ref, win_ref, gq_ref, wq_ref, gkv_ref, wkv_ref, cos_ref, sin_ref, aw_ref, ag_ref, am_ref, av_ref,
             gx_ref, dmeta_ref, dsl_ref, dwq_ref, dwkv_ref, dg_ref, dgq_ref, dgkv_ref, ago_ref, ad_ref, anm_ref, anv_ref,
             dh_buf, gx_sem, h_buf, dh2_buf, ld_sems):
        i = pl.program_id(0)

        def tile_load(t, which):
            src, dst = ((h_hbm, h_buf), (dh2_hbm, dh2_buf))[which]
            return pltpu.make_async_copy(src.at[pl.ds(pl.multiple_of(t * tr, 16), tr), :], dst.at[t], ld_sems.at[which, t])

        @pl.when(i == 0)
        def _():
            for t in range(nb):
                for which in (0, 1):
                    tile_load(t, which).start()

        tile_load(i, 0).wait()
        tile_load(i, 1).wait()
        grad_out = ag_ref[...]
        ago_ref[...] = grad_out
        ad_ref[...], anm_ref[...], anv_ref[...] = _adamw_math(aw_ref[...], grad_out, am_ref[...], av_ref[...])

        @pl.when(i == 0)
        def _():
            dwq_ref[...] = jnp.zeros_like(dwq_ref)
            dwkv_ref[...] = jnp.zeros_like(dwkv_ref)
            dg_ref[...] = jnp.zeros_like(dg_ref)
            dgq_ref[...] = jnp.zeros_like(dgq_ref)
            dgkv_ref[...] = jnp.zeros_like(dgkv_ref)

        row0 = i * tr
        h = h_buf[i]
        r = lax.rsqrt(jnp.mean(h * h, axis=-1, keepdims=True) + EPS)
        n = h * r
        gv = g_ref[...]
        cq = cq_ref[...]
        rq = lax.rsqrt(jnp.mean(cq * cq, axis=-1, keepdims=True) + EPS)
        nq = cq * rq
        gqv = gq_ref[...]
        cqn = (nq * gqv).astype(BF16)
        dcqn = jnp.zeros((tr, Q_LORA), F32)
        for hd in range(HEADS):
            dqf = dq_ref[hd]
            dcqn = dcqn + _nn(dqf, wq_ref[hd])
            dwq_ref[hd] += _tn(dqf, cqn)
        dgq_ref[...] += jnp.sum(dcqn * nq, axis=0, keepdims=True)
        dnq = dcqn * gqv
        dcq = rq * (dnq - nq * jnp.mean(dnq * nq, axis=-1, keepdims=True))

        ckv = ckv_ref[...]
        rkv = lax.rsqrt(jnp.mean(ckv * ckv, axis=-1, keepdims=True) + EPS)
        nkv = ckv * rkv
        gkvv = gkv_ref[...]
        ckvn = (nkv * gkvv).astype(BF16)
        dckvn = jnp.zeros((tr, KV_LORA), F32)
        for hd in range(HEADS):
            dkv = dkv_ref[hd]
            dckvn = dckvn + _nt(dkv, wkv_ref[hd])
            dwkv_ref[hd] += _tn(ckvn, dkv)
        dgkv_ref[...] += jnp.sum(dckvn * nkv, axis=0, keepdims=True)
        dnkv = dckvn * gkvv
        dckv = rkv * (dnkv - nkv * jnp.mean(dnkv * nkv, axis=-1, keepdims=True))
        dkr = _unrope(dkr_ref[...], cos_ref[...], sin_ref[...])

        cur = dpl_ref[...]
        halo = jnp.where(i < nb - 1, halo_ref[...], 0.0)
        dpi = []
        for g, w in enumerate(POOL_WINDOWS):
            sl = slice(g * POOL_GROUP, (g + 1) * POOL_GROUP)
            a = jnp.concatenate([cur[:, sl] * _inv_count(row0, tr, w), halo[:, sl] * _inv_count(row0 + tr, HALO, w)], axis=0)
            acc = a
            shift = 1
            while shift < w:
                acc = acc + pltpu.roll(acc, tr + HALO - shift, 0)
                shift *= 2
            dpi.append(acc[0:tr] - cur[:, sl])

        du = jnp.concatenate([t.astype(BF16) for t in dpi] + [dpg_ref[...]] + [t.astype(BF16) for t in (dcq, dckv, dkr)],
                             axis=1)
        dagb = dag_ref[...]
        by_row = jnp.concatenate(dpi + [dpg_ref[...].astype(F32), dcq, dckv, dkr[:, 0:QK_ROPE], dagb.astype(F32),
                                        jnp.zeros((tr, SHARD_PAD - SHARD_IN), F32)], axis=1)
        for chip in range(CHIPS):
            dsl_ref[chip] = by_row[:, SHARD_IN * chip:SHARD_IN * chip + SHARD_PAD].astype(BF16)
        dhn = _nn(du, win_ref[0:O_KR_END, :]) + _nn(dagb, win_ref[O_AG:D_IN, :])
        dg_ref[...] += jnp.sum(dhn * n, axis=0, keepdims=True)
        dn = dhn * gv
        dh = dh2_buf[i] + r * (dn - n * jnp.mean(dn * n, axis=-1, keepdims=True))

        first = pltpu.make_async_copy(dh_buf.at[pl.ds(lead, tr - lead), :], gx_ref.at[pl.ds(0, tr - lead), :], gx_sem)
        later = lambda step: pltpu.make_async_copy(
            dh_buf, gx_ref.at[pl.ds(pl.multiple_of(step * tr - lead, 16), tr), :], gx_sem)

        @pl.when(i == 1)
        def _():
            first.wait()

        @pl.when(i > 1)
        def _():
            later(i - 1).wait()

        dh_buf[...] = dh

        @pl.when(i == 0)
        def _():
            first.start()
            for chip in range(CHIPS):
                dmeta_ref[chip] = dh[PAD:HEAD_ROWS, chip * 256:(chip + 1) * 256]

        @pl.when(i > 0)
        def _():
            later(i).start()

        @pl.when(i == nb - 1)
        def _():
            later(i).wait()

    head = lambda w: pl.BlockSpec((HEADS, tr, w), lambda i: (0, i, 0))
    halo_spec = pl.BlockSpec((HALO, D_POOL), lambda i: (jnp.minimum((i + 1) * per, N // HALO - 1), 0))
    return pl.pallas_call(
        body,
        name="bwd_in",
        grid=(nb,),
        in_specs=[
            pl.BlockSpec(memory_space=pl.ANY), pl.BlockSpec(memory_space=pl.ANY),
            head(256), head(256), _rows(128, tr), _rows(Q_LORA, tr), _rows(KV_LORA, tr),
            _rows(D_POOL, tr), halo_spec, _rows(D_POOL, tr), _rows(D_POOL, tr),
            _const(1, D), _const(D_IN, D), _const(1, Q_LORA), _const(HEADS, 256, Q_LORA),
            _const(1, KV_LORA), _const(HEADS, KV_LORA, 256), _rows(128, tr), _rows(128, tr),
        ] + [_rows(D, adam_rows)] * 4,
        out_specs=[
            pl.BlockSpec(memory_space=pl.ANY), _const(CHIPS, N_META, 256),
            pl.BlockSpec((CHIPS, tr, SHARD_PAD), lambda i: (0, i, 0)), _const(HEADS, 256, Q_LORA),
            _const(HEADS, KV_LORA, 256), _const(1, D), _const(1, Q_LORA), _const(1, KV_LORA),
        ] + [_rows(D, adam_rows)] * 4,
        out_shape=[
            jax.ShapeDtypeStruct((S, D), F32), jax.ShapeDtypeStruct((CHIPS, N_META, 256), F32),
            jax.ShapeDtypeStruct((CHIPS, N, SHARD_PAD), BF16), jax.ShapeDtypeStruct((HEADS, 256, Q_LORA), F32),
            jax.ShapeDtypeStruct((HEADS, KV_LORA, 256), F32),
            jax.ShapeDtypeStruct((1, D), F32), jax.ShapeDtypeStruct((1, Q_LORA), F32), jax.ShapeDtypeStruct((1, KV_LORA), F32),
        ] + [jax.ShapeDtypeStruct((SHARD_OUT, D), F32)] * 4,
        scratch_shapes=[pltpu.VMEM((tr, D), F32), pltpu.SemaphoreType.DMA,
                        pltpu.VMEM((nb, tr, D), F32), pltpu.VMEM((nb, tr, D), F32), pltpu.SemaphoreType.DMA((2, nb))],
        compiler_params=_cparams(dimension_semantics=("arbitrary",)),
    )(h, dh2, dq, dkv, dkr, cq, ckv, dpl, dpl, dpg, dag, norm_g, win, gq, wq, gkv, wkv, cosf, sinf, *adam_out)


def _local_step(h, tgt, norm_g, win, gq, wq, gkv, wkv, pool_w, pool_scale, wout_s, m_wout_s, v_wout_s, gf, cosf, sinf):
    pool_in, pool_gate, cq, ckv, attn_gate, q, k, v, hn = _fwd_in(h, norm_g, win, gq, wq, gkv, wkv, cosf, sinf)
    attn, lse, wout = _attn_fwd(q, k, v, wout_s)
    dh2, do, delta, dag, dpg, dpl, dwout, dpw, dps, dgf, loss = _mid(
        h, tgt, pool_in, pool_gate, attn_gate, attn, pool_w, pool_scale, wout, gf)
    dq, dkv, dkr, gwout, gpw = _attn_bwd(q, k, v, do, lse, delta, cosf, sinf, dwout, dpw.reshape(PW_ROWS, POOL_GROUP))
    gx, dmeta, dsl, dwq, dwkv, dg, dgq, dgkv, *r_out = _bwd_in(
        h, dh2, dq, dkv, dkr, cq, ckv, dpl, dpg, dag, norm_g, win, gq, wq, gkv, wkv, cosf, sinf,
        (wout_s, gwout, m_wout_s, v_wout_s))
    return dict(gx=gx, dmeta=dmeta, dsl=dsl, hn=hn, dwq=dwq, dwkv=dwkv, r_out=tuple(r_out), dg=dg, dgq=dgq,
                dgkv=dgkv, gpw=gpw, dps=dps, dgf=dgf, loss=loss)


_CHIP_RELS = ((0, 0), (1, 0), (0, 1), (1, 1))

_ARR_ROWS = (SHARD_IN, SHARD_OUT, 256, KV_LORA, N_META)
_ARR_COLS = (D, D, Q_LORA, 256, 256)
_PIECES = (
    (0, 0, 256, 0), (0, 256, SHARD_IN - 256, 1),
    (1, 0, 128, 0), (1, 128, 128, 1),
    (2, 0, 128, 0), (2, 128, 128, 1),
    (3, 0, 64, 0), (3, 64, 64, 1),
    (4, 0, N_META, 0),
)
_NP = len(_PIECES)
_PIECE_MAX = (256, 128, 128, 64, N_META)


def _gathered_at(refs, arr, chip, r0, n):
    if arr in (0, 1):
        return refs[arr].at[pl.ds(pl.multiple_of(_ARR_ROWS[arr] * chip + r0, 16), n), :]
    return refs[arr].at[chip, pl.ds(r0, n), :]


def _remote(src, dst, send_sem, recv_sem, to):
    return pltpu.make_async_remote_copy(src_ref=src, dst_ref=dst, send_sem=send_sem, recv_sem=recv_sem,
                                        device_id=to, device_id_type=MESH)


def _gather_weights(winT_s, wqT_s, wkv_s, meta_s, x2, tgt2):
    arrays = (0, 2, 3, 4)

    def body(win_ref, wq_ref, wkv_ref, meta_ref, x_ref, t_ref, win_o, wq_o, wkv_o, h_o, tp_o,
             s_win, s_wq, s_wkv, meta_all, head_buf, x_buf, t_buf, ici_send, ici_recv, fwd_send, fwd_recv,
             loc_sems, own_sems):
        x, y, c = lax.axis_index("x"), lax.axis_index("y"), lax.axis_index("c")
        me = 2 * x + y
        stage = (s_win, None, s_wq, s_wkv, meta_ref)
        outs = (win_o, None, wq_o, wkv_o, meta_all)

        _peer_signal(x, y, c)

        frames = pl.ds(HEAD_ROWS, S)
        loads = [pltpu.make_async_copy(x_ref, x_buf, loc_sems.at[0]), pltpu.make_async_copy(t_ref, t_buf, loc_sems.at[1])]
        local = [pltpu.make_async_copy(x_buf, h_o.at[frames, :], loc_sems.at[0]),
                 pltpu.make_async_copy(t_buf, tp_o.at[frames, :], loc_sems.at[1])]
        for cp in loads:
            cp.start()

        s_win[...] = win_ref[...].astype(BF16)
        s_wq[0:QK, :] = wq_ref[...].astype(BF16)
        s_wq[QK:256, :] = jnp.zeros((256 - QK, Q_LORA), BF16)
        s_wkv[...] = wkv_ref[...].astype(BF16)
        head_buf[...] = jnp.zeros_like(head_buf)
        zeros = pltpu.make_async_copy(head_buf, tp_o.at[pl.ds(0, HEAD_ROWS), :], loc_sems.at[2])
        zeros.start()

        def chip_of(rel):
            fx, fy = _CHIP_RELS[rel]
            return 2 * (x ^ fx) + (y ^ fy)

        def same_core_of(rel):
            fx, fy = _CHIP_RELS[rel]
            return (x ^ fx, y ^ fy, c)

        def ici_copy(rel, i, src_chip, to):
            arr, r0, n, _ = _PIECES[i]
            k = (rel - 1) * _NP + i
            return _remote(stage[arr].at[pl.ds(r0, n), :], _gathered_at(outs, arr, src_chip, r0, n),
                           ici_send.at[k], ici_recv.at[k], to)

        def fwd_copy(rel, i, to):
            arr, r0, n, _ = _PIECES[i]
            k = (rel - 1) * _NP + i
            place = _gathered_at(outs, arr, chip_of(rel), r0, n)
            return _remote(place, place, fwd_send.at[k], fwd_recv.at[k], to)

        _peer_wait()
        for core in (0, 1):
            @pl.when(c == core)
            def _(core=core):
                mine = [i for i in range(_NP) if _PIECES[i][3] == core and _PIECES[i][0] in arrays]
                theirs = [i for i in range(_NP) if _PIECES[i][3] != core and _PIECES[i][0] in arrays]
                order = (1, 2, 3)
                sends = [ici_copy(rel, i, me, same_core_of(rel)) for rel in order for i in mine]
                for cp in sends:
                    cp.start()
                for ld, st in zip(loads, local):
                    ld.wait()
                    st.start()
                own = [pltpu.make_async_copy(stage[arr], _gathered_at(outs, arr, me, 0, _ARR_ROWS[arr]), own_sems.at[arr])
                       for arr in arrays if arr != 4]
                for cp in own:
                    cp.start()
                meta_all[me] = meta_ref[...]
                for rel in order:
                    for i in mine:
                        ici_copy(rel, i, chip_of(rel), (x, y, c)).wait_recv()
                        fwd = fwd_copy(rel, i, (x, y, 1 - c))
                        fwd.start()
                        sends.append(fwd)
                for rel in order:
                    for i in theirs:
                        fwd_copy(rel, i, (x, y, c)).wait_recv()
                for cp in sends:
                    cp.wait_send()
                for cp in own:
                    cp.wait()

        zeros.wait()
        for chip in range(CHIPS):
            head_buf[PAD:HEAD_ROWS, chip * 256:(chip + 1) * 256] = meta_all[chip]
        head = pltpu.make_async_copy(head_buf, h_o.at[pl.ds(0, HEAD_ROWS), :], loc_sems.at[2])
        head.start()
        head.wait()
        for cp in local:
            cp.wait()

    vm = pl.BlockSpec(memory_space=pltpu.VMEM)
    hbm = pl.BlockSpec(memory_space=pl.ANY)
    return pl.pallas_call(
        body,
        name="gather_weights",
        in_specs=[vm] * 4 + [hbm] * 2,
        out_specs=[hbm] * 5,
        out_shape=[
            jax.ShapeDtypeStruct((D_IN, D), BF16),
            jax.ShapeDtypeStruct((CHIPS, 256, Q_LORA), BF16), jax.ShapeDtypeStruct((CHIPS, KV_LORA, 256), BF16),
            jax.ShapeDtypeStruct((N, D), F32), jax.ShapeDtypeStruct((N, D), F32),
        ],
        scratch_shapes=[pltpu.VMEM((_ARR_ROWS[a], _ARR_COLS[a]), BF16) for a in (0, 2, 3)]
        + [pltpu.VMEM((CHIPS, N_META, 256), F32), pltpu.VMEM((HEAD_ROWS, D), F32), pltpu.VMEM((S, D), F32),
           pltpu.VMEM((S, D), F32)]
        + [pltpu.SemaphoreType.DMA((3 * _NP,))] * 4 + [pltpu.SemaphoreType.DMA((3,)), pltpu.SemaphoreType.DMA((4,))],
        compiler_params=_cparams(collective_id=0),
    )(winT_s, wqT_s, wkv_s, meta_s, x2, tgt2)


_SM_ROWS = (VEC_ROWS,)
_SM_COLS = (D,)
_SM_PIECES = ((0, 0, VEC_ROWS, 0),)
_NSP = len(_SM_PIECES)
_NSB = len(_SM_ROWS)


def _reduce_grads(dsl, hn, dwq, dwkv, dmeta4, dg, dgf, dgq, dgkv, dps, loss):
    arrays = (0, 2, 3, 4)
    loaded = (2, 3, 4)
    shard_order = SEND_ORDER + (0,)

    def body(dsl_hbm, hn_hbm, dwq_ref, dwkv_ref, dmeta_ref, dg_ref, dgf_ref, dgq_ref, dgkv_ref, dps_ref,
             loss_ref, gwin_o, gwq_o, gwkv_o, gmeta_o, gg_o, ggf_o, ggq_o, ggkv_o, gps_o, gloss_o,
             ow2, ow3, ow4, sb0, sb2, sb3, sb4, st0, st2, st3, st4, rc0, rc2, rc3, rc4,
             vec, sm_sb0, sm_cs0, sm_rc0, vec_fin, slab_v, hn_v, dwin_buf, own0,
             own_sems, d2d_send, d2d_recv, ici_send, ici_recv, fin_send, fin_recv,
             swap_send, swap_recv, smi_send, smi_recv, smf_send, smf_recv, ld_sems):
        x, y, c = lax.axis_index("x"), lax.axis_index("y"), lax.axis_index("c")
        me = 2 * x + y
        _peer_signal(x, y, c)
        grads = (None, None, dwq_ref, dwkv_ref, dmeta_ref)
        outs = (gwin_o, None, gwq_o, gwkv_o, gmeta_o)
        own_buf = (None, None, ow2, ow3, ow4)
        sib_buf = (sb0, None, sb2, sb3, sb4)
        stage = (st0, None, st2, st3, st4)
        recv = (rc0, None, rc2, rc3, rc4)
        sm_mine = (vec,)
        sm_sib = (sm_sb0,)
        sm_chip = (sm_cs0,)
        sm_recv = (sm_rc0,)
        sm_out = (vec_fin,)
        sibling = (x, y, 1 - c)

        def chip_of(rel):
            fx, fy = _CHIP_RELS[rel]
            return 2 * (x ^ fx) + (y ^ fy)

        def same_core_of(rel):
            fx, fy = _CHIP_RELS[rel]
            return (x ^ fx, y ^ fy, c)

        hn_load = pltpu.make_async_copy(hn_hbm, hn_v, ld_sems.at[CHIPS])

        def slab_load(rel):
            return pltpu.make_async_copy(dsl_hbm.at[chip_of(rel)], slab_v.at[rel], ld_sems.at[rel])

        hn_load.start()
        slab_load(shard_order[0]).start()

        def slot(bufs, i, idx):
            arr, _, n, _ = _PIECES[i]
            return bufs[arr].at[idx, pl.ds(0, n), :]

        def own_load(rel, i):
            arr, r0, n, _ = _PIECES[i]
            return pltpu.make_async_copy(_gathered_at(grads, arr, chip_of(rel), r0, n), slot(own_buf, i, rel),
                                         own_sems.at[rel * _NP + i])

        def d2d_copy(rel, i):
            arr, r0, n, _ = _PIECES[i]
            k = rel * _NP + i
            return _remote(_gathered_at(grads, arr, chip_of(rel), r0, n), slot(sib_buf, i, rel),
                           d2d_send.at[k], d2d_recv.at[k], sibling)

        def ici_copy(rel, i):
            k = (rel - 1) * _NP + i
            return _remote(slot(stage, i, rel - 1), slot(recv, i, rel - 1), ici_send.at[k], ici_recv.at[k],
                           same_core_of(rel))

        def fin_copy(i):
            arr, r0, n, _ = _PIECES[i]
            place = outs[arr].at[pl.ds(r0, n), :]
            return _remote(place, place, fin_send.at[i], fin_recv.at[i], sibling)

        def sm_ici_copy(rel, j):
            blk, r0, n, _ = _SM_PIECES[j]
            k = (rel - 1) * _NSP + j
            return _remote(sm_chip[blk].at[pl.ds(r0, n), :], sm_recv[blk].at[rel - 1, pl.ds(r0, n), :],
                           smi_send.at[k], smi_recv.at[k], same_core_of(rel))

        def sm_fin_copy(j):
            blk, r0, n, _ = _SM_PIECES[j]
            place = sm_out[blk].at[pl.ds(r0, n), :]
            return _remote(place, place, smf_send.at[j], smf_recv.at[j], sibling)

        vec[...] = jnp.zeros_like(vec)
        vec[0:1, :] = dg_ref[...]
        vec[1:2, :] = dgf_ref[...]
        vec[2:3, V_GQ:V_GQ + Q_LORA] = dgq_ref[...]
        vec[2:3, V_GKV:V_GKV + KV_LORA] = dgkv_ref[...]
        vec[2:3, V_PS:V_PS + D_POOL] = dps_ref[...]
        vec[2:3, V_LOSS:D] = loss_ref[...]
        _peer_wait()
        swaps = [_remote(sm_mine[b], sm_sib[b], swap_send.at[b], swap_recv.at[b], sibling) for b in range(_NSB)]
        for cp in swaps:
            cp.start()

        for core in (0, 1):
            @pl.when(c == core)
            def _(core=core):
                mine = [i for i in range(_NP) if _PIECES[i][3] == core and _PIECES[i][0] in loaded]
                theirs = [i for i in range(_NP) if _PIECES[i][3] != core and _PIECES[i][0] in loaded]
                i0 = next(i for i in range(_NP) if _PIECES[i][0] == 0 and _PIECES[i][3] == core)
                j0 = next(i for i in range(_NP) if _PIECES[i][0] == 0 and _PIECES[i][3] != core)
                sm_mine_p = [j for j in range(_NSP) if _SM_PIECES[j][3] == core]
                sm_theirs_p = [j for j in range(_NSP) if _SM_PIECES[j][3] != core]
                sends = list(swaps)

                for rel in shard_order:
                    for i in theirs:
                        cp = d2d_copy(rel, i)
                        cp.start()
                        sends.append(cp)
                    for i in mine:
                        own_load(rel, i).start()

                def piece_rows(i):
                    return pl.ds(_PIECES[i][1], _PIECES[i][2])

                def form(rel, i):
                    r0, n = _PIECES[i][1], _PIECES[i][2]
                    dwin_buf[rel, r0:r0 + n, :] = _tn(slab_v[rel, :, r0:r0 + _PIECE_MAX[0]], hn_v[...])[0:n, :]

                def d2d0(rel, i):
                    return _remote(dwin_buf.at[rel, piece_rows(i), :], slot(sib_buf, i, rel),
                                   d2d_send.at[rel * _NP + i], d2d_recv.at[rel * _NP + i], sibling)

                def settle(rel):
                    d2d0(rel, i0).wait_recv()
                    total = dwin_buf[rel, piece_rows(i0), :] + slot(sib_buf, i0, rel)[...]
                    if rel == 0:
                        own0[0:_PIECES[i0][2], :] = total
                    else:
                        slot(stage, i0, rel - 1)[...] = total.astype(BF16)
                        cp = ici_copy(rel, i0)
                        cp.start()
                        sends.append(cp)

                for rel in SEND_ORDER:
                    for i in mine:
                        arr, r0, n, _ = _PIECES[i]
                        own_load(rel, i).wait()
                        d2d_copy(rel, i).wait_recv()
                        total = slot(own_buf, i, rel)[...] + slot(sib_buf, i, rel)[...]
                        slot(stage, i, rel - 1)[...] = total.astype(stage[arr].dtype)
                        cp = ici_copy(rel, i)
                        cp.start()
                        sends.append(cp)

                for b in range(_NSB):
                    swaps[b].wait_recv()
                    sm_chip[b][...] = sm_mine[b][...] + sm_sib[b][...]
                for rel in SEND_ORDER:
                    for j in sm_mine_p:
                        cp = sm_ici_copy(rel, j)
                        cp.start()
                        sends.append(cp)

                hn_load.wait()
                for n, rel in enumerate(shard_order):
                    slab_load(rel).wait()
                    if n == 0:
                        for later in shard_order[1:]:
                            slab_load(later).start()
                    form(rel, j0)
                    cp = d2d0(rel, j0)
                    cp.start()
                    sends.append(cp)
                    form(rel, i0)
                    settle(rel)

                for i in mine:
                    arr, r0, n, _ = _PIECES[i]
                    own_load(0, i).wait()
                    d2d_copy(0, i).wait_recv()
                    total = slot(own_buf, i, 0)[...] + slot(sib_buf, i, 0)[...]
                    for rel in (1, 2, 3):
                        ici_copy(rel, i).wait_recv()
                        total = total + slot(recv, i, rel - 1)[...].astype(F32)
                    outs[arr][pl.ds(r0, n), :] = total
                    cp = fin_copy(i)
                    cp.start()
                    sends.append(cp)
                total = own0[0:_PIECES[i0][2], :]
                for rel in (1, 2, 3):
                    ici_copy(rel, i0).wait_recv()
                    total = total + slot(recv, i0, rel - 1)[...].astype(F32)
                outs[0][pl.ds(_PIECES[i0][1], _PIECES[i0][2]), :] = total
                cp = fin_copy(i0)
                cp.start()
                sends.append(cp)

                for j in sm_mine_p:
                    blk, r0, n, _ = _SM_PIECES[j]
                    for rel in (1, 2, 3):
                        sm_ici_copy(rel, j).wait_recv()
                    total = jnp.zeros((n, _SM_COLS[blk]), F32)
                    for chip in range(CHIPS):
                        flips = chip ^ me
                        rel = jnp.where(flips == 2, 1, jnp.where(flips == 1, 2, flips))
                        theirs_rows = sm_recv[blk][jnp.maximum(rel - 1, 0), pl.ds(r0, n), :]
                        total = total + jnp.where(rel == 0, sm_chip[blk][pl.ds(r0, n), :], theirs_rows)
                    sm_out[blk][pl.ds(r0, n), :] = total
                    cp = sm_fin_copy(j)
                    cp.start()
                    sends.append(cp)

                for i in theirs + [j0]:
                    fin_copy(i).wait_recv()
                for j in sm_theirs_p:
                    sm_fin_copy(j).wait_recv()
                for cp in sends:
                    cp.wait_send()

        gg_o[...] = vec_fin[0:1, :]
        ggf_o[...] = vec_fin[1:2, :]
        ggq_o[...] = vec_fin[2:3, V_GQ:V_GQ + Q_LORA]
        ggkv_o[...] = vec_fin[2:3, V_GKV:V_GKV + KV_LORA]
        gps_o[...] = vec_fin[2:3, V_PS:V_PS + D_POOL]
        gloss_o[...] = vec_fin[2:3, V_LOSS:D]

    vm = pl.BlockSpec(memory_space=pltpu.VMEM)
    piece_buf = lambda lead, dtype, which=arrays: [
        pltpu.VMEM((lead, _PIECE_MAX[a], _ARR_COLS[a]), F32 if a == 4 else dtype) for a in which]
    sm_buf = lambda *lead: [pltpu.VMEM(lead + (_SM_ROWS[b], _SM_COLS[b]), F32) for b in range(_NSB)]
    dma = lambda n: [pltpu.SemaphoreType.DMA((n,))] * 2
    return pl.pallas_call(
        body,
        name="reduce_grads",
        in_specs=[pl.BlockSpec(memory_space=pl.ANY)] * 4 + [vm] * 7,
        out_specs=[vm] * 10,
        out_shape=[jax.ShapeDtypeStruct((_ARR_ROWS[a], _ARR_COLS[a]), F32) for a in arrays]
        + [jax.ShapeDtypeStruct((1, D), F32),
           jax.ShapeDtypeStruct((1, D), F32), jax.ShapeDtypeStruct((1, Q_LORA), F32),
           jax.ShapeDtypeStruct((1, KV_LORA), F32), jax.ShapeDtypeStruct((1, D_POOL), F32),
           jax.ShapeDtypeStruct((1, 128), F32)],
        scratch_shapes=piece_buf(CHIPS, F32, loaded) + piece_buf(CHIPS, F32) + piece_buf(3, BF16) + piece_buf(3, BF16)
        + [pltpu.VMEM((VEC_ROWS, D), F32)] + sm_buf() + sm_buf() + sm_buf(3) + [pltpu.VMEM((VEC_ROWS, D), F32)]
        + [pltpu.VMEM((CHIPS, N, SHARD_PAD), BF16), pltpu.VMEM((N, D), BF16),
           pltpu.VMEM((CHIPS, SHARD_PAD, D), F32), pltpu.VMEM((_PIECE_MAX[0], D), F32)]
        + [pltpu.SemaphoreType.DMA((CHIPS * _NP,))]
        + dma(CHIPS * _NP) + dma(3 * _NP) + dma(_NP) + dma(_NSB) + dma(3 * _NSP) + dma(_NSP)
        + [pltpu.SemaphoreType.DMA((CHIPS + 1,))],
        compiler_params=_cparams(collective_id=3),
    )(dsl, hn, dwq, dwkv, dmeta4, dg, dgf, dgq, dgkv, dps, loss)


def _adamw_math(w, g, m, v):
    m = B1 * m + (1.0 - B1) * g
    v = B2 * v + (1.0 - B2) * (g * g)
    m_hat = m / C1
    v_hat = v / C2
    delta = -LR * (m_hat / (jnp.sqrt(v_hat) + ADAM_EPS) + WD * w)
    return delta, m, v


def _adamw(big, block_rows, groups):
    rows, cols = big[0].shape
    n = len(groups)

    def body(*refs):
        w_ref, g_ref, m_ref, v_ref = refs[0:4]
        small_in = refs[4:4 + 4 * n]
        go_ref, d_ref, nm_ref, nv_ref = refs[4 + 4 * n:8 + 4 * n]
        small_out = refs[8 + 4 * n:]
        g = g_ref[...]
        go_ref[...] = g
        d_ref[...], nm_ref[...], nv_ref[...] = _adamw_math(w_ref[...], g, m_ref[...], v_ref[...])

        @pl.when(pl.program_id(0) == 0)
        def _():
            for t in range(n):
                sw_ref, sg_ref, sm_ref, sv_ref = small_in[4 * t:4 * t + 4]
                sg = sg_ref[0:sw_ref.shape[0], :]
                small_out[4 * t][...] = sg
                small_out[4 * t + 1][...], small_out[4 * t + 2][...], small_out[4 * t + 3][...] = _adamw_math(
                    sw_ref[...], sg, sm_ref[...], sv_ref[...])

    spec = pl.BlockSpec((block_rows, cols), lambda i: (i, 0))
    vm = pl.BlockSpec(memory_space=pltpu.VMEM)
    outs = pl.pallas_call(
        body,
        name="adamw",
        grid=(rows // block_rows,),
        in_specs=[spec] * 4 + [vm] * (4 * n),
        out_specs=[spec] * 4 + [vm] * (4 * n),
        out_shape=[jax.ShapeDtypeStruct(big[0].shape, F32)] * 4
        + [jax.ShapeDtypeStruct(grp[0].shape, F32) for grp in groups for _ in range(4)],
        compiler_params=_cparams(dimension_semantics=("arbitrary",)),
    )(*big, *[a for grp in groups for a in grp])
    return tuple(outs[0:4]), [tuple(outs[4 + 4 * t:8 + 4 * t]) for t in range(n)]


def _rope_tables():
    half = QK_ROPE // 2
    f32 = np.float32
    inv_freq = (f32(1.0) / (f32(ROPE_THETA) ** (np.arange(half, dtype=f32) / f32(half)))).astype(f32)
    pos = np.arange(N, dtype=f32) - f32(PAD)
    ang = (pos[:, None] * inv_freq[None, :]).astype(f32)
    cos, sin = np.cos(ang).astype(f32), np.sin(ang).astype(f32)
    zero = np.zeros((N, 128 - QK_ROPE), f32)
    return jnp.asarray(np.concatenate([cos, cos, zero], axis=1)), jnp.asarray(np.concatenate([-sin, sin, zero], axis=1))


def kernel(x, meta_tokens, norm_g, w_in, q_norm_g, w_q_b, kv_norm_g, w_kv_b, pool_w, pool_scale, w_out, final_norm_g, loss_target, m_meta_tokens, m_norm_g, m_w_in, m_q_norm_g, m_w_q_b, m_kv_norm_g, m_w_kv_b, m_pool_w, m_pool_scale, m_w_out, m_final_norm_g, v_meta_tokens, v_norm_g, v_w_in, v_q_norm_g, v_w_q_b, v_kv_norm_g, v_w_kv_b, v_pool_w, v_pool_scale, v_w_out, v_final_norm_g):
    tr = lambda a: a[0].T
    win, wq, wkv, h, tgt = _gather_weights(tr(w_in), tr(w_q_b), w_kv_b[0], meta_tokens, x[0], loss_target[0])
    cosf, sinf = _rope_tables()
    gf = final_norm_g.reshape(1, D)

    part = _local_step(h, tgt, norm_g, win, q_norm_g, wq, kv_norm_g, wkv, pool_w[0], pool_scale, w_out[0], m_w_out[0],
                       v_w_out[0], gf, cosf, sinf)

    pw2 = lambda a: a.reshape(len(POOL_WINDOWS) * POOL_GROUP, POOL_GROUP)
    gpw = part["gpw"]
    gwinT, gwqT, gwkv, gmeta, gg, ggf, ggq, ggkv, gps, gloss = _reduce_grads(
        part["dsl"], part["hn"], part["dwq"], part["dwkv"], part["dmeta"], part["dg"],
        part["dgf"], part["dgq"], part["dgkv"], part["dps"], part["loss"])

    r_out = part["r_out"]
    fn2 = lambda a: a.reshape(1, D)
    r_in, (r_meta, r_norm, r_gq, r_wq, r_gkv, r_wkv, r_pw, r_ps, r_fn) = _adamw((tr(w_in), gwinT, tr(m_w_in), tr(v_w_in)), 248, [
        (meta_tokens, gmeta, m_meta_tokens, v_meta_tokens),
        (norm_g, gg, m_norm_g, v_norm_g),
        (q_norm_g, ggq, m_q_norm_g, v_q_norm_g),
        (tr(w_q_b), gwqT, tr(m_w_q_b), tr(v_w_q_b)),
        (kv_norm_g, ggkv, m_kv_norm_g, v_kv_norm_g),
        (w_kv_b[0], gwkv, m_w_kv_b[0], v_w_kv_b[0]),
        (pw2(pool_w), gpw, pw2(m_pool_w), pw2(v_pool_w)),
        (pool_scale, gps, m_pool_scale, v_pool_scale),
        (fn2(final_norm_g), ggf, fn2(m_final_norm_g), fn2(v_final_norm_g)),
    ])
    untr = lambda a: a.T[None]
    pw4 = lambda a: a.reshape(1, len(POOL_WINDOWS), POOL_GROUP, POOL_GROUP)
    per_kind = [[
        r_meta[kind], r_norm[kind], untr(r_in[kind]), r_gq[kind], untr(r_wq[kind]), r_gkv[kind], r_wkv[kind][None],
        pw4(r_pw[kind]), r_ps[kind], r_out[kind][None], r_fn[kind].reshape(D),
    ] for kind in range(4)]
    return (gloss[0, 0], part["gx"][None], *per_kind[0], *per_kind[1], *per_kind[2], *per_kind[3])
```

```python
import jax
import jax.numpy as jnp
import numpy as np
from jax import lax
from jax.experimental import pallas as pl
from jax.experimental.pallas import tpu as pltpu

F32 = jnp.float32
BF16 = jnp.bfloat16

D = 1024
S = 2048
N_META = 16
PAD = 112
HEAD_ROWS = PAD + N_META
N = HEAD_ROWS + S
D_POOL = 512
POOL_WINDOWS = (2, 4, 8, 16)
POOL_GROUP = 128
HALO = 16
HEADS = 4
QK_NOPE = 128
QK_ROPE = 64
QK = QK_NOPE + QK_ROPE
V_HEAD = 128
Q_LORA = 256
KV_LORA = 128
D_IN = 1984
EPS = 1e-6
ROPE_THETA = 10000.0
SCALE = QK ** -0.5
CHIPS = 4

ROWS_FWD = 544
ROWS_MID = 544
ROWS_BWD = 544
TK = 128
TQ = 256
NQ = S // TQ
HEADS_PER_STEP_BWD = 2

O_PI, O_PG, O_CQ, O_CKV, O_KR, O_AG = 0, 512, 1024, 1280, 1408, 1472
O_KR_END = O_KR + 128
SHARD_IN = D_IN // CHIPS
SHARD_PAD = 512
SHARD_OUT = D // CHIPS

LR, B1, B2, ADAM_EPS, WD, STEP = 0.001, 0.9, 0.999, 1e-08, 0.01, 10
C1 = 1.0 - B1**STEP
C2 = 1.0 - B2**STEP

VMEM_LIMIT = 60 * 1024 * 1024
MESH = pl.DeviceIdType.MESH
NEG = -1e30

VEC_ROWS = 8
PW_ROWS = len(POOL_WINDOWS) * POOL_GROUP
V_GQ, V_GKV, V_PS, V_LOSS = 0, 256, 384, 896


def _cparams(**kw):
    return pltpu.CompilerParams(vmem_limit_bytes=VMEM_LIMIT, **kw)


def _nt(a, b):
    return lax.dot_general(a, b, (((1,), (1,)), ((), ())), preferred_element_type=F32)


def _tn(a, b):
    return lax.dot_general(a, b, (((0,), (0,)), ((), ())), preferred_element_type=F32)


def _nn(a, b):
    return jnp.dot(a, b, preferred_element_type=F32)


def _swap64(t):
    return pltpu.roll(t, 32, 1) + pltpu.roll(t, 96, 1)


def _sigmoid(x):
    return 1.0 / (1.0 + jnp.exp(-x))


def _low_lanes():
    return (lax.broadcasted_iota(jnp.int32, (1, 128), 1) < QK_ROPE).astype(F32)


def _rows(w, rows):
    return pl.BlockSpec((rows, w), lambda i: (i, 0))


def _const(*shape):
    return pl.BlockSpec(shape, lambda *_: (0,) * len(shape), pipeline_mode=pl.Buffered(1))


STAT_GROUPS = HEADS // HEADS_PER_STEP_BWD


def _stat_slot(head):
    return head // HEADS_PER_STEP_BWD, head % HEADS_PER_STEP_BWD


N_PEERS = 4
SEND_ORDER = (3, 1, 2)


def _peer_signal(x, y, c):
    barrier = pltpu.get_barrier_semaphore()
    peers = [(x, y, 1 - c)] + [(x ^ fx, y ^ fy, c) for fx, fy in _CHIP_RELS[1:]]
    assert len(peers) == N_PEERS
    for peer in peers:
        pl.semaphore_signal(barrier, inc=1, device_id=peer, device_id_type=MESH)


def _peer_wait():
    pl.semaphore_wait(pltpu.get_barrier_semaphore(), N_PEERS)


def _attn_tiles():
    return [(0, TK, TK)] + [(TK + TQ * t, TQ, TK + TQ * (t + 1)) for t in range(NQ)]


def _masked_scores(q, k, rows, klen):
    s = _nt(q, k)
    col = lax.broadcasted_iota(jnp.int32, (1, TK), 1)
    head_bias = jnp.where(col >= PAD, 0.0, NEG)
    if klen == TK:
        return s + head_bias
    r = lax.broadcasted_iota(jnp.int32, (rows, 1), 0) >> 6
    c = lax.broadcasted_iota(jnp.int32, (1, rows), 1) >> 6
    diag_bias = jnp.where(c <= r, 0.0, NEG)
    parts = [s[:, 0:TK] + head_bias]
    if klen - rows > TK:
        parts.append(s[:, TK:klen - rows])
    parts.append(s[:, klen - rows:klen] + diag_bias)
    return jnp.concatenate(parts, axis=1)


def _fwd_in(h, norm_g, win, gq, wq, gkv, wkv, cosf, sinf):
    tr = ROWS_FWD

    def body(h_ref, g_ref, win_ref, gq_ref, wq_ref, gkv_ref, wkv_ref, cos_ref, sin_ref,
             pi_ref, pg_ref, cq_ref, ckv_ref, ag_ref, q_ref, k_ref, v_ref, hn_ref):
        h = h_ref[...]
        r = lax.rsqrt(jnp.mean(h * h, axis=-1, keepdims=True) + EPS)
        hn = ((h * r) * g_ref[...]).astype(BF16)
        hn_ref[...] = hn
        u = _nt(hn, win_ref[0:O_KR_END, :])
        pi_ref[...] = u[:, O_PI:O_PG]
        pg_ref[...] = u[:, O_PG:O_CQ]
        cq = u[:, O_CQ:O_CKV]
        ckv = u[:, O_CKV:O_KR]
        cq_ref[...] = cq
        ckv_ref[...] = ckv
        ag_ref[...] = _nt(hn, win_ref[O_AG:D_IN, :])
        cosv = cos_ref[...]
        sinv = sin_ref[...]
        kr = u[:, O_KR:O_KR_END] * _low_lanes()
        kr = (kr * cosv + _swap64(kr) * sinv).astype(BF16)
        rq = lax.rsqrt(jnp.mean(cq * cq, axis=-1, keepdims=True) + EPS)
        cqn = ((cq * rq) * gq_ref[...]).astype(BF16)
        rkv = lax.rsqrt(jnp.mean(ckv * ckv, axis=-1, keepdims=True) + EPS)
        ckvn = ((ckv * rkv) * gkv_ref[...]).astype(BF16)
        for hd in range(HEADS):
            qh = _nt(cqn, wq_ref[hd]) * SCALE
            z = qh[:, QK_NOPE:]
            q_ref[hd, :, 0:QK_NOPE] = qh[:, 0:QK_NOPE].astype(BF16)
            q_ref[hd, :, QK_NOPE:] = (z * cosv + _swap64(z) * sinv).astype(BF16)
            kvh = _nn(ckvn, wkv_ref[hd])
            k_ref[hd, :, 0:QK_NOPE] = kvh[:, 0:QK_NOPE].astype(BF16)
            k_ref[hd, :, QK_NOPE:] = kr
            v_ref[hd] = kvh[:, QK_NOPE:].astype(BF16)

    head = lambda w: pl.BlockSpec((HEADS, tr, w), lambda i: (0, i, 0))
    return pl.pallas_call(
        body,
        name="fwd_in",
        grid=(N // tr,),
        in_specs=[
            _rows(D, tr), _const(1, D), _const(D_IN, D), _const(1, Q_LORA), _const(HEADS, 256, Q_LORA),
            _const(1, KV_LORA), _const(HEADS, KV_LORA, 256), _rows(128, tr), _rows(128, tr),
        ],
        out_specs=[_rows(D_POOL, tr), _rows(D_POOL, tr), _rows(Q_LORA, tr), _rows(KV_LORA, tr), _rows(D_POOL, tr),
                   head(256), head(256), head(V_HEAD), _rows(D, tr)],
        out_shape=[
            jax.ShapeDtypeStruct((N, D_POOL), F32), jax.ShapeDtypeStruct((N, D_POOL), F32),
            jax.ShapeDtypeStruct((N, Q_LORA), F32), jax.ShapeDtypeStruct((N, KV_LORA), F32),
            jax.ShapeDtypeStruct((N, D_POOL), F32),
            jax.ShapeDtypeStruct((HEADS, N, 256), BF16), jax.ShapeDtypeStruct((HEADS, N, 256), BF16),
            jax.ShapeDtypeStruct((HEADS, N, V_HEAD), BF16), jax.ShapeDtypeStruct((N, D), BF16),
        ],
        compiler_params=_cparams(dimension_semantics=("arbitrary",)),
    )(h, norm_g, win, gq, wq, gkv, wkv, cosf, sinf)


def _attn_fwd(q, k, v, wout_s):
    tiles = _attn_tiles()
    n_t = len(tiles)
    half = SHARD_OUT // 2
    send_step = 2
    fwd_step = n_t - 2

    def body(q_hbm, k_hbm, v_hbm, wout_ref, o_hbm, lse_ref, wout_o, q_buf, k_buf, v_buf, o_buf, s_wout, in_sems, out_sems,
             ici_send, ici_recv, fwd_send, fwd_recv, own_sem):
        step = pl.program_id(0)
        x, y, c = lax.axis_index("x"), lax.axis_index("y"), lax.axis_index("c")
        me = 2 * x + y

        def chip_of(rel):
            fx, fy = _CHIP_RELS[rel]
            return 2 * (x ^ fx) + (y ^ fy)

        def place(chip, core):
            return wout_o.at[pl.ds(pl.multiple_of(SHARD_OUT * chip + half * core, half), half), :]

        def ici_copy(rel, src_chip, to):
            return _remote(s_wout.at[pl.ds(pl.multiple_of(half * c, half), half), :], place(src_chip, c),
                           ici_send.at[rel - 1], ici_recv.at[rel - 1], to)

        def fwd_copy(rel, core, to):
            spot = place(chip_of(rel), core)
            return _remote(spot, spot, fwd_send.at[rel - 1], fwd_recv.at[rel - 1], to)

        own = pltpu.make_async_copy(s_wout, wout_o.at[pl.ds(pl.multiple_of(SHARD_OUT * me, SHARD_OUT), SHARD_OUT), :], own_sem)

        @pl.when(step == 0)
        def _():
            _peer_signal(x, y, c)
            s_wout[...] = wout_ref[...].astype(BF16)
            own.start()

        @pl.when(step == send_step)
        def _():
            _peer_wait()
            for rel in SEND_ORDER:
                fx, fy = _CHIP_RELS[rel]
                ici_copy(rel, me, (x ^ fx, y ^ fy, c)).start()

        @pl.when(step == fwd_step)
        def _():
            for rel in (1, 2, 3):
                ici_copy(rel, chip_of(rel), (x, y, c)).wait_recv()
                fwd_copy(rel, c, (x, y, 1 - c)).start()

        def finish_wout():
            for rel in (1, 2, 3):
                fwd_copy(rel, 1 - c, (x, y, c)).wait_recv()
            for rel in (1, 2, 3):
                ici_copy(rel, me, (x, y, c)).wait_send()
                fwd_copy(rel, c, (x, y, c)).wait_send()
            own.wait()

        def loads(idx):
            q0, rows, _ = tiles[idx]
            rs = pl.ds(q0, rows)
            return [pltpu.make_async_copy(src.at[:, rs, :], dst.at[:, rs, :], in_sems.at[a, idx % 2])
                    for a, (src, dst) in enumerate(((q_hbm, q_buf), (k_hbm, k_buf), (v_hbm, v_buf)))]

        def store(idx):
            q0, rows, _ = tiles[idx]
            return pltpu.make_async_copy(o_buf.at[idx % 2, pl.ds(0, rows), :], o_hbm.at[pl.ds(q0, rows), :],
                                         out_sems.at[idx % 2])

        @pl.when(step == 0)
        def _():
            lse_ref[...] = jnp.zeros_like(lse_ref)
            for cp in loads(0):
                cp.start()

        for idx, (q0, rows, klen) in enumerate(tiles):
            @pl.when(step == idx)
            def _(idx=idx, q0=q0, rows=rows, klen=klen):
                for cp in loads(idx):
                    cp.wait()
                if idx + 1 < n_t:
                    for cp in loads(idx + 1):
                        cp.start()
                if idx >= 2:
                    store(idx - 2).wait()
                for hd in range(HEADS):
                    s = _masked_scores(q_buf[hd, q0:q0 + rows, :], k_buf[hd, 0:klen, :], rows, klen)
                    m = jnp.max(s, axis=-1, keepdims=True)
                    p = jnp.exp(s - m)
                    l = jnp.sum(p, axis=-1, keepdims=True)
                    o_buf[idx % 2, 0:rows, hd * V_HEAD:(hd + 1) * V_HEAD] = _nn(p.astype(BF16), v_buf[hd, 0:klen, :]) / l
                    grp, lane = _stat_slot(hd)
                    lse_ref[grp, q0:q0 + rows, lane:lane + 1] = m + jnp.log(l)
                store(idx).start()
                if idx == n_t - 1:
                    store(idx - 1).wait()
                    store(idx).wait()
                    finish_wout()

    hbm = pl.BlockSpec(memory_space=pl.ANY)
    return pl.pallas_call(
        body,
        name="attn_fwd",
        grid=(n_t,),
        in_specs=[hbm, hbm, hbm, _const(SHARD_OUT, D)],
        out_specs=[hbm, _const(STAT_GROUPS, N, 128), hbm],
        out_shape=[jax.ShapeDtypeStruct((N, HEADS * V_HEAD), F32), jax.ShapeDtypeStruct((STAT_GROUPS, N, 128), F32),
                   jax.ShapeDtypeStruct((D, D), BF16)],
        scratch_shapes=[pltpu.VMEM((HEADS, N, 256), BF16), pltpu.VMEM((HEADS, N, 256), BF16),
                        pltpu.VMEM((HEADS, N, V_HEAD), BF16), pltpu.VMEM((2, TQ, HEADS * V_HEAD), F32),
                        pltpu.VMEM((SHARD_OUT, D), BF16),
                        pltpu.SemaphoreType.DMA((3, 2)), pltpu.SemaphoreType.DMA((2,))]
        + [pltpu.SemaphoreType.DMA((3,))] * 4 + [pltpu.SemaphoreType.DMA],
        compiler_params=_cparams(dimension_semantics=("arbitrary",), collective_id=1),
    )(q, k, v, wout_s)


def _inv_count(row0, rows, w):
    row = row0 + lax.broadcasted_iota(jnp.int32, (rows, 1), 0)
    return 1.0 / jnp.clip(row - (PAD - 1), 1, w).astype(F32)


def _mid(h, tgt, pool_in, pool_gate, attn_gate, attn, pool_w, pool_scale, wout, gf):
    tr = ROWS_MID
    per = tr // HALO
    ng = len(POOL_WINDOWS)

    def body(h_ref, t_ref, pin_ref, halo_ref, pg_ref, ag_ref, at_ref, pw_ref, ps_ref, wout_ref, gf_ref,
             dh2_ref, do_ref, delta_ref, dag_ref, dpg_ref, dpl_ref, dwout_ref, dpw_ref, dps_ref, dgf_ref, loss_ref):
        i = pl.program_id(0)

        @pl.when(i == 0)
        def _():
            dwout_ref[...] = jnp.zeros_like(dwout_ref)
            dpw_ref[...] = jnp.zeros_like(dpw_ref)
            dps_ref[...] = jnp.zeros_like(dps_ref)
            dgf_ref[...] = jnp.zeros_like(dgf_ref)
            loss_ref[...] = jnp.zeros_like(loss_ref)

        row0 = i * tr
        real = (row0 + lax.broadcasted_iota(jnp.int32, (tr, 1), 0)) >= HEAD_ROWS
        h = h_ref[...]

        halo = jnp.where(i > 0, halo_ref[...], 0.0)
        ext = jnp.concatenate([halo, pin_ref[...]], axis=0)
        pooled = []
        for g, w in enumerate(POOL_WINDOWS):
            e = ext[:, g * POOL_GROUP:(g + 1) * POOL_GROUP]
            acc = e
            shift = 1
            while shift < w:
                acc = acc + pltpu.roll(acc, shift, 0)
                shift *= 2
            pooled.append((acc[HALO:] * _inv_count(row0, tr, w) - e[HALO:]).astype(BF16))
        pw = [pw_ref[g].astype(BF16) for g in range(ng)]
        mixed = jnp.concatenate([_nn(pooled[g], pw[g]) for g in range(ng)], axis=1)
        ps = ps_ref[...]
        mixed_s = mixed * ps
        pg = pg_ref[...]
        sig_p = _sigmoid(pg)
        silu_p = pg * sig_p
        pool_out = (silu_p * mixed_s).astype(BF16)
        ag = ag_ref[...]
        sig_a = _sigmoid(ag)
        silu_a = ag * sig_a
        at = at_ref[...]
        attn_out = (silu_a * at).astype(BF16)
        cat = jnp.concatenate([pool_out, attn_out], axis=1)
        h2 = h + _nn(cat, wout_ref[...])

        r2 = lax.rsqrt(jnp.mean(h2 * h2, axis=-1, keepdims=True) + EPS)
        n2 = h2 * r2
        gfv = gf_ref[...]
        err = jnp.where(real, n2 * gfv - t_ref[...], 0.0)
        loss_ref[...] += jnp.sum(jnp.sum(err * err, axis=-1, keepdims=True), axis=0, keepdims=True) * (0.5 / D)
        dy = err * (1.0 / D)
        dgf_ref[...] += jnp.sum(dy * n2, axis=0, keepdims=True)
        dn = dy * gfv
        dh2 = r2 * (dn - n2 * jnp.mean(dn * n2, axis=-1, keepdims=True))
        dh2_ref[...] = dh2
        dh2b = dh2.astype(BF16)

        dwout_ref[...] += _tn(cat, dh2b)
        dcat = _nt(dh2b, wout_ref[...])
        dpo = dcat[:, 0:D_POOL]
        dao = dcat[:, D_POOL:D]
        do = dao * silu_a
        prod = do * at
        delta_ref[...] = jnp.zeros_like(delta_ref)
        for hd in range(HEADS):
            grp, lane = _stat_slot(hd)
            cols = slice(hd * V_HEAD, (hd + 1) * V_HEAD)
            do_ref[grp, :, lane * V_HEAD:(lane + 1) * V_HEAD] = do[:, cols].astype(BF16)
            delta_ref[grp, :, lane:lane + 1] = jnp.sum(prod[:, cols], axis=-1, keepdims=True)
        dag_ref[...] = (dao * at * (sig_a * (1.0 + ag * (1.0 - sig_a)))).astype(BF16)
        dmixed_s = dpo * silu_p
        dpg_ref[...] = (dpo * mixed_s * (sig_p * (1.0 + pg * (1.0 - sig_p)))).astype(BF16)
        dps_ref[...] += jnp.sum(dmixed_s * mixed, axis=0, keepdims=True)
        dmixed = (dmixed_s * ps).astype(BF16)
        dpl = []
        for g in range(ng):
            dm = dmixed[:, g * POOL_GROUP:(g + 1) * POOL_GROUP]
            dpl.append(_nt(dm, pw[g]))
            dpw_ref[g] += _tn(pooled[g], dm)
        dpl_ref[...] = jnp.concatenate(dpl, axis=1)

    halo_spec = pl.BlockSpec((HALO, D_POOL), lambda i: (jnp.maximum(i * per - 1, 0), 0))
    return pl.pallas_call(
        body,
        name="mid",
        grid=(N // tr,),
        in_specs=[
            _rows(D, tr), _rows(D, tr), _rows(D_POOL, tr), halo_spec, _rows(D_POOL, tr), _rows(D_POOL, tr),
            _rows(D_POOL, tr), _const(ng, POOL_GROUP, POOL_GROUP), _const(1, D_POOL), _const(D, D), _const(1, D),
        ],
        out_specs=[
            _rows(D, tr), pl.BlockSpec((STAT_GROUPS, tr, HEADS_PER_STEP_BWD * V_HEAD), lambda i: (0, i, 0)),
            pl.BlockSpec((STAT_GROUPS, tr, 128), lambda i: (0, i, 0)),
            _rows(D_POOL, tr), _rows(D_POOL, tr), _rows(D_POOL, tr),
            _const(D, D), _const(ng, POOL_GROUP, POOL_GROUP), _const(1, D_POOL), _const(1, D), _const(1, 128),
        ],
        out_shape=[
            jax.ShapeDtypeStruct((N, D), F32), jax.ShapeDtypeStruct((STAT_GROUPS, N, HEADS_PER_STEP_BWD * V_HEAD), BF16),
            jax.ShapeDtypeStruct((STAT_GROUPS, N, 128), F32),
            jax.ShapeDtypeStruct((N, D_POOL), BF16), jax.ShapeDtypeStruct((N, D_POOL), BF16),
            jax.ShapeDtypeStruct((N, D_POOL), F32), jax.ShapeDtypeStruct((D, D), F32),
            jax.ShapeDtypeStruct((ng, POOL_GROUP, POOL_GROUP), F32),
            jax.ShapeDtypeStruct((1, D_POOL), F32), jax.ShapeDtypeStruct((1, D), F32), jax.ShapeDtypeStruct((1, 128), F32),
        ],
        compiler_params=_cparams(dimension_semantics=("arbitrary",)),
    )(h, tgt, pool_in, pool_in, pool_gate, attn_gate, attn, pool_w, pool_scale, wout, gf)


def _unrope(dy, cosv, sinv):
    return dy * cosv + _swap64(dy * sinv) * _low_lanes()


def _attn_bwd(q, k, v, do, lse, delta, cosf, sinf, dwout, dpw):
    tiles = _attn_tiles()
    hp = HEADS_PER_STEP_BWD
    n_g = HEADS // hp
    n_t = len(tiles)
    half = SHARD_OUT // 2
    half_pw = PW_ROWS // 2
    swap_at, send_at, sum_at = (0, 3), (0, 5), (n_g - 1, n_t // 2)

    def body(q_hbm, k_hbm, v_hbm, do_hbm, lse_ref, delta_ref, cos_ref, sin_ref, dwout_hbm, dpw_ref, dq_hbm, dkv_ref, dkr_ref,
             gwout_ref, gpw_ref, q_buf, k_buf, v_buf, do_buf, dq_buf, dk_acc, dv_acc, own_w, sib_w, stage_w, recv_w, gw_buf,
             pw_sib, pw_chip, pw_recv, pw_buf,
             in_sems, out_sems, ow_sems, d2d_send, d2d_recv, ici_send, ici_recv, fin_send, fin_recv,
             pw_swap_send, pw_swap_recv, pw_ici_send, pw_ici_recv, pw_fin_send, pw_fin_recv):
        grp = pl.program_id(0)
        step = pl.program_id(1)
        heads = pl.ds(grp * hp, hp)
        x, y, c = lax.axis_index("x"), lax.axis_index("y"), lax.axis_index("c")
        me = 2 * x + y
        sibling = (x, y, 1 - c)

        def pw_rows(core):
            return pl.ds(pl.multiple_of(half_pw * core, half_pw), half_pw)

        def pw_swap():
            return _remote(dpw_ref, pw_sib, pw_swap_send.at[0], pw_swap_recv.at[0], sibling)

        def pw_ici(rel):
            fx, fy = _CHIP_RELS[rel]
            return _remote(pw_chip.at[pw_rows(c), :], pw_recv.at[rel - 1], pw_ici_send.at[rel - 1], pw_ici_recv.at[rel - 1],
                           (x ^ fx, y ^ fy, c))

        def pw_fin(core):
            spot = pw_buf.at[pw_rows(core), :]
            return _remote(spot, spot, pw_fin_send.at[0], pw_fin_recv.at[0], sibling)

        def chip_of(rel):
            fx, fy = _CHIP_RELS[rel]
            return 2 * (x ^ fx) + (y ^ fy)

        def piece(chip, core):
            return dwout_hbm.at[pl.ds(pl.multiple_of(SHARD_OUT * chip + half * core, half), half), :]

        def own_load(rel):
            return pltpu.make_async_copy(piece(chip_of(rel), c), own_w.at[rel], ow_sems.at[rel])

        def d2d_copy(rel):
            return _remote(piece(chip_of(rel), 1 - c), sib_w.at[rel], d2d_send.at[rel], d2d_recv.at[rel], sibling)

        def ici_copy(rel):
            fx, fy = _CHIP_RELS[rel]
            return _remote(stage_w.at[rel - 1], recv_w.at[rel - 1], ici_send.at[rel - 1], ici_recv.at[rel - 1],
                           (x ^ fx, y ^ fy, c))

        def fin_copy(core):
            spot = gw_buf.at[pl.ds(pl.multiple_of(half * core, half), half), :]
            return _remote(spot, spot, fin_send.at[0], fin_recv.at[0], sibling)

        @pl.when((grp == 0) & (step == 0))
        def _():
            _peer_signal(x, y, c)
            for rel in SEND_ORDER + (0,):
                own_load(rel).start()

        @pl.when((grp == swap_at[0]) & (step == swap_at[1]))
        def _():
            _peer_wait()
            pw_swap().start()
            for rel in SEND_ORDER + (0,):
                d2d_copy(rel).start()

        @pl.when((grp == send_at[0]) & (step == send_at[1]))
        def _():
            for rel in SEND_ORDER:
                own_load(rel).wait()
                d2d_copy(rel).wait_recv()
                stage_w[rel - 1] = (own_w[rel] + sib_w[rel]).astype(BF16)
                ici_copy(rel).start()
            pw_swap().wait_recv()
            pw_chip[...] = dpw_ref[...] + pw_sib[...]
            for rel in SEND_ORDER:
                pw_ici(rel).start()

        @pl.when((grp == sum_at[0]) & (step == sum_at[1]))
        def _():
            own_load(0).wait()
            d2d_copy(0).wait_recv()
            total = own_w[0] + sib_w[0]
            for rel in (1, 2, 3):
                ici_copy(rel).wait_recv()
                total = total + recv_w[rel - 1].astype(F32)
            gw_buf[pl.ds(pl.multiple_of(half * c, half), half), :] = total
            fin_copy(c).start()
            for rel in (1, 2, 3):
                pw_ici(rel).wait_recv()
            total = jnp.zeros((half_pw, POOL_GROUP), F32)
            for chip in range(CHIPS):
                flips = chip ^ me
                rel = jnp.where(flips == 2, 1, jnp.where(flips == 1, 2, flips))
                total = total + jnp.where(rel == 0, pw_chip[pw_rows(c), :], pw_recv[jnp.maximum(rel - 1, 0)])
            pw_buf[pw_rows(c), :] = total
            pw_fin(c).start()

        def finish_dwout():
            fin_copy(1 - c).wait_recv()
            pw_fin(1 - c).wait_recv()
            for rel in (0, 1, 2, 3):
                d2d_copy(rel).wait_send()
            for rel in (1, 2, 3):
                ici_copy(rel).wait_send()
                pw_ici(rel).wait_send()
            fin_copy(c).wait_send()
            pw_swap().wait_send()
            pw_fin(c).wait_send()
            gwout_ref[...] = gw_buf[...]
            gpw_ref[...] = pw_buf[...]

        def loads(g, idx):
            q0, rows, _ = tiles[idx]
            rs = pl.ds(q0, rows)
            par = (g * n_t + idx) % 2
            hs = pl.ds(g * hp, hp)
            pairs = ((q_hbm.at[hs, rs, :], q_buf.at[:, rs, :]), (k_hbm.at[hs, rs, :], k_buf.at[:, rs, :]),
                     (v_hbm.at[hs, rs, :], v_buf.at[:, rs, :]), (do_hbm.at[g, rs, :], do_buf.at[rs, :]))
            return [pltpu.make_async_copy(src, dst, in_sems.at[a, par]) for a, (src, dst) in enumerate(pairs)]

        def store(idx):
            q0, rows, _ = tiles[idx]
            return pltpu.make_async_copy(dq_buf.at[idx % 2, :, pl.ds(0, rows), :], dq_hbm.at[heads, pl.ds(q0, rows), :],
                                         out_sems.at[idx % 2])

        @pl.when(step == 0)
        def _():
            dk_acc[...] = jnp.zeros_like(dk_acc)
            dv_acc[...] = jnp.zeros_like(dv_acc)

        @pl.when((step == 0) & (grp == 0))
        def _():
            dkr_ref[...] = jnp.zeros_like(dkr_ref)
            for cp in loads(grp, 0):
                cp.start()

        for idx, (q0, rows, klen) in enumerate(tiles):
            @pl.when(step == idx)
            def _(idx=idx, q0=q0, rows=rows, klen=klen):
                for cp in loads(grp, idx):
                    cp.wait()
                if idx + 1 < n_t:
                    for cp in loads(grp, idx + 1):
                        cp.start()
                if idx >= 2:
                    store(idx - 2).wait()
                qs = pl.ds(q0, rows)
                for hd in range(hp):
                    qv = q_buf[hd, qs, :]
                    kv = k_buf[hd, 0:klen, :]
                    p = jnp.exp(_masked_scores(qv, kv, rows, klen) - lse_ref[0, qs, hd:hd + 1])
                    dob = do_buf[qs, hd * V_HEAD:(hd + 1) * V_HEAD]
                    ds = (p * (_nt(dob, v_buf[hd, 0:klen, :]) - delta_ref[0, qs, hd:hd + 1])).astype(BF16)
                    dq = _nn(ds, kv) * SCALE
                    dq_buf[idx % 2, hd, 0:rows, 0:QK_NOPE] = dq[:, 0:QK_NOPE].astype(BF16)
                    dq_buf[idx % 2, hd, 0:rows, QK_NOPE:] = _unrope(dq[:, QK_NOPE:], cos_ref[qs, :], sin_ref[qs, :]).astype(BF16)
                    dk_acc[hd, 0:klen, :] += _tn(ds, qv)
                    dv_acc[hd, 0:klen, :] += _tn(p.astype(BF16), dob)
                store(idx).start()

        @pl.when(step == n_t - 1)
        def _():
            @pl.when(grp + 1 < n_g)
            def _():
                for cp in loads(grp + 1, 0):
                    cp.start()

            for hd in range(hp):
                dkv_ref[hd, :, 0:QK_NOPE] = dk_acc[hd, :, 0:QK_NOPE].astype(BF16)
                dkv_ref[hd, :, QK_NOPE:] = dv_acc[hd].astype(BF16)
                dkr_ref[...] += dk_acc[hd, :, QK_NOPE:]
            store(n_t - 2).wait()
            store(n_t - 1).wait()

            @pl.when(grp == n_g - 1)
            def _():
                finish_dwout()

    hbm = pl.BlockSpec(memory_space=pl.ANY)
    stat = pl.BlockSpec((1, N, 128), lambda g, t: (g, 0, 0), pipeline_mode=pl.Buffered(1))
    piece_f32 = lambda lead: pltpu.VMEM((lead, half, D), F32)
    piece_bf16 = lambda lead: pltpu.VMEM((lead, half, D), BF16)
    return pl.pallas_call(
        body,
        name="attn_bwd",
        grid=(n_g, n_t),
        in_specs=[hbm, hbm, hbm, hbm, stat, stat, _const(N, 128), _const(N, 128), hbm, _const(PW_ROWS, POOL_GROUP)],
        out_specs=[hbm, pl.BlockSpec((hp, N, 256), lambda g, t: (g, 0, 0), pipeline_mode=pl.Buffered(1)), _const(N, 128),
                   _const(SHARD_OUT, D), _const(PW_ROWS, POOL_GROUP)],
        out_shape=[
            jax.ShapeDtypeStruct((HEADS, N, 256), BF16), jax.ShapeDtypeStruct((HEADS, N, 256), BF16),
            jax.ShapeDtypeStruct((N, 128), F32), jax.ShapeDtypeStruct((SHARD_OUT, D), F32),
            jax.ShapeDtypeStruct((PW_ROWS, POOL_GROUP), F32),
        ],
        scratch_shapes=[pltpu.VMEM((hp, N, 256), BF16), pltpu.VMEM((hp, N, 256), BF16), pltpu.VMEM((hp, N, V_HEAD), BF16),
                        pltpu.VMEM((N, hp * V_HEAD), BF16), pltpu.VMEM((2, hp, TQ, 256), BF16),
                        pltpu.VMEM((hp, N, 256), F32), pltpu.VMEM((hp, N, V_HEAD), F32),
                        piece_f32(CHIPS), piece_f32(CHIPS), piece_bf16(3), piece_bf16(3), pltpu.VMEM((SHARD_OUT, D), F32),
                        pltpu.VMEM((PW_ROWS, POOL_GROUP), F32), pltpu.VMEM((PW_ROWS, POOL_GROUP), F32),
                        pltpu.VMEM((3, half_pw, POOL_GROUP), F32), pltpu.VMEM((PW_ROWS, POOL_GROUP), F32),
                        pltpu.SemaphoreType.DMA((4, 2)), pltpu.SemaphoreType.DMA((2,)), pltpu.SemaphoreType.DMA((CHIPS,)),
                        pltpu.SemaphoreType.DMA((CHIPS,)), pltpu.SemaphoreType.DMA((CHIPS,)),
                        pltpu.SemaphoreType.DMA((3,)), pltpu.SemaphoreType.DMA((3,)),
                        pltpu.SemaphoreType.DMA((1,)), pltpu.SemaphoreType.DMA((1,)),
                        pltpu.SemaphoreType.DMA((1,)), pltpu.SemaphoreType.DMA((1,)),
                        pltpu.SemaphoreType.DMA((3,)), pltpu.SemaphoreType.DMA((3,)),
                        pltpu.SemaphoreType.DMA((1,)), pltpu.SemaphoreType.DMA((1,))],
        compiler_params=_cparams(dimension_semantics=("arbitrary", "arbitrary"), collective_id=2),
    )(q, k, v, do, lse, delta, cosf, sinf, dwout, dpw)


def _bwd_in(h, dh2, dq, dkv, dkr, cq, ckv, dpl, dpg, dag, norm_g, win, gq, wq, gkv, wkv, cosf, sinf, adam_out):
    tr = ROWS_BWD
    nb = N // tr
    per = tr // HALO
    lead = HEAD_ROWS
    adam_rows = SHARD_OUT // nb

    def body(h_hbm, dh2_hbm, dq_ref, dkv_ref, dkr_ref, cq_ref, ckv_ref, dpl_ref, halo_ref, dpg_ref, dag_ref,
             g_ref, win_ref, gq_ref, wq_ref, gkv_ref, wkv_ref, cos_ref, sin_ref, aw_ref, ag_ref, am_ref, av_ref,
             gx_ref, dmeta_ref, dsl_ref, dwq_ref, dwkv_ref, dg_ref, dgq_ref, dgkv_ref, ago_ref, ad_ref, anm_ref, anv_ref,
             dh_buf, gx_sem, h_buf, dh2_buf, ld_sems):
        i = pl.program_id(0)

        def tile_load(t, which):
            src, dst = ((h_hbm, h_buf), (dh2_hbm, dh2_buf))[which]
            return pltpu.make_async_copy(src.at[pl.ds(pl.multiple_of(t * tr, 16), tr), :], dst.at[t], ld_sems.at[which, t])

        @pl.when(i == 0)
        def _():
            for t in (0, 1):
                for which in (0, 1):
                    tile_load(t, which).start()

        @pl.when(i + 2 < nb)
        def _():
            for which in (0, 1):
                tile_load(i + 2, which).start()

        tile_load(i, 0).wait()
        tile_load(i, 1).wait()
        grad_out = ag_ref[...]
        ago_ref[...] = grad_out
        ad_ref[...], anm_ref[...], anv_ref[...] = _adamw_math(aw_ref[...], grad_out, am_ref[...], av_ref[...])

        @pl.when(i == 0)
        def _():
            dwq_ref[...] = jnp.zeros_like(dwq_ref)
            dwkv_ref[...] = jnp.zeros_like(dwkv_ref)
            dg_ref[...] = jnp.zeros_like(dg_ref)
            dgq_ref[...] = jnp.zeros_like(dgq_ref)
            dgkv_ref[...] = jnp.zeros_like(dgkv_ref)

        row0 = i * tr
        h = h_buf[i]
        r = lax.rsqrt(jnp.mean(h * h, axis=-1, keepdims=True) + EPS)
        n = h * r
        gv = g_ref[...]
        cq = cq_ref[...]
        rq = lax.rsqrt(jnp.mean(cq * cq, axis=-1, keepdims=True) + EPS)
        nq = cq * rq
        gqv = gq_ref[...]
        cqn = (nq * gqv).astype(BF16)
        dcqn = jnp.zeros((tr, Q_LORA), F32)
        for hd in range(HEADS):
            dqf = dq_ref[hd]
            dcqn = dcqn + _nn(dqf, wq_ref[hd])
            dwq_ref[hd] += _tn(dqf, cqn)
        dgq_ref[...] += jnp.sum(dcqn * nq, axis=0, keepdims=True)
        dnq = dcqn * gqv
        dcq = rq * (dnq - nq * jnp.mean(dnq * nq, axis=-1, keepdims=True))

        ckv = ckv_ref[...]
        rkv = lax.rsqrt(jnp.mean(ckv * ckv, axis=-1, keepdims=True) + EPS)
        nkv = ckv * rkv
        gkvv = gkv_ref[...]
        ckvn = (nkv * gkvv).astype(BF16)
        dckvn = jnp.zeros((tr, KV_LORA), F32)
        for hd in range(HEADS):
            dkv = dkv_ref[hd]
            dckvn = dckvn + _nt(dkv, wkv_ref[hd])
            dwkv_ref[hd] += _tn(ckvn, dkv)
        dgkv_ref[...] += jnp.sum(dckvn * nkv, axis=0, keepdims=True)
        dnkv = dckvn * gkvv
        dckv = rkv * (dnkv - nkv * jnp.mean(dnkv * nkv, axis=-1, keepdims=True))
        dkr = _unrope(dkr_ref[...], cos_ref[...], sin_ref[...])

        cur = dpl_ref[...]
        halo = jnp.where(i < nb - 1, halo_ref[...], 0.0)
        dpi = []
        for g, w in enumerate(POOL_WINDOWS):
            sl = slice(g * POOL_GROUP, (g + 1) * POOL_GROUP)
            a = jnp.concatenate([cur[:, sl] * _inv_count(row0, tr, w), halo[:, sl] * _inv_count(row0 + tr, HALO, w)], axis=0)
            acc = a
            shift = 1
            while shift < w:
                acc = acc + pltpu.roll(acc, tr + HALO - shift, 0)
                shift *= 2
            dpi.append(acc[0:tr] - cur[:, sl])

        du = jnp.concatenate([t.astype(BF16) for t in dpi] + [dpg_ref[...]] + [t.astype(BF16) for t in (dcq, dckv, dkr)],
                             axis=1)
        dagb = dag_ref[...]
        by_row = jnp.concatenate(dpi + [dpg_ref[...].astype(F32), dcq, dckv, dkr[:, 0:QK_ROPE], dagb.astype(F32),
                                        jnp.zeros((tr, SHARD_PAD - SHARD_IN), F32)], axis=1)
        for chip in range(CHIPS):
            dsl_ref[chip] = by_row[:, SHARD_IN * chip:SHARD_IN * chip + SHARD_PAD].astype(BF16)
        dhn = _nn(du, win_ref[0:O_KR_END, :]) + _nn(dagb, win_ref[O_AG:D_IN, :])
        dg_ref[...] += jnp.sum(dhn * n, axis=0, keepdims=True)
        dn = dhn * gv
        dh = dh2_buf[i] + r * (dn - n * jnp.mean(dn * n, axis=-1, keepdims=True))

        first = pltpu.make_async_copy(dh_buf.at[pl.ds(lead, tr - lead), :], gx_ref.at[pl.ds(0, tr - lead), :], gx_sem)
        later = lambda step: pltpu.make_async_copy(
            dh_buf, gx_ref.at[pl.ds(pl.multiple_of(step * tr - lead, 16), tr), :], gx_sem)

        @pl.when(i == 1)
        def _():
            first.wait()

        @pl.when(i > 1)
        def _():
            later(i - 1).wait()

        dh_buf[...] = dh

        @pl.when(i == 0)
        def _():
            first.start()
            for chip in range(CHIPS):
                dmeta_ref[chip] = dh[PAD:HEAD_ROWS, chip * 256:(chip + 1) * 256]

        @pl.when(i > 0)
        def _():
            later(i).start()

        @pl.when(i == nb - 1)
        def _():
            later(i).wait()

    head = lambda w: pl.BlockSpec((HEADS, tr, w), lambda i: (0, i, 0))
    halo_spec = pl.BlockSpec((HALO, D_POOL), lambda i: (jnp.minimum((i + 1) * per, N // HALO - 1), 0))
    return pl.pallas_call(
        body,
        name="bwd_in",
        grid=(nb,),
        in_specs=[
            pl.BlockSpec(memory_space=pl.ANY), pl.BlockSpec(memory_space=pl.ANY),
            head(256), head(256), _rows(128, tr), _rows(Q_LORA, tr), _rows(KV_LORA, tr),
            _rows(D_POOL, tr), halo_spec, _rows(D_POOL, tr), _rows(D_POOL, tr),
            _const(1, D), _const(D_IN, D), _const(1, Q_LORA), _const(HEADS, 256, Q_LORA),
            _const(1, KV_LORA), _const(HEADS, KV_LORA, 256), _rows(128, tr), _rows(128, tr),
        ] + [_rows(D, adam_rows)] * 4,
        out_specs=[
            pl.BlockSpec(memory_space=pl.ANY), _const(CHIPS, N_META, 256),
            pl.BlockSpec((CHIPS, tr, SHARD_PAD), lambda i: (0, i, 0)), _const(HEADS, 256, Q_LORA),
            _const(HEADS, KV_LORA, 256), _const(1, D), _const(1, Q_LORA), _const(1, KV_LORA),
        ] + [_rows(D, adam_rows)] * 4,
        out_shape=[
            jax.ShapeDtypeStruct((S, D), F32), jax.ShapeDtypeStruct((CHIPS, N_META, 256), F32),
            jax.ShapeDtypeStruct((CHIPS, N, SHARD_PAD), BF16), jax.ShapeDtypeStruct((HEADS, 256, Q_LORA), F32),
            jax.ShapeDtypeStruct((HEADS, KV_LORA, 256), F32),
            jax.ShapeDtypeStruct((1, D), F32), jax.ShapeDtypeStruct((1, Q_LORA), F32), jax.ShapeDtypeStruct((1, KV_LORA), F32),
        ] + [jax.ShapeDtypeStruct((SHARD_OUT, D), F32)] * 4,
        scratch_shapes=[pltpu.VMEM((tr, D), F32), pltpu.SemaphoreType.DMA,
                        pltpu.VMEM((nb, tr, D), F32), pltpu.VMEM((nb, tr, D), F32), pltpu.SemaphoreType.DMA((2, nb))],
        compiler_params=_cparams(dimension_semantics=("arbitrary",)),
    )(h, dh2, dq, dkv, dkr, cq, ckv, dpl, dpl, dpg, dag, norm_g, win, gq, wq, gkv, wkv, cosf, sinf, *adam_out)


def _local_step(h, tgt, norm_g, win, gq, wq, gkv, wkv, pool_w, pool_scale, wout_s, m_wout_s, v_wout_s, gf, cosf, sinf):
    pool_in, pool_gate, cq, ckv, attn_gate, q, k, v, hn = _fwd_in(h, norm_g, win, gq, wq, gkv, wkv, cosf, sinf)
    attn, lse, wout = _attn_fwd(q, k, v, wout_s)
    dh2, do, delta, dag, dpg, dpl, dwout, dpw, dps, dgf, loss = _mid(
        h, tgt, pool_in, pool_gate, attn_gate, attn, pool_w, pool_scale, wout, gf)
    dq, dkv, dkr, gwout, gpw = _attn_bwd(q, k, v, do, lse, delta, cosf, sinf, dwout, dpw.reshape(PW_ROWS, POOL_GROUP))
    gx, dmeta, dsl, dwq, dwkv, dg, dgq, dgkv, *r_out = _bwd_in(
        h, dh2, dq, dkv, dkr, cq, ckv, dpl, dpg, dag, norm_g, win, gq, wq, gkv, wkv, cosf, sinf,
        (wout_s, gwout, m_wout_s, v_wout_s))
    return dict(gx=gx, dmeta=dmeta, dsl=dsl, hn=hn, dwq=dwq, dwkv=dwkv, r_out=tuple(r_out), dg=dg, dgq=dgq,
                dgkv=dgkv, gpw=gpw, dps=dps, dgf=dgf, loss=loss)


_CHIP_RELS = ((0, 0), (1, 0), (0, 1), (1, 1))

_ARR_ROWS = (SHARD_IN, SHARD_OUT, 256, KV_LORA, N_META)
_ARR_COLS = (D, D, Q_LORA, 256, 256)
_PIECES = (
    (0, 0, 256, 0), (0, 256, SHARD_IN - 256, 1),
    (1, 0, 128, 0), (1, 128, 128, 1),
    (2, 0, 128, 0), (2, 128, 128, 1),
    (3, 0, 64, 0), (3, 64, 64, 1),
    (4, 0, N_META, 0),
)
_NP = len(_PIECES)
_PIECE_MAX = (256, 128, 128, 64, N_META)


def _gathered_at(refs, arr, chip, r0, n):
    if arr in (0, 1):
        return refs[arr].at[pl.ds(pl.multiple_of(_ARR_ROWS[arr] * chip + r0, 16), n), :]
    return refs[arr].at[chip, pl.ds(r0, n), :]


def _remote(src, dst, send_sem, recv_sem, to):
    return pltpu.make_async_remote_copy(src_ref=src, dst_ref=dst, send_sem=send_sem, recv_sem=recv_sem,
                                        device_id=to, device_id_type=MESH)


def _gather_weights(winT_s, wqT_s, wkv_s, meta_s, x2, tgt2):
    arrays = (0, 2, 3, 4)

    def body(win_ref, wq_ref, wkv_ref, meta_ref, x_ref, t_ref, win_o, wq_o, wkv_o, h_o, tp_o,
             s_win, s_wq, s_wkv, meta_all, head_buf, x_buf, t_buf, ici_send, ici_recv, fwd_send, fwd_recv,
             loc_sems, own_sems):
        x, y, c = lax.axis_index("x"), lax.axis_index("y"), lax.axis_index("c")
        me = 2 * x + y
        stage = (s_win, None, s_wq, s_wkv, meta_ref)
        outs = (win_o, None, wq_o, wkv_o, meta_all)

        _peer_signal(x, y, c)

        frames = pl.ds(HEAD_ROWS, S)
        loads = [pltpu.make_async_copy(x_ref, x_buf, loc_sems.at[0]), pltpu.make_async_copy(t_ref, t_buf, loc_sems.at[1])]
        local = [pltpu.make_async_copy(x_buf, h_o.at[frames, :], loc_sems.at[0]),
                 pltpu.make_async_copy(t_buf, tp_o.at[frames, :], loc_sems.at[1])]
        for cp in loads:
            cp.start()

        s_win[...] = win_ref[...].astype(BF16)
        s_wq[0:QK, :] = wq_ref[...].astype(BF16)
        s_wq[QK:256, :] = jnp.zeros((256 - QK, Q_LORA), BF16)
        s_wkv[...] = wkv_ref[...].astype(BF16)
        head_buf[...] = jnp.zeros_like(head_buf)
        zeros = pltpu.make_async_copy(head_buf, tp_o.at[pl.ds(0, HEAD_ROWS), :], loc_sems.at[2])
        zeros.start()

        def chip_of(rel):
            fx, fy = _CHIP_RELS[rel]
            return 2 * (x ^ fx) + (y ^ fy)

        def same_core_of(rel):
            fx, fy = _CHIP_RELS[rel]
            return (x ^ fx, y ^ fy, c)

        def ici_copy(rel, i, src_chip, to):
            arr, r0, n, _ = _PIECES[i]
            k = (rel - 1) * _NP + i
            return _remote(stage[arr].at[pl.ds(r0, n), :], _gathered_at(outs, arr, src_chip, r0, n),
                           ici_send.at[k], ici_recv.at[k], to)

        def fwd_copy(rel, i, to):
            arr, r0, n, _ = _PIECES[i]
            k = (rel - 1) * _NP + i
            place = _gathered_at(outs, arr, chip_of(rel), r0, n)
            return _remote(place, place, fwd_send.at[k], fwd_recv.at[k], to)

        _peer_wait()
        for core in (0, 1):
            @pl.when(c == core)
            def _(core=core):
                mine = [i for i in range(_NP) if _PIECES[i][3] == core and _PIECES[i][0] in arrays]
                theirs = [i for i in range(_NP) if _PIECES[i][3] != core and _PIECES[i][0] in arrays]
                order = (1, 2, 3)
                sends = [ici_copy(rel, i, me, same_core_of(rel)) for rel in order for i in mine]
                for cp in sends:
                    cp.start()
                for ld, st in zip(loads, local):
                    ld.wait()
                    st.start()
                own = [pltpu.make_async_copy(stage[arr], _gathered_at(outs, arr, me, 0, _ARR_ROWS[arr]), own_sems.at[arr])
                       for arr in arrays if arr != 4]
                for cp in own:
                    cp.start()
                meta_all[me] = meta_ref[...]
                for rel in order:
                    for i in mine:
                        ici_copy(rel, i, chip_of(rel), (x, y, c)).wait_recv()
                        fwd = fwd_copy(rel, i, (x, y, 1 - c))
                        fwd.start()
                        sends.append(fwd)
                for rel in order:
                    for i in theirs:
                        fwd_copy(rel, i, (x, y, c)).wait_recv()
                for cp in sends:
                    cp.wait_send()
                for cp in own:
                    cp.wait()

        zeros.wait()
        for chip in range(CHIPS):
            head_buf[PAD:HEAD_ROWS, chip * 256:(chip + 1) * 256] = meta_all[chip]
        head = pltpu.make_async_copy(head_buf, h_o.at[pl.ds(0, HEAD_ROWS), :], loc_sems.at[2])
        head.start()
        head.wait()
        for cp in local:
            cp.wait()

    vm = pl.BlockSpec(memory_space=pltpu.VMEM)
    hbm = pl.BlockSpec(memory_space=pl.ANY)
    return pl.pallas_call(
        body,
        name="gather_weights",
        in_specs=[vm] * 4 + [hbm] * 2,
        out_specs=[hbm] * 5,
        out_shape=[
            jax.ShapeDtypeStruct((D_IN, D), BF16),
            jax.ShapeDtypeStruct((CHIPS, 256, Q_LORA), BF16), jax.ShapeDtypeStruct((CHIPS, KV_LORA, 256), BF16),
            jax.ShapeDtypeStruct((N, D), F32), jax.ShapeDtypeStruct((N, D), F32),
        ],
        scratch_shapes=[pltpu.VMEM((_ARR_ROWS[a], _ARR_COLS[a]), BF16) for a in (0, 2, 3)]
        + [pltpu.VMEM((CHIPS, N_META, 256), F32), pltpu.VMEM((HEAD_ROWS, D), F32), pltpu.VMEM((S, D), F32),
           pltpu.VMEM((S, D), F32)]
        + [pltpu.SemaphoreType.DMA((3 * _NP,))] * 4 + [pltpu.SemaphoreType.DMA((3,)), pltpu.SemaphoreType.DMA((4,))],
        compiler_params=_cparams(collective_id=0),
    )(winT_s, wqT_s, wkv_s, meta_s, x2, tgt2)


_SM_ROWS = (VEC_ROWS,)
_SM_COLS = (D,)
_SM_PIECES = ((0, 0, VEC_ROWS, 0),)
_NSP = len(_SM_PIECES)
_NSB = len(_SM_ROWS)


def _reduce_grads(dsl, hn, dwq, dwkv, dmeta4, dg, dgf, dgq, dgkv, dps, loss):
    arrays = (0, 2, 3, 4)
    loaded = (2, 3, 4)
    shard_order = SEND_ORDER + (0,)

    def body(dsl_hbm, hn_hbm, dwq_ref, dwkv_ref, dmeta_ref, dg_ref, dgf_ref, dgq_ref, dgkv_ref, dps_ref,
             loss_ref, gwin_o, gwq_o, gwkv_o, gmeta_o, gg_o, ggf_o, ggq_o, ggkv_o, gps_o, gloss_o,
             ow2, ow3, ow4, sb0, sb2, sb3, sb4, st0, st2, st3, st4, rc0, rc2, rc3, rc4,
             vec, sm_sb0, sm_cs0, sm_rc0, vec_fin, slab_v, hn_v, dwin_buf, own0,
             own_sems, d2d_send, d2d_recv, ici_send, ici_recv, fin_send, fin_recv,
             swap_send, swap_recv, smi_send, smi_recv, smf_send, smf_recv, ld_sems):
        x, y, c = lax.axis_index("x"), lax.axis_index("y"), lax.axis_index("c")
        me = 2 * x + y
        _peer_signal(x, y, c)
        grads = (None, None, dwq_ref, dwkv_ref, dmeta_ref)
        outs = (gwin_o, None, gwq_o, gwkv_o, gmeta_o)
        own_buf = (None, None, ow2, ow3, ow4)
        sib_buf = (sb0, None, sb2, sb3, sb4)
        stage = (st0, None, st2, st3, st4)
        recv = (rc0, None, rc2, rc3, rc4)
        sm_mine = (vec,)
        sm_sib = (sm_sb0,)
        sm_chip = (sm_cs0,)
        sm_recv = (sm_rc0,)
        sm_out = (vec_fin,)
        sibling = (x, y, 1 - c)

        def chip_of(rel):
            fx, fy = _CHIP_RELS[rel]
            return 2 * (x ^ fx) + (y ^ fy)

        def same_core_of(rel):
            fx, fy = _CHIP_RELS[rel]
            return (x ^ fx, y ^ fy, c)

        hn_load = pltpu.make_async_copy(hn_hbm, hn_v, ld_sems.at[CHIPS])

        def slab_load(rel):
            return pltpu.make_async_copy(dsl_hbm.at[chip_of(rel)], slab_v.at[rel], ld_sems.at[rel])

        hn_load.start()
        slab_load(shard_order[0]).start()

        def slot(bufs, i, idx):
            arr, _, n, _ = _PIECES[i]
            return bufs[arr].at[idx, pl.ds(0, n), :]

        def own_load(rel, i):
            arr, r0, n, _ = _PIECES[i]
            return pltpu.make_async_copy(_gathered_at(grads, arr, chip_of(rel), r0, n), slot(own_buf, i, rel),
                                         own_sems.at[rel * _NP + i])

        def d2d_copy(rel, i):
            arr, r0, n, _ = _PIECES[i]
            k = rel * _NP + i
            return _remote(_gathered_at(grads, arr, chip_of(rel), r0, n), slot(sib_buf, i, rel),
                           d2d_send.at[k], d2d_recv.at[k], sibling)

        def ici_copy(rel, i):
            k = (rel - 1) * _NP + i
            return _remote(slot(stage, i, rel - 1), slot(recv, i, rel - 1), ici_send.at[k], ici_recv.at[k],
                           same_core_of(rel))

        def fin_copy(i):
            arr, r0, n, _ = _PIECES[i]
            place = outs[arr].at[pl.ds(r0, n), :]
            return _remote(place, place, fin_send.at[i], fin_recv.at[i], sibling)

        def sm_ici_copy(rel, j):
            blk, r0, n, _ = _SM_PIECES[j]
            k = (rel - 1) * _NSP + j
            return _remote(sm_chip[blk].at[pl.ds(r0, n), :], sm_recv[blk].at[rel - 1, pl.ds(r0, n), :],
                           smi_send.at[k], smi_recv.at[k], same_core_of(rel))

        def sm_fin_copy(j):
            blk, r0, n, _ = _SM_PIECES[j]
            place = sm_out[blk].at[pl.ds(r0, n), :]
            return _remote(place, place, smf_send.at[j], smf_recv.at[j], sibling)

        vec[...] = jnp.zeros_like(vec)
        vec[0:1, :] = dg_ref[...]
        vec[1:2, :] = dgf_ref[...]
        vec[2:3, V_GQ:V_GQ + Q_LORA] = dgq_ref[...]
        vec[2:3, V_GKV:V_GKV + KV_LORA] = dgkv_ref[...]
        vec[2:3, V_PS:V_PS + D_POOL] = dps_ref[...]
        vec[2:3, V_LOSS:D] = loss_ref[...]
        _peer_wait()
        swaps = [_remote(sm_mine[b], sm_sib[b], swap_send.at[b], swap_recv.at[b], sibling) for b in range(_NSB)]
        for cp in swaps:
            cp.start()

        for core in (0, 1):
            @pl.when(c == core)
            def _(core=core):
                mine = [i for i in range(_NP) if _PIECES[i][3] == core and _PIECES[i][0] in loaded]
                theirs = [i for i in range(_NP) if _PIECES[i][3] != core and _PIECES[i][0] in loaded]
                i0 = next(i for i in range(_NP) if _PIECES[i][0] == 0 and _PIECES[i][3] == core)
                j0 = next(i for i in range(_NP) if _PIECES[i][0] == 0 and _PIECES[i][3] != core)
                sm_mine_p = [j for j in range(_NSP) if _SM_PIECES[j][3] == core]
                sm_theirs_p = [j for j in range(_NSP) if _SM_PIECES[j][3] != core]
                sends = list(swaps)

                for rel in shard_order:
                    for i in theirs:
                        cp = d2d_copy(rel, i)
                        cp.start()
                        sends.append(cp)
                    for i in mine:
                        own_load(rel, i).start()

                def piece_rows(i):
                    return pl.ds(_PIECES[i][1], _PIECES[i][2])

                def form(rel, i):
                    r0, n = _PIECES[i][1], _PIECES[i][2]
                    dwin_buf[rel, r0:r0 + n, :] = _tn(slab_v[rel, :, r0:r0 + _PIECE_MAX[0]], hn_v[...])[0:n, :]

                def d2d0(rel, i):
                    return _remote(dwin_buf.at[rel, piece_rows(i), :], slot(sib_buf, i, rel),
                                   d2d_send.at[rel * _NP + i], d2d_recv.at[rel * _NP + i], sibling)

                def settle(rel):
                    d2d0(rel, i0).wait_recv()
                    total = dwin_buf[rel, piece_rows(i0), :] + slot(sib_buf, i0, rel)[...]
                    if rel == 0:
                        own0[0:_PIECES[i0][2], :] = total
                    else:
                        slot(stage, i0, rel - 1)[...] = total.astype(BF16)
                        cp = ici_copy(rel, i0)
                        cp.start()
                        sends.append(cp)

                for rel in SEND_ORDER:
                    for i in mine:
                        arr, r0, n, _ = _PIECES[i]
                        own_load(rel, i).wait()
                        d2d_copy(rel, i).wait_recv()
                        total = slot(own_buf, i, rel)[...] + slot(sib_buf, i, rel)[...]
                        slot(stage, i, rel - 1)[...] = total.astype(stage[arr].dtype)
                        cp = ici_copy(rel, i)
                        cp.start()
                        sends.append(cp)

                for b in range(_NSB):
                    swaps[b].wait_recv()
                    sm_chip[b][...] = sm_mine[b][...] + sm_sib[b][...]
                for rel in SEND_ORDER:
                    for j in sm_mine_p:
                        cp = sm_ici_copy(rel, j)
                        cp.start()
                        sends.append(cp)

                hn_load.wait()
                for n, rel in enumerate(shard_order):
                    slab_load(rel).wait()
                    if n == 0:
                        for later in shard_order[1:]:
                            slab_load(later).start()
                    form(rel, j0)
                    cp = d2d0(rel, j0)
                    cp.start()
                    sends.append(cp)
                    form(rel, i0)
                    settle(rel)

                for i in mine:
                    arr, r0, n, _ = _PIECES[i]
                    own_load(0, i).wait()
                    d2d_copy(0, i).wait_recv()
                    total = slot(own_buf, i, 0)[...] + slot(sib_buf, i, 0)[...]
                    for rel in (1, 2, 3):
                        ici_copy(rel, i).wait_recv()
                        total = total + slot(recv, i, rel - 1)[...].astype(F32)
                    outs[arr][pl.ds(r0, n), :] = total
                    cp = fin_copy(i)
                    cp.start()
                    sends.append(cp)
                total = own0[0:_PIECES[i0][2], :]
                for rel in (1, 2, 3):
                    ici_copy(rel, i0).wait_recv()
                    total = total + slot(recv, i0, rel - 1)[...].astype(F32)
                outs[0][pl.ds(_PIECES[i0][1], _PIECES[i0][2]), :] = total
                cp = fin_copy(i0)
                cp.start()
                sends.append(cp)

                for j in sm_mine_p:
                    blk, r0, n, _ = _SM_PIECES[j]
                    for rel in (1, 2, 3):
                        sm_ici_copy(rel, j).wait_recv()
                    total = jnp.zeros((n, _SM_COLS[blk]), F32)
                    for chip in range(CHIPS):
                        flips = chip ^ me
                        rel = jnp.where(flips == 2, 1, jnp.where(flips == 1, 2, flips))
                        theirs_rows = sm_recv[blk][jnp.maximum(rel - 1, 0), pl.ds(r0, n), :]
                        total = total + jnp.where(rel == 0, sm_chip[blk][pl.ds(r0, n), :], theirs_rows)
                    sm_out[blk][pl.ds(r0, n), :] = total
                    cp = sm_fin_copy(j)
                    cp.start()
                    sends.append(cp)

                for i in theirs + [j0]:
                    fin_copy(i).wait_recv()
                for j in sm_theirs_p:
                    sm_fin_copy(j).wait_recv()
                for cp in sends:
                    cp.wait_send()

        gg_o[...] = vec_fin[0:1, :]
        ggf_o[...] = vec_fin[1:2, :]
        ggq_o[...] = vec_fin[2:3, V_GQ:V_GQ + Q_LORA]
        ggkv_o[...] = vec_fin[2:3, V_GKV:V_GKV + KV_LORA]
        gps_o[...] = vec_fin[2:3, V_PS:V_PS + D_POOL]
        gloss_o[...] = vec_fin[2:3, V_LOSS:D]

    vm = pl.BlockSpec(memory_space=pltpu.VMEM)
    piece_buf = lambda lead, dtype, which=arrays: [
        pltpu.VMEM((lead, _PIECE_MAX[a], _ARR_COLS[a]), F32 if a == 4 else dtype) for a in which]
    sm_buf = lambda *lead: [pltpu.VMEM(lead + (_SM_ROWS[b], _SM_COLS[b]), F32) for b in range(_NSB)]
    dma = lambda n: [pltpu.SemaphoreType.DMA((n,))] * 2
    return pl.pallas_call(
        body,
        name="reduce_grads",
        in_specs=[pl.BlockSpec(memory_space=pl.ANY)] * 4 + [vm] * 7,
        out_specs=[vm] * 10,
        out_shape=[jax.ShapeDtypeStruct((_ARR_ROWS[a], _ARR_COLS[a]), F32) for a in arrays]
        + [jax.ShapeDtypeStruct((1, D), F32),
           jax.ShapeDtypeStruct((1, D), F32), jax.ShapeDtypeStruct((1, Q_LORA), F32),
           jax.ShapeDtypeStruct((1, KV_LORA), F32), jax.ShapeDtypeStruct((1, D_POOL), F32),
           jax.ShapeDtypeStruct((1, 128), F32)],
        scratch_shapes=piece_buf(CHIPS, F32, loaded) + piece_buf(CHIPS, F32) + piece_buf(3, BF16) + piece_buf(3, BF16)
        + [pltpu.VMEM((VEC_ROWS, D), F32)] + sm_buf() + sm_buf() + sm_buf(3) + [pltpu.VMEM((VEC_ROWS, D), F32)]
        + [pltpu.VMEM((CHIPS, N, SHARD_PAD), BF16), pltpu.VMEM((N, D), BF16),
           pltpu.VMEM((CHIPS, SHARD_PAD, D), F32), pltpu.VMEM((_PIECE_MAX[0], D), F32)]
        + [pltpu.SemaphoreType.DMA((CHIPS * _NP,))]
        + dma(CHIPS * _NP) + dma(3 * _NP) + dma(_NP) + dma(_NSB) + dma(3 * _NSP) + dma(_NSP)
        + [pltpu.SemaphoreType.DMA((CHIPS + 1,))],
        compiler_params=_cparams(collective_id=3),
    )(dsl, hn, dwq, dwkv, dmeta4, dg, dgf, dgq, dgkv, dps, loss)


def _adamw_math(w, g, m, v):
    m = B1 * m + (1.0 - B1) * g
    v = B2 * v + (1.0 - B2) * (g * g)
    m_hat = m / C1
    v_hat = v / C2
    delta = -LR * (m_hat / (jnp.sqrt(v_hat) + ADAM_EPS) + WD * w)
    return delta, m, v


def _adamw(big, block_rows, groups):
    rows, cols = big[0].shape
    n = len(groups)

    def body(*refs):
        w_ref, g_ref, m_ref, v_ref = refs[0:4]
        small_in = refs[4:4 + 4 * n]
        go_ref, d_ref, nm_ref, nv_ref = refs[4 + 4 * n:8 + 4 * n]
        small_out = refs[8 + 4 * n:]
        g = g_ref[...]
        go_ref[...] = g
        d_ref[...], nm_ref[...], nv_ref[...] = _adamw_math(w_ref[...], g, m_ref[...], v_ref[...])

        @pl.when(pl.program_id(0) == 0)
        def _():
            for t in range(n):
                sw_ref, sg_ref, sm_ref, sv_ref = small_in[4 * t:4 * t + 4]
                sg = sg_ref[0:sw_ref.shape[0], :]
                small_out[4 * t][...] = sg
                small_out[4 * t + 1][...], small_out[4 * t + 2][...], small_out[4 * t + 3][...] = _adamw_math(
                    sw_ref[...], sg, sm_ref[...], sv_ref[...])

    spec = pl.BlockSpec((block_rows, cols), lambda i: (i, 0))
    vm = pl.BlockSpec(memory_space=pltpu.VMEM)
    outs = pl.pallas_call(
        body,
        name="adamw",
        grid=(rows // block_rows,),
        in_specs=[spec] * 4 + [vm] * (4 * n),
        out_specs=[spec] * 4 + [vm] * (4 * n),
        out_shape=[jax.ShapeDtypeStruct(big[0].shape, F32)] * 4
        + [jax.ShapeDtypeStruct(grp[0].shape, F32) for grp in groups for _ in range(4)],
        compiler_params=_cparams(dimension_semantics=("arbitrary",)),
    )(*big, *[a for grp in groups for a in grp])
    return tuple(outs[0:4]), [tuple(outs[4 + 4 * t:8 + 4 * t]) for t in range(n)]


def _rope_tables():
    half = QK_ROPE // 2
    f32 = np.float32
    inv_freq = (f32(1.0) / (f32(ROPE_THETA) ** (np.arange(half, dtype=f32) / f32(half)))).astype(f32)
    pos = np.arange(N, dtype=f32) - f32(PAD)
    ang = (pos[:, None] * inv_freq[None, :]).astype(f32)
    cos, sin = np.cos(ang).astype(f32), np.sin(ang).astype(f32)
    zero = np.zeros((N, 128 - QK_ROPE), f32)
    return jnp.asarray(np.concatenate([cos, cos, zero], axis=1)), jnp.asarray(np.concatenate([-sin, sin, zero], axis=1))


def kernel(x, meta_tokens, norm_g, w_in, q_norm_g, w_q_b, kv_norm_g, w_kv_b, pool_w, pool_scale, w_out, final_norm_g, loss_target, m_meta_tokens, m_norm_g, m_w_in, m_q_norm_g, m_w_q_b, m_kv_norm_g, m_w_kv_b, m_pool_w, m_pool_scale, m_w_out, m_final_norm_g, v_meta_tokens, v_norm_g, v_w_in, v_q_norm_g, v_w_q_b, v_kv_norm_g, v_w_kv_b, v_pool_w, v_pool_scale, v_w_out, v_final_norm_g):
    tr = lambda a: a[0].T
    win, wq, wkv, h, tgt = _gather_weights(tr(w_in), tr(w_q_b), w_kv_b[0], meta_tokens, x[0], loss_target[0])
    cosf, sinf = _rope_tables()
    gf = final_norm_g.reshape(1, D)

    part = _local_step(h, tgt, norm_g, win, q_norm_g, wq, kv_norm_g, wkv, pool_w[0], pool_scale, w_out[0], m_w_out[0],
                       v_w_out[0], gf, cosf, sinf)

    pw2 = lambda a: a.reshape(len(POOL_WINDOWS) * POOL_GROUP, POOL_GROUP)
    gpw = part["gpw"]
    gwinT, gwqT, gwkv, gmeta, gg, ggf, ggq, ggkv, gps, gloss = _reduce_grads(
        part["dsl"], part["hn"], part["dwq"], part["dwkv"], part["dmeta"], part["dg"],
        part["dgf"], part["dgq"], part["dgkv"], part["dps"], part["loss"])

    r_out = part["r_out"]
    fn2 = lambda a: a.reshape(1, D)
    r_in, (r_meta, r_norm, r_gq, r_wq, r_gkv, r_wkv, r_pw, r_ps, r_fn) = _adamw((tr(w_in), gwinT, tr(m_w_in), tr(v_w_in)), 248, [
        (meta_tokens, gmeta, m_meta_tokens, v_meta_tokens),
        (norm_g, gg, m_norm_g, v_norm_g),
        (q_norm_g, ggq, m_q_norm_g, v_q_norm_g),
        (tr(w_q_b), gwqT, tr(m_w_q_b), tr(v_w_q_b)),
        (kv_norm_g, ggkv, m_kv_norm_g, v_kv_norm_g),
        (w_kv_b[0], gwkv, m_w_kv_b[0], v_w_kv_b[0]),
        (pw2(pool_w), gpw, pw2(m_pool_w), pw2(v_pool_w)),
        (pool_scale, gps, m_pool_scale, v_pool_scale),
        (fn2(final_norm_g), ggf, fn2(m_final_norm_g), fn2(v_final_norm_g)),
    ])
    untr = lambda a: a.T[None]
    pw4 = lambda a: a.reshape(1, len(POOL_WINDOWS), POOL_GROUP, POOL_GROUP)
    per_kind = [[
        r_meta[kind], r_norm[kind], untr(r_in[kind]), r_gq[kind], untr(r_wq[kind]), r_gkv[kind], r_wkv[kind][None],
        pw4(r_pw[kind]), r_ps[kind], r_out[kind][None], r_fn[kind].reshape(D),
    ] for kind in range(4)]
    return (gloss[0, 0], part["gx"][None], *per_kind[0], *per_kind[1], *per_kind[2], *per_kind[3])
```
